```python
import jax, jax.numpy as jnp
from jax import lax
import numpy as np

D_MODEL = 1024
BATCH = 32
SEQ = 2048
DEPTH = 1

D_MIX = D_MODEL
SG_WIDTH = D_MIX // 2
SG_GROUPS = 8
SG_GROUP_DIM = SG_WIDTH // SG_GROUPS
SG_CHUNK = 128
DN_WIDTH = D_MIX - SG_WIDTH
DN_HEAD_DIM = 128
DN_HEADS = DN_WIDTH // DN_HEAD_DIM
DN_CHUNK = 64
CONV_K = 4
D_FF = 2816
EPS = 1e-6
IN_COLS = 2 * SG_WIDTH + 4 * DN_WIDTH + 2 * DN_HEADS

kernel_name = "hybrid_gmlp_gated_deltanet_macaron"


def rmsnorm(x, g):
    xf = x.astype(jnp.float32)
    y = xf * lax.rsqrt(jnp.mean(xf * xf, axis=-1, keepdims=True) + EPS)
    return (y * g.astype(jnp.float32)).astype(x.dtype)


def layernorm(x, g, b):
    xf = x.astype(jnp.float32)
    mu = jnp.mean(xf, axis=-1, keepdims=True)
    var = jnp.mean(jnp.square(xf - mu), axis=-1, keepdims=True)
    y = (xf - mu) * lax.rsqrt(var + EPS)
    return (y * g.astype(jnp.float32) + b.astype(jnp.float32)).astype(x.dtype)


def l2norm(x):
    return x * lax.rsqrt(jnp.sum(x * x, axis=-1, keepdims=True) + EPS)


def swiglu(h, w_gate, w_up, w_down):
    return (jax.nn.silu(h @ w_gate) * (h @ w_up)) @ w_down


def causal_dwconv(x, w):
    k_taps = w.shape[0]
    t_len = x.shape[1]
    xp = jnp.pad(x, ((0, 0), (k_taps - 1, 0), (0, 0)))
    y = xp[:, 0:t_len] * w[0]
    for j in range(1, k_taps):
        y = y + xp[:, j:j + t_len] * w[j]
    return y


def chunked_spatial_gating(u, v, ln_g, ln_b, w_s, b_s):
    bsz, t_len, _ = v.shape
    n_chunks = t_len // SG_CHUNK
    v = layernorm(v, ln_g, ln_b).reshape(bsz, n_chunks, SG_CHUNK, SG_GROUPS, SG_GROUP_DIM)
    pos = jnp.arange(SG_CHUNK)
    causal = pos[:, None] >= pos[None, :]
    w_causal = jnp.where(causal, w_s, jnp.zeros((), w_s.dtype))
    vs = jnp.einsum('gts,bnsgc->bntgc', w_causal, v) + b_s.T[:, :, None]
    return u * vs.reshape(bsz, t_len, SG_WIDTH)


def gated_delta_rule(q, k, v, g, beta):
    bsz, t_len, n_heads, dk = q.shape
    dv = v.shape[-1]
    c = DN_CHUNK
    n_chunks = t_len // c

    def chunks(a):
        return a.reshape(bsz, n_chunks, c, n_heads, a.shape[-1]).transpose(1, 0, 3, 2, 4)

    q = chunks(q) * (dk ** -0.5)
    k = chunks(k)
    v = chunks(v)
    g = chunks(g[..., None])[..., 0]
    beta = chunks(beta[..., None])[..., 0]
    gc = jnp.cumsum(g, axis=-1)
    pos = jnp.arange(c)
    incl = pos[:, None] >= pos[None, :]
    strict = pos[:, None] > pos[None, :]
    decay = jnp.exp(jnp.where(incl, gc[..., :, None] - gc[..., None, :], -jnp.inf))
    k_beta = k * beta[..., None]
    v_beta = v * beta[..., None]
    l_mat = jnp.where(strict, jnp.einsum('nbhcd,nbhsd->nbhcs', k_beta, k) * decay, 0.0)
    rhs = jnp.concatenate([v_beta, k_beta * jnp.exp(gc)[..., None]], axis=-1)
    sol = lax.linalg.triangular_solve(l_mat, rhs, left_side=True, lower=True, unit_diagonal=True)
    u_wy, w_wy = sol[..., :dv], sol[..., dv:]
    qk = jnp.einsum('nbhcd,nbhsd->nbhcs', q, k) * decay
    q_dec = q * jnp.exp(gc)[..., None]
    k_dec = k * jnp.exp(gc[..., -1:] - gc)[..., None]
    g_last = jnp.exp(gc[..., -1])

    def step(state, xs):
        q_n, k_n, u_n, w_n, qk_n, gl_n = xs
        v_new = u_n - jnp.einsum('bhcd,bhde->bhce', w_n, state)
        o_n = jnp.einsum('bhcd,bhde->bhce', q_n, state) + jnp.einsum('bhcs,bhse->bhce', qk_n, v_new)
        state = state * gl_n[..., None, None] + jnp.einsum('bhcd,bhce->bhde', k_n, v_new)
        return state, o_n

    s0 = jnp.zeros((bsz, n_heads, dk, dv), jnp.float32)
    _, o = lax.scan(step, s0, (q_dec, k_dec, u_wy, w_wy, qk, g_last))
    return o.transpose(1, 0, 3, 2, 4).reshape(bsz, t_len, n_heads, dv)


def hybrid_mixer(h, w_in, conv_w, a_log, dt_bias, dn_norm, sg_ln_g, sg_ln_b, sg_w, sg_b, w_out):
    bsz, t_len, _ = h.shape
    proj = h @ w_in
    o1 = SG_WIDTH
    o2 = 2 * SG_WIDTH
    o3 = o2 + 3 * DN_WIDTH
    o4 = o3 + DN_WIDTH
    o5 = o4 + DN_HEADS
    sg_u, sg_v, dn_qkv, dn_z, dn_b, dn_a = jnp.split(proj, [o1, o2, o3, o4, o5], axis=-1)

    sg_out = chunked_spatial_gating(jax.nn.gelu(sg_u), jax.nn.gelu(sg_v), sg_ln_g, sg_ln_b, sg_w, sg_b)

    qkv = jax.nn.silu(causal_dwconv(dn_qkv, conv_w)).astype(jnp.float32)
    q, k, v = jnp.split(qkv, 3, axis=-1)
    q = l2norm(q.reshape(bsz, t_len, DN_HEADS, DN_HEAD_DIM))
    k = l2norm(k.reshape(bsz, t_len, DN_HEADS, DN_HEAD_DIM))
    v = v.reshape(bsz, t_len, DN_HEADS, DN_HEAD_DIM)
    beta = jax.nn.sigmoid(dn_b.astype(jnp.float32))
    g = -jnp.exp(a_log.astype(jnp.float32)) * jax.nn.softplus(dn_a.astype(jnp.float32) + dt_bias.astype(jnp.float32))
    o = gated_delta_rule(q, k, v, g, beta)
    o = o * lax.rsqrt(jnp.mean(o * o, axis=-1, keepdims=True) + EPS) * dn_norm.astype(jnp.float32)
    z = dn_z.reshape(bsz, t_len, DN_HEADS, DN_HEAD_DIM).astype(jnp.float32)
    dn_out = (o * jax.nn.silu(z)).reshape(bsz, t_len, DN_WIDTH).astype(h.dtype)

    return jnp.concatenate([sg_out, dn_out], axis=-1) @ w_out


def _fwd_setup_inputs(seed: int = 0) -> dict:
    key = jax.random.key(seed)
    ks = jax.random.split(key, 24)
    f32 = jnp.float32

    def nrm(k, shape, scale):
        return jax.random.normal(k, shape, f32) * scale

    def gain(k, shape):
        return 1.0 + 0.02 * jax.random.normal(k, shape, f32)

    dt = jnp.exp(jax.random.uniform(ks[8], (DEPTH, DN_HEADS), f32, np.log(0.001), np.log(0.1)))
    return {
        "x": jax.random.normal(ks[0], (BATCH, SEQ, D_MODEL), f32),
        "ffn1_norm": gain(ks[1], (DEPTH, D_MODEL)),
        "ffn1_w_gate": nrm(ks[2], (DEPTH, D_MODEL, D_FF), D_MODEL ** -0.5),
        "ffn1_w_up": nrm(ks[3], (DEPTH, D_MODEL, D_FF), D_MODEL ** -0.5),
        "ffn1_w_down": nrm(ks[4], (DEPTH, D_FF, D_MODEL), D_FF ** -0.5),
        "mix_norm": gain(ks[5], (DEPTH, D_MODEL)),
        "w_in": nrm(ks[6], (DEPTH, D_MODEL, IN_COLS), D_MODEL ** -0.5),
        "conv_w": nrm(ks[7], (DEPTH, CONV_K, 3 * DN_WIDTH), CONV_K ** -0.5),
        "a_log": jnp.log(jax.random.uniform(ks[9], (DEPTH, DN_HEADS), f32, 1.0, 16.0)),
        "dt_bias": dt + jnp.log(-jnp.expm1(-dt)),
        "dn_norm": gain(ks[10], (DEPTH, DN_HEAD_DIM)),
        "sg_ln_g": gain(ks[11], (DEPTH, SG_WIDTH)),
        "sg_ln_b": nrm(ks[12], (DEPTH, SG_WIDTH), 0.02),
        "sg_w": nrm(ks[13], (DEPTH, SG_GROUPS, SG_CHUNK, SG_CHUNK), SG_CHUNK ** -0.5),
        "sg_b": gain(ks[14], (DEPTH, SG_GROUPS, SG_CHUNK)),
        "w_out": nrm(ks[15], (DEPTH, D_MIX, D_MODEL), D_MIX ** -0.5),
        "ffn2_norm": gain(ks[16], (DEPTH, D_MODEL)),
        "ffn2_w_gate": nrm(ks[17], (DEPTH, D_MODEL, D_FF), D_MODEL ** -0.5),
        "ffn2_w_up": nrm(ks[18], (DEPTH, D_MODEL, D_FF), D_MODEL ** -0.5),
        "ffn2_w_down": nrm(ks[19], (DEPTH, D_FF, D_MODEL), D_FF ** -0.5),
        "final_norm": gain(ks[20], (D_MODEL,)),
    }


def _fwd_reference(x, ffn1_norm, ffn1_w_gate, ffn1_w_up, ffn1_w_down, mix_norm, w_in, conv_w, a_log,
              dt_bias, dn_norm, sg_ln_g, sg_ln_b, sg_w, sg_b, w_out, ffn2_norm, ffn2_w_gate,
              ffn2_w_up, ffn2_w_down, final_norm):
    for l in range(DEPTH):
        h = rmsnorm(x, ffn1_norm[l])
        x = x + 0.5 * swiglu(h, ffn1_w_gate[l], ffn1_w_up[l], ffn1_w_down[l])
        h = rmsnorm(x, mix_norm[l])
        x = x + hybrid_mixer(h, w_in[l], conv_w[l], a_log[l], dt_bias[l], dn_norm[l],
                             sg_ln_g[l], sg_ln_b[l], sg_w[l], sg_b[l], w_out[l])
        h = rmsnorm(x, ffn2_norm[l])
        x = x + 0.5 * swiglu(h, ffn2_w_gate[l], ffn2_w_up[l], ffn2_w_down[l])
    return rmsnorm(x, final_norm)


import jax as _jax
import jax.numpy as _jnp

TWIN_FORMAT = 'train_step'
FWD_PARAMS = ['x', 'ffn1_norm', 'ffn1_w_gate', 'ffn1_w_up', 'ffn1_w_down', 'mix_norm', 'w_in', 'conv_w', 'a_log', 'dt_bias', 'dn_norm', 'sg_ln_g', 'sg_ln_b', 'sg_w', 'sg_b', 'w_out', 'ffn2_norm', 'ffn2_w_gate', 'ffn2_w_up', 'ffn2_w_down', 'final_norm']
TWIN_WEIGHTS = ['ffn1_norm', 'ffn1_w_gate', 'ffn1_w_up', 'ffn1_w_down', 'mix_norm', 'w_in', 'conv_w', 'a_log', 'dt_bias', 'dn_norm', 'sg_ln_g', 'sg_ln_b', 'sg_w', 'sg_b', 'w_out', 'ffn2_norm', 'ffn2_w_gate', 'ffn2_w_up', 'ffn2_w_down', 'final_norm']
TWIN_DIFF_INPUT = 'x'
TWIN_INPUTS = ['x', 'ffn1_norm', 'ffn1_w_gate', 'ffn1_w_up', 'ffn1_w_down', 'mix_norm', 'w_in', 'conv_w', 'a_log', 'dt_bias', 'dn_norm', 'sg_ln_g', 'sg_ln_b', 'sg_w', 'sg_b', 'w_out', 'ffn2_norm', 'ffn2_w_gate', 'ffn2_w_up', 'ffn2_w_down', 'final_norm', 'loss_target', 'm_ffn1_norm', 'm_ffn1_w_gate', 'm_ffn1_w_up', 'm_ffn1_w_down', 'm_mix_norm', 'm_w_in', 'm_conv_w', 'm_a_log', 'm_dt_bias', 'm_dn_norm', 'm_sg_ln_g', 'm_sg_ln_b', 'm_sg_w', 'm_sg_b', 'm_w_out', 'm_ffn2_norm', 'm_ffn2_w_gate', 'm_ffn2_w_up', 'm_ffn2_w_down', 'm_final_norm', 'v_ffn1_norm', 'v_ffn1_w_gate', 'v_ffn1_w_up', 'v_ffn1_w_down', 'v_mix_norm', 'v_w_in', 'v_conv_w', 'v_a_log', 'v_dt_bias', 'v_dn_norm', 'v_sg_ln_g', 'v_sg_ln_b', 'v_sg_w', 'v_sg_b', 'v_w_out', 'v_ffn2_norm', 'v_ffn2_w_gate', 'v_ffn2_w_up', 'v_ffn2_w_down', 'v_final_norm']
TWIN_OUTPUTS = ['loss', 'grad_x', 'grad_ffn1_norm', 'grad_ffn1_w_gate', 'grad_ffn1_w_up', 'grad_ffn1_w_down', 'grad_mix_norm', 'grad_w_in', 'grad_conv_w', 'grad_a_log', 'grad_dt_bias', 'grad_dn_norm', 'grad_sg_ln_g', 'grad_sg_ln_b', 'grad_sg_w', 'grad_sg_b', 'grad_w_out', 'grad_ffn2_norm', 'grad_ffn2_w_gate', 'grad_ffn2_w_up', 'grad_ffn2_w_down', 'grad_final_norm', 'delta_ffn1_norm', 'delta_ffn1_w_gate', 'delta_ffn1_w_up', 'delta_ffn1_w_down', 'delta_mix_norm', 'delta_w_in', 'delta_conv_w', 'delta_a_log', 'delta_dt_bias', 'delta_dn_norm', 'delta_sg_ln_g', 'delta_sg_ln_b', 'delta_sg_w', 'delta_sg_b', 'delta_w_out', 'delta_ffn2_norm', 'delta_ffn2_w_gate', 'delta_ffn2_w_up', 'delta_ffn2_w_down', 'delta_final_norm', 'new_m_ffn1_norm', 'new_m_ffn1_w_gate', 'new_m_ffn1_w_up', 'new_m_ffn1_w_down', 'new_m_mix_norm', 'new_m_w_in', 'new_m_conv_w', 'new_m_a_log', 'new_m_dt_bias', 'new_m_dn_norm', 'new_m_sg_ln_g', 'new_m_sg_ln_b', 'new_m_sg_w', 'new_m_sg_b', 'new_m_w_out', 'new_m_ffn2_norm', 'new_m_ffn2_w_gate', 'new_m_ffn2_w_up', 'new_m_ffn2_w_down', 'new_m_final_norm', 'new_v_ffn1_norm', 'new_v_ffn1_w_gate', 'new_v_ffn1_w_up', 'new_v_ffn1_w_down', 'new_v_mix_norm', 'new_v_w_in', 'new_v_conv_w', 'new_v_a_log', 'new_v_dt_bias', 'new_v_dn_norm', 'new_v_sg_ln_g', 'new_v_sg_ln_b', 'new_v_sg_w', 'new_v_sg_b', 'new_v_w_out', 'new_v_ffn2_norm', 'new_v_ffn2_w_gate', 'new_v_ffn2_w_up', 'new_v_ffn2_w_down', 'new_v_final_norm']
TWIN_LEAF_KINDS = {'loss': 'loss', 'grad_x': 'grad_x', 'grad_ffn1_norm': 'grad_w', 'grad_ffn1_w_gate': 'grad_w', 'grad_ffn1_w_up': 'grad_w', 'grad_ffn1_w_down': 'grad_w', 'grad_mix_norm': 'grad_w', 'grad_w_in': 'grad_w', 'grad_conv_w': 'grad_w', 'grad_a_log': 'grad_w', 'grad_dt_bias': 'grad_w', 'grad_dn_norm': 'grad_w', 'grad_sg_ln_g': 'grad_w', 'grad_sg_ln_b': 'grad_w', 'grad_sg_w': 'grad_w', 'grad_sg_b': 'grad_w', 'grad_w_out': 'grad_w', 'grad_ffn2_norm': 'grad_w', 'grad_ffn2_w_gate': 'grad_w', 'grad_ffn2_w_up': 'grad_w', 'grad_ffn2_w_down': 'grad_w', 'grad_final_norm': 'grad_w', 'delta_ffn1_norm': 'delta_w', 'delta_ffn1_w_gate': 'delta_w', 'delta_ffn1_w_up': 'delta_w', 'delta_ffn1_w_down': 'delta_w', 'delta_mix_norm': 'delta_w', 'delta_w_in': 'delta_w', 'delta_conv_w': 'delta_w', 'delta_a_log': 'delta_w', 'delta_dt_bias': 'delta_w', 'delta_dn_norm': 'delta_w', 'delta_sg_ln_g': 'delta_w', 'delta_sg_ln_b': 'delta_w', 'delta_sg_w': 'delta_w', 'delta_sg_b': 'delta_w', 'delta_w_out': 'delta_w', 'delta_ffn2_norm': 'delta_w', 'delta_ffn2_w_gate': 'delta_w', 'delta_ffn2_w_up': 'delta_w', 'delta_ffn2_w_down': 'delta_w', 'delta_final_norm': 'delta_w', 'new_m_ffn1_norm': 'new_m', 'new_m_ffn1_w_gate': 'new_m', 'new_m_ffn1_w_up': 'new_m', 'new_m_ffn1_w_down': 'new_m', 'new_m_mix_norm': 'new_m', 'new_m_w_in': 'new_m', 'new_m_conv_w': 'new_m', 'new_m_a_log': 'new_m', 'new_m_dt_bias': 'new_m', 'new_m_dn_norm': 'new_m', 'new_m_sg_ln_g': 'new_m', 'new_m_sg_ln_b': 'new_m', 'new_m_sg_w': 'new_m', 'new_m_sg_b': 'new_m', 'new_m_w_out': 'new_m', 'new_m_ffn2_norm': 'new_m', 'new_m_ffn2_w_gate': 'new_m', 'new_m_ffn2_w_up': 'new_m', 'new_m_ffn2_w_down': 'new_m', 'new_m_final_norm': 'new_m', 'new_v_ffn1_norm': 'new_v', 'new_v_ffn1_w_gate': 'new_v', 'new_v_ffn1_w_up': 'new_v', 'new_v_ffn1_w_down': 'new_v', 'new_v_mix_norm': 'new_v', 'new_v_w_in': 'new_v', 'new_v_conv_w': 'new_v', 'new_v_a_log': 'new_v', 'new_v_dt_bias': 'new_v', 'new_v_dn_norm': 'new_v', 'new_v_sg_ln_g': 'new_v', 'new_v_sg_ln_b': 'new_v', 'new_v_sg_w': 'new_v', 'new_v_sg_b': 'new_v', 'new_v_w_out': 'new_v', 'new_v_ffn2_norm': 'new_v', 'new_v_ffn2_w_gate': 'new_v', 'new_v_ffn2_w_up': 'new_v', 'new_v_ffn2_w_down': 'new_v', 'new_v_final_norm': 'new_v'}


def _forward(args):
    return _fwd_reference(*[args[k] for k in FWD_PARAMS])


def _output_shape():
    out = _jax.eval_shape(lambda: _forward(_fwd_setup_inputs(0)))
    return out.shape, out.dtype

N_MICROBATCH = 1
ADAM_LR = 0.001
ADAM_B1 = 0.9
ADAM_B2 = 0.999
ADAM_EPS = 1e-08
ADAM_WD = 0.01
ADAM_STEP = 10
PER_EXAMPLE_BATCH_AXIS = {'x': 0, 'loss_target': 0}
SHARED_INPUTS = []
_WEIGHT_DTYPES = {'ffn1_norm': _jnp.float32, 'ffn1_w_gate': _jnp.float32, 'ffn1_w_up': _jnp.float32, 'ffn1_w_down': _jnp.float32, 'mix_norm': _jnp.float32, 'w_in': _jnp.float32, 'conv_w': _jnp.float32, 'a_log': _jnp.float32, 'dt_bias': _jnp.float32, 'dn_norm': _jnp.float32, 'sg_ln_g': _jnp.float32, 'sg_ln_b': _jnp.float32, 'sg_w': _jnp.float32, 'sg_b': _jnp.float32, 'w_out': _jnp.float32, 'ffn2_norm': _jnp.float32, 'ffn2_w_gate': _jnp.float32, 'ffn2_w_up': _jnp.float32, 'ffn2_w_down': _jnp.float32, 'final_norm': _jnp.float32}
MOMENT_SCALE = {'ffn1_norm': 1.383292e-01, 'ffn1_w_gate': 5.251437e-02, 'ffn1_w_up': 5.084509e-02, 'ffn1_w_down': 8.428516e-02, 'mix_norm': 2.033117e-01, 'w_in': 1.179638e-01, 'conv_w': 1.024653e-01, 'a_log': 5.325245e-01, 'dt_bias': 5.203210e-01, 'dn_norm': 2.940833e-01, 'sg_ln_g': 1.000481e-01, 'sg_ln_b': 9.101014e-02, 'sg_w': 6.623132e-02, 'sg_b': 9.553911e-02, 'w_out': 1.486193e-01, 'ffn2_norm': 8.667406e-02, 'ffn2_w_gate': 3.627721e-02, 'ffn2_w_up': 3.513815e-02, 'ffn2_w_down': 5.827746e-02, 'final_norm': 6.390095e+01}


def _to_microbatches(a, axis):
    t = _jnp.moveaxis(a, axis, 0)
    t = t.reshape((N_MICROBATCH, t.shape[0] // N_MICROBATCH) + t.shape[1:])
    return _jnp.moveaxis(t, 1, axis + 1)


def setup_inputs(seed: int = 0) -> dict:
    inp = _fwd_setup_inputs(seed)
    key = _jax.random.fold_in(_jax.random.key(seed), 7919)
    shape, _ = _output_shape()
    out = dict(inp)
    out["loss_target"] = _jax.random.normal(_jax.random.fold_in(key, 0), shape, _jnp.float32)
    for i, name in enumerate(TWIN_WEIGHTS):
        w = inp[name].astype(_jnp.float32)
        if MOMENT_SCALE is None:
            s = _jnp.sqrt(_jnp.mean(_jnp.square(w)) + 1e-30)
        else:
            s = MOMENT_SCALE[name]
        km, kv = _jax.random.split(_jax.random.fold_in(key, i + 1))
        out[name] = w
        out["m_" + name] = s * _jax.random.normal(km, w.shape, _jnp.float32)
        out["v_" + name] = (s * s) * _jax.random.uniform(kv, w.shape, _jnp.float32, 0.5, 1.5)
    if N_MICROBATCH > 1:
        for name, axis in PER_EXAMPLE_BATCH_AXIS.items():
            out[name] = _to_microbatches(out[name], axis)
    return {'x': out['x'], 'ffn1_norm': out['ffn1_norm'], 'ffn1_w_gate': out['ffn1_w_gate'], 'ffn1_w_up': out['ffn1_w_up'], 'ffn1_w_down': out['ffn1_w_down'], 'mix_norm': out['mix_norm'], 'w_in': out['w_in'], 'conv_w': out['conv_w'], 'a_log': out['a_log'], 'dt_bias': out['dt_bias'], 'dn_norm': out['dn_norm'], 'sg_ln_g': out['sg_ln_g'], 'sg_ln_b': out['sg_ln_b'], 'sg_w': out['sg_w'], 'sg_b': out['sg_b'], 'w_out': out['w_out'], 'ffn2_norm': out['ffn2_norm'], 'ffn2_w_gate': out['ffn2_w_gate'], 'ffn2_w_up': out['ffn2_w_up'], 'ffn2_w_down': out['ffn2_w_down'], 'final_norm': out['final_norm'], 'loss_target': out['loss_target'], 'm_ffn1_norm': out['m_ffn1_norm'], 'm_ffn1_w_gate': out['m_ffn1_w_gate'], 'm_ffn1_w_up': out['m_ffn1_w_up'], 'm_ffn1_w_down': out['m_ffn1_w_down'], 'm_mix_norm': out['m_mix_norm'], 'm_w_in': out['m_w_in'], 'm_conv_w': out['m_conv_w'], 'm_a_log': out['m_a_log'], 'm_dt_bias': out['m_dt_bias'], 'm_dn_norm': out['m_dn_norm'], 'm_sg_ln_g': out['m_sg_ln_g'], 'm_sg_ln_b': out['m_sg_ln_b'], 'm_sg_w': out['m_sg_w'], 'm_sg_b': out['m_sg_b'], 'm_w_out': out['m_w_out'], 'm_ffn2_norm': out['m_ffn2_norm'], 'm_ffn2_w_gate': out['m_ffn2_w_gate'], 'm_ffn2_w_up': out['m_ffn2_w_up'], 'm_ffn2_w_down': out['m_ffn2_w_down'], 'm_final_norm': out['m_final_norm'], 'v_ffn1_norm': out['v_ffn1_norm'], 'v_ffn1_w_gate': out['v_ffn1_w_gate'], 'v_ffn1_w_up': out['v_ffn1_w_up'], 'v_ffn1_w_down': out['v_ffn1_w_down'], 'v_mix_norm': out['v_mix_norm'], 'v_w_in': out['v_w_in'], 'v_conv_w': out['v_conv_w'], 'v_a_log': out['v_a_log'], 'v_dt_bias': out['v_dt_bias'], 'v_dn_norm': out['v_dn_norm'], 'v_sg_ln_g': out['v_sg_ln_g'], 'v_sg_ln_b': out['v_sg_ln_b'], 'v_sg_w': out['v_sg_w'], 'v_sg_b': out['v_sg_b'], 'v_w_out': out['v_w_out'], 'v_ffn2_norm': out['v_ffn2_norm'], 'v_ffn2_w_gate': out['v_ffn2_w_gate'], 'v_ffn2_w_up': out['v_ffn2_w_up'], 'v_ffn2_w_down': out['v_ffn2_w_down'], 'v_final_norm': out['v_final_norm']}


def _loss(weights, diff, rest, loss_target):
    with _jax.named_scope("forward"):
        args = {**rest, TWIN_DIFF_INPUT: diff, **{k: w.astype(_WEIGHT_DTYPES[k]) for k, w in weights.items()}}
        y = _forward(args)
    with _jax.named_scope("loss_head"):
        err = _jnp.square(y.astype(_jnp.float32) - loss_target)
        return 0.5 * _jnp.sum(_jnp.mean(err, axis=-1)) if err.ndim else 0.5 * err


def _adamw(w, g, m, v):
    m = ADAM_B1 * m + (1.0 - ADAM_B1) * g
    v = ADAM_B2 * v + (1.0 - ADAM_B2) * _jnp.square(g)
    m_hat = m / (1.0 - ADAM_B1 ** ADAM_STEP)
    v_hat = v / (1.0 - ADAM_B2 ** ADAM_STEP)
    delta = -ADAM_LR * (m_hat / (_jnp.sqrt(v_hat) + ADAM_EPS) + ADAM_WD * w)
    return delta, m, v


def reference(x, ffn1_norm, ffn1_w_gate, ffn1_w_up, ffn1_w_down, mix_norm, w_in, conv_w, a_log, dt_bias, dn_norm, sg_ln_g, sg_ln_b, sg_w, sg_b, w_out, ffn2_norm, ffn2_w_gate, ffn2_w_up, ffn2_w_down, final_norm, loss_target, m_ffn1_norm, m_ffn1_w_gate, m_ffn1_w_up, m_ffn1_w_down, m_mix_norm, m_w_in, m_conv_w, m_a_log, m_dt_bias, m_dn_norm, m_sg_ln_g, m_sg_ln_b, m_sg_w, m_sg_b, m_w_out, m_ffn2_norm, m_ffn2_w_gate, m_ffn2_w_up, m_ffn2_w_down, m_final_norm, v_ffn1_norm, v_ffn1_w_gate, v_ffn1_w_up, v_ffn1_w_down, v_mix_norm, v_w_in, v_conv_w, v_a_log, v_dt_bias, v_dn_norm, v_sg_ln_g, v_sg_ln_b, v_sg_w, v_sg_b, v_w_out, v_ffn2_norm, v_ffn2_w_gate, v_ffn2_w_up, v_ffn2_w_down, v_final_norm):
    given = dict(x=x, ffn1_norm=ffn1_norm, ffn1_w_gate=ffn1_w_gate, ffn1_w_up=ffn1_w_up, ffn1_w_down=ffn1_w_down, mix_norm=mix_norm, w_in=w_in, conv_w=conv_w, a_log=a_log, dt_bias=dt_bias, dn_norm=dn_norm, sg_ln_g=sg_ln_g, sg_ln_b=sg_ln_b, sg_w=sg_w, sg_b=sg_b, w_out=w_out, ffn2_norm=ffn2_norm, ffn2_w_gate=ffn2_w_gate, ffn2_w_up=ffn2_w_up, ffn2_w_down=ffn2_w_down, final_norm=final_norm, loss_target=loss_target, m_ffn1_norm=m_ffn1_norm, m_ffn1_w_gate=m_ffn1_w_gate, m_ffn1_w_up=m_ffn1_w_up, m_ffn1_w_down=m_ffn1_w_down, m_mix_norm=m_mix_norm, m_w_in=m_w_in, m_conv_w=m_conv_w, m_a_log=m_a_log, m_dt_bias=m_dt_bias, m_dn_norm=m_dn_norm, m_sg_ln_g=m_sg_ln_g, m_sg_ln_b=m_sg_ln_b, m_sg_w=m_sg_w, m_sg_b=m_sg_b, m_w_out=m_w_out, m_ffn2_norm=m_ffn2_norm, m_ffn2_w_gate=m_ffn2_w_gate, m_ffn2_w_up=m_ffn2_w_up, m_ffn2_w_down=m_ffn2_w_down, m_final_norm=m_final_norm, v_ffn1_norm=v_ffn1_norm, v_ffn1_w_gate=v_ffn1_w_gate, v_ffn1_w_up=v_ffn1_w_up, v_ffn1_w_down=v_ffn1_w_down, v_mix_norm=v_mix_norm, v_w_in=v_w_in, v_conv_w=v_conv_w, v_a_log=v_a_log, v_dt_bias=v_dt_bias, v_dn_norm=v_dn_norm, v_sg_ln_g=v_sg_ln_g, v_sg_ln_b=v_sg_ln_b, v_sg_w=v_sg_w, v_sg_b=v_sg_b, v_w_out=v_w_out, v_ffn2_norm=v_ffn2_norm, v_ffn2_w_gate=v_ffn2_w_gate, v_ffn2_w_up=v_ffn2_w_up, v_ffn2_w_down=v_ffn2_w_down, v_final_norm=v_final_norm)
    weights = {n: given[n] for n in TWIN_WEIGHTS}
    shared = {n: given[n] for n in SHARED_INPUTS}
    per_example = {n: given[n] for n in ['x']}
    grad_fn = _jax.value_and_grad(_loss, argnums=(0, 1))

    def one_microbatch(ex, loss_target):
        ex = dict(ex)
        diff = ex.pop(TWIN_DIFF_INPUT)
        return grad_fn(weights, diff, {**shared, **ex}, loss_target)

    if N_MICROBATCH == 1:
        loss, (grad_w, grad_x) = one_microbatch(per_example, given["loss_target"])
    else:
        def body(carry, xs):
            loss_sum, grad_sum = carry
            l_k, (gw_k, gx_k) = one_microbatch(xs[0], xs[1])
            with _jax.named_scope("update"):
                return (loss_sum + l_k, _jax.tree.map(_jnp.add, grad_sum, gw_k)), gx_k

        init = (_jnp.zeros((), _jnp.float32), _jax.tree.map(_jnp.zeros_like, weights))
        (loss, grad_w), grad_x = _jax.lax.scan(body, init, (per_example, given["loss_target"]))
    with _jax.named_scope("update"):
        delta_w, new_m, new_v = {}, {}, {}
        for n in TWIN_WEIGHTS:
            delta_w[n], new_m[n], new_v[n] = _adamw(weights[n], grad_w[n], given["m_" + n], given["v_" + n])
    return (loss, grad_x, *[grad_w[n] for n in TWIN_WEIGHTS], *[delta_w[n] for n in TWIN_WEIGHTS],
            *[new_m[n] for n in TWIN_WEIGHTS], *[new_v[n] for n in TWIN_WEIGHTS])
```

```python
import functools

import jax
import jax.numpy as jnp
from jax import lax
from jax.experimental import pallas as pl
from jax.experimental.pallas import tpu as pltpu

F32 = jnp.float32
BF16 = jnp.bfloat16
EPS = 1e-6

D_MODEL = 1024
N_SHARD = 4
HEAD_DIM = 128
N_HEADS = 4
DN_CHUNK = 64
SG_CHUNK = 128
SG_GROUPS = 8
SG_GROUP_DIM = 64
HALF_W = 512
PROJ_W = 3200
IN_COLS = 3080
GATE_COL_BLOCK = 24
QK_SCALE = HEAD_DIM ** -0.5
LANES = 128

ADAM_LR = 0.001
ADAM_B1 = 0.9
ADAM_B2 = 0.999
ADAM_EPS = 1e-08
ADAM_WD = 0.01
ADAM_STEP = 10

VMEM_LIMIT = 56 * 1024 * 1024
ROW_TILE = 512

NN = ((1,), (0,))
NT = ((1,), (1,))
TN = ((0,), (0,))
MESH = pl.DeviceIdType.MESH


def _dot(a, b, dims):
    return lax.dot_general(a, b, (dims, ((), ())), preferred_element_type=F32)


def _bdot(a, b, dims):
    return _dot(a.astype(BF16), b.astype(BF16), dims)


def _hdot(a, b, dims=NN):
    return lax.dot_general(a, b, (dims, ((), ())), preferred_element_type=F32,
                           precision=lax.Precision.HIGHEST)


def _call(body, *, name, out_shape, in_specs, out_specs, grid=(), scratch=(), **kw):
    params = dict(vmem_limit_bytes=VMEM_LIMIT)
    if grid:
        params["dimension_semantics"] = ("arbitrary",) * len(grid)
    return pl.pallas_call(
        body, name=name, grid=grid, in_specs=in_specs, out_specs=out_specs,
        out_shape=out_shape, scratch_shapes=list(scratch),
        compiler_params=pltpu.CompilerParams(**params), **kw)


def _sds(shape, dtype):
    return jax.ShapeDtypeStruct(tuple(shape), dtype)


def _sigmoid(x):
    return jax.nn.sigmoid(x)


def _softplus(x):
    return jnp.maximum(x, 0.0) + jnp.log(1.0 + jnp.exp(-jnp.abs(x)))


_GELU_C = 0.7978845608028654
_GELU_A = 0.044715


def _gelu(x):
    t = jnp.tanh(_GELU_C * (x + _GELU_A * x * x * x))
    return 0.5 * x * (1.0 + t)


def _gelu_grad(x):
    t = jnp.tanh(_GELU_C * (x + _GELU_A * x * x * x))
    return 0.5 * (1.0 + t) + 0.5 * x * (1.0 - t * t) * _GELU_C * (1.0 + 3.0 * _GELU_A * x * x)


def _silu_grad(x):
    s = _sigmoid(x)
    return s * (1.0 + x * (1.0 - s))


def _rms_scale(xv):
    return lax.rsqrt(jnp.mean(xv * xv, axis=-1, keepdims=True) + EPS)


def _rms_bwd(dh, xv, g):
    r = _rms_scale(xv)
    xn = xv * r
    dg = jnp.sum(dh * xn, axis=0, keepdims=True)
    dxn = dh * g
    dx = r * (dxn - xn * jnp.mean(dxn * xn, axis=-1, keepdims=True))
    return dx, dg


def _iota2(shape, dim):
    return lax.broadcasted_iota(jnp.int32, shape, dim)


def _col_to_row(col):
    n = col.shape[0]
    eye = _iota2((n, n), 0) == _iota2((n, n), 1)
    return jnp.sum(jnp.where(eye, col, 0.0), axis=0, keepdims=True)


def _row_to_col(row):
    n = row.shape[1]
    eye = _iota2((n, n), 0) == _iota2((n, n), 1)
    return jnp.sum(jnp.where(eye, row, 0.0), axis=1, keepdims=True)


def ffn_fwd(x, gnorm, wg, wu, wd, name):
    n, d = x.shape
    nb, _, fb = wg.shape
    tm = min(ROW_TILE, n)

    def body(x_ref, g_ref, wg_ref, wu_ref, wd_ref, xo_ref, h_ref, gate_ref, up_ref, acc_ref):
        j = pl.program_id(1)

        @pl.when(j == 0)
        def _():
            xv = x_ref[...]
            h_ref[...] = (xv * _rms_scale(xv) * g_ref[...]).astype(BF16)
            acc_ref[...] = jnp.zeros_like(acc_ref)

        h = h_ref[...]
        gate = _dot(h, wg_ref[0], NN)
        up = _dot(h, wu_ref[0], NN)
        gate_ref[0] = gate.astype(BF16)
        up_ref[0] = up.astype(BF16)
        act = (gate * _sigmoid(gate) * up).astype(BF16)
        acc_ref[...] += _dot(act, wd_ref[0], NN)

        @pl.when(j == nb - 1)
        def _():
            xo_ref[...] = x_ref[...] + 0.5 * acc_ref[...]

    row = pl.BlockSpec((tm, d), lambda i, j: (i, 0))
    return _call(
        body, name=name, grid=(n // tm, nb),
        in_specs=[row, pl.BlockSpec((1, d), lambda i, j: (0, 0)),
                  pl.BlockSpec((1, d, fb), lambda i, j: (j, 0, 0)),
                  pl.BlockSpec((1, d, fb), lambda i, j: (j, 0, 0)),
                  pl.BlockSpec((1, fb, d), lambda i, j: (j, 0, 0))],
        out_specs=[row, row,
                   pl.BlockSpec((1, tm, fb), lambda i, j: (j, i, 0)),
                   pl.BlockSpec((1, tm, fb), lambda i, j: (j, i, 0))],
        out_shape=[_sds((n, d), F32), _sds((n, d), BF16),
                   _sds((nb, n, fb), BF16), _sds((nb, n, fb), BF16)],
        scratch=[pltpu.VMEM((tm, d), F32)],
    )(x, gnorm, wg, wu, wd)


def ffn_bwd_act(dy, x, gnorm, gate, up, wg, wu, wd, name):
    n, d = x.shape
    nb, _, fb = wg.shape
    tm = min(ROW_TILE, n)

    def body(dy_ref, x_ref, g_ref, gate_ref, up_ref, wg_ref, wu_ref, wd_ref,
             dx_ref, dgate_ref, dup_ref, act_ref, dyh_ref, dg_ref, acc_ref):
        i = pl.program_id(0)
        j = pl.program_id(1)

        @pl.when(jnp.logical_and(i == 0, j == 0))
        def _():
            dg_ref[...] = jnp.zeros_like(dg_ref)

        @pl.when(j == 0)
        def _():
            dyh_ref[...] = (0.5 * dy_ref[...]).astype(BF16)
            acc_ref[...] = jnp.zeros_like(acc_ref)

        dact = _dot(dyh_ref[...], wd_ref[0], NT)
        gt = gate_ref[0].astype(F32)
        u = up_ref[0].astype(F32)
        s = _sigmoid(gt)
        silu = gt * s
        dup = (dact * silu).astype(BF16)
        dgate = (dact * u * (s * (1.0 + gt * (1.0 - s)))).astype(BF16)
        dup_ref[0] = dup
        dgate_ref[0] = dgate
        act_ref[0] = (silu * u).astype(BF16)
        acc_ref[...] += _dot(dgate, wg_ref[0], NT) + _dot(dup, wu_ref[0], NT)

        @pl.when(j == nb - 1)
        def _():
            dxn, dg = _rms_bwd(acc_ref[...], x_ref[...], g_ref[...])
            dx_ref[...] = dy_ref[...] + dxn
            dg_ref[...] += dg

    row = pl.BlockSpec((tm, d), lambda i, j: (i, 0))
    blk = pl.BlockSpec((1, tm, fb), lambda i, j: (j, i, 0))
    vec = pl.BlockSpec((1, d), lambda i, j: (0, 0))
    wcol = pl.BlockSpec((1, d, fb), lambda i, j: (j, 0, 0))
    return _call(
        body, name=name, grid=(n // tm, nb),
        in_specs=[row, row, vec, blk, blk, wcol, wcol,
                  pl.BlockSpec((1, fb, d), lambda i, j: (j, 0, 0))],
        out_specs=[row, blk, blk, blk, row, vec],
        out_shape=[_sds((n, d), F32), _sds((nb, n, fb), BF16), _sds((nb, n, fb), BF16),
                   _sds((nb, n, fb), BF16), _sds((n, d), BF16), _sds((1, d), F32)],
        scratch=[pltpu.VMEM((tm, d), F32)],
    )(dy, x, gnorm, gate, up, wg, wu, wd)


def ffn_bwd_w(h, dyh, dgate, dup, act, name):
    n, d = h.shape
    nb, _, fb = dgate.shape
    tk = min(ROW_TILE, n)

    def body(h_ref, dyh_ref, dgate_ref, dup_ref, act_ref, dwg_ref, dwu_ref, dwd_ref):
        @pl.when(pl.program_id(1) == 0)
        def _():
            dwg_ref[...] = jnp.zeros_like(dwg_ref)
            dwu_ref[...] = jnp.zeros_like(dwu_ref)
            dwd_ref[...] = jnp.zeros_like(dwd_ref)

        hv = h_ref[...]
        dwg_ref[0] += _dot(hv, dgate_ref[0], TN)
        dwu_ref[0] += _dot(hv, dup_ref[0], TN)
        dwd_ref[0] += _dot(act_ref[0], dyh_ref[...], TN)

    row = pl.BlockSpec((tk, d), lambda j, k: (k, 0))
    blk = pl.BlockSpec((1, tk, fb), lambda j, k: (j, k, 0))
    return _call(
        body, name=name, grid=(nb, n // tk),
        in_specs=[row, row, blk, blk, blk],
        out_specs=[pl.BlockSpec((1, d, fb), lambda j, k: (j, 0, 0)),
                   pl.BlockSpec((1, d, fb), lambda j, k: (j, 0, 0)),
                   pl.BlockSpec((1, fb, d), lambda j, k: (j, 0, 0))],
        out_shape=[_sds((nb, d, fb), F32), _sds((nb, d, fb), F32), _sds((nb, fb, d), F32)],
    )(h, dyh, dgate, dup, act)


def final_loss(x, gnorm, target, name):
    n, d = x.shape
    tm = min(ROW_TILE, n)

    def body(x_ref, g_ref, t_ref, dx_ref, dg_ref, loss_ref):
        @pl.when(pl.program_id(0) == 0)
        def _():
            dg_ref[...] = jnp.zeros_like(dg_ref)
            loss_ref[...] = jnp.zeros_like(loss_ref)

        xv = x_ref[...]
        y = xv * _rms_scale(xv) * g_ref[...]
        err = y - t_ref[...]
        part = 0.5 * jnp.sum(jnp.mean(err * err, axis=-1, keepdims=True), axis=0, keepdims=True)
        loss_ref[...] += jnp.broadcast_to(part, loss_ref.shape)
        dx, dg = _rms_bwd(err * (1.0 / d), xv, g_ref[...])
        dx_ref[...] = dx
        dg_ref[...] += dg

    row = pl.BlockSpec((tm, d), lambda i: (i, 0))
    vec = pl.BlockSpec((1, d), lambda i: (0, 0))
    return _call(
        body, name=name, grid=(n // tm,),
        in_specs=[row, vec, row],
        out_specs=[row, vec, pl.BlockSpec((1, LANES), lambda i: (0, 0))],
        out_shape=[_sds((n, d), F32), _sds((1, d), F32), _sds((1, LANES), F32)],
    )(x, gnorm, target)


def in_proj_fwd(x, gnorm, w, name):
    n, d = x.shape
    cols = w.shape[1]
    tm = min(ROW_TILE, n)
    tn = 640

    def body(x_ref, g_ref, w_ref, p_ref, h_ref):
        @pl.when(pl.program_id(1) == 0)
        def _():
            xv = x_ref[...]
            h_ref[...] = (xv * _rms_scale(xv) * g_ref[...]).astype(BF16)

        p_ref[...] = _dot(h_ref[...], w_ref[...], NN)

    return _call(
        body, name=name, grid=(n // tm, cols // tn),
        in_specs=[pl.BlockSpec((tm, d), lambda i, j: (i, 0)),
                  pl.BlockSpec((1, d), lambda i, j: (0, 0)),
                  pl.BlockSpec((d, tn), lambda i, j: (0, j))],
        out_specs=[pl.BlockSpec((tm, tn), lambda i, j: (i, j)),
                   pl.BlockSpec((tm, d), lambda i, j: (i, 0))],
        out_shape=[_sds((n, cols), F32), _sds((n, d), BF16)],
    )(x, gnorm, w)


def in_proj_bwd_x(dproj, w, x, gnorm, dres, name):
    n, d = x.shape
    cols = w.shape[1]
    tm = min(ROW_TILE, n)

    def body(dp_ref, w_ref, x_ref, g_ref, dr_ref, dx_ref, dg_ref):
        @pl.when(pl.program_id(0) == 0)
        def _():
            dg_ref[...] = jnp.zeros_like(dg_ref)

        dh = _dot(dp_ref[...], w_ref[...], NT)
        dxn, dg = _rms_bwd(dh, x_ref[...], g_ref[...])
        dx_ref[...] = dr_ref[...] + dxn
        dg_ref[...] += dg

    row = pl.BlockSpec((tm, d), lambda i: (i, 0))
    vec = pl.BlockSpec((1, d), lambda i: (0, 0))
    return _call(
        body, name=name, grid=(n // tm,),
        in_specs=[pl.BlockSpec((tm, cols), lambda i: (i, 0)),
                  pl.BlockSpec((d, cols), lambda i: (0, 0)), row, vec, row],
        out_specs=[row, vec],
        out_shape=[_sds((n, d), F32), _sds((1, d), F32)],
    )(dproj, w, x, gnorm, dres)


def matmul_tn(a, b, tn, name):
    n, ka = a.shape
    cb = b.shape[1]
    tk = min(ROW_TILE, n)

    def body(a_ref, b_ref, o_ref):
        @pl.when(pl.program_id(1) == 0)
        def _():
            o_ref[...] = jnp.zeros_like(o_ref)

        o_ref[...] += _dot(a_ref[...], b_ref[...], TN)

    return _call(
        body, name=name, grid=(cb // tn, n // tk),
        in_specs=[pl.BlockSpec((tk, ka), lambda j, k: (k, 0)),
                  pl.BlockSpec((tk, tn), lambda j, k: (k, j))],
        out_specs=pl.BlockSpec((ka, tn), lambda j, k: (0, j)),
        out_shape=_sds((ka, cb), F32),
    )(a, b)


def out_proj_fwd(x, sg_out, dn_out, w, name):
    n, d = x.shape
    tm = min(ROW_TILE, n)

    def body(x_ref, a_ref, b_ref, w_ref, o_ref):
        o_ref[...] = (x_ref[...] + _dot(a_ref[...], w_ref[0:HALF_W, :], NN)
                      + _dot(b_ref[...], w_ref[HALF_W:2 * HALF_W, :], NN))

    row = pl.BlockSpec((tm, d), lambda i: (i, 0))
    half = pl.BlockSpec((tm, HALF_W), lambda i: (i, 0))
    return _call(
        body, name=name, grid=(n // tm,),
        in_specs=[row, half, half, pl.BlockSpec((2 * HALF_W, d), lambda i: (0, 0))],
        out_specs=row, out_shape=_sds((n, d), F32),
    )(x, sg_out, dn_out, w)


def out_proj_bwd_x(dy, w, name):
    n, d = dy.shape
    tm = min(ROW_TILE, n)

    def body(dy_ref, w_ref, dsg_ref, ddn_ref, dyb_ref):
        dyb = dy_ref[...].astype(BF16)
        dyb_ref[...] = dyb
        dsg_ref[...] = _dot(dyb, w_ref[0:HALF_W, :], NT)
        ddn_ref[...] = _dot(dyb, w_ref[HALF_W:2 * HALF_W, :], NT)

    row = pl.BlockSpec((tm, d), lambda i: (i, 0))
    half = pl.BlockSpec((tm, HALF_W), lambda i: (i, 0))
    return _call(
        body, name=name, grid=(n // tm,),
        in_specs=[row, pl.BlockSpec((2 * HALF_W, d), lambda i: (0, 0))],
        out_specs=[half, half, row],
        out_shape=[_sds((n, HALF_W), F32), _sds((n, HALF_W), F32), _sds((n, d), BF16)],
    )(dy, w)


def _sg_group_masks():
    col = _iota2((SG_CHUNK, HALF_W), 1)
    return [jnp.logical_and(col >= g * SG_GROUP_DIM, col < (g + 1) * SG_GROUP_DIM)
            for g in range(SG_GROUPS)]


def _sg_causal():
    return _iota2((SG_CHUNK, SG_CHUNK), 0) >= _iota2((SG_CHUNK, SG_CHUNK), 1)


def _sg_forward_chunk(pu, pv, ln_g, ln_b, wc, bias, masks):
    u = _gelu(pu)
    v = _gelu(pv)
    mu = jnp.mean(v, axis=-1, keepdims=True)
    vc = v - mu
    rs = lax.rsqrt(jnp.mean(vc * vc, axis=-1, keepdims=True) + EPS)
    xhat = vc * rs
    vn = (xhat * ln_g + ln_b).astype(BF16)
    vs = bias
    for g in range(SG_GROUPS):
        vs = vs + jnp.where(masks[g], _dot(wc[g], vn, NN), 0.0)
    return u, xhat, rs, vn, vs


def sg_fwd(proj, ln_g, ln_b, w_s, bias_tile, name):
    n = proj.shape[0]
    tm = min(ROW_TILE, n)

    def body(pu_ref, pv_ref, g_ref, b_ref, w_ref, bias_ref, o_ref):
        causal = _sg_causal()
        wc = [jnp.where(causal, w_ref[g], 0.0).astype(BF16) for g in range(SG_GROUPS)]
        masks = _sg_group_masks()
        for ci in range(tm // SG_CHUNK):
            rows = slice(ci * SG_CHUNK, (ci + 1) * SG_CHUNK)
            u, _, _, _, vs = _sg_forward_chunk(pu_ref[rows, :], pv_ref[rows, :], g_ref[...],
                                               b_ref[...], wc, bias_ref[...], masks)
            o_ref[rows, :] = (u * vs).astype(BF16)

    vec = pl.BlockSpec((1, HALF_W), lambda i: (0, 0))
    return _call(
        body, name=name, grid=(n // tm,),
        in_specs=[pl.BlockSpec((tm, HALF_W), lambda i: (i, 0)),
                  pl.BlockSpec((tm, HALF_W), lambda i: (i, 1)), vec, vec,
                  pl.BlockSpec((SG_GROUPS, SG_CHUNK, SG_CHUNK), lambda i: (0, 0, 0)),
                  pl.BlockSpec((SG_CHUNK, HALF_W), lambda i: (0, 0))],
        out_specs=pl.BlockSpec((tm, HALF_W), lambda i: (i, 0)),
        out_shape=_sds((n, HALF_W), BF16),
    )(proj, proj, ln_g, ln_b, w_s, bias_tile)


def sg_bwd(dsg, proj, ln_g, ln_b, w_s, bias_tile, name):
    n = proj.shape[0]
    tm = min(ROW_TILE, n)

    def body(d_ref, pu_ref, pv_ref, g_ref, b_ref, w_ref, bias_ref,
             dp_ref, dw_ref, db_ref, dlg_ref, dlb_ref):
        @pl.when(pl.program_id(0) == 0)
        def _():
            dw_ref[...] = jnp.zeros_like(dw_ref)
            db_ref[...] = jnp.zeros_like(db_ref)
            dlg_ref[...] = jnp.zeros_like(dlg_ref)
            dlb_ref[...] = jnp.zeros_like(dlb_ref)

        causal = _sg_causal()
        wc = [jnp.where(causal, w_ref[g], 0.0).astype(BF16) for g in range(SG_GROUPS)]
        masks = _sg_group_masks()
        ln_g_v = g_ref[...]
        for ci in range(tm // SG_CHUNK):
            rows = slice(ci * SG_CHUNK, (ci + 1) * SG_CHUNK)
            pu = pu_ref[rows, :]
            pv = pv_ref[rows, :]
            u, xhat, rs, vn, vs = _sg_forward_chunk(pu, pv, ln_g_v, b_ref[...], wc,
                                                    bias_ref[...], masks)
            dout = d_ref[rows, :]
            dp_ref[rows, 0:HALF_W] = dout * vs * _gelu_grad(pu)
            dvs = dout * u
            dvs_b = dvs.astype(BF16)
            db_ref[...] += dvs
            dvn = jnp.zeros_like(dvs)
            for g in range(SG_GROUPS):
                dvn = dvn + jnp.where(masks[g], _dot(wc[g], dvs_b, TN), 0.0)
                dwg = _dot(jnp.where(masks[g], dvs_b, jnp.zeros_like(dvs_b)), vn, NT)
                dw_ref[g] += jnp.where(causal, dwg, 0.0)
            dlg_ref[...] += jnp.sum(dvn * xhat, axis=0, keepdims=True)
            dlb_ref[...] += jnp.sum(dvn, axis=0, keepdims=True)
            dxh = dvn * ln_g_v
            dv = rs * (dxh - jnp.mean(dxh, axis=-1, keepdims=True)
                       - xhat * jnp.mean(dxh * xhat, axis=-1, keepdims=True))
            dp_ref[rows, HALF_W:2 * HALF_W] = dv * _gelu_grad(pv)

    vec = pl.BlockSpec((1, HALF_W), lambda i: (0, 0))
    wspec = pl.BlockSpec((SG_GROUPS, SG_CHUNK, SG_CHUNK), lambda i: (0, 0, 0))
    tile = pl.BlockSpec((SG_CHUNK, HALF_W), lambda i: (0, 0))
    return _call(
        body, name=name, grid=(n // tm,),
        in_specs=[pl.BlockSpec((tm, HALF_W), lambda i: (i, 0)),
                  pl.BlockSpec((tm, HALF_W), lambda i: (i, 0)),
                  pl.BlockSpec((tm, HALF_W), lambda i: (i, 1)), vec, vec, wspec, tile],
        out_specs=[pl.BlockSpec((tm, 2 * HALF_W), lambda i: (i, 0)), wspec, tile, vec, vec],
        out_shape=[_sds((n, 2 * HALF_W), F32), _sds((SG_GROUPS, SG_CHUNK, SG_CHUNK), F32),
                   _sds((SG_CHUNK, HALF_W), F32), _sds((1, HALF_W), F32), _sds((1, HALF_W), F32)],
    )(dsg, proj, proj, ln_g, ln_b, w_s, bias_tile)


CONV_K = 4
CONV_BLOCK = 256


def _shift_down(x, s):
    if s == 0:
        return x
    rolled = pltpu.roll(x, s, 0)
    return jnp.where(_iota2(x.shape, 0) >= s, rolled, 0.0)


def _shift_up(x, s):
    if s == 0:
        return x
    t_len = x.shape[0]
    rolled = pltpu.roll(x, t_len - s, 0)
    return jnp.where(_iota2(x.shape, 0) < t_len - s, rolled, 0.0)


def _conv(x, w):
    y = _shift_down(x, CONV_K - 1) * w[0:1, :]
    for j in range(1, CONV_K):
        y = y + _shift_down(x, CONV_K - 1 - j) * w[j:j + 1, :]
    return y


def dn_conv_fwd(proj3, conv_w, name):
    b, t, _ = proj3.shape
    nblk = 3 * HALF_W // CONV_BLOCK
    first = 2 * HALF_W // CONV_BLOCK
    n_norm = 2 * HALF_W // CONV_BLOCK

    def body(x_ref, w_ref, o_ref):
        s = pl.program_id(1)
        y = _conv(x_ref[0], w_ref[...])
        y = y * _sigmoid(y)

        @pl.when(s < n_norm)
        def _():
            for h in range(CONV_BLOCK // HEAD_DIM):
                cs = slice(h * HEAD_DIM, (h + 1) * HEAD_DIM)
                yh = y[:, cs]
                o_ref[0, :, cs] = yh * lax.rsqrt(jnp.sum(yh * yh, axis=-1, keepdims=True) + EPS)

        @pl.when(s >= n_norm)
        def _():
            o_ref[0] = y

    return _call(
        body, name=name, grid=(b, nblk),
        in_specs=[pl.BlockSpec((1, t, CONV_BLOCK), lambda i, s: (i, 0, first + s)),
                  pl.BlockSpec((CONV_K, CONV_BLOCK), lambda i, s: (0, s))],
        out_specs=pl.BlockSpec((1, t, CONV_BLOCK), lambda i, s: (i, 0, s)),
        out_shape=_sds((b, t, 3 * HALF_W), F32),
    )(proj3, conv_w)


def dn_conv_bwd(dqkv, proj3, conv_w, name):
    b, t, _ = proj3.shape
    nblk = 3 * HALF_W // CONV_BLOCK
    first = 2 * HALF_W // CONV_BLOCK
    n_norm = 2 * HALF_W // CONV_BLOCK

    def body(d_ref, x_ref, w_ref, dx_ref, dw_ref, ds_ref):
        s = pl.program_id(0)

        @pl.when(pl.program_id(1) == 0)
        def _():
            dw_ref[...] = jnp.zeros_like(dw_ref)

        x = x_ref[0]
        w = w_ref[...]
        c = _conv(x, w)
        sg = _sigmoid(c)
        y = c * sg

        @pl.when(s < n_norm)
        def _():
            for h in range(CONV_BLOCK // HEAD_DIM):
                cs = slice(h * HEAD_DIM, (h + 1) * HEAD_DIM)
                yh = y[:, cs]
                r = lax.rsqrt(jnp.sum(yh * yh, axis=-1, keepdims=True) + EPS)
                nh = yh * r
                dn = d_ref[0, :, cs]
                ds_ref[:, cs] = r * (dn - nh * jnp.sum(dn * nh, axis=-1, keepdims=True))

        @pl.when(s >= n_norm)
        def _():
            ds_ref[...] = d_ref[0]

        dc = ds_ref[...] * (sg * (1.0 + c * (1.0 - sg)))
        dx = _shift_up(dc, CONV_K - 1) * w[0:1, :]
        for j in range(1, CONV_K):
            dx = dx + _shift_up(dc, CONV_K - 1 - j) * w[j:j + 1, :]
        dx_ref[0] = dx
        for j in range(CONV_K):
            dw_ref[j:j + 1, :] += jnp.sum(dc * _shift_down(x, CONV_K - 1 - j), axis=0, keepdims=True)

    return _call(
        body, name=name, grid=(nblk, b),
        in_specs=[pl.BlockSpec((1, t, CONV_BLOCK), lambda s, i: (i, 0, s)),
                  pl.BlockSpec((1, t, CONV_BLOCK), lambda s, i: (i, 0, first + s)),
                  pl.BlockSpec((CONV_K, CONV_BLOCK), lambda s, i: (0, s))],
        out_specs=[pl.BlockSpec((1, t, CONV_BLOCK), lambda s, i: (i, 0, s)),
                   pl.BlockSpec((CONV_K, CONV_BLOCK), lambda s, i: (0, s))],
        out_shape=[_sds((b, t, 3 * HALF_W), F32), _sds((CONV_K, 3 * HALF_W), F32)],
        scratch=[pltpu.VMEM((t, CONV_BLOCK), F32)],
    )(dqkv, proj3, conv_w)


def _chunk_masks():
    ii = _iota2((DN_CHUNK, DN_CHUNK), 0)
    jj = _iota2((DN_CHUNK, DN_CHUNK), 1)
    return ii >= jj, ii > jj, ii == jj


def _inv_unit_lower(l_mat, eye):
    p = -l_mat
    t = jnp.where(eye, 1.0, 0.0) + p
    size = 2
    while size < DN_CHUNK:
        p = _hdot(p, p)
        t = t + _hdot(t, p)
        size *= 2
    return t


def _gates(pba, ea_row, dtb_row):
    beta = _sigmoid(pba)
    g = -ea_row * _softplus(pba + dtb_row)
    return beta, g


def _chunk_decay(gcol):
    incl, strict, eye = _chunk_masks()
    grow = jnp.sum(jnp.where(eye, gcol, 0.0), axis=0, keepdims=True)
    decay = jnp.where(incl, jnp.exp(jnp.where(incl, gcol - grow, 0.0)), 0.0)
    return decay, incl, strict, eye


def dn_chunk_fwd(qkv, proj3, alog_row, dtb_row, name):
    b, t, _ = qkv.shape
    rblk = min(256, t)
    n_in = rblk // DN_CHUNK

    def body(q_ref, k_ref, v_ref, pba_ref, al_ref, dtb_ref,
             u_ref, w_ref, qd_ref, kd_ref, qk_ref, ti_ref, gc_ref):
        ea = jnp.exp(al_ref[...])
        tri = jnp.where(_chunk_masks()[0], 1.0, 0.0)

        def chunk(ci, carry):
            rows = pl.ds(pl.multiple_of(ci * DN_CHUNK, DN_CHUNK), DN_CHUNK)
            beta_all, g_all = _gates(pba_ref[0, rows, :], ea, dtb_ref[...])
            gc = _hdot(tri, g_all)
            gc_ref[0, rows, :] = gc
            for h in range(N_HEADS):
                cs = slice(h * HEAD_DIM, (h + 1) * HEAD_DIM)
                q = q_ref[0, rows, cs] * QK_SCALE
                k = k_ref[0, rows, cs]
                v = v_ref[0, rows, cs]
                beta = beta_all[:, h:h + 1]
                gcol = gc[:, N_HEADS + h:N_HEADS + h + 1]
                decay, incl, strict, eye = _chunk_decay(gcol)
                kb = k * beta
                l_mat = jnp.where(strict, _bdot(kb, k, NT) * decay, 0.0)
                tinv = _inv_unit_lower(l_mat, eye)
                eg = jnp.exp(gcol)
                u_ref[0, rows, cs] = _hdot(tinv, v * beta)
                w_ref[0, rows, cs] = _hdot(tinv, kb * eg)
                qk_ref[0, h, rows, :] = _bdot(q, k, NT) * decay
                ti_ref[0, h, rows, :] = tinv
                qd_ref[0, rows, cs] = q * eg
                kd_ref[0, rows, cs] = k * jnp.exp(gcol[DN_CHUNK - 1:DN_CHUNK, :] - gcol)
            return carry

        lax.fori_loop(0, n_in, chunk, 0)

    def seg(cblk):
        return pl.BlockSpec((1, rblk, HALF_W), lambda i, r: (i, r, cblk))

    vec = pl.BlockSpec((1, LANES), lambda i, r: (0, 0))
    wide = pl.BlockSpec((1, rblk, HALF_W), lambda i, r: (i, r, 0))
    sq = pl.BlockSpec((1, N_HEADS, rblk, DN_CHUNK), lambda i, r: (i, 0, r, 0))
    return _call(
        body, name=name, grid=(b, t // rblk),
        in_specs=[seg(0), seg(1), seg(2),
                  pl.BlockSpec((1, rblk, LANES), lambda i, r: (i, r, GATE_COL_BLOCK)), vec, vec],
        out_specs=[wide, wide, wide, wide, sq, sq,
                   pl.BlockSpec((1, rblk, LANES), lambda i, r: (i, r, 0))],
        out_shape=[_sds((b, t, HALF_W), F32)] * 4
        + [_sds((b, N_HEADS, t, DN_CHUNK), F32)] * 2 + [_sds((b, t, LANES), F32)],
    )(qkv, qkv, qkv, proj3, alog_row, dtb_row)


def dn_scan_fwd(u, w, qd, kd, qk, gc, name):
    b, t, _ = u.shape
    nc = t // DN_CHUNK
    bh = b * N_HEADS

    def body(u_ref, w_ref, qd_ref, kd_ref, qk_ref, gc_ref, o_ref, sin_ref, s_ref):
        @pl.when(pl.program_id(0) == 0)
        def _():
            s_ref[...] = jnp.zeros_like(s_ref)

        for bi in range(b):
            for h in range(N_HEADS):
                cs = slice(h * HEAD_DIM, (h + 1) * HEAD_DIM)
                idx = bi * N_HEADS + h
                s = s_ref[idx]
                sin_ref[0, idx] = s
                sb = s.astype(BF16)
                v_new = u_ref[bi, :, cs] - _bdot(w_ref[bi, :, cs], sb, NN)
                v_new_b = v_new.astype(BF16)
                o_ref[bi, :, cs] = (_bdot(qd_ref[bi, :, cs], sb, NN)
                                    + _bdot(qk_ref[bi, h], v_new_b, NN))
                gl = jnp.exp(gc_ref[bi, DN_CHUNK - 1:DN_CHUNK, N_HEADS + h:N_HEADS + h + 1])
                s_ref[idx] = s * gl + _bdot(kd_ref[bi, :, cs], v_new_b, TN)

    wide = pl.BlockSpec((b, DN_CHUNK, HALF_W), lambda c: (0, c, 0))
    return _call(
        body, name=name, grid=(nc,),
        in_specs=[wide, wide, wide, wide,
                  pl.BlockSpec((b, N_HEADS, DN_CHUNK, DN_CHUNK), lambda c: (0, 0, c, 0)),
                  pl.BlockSpec((b, DN_CHUNK, LANES), lambda c: (0, c, 0))],
        out_specs=[wide, pl.BlockSpec((1, bh, HEAD_DIM, HEAD_DIM), lambda c: (c, 0, 0, 0))],
        out_shape=[_sds((b, t, HALF_W), F32), _sds((nc, bh, HEAD_DIM, HEAD_DIM), F32)],
        scratch=[pltpu.VMEM((bh, HEAD_DIM, HEAD_DIM), F32)],
    )(u, w, qd, kd, qk, gc)


def dn_scan_bwd(do, u, w, qd, kd, qk, gc, s_in, name):
    b, t, _ = u.shape
    nc = t // DN_CHUNK
    bh = b * N_HEADS

    def body(do_ref, u_ref, w_ref, qd_ref, kd_ref, qk_ref, gc_ref, sin_ref,
             du_ref, dw_ref, dqd_ref, dkd_ref, dqk_ref, dgc_ref, ds_ref):
        @pl.when(pl.program_id(0) == 0)
        def _():
            ds_ref[...] = jnp.zeros_like(ds_ref)

        last_row = _iota2((DN_CHUNK, LANES), 0) == DN_CHUNK - 1
        lane = _iota2((DN_CHUNK, LANES), 1)
        for bi in range(b):
            dgc = jnp.zeros((DN_CHUNK, LANES), F32)
            for h in range(N_HEADS):
                cs = slice(h * HEAD_DIM, (h + 1) * HEAD_DIM)
                idx = bi * N_HEADS + h
                s = sin_ref[0, idx]
                sb = s.astype(BF16)
                wv = w_ref[bi, :, cs].astype(BF16)
                qdv = qd_ref[bi, :, cs].astype(BF16)
                kdv = kd_ref[bi, :, cs].astype(BF16)
                qkv_ = qk_ref[bi, h].astype(BF16)
                dov = do_ref[bi, :, cs].astype(BF16)
                v_new_b = (u_ref[bi, :, cs] - _dot(wv, sb, NN)).astype(BF16)
                gl = jnp.exp(gc_ref[bi, DN_CHUNK - 1:DN_CHUNK, N_HEADS + h:N_HEADS + h + 1])
                ds = ds_ref[idx]
                dsb = ds.astype(BF16)
                dgl = jnp.sum(jnp.sum(ds * s, axis=1, keepdims=True), axis=0, keepdims=True)
                dkd_ref[bi, :, cs] = _dot(v_new_b, dsb, NT)
                dvn = _dot(kdv, dsb, NN) + _dot(qkv_, dov, TN)
                dqd_ref[bi, :, cs] = _dot(dov, sb, NT)
                dqk_ref[bi, h] = _dot(dov, v_new_b, NT)
                du_ref[bi, :, cs] = dvn
                dvn_b = dvn.astype(BF16)
                dw_ref[bi, :, cs] = -_dot(dvn_b, sb, NT)
                ds_ref[idx] = ds * gl + _dot(qdv, dov, TN) - _dot(wv, dvn_b, TN)
                dgc = dgc + jnp.where(jnp.logical_and(last_row, lane == N_HEADS + h), dgl * gl, 0.0)
            dgc_ref[bi] = dgc

    def rev(c):
        return nc - 1 - c

    wide = pl.BlockSpec((b, DN_CHUNK, HALF_W), lambda c: (0, rev(c), 0))
    sq = pl.BlockSpec((b, N_HEADS, DN_CHUNK, DN_CHUNK), lambda c: (0, 0, rev(c), 0))
    gates = pl.BlockSpec((b, DN_CHUNK, LANES), lambda c: (0, rev(c), 0))
    return _call(
        body, name=name, grid=(nc,),
        in_specs=[wide, wide, wide, wide, wide, sq, gates,
                  pl.BlockSpec((1, bh, HEAD_DIM, HEAD_DIM), lambda c: (rev(c), 0, 0, 0))],
        out_specs=[wide, wide, wide, wide, sq, gates],
        out_shape=[_sds((b, t, HALF_W), F32)] * 4
        + [_sds((b, N_HEADS, t, DN_CHUNK), F32), _sds((b, t, LANES), F32)],
        scratch=[pltpu.VMEM((bh, HEAD_DIM, HEAD_DIM), F32)],
    )(do, u, w, qd, kd, qk, gc, s_in)


def dn_chunk_bwd(qkv, proj3, alog_row, dtb_row, tinv, u, w, du, dw, dqd, dkd, dqk, dgc_scan, name):
    b, t, _ = qkv.shape
    rblk = min(256, t)
    n_in = rblk // DN_CHUNK

    def body(q_ref, k_ref, v_ref, pba_ref, al_ref, dtb_ref, ti_ref, u_ref, w_ref,
             du_ref, dw_ref, dqd_ref, dkd_ref, dqk_ref, dgs_ref,
             dq_ref, dpba_ref, dal_ref, ddtb_ref):
        @pl.when(jnp.logical_and(pl.program_id(0) == 0, pl.program_id(1) == 0))
        def _():
            dal_ref[...] = jnp.zeros_like(dal_ref)
            ddtb_ref[...] = jnp.zeros_like(ddtb_ref)

        ea = jnp.exp(al_ref[...])
        incl0 = _chunk_masks()[0]
        tri = jnp.where(incl0, 1.0, 0.0)
        tri_up = jnp.where(_iota2((DN_CHUNK, DN_CHUNK), 1) >= _iota2((DN_CHUNK, DN_CHUNK), 0), 1.0, 0.0)
        lane = _iota2((DN_CHUNK, LANES), 1)
        last_col = _iota2((DN_CHUNK, 1), 0) == DN_CHUNK - 1

        def chunk(ci, carry):
            rows = pl.ds(pl.multiple_of(ci * DN_CHUNK, DN_CHUNK), DN_CHUNK)
            pba = pba_ref[0, rows, :]
            beta_all, g_all = _gates(pba, ea, dtb_ref[...])
            gc = _hdot(tri, g_all)
            dgc_tile = dgs_ref[0, rows, :]
            dbeta_tile = jnp.zeros((DN_CHUNK, LANES), F32)
            for h in range(N_HEADS):
                cs = slice(h * HEAD_DIM, (h + 1) * HEAD_DIM)
                q = q_ref[0, rows, cs] * QK_SCALE
                k = k_ref[0, rows, cs]
                v = v_ref[0, rows, cs]
                beta = beta_all[:, h:h + 1]
                gcol = gc[:, N_HEADS + h:N_HEADS + h + 1]
                decay, incl, strict, eye = _chunk_decay(gcol)
                eg = jnp.exp(gcol)
                ek = jnp.exp(gcol[DN_CHUNK - 1:DN_CHUNK, :] - gcol)
                kb = k * beta
                qb, kbf, kbb = q.astype(BF16), k.astype(BF16), kb.astype(BF16)
                tv = ti_ref[0, h, rows, :]
                uu = u_ref[0, rows, cs]
                ww = w_ref[0, rows, cs]
                dru = _hdot(tv, du_ref[0, rows, cs], TN)
                drw = _hdot(tv, dw_ref[0, rows, cs], TN)
                dl = -jnp.where(strict, _hdot(dru, uu, NT) + _hdot(drw, ww, NT), 0.0)
                dkb = drw * eg
                dg = jnp.sum(drw * kb * eg, axis=-1, keepdims=True)
                m_kk = _dot(kbb, kbf, NT)
                a_qk = _dot(qb, kbf, NT)
                dqk = dqk_ref[0, h, rows, :]
                dm = (dl * decay).astype(BF16)
                da = (dqk * decay).astype(BF16)
                e_mat = (dl * m_kk + dqk * a_qk) * decay
                dkb = dkb + _dot(dm, kbf, NN)
                dk = _dot(dm, kbb, TN) + _dot(da, qb, TN)
                dqs = _dot(da, kbf, NN)
                dg = dg + jnp.sum(e_mat, axis=1, keepdims=True) \
                    - _row_to_col(jnp.sum(e_mat, axis=0, keepdims=True))
                dqd = dqd_ref[0, rows, cs]
                dqs = dqs + dqd * eg
                dg = dg + jnp.sum(dqd * q * eg, axis=-1, keepdims=True)
                dkd = dkd_ref[0, rows, cs]
                dk = dk + dkd * ek
                tk_ = jnp.sum(dkd * k * ek, axis=-1, keepdims=True)
                dg = dg - tk_ + jnp.where(last_col, jnp.sum(tk_, axis=0, keepdims=True), 0.0)
                dk = dk + dkb * beta
                dbeta = jnp.sum(dkb * k, axis=-1, keepdims=True) + jnp.sum(dru * v, axis=-1, keepdims=True)
                dq_ref[0, rows, cs] = dqs * QK_SCALE
                dq_ref[0, rows, pl.ds(HALF_W + h * HEAD_DIM, HEAD_DIM)] = dk
                dq_ref[0, rows, pl.ds(2 * HALF_W + h * HEAD_DIM, HEAD_DIM)] = dru * beta
                dgc_tile = dgc_tile + jnp.where(lane == N_HEADS + h, dg, 0.0)
                dbeta_tile = dbeta_tile + jnp.where(lane == h, dbeta, 0.0)
            dg_tile = _hdot(tri_up, dgc_tile)
            da_pre = dg_tile * (-ea) * _sigmoid(pba + dtb_ref[...])
            gate_lane = jnp.logical_and(lane >= N_HEADS, lane < 2 * N_HEADS)
            dal_ref[...] += jnp.sum(jnp.where(gate_lane, dg_tile * g_all, 0.0), axis=0, keepdims=True)
            ddtb_ref[...] += jnp.sum(jnp.where(gate_lane, da_pre, 0.0), axis=0, keepdims=True)
            dpba_ref[0, rows, :] = jnp.where(lane < N_HEADS, dbeta_tile * beta_all * (1.0 - beta_all),
                                             jnp.where(gate_lane, da_pre, 0.0))
            return carry

        lax.fori_loop(0, n_in, chunk, 0)

    def seg(cblk):
        return pl.BlockSpec((1, rblk, HALF_W), lambda i, r: (i, r, cblk))

    vec = pl.BlockSpec((1, LANES), lambda i, r: (0, 0))
    wide = pl.BlockSpec((1, rblk, HALF_W), lambda i, r: (i, r, 0))
    sq = pl.BlockSpec((1, N_HEADS, rblk, DN_CHUNK), lambda i, r: (i, 0, r, 0))
    gates = pl.BlockSpec((1, rblk, LANES), lambda i, r: (i, r, 0))
    return _call(
        body, name=name, grid=(b, t // rblk),
        in_specs=[seg(0), seg(1), seg(2),
                  pl.BlockSpec((1, rblk, LANES), lambda i, r: (i, r, GATE_COL_BLOCK)), vec, vec,
                  sq, wide, wide, wide, wide, wide, wide, sq, gates],
        out_specs=[pl.BlockSpec((1, rblk, 3 * HALF_W), lambda i, r: (i, r, 0)), gates, vec, vec],
        out_shape=[_sds((b, t, 3 * HALF_W), F32), _sds((b, t, LANES), F32),
                   _sds((1, LANES), F32), _sds((1, LANES), F32)],
    )(qkv, qkv, qkv, proj3, alog_row, dtb_row, tinv, u, w, du, dw, dqd, dkd, dqk, dgc_scan)


def dn_out_fwd(o, proj, dn_norm, name):
    n = o.shape[0]
    tm = min(ROW_TILE, n)

    def body(o_ref, z_ref, g_ref, y_ref):
        for h in range(N_HEADS):
            cs = slice(h * HEAD_DIM, (h + 1) * HEAD_DIM)
            oh = o_ref[:, cs]
            z = z_ref[:, cs]
            y = oh * _rms_scale(oh) * g_ref[...]
            y_ref[:, cs] = (y * (z * _sigmoid(z))).astype(BF16)

    half = pl.BlockSpec((tm, HALF_W), lambda i: (i, 0))
    return _call(
        body, name=name, grid=(n // tm,),
        in_specs=[half, pl.BlockSpec((tm, HALF_W), lambda i: (i, 5)),
                  pl.BlockSpec((1, HEAD_DIM), lambda i: (0, 0))],
        out_specs=half, out_shape=_sds((n, HALF_W), BF16),
    )(o, proj, dn_norm)


def dn_out_bwd(dy, o, proj, dn_norm, name):
    n = o.shape[0]
    tm = min(ROW_TILE, n)

    def body(dy_ref, o_ref, z_ref, g_ref, do_ref, dz_ref, dg_ref):
        @pl.when(pl.program_id(0) == 0)
        def _():
            dg_ref[...] = jnp.zeros_like(dg_ref)

        g = g_ref[...]
        dg = jnp.zeros_like(g)
        for h in range(N_HEADS):
            cs = slice(h * HEAD_DIM, (h + 1) * HEAD_DIM)
            oh = o_ref[:, cs]
            z = z_ref[:, cs]
            d = dy_ref[:, cs]
            r = _rms_scale(oh)
            nh = oh * r
            sz = _sigmoid(z)
            dyn = d * (z * sz)
            dz_ref[:, cs] = d * (nh * g) * (sz * (1.0 + z * (1.0 - sz)))
            dg = dg + jnp.sum(dyn * nh, axis=0, keepdims=True)
            dn = dyn * g
            do_ref[:, cs] = r * (dn - nh * jnp.mean(dn * nh, axis=-1, keepdims=True))
        dg_ref[...] += dg

    half = pl.BlockSpec((tm, HALF_W), lambda i: (i, 0))
    vec = pl.BlockSpec((1, HEAD_DIM), lambda i: (0, 0))
    return _call(
        body, name=name, grid=(n // tm,),
        in_specs=[half, half, pl.BlockSpec((tm, HALF_W), lambda i: (i, 5)), vec],
        out_specs=[half, half, vec],
        out_shape=[_sds((n, HALF_W), F32), _sds((n, HALF_W), F32), _sds((1, HEAD_DIM), F32)],
    )(dy, o, proj, dn_norm)


def _adamw_math(w, g, m, v):
    m_new = ADAM_B1 * m + (1.0 - ADAM_B1) * g
    v_new = ADAM_B2 * v + (1.0 - ADAM_B2) * (g * g)
    m_hat = m_new / (1.0 - ADAM_B1 ** ADAM_STEP)
    v_hat = v_new / (1.0 - ADAM_B2 ** ADAM_STEP)
    delta = -ADAM_LR * (m_hat / (jnp.sqrt(v_hat) + ADAM_EPS) + ADAM_WD * w)
    return delta, m_new, v_new


def adamw(w, g, m, v, name):
    r, c = w.shape
    tr = r
    for cand in (256, 352):
        if r % cand == 0 and r > cand:
            tr = cand
            break

    def body(w_ref, g_ref, m_ref, v_ref, d_ref, mo_ref, vo_ref):
        d, mn, vn = _adamw_math(w_ref[...], g_ref[...], m_ref[...], v_ref[...])
        d_ref[...] = d
        mo_ref[...] = mn
        vo_ref[...] = vn

    spec = pl.BlockSpec((tr, c), lambda i: (i, 0))
    return _call(
        body, name=name, grid=(r // tr,),
        in_specs=[spec] * 4, out_specs=[spec] * 3, out_shape=[_sds((r, c), F32)] * 3,
    )(w, g, m, v)


def _place():
    return lax.axis_index("x"), lax.axis_index("y"), lax.axis_index("c")


def _other_chips(x, y):
    return [(1 - x, y), (x, 1 - y), (1 - x, 1 - y)]


_ANY = pl.BlockSpec(memory_space=pl.ANY)


def all_gather_chips(arrs, name):
    n = len(arrs)

    def body(*refs):
        ins, outs = refs[:n], refs[n:2 * n]
        send, recv, loc = refs[2 * n:]
        x, y, c = _place()
        j = 2 * x + y
        chips = _other_chips(x, y)
        local = [pltpu.make_async_copy(ins[a], outs[a].at[j], loc.at[a]) for a in range(n)]
        for cp in local:
            cp.start()
        sends = []
        for a in range(n):
            for k, (px, py) in enumerate(chips):
                sends.append(pltpu.make_async_remote_copy(
                    src_ref=ins[a], dst_ref=outs[a].at[j], send_sem=send.at[3 * a + k],
                    recv_sem=recv.at[3 * a + k], device_id=(px, py, c), device_id_type=MESH))
        for cp in sends:
            cp.start()
        for a in range(n):
            for k, (px, py) in enumerate(chips):
                pltpu.make_async_remote_copy(
                    src_ref=ins[a], dst_ref=outs[a].at[2 * px + py], send_sem=send.at[3 * a + k],
                    recv_sem=recv.at[3 * a + k], device_id=(px, py, c), device_id_type=MESH).wait_recv()
        for cp in sends:
            cp.wait_send()
        for cp in local:
            cp.wait()

    return _call(
        body, name=name, in_specs=[_ANY] * n, out_specs=[_ANY] * n,
        out_shape=[_sds((N_SHARD,) + a.shape, a.dtype) for a in arrs],
        scratch=[pltpu.SemaphoreType.DMA((3 * n,)), pltpu.SemaphoreType.DMA((3 * n,)),
                 pltpu.SemaphoreType.DMA((n,))],
    )(*arrs)


def pair_exchange(arrs, name):
    n = len(arrs)

    def body(*refs):
        ins, outs = refs[:n], refs[n:2 * n]
        send, recv = refs[2 * n:]
        x, y, c = _place()
        cps = []
        for a in range(n):
            rh = arrs[a].shape[1] // 2
            cps.append(pltpu.make_async_remote_copy(
                src_ref=ins[a].at[:, pl.ds((1 - c) * rh, rh), :], dst_ref=outs[a],
                send_sem=send.at[a], recv_sem=recv.at[a], device_id=(x, y, 1 - c), device_id_type=MESH))
        for cp in cps:
            cp.start()
        for cp in cps:
            cp.wait_recv()
        for cp in cps:
            cp.wait_send()

    return _call(
        body, name=name, in_specs=[_ANY] * n, out_specs=[_ANY] * n,
        out_shape=[_sds((a.shape[0], a.shape[1] // 2, a.shape[2]), a.dtype) for a in arrs],
        scratch=[pltpu.SemaphoreType.DMA((n,)), pltpu.SemaphoreType.DMA((n,))],
    )(*arrs)


def pair_add(g, s, c_idx, name):
    nb, r, cols = g.shape
    rh = r // 2

    def body(c_ref, g_ref, s_ref, o_ref):
        o_ref[...] = (g_ref[...] + s_ref[...]).astype(BF16)

    return pl.pallas_call(
        body, name=name,
        grid_spec=pltpu.PrefetchScalarGridSpec(
            num_scalar_prefetch=1, grid=(nb,),
            in_specs=[pl.BlockSpec((1, rh, cols), lambda j, c: (j, c[0], 0)),
                      pl.BlockSpec((1, rh, cols), lambda j, c: (j, 0, 0))],
            out_specs=pl.BlockSpec((1, rh, cols), lambda j, c: (j, 0, 0))),
        out_shape=_sds((nb, rh, cols), BF16),
        compiler_params=pltpu.CompilerParams(dimension_semantics=("arbitrary",),
                                             vmem_limit_bytes=VMEM_LIMIT),
    )(c_idx, g, s)


def chip_exchange(arrs, name):
    n = len(arrs)

    def body(*refs):
        ins, outs = refs[:n], refs[n:2 * n]
        send, recv, loc = refs[2 * n:]
        x, y, c = _place()
        j = 2 * x + y
        chips = _other_chips(x, y)
        local = [pltpu.make_async_copy(ins[a].at[j], outs[a].at[j], loc.at[a]) for a in range(n)]
        for cp in local:
            cp.start()
        sends = []
        for a in range(n):
            for k, (px, py) in enumerate(chips):
                sends.append(pltpu.make_async_remote_copy(
                    src_ref=ins[a].at[2 * px + py], dst_ref=outs[a].at[j], send_sem=send.at[3 * a + k],
                    recv_sem=recv.at[3 * a + k], device_id=(px, py, c), device_id_type=MESH))
        for cp in sends:
            cp.start()
        for a in range(n):
            for k, (px, py) in enumerate(chips):
                pltpu.make_async_remote_copy(
                    src_ref=ins[a].at[j], dst_ref=outs[a].at[2 * px + py], send_sem=send.at[3 * a + k],
                    recv_sem=recv.at[3 * a + k], device_id=(px, py, c), device_id_type=MESH).wait_recv()
        for cp in sends:
            cp.wait_send()
        for cp in local:
            cp.wait()

    return _call(
        body, name=name, in_specs=[_ANY] * n, out_specs=[_ANY] * n,
        out_shape=[_sds(a.shape, a.dtype) for a in arrs],
        scratch=[pltpu.SemaphoreType.DMA((3 * n,)), pltpu.SemaphoreType.DMA((3 * n,)),
                 pltpu.SemaphoreType.DMA((n,))],
    )(*arrs)


def sum_chips(r, name):
    nb, rh, cols = r.shape

    def body(r_ref, o_ref):
        acc = r_ref[0].astype(F32) + r_ref[1].astype(F32)
        for j in range(2, nb):
            acc = acc + r_ref[j].astype(F32)
        o_ref[...] = acc

    return _call(
        body, name=name, grid=(1,),
        in_specs=[pl.BlockSpec((nb, rh, cols), lambda i: (0, 0, 0))],
        out_specs=pl.BlockSpec((rh, cols), lambda i: (0, 0)),
        out_shape=_sds((rh, cols), F32),
    )(r)


def pair_concat(arrs, name):
    n = len(arrs)

    def body(*refs):
        ins, outs = refs[:n], refs[n:2 * n]
        send, recv, loc = refs[2 * n:]
        x, y, c = _place()
        local, sends = [], []
        for a in range(n):
            rh = arrs[a].shape[0]
            mine = outs[a].at[pl.ds(c * rh, rh), :]
            local.append(pltpu.make_async_copy(ins[a], mine, loc.at[a]))
            sends.append(pltpu.make_async_remote_copy(
                src_ref=ins[a], dst_ref=mine, send_sem=send.at[a], recv_sem=recv.at[a],
                device_id=(x, y, 1 - c), device_id_type=MESH))
        for cp in local + sends:
            cp.start()
        for a in range(n):
            rh = arrs[a].shape[0]
            pltpu.make_async_remote_copy(
                src_ref=ins[a], dst_ref=outs[a].at[pl.ds((1 - c) * rh, rh), :], send_sem=send.at[a],
                recv_sem=recv.at[a], device_id=(x, y, 1 - c), device_id_type=MESH).wait_recv()
        for cp in sends:
            cp.wait_send()
        for cp in local:
            cp.wait()

    return _call(
        body, name=name, in_specs=[_ANY] * n, out_specs=[_ANY] * n,
        out_shape=[_sds((2 * a.shape[0], a.shape[1]), a.dtype) for a in arrs],
        scratch=[pltpu.SemaphoreType.DMA((n,)), pltpu.SemaphoreType.DMA((n,)),
                 pltpu.SemaphoreType.DMA((n,))],
    )(*arrs)


N_DEV = 8


def all_reduce_small(pack, name):
    r, cols = pack.shape

    def body(in_ref, out_ref, buf, send, recv):
        x, y, c = _place()
        me = 4 * x + 2 * y + c
        buf[me] = in_ref[...]
        peers = []
        for k in range(1, N_DEV):
            fx, fy, fc = (k >> 2) & 1, (k >> 1) & 1, k & 1
            peers.append((1 - x if fx else x, 1 - y if fy else y, 1 - c if fc else c))
        sends = [pltpu.make_async_remote_copy(
            src_ref=in_ref, dst_ref=buf.at[me], send_sem=send.at[k], recv_sem=recv.at[k],
            device_id=p, device_id_type=MESH) for k, p in enumerate(peers)]
        for cp in sends:
            cp.start()
        for k, (px, py, pc) in enumerate(peers):
            pltpu.make_async_remote_copy(
                src_ref=in_ref, dst_ref=buf.at[4 * px + 2 * py + pc], send_sem=send.at[k],
                recv_sem=recv.at[k], device_id=(px, py, pc), device_id_type=MESH).wait_recv()
        for cp in sends:
            cp.wait_send()
        acc = buf[0] + buf[1]
        for i in range(2, N_DEV):
            acc = acc + buf[i]
        out_ref[...] = acc

    vm = pl.BlockSpec(memory_space=pltpu.VMEM)
    return _call(
        body, name=name, in_specs=[vm], out_specs=vm, out_shape=_sds((r, cols), F32),
        scratch=[pltpu.VMEM((N_DEV, r, cols), F32), pltpu.SemaphoreType.DMA((N_DEV - 1,)),
                 pltpu.SemaphoreType.DMA((N_DEV - 1,))],
    )(pack)


SMALL_NAMES = ("ffn1_norm", "mix_norm", "ffn2_norm", "final_norm", "sg_ln_g", "sg_ln_b",
               "dn_norm", "a_log", "dt_bias", "sg_b", "sg_w", "conv_w")


def _to_rows(a):
    flat = a.reshape(-1)
    pad = (-flat.shape[0]) % LANES
    if pad:
        flat = jnp.pad(flat, (0, pad))
    return flat.reshape(-1, LANES)


def _pack_small(parts):
    rows = [_to_rows(parts[k]) for k in SMALL_NAMES]
    pack = jnp.concatenate(rows, axis=0)
    pad = (-pack.shape[0]) % 8
    if pad:
        pack = jnp.pad(pack, ((0, pad), (0, 0)))
    return pack


def _unpack_small(pack, shapes):
    out, r0 = {}, 0
    for k in SMALL_NAMES:
        size = 1
        for s in shapes[k]:
            size *= s
        nrows = -(-size // LANES)
        out[k] = pack[r0:r0 + nrows].reshape(-1)[:size].reshape(shapes[k])
        r0 += nrows
    return out


def kernel(x, ffn1_norm, ffn1_w_gate, ffn1_w_up, ffn1_w_down, mix_norm, w_in, conv_w, a_log, dt_bias, dn_norm, sg_ln_g, sg_ln_b, sg_w, sg_b, w_out, ffn2_norm, ffn2_w_gate, ffn2_w_up, ffn2_w_down, final_norm, loss_target, m_ffn1_norm, m_ffn1_w_gate, m_ffn1_w_up, m_ffn1_w_down, m_mix_norm, m_w_in, m_conv_w, m_a_log, m_dt_bias, m_dn_norm, m_sg_ln_g, m_sg_ln_b, m_sg_w, m_sg_b, m_w_out, m_ffn2_norm, m_ffn2_w_gate, m_ffn2_w_up, m_ffn2_w_down, m_final_norm, v_ffn1_norm, v_ffn1_w_gate, v_ffn1_w_up, v_ffn1_w_down, v_mix_norm, v_w_in, v_conv_w, v_a_log, v_dt_bias, v_dn_norm, v_sg_ln_g, v_sg_ln_b, v_sg_w, v_sg_b, v_w_out, v_ffn2_norm, v_ffn2_w_gate, v_ffn2_w_up, v_ffn2_w_down, v_final_norm):
    bsz, t_len, d = x.shape
    n = bsz * t_len
    xy, yy, cc = _place()
    shard = 2 * xy + yy

    big_names = ["ffn1_w_gate", "ffn1_w_up", "ffn1_w_down", "w_in", "w_out",
                 "ffn2_w_gate", "ffn2_w_up", "ffn2_w_down"]
    big_w = dict(ffn1_w_gate=ffn1_w_gate, ffn1_w_up=ffn1_w_up, ffn1_w_down=ffn1_w_down, w_in=w_in,
                 w_out=w_out, ffn2_w_gate=ffn2_w_gate, ffn2_w_up=ffn2_w_up, ffn2_w_down=ffn2_w_down)
    big_m = dict(ffn1_w_gate=m_ffn1_w_gate, ffn1_w_up=m_ffn1_w_up, ffn1_w_down=m_ffn1_w_down, w_in=m_w_in,
                 w_out=m_w_out, ffn2_w_gate=m_ffn2_w_gate, ffn2_w_up=m_ffn2_w_up, ffn2_w_down=m_ffn2_w_down)
    big_v = dict(ffn1_w_gate=v_ffn1_w_gate, ffn1_w_up=v_ffn1_w_up, ffn1_w_down=v_ffn1_w_down, w_in=v_w_in,
                 w_out=v_w_out, ffn2_w_gate=v_ffn2_w_gate, ffn2_w_up=v_ffn2_w_up, ffn2_w_down=v_ffn2_w_down)
    gathered = all_gather_chips([big_w[k][0].astype(BF16) for k in big_names] + [conv_w[0]],
                                name="gather_weights")
    gw = dict(zip(big_names, gathered[:-1]))
    conv_full = gathered[-1].transpose(1, 0, 2).reshape(CONV_K, 3 * HALF_W)
    w_in_full = gw["w_in"].transpose(1, 0, 2).reshape(d, IN_COLS)
    w_in_full = jnp.pad(w_in_full, ((0, 0), (0, PROJ_W - IN_COLS)))
    w_out_full = gw["w_out"].reshape(2 * HALF_W, d)

    x0 = x.reshape(n, d)
    x1, h1, gate1, up1 = ffn_fwd(x0, ffn1_norm, gw["ffn1_w_gate"], gw["ffn1_w_up"],
                                 gw["ffn1_w_down"], name="ffn1_fwd")
    proj, h2 = in_proj_fwd(x1, mix_norm, w_in_full, name="in_proj_fwd")
    proj3 = proj.reshape(bsz, t_len, PROJ_W)
    bias_tile = jnp.repeat(sg_b[0].T, SG_GROUP_DIM, axis=1)
    sg_out = sg_fwd(proj, sg_ln_g, sg_ln_b, sg_w[0], bias_tile, name="sg_fwd")
    qkv = dn_conv_fwd(proj3, conv_full, name="dn_conv_fwd")
    alog_row = jnp.zeros((1, LANES), F32).at[0, N_HEADS:2 * N_HEADS].set(a_log[0])
    dtb_row = jnp.zeros((1, LANES), F32).at[0, N_HEADS:2 * N_HEADS].set(dt_bias[0])
    u_wy, w_wy, q_dec, k_dec, qk, tinv, gc = dn_chunk_fwd(qkv, proj3, alog_row, dtb_row,
                                                           name="dn_chunk_fwd")
    o, s_in = dn_scan_fwd(u_wy, w_wy, q_dec, k_dec, qk, gc, name="dn_scan_fwd")
    dn_out = dn_out_fwd(o.reshape(n, HALF_W), proj, dn_norm, name="dn_out_fwd")
    x2 = out_proj_fwd(x1, sg_out, dn_out, w_out_full, name="out_proj_fwd")
    x3, h3, gate2, up2 = ffn_fwd(x2, ffn2_norm, gw["ffn2_w_gate"], gw["ffn2_w_up"],
                                 gw["ffn2_w_down"], name="ffn2_fwd")
    dx3, d_final_norm, loss_tile = final_loss(x3, final_norm.reshape(1, d),
                                              loss_target.reshape(n, d), name="final_loss")
    loss = lax.psum(loss_tile[0, 0], ("x", "y", "c"))

    dx2, dgate2, dup2, act2, dyh2, d_ffn2_norm = ffn_bwd_act(
        dx3, x2, ffn2_norm, gate2, up2, gw["ffn2_w_gate"], gw["ffn2_w_up"], gw["ffn2_w_down"],
        name="ffn2_bwd_act")
    g_big = {}
    g_big["ffn2_w_gate"], g_big["ffn2_w_up"], g_big["ffn2_w_down"] = ffn_bwd_w(
        h3, dyh2, dgate2, dup2, act2, name="ffn2_bwd_w")

    d_sg, d_dn, dx2b = out_proj_bwd_x(dx2, w_out_full, name="out_proj_bwd_x")
    g_w_out = jnp.concatenate([matmul_tn(sg_out, dx2b, d, name="w_out_grad_sg"),
                               matmul_tn(dn_out, dx2b, d, name="w_out_grad_dn")], axis=0)
    g_big["w_out"] = g_w_out.reshape(N_SHARD, (2 * HALF_W) // N_SHARD, d)

    d_o, d_z, d_dn_norm = dn_out_bwd(d_dn, o.reshape(n, HALF_W), proj, dn_norm, name="dn_out_bwd")
    du, dw, dqd, dkd, dqk, dgc_scan = dn_scan_bwd(d_o.reshape(bsz, t_len, HALF_W), u_wy, w_wy, q_dec,
                                                  k_dec, qk, gc, s_in, name="dn_scan_bwd")
    d_qkv, d_pba, d_alog_row, d_dtb_row = dn_chunk_bwd(
        qkv, proj3, alog_row, dtb_row, tinv, u_wy, w_wy, du, dw, dqd, dkd, dqk, dgc_scan,
        name="dn_chunk_bwd")
    d_pqkv, d_conv = dn_conv_bwd(d_qkv, proj3, conv_full, name="dn_conv_bwd")
    d_psg, d_sg_w, d_bias_tile, d_ln_g, d_ln_b = sg_bwd(d_sg, proj, sg_ln_g, sg_ln_b, sg_w[0],
                                                        bias_tile, name="sg_bwd")
    d_proj = jnp.concatenate([d_psg, d_pqkv.reshape(n, 3 * HALF_W), d_z, d_pba.reshape(n, LANES)],
                             axis=1).astype(BF16)
    dx1, d_mix_norm = in_proj_bwd_x(d_proj, w_in_full, x1, mix_norm, dx2, name="in_proj_bwd_x")
    g_w_in = matmul_tn(h2, d_proj, 640, name="w_in_grad")[:, :IN_COLS]
    g_big["w_in"] = g_w_in.reshape(d, N_SHARD, IN_COLS // N_SHARD).transpose(1, 0, 2)

    dx0, dgate1, dup1, act1, dyh1, d_ffn1_norm = ffn_bwd_act(
        dx1, x0, ffn1_norm, gate1, up1, gw["ffn1_w_gate"], gw["ffn1_w_up"], gw["ffn1_w_down"],
        name="ffn1_bwd_act")
    g_big["ffn1_w_gate"], g_big["ffn1_w_up"], g_big["ffn1_w_down"] = ffn_bwd_w(
        h1, dyh1, dgate1, dup1, act1, name="ffn1_bwd_w")
    grad_x = dx0.reshape(bsz, t_len, d)

    g_list = [g_big[k] for k in big_names]
    from_sibling = pair_exchange(g_list, name="grad_pair_exchange")
    c_idx = jnp.reshape(cc, (1,)).astype(jnp.int32)
    pair_sums = [pair_add(g, s, c_idx, name="grad_pair_add_" + k)
                 for k, g, s in zip(big_names, g_list, from_sibling)]
    from_chips = chip_exchange(pair_sums, name="grad_chip_exchange")
    halves = [sum_chips(r, name="grad_chip_sum_" + k) for k, r in zip(big_names, from_chips)]
    full = pair_concat(halves, name="grad_pair_concat")
    outs = {}
    for k, g in zip(big_names, full):
        delta, m_new, v_new = adamw(big_w[k][0], g, big_m[k][0], big_v[k][0], name="adamw_" + k)
        outs[k] = (g[None], delta[None], m_new[None], v_new[None])

    small_w = dict(ffn1_norm=ffn1_norm, mix_norm=mix_norm, ffn2_norm=ffn2_norm, final_norm=final_norm,
                   sg_ln_g=sg_ln_g, sg_ln_b=sg_ln_b, dn_norm=dn_norm, a_log=a_log, dt_bias=dt_bias,
                   sg_b=sg_b, sg_w=sg_w)
    small_m = dict(ffn1_norm=m_ffn1_norm, mix_norm=m_mix_norm, ffn2_norm=m_ffn2_norm,
                   final_norm=m_final_norm, sg_ln_g=m_sg_ln_g, sg_ln_b=m_sg_ln_b, dn_norm=m_dn_norm,
                   a_log=m_a_log, dt_bias=m_dt_bias, sg_b=m_sg_b, sg_w=m_sg_w)
    small_v = dict(ffn1_norm=v_ffn1_norm, mix_norm=v_mix_norm, ffn2_norm=v_ffn2_norm,
                   final_norm=v_final_norm, sg_ln_g=v_sg_ln_g, sg_ln_b=v_sg_ln_b, dn_norm=v_dn_norm,
                   a_log=v_a_log, dt_bias=v_dt_bias, sg_b=v_sg_b, sg_w=v_sg_w)
    shapes = {k: small_w[k].shape for k in small_w}
    shapes["conv_w"] = (CONV_K, 3 * HALF_W)
    d_sg_b = d_bias_tile.reshape(SG_CHUNK, SG_GROUPS, SG_GROUP_DIM).sum(axis=-1).T
    small_g = dict(ffn1_norm=d_ffn1_norm, mix_norm=d_mix_norm, ffn2_norm=d_ffn2_norm,
                   final_norm=d_final_norm, sg_ln_g=d_ln_g, sg_ln_b=d_ln_b, dn_norm=d_dn_norm,
                   a_log=d_alog_row[:, N_HEADS:2 * N_HEADS], dt_bias=d_dtb_row[:, N_HEADS:2 * N_HEADS],
                   sg_b=d_sg_b, sg_w=d_sg_w, conv_w=d_conv)
    g_pack = all_reduce_small(_pack_small(small_g), name="small_all_reduce")
    g_small = _unpack_small(g_pack, shapes)
    cw = 3 * HALF_W // N_SHARD
    g_conv = lax.dynamic_slice_in_dim(g_small["conv_w"], shard * cw, cw, axis=1)
    zero_conv = jnp.zeros((CONV_K, 3 * HALF_W), F32)

    def packed(src, conv):
        parts = dict(src)
        parts["conv_w"] = lax.dynamic_update_slice_in_dim(zero_conv, conv[0], shard * cw, axis=1)
        return _pack_small(parts)

    d_pack, m_pack, v_pack = adamw(packed(small_w, conv_w), g_pack, packed(small_m, m_conv_w),
                                   packed(small_v, v_conv_w), name="adamw_small")
    d_small = _unpack_small(d_pack, shapes)
    m_small = _unpack_small(m_pack, shapes)
    v_small = _unpack_small(v_pack, shapes)

    def conv_block(full_arr):
        return lax.dynamic_slice_in_dim(full_arr, shard * cw, cw, axis=1)[None]

    for k in small_w:
        outs[k] = (g_small[k].reshape(small_w[k].shape), d_small[k], m_small[k], v_small[k])
    outs["conv_w"] = (g_conv[None], conv_block(d_small["conv_w"]), conv_block(m_small["conv_w"]),
                      conv_block(v_small["conv_w"]))

    order = ["ffn1_norm", "ffn1_w_gate", "ffn1_w_up", "ffn1_w_down", "mix_norm", "w_in", "conv_w",
             "a_log", "dt_bias", "dn_norm", "sg_ln_g", "sg_ln_b", "sg_w", "sg_b", "w_out", "ffn2_norm",
             "ffn2_w_gate", "ffn2_w_up", "ffn2_w_down", "final_norm"]
    return (loss, grad_x, *[outs[k][0] for k in order], *[outs[k][1] for k in order],
            *[outs[k][2] for k in order], *[outs[k][3] for k in order])
```

```python
import functools

import jax
import jax.numpy as jnp
from jax import lax
from jax.experimental import pallas as pl
from jax.experimental.pallas import tpu as pltpu

F32 = jnp.float32
BF16 = jnp.bfloat16
EPS = 1e-6

D_MODEL = 1024
N_SHARD = 4
HEAD_DIM = 128
N_HEADS = 4
DN_CHUNK = 64
SG_CHUNK = 128
SG_GROUPS = 8
SG_GROUP_DIM = 64
HALF_W = 512
PROJ_W = 3200
IN_COLS = 3080
GATE_COL_BLOCK = 24
QK_SCALE = HEAD_DIM ** -0.5
LANES = 128

ADAM_LR = 0.001
ADAM_B1 = 0.9
ADAM_B2 = 0.999
ADAM_EPS = 1e-08
ADAM_WD = 0.01
ADAM_STEP = 10

VMEM_LIMIT = 56 * 1024 * 1024
ROW_TILE = 512

NN = ((1,), (0,))
NT = ((1,), (1,))
TN = ((0,), (0,))
MESH = pl.DeviceIdType.MESH


def _dot(a, b, dims):
    return lax.dot_general(a, b, (dims, ((), ())), preferred_element_type=F32)


def _bdot(a, b, dims):
    return _dot(a.astype(BF16), b.astype(BF16), dims)


def _split(a):
    hi = a.astype(BF16)
    lo = (a - hi.astype(F32)).astype(BF16)
    return hi, lo


def _dot3(a, b, dims=NN):
    return _dot(a[0], b[0], dims) + (_dot(a[0], b[1], dims) + _dot(a[1], b[0], dims))


def _dot_exact_lhs(a, b):
    ab = a.astype(BF16)
    b1 = b.astype(BF16)
    r1 = b - b1.astype(F32)
    b2 = r1.astype(BF16)
    b3 = (r1 - b2.astype(F32)).astype(BF16)
    return _dot(ab, b1, NN) + (_dot(ab, b2, NN) + _dot(ab, b3, NN))


def _call(body, *, name, out_shape, in_specs, out_specs, grid=(), scratch=(), **kw):
    params = dict(vmem_limit_bytes=VMEM_LIMIT)
    if grid:
        params["dimension_semantics"] = ("arbitrary",) * len(grid)
    return pl.pallas_call(
        body, name=name, grid=grid, in_specs=in_specs, out_specs=out_specs,
        out_shape=out_shape, scratch_shapes=list(scratch),
        compiler_params=pltpu.CompilerParams(**params), **kw)


def _sds(shape, dtype):
    return jax.ShapeDtypeStruct(tuple(shape), dtype)


def _sigmoid(x):
    return jax.nn.sigmoid(x)


def _softplus(x):
    return jnp.maximum(x, 0.0) + jnp.log(1.0 + jnp.exp(-jnp.abs(x)))


_GELU_C = 0.7978845608028654
_GELU_A = 0.044715


def _gelu(x):
    t = jnp.tanh(_GELU_C * (x + _GELU_A * x * x * x))
    return 0.5 * x * (1.0 + t)


def _gelu_grad(x):
    t = jnp.tanh(_GELU_C * (x + _GELU_A * x * x * x))
    return 0.5 * (1.0 + t) + 0.5 * x * (1.0 - t * t) * _GELU_C * (1.0 + 3.0 * _GELU_A * x * x)


def _silu_grad(x):
    s = _sigmoid(x)
    return s * (1.0 + x * (1.0 - s))


def _rms_scale(xv):
    return lax.rsqrt(jnp.mean(xv * xv, axis=-1, keepdims=True) + EPS)


def _rms_bwd(dh, xv, g):
    r = _rms_scale(xv)
    xn = xv * r
    dg = jnp.sum(dh * xn, axis=0, keepdims=True)
    dxn = dh * g
    dx = r * (dxn - xn * jnp.mean(dxn * xn, axis=-1, keepdims=True))
    return dx, dg


def _iota2(shape, dim):
    return lax.broadcasted_iota(jnp.int32, shape, dim)


def _col_to_row(col):
    n = col.shape[0]
    eye = _iota2((n, n), 0) == _iota2((n, n), 1)
    return jnp.sum(jnp.where(eye, col, 0.0), axis=0, keepdims=True)


def _row_to_col(row):
    n = row.shape[1]
    eye = _iota2((n, n), 0) == _iota2((n, n), 1)
    return jnp.sum(jnp.where(eye, row, 0.0), axis=1, keepdims=True)


def ffn_fwd(x, gnorm, wg, wu, wd, name):
    n, d = x.shape
    nb, _, fb = wg.shape
    tm = min(ROW_TILE, n)

    def body(x_ref, g_ref, wg_ref, wu_ref, wd_ref, xo_ref, h_ref, gate_ref, up_ref, acc_ref):
        j = pl.program_id(1)

        @pl.when(j == 0)
        def _():
            xv = x_ref[...]
            h_ref[...] = (xv * _rms_scale(xv) * g_ref[...]).astype(BF16)
            acc_ref[...] = jnp.zeros_like(acc_ref)

        h = h_ref[...]
        gate = _dot(h, wg_ref[0], NN)
        up = _dot(h, wu_ref[0], NN)
        gate_ref[0] = gate.astype(BF16)
        up_ref[0] = up.astype(BF16)
        act = (gate * _sigmoid(gate) * up).astype(BF16)
        acc_ref[...] += _dot(act, wd_ref[0], NN)

        @pl.when(j == nb - 1)
        def _():
            xo_ref[...] = x_ref[...] + 0.5 * acc_ref[...]

    row = pl.BlockSpec((tm, d), lambda i, j: (i, 0))
    return _call(
        body, name=name, grid=(n // tm, nb),
        in_specs=[row, pl.BlockSpec((1, d), lambda i, j: (0, 0)),
                  pl.BlockSpec((1, d, fb), lambda i, j: (j, 0, 0)),
                  pl.BlockSpec((1, d, fb), lambda i, j: (j, 0, 0)),
                  pl.BlockSpec((1, fb, d), lambda i, j: (j, 0, 0))],
        out_specs=[row, row,
                   pl.BlockSpec((1, tm, fb), lambda i, j: (j, i, 0)),
                   pl.BlockSpec((1, tm, fb), lambda i, j: (j, i, 0))],
        out_shape=[_sds((n, d), F32), _sds((n, d), BF16),
                   _sds((nb, n, fb), BF16), _sds((nb, n, fb), BF16)],
        scratch=[pltpu.VMEM((tm, d), F32)],
    )(x, gnorm, wg, wu, wd)


def ffn_bwd_act(dy, x, gnorm, gate, up, wg, wu, wd, name):
    n, d = x.shape
    nb, _, fb = wg.shape
    tm = min(ROW_TILE, n)

    def body(dy_ref, x_ref, g_ref, gate_ref, up_ref, wg_ref, wu_ref, wd_ref,
             dx_ref, dgate_ref, dup_ref, act_ref, dyh_ref, dg_ref, acc_ref):
        i = pl.program_id(0)
        j = pl.program_id(1)

        @pl.when(jnp.logical_and(i == 0, j == 0))
        def _():
            dg_ref[...] = jnp.zeros_like(dg_ref)

        @pl.when(j == 0)
        def _():
            dyh_ref[...] = (0.5 * dy_ref[...]).astype(BF16)
            acc_ref[...] = jnp.zeros_like(acc_ref)

        dact = _dot(dyh_ref[...], wd_ref[0], NT)
        gt = gate_ref[0].astype(F32)
        u = up_ref[0].astype(F32)
        s = _sigmoid(gt)
        silu = gt * s
        dup = (dact * silu).astype(BF16)
        dgate = (dact * u * (s * (1.0 + gt * (1.0 - s)))).astype(BF16)
        dup_ref[0] = dup
        dgate_ref[0] = dgate
        act_ref[0] = (silu * u).astype(BF16)
        acc_ref[...] += _dot(dgate, wg_ref[0], NT) + _dot(dup, wu_ref[0], NT)

        @pl.when(j == nb - 1)
        def _():
            dxn, dg = _rms_bwd(acc_ref[...], x_ref[...], g_ref[...])
            dx_ref[...] = dy_ref[...] + dxn
            dg_ref[...] += dg

    row = pl.BlockSpec((tm, d), lambda i, j: (i, 0))
    blk = pl.BlockSpec((1, tm, fb), lambda i, j: (j, i, 0))
    vec = pl.BlockSpec((1, d), lambda i, j: (0, 0))
    wcol = pl.BlockSpec((1, d, fb), lambda i, j: (j, 0, 0))
    return _call(
        body, name=name, grid=(n // tm, nb),
        in_specs=[row, row, vec, blk, blk, wcol, wcol,
                  pl.BlockSpec((1, fb, d), lambda i, j: (j, 0, 0))],
        out_specs=[row, blk, blk, blk, row, vec],
        out_shape=[_sds((n, d), F32), _sds((nb, n, fb), BF16), _sds((nb, n, fb), BF16),
                   _sds((nb, n, fb), BF16), _sds((n, d), BF16), _sds((1, d), F32)],
        scratch=[pltpu.VMEM((tm, d), F32)],
    )(dy, x, gnorm, gate, up, wg, wu, wd)


def ffn_bwd_w(h, dyh, dgate, dup, act, name):
    n, d = h.shape
    nb, _, fb = dgate.shape
    tk = min(ROW_TILE, n)

    def body(h_ref, dyh_ref, dgate_ref, dup_ref, act_ref, dwg_ref, dwu_ref, dwd_ref):
        @pl.when(pl.program_id(1) == 0)
        def _():
            dwg_ref[...] = jnp.zeros_like(dwg_ref)
            dwu_ref[...] = jnp.zeros_like(dwu_ref)
            dwd_ref[...] = jnp.zeros_like(dwd_ref)

        hv = h_ref[...]
        dwg_ref[0] += _dot(hv, dgate_ref[0], TN)
        dwu_ref[0] += _dot(hv, dup_ref[0], TN)
        dwd_ref[0] += _dot(act_ref[0], dyh_ref[...], TN)

    row = pl.BlockSpec((tk, d), lambda j, k: (k, 0))
    blk = pl.BlockSpec((1, tk, fb), lambda j, k: (j, k, 0))
    return _call(
        body, name=name, grid=(nb, n // tk),
        in_specs=[row, row, blk, blk, blk],
        out_specs=[pl.BlockSpec((1, d, fb), lambda j, k: (j, 0, 0)),
                   pl.BlockSpec((1, d, fb), lambda j, k: (j, 0, 0)),
                   pl.BlockSpec((1, fb, d), lambda j, k: (j, 0, 0))],
        out_shape=[_sds((nb, d, fb), F32), _sds((nb, d, fb), F32), _sds((nb, fb, d), F32)],
    )(h, dyh, dgate, dup, act)


def final_loss(x, gnorm, target, name):
    n, d = x.shape
    tm = min(ROW_TILE, n)

    def body(x_ref, g_ref, t_ref, dx_ref, dg_ref, loss_ref):
        @pl.when(pl.program_id(0) == 0)
        def _():
            dg_ref[...] = jnp.zeros_like(dg_ref)
            loss_ref[...] = jnp.zeros_like(loss_ref)

        xv = x_ref[...]
        y = xv * _rms_scale(xv) * g_ref[...]
        err = y - t_ref[...]
        part = 0.5 * jnp.sum(jnp.mean(err * err, axis=-1, keepdims=True), axis=0, keepdims=True)
        loss_ref[...] += jnp.broadcast_to(part, loss_ref.shape)
        dx, dg = _rms_bwd(err * (1.0 / d), xv, g_ref[...])
        dx_ref[...] = dx
        dg_ref[...] += dg

    row = pl.BlockSpec((tm, d), lambda i: (i, 0))
    vec = pl.BlockSpec((1, d), lambda i: (0, 0))
    return _call(
        body, name=name, grid=(n // tm,),
        in_specs=[row, vec, row],
        out_specs=[row, vec, pl.BlockSpec((1, LANES), lambda i: (0, 0))],
        out_shape=[_sds((n, d), F32), _sds((1, d), F32), _sds((1, LANES), F32)],
    )(x, gnorm, target)


def in_proj_fwd(x, gnorm, w, name):
    n, d = x.shape
    cols = w.shape[1]
    tm = min(ROW_TILE, n)
    tn = 640

    def body(x_ref, g_ref, w_ref, p_ref, h_ref):
        @pl.when(pl.program_id(1) == 0)
        def _():
            xv = x_ref[...]
            h_ref[...] = (xv * _rms_scale(xv) * g_ref[...]).astype(BF16)

        p_ref[...] = _dot(h_ref[...], w_ref[...], NN)

    return _call(
        body, name=name, grid=(n // tm, cols // tn),
        in_specs=[pl.BlockSpec((tm, d), lambda i, j: (i, 0)),
                  pl.BlockSpec((1, d), lambda i, j: (0, 0)),
                  pl.BlockSpec((d, tn), lambda i, j: (0, j))],
        out_specs=[pl.BlockSpec((tm, tn), lambda i, j: (i, j)),
                   pl.BlockSpec((tm, d), lambda i, j: (i, 0))],
        out_shape=[_sds((n, cols), F32), _sds((n, d), BF16)],
    )(x, gnorm, w)


def in_proj_bwd_x(dproj, w, x, gnorm, dres, name):
    n, d = x.shape
    cols = w.shape[1]
    tm = min(ROW_TILE, n)

    def body(dp_ref, w_ref, x_ref, g_ref, dr_ref, dx_ref, dg_ref):
        @pl.when(pl.program_id(0) == 0)
        def _():
            dg_ref[...] = jnp.zeros_like(dg_ref)

        dh = _dot(dp_ref[...], w_ref[...], NT)
        dxn, dg = _rms_bwd(dh, x_ref[...], g_ref[...])
        dx_ref[...] = dr_ref[...] + dxn
        dg_ref[...] += dg

    row = pl.BlockSpec((tm, d), lambda i: (i, 0))
    vec = pl.BlockSpec((1, d), lambda i: (0, 0))
    return _call(
        body, name=name, grid=(n // tm,),
        in_specs=[pl.BlockSpec((tm, cols), lambda i: (i, 0)),
                  pl.BlockSpec((d, cols), lambda i: (0, 0)), row, vec, row],
        out_specs=[row, vec],
        out_shape=[_sds((n, d), F32), _sds((1, d), F32)],
    )(dproj, w, x, gnorm, dres)


def matmul_tn(a, b, tn, name):
    n, ka = a.shape
    cb = b.shape[1]
    tk = min(ROW_TILE, n)

    def body(a_ref, b_ref, o_ref):
        @pl.when(pl.program_id(1) == 0)
        def _():
            o_ref[...] = jnp.zeros_like(o_ref)

        o_ref[...] += _dot(a_ref[...], b_ref[...], TN)

    return _call(
        body, name=name, grid=(cb // tn, n // tk),
        in_specs=[pl.BlockSpec((tk, ka), lambda j, k: (k, 0)),
                  pl.BlockSpec((tk, tn), lambda j, k: (k, j))],
        out_specs=pl.BlockSpec((ka, tn), lambda j, k: (0, j)),
        out_shape=_sds((ka, cb), F32),
    )(a, b)


def out_proj_fwd(x, sg_out, dn_out, w, name):
    n, d = x.shape
    tm = min(ROW_TILE, n)

    def body(x_ref, a_ref, b_ref, w_ref, o_ref):
        o_ref[...] = (x_ref[...] + _dot(a_ref[...], w_ref[0:HALF_W, :], NN)
                      + _dot(b_ref[...], w_ref[HALF_W:2 * HALF_W, :], NN))

    row = pl.BlockSpec((tm, d), lambda i: (i, 0))
    half = pl.BlockSpec((tm, HALF_W), lambda i: (i, 0))
    return _call(
        body, name=name, grid=(n // tm,),
        in_specs=[row, half, half, pl.BlockSpec((2 * HALF_W, d), lambda i: (0, 0))],
        out_specs=row, out_shape=_sds((n, d), F32),
    )(x, sg_out, dn_out, w)


def out_proj_bwd_x(dy, w, name):
    n, d = dy.shape
    tm = min(ROW_TILE, n)

    def body(dy_ref, w_ref, dsg_ref, ddn_ref, dyb_ref):
        dyb = dy_ref[...].astype(BF16)
        dyb_ref[...] = dyb
        dsg_ref[...] = _dot(dyb, w_ref[0:HALF_W, :], NT)
        ddn_ref[...] = _dot(dyb, w_ref[HALF_W:2 * HALF_W, :], NT)

    row = pl.BlockSpec((tm, d), lambda i: (i, 0))
    half = pl.BlockSpec((tm, HALF_W), lambda i: (i, 0))
    return _call(
        body, name=name, grid=(n // tm,),
        in_specs=[row, pl.BlockSpec((2 * HALF_W, d), lambda i: (0, 0))],
        out_specs=[half, half, row],
        out_shape=[_sds((n, HALF_W), F32), _sds((n, HALF_W), F32), _sds((n, d), BF16)],
    )(dy, w)


def _sg_group_masks():
    col = _iota2((SG_CHUNK, HALF_W), 1)
    return [jnp.logical_and(col >= g * SG_GROUP_DIM, col < (g + 1) * SG_GROUP_DIM)
            for g in range(SG_GROUPS)]


def _sg_causal():
    return _iota2((SG_CHUNK, SG_CHUNK), 0) >= _iota2((SG_CHUNK, SG_CHUNK), 1)


def _sg_forward_chunk(pu, pv, ln_g, ln_b, wc, bias, masks):
    u = _gelu(pu)
    v = _gelu(pv)
    mu = jnp.mean(v, axis=-1, keepdims=True)
    vc = v - mu
    rs = lax.rsqrt(jnp.mean(vc * vc, axis=-1, keepdims=True) + EPS)
    xhat = vc * rs
    vn = (xhat * ln_g + ln_b).astype(BF16)
    vs = bias
    for g in range(SG_GROUPS):
        vs = vs + jnp.where(masks[g], _dot(wc[g], vn, NN), 0.0)
    return u, xhat, rs, vn, vs


def sg_fwd(proj, ln_g, ln_b, w_s, bias_tile, name):
    n = proj.shape[0]
    tm = min(ROW_TILE, n)

    def body(pu_ref, pv_ref, g_ref, b_ref, w_ref, bias_ref, o_ref):
        causal = _sg_causal()
        wc = [jnp.where(causal, w_ref[g], 0.0).astype(BF16) for g in range(SG_GROUPS)]
        masks = _sg_group_masks()
        for ci in range(tm // SG_CHUNK):
            rows = slice(ci * SG_CHUNK, (ci + 1) * SG_CHUNK)
            u, _, _, _, vs = _sg_forward_chunk(pu_ref[rows, :], pv_ref[rows, :], g_ref[...],
                                               b_ref[...], wc, bias_ref[...], masks)
            o_ref[rows, :] = (u * vs).astype(BF16)

    vec = pl.BlockSpec((1, HALF_W), lambda i: (0, 0))
    return _call(
        body, name=name, grid=(n // tm,),
        in_specs=[pl.BlockSpec((tm, HALF_W), lambda i: (i, 0)),
                  pl.BlockSpec((tm, HALF_W), lambda i: (i, 1)), vec, vec,
                  pl.BlockSpec((SG_GROUPS, SG_CHUNK, SG_CHUNK), lambda i: (0, 0, 0)),
                  pl.BlockSpec((SG_CHUNK, HALF_W), lambda i: (0, 0))],
        out_specs=pl.BlockSpec((tm, HALF_W), lambda i: (i, 0)),
        out_shape=_sds((n, HALF_W), BF16),
    )(proj, proj, ln_g, ln_b, w_s, bias_tile)


def sg_bwd(dsg, proj, ln_g, ln_b, w_s, bias_tile, name):
    n = proj.shape[0]
    tm = min(ROW_TILE, n)

    def body(d_ref, pu_ref, pv_ref, g_ref, b_ref, w_ref, bias_ref,
             dp_ref, dw_ref, db_ref, dlg_ref, dlb_ref):
        @pl.when(pl.program_id(0) == 0)
        def _():
            dw_ref[...] = jnp.zeros_like(dw_ref)
            db_ref[...] = jnp.zeros_like(db_ref)
            dlg_ref[...] = jnp.zeros_like(dlg_ref)
            dlb_ref[...] = jnp.zeros_like(dlb_ref)

        causal = _sg_causal()
        wc = [jnp.where(causal, w_ref[g], 0.0).astype(BF16) for g in range(SG_GROUPS)]
        masks = _sg_group_masks()
        ln_g_v = g_ref[...]
        for ci in range(tm // SG_CHUNK):
            rows = slice(ci * SG_CHUNK, (ci + 1) * SG_CHUNK)
            pu = pu_ref[rows, :]
            pv = pv_ref[rows, :]
            u, xhat, rs, vn, vs = _sg_forward_chunk(pu, pv, ln_g_v, b_ref[...], wc,
                                                    bias_ref[...], masks)
            dout = d_ref[rows, :]
            dp_ref[rows, 0:HALF_W] = dout * vs * _gelu_grad(pu)
            dvs = dout * u
            dvs_b = dvs.astype(BF16)
            db_ref[...] += dvs
            dvn = jnp.zeros_like(dvs)
            for g in range(SG_GROUPS):
                dvn = dvn + jnp.where(masks[g], _dot(wc[g], dvs_b, TN), 0.0)
                dwg = _dot(jnp.where(masks[g], dvs_b, jnp.zeros_like(dvs_b)), vn, NT)
                dw_ref[g] += jnp.where(causal, dwg, 0.0)
            dlg_ref[...] += jnp.sum(dvn * xhat, axis=0, keepdims=True)
            dlb_ref[...] += jnp.sum(dvn, axis=0, keepdims=True)
            dxh = dvn * ln_g_v
            dv = rs * (dxh - jnp.mean(dxh, axis=-1, keepdims=True)
                       - xhat * jnp.mean(dxh * xhat, axis=-1, keepdims=True))
            dp_ref[rows, HALF_W:2 * HALF_W] = dv * _gelu_grad(pv)

    vec = pl.BlockSpec((1, HALF_W), lambda i: (0, 0))
    wspec = pl.BlockSpec((SG_GROUPS, SG_CHUNK, SG_CHUNK), lambda i: (0, 0, 0))
    tile = pl.BlockSpec((SG_CHUNK, HALF_W), lambda i: (0, 0))
    return _call(
        body, name=name, grid=(n // tm,),
        in_specs=[pl.BlockSpec((tm, HALF_W), lambda i: (i, 0)),
                  pl.BlockSpec((tm, HALF_W), lambda i: (i, 0)),
                  pl.BlockSpec((tm, HALF_W), lambda i: (i, 1)), vec, vec, wspec, tile],
        out_specs=[pl.BlockSpec((tm, 2 * HALF_W), lambda i: (i, 0)), wspec, tile, vec, vec],
        out_shape=[_sds((n, 2 * HALF_W), F32), _sds((SG_GROUPS, SG_CHUNK, SG_CHUNK), F32),
                   _sds((SG_CHUNK, HALF_W), F32), _sds((1, HALF_W), F32), _sds((1, HALF_W), F32)],
    )(dsg, proj, proj, ln_g, ln_b, w_s, bias_tile)


CONV_K = 4
CONV_BLOCK = 256


def _shift_down(x, s):
    if s == 0:
        return x
    rolled = pltpu.roll(x, s, 0)
    return jnp.where(_iota2(x.shape, 0) >= s, rolled, 0.0)


def _shift_up(x, s):
    if s == 0:
        return x
    t_len = x.shape[0]
    rolled = pltpu.roll(x, t_len - s, 0)
    return jnp.where(_iota2(x.shape, 0) < t_len - s, rolled, 0.0)


def _conv(x, w):
    y = _shift_down(x, CONV_K - 1) * w[0:1, :]
    for j in range(1, CONV_K):
        y = y + _shift_down(x, CONV_K - 1 - j) * w[j:j + 1, :]
    return y


def dn_conv_fwd(proj3, conv_w, name):
    b, t, _ = proj3.shape
    nblk = 3 * HALF_W // CONV_BLOCK
    first = 2 * HALF_W // CONV_BLOCK
    n_norm = 2 * HALF_W // CONV_BLOCK

    def body(x_ref, w_ref, o_ref):
        s = pl.program_id(1)
        y = _conv(x_ref[0], w_ref[...])
        y = y * _sigmoid(y)

        @pl.when(s < n_norm)
        def _():
            for h in range(CONV_BLOCK // HEAD_DIM):
                cs = slice(h * HEAD_DIM, (h + 1) * HEAD_DIM)
                yh = y[:, cs]
                o_ref[0, :, cs] = yh * lax.rsqrt(jnp.sum(yh * yh, axis=-1, keepdims=True) + EPS)

        @pl.when(s >= n_norm)
        def _():
            o_ref[0] = y

    return _call(
        body, name=name, grid=(b, nblk),
        in_specs=[pl.BlockSpec((1, t, CONV_BLOCK), lambda i, s: (i, 0, first + s)),
                  pl.BlockSpec((CONV_K, CONV_BLOCK), lambda i, s: (0, s))],
        out_specs=pl.BlockSpec((1, t, CONV_BLOCK), lambda i, s: (i, 0, s)),
        out_shape=_sds((b, t, 3 * HALF_W), F32),
    )(proj3, conv_w)


def dn_conv_bwd(dqkv, proj3, conv_w, name):
    b, t, _ = proj3.shape
    nblk = 3 * HALF_W // CONV_BLOCK
    first = 2 * HALF_W // CONV_BLOCK
    n_norm = 2 * HALF_W // CONV_BLOCK

    def body(d_ref, x_ref, w_ref, dx_ref, dw_ref, ds_ref):
        s = pl.program_id(0)

        @pl.when(pl.program_id(1) == 0)
        def _():
            dw_ref[...] = jnp.zeros_like(dw_ref)

        x = x_ref[0]
        w = w_ref[...]
        c = _conv(x, w)
        sg = _sigmoid(c)
        y = c * sg

        @pl.when(s < n_norm)
        def _():
            for h in range(CONV_BLOCK // HEAD_DIM):
                cs = slice(h * HEAD_DIM, (h + 1) * HEAD_DIM)
                yh = y[:, cs]
                r = lax.rsqrt(jnp.sum(yh * yh, axis=-1, keepdims=True) + EPS)
                nh = yh * r
                dn = d_ref[0, :, cs]
                ds_ref[:, cs] = r * (dn - nh * jnp.sum(dn * nh, axis=-1, keepdims=True))

        @pl.when(s >= n_norm)
        def _():
            ds_ref[...] = d_ref[0]

        dc = ds_ref[...] * (sg * (1.0 + c * (1.0 - sg)))
        dx = _shift_up(dc, CONV_K - 1) * w[0:1, :]
        for j in range(1, CONV_K):
            dx = dx + _shift_up(dc, CONV_K - 1 - j) * w[j:j + 1, :]
        dx_ref[0] = dx
        for j in range(CONV_K):
            dw_ref[j:j + 1, :] += jnp.sum(dc * _shift_down(x, CONV_K - 1 - j), axis=0, keepdims=True)

    return _call(
        body, name=name, grid=(nblk, b),
        in_specs=[pl.BlockSpec((1, t, CONV_BLOCK), lambda s, i: (i, 0, s)),
                  pl.BlockSpec((1, t, CONV_BLOCK), lambda s, i: (i, 0, first + s)),
                  pl.BlockSpec((CONV_K, CONV_BLOCK), lambda s, i: (0, s))],
        out_specs=[pl.BlockSpec((1, t, CONV_BLOCK), lambda s, i: (i, 0, s)),
                   pl.BlockSpec((CONV_K, CONV_BLOCK), lambda s, i: (0, s))],
        out_shape=[_sds((b, t, 3 * HALF_W), F32), _sds((CONV_K, 3 * HALF_W), F32)],
        scratch=[pltpu.VMEM((t, CONV_BLOCK), F32)],
    )(dqkv, proj3, conv_w)


def _chunk_masks():
    ii = _iota2((DN_CHUNK, DN_CHUNK), 0)
    jj = _iota2((DN_CHUNK, DN_CHUNK), 1)
    return ii >= jj, ii > jj, ii == jj


def _inv_unit_lower(l_mat, eye):
    p = -l_mat
    t = jnp.where(eye, 1.0, 0.0) + p
    ps = _split(p)
    size = 2
    while size < DN_CHUNK:
        p = _dot3(ps, ps)
        ps = _split(p)
        t = t + _dot3(_split(t), ps)
        size *= 2
    return t


def _gates(pba, ea_row, dtb_row):
    beta = _sigmoid(pba)
    g = -ea_row * _softplus(pba + dtb_row)
    return beta, g


def _chunk_decay(gcol):
    incl, strict, eye = _chunk_masks()
    grow = jnp.sum(jnp.where(eye, gcol, 0.0), axis=0, keepdims=True)
    decay = jnp.where(incl, jnp.exp(jnp.where(incl, gcol - grow, 0.0)), 0.0)
    return decay, incl, strict, eye


def dn_chunk_fwd(qkv, proj3, alog_row, dtb_row, name):
    b, t, _ = qkv.shape
    rblk = min(256, t)
    n_in = rblk // DN_CHUNK

    def body(q_ref, k_ref, v_ref, pba_ref, al_ref, dtb_ref,
             u_ref, w_ref, qd_ref, kd_ref, qk_ref, ti_ref, gc_ref):
        ea = jnp.exp(al_ref[...])
        tri = jnp.where(_chunk_masks()[0], 1.0, 0.0)

        def chunk(ci, carry):
            rows = pl.ds(pl.multiple_of(ci * DN_CHUNK, DN_CHUNK), DN_CHUNK)
            beta_all, g_all = _gates(pba_ref[0, rows, :], ea, dtb_ref[...])
            gc = _dot_exact_lhs(tri, g_all)
            gc_ref[0, rows, :] = gc
            for h in range(N_HEADS):
                cs = slice(h * HEAD_DIM, (h + 1) * HEAD_DIM)
                q = q_ref[0, rows, cs] * QK_SCALE
                k = k_ref[0, rows, cs]
                v = v_ref[0, rows, cs]
                beta = beta_all[:, h:h + 1]
                gcol = gc[:, N_HEADS + h:N_HEADS + h + 1]
                decay, incl, strict, eye = _chunk_decay(gcol)
                kb = k * beta
                l_mat = jnp.where(strict, _bdot(kb, k, NT) * decay, 0.0)
                tinv = _inv_unit_lower(l_mat, eye)
                eg = jnp.exp(gcol)
                tsp = _split(tinv)
                u_ref[0, rows, cs] = _dot3(tsp, _split(v * beta))
                w_ref[0, rows, cs] = _dot3(tsp, _split(kb * eg))
                qk_ref[0, h, rows, :] = _bdot(q, k, NT) * decay
                ti_ref[0, h, rows, :] = tinv
                qd_ref[0, rows, cs] = q * eg
                kd_ref[0, rows, cs] = k * jnp.exp(gcol[DN_CHUNK - 1:DN_CHUNK, :] - gcol)
            return carry

        lax.fori_loop(0, n_in, chunk, 0, unroll=2)

    def seg(cblk):
        return pl.BlockSpec((1, rblk, HALF_W), lambda i, r: (i, r, cblk))

    vec = pl.BlockSpec((1, LANES), lambda i, r: (0, 0))
    wide = pl.BlockSpec((1, rblk, HALF_W), lambda i, r: (i, r, 0))
    sq = pl.BlockSpec((1, N_HEADS, rblk, DN_CHUNK), lambda i, r: (i, 0, r, 0))
    return _call(
        body, name=name, grid=(b, t // rblk),
        in_specs=[seg(0), seg(1), seg(2),
                  pl.BlockSpec((1, rblk, LANES), lambda i, r: (i, r, GATE_COL_BLOCK)), vec, vec],
        out_specs=[wide, wide, wide, wide, sq, sq,
                   pl.BlockSpec((1, rblk, LANES), lambda i, r: (i, r, 0))],
        out_shape=[_sds((b, t, HALF_W), F32)] * 4
        + [_sds((b, N_HEADS, t, DN_CHUNK), F32)] * 2 + [_sds((b, t, LANES), F32)],
    )(qkv, qkv, qkv, proj3, alog_row, dtb_row)


def dn_scan_fwd(u, w, qd, kd, qk, gc, name):
    b, t, _ = u.shape
    nc = t // DN_CHUNK
    bh = b * N_HEADS

    def body(u_ref, w_ref, qd_ref, kd_ref, qk_ref, gc_ref, o_ref, sin_ref, s_ref):
        @pl.when(pl.program_id(0) == 0)
        def _():
            s_ref[...] = jnp.zeros_like(s_ref)

        for bi in range(b):
            for h in range(N_HEADS):
                cs = slice(h * HEAD_DIM, (h + 1) * HEAD_DIM)
                idx = bi * N_HEADS + h
                s = s_ref[idx]
                sin_ref[0, idx] = s
                sb = s.astype(BF16)
                v_new = u_ref[bi, :, cs] - _bdot(w_ref[bi, :, cs], sb, NN)
                v_new_b = v_new.astype(BF16)
                o_ref[bi, :, cs] = (_bdot(qd_ref[bi, :, cs], sb, NN)
                                    + _bdot(qk_ref[bi, h], v_new_b, NN))
                gl = jnp.exp(gc_ref[bi, DN_CHUNK - 1:DN_CHUNK, N_HEADS + h:N_HEADS + h + 1])
                s_ref[idx] = s * gl + _bdot(kd_ref[bi, :, cs], v_new_b, TN)

    wide = pl.BlockSpec((b, DN_CHUNK, HALF_W), lambda c: (0, c, 0))
    return _call(
        body, name=name, grid=(nc,),
        in_specs=[wide, wide, wide, wide,
                  pl.BlockSpec((b, N_HEADS, DN_CHUNK, DN_CHUNK), lambda c: (0, 0, c, 0)),
                  pl.BlockSpec((b, DN_CHUNK, LANES), lambda c: (0, c, 0))],
        out_specs=[wide, pl.BlockSpec((1, bh, HEAD_DIM, HEAD_DIM), lambda c: (c, 0, 0, 0))],
        out_shape=[_sds((b, t, HALF_W), F32), _sds((nc, bh, HEAD_DIM, HEAD_DIM), F32)],
        scratch=[pltpu.VMEM((bh, HEAD_DIM, HEAD_DIM), F32)],
    )(u, w, qd, kd, qk, gc)


def dn_scan_bwd(do, u, w, qd, kd, qk, gc, s_in, name):
    b, t, _ = u.shape
    nc = t // DN_CHUNK
    bh = b * N_HEADS

    def body(do_ref, u_ref, w_ref, qd_ref, kd_ref, qk_ref, gc_ref, sin_ref,
             du_ref, dw_ref, dqd_ref, dkd_ref, dqk_ref, dgc_ref, ds_ref):
        @pl.when(pl.program_id(0) == 0)
        def _():
            ds_ref[...] = jnp.zeros_like(ds_ref)

        last_row = _iota2((DN_CHUNK, LANES), 0) == DN_CHUNK - 1
        lane = _iota2((DN_CHUNK, LANES), 1)
        for bi in range(b):
            dgc = jnp.zeros((DN_CHUNK, LANES), F32)
            for h in range(N_HEADS):
                cs = slice(h * HEAD_DIM, (h + 1) * HEAD_DIM)
                idx = bi * N_HEADS + h
                s = sin_ref[0, idx]
                sb = s.astype(BF16)
                wv = w_ref[bi, :, cs].astype(BF16)
                qdv = qd_ref[bi, :, cs].astype(BF16)
                kdv = kd_ref[bi, :, cs].astype(BF16)
                qkv_ = qk_ref[bi, h].astype(BF16)
                dov = do_ref[bi, :, cs].astype(BF16)
                v_new_b = (u_ref[bi, :, cs] - _dot(wv, sb, NN)).astype(BF16)
                gl = jnp.exp(gc_ref[bi, DN_CHUNK - 1:DN_CHUNK, N_HEADS + h:N_HEADS + h + 1])
                ds = ds_ref[idx]
                dsb = ds.astype(BF16)
                dgl = jnp.sum(jnp.sum(ds * s, axis=1, keepdims=True), axis=0, keepdims=True)
                dkd_ref[bi, :, cs] = _dot(v_new_b, dsb, NT)
                dvn = _dot(kdv, dsb, NN) + _dot(qkv_, dov, TN)
                dqd_ref[bi, :, cs] = _dot(dov, sb, NT)
                dqk_ref[bi, h] = _dot(dov, v_new_b, NT)
                du_ref[bi, :, cs] = dvn
                dvn_b = dvn.astype(BF16)
                dw_ref[bi, :, cs] = -_dot(dvn_b, sb, NT)
                ds_ref[idx] = ds * gl + _dot(qdv, dov, TN) - _dot(wv, dvn_b, TN)
                dgc = dgc + jnp.where(jnp.logical_and(last_row, lane == N_HEADS + h), dgl * gl, 0.0)
            dgc_ref[bi] = dgc

    def rev(c):
        return nc - 1 - c

    wide = pl.BlockSpec((b, DN_CHUNK, HALF_W), lambda c: (0, rev(c), 0))
    sq = pl.BlockSpec((b, N_HEADS, DN_CHUNK, DN_CHUNK), lambda c: (0, 0, rev(c), 0))
    gates = pl.BlockSpec((b, DN_CHUNK, LANES), lambda c: (0, rev(c), 0))
    return _call(
        body, name=name, grid=(nc,),
        in_specs=[wide, wide, wide, wide, wide, sq, gates,
                  pl.BlockSpec((1, bh, HEAD_DIM, HEAD_DIM), lambda c: (rev(c), 0, 0, 0))],
        out_specs=[wide, wide, wide, wide, sq, gates],
        out_shape=[_sds((b, t, HALF_W), F32)] * 4
        + [_sds((b, N_HEADS, t, DN_CHUNK), F32), _sds((b, t, LANES), F32)],
        scratch=[pltpu.VMEM((bh, HEAD_DIM, HEAD_DIM), F32)],
    )(do, u, w, qd, kd, qk, gc, s_in)


def dn_chunk_bwd(qkv, proj3, alog_row, dtb_row, tinv, u, w, du, dw, dqd, dkd, dqk, dgc_scan, name):
    b, t, _ = qkv.shape
    rblk = min(256, t)
    n_in = rblk // DN_CHUNK

    def body(q_ref, k_ref, v_ref, pba_ref, al_ref, dtb_ref, ti_ref, u_ref, w_ref,
             du_ref, dw_ref, dqd_ref, dkd_ref, dqk_ref, dgs_ref,
             dq_ref, dpba_ref, dal_ref, ddtb_ref):
        @pl.when(jnp.logical_and(pl.program_id(0) == 0, pl.program_id(1) == 0))
        def _():
            dal_ref[...] = jnp.zeros_like(dal_ref)
            ddtb_ref[...] = jnp.zeros_like(ddtb_ref)

        ea = jnp.exp(al_ref[...])
        incl0 = _chunk_masks()[0]
        tri = jnp.where(incl0, 1.0, 0.0)
        tri_up = jnp.where(_iota2((DN_CHUNK, DN_CHUNK), 1) >= _iota2((DN_CHUNK, DN_CHUNK), 0), 1.0, 0.0)
        lane = _iota2((DN_CHUNK, LANES), 1)
        last_col = _iota2((DN_CHUNK, 1), 0) == DN_CHUNK - 1

        def chunk(ci, carry):
            rows = pl.ds(pl.multiple_of(ci * DN_CHUNK, DN_CHUNK), DN_CHUNK)
            pba = pba_ref[0, rows, :]
            beta_all, g_all = _gates(pba, ea, dtb_ref[...])
            gc = _dot_exact_lhs(tri, g_all)
            dgc_tile = dgs_ref[0, rows, :]
            dbeta_tile = jnp.zeros((DN_CHUNK, LANES), F32)
            for h in range(N_HEADS):
                cs = slice(h * HEAD_DIM, (h + 1) * HEAD_DIM)
                q = q_ref[0, rows, cs] * QK_SCALE
                k = k_ref[0, rows, cs]
                v = v_ref[0, rows, cs]
                beta = beta_all[:, h:h + 1]
                gcol = gc[:, N_HEADS + h:N_HEADS + h + 1]
                decay, incl, strict, eye = _chunk_decay(gcol)
                eg = jnp.exp(gcol)
                ek = jnp.exp(gcol[DN_CHUNK - 1:DN_CHUNK, :] - gcol)
                kb = k * beta
                qb, kbf, kbb = q.astype(BF16), k.astype(BF16), kb.astype(BF16)
                tv = ti_ref[0, h, rows, :]
                uu = u_ref[0, rows, cs]
                ww = w_ref[0, rows, cs]
                tsp = _split(tv)
                dru = _dot3(tsp, _split(du_ref[0, rows, cs]), TN)
                drw = _dot3(tsp, _split(dw_ref[0, rows, cs]), TN)
                dl = -jnp.where(strict, _dot3(_split(dru), _split(uu), NT)
                                + _dot3(_split(drw), _split(ww), NT), 0.0)
                dkb = drw * eg
                dg = jnp.sum(drw * kb * eg, axis=-1, keepdims=True)
                m_kk = _dot(kbb, kbf, NT)
                a_qk = _dot(qb, kbf, NT)
                dqk = dqk_ref[0, h, rows, :]
                dm = (dl * decay).astype(BF16)
                da = (dqk * decay).astype(BF16)
                e_mat = (dl * m_kk + dqk * a_qk) * decay
                dkb = dkb + _dot(dm, kbf, NN)
                dk = _dot(dm, kbb, TN) + _dot(da, qb, TN)
                dqs = _dot(da, kbf, NN)
                dg = dg + jnp.sum(e_mat, axis=1, keepdims=True) \
                    - _row_to_col(jnp.sum(e_mat, axis=0, keepdims=True))
                dqd = dqd_ref[0, rows, cs]
                dqs = dqs + dqd * eg
                dg = dg + jnp.sum(dqd * q * eg, axis=-1, keepdims=True)
                dkd = dkd_ref[0, rows, cs]
                dk = dk + dkd * ek
                tk_ = jnp.sum(dkd * k * ek, axis=-1, keepdims=True)
                dg = dg - tk_ + jnp.where(last_col, jnp.sum(tk_, axis=0, keepdims=True), 0.0)
                dk = dk + dkb * beta
                dbeta = jnp.sum(dkb * k, axis=-1, keepdims=True) + jnp.sum(dru * v, axis=-1, keepdims=True)
                dq_ref[0, rows, cs] = dqs * QK_SCALE
                dq_ref[0, rows, pl.ds(HALF_W + h * HEAD_DIM, HEAD_DIM)] = dk
                dq_ref[0, rows, pl.ds(2 * HALF_W + h * HEAD_DIM, HEAD_DIM)] = dru * beta
                dgc_tile = dgc_tile + jnp.where(lane == N_HEADS + h, dg, 0.0)
                dbeta_tile = dbeta_tile + jnp.where(lane == h, dbeta, 0.0)
            dg_tile = _dot_exact_lhs(tri_up, dgc_tile)
            da_pre = dg_tile * (-ea) * _sigmoid(pba + dtb_ref[...])
            gate_lane = jnp.logical_and(lane >= N_HEADS, lane < 2 * N_HEADS)
            dal_ref[...] += jnp.sum(jnp.where(gate_lane, dg_tile * g_all, 0.0), axis=0, keepdims=True)
            ddtb_ref[...] += jnp.sum(jnp.where(gate_lane, da_pre, 0.0), axis=0, keepdims=True)
            dpba_ref[0, rows, :] = jnp.where(lane < N_HEADS, dbeta_tile * beta_all * (1.0 - beta_all),
                                             jnp.where(gate_lane, da_pre, 0.0))
            return carry

        lax.fori_loop(0, n_in, chunk, 0, unroll=2)

    def seg(cblk):
        return pl.BlockSpec((1, rblk, HALF_W), lambda i, r: (i, r, cblk))

    vec = pl.BlockSpec((1, LANES), lambda i, r: (0, 0))
    wide = pl.BlockSpec((1, rblk, HALF_W), lambda i, r: (i, r, 0))
    sq = pl.BlockSpec((1, N_HEADS, rblk, DN_CHUNK), lambda i, r: (i, 0, r, 0))
    gates = pl.BlockSpec((1, rblk, LANES), lambda i, r: (i, r, 0))
    return _call(
        body, name=name, grid=(b, t // rblk),
        in_specs=[seg(0), seg(1), seg(2),
                  pl.BlockSpec((1, rblk, LANES), lambda i, r: (i, r, GATE_COL_BLOCK)), vec, vec,
                  sq, wide, wide, wide, wide, wide, wide, sq, gates],
        out_specs=[pl.BlockSpec((1, rblk, 3 * HALF_W), lambda i, r: (i, r, 0)), gates, vec, vec],
        out_shape=[_sds((b, t, 3 * HALF_W), F32), _sds((b, t, LANES), F32),
                   _sds((1, LANES), F32), _sds((1, LANES), F32)],
    )(qkv, qkv, qkv, proj3, alog_row, dtb_row, tinv, u, w, du, dw, dqd, dkd, dqk, dgc_scan)


def dn_out_fwd(o, proj, dn_norm, name):
    n = o.shape[0]
    tm = min(ROW_TILE, n)

    def body(o_ref, z_ref, g_ref, y_ref):
        for h in range(N_HEADS):
            cs = slice(h * HEAD_DIM, (h + 1) * HEAD_DIM)
            oh = o_ref[:, cs]
            z = z_ref[:, cs]
            y = oh * _rms_scale(oh) * g_ref[...]
            y_ref[:, cs] = (y * (z * _sigmoid(z))).astype(BF16)

    half = pl.BlockSpec((tm, HALF_W), lambda i: (i, 0))
    return _call(
        body, name=name, grid=(n // tm,),
        in_specs=[half, pl.BlockSpec((tm, HALF_W), lambda i: (i, 5)),
                  pl.BlockSpec((1, HEAD_DIM), lambda i: (0, 0))],
        out_specs=half, out_shape=_sds((n, HALF_W), BF16),
    )(o, proj, dn_norm)


def dn_out_bwd(dy, o, proj, dn_norm, name):
    n = o.shape[0]
    tm = min(ROW_TILE, n)

    def body(dy_ref, o_ref, z_ref, g_ref, do_ref, dz_ref, dg_ref):
        @pl.when(pl.program_id(0) == 0)
        def _():
            dg_ref[...] = jnp.zeros_like(dg_ref)

        g = g_ref[...]
        dg = jnp.zeros_like(g)
        for h in range(N_HEADS):
            cs = slice(h * HEAD_DIM, (h + 1) * HEAD_DIM)
            oh = o_ref[:, cs]
            z = z_ref[:, cs]
            d = dy_ref[:, cs]
            r = _rms_scale(oh)
            nh = oh * r
            sz = _sigmoid(z)
            dyn = d * (z * sz)
            dz_ref[:, cs] = d * (nh * g) * (sz * (1.0 + z * (1.0 - sz)))
            dg = dg + jnp.sum(dyn * nh, axis=0, keepdims=True)
            dn = dyn * g
            do_ref[:, cs] = r * (dn - nh * jnp.mean(dn * nh, axis=-1, keepdims=True))
        dg_ref[...] += dg

    half = pl.BlockSpec((tm, HALF_W), lambda i: (i, 0))
    vec = pl.BlockSpec((1, HEAD_DIM), lambda i: (0, 0))
    return _call(
        body, name=name, grid=(n // tm,),
        in_specs=[half, half, pl.BlockSpec((tm, HALF_W), lambda i: (i, 5)), vec],
        out_specs=[half, half, vec],
        out_shape=[_sds((n, HALF_W), F32), _sds((n, HALF_W), F32), _sds((1, HEAD_DIM), F32)],
    )(dy, o, proj, dn_norm)


def _adamw_math(w, g, m, v):
    m_new = ADAM_B1 * m + (1.0 - ADAM_B1) * g
    v_new = ADAM_B2 * v + (1.0 - ADAM_B2) * (g * g)
    m_hat = m_new / (1.0 - ADAM_B1 ** ADAM_STEP)
    v_hat = v_new / (1.0 - ADAM_B2 ** ADAM_STEP)
    delta = -ADAM_LR * (m_hat / (jnp.sqrt(v_hat) + ADAM_EPS) + ADAM_WD * w)
    return delta, m_new, v_new


def adamw(w, g, m, v, name):
    r, c = w.shape
    tr = r
    for cand in (256, 352):
        if r % cand == 0 and r > cand:
            tr = cand
            break

    def body(w_ref, g_ref, m_ref, v_ref, d_ref, mo_ref, vo_ref):
        d, mn, vn = _adamw_math(w_ref[...], g_ref[...], m_ref[...], v_ref[...])
        d_ref[...] = d
        mo_ref[...] = mn
        vo_ref[...] = vn

    spec = pl.BlockSpec((tr, c), lambda i: (i, 0))
    return _call(
        body, name=name, grid=(r // tr,),
        in_specs=[spec] * 4, out_specs=[spec] * 3, out_shape=[_sds((r, c), F32)] * 3,
    )(w, g, m, v)


def _place():
    return lax.axis_index("x"), lax.axis_index("y"), lax.axis_index("c")


def _other_chips(x, y):
    return [(1 - x, y), (x, 1 - y), (1 - x, 1 - y)]


_ANY = pl.BlockSpec(memory_space=pl.ANY)


def cast_place(w, shard_idx, name):
    r, cols = w.shape
    tr = r // 2

    def body(j_ref, w_ref, o_ref):
        o_ref[0] = w_ref[...].astype(BF16)

    return pl.pallas_call(
        body, name=name,
        grid_spec=pltpu.PrefetchScalarGridSpec(
            num_scalar_prefetch=1, grid=(r // tr,),
            in_specs=[pl.BlockSpec((tr, cols), lambda i, j: (i, 0))],
            out_specs=pl.BlockSpec((1, tr, cols), lambda i, j: (j[0], i, 0))),
        out_shape=_sds((N_SHARD, r, cols), BF16),
        compiler_params=pltpu.CompilerParams(dimension_semantics=("arbitrary",),
                                             vmem_limit_bytes=VMEM_LIMIT),
    )(shard_idx, w)


def all_gather_chips(bufs, small, name):
    n = len(bufs)

    def body(*refs):
        small_in = refs[n]
        outs, small_out = refs[n + 1:2 * n + 1], refs[2 * n + 1]
        send, recv, fsend, frecv, loc = refs[2 * n + 2:]
        x, y, c = _place()
        j = 2 * x + y
        chips = _other_chips(x, y)
        sib = (x, y, 1 - c)

        def half(a, blk, hc):
            rh = bufs[a].shape[1] // 2
            return outs[a].at[blk, pl.ds(hc * rh, rh), :]

        local = pltpu.make_async_copy(small_in, small_out.at[j], loc.at[0])
        local.start()
        sends = []
        for k, (px, py) in enumerate(chips):
            sends.append(pltpu.make_async_remote_copy(
                src_ref=small_in, dst_ref=small_out.at[j], send_sem=send.at[3 * n + k],
                recv_sem=recv.at[3 * n + k], device_id=(px, py, c), device_id_type=MESH))
            for a in range(n):
                sends.append(pltpu.make_async_remote_copy(
                    src_ref=half(a, j, c), dst_ref=half(a, j, c), send_sem=send.at[3 * a + k],
                    recv_sem=recv.at[3 * a + k], device_id=(px, py, c), device_id_type=MESH))
        for cp in sends:
            cp.start()
        forwards = []
        for k, (px, py) in enumerate(chips):
            blk = 2 * px + py
            for a in range(n):
                pltpu.make_async_remote_copy(
                    src_ref=half(a, blk, c), dst_ref=half(a, blk, c), send_sem=send.at[3 * a + k],
                    recv_sem=recv.at[3 * a + k], device_id=(px, py, c), device_id_type=MESH).wait_recv()
                fw = pltpu.make_async_remote_copy(
                    src_ref=half(a, blk, c), dst_ref=half(a, blk, c), send_sem=fsend.at[3 * a + k],
                    recv_sem=frecv.at[3 * a + k], device_id=sib, device_id_type=MESH)
                fw.start()
                forwards.append(fw)
        for k, (px, py) in enumerate(chips):
            blk = 2 * px + py
            pltpu.make_async_remote_copy(
                src_ref=small_in, dst_ref=small_out.at[blk], send_sem=send.at[3 * n + k],
                recv_sem=recv.at[3 * n + k], device_id=(px, py, c), device_id_type=MESH).wait_recv()
            for a in range(n):
                pltpu.make_async_remote_copy(
                    src_ref=half(a, blk, 1 - c), dst_ref=half(a, blk, 1 - c), send_sem=fsend.at[3 * a + k],
                    recv_sem=frecv.at[3 * a + k], device_id=sib, device_id_type=MESH).wait_recv()
        for cp in sends + forwards:
            cp.wait_send()
        local.wait()

    res = _call(
        body, name=name, in_specs=[_ANY] * (n + 1), out_specs=[_ANY] * (n + 1),
        out_shape=[_sds(b.shape, b.dtype) for b in bufs] + [_sds((N_SHARD,) + small.shape, small.dtype)],
        scratch=[pltpu.SemaphoreType.DMA((3 * n + 3,)), pltpu.SemaphoreType.DMA((3 * n + 3,)),
                 pltpu.SemaphoreType.DMA((3 * n,)), pltpu.SemaphoreType.DMA((3 * n,)),
                 pltpu.SemaphoreType.DMA((1,))],
        input_output_aliases={a: a for a in range(n)},
    )(*bufs, small)
    return res[:n], res[n]


def pair_exchange(arrs, name):
    n = len(arrs)

    def body(*refs):
        ins, outs = refs[:n], refs[n:2 * n]
        send, recv = refs[2 * n:]
        x, y, c = _place()
        cps = []
        for a in range(n):
            rh = arrs[a].shape[1] // 2
            cps.append(pltpu.make_async_remote_copy(
                src_ref=ins[a].at[:, pl.ds((1 - c) * rh, rh), :], dst_ref=outs[a],
                send_sem=send.at[a], recv_sem=recv.at[a], device_id=(x, y, 1 - c), device_id_type=MESH))
        for cp in cps:
            cp.start()
        for cp in cps:
            cp.wait_recv()
        for cp in cps:
            cp.wait_send()

    return _call(
        body, name=name, in_specs=[_ANY] * n, out_specs=[_ANY] * n,
        out_shape=[_sds((a.shape[0], a.shape[1] // 2, a.shape[2]), a.dtype) for a in arrs],
        scratch=[pltpu.SemaphoreType.DMA((n,)), pltpu.SemaphoreType.DMA((n,))],
    )(*arrs)


def pair_add(g, s, c_idx, name):
    nb, r, cols = g.shape
    rh = r // 2

    def body(c_ref, g_ref, s_ref, o_ref):
        o_ref[...] = (g_ref[...] + s_ref[...]).astype(BF16)

    return pl.pallas_call(
        body, name=name,
        grid_spec=pltpu.PrefetchScalarGridSpec(
            num_scalar_prefetch=1, grid=(nb,),
            in_specs=[pl.BlockSpec((1, rh, cols), lambda j, c: (j, c[0], 0)),
                      pl.BlockSpec((1, rh, cols), lambda j, c: (j, 0, 0))],
            out_specs=pl.BlockSpec((1, rh, cols), lambda j, c: (j, 0, 0))),
        out_shape=_sds((nb, rh, cols), BF16),
        compiler_params=pltpu.CompilerParams(dimension_semantics=("arbitrary",),
                                             vmem_limit_bytes=VMEM_LIMIT),
    )(c_idx, g, s)


def chip_exchange(arrs, name):
    n = len(arrs)

    def body(*refs):
        ins, outs = refs[:n], refs[n:2 * n]
        send, recv = refs[2 * n:]
        x, y, c = _place()
        j = 2 * x + y
        chips = _other_chips(x, y)
        sends = []
        for a in range(n):
            for k, (px, py) in enumerate(chips):
                sends.append(pltpu.make_async_remote_copy(
                    src_ref=ins[a].at[2 * px + py], dst_ref=outs[a].at[j], send_sem=send.at[3 * a + k],
                    recv_sem=recv.at[3 * a + k], device_id=(px, py, c), device_id_type=MESH))
        for cp in sends:
            cp.start()
        for a in range(n):
            for k, (px, py) in enumerate(chips):
                pltpu.make_async_remote_copy(
                    src_ref=ins[a].at[j], dst_ref=outs[a].at[2 * px + py], send_sem=send.at[3 * a + k],
                    recv_sem=recv.at[3 * a + k], device_id=(px, py, c), device_id_type=MESH).wait_recv()
        for cp in sends:
            cp.wait_send()

    return _call(
        body, name=name, in_specs=[_ANY] * n, out_specs=[_ANY] * n,
        out_shape=[_sds(a.shape, a.dtype) for a in arrs],
        scratch=[pltpu.SemaphoreType.DMA((3 * n,)), pltpu.SemaphoreType.DMA((3 * n,))],
    )(*arrs)


def sum_chips(r, p, shard_idx, name):
    nb, rh, cols = r.shape
    tr = rh // 2

    def body(j_ref, p_ref, *refs):
        o_ref = refs[nb]
        j = j_ref[0]
        acc = None
        for i in range(nb):
            term = jnp.where(j == i, p_ref[0], refs[i][0]).astype(F32)
            acc = term if acc is None else acc + term
        o_ref[...] = acc

    def slot(i):
        return pl.BlockSpec((1, tr, cols), lambda t, j: (jnp.where(j[0] == i, (i + 1) % nb, i), t, 0))

    return pl.pallas_call(
        body, name=name,
        grid_spec=pltpu.PrefetchScalarGridSpec(
            num_scalar_prefetch=1, grid=(rh // tr,),
            in_specs=[pl.BlockSpec((1, tr, cols), lambda t, j: (j[0], t, 0))] + [slot(i) for i in range(nb)],
            out_specs=pl.BlockSpec((tr, cols), lambda t, j: (t, 0))),
        out_shape=_sds((rh, cols), F32),
        compiler_params=pltpu.CompilerParams(dimension_semantics=("arbitrary",),
                                             vmem_limit_bytes=VMEM_LIMIT),
    )(shard_idx, p, *([r] * nb))


def pair_swap(arrs, name):
    n = len(arrs)

    def body(*refs):
        ins, outs = refs[:n], refs[n:2 * n]
        send, recv = refs[2 * n:]
        x, y, c = _place()
        cps = [pltpu.make_async_remote_copy(
            src_ref=ins[a], dst_ref=outs[a], send_sem=send.at[a], recv_sem=recv.at[a],
            device_id=(x, y, 1 - c), device_id_type=MESH) for a in range(n)]
        for cp in cps:
            cp.start()
        for cp in cps:
            cp.wait_recv()
        for cp in cps:
            cp.wait_send()

    return _call(
        body, name=name, in_specs=[_ANY] * n, out_specs=[_ANY] * n,
        out_shape=[_sds(a.shape, a.dtype) for a in arrs],
        scratch=[pltpu.SemaphoreType.DMA((n,)), pltpu.SemaphoreType.DMA((n,))],
    )(*arrs)


def adamw_pair(w, g_mine, g_sib, m, v, c_idx, name):
    r, cols = w.shape
    rh = r // 2
    tr = rh // 2
    nh = rh // tr

    def body(c_ref, w_ref, gm_ref, gs_ref, m_ref, v_ref, g_ref, d_ref, mo_ref, vo_ref):
        mine = (pl.program_id(0) // nh) == c_ref[0]
        g = jnp.where(mine, gm_ref[...], gs_ref[...])
        d, mn, vn = _adamw_math(w_ref[...], g, m_ref[...], v_ref[...])
        g_ref[...] = g
        d_ref[...] = d
        mo_ref[...] = mn
        vo_ref[...] = vn

    full = pl.BlockSpec((tr, cols), lambda i, c: (i, 0))
    part = pl.BlockSpec((tr, cols), lambda i, c: (i % nh, 0))
    return pl.pallas_call(
        body, name=name,
        grid_spec=pltpu.PrefetchScalarGridSpec(
            num_scalar_prefetch=1, grid=(r // tr,),
            in_specs=[full, part, part, full, full], out_specs=[full] * 4),
        out_shape=[_sds((r, cols), F32)] * 4,
        compiler_params=pltpu.CompilerParams(dimension_semantics=("arbitrary",),
                                             vmem_limit_bytes=VMEM_LIMIT),
    )(c_idx, w, g_mine, g_sib, m, v)


N_DEV = 8


def all_reduce_small(pack, name):
    r, cols = pack.shape

    def body(in_ref, out_ref, buf, send, recv):
        x, y, c = _place()
        me = 4 * x + 2 * y + c
        buf[me] = in_ref[...]
        peers = []
        for k in range(1, N_DEV):
            fx, fy, fc = (k >> 2) & 1, (k >> 1) & 1, k & 1
            peers.append((1 - x if fx else x, 1 - y if fy else y, 1 - c if fc else c))
        sends = [pltpu.make_async_remote_copy(
            src_ref=in_ref, dst_ref=buf.at[me], send_sem=send.at[k], recv_sem=recv.at[k],
            device_id=p, device_id_type=MESH) for k, p in enumerate(peers)]
        for cp in sends:
            cp.start()
        for k, (px, py, pc) in enumerate(peers):
            pltpu.make_async_remote_copy(
                src_ref=in_ref, dst_ref=buf.at[4 * px + 2 * py + pc], send_sem=send.at[k],
                recv_sem=recv.at[k], device_id=(px, py, pc), device_id_type=MESH).wait_recv()
        for cp in sends:
            cp.wait_send()
        acc = buf[0] + buf[1]
        for i in range(2, N_DEV):
            acc = acc + buf[i]
        out_ref[...] = acc

    vm = pl.BlockSpec(memory_space=pltpu.VMEM)
    return _call(
        body, name=name, in_specs=[vm], out_specs=vm, out_shape=_sds((r, cols), F32),
        scratch=[pltpu.VMEM((N_DEV, r, cols), F32), pltpu.SemaphoreType.DMA((N_DEV - 1,)),
                 pltpu.SemaphoreType.DMA((N_DEV - 1,))],
    )(pack)


SMALL_NAMES = ("ffn1_norm", "mix_norm", "ffn2_norm", "final_norm", "sg_ln_g", "sg_ln_b",
               "dn_norm", "a_log", "dt_bias", "sg_b", "sg_w", "conv_w")


def _to_rows(a):
    flat = a.reshape(-1)
    pad = (-flat.shape[0]) % LANES
    if pad:
        flat = jnp.pad(flat, (0, pad))
    return flat.reshape(-1, LANES)


def _pack_small(parts):
    rows = [_to_rows(parts[k]) for k in SMALL_NAMES]
    pack = jnp.concatenate(rows, axis=0)
    pad = (-pack.shape[0]) % 8
    if pad:
        pack = jnp.pad(pack, ((0, pad), (0, 0)))
    return pack


def _unpack_small(pack, shapes):
    out, r0 = {}, 0
    for k in SMALL_NAMES:
        size = 1
        for s in shapes[k]:
            size *= s
        nrows = -(-size // LANES)
        out[k] = pack[r0:r0 + nrows].reshape(-1)[:size].reshape(shapes[k])
        r0 += nrows
    return out


def kernel(x, ffn1_norm, ffn1_w_gate, ffn1_w_up, ffn1_w_down, mix_norm, w_in, conv_w, a_log, dt_bias, dn_norm, sg_ln_g, sg_ln_b, sg_w, sg_b, w_out, ffn2_norm, ffn2_w_gate, ffn2_w_up, ffn2_w_down, final_norm, loss_target, m_ffn1_norm, m_ffn1_w_gate, m_ffn1_w_up, m_ffn1_w_down, m_mix_norm, m_w_in, m_conv_w, m_a_log, m_dt_bias, m_dn_norm, m_sg_ln_g, m_sg_ln_b, m_sg_w, m_sg_b, m_w_out, m_ffn2_norm, m_ffn2_w_gate, m_ffn2_w_up, m_ffn2_w_down, m_final_norm, v_ffn1_norm, v_ffn1_w_gate, v_ffn1_w_up, v_ffn1_w_down, v_mix_norm, v_w_in, v_conv_w, v_a_log, v_dt_bias, v_dn_norm, v_sg_ln_g, v_sg_ln_b, v_sg_w, v_sg_b, v_w_out, v_ffn2_norm, v_ffn2_w_gate, v_ffn2_w_up, v_ffn2_w_down, v_final_norm):
    bsz, t_len, d = x.shape
    n = bsz * t_len
    xy, yy, cc = _place()
    shard = 2 * xy + yy

    big_names = ["ffn1_w_gate", "ffn1_w_up", "ffn1_w_down", "w_in", "w_out",
                 "ffn2_w_gate", "ffn2_w_up", "ffn2_w_down"]
    big_w = dict(ffn1_w_gate=ffn1_w_gate, ffn1_w_up=ffn1_w_up, ffn1_w_down=ffn1_w_down, w_in=w_in,
                 w_out=w_out, ffn2_w_gate=ffn2_w_gate, ffn2_w_up=ffn2_w_up, ffn2_w_down=ffn2_w_down)
    big_m = dict(ffn1_w_gate=m_ffn1_w_gate, ffn1_w_up=m_ffn1_w_up, ffn1_w_down=m_ffn1_w_down, w_in=m_w_in,
                 w_out=m_w_out, ffn2_w_gate=m_ffn2_w_gate, ffn2_w_up=m_ffn2_w_up, ffn2_w_down=m_ffn2_w_down)
    big_v = dict(ffn1_w_gate=v_ffn1_w_gate, ffn1_w_up=v_ffn1_w_up, ffn1_w_down=v_ffn1_w_down, w_in=v_w_in,
                 w_out=v_w_out, ffn2_w_gate=v_ffn2_w_gate, ffn2_w_up=v_ffn2_w_up, ffn2_w_down=v_ffn2_w_down)
    shard_idx = jnp.reshape(shard, (1,)).astype(jnp.int32)
    c_idx = jnp.reshape(cc, (1,)).astype(jnp.int32)
    placed = [cast_place(big_w[k][0], shard_idx, name="cast_" + k) for k in big_names]
    gathered, conv_g = all_gather_chips(placed, conv_w[0], name="gather_weights")
    gw = dict(zip(big_names, gathered))
    conv_full = conv_g.transpose(1, 0, 2).reshape(CONV_K, 3 * HALF_W)
    w_in_full = gw["w_in"].transpose(1, 0, 2).reshape(d, IN_COLS)
    w_in_full = jnp.pad(w_in_full, ((0, 0), (0, PROJ_W - IN_COLS)))
    w_out_full = gw["w_out"].reshape(2 * HALF_W, d)

    x0 = x.reshape(n, d)
    x1, h1, gate1, up1 = ffn_fwd(x0, ffn1_norm, gw["ffn1_w_gate"], gw["ffn1_w_up"],
                                 gw["ffn1_w_down"], name="ffn1_fwd")
    proj, h2 = in_proj_fwd(x1, mix_norm, w_in_full, name="in_proj_fwd")
    proj3 = proj.reshape(bsz, t_len, PROJ_W)
    bias_tile = jnp.repeat(sg_b[0].T, SG_GROUP_DIM, axis=1)
    sg_out = sg_fwd(proj, sg_ln_g, sg_ln_b, sg_w[0], bias_tile, name="sg_fwd")
    qkv = dn_conv_fwd(proj3, conv_full, name="dn_conv_fwd")
    alog_row = jnp.zeros((1, LANES), F32).at[0, N_HEADS:2 * N_HEADS].set(a_log[0])
    dtb_row = jnp.zeros((1, LANES), F32).at[0, N_HEADS:2 * N_HEADS].set(dt_bias[0])
    u_wy, w_wy, q_dec, k_dec, qk, tinv, gc = dn_chunk_fwd(qkv, proj3, alog_row, dtb_row,
                                                           name="dn_chunk_fwd")
    o, s_in = dn_scan_fwd(u_wy, w_wy, q_dec, k_dec, qk, gc, name="dn_scan_fwd")
    dn_out = dn_out_fwd(o.reshape(n, HALF_W), proj, dn_norm, name="dn_out_fwd")
    x2 = out_proj_fwd(x1, sg_out, dn_out, w_out_full, name="out_proj_fwd")
    x3, h3, gate2, up2 = ffn_fwd(x2, ffn2_norm, gw["ffn2_w_gate"], gw["ffn2_w_up"],
                                 gw["ffn2_w_down"], name="ffn2_fwd")
    dx3, d_final_norm, loss_tile = final_loss(x3, final_norm.reshape(1, d),
                                              loss_target.reshape(n, d), name="final_loss")
    loss = lax.psum(loss_tile[0, 0], ("x", "y", "c"))

    dx2, dgate2, dup2, act2, dyh2, d_ffn2_norm = ffn_bwd_act(
        dx3, x2, ffn2_norm, gate2, up2, gw["ffn2_w_gate"], gw["ffn2_w_up"], gw["ffn2_w_down"],
        name="ffn2_bwd_act")
    g_big = {}
    g_big["ffn2_w_gate"], g_big["ffn2_w_up"], g_big["ffn2_w_down"] = ffn_bwd_w(
        h3, dyh2, dgate2, dup2, act2, name="ffn2_bwd_w")

    d_sg, d_dn, dx2b = out_proj_bwd_x(dx2, w_out_full, name="out_proj_bwd_x")
    g_w_out = jnp.concatenate([matmul_tn(sg_out, dx2b, d, name="w_out_grad_sg"),
                               matmul_tn(dn_out, dx2b, d, name="w_out_grad_dn")], axis=0)
    g_big["w_out"] = g_w_out.reshape(N_SHARD, (2 * HALF_W) // N_SHARD, d)

    d_o, d_z, d_dn_norm = dn_out_bwd(d_dn, o.reshape(n, HALF_W), proj, dn_norm, name="dn_out_bwd")
    du, dw, dqd, dkd, dqk, dgc_scan = dn_scan_bwd(d_o.reshape(bsz, t_len, HALF_W), u_wy, w_wy, q_dec,
                                                  k_dec, qk, gc, s_in, name="dn_scan_bwd")
    d_qkv, d_pba, d_alog_row, d_dtb_row = dn_chunk_bwd(
        qkv, proj3, alog_row, dtb_row, tinv, u_wy, w_wy, du, dw, dqd, dkd, dqk, dgc_scan,
        name="dn_chunk_bwd")
    d_pqkv, d_conv = dn_conv_bwd(d_qkv, proj3, conv_full, name="dn_conv_bwd")
    d_psg, d_sg_w, d_bias_tile, d_ln_g, d_ln_b = sg_bwd(d_sg, proj, sg_ln_g, sg_ln_b, sg_w[0],
                                                        bias_tile, name="sg_bwd")
    d_proj = jnp.concatenate([d_psg, d_pqkv.reshape(n, 3 * HALF_W), d_z, d_pba.reshape(n, LANES)],
                             axis=1).astype(BF16)
    dx1, d_mix_norm = in_proj_bwd_x(d_proj, w_in_full, x1, mix_norm, dx2, name="in_proj_bwd_x")
    g_w_in = matmul_tn(h2, d_proj, 640, name="w_in_grad")[:, :IN_COLS]
    g_big["w_in"] = g_w_in.reshape(d, N_SHARD, IN_COLS // N_SHARD).transpose(1, 0, 2)

    dx0, dgate1, dup1, act1, dyh1, d_ffn1_norm = ffn_bwd_act(
        dx1, x0, ffn1_norm, gate1, up1, gw["ffn1_w_gate"], gw["ffn1_w_up"], gw["ffn1_w_down"],
        name="ffn1_bwd_act")
    g_big["ffn1_w_gate"], g_big["ffn1_w_up"], g_big["ffn1_w_down"] = ffn_bwd_w(
        h1, dyh1, dgate1, dup1, act1, name="ffn1_bwd_w")
    grad_x = dx0.reshape(bsz, t_len, d)

    g_list = [g_big[k] for k in big_names]
    from_sibling = pair_exchange(g_list, name="grad_pair_exchange")
    pair_sums = [pair_add(g, s, c_idx, name="grad_pair_add_" + k)
                 for k, g, s in zip(big_names, g_list, from_sibling)]
    from_chips = chip_exchange(pair_sums, name="grad_chip_exchange")
    halves = [sum_chips(r, p, shard_idx, name="grad_chip_sum_" + k)
              for k, r, p in zip(big_names, from_chips, pair_sums)]
    sib_halves = pair_swap(halves, name="grad_pair_swap")
    outs = {}
    for k, g_mine, g_sib in zip(big_names, halves, sib_halves):
        g, delta, m_new, v_new = adamw_pair(big_w[k][0], g_mine, g_sib, big_m[k][0], big_v[k][0], c_idx,
                                            name="adamw_" + k)
        outs[k] = (g[None], delta[None], m_new[None], v_new[None])

    small_w = dict(ffn1_norm=ffn1_norm, mix_norm=mix_norm, ffn2_norm=ffn2_norm, final_norm=final_norm,
                   sg_ln_g=sg_ln_g, sg_ln_b=sg_ln_b, dn_norm=dn_norm, a_log=a_log, dt_bias=dt_bias,
                   sg_b=sg_b, sg_w=sg_w)
    small_m = dict(ffn1_norm=m_ffn1_norm, mix_norm=m_mix_norm, ffn2_norm=m_ffn2_norm,
                   final_norm=m_final_norm, sg_ln_g=m_sg_ln_g, sg_ln_b=m_sg_ln_b, dn_norm=m_dn_norm,
                   a_log=m_a_log, dt_bias=m_dt_bias, sg_b=m_sg_b, sg_w=m_sg_w)
    small_v = dict(ffn1_norm=v_ffn1_norm, mix_norm=v_mix_norm, ffn2_norm=v_ffn2_norm,
                   final_norm=v_final_norm, sg_ln_g=v_sg_ln_g, sg_ln_b=v_sg_ln_b, dn_norm=v_dn_norm,
                   a_log=v_a_log, dt_bias=v_dt_bias, sg_b=v_sg_b, sg_w=v_sg_w)
    shapes = {k: small_w[k].shape for k in small_w}
    shapes["conv_w"] = (CONV_K, 3 * HALF_W)
    d_sg_b = d_bias_tile.reshape(SG_CHUNK, SG_GROUPS, SG_GROUP_DIM).sum(axis=-1).T
    small_g = dict(ffn1_norm=d_ffn1_norm, mix_norm=d_mix_norm, ffn2_norm=d_ffn2_norm,
                   final_norm=d_final_norm, sg_ln_g=d_ln_g, sg_ln_b=d_ln_b, dn_norm=d_dn_norm,
                   a_log=d_alog_row[:, N_HEADS:2 * N_HEADS], dt_bias=d_dtb_row[:, N_HEADS:2 * N_HEADS],
                   sg_b=d_sg_b, sg_w=d_sg_w, conv_w=d_conv)
    g_pack = all_reduce_small(_pack_small(small_g), name="small_all_reduce")
    g_small = _unpack_small(g_pack, shapes)
    cw = 3 * HALF_W // N_SHARD
    g_conv = lax.dynamic_slice_in_dim(g_small["conv_w"], shard * cw, cw, axis=1)
    zero_conv = jnp.zeros((CONV_K, 3 * HALF_W), F32)

    def packed(src, conv):
        parts = dict(src)
        parts["conv_w"] = lax.dynamic_update_slice_in_dim(zero_conv, conv[0], shard * cw, axis=1)
        return _pack_small(parts)

    d_pack, m_pack, v_pack = adamw(packed(small_w, conv_w), g_pack, packed(small_m, m_conv_w),
                                   packed(small_v, v_conv_w), name="adamw_small")
    d_small = _unpack_small(d_pack, shapes)
    m_small = _unpack_small(m_pack, shapes)
    v_small = _unpack_small(v_pack, shapes)

    def conv_block(full_arr):
        return lax.dynamic_slice_in_dim(full_arr, shard * cw, cw, axis=1)[None]

    for k in small_w:
        outs[k] = (g_small[k].reshape(small_w[k].shape), d_small[k], m_small[k], v_small[k])
    outs["conv_w"] = (g_conv[None], conv_block(d_small["conv_w"]), conv_block(m_small["conv_w"]),
                      conv_block(v_small["conv_w"]))

    order = ["ffn1_norm", "ffn1_w_gate", "ffn1_w_up", "ffn1_w_down", "mix_norm", "w_in", "conv_w",
             "a_log", "dt_bias", "dn_norm", "sg_ln_g", "sg_ln_b", "sg_w", "sg_b", "w_out", "ffn2_norm",
             "ffn2_w_gate", "ffn2_w_up", "ffn2_w_down", "final_norm"]
    return (loss, grad_x, *[outs[k][0] for k in order], *[outs[k][1] for k in order],
            *[outs[k][2] for k in order], *[outs[k][3] for k in order])
```

```python
import functools

import jax
import jax.numpy as jnp
from jax import lax
from jax.experimental import pallas as pl
from jax.experimental.pallas import tpu as pltpu

F32 = jnp.float32
BF16 = jnp.bfloat16
EPS = 1e-6

D_MODEL = 1024
N_SHARD = 4
HEAD_DIM = 128
N_HEADS = 4
DN_CHUNK = 64
SG_CHUNK = 128
SG_GROUPS = 8
SG_GROUP_DIM = 64
HALF_W = 512
PROJ_W = 3200
IN_COLS = 3080
GATE_COL_BLOCK = 24
QK_SCALE = HEAD_DIM ** -0.5
LANES = 128

ADAM_LR = 0.001
ADAM_B1 = 0.9
ADAM_B2 = 0.999
ADAM_EPS = 1e-08
ADAM_WD = 0.01
ADAM_STEP = 10

VMEM_LIMIT = 56 * 1024 * 1024
ROW_TILE = 512

NN = ((1,), (0,))
NT = ((1,), (1,))
TN = ((0,), (0,))
MESH = pl.DeviceIdType.MESH


def _dot(a, b, dims):
    return lax.dot_general(a, b, (dims, ((), ())), preferred_element_type=F32)


def _bdot(a, b, dims):
    return _dot(a.astype(BF16), b.astype(BF16), dims)


def _split(a):
    hi = a.astype(BF16)
    lo = (a - hi.astype(F32)).astype(BF16)
    return hi, lo


def _dot3(a, b, dims=NN):
    return _dot(a[0], b[0], dims) + (_dot(a[0], b[1], dims) + _dot(a[1], b[0], dims))


def _dot_exact_lhs(a, b):
    ab = a.astype(BF16)
    b1 = b.astype(BF16)
    r1 = b - b1.astype(F32)
    b2 = r1.astype(BF16)
    b3 = (r1 - b2.astype(F32)).astype(BF16)
    return _dot(ab, b1, NN) + (_dot(ab, b2, NN) + _dot(ab, b3, NN))


def _call(body, *, name, out_shape, in_specs, out_specs, grid=(), scratch=(), **kw):
    params = dict(vmem_limit_bytes=VMEM_LIMIT)
    if grid:
        params["dimension_semantics"] = ("arbitrary",) * len(grid)
    return pl.pallas_call(
        body, name=name, grid=grid, in_specs=in_specs, out_specs=out_specs,
        out_shape=out_shape, scratch_shapes=list(scratch),
        compiler_params=pltpu.CompilerParams(**params), **kw)


def _sds(shape, dtype):
    return jax.ShapeDtypeStruct(tuple(shape), dtype)


def _sigmoid(x):
    return jax.nn.sigmoid(x)


def _softplus(x):
    return jnp.maximum(x, 0.0) + jnp.log(1.0 + jnp.exp(-jnp.abs(x)))


_GELU_C = 0.7978845608028654
_GELU_A = 0.044715


def _gelu(x):
    t = jnp.tanh(_GELU_C * (x + _GELU_A * x * x * x))
    return 0.5 * x * (1.0 + t)


def _gelu_grad(x):
    t = jnp.tanh(_GELU_C * (x + _GELU_A * x * x * x))
    return 0.5 * (1.0 + t) + 0.5 * x * (1.0 - t * t) * _GELU_C * (1.0 + 3.0 * _GELU_A * x * x)


def _silu_grad(x):
    s = _sigmoid(x)
    return s * (1.0 + x * (1.0 - s))


def _rms_scale(xv):
    return lax.rsqrt(jnp.mean(xv * xv, axis=-1, keepdims=True) + EPS)


def _rms_bwd(dh, xv, g):
    r = _rms_scale(xv)
    xn = xv * r
    dg = jnp.sum(dh * xn, axis=0, keepdims=True)
    dxn = dh * g
    dx = r * (dxn - xn * jnp.mean(dxn * xn, axis=-1, keepdims=True))
    return dx, dg


def _iota2(shape, dim):
    return lax.broadcasted_iota(jnp.int32, shape, dim)


def _col_to_row(col):
    n = col.shape[0]
    eye = _iota2((n, n), 0) == _iota2((n, n), 1)
    return jnp.sum(jnp.where(eye, col, 0.0), axis=0, keepdims=True)


def _row_to_col(row):
    n = row.shape[1]
    eye = _iota2((n, n), 0) == _iota2((n, n), 1)
    return jnp.sum(jnp.where(eye, row, 0.0), axis=1, keepdims=True)


def ffn_fwd(x, gnorm, wg, wu, wd, name):
    n, d = x.shape
    nb, _, fb = wg.shape
    tm = min(ROW_TILE, n)

    def body(x_ref, g_ref, wg_ref, wu_ref, wd_ref, xo_ref, h_ref, gate_ref, up_ref, acc_ref):
        j = pl.program_id(1)

        @pl.when(j == 0)
        def _():
            xv = x_ref[...]
            h_ref[...] = (xv * _rms_scale(xv) * g_ref[...]).astype(BF16)
            acc_ref[...] = jnp.zeros_like(acc_ref)

        h = h_ref[...]
        gate = _dot(h, wg_ref[0], NN)
        up = _dot(h, wu_ref[0], NN)
        gate_ref[0] = gate.astype(BF16)
        up_ref[0] = up.astype(BF16)
        act = (gate * _sigmoid(gate) * up).astype(BF16)
        acc_ref[...] += _dot(act, wd_ref[0], NN)

        @pl.when(j == nb - 1)
        def _():
            xo_ref[...] = x_ref[...] + 0.5 * acc_ref[...]

    row = pl.BlockSpec((tm, d), lambda i, j: (i, 0))
    return _call(
        body, name=name, grid=(n // tm, nb),
        in_specs=[row, pl.BlockSpec((1, d), lambda i, j: (0, 0)),
                  pl.BlockSpec((1, d, fb), lambda i, j: (j, 0, 0)),
                  pl.BlockSpec((1, d, fb), lambda i, j: (j, 0, 0)),
                  pl.BlockSpec((1, fb, d), lambda i, j: (j, 0, 0))],
        out_specs=[row, row,
                   pl.BlockSpec((1, tm, fb), lambda i, j: (j, i, 0)),
                   pl.BlockSpec((1, tm, fb), lambda i, j: (j, i, 0))],
        out_shape=[_sds((n, d), F32), _sds((n, d), BF16),
                   _sds((nb, n, fb), BF16), _sds((nb, n, fb), BF16)],
        scratch=[pltpu.VMEM((tm, d), F32)],
    )(x, gnorm, wg, wu, wd)


def ffn_bwd_act(dy, x, gnorm, gate, up, wg, wu, wd, name):
    n, d = x.shape
    nb, _, fb = wg.shape
    tm = min(ROW_TILE, n)

    def body(dy_ref, x_ref, g_ref, gate_ref, up_ref, wg_ref, wu_ref, wd_ref,
             dx_ref, dgate_ref, dup_ref, act_ref, dyh_ref, dg_ref, acc_ref):
        i = pl.program_id(0)
        j = pl.program_id(1)

        @pl.when(jnp.logical_and(i == 0, j == 0))
        def _():
            dg_ref[...] = jnp.zeros_like(dg_ref)

        @pl.when(j == 0)
        def _():
            dyh_ref[...] = (0.5 * dy_ref[...]).astype(BF16)
            acc_ref[...] = jnp.zeros_like(acc_ref)

        dact = _dot(dyh_ref[...], wd_ref[0], NT)
        gt = gate_ref[0].astype(F32)
        u = up_ref[0].astype(F32)
        s = _sigmoid(gt)
        silu = gt * s
        dup = (dact * silu).astype(BF16)
        dgate = (dact * u * (s * (1.0 + gt * (1.0 - s)))).astype(BF16)
        dup_ref[0] = dup
        dgate_ref[0] = dgate
        act_ref[0] = (silu * u).astype(BF16)
        acc_ref[...] += _dot(dgate, wg_ref[0], NT) + _dot(dup, wu_ref[0], NT)

        @pl.when(j == nb - 1)
        def _():
            dxn, dg = _rms_bwd(acc_ref[...], x_ref[...], g_ref[...])
            dx_ref[...] = dy_ref[...] + dxn
            dg_ref[...] += dg

    row = pl.BlockSpec((tm, d), lambda i, j: (i, 0))
    blk = pl.BlockSpec((1, tm, fb), lambda i, j: (j, i, 0))
    vec = pl.BlockSpec((1, d), lambda i, j: (0, 0))
    wcol = pl.BlockSpec((1, d, fb), lambda i, j: (j, 0, 0))
    return _call(
        body, name=name, grid=(n // tm, nb),
        in_specs=[row, row, vec, blk, blk, wcol, wcol,
                  pl.BlockSpec((1, fb, d), lambda i, j: (j, 0, 0))],
        out_specs=[row, blk, blk, blk, row, vec],
        out_shape=[_sds((n, d), F32), _sds((nb, n, fb), BF16), _sds((nb, n, fb), BF16),
                   _sds((nb, n, fb), BF16), _sds((n, d), BF16), _sds((1, d), F32)],
        scratch=[pltpu.VMEM((tm, d), F32)],
    )(dy, x, gnorm, gate, up, wg, wu, wd)


def ffn_bwd_w(h, dyh, dgate, dup, act, name):
    n, d = h.shape
    nb, _, fb = dgate.shape
    tk = min(ROW_TILE, n)

    def body(h_ref, dyh_ref, dgate_ref, dup_ref, act_ref, dwg_ref, dwu_ref, dwd_ref):
        @pl.when(pl.program_id(1) == 0)
        def _():
            dwg_ref[...] = jnp.zeros_like(dwg_ref)
            dwu_ref[...] = jnp.zeros_like(dwu_ref)
            dwd_ref[...] = jnp.zeros_like(dwd_ref)

        hv = h_ref[...]
        dwg_ref[0] += _dot(hv, dgate_ref[0], TN)
        dwu_ref[0] += _dot(hv, dup_ref[0], TN)
        dwd_ref[0] += _dot(act_ref[0], dyh_ref[...], TN)

    row = pl.BlockSpec((tk, d), lambda j, k: (k, 0))
    blk = pl.BlockSpec((1, tk, fb), lambda j, k: (j, k, 0))
    return _call(
        body, name=name, grid=(nb, n // tk),
        in_specs=[row, row, blk, blk, blk],
        out_specs=[pl.BlockSpec((1, d, fb), lambda j, k: (j, 0, 0)),
                   pl.BlockSpec((1, d, fb), lambda j, k: (j, 0, 0)),
                   pl.BlockSpec((1, fb, d), lambda j, k: (j, 0, 0))],
        out_shape=[_sds((nb, d, fb), F32), _sds((nb, d, fb), F32), _sds((nb, fb, d), F32)],
    )(h, dyh, dgate, dup, act)


def final_loss(x, gnorm, target, name):
    n, d = x.shape
    tm = min(ROW_TILE, n)

    def body(x_ref, g_ref, t_ref, dx_ref, dg_ref, loss_ref):
        @pl.when(pl.program_id(0) == 0)
        def _():
            dg_ref[...] = jnp.zeros_like(dg_ref)
            loss_ref[...] = jnp.zeros_like(loss_ref)

        xv = x_ref[...]
        y = xv * _rms_scale(xv) * g_ref[...]
        err = y - t_ref[...]
        part = 0.5 * jnp.sum(jnp.mean(err * err, axis=-1, keepdims=True), axis=0, keepdims=True)
        loss_ref[...] += jnp.broadcast_to(part, loss_ref.shape)
        dx, dg = _rms_bwd(err * (1.0 / d), xv, g_ref[...])
        dx_ref[...] = dx
        dg_ref[...] += dg

    row = pl.BlockSpec((tm, d), lambda i: (i, 0))
    vec = pl.BlockSpec((1, d), lambda i: (0, 0))
    return _call(
        body, name=name, grid=(n // tm,),
        in_specs=[row, vec, row],
        out_specs=[row, vec, pl.BlockSpec((1, LANES), lambda i: (0, 0))],
        out_shape=[_sds((n, d), F32), _sds((1, d), F32), _sds((1, LANES), F32)],
    )(x, gnorm, target)


def in_proj_fwd(x, gnorm, w, name):
    n, d = x.shape
    cols = w.shape[1]
    tm = min(ROW_TILE, n)
    tn = 640

    def body(x_ref, g_ref, w_ref, p_ref, h_ref):
        @pl.when(pl.program_id(1) == 0)
        def _():
            xv = x_ref[...]
            h_ref[...] = (xv * _rms_scale(xv) * g_ref[...]).astype(BF16)

        p_ref[...] = _dot(h_ref[...], w_ref[...], NN)

    return _call(
        body, name=name, grid=(n // tm, cols // tn),
        in_specs=[pl.BlockSpec((tm, d), lambda i, j: (i, 0)),
                  pl.BlockSpec((1, d), lambda i, j: (0, 0)),
                  pl.BlockSpec((d, tn), lambda i, j: (0, j))],
        out_specs=[pl.BlockSpec((tm, tn), lambda i, j: (i, j)),
                   pl.BlockSpec((tm, d), lambda i, j: (i, 0))],
        out_shape=[_sds((n, cols), F32), _sds((n, d), BF16)],
    )(x, gnorm, w)


def in_proj_bwd_x(dproj, w, x, gnorm, dres, name):
    n, d = x.shape
    cols = w.shape[1]
    tm = min(ROW_TILE, n)

    def body(dp_ref, w_ref, x_ref, g_ref, dr_ref, dx_ref, dg_ref):
        @pl.when(pl.program_id(0) == 0)
        def _():
            dg_ref[...] = jnp.zeros_like(dg_ref)

        dh = _dot(dp_ref[...], w_ref[...], NT)
        dxn, dg = _rms_bwd(dh, x_ref[...], g_ref[...])
        dx_ref[...] = dr_ref[...] + dxn
        dg_ref[...] += dg

    row = pl.BlockSpec((tm, d), lambda i: (i, 0))
    vec = pl.BlockSpec((1, d), lambda i: (0, 0))
    return _call(
        body, name=name, grid=(n // tm,),
        in_specs=[pl.BlockSpec((tm, cols), lambda i: (i, 0)),
                  pl.BlockSpec((d, cols), lambda i: (0, 0)), row, vec, row],
        out_specs=[row, vec],
        out_shape=[_sds((n, d), F32), _sds((1, d), F32)],
    )(dproj, w, x, gnorm, dres)


def matmul_tn(a, b, tn, name):
    n, ka = a.shape
    cb = b.shape[1]
    tk = min(ROW_TILE, n)

    def body(a_ref, b_ref, o_ref):
        @pl.when(pl.program_id(1) == 0)
        def _():
            o_ref[...] = jnp.zeros_like(o_ref)

        o_ref[...] += _dot(a_ref[...], b_ref[...], TN)

    return _call(
        body, name=name, grid=(cb // tn, n // tk),
        in_specs=[pl.BlockSpec((tk, ka), lambda j, k: (k, 0)),
                  pl.BlockSpec((tk, tn), lambda j, k: (k, j))],
        out_specs=pl.BlockSpec((ka, tn), lambda j, k: (0, j)),
        out_shape=_sds((ka, cb), F32),
    )(a, b)


def out_proj_fwd(x, sg_out, dn_out, w, name):
    n, d = x.shape
    tm = min(ROW_TILE, n)

    def body(x_ref, a_ref, b_ref, w_ref, o_ref):
        o_ref[...] = (x_ref[...] + _dot(a_ref[...], w_ref[0:HALF_W, :], NN)
                      + _dot(b_ref[...], w_ref[HALF_W:2 * HALF_W, :], NN))

    row = pl.BlockSpec((tm, d), lambda i: (i, 0))
    half = pl.BlockSpec((tm, HALF_W), lambda i: (i, 0))
    return _call(
        body, name=name, grid=(n // tm,),
        in_specs=[row, half, half, pl.BlockSpec((2 * HALF_W, d), lambda i: (0, 0))],
        out_specs=row, out_shape=_sds((n, d), F32),
    )(x, sg_out, dn_out, w)


def out_proj_bwd_x(dy, w, name):
    n, d = dy.shape
    tm = min(ROW_TILE, n)

    def body(dy_ref, w_ref, dsg_ref, ddn_ref, dyb_ref):
        dyb = dy_ref[...].astype(BF16)
        dyb_ref[...] = dyb
        dsg_ref[...] = _dot(dyb, w_ref[0:HALF_W, :], NT)
        ddn_ref[...] = _dot(dyb, w_ref[HALF_W:2 * HALF_W, :], NT)

    row = pl.BlockSpec((tm, d), lambda i: (i, 0))
    half = pl.BlockSpec((tm, HALF_W), lambda i: (i, 0))
    return _call(
        body, name=name, grid=(n // tm,),
        in_specs=[row, pl.BlockSpec((2 * HALF_W, d), lambda i: (0, 0))],
        out_specs=[half, half, row],
        out_shape=[_sds((n, HALF_W), F32), _sds((n, HALF_W), F32), _sds((n, d), BF16)],
    )(dy, w)


def _sg_group_masks():
    col = _iota2((SG_CHUNK, HALF_W), 1)
    return [jnp.logical_and(col >= g * SG_GROUP_DIM, col < (g + 1) * SG_GROUP_DIM)
            for g in range(SG_GROUPS)]


def _sg_causal():
    return _iota2((SG_CHUNK, SG_CHUNK), 0) >= _iota2((SG_CHUNK, SG_CHUNK), 1)


def _sg_forward_chunk(pu, pv, ln_g, ln_b, wc, bias, masks):
    u = _gelu(pu)
    v = _gelu(pv)
    mu = jnp.mean(v, axis=-1, keepdims=True)
    vc = v - mu
    rs = lax.rsqrt(jnp.mean(vc * vc, axis=-1, keepdims=True) + EPS)
    xhat = vc * rs
    vn = (xhat * ln_g + ln_b).astype(BF16)
    vs = bias
    for g in range(SG_GROUPS):
        vs = vs + jnp.where(masks[g], _dot(wc[g], vn, NN), 0.0)
    return u, xhat, rs, vn, vs


def sg_fwd(proj, ln_g, ln_b, w_s, bias_tile, name):
    n = proj.shape[0]
    tm = min(ROW_TILE, n)

    def body(pu_ref, pv_ref, g_ref, b_ref, w_ref, bias_ref, o_ref):
        causal = _sg_causal()
        wc = [jnp.where(causal, w_ref[g], 0.0).astype(BF16) for g in range(SG_GROUPS)]
        masks = _sg_group_masks()
        for ci in range(tm // SG_CHUNK):
            rows = slice(ci * SG_CHUNK, (ci + 1) * SG_CHUNK)
            u, _, _, _, vs = _sg_forward_chunk(pu_ref[rows, :], pv_ref[rows, :], g_ref[...],
                                               b_ref[...], wc, bias_ref[...], masks)
            o_ref[rows, :] = (u * vs).astype(BF16)

    vec = pl.BlockSpec((1, HALF_W), lambda i: (0, 0))
    return _call(
        body, name=name, grid=(n // tm,),
        in_specs=[pl.BlockSpec((tm, HALF_W), lambda i: (i, 0)),
                  pl.BlockSpec((tm, HALF_W), lambda i: (i, 1)), vec, vec,
                  pl.BlockSpec((SG_GROUPS, SG_CHUNK, SG_CHUNK), lambda i: (0, 0, 0)),
                  pl.BlockSpec((SG_CHUNK, HALF_W), lambda i: (0, 0))],
        out_specs=pl.BlockSpec((tm, HALF_W), lambda i: (i, 0)),
        out_shape=_sds((n, HALF_W), BF16),
    )(proj, proj, ln_g, ln_b, w_s, bias_tile)


def sg_bwd(dsg, proj, ln_g, ln_b, w_s, bias_tile, name):
    n = proj.shape[0]
    tm = min(ROW_TILE, n)

    def body(d_ref, pu_ref, pv_ref, g_ref, b_ref, w_ref, bias_ref,
             dp_ref, dw_ref, db_ref, dlg_ref, dlb_ref):
        @pl.when(pl.program_id(0) == 0)
        def _():
            dw_ref[...] = jnp.zeros_like(dw_ref)
            db_ref[...] = jnp.zeros_like(db_ref)
            dlg_ref[...] = jnp.zeros_like(dlg_ref)
            dlb_ref[...] = jnp.zeros_like(dlb_ref)

        causal = _sg_causal()
        wc = [jnp.where(causal, w_ref[g], 0.0).astype(BF16) for g in range(SG_GROUPS)]
        masks = _sg_group_masks()
        ln_g_v = g_ref[...]
        for ci in range(tm // SG_CHUNK):
            rows = slice(ci * SG_CHUNK, (ci + 1) * SG_CHUNK)
            pu = pu_ref[rows, :]
            pv = pv_ref[rows, :]
            u, xhat, rs, vn, vs = _sg_forward_chunk(pu, pv, ln_g_v, b_ref[...], wc,
                                                    bias_ref[...], masks)
            dout = d_ref[rows, :]
            dp_ref[rows, 0:HALF_W] = dout * vs * _gelu_grad(pu)
            dvs = dout * u
            dvs_b = dvs.astype(BF16)
            db_ref[...] += dvs
            dvn = jnp.zeros_like(dvs)
            for g in range(SG_GROUPS):
                dvn = dvn + jnp.where(masks[g], _dot(wc[g], dvs_b, TN), 0.0)
                dwg = _dot(jnp.where(masks[g], dvs_b, jnp.zeros_like(dvs_b)), vn, NT)
                dw_ref[g] += jnp.where(causal, dwg, 0.0)
            dlg_ref[...] += jnp.sum(dvn * xhat, axis=0, keepdims=True)
            dlb_ref[...] += jnp.sum(dvn, axis=0, keepdims=True)
            dxh = dvn * ln_g_v
            dv = rs * (dxh - jnp.mean(dxh, axis=-1, keepdims=True)
                       - xhat * jnp.mean(dxh * xhat, axis=-1, keepdims=True))
            dp_ref[rows, HALF_W:2 * HALF_W] = dv * _gelu_grad(pv)

    vec = pl.BlockSpec((1, HALF_W), lambda i: (0, 0))
    wspec = pl.BlockSpec((SG_GROUPS, SG_CHUNK, SG_CHUNK), lambda i: (0, 0, 0))
    tile = pl.BlockSpec((SG_CHUNK, HALF_W), lambda i: (0, 0))
    return _call(
        body, name=name, grid=(n // tm,),
        in_specs=[pl.BlockSpec((tm, HALF_W), lambda i: (i, 0)),
                  pl.BlockSpec((tm, HALF_W), lambda i: (i, 0)),
                  pl.BlockSpec((tm, HALF_W), lambda i: (i, 1)), vec, vec, wspec, tile],
        out_specs=[pl.BlockSpec((tm, 2 * HALF_W), lambda i: (i, 0)), wspec, tile, vec, vec],
        out_shape=[_sds((n, 2 * HALF_W), F32), _sds((SG_GROUPS, SG_CHUNK, SG_CHUNK), F32),
                   _sds((SG_CHUNK, HALF_W), F32), _sds((1, HALF_W), F32), _sds((1, HALF_W), F32)],
    )(dsg, proj, proj, ln_g, ln_b, w_s, bias_tile)


CONV_K = 4
CONV_BLOCK = 256


def _shift_down(x, s):
    if s == 0:
        return x
    rolled = pltpu.roll(x, s, 0)
    return jnp.where(_iota2(x.shape, 0) >= s, rolled, 0.0)


def _shift_up(x, s):
    if s == 0:
        return x
    t_len = x.shape[0]
    rolled = pltpu.roll(x, t_len - s, 0)
    return jnp.where(_iota2(x.shape, 0) < t_len - s, rolled, 0.0)


def _conv(x, w):
    y = _shift_down(x, CONV_K - 1) * w[0:1, :]
    for j in range(1, CONV_K):
        y = y + _shift_down(x, CONV_K - 1 - j) * w[j:j + 1, :]
    return y


def dn_conv_fwd(proj3, conv_w, name):
    b, t, _ = proj3.shape
    nblk = 3 * HALF_W // CONV_BLOCK
    first = 2 * HALF_W // CONV_BLOCK
    n_norm = 2 * HALF_W // CONV_BLOCK

    def body(x_ref, w_ref, o_ref):
        s = pl.program_id(1)
        y = _conv(x_ref[0], w_ref[...])
        y = y * _sigmoid(y)

        @pl.when(s < n_norm)
        def _():
            for h in range(CONV_BLOCK // HEAD_DIM):
                cs = slice(h * HEAD_DIM, (h + 1) * HEAD_DIM)
                yh = y[:, cs]
                o_ref[0, :, cs] = yh * lax.rsqrt(jnp.sum(yh * yh, axis=-1, keepdims=True) + EPS)

        @pl.when(s >= n_norm)
        def _():
            o_ref[0] = y

    return _call(
        body, name=name, grid=(b, nblk),
        in_specs=[pl.BlockSpec((1, t, CONV_BLOCK), lambda i, s: (i, 0, first + s)),
                  pl.BlockSpec((CONV_K, CONV_BLOCK), lambda i, s: (0, s))],
        out_specs=pl.BlockSpec((1, t, CONV_BLOCK), lambda i, s: (i, 0, s)),
        out_shape=_sds((b, t, 3 * HALF_W), F32),
    )(proj3, conv_w)


def dn_conv_bwd(dqkv, proj3, conv_w, name):
    b, t, _ = proj3.shape
    nblk = 3 * HALF_W // CONV_BLOCK
    first = 2 * HALF_W // CONV_BLOCK
    n_norm = 2 * HALF_W // CONV_BLOCK

    def body(d_ref, x_ref, w_ref, dx_ref, dw_ref, ds_ref):
        s = pl.program_id(0)

        @pl.when(pl.program_id(1) == 0)
        def _():
            dw_ref[...] = jnp.zeros_like(dw_ref)

        x = x_ref[0]
        w = w_ref[...]
        c = _conv(x, w)
        sg = _sigmoid(c)
        y = c * sg

        @pl.when(s < n_norm)
        def _():
            for h in range(CONV_BLOCK // HEAD_DIM):
                cs = slice(h * HEAD_DIM, (h + 1) * HEAD_DIM)
                yh = y[:, cs]
                r = lax.rsqrt(jnp.sum(yh * yh, axis=-1, keepdims=True) + EPS)
                nh = yh * r
                dn = d_ref[0, :, cs]
                ds_ref[:, cs] = r * (dn - nh * jnp.sum(dn * nh, axis=-1, keepdims=True))

        @pl.when(s >= n_norm)
        def _():
            ds_ref[...] = d_ref[0]

        dc = ds_ref[...] * (sg * (1.0 + c * (1.0 - sg)))
        dx = _shift_up(dc, CONV_K - 1) * w[0:1, :]
        for j in range(1, CONV_K):
            dx = dx + _shift_up(dc, CONV_K - 1 - j) * w[j:j + 1, :]
        dx_ref[0] = dx
        for j in range(CONV_K):
            dw_ref[j:j + 1, :] += jnp.sum(dc * _shift_down(x, CONV_K - 1 - j), axis=0, keepdims=True)

    return _call(
        body, name=name, grid=(nblk, b),
        in_specs=[pl.BlockSpec((1, t, CONV_BLOCK), lambda s, i: (i, 0, s)),
                  pl.BlockSpec((1, t, CONV_BLOCK), lambda s, i: (i, 0, first + s)),
                  pl.BlockSpec((CONV_K, CONV_BLOCK), lambda s, i: (0, s))],
        out_specs=[pl.BlockSpec((1, t, CONV_BLOCK), lambda s, i: (i, 0, s)),
                   pl.BlockSpec((CONV_K, CONV_BLOCK), lambda s, i: (0, s))],
        out_shape=[_sds((b, t, 3 * HALF_W), F32), _sds((CONV_K, 3 * HALF_W), F32)],
        scratch=[pltpu.VMEM((t, CONV_BLOCK), F32)],
    )(dqkv, proj3, conv_w)


def _chunk_masks():
    ii = _iota2((DN_CHUNK, DN_CHUNK), 0)
    jj = _iota2((DN_CHUNK, DN_CHUNK), 1)
    return ii >= jj, ii > jj, ii == jj


LOCKSTEP_CHUNKS = 2


def _inv_unit_lower_many(l_mats, eye):
    eye_f = jnp.where(eye, 1.0, 0.0)
    ps = [-l for l in l_mats]
    ts = [eye_f + p for p in ps]
    pss = [_split(p) for p in ps]
    size = 2
    while size < DN_CHUNK:
        ps = [_dot3(s, s) for s in pss]
        pss = [_split(p) for p in ps]
        ts = [t + _dot3(_split(t), s) for t, s in zip(ts, pss)]
        size *= 2
    return ts


def _gates(pba, ea_row, dtb_row):
    beta = _sigmoid(pba)
    g = -ea_row * _softplus(pba + dtb_row)
    return beta, g


def _chunk_decay(gcol):
    incl, strict, eye = _chunk_masks()
    grow = jnp.sum(jnp.where(eye, gcol, 0.0), axis=0, keepdims=True)
    decay = jnp.where(incl, jnp.exp(jnp.where(incl, gcol - grow, 0.0)), 0.0)
    return decay, incl, strict, eye


def dn_chunk_fwd(qkv, proj3, alog_row, dtb_row, name):
    b, t, _ = qkv.shape
    rblk = min(256, t)
    n_in = rblk // DN_CHUNK

    def body(q_ref, k_ref, v_ref, pba_ref, al_ref, dtb_ref,
             u_ref, w_ref, qd_ref, kd_ref, qk_ref, ti_ref, gc_ref):
        ea = jnp.exp(al_ref[...])
        tri = jnp.where(_chunk_masks()[0], 1.0, 0.0)

        _, strict, eye = _chunk_masks()

        def chunk_group(cg, carry):
            items = []
            for sub in range(LOCKSTEP_CHUNKS):
                rows = pl.ds(pl.multiple_of((cg * LOCKSTEP_CHUNKS + sub) * DN_CHUNK, DN_CHUNK), DN_CHUNK)
                beta_all, g_all = _gates(pba_ref[0, rows, :], ea, dtb_ref[...])
                gc = _dot_exact_lhs(tri, g_all)
                gc_ref[0, rows, :] = gc
                for h in range(N_HEADS):
                    items.append((rows, h, beta_all[:, h:h + 1], gc[:, N_HEADS + h:N_HEADS + h + 1]))
            ks, kbs, decays, egs = [], [], [], []
            for rows, h, beta, gcol in items:
                cs = slice(h * HEAD_DIM, (h + 1) * HEAD_DIM)
                k = k_ref[0, rows, cs]
                ks.append(k)
                kbs.append(k * beta)
                decays.append(_chunk_decay(gcol)[0])
                egs.append(jnp.exp(gcol))
            ms = [_bdot(kb, k, NT) for kb, k in zip(kbs, ks)]
            tinvs = _inv_unit_lower_many([jnp.where(strict, m * dc, 0.0) for m, dc in zip(ms, decays)], eye)
            tsps = [_split(t) for t in tinvs]
            for (rows, h, beta, gcol), tsp, tinv in zip(items, tsps, tinvs):
                cs = slice(h * HEAD_DIM, (h + 1) * HEAD_DIM)
                u_ref[0, rows, cs] = _dot3(tsp, _split(v_ref[0, rows, cs] * beta))
                ti_ref[0, h, rows, :] = tinv
            for (rows, h, beta, gcol), tsp, kb, eg in zip(items, tsps, kbs, egs):
                cs = slice(h * HEAD_DIM, (h + 1) * HEAD_DIM)
                w_ref[0, rows, cs] = _dot3(tsp, _split(kb * eg))
            for (rows, h, beta, gcol), k, dc, eg in zip(items, ks, decays, egs):
                cs = slice(h * HEAD_DIM, (h + 1) * HEAD_DIM)
                q = q_ref[0, rows, cs] * QK_SCALE
                qk_ref[0, h, rows, :] = _bdot(q, k, NT) * dc
                qd_ref[0, rows, cs] = q * eg
                kd_ref[0, rows, cs] = k * jnp.exp(gcol[DN_CHUNK - 1:DN_CHUNK, :] - gcol)
            return carry

        lax.fori_loop(0, n_in // LOCKSTEP_CHUNKS, chunk_group, 0)

    def seg(cblk):
        return pl.BlockSpec((1, rblk, HALF_W), lambda i, r: (i, r, cblk))

    vec = pl.BlockSpec((1, LANES), lambda i, r: (0, 0))
    wide = pl.BlockSpec((1, rblk, HALF_W), lambda i, r: (i, r, 0))
    sq = pl.BlockSpec((1, N_HEADS, rblk, DN_CHUNK), lambda i, r: (i, 0, r, 0))
    return _call(
        body, name=name, grid=(b, t // rblk),
        in_specs=[seg(0), seg(1), seg(2),
                  pl.BlockSpec((1, rblk, LANES), lambda i, r: (i, r, GATE_COL_BLOCK)), vec, vec],
        out_specs=[wide, wide, wide, wide, sq, sq,
                   pl.BlockSpec((1, rblk, LANES), lambda i, r: (i, r, 0))],
        out_shape=[_sds((b, t, HALF_W), F32)] * 4
        + [_sds((b, N_HEADS, t, DN_CHUNK), F32)] * 2 + [_sds((b, t, LANES), F32)],
    )(qkv, qkv, qkv, proj3, alog_row, dtb_row)


def dn_scan_fwd(u, w, qd, kd, qk, gc, name):
    b, t, _ = u.shape
    nc = t // DN_CHUNK
    bh = b * N_HEADS

    def body(u_ref, w_ref, qd_ref, kd_ref, qk_ref, gc_ref, o_ref, sin_ref, s_ref):
        @pl.when(pl.program_id(0) == 0)
        def _():
            s_ref[...] = jnp.zeros_like(s_ref)

        items = [(bi, h, slice(h * HEAD_DIM, (h + 1) * HEAD_DIM)) for bi in range(b) for h in range(N_HEADS)]
        sbs = []
        for bi, h, cs in items:
            s = s_ref[bi * N_HEADS + h]
            sin_ref[0, bi * N_HEADS + h] = s
            sbs.append(s.astype(BF16))
        ws = [_bdot(w_ref[bi, :, cs], sb, NN) for (bi, h, cs), sb in zip(items, sbs)]
        qs = [_bdot(qd_ref[bi, :, cs], sb, NN) for (bi, h, cs), sb in zip(items, sbs)]
        vbs = [(u_ref[bi, :, cs] - wsi).astype(BF16) for (bi, h, cs), wsi in zip(items, ws)]
        for (bi, h, cs), qsi, vb in zip(items, qs, vbs):
            o_ref[bi, :, cs] = qsi + _bdot(qk_ref[bi, h], vb, NN)
        for (bi, h, cs), vb in zip(items, vbs):
            gl = jnp.exp(gc_ref[bi, DN_CHUNK - 1:DN_CHUNK, N_HEADS + h:N_HEADS + h + 1])
            idx = bi * N_HEADS + h
            s_ref[idx] = s_ref[idx] * gl + _bdot(kd_ref[bi, :, cs], vb, TN)

    wide = pl.BlockSpec((b, DN_CHUNK, HALF_W), lambda c: (0, c, 0))
    return _call(
        body, name=name, grid=(nc,),
        in_specs=[wide, wide, wide, wide,
                  pl.BlockSpec((b, N_HEADS, DN_CHUNK, DN_CHUNK), lambda c: (0, 0, c, 0)),
                  pl.BlockSpec((b, DN_CHUNK, LANES), lambda c: (0, c, 0))],
        out_specs=[wide, pl.BlockSpec((1, bh, HEAD_DIM, HEAD_DIM), lambda c: (c, 0, 0, 0))],
        out_shape=[_sds((b, t, HALF_W), F32), _sds((nc, bh, HEAD_DIM, HEAD_DIM), F32)],
        scratch=[pltpu.VMEM((bh, HEAD_DIM, HEAD_DIM), F32)],
    )(u, w, qd, kd, qk, gc)


def dn_scan_bwd(do, u, w, qd, kd, qk, gc, s_in, name):
    b, t, _ = u.shape
    nc = t // DN_CHUNK
    bh = b * N_HEADS

    def body(do_ref, u_ref, w_ref, qd_ref, kd_ref, qk_ref, gc_ref, sin_ref,
             du_ref, dw_ref, dqd_ref, dkd_ref, dqk_ref, dgc_ref, ds_ref):
        @pl.when(pl.program_id(0) == 0)
        def _():
            ds_ref[...] = jnp.zeros_like(ds_ref)

        last_row = _iota2((DN_CHUNK, LANES), 0) == DN_CHUNK - 1
        lane = _iota2((DN_CHUNK, LANES), 1)
        items = [(bi, h, slice(h * HEAD_DIM, (h + 1) * HEAD_DIM)) for bi in range(b) for h in range(N_HEADS)]
        sbs = [sin_ref[0, bi * N_HEADS + h].astype(BF16) for bi, h, cs in items]
        wvs = [w_ref[bi, :, cs].astype(BF16) for bi, h, cs in items]
        dovs = [do_ref[bi, :, cs].astype(BF16) for bi, h, cs in items]
        dsbs = [ds_ref[bi * N_HEADS + h].astype(BF16) for bi, h, cs in items]
        vbs = [(u_ref[bi, :, cs] - _dot(wv, sb, NN)).astype(BF16)
               for (bi, h, cs), wv, sb in zip(items, wvs, sbs)]
        for (bi, h, cs), dov, sb in zip(items, dovs, sbs):
            dqd_ref[bi, :, cs] = _dot(dov, sb, NT)
        dvns = [_dot(kd_ref[bi, :, cs].astype(BF16), dsb, NN) + _dot(qk_ref[bi, h].astype(BF16), dov, TN)
                for (bi, h, cs), dsb, dov in zip(items, dsbs, dovs)]
        for (bi, h, cs), vb, dsb, dov in zip(items, vbs, dsbs, dovs):
            dkd_ref[bi, :, cs] = _dot(vb, dsb, NT)
            dqk_ref[bi, h] = _dot(dov, vb, NT)
        dgls = []
        for (bi, h, cs), dvn, sb, wv, dov in zip(items, dvns, sbs, wvs, dovs):
            idx = bi * N_HEADS + h
            du_ref[bi, :, cs] = dvn
            dvn_b = dvn.astype(BF16)
            dw_ref[bi, :, cs] = -_dot(dvn_b, sb, NT)
            gl = jnp.exp(gc_ref[bi, DN_CHUNK - 1:DN_CHUNK, N_HEADS + h:N_HEADS + h + 1])
            ds = ds_ref[idx]
            dgl = jnp.sum(jnp.sum(ds * sin_ref[0, idx], axis=1, keepdims=True), axis=0, keepdims=True)
            dgls.append(dgl * gl)
            ds_ref[idx] = (ds * gl + _dot(qd_ref[bi, :, cs].astype(BF16), dov, TN)
                           - _dot(wv, dvn_b, TN))
        for bi in range(b):
            dgc = jnp.zeros((DN_CHUNK, LANES), F32)
            for h in range(N_HEADS):
                dgc = dgc + jnp.where(jnp.logical_and(last_row, lane == N_HEADS + h),
                                      dgls[bi * N_HEADS + h], 0.0)
            dgc_ref[bi] = dgc

    def rev(c):
        return nc - 1 - c

    wide = pl.BlockSpec((b, DN_CHUNK, HALF_W), lambda c: (0, rev(c), 0))
    sq = pl.BlockSpec((b, N_HEADS, DN_CHUNK, DN_CHUNK), lambda c: (0, 0, rev(c), 0))
    gates = pl.BlockSpec((b, DN_CHUNK, LANES), lambda c: (0, rev(c), 0))
    return _call(
        body, name=name, grid=(nc,),
        in_specs=[wide, wide, wide, wide, wide, sq, gates,
                  pl.BlockSpec((1, bh, HEAD_DIM, HEAD_DIM), lambda c: (rev(c), 0, 0, 0))],
        out_specs=[wide, wide, wide, wide, sq, gates],
        out_shape=[_sds((b, t, HALF_W), F32)] * 4
        + [_sds((b, N_HEADS, t, DN_CHUNK), F32), _sds((b, t, LANES), F32)],
        scratch=[pltpu.VMEM((bh, HEAD_DIM, HEAD_DIM), F32)],
    )(do, u, w, qd, kd, qk, gc, s_in)


def dn_chunk_bwd(qkv, proj3, alog_row, dtb_row, tinv, u, w, du, dw, dqd, dkd, dqk, dgc_scan, name):
    b, t, _ = qkv.shape
    rblk = min(256, t)
    n_in = rblk // DN_CHUNK

    def body(q_ref, k_ref, v_ref, pba_ref, al_ref, dtb_ref, ti_ref, u_ref, w_ref,
             du_ref, dw_ref, dqd_ref, dkd_ref, dqk_ref, dgs_ref,
             dq_ref, dpba_ref, dal_ref, ddtb_ref):
        @pl.when(jnp.logical_and(pl.program_id(0) == 0, pl.program_id(1) == 0))
        def _():
            dal_ref[...] = jnp.zeros_like(dal_ref)
            ddtb_ref[...] = jnp.zeros_like(ddtb_ref)

        ea = jnp.exp(al_ref[...])
        incl0 = _chunk_masks()[0]
        tri = jnp.where(incl0, 1.0, 0.0)
        tri_up = jnp.where(_iota2((DN_CHUNK, DN_CHUNK), 1) >= _iota2((DN_CHUNK, DN_CHUNK), 0), 1.0, 0.0)
        lane = _iota2((DN_CHUNK, LANES), 1)
        last_col = _iota2((DN_CHUNK, 1), 0) == DN_CHUNK - 1

        _, strict, _ = _chunk_masks()
        gate_lane = jnp.logical_and(lane >= N_HEADS, lane < 2 * N_HEADS)

        def chunk_group(cg, carry):
            tiles, items = [], []
            for sub in range(LOCKSTEP_CHUNKS):
                rows = pl.ds(pl.multiple_of((cg * LOCKSTEP_CHUNKS + sub) * DN_CHUNK, DN_CHUNK), DN_CHUNK)
                pba = pba_ref[0, rows, :]
                beta_all, g_all = _gates(pba, ea, dtb_ref[...])
                gc = _dot_exact_lhs(tri, g_all)
                tiles.append((rows, pba, beta_all, g_all))
                for h in range(N_HEADS):
                    items.append((sub, rows, h, slice(h * HEAD_DIM, (h + 1) * HEAD_DIM),
                                  beta_all[:, h:h + 1], gc[:, N_HEADS + h:N_HEADS + h + 1]))
            decays = [_chunk_decay(gcol)[0] for _, _, _, _, _, gcol in items]
            egs = [jnp.exp(gcol) for _, _, _, _, _, gcol in items]
            qbs = [(q_ref[0, rows, cs] * QK_SCALE).astype(BF16) for _, rows, h, cs, _, _ in items]
            kfs = [k_ref[0, rows, cs].astype(BF16) for _, rows, h, cs, _, _ in items]
            kbs = [k_ref[0, rows, cs] * beta for _, rows, h, cs, beta, _ in items]
            kbbs = [kb.astype(BF16) for kb in kbs]
            tsps = [_split(ti_ref[0, h, rows, :]) for _, rows, h, cs, _, _ in items]
            drus = [_dot3(tsp, _split(du_ref[0, rows, cs]), TN)
                    for (_, rows, h, cs, _, _), tsp in zip(items, tsps)]
            drws = [_dot3(tsp, _split(dw_ref[0, rows, cs]), TN)
                    for (_, rows, h, cs, _, _), tsp in zip(items, tsps)]
            m_kks = [_dot(kbb, kf, NT) for kbb, kf in zip(kbbs, kfs)]
            a_qks = [_dot(qb, kf, NT) for qb, kf in zip(qbs, kfs)]
            dls = [-jnp.where(strict, _dot3(_split(dru), _split(u_ref[0, rows, cs]), NT)
                              + _dot3(_split(drw), _split(w_ref[0, rows, cs]), NT), 0.0)
                   for (_, rows, h, cs, _, _), dru, drw in zip(items, drus, drws)]
            dms = [(dl * dc).astype(BF16) for dl, dc in zip(dls, decays)]
            das = [(dqk_ref[0, h, rows, :] * dc).astype(BF16)
                   for (_, rows, h, cs, _, _), dc in zip(items, decays)]
            dkb_mm = [_dot(dm, kf, NN) for dm, kf in zip(dms, kfs)]
            dk_mm = [_dot(dm, kbb, TN) + _dot(da, qb, TN) for dm, kbb, da, qb in zip(dms, kbbs, das, qbs)]
            dqs_mm = [_dot(da, kf, NN) for da, kf in zip(das, kfs)]
            dgc_tiles = [dgs_ref[0, rows, :] for rows, _, _, _ in tiles]
            dbeta_tiles = [jnp.zeros((DN_CHUNK, LANES), F32) for _ in tiles]
            for n_it, (sub, rows, h, cs, beta, gcol) in enumerate(items):
                eg, dc = egs[n_it], decays[n_it]
                k = k_ref[0, rows, cs]
                q = q_ref[0, rows, cs] * QK_SCALE
                kb, dru, drw = kbs[n_it], drus[n_it], drws[n_it]
                ek = jnp.exp(gcol[DN_CHUNK - 1:DN_CHUNK, :] - gcol)
                e_mat = (dls[n_it] * m_kks[n_it] + dqk_ref[0, h, rows, :] * a_qks[n_it]) * dc
                dkb = drw * eg + dkb_mm[n_it]
                dg = (jnp.sum(drw * kb * eg, axis=-1, keepdims=True)
                      + jnp.sum(e_mat, axis=1, keepdims=True)
                      - _row_to_col(jnp.sum(e_mat, axis=0, keepdims=True)))
                dqd = dqd_ref[0, rows, cs]
                dg = dg + jnp.sum(dqd * q * eg, axis=-1, keepdims=True)
                dkd = dkd_ref[0, rows, cs]
                tk_ = jnp.sum(dkd * k * ek, axis=-1, keepdims=True)
                dg = dg - tk_ + jnp.where(last_col, jnp.sum(tk_, axis=0, keepdims=True), 0.0)
                dbeta = (jnp.sum(dkb * k, axis=-1, keepdims=True)
                         + jnp.sum(dru * v_ref[0, rows, cs], axis=-1, keepdims=True))
                dq_ref[0, rows, cs] = (dqs_mm[n_it] + dqd * eg) * QK_SCALE
                dq_ref[0, rows, pl.ds(HALF_W + h * HEAD_DIM, HEAD_DIM)] = dk_mm[n_it] + dkd * ek + dkb * beta
                dq_ref[0, rows, pl.ds(2 * HALF_W + h * HEAD_DIM, HEAD_DIM)] = dru * beta
                dgc_tiles[sub] = dgc_tiles[sub] + jnp.where(lane == N_HEADS + h, dg, 0.0)
                dbeta_tiles[sub] = dbeta_tiles[sub] + jnp.where(lane == h, dbeta, 0.0)
            for (rows, pba, beta_all, g_all), dgc_tile, dbeta_tile in zip(tiles, dgc_tiles, dbeta_tiles):
                dg_tile = _dot_exact_lhs(tri_up, dgc_tile)
                da_pre = dg_tile * (-ea) * _sigmoid(pba + dtb_ref[...])
                dal_ref[...] += jnp.sum(jnp.where(gate_lane, dg_tile * g_all, 0.0), axis=0, keepdims=True)
                ddtb_ref[...] += jnp.sum(jnp.where(gate_lane, da_pre, 0.0), axis=0, keepdims=True)
                dpba_ref[0, rows, :] = jnp.where(lane < N_HEADS, dbeta_tile * beta_all * (1.0 - beta_all),
                                                 jnp.where(gate_lane, da_pre, 0.0))
            return carry

        lax.fori_loop(0, n_in // LOCKSTEP_CHUNKS, chunk_group, 0)

    def seg(cblk):
        return pl.BlockSpec((1, rblk, HALF_W), lambda i, r: (i, r, cblk))

    vec = pl.BlockSpec((1, LANES), lambda i, r: (0, 0))
    wide = pl.BlockSpec((1, rblk, HALF_W), lambda i, r: (i, r, 0))
    sq = pl.BlockSpec((1, N_HEADS, rblk, DN_CHUNK), lambda i, r: (i, 0, r, 0))
    gates = pl.BlockSpec((1, rblk, LANES), lambda i, r: (i, r, 0))
    return _call(
        body, name=name, grid=(b, t // rblk),
        in_specs=[seg(0), seg(1), seg(2),
                  pl.BlockSpec((1, rblk, LANES), lambda i, r: (i, r, GATE_COL_BLOCK)), vec, vec,
                  sq, wide, wide, wide, wide, wide, wide, sq, gates],
        out_specs=[pl.BlockSpec((1, rblk, 3 * HALF_W), lambda i, r: (i, r, 0)), gates, vec, vec],
        out_shape=[_sds((b, t, 3 * HALF_W), F32), _sds((b, t, LANES), F32),
                   _sds((1, LANES), F32), _sds((1, LANES), F32)],
    )(qkv, qkv, qkv, proj3, alog_row, dtb_row, tinv, u, w, du, dw, dqd, dkd, dqk, dgc_scan)


def dn_out_fwd(o, proj, dn_norm, name):
    n = o.shape[0]
    tm = min(ROW_TILE, n)

    def body(o_ref, z_ref, g_ref, y_ref):
        for h in range(N_HEADS):
            cs = slice(h * HEAD_DIM, (h + 1) * HEAD_DIM)
            oh = o_ref[:, cs]
            z = z_ref[:, cs]
            y = oh * _rms_scale(oh) * g_ref[...]
            y_ref[:, cs] = (y * (z * _sigmoid(z))).astype(BF16)

    half = pl.BlockSpec((tm, HALF_W), lambda i: (i, 0))
    return _call(
        body, name=name, grid=(n // tm,),
        in_specs=[half, pl.BlockSpec((tm, HALF_W), lambda i: (i, 5)),
                  pl.BlockSpec((1, HEAD_DIM), lambda i: (0, 0))],
        out_specs=half, out_shape=_sds((n, HALF_W), BF16),
    )(o, proj, dn_norm)


def dn_out_bwd(dy, o, proj, dn_norm, name):
    n = o.shape[0]
    tm = min(ROW_TILE, n)

    def body(dy_ref, o_ref, z_ref, g_ref, do_ref, dz_ref, dg_ref):
        @pl.when(pl.program_id(0) == 0)
        def _():
            dg_ref[...] = jnp.zeros_like(dg_ref)

        g = g_ref[...]
        dg = jnp.zeros_like(g)
        for h in range(N_HEADS):
            cs = slice(h * HEAD_DIM, (h + 1) * HEAD_DIM)
            oh = o_ref[:, cs]
            z = z_ref[:, cs]
            d = dy_ref[:, cs]
            r = _rms_scale(oh)
            nh = oh * r
            sz = _sigmoid(z)
            dyn = d * (z * sz)
            dz_ref[:, cs] = d * (nh * g) * (sz * (1.0 + z * (1.0 - sz)))
            dg = dg + jnp.sum(dyn * nh, axis=0, keepdims=True)
            dn = dyn * g
            do_ref[:, cs] = r * (dn - nh * jnp.mean(dn * nh, axis=-1, keepdims=True))
        dg_ref[...] += dg

    half = pl.BlockSpec((tm, HALF_W), lambda i: (i, 0))
    vec = pl.BlockSpec((1, HEAD_DIM), lambda i: (0, 0))
    return _call(
        body, name=name, grid=(n // tm,),
        in_specs=[half, half, pl.BlockSpec((tm, HALF_W), lambda i: (i, 5)), vec],
        out_specs=[half, half, vec],
        out_shape=[_sds((n, HALF_W), F32), _sds((n, HALF_W), F32), _sds((1, HEAD_DIM), F32)],
    )(dy, o, proj, dn_norm)


def _adamw_math(w, g, m, v):
    m_new = ADAM_B1 * m + (1.0 - ADAM_B1) * g
    v_new = ADAM_B2 * v + (1.0 - ADAM_B2) * (g * g)
    m_hat = m_new / (1.0 - ADAM_B1 ** ADAM_STEP)
    v_hat = v_new / (1.0 - ADAM_B2 ** ADAM_STEP)
    delta = -ADAM_LR * (m_hat / (jnp.sqrt(v_hat) + ADAM_EPS) + ADAM_WD * w)
    return delta, m_new, v_new


def adamw(w, g, m, v, name):
    r, c = w.shape
    tr = r
    for cand in (256, 352):
        if r % cand == 0 and r > cand:
            tr = cand
            break

    def body(w_ref, g_ref, m_ref, v_ref, d_ref, mo_ref, vo_ref):
        d, mn, vn = _adamw_math(w_ref[...], g_ref[...], m_ref[...], v_ref[...])
        d_ref[...] = d
        mo_ref[...] = mn
        vo_ref[...] = vn

    spec = pl.BlockSpec((tr, c), lambda i: (i, 0))
    return _call(
        body, name=name, grid=(r // tr,),
        in_specs=[spec] * 4, out_specs=[spec] * 3, out_shape=[_sds((r, c), F32)] * 3,
    )(w, g, m, v)


def _place():
    return lax.axis_index("x"), lax.axis_index("y"), lax.axis_index("c")


def _other_chips(x, y):
    return [(1 - x, y), (x, 1 - y), (1 - x, 1 - y)]


_ANY = pl.BlockSpec(memory_space=pl.ANY)


def cast_place(w, shard_idx, name):
    r, cols = w.shape
    tr = r // 2

    def body(j_ref, w_ref, o_ref):
        o_ref[0] = w_ref[...].astype(BF16)

    return pl.pallas_call(
        body, name=name,
        grid_spec=pltpu.PrefetchScalarGridSpec(
            num_scalar_prefetch=1, grid=(r // tr,),
            in_specs=[pl.BlockSpec((tr, cols), lambda i, j: (i, 0))],
            out_specs=pl.BlockSpec((1, tr, cols), lambda i, j: (j[0], i, 0))),
        out_shape=_sds((N_SHARD, r, cols), BF16),
        compiler_params=pltpu.CompilerParams(dimension_semantics=("arbitrary",),
                                             vmem_limit_bytes=VMEM_LIMIT),
    )(shard_idx, w)


def all_gather_chips(bufs, small, name):
    n = len(bufs)

    def body(*refs):
        small_in = refs[n]
        outs, small_out = refs[n + 1:2 * n + 1], refs[2 * n + 1]
        send, recv, fsend, frecv, loc = refs[2 * n + 2:]
        x, y, c = _place()
        j = 2 * x + y
        chips = _other_chips(x, y)
        sib = (x, y, 1 - c)

        def half(a, blk, hc):
            rh = bufs[a].shape[1] // 2
            return outs[a].at[blk, pl.ds(hc * rh, rh), :]

        local = pltpu.make_async_copy(small_in, small_out.at[j], loc.at[0])
        local.start()
        sends = []
        for k, (px, py) in enumerate(chips):
            sends.append(pltpu.make_async_remote_copy(
                src_ref=small_in, dst_ref=small_out.at[j], send_sem=send.at[3 * n + k],
                recv_sem=recv.at[3 * n + k], device_id=(px, py, c), device_id_type=MESH))
            for a in range(n):
                sends.append(pltpu.make_async_remote_copy(
                    src_ref=half(a, j, c), dst_ref=half(a, j, c), send_sem=send.at[3 * a + k],
                    recv_sem=recv.at[3 * a + k], device_id=(px, py, c), device_id_type=MESH))
        for cp in sends:
            cp.start()
        forwards = []
        for k, (px, py) in enumerate(chips):
            blk = 2 * px + py
            for a in range(n):
                pltpu.make_async_remote_copy(
                    src_ref=half(a, blk, c), dst_ref=half(a, blk, c), send_sem=send.at[3 * a + k],
                    recv_sem=recv.at[3 * a + k], device_id=(px, py, c), device_id_type=MESH).wait_recv()
                fw = pltpu.make_async_remote_copy(
                    src_ref=half(a, blk, c), dst_ref=half(a, blk, c), send_sem=fsend.at[3 * a + k],
                    recv_sem=frecv.at[3 * a + k], device_id=sib, device_id_type=MESH)
                fw.start()
                forwards.append(fw)
        for k, (px, py) in enumerate(chips):
            blk = 2 * px + py
            pltpu.make_async_remote_copy(
                src_ref=small_in, dst_ref=small_out.at[blk], send_sem=send.at[3 * n + k],
                recv_sem=recv.at[3 * n + k], device_id=(px, py, c), device_id_type=MESH).wait_recv()
            for a in range(n):
                pltpu.make_async_remote_copy(
                    src_ref=half(a, blk, 1 - c), dst_ref=half(a, blk, 1 - c), send_sem=fsend.at[3 * a + k],
                    recv_sem=frecv.at[3 * a + k], device_id=sib, device_id_type=MESH).wait_recv()
        for cp in sends + forwards:
            cp.wait_send()
        local.wait()

    res = _call(
        body, name=name, in_specs=[_ANY] * (n + 1), out_specs=[_ANY] * (n + 1),
        out_shape=[_sds(b.shape, b.dtype) for b in bufs] + [_sds((N_SHARD,) + small.shape, small.dtype)],
        scratch=[pltpu.SemaphoreType.DMA((3 * n + 3,)), pltpu.SemaphoreType.DMA((3 * n + 3,)),
                 pltpu.SemaphoreType.DMA((3 * n,)), pltpu.SemaphoreType.DMA((3 * n,)),
                 pltpu.SemaphoreType.DMA((1,))],
        input_output_aliases={a: a for a in range(n)},
    )(*bufs, small)
    return res[:n], res[n]


def pair_exchange(arrs, name):
    n = len(arrs)

    def body(*refs):
        ins, outs = refs[:n], refs[n:2 * n]
        send, recv = refs[2 * n:]
        x, y, c = _place()
        cps = []
        for a in range(n):
            rh = arrs[a].shape[1] // 2
            cps.append(pltpu.make_async_remote_copy(
                src_ref=ins[a].at[:, pl.ds((1 - c) * rh, rh), :], dst_ref=outs[a],
                send_sem=send.at[a], recv_sem=recv.at[a], device_id=(x, y, 1 - c), device_id_type=MESH))
        for cp in cps:
            cp.start()
        for cp in cps:
            cp.wait_recv()
        for cp in cps:
            cp.wait_send()

    return _call(
        body, name=name, in_specs=[_ANY] * n, out_specs=[_ANY] * n,
        out_shape=[_sds((a.shape[0], a.shape[1] // 2, a.shape[2]), a.dtype) for a in arrs],
        scratch=[pltpu.SemaphoreType.DMA((n,)), pltpu.SemaphoreType.DMA((n,))],
    )(*arrs)


def pair_add(g, s, c_idx, name):
    nb, r, cols = g.shape
    rh = r // 2

    def body(c_ref, g_ref, s_ref, o_ref):
        o_ref[...] = (g_ref[...] + s_ref[...]).astype(BF16)

    return pl.pallas_call(
        body, name=name,
        grid_spec=pltpu.PrefetchScalarGridSpec(
            num_scalar_prefetch=1, grid=(nb,),
            in_specs=[pl.BlockSpec((1, rh, cols), lambda j, c: (j, c[0], 0)),
                      pl.BlockSpec((1, rh, cols), lambda j, c: (j, 0, 0))],
            out_specs=pl.BlockSpec((1, rh, cols), lambda j, c: (j, 0, 0))),
        out_shape=_sds((nb, rh, cols), BF16),
        compiler_params=pltpu.CompilerParams(dimension_semantics=("arbitrary",),
                                             vmem_limit_bytes=VMEM_LIMIT),
    )(c_idx, g, s)


def chip_exchange(arrs, name):
    n = len(arrs)

    def body(*refs):
        ins, outs = refs[:n], refs[n:2 * n]
        send, recv = refs[2 * n:]
        x, y, c = _place()
        j = 2 * x + y
        chips = _other_chips(x, y)
        sends = []
        for a in range(n):
            for k, (px, py) in enumerate(chips):
                sends.append(pltpu.make_async_remote_copy(
                    src_ref=ins[a].at[2 * px + py], dst_ref=outs[a].at[j], send_sem=send.at[3 * a + k],
                    recv_sem=recv.at[3 * a + k], device_id=(px, py, c), device_id_type=MESH))
        for cp in sends:
            cp.start()
        for a in range(n):
            for k, (px, py) in enumerate(chips):
                pltpu.make_async_remote_copy(
                    src_ref=ins[a].at[j], dst_ref=outs[a].at[2 * px + py], send_sem=send.at[3 * a + k],
                    recv_sem=recv.at[3 * a + k], device_id=(px, py, c), device_id_type=MESH).wait_recv()
        for cp in sends:
            cp.wait_send()

    return _call(
        body, name=name, in_specs=[_ANY] * n, out_specs=[_ANY] * n,
        out_shape=[_sds(a.shape, a.dtype) for a in arrs],
        scratch=[pltpu.SemaphoreType.DMA((3 * n,)), pltpu.SemaphoreType.DMA((3 * n,))],
    )(*arrs)


def sum_chips(r, p, shard_idx, name):
    nb, rh, cols = r.shape
    tr = rh // 2

    def body(j_ref, p_ref, *refs):
        o_ref = refs[nb]
        j = j_ref[0]
        acc = None
        for i in range(nb):
            term = jnp.where(j == i, p_ref[0], refs[i][0]).astype(F32)
            acc = term if acc is None else acc + term
        o_ref[...] = acc

    def slot(i):
        return pl.BlockSpec((1, tr, cols), lambda t, j: (jnp.where(j[0] == i, (i + 1) % nb, i), t, 0))

    return pl.pallas_call(
        body, name=name,
        grid_spec=pltpu.PrefetchScalarGridSpec(
            num_scalar_prefetch=1, grid=(rh // tr,),
            in_specs=[pl.BlockSpec((1, tr, cols), lambda t, j: (j[0], t, 0))] + [slot(i) for i in range(nb)],
            out_specs=pl.BlockSpec((tr, cols), lambda t, j: (t, 0))),
        out_shape=_sds((rh, cols), F32),
        compiler_params=pltpu.CompilerParams(dimension_semantics=("arbitrary",),
                                             vmem_limit_bytes=VMEM_LIMIT),
    )(shard_idx, p, *([r] * nb))


def pair_swap(arrs, name):
    n = len(arrs)

    def body(*refs):
        ins, outs = refs[:n], refs[n:2 * n]
        send, recv = refs[2 * n:]
        x, y, c = _place()
        cps = [pltpu.make_async_remote_copy(
            src_ref=ins[a], dst_ref=outs[a], send_sem=send.at[a], recv_sem=recv.at[a],
            device_id=(x, y, 1 - c), device_id_type=MESH) for a in range(n)]
        for cp in cps:
            cp.start()
        for cp in cps:
            cp.wait_recv()
        for cp in cps:
            cp.wait_send()

    return _call(
        body, name=name, in_specs=[_ANY] * n, out_specs=[_ANY] * n,
        out_shape=[_sds(a.shape, a.dtype) for a in arrs],
        scratch=[pltpu.SemaphoreType.DMA((n,)), pltpu.SemaphoreType.DMA((n,))],
    )(*arrs)


def adamw_pair(w, g_mine, g_sib, m, v, c_idx, name):
    r, cols = w.shape
    rh = r // 2
    tr = rh // 2
    nh = rh // tr

    def body(c_ref, w_ref, gm_ref, gs_ref, m_ref, v_ref, g_ref, d_ref, mo_ref, vo_ref):
        mine = (pl.program_id(0) // nh) == c_ref[0]
        g = jnp.where(mine, gm_ref[...], gs_ref[...])
        d, mn, vn = _adamw_math(w_ref[...], g, m_ref[...], v_ref[...])
        g_ref[...] = g
        d_ref[...] = d
        mo_ref[...] = mn
        vo_ref[...] = vn

    full = pl.BlockSpec((tr, cols), lambda i, c: (i, 0))
    part = pl.BlockSpec((tr, cols), lambda i, c: (i % nh, 0))
    return pl.pallas_call(
        body, name=name,
        grid_spec=pltpu.PrefetchScalarGridSpec(
            num_scalar_prefetch=1, grid=(r // tr,),
            in_specs=[full, part, part, full, full], out_specs=[full] * 4),
        out_shape=[_sds((r, cols), F32)] * 4,
        compiler_params=pltpu.CompilerParams(dimension_semantics=("arbitrary",),
                                             vmem_limit_bytes=VMEM_LIMIT),
    )(c_idx, w, g_mine, g_sib, m, v)


N_DEV = 8


def all_reduce_small(pack, name):
    r, cols = pack.shape

    def body(in_ref, out_ref, buf, send, recv):
        x, y, c = _place()
        me = 4 * x + 2 * y + c
        buf[me] = in_ref[...]
        peers = []
        for k in range(1, N_DEV):
            fx, fy, fc = (k >> 2) & 1, (k >> 1) & 1, k & 1
            peers.append((1 - x if fx else x, 1 - y if fy else y, 1 - c if fc else c))
        sends = [pltpu.make_async_remote_copy(
            src_ref=in_ref, dst_ref=buf.at[me], send_sem=send.at[k], recv_sem=recv.at[k],
            device_id=p, device_id_type=MESH) for k, p in enumerate(peers)]
        for cp in sends:
            cp.start()
        for k, (px, py, pc) in enumerate(peers):
            pltpu.make_async_remote_copy(
                src_ref=in_ref, dst_ref=buf.at[4 * px + 2 * py + pc], send_sem=send.at[k],
                recv_sem=recv.at[k], device_id=(px, py, pc), device_id_type=MESH).wait_recv()
        for cp in sends:
            cp.wait_send()
        acc = buf[0] + buf[1]
        for i in range(2, N_DEV):
            acc = acc + buf[i]
        out_ref[...] = acc

    vm = pl.BlockSpec(memory_space=pltpu.VMEM)
    return _call(
        body, name=name, in_specs=[vm], out_specs=vm, out_shape=_sds((r, cols), F32),
        scratch=[pltpu.VMEM((N_DEV, r, cols), F32), pltpu.SemaphoreType.DMA((N_DEV - 1,)),
                 pltpu.SemaphoreType.DMA((N_DEV - 1,))],
    )(pack)


SMALL_NAMES = ("ffn1_norm", "mix_norm", "ffn2_norm", "final_norm", "sg_ln_g", "sg_ln_b",
               "dn_norm", "a_log", "dt_bias", "sg_b", "sg_w", "conv_w")


def _to_rows(a):
    flat = a.reshape(-1)
    pad = (-flat.shape[0]) % LANES
    if pad:
        flat = jnp.pad(flat, (0, pad))
    return flat.reshape(-1, LANES)


def _pack_small(parts):
    rows = [_to_rows(parts[k]) for k in SMALL_NAMES]
    pack = jnp.concatenate(rows, axis=0)
    pad = (-pack.shape[0]) % 8
    if pad:
        pack = jnp.pad(pack, ((0, pad), (0, 0)))
    return pack


def _unpack_small(pack, shapes):
    out, r0 = {}, 0
    for k in SMALL_NAMES:
        size = 1
        for s in shapes[k]:
            size *= s
        nrows = -(-size // LANES)
        out[k] = pack[r0:r0 + nrows].reshape(-1)[:size].reshape(shapes[k])
        r0 += nrows
    return out


def kernel(x, ffn1_norm, ffn1_w_gate, ffn1_w_up, ffn1_w_down, mix_norm, w_in, conv_w, a_log, dt_bias, dn_norm, sg_ln_g, sg_ln_b, sg_w, sg_b, w_out, ffn2_norm, ffn2_w_gate, ffn2_w_up, ffn2_w_down, final_norm, loss_target, m_ffn1_norm, m_ffn1_w_gate, m_ffn1_w_up, m_ffn1_w_down, m_mix_norm, m_w_in, m_conv_w, m_a_log, m_dt_bias, m_dn_norm, m_sg_ln_g, m_sg_ln_b, m_sg_w, m_sg_b, m_w_out, m_ffn2_norm, m_ffn2_w_gate, m_ffn2_w_up, m_ffn2_w_down, m_final_norm, v_ffn1_norm, v_ffn1_w_gate, v_ffn1_w_up, v_ffn1_w_down, v_mix_norm, v_w_in, v_conv_w, v_a_log, v_dt_bias, v_dn_norm, v_sg_ln_g, v_sg_ln_b, v_sg_w, v_sg_b, v_w_out, v_ffn2_norm, v_ffn2_w_gate, v_ffn2_w_up, v_ffn2_w_down, v_final_norm):
    bsz, t_len, d = x.shape
    n = bsz * t_len
    xy, yy, cc = _place()
    shard = 2 * xy + yy

    big_names = ["ffn1_w_gate", "ffn1_w_up", "ffn1_w_down", "w_in", "w_out",
                 "ffn2_w_gate", "ffn2_w_up", "ffn2_w_down"]
    big_w = dict(ffn1_w_gate=ffn1_w_gate, ffn1_w_up=ffn1_w_up, ffn1_w_down=ffn1_w_down, w_in=w_in,
                 w_out=w_out, ffn2_w_gate=ffn2_w_gate, ffn2_w_up=ffn2_w_up, ffn2_w_down=ffn2_w_down)
    big_m = dict(ffn1_w_gate=m_ffn1_w_gate, ffn1_w_up=m_ffn1_w_up, ffn1_w_down=m_ffn1_w_down, w_in=m_w_in,
                 w_out=m_w_out, ffn2_w_gate=m_ffn2_w_gate, ffn2_w_up=m_ffn2_w_up, ffn2_w_down=m_ffn2_w_down)
    big_v = dict(ffn1_w_gate=v_ffn1_w_gate, ffn1_w_up=v_ffn1_w_up, ffn1_w_down=v_ffn1_w_down, w_in=v_w_in,
                 w_out=v_w_out, ffn2_w_gate=v_ffn2_w_gate, ffn2_w_up=v_ffn2_w_up, ffn2_w_down=v_ffn2_w_down)
    shard_idx = jnp.reshape(shard, (1,)).astype(jnp.int32)
    c_idx = jnp.reshape(cc, (1,)).astype(jnp.int32)
    placed = [cast_place(big_w[k][0], shard_idx, name="cast_" + k) for k in big_names]
    gathered, conv_g = all_gather_chips(placed, conv_w[0], name="gather_weights")
    gw = dict(zip(big_names, gathered))
    conv_full = conv_g.transpose(1, 0, 2).reshape(CONV_K, 3 * HALF_W)
    w_in_full = gw["w_in"].transpose(1, 0, 2).reshape(d, IN_COLS)
    w_in_full = jnp.pad(w_in_full, ((0, 0), (0, PROJ_W - IN_COLS)))
    w_out_full = gw["w_out"].reshape(2 * HALF_W, d)

    x0 = x.reshape(n, d)
    x1, h1, gate1, up1 = ffn_fwd(x0, ffn1_norm, gw["ffn1_w_gate"], gw["ffn1_w_up"],
                                 gw["ffn1_w_down"], name="ffn1_fwd")
    proj, h2 = in_proj_fwd(x1, mix_norm, w_in_full, name="in_proj_fwd")
    proj3 = proj.reshape(bsz, t_len, PROJ_W)
    bias_tile = jnp.repeat(sg_b[0].T, SG_GROUP_DIM, axis=1)
    sg_out = sg_fwd(proj, sg_ln_g, sg_ln_b, sg_w[0], bias_tile, name="sg_fwd")
    qkv = dn_conv_fwd(proj3, conv_full, name="dn_conv_fwd")
    alog_row = jnp.zeros((1, LANES), F32).at[0, N_HEADS:2 * N_HEADS].set(a_log[0])
    dtb_row = jnp.zeros((1, LANES), F32).at[0, N_HEADS:2 * N_HEADS].set(dt_bias[0])
    u_wy, w_wy, q_dec, k_dec, qk, tinv, gc = dn_chunk_fwd(qkv, proj3, alog_row, dtb_row,
                                                           name="dn_chunk_fwd")
    o, s_in = dn_scan_fwd(u_wy, w_wy, q_dec, k_dec, qk, gc, name="dn_scan_fwd")
    dn_out = dn_out_fwd(o.reshape(n, HALF_W), proj, dn_norm, name="dn_out_fwd")
    x2 = out_proj_fwd(x1, sg_out, dn_out, w_out_full, name="out_proj_fwd")
    x3, h3, gate2, up2 = ffn_fwd(x2, ffn2_norm, gw["ffn2_w_gate"], gw["ffn2_w_up"],
                                 gw["ffn2_w_down"], name="ffn2_fwd")
    dx3, d_final_norm, loss_tile = final_loss(x3, final_norm.reshape(1, d),
                                              loss_target.reshape(n, d), name="final_loss")
    loss = lax.psum(loss_tile[0, 0], ("x", "y", "c"))

    dx2, dgate2, dup2, act2, dyh2, d_ffn2_norm = ffn_bwd_act(
        dx3, x2, ffn2_norm, gate2, up2, gw["ffn2_w_gate"], gw["ffn2_w_up"], gw["ffn2_w_down"],
        name="ffn2_bwd_act")
    g_big = {}
    g_big["ffn2_w_gate"], g_big["ffn2_w_up"], g_big["ffn2_w_down"] = ffn_bwd_w(
        h3, dyh2, dgate2, dup2, act2, name="ffn2_bwd_w")

    d_sg, d_dn, dx2b = out_proj_bwd_x(dx2, w_out_full, name="out_proj_bwd_x")
    g_w_out = jnp.concatenate([matmul_tn(sg_out, dx2b, d, name="w_out_grad_sg"),
                               matmul_tn(dn_out, dx2b, d, name="w_out_grad_dn")], axis=0)
    g_big["w_out"] = g_w_out.reshape(N_SHARD, (2 * HALF_W) // N_SHARD, d)

    d_o, d_z, d_dn_norm = dn_out_bwd(d_dn, o.reshape(n, HALF_W), proj, dn_norm, name="dn_out_bwd")
    du, dw, dqd, dkd, dqk, dgc_scan = dn_scan_bwd(d_o.reshape(bsz, t_len, HALF_W), u_wy, w_wy, q_dec,
                                                  k_dec, qk, gc, s_in, name="dn_scan_bwd")
    d_qkv, d_pba, d_alog_row, d_dtb_row = dn_chunk_bwd(
        qkv, proj3, alog_row, dtb_row, tinv, u_wy, w_wy, du, dw, dqd, dkd, dqk, dgc_scan,
        name="dn_chunk_bwd")
    d_pqkv, d_conv = dn_conv_bwd(d_qkv, proj3, conv_full, name="dn_conv_bwd")
    d_psg, d_sg_w, d_bias_tile, d_ln_g, d_ln_b = sg_bwd(d_sg, proj, sg_ln_g, sg_ln_b, sg_w[0],
                                                        bias_tile, name="sg_bwd")
    d_proj = jnp.concatenate([d_psg, d_pqkv.reshape(n, 3 * HALF_W), d_z, d_pba.reshape(n, LANES)],
                             axis=1).astype(BF16)
    dx1, d_mix_norm = in_proj_bwd_x(d_proj, w_in_full, x1, mix_norm, dx2, name="in_proj_bwd_x")
    g_w_in = matmul_tn(h2, d_proj, 640, name="w_in_grad")[:, :IN_COLS]
    g_big["w_in"] = g_w_in.reshape(d, N_SHARD, IN_COLS // N_SHARD).transpose(1, 0, 2)

    dx0, dgate1, dup1, act1, dyh1, d_ffn1_norm = ffn_bwd_act(
        dx1, x0, ffn1_norm, gate1, up1, gw["ffn1_w_gate"], gw["ffn1_w_up"], gw["ffn1_w_down"],
        name="ffn1_bwd_act")
    g_big["ffn1_w_gate"], g_big["ffn1_w_up"], g_big["ffn1_w_down"] = ffn_bwd_w(
        h1, dyh1, dgate1, dup1, act1, name="ffn1_bwd_w")
    grad_x = dx0.reshape(bsz, t_len, d)

    g_list = [g_big[k] for k in big_names]
    from_sibling = pair_exchange(g_list, name="grad_pair_exchange")
    pair_sums = [pair_add(g, s, c_idx, name="grad_pair_add_" + k)
                 for k, g, s in zip(big_names, g_list, from_sibling)]
    from_chips = chip_exchange(pair_sums, name="grad_chip_exchange")
    halves = [sum_chips(r, p, shard_idx, name="grad_chip_sum_" + k)
              for k, r, p in zip(big_names, from_chips, pair_sums)]
    sib_halves = pair_swap(halves, name="grad_pair_swap")
    outs = {}
    for k, g_mine, g_sib in zip(big_names, halves, sib_halves):
        g, delta, m_new, v_new = adamw_pair(big_w[k][0], g_mine, g_sib, big_m[k][0], big_v[k][0], c_idx,
                                            name="adamw_" + k)
        outs[k] = (g[None], delta[None], m_new[None], v_new[None])

    small_w = dict(ffn1_norm=ffn1_norm, mix_norm=mix_norm, ffn2_norm=ffn2_norm, final_norm=final_norm,
                   sg_ln_g=sg_ln_g, sg_ln_b=sg_ln_b, dn_norm=dn_norm, a_log=a_log, dt_bias=dt_bias,
                   sg_b=sg_b, sg_w=sg_w)
    small_m = dict(ffn1_norm=m_ffn1_norm, mix_norm=m_mix_norm, ffn2_norm=m_ffn2_norm,
                   final_norm=m_final_norm, sg_ln_g=m_sg_ln_g, sg_ln_b=m_sg_ln_b, dn_norm=m_dn_norm,
                   a_log=m_a_log, dt_bias=m_dt_bias, sg_b=m_sg_b, sg_w=m_sg_w)
    small_v = dict(ffn1_norm=v_ffn1_norm, mix_norm=v_mix_norm, ffn2_norm=v_ffn2_norm,
                   final_norm=v_final_norm, sg_ln_g=v_sg_ln_g, sg_ln_b=v_sg_ln_b, dn_norm=v_dn_norm,
                   a_log=v_a_log, dt_bias=v_dt_bias, sg_b=v_sg_b, sg_w=v_sg_w)
    shapes = {k: small_w[k].shape for k in small_w}
    shapes["conv_w"] = (CONV_K, 3 * HALF_W)
    d_sg_b = d_bias_tile.reshape(SG_CHUNK, SG_GROUPS, SG_GROUP_DIM).sum(axis=-1).T
    small_g = dict(ffn1_norm=d_ffn1_norm, mix_norm=d_mix_norm, ffn2_norm=d_ffn2_norm,
                   final_norm=d_final_norm, sg_ln_g=d_ln_g, sg_ln_b=d_ln_b, dn_norm=d_dn_norm,
                   a_log=d_alog_row[:, N_HEADS:2 * N_HEADS], dt_bias=d_dtb_row[:, N_HEADS:2 * N_HEADS],
                   sg_b=d_sg_b, sg_w=d_sg_w, conv_w=d_conv)
    g_pack = all_reduce_small(_pack_small(small_g), name="small_all_reduce")
    g_small = _unpack_small(g_pack, shapes)
    cw = 3 * HALF_W // N_SHARD
    g_conv = lax.dynamic_slice_in_dim(g_small["conv_w"], shard * cw, cw, axis=1)
    zero_conv = jnp.zeros((CONV_K, 3 * HALF_W), F32)

    def packed(src, conv):
        parts = dict(src)
        parts["conv_w"] = lax.dynamic_update_slice_in_dim(zero_conv, conv[0], shard * cw, axis=1)
        return _pack_small(parts)

    d_pack, m_pack, v_pack = adamw(packed(small_w, conv_w), g_pack, packed(small_m, m_conv_w),
                                   packed(small_v, v_conv_w), name="adamw_small")
    d_small = _unpack_small(d_pack, shapes)
    m_small = _unpack_small(m_pack, shapes)
    v_small = _unpack_small(v_pack, shapes)

    def conv_block(full_arr):
        return lax.dynamic_slice_in_dim(full_arr, shard * cw, cw, axis=1)[None]

    for k in small_w:
        outs[k] = (g_small[k].reshape(small_w[k].shape), d_small[k], m_small[k], v_small[k])
    outs["conv_w"] = (g_conv[None], conv_block(d_small["conv_w"]), conv_block(m_small["conv_w"]),
                      conv_block(v_small["conv_w"]))

    order = ["ffn1_norm", "ffn1_w_gate", "ffn1_w_up", "ffn1_w_down", "mix_norm", "w_in", "conv_w",
             "a_log", "dt_bias", "dn_norm", "sg_ln_g", "sg_ln_b", "sg_w", "sg_b", "w_out", "ffn2_norm",
             "ffn2_w_gate", "ffn2_w_up", "ffn2_w_down", "final_norm"]
    return (loss, grad_x, *[outs[k][0] for k in order], *[outs[k][1] for k in order],
            *[outs[k][2] for k in order], *[outs[k][3] for k in order])
```

```python
import functools

import jax
import jax.numpy as jnp
from jax import lax
from jax.experimental import pallas as pl
from jax.experimental.pallas import tpu as pltpu

F32 = jnp.float32
BF16 = jnp.bfloat16
EPS = 1e-6

D_MODEL = 1024
N_SHARD = 4
HEAD_DIM = 128
N_HEADS = 4
DN_CHUNK = 64
SG_CHUNK = 128
SG_GROUPS = 8
SG_GROUP_DIM = 64
HALF_W = 512
PROJ_W = 3200
IN_COLS = 3080
GATE_COL_BLOCK = 24
QK_SCALE = HEAD_DIM ** -0.5
LANES = 128

ADAM_LR = 0.001
ADAM_B1 = 0.9
ADAM_B2 = 0.999
ADAM_EPS = 1e-08
ADAM_WD = 0.01
ADAM_STEP = 10

VMEM_LIMIT = 56 * 1024 * 1024
ROW_TILE = 512

NN = ((1,), (0,))
NT = ((1,), (1,))
TN = ((0,), (0,))
MESH = pl.DeviceIdType.MESH


def _dot(a, b, dims):
    return lax.dot_general(a, b, (dims, ((), ())), preferred_element_type=F32)


def _bdot(a, b, dims):
    return _dot(a.astype(BF16), b.astype(BF16), dims)


def _split(a):
    hi = a.astype(BF16)
    lo = (a - hi.astype(F32)).astype(BF16)
    return hi, lo


def _dot3(a, b, dims=NN):
    return _dot(a[0], b[0], dims) + (_dot(a[0], b[1], dims) + _dot(a[1], b[0], dims))


def _dot_exact_lhs(a, b):
    ab = a.astype(BF16)
    b1 = b.astype(BF16)
    r1 = b - b1.astype(F32)
    b2 = r1.astype(BF16)
    b3 = (r1 - b2.astype(F32)).astype(BF16)
    return _dot(ab, b1, NN) + (_dot(ab, b2, NN) + _dot(ab, b3, NN))


def _call(body, *, name, out_shape, in_specs, out_specs, grid=(), scratch=(), **kw):
    params = dict(vmem_limit_bytes=VMEM_LIMIT)
    if grid:
        params["dimension_semantics"] = ("arbitrary",) * len(grid)
    return pl.pallas_call(
        body, name=name, grid=grid, in_specs=in_specs, out_specs=out_specs,
        out_shape=out_shape, scratch_shapes=list(scratch),
        compiler_params=pltpu.CompilerParams(**params), **kw)


def _sds(shape, dtype):
    return jax.ShapeDtypeStruct(tuple(shape), dtype)


def _sigmoid(x):
    return jax.nn.sigmoid(x)


def _softplus(x):
    return jnp.maximum(x, 0.0) + jnp.log(1.0 + jnp.exp(-jnp.abs(x)))


_GELU_C = 0.7978845608028654
_GELU_A = 0.044715


def _gelu(x):
    t = jnp.tanh(_GELU_C * (x + _GELU_A * x * x * x))
    return 0.5 * x * (1.0 + t)


def _gelu_grad(x):
    t = jnp.tanh(_GELU_C * (x + _GELU_A * x * x * x))
    return 0.5 * (1.0 + t) + 0.5 * x * (1.0 - t * t) * _GELU_C * (1.0 + 3.0 * _GELU_A * x * x)


def _silu_grad(x):
    s = _sigmoid(x)
    return s * (1.0 + x * (1.0 - s))


def _rms_scale(xv):
    return lax.rsqrt(jnp.mean(xv * xv, axis=-1, keepdims=True) + EPS)


def _rms_bwd(dh, xv, g):
    r = _rms_scale(xv)
    xn = xv * r
    dg = jnp.sum(dh * xn, axis=0, keepdims=True)
    dxn = dh * g
    dx = r * (dxn - xn * jnp.mean(dxn * xn, axis=-1, keepdims=True))
    return dx, dg


def _iota2(shape, dim):
    return lax.broadcasted_iota(jnp.int32, shape, dim)


def _col_to_row(col):
    n = col.shape[0]
    eye = _iota2((n, n), 0) == _iota2((n, n), 1)
    return jnp.sum(jnp.where(eye, col, 0.0), axis=0, keepdims=True)


def _row_to_col(row):
    n = row.shape[1]
    eye = _iota2((n, n), 0) == _iota2((n, n), 1)
    return jnp.sum(jnp.where(eye, row, 0.0), axis=1, keepdims=True)


def ffn_fwd(x, gnorm, wg, wu, wd, name):
    n, d = x.shape
    nb, fb, _ = wg.shape
    tm = min(ROW_TILE, n)

    def body(x_ref, g_ref, wg_ref, wu_ref, wd_ref, xo_ref, h_ref, gate_ref, up_ref, acc_ref):
        j = pl.program_id(1)

        @pl.when(j == 0)
        def _():
            xv = x_ref[...]
            h_ref[...] = (xv * _rms_scale(xv) * g_ref[...]).astype(BF16)
            acc_ref[...] = jnp.zeros_like(acc_ref)

        h = h_ref[...]
        gate = _dot(h, wg_ref[0], NT)
        up = _dot(h, wu_ref[0], NT)
        gate_ref[0] = gate.astype(BF16)
        up_ref[0] = up.astype(BF16)
        act = (gate * _sigmoid(gate) * up).astype(BF16)
        acc_ref[...] += _dot(act, wd_ref[0], NN)

        @pl.when(j == nb - 1)
        def _():
            xo_ref[...] = x_ref[...] + 0.5 * acc_ref[...]

    row = pl.BlockSpec((tm, d), lambda i, j: (i, 0))
    return _call(
        body, name=name, grid=(n // tm, nb),
        in_specs=[row, pl.BlockSpec((1, d), lambda i, j: (0, 0))]
        + [pl.BlockSpec((1, fb, d), lambda i, j: (j, 0, 0))] * 3,
        out_specs=[row, row,
                   pl.BlockSpec((1, tm, fb), lambda i, j: (j, i, 0)),
                   pl.BlockSpec((1, tm, fb), lambda i, j: (j, i, 0))],
        out_shape=[_sds((n, d), F32), _sds((n, d), BF16),
                   _sds((nb, n, fb), BF16), _sds((nb, n, fb), BF16)],
        scratch=[pltpu.VMEM((tm, d), F32)],
    )(x, gnorm, wg, wu, wd)


def ffn_bwd_act(dy, x, gnorm, gate, up, wg, wu, wd, name):
    n, d = x.shape
    nb, fb, _ = wg.shape
    tm = min(ROW_TILE, n)

    def body(dy_ref, x_ref, g_ref, gate_ref, up_ref, wg_ref, wu_ref, wd_ref,
             dx_ref, dgate_ref, dup_ref, act_ref, dyh_ref, dg_ref, acc_ref):
        i = pl.program_id(0)
        j = pl.program_id(1)

        @pl.when(jnp.logical_and(i == 0, j == 0))
        def _():
            dg_ref[...] = jnp.zeros_like(dg_ref)

        @pl.when(j == 0)
        def _():
            dyh_ref[...] = (0.5 * dy_ref[...]).astype(BF16)
            acc_ref[...] = jnp.zeros_like(acc_ref)

        dact = _dot(dyh_ref[...], wd_ref[0], NT)
        gt = gate_ref[0].astype(F32)
        u = up_ref[0].astype(F32)
        s = _sigmoid(gt)
        silu = gt * s
        dup = (dact * silu).astype(BF16)
        dgate = (dact * u * (s * (1.0 + gt * (1.0 - s)))).astype(BF16)
        dup_ref[0] = dup
        dgate_ref[0] = dgate
        act_ref[0] = (silu * u).astype(BF16)
        acc_ref[...] += _dot(dgate, wg_ref[0], NN) + _dot(dup, wu_ref[0], NN)

        @pl.when(j == nb - 1)
        def _():
            dxn, dg = _rms_bwd(acc_ref[...], x_ref[...], g_ref[...])
            dx_ref[...] = dy_ref[...] + dxn
            dg_ref[...] += dg

    row = pl.BlockSpec((tm, d), lambda i, j: (i, 0))
    blk = pl.BlockSpec((1, tm, fb), lambda i, j: (j, i, 0))
    vec = pl.BlockSpec((1, d), lambda i, j: (0, 0))
    wblk = pl.BlockSpec((1, fb, d), lambda i, j: (j, 0, 0))
    return _call(
        body, name=name, grid=(n // tm, nb),
        in_specs=[row, row, vec, blk, blk, wblk, wblk, wblk],
        out_specs=[row, blk, blk, blk, row, vec],
        out_shape=[_sds((n, d), F32), _sds((nb, n, fb), BF16), _sds((nb, n, fb), BF16),
                   _sds((nb, n, fb), BF16), _sds((n, d), BF16), _sds((1, d), F32)],
        scratch=[pltpu.VMEM((tm, d), F32)],
    )(dy, x, gnorm, gate, up, wg, wu, wd)


def ffn_bwd_w(h, dyh, dgate, dup, act, name):
    n, d = h.shape
    nb, _, fb = dgate.shape
    tk = min(ROW_TILE, n)

    def body(h_ref, dyh_ref, dgate_ref, dup_ref, act_ref, dwg_ref, dwu_ref, dwd_ref):
        @pl.when(pl.program_id(1) == 0)
        def _():
            dwg_ref[...] = jnp.zeros_like(dwg_ref)
            dwu_ref[...] = jnp.zeros_like(dwu_ref)
            dwd_ref[...] = jnp.zeros_like(dwd_ref)

        hv = h_ref[...]
        dwg_ref[0] += _dot(dgate_ref[0], hv, TN)
        dwu_ref[0] += _dot(dup_ref[0], hv, TN)
        dwd_ref[0] += _dot(act_ref[0], dyh_ref[...], TN)

    row = pl.BlockSpec((tk, d), lambda j, k: (k, 0))
    blk = pl.BlockSpec((1, tk, fb), lambda j, k: (j, k, 0))
    return _call(
        body, name=name, grid=(nb, n // tk),
        in_specs=[row, row, blk, blk, blk],
        out_specs=[pl.BlockSpec((1, fb, d), lambda j, k: (j, 0, 0))] * 3,
        out_shape=[_sds((nb, fb, d), F32)] * 3,
    )(h, dyh, dgate, dup, act)


def final_loss(x, gnorm, target, name):
    n, d = x.shape
    tm = min(ROW_TILE, n)

    def body(x_ref, g_ref, t_ref, dx_ref, dg_ref, loss_ref):
        @pl.when(pl.program_id(0) == 0)
        def _():
            dg_ref[...] = jnp.zeros_like(dg_ref)
            loss_ref[...] = jnp.zeros_like(loss_ref)

        xv = x_ref[...]
        y = xv * _rms_scale(xv) * g_ref[...]
        err = y - t_ref[...]
        part = 0.5 * jnp.sum(jnp.mean(err * err, axis=-1, keepdims=True), axis=0, keepdims=True)
        loss_ref[...] += jnp.broadcast_to(part, loss_ref.shape)
        dx, dg = _rms_bwd(err * (1.0 / d), xv, g_ref[...])
        dx_ref[...] = dx
        dg_ref[...] += dg

    row = pl.BlockSpec((tm, d), lambda i: (i, 0))
    vec = pl.BlockSpec((1, d), lambda i: (0, 0))
    return _call(
        body, name=name, grid=(n // tm,),
        in_specs=[row, vec, row],
        out_specs=[row, vec, pl.BlockSpec((1, LANES), lambda i: (0, 0))],
        out_shape=[_sds((n, d), F32), _sds((1, d), F32), _sds((1, LANES), F32)],
    )(x, gnorm, target)


def in_proj_fwd(x, gnorm, w, name):
    n, d = x.shape
    cols = w.shape[1]
    tm = min(ROW_TILE, n)
    tn = 640

    def body(x_ref, g_ref, w_ref, p_ref, h_ref):
        @pl.when(pl.program_id(1) == 0)
        def _():
            xv = x_ref[...]
            h_ref[...] = (xv * _rms_scale(xv) * g_ref[...]).astype(BF16)

        p_ref[...] = _dot(h_ref[...], w_ref[...], NN)

    return _call(
        body, name=name, grid=(n // tm, cols // tn),
        in_specs=[pl.BlockSpec((tm, d), lambda i, j: (i, 0)),
                  pl.BlockSpec((1, d), lambda i, j: (0, 0)),
                  pl.BlockSpec((d, tn), lambda i, j: (0, j))],
        out_specs=[pl.BlockSpec((tm, tn), lambda i, j: (i, j)),
                   pl.BlockSpec((tm, d), lambda i, j: (i, 0))],
        out_shape=[_sds((n, cols), F32), _sds((n, d), BF16)],
    )(x, gnorm, w)


def in_proj_bwd_x(dproj, w, x, gnorm, dres, name):
    n, d = x.shape
    cols = w.shape[1]
    tm = min(ROW_TILE, n)

    def body(dp_ref, w_ref, x_ref, g_ref, dr_ref, dx_ref, dg_ref):
        @pl.when(pl.program_id(0) == 0)
        def _():
            dg_ref[...] = jnp.zeros_like(dg_ref)

        dh = _dot(dp_ref[...], w_ref[...], NT)
        dxn, dg = _rms_bwd(dh, x_ref[...], g_ref[...])
        dx_ref[...] = dr_ref[...] + dxn
        dg_ref[...] += dg

    row = pl.BlockSpec((tm, d), lambda i: (i, 0))
    vec = pl.BlockSpec((1, d), lambda i: (0, 0))
    return _call(
        body, name=name, grid=(n // tm,),
        in_specs=[pl.BlockSpec((tm, cols), lambda i: (i, 0)),
                  pl.BlockSpec((d, cols), lambda i: (0, 0)), row, vec, row],
        out_specs=[row, vec],
        out_shape=[_sds((n, d), F32), _sds((1, d), F32)],
    )(dproj, w, x, gnorm, dres)


def matmul_tn(a, b, tn, name):
    n, ka = a.shape
    cb = b.shape[1]
    tk = min(ROW_TILE, n)

    def body(a_ref, b_ref, o_ref):
        @pl.when(pl.program_id(1) == 0)
        def _():
            o_ref[...] = jnp.zeros_like(o_ref)

        o_ref[...] += _dot(a_ref[...], b_ref[...], TN)

    return _call(
        body, name=name, grid=(cb // tn, n // tk),
        in_specs=[pl.BlockSpec((tk, ka), lambda j, k: (k, 0)),
                  pl.BlockSpec((tk, tn), lambda j, k: (k, j))],
        out_specs=pl.BlockSpec((ka, tn), lambda j, k: (0, j)),
        out_shape=_sds((ka, cb), F32),
    )(a, b)


def out_proj_fwd(x, sg_out, dn_out, w, name):
    n, d = x.shape
    tm = min(ROW_TILE, n)

    def body(x_ref, a_ref, b_ref, w_ref, o_ref):
        o_ref[...] = (x_ref[...] + _dot(a_ref[...], w_ref[0:HALF_W, :], NN)
                      + _dot(b_ref[...], w_ref[HALF_W:2 * HALF_W, :], NN))

    row = pl.BlockSpec((tm, d), lambda i: (i, 0))
    half = pl.BlockSpec((tm, HALF_W), lambda i: (i, 0))
    return _call(
        body, name=name, grid=(n // tm,),
        in_specs=[row, half, half, pl.BlockSpec((2 * HALF_W, d), lambda i: (0, 0))],
        out_specs=row, out_shape=_sds((n, d), F32),
    )(x, sg_out, dn_out, w)


def out_proj_bwd_x(dy, w, name):
    n, d = dy.shape
    tm = min(ROW_TILE, n)

    def body(dy_ref, w_ref, dsg_ref, ddn_ref, dyb_ref):
        dyb = dy_ref[...].astype(BF16)
        dyb_ref[...] = dyb
        dsg_ref[...] = _dot(dyb, w_ref[0:HALF_W, :], NT)
        ddn_ref[...] = _dot(dyb, w_ref[HALF_W:2 * HALF_W, :], NT)

    row = pl.BlockSpec((tm, d), lambda i: (i, 0))
    half = pl.BlockSpec((tm, HALF_W), lambda i: (i, 0))
    return _call(
        body, name=name, grid=(n // tm,),
        in_specs=[row, pl.BlockSpec((2 * HALF_W, d), lambda i: (0, 0))],
        out_specs=[half, half, row],
        out_shape=[_sds((n, HALF_W), F32), _sds((n, HALF_W), F32), _sds((n, d), BF16)],
    )(dy, w)


def _sg_group_masks():
    col = _iota2((SG_CHUNK, HALF_W), 1)
    return [jnp.logical_and(col >= g * SG_GROUP_DIM, col < (g + 1) * SG_GROUP_DIM)
            for g in range(SG_GROUPS)]


def _sg_causal():
    return _iota2((SG_CHUNK, SG_CHUNK), 0) >= _iota2((SG_CHUNK, SG_CHUNK), 1)


def _sg_forward_chunk(pu, pv, ln_g, ln_b, wc, bias, masks):
    u = _gelu(pu)
    v = _gelu(pv)
    mu = jnp.mean(v, axis=-1, keepdims=True)
    vc = v - mu
    rs = lax.rsqrt(jnp.mean(vc * vc, axis=-1, keepdims=True) + EPS)
    xhat = vc * rs
    vn = (xhat * ln_g + ln_b).astype(BF16)
    vs = bias
    for g in range(SG_GROUPS):
        vs = vs + jnp.where(masks[g], _dot(wc[g], vn, NN), 0.0)
    return u, xhat, rs, vn, vs


def sg_fwd(proj, ln_g, ln_b, w_s, bias_tile, name):
    n = proj.shape[0]
    tm = min(ROW_TILE, n)

    def body(pu_ref, pv_ref, g_ref, b_ref, w_ref, bias_ref, o_ref):
        causal = _sg_causal()
        wc = [jnp.where(causal, w_ref[g], 0.0).astype(BF16) for g in range(SG_GROUPS)]
        masks = _sg_group_masks()
        for ci in range(tm // SG_CHUNK):
            rows = slice(ci * SG_CHUNK, (ci + 1) * SG_CHUNK)
            u, _, _, _, vs = _sg_forward_chunk(pu_ref[rows, :], pv_ref[rows, :], g_ref[...],
                                               b_ref[...], wc, bias_ref[...], masks)
            o_ref[rows, :] = (u * vs).astype(BF16)

    vec = pl.BlockSpec((1, HALF_W), lambda i: (0, 0))
    return _call(
        body, name=name, grid=(n // tm,),
        in_specs=[pl.BlockSpec((tm, HALF_W), lambda i: (i, 0)),
                  pl.BlockSpec((tm, HALF_W), lambda i: (i, 1)), vec, vec,
                  pl.BlockSpec((SG_GROUPS, SG_CHUNK, SG_CHUNK), lambda i: (0, 0, 0)),
                  pl.BlockSpec((SG_CHUNK, HALF_W), lambda i: (0, 0))],
        out_specs=pl.BlockSpec((tm, HALF_W), lambda i: (i, 0)),
        out_shape=_sds((n, HALF_W), BF16),
    )(proj, proj, ln_g, ln_b, w_s, bias_tile)


def sg_bwd(dsg, proj, ln_g, ln_b, w_s, bias_tile, name):
    n = proj.shape[0]
    tm = min(ROW_TILE, n)

    def body(d_ref, pu_ref, pv_ref, g_ref, b_ref, w_ref, bias_ref,
             dp_ref, dw_ref, db_ref, dlg_ref, dlb_ref):
        @pl.when(pl.program_id(0) == 0)
        def _():
            dw_ref[...] = jnp.zeros_like(dw_ref)
            db_ref[...] = jnp.zeros_like(db_ref)
            dlg_ref[...] = jnp.zeros_like(dlg_ref)
            dlb_ref[...] = jnp.zeros_like(dlb_ref)

        causal = _sg_causal()
        wc = [jnp.where(causal, w_ref[g], 0.0).astype(BF16) for g in range(SG_GROUPS)]
        masks = _sg_group_masks()
        ln_g_v = g_ref[...]
        for ci in range(tm // SG_CHUNK):
            rows = slice(ci * SG_CHUNK, (ci + 1) * SG_CHUNK)
            pu = pu_ref[rows, :]
            pv = pv_ref[rows, :]
            u, xhat, rs, vn, vs = _sg_forward_chunk(pu, pv, ln_g_v, b_ref[...], wc,
                                                    bias_ref[...], masks)
            dout = d_ref[rows, :]
            dp_ref[rows, 0:HALF_W] = (dout * vs * _gelu_grad(pu)).astype(BF16)
            dvs = dout * u
            dvs_b = dvs.astype(BF16)
            db_ref[...] += dvs
            dvn = jnp.zeros_like(dvs)
            for g in range(SG_GROUPS):
                dvn = dvn + jnp.where(masks[g], _dot(wc[g], dvs_b, TN), 0.0)
                dwg = _dot(jnp.where(masks[g], dvs_b, jnp.zeros_like(dvs_b)), vn, NT)
                dw_ref[g] += jnp.where(causal, dwg, 0.0)
            dlg_ref[...] += jnp.sum(dvn * xhat, axis=0, keepdims=True)
            dlb_ref[...] += jnp.sum(dvn, axis=0, keepdims=True)
            dxh = dvn * ln_g_v
            dv = rs * (dxh - jnp.mean(dxh, axis=-1, keepdims=True)
                       - xhat * jnp.mean(dxh * xhat, axis=-1, keepdims=True))
            dp_ref[rows, HALF_W:2 * HALF_W] = (dv * _gelu_grad(pv)).astype(BF16)

    vec = pl.BlockSpec((1, HALF_W), lambda i: (0, 0))
    wspec = pl.BlockSpec((SG_GROUPS, SG_CHUNK, SG_CHUNK), lambda i: (0, 0, 0))
    tile = pl.BlockSpec((SG_CHUNK, HALF_W), lambda i: (0, 0))
    return _call(
        body, name=name, grid=(n // tm,),
        in_specs=[pl.BlockSpec((tm, HALF_W), lambda i: (i, 0)),
                  pl.BlockSpec((tm, HALF_W), lambda i: (i, 0)),
                  pl.BlockSpec((tm, HALF_W), lambda i: (i, 1)), vec, vec, wspec, tile],
        out_specs=[pl.BlockSpec((tm, 2 * HALF_W), lambda i: (i, 0)), wspec, tile, vec, vec],
        out_shape=[_sds((n, PROJ_W), BF16), _sds((SG_GROUPS, SG_CHUNK, SG_CHUNK), F32),
                   _sds((SG_CHUNK, HALF_W), F32), _sds((1, HALF_W), F32), _sds((1, HALF_W), F32)],
    )(dsg, proj, proj, ln_g, ln_b, w_s, bias_tile)


CONV_K = 4
CONV_BLOCK = 256


def _shift_down(x, s):
    if s == 0:
        return x
    rolled = pltpu.roll(x, s, 0)
    return jnp.where(_iota2(x.shape, 0) >= s, rolled, 0.0)


def _shift_up(x, s):
    if s == 0:
        return x
    t_len = x.shape[0]
    rolled = pltpu.roll(x, t_len - s, 0)
    return jnp.where(_iota2(x.shape, 0) < t_len - s, rolled, 0.0)


def _conv(x, w):
    y = _shift_down(x, CONV_K - 1) * w[0:1, :]
    for j in range(1, CONV_K):
        y = y + _shift_down(x, CONV_K - 1 - j) * w[j:j + 1, :]
    return y


def dn_conv_fwd(proj3, conv_w, name):
    b, t, _ = proj3.shape
    nblk = 3 * HALF_W // CONV_BLOCK
    first = 2 * HALF_W // CONV_BLOCK
    n_norm = 2 * HALF_W // CONV_BLOCK

    def body(x_ref, w_ref, o_ref):
        s = pl.program_id(1)
        y = _conv(x_ref[0], w_ref[...])
        y = y * _sigmoid(y)

        @pl.when(s < n_norm)
        def _():
            for h in range(CONV_BLOCK // HEAD_DIM):
                cs = slice(h * HEAD_DIM, (h + 1) * HEAD_DIM)
                yh = y[:, cs]
                o_ref[0, :, cs] = yh * lax.rsqrt(jnp.sum(yh * yh, axis=-1, keepdims=True) + EPS)

        @pl.when(s >= n_norm)
        def _():
            o_ref[0] = y

    return _call(
        body, name=name, grid=(b, nblk),
        in_specs=[pl.BlockSpec((1, t, CONV_BLOCK), lambda i, s: (i, 0, first + s)),
                  pl.BlockSpec((CONV_K, CONV_BLOCK), lambda i, s: (0, s))],
        out_specs=pl.BlockSpec((1, t, CONV_BLOCK), lambda i, s: (i, 0, s)),
        out_shape=_sds((b, t, 3 * HALF_W), F32),
    )(proj3, conv_w)


def dn_conv_bwd(dqkv, proj3, conv_w, dproj3, name):
    b, t, _ = proj3.shape
    nblk = 3 * HALF_W // CONV_BLOCK
    first = 2 * HALF_W // CONV_BLOCK
    n_norm = 2 * HALF_W // CONV_BLOCK

    def body(d_ref, x_ref, w_ref, dproj_in, dx_ref, dw_ref, ds_ref):
        s = pl.program_id(0)

        @pl.when(pl.program_id(1) == 0)
        def _():
            dw_ref[...] = jnp.zeros_like(dw_ref)

        x = x_ref[0]
        w = w_ref[...]
        c = _conv(x, w)
        sg = _sigmoid(c)
        y = c * sg

        @pl.when(s < n_norm)
        def _():
            for h in range(CONV_BLOCK // HEAD_DIM):
                cs = slice(h * HEAD_DIM, (h + 1) * HEAD_DIM)
                yh = y[:, cs]
                r = lax.rsqrt(jnp.sum(yh * yh, axis=-1, keepdims=True) + EPS)
                nh = yh * r
                dn = d_ref[0, :, cs]
                ds_ref[:, cs] = r * (dn - nh * jnp.sum(dn * nh, axis=-1, keepdims=True))

        @pl.when(s >= n_norm)
        def _():
            ds_ref[...] = d_ref[0]

        dc = ds_ref[...] * (sg * (1.0 + c * (1.0 - sg)))
        dx = _shift_up(dc, CONV_K - 1) * w[0:1, :]
        for j in range(1, CONV_K):
            dx = dx + _shift_up(dc, CONV_K - 1 - j) * w[j:j + 1, :]
        dx_ref[0] = dx.astype(BF16)
        for j in range(CONV_K):
            dw_ref[j:j + 1, :] += jnp.sum(dc * _shift_down(x, CONV_K - 1 - j), axis=0, keepdims=True)

    return _call(
        body, name=name, grid=(nblk, b),
        in_specs=[pl.BlockSpec((1, t, CONV_BLOCK), lambda s, i: (i, 0, s)),
                  pl.BlockSpec((1, t, CONV_BLOCK), lambda s, i: (i, 0, first + s)),
                  pl.BlockSpec((CONV_K, CONV_BLOCK), lambda s, i: (0, s)), _ANY],
        out_specs=[pl.BlockSpec((1, t, CONV_BLOCK), lambda s, i: (i, 0, first + s)),
                   pl.BlockSpec((CONV_K, CONV_BLOCK), lambda s, i: (0, s))],
        out_shape=[_sds(dproj3.shape, BF16), _sds((CONV_K, 3 * HALF_W), F32)],
        scratch=[pltpu.VMEM((t, CONV_BLOCK), F32)],
        input_output_aliases={3: 0},
    )(dqkv, proj3, conv_w, dproj3)


def _chunk_masks():
    ii = _iota2((DN_CHUNK, DN_CHUNK), 0)
    jj = _iota2((DN_CHUNK, DN_CHUNK), 1)
    return ii >= jj, ii > jj, ii == jj


LOCKSTEP_CHUNKS = 2


def _inv_unit_lower_many(l_mats, eye):
    eye_f = jnp.where(eye, 1.0, 0.0)
    ps = [-l for l in l_mats]
    ts = [eye_f + p for p in ps]
    pss = [_split(p) for p in ps]
    size = 2
    while size < DN_CHUNK:
        ps = [_dot3(s, s) for s in pss]
        pss = [_split(p) for p in ps]
        ts = [t + _dot3(_split(t), s) for t, s in zip(ts, pss)]
        size *= 2
    return ts


def _gates(pba, ea_row, dtb_row):
    beta = _sigmoid(pba)
    g = -ea_row * _softplus(pba + dtb_row)
    return beta, g


def _chunk_decay(gcol):
    incl, strict, eye = _chunk_masks()
    grow = jnp.sum(jnp.where(eye, gcol, 0.0), axis=0, keepdims=True)
    decay = jnp.where(incl, jnp.exp(jnp.where(incl, gcol - grow, 0.0)), 0.0)
    return decay, incl, strict, eye


def dn_chunk_fwd(qkv, proj3, alog_row, dtb_row, name):
    b, t, _ = qkv.shape
    rblk = min(256, t)
    n_in = rblk // DN_CHUNK

    def body(q_ref, k_ref, v_ref, pba_ref, al_ref, dtb_ref,
             u_ref, w_ref, qd_ref, kd_ref, qk_ref, ti_ref, gc_ref):
        ea = jnp.exp(al_ref[...])
        tri = jnp.where(_chunk_masks()[0], 1.0, 0.0)

        _, strict, eye = _chunk_masks()

        def chunk_group(cg, carry):
            items = []
            for sub in range(LOCKSTEP_CHUNKS):
                rows = pl.ds(pl.multiple_of((cg * LOCKSTEP_CHUNKS + sub) * DN_CHUNK, DN_CHUNK), DN_CHUNK)
                beta_all, g_all = _gates(pba_ref[0, rows, :], ea, dtb_ref[...])
                gc = _dot_exact_lhs(tri, g_all)
                gc_ref[0, rows, :] = gc
                for h in range(N_HEADS):
                    items.append((rows, h, beta_all[:, h:h + 1], gc[:, N_HEADS + h:N_HEADS + h + 1]))
            ks, kbs, decays, egs = [], [], [], []
            for rows, h, beta, gcol in items:
                cs = slice(h * HEAD_DIM, (h + 1) * HEAD_DIM)
                k = k_ref[0, rows, cs]
                ks.append(k)
                kbs.append(k * beta)
                decays.append(_chunk_decay(gcol)[0])
                egs.append(jnp.exp(gcol))
            ms = [_bdot(kb, k, NT) for kb, k in zip(kbs, ks)]
            tinvs = _inv_unit_lower_many([jnp.where(strict, m * dc, 0.0) for m, dc in zip(ms, decays)], eye)
            tsps = [_split(t) for t in tinvs]
            for (rows, h, beta, gcol), tsp, tinv in zip(items, tsps, tinvs):
                cs = slice(h * HEAD_DIM, (h + 1) * HEAD_DIM)
                u_ref[0, rows, cs] = _dot3(tsp, _split(v_ref[0, rows, cs] * beta))
                ti_ref[0, h, rows, :] = tinv
            for (rows, h, beta, gcol), tsp, kb, eg in zip(items, tsps, kbs, egs):
                cs = slice(h * HEAD_DIM, (h + 1) * HEAD_DIM)
                w_ref[0, rows, cs] = _dot3(tsp, _split(kb * eg))
            for (rows, h, beta, gcol), k, dc, eg in zip(items, ks, decays, egs):
                cs = slice(h * HEAD_DIM, (h + 1) * HEAD_DIM)
                q = q_ref[0, rows, cs] * QK_SCALE
                qk_ref[0, h, rows, :] = _bdot(q, k, NT) * dc
                qd_ref[0, rows, cs] = q * eg
                kd_ref[0, rows, cs] = k * jnp.exp(gcol[DN_CHUNK - 1:DN_CHUNK, :] - gcol)
            return carry

        lax.fori_loop(0, n_in // LOCKSTEP_CHUNKS, chunk_group, 0)

    def seg(cblk):
        return pl.BlockSpec((1, rblk, HALF_W), lambda i, r: (i, r, cblk))

    vec = pl.BlockSpec((1, LANES), lambda i, r: (0, 0))
    wide = pl.BlockSpec((1, rblk, HALF_W), lambda i, r: (i, r, 0))
    sq = pl.BlockSpec((1, N_HEADS, rblk, DN_CHUNK), lambda i, r: (i, 0, r, 0))
    return _call(
        body, name=name, grid=(b, t // rblk),
        in_specs=[seg(0), seg(1), seg(2),
                  pl.BlockSpec((1, rblk, LANES), lambda i, r: (i, r, GATE_COL_BLOCK)), vec, vec],
        out_specs=[wide, wide, wide, wide, sq, sq,
                   pl.BlockSpec((1, rblk, LANES), lambda i, r: (i, r, 0))],
        out_shape=[_sds((b, t, HALF_W), F32)] * 4
        + [_sds((b, N_HEADS, t, DN_CHUNK), F32)] * 2 + [_sds((b, t, LANES), F32)],
    )(qkv, qkv, qkv, proj3, alog_row, dtb_row)


def dn_scan_fwd(u, w, qd, kd, qk, gc, name):
    b, t, _ = u.shape
    nc = t // DN_CHUNK
    bh = b * N_HEADS

    def body(u_ref, w_ref, qd_ref, kd_ref, qk_ref, gc_ref, o_ref, sin_ref, s_ref):
        @pl.when(pl.program_id(0) == 0)
        def _():
            s_ref[...] = jnp.zeros_like(s_ref)

        items = [(bi, h, slice(h * HEAD_DIM, (h + 1) * HEAD_DIM)) for bi in range(b) for h in range(N_HEADS)]
        sbs = []
        for bi, h, cs in items:
            s = s_ref[bi * N_HEADS + h]
            sin_ref[0, bi * N_HEADS + h] = s
            sbs.append(s.astype(BF16))
        ws = [_bdot(w_ref[bi, :, cs], sb, NN) for (bi, h, cs), sb in zip(items, sbs)]
        qs = [_bdot(qd_ref[bi, :, cs], sb, NN) for (bi, h, cs), sb in zip(items, sbs)]
        vbs = [(u_ref[bi, :, cs] - wsi).astype(BF16) for (bi, h, cs), wsi in zip(items, ws)]
        for (bi, h, cs), qsi, vb in zip(items, qs, vbs):
            o_ref[bi, :, cs] = qsi + _bdot(qk_ref[bi, h], vb, NN)
        for (bi, h, cs), vb in zip(items, vbs):
            gl = jnp.exp(gc_ref[bi, DN_CHUNK - 1:DN_CHUNK, N_HEADS + h:N_HEADS + h + 1])
            idx = bi * N_HEADS + h
            s_ref[idx] = s_ref[idx] * gl + _bdot(kd_ref[bi, :, cs], vb, TN)

    wide = pl.BlockSpec((b, DN_CHUNK, HALF_W), lambda c: (0, c, 0))
    return _call(
        body, name=name, grid=(nc,),
        in_specs=[wide, wide, wide, wide,
                  pl.BlockSpec((b, N_HEADS, DN_CHUNK, DN_CHUNK), lambda c: (0, 0, c, 0)),
                  pl.BlockSpec((b, DN_CHUNK, LANES), lambda c: (0, c, 0))],
        out_specs=[wide, pl.BlockSpec((1, bh, HEAD_DIM, HEAD_DIM), lambda c: (c, 0, 0, 0))],
        out_shape=[_sds((b, t, HALF_W), F32), _sds((nc, bh, HEAD_DIM, HEAD_DIM), F32)],
        scratch=[pltpu.VMEM((bh, HEAD_DIM, HEAD_DIM), F32)],
    )(u, w, qd, kd, qk, gc)


def dn_scan_bwd(do, u, w, qd, kd, qk, gc, s_in, name):
    b, t, _ = u.shape
    nc = t // DN_CHUNK
    bh = b * N_HEADS

    def body(do_ref, u_ref, w_ref, qd_ref, kd_ref, qk_ref, gc_ref, sin_ref,
             du_ref, dw_ref, dqd_ref, dkd_ref, dqk_ref, dgc_ref, ds_ref):
        @pl.when(pl.program_id(0) == 0)
        def _():
            ds_ref[...] = jnp.zeros_like(ds_ref)

        last_row = _iota2((DN_CHUNK, LANES), 0) == DN_CHUNK - 1
        lane = _iota2((DN_CHUNK, LANES), 1)
        items = [(bi, h, slice(h * HEAD_DIM, (h + 1) * HEAD_DIM)) for bi in range(b) for h in range(N_HEADS)]
        sbs = [sin_ref[0, bi * N_HEADS + h].astype(BF16) for bi, h, cs in items]
        wvs = [w_ref[bi, :, cs].astype(BF16) for bi, h, cs in items]
        dovs = [do_ref[bi, :, cs].astype(BF16) for bi, h, cs in items]
        dsbs = [ds_ref[bi * N_HEADS + h].astype(BF16) for bi, h, cs in items]
        vbs = [(u_ref[bi, :, cs] - _dot(wv, sb, NN)).astype(BF16)
               for (bi, h, cs), wv, sb in zip(items, wvs, sbs)]
        for (bi, h, cs), dov, sb in zip(items, dovs, sbs):
            dqd_ref[bi, :, cs] = _dot(dov, sb, NT)
        dvns = [_dot(kd_ref[bi, :, cs].astype(BF16), dsb, NN) + _dot(qk_ref[bi, h].astype(BF16), dov, TN)
                for (bi, h, cs), dsb, dov in zip(items, dsbs, dovs)]
        for (bi, h, cs), vb, dsb, dov in zip(items, vbs, dsbs, dovs):
            dkd_ref[bi, :, cs] = _dot(vb, dsb, NT)
            dqk_ref[bi, h] = _dot(dov, vb, NT)
        dgls = []
        for (bi, h, cs), dvn, sb, wv, dov in zip(items, dvns, sbs, wvs, dovs):
            idx = bi * N_HEADS + h
            du_ref[bi, :, cs] = dvn
            dvn_b = dvn.astype(BF16)
            dw_ref[bi, :, cs] = -_dot(dvn_b, sb, NT)
            gl = jnp.exp(gc_ref[bi, DN_CHUNK - 1:DN_CHUNK, N_HEADS + h:N_HEADS + h + 1])
            ds = ds_ref[idx]
            dgl = jnp.sum(jnp.sum(ds * sin_ref[0, idx], axis=1, keepdims=True), axis=0, keepdims=True)
            dgls.append(dgl * gl)
            ds_ref[idx] = (ds * gl + _dot(qd_ref[bi, :, cs].astype(BF16), dov, TN)
                           - _dot(wv, dvn_b, TN))
        for bi in range(b):
            dgc = jnp.zeros((DN_CHUNK, LANES), F32)
            for h in range(N_HEADS):
                dgc = dgc + jnp.where(jnp.logical_and(last_row, lane == N_HEADS + h),
                                      dgls[bi * N_HEADS + h], 0.0)
            dgc_ref[bi] = dgc

    def rev(c):
        return nc - 1 - c

    wide = pl.BlockSpec((b, DN_CHUNK, HALF_W), lambda c: (0, rev(c), 0))
    sq = pl.BlockSpec((b, N_HEADS, DN_CHUNK, DN_CHUNK), lambda c: (0, 0, rev(c), 0))
    gates = pl.BlockSpec((b, DN_CHUNK, LANES), lambda c: (0, rev(c), 0))
    return _call(
        body, name=name, grid=(nc,),
        in_specs=[wide, wide, wide, wide, wide, sq, gates,
                  pl.BlockSpec((1, bh, HEAD_DIM, HEAD_DIM), lambda c: (rev(c), 0, 0, 0))],
        out_specs=[wide, wide, wide, wide, sq, gates],
        out_shape=[_sds((b, t, HALF_W), F32)] * 4
        + [_sds((b, N_HEADS, t, DN_CHUNK), F32), _sds((b, t, LANES), F32)],
        scratch=[pltpu.VMEM((bh, HEAD_DIM, HEAD_DIM), F32)],
    )(do, u, w, qd, kd, qk, gc, s_in)


def dn_chunk_bwd(qkv, proj3, alog_row, dtb_row, tinv, u, w, du, dw, dqd, dkd, dqk, dgc_scan, dproj3, name):
    b, t, _ = qkv.shape
    rblk = min(256, t)
    n_in = rblk // DN_CHUNK

    def body(q_ref, k_ref, v_ref, pba_ref, al_ref, dtb_ref, ti_ref, u_ref, w_ref,
             du_ref, dw_ref, dqd_ref, dkd_ref, dqk_ref, dgs_ref, dproj_in,
             dq_ref, dpba_ref, dal_ref, ddtb_ref):
        @pl.when(jnp.logical_and(pl.program_id(0) == 0, pl.program_id(1) == 0))
        def _():
            dal_ref[...] = jnp.zeros_like(dal_ref)
            ddtb_ref[...] = jnp.zeros_like(ddtb_ref)

        ea = jnp.exp(al_ref[...])
        incl0 = _chunk_masks()[0]
        tri = jnp.where(incl0, 1.0, 0.0)
        tri_up = jnp.where(_iota2((DN_CHUNK, DN_CHUNK), 1) >= _iota2((DN_CHUNK, DN_CHUNK), 0), 1.0, 0.0)
        lane = _iota2((DN_CHUNK, LANES), 1)
        last_col = _iota2((DN_CHUNK, 1), 0) == DN_CHUNK - 1

        _, strict, _ = _chunk_masks()
        gate_lane = jnp.logical_and(lane >= N_HEADS, lane < 2 * N_HEADS)

        def chunk_group(cg, carry):
            tiles, items = [], []
            for sub in range(LOCKSTEP_CHUNKS):
                rows = pl.ds(pl.multiple_of((cg * LOCKSTEP_CHUNKS + sub) * DN_CHUNK, DN_CHUNK), DN_CHUNK)
                pba = pba_ref[0, rows, :]
                beta_all, g_all = _gates(pba, ea, dtb_ref[...])
                gc = _dot_exact_lhs(tri, g_all)
                tiles.append((rows, pba, beta_all, g_all))
                for h in range(N_HEADS):
                    items.append((sub, rows, h, slice(h * HEAD_DIM, (h + 1) * HEAD_DIM),
                                  beta_all[:, h:h + 1], gc[:, N_HEADS + h:N_HEADS + h + 1]))
            decays = [_chunk_decay(gcol)[0] for _, _, _, _, _, gcol in items]
            egs = [jnp.exp(gcol) for _, _, _, _, _, gcol in items]
            qbs = [(q_ref[0, rows, cs] * QK_SCALE).astype(BF16) for _, rows, h, cs, _, _ in items]
            kfs = [k_ref[0, rows, cs].astype(BF16) for _, rows, h, cs, _, _ in items]
            kbs = [k_ref[0, rows, cs] * beta for _, rows, h, cs, beta, _ in items]
            kbbs = [kb.astype(BF16) for kb in kbs]
            tsps = [_split(ti_ref[0, h, rows, :]) for _, rows, h, cs, _, _ in items]
            drus = [_dot3(tsp, _split(du_ref[0, rows, cs]), TN)
                    for (_, rows, h, cs, _, _), tsp in zip(items, tsps)]
            drws = [_dot3(tsp, _split(dw_ref[0, rows, cs]), TN)
                    for (_, rows, h, cs, _, _), tsp in zip(items, tsps)]
            m_kks = [_dot(kbb, kf, NT) for kbb, kf in zip(kbbs, kfs)]
            a_qks = [_dot(qb, kf, NT) for qb, kf in zip(qbs, kfs)]
            dls = [-jnp.where(strict, _dot3(_split(dru), _split(u_ref[0, rows, cs]), NT)
                              + _dot3(_split(drw), _split(w_ref[0, rows, cs]), NT), 0.0)
                   for (_, rows, h, cs, _, _), dru, drw in zip(items, drus, drws)]
            dms = [(dl * dc).astype(BF16) for dl, dc in zip(dls, decays)]
            das = [(dqk_ref[0, h, rows, :] * dc).astype(BF16)
                   for (_, rows, h, cs, _, _), dc in zip(items, decays)]
            dkb_mm = [_dot(dm, kf, NN) for dm, kf in zip(dms, kfs)]
            dk_mm = [_dot(dm, kbb, TN) + _dot(da, qb, TN) for dm, kbb, da, qb in zip(dms, kbbs, das, qbs)]
            dqs_mm = [_dot(da, kf, NN) for da, kf in zip(das, kfs)]
            dgc_tiles = [dgs_ref[0, rows, :] for rows, _, _, _ in tiles]
            dbeta_tiles = [jnp.zeros((DN_CHUNK, LANES), F32) for _ in tiles]
            for n_it, (sub, rows, h, cs, beta, gcol) in enumerate(items):
                eg, dc = egs[n_it], decays[n_it]
                k = k_ref[0, rows, cs]
                q = q_ref[0, rows, cs] * QK_SCALE
                kb, dru, drw = kbs[n_it], drus[n_it], drws[n_it]
                ek = jnp.exp(gcol[DN_CHUNK - 1:DN_CHUNK, :] - gcol)
                e_mat = (dls[n_it] * m_kks[n_it] + dqk_ref[0, h, rows, :] * a_qks[n_it]) * dc
                dkb = drw * eg + dkb_mm[n_it]
                dg = (jnp.sum(drw * kb * eg, axis=-1, keepdims=True)
                      + jnp.sum(e_mat, axis=1, keepdims=True)
                      - _row_to_col(jnp.sum(e_mat, axis=0, keepdims=True)))
                dqd = dqd_ref[0, rows, cs]
                dg = dg + jnp.sum(dqd * q * eg, axis=-1, keepdims=True)
                dkd = dkd_ref[0, rows, cs]
                tk_ = jnp.sum(dkd * k * ek, axis=-1, keepdims=True)
                dg = dg - tk_ + jnp.where(last_col, jnp.sum(tk_, axis=0, keepdims=True), 0.0)
                dbeta = (jnp.sum(dkb * k, axis=-1, keepdims=True)
                         + jnp.sum(dru * v_ref[0, rows, cs], axis=-1, keepdims=True))
                dq_ref[0, rows, cs] = (dqs_mm[n_it] + dqd * eg) * QK_SCALE
                dq_ref[0, rows, pl.ds(HALF_W + h * HEAD_DIM, HEAD_DIM)] = dk_mm[n_it] + dkd * ek + dkb * beta
                dq_ref[0, rows, pl.ds(2 * HALF_W + h * HEAD_DIM, HEAD_DIM)] = dru * beta
                dgc_tiles[sub] = dgc_tiles[sub] + jnp.where(lane == N_HEADS + h, dg, 0.0)
                dbeta_tiles[sub] = dbeta_tiles[sub] + jnp.where(lane == h, dbeta, 0.0)
            for (rows, pba, beta_all, g_all), dgc_tile, dbeta_tile in zip(tiles, dgc_tiles, dbeta_tiles):
                dg_tile = _dot_exact_lhs(tri_up, dgc_tile)
                da_pre = dg_tile * (-ea) * _sigmoid(pba + dtb_ref[...])
                dal_ref[...] += jnp.sum(jnp.where(gate_lane, dg_tile * g_all, 0.0), axis=0, keepdims=True)
                ddtb_ref[...] += jnp.sum(jnp.where(gate_lane, da_pre, 0.0), axis=0, keepdims=True)
                dpba_ref[0, rows, :] = jnp.where(lane < N_HEADS, dbeta_tile * beta_all * (1.0 - beta_all),
                                                 jnp.where(gate_lane, da_pre, 0.0)).astype(BF16)
            return carry

        lax.fori_loop(0, n_in // LOCKSTEP_CHUNKS, chunk_group, 0)

    def seg(cblk):
        return pl.BlockSpec((1, rblk, HALF_W), lambda i, r: (i, r, cblk))

    vec = pl.BlockSpec((1, LANES), lambda i, r: (0, 0))
    wide = pl.BlockSpec((1, rblk, HALF_W), lambda i, r: (i, r, 0))
    sq = pl.BlockSpec((1, N_HEADS, rblk, DN_CHUNK), lambda i, r: (i, 0, r, 0))
    gates = pl.BlockSpec((1, rblk, LANES), lambda i, r: (i, r, 0))
    return _call(
        body, name=name, grid=(b, t // rblk),
        in_specs=[seg(0), seg(1), seg(2),
                  pl.BlockSpec((1, rblk, LANES), lambda i, r: (i, r, GATE_COL_BLOCK)), vec, vec,
                  sq, wide, wide, wide, wide, wide, wide, sq, gates, _ANY],
        out_specs=[pl.BlockSpec((1, rblk, 3 * HALF_W), lambda i, r: (i, r, 0)),
                   pl.BlockSpec((1, rblk, LANES), lambda i, r: (i, r, GATE_COL_BLOCK)), vec, vec],
        out_shape=[_sds((b, t, 3 * HALF_W), F32), _sds(dproj3.shape, BF16),
                   _sds((1, LANES), F32), _sds((1, LANES), F32)],
        input_output_aliases={15: 1},
    )(qkv, qkv, qkv, proj3, alog_row, dtb_row, tinv, u, w, du, dw, dqd, dkd, dqk, dgc_scan, dproj3)


def dn_out_fwd(o, proj, dn_norm, name):
    n = o.shape[0]
    tm = min(ROW_TILE, n)

    def body(o_ref, z_ref, g_ref, y_ref):
        for h in range(N_HEADS):
            cs = slice(h * HEAD_DIM, (h + 1) * HEAD_DIM)
            oh = o_ref[:, cs]
            z = z_ref[:, cs]
            y = oh * _rms_scale(oh) * g_ref[...]
            y_ref[:, cs] = (y * (z * _sigmoid(z))).astype(BF16)

    half = pl.BlockSpec((tm, HALF_W), lambda i: (i, 0))
    return _call(
        body, name=name, grid=(n // tm,),
        in_specs=[half, pl.BlockSpec((tm, HALF_W), lambda i: (i, 5)),
                  pl.BlockSpec((1, HEAD_DIM), lambda i: (0, 0))],
        out_specs=half, out_shape=_sds((n, HALF_W), BF16),
    )(o, proj, dn_norm)


def dn_out_bwd(dy, o, proj, dn_norm, dproj, name):
    n = o.shape[0]
    tm = min(ROW_TILE, n)

    def body(dy_ref, o_ref, z_ref, g_ref, dproj_in, do_ref, dz_ref, dg_ref):
        @pl.when(pl.program_id(0) == 0)
        def _():
            dg_ref[...] = jnp.zeros_like(dg_ref)

        g = g_ref[...]
        dg = jnp.zeros_like(g)
        for h in range(N_HEADS):
            cs = slice(h * HEAD_DIM, (h + 1) * HEAD_DIM)
            oh = o_ref[:, cs]
            z = z_ref[:, cs]
            d = dy_ref[:, cs]
            r = _rms_scale(oh)
            nh = oh * r
            sz = _sigmoid(z)
            dyn = d * (z * sz)
            dz_ref[:, cs] = (d * (nh * g) * (sz * (1.0 + z * (1.0 - sz)))).astype(BF16)
            dg = dg + jnp.sum(dyn * nh, axis=0, keepdims=True)
            dn = dyn * g
            do_ref[:, cs] = r * (dn - nh * jnp.mean(dn * nh, axis=-1, keepdims=True))
        dg_ref[...] += dg

    half = pl.BlockSpec((tm, HALF_W), lambda i: (i, 0))
    vec = pl.BlockSpec((1, HEAD_DIM), lambda i: (0, 0))
    return _call(
        body, name=name, grid=(n // tm,),
        in_specs=[half, half, pl.BlockSpec((tm, HALF_W), lambda i: (i, 5)), vec, _ANY],
        out_specs=[half, pl.BlockSpec((tm, HALF_W), lambda i: (i, 5)), vec],
        out_shape=[_sds((n, HALF_W), F32), _sds(dproj.shape, BF16), _sds((1, HEAD_DIM), F32)],
        input_output_aliases={4: 1},
    )(dy, o, proj, dn_norm, dproj)


def _adamw_math(w, g, m, v):
    m_new = ADAM_B1 * m + (1.0 - ADAM_B1) * g
    v_new = ADAM_B2 * v + (1.0 - ADAM_B2) * (g * g)
    m_hat = m_new / (1.0 - ADAM_B1 ** ADAM_STEP)
    v_hat = v_new / (1.0 - ADAM_B2 ** ADAM_STEP)
    delta = -ADAM_LR * (m_hat / (jnp.sqrt(v_hat) + ADAM_EPS) + ADAM_WD * w)
    return delta, m_new, v_new


def adamw(w, g, m, v, name):
    r, c = w.shape
    tr = r
    for cand in (256, 352):
        if r % cand == 0 and r > cand:
            tr = cand
            break

    def body(w_ref, g_ref, m_ref, v_ref, d_ref, mo_ref, vo_ref):
        d, mn, vn = _adamw_math(w_ref[...], g_ref[...], m_ref[...], v_ref[...])
        d_ref[...] = d
        mo_ref[...] = mn
        vo_ref[...] = vn

    spec = pl.BlockSpec((tr, c), lambda i: (i, 0))
    return _call(
        body, name=name, grid=(r // tr,),
        in_specs=[spec] * 4, out_specs=[spec] * 3, out_shape=[_sds((r, c), F32)] * 3,
    )(w, g, m, v)


def _place():
    return lax.axis_index("x"), lax.axis_index("y"), lax.axis_index("c")


def _other_chips(x, y):
    return [(1 - x, y), (x, 1 - y), (1 - x, 1 - y)]


_ANY = pl.BlockSpec(memory_space=pl.ANY)


def cast_place(w, shard_idx, name):
    r, cols = w.shape
    tr = r // 2

    def body(j_ref, w_ref, o_ref):
        o_ref[0] = w_ref[...].astype(BF16)

    return pl.pallas_call(
        body, name=name,
        grid_spec=pltpu.PrefetchScalarGridSpec(
            num_scalar_prefetch=1, grid=(r // tr,),
            in_specs=[pl.BlockSpec((tr, cols), lambda i, j: (i, 0))],
            out_specs=pl.BlockSpec((1, tr, cols), lambda i, j: (j[0], i, 0))),
        out_shape=_sds((N_SHARD, r, cols), BF16),
        compiler_params=pltpu.CompilerParams(dimension_semantics=("arbitrary",),
                                             vmem_limit_bytes=VMEM_LIMIT),
    )(shard_idx, w)


def all_gather_chips(bufs, small, name):
    n = len(bufs)

    def body(*refs):
        small_in = refs[n]
        outs, small_out = refs[n + 1:2 * n + 1], refs[2 * n + 1]
        send, recv, fsend, frecv, loc = refs[2 * n + 2:]
        x, y, c = _place()
        j = 2 * x + y
        chips = _other_chips(x, y)
        sib = (x, y, 1 - c)

        def half(a, blk, hc):
            rh = bufs[a].shape[1] // 2
            return outs[a].at[blk, pl.ds(hc * rh, rh), :]

        local = pltpu.make_async_copy(small_in, small_out.at[j], loc.at[0])
        local.start()
        sends = []
        for k, (px, py) in enumerate(chips):
            sends.append(pltpu.make_async_remote_copy(
                src_ref=small_in, dst_ref=small_out.at[j], send_sem=send.at[3 * n + k],
                recv_sem=recv.at[3 * n + k], device_id=(px, py, c), device_id_type=MESH))
            for a in range(n):
                sends.append(pltpu.make_async_remote_copy(
                    src_ref=half(a, j, c), dst_ref=half(a, j, c), send_sem=send.at[3 * a + k],
                    recv_sem=recv.at[3 * a + k], device_id=(px, py, c), device_id_type=MESH))
        for cp in sends:
            cp.start()
        forwards = []
        for k, (px, py) in enumerate(chips):
            blk = 2 * px + py
            for a in range(n):
                pltpu.make_async_remote_copy(
                    src_ref=half(a, blk, c), dst_ref=half(a, blk, c), send_sem=send.at[3 * a + k],
                    recv_sem=recv.at[3 * a + k], device_id=(px, py, c), device_id_type=MESH).wait_recv()
                fw = pltpu.make_async_remote_copy(
                    src_ref=half(a, blk, c), dst_ref=half(a, blk, c), send_sem=fsend.at[3 * a + k],
                    recv_sem=frecv.at[3 * a + k], device_id=sib, device_id_type=MESH)
                fw.start()
                forwards.append(fw)
        for k, (px, py) in enumerate(chips):
            blk = 2 * px + py
            pltpu.make_async_remote_copy(
                src_ref=small_in, dst_ref=small_out.at[blk], send_sem=send.at[3 * n + k],
                recv_sem=recv.at[3 * n + k], device_id=(px, py, c), device_id_type=MESH).wait_recv()
            for a in range(n):
                pltpu.make_async_remote_copy(
                    src_ref=half(a, blk, 1 - c), dst_ref=half(a, blk, 1 - c), send_sem=fsend.at[3 * a + k],
                    recv_sem=frecv.at[3 * a + k], device_id=sib, device_id_type=MESH).wait_recv()
        for cp in sends + forwards:
            cp.wait_send()
        local.wait()

    res = _call(
        body, name=name, in_specs=[_ANY] * (n + 1), out_specs=[_ANY] * (n + 1),
        out_shape=[_sds(b.shape, b.dtype) for b in bufs] + [_sds((N_SHARD,) + small.shape, small.dtype)],
        scratch=[pltpu.SemaphoreType.DMA((3 * n + 3,)), pltpu.SemaphoreType.DMA((3 * n + 3,)),
                 pltpu.SemaphoreType.DMA((3 * n,)), pltpu.SemaphoreType.DMA((3 * n,)),
                 pltpu.SemaphoreType.DMA((1,))],
        input_output_aliases={a: a for a in range(n)},
    )(*bufs, small)
    return res[:n], res[n]


def pair_exchange(arrs, name):
    n = len(arrs)

    def body(*refs):
        ins, outs = refs[:n], refs[n:2 * n]
        send, recv = refs[2 * n:]
        x, y, c = _place()
        cps = []
        for a in range(n):
            rh = arrs[a].shape[1] // 2
            cps.append(pltpu.make_async_remote_copy(
                src_ref=ins[a].at[:, pl.ds((1 - c) * rh, rh), :], dst_ref=outs[a],
                send_sem=send.at[a], recv_sem=recv.at[a], device_id=(x, y, 1 - c), device_id_type=MESH))
        for cp in cps:
            cp.start()
        for cp in cps:
            cp.wait_recv()
        for cp in cps:
            cp.wait_send()

    return _call(
        body, name=name, in_specs=[_ANY] * n, out_specs=[_ANY] * n,
        out_shape=[_sds((a.shape[0], a.shape[1] // 2, a.shape[2]), a.dtype) for a in arrs],
        scratch=[pltpu.SemaphoreType.DMA((n,)), pltpu.SemaphoreType.DMA((n,))],
    )(*arrs)


def pair_add(g, s, c_idx, name):
    nb, r, cols = g.shape
    rh = r // 2

    def body(c_ref, g_ref, s_ref, o_ref):
        o_ref[...] = (g_ref[...] + s_ref[...]).astype(BF16)

    return pl.pallas_call(
        body, name=name,
        grid_spec=pltpu.PrefetchScalarGridSpec(
            num_scalar_prefetch=1, grid=(nb,),
            in_specs=[pl.BlockSpec((1, rh, cols), lambda j, c: (j, c[0], 0)),
                      pl.BlockSpec((1, rh, cols), lambda j, c: (j, 0, 0))],
            out_specs=pl.BlockSpec((1, rh, cols), lambda j, c: (j, 0, 0))),
        out_shape=_sds((nb, rh, cols), BF16),
        compiler_params=pltpu.CompilerParams(dimension_semantics=("arbitrary",),
                                             vmem_limit_bytes=VMEM_LIMIT),
    )(c_idx, g, s)


def chip_exchange(arrs, name):
    n = len(arrs)

    def body(*refs):
        ins, outs = refs[:n], refs[n:2 * n]
        send, recv = refs[2 * n:]
        x, y, c = _place()
        j = 2 * x + y
        chips = _other_chips(x, y)
        sends = []
        for a in range(n):
            for k, (px, py) in enumerate(chips):
                sends.append(pltpu.make_async_remote_copy(
                    src_ref=ins[a].at[2 * px + py], dst_ref=outs[a].at[j], send_sem=send.at[3 * a + k],
                    recv_sem=recv.at[3 * a + k], device_id=(px, py, c), device_id_type=MESH))
        for cp in sends:
            cp.start()
        for a in range(n):
            for k, (px, py) in enumerate(chips):
                pltpu.make_async_remote_copy(
                    src_ref=ins[a].at[j], dst_ref=outs[a].at[2 * px + py], send_sem=send.at[3 * a + k],
                    recv_sem=recv.at[3 * a + k], device_id=(px, py, c), device_id_type=MESH).wait_recv()
        for cp in sends:
            cp.wait_send()

    return _call(
        body, name=name, in_specs=[_ANY] * n, out_specs=[_ANY] * n,
        out_shape=[_sds(a.shape, a.dtype) for a in arrs],
        scratch=[pltpu.SemaphoreType.DMA((3 * n,)), pltpu.SemaphoreType.DMA((3 * n,))],
    )(*arrs)


def sum_chips(r, p, shard_idx, name):
    nb, rh, cols = r.shape
    tr = rh // 2

    def body(j_ref, p_ref, *refs):
        o_ref = refs[nb]
        j = j_ref[0]
        acc = None
        for i in range(nb):
            term = jnp.where(j == i, p_ref[0], refs[i][0]).astype(F32)
            acc = term if acc is None else acc + term
        o_ref[...] = acc

    def slot(i):
        return pl.BlockSpec((1, tr, cols), lambda t, j: (jnp.where(j[0] == i, (i + 1) % nb, i), t, 0))

    return pl.pallas_call(
        body, name=name,
        grid_spec=pltpu.PrefetchScalarGridSpec(
            num_scalar_prefetch=1, grid=(rh // tr,),
            in_specs=[pl.BlockSpec((1, tr, cols), lambda t, j: (j[0], t, 0))] + [slot(i) for i in range(nb)],
            out_specs=pl.BlockSpec((tr, cols), lambda t, j: (t, 0))),
        out_shape=_sds((rh, cols), F32),
        compiler_params=pltpu.CompilerParams(dimension_semantics=("arbitrary",),
                                             vmem_limit_bytes=VMEM_LIMIT),
    )(shard_idx, p, *([r] * nb))


def pair_swap(arrs, name):
    n = len(arrs)

    def body(*refs):
        ins, outs = refs[:n], refs[n:2 * n]
        send, recv = refs[2 * n:]
        x, y, c = _place()
        cps = [pltpu.make_async_remote_copy(
            src_ref=ins[a], dst_ref=outs[a], send_sem=send.at[a], recv_sem=recv.at[a],
            device_id=(x, y, 1 - c), device_id_type=MESH) for a in range(n)]
        for cp in cps:
            cp.start()
        for cp in cps:
            cp.wait_recv()
        for cp in cps:
            cp.wait_send()

    return _call(
        body, name=name, in_specs=[_ANY] * n, out_specs=[_ANY] * n,
        out_shape=[_sds(a.shape, a.dtype) for a in arrs],
        scratch=[pltpu.SemaphoreType.DMA((n,)), pltpu.SemaphoreType.DMA((n,))],
    )(*arrs)


def adamw_pair(w, g_mine, g_sib, m, v, c_idx, name):
    r, cols = w.shape
    rh = r // 2
    tr = rh // 2
    nh = rh // tr

    def body(c_ref, w_ref, gm_ref, gs_ref, m_ref, v_ref, g_ref, d_ref, mo_ref, vo_ref):
        mine = (pl.program_id(0) // nh) == c_ref[0]
        g = jnp.where(mine, gm_ref[...], gs_ref[...])
        d, mn, vn = _adamw_math(w_ref[...], g, m_ref[...], v_ref[...])
        g_ref[...] = g
        d_ref[...] = d
        mo_ref[...] = mn
        vo_ref[...] = vn

    full = pl.BlockSpec((tr, cols), lambda i, c: (i, 0))
    part = pl.BlockSpec((tr, cols), lambda i, c: (i % nh, 0))
    return pl.pallas_call(
        body, name=name,
        grid_spec=pltpu.PrefetchScalarGridSpec(
            num_scalar_prefetch=1, grid=(r // tr,),
            in_specs=[full, part, part, full, full], out_specs=[full] * 4),
        out_shape=[_sds((r, cols), F32)] * 4,
        compiler_params=pltpu.CompilerParams(dimension_semantics=("arbitrary",),
                                             vmem_limit_bytes=VMEM_LIMIT),
    )(c_idx, w, g_mine, g_sib, m, v)


N_DEV = 8


def all_reduce_small(pack, name):
    r, cols = pack.shape

    def body(in_ref, out_ref, buf, send, recv):
        x, y, c = _place()
        me = 4 * x + 2 * y + c
        buf[me] = in_ref[...]
        peers = []
        for k in range(1, N_DEV):
            fx, fy, fc = (k >> 2) & 1, (k >> 1) & 1, k & 1
            peers.append((1 - x if fx else x, 1 - y if fy else y, 1 - c if fc else c))
        sends = [pltpu.make_async_remote_copy(
            src_ref=in_ref, dst_ref=buf.at[me], send_sem=send.at[k], recv_sem=recv.at[k],
            device_id=p, device_id_type=MESH) for k, p in enumerate(peers)]
        for cp in sends:
            cp.start()
        for k, (px, py, pc) in enumerate(peers):
            pltpu.make_async_remote_copy(
                src_ref=in_ref, dst_ref=buf.at[4 * px + 2 * py + pc], send_sem=send.at[k],
                recv_sem=recv.at[k], device_id=(px, py, pc), device_id_type=MESH).wait_recv()
        for cp in sends:
            cp.wait_send()
        acc = buf[0] + buf[1]
        for i in range(2, N_DEV):
            acc = acc + buf[i]
        out_ref[...] = acc

    vm = pl.BlockSpec(memory_space=pltpu.VMEM)
    return _call(
        body, name=name, in_specs=[vm], out_specs=vm, out_shape=_sds((r, cols), F32),
        scratch=[pltpu.VMEM((N_DEV, r, cols), F32), pltpu.SemaphoreType.DMA((N_DEV - 1,)),
                 pltpu.SemaphoreType.DMA((N_DEV - 1,))],
    )(pack)


SMALL_NAMES = ("ffn1_norm", "mix_norm", "ffn2_norm", "final_norm", "sg_ln_g", "sg_ln_b",
               "dn_norm", "a_log", "dt_bias", "sg_b", "sg_w", "conv_w")


def _to_rows(a):
    flat = a.reshape(-1)
    pad = (-flat.shape[0]) % LANES
    if pad:
        flat = jnp.pad(flat, (0, pad))
    return flat.reshape(-1, LANES)


def _pack_small(parts):
    rows = [_to_rows(parts[k]) for k in SMALL_NAMES]
    pack = jnp.concatenate(rows, axis=0)
    pad = (-pack.shape[0]) % 8
    if pad:
        pack = jnp.pad(pack, ((0, pad), (0, 0)))
    return pack


def _unpack_small(pack, shapes):
    out, r0 = {}, 0
    for k in SMALL_NAMES:
        size = 1
        for s in shapes[k]:
            size *= s
        nrows = -(-size // LANES)
        out[k] = pack[r0:r0 + nrows].reshape(-1)[:size].reshape(shapes[k])
        r0 += nrows
    return out


def kernel(x, ffn1_norm, ffn1_w_gate, ffn1_w_up, ffn1_w_down, mix_norm, w_in, conv_w, a_log, dt_bias, dn_norm, sg_ln_g, sg_ln_b, sg_w, sg_b, w_out, ffn2_norm, ffn2_w_gate, ffn2_w_up, ffn2_w_down, final_norm, loss_target, m_ffn1_norm, m_ffn1_w_gate, m_ffn1_w_up, m_ffn1_w_down, m_mix_norm, m_w_in, m_conv_w, m_a_log, m_dt_bias, m_dn_norm, m_sg_ln_g, m_sg_ln_b, m_sg_w, m_sg_b, m_w_out, m_ffn2_norm, m_ffn2_w_gate, m_ffn2_w_up, m_ffn2_w_down, m_final_norm, v_ffn1_norm, v_ffn1_w_gate, v_ffn1_w_up, v_ffn1_w_down, v_mix_norm, v_w_in, v_conv_w, v_a_log, v_dt_bias, v_dn_norm, v_sg_ln_g, v_sg_ln_b, v_sg_w, v_sg_b, v_w_out, v_ffn2_norm, v_ffn2_w_gate, v_ffn2_w_up, v_ffn2_w_down, v_final_norm):
    bsz, t_len, d = x.shape
    n = bsz * t_len
    xy, yy, cc = _place()
    shard = 2 * xy + yy

    big_names = ["ffn1_w_gate", "ffn1_w_up", "ffn1_w_down", "w_in", "w_out",
                 "ffn2_w_gate", "ffn2_w_up", "ffn2_w_down"]
    big_w = dict(ffn1_w_gate=ffn1_w_gate, ffn1_w_up=ffn1_w_up, ffn1_w_down=ffn1_w_down, w_in=w_in,
                 w_out=w_out, ffn2_w_gate=ffn2_w_gate, ffn2_w_up=ffn2_w_up, ffn2_w_down=ffn2_w_down)
    big_m = dict(ffn1_w_gate=m_ffn1_w_gate, ffn1_w_up=m_ffn1_w_up, ffn1_w_down=m_ffn1_w_down, w_in=m_w_in,
                 w_out=m_w_out, ffn2_w_gate=m_ffn2_w_gate, ffn2_w_up=m_ffn2_w_up, ffn2_w_down=m_ffn2_w_down)
    big_v = dict(ffn1_w_gate=v_ffn1_w_gate, ffn1_w_up=v_ffn1_w_up, ffn1_w_down=v_ffn1_w_down, w_in=v_w_in,
                 w_out=v_w_out, ffn2_w_gate=v_ffn2_w_gate, ffn2_w_up=v_ffn2_w_up, ffn2_w_down=v_ffn2_w_down)
    shard_idx = jnp.reshape(shard, (1,)).astype(jnp.int32)
    c_idx = jnp.reshape(cc, (1,)).astype(jnp.int32)
    transposed = ("ffn1_w_gate", "ffn1_w_up", "ffn2_w_gate", "ffn2_w_up")

    def as2d(a, k):
        return a[0].T if k in transposed else a[0]

    def from2d(a, k):
        return a.T[None] if k in transposed else a[None]

    placed = [cast_place(as2d(big_w[k], k), shard_idx, name="cast_" + k) for k in big_names]
    gathered, conv_g = all_gather_chips(placed, conv_w[0], name="gather_weights")
    gw = dict(zip(big_names, gathered))
    conv_full = conv_g.transpose(1, 0, 2).reshape(CONV_K, 3 * HALF_W)
    w_in_full = gw["w_in"].transpose(1, 0, 2).reshape(d, IN_COLS)
    w_in_full = jnp.pad(w_in_full, ((0, 0), (0, PROJ_W - IN_COLS)))
    w_out_full = gw["w_out"].reshape(2 * HALF_W, d)

    x0 = x.reshape(n, d)
    x1, h1, gate1, up1 = ffn_fwd(x0, ffn1_norm, gw["ffn1_w_gate"], gw["ffn1_w_up"],
                                 gw["ffn1_w_down"], name="ffn1_fwd")
    proj, h2 = in_proj_fwd(x1, mix_norm, w_in_full, name="in_proj_fwd")
    proj3 = proj.reshape(bsz, t_len, PROJ_W)
    bias_tile = jnp.repeat(sg_b[0].T, SG_GROUP_DIM, axis=1)
    sg_out = sg_fwd(proj, sg_ln_g, sg_ln_b, sg_w[0], bias_tile, name="sg_fwd")
    qkv = dn_conv_fwd(proj3, conv_full, name="dn_conv_fwd")
    alog_row = jnp.zeros((1, LANES), F32).at[0, N_HEADS:2 * N_HEADS].set(a_log[0])
    dtb_row = jnp.zeros((1, LANES), F32).at[0, N_HEADS:2 * N_HEADS].set(dt_bias[0])
    u_wy, w_wy, q_dec, k_dec, qk, tinv, gc = dn_chunk_fwd(qkv, proj3, alog_row, dtb_row,
                                                           name="dn_chunk_fwd")
    o, s_in = dn_scan_fwd(u_wy, w_wy, q_dec, k_dec, qk, gc, name="dn_scan_fwd")
    dn_out = dn_out_fwd(o.reshape(n, HALF_W), proj, dn_norm, name="dn_out_fwd")
    x2 = out_proj_fwd(x1, sg_out, dn_out, w_out_full, name="out_proj_fwd")
    x3, h3, gate2, up2 = ffn_fwd(x2, ffn2_norm, gw["ffn2_w_gate"], gw["ffn2_w_up"],
                                 gw["ffn2_w_down"], name="ffn2_fwd")
    dx3, d_final_norm, loss_tile = final_loss(x3, final_norm.reshape(1, d),
                                              loss_target.reshape(n, d), name="final_loss")
    loss = lax.psum(loss_tile[0, 0], ("x", "y", "c"))

    dx2, dgate2, dup2, act2, dyh2, d_ffn2_norm = ffn_bwd_act(
        dx3, x2, ffn2_norm, gate2, up2, gw["ffn2_w_gate"], gw["ffn2_w_up"], gw["ffn2_w_down"],
        name="ffn2_bwd_act")
    g_big = {}
    g_big["ffn2_w_gate"], g_big["ffn2_w_up"], g_big["ffn2_w_down"] = ffn_bwd_w(
        h3, dyh2, dgate2, dup2, act2, name="ffn2_bwd_w")

    d_sg, d_dn, dx2b = out_proj_bwd_x(dx2, w_out_full, name="out_proj_bwd_x")
    g_w_out = jnp.concatenate([matmul_tn(sg_out, dx2b, d, name="w_out_grad_sg"),
                               matmul_tn(dn_out, dx2b, d, name="w_out_grad_dn")], axis=0)
    g_big["w_out"] = g_w_out.reshape(N_SHARD, (2 * HALF_W) // N_SHARD, d)

    d_proj, d_sg_w, d_bias_tile, d_ln_g, d_ln_b = sg_bwd(d_sg, proj, sg_ln_g, sg_ln_b, sg_w[0],
                                                         bias_tile, name="sg_bwd")
    d_o, d_proj, d_dn_norm = dn_out_bwd(d_dn, o.reshape(n, HALF_W), proj, dn_norm, d_proj,
                                        name="dn_out_bwd")
    du, dw, dqd, dkd, dqk, dgc_scan = dn_scan_bwd(d_o.reshape(bsz, t_len, HALF_W), u_wy, w_wy, q_dec,
                                                  k_dec, qk, gc, s_in, name="dn_scan_bwd")
    d_qkv, d_proj3, d_alog_row, d_dtb_row = dn_chunk_bwd(
        qkv, proj3, alog_row, dtb_row, tinv, u_wy, w_wy, du, dw, dqd, dkd, dqk, dgc_scan,
        d_proj.reshape(bsz, t_len, PROJ_W), name="dn_chunk_bwd")
    d_proj3, d_conv = dn_conv_bwd(d_qkv, proj3, conv_full, d_proj3, name="dn_conv_bwd")
    d_proj = d_proj3.reshape(n, PROJ_W)
    dx1, d_mix_norm = in_proj_bwd_x(d_proj, w_in_full, x1, mix_norm, dx2, name="in_proj_bwd_x")
    g_w_in = matmul_tn(h2, d_proj, 640, name="w_in_grad")[:, :IN_COLS]
    g_big["w_in"] = g_w_in.reshape(d, N_SHARD, IN_COLS // N_SHARD).transpose(1, 0, 2)

    dx0, dgate1, dup1, act1, dyh1, d_ffn1_norm = ffn_bwd_act(
        dx1, x0, ffn1_norm, gate1, up1, gw["ffn1_w_gate"], gw["ffn1_w_up"], gw["ffn1_w_down"],
        name="ffn1_bwd_act")
    g_big["ffn1_w_gate"], g_big["ffn1_w_up"], g_big["ffn1_w_down"] = ffn_bwd_w(
        h1, dyh1, dgate1, dup1, act1, name="ffn1_bwd_w")
    grad_x = dx0.reshape(bsz, t_len, d)

    g_list = [g_big[k] for k in big_names]
    from_sibling = pair_exchange(g_list, name="grad_pair_exchange")
    pair_sums = [pair_add(g, s, c_idx, name="grad_pair_add_" + k)
                 for k, g, s in zip(big_names, g_list, from_sibling)]
    from_chips = chip_exchange(pair_sums, name="grad_chip_exchange")
    halves = [sum_chips(r, p, shard_idx, name="grad_chip_sum_" + k)
              for k, r, p in zip(big_names, from_chips, pair_sums)]
    sib_halves = pair_swap(halves, name="grad_pair_swap")
    outs = {}
    for k, g_mine, g_sib in zip(big_names, halves, sib_halves):
        res = adamw_pair(as2d(big_w[k], k), g_mine, g_sib, as2d(big_m[k], k), as2d(big_v[k], k), c_idx,
                         name="adamw_" + k)
        outs[k] = tuple(from2d(a, k) for a in res)

    small_w = dict(ffn1_norm=ffn1_norm, mix_norm=mix_norm, ffn2_norm=ffn2_norm, final_norm=final_norm,
                   sg_ln_g=sg_ln_g, sg_ln_b=sg_ln_b, dn_norm=dn_norm, a_log=a_log, dt_bias=dt_bias,
                   sg_b=sg_b, sg_w=sg_w)
    small_m = dict(ffn1_norm=m_ffn1_norm, mix_norm=m_mix_norm, ffn2_norm=m_ffn2_norm,
                   final_norm=m_final_norm, sg_ln_g=m_sg_ln_g, sg_ln_b=m_sg_ln_b, dn_norm=m_dn_norm,
                   a_log=m_a_log, dt_bias=m_dt_bias, sg_b=m_sg_b, sg_w=m_sg_w)
    small_v = dict(ffn1_norm=v_ffn1_norm, mix_norm=v_mix_norm, ffn2_norm=v_ffn2_norm,
                   final_norm=v_final_norm, sg_ln_g=v_sg_ln_g, sg_ln_b=v_sg_ln_b, dn_norm=v_dn_norm,
                   a_log=v_a_log, dt_bias=v_dt_bias, sg_b=v_sg_b, sg_w=v_sg_w)
    shapes = {k: small_w[k].shape for k in small_w}
    shapes["conv_w"] = (CONV_K, 3 * HALF_W)
    d_sg_b = d_bias_tile.reshape(SG_CHUNK, SG_GROUPS, SG_GROUP_DIM).sum(axis=-1).T
    small_g = dict(ffn1_norm=d_ffn1_norm, mix_norm=d_mix_norm, ffn2_norm=d_ffn2_norm,
                   final_norm=d_final_norm, sg_ln_g=d_ln_g, sg_ln_b=d_ln_b, dn_norm=d_dn_norm,
                   a_log=d_alog_row[:, N_HEADS:2 * N_HEADS], dt_bias=d_dtb_row[:, N_HEADS:2 * N_HEADS],
                   sg_b=d_sg_b, sg_w=d_sg_w, conv_w=d_conv)
    g_pack = all_reduce_small(_pack_small(small_g), name="small_all_reduce")
    g_small = _unpack_small(g_pack, shapes)
    cw = 3 * HALF_W // N_SHARD
    g_conv = lax.dynamic_slice_in_dim(g_small["conv_w"], shard * cw, cw, axis=1)
    zero_conv = jnp.zeros((CONV_K, 3 * HALF_W), F32)

    def packed(src, conv):
        parts = dict(src)
        parts["conv_w"] = lax.dynamic_update_slice_in_dim(zero_conv, conv[0], shard * cw, axis=1)
        return _pack_small(parts)

    d_pack, m_pack, v_pack = adamw(packed(small_w, conv_w), g_pack, packed(small_m, m_conv_w),
                                   packed(small_v, v_conv_w), name="adamw_small")
    d_small = _unpack_small(d_pack, shapes)
    m_small = _unpack_small(m_pack, shapes)
    v_small = _unpack_small(v_pack, shapes)

    def conv_block(full_arr):
        return lax.dynamic_slice_in_dim(full_arr, shard * cw, cw, axis=1)[None]

    for k in small_w:
        outs[k] = (g_small[k].reshape(small_w[k].shape), d_small[k], m_small[k], v_small[k])
    outs["conv_w"] = (g_conv[None], conv_block(d_small["conv_w"]), conv_block(m_small["conv_w"]),
                      conv_block(v_small["conv_w"]))

    order = ["ffn1_norm", "ffn1_w_gate", "ffn1_w_up", "ffn1_w_down", "mix_norm", "w_in", "conv_w",
             "a_log", "dt_bias", "dn_norm", "sg_ln_g", "sg_ln_b", "sg_w", "sg_b", "w_out", "ffn2_norm",
             "ffn2_w_gate", "ffn2_w_up", "ffn2_w_down", "final_norm"]
    return (loss, grad_x, *[outs[k][0] for k in order], *[outs[k][1] for k in order],
            *[outs[k][2] for k in order], *[outs[k][3] for k in order])
```

```python
import functools

import jax
import jax.numpy as jnp
from jax import lax
from jax.experimental import pallas as pl
from jax.experimental.pallas import tpu as pltpu

F32 = jnp.float32
BF16 = jnp.bfloat16
EPS = 1e-6

D_MODEL = 1024
N_SHARD = 4
HEAD_DIM = 128
N_HEADS = 4
DN_CHUNK = 64
SG_CHUNK = 128
SG_GROUPS = 8
SG_GROUP_DIM = 64
HALF_W = 512
PROJ_W = 3200
IN_COLS = 3080
GATE_COL_BLOCK = 24
QK_SCALE = HEAD_DIM ** -0.5
LANES = 128

ADAM_LR = 0.001
ADAM_B1 = 0.9
ADAM_B2 = 0.999
ADAM_EPS = 1e-08
ADAM_WD = 0.01
ADAM_STEP = 10

VMEM_LIMIT = 56 * 1024 * 1024
ROW_TILE = 512

NN = ((1,), (0,))
NT = ((1,), (1,))
TN = ((0,), (0,))
MESH = pl.DeviceIdType.MESH


def _dot(a, b, dims):
    return lax.dot_general(a, b, (dims, ((), ())), preferred_element_type=F32)


def _bdot(a, b, dims):
    return _dot(a.astype(BF16), b.astype(BF16), dims)


def _split(a):
    hi = a.astype(BF16)
    lo = (a - hi.astype(F32)).astype(BF16)
    return hi, lo


def _dot3(a, b, dims=NN):
    return _dot(a[0], b[0], dims) + (_dot(a[0], b[1], dims) + _dot(a[1], b[0], dims))


def _dot_exact_lhs(a, b):
    ab = a.astype(BF16)
    b1 = b.astype(BF16)
    r1 = b - b1.astype(F32)
    b2 = r1.astype(BF16)
    b3 = (r1 - b2.astype(F32)).astype(BF16)
    return _dot(ab, b1, NN) + (_dot(ab, b2, NN) + _dot(ab, b3, NN))


def _call(body, *, name, out_shape, in_specs, out_specs, grid=(), scratch=(), **kw):
    params = dict(vmem_limit_bytes=VMEM_LIMIT)
    if grid:
        params["dimension_semantics"] = ("arbitrary",) * len(grid)
    return pl.pallas_call(
        body, name=name, grid=grid, in_specs=in_specs, out_specs=out_specs,
        out_shape=out_shape, scratch_shapes=list(scratch),
        compiler_params=pltpu.CompilerParams(**params), **kw)


def _sds(shape, dtype):
    return jax.ShapeDtypeStruct(tuple(shape), dtype)


def _resident(shape):
    zeros = (0,) * len(shape)
    return pl.BlockSpec(tuple(shape), lambda *_: zeros, pipeline_mode=pl.Buffered(1))


def _sigmoid(x):
    return jax.nn.sigmoid(x)


def _softplus(x):
    return jnp.maximum(x, 0.0) + jnp.log(1.0 + jnp.exp(-jnp.abs(x)))


_GELU_C = 0.7978845608028654
_GELU_A = 0.044715


def _gelu(x):
    t = jnp.tanh(_GELU_C * (x + _GELU_A * x * x * x))
    return 0.5 * x * (1.0 + t)


def _gelu_grad(x):
    t = jnp.tanh(_GELU_C * (x + _GELU_A * x * x * x))
    return 0.5 * (1.0 + t) + 0.5 * x * (1.0 - t * t) * _GELU_C * (1.0 + 3.0 * _GELU_A * x * x)


def _silu_grad(x):
    s = _sigmoid(x)
    return s * (1.0 + x * (1.0 - s))


def _rms_scale(xv):
    return lax.rsqrt(jnp.mean(xv * xv, axis=-1, keepdims=True) + EPS)


def _rms_bwd(dh, xv, g):
    r = _rms_scale(xv)
    xn = xv * r
    dg = jnp.sum(dh * xn, axis=0, keepdims=True)
    dxn = dh * g
    dx = r * (dxn - xn * jnp.mean(dxn * xn, axis=-1, keepdims=True))
    return dx, dg


def _iota2(shape, dim):
    return lax.broadcasted_iota(jnp.int32, shape, dim)


def _col_to_row(col):
    n = col.shape[0]
    eye = _iota2((n, n), 0) == _iota2((n, n), 1)
    return jnp.sum(jnp.where(eye, col, 0.0), axis=0, keepdims=True)


def _row_to_col(row):
    n = row.shape[1]
    eye = _iota2((n, n), 0) == _iota2((n, n), 1)
    return jnp.sum(jnp.where(eye, row, 0.0), axis=1, keepdims=True)


def ffn_fwd(x, gnorm, wg, wu, wd, name):
    n, d = x.shape
    nb, fb, _ = wg.shape
    tm = min(ROW_TILE, n)

    def body(x_ref, g_ref, wg_ref, wu_ref, wd_ref, xo_ref, h_ref, gate_ref, up_ref, acc_ref):
        xv = x_ref[...]
        h = (xv * _rms_scale(xv) * g_ref[...]).astype(BF16)
        h_ref[...] = h
        for j in range(nb):
            gate = _dot(h, wg_ref[j], NT)
            up = _dot(h, wu_ref[j], NT)
            gate_ref[j] = gate.astype(BF16)
            up_ref[j] = up.astype(BF16)
            part = _dot((gate * _sigmoid(gate) * up).astype(BF16), wd_ref[j], NN)
            if j == 0:
                acc_ref[...] = part
            else:
                acc_ref[...] += part
        xo_ref[...] = xv + 0.5 * acc_ref[...]

    row = pl.BlockSpec((tm, d), lambda i: (i, 0))
    blk = pl.BlockSpec((nb, tm, fb), lambda i: (0, i, 0))
    return _call(
        body, name=name, grid=(n // tm,),
        in_specs=[row, pl.BlockSpec((1, d), lambda i: (0, 0))] + [_resident((nb, fb, d))] * 3,
        out_specs=[row, row, blk, blk],
        out_shape=[_sds((n, d), F32), _sds((n, d), BF16),
                   _sds((nb, n, fb), BF16), _sds((nb, n, fb), BF16)],
        scratch=[pltpu.VMEM((tm, d), F32)],
    )(x, gnorm, wg, wu, wd)


def ffn_bwd_act(dy, x, gnorm, gate, up, wg, wu, wd, name):
    n, d = x.shape
    nb, fb, _ = wg.shape
    tm = min(ROW_TILE // 2, n)

    def body(dy_ref, x_ref, g_ref, gate_ref, up_ref, wg_ref, wu_ref, wd_ref,
             dx_ref, dgate_ref, dup_ref, act_ref, dyh_ref, dg_ref, acc_ref):
        @pl.when(pl.program_id(0) == 0)
        def _():
            dg_ref[...] = jnp.zeros_like(dg_ref)

        dyh = (0.5 * dy_ref[...]).astype(BF16)
        dyh_ref[...] = dyh
        for j in range(nb):
            dact = _dot(dyh, wd_ref[j], NT)
            gt = gate_ref[j].astype(F32)
            u = up_ref[j].astype(F32)
            s = _sigmoid(gt)
            silu = gt * s
            dup = (dact * silu).astype(BF16)
            dgate = (dact * u * (s * (1.0 + gt * (1.0 - s)))).astype(BF16)
            dup_ref[j] = dup
            dgate_ref[j] = dgate
            act_ref[j] = (silu * u).astype(BF16)
            part = _dot(dgate, wg_ref[j], NN) + _dot(dup, wu_ref[j], NN)
            if j == 0:
                acc_ref[...] = part
            else:
                acc_ref[...] += part
        dxn, dg = _rms_bwd(acc_ref[...], x_ref[...], g_ref[...])
        dx_ref[...] = dy_ref[...] + dxn
        dg_ref[...] += dg

    row = pl.BlockSpec((tm, d), lambda i: (i, 0))
    blk = pl.BlockSpec((nb, tm, fb), lambda i: (0, i, 0))
    vec = pl.BlockSpec((1, d), lambda i: (0, 0))
    wblk = _resident((nb, fb, d))
    return _call(
        body, name=name, grid=(n // tm,),
        in_specs=[row, row, vec, blk, blk, wblk, wblk, wblk],
        out_specs=[row, blk, blk, blk, row, vec],
        out_shape=[_sds((n, d), F32), _sds((nb, n, fb), BF16), _sds((nb, n, fb), BF16),
                   _sds((nb, n, fb), BF16), _sds((n, d), BF16), _sds((1, d), F32)],
        scratch=[pltpu.VMEM((tm, d), F32)],
    )(dy, x, gnorm, gate, up, wg, wu, wd)


def ffn_bwd_w(h, dyh, dgate, dup, act, name):
    n, d = h.shape
    nb, _, fb = dgate.shape
    tk = min(2 * ROW_TILE, n)

    def body(h_ref, dyh_ref, dgate_ref, dup_ref, act_ref, dwg_ref, dwu_ref, dwd_ref):
        @pl.when(pl.program_id(1) == 0)
        def _():
            dwg_ref[...] = jnp.zeros_like(dwg_ref)
            dwu_ref[...] = jnp.zeros_like(dwu_ref)
            dwd_ref[...] = jnp.zeros_like(dwd_ref)

        hv = h_ref[...]
        dwg_ref[0] += _dot(dgate_ref[0], hv, TN)
        dwu_ref[0] += _dot(dup_ref[0], hv, TN)
        dwd_ref[0] += _dot(act_ref[0], dyh_ref[...], TN)

    row = pl.BlockSpec((tk, d), lambda j, k: (k, 0))
    blk = pl.BlockSpec((1, tk, fb), lambda j, k: (j, k, 0))
    return _call(
        body, name=name, grid=(nb, n // tk),
        in_specs=[row, row, blk, blk, blk],
        out_specs=[pl.BlockSpec((1, fb, d), lambda j, k: (j, 0, 0))] * 3,
        out_shape=[_sds((nb, fb, d), F32)] * 3,
    )(h, dyh, dgate, dup, act)


def final_loss(x, gnorm, target, name):
    n, d = x.shape
    tm = min(ROW_TILE, n)

    def body(x_ref, g_ref, t_ref, dx_ref, dg_ref, loss_ref):
        @pl.when(pl.program_id(0) == 0)
        def _():
            dg_ref[...] = jnp.zeros_like(dg_ref)
            loss_ref[...] = jnp.zeros_like(loss_ref)

        xv = x_ref[...]
        y = xv * _rms_scale(xv) * g_ref[...]
        err = y - t_ref[...]
        part = 0.5 * jnp.sum(jnp.mean(err * err, axis=-1, keepdims=True), axis=0, keepdims=True)
        loss_ref[...] += jnp.broadcast_to(part, loss_ref.shape)
        dx, dg = _rms_bwd(err * (1.0 / d), xv, g_ref[...])
        dx_ref[...] = dx
        dg_ref[...] += dg

    row = pl.BlockSpec((tm, d), lambda i: (i, 0))
    vec = pl.BlockSpec((1, d), lambda i: (0, 0))
    return _call(
        body, name=name, grid=(n // tm,),
        in_specs=[row, vec, row],
        out_specs=[row, vec, pl.BlockSpec((1, LANES), lambda i: (0, 0))],
        out_shape=[_sds((n, d), F32), _sds((1, d), F32), _sds((1, LANES), F32)],
    )(x, gnorm, target)


def in_proj_fwd(x, gnorm, w, name):
    n, d = x.shape
    cols = w.shape[1]
    tm = min(ROW_TILE, n)
    tn = 640

    def body(x_ref, g_ref, w_ref, p_ref, h_ref):
        xv = x_ref[...]
        h = (xv * _rms_scale(xv) * g_ref[...]).astype(BF16)
        h_ref[...] = h
        for c0 in range(0, cols, tn):
            p_ref[:, c0:c0 + tn] = _dot(h, w_ref[:, c0:c0 + tn], NN)

    return _call(
        body, name=name, grid=(n // tm,),
        in_specs=[pl.BlockSpec((tm, d), lambda i: (i, 0)),
                  pl.BlockSpec((1, d), lambda i: (0, 0)), _resident((d, cols))],
        out_specs=[pl.BlockSpec((tm, cols), lambda i: (i, 0)),
                   pl.BlockSpec((tm, d), lambda i: (i, 0))],
        out_shape=[_sds((n, cols), F32), _sds((n, d), BF16)],
    )(x, gnorm, w)


def in_proj_bwd_x(dproj, w, x, gnorm, dres, name):
    n, d = x.shape
    cols = w.shape[1]
    tm = min(ROW_TILE, n)

    def body(dp_ref, w_ref, x_ref, g_ref, dr_ref, dx_ref, dg_ref):
        @pl.when(pl.program_id(0) == 0)
        def _():
            dg_ref[...] = jnp.zeros_like(dg_ref)

        dh = _dot(dp_ref[...], w_ref[...], NT)
        dxn, dg = _rms_bwd(dh, x_ref[...], g_ref[...])
        dx_ref[...] = dr_ref[...] + dxn
        dg_ref[...] += dg

    row = pl.BlockSpec((tm, d), lambda i: (i, 0))
    vec = pl.BlockSpec((1, d), lambda i: (0, 0))
    return _call(
        body, name=name, grid=(n // tm,),
        in_specs=[pl.BlockSpec((tm, cols), lambda i: (i, 0)),
                  _resident((d, cols)), row, vec, row],
        out_specs=[row, vec],
        out_shape=[_sds((n, d), F32), _sds((1, d), F32)],
    )(dproj, w, x, gnorm, dres)


def matmul_tn(a, b, tn, name):
    n, ka = a.shape
    cb = b.shape[1]
    tk = min(ROW_TILE, n)

    def body(a_ref, b_ref, o_ref):
        @pl.when(pl.program_id(0) == 0)
        def _():
            o_ref[...] = jnp.zeros_like(o_ref)

        av = a_ref[...]
        for c0 in range(0, cb, tn):
            o_ref[:, c0:c0 + tn] += _dot(av, b_ref[:, c0:c0 + tn], TN)

    return _call(
        body, name=name, grid=(n // tk,),
        in_specs=[pl.BlockSpec((tk, ka), lambda k: (k, 0)),
                  pl.BlockSpec((tk, cb), lambda k: (k, 0))],
        out_specs=pl.BlockSpec((ka, cb), lambda k: (0, 0)),
        out_shape=_sds((ka, cb), F32),
    )(a, b)


def out_proj_fwd(x, sg_out, dn_out, w, name):
    n, d = x.shape
    tm = min(ROW_TILE, n)

    def body(x_ref, a_ref, b_ref, w_ref, o_ref):
        o_ref[...] = (x_ref[...] + _dot(a_ref[...], w_ref[0:HALF_W, :], NN)
                      + _dot(b_ref[...], w_ref[HALF_W:2 * HALF_W, :], NN))

    row = pl.BlockSpec((tm, d), lambda i: (i, 0))
    half = pl.BlockSpec((tm, HALF_W), lambda i: (i, 0))
    return _call(
        body, name=name, grid=(n // tm,),
        in_specs=[row, half, half, pl.BlockSpec((2 * HALF_W, d), lambda i: (0, 0))],
        out_specs=row, out_shape=_sds((n, d), F32),
    )(x, sg_out, dn_out, w)


def out_proj_bwd_x(dy, w, name):
    n, d = dy.shape
    tm = min(ROW_TILE, n)

    def body(dy_ref, w_ref, dsg_ref, ddn_ref, dyb_ref):
        dyb = dy_ref[...].astype(BF16)
        dyb_ref[...] = dyb
        dsg_ref[...] = _dot(dyb, w_ref[0:HALF_W, :], NT)
        ddn_ref[...] = _dot(dyb, w_ref[HALF_W:2 * HALF_W, :], NT)

    row = pl.BlockSpec((tm, d), lambda i: (i, 0))
    half = pl.BlockSpec((tm, HALF_W), lambda i: (i, 0))
    return _call(
        body, name=name, grid=(n // tm,),
        in_specs=[row, pl.BlockSpec((2 * HALF_W, d), lambda i: (0, 0))],
        out_specs=[half, half, row],
        out_shape=[_sds((n, HALF_W), F32), _sds((n, HALF_W), F32), _sds((n, d), BF16)],
    )(dy, w)


def _sg_group_masks():
    col = _iota2((SG_CHUNK, HALF_W), 1)
    return [jnp.logical_and(col >= g * SG_GROUP_DIM, col < (g + 1) * SG_GROUP_DIM)
            for g in range(SG_GROUPS)]


def _sg_causal():
    return _iota2((SG_CHUNK, SG_CHUNK), 0) >= _iota2((SG_CHUNK, SG_CHUNK), 1)


def _sg_forward_chunk(pu, pv, ln_g, ln_b, wc, bias, masks):
    u = _gelu(pu)
    v = _gelu(pv)
    mu = jnp.mean(v, axis=-1, keepdims=True)
    vc = v - mu
    rs = lax.rsqrt(jnp.mean(vc * vc, axis=-1, keepdims=True) + EPS)
    xhat = vc * rs
    vn = (xhat * ln_g + ln_b).astype(BF16)
    vs = bias
    for g in range(SG_GROUPS):
        vs = vs + jnp.where(masks[g], _dot(wc[g], vn, NN), 0.0)
    return u, xhat, rs, vn, vs


def sg_fwd(proj, ln_g, ln_b, w_s, bias_tile, name):
    n = proj.shape[0]
    tm = min(ROW_TILE, n)

    def body(pu_ref, pv_ref, g_ref, b_ref, w_ref, bias_ref, o_ref):
        causal = _sg_causal()
        wc = [jnp.where(causal, w_ref[g], 0.0).astype(BF16) for g in range(SG_GROUPS)]
        masks = _sg_group_masks()
        for ci in range(tm // SG_CHUNK):
            rows = slice(ci * SG_CHUNK, (ci + 1) * SG_CHUNK)
            u, _, _, _, vs = _sg_forward_chunk(pu_ref[rows, :], pv_ref[rows, :], g_ref[...],
                                               b_ref[...], wc, bias_ref[...], masks)
            o_ref[rows, :] = (u * vs).astype(BF16)

    vec = pl.BlockSpec((1, HALF_W), lambda i: (0, 0))
    return _call(
        body, name=name, grid=(n // tm,),
        in_specs=[pl.BlockSpec((tm, HALF_W), lambda i: (i, 0)),
                  pl.BlockSpec((tm, HALF_W), lambda i: (i, 1)), vec, vec,
                  pl.BlockSpec((SG_GROUPS, SG_CHUNK, SG_CHUNK), lambda i: (0, 0, 0)),
                  pl.BlockSpec((SG_CHUNK, HALF_W), lambda i: (0, 0))],
        out_specs=pl.BlockSpec((tm, HALF_W), lambda i: (i, 0)),
        out_shape=_sds((n, HALF_W), BF16),
    )(proj, proj, ln_g, ln_b, w_s, bias_tile)


def sg_bwd(dsg, proj, ln_g, ln_b, w_s, bias_tile, name):
    n = proj.shape[0]
    tm = min(ROW_TILE, n)

    def body(d_ref, pu_ref, pv_ref, g_ref, b_ref, w_ref, bias_ref,
             dp_ref, dw_ref, db_ref, dlg_ref, dlb_ref):
        @pl.when(pl.program_id(0) == 0)
        def _():
            dw_ref[...] = jnp.zeros_like(dw_ref)
            db_ref[...] = jnp.zeros_like(db_ref)
            dlg_ref[...] = jnp.zeros_like(dlg_ref)
            dlb_ref[...] = jnp.zeros_like(dlb_ref)

        causal = _sg_causal()
        wc = [jnp.where(causal, w_ref[g], 0.0).astype(BF16) for g in range(SG_GROUPS)]
        masks = _sg_group_masks()
        ln_g_v = g_ref[...]
        for ci in range(tm // SG_CHUNK):
            rows = slice(ci * SG_CHUNK, (ci + 1) * SG_CHUNK)
            pu = pu_ref[rows, :]
            pv = pv_ref[rows, :]
            u, xhat, rs, vn, vs = _sg_forward_chunk(pu, pv, ln_g_v, b_ref[...], wc,
                                                    bias_ref[...], masks)
            dout = d_ref[rows, :]
            dp_ref[rows, 0:HALF_W] = (dout * vs * _gelu_grad(pu)).astype(BF16)
            dvs = dout * u
            dvs_b = dvs.astype(BF16)
            db_ref[...] += dvs
            dvn = jnp.zeros_like(dvs)
            for g in range(SG_GROUPS):
                dvn = dvn + jnp.where(masks[g], _dot(wc[g], dvs_b, TN), 0.0)
                dwg = _dot(jnp.where(masks[g], dvs_b, jnp.zeros_like(dvs_b)), vn, NT)
                dw_ref[g] += jnp.where(causal, dwg, 0.0)
            dlg_ref[...] += jnp.sum(dvn * xhat, axis=0, keepdims=True)
            dlb_ref[...] += jnp.sum(dvn, axis=0, keepdims=True)
            dxh = dvn * ln_g_v
            dv = rs * (dxh - jnp.mean(dxh, axis=-1, keepdims=True)
                       - xhat * jnp.mean(dxh * xhat, axis=-1, keepdims=True))
            dp_ref[rows, HALF_W:2 * HALF_W] = (dv * _gelu_grad(pv)).astype(BF16)

    vec = pl.BlockSpec((1, HALF_W), lambda i: (0, 0))
    wspec = pl.BlockSpec((SG_GROUPS, SG_CHUNK, SG_CHUNK), lambda i: (0, 0, 0))
    tile = pl.BlockSpec((SG_CHUNK, HALF_W), lambda i: (0, 0))
    return _call(
        body, name=name, grid=(n // tm,),
        in_specs=[pl.BlockSpec((tm, HALF_W), lambda i: (i, 0)),
                  pl.BlockSpec((tm, HALF_W), lambda i: (i, 0)),
                  pl.BlockSpec((tm, HALF_W), lambda i: (i, 1)), vec, vec, wspec, tile],
        out_specs=[pl.BlockSpec((tm, 2 * HALF_W), lambda i: (i, 0)), wspec, tile, vec, vec],
        out_shape=[_sds((n, PROJ_W), BF16), _sds((SG_GROUPS, SG_CHUNK, SG_CHUNK), F32),
                   _sds((SG_CHUNK, HALF_W), F32), _sds((1, HALF_W), F32), _sds((1, HALF_W), F32)],
    )(dsg, proj, proj, ln_g, ln_b, w_s, bias_tile)


CONV_K = 4
CONV_BLOCK = 256


def _shift_down(x, s):
    if s == 0:
        return x
    rolled = pltpu.roll(x, s, 0)
    return jnp.where(_iota2(x.shape, 0) >= s, rolled, 0.0)


def _shift_up(x, s):
    if s == 0:
        return x
    t_len = x.shape[0]
    rolled = pltpu.roll(x, t_len - s, 0)
    return jnp.where(_iota2(x.shape, 0) < t_len - s, rolled, 0.0)


def _conv(x, w):
    y = _shift_down(x, CONV_K - 1) * w[0:1, :]
    for j in range(1, CONV_K):
        y = y + _shift_down(x, CONV_K - 1 - j) * w[j:j + 1, :]
    return y


def dn_conv_fwd(proj3, conv_w, name):
    b, t, _ = proj3.shape
    nblk = 3 * HALF_W // CONV_BLOCK
    first = 2 * HALF_W // CONV_BLOCK
    n_norm = 2 * HALF_W // CONV_BLOCK

    def body(x_ref, w_ref, o_ref):
        s = pl.program_id(1)
        y = _conv(x_ref[0], w_ref[...])
        y = y * _sigmoid(y)

        @pl.when(s < n_norm)
        def _():
            for h in range(CONV_BLOCK // HEAD_DIM):
                cs = slice(h * HEAD_DIM, (h + 1) * HEAD_DIM)
                yh = y[:, cs]
                o_ref[0, :, cs] = yh * lax.rsqrt(jnp.sum(yh * yh, axis=-1, keepdims=True) + EPS)

        @pl.when(s >= n_norm)
        def _():
            o_ref[0] = y

    return _call(
        body, name=name, grid=(b, nblk),
        in_specs=[pl.BlockSpec((1, t, CONV_BLOCK), lambda i, s: (i, 0, first + s)),
                  pl.BlockSpec((CONV_K, CONV_BLOCK), lambda i, s: (0, s))],
        out_specs=pl.BlockSpec((1, t, CONV_BLOCK), lambda i, s: (i, 0, s)),
        out_shape=_sds((b, t, 3 * HALF_W), F32),
    )(proj3, conv_w)


def dn_conv_bwd(dqkv, proj3, conv_w, dproj3, name):
    b, t, _ = proj3.shape
    nblk = 3 * HALF_W // CONV_BLOCK
    first = 2 * HALF_W // CONV_BLOCK
    n_norm = 2 * HALF_W // CONV_BLOCK

    def body(d_ref, x_ref, w_ref, dproj_in, dx_ref, dw_ref, ds_ref):
        s = pl.program_id(0)

        @pl.when(pl.program_id(1) == 0)
        def _():
            dw_ref[...] = jnp.zeros_like(dw_ref)

        x = x_ref[0]
        w = w_ref[...]
        c = _conv(x, w)
        sg = _sigmoid(c)
        y = c * sg

        @pl.when(s < n_norm)
        def _():
            for h in range(CONV_BLOCK // HEAD_DIM):
                cs = slice(h * HEAD_DIM, (h + 1) * HEAD_DIM)
                yh = y[:, cs]
                r = lax.rsqrt(jnp.sum(yh * yh, axis=-1, keepdims=True) + EPS)
                nh = yh * r
                dn = d_ref[0, :, cs]
                ds_ref[:, cs] = r * (dn - nh * jnp.sum(dn * nh, axis=-1, keepdims=True))

        @pl.when(s >= n_norm)
        def _():
            ds_ref[...] = d_ref[0]

        dc = ds_ref[...] * (sg * (1.0 + c * (1.0 - sg)))
        dx = _shift_up(dc, CONV_K - 1) * w[0:1, :]
        for j in range(1, CONV_K):
            dx = dx + _shift_up(dc, CONV_K - 1 - j) * w[j:j + 1, :]
        dx_ref[0] = dx.astype(BF16)
        for j in range(CONV_K):
            dw_ref[j:j + 1, :] += jnp.sum(dc * _shift_down(x, CONV_K - 1 - j), axis=0, keepdims=True)

    return _call(
        body, name=name, grid=(nblk, b),
        in_specs=[pl.BlockSpec((1, t, CONV_BLOCK), lambda s, i: (i, 0, s)),
                  pl.BlockSpec((1, t, CONV_BLOCK), lambda s, i: (i, 0, first + s)),
                  pl.BlockSpec((CONV_K, CONV_BLOCK), lambda s, i: (0, s)), _ANY],
        out_specs=[pl.BlockSpec((1, t, CONV_BLOCK), lambda s, i: (i, 0, first + s)),
                   pl.BlockSpec((CONV_K, CONV_BLOCK), lambda s, i: (0, s))],
        out_shape=[_sds(dproj3.shape, BF16), _sds((CONV_K, 3 * HALF_W), F32)],
        scratch=[pltpu.VMEM((t, CONV_BLOCK), F32)],
        input_output_aliases={3: 0},
    )(dqkv, proj3, conv_w, dproj3)


def _chunk_masks():
    ii = _iota2((DN_CHUNK, DN_CHUNK), 0)
    jj = _iota2((DN_CHUNK, DN_CHUNK), 1)
    return ii >= jj, ii > jj, ii == jj


LOCKSTEP_CHUNKS = 2


def _inv_unit_lower_many(l_mats, eye):
    eye_f = jnp.where(eye, 1.0, 0.0)
    ps = [-l for l in l_mats]
    ts = [eye_f + p for p in ps]
    pss = [_split(p) for p in ps]
    size = 2
    while size < DN_CHUNK:
        ps = [_dot3(s, s) for s in pss]
        pss = [_split(p) for p in ps]
        ts = [t + _dot3(_split(t), s) for t, s in zip(ts, pss)]
        size *= 2
    return ts


def _gates(pba, ea_row, dtb_row):
    beta = _sigmoid(pba)
    g = -ea_row * _softplus(pba + dtb_row)
    return beta, g


def _chunk_decay(gcol):
    incl, strict, eye = _chunk_masks()
    grow = jnp.sum(jnp.where(eye, gcol, 0.0), axis=0, keepdims=True)
    decay = jnp.where(incl, jnp.exp(jnp.where(incl, gcol - grow, 0.0)), 0.0)
    return decay, incl, strict, eye


def dn_chunk_fwd(qkv, proj3, alog_row, dtb_row, name):
    b, t, _ = qkv.shape
    rblk = min(256, t)
    n_in = rblk // DN_CHUNK

    def body(q_ref, k_ref, v_ref, pba_ref, al_ref, dtb_ref,
             u_ref, w_ref, qd_ref, kd_ref, qk_ref, ti_ref, gc_ref):
        ea = jnp.exp(al_ref[...])
        tri = jnp.where(_chunk_masks()[0], 1.0, 0.0)

        _, strict, eye = _chunk_masks()

        def chunk_group(cg, carry):
            items = []
            for sub in range(LOCKSTEP_CHUNKS):
                rows = pl.ds(pl.multiple_of((cg * LOCKSTEP_CHUNKS + sub) * DN_CHUNK, DN_CHUNK), DN_CHUNK)
                beta_all, g_all = _gates(pba_ref[0, rows, :], ea, dtb_ref[...])
                gc = _dot_exact_lhs(tri, g_all)
                gc_ref[0, rows, :] = gc
                for h in range(N_HEADS):
                    items.append((rows, h, beta_all[:, h:h + 1], gc[:, N_HEADS + h:N_HEADS + h + 1]))
            ks, kbs, decays, egs = [], [], [], []
            for rows, h, beta, gcol in items:
                cs = slice(h * HEAD_DIM, (h + 1) * HEAD_DIM)
                k = k_ref[0, rows, cs]
                ks.append(k)
                kbs.append(k * beta)
                decays.append(_chunk_decay(gcol)[0])
                egs.append(jnp.exp(gcol))
            ms = [_bdot(kb, k, NT) for kb, k in zip(kbs, ks)]
            tinvs = _inv_unit_lower_many([jnp.where(strict, m * dc, 0.0) for m, dc in zip(ms, decays)], eye)
            tsps = [_split(t) for t in tinvs]
            for (rows, h, beta, gcol), tsp, tinv in zip(items, tsps, tinvs):
                cs = slice(h * HEAD_DIM, (h + 1) * HEAD_DIM)
                u_ref[0, rows, cs] = _dot3(tsp, _split(v_ref[0, rows, cs] * beta))
                ti_ref[0, h, rows, :] = tinv
            for (rows, h, beta, gcol), tsp, kb, eg in zip(items, tsps, kbs, egs):
                cs = slice(h * HEAD_DIM, (h + 1) * HEAD_DIM)
                w_ref[0, rows, cs] = _dot3(tsp, _split(kb * eg))
            for (rows, h, beta, gcol), k, dc, eg in zip(items, ks, decays, egs):
                cs = slice(h * HEAD_DIM, (h + 1) * HEAD_DIM)
                q = q_ref[0, rows, cs] * QK_SCALE
                qk_ref[0, h, rows, :] = _bdot(q, k, NT) * dc
                qd_ref[0, rows, cs] = q * eg
                kd_ref[0, rows, cs] = k * jnp.exp(gcol[DN_CHUNK - 1:DN_CHUNK, :] - gcol)
            return carry

        lax.fori_loop(0, n_in // LOCKSTEP_CHUNKS, chunk_group, 0)

    def seg(cblk):
        return pl.BlockSpec((1, rblk, HALF_W), lambda i, r: (i, r, cblk))

    vec = pl.BlockSpec((1, LANES), lambda i, r: (0, 0))
    wide = pl.BlockSpec((1, rblk, HALF_W), lambda i, r: (i, r, 0))
    sq = pl.BlockSpec((1, N_HEADS, rblk, DN_CHUNK), lambda i, r: (i, 0, r, 0))
    return _call(
        body, name=name, grid=(b, t // rblk),
        in_specs=[seg(0), seg(1), seg(2),
                  pl.BlockSpec((1, rblk, LANES), lambda i, r: (i, r, GATE_COL_BLOCK)), vec, vec],
        out_specs=[wide, wide, wide, wide, sq, sq,
                   pl.BlockSpec((1, rblk, LANES), lambda i, r: (i, r, 0))],
        out_shape=[_sds((b, t, HALF_W), F32)] * 4
        + [_sds((b, N_HEADS, t, DN_CHUNK), F32)] * 2 + [_sds((b, t, LANES), F32)],
    )(qkv, qkv, qkv, proj3, alog_row, dtb_row)


def dn_scan_fwd(u, w, qd, kd, qk, gc, name):
    b, t, _ = u.shape
    nc = t // DN_CHUNK
    bh = b * N_HEADS

    def body(u_ref, w_ref, qd_ref, kd_ref, qk_ref, gc_ref, o_ref, sin_ref, s_ref):
        @pl.when(pl.program_id(0) == 0)
        def _():
            s_ref[...] = jnp.zeros_like(s_ref)

        items = [(bi, h, slice(h * HEAD_DIM, (h + 1) * HEAD_DIM)) for bi in range(b) for h in range(N_HEADS)]
        sbs = []
        for bi, h, cs in items:
            s = s_ref[bi * N_HEADS + h]
            sin_ref[0, bi * N_HEADS + h] = s
            sbs.append(s.astype(BF16))
        ws = [_bdot(w_ref[bi, :, cs], sb, NN) for (bi, h, cs), sb in zip(items, sbs)]
        qs = [_bdot(qd_ref[bi, :, cs], sb, NN) for (bi, h, cs), sb in zip(items, sbs)]
        vbs = [(u_ref[bi, :, cs] - wsi).astype(BF16) for (bi, h, cs), wsi in zip(items, ws)]
        for (bi, h, cs), qsi, vb in zip(items, qs, vbs):
            o_ref[bi, :, cs] = qsi + _bdot(qk_ref[bi, h], vb, NN)
        for (bi, h, cs), vb in zip(items, vbs):
            gl = jnp.exp(gc_ref[bi, DN_CHUNK - 1:DN_CHUNK, N_HEADS + h:N_HEADS + h + 1])
            idx = bi * N_HEADS + h
            s_ref[idx] = s_ref[idx] * gl + _bdot(kd_ref[bi, :, cs], vb, TN)

    wide = pl.BlockSpec((b, DN_CHUNK, HALF_W), lambda c: (0, c, 0))
    return _call(
        body, name=name, grid=(nc,),
        in_specs=[wide, wide, wide, wide,
                  pl.BlockSpec((b, N_HEADS, DN_CHUNK, DN_CHUNK), lambda c: (0, 0, c, 0)),
                  pl.BlockSpec((b, DN_CHUNK, LANES), lambda c: (0, c, 0))],
        out_specs=[wide, pl.BlockSpec((1, bh, HEAD_DIM, HEAD_DIM), lambda c: (c, 0, 0, 0))],
        out_shape=[_sds((b, t, HALF_W), F32), _sds((nc, bh, HEAD_DIM, HEAD_DIM), F32)],
        scratch=[pltpu.VMEM((bh, HEAD_DIM, HEAD_DIM), F32)],
    )(u, w, qd, kd, qk, gc)


def dn_scan_bwd(do, u, w, qd, kd, qk, gc, s_in, name):
    b, t, _ = u.shape
    nc = t // DN_CHUNK
    bh = b * N_HEADS

    def body(do_ref, u_ref, w_ref, qd_ref, kd_ref, qk_ref, gc_ref, sin_ref,
             du_ref, dw_ref, dqd_ref, dkd_ref, dqk_ref, dgc_ref, ds_ref):
        @pl.when(pl.program_id(0) == 0)
        def _():
            ds_ref[...] = jnp.zeros_like(ds_ref)

        last_row = _iota2((DN_CHUNK, LANES), 0) == DN_CHUNK - 1
        lane = _iota2((DN_CHUNK, LANES), 1)
        items = [(bi, h, slice(h * HEAD_DIM, (h + 1) * HEAD_DIM)) for bi in range(b) for h in range(N_HEADS)]
        sbs = [sin_ref[0, bi * N_HEADS + h].astype(BF16) for bi, h, cs in items]
        wvs = [w_ref[bi, :, cs].astype(BF16) for bi, h, cs in items]
        dovs = [do_ref[bi, :, cs].astype(BF16) for bi, h, cs in items]
        dsbs = [ds_ref[bi * N_HEADS + h].astype(BF16) for bi, h, cs in items]
        vbs = [(u_ref[bi, :, cs] - _dot(wv, sb, NN)).astype(BF16)
               for (bi, h, cs), wv, sb in zip(items, wvs, sbs)]
        for (bi, h, cs), dov, sb in zip(items, dovs, sbs):
            dqd_ref[bi, :, cs] = _dot(dov, sb, NT)
        dvns = [_dot(kd_ref[bi, :, cs].astype(BF16), dsb, NN) + _dot(qk_ref[bi, h].astype(BF16), dov, TN)
                for (bi, h, cs), dsb, dov in zip(items, dsbs, dovs)]
        for (bi, h, cs), vb, dsb, dov in zip(items, vbs, dsbs, dovs):
            dkd_ref[bi, :, cs] = _dot(vb, dsb, NT)
            dqk_ref[bi, h] = _dot(dov, vb, NT)
        dgls = []
        for (bi, h, cs), dvn, sb, wv, dov in zip(items, dvns, sbs, wvs, dovs):
            idx = bi * N_HEADS + h
            du_ref[bi, :, cs] = dvn
            dvn_b = dvn.astype(BF16)
            dw_ref[bi, :, cs] = -_dot(dvn_b, sb, NT)
            gl = jnp.exp(gc_ref[bi, DN_CHUNK - 1:DN_CHUNK, N_HEADS + h:N_HEADS + h + 1])
            ds = ds_ref[idx]
            dgl = jnp.sum(jnp.sum(ds * sin_ref[0, idx], axis=1, keepdims=True), axis=0, keepdims=True)
            dgls.append(dgl * gl)
            ds_ref[idx] = (ds * gl + _dot(qd_ref[bi, :, cs].astype(BF16), dov, TN)
                           - _dot(wv, dvn_b, TN))
        for bi in range(b):
            dgc = jnp.zeros((DN_CHUNK, LANES), F32)
            for h in range(N_HEADS):
                dgc = dgc + jnp.where(jnp.logical_and(last_row, lane == N_HEADS + h),
                                      dgls[bi * N_HEADS + h], 0.0)
            dgc_ref[bi] = dgc

    def rev(c):
        return nc - 1 - c

    wide = pl.BlockSpec((b, DN_CHUNK, HALF_W), lambda c: (0, rev(c), 0))
    sq = pl.BlockSpec((b, N_HEADS, DN_CHUNK, DN_CHUNK), lambda c: (0, 0, rev(c), 0))
    gates = pl.BlockSpec((b, DN_CHUNK, LANES), lambda c: (0, rev(c), 0))
    return _call(
        body, name=name, grid=(nc,),
        in_specs=[wide, wide, wide, wide, wide, sq, gates,
                  pl.BlockSpec((1, bh, HEAD_DIM, HEAD_DIM), lambda c: (rev(c), 0, 0, 0))],
        out_specs=[wide, wide, wide, wide, sq, gates],
        out_shape=[_sds((b, t, HALF_W), F32)] * 4
        + [_sds((b, N_HEADS, t, DN_CHUNK), F32), _sds((b, t, LANES), F32)],
        scratch=[pltpu.VMEM((bh, HEAD_DIM, HEAD_DIM), F32)],
    )(do, u, w, qd, kd, qk, gc, s_in)


def dn_chunk_bwd(qkv, proj3, alog_row, dtb_row, tinv, u, w, du, dw, dqd, dkd, dqk, dgc_scan, dproj3, name):
    b, t, _ = qkv.shape
    rblk = min(256, t)
    n_in = rblk // DN_CHUNK

    def body(q_ref, k_ref, v_ref, pba_ref, al_ref, dtb_ref, ti_ref, u_ref, w_ref,
             du_ref, dw_ref, dqd_ref, dkd_ref, dqk_ref, dgs_ref, dproj_in,
             dq_ref, dpba_ref, dal_ref, ddtb_ref):
        @pl.when(jnp.logical_and(pl.program_id(0) == 0, pl.program_id(1) == 0))
        def _():
            dal_ref[...] = jnp.zeros_like(dal_ref)
            ddtb_ref[...] = jnp.zeros_like(ddtb_ref)

        ea = jnp.exp(al_ref[...])
        incl0 = _chunk_masks()[0]
        tri = jnp.where(incl0, 1.0, 0.0)
        tri_up = jnp.where(_iota2((DN_CHUNK, DN_CHUNK), 1) >= _iota2((DN_CHUNK, DN_CHUNK), 0), 1.0, 0.0)
        lane = _iota2((DN_CHUNK, LANES), 1)
        last_col = _iota2((DN_CHUNK, 1), 0) == DN_CHUNK - 1

        _, strict, _ = _chunk_masks()
        gate_lane = jnp.logical_and(lane >= N_HEADS, lane < 2 * N_HEADS)

        def chunk_group(cg, carry):
            tiles, items = [], []
            for sub in range(LOCKSTEP_CHUNKS):
                rows = pl.ds(pl.multiple_of((cg * LOCKSTEP_CHUNKS + sub) * DN_CHUNK, DN_CHUNK), DN_CHUNK)
                pba = pba_ref[0, rows, :]
                beta_all, g_all = _gates(pba, ea, dtb_ref[...])
                gc = _dot_exact_lhs(tri, g_all)
                tiles.append((rows, pba, beta_all, g_all))
                for h in range(N_HEADS):
                    items.append((sub, rows, h, slice(h * HEAD_DIM, (h + 1) * HEAD_DIM),
                                  beta_all[:, h:h + 1], gc[:, N_HEADS + h:N_HEADS + h + 1]))
            decays = [_chunk_decay(gcol)[0] for _, _, _, _, _, gcol in items]
            egs = [jnp.exp(gcol) for _, _, _, _, _, gcol in items]
            qbs = [(q_ref[0, rows, cs] * QK_SCALE).astype(BF16) for _, rows, h, cs, _, _ in items]
            kfs = [k_ref[0, rows, cs].astype(BF16) for _, rows, h, cs, _, _ in items]
            kbs = [k_ref[0, rows, cs] * beta for _, rows, h, cs, beta, _ in items]
            kbbs = [kb.astype(BF16) for kb in kbs]
            tsps = [_split(ti_ref[0, h, rows, :]) for _, rows, h, cs, _, _ in items]
            drus = [_dot3(tsp, _split(du_ref[0, rows, cs]), TN)
                    for (_, rows, h, cs, _, _), tsp in zip(items, tsps)]
            drws = [_dot3(tsp, _split(dw_ref[0, rows, cs]), TN)
                    for (_, rows, h, cs, _, _), tsp in zip(items, tsps)]
            m_kks = [_dot(kbb, kf, NT) for kbb, kf in zip(kbbs, kfs)]
            a_qks = [_dot(qb, kf, NT) for qb, kf in zip(qbs, kfs)]
            dls = [-jnp.where(strict, _dot3(_split(dru), _split(u_ref[0, rows, cs]), NT)
                              + _dot3(_split(drw), _split(w_ref[0, rows, cs]), NT), 0.0)
                   for (_, rows, h, cs, _, _), dru, drw in zip(items, drus, drws)]
            dms = [(dl * dc).astype(BF16) for dl, dc in zip(dls, decays)]
            das = [(dqk_ref[0, h, rows, :] * dc).astype(BF16)
                   for (_, rows, h, cs, _, _), dc in zip(items, decays)]
            dkb_mm = [_dot(dm, kf, NN) for dm, kf in zip(dms, kfs)]
            dk_mm = [_dot(dm, kbb, TN) + _dot(da, qb, TN) for dm, kbb, da, qb in zip(dms, kbbs, das, qbs)]
            dqs_mm = [_dot(da, kf, NN) for da, kf in zip(das, kfs)]
            dgc_tiles = [dgs_ref[0, rows, :] for rows, _, _, _ in tiles]
            dbeta_tiles = [jnp.zeros((DN_CHUNK, LANES), F32) for _ in tiles]
            for n_it, (sub, rows, h, cs, beta, gcol) in enumerate(items):
                eg, dc = egs[n_it], decays[n_it]
                k = k_ref[0, rows, cs]
                q = q_ref[0, rows, cs] * QK_SCALE
                kb, dru, drw = kbs[n_it], drus[n_it], drws[n_it]
                ek = jnp.exp(gcol[DN_CHUNK - 1:DN_CHUNK, :] - gcol)
                e_mat = (dls[n_it] * m_kks[n_it] + dqk_ref[0, h, rows, :] * a_qks[n_it]) * dc
                dkb = drw * eg + dkb_mm[n_it]
                dg = (jnp.sum(drw * kb * eg, axis=-1, keepdims=True)
                      + jnp.sum(e_mat, axis=1, keepdims=True)
                      - _row_to_col(jnp.sum(e_mat, axis=0, keepdims=True)))
                dqd = dqd_ref[0, rows, cs]
                dg = dg + jnp.sum(dqd * q * eg, axis=-1, keepdims=True)
                dkd = dkd_ref[0, rows, cs]
                tk_ = jnp.sum(dkd * k * ek, axis=-1, keepdims=True)
                dg = dg - tk_ + jnp.where(last_col, jnp.sum(tk_, axis=0, keepdims=True), 0.0)
                dbeta = (jnp.sum(dkb * k, axis=-1, keepdims=True)
                         + jnp.sum(dru * v_ref[0, rows, cs], axis=-1, keepdims=True))
                dq_ref[0, rows, cs] = (dqs_mm[n_it] + dqd * eg) * QK_SCALE
                dq_ref[0, rows, pl.ds(HALF_W + h * HEAD_DIM, HEAD_DIM)] = dk_mm[n_it] + dkd * ek + dkb * beta
                dq_ref[0, rows, pl.ds(2 * HALF_W + h * HEAD_DIM, HEAD_DIM)] = dru * beta
                dgc_tiles[sub] = dgc_tiles[sub] + jnp.where(lane == N_HEADS + h, dg, 0.0)
                dbeta_tiles[sub] = dbeta_tiles[sub] + jnp.where(lane == h, dbeta, 0.0)
            for (rows, pba, beta_all, g_all), dgc_tile, dbeta_tile in zip(tiles, dgc_tiles, dbeta_tiles):
                dg_tile = _dot_exact_lhs(tri_up, dgc_tile)
                da_pre = dg_tile * (-ea) * _sigmoid(pba + dtb_ref[...])
                dal_ref[...] += jnp.sum(jnp.where(gate_lane, dg_tile * g_all, 0.0), axis=0, keepdims=True)
                ddtb_ref[...] += jnp.sum(jnp.where(gate_lane, da_pre, 0.0), axis=0, keepdims=True)
                dpba_ref[0, rows, :] = jnp.where(lane < N_HEADS, dbeta_tile * beta_all * (1.0 - beta_all),
                                                 jnp.where(gate_lane, da_pre, 0.0)).astype(BF16)
            return carry

        lax.fori_loop(0, n_in // LOCKSTEP_CHUNKS, chunk_group, 0)

    def seg(cblk):
        return pl.BlockSpec((1, rblk, HALF_W), lambda i, r: (i, r, cblk))

    vec = pl.BlockSpec((1, LANES), lambda i, r: (0, 0))
    wide = pl.BlockSpec((1, rblk, HALF_W), lambda i, r: (i, r, 0))
    sq = pl.BlockSpec((1, N_HEADS, rblk, DN_CHUNK), lambda i, r: (i, 0, r, 0))
    gates = pl.BlockSpec((1, rblk, LANES), lambda i, r: (i, r, 0))
    return _call(
        body, name=name, grid=(b, t // rblk),
        in_specs=[seg(0), seg(1), seg(2),
                  pl.BlockSpec((1, rblk, LANES), lambda i, r: (i, r, GATE_COL_BLOCK)), vec, vec,
                  sq, wide, wide, wide, wide, wide, wide, sq, gates, _ANY],
        out_specs=[pl.BlockSpec((1, rblk, 3 * HALF_W), lambda i, r: (i, r, 0)),
                   pl.BlockSpec((1, rblk, LANES), lambda i, r: (i, r, GATE_COL_BLOCK)), vec, vec],
        out_shape=[_sds((b, t, 3 * HALF_W), F32), _sds(dproj3.shape, BF16),
                   _sds((1, LANES), F32), _sds((1, LANES), F32)],
        input_output_aliases={15: 1},
    )(qkv, qkv, qkv, proj3, alog_row, dtb_row, tinv, u, w, du, dw, dqd, dkd, dqk, dgc_scan, dproj3)


def dn_out_fwd(o, proj, dn_norm, name):
    n = o.shape[0]
    tm = min(ROW_TILE, n)

    def body(o_ref, z_ref, g_ref, y_ref):
        for h in range(N_HEADS):
            cs = slice(h * HEAD_DIM, (h + 1) * HEAD_DIM)
            oh = o_ref[:, cs]
            z = z_ref[:, cs]
            y = oh * _rms_scale(oh) * g_ref[...]
            y_ref[:, cs] = (y * (z * _sigmoid(z))).astype(BF16)

    half = pl.BlockSpec((tm, HALF_W), lambda i: (i, 0))
    return _call(
        body, name=name, grid=(n // tm,),
        in_specs=[half, pl.BlockSpec((tm, HALF_W), lambda i: (i, 5)),
                  pl.BlockSpec((1, HEAD_DIM), lambda i: (0, 0))],
        out_specs=half, out_shape=_sds((n, HALF_W), BF16),
    )(o, proj, dn_norm)


def dn_out_bwd(dy, o, proj, dn_norm, dproj, name):
    n = o.shape[0]
    tm = min(ROW_TILE, n)

    def body(dy_ref, o_ref, z_ref, g_ref, dproj_in, do_ref, dz_ref, dg_ref):
        @pl.when(pl.program_id(0) == 0)
        def _():
            dg_ref[...] = jnp.zeros_like(dg_ref)

        g = g_ref[...]
        dg = jnp.zeros_like(g)
        for h in range(N_HEADS):
            cs = slice(h * HEAD_DIM, (h + 1) * HEAD_DIM)
            oh = o_ref[:, cs]
            z = z_ref[:, cs]
            d = dy_ref[:, cs]
            r = _rms_scale(oh)
            nh = oh * r
            sz = _sigmoid(z)
            dyn = d * (z * sz)
            dz_ref[:, cs] = (d * (nh * g) * (sz * (1.0 + z * (1.0 - sz)))).astype(BF16)
            dg = dg + jnp.sum(dyn * nh, axis=0, keepdims=True)
            dn = dyn * g
            do_ref[:, cs] = r * (dn - nh * jnp.mean(dn * nh, axis=-1, keepdims=True))
        dg_ref[...] += dg

    half = pl.BlockSpec((tm, HALF_W), lambda i: (i, 0))
    vec = pl.BlockSpec((1, HEAD_DIM), lambda i: (0, 0))
    return _call(
        body, name=name, grid=(n // tm,),
        in_specs=[half, half, pl.BlockSpec((tm, HALF_W), lambda i: (i, 5)), vec, _ANY],
        out_specs=[half, pl.BlockSpec((tm, HALF_W), lambda i: (i, 5)), vec],
        out_shape=[_sds((n, HALF_W), F32), _sds(dproj.shape, BF16), _sds((1, HEAD_DIM), F32)],
        input_output_aliases={4: 1},
    )(dy, o, proj, dn_norm, dproj)


def _adamw_math(w, g, m, v):
    m_new = ADAM_B1 * m + (1.0 - ADAM_B1) * g
    v_new = ADAM_B2 * v + (1.0 - ADAM_B2) * (g * g)
    m_hat = m_new / (1.0 - ADAM_B1 ** ADAM_STEP)
    v_hat = v_new / (1.0 - ADAM_B2 ** ADAM_STEP)
    delta = -ADAM_LR * (m_hat / (jnp.sqrt(v_hat) + ADAM_EPS) + ADAM_WD * w)
    return delta, m_new, v_new


def adamw(w, g, m, v, name):
    r, c = w.shape
    tr = r
    for cand in (256, 352):
        if r % cand == 0 and r > cand:
            tr = cand
            break

    def body(w_ref, g_ref, m_ref, v_ref, d_ref, mo_ref, vo_ref):
        d, mn, vn = _adamw_math(w_ref[...], g_ref[...], m_ref[...], v_ref[...])
        d_ref[...] = d
        mo_ref[...] = mn
        vo_ref[...] = vn

    spec = pl.BlockSpec((tr, c), lambda i: (i, 0))
    return _call(
        body, name=name, grid=(r // tr,),
        in_specs=[spec] * 4, out_specs=[spec] * 3, out_shape=[_sds((r, c), F32)] * 3,
    )(w, g, m, v)


def _place():
    return lax.axis_index("x"), lax.axis_index("y"), lax.axis_index("c")


def _other_chips(x, y):
    return [(1 - x, y), (x, 1 - y), (1 - x, 1 - y)]


_ANY = pl.BlockSpec(memory_space=pl.ANY)


def cast_place(w, shard_idx, name):
    r, cols = w.shape
    tr = r // 2

    def body(j_ref, w_ref, o_ref):
        o_ref[0] = w_ref[...].astype(BF16)

    return pl.pallas_call(
        body, name=name,
        grid_spec=pltpu.PrefetchScalarGridSpec(
            num_scalar_prefetch=1, grid=(r // tr,),
            in_specs=[pl.BlockSpec((tr, cols), lambda i, j: (i, 0))],
            out_specs=pl.BlockSpec((1, tr, cols), lambda i, j: (j[0], i, 0))),
        out_shape=_sds((N_SHARD, r, cols), BF16),
        compiler_params=pltpu.CompilerParams(dimension_semantics=("arbitrary",),
                                             vmem_limit_bytes=VMEM_LIMIT),
    )(shard_idx, w)


def all_gather_chips(bufs, small, name):
    n = len(bufs)

    def body(*refs):
        small_in = refs[n]
        outs, small_out = refs[n + 1:2 * n + 1], refs[2 * n + 1]
        send, recv, fsend, frecv, loc = refs[2 * n + 2:]
        x, y, c = _place()
        j = 2 * x + y
        chips = _other_chips(x, y)
        sib = (x, y, 1 - c)

        def half(a, blk, hc):
            rh = bufs[a].shape[1] // 2
            return outs[a].at[blk, pl.ds(hc * rh, rh), :]

        local = pltpu.make_async_copy(small_in, small_out.at[j], loc.at[0])
        local.start()
        sends = []
        for k, (px, py) in enumerate(chips):
            sends.append(pltpu.make_async_remote_copy(
                src_ref=small_in, dst_ref=small_out.at[j], send_sem=send.at[3 * n + k],
                recv_sem=recv.at[3 * n + k], device_id=(px, py, c), device_id_type=MESH))
            for a in range(n):
                sends.append(pltpu.make_async_remote_copy(
                    src_ref=half(a, j, c), dst_ref=half(a, j, c), send_sem=send.at[3 * a + k],
                    recv_sem=recv.at[3 * a + k], device_id=(px, py, c), device_id_type=MESH))
        for cp in sends:
            cp.start()
        forwards = []
        for k, (px, py) in enumerate(chips):
            blk = 2 * px + py
            for a in range(n):
                pltpu.make_async_remote_copy(
                    src_ref=half(a, blk, c), dst_ref=half(a, blk, c), send_sem=send.at[3 * a + k],
                    recv_sem=recv.at[3 * a + k], device_id=(px, py, c), device_id_type=MESH).wait_recv()
                fw = pltpu.make_async_remote_copy(
                    src_ref=half(a, blk, c), dst_ref=half(a, blk, c), send_sem=fsend.at[3 * a + k],
                    recv_sem=frecv.at[3 * a + k], device_id=sib, device_id_type=MESH)
                fw.start()
                forwards.append(fw)
        for k, (px, py) in enumerate(chips):
            blk = 2 * px + py
            pltpu.make_async_remote_copy(
                src_ref=small_in, dst_ref=small_out.at[blk], send_sem=send.at[3 * n + k],
                recv_sem=recv.at[3 * n + k], device_id=(px, py, c), device_id_type=MESH).wait_recv()
            for a in range(n):
                pltpu.make_async_remote_copy(
                    src_ref=half(a, blk, 1 - c), dst_ref=half(a, blk, 1 - c), send_sem=fsend.at[3 * a + k],
                    recv_sem=frecv.at[3 * a + k], device_id=sib, device_id_type=MESH).wait_recv()
        for cp in sends + forwards:
            cp.wait_send()
        local.wait()

    res = _call(
        body, name=name, in_specs=[_ANY] * (n + 1), out_specs=[_ANY] * (n + 1),
        out_shape=[_sds(b.shape, b.dtype) for b in bufs] + [_sds((N_SHARD,) + small.shape, small.dtype)],
        scratch=[pltpu.SemaphoreType.DMA((3 * n + 3,)), pltpu.SemaphoreType.DMA((3 * n + 3,)),
                 pltpu.SemaphoreType.DMA((3 * n,)), pltpu.SemaphoreType.DMA((3 * n,)),
                 pltpu.SemaphoreType.DMA((1,))],
        input_output_aliases={a: a for a in range(n)},
    )(*bufs, small)
    return res[:n], res[n]


def pair_exchange(arrs, name):
    n = len(arrs)

    def body(*refs):
        ins, outs = refs[:n], refs[n:2 * n]
        send, recv = refs[2 * n:]
        x, y, c = _place()
        cps = []
        for a in range(n):
            rh = arrs[a].shape[1] // 2
            cps.append(pltpu.make_async_remote_copy(
                src_ref=ins[a].at[:, pl.ds((1 - c) * rh, rh), :], dst_ref=outs[a],
                send_sem=send.at[a], recv_sem=recv.at[a], device_id=(x, y, 1 - c), device_id_type=MESH))
        for cp in cps:
            cp.start()
        for cp in cps:
            cp.wait_recv()
        for cp in cps:
            cp.wait_send()

    return _call(
        body, name=name, in_specs=[_ANY] * n, out_specs=[_ANY] * n,
        out_shape=[_sds((a.shape[0], a.shape[1] // 2, a.shape[2]), a.dtype) for a in arrs],
        scratch=[pltpu.SemaphoreType.DMA((n,)), pltpu.SemaphoreType.DMA((n,))],
    )(*arrs)


def pair_add(g, s, c_idx, name):
    nb, r, cols = g.shape
    rh = r // 2

    def body(c_ref, g_ref, s_ref, o_ref):
        o_ref[...] = (g_ref[...] + s_ref[...]).astype(BF16)

    return pl.pallas_call(
        body, name=name,
        grid_spec=pltpu.PrefetchScalarGridSpec(
            num_scalar_prefetch=1, grid=(nb,),
            in_specs=[pl.BlockSpec((1, rh, cols), lambda j, c: (j, c[0], 0)),
                      pl.BlockSpec((1, rh, cols), lambda j, c: (j, 0, 0))],
            out_specs=pl.BlockSpec((1, rh, cols), lambda j, c: (j, 0, 0))),
        out_shape=_sds((nb, rh, cols), BF16),
        compiler_params=pltpu.CompilerParams(dimension_semantics=("arbitrary",),
                                             vmem_limit_bytes=VMEM_LIMIT),
    )(c_idx, g, s)


def chip_exchange(arrs, name):
    n = len(arrs)

    def body(*refs):
        ins, outs = refs[:n], refs[n:2 * n]
        send, recv = refs[2 * n:]
        x, y, c = _place()
        j = 2 * x + y
        chips = _other_chips(x, y)
        sends = []
        for a in range(n):
            for k, (px, py) in enumerate(chips):
                sends.append(pltpu.make_async_remote_copy(
                    src_ref=ins[a].at[2 * px + py], dst_ref=outs[a].at[j], send_sem=send.at[3 * a + k],
                    recv_sem=recv.at[3 * a + k], device_id=(px, py, c), device_id_type=MESH))
        for cp in sends:
            cp.start()
        for a in range(n):
            for k, (px, py) in enumerate(chips):
                pltpu.make_async_remote_copy(
                    src_ref=ins[a].at[j], dst_ref=outs[a].at[2 * px + py], send_sem=send.at[3 * a + k],
                    recv_sem=recv.at[3 * a + k], device_id=(px, py, c), device_id_type=MESH).wait_recv()
        for cp in sends:
            cp.wait_send()

    return _call(
        body, name=name, in_specs=[_ANY] * n, out_specs=[_ANY] * n,
        out_shape=[_sds(a.shape, a.dtype) for a in arrs],
        scratch=[pltpu.SemaphoreType.DMA((3 * n,)), pltpu.SemaphoreType.DMA((3 * n,))],
    )(*arrs)


def sum_chips(r, p, shard_idx, name):
    nb, rh, cols = r.shape
    tr = rh // 2

    def body(j_ref, p_ref, *refs):
        o_ref = refs[nb]
        j = j_ref[0]
        acc = None
        for i in range(nb):
            term = jnp.where(j == i, p_ref[0], refs[i][0]).astype(F32)
            acc = term if acc is None else acc + term
        o_ref[...] = acc

    def slot(i):
        return pl.BlockSpec((1, tr, cols), lambda t, j: (jnp.where(j[0] == i, (i + 1) % nb, i), t, 0))

    return pl.pallas_call(
        body, name=name,
        grid_spec=pltpu.PrefetchScalarGridSpec(
            num_scalar_prefetch=1, grid=(rh // tr,),
            in_specs=[pl.BlockSpec((1, tr, cols), lambda t, j: (j[0], t, 0))] + [slot(i) for i in range(nb)],
            out_specs=pl.BlockSpec((tr, cols), lambda t, j: (t, 0))),
        out_shape=_sds((rh, cols), F32),
        compiler_params=pltpu.CompilerParams(dimension_semantics=("arbitrary",),
                                             vmem_limit_bytes=VMEM_LIMIT),
    )(shard_idx, p, *([r] * nb))


def pair_swap(arrs, name):
    n = len(arrs)

    def body(*refs):
        ins, outs = refs[:n], refs[n:2 * n]
        send, recv = refs[2 * n:]
        x, y, c = _place()
        cps = [pltpu.make_async_remote_copy(
            src_ref=ins[a], dst_ref=outs[a], send_sem=send.at[a], recv_sem=recv.at[a],
            device_id=(x, y, 1 - c), device_id_type=MESH) for a in range(n)]
        for cp in cps:
            cp.start()
        for cp in cps:
            cp.wait_recv()
        for cp in cps:
            cp.wait_send()

    return _call(
        body, name=name, in_specs=[_ANY] * n, out_specs=[_ANY] * n,
        out_shape=[_sds(a.shape, a.dtype) for a in arrs],
        scratch=[pltpu.SemaphoreType.DMA((n,)), pltpu.SemaphoreType.DMA((n,))],
    )(*arrs)


def adamw_pair(w, g_mine, g_sib, m, v, c_idx, name):
    r, cols = w.shape
    rh = r // 2
    tr = rh // 2
    nh = rh // tr

    def body(c_ref, w_ref, gm_ref, gs_ref, m_ref, v_ref, g_ref, d_ref, mo_ref, vo_ref):
        mine = (pl.program_id(0) // nh) == c_ref[0]
        g = jnp.where(mine, gm_ref[...], gs_ref[...])
        d, mn, vn = _adamw_math(w_ref[...], g, m_ref[...], v_ref[...])
        g_ref[...] = g
        d_ref[...] = d
        mo_ref[...] = mn
        vo_ref[...] = vn

    full = pl.BlockSpec((tr, cols), lambda i, c: (i, 0))
    part = pl.BlockSpec((tr, cols), lambda i, c: (i % nh, 0))
    return pl.pallas_call(
        body, name=name,
        grid_spec=pltpu.PrefetchScalarGridSpec(
            num_scalar_prefetch=1, grid=(r // tr,),
            in_specs=[full, part, part, full, full], out_specs=[full] * 4),
        out_shape=[_sds((r, cols), F32)] * 4,
        compiler_params=pltpu.CompilerParams(dimension_semantics=("arbitrary",),
                                             vmem_limit_bytes=VMEM_LIMIT),
    )(c_idx, w, g_mine, g_sib, m, v)


N_DEV = 8


def all_reduce_small(pack, name):
    r, cols = pack.shape

    def body(in_ref, out_ref, buf, send, recv):
        x, y, c = _place()
        me = 4 * x + 2 * y + c
        buf[me] = in_ref[...]
        peers = []
        for k in range(1, N_DEV):
            fx, fy, fc = (k >> 2) & 1, (k >> 1) & 1, k & 1
            peers.append((1 - x if fx else x, 1 - y if fy else y, 1 - c if fc else c))
        sends = [pltpu.make_async_remote_copy(
            src_ref=in_ref, dst_ref=buf.at[me], send_sem=send.at[k], recv_sem=recv.at[k],
            device_id=p, device_id_type=MESH) for k, p in enumerate(peers)]
        for cp in sends:
            cp.start()
        for k, (px, py, pc) in enumerate(peers):
            pltpu.make_async_remote_copy(
                src_ref=in_ref, dst_ref=buf.at[4 * px + 2 * py + pc], send_sem=send.at[k],
                recv_sem=recv.at[k], device_id=(px, py, pc), device_id_type=MESH).wait_recv()
        for cp in sends:
            cp.wait_send()
        acc = buf[0] + buf[1]
        for i in range(2, N_DEV):
            acc = acc + buf[i]
        out_ref[...] = acc

    vm = pl.BlockSpec(memory_space=pltpu.VMEM)
    return _call(
        body, name=name, in_specs=[vm], out_specs=vm, out_shape=_sds((r, cols), F32),
        scratch=[pltpu.VMEM((N_DEV, r, cols), F32), pltpu.SemaphoreType.DMA((N_DEV - 1,)),
                 pltpu.SemaphoreType.DMA((N_DEV - 1,))],
    )(pack)


SMALL_NAMES = ("ffn1_norm", "mix_norm", "ffn2_norm", "final_norm", "sg_ln_g", "sg_ln_b",
               "dn_norm", "a_log", "dt_bias", "sg_b", "sg_w", "conv_w")


def _to_rows(a):
    flat = a.reshape(-1)
    pad = (-flat.shape[0]) % LANES
    if pad:
        flat = jnp.pad(flat, (0, pad))
    return flat.reshape(-1, LANES)


def _pack_small(parts):
    rows = [_to_rows(parts[k]) for k in SMALL_NAMES]
    pack = jnp.concatenate(rows, axis=0)
    pad = (-pack.shape[0]) % 8
    if pad:
        pack = jnp.pad(pack, ((0, pad), (0, 0)))
    return pack


def _unpack_small(pack, shapes):
    out, r0 = {}, 0
    for k in SMALL_NAMES:
        size = 1
        for s in shapes[k]:
            size *= s
        nrows = -(-size // LANES)
        out[k] = pack[r0:r0 + nrows].reshape(-1)[:size].reshape(shapes[k])
        r0 += nrows
    return out


def kernel(x, ffn1_norm, ffn1_w_gate, ffn1_w_up, ffn1_w_down, mix_norm, w_in, conv_w, a_log, dt_bias, dn_norm, sg_ln_g, sg_ln_b, sg_w, sg_b, w_out, ffn2_norm, ffn2_w_gate, ffn2_w_up, ffn2_w_down, final_norm, loss_target, m_ffn1_norm, m_ffn1_w_gate, m_ffn1_w_up, m_ffn1_w_down, m_mix_norm, m_w_in, m_conv_w, m_a_log, m_dt_bias, m_dn_norm, m_sg_ln_g, m_sg_ln_b, m_sg_w, m_sg_b, m_w_out, m_ffn2_norm, m_ffn2_w_gate, m_ffn2_w_up, m_ffn2_w_down, m_final_norm, v_ffn1_norm, v_ffn1_w_gate, v_ffn1_w_up, v_ffn1_w_down, v_mix_norm, v_w_in, v_conv_w, v_a_log, v_dt_bias, v_dn_norm, v_sg_ln_g, v_sg_ln_b, v_sg_w, v_sg_b, v_w_out, v_ffn2_norm, v_ffn2_w_gate, v_ffn2_w_up, v_ffn2_w_down, v_final_norm):
    bsz, t_len, d = x.shape
    n = bsz * t_len
    xy, yy, cc = _place()
    shard = 2 * xy + yy

    big_names = ["ffn1_w_gate", "ffn1_w_up", "ffn1_w_down", "w_in", "w_out",
                 "ffn2_w_gate", "ffn2_w_up", "ffn2_w_down"]
    big_w = dict(ffn1_w_gate=ffn1_w_gate, ffn1_w_up=ffn1_w_up, ffn1_w_down=ffn1_w_down, w_in=w_in,
                 w_out=w_out, ffn2_w_gate=ffn2_w_gate, ffn2_w_up=ffn2_w_up, ffn2_w_down=ffn2_w_down)
    big_m = dict(ffn1_w_gate=m_ffn1_w_gate, ffn1_w_up=m_ffn1_w_up, ffn1_w_down=m_ffn1_w_down, w_in=m_w_in,
                 w_out=m_w_out, ffn2_w_gate=m_ffn2_w_gate, ffn2_w_up=m_ffn2_w_up, ffn2_w_down=m_ffn2_w_down)
    big_v = dict(ffn1_w_gate=v_ffn1_w_gate, ffn1_w_up=v_ffn1_w_up, ffn1_w_down=v_ffn1_w_down, w_in=v_w_in,
                 w_out=v_w_out, ffn2_w_gate=v_ffn2_w_gate, ffn2_w_up=v_ffn2_w_up, ffn2_w_down=v_ffn2_w_down)
    shard_idx = jnp.reshape(shard, (1,)).astype(jnp.int32)
    c_idx = jnp.reshape(cc, (1,)).astype(jnp.int32)
    transposed = ("ffn1_w_gate", "ffn1_w_up", "ffn2_w_gate", "ffn2_w_up")

    def as2d(a, k):
        return a[0].T if k in transposed else a[0]

    def from2d(a, k):
        return a.T[None] if k in transposed else a[None]

    placed = [cast_place(as2d(big_w[k], k), shard_idx, name="cast_" + k) for k in big_names]
    gathered, conv_g = all_gather_chips(placed, conv_w[0], name="gather_weights")
    gw = dict(zip(big_names, gathered))
    conv_full = conv_g.transpose(1, 0, 2).reshape(CONV_K, 3 * HALF_W)
    w_in_full = gw["w_in"].transpose(1, 0, 2).reshape(d, IN_COLS)
    w_in_full = jnp.pad(w_in_full, ((0, 0), (0, PROJ_W - IN_COLS)))
    w_out_full = gw["w_out"].reshape(2 * HALF_W, d)

    x0 = x.reshape(n, d)
    x1, h1, gate1, up1 = ffn_fwd(x0, ffn1_norm, gw["ffn1_w_gate"], gw["ffn1_w_up"],
                                 gw["ffn1_w_down"], name="ffn1_fwd")
    proj, h2 = in_proj_fwd(x1, mix_norm, w_in_full, name="in_proj_fwd")
    proj3 = proj.reshape(bsz, t_len, PROJ_W)
    bias_tile = jnp.repeat(sg_b[0].T, SG_GROUP_DIM, axis=1)
    sg_out = sg_fwd(proj, sg_ln_g, sg_ln_b, sg_w[0], bias_tile, name="sg_fwd")
    qkv = dn_conv_fwd(proj3, conv_full, name="dn_conv_fwd")
    alog_row = jnp.zeros((1, LANES), F32).at[0, N_HEADS:2 * N_HEADS].set(a_log[0])
    dtb_row = jnp.zeros((1, LANES), F32).at[0, N_HEADS:2 * N_HEADS].set(dt_bias[0])
    u_wy, w_wy, q_dec, k_dec, qk, tinv, gc = dn_chunk_fwd(qkv, proj3, alog_row, dtb_row,
                                                           name="dn_chunk_fwd")
    o, s_in = dn_scan_fwd(u_wy, w_wy, q_dec, k_dec, qk, gc, name="dn_scan_fwd")
    dn_out = dn_out_fwd(o.reshape(n, HALF_W), proj, dn_norm, name="dn_out_fwd")
    x2 = out_proj_fwd(x1, sg_out, dn_out, w_out_full, name="out_proj_fwd")
    x3, h3, gate2, up2 = ffn_fwd(x2, ffn2_norm, gw["ffn2_w_gate"], gw["ffn2_w_up"],
                                 gw["ffn2_w_down"], name="ffn2_fwd")
    dx3, d_final_norm, loss_tile = final_loss(x3, final_norm.reshape(1, d),
                                              loss_target.reshape(n, d), name="final_loss")
    loss = lax.psum(loss_tile[0, 0], ("x", "y", "c"))

    dx2, dgate2, dup2, act2, dyh2, d_ffn2_norm = ffn_bwd_act(
        dx3, x2, ffn2_norm, gate2, up2, gw["ffn2_w_gate"], gw["ffn2_w_up"], gw["ffn2_w_down"],
        name="ffn2_bwd_act")
    g_big = {}
    g_big["ffn2_w_gate"], g_big["ffn2_w_up"], g_big["ffn2_w_down"] = ffn_bwd_w(
        h3, dyh2, dgate2, dup2, act2, name="ffn2_bwd_w")

    d_sg, d_dn, dx2b = out_proj_bwd_x(dx2, w_out_full, name="out_proj_bwd_x")
    g_w_out = jnp.concatenate([matmul_tn(sg_out, dx2b, d, name="w_out_grad_sg"),
                               matmul_tn(dn_out, dx2b, d, name="w_out_grad_dn")], axis=0)
    g_big["w_out"] = g_w_out.reshape(N_SHARD, (2 * HALF_W) // N_SHARD, d)

    d_proj, d_sg_w, d_bias_tile, d_ln_g, d_ln_b = sg_bwd(d_sg, proj, sg_ln_g, sg_ln_b, sg_w[0],
                                                         bias_tile, name="sg_bwd")
    d_o, d_proj, d_dn_norm = dn_out_bwd(d_dn, o.reshape(n, HALF_W), proj, dn_norm, d_proj,
                                        name="dn_out_bwd")
    du, dw, dqd, dkd, dqk, dgc_scan = dn_scan_bwd(d_o.reshape(bsz, t_len, HALF_W), u_wy, w_wy, q_dec,
                                                  k_dec, qk, gc, s_in, name="dn_scan_bwd")
    d_qkv, d_proj3, d_alog_row, d_dtb_row = dn_chunk_bwd(
        qkv, proj3, alog_row, dtb_row, tinv, u_wy, w_wy, du, dw, dqd, dkd, dqk, dgc_scan,
        d_proj.reshape(bsz, t_len, PROJ_W), name="dn_chunk_bwd")
    d_proj3, d_conv = dn_conv_bwd(d_qkv, proj3, conv_full, d_proj3, name="dn_conv_bwd")
    d_proj = d_proj3.reshape(n, PROJ_W)
    dx1, d_mix_norm = in_proj_bwd_x(d_proj, w_in_full, x1, mix_norm, dx2, name="in_proj_bwd_x")
    g_w_in = matmul_tn(h2, d_proj, 640, name="w_in_grad")[:, :IN_COLS]
    g_big["w_in"] = g_w_in.reshape(d, N_SHARD, IN_COLS // N_SHARD).transpose(1, 0, 2)

    dx0, dgate1, dup1, act1, dyh1, d_ffn1_norm = ffn_bwd_act(
        dx1, x0, ffn1_norm, gate1, up1, gw["ffn1_w_gate"], gw["ffn1_w_up"], gw["ffn1_w_down"],
        name="ffn1_bwd_act")
    g_big["ffn1_w_gate"], g_big["ffn1_w_up"], g_big["ffn1_w_down"] = ffn_bwd_w(
        h1, dyh1, dgate1, dup1, act1, name="ffn1_bwd_w")
    grad_x = dx0.reshape(bsz, t_len, d)

    g_list = [g_big[k] for k in big_names]
    from_sibling = pair_exchange(g_list, name="grad_pair_exchange")
    pair_sums = [pair_add(g, s, c_idx, name="grad_pair_add_" + k)
                 for k, g, s in zip(big_names, g_list, from_sibling)]
    from_chips = chip_exchange(pair_sums, name="grad_chip_exchange")
    halves = [sum_chips(r, p, shard_idx, name="grad_chip_sum_" + k)
              for k, r, p in zip(big_names, from_chips, pair_sums)]
    sib_halves = pair_swap(halves, name="grad_pair_swap")
    outs = {}
    for k, g_mine, g_sib in zip(big_names, halves, sib_halves):
        res = adamw_pair(as2d(big_w[k], k), g_mine, g_sib, as2d(big_m[k], k), as2d(big_v[k], k), c_idx,
                         name="adamw_" + k)
        outs[k] = tuple(from2d(a, k) for a in res)

    small_w = dict(ffn1_norm=ffn1_norm, mix_norm=mix_norm, ffn2_norm=ffn2_norm, final_norm=final_norm,
                   sg_ln_g=sg_ln_g, sg_ln_b=sg_ln_b, dn_norm=dn_norm, a_log=a_log, dt_bias=dt_bias,
                   sg_b=sg_b, sg_w=sg_w)
    small_m = dict(ffn1_norm=m_ffn1_norm, mix_norm=m_mix_norm, ffn2_norm=m_ffn2_norm,
                   final_norm=m_final_norm, sg_ln_g=m_sg_ln_g, sg_ln_b=m_sg_ln_b, dn_norm=m_dn_norm,
                   a_log=m_a_log, dt_bias=m_dt_bias, sg_b=m_sg_b, sg_w=m_sg_w)
    small_v = dict(ffn1_norm=v_ffn1_norm, mix_norm=v_mix_norm, ffn2_norm=v_ffn2_norm,
                   final_norm=v_final_norm, sg_ln_g=v_sg_ln_g, sg_ln_b=v_sg_ln_b, dn_norm=v_dn_norm,
                   a_log=v_a_log, dt_bias=v_dt_bias, sg_b=v_sg_b, sg_w=v_sg_w)
    shapes = {k: small_w[k].shape for k in small_w}
    shapes["conv_w"] = (CONV_K, 3 * HALF_W)
    d_sg_b = d_bias_tile.reshape(SG_CHUNK, SG_GROUPS, SG_GROUP_DIM).sum(axis=-1).T
    small_g = dict(ffn1_norm=d_ffn1_norm, mix_norm=d_mix_norm, ffn2_norm=d_ffn2_norm,
                   final_norm=d_final_norm, sg_ln_g=d_ln_g, sg_ln_b=d_ln_b, dn_norm=d_dn_norm,
                   a_log=d_alog_row[:, N_HEADS:2 * N_HEADS], dt_bias=d_dtb_row[:, N_HEADS:2 * N_HEADS],
                   sg_b=d_sg_b, sg_w=d_sg_w, conv_w=d_conv)
    g_pack = all_reduce_small(_pack_small(small_g), name="small_all_reduce")
    g_small = _unpack_small(g_pack, shapes)
    cw = 3 * HALF_W // N_SHARD
    g_conv = lax.dynamic_slice_in_dim(g_small["conv_w"], shard * cw, cw, axis=1)
    zero_conv = jnp.zeros((CONV_K, 3 * HALF_W), F32)

    def packed(src, conv):
        parts = dict(src)
        parts["conv_w"] = lax.dynamic_update_slice_in_dim(zero_conv, conv[0], shard * cw, axis=1)
        return _pack_small(parts)

    d_pack, m_pack, v_pack = adamw(packed(small_w, conv_w), g_pack, packed(small_m, m_conv_w),
                                   packed(small_v, v_conv_w), name="adamw_small")
    d_small = _unpack_small(d_pack, shapes)
    m_small = _unpack_small(m_pack, shapes)
    v_small = _unpack_small(v_pack, shapes)

    def conv_block(full_arr):
        return lax.dynamic_slice_in_dim(full_arr, shard * cw, cw, axis=1)[None]

    for k in small_w:
        outs[k] = (g_small[k].reshape(small_w[k].shape), d_small[k], m_small[k], v_small[k])
    outs["conv_w"] = (g_conv[None], conv_block(d_small["conv_w"]), conv_block(m_small["conv_w"]),
                      conv_block(v_small["conv_w"]))

    order = ["ffn1_norm", "ffn1_w_gate", "ffn1_w_up", "ffn1_w_down", "mix_norm", "w_in", "conv_w",
             "a_log", "dt_bias", "dn_norm", "sg_ln_g", "sg_ln_b", "sg_w", "sg_b", "w_out", "ffn2_norm",
             "ffn2_w_gate", "ffn2_w_up", "ffn2_w_down", "final_norm"]
    return (loss, grad_x, *[outs[k][0] for k in order], *[outs[k][1] for k in order],
            *[outs[k][2] for k in order], *[outs[k][3] for k in order])
```

```python
import functools

import jax
import jax.numpy as jnp
from jax import lax
from jax.experimental import pallas as pl
from jax.experimental.pallas import tpu as pltpu

F32 = jnp.float32
BF16 = jnp.bfloat16
EPS = 1e-6

D_MODEL = 1024
N_SHARD = 4
HEAD_DIM = 128
N_HEADS = 4
DN_CHUNK = 64
SG_CHUNK = 128
SG_GROUPS = 8
SG_GROUP_DIM = 64
HALF_W = 512
PROJ_W = 3200
IN_COLS = 3080
GATE_COL_BLOCK = 24
QK_SCALE = HEAD_DIM ** -0.5
LANES = 128

ADAM_LR = 0.001
ADAM_B1 = 0.9
ADAM_B2 = 0.999
ADAM_EPS = 1e-08
ADAM_WD = 0.01
ADAM_STEP = 10

VMEM_LIMIT = 56 * 1024 * 1024
ROW_TILE = 512

NN = ((1,), (0,))
NT = ((1,), (1,))
TN = ((0,), (0,))
MESH = pl.DeviceIdType.MESH


def _dot(a, b, dims):
    return lax.dot_general(a, b, (dims, ((), ())), preferred_element_type=F32)


def _bdot(a, b, dims):
    return _dot(a.astype(BF16), b.astype(BF16), dims)


def _split(a):
    hi = a.astype(BF16)
    lo = (a - hi.astype(F32)).astype(BF16)
    return hi, lo


def _dot3(a, b, dims=NN):
    return _dot(a[0], b[0], dims) + (_dot(a[0], b[1], dims) + _dot(a[1], b[0], dims))


def _dot_exact_lhs(a, b):
    ab = a.astype(BF16)
    b1 = b.astype(BF16)
    r1 = b - b1.astype(F32)
    b2 = r1.astype(BF16)
    b3 = (r1 - b2.astype(F32)).astype(BF16)
    return _dot(ab, b1, NN) + (_dot(ab, b2, NN) + _dot(ab, b3, NN))


def _call(body, *, name, out_shape, in_specs, out_specs, grid=(), scratch=(), comm=None, **kw):
    params = dict(vmem_limit_bytes=VMEM_LIMIT)
    if grid:
        params["dimension_semantics"] = ("arbitrary",) * len(grid)
    if comm is None:
        return pl.pallas_call(
            body, name=name, grid=grid, in_specs=in_specs, out_specs=out_specs,
            out_shape=out_shape, scratch_shapes=list(scratch),
            compiler_params=pltpu.CompilerParams(**params), **kw)

    n_in, n_out, n_sc = len(in_specs), len(out_specs), len(scratch)
    c_in, c_out = len(comm.inputs), len(comm.out_shape)
    steps = 1
    for g in grid:
        steps *= g

    def hosted(*refs):
        ins, cins = refs[:n_in], refs[n_in:n_in + c_in]
        o0 = n_in + c_in
        outs, couts = refs[o0:o0 + n_out], refs[o0 + n_out:o0 + n_out + c_out]
        s0 = o0 + n_out + c_out
        sc, csems = refs[s0:s0 + n_sc], refs[s0 + n_sc:]
        lin = 0
        for axis, g in enumerate(grid):
            lin = lin * g + pl.program_id(axis)

        def at(step, fn):
            @pl.when(lin == step % steps)
            def _():
                fn(cins, couts, csems)

        for step, fn in comm.phases:
            if step >= 0:
                at(step, fn)
        body(*ins, *outs, *sc)
        for step, fn in comm.phases:
            if step < 0:
                at(step, fn)

    aliases = dict(kw.pop("input_output_aliases", {}))
    for k, m in comm.aliases.items():
        aliases[n_in + k] = n_out + m
    call = pl.pallas_call(
        hosted, name=name, grid=grid, in_specs=list(in_specs) + [_ANY] * c_in,
        out_specs=list(out_specs) + [_ANY] * c_out, out_shape=list(out_shape) + comm.out_shape,
        scratch_shapes=list(scratch) + comm.sems, input_output_aliases=aliases,
        compiler_params=pltpu.CompilerParams(**params), **kw)

    def run(*args):
        res = call(*args, *comm.inputs)
        return res[:n_out], res[n_out:]

    return run


def _sds(shape, dtype):
    return jax.ShapeDtypeStruct(tuple(shape), dtype)


def _resident(shape):
    zeros = (0,) * len(shape)
    return pl.BlockSpec(tuple(shape), lambda *_: zeros, pipeline_mode=pl.Buffered(1))


def _sigmoid(x):
    return jax.nn.sigmoid(x)


def _softplus(x):
    return jnp.maximum(x, 0.0) + jnp.log(1.0 + jnp.exp(-jnp.abs(x)))


_GELU_C = 0.7978845608028654
_GELU_A = 0.044715


def _gelu(x):
    t = jnp.tanh(_GELU_C * (x + _GELU_A * x * x * x))
    return 0.5 * x * (1.0 + t)


def _gelu_grad(x):
    t = jnp.tanh(_GELU_C * (x + _GELU_A * x * x * x))
    return 0.5 * (1.0 + t) + 0.5 * x * (1.0 - t * t) * _GELU_C * (1.0 + 3.0 * _GELU_A * x * x)


def _silu_grad(x):
    s = _sigmoid(x)
    return s * (1.0 + x * (1.0 - s))


def _rms_scale(xv):
    return lax.rsqrt(jnp.mean(xv * xv, axis=-1, keepdims=True) + EPS)


def _rms_bwd(dh, xv, g):
    r = _rms_scale(xv)
    xn = xv * r
    dg = jnp.sum(dh * xn, axis=0, keepdims=True)
    dxn = dh * g
    dx = r * (dxn - xn * jnp.mean(dxn * xn, axis=-1, keepdims=True))
    return dx, dg


def _iota2(shape, dim):
    return lax.broadcasted_iota(jnp.int32, shape, dim)


def _col_to_row(col):
    n = col.shape[0]
    eye = _iota2((n, n), 0) == _iota2((n, n), 1)
    return jnp.sum(jnp.where(eye, col, 0.0), axis=0, keepdims=True)


def _row_to_col(row):
    n = row.shape[1]
    eye = _iota2((n, n), 0) == _iota2((n, n), 1)
    return jnp.sum(jnp.where(eye, row, 0.0), axis=1, keepdims=True)


def ffn_fwd(x, gnorm, wg, wu, wd, name, comm=None):
    n, d = x.shape
    nb, fb, _ = wg.shape
    tm = min(ROW_TILE, n)

    def body(x_ref, g_ref, wg_ref, wu_ref, wd_ref, xo_ref, h_ref, gate_ref, up_ref, acc_ref):
        xv = x_ref[...]
        h = (xv * _rms_scale(xv) * g_ref[...]).astype(BF16)
        h_ref[...] = h
        for j in range(nb):
            gate = _dot(h, wg_ref[j], NT)
            up = _dot(h, wu_ref[j], NT)
            gate_ref[j] = gate.astype(BF16)
            up_ref[j] = up.astype(BF16)
            part = _dot((gate * _sigmoid(gate) * up).astype(BF16), wd_ref[j], NN)
            if j == 0:
                acc_ref[...] = part
            else:
                acc_ref[...] += part
        xo_ref[...] = xv + 0.5 * acc_ref[...]

    row = pl.BlockSpec((tm, d), lambda i: (i, 0))
    blk = pl.BlockSpec((nb, tm, fb), lambda i: (0, i, 0))
    return _call(
        body, name=name, grid=(n // tm,),
        in_specs=[row, pl.BlockSpec((1, d), lambda i: (0, 0))] + [_resident((nb, fb, d))] * 3,
        out_specs=[row, row, blk, blk],
        out_shape=[_sds((n, d), F32), _sds((n, d), BF16),
                   _sds((nb, n, fb), BF16), _sds((nb, n, fb), BF16)],
        scratch=[pltpu.VMEM((tm, d), F32)], comm=comm,
    )(x, gnorm, wg, wu, wd)


def ffn_bwd_act(dy, x, gnorm, gate, up, wg, wu, wd, name):
    n, d = x.shape
    nb, fb, _ = wg.shape
    tm = min(ROW_TILE // 2, n)

    def body(dy_ref, x_ref, g_ref, gate_ref, up_ref, wg_ref, wu_ref, wd_ref,
             dx_ref, dgate_ref, dup_ref, act_ref, dyh_ref, dg_ref, acc_ref):
        @pl.when(pl.program_id(0) == 0)
        def _():
            dg_ref[...] = jnp.zeros_like(dg_ref)

        dyh = (0.5 * dy_ref[...]).astype(BF16)
        dyh_ref[...] = dyh
        for j in range(nb):
            dact = _dot(dyh, wd_ref[j], NT)
            gt = gate_ref[j].astype(F32)
            u = up_ref[j].astype(F32)
            s = _sigmoid(gt)
            silu = gt * s
            dup = (dact * silu).astype(BF16)
            dgate = (dact * u * (s * (1.0 + gt * (1.0 - s)))).astype(BF16)
            dup_ref[j] = dup
            dgate_ref[j] = dgate
            act_ref[j] = (silu * u).astype(BF16)
            part = _dot(dgate, wg_ref[j], NN) + _dot(dup, wu_ref[j], NN)
            if j == 0:
                acc_ref[...] = part
            else:
                acc_ref[...] += part
        dxn, dg = _rms_bwd(acc_ref[...], x_ref[...], g_ref[...])
        dx_ref[...] = dy_ref[...] + dxn
        dg_ref[...] += dg

    row = pl.BlockSpec((tm, d), lambda i: (i, 0))
    blk = pl.BlockSpec((nb, tm, fb), lambda i: (0, i, 0))
    vec = pl.BlockSpec((1, d), lambda i: (0, 0))
    wblk = _resident((nb, fb, d))
    return _call(
        body, name=name, grid=(n // tm,),
        in_specs=[row, row, vec, blk, blk, wblk, wblk, wblk],
        out_specs=[row, blk, blk, blk, row, vec],
        out_shape=[_sds((n, d), F32), _sds((nb, n, fb), BF16), _sds((nb, n, fb), BF16),
                   _sds((nb, n, fb), BF16), _sds((n, d), BF16), _sds((1, d), F32)],
        scratch=[pltpu.VMEM((tm, d), F32)],
    )(dy, x, gnorm, gate, up, wg, wu, wd)


def ffn_bwd_w(h, dyh, dgate, dup, act, name):
    n, d = h.shape
    nb, _, fb = dgate.shape
    tk = min(2 * ROW_TILE, n)

    def body(h_ref, dyh_ref, dgate_ref, dup_ref, act_ref, dwg_ref, dwu_ref, dwd_ref):
        @pl.when(pl.program_id(1) == 0)
        def _():
            dwg_ref[...] = jnp.zeros_like(dwg_ref)
            dwu_ref[...] = jnp.zeros_like(dwu_ref)
            dwd_ref[...] = jnp.zeros_like(dwd_ref)

        hv = h_ref[...]
        dwg_ref[0] += _dot(dgate_ref[0], hv, TN)
        dwu_ref[0] += _dot(dup_ref[0], hv, TN)
        dwd_ref[0] += _dot(act_ref[0], dyh_ref[...], TN)

    row = pl.BlockSpec((tk, d), lambda j, k: (k, 0))
    blk = pl.BlockSpec((1, tk, fb), lambda j, k: (j, k, 0))
    return _call(
        body, name=name, grid=(nb, n // tk),
        in_specs=[row, row, blk, blk, blk],
        out_specs=[pl.BlockSpec((1, fb, d), lambda j, k: (j, 0, 0))] * 3,
        out_shape=[_sds((nb, fb, d), F32)] * 3,
    )(h, dyh, dgate, dup, act)


def final_loss(x, gnorm, target, name):
    n, d = x.shape
    tm = min(ROW_TILE, n)

    def body(x_ref, g_ref, t_ref, dx_ref, dg_ref, loss_ref):
        @pl.when(pl.program_id(0) == 0)
        def _():
            dg_ref[...] = jnp.zeros_like(dg_ref)
            loss_ref[...] = jnp.zeros_like(loss_ref)

        xv = x_ref[...]
        y = xv * _rms_scale(xv) * g_ref[...]
        err = y - t_ref[...]
        part = 0.5 * jnp.sum(jnp.mean(err * err, axis=-1, keepdims=True), axis=0, keepdims=True)
        loss_ref[...] += jnp.broadcast_to(part, loss_ref.shape)
        dx, dg = _rms_bwd(err * (1.0 / d), xv, g_ref[...])
        dx_ref[...] = dx
        dg_ref[...] += dg

    row = pl.BlockSpec((tm, d), lambda i: (i, 0))
    vec = pl.BlockSpec((1, d), lambda i: (0, 0))
    return _call(
        body, name=name, grid=(n // tm,),
        in_specs=[row, vec, row],
        out_specs=[row, vec, pl.BlockSpec((1, LANES), lambda i: (0, 0))],
        out_shape=[_sds((n, d), F32), _sds((1, d), F32), _sds((1, LANES), F32)],
    )(x, gnorm, target)


def in_proj_fwd(x, gnorm, w, name):
    n, d = x.shape
    cols = w.shape[1]
    tm = min(ROW_TILE, n)
    tn = 640

    def body(x_ref, g_ref, w_ref, p_ref, h_ref):
        xv = x_ref[...]
        h = (xv * _rms_scale(xv) * g_ref[...]).astype(BF16)
        h_ref[...] = h
        for c0 in range(0, cols, tn):
            p_ref[:, c0:c0 + tn] = _dot(h, w_ref[:, c0:c0 + tn], NN)

    return _call(
        body, name=name, grid=(n // tm,),
        in_specs=[pl.BlockSpec((tm, d), lambda i: (i, 0)),
                  pl.BlockSpec((1, d), lambda i: (0, 0)), _resident((d, cols))],
        out_specs=[pl.BlockSpec((tm, cols), lambda i: (i, 0)),
                   pl.BlockSpec((tm, d), lambda i: (i, 0))],
        out_shape=[_sds((n, cols), F32), _sds((n, d), BF16)],
    )(x, gnorm, w)


def in_proj_bwd_x(dproj, w, x, gnorm, dres, name):
    n, d = x.shape
    cols = w.shape[1]
    tm = min(ROW_TILE, n)

    def body(dp_ref, w_ref, x_ref, g_ref, dr_ref, dx_ref, dg_ref):
        @pl.when(pl.program_id(0) == 0)
        def _():
            dg_ref[...] = jnp.zeros_like(dg_ref)

        dh = _dot(dp_ref[...], w_ref[...], NT)
        dxn, dg = _rms_bwd(dh, x_ref[...], g_ref[...])
        dx_ref[...] = dr_ref[...] + dxn
        dg_ref[...] += dg

    row = pl.BlockSpec((tm, d), lambda i: (i, 0))
    vec = pl.BlockSpec((1, d), lambda i: (0, 0))
    return _call(
        body, name=name, grid=(n // tm,),
        in_specs=[pl.BlockSpec((tm, cols), lambda i: (i, 0)),
                  _resident((d, cols)), row, vec, row],
        out_specs=[row, vec],
        out_shape=[_sds((n, d), F32), _sds((1, d), F32)],
    )(dproj, w, x, gnorm, dres)


def matmul_tn(a, b, tn, name):
    n, ka = a.shape
    cb = b.shape[1]
    tk = min(ROW_TILE, n)

    def body(a_ref, b_ref, o_ref):
        @pl.when(pl.program_id(0) == 0)
        def _():
            o_ref[...] = jnp.zeros_like(o_ref)

        av = a_ref[...]
        for c0 in range(0, cb, tn):
            o_ref[:, c0:c0 + tn] += _dot(av, b_ref[:, c0:c0 + tn], TN)

    return _call(
        body, name=name, grid=(n // tk,),
        in_specs=[pl.BlockSpec((tk, ka), lambda k: (k, 0)),
                  pl.BlockSpec((tk, cb), lambda k: (k, 0))],
        out_specs=pl.BlockSpec((ka, cb), lambda k: (0, 0)),
        out_shape=_sds((ka, cb), F32),
    )(a, b)


def out_proj_fwd(x, sg_out, dn_out, w, name):
    n, d = x.shape
    tm = min(ROW_TILE, n)

    def body(x_ref, a_ref, b_ref, w_ref, o_ref):
        o_ref[...] = (x_ref[...] + _dot(a_ref[...], w_ref[0:HALF_W, :], NN)
                      + _dot(b_ref[...], w_ref[HALF_W:2 * HALF_W, :], NN))

    row = pl.BlockSpec((tm, d), lambda i: (i, 0))
    half = pl.BlockSpec((tm, HALF_W), lambda i: (i, 0))
    return _call(
        body, name=name, grid=(n // tm,),
        in_specs=[row, half, half, pl.BlockSpec((2 * HALF_W, d), lambda i: (0, 0))],
        out_specs=row, out_shape=_sds((n, d), F32),
    )(x, sg_out, dn_out, w)


def out_proj_bwd_x(dy, w, name, comm=None):
    n, d = dy.shape
    tm = min(ROW_TILE, n)

    def body(dy_ref, w_ref, dsg_ref, ddn_ref, dyb_ref):
        dyb = dy_ref[...].astype(BF16)
        dyb_ref[...] = dyb
        dsg_ref[...] = _dot(dyb, w_ref[0:HALF_W, :], NT)
        ddn_ref[...] = _dot(dyb, w_ref[HALF_W:2 * HALF_W, :], NT)

    row = pl.BlockSpec((tm, d), lambda i: (i, 0))
    half = pl.BlockSpec((tm, HALF_W), lambda i: (i, 0))
    return _call(
        body, name=name, grid=(n // tm,),
        in_specs=[row, pl.BlockSpec((2 * HALF_W, d), lambda i: (0, 0))],
        out_specs=[half, half, row],
        out_shape=[_sds((n, HALF_W), F32), _sds((n, HALF_W), F32), _sds((n, d), BF16)], comm=comm,
    )(dy, w)


def _sg_group_masks():
    col = _iota2((SG_CHUNK, HALF_W), 1)
    return [jnp.logical_and(col >= g * SG_GROUP_DIM, col < (g + 1) * SG_GROUP_DIM)
            for g in range(SG_GROUPS)]


def _sg_causal():
    return _iota2((SG_CHUNK, SG_CHUNK), 0) >= _iota2((SG_CHUNK, SG_CHUNK), 1)


def _sg_forward_chunk(pu, pv, ln_g, ln_b, wc, bias, masks):
    u = _gelu(pu)
    v = _gelu(pv)
    mu = jnp.mean(v, axis=-1, keepdims=True)
    vc = v - mu
    rs = lax.rsqrt(jnp.mean(vc * vc, axis=-1, keepdims=True) + EPS)
    xhat = vc * rs
    vn = (xhat * ln_g + ln_b).astype(BF16)
    vs = bias
    for g in range(SG_GROUPS):
        vs = vs + jnp.where(masks[g], _dot(wc[g], vn, NN), 0.0)
    return u, xhat, rs, vn, vs


def sg_fwd(proj, ln_g, ln_b, w_s, bias_tile, name):
    n = proj.shape[0]
    tm = min(ROW_TILE, n)

    def body(pu_ref, pv_ref, g_ref, b_ref, w_ref, bias_ref, o_ref):
        causal = _sg_causal()
        wc = [jnp.where(causal, w_ref[g], 0.0).astype(BF16) for g in range(SG_GROUPS)]
        masks = _sg_group_masks()
        for ci in range(tm // SG_CHUNK):
            rows = slice(ci * SG_CHUNK, (ci + 1) * SG_CHUNK)
            u, _, _, _, vs = _sg_forward_chunk(pu_ref[rows, :], pv_ref[rows, :], g_ref[...],
                                               b_ref[...], wc, bias_ref[...], masks)
            o_ref[rows, :] = (u * vs).astype(BF16)

    vec = pl.BlockSpec((1, HALF_W), lambda i: (0, 0))
    return _call(
        body, name=name, grid=(n // tm,),
        in_specs=[pl.BlockSpec((tm, HALF_W), lambda i: (i, 0)),
                  pl.BlockSpec((tm, HALF_W), lambda i: (i, 1)), vec, vec,
                  pl.BlockSpec((SG_GROUPS, SG_CHUNK, SG_CHUNK), lambda i: (0, 0, 0)),
                  pl.BlockSpec((SG_CHUNK, HALF_W), lambda i: (0, 0))],
        out_specs=pl.BlockSpec((tm, HALF_W), lambda i: (i, 0)),
        out_shape=_sds((n, HALF_W), BF16),
    )(proj, proj, ln_g, ln_b, w_s, bias_tile)


def sg_bwd(dsg, proj, ln_g, ln_b, w_s, bias_tile, name):
    n = proj.shape[0]
    tm = min(ROW_TILE, n)

    def body(d_ref, pu_ref, pv_ref, g_ref, b_ref, w_ref, bias_ref,
             dp_ref, dw_ref, db_ref, dlg_ref, dlb_ref):
        @pl.when(pl.program_id(0) == 0)
        def _():
            dw_ref[...] = jnp.zeros_like(dw_ref)
            db_ref[...] = jnp.zeros_like(db_ref)
            dlg_ref[...] = jnp.zeros_like(dlg_ref)
            dlb_ref[...] = jnp.zeros_like(dlb_ref)

        causal = _sg_causal()
        wc = [jnp.where(causal, w_ref[g], 0.0).astype(BF16) for g in range(SG_GROUPS)]
        masks = _sg_group_masks()
        ln_g_v = g_ref[...]
        for ci in range(tm // SG_CHUNK):
            rows = slice(ci * SG_CHUNK, (ci + 1) * SG_CHUNK)
            pu = pu_ref[rows, :]
            pv = pv_ref[rows, :]
            u, xhat, rs, vn, vs = _sg_forward_chunk(pu, pv, ln_g_v, b_ref[...], wc,
                                                    bias_ref[...], masks)
            dout = d_ref[rows, :]
            dp_ref[rows, 0:HALF_W] = (dout * vs * _gelu_grad(pu)).astype(BF16)
            dvs = dout * u
            dvs_b = dvs.astype(BF16)
            db_ref[...] += dvs
            dvn = jnp.zeros_like(dvs)
            for g in range(SG_GROUPS):
                dvn = dvn + jnp.where(masks[g], _dot(wc[g], dvs_b, TN), 0.0)
                dwg = _dot(jnp.where(masks[g], dvs_b, jnp.zeros_like(dvs_b)), vn, NT)
                dw_ref[g] += jnp.where(causal, dwg, 0.0)
            dlg_ref[...] += jnp.sum(dvn * xhat, axis=0, keepdims=True)
            dlb_ref[...] += jnp.sum(dvn, axis=0, keepdims=True)
            dxh = dvn * ln_g_v
            dv = rs * (dxh - jnp.mean(dxh, axis=-1, keepdims=True)
                       - xhat * jnp.mean(dxh * xhat, axis=-1, keepdims=True))
            dp_ref[rows, HALF_W:2 * HALF_W] = (dv * _gelu_grad(pv)).astype(BF16)

    vec = pl.BlockSpec((1, HALF_W), lambda i: (0, 0))
    wspec = pl.BlockSpec((SG_GROUPS, SG_CHUNK, SG_CHUNK), lambda i: (0, 0, 0))
    tile = pl.BlockSpec((SG_CHUNK, HALF_W), lambda i: (0, 0))
    return _call(
        body, name=name, grid=(n // tm,),
        in_specs=[pl.BlockSpec((tm, HALF_W), lambda i: (i, 0)),
                  pl.BlockSpec((tm, HALF_W), lambda i: (i, 0)),
                  pl.BlockSpec((tm, HALF_W), lambda i: (i, 1)), vec, vec, wspec, tile],
        out_specs=[pl.BlockSpec((tm, 2 * HALF_W), lambda i: (i, 0)), wspec, tile, vec, vec],
        out_shape=[_sds((n, PROJ_W), BF16), _sds((SG_GROUPS, SG_CHUNK, SG_CHUNK), F32),
                   _sds((SG_CHUNK, HALF_W), F32), _sds((1, HALF_W), F32), _sds((1, HALF_W), F32)],
    )(dsg, proj, proj, ln_g, ln_b, w_s, bias_tile)


CONV_K = 4
CONV_BLOCK = 256


def _shift_down(x, s):
    if s == 0:
        return x
    rolled = pltpu.roll(x, s, 0)
    return jnp.where(_iota2(x.shape, 0) >= s, rolled, 0.0)


def _shift_up(x, s):
    if s == 0:
        return x
    t_len = x.shape[0]
    rolled = pltpu.roll(x, t_len - s, 0)
    return jnp.where(_iota2(x.shape, 0) < t_len - s, rolled, 0.0)


def _conv(x, w):
    y = _shift_down(x, CONV_K - 1) * w[0:1, :]
    for j in range(1, CONV_K):
        y = y + _shift_down(x, CONV_K - 1 - j) * w[j:j + 1, :]
    return y


def dn_conv_fwd(proj3, conv_w, name):
    b, t, _ = proj3.shape
    nblk = 3 * HALF_W // CONV_BLOCK
    first = 2 * HALF_W // CONV_BLOCK
    n_norm = 2 * HALF_W // CONV_BLOCK

    def body(x_ref, w_ref, o_ref):
        s = pl.program_id(1)
        y = _conv(x_ref[0], w_ref[...])
        y = y * _sigmoid(y)

        @pl.when(s < n_norm)
        def _():
            for h in range(CONV_BLOCK // HEAD_DIM):
                cs = slice(h * HEAD_DIM, (h + 1) * HEAD_DIM)
                yh = y[:, cs]
                o_ref[0, :, cs] = yh * lax.rsqrt(jnp.sum(yh * yh, axis=-1, keepdims=True) + EPS)

        @pl.when(s >= n_norm)
        def _():
            o_ref[0] = y

    return _call(
        body, name=name, grid=(b, nblk),
        in_specs=[pl.BlockSpec((1, t, CONV_BLOCK), lambda i, s: (i, 0, first + s)),
                  pl.BlockSpec((CONV_K, CONV_BLOCK), lambda i, s: (0, s))],
        out_specs=pl.BlockSpec((1, t, CONV_BLOCK), lambda i, s: (i, 0, s)),
        out_shape=_sds((b, t, 3 * HALF_W), F32),
    )(proj3, conv_w)


def dn_conv_bwd(dqkv, proj3, conv_w, dproj3, name, comm=None):
    b, t, _ = proj3.shape
    nblk = 3 * HALF_W // CONV_BLOCK
    first = 2 * HALF_W // CONV_BLOCK
    n_norm = 2 * HALF_W // CONV_BLOCK

    def body(d_ref, x_ref, w_ref, dproj_in, dx_ref, dw_ref, ds_ref):
        s = pl.program_id(0)

        @pl.when(pl.program_id(1) == 0)
        def _():
            dw_ref[...] = jnp.zeros_like(dw_ref)

        x = x_ref[0]
        w = w_ref[...]
        c = _conv(x, w)
        sg = _sigmoid(c)
        y = c * sg

        @pl.when(s < n_norm)
        def _():
            for h in range(CONV_BLOCK // HEAD_DIM):
                cs = slice(h * HEAD_DIM, (h + 1) * HEAD_DIM)
                yh = y[:, cs]
                r = lax.rsqrt(jnp.sum(yh * yh, axis=-1, keepdims=True) + EPS)
                nh = yh * r
                dn = d_ref[0, :, cs]
                ds_ref[:, cs] = r * (dn - nh * jnp.sum(dn * nh, axis=-1, keepdims=True))

        @pl.when(s >= n_norm)
        def _():
            ds_ref[...] = d_ref[0]

        dc = ds_ref[...] * (sg * (1.0 + c * (1.0 - sg)))
        dx = _shift_up(dc, CONV_K - 1) * w[0:1, :]
        for j in range(1, CONV_K):
            dx = dx + _shift_up(dc, CONV_K - 1 - j) * w[j:j + 1, :]
        dx_ref[0] = dx.astype(BF16)
        for j in range(CONV_K):
            dw_ref[j:j + 1, :] += jnp.sum(dc * _shift_down(x, CONV_K - 1 - j), axis=0, keepdims=True)

    return _call(
        body, name=name, grid=(nblk, b),
        in_specs=[pl.BlockSpec((1, t, CONV_BLOCK), lambda s, i: (i, 0, s)),
                  pl.BlockSpec((1, t, CONV_BLOCK), lambda s, i: (i, 0, first + s)),
                  pl.BlockSpec((CONV_K, CONV_BLOCK), lambda s, i: (0, s)), _ANY],
        out_specs=[pl.BlockSpec((1, t, CONV_BLOCK), lambda s, i: (i, 0, first + s)),
                   pl.BlockSpec((CONV_K, CONV_BLOCK), lambda s, i: (0, s))],
        out_shape=[_sds(dproj3.shape, BF16), _sds((CONV_K, 3 * HALF_W), F32)],
        scratch=[pltpu.VMEM((t, CONV_BLOCK), F32)],
        input_output_aliases={3: 0}, comm=comm,
    )(dqkv, proj3, conv_w, dproj3)


def _chunk_masks():
    ii = _iota2((DN_CHUNK, DN_CHUNK), 0)
    jj = _iota2((DN_CHUNK, DN_CHUNK), 1)
    return ii >= jj, ii > jj, ii == jj


LOCKSTEP_CHUNKS = 2


def _inv_unit_lower_many(l_mats, eye):
    eye_f = jnp.where(eye, 1.0, 0.0)
    ps = [-l for l in l_mats]
    ts = [eye_f + p for p in ps]
    pss = [_split(p) for p in ps]
    size = 2
    while size < DN_CHUNK:
        ps = [_dot3(s, s) for s in pss]
        pss = [_split(p) for p in ps]
        ts = [t + _dot3(_split(t), s) for t, s in zip(ts, pss)]
        size *= 2
    return ts


def _gates(pba, ea_row, dtb_row):
    beta = _sigmoid(pba)
    g = -ea_row * _softplus(pba + dtb_row)
    return beta, g


def _chunk_decay(gcol):
    incl, strict, eye = _chunk_masks()
    grow = jnp.sum(jnp.where(eye, gcol, 0.0), axis=0, keepdims=True)
    decay = jnp.where(incl, jnp.exp(jnp.where(incl, gcol - grow, 0.0)), 0.0)
    return decay, incl, strict, eye


def dn_chunk_fwd(qkv, proj3, alog_row, dtb_row, name):
    b, t, _ = qkv.shape
    rblk = min(256, t)
    n_in = rblk // DN_CHUNK

    def body(q_ref, k_ref, v_ref, pba_ref, al_ref, dtb_ref,
             u_ref, w_ref, qd_ref, kd_ref, qk_ref, ti_ref, gc_ref):
        ea = jnp.exp(al_ref[...])
        tri = jnp.where(_chunk_masks()[0], 1.0, 0.0)

        _, strict, eye = _chunk_masks()

        def chunk_group(cg, carry):
            items = []
            for sub in range(LOCKSTEP_CHUNKS):
                rows = pl.ds(pl.multiple_of((cg * LOCKSTEP_CHUNKS + sub) * DN_CHUNK, DN_CHUNK), DN_CHUNK)
                beta_all, g_all = _gates(pba_ref[0, rows, :], ea, dtb_ref[...])
                gc = _dot_exact_lhs(tri, g_all)
                gc_ref[0, rows, :] = gc
                for h in range(N_HEADS):
                    items.append((rows, h, beta_all[:, h:h + 1], gc[:, N_HEADS + h:N_HEADS + h + 1]))
            ks, kbs, decays, egs = [], [], [], []
            for rows, h, beta, gcol in items:
                cs = slice(h * HEAD_DIM, (h + 1) * HEAD_DIM)
                k = k_ref[0, rows, cs]
                ks.append(k)
                kbs.append(k * beta)
                decays.append(_chunk_decay(gcol)[0])
                egs.append(jnp.exp(gcol))
            ms = [_bdot(kb, k, NT) for kb, k in zip(kbs, ks)]
            tinvs = _inv_unit_lower_many([jnp.where(strict, m * dc, 0.0) for m, dc in zip(ms, decays)], eye)
            tsps = [_split(t) for t in tinvs]
            for (rows, h, beta, gcol), tsp, tinv in zip(items, tsps, tinvs):
                cs = slice(h * HEAD_DIM, (h + 1) * HEAD_DIM)
                u_ref[0, rows, cs] = _dot3(tsp, _split(v_ref[0, rows, cs] * beta))
                ti_ref[0, h, rows, :] = tinv
            for (rows, h, beta, gcol), tsp, kb, eg in zip(items, tsps, kbs, egs):
                cs = slice(h * HEAD_DIM, (h + 1) * HEAD_DIM)
                w_ref[0, rows, cs] = _dot3(tsp, _split(kb * eg))
            for (rows, h, beta, gcol), k, dc, eg in zip(items, ks, decays, egs):
                cs = slice(h * HEAD_DIM, (h + 1) * HEAD_DIM)
                q = q_ref[0, rows, cs] * QK_SCALE
                qk_ref[0, h, rows, :] = _bdot(q, k, NT) * dc
                qd_ref[0, rows, cs] = q * eg
                kd_ref[0, rows, cs] = k * jnp.exp(gcol[DN_CHUNK - 1:DN_CHUNK, :] - gcol)
            return carry

        lax.fori_loop(0, n_in // LOCKSTEP_CHUNKS, chunk_group, 0)

    def seg(cblk):
        return pl.BlockSpec((1, rblk, HALF_W), lambda i, r: (i, r, cblk))

    vec = pl.BlockSpec((1, LANES), lambda i, r: (0, 0))
    wide = pl.BlockSpec((1, rblk, HALF_W), lambda i, r: (i, r, 0))
    sq = pl.BlockSpec((1, N_HEADS, rblk, DN_CHUNK), lambda i, r: (i, 0, r, 0))
    return _call(
        body, name=name, grid=(b, t // rblk),
        in_specs=[seg(0), seg(1), seg(2),
                  pl.BlockSpec((1, rblk, LANES), lambda i, r: (i, r, GATE_COL_BLOCK)), vec, vec],
        out_specs=[wide, wide, wide, wide, sq, sq,
                   pl.BlockSpec((1, rblk, LANES), lambda i, r: (i, r, 0))],
        out_shape=[_sds((b, t, HALF_W), F32)] * 4
        + [_sds((b, N_HEADS, t, DN_CHUNK), F32)] * 2 + [_sds((b, t, LANES), F32)],
    )(qkv, qkv, qkv, proj3, alog_row, dtb_row)


def dn_scan_fwd(u, w, qd, kd, qk, gc, name):
    b, t, _ = u.shape
    nc = t // DN_CHUNK
    bh = b * N_HEADS

    def body(u_ref, w_ref, qd_ref, kd_ref, qk_ref, gc_ref, o_ref, sin_ref, s_ref):
        @pl.when(pl.program_id(0) == 0)
        def _():
            s_ref[...] = jnp.zeros_like(s_ref)

        items = [(bi, h, slice(h * HEAD_DIM, (h + 1) * HEAD_DIM)) for bi in range(b) for h in range(N_HEADS)]
        sbs = []
        for bi, h, cs in items:
            s = s_ref[bi * N_HEADS + h]
            sin_ref[0, bi * N_HEADS + h] = s
            sbs.append(s.astype(BF16))
        ws = [_bdot(w_ref[bi, :, cs], sb, NN) for (bi, h, cs), sb in zip(items, sbs)]
        qs = [_bdot(qd_ref[bi, :, cs], sb, NN) for (bi, h, cs), sb in zip(items, sbs)]
        vbs = [(u_ref[bi, :, cs] - wsi).astype(BF16) for (bi, h, cs), wsi in zip(items, ws)]
        for (bi, h, cs), qsi, vb in zip(items, qs, vbs):
            o_ref[bi, :, cs] = qsi + _bdot(qk_ref[bi, h], vb, NN)
        for (bi, h, cs), vb in zip(items, vbs):
            gl = jnp.exp(gc_ref[bi, DN_CHUNK - 1:DN_CHUNK, N_HEADS + h:N_HEADS + h + 1])
            idx = bi * N_HEADS + h
            s_ref[idx] = s_ref[idx] * gl + _bdot(kd_ref[bi, :, cs], vb, TN)

    wide = pl.BlockSpec((b, DN_CHUNK, HALF_W), lambda c: (0, c, 0))
    return _call(
        body, name=name, grid=(nc,),
        in_specs=[wide, wide, wide, wide,
                  pl.BlockSpec((b, N_HEADS, DN_CHUNK, DN_CHUNK), lambda c: (0, 0, c, 0)),
                  pl.BlockSpec((b, DN_CHUNK, LANES), lambda c: (0, c, 0))],
        out_specs=[wide, pl.BlockSpec((1, bh, HEAD_DIM, HEAD_DIM), lambda c: (c, 0, 0, 0))],
        out_shape=[_sds((b, t, HALF_W), F32), _sds((nc, bh, HEAD_DIM, HEAD_DIM), F32)],
        scratch=[pltpu.VMEM((bh, HEAD_DIM, HEAD_DIM), F32)],
    )(u, w, qd, kd, qk, gc)


def dn_scan_bwd(do, u, w, qd, kd, qk, gc, s_in, name):
    b, t, _ = u.shape
    nc = t // DN_CHUNK
    bh = b * N_HEADS

    def body(do_ref, u_ref, w_ref, qd_ref, kd_ref, qk_ref, gc_ref, sin_ref,
             du_ref, dw_ref, dqd_ref, dkd_ref, dqk_ref, dgc_ref, ds_ref):
        @pl.when(pl.program_id(0) == 0)
        def _():
            ds_ref[...] = jnp.zeros_like(ds_ref)

        last_row = _iota2((DN_CHUNK, LANES), 0) == DN_CHUNK - 1
        lane = _iota2((DN_CHUNK, LANES), 1)
        items = [(bi, h, slice(h * HEAD_DIM, (h + 1) * HEAD_DIM)) for bi in range(b) for h in range(N_HEADS)]
        sbs = [sin_ref[0, bi * N_HEADS + h].astype(BF16) for bi, h, cs in items]
        wvs = [w_ref[bi, :, cs].astype(BF16) for bi, h, cs in items]
        dovs = [do_ref[bi, :, cs].astype(BF16) for bi, h, cs in items]
        dsbs = [ds_ref[bi * N_HEADS + h].astype(BF16) for bi, h, cs in items]
        vbs = [(u_ref[bi, :, cs] - _dot(wv, sb, NN)).astype(BF16)
               for (bi, h, cs), wv, sb in zip(items, wvs, sbs)]
        for (bi, h, cs), dov, sb in zip(items, dovs, sbs):
            dqd_ref[bi, :, cs] = _dot(dov, sb, NT)
        dvns = [_dot(kd_ref[bi, :, cs].astype(BF16), dsb, NN) + _dot(qk_ref[bi, h].astype(BF16), dov, TN)
                for (bi, h, cs), dsb, dov in zip(items, dsbs, dovs)]
        for (bi, h, cs), vb, dsb, dov in zip(items, vbs, dsbs, dovs):
            dkd_ref[bi, :, cs] = _dot(vb, dsb, NT)
            dqk_ref[bi, h] = _dot(dov, vb, NT)
        dgls = []
        for (bi, h, cs), dvn, sb, wv, dov in zip(items, dvns, sbs, wvs, dovs):
            idx = bi * N_HEADS + h
            du_ref[bi, :, cs] = dvn
            dvn_b = dvn.astype(BF16)
            dw_ref[bi, :, cs] = -_dot(dvn_b, sb, NT)
            gl = jnp.exp(gc_ref[bi, DN_CHUNK - 1:DN_CHUNK, N_HEADS + h:N_HEADS + h + 1])
            ds = ds_ref[idx]
            dgl = jnp.sum(jnp.sum(ds * sin_ref[0, idx], axis=1, keepdims=True), axis=0, keepdims=True)
            dgls.append(dgl * gl)
            ds_ref[idx] = (ds * gl + _dot(qd_ref[bi, :, cs].astype(BF16), dov, TN)
                           - _dot(wv, dvn_b, TN))
        for bi in range(b):
            dgc = jnp.zeros((DN_CHUNK, LANES), F32)
            for h in range(N_HEADS):
                dgc = dgc + jnp.where(jnp.logical_and(last_row, lane == N_HEADS + h),
                                      dgls[bi * N_HEADS + h], 0.0)
            dgc_ref[bi] = dgc

    def rev(c):
        return nc - 1 - c

    wide = pl.BlockSpec((b, DN_CHUNK, HALF_W), lambda c: (0, rev(c), 0))
    sq = pl.BlockSpec((b, N_HEADS, DN_CHUNK, DN_CHUNK), lambda c: (0, 0, rev(c), 0))
    gates = pl.BlockSpec((b, DN_CHUNK, LANES), lambda c: (0, rev(c), 0))
    return _call(
        body, name=name, grid=(nc,),
        in_specs=[wide, wide, wide, wide, wide, sq, gates,
                  pl.BlockSpec((1, bh, HEAD_DIM, HEAD_DIM), lambda c: (rev(c), 0, 0, 0))],
        out_specs=[wide, wide, wide, wide, sq, gates],
        out_shape=[_sds((b, t, HALF_W), F32)] * 4
        + [_sds((b, N_HEADS, t, DN_CHUNK), F32), _sds((b, t, LANES), F32)],
        scratch=[pltpu.VMEM((bh, HEAD_DIM, HEAD_DIM), F32)],
    )(do, u, w, qd, kd, qk, gc, s_in)


def dn_chunk_bwd(qkv, proj3, alog_row, dtb_row, tinv, u, w, du, dw, dqd, dkd, dqk, dgc_scan, dproj3, name,
                 comm=None):
    b, t, _ = qkv.shape
    rblk = min(256, t)
    n_in = rblk // DN_CHUNK

    def body(q_ref, k_ref, v_ref, pba_ref, al_ref, dtb_ref, ti_ref, u_ref, w_ref,
             du_ref, dw_ref, dqd_ref, dkd_ref, dqk_ref, dgs_ref, dproj_in,
             dq_ref, dpba_ref, dal_ref, ddtb_ref):
        @pl.when(jnp.logical_and(pl.program_id(0) == 0, pl.program_id(1) == 0))
        def _():
            dal_ref[...] = jnp.zeros_like(dal_ref)
            ddtb_ref[...] = jnp.zeros_like(ddtb_ref)

        ea = jnp.exp(al_ref[...])
        incl0 = _chunk_masks()[0]
        tri = jnp.where(incl0, 1.0, 0.0)
        tri_up = jnp.where(_iota2((DN_CHUNK, DN_CHUNK), 1) >= _iota2((DN_CHUNK, DN_CHUNK), 0), 1.0, 0.0)
        lane = _iota2((DN_CHUNK, LANES), 1)
        last_col = _iota2((DN_CHUNK, 1), 0) == DN_CHUNK - 1

        _, strict, _ = _chunk_masks()
        gate_lane = jnp.logical_and(lane >= N_HEADS, lane < 2 * N_HEADS)

        def chunk_group(cg, carry):
            tiles, items = [], []
            for sub in range(LOCKSTEP_CHUNKS):
                rows = pl.ds(pl.multiple_of((cg * LOCKSTEP_CHUNKS + sub) * DN_CHUNK, DN_CHUNK), DN_CHUNK)
                pba = pba_ref[0, rows, :]
                beta_all, g_all = _gates(pba, ea, dtb_ref[...])
                gc = _dot_exact_lhs(tri, g_all)
                tiles.append((rows, pba, beta_all, g_all))
                for h in range(N_HEADS):
                    items.append((sub, rows, h, slice(h * HEAD_DIM, (h + 1) * HEAD_DIM),
                                  beta_all[:, h:h + 1], gc[:, N_HEADS + h:N_HEADS + h + 1]))
            decays = [_chunk_decay(gcol)[0] for _, _, _, _, _, gcol in items]
            egs = [jnp.exp(gcol) for _, _, _, _, _, gcol in items]
            qbs = [(q_ref[0, rows, cs] * QK_SCALE).astype(BF16) for _, rows, h, cs, _, _ in items]
            kfs = [k_ref[0, rows, cs].astype(BF16) for _, rows, h, cs, _, _ in items]
            kbs = [k_ref[0, rows, cs] * beta for _, rows, h, cs, beta, _ in items]
            kbbs = [kb.astype(BF16) for kb in kbs]
            tsps = [_split(ti_ref[0, h, rows, :]) for _, rows, h, cs, _, _ in items]
            drus = [_dot3(tsp, _split(du_ref[0, rows, cs]), TN)
                    for (_, rows, h, cs, _, _), tsp in zip(items, tsps)]
            drws = [_dot3(tsp, _split(dw_ref[0, rows, cs]), TN)
                    for (_, rows, h, cs, _, _), tsp in zip(items, tsps)]
            m_kks = [_dot(kbb, kf, NT) for kbb, kf in zip(kbbs, kfs)]
            a_qks = [_dot(qb, kf, NT) for qb, kf in zip(qbs, kfs)]
            dls = [-jnp.where(strict, _dot3(_split(dru), _split(u_ref[0, rows, cs]), NT)
                              + _dot3(_split(drw), _split(w_ref[0, rows, cs]), NT), 0.0)
                   for (_, rows, h, cs, _, _), dru, drw in zip(items, drus, drws)]
            dms = [(dl * dc).astype(BF16) for dl, dc in zip(dls, decays)]
            das = [(dqk_ref[0, h, rows, :] * dc).astype(BF16)
                   for (_, rows, h, cs, _, _), dc in zip(items, decays)]
            dkb_mm = [_dot(dm, kf, NN) for dm, kf in zip(dms, kfs)]
            dk_mm = [_dot(dm, kbb, TN) + _dot(da, qb, TN) for dm, kbb, da, qb in zip(dms, kbbs, das, qbs)]
            dqs_mm = [_dot(da, kf, NN) for da, kf in zip(das, kfs)]
            dgc_tiles = [dgs_ref[0, rows, :] for rows, _, _, _ in tiles]
            dbeta_tiles = [jnp.zeros((DN_CHUNK, LANES), F32) for _ in tiles]
            for n_it, (sub, rows, h, cs, beta, gcol) in enumerate(items):
                eg, dc = egs[n_it], decays[n_it]
                k = k_ref[0, rows, cs]
                q = q_ref[0, rows, cs] * QK_SCALE
                kb, dru, drw = kbs[n_it], drus[n_it], drws[n_it]
                ek = jnp.exp(gcol[DN_CHUNK - 1:DN_CHUNK, :] - gcol)
                e_mat = (dls[n_it] * m_kks[n_it] + dqk_ref[0, h, rows, :] * a_qks[n_it]) * dc
                dkb = drw * eg + dkb_mm[n_it]
                dg = (jnp.sum(drw * kb * eg, axis=-1, keepdims=True)
                      + jnp.sum(e_mat, axis=1, keepdims=True)
                      - _row_to_col(jnp.sum(e_mat, axis=0, keepdims=True)))
                dqd = dqd_ref[0, rows, cs]
                dg = dg + jnp.sum(dqd * q * eg, axis=-1, keepdims=True)
                dkd = dkd_ref[0, rows, cs]
                tk_ = jnp.sum(dkd * k * ek, axis=-1, keepdims=True)
                dg = dg - tk_ + jnp.where(last_col, jnp.sum(tk_, axis=0, keepdims=True), 0.0)
                dbeta = (jnp.sum(dkb * k, axis=-1, keepdims=True)
                         + jnp.sum(dru * v_ref[0, rows, cs], axis=-1, keepdims=True))
                dq_ref[0, rows, cs] = (dqs_mm[n_it] + dqd * eg) * QK_SCALE
                dq_ref[0, rows, pl.ds(HALF_W + h * HEAD_DIM, HEAD_DIM)] = dk_mm[n_it] + dkd * ek + dkb * beta
                dq_ref[0, rows, pl.ds(2 * HALF_W + h * HEAD_DIM, HEAD_DIM)] = dru * beta
                dgc_tiles[sub] = dgc_tiles[sub] + jnp.where(lane == N_HEADS + h, dg, 0.0)
                dbeta_tiles[sub] = dbeta_tiles[sub] + jnp.where(lane == h, dbeta, 0.0)
            for (rows, pba, beta_all, g_all), dgc_tile, dbeta_tile in zip(tiles, dgc_tiles, dbeta_tiles):
                dg_tile = _dot_exact_lhs(tri_up, dgc_tile)
                da_pre = dg_tile * (-ea) * _sigmoid(pba + dtb_ref[...])
                dal_ref[...] += jnp.sum(jnp.where(gate_lane, dg_tile * g_all, 0.0), axis=0, keepdims=True)
                ddtb_ref[...] += jnp.sum(jnp.where(gate_lane, da_pre, 0.0), axis=0, keepdims=True)
                dpba_ref[0, rows, :] = jnp.where(lane < N_HEADS, dbeta_tile * beta_all * (1.0 - beta_all),
                                                 jnp.where(gate_lane, da_pre, 0.0)).astype(BF16)
            return carry

        lax.fori_loop(0, n_in // LOCKSTEP_CHUNKS, chunk_group, 0)

    def seg(cblk):
        return pl.BlockSpec((1, rblk, HALF_W), lambda i, r: (i, r, cblk))

    vec = pl.BlockSpec((1, LANES), lambda i, r: (0, 0))
    wide = pl.BlockSpec((1, rblk, HALF_W), lambda i, r: (i, r, 0))
    sq = pl.BlockSpec((1, N_HEADS, rblk, DN_CHUNK), lambda i, r: (i, 0, r, 0))
    gates = pl.BlockSpec((1, rblk, LANES), lambda i, r: (i, r, 0))
    return _call(
        body, name=name, grid=(b, t // rblk),
        in_specs=[seg(0), seg(1), seg(2),
                  pl.BlockSpec((1, rblk, LANES), lambda i, r: (i, r, GATE_COL_BLOCK)), vec, vec,
                  sq, wide, wide, wide, wide, wide, wide, sq, gates, _ANY],
        out_specs=[pl.BlockSpec((1, rblk, 3 * HALF_W), lambda i, r: (i, r, 0)),
                   pl.BlockSpec((1, rblk, LANES), lambda i, r: (i, r, GATE_COL_BLOCK)), vec, vec],
        out_shape=[_sds((b, t, 3 * HALF_W), F32), _sds(dproj3.shape, BF16),
                   _sds((1, LANES), F32), _sds((1, LANES), F32)],
        input_output_aliases={15: 1}, comm=comm,
    )(qkv, qkv, qkv, proj3, alog_row, dtb_row, tinv, u, w, du, dw, dqd, dkd, dqk, dgc_scan, dproj3)


def dn_out_fwd(o, proj, dn_norm, name):
    n = o.shape[0]
    tm = min(ROW_TILE, n)

    def body(o_ref, z_ref, g_ref, y_ref):
        for h in range(N_HEADS):
            cs = slice(h * HEAD_DIM, (h + 1) * HEAD_DIM)
            oh = o_ref[:, cs]
            z = z_ref[:, cs]
            y = oh * _rms_scale(oh) * g_ref[...]
            y_ref[:, cs] = (y * (z * _sigmoid(z))).astype(BF16)

    half = pl.BlockSpec((tm, HALF_W), lambda i: (i, 0))
    return _call(
        body, name=name, grid=(n // tm,),
        in_specs=[half, pl.BlockSpec((tm, HALF_W), lambda i: (i, 5)),
                  pl.BlockSpec((1, HEAD_DIM), lambda i: (0, 0))],
        out_specs=half, out_shape=_sds((n, HALF_W), BF16),
    )(o, proj, dn_norm)


def dn_out_bwd(dy, o, proj, dn_norm, dproj, name):
    n = o.shape[0]
    tm = min(ROW_TILE, n)

    def body(dy_ref, o_ref, z_ref, g_ref, dproj_in, do_ref, dz_ref, dg_ref):
        @pl.when(pl.program_id(0) == 0)
        def _():
            dg_ref[...] = jnp.zeros_like(dg_ref)

        g = g_ref[...]
        dg = jnp.zeros_like(g)
        for h in range(N_HEADS):
            cs = slice(h * HEAD_DIM, (h + 1) * HEAD_DIM)
            oh = o_ref[:, cs]
            z = z_ref[:, cs]
            d = dy_ref[:, cs]
            r = _rms_scale(oh)
            nh = oh * r
            sz = _sigmoid(z)
            dyn = d * (z * sz)
            dz_ref[:, cs] = (d * (nh * g) * (sz * (1.0 + z * (1.0 - sz)))).astype(BF16)
            dg = dg + jnp.sum(dyn * nh, axis=0, keepdims=True)
            dn = dyn * g
            do_ref[:, cs] = r * (dn - nh * jnp.mean(dn * nh, axis=-1, keepdims=True))
        dg_ref[...] += dg

    half = pl.BlockSpec((tm, HALF_W), lambda i: (i, 0))
    vec = pl.BlockSpec((1, HEAD_DIM), lambda i: (0, 0))
    return _call(
        body, name=name, grid=(n // tm,),
        in_specs=[half, half, pl.BlockSpec((tm, HALF_W), lambda i: (i, 5)), vec, _ANY],
        out_specs=[half, pl.BlockSpec((tm, HALF_W), lambda i: (i, 5)), vec],
        out_shape=[_sds((n, HALF_W), F32), _sds(dproj.shape, BF16), _sds((1, HEAD_DIM), F32)],
        input_output_aliases={4: 1},
    )(dy, o, proj, dn_norm, dproj)


def _adamw_math(w, g, m, v):
    m_new = ADAM_B1 * m + (1.0 - ADAM_B1) * g
    v_new = ADAM_B2 * v + (1.0 - ADAM_B2) * (g * g)
    m_hat = m_new / (1.0 - ADAM_B1 ** ADAM_STEP)
    v_hat = v_new / (1.0 - ADAM_B2 ** ADAM_STEP)
    delta = -ADAM_LR * (m_hat / (jnp.sqrt(v_hat) + ADAM_EPS) + ADAM_WD * w)
    return delta, m_new, v_new


def adamw(w, g, m, v, name):
    r, c = w.shape
    tr = r
    for cand in (256, 352):
        if r % cand == 0 and r > cand:
            tr = cand
            break

    def body(w_ref, g_ref, m_ref, v_ref, d_ref, mo_ref, vo_ref):
        d, mn, vn = _adamw_math(w_ref[...], g_ref[...], m_ref[...], v_ref[...])
        d_ref[...] = d
        mo_ref[...] = mn
        vo_ref[...] = vn

    spec = pl.BlockSpec((tr, c), lambda i: (i, 0))
    return _call(
        body, name=name, grid=(r // tr,),
        in_specs=[spec] * 4, out_specs=[spec] * 3, out_shape=[_sds((r, c), F32)] * 3,
    )(w, g, m, v)


def _place():
    return lax.axis_index("x"), lax.axis_index("y"), lax.axis_index("c")


def _other_chips(x, y):
    return [(1 - x, y), (x, 1 - y), (1 - x, 1 - y)]


_ANY = pl.BlockSpec(memory_space=pl.ANY)


def cast_place(w, shard_idx, name):
    r, cols = w.shape
    tr = r // 2

    def body(j_ref, w_ref, o_ref):
        o_ref[0] = w_ref[...].astype(BF16)

    return pl.pallas_call(
        body, name=name,
        grid_spec=pltpu.PrefetchScalarGridSpec(
            num_scalar_prefetch=1, grid=(r // tr,),
            in_specs=[pl.BlockSpec((tr, cols), lambda i, j: (i, 0))],
            out_specs=pl.BlockSpec((1, tr, cols), lambda i, j: (j[0], i, 0))),
        out_shape=_sds((N_SHARD, r, cols), BF16),
        compiler_params=pltpu.CompilerParams(dimension_semantics=("arbitrary",),
                                             vmem_limit_bytes=VMEM_LIMIT),
    )(shard_idx, w)


class Exchange:
    def __init__(self, inputs, out_shape, aliases, sems, phases):
        self.inputs, self.out_shape, self.aliases = list(inputs), list(out_shape), dict(aliases)
        self.sems, self.phases = list(sems), list(phases)


def run_exchange(ex, name):
    def body(*refs):
        n_in, n_out = len(ex.inputs), len(ex.out_shape)
        for _, fn in ex.phases:
            fn(refs[:n_in], refs[n_in:n_in + n_out], refs[n_in + n_out:])

    return _call(body, name=name, in_specs=[_ANY] * len(ex.inputs), out_specs=[_ANY] * len(ex.out_shape),
                 out_shape=ex.out_shape, scratch=ex.sems, input_output_aliases=ex.aliases)(*ex.inputs)


def _dma_sems(*sizes):
    return [pltpu.SemaphoreType.DMA((s,)) for s in sizes]


def gather_exchange(bufs, small=None, relay_step=-2):
    n = len(bufs)
    n_small = 0 if small is None else 1

    def half(outs, a, blk, hc):
        rh = bufs[a].shape[1] // 2
        return outs[a].at[blk, pl.ds(hc * rh, rh), :]

    def ici(outs, sems, a, k, blk, to):
        return pltpu.make_async_remote_copy(
            src_ref=half(outs, a, blk, to[2]), dst_ref=half(outs, a, blk, to[2]), send_sem=sems[0].at[3 * a + k],
            recv_sem=sems[1].at[3 * a + k], device_id=to, device_id_type=MESH)

    def d2d(outs, sems, a, k, blk, hc, to):
        return pltpu.make_async_remote_copy(
            src_ref=half(outs, a, blk, hc), dst_ref=half(outs, a, blk, hc), send_sem=sems[2].at[3 * a + k],
            recv_sem=sems[3].at[3 * a + k], device_id=to, device_id_type=MESH)

    def small_copy(ins, outs, sems, k, blk, to):
        return pltpu.make_async_remote_copy(
            src_ref=ins[n], dst_ref=outs[n].at[blk], send_sem=sems[0].at[3 * n + k],
            recv_sem=sems[1].at[3 * n + k], device_id=to, device_id_type=MESH)

    def start(ins, outs, sems):
        x, y, c = _place()
        j = 2 * x + y
        if n_small:
            pltpu.make_async_copy(ins[n], outs[n].at[j], sems[4].at[0]).start()
        for k, (px, py) in enumerate(_other_chips(x, y)):
            if n_small:
                small_copy(ins, outs, sems, k, j, (px, py, c)).start()
            for a in range(n):
                ici(outs, sems, a, k, j, (px, py, c)).start()

    def relay(ins, outs, sems):
        x, y, c = _place()
        for k, (px, py) in enumerate(_other_chips(x, y)):
            for a in range(n):
                ici(outs, sems, a, k, 2 * px + py, (px, py, c)).wait_recv()
                d2d(outs, sems, a, k, 2 * px + py, c, (x, y, 1 - c)).start()

    def finish(ins, outs, sems):
        x, y, c = _place()
        j = 2 * x + y
        for k, (px, py) in enumerate(_other_chips(x, y)):
            blk = 2 * px + py
            if n_small:
                small_copy(ins, outs, sems, k, blk, (px, py, c)).wait_recv()
                small_copy(ins, outs, sems, k, j, (px, py, c)).wait_send()
            for a in range(n):
                d2d(outs, sems, a, k, blk, 1 - c, (x, y, 1 - c)).wait_recv()
                ici(outs, sems, a, k, j, (px, py, c)).wait_send()
                d2d(outs, sems, a, k, blk, c, (x, y, 1 - c)).wait_send()
        if n_small:
            pltpu.make_async_copy(ins[n], outs[n].at[j], sems[4].at[0]).wait()

    out_shape = [_sds(b.shape, b.dtype) for b in bufs]
    if n_small:
        out_shape.append(_sds((N_SHARD,) + small.shape, small.dtype))
    return Exchange(list(bufs) + ([small] if n_small else []), out_shape, {a: a for a in range(n)},
                    _dma_sems(3 * n + 3, 3 * n + 3, 3 * n, 3 * n, 1),
                    [(0, start), (relay_step, relay), (-1, finish)])


def _start_then_wait(copies):
    def start(ins, outs, sems):
        for sent, _ in copies(ins, outs, sems):
            sent().start()

    def finish(ins, outs, sems):
        pairs = copies(ins, outs, sems)
        for _, arrival in pairs:
            arrival().wait_recv()
        for sent, _ in pairs:
            sent().wait_send()

    return [(0, start), (-1, finish)]


def pair_exchange(arrs):
    n = len(arrs)

    def copies(ins, outs, sems):
        x, y, c = _place()
        res = []
        for a in range(n):
            def mk(a=a):
                rh = arrs[a].shape[1] // 2
                return pltpu.make_async_remote_copy(
                    src_ref=ins[a].at[:, pl.ds((1 - c) * rh, rh), :], dst_ref=outs[a], send_sem=sems[0].at[a],
                    recv_sem=sems[1].at[a], device_id=(x, y, 1 - c), device_id_type=MESH)
            res.append((mk, mk))
        return res

    return Exchange(arrs, [_sds((a.shape[0], a.shape[1] // 2, a.shape[2]), a.dtype) for a in arrs], {},
                    _dma_sems(n, n), _start_then_wait(copies))


def pair_add(g, s, c_idx, name):
    nb, r, cols = g.shape
    rh = r // 2

    def body(c_ref, g_ref, s_ref, o_ref):
        o_ref[...] = (g_ref[...] + s_ref[...]).astype(BF16)

    return pl.pallas_call(
        body, name=name,
        grid_spec=pltpu.PrefetchScalarGridSpec(
            num_scalar_prefetch=1, grid=(nb,),
            in_specs=[pl.BlockSpec((1, rh, cols), lambda j, c: (j, c[0], 0)),
                      pl.BlockSpec((1, rh, cols), lambda j, c: (j, 0, 0))],
            out_specs=pl.BlockSpec((1, rh, cols), lambda j, c: (j, 0, 0))),
        out_shape=_sds((nb, rh, cols), BF16),
        compiler_params=pltpu.CompilerParams(dimension_semantics=("arbitrary",),
                                             vmem_limit_bytes=VMEM_LIMIT),
    )(c_idx, g, s)


def chip_exchange(arrs):
    n = len(arrs)

    def copies(ins, outs, sems):
        x, y, c = _place()
        j = 2 * x + y
        res = []
        for a in range(n):
            for k, (px, py) in enumerate(_other_chips(x, y)):
                def mk(src_blk, dst_blk, a=a, k=k, to=(px, py, c)):
                    return pltpu.make_async_remote_copy(
                        src_ref=ins[a].at[src_blk], dst_ref=outs[a].at[dst_blk], send_sem=sems[0].at[3 * a + k],
                        recv_sem=sems[1].at[3 * a + k], device_id=to, device_id_type=MESH)
                res.append((functools.partial(mk, 2 * px + py, j), functools.partial(mk, j, 2 * px + py)))
        return res

    return Exchange(arrs, [_sds(a.shape, a.dtype) for a in arrs], {}, _dma_sems(3 * n, 3 * n),
                    _start_then_wait(copies))


def sum_chips(r, p, shard_idx, name):
    nb, rh, cols = r.shape
    tr = rh // 2

    def body(j_ref, p_ref, *refs):
        o_ref = refs[nb]
        j = j_ref[0]
        acc = None
        for i in range(nb):
            term = jnp.where(j == i, p_ref[0], refs[i][0]).astype(F32)
            acc = term if acc is None else acc + term
        o_ref[...] = acc

    def slot(i):
        return pl.BlockSpec((1, tr, cols), lambda t, j: (jnp.where(j[0] == i, (i + 1) % nb, i), t, 0))

    return pl.pallas_call(
        body, name=name,
        grid_spec=pltpu.PrefetchScalarGridSpec(
            num_scalar_prefetch=1, grid=(rh // tr,),
            in_specs=[pl.BlockSpec((1, tr, cols), lambda t, j: (j[0], t, 0))] + [slot(i) for i in range(nb)],
            out_specs=pl.BlockSpec((tr, cols), lambda t, j: (t, 0))),
        out_shape=_sds((rh, cols), F32),
        compiler_params=pltpu.CompilerParams(dimension_semantics=("arbitrary",),
                                             vmem_limit_bytes=VMEM_LIMIT),
    )(shard_idx, p, *([r] * nb))


def pair_swap(arrs):
    n = len(arrs)

    def copies(ins, outs, sems):
        x, y, c = _place()
        res = []
        for a in range(n):
            def mk(a=a):
                return pltpu.make_async_remote_copy(
                    src_ref=ins[a], dst_ref=outs[a], send_sem=sems[0].at[a], recv_sem=sems[1].at[a],
                    device_id=(x, y, 1 - c), device_id_type=MESH)
            res.append((mk, mk))
        return res

    return Exchange(arrs, [_sds(a.shape, a.dtype) for a in arrs], {}, _dma_sems(n, n),
                    _start_then_wait(copies))


def adamw_pair(w, g_mine, g_sib, m, v, c_idx, name):
    r, cols = w.shape
    rh = r // 2
    tr = rh // 2
    nh = rh // tr

    def body(c_ref, w_ref, gm_ref, gs_ref, m_ref, v_ref, g_ref, d_ref, mo_ref, vo_ref):
        mine = (pl.program_id(0) // nh) == c_ref[0]
        g = jnp.where(mine, gm_ref[...], gs_ref[...])
        d, mn, vn = _adamw_math(w_ref[...], g, m_ref[...], v_ref[...])
        g_ref[...] = g
        d_ref[...] = d
        mo_ref[...] = mn
        vo_ref[...] = vn

    full = pl.BlockSpec((tr, cols), lambda i, c: (i, 0))
    part = pl.BlockSpec((tr, cols), lambda i, c: (i % nh, 0))
    return pl.pallas_call(
        body, name=name,
        grid_spec=pltpu.PrefetchScalarGridSpec(
            num_scalar_prefetch=1, grid=(r // tr,),
            in_specs=[full, part, part, full, full], out_specs=[full] * 4),
        out_shape=[_sds((r, cols), F32)] * 4,
        compiler_params=pltpu.CompilerParams(dimension_semantics=("arbitrary",),
                                             vmem_limit_bytes=VMEM_LIMIT),
    )(c_idx, w, g_mine, g_sib, m, v)


N_DEV = 8


def all_reduce_small(pack, name):
    r, cols = pack.shape

    def body(in_ref, out_ref, buf, send, recv):
        x, y, c = _place()
        me = 4 * x + 2 * y + c
        buf[me] = in_ref[...]
        peers = []
        for k in range(1, N_DEV):
            fx, fy, fc = (k >> 2) & 1, (k >> 1) & 1, k & 1
            peers.append((1 - x if fx else x, 1 - y if fy else y, 1 - c if fc else c))
        sends = [pltpu.make_async_remote_copy(
            src_ref=in_ref, dst_ref=buf.at[me], send_sem=send.at[k], recv_sem=recv.at[k],
            device_id=p, device_id_type=MESH) for k, p in enumerate(peers)]
        for cp in sends:
            cp.start()
        for k, (px, py, pc) in enumerate(peers):
            pltpu.make_async_remote_copy(
                src_ref=in_ref, dst_ref=buf.at[4 * px + 2 * py + pc], send_sem=send.at[k],
                recv_sem=recv.at[k], device_id=(px, py, pc), device_id_type=MESH).wait_recv()
        for cp in sends:
            cp.wait_send()
        acc = buf[0] + buf[1]
        for i in range(2, N_DEV):
            acc = acc + buf[i]
        out_ref[...] = acc

    vm = pl.BlockSpec(memory_space=pltpu.VMEM)
    return _call(
        body, name=name, in_specs=[vm], out_specs=vm, out_shape=_sds((r, cols), F32),
        scratch=[pltpu.VMEM((N_DEV, r, cols), F32), pltpu.SemaphoreType.DMA((N_DEV - 1,)),
                 pltpu.SemaphoreType.DMA((N_DEV - 1,))],
    )(pack)


SMALL_NAMES = ("ffn1_norm", "mix_norm", "ffn2_norm", "final_norm", "sg_ln_g", "sg_ln_b",
               "dn_norm", "a_log", "dt_bias", "sg_b", "sg_w", "conv_w")


def _to_rows(a):
    flat = a.reshape(-1)
    pad = (-flat.shape[0]) % LANES
    if pad:
        flat = jnp.pad(flat, (0, pad))
    return flat.reshape(-1, LANES)


def _pack_small(parts):
    rows = [_to_rows(parts[k]) for k in SMALL_NAMES]
    pack = jnp.concatenate(rows, axis=0)
    pad = (-pack.shape[0]) % 8
    if pad:
        pack = jnp.pad(pack, ((0, pad), (0, 0)))
    return pack


def _unpack_small(pack, shapes):
    out, r0 = {}, 0
    for k in SMALL_NAMES:
        size = 1
        for s in shapes[k]:
            size *= s
        nrows = -(-size // LANES)
        out[k] = pack[r0:r0 + nrows].reshape(-1)[:size].reshape(shapes[k])
        r0 += nrows
    return out


def kernel(x, ffn1_norm, ffn1_w_gate, ffn1_w_up, ffn1_w_down, mix_norm, w_in, conv_w, a_log, dt_bias, dn_norm, sg_ln_g, sg_ln_b, sg_w, sg_b, w_out, ffn2_norm, ffn2_w_gate, ffn2_w_up, ffn2_w_down, final_norm, loss_target, m_ffn1_norm, m_ffn1_w_gate, m_ffn1_w_up, m_ffn1_w_down, m_mix_norm, m_w_in, m_conv_w, m_a_log, m_dt_bias, m_dn_norm, m_sg_ln_g, m_sg_ln_b, m_sg_w, m_sg_b, m_w_out, m_ffn2_norm, m_ffn2_w_gate, m_ffn2_w_up, m_ffn2_w_down, m_final_norm, v_ffn1_norm, v_ffn1_w_gate, v_ffn1_w_up, v_ffn1_w_down, v_mix_norm, v_w_in, v_conv_w, v_a_log, v_dt_bias, v_dn_norm, v_sg_ln_g, v_sg_ln_b, v_sg_w, v_sg_b, v_w_out, v_ffn2_norm, v_ffn2_w_gate, v_ffn2_w_up, v_ffn2_w_down, v_final_norm):
    bsz, t_len, d = x.shape
    n = bsz * t_len
    xy, yy, cc = _place()
    shard = 2 * xy + yy

    big_names = ["ffn1_w_gate", "ffn1_w_up", "ffn1_w_down", "w_in", "w_out",
                 "ffn2_w_gate", "ffn2_w_up", "ffn2_w_down"]
    big_w = dict(ffn1_w_gate=ffn1_w_gate, ffn1_w_up=ffn1_w_up, ffn1_w_down=ffn1_w_down, w_in=w_in,
                 w_out=w_out, ffn2_w_gate=ffn2_w_gate, ffn2_w_up=ffn2_w_up, ffn2_w_down=ffn2_w_down)
    big_m = dict(ffn1_w_gate=m_ffn1_w_gate, ffn1_w_up=m_ffn1_w_up, ffn1_w_down=m_ffn1_w_down, w_in=m_w_in,
                 w_out=m_w_out, ffn2_w_gate=m_ffn2_w_gate, ffn2_w_up=m_ffn2_w_up, ffn2_w_down=m_ffn2_w_down)
    big_v = dict(ffn1_w_gate=v_ffn1_w_gate, ffn1_w_up=v_ffn1_w_up, ffn1_w_down=v_ffn1_w_down, w_in=v_w_in,
                 w_out=v_w_out, ffn2_w_gate=v_ffn2_w_gate, ffn2_w_up=v_ffn2_w_up, ffn2_w_down=v_ffn2_w_down)
    shard_idx = jnp.reshape(shard, (1,)).astype(jnp.int32)
    c_idx = jnp.reshape(cc, (1,)).astype(jnp.int32)
    transposed = ("ffn1_w_gate", "ffn1_w_up", "ffn2_w_gate", "ffn2_w_up")

    def as2d(a, k):
        return a[0].T if k in transposed else a[0]

    def from2d(a, k):
        return a.T[None] if k in transposed else a[None]

    placed = {k: cast_place(as2d(big_w[k], k), shard_idx, name="cast_" + k) for k in big_names}
    first_names = big_names[:3]
    later_names = big_names[3:]
    res = run_exchange(gather_exchange([placed[k] for k in first_names], conv_w[0]), name="gather_first")
    gw = dict(zip(first_names, res[:3]))
    conv_full = res[3].transpose(1, 0, 2).reshape(CONV_K, 3 * HALF_W)

    x0 = x.reshape(n, d)
    (x1, h1, gate1, up1), later = ffn_fwd(
        x0, ffn1_norm, gw["ffn1_w_gate"], gw["ffn1_w_up"], gw["ffn1_w_down"], name="ffn1_fwd",
        comm=gather_exchange([placed[k] for k in later_names]))
    gw.update(zip(later_names, later))
    w_in_full = gw["w_in"].transpose(1, 0, 2).reshape(d, IN_COLS)
    w_in_full = jnp.pad(w_in_full, ((0, 0), (0, PROJ_W - IN_COLS)))
    w_out_full = gw["w_out"].reshape(2 * HALF_W, d)
    proj, h2 = in_proj_fwd(x1, mix_norm, w_in_full, name="in_proj_fwd")
    proj3 = proj.reshape(bsz, t_len, PROJ_W)
    bias_tile = jnp.repeat(sg_b[0].T, SG_GROUP_DIM, axis=1)
    sg_out = sg_fwd(proj, sg_ln_g, sg_ln_b, sg_w[0], bias_tile, name="sg_fwd")
    qkv = dn_conv_fwd(proj3, conv_full, name="dn_conv_fwd")
    alog_row = jnp.zeros((1, LANES), F32).at[0, N_HEADS:2 * N_HEADS].set(a_log[0])
    dtb_row = jnp.zeros((1, LANES), F32).at[0, N_HEADS:2 * N_HEADS].set(dt_bias[0])
    u_wy, w_wy, q_dec, k_dec, qk, tinv, gc = dn_chunk_fwd(qkv, proj3, alog_row, dtb_row,
                                                           name="dn_chunk_fwd")
    o, s_in = dn_scan_fwd(u_wy, w_wy, q_dec, k_dec, qk, gc, name="dn_scan_fwd")
    dn_out = dn_out_fwd(o.reshape(n, HALF_W), proj, dn_norm, name="dn_out_fwd")
    x2 = out_proj_fwd(x1, sg_out, dn_out, w_out_full, name="out_proj_fwd")
    x3, h3, gate2, up2 = ffn_fwd(x2, ffn2_norm, gw["ffn2_w_gate"], gw["ffn2_w_up"],
                                 gw["ffn2_w_down"], name="ffn2_fwd")
    dx3, d_final_norm, loss_tile = final_loss(x3, final_norm.reshape(1, d),
                                              loss_target.reshape(n, d), name="final_loss")
    loss = lax.psum(loss_tile[0, 0], ("x", "y", "c"))

    dx2, dgate2, dup2, act2, dyh2, d_ffn2_norm = ffn_bwd_act(
        dx3, x2, ffn2_norm, gate2, up2, gw["ffn2_w_gate"], gw["ffn2_w_up"], gw["ffn2_w_down"],
        name="ffn2_bwd_act")
    g_big = {}
    g_big["ffn2_w_gate"], g_big["ffn2_w_up"], g_big["ffn2_w_down"] = ffn_bwd_w(
        h3, dyh2, dgate2, dup2, act2, name="ffn2_bwd_w")

    early = ["ffn2_w_gate", "ffn2_w_up", "ffn2_w_down"]
    (d_sg, d_dn, dx2b), early_sib = out_proj_bwd_x(dx2, w_out_full, name="out_proj_bwd_x",
                                                   comm=pair_exchange([g_big[k] for k in early]))
    early_sums = [pair_add(g_big[k], s, c_idx, name="grad_pair_add_" + k) for k, s in zip(early, early_sib)]
    g_w_out = jnp.concatenate([matmul_tn(sg_out, dx2b, d, name="w_out_grad_sg"),
                               matmul_tn(dn_out, dx2b, d, name="w_out_grad_dn")], axis=0)
    g_big["w_out"] = g_w_out.reshape(N_SHARD, (2 * HALF_W) // N_SHARD, d)

    d_proj, d_sg_w, d_bias_tile, d_ln_g, d_ln_b = sg_bwd(d_sg, proj, sg_ln_g, sg_ln_b, sg_w[0],
                                                         bias_tile, name="sg_bwd")
    d_o, d_proj, d_dn_norm = dn_out_bwd(d_dn, o.reshape(n, HALF_W), proj, dn_norm, d_proj,
                                        name="dn_out_bwd")
    du, dw, dqd, dkd, dqk, dgc_scan = dn_scan_bwd(d_o.reshape(bsz, t_len, HALF_W), u_wy, w_wy, q_dec,
                                                  k_dec, qk, gc, s_in, name="dn_scan_bwd")
    (d_qkv, d_proj3, d_alog_row, d_dtb_row), early_chips = dn_chunk_bwd(
        qkv, proj3, alog_row, dtb_row, tinv, u_wy, w_wy, du, dw, dqd, dkd, dqk, dgc_scan,
        d_proj.reshape(bsz, t_len, PROJ_W), name="dn_chunk_bwd", comm=chip_exchange(early_sums))
    early_halves = [sum_chips(r, p, shard_idx, name="grad_chip_sum_" + k)
                    for k, r, p in zip(early, early_chips, early_sums)]
    (d_proj3, d_conv), early_sib_halves = dn_conv_bwd(d_qkv, proj3, conv_full, d_proj3, name="dn_conv_bwd",
                                                      comm=pair_swap(early_halves))
    d_proj = d_proj3.reshape(n, PROJ_W)
    dx1, d_mix_norm = in_proj_bwd_x(d_proj, w_in_full, x1, mix_norm, dx2, name="in_proj_bwd_x")
    g_w_in = matmul_tn(h2, d_proj, 640, name="w_in_grad")[:, :IN_COLS]
    g_big["w_in"] = g_w_in.reshape(d, N_SHARD, IN_COLS // N_SHARD).transpose(1, 0, 2)

    dx0, dgate1, dup1, act1, dyh1, d_ffn1_norm = ffn_bwd_act(
        dx1, x0, ffn1_norm, gate1, up1, gw["ffn1_w_gate"], gw["ffn1_w_up"], gw["ffn1_w_down"],
        name="ffn1_bwd_act")
    g_big["ffn1_w_gate"], g_big["ffn1_w_up"], g_big["ffn1_w_down"] = ffn_bwd_w(
        h1, dyh1, dgate1, dup1, act1, name="ffn1_bwd_w")
    grad_x = dx0.reshape(bsz, t_len, d)

    late = [k for k in big_names if k not in early]
    g_list = [g_big[k] for k in late]
    from_sibling = run_exchange(pair_exchange(g_list), name="grad_pair_exchange")
    pair_sums = [pair_add(g, s, c_idx, name="grad_pair_add_" + k)
                 for k, g, s in zip(late, g_list, from_sibling)]
    from_chips = run_exchange(chip_exchange(pair_sums), name="grad_chip_exchange")
    halves = [sum_chips(r, p, shard_idx, name="grad_chip_sum_" + k)
              for k, r, p in zip(late, from_chips, pair_sums)]
    sib_halves = run_exchange(pair_swap(halves), name="grad_pair_swap")
    outs = {}
    for k, g_mine, g_sib in zip(early + late, early_halves + halves,
                                list(early_sib_halves) + list(sib_halves)):
        res = adamw_pair(as2d(big_w[k], k), g_mine, g_sib, as2d(big_m[k], k), as2d(big_v[k], k), c_idx,
                         name="adamw_" + k)
        outs[k] = tuple(from2d(a, k) for a in res)

    small_w = dict(ffn1_norm=ffn1_norm, mix_norm=mix_norm, ffn2_norm=ffn2_norm, final_norm=final_norm,
                   sg_ln_g=sg_ln_g, sg_ln_b=sg_ln_b, dn_norm=dn_norm, a_log=a_log, dt_bias=dt_bias,
                   sg_b=sg_b, sg_w=sg_w)
    small_m = dict(ffn1_norm=m_ffn1_norm, mix_norm=m_mix_norm, ffn2_norm=m_ffn2_norm,
                   final_norm=m_final_norm, sg_ln_g=m_sg_ln_g, sg_ln_b=m_sg_ln_b, dn_norm=m_dn_norm,
                   a_log=m_a_log, dt_bias=m_dt_bias, sg_b=m_sg_b, sg_w=m_sg_w)
    small_v = dict(ffn1_norm=v_ffn1_norm, mix_norm=v_mix_norm, ffn2_norm=v_ffn2_norm,
                   final_norm=v_final_norm, sg_ln_g=v_sg_ln_g, sg_ln_b=v_sg_ln_b, dn_norm=v_dn_norm,
                   a_log=v_a_log, dt_bias=v_dt_bias, sg_b=v_sg_b, sg_w=v_sg_w)
    shapes = {k: small_w[k].shape for k in small_w}
    shapes["conv_w"] = (CONV_K, 3 * HALF_W)
    d_sg_b = d_bias_tile.reshape(SG_CHUNK, SG_GROUPS, SG_GROUP_DIM).sum(axis=-1).T
    small_g = dict(ffn1_norm=d_ffn1_norm, mix_norm=d_mix_norm, ffn2_norm=d_ffn2_norm,
                   final_norm=d_final_norm, sg_ln_g=d_ln_g, sg_ln_b=d_ln_b, dn_norm=d_dn_norm,
                   a_log=d_alog_row[:, N_HEADS:2 * N_HEADS], dt_bias=d_dtb_row[:, N_HEADS:2 * N_HEADS],
                   sg_b=d_sg_b, sg_w=d_sg_w, conv_w=d_conv)
    g_pack = all_reduce_small(_pack_small(small_g), name="small_all_reduce")
    g_small = _unpack_small(g_pack, shapes)
    cw = 3 * HALF_W // N_SHARD
    g_conv = lax.dynamic_slice_in_dim(g_small["conv_w"], shard * cw, cw, axis=1)
    zero_conv = jnp.zeros((CONV_K, 3 * HALF_W), F32)

    def packed(src, conv):
        parts = dict(src)
        parts["conv_w"] = lax.dynamic_update_slice_in_dim(zero_conv, conv[0], shard * cw, axis=1)
        return _pack_small(parts)

    d_pack, m_pack, v_pack = adamw(packed(small_w, conv_w), g_pack, packed(small_m, m_conv_w),
                                   packed(small_v, v_conv_w), name="adamw_small")
    d_small = _unpack_small(d_pack, shapes)
    m_small = _unpack_small(m_pack, shapes)
    v_small = _unpack_small(v_pack, shapes)

    def conv_block(full_arr):
        return lax.dynamic_slice_in_dim(full_arr, shard * cw, cw, axis=1)[None]

    for k in small_w:
        outs[k] = (g_small[k].reshape(small_w[k].shape), d_small[k], m_small[k], v_small[k])
    outs["conv_w"] = (g_conv[None], conv_block(d_small["conv_w"]), conv_block(m_small["conv_w"]),
                      conv_block(v_small["conv_w"]))

    order = ["ffn1_norm", "ffn1_w_gate", "ffn1_w_up", "ffn1_w_down", "mix_norm", "w_in", "conv_w",
             "a_log", "dt_bias", "dn_norm", "sg_ln_g", "sg_ln_b", "sg_w", "sg_b", "w_out", "ffn2_norm",
             "ffn2_w_gate", "ffn2_w_up", "ffn2_w_down", "final_norm"]
    return (loss, grad_x, *[outs[k][0] for k in order], *[outs[k][1] for k in order],
            *[outs[k][2] for k in order], *[outs[k][3] for k in order])
```

```python
import functools

import jax
import jax.numpy as jnp
from jax import lax
from jax.experimental import pallas as pl
from jax.experimental.pallas import tpu as pltpu

F32 = jnp.float32
BF16 = jnp.bfloat16
EPS = 1e-6

D_MODEL = 1024
N_SHARD = 4
HEAD_DIM = 128
N_HEADS = 4
DN_CHUNK = 64
SG_CHUNK = 128
SG_GROUPS = 8
SG_GROUP_DIM = 64
HALF_W = 512
PROJ_W = 3200
IN_COLS = 3080
GATE_COL_BLOCK = 24
QK_SCALE = HEAD_DIM ** -0.5
LANES = 128

ADAM_LR = 0.001
ADAM_B1 = 0.9
ADAM_B2 = 0.999
ADAM_EPS = 1e-08
ADAM_WD = 0.01
ADAM_STEP = 10

VMEM_LIMIT = 56 * 1024 * 1024
ROW_TILE = 512

NN = ((1,), (0,))
NT = ((1,), (1,))
TN = ((0,), (0,))
MESH = pl.DeviceIdType.MESH


def _dot(a, b, dims):
    return lax.dot_general(a, b, (dims, ((), ())), preferred_element_type=F32)


def _bdot(a, b, dims):
    return _dot(a.astype(BF16), b.astype(BF16), dims)


def _split(a):
    hi = a.astype(BF16)
    lo = (a - hi.astype(F32)).astype(BF16)
    return hi, lo


def _dot3(a, b, dims=NN):
    return _dot(a[0], b[0], dims) + (_dot(a[0], b[1], dims) + _dot(a[1], b[0], dims))


def _dot_exact_lhs(a, b):
    ab = a.astype(BF16)
    b1 = b.astype(BF16)
    r1 = b - b1.astype(F32)
    b2 = r1.astype(BF16)
    b3 = (r1 - b2.astype(F32)).astype(BF16)
    return _dot(ab, b1, NN) + (_dot(ab, b2, NN) + _dot(ab, b3, NN))


def _call(body, *, name, out_shape, in_specs, out_specs, grid=(), scratch=(), comm=None, **kw):
    params = dict(vmem_limit_bytes=VMEM_LIMIT)
    if grid:
        params["dimension_semantics"] = ("arbitrary",) * len(grid)
    if comm is None:
        return pl.pallas_call(
            body, name=name, grid=grid, in_specs=in_specs, out_specs=out_specs,
            out_shape=out_shape, scratch_shapes=list(scratch),
            compiler_params=pltpu.CompilerParams(**params), **kw)

    n_in, n_out, n_sc = len(in_specs), len(out_specs), len(scratch)
    c_in, c_out = len(comm.inputs), len(comm.out_shape)
    steps = 1
    for g in grid:
        steps *= g

    def hosted(*refs):
        ins, cins = refs[:n_in], refs[n_in:n_in + c_in]
        o0 = n_in + c_in
        outs, couts = refs[o0:o0 + n_out], refs[o0 + n_out:o0 + n_out + c_out]
        s0 = o0 + n_out + c_out
        sc, csems = refs[s0:s0 + n_sc], refs[s0 + n_sc:]
        lin = 0
        for axis, g in enumerate(grid):
            lin = lin * g + pl.program_id(axis)

        def at(step, fn):
            @pl.when(lin == step % steps)
            def _():
                fn(cins, couts, csems)

        for step, fn in comm.phases:
            if step >= 0:
                at(step, fn)
        body(*ins, *outs, *sc)
        for step, fn in comm.phases:
            if step < 0:
                at(step, fn)

    aliases = dict(kw.pop("input_output_aliases", {}))
    for k, m in comm.aliases.items():
        aliases[n_in + k] = n_out + m
    call = pl.pallas_call(
        hosted, name=name, grid=grid, in_specs=list(in_specs) + [_ANY] * c_in,
        out_specs=list(out_specs) + [_ANY] * c_out, out_shape=list(out_shape) + comm.out_shape,
        scratch_shapes=list(scratch) + comm.sems, input_output_aliases=aliases,
        compiler_params=pltpu.CompilerParams(**params), **kw)

    def run(*args):
        res = call(*args, *comm.inputs)
        return res[:n_out], res[n_out:]

    return run


def _sds(shape, dtype):
    return jax.ShapeDtypeStruct(tuple(shape), dtype)


def _resident(shape):
    zeros = (0,) * len(shape)
    return pl.BlockSpec(tuple(shape), lambda *_: zeros, pipeline_mode=pl.Buffered(1))


def _sigmoid(x):
    return jax.nn.sigmoid(x)


def _softplus(x):
    return jnp.maximum(x, 0.0) + jnp.log(1.0 + jnp.exp(-jnp.abs(x)))


_GELU_C = 0.7978845608028654
_GELU_A = 0.044715


def _gelu(x):
    t = jnp.tanh(_GELU_C * (x + _GELU_A * x * x * x))
    return 0.5 * x * (1.0 + t)


def _gelu_grad(x):
    t = jnp.tanh(_GELU_C * (x + _GELU_A * x * x * x))
    return 0.5 * (1.0 + t) + 0.5 * x * (1.0 - t * t) * _GELU_C * (1.0 + 3.0 * _GELU_A * x * x)


def _silu_grad(x):
    s = _sigmoid(x)
    return s * (1.0 + x * (1.0 - s))


def _rms_scale(xv):
    return lax.rsqrt(jnp.mean(xv * xv, axis=-1, keepdims=True) + EPS)


def _rms_bwd(dh, xv, g):
    r = _rms_scale(xv)
    xn = xv * r
    dg = jnp.sum(dh * xn, axis=0, keepdims=True)
    dxn = dh * g
    dx = r * (dxn - xn * jnp.mean(dxn * xn, axis=-1, keepdims=True))
    return dx, dg


def _iota2(shape, dim):
    return lax.broadcasted_iota(jnp.int32, shape, dim)


def _col_to_row(col):
    n = col.shape[0]
    eye = _iota2((n, n), 0) == _iota2((n, n), 1)
    return jnp.sum(jnp.where(eye, col, 0.0), axis=0, keepdims=True)


def _row_to_col(row):
    n = row.shape[1]
    eye = _iota2((n, n), 0) == _iota2((n, n), 1)
    return jnp.sum(jnp.where(eye, row, 0.0), axis=1, keepdims=True)


def ffn_fwd(x, gnorm, wg, wu, wd, name, comm=None):
    n, d = x.shape
    nb, fb, _ = wg.shape
    tm = min(ROW_TILE, n)

    def body(x_ref, g_ref, wg_ref, wu_ref, wd_ref, xo_ref, h_ref, gate_ref, up_ref, acc_ref):
        xv = x_ref[...]
        h = (xv * _rms_scale(xv) * g_ref[...]).astype(BF16)
        h_ref[...] = h
        for j in range(nb):
            gate = _dot(h, wg_ref[j], NT)
            up = _dot(h, wu_ref[j], NT)
            gate_ref[j] = gate.astype(BF16)
            up_ref[j] = up.astype(BF16)
            part = _dot((gate * _sigmoid(gate) * up).astype(BF16), wd_ref[j], NN)
            if j == 0:
                acc_ref[...] = part
            else:
                acc_ref[...] += part
        xo_ref[...] = xv + 0.5 * acc_ref[...]

    row = pl.BlockSpec((tm, d), lambda i: (i, 0))
    blk = pl.BlockSpec((nb, tm, fb), lambda i: (0, i, 0))
    return _call(
        body, name=name, grid=(n // tm,),
        in_specs=[row, pl.BlockSpec((1, d), lambda i: (0, 0))] + [_resident((nb, fb, d))] * 3,
        out_specs=[row, row, blk, blk],
        out_shape=[_sds((n, d), F32), _sds((n, d), BF16),
                   _sds((nb, n, fb), BF16), _sds((nb, n, fb), BF16)],
        scratch=[pltpu.VMEM((tm, d), F32)], comm=comm,
    )(x, gnorm, wg, wu, wd)


def ffn_bwd_act(dy, x, gnorm, gate, up, wg, wu, wd, name, comm=None):
    n, d = x.shape
    nb, fb, _ = wg.shape
    tm = min(ROW_TILE // 2, n)

    def body(dy_ref, x_ref, g_ref, gate_ref, up_ref, wg_ref, wu_ref, wd_ref,
             dx_ref, dgate_ref, dup_ref, act_ref, dyh_ref, dg_ref, acc_ref):
        @pl.when(pl.program_id(0) == 0)
        def _():
            dg_ref[...] = jnp.zeros_like(dg_ref)

        dyh = (0.5 * dy_ref[...]).astype(BF16)
        dyh_ref[...] = dyh
        for j in range(nb):
            dact = _dot(dyh, wd_ref[j], NT)
            gt = gate_ref[j].astype(F32)
            u = up_ref[j].astype(F32)
            s = _sigmoid(gt)
            silu = gt * s
            dup = (dact * silu).astype(BF16)
            dgate = (dact * u * (s * (1.0 + gt * (1.0 - s)))).astype(BF16)
            dup_ref[j] = dup
            dgate_ref[j] = dgate
            act_ref[j] = (silu * u).astype(BF16)
            part = _dot(dgate, wg_ref[j], NN) + _dot(dup, wu_ref[j], NN)
            if j == 0:
                acc_ref[...] = part
            else:
                acc_ref[...] += part
        dxn, dg = _rms_bwd(acc_ref[...], x_ref[...], g_ref[...])
        dx_ref[...] = dy_ref[...] + dxn
        dg_ref[...] += dg

    row = pl.BlockSpec((tm, d), lambda i: (i, 0))
    blk = pl.BlockSpec((nb, tm, fb), lambda i: (0, i, 0))
    vec = pl.BlockSpec((1, d), lambda i: (0, 0))
    wblk = _resident((nb, fb, d))
    return _call(
        body, name=name, grid=(n // tm,),
        in_specs=[row, row, vec, blk, blk, wblk, wblk, wblk],
        out_specs=[row, blk, blk, blk, row, vec],
        out_shape=[_sds((n, d), F32), _sds((nb, n, fb), BF16), _sds((nb, n, fb), BF16),
                   _sds((nb, n, fb), BF16), _sds((n, d), BF16), _sds((1, d), F32)],
        scratch=[pltpu.VMEM((tm, d), F32)], comm=comm,
    )(dy, x, gnorm, gate, up, wg, wu, wd)


def ffn_bwd_w(h, dyh, dgate, dup, act, name, comm=None):
    n, d = h.shape
    nb, _, fb = dgate.shape
    tk = min(2 * ROW_TILE, n)

    def body(h_ref, dyh_ref, dgate_ref, dup_ref, act_ref, dwg_ref, dwu_ref, dwd_ref):
        @pl.when(pl.program_id(1) == 0)
        def _():
            dwg_ref[...] = jnp.zeros_like(dwg_ref)
            dwu_ref[...] = jnp.zeros_like(dwu_ref)
            dwd_ref[...] = jnp.zeros_like(dwd_ref)

        hv = h_ref[...]
        dwg_ref[0] += _dot(dgate_ref[0], hv, TN)
        dwu_ref[0] += _dot(dup_ref[0], hv, TN)
        dwd_ref[0] += _dot(act_ref[0], dyh_ref[...], TN)

    row = pl.BlockSpec((tk, d), lambda j, k: (k, 0))
    blk = pl.BlockSpec((1, tk, fb), lambda j, k: (j, k, 0))
    return _call(
        body, name=name, grid=(nb, n // tk),
        in_specs=[row, row, blk, blk, blk],
        out_specs=[pl.BlockSpec((1, fb, d), lambda j, k: (j, 0, 0))] * 3,
        out_shape=[_sds((nb, fb, d), F32)] * 3, comm=comm,
    )(h, dyh, dgate, dup, act)


def final_loss(x, gnorm, target, name):
    n, d = x.shape
    tm = min(ROW_TILE, n)

    def body(x_ref, g_ref, t_ref, dx_ref, dg_ref, loss_ref):
        @pl.when(pl.program_id(0) == 0)
        def _():
            dg_ref[...] = jnp.zeros_like(dg_ref)
            loss_ref[...] = jnp.zeros_like(loss_ref)

        xv = x_ref[...]
        y = xv * _rms_scale(xv) * g_ref[...]
        err = y - t_ref[...]
        part = 0.5 * jnp.sum(jnp.mean(err * err, axis=-1, keepdims=True), axis=0, keepdims=True)
        loss_ref[...] += jnp.broadcast_to(part, loss_ref.shape)
        dx, dg = _rms_bwd(err * (1.0 / d), xv, g_ref[...])
        dx_ref[...] = dx
        dg_ref[...] += dg

    row = pl.BlockSpec((tm, d), lambda i: (i, 0))
    vec = pl.BlockSpec((1, d), lambda i: (0, 0))
    return _call(
        body, name=name, grid=(n // tm,),
        in_specs=[row, vec, row],
        out_specs=[row, vec, pl.BlockSpec((1, LANES), lambda i: (0, 0))],
        out_shape=[_sds((n, d), F32), _sds((1, d), F32), _sds((1, LANES), F32)],
    )(x, gnorm, target)


def in_proj_fwd(x, gnorm, w, name):
    n, d = x.shape
    cols = w.shape[1]
    tm = min(ROW_TILE, n)
    tn = 640

    def body(x_ref, g_ref, w_ref, p_ref, h_ref):
        xv = x_ref[...]
        h = (xv * _rms_scale(xv) * g_ref[...]).astype(BF16)
        h_ref[...] = h
        for c0 in range(0, cols, tn):
            p_ref[:, c0:c0 + tn] = _dot(h, w_ref[:, c0:c0 + tn], NN)

    return _call(
        body, name=name, grid=(n // tm,),
        in_specs=[pl.BlockSpec((tm, d), lambda i: (i, 0)),
                  pl.BlockSpec((1, d), lambda i: (0, 0)), _resident((d, cols))],
        out_specs=[pl.BlockSpec((tm, cols), lambda i: (i, 0)),
                   pl.BlockSpec((tm, d), lambda i: (i, 0))],
        out_shape=[_sds((n, cols), F32), _sds((n, d), BF16)],
    )(x, gnorm, w)


def in_proj_bwd_x(dproj, w, x, gnorm, dres, name):
    n, d = x.shape
    cols = w.shape[1]
    tm = min(ROW_TILE, n)

    def body(dp_ref, w_ref, x_ref, g_ref, dr_ref, dx_ref, dg_ref):
        @pl.when(pl.program_id(0) == 0)
        def _():
            dg_ref[...] = jnp.zeros_like(dg_ref)

        dh = _dot(dp_ref[...], w_ref[...], NT)
        dxn, dg = _rms_bwd(dh, x_ref[...], g_ref[...])
        dx_ref[...] = dr_ref[...] + dxn
        dg_ref[...] += dg

    row = pl.BlockSpec((tm, d), lambda i: (i, 0))
    vec = pl.BlockSpec((1, d), lambda i: (0, 0))
    return _call(
        body, name=name, grid=(n // tm,),
        in_specs=[pl.BlockSpec((tm, cols), lambda i: (i, 0)),
                  _resident((d, cols)), row, vec, row],
        out_specs=[row, vec],
        out_shape=[_sds((n, d), F32), _sds((1, d), F32)],
    )(dproj, w, x, gnorm, dres)


def matmul_tn(a, b, tn, name):
    n, ka = a.shape
    cb = b.shape[1]
    tk = min(ROW_TILE, n)

    def body(a_ref, b_ref, o_ref):
        @pl.when(pl.program_id(0) == 0)
        def _():
            o_ref[...] = jnp.zeros_like(o_ref)

        av = a_ref[...]
        for c0 in range(0, cb, tn):
            o_ref[:, c0:c0 + tn] += _dot(av, b_ref[:, c0:c0 + tn], TN)

    return _call(
        body, name=name, grid=(n // tk,),
        in_specs=[pl.BlockSpec((tk, ka), lambda k: (k, 0)),
                  pl.BlockSpec((tk, cb), lambda k: (k, 0))],
        out_specs=pl.BlockSpec((ka, cb), lambda k: (0, 0)),
        out_shape=_sds((ka, cb), F32),
    )(a, b)


def out_proj_fwd(x, sg_out, dn_out, w, name):
    n, d = x.shape
    tm = min(ROW_TILE, n)

    def body(x_ref, a_ref, b_ref, w_ref, o_ref):
        o_ref[...] = (x_ref[...] + _dot(a_ref[...], w_ref[0:HALF_W, :], NN)
                      + _dot(b_ref[...], w_ref[HALF_W:2 * HALF_W, :], NN))

    row = pl.BlockSpec((tm, d), lambda i: (i, 0))
    half = pl.BlockSpec((tm, HALF_W), lambda i: (i, 0))
    return _call(
        body, name=name, grid=(n // tm,),
        in_specs=[row, half, half, pl.BlockSpec((2 * HALF_W, d), lambda i: (0, 0))],
        out_specs=row, out_shape=_sds((n, d), F32),
    )(x, sg_out, dn_out, w)


def out_proj_bwd_x(dy, w, name, comm=None):
    n, d = dy.shape
    tm = min(ROW_TILE, n)

    def body(dy_ref, w_ref, dsg_ref, ddn_ref, dyb_ref):
        dyb = dy_ref[...].astype(BF16)
        dyb_ref[...] = dyb
        dsg_ref[...] = _dot(dyb, w_ref[0:HALF_W, :], NT)
        ddn_ref[...] = _dot(dyb, w_ref[HALF_W:2 * HALF_W, :], NT)

    row = pl.BlockSpec((tm, d), lambda i: (i, 0))
    half = pl.BlockSpec((tm, HALF_W), lambda i: (i, 0))
    return _call(
        body, name=name, grid=(n // tm,),
        in_specs=[row, pl.BlockSpec((2 * HALF_W, d), lambda i: (0, 0))],
        out_specs=[half, half, row],
        out_shape=[_sds((n, HALF_W), F32), _sds((n, HALF_W), F32), _sds((n, d), BF16)], comm=comm,
    )(dy, w)


def _sg_group_masks():
    col = _iota2((SG_CHUNK, HALF_W), 1)
    return [jnp.logical_and(col >= g * SG_GROUP_DIM, col < (g + 1) * SG_GROUP_DIM)
            for g in range(SG_GROUPS)]


def _sg_causal():
    return _iota2((SG_CHUNK, SG_CHUNK), 0) >= _iota2((SG_CHUNK, SG_CHUNK), 1)


def _sg_forward_chunk(pu, pv, ln_g, ln_b, wc, bias, masks):
    u = _gelu(pu)
    v = _gelu(pv)
    mu = jnp.mean(v, axis=-1, keepdims=True)
    vc = v - mu
    rs = lax.rsqrt(jnp.mean(vc * vc, axis=-1, keepdims=True) + EPS)
    xhat = vc * rs
    vn = (xhat * ln_g + ln_b).astype(BF16)
    vs = bias
    for g in range(SG_GROUPS):
        vs = vs + jnp.where(masks[g], _dot(wc[g], vn, NN), 0.0)
    return u, xhat, rs, vn, vs


def sg_fwd(proj, ln_g, ln_b, w_s, bias_tile, name):
    n = proj.shape[0]
    tm = min(ROW_TILE, n)

    def body(pu_ref, pv_ref, g_ref, b_ref, w_ref, bias_ref, o_ref):
        causal = _sg_causal()
        wc = [jnp.where(causal, w_ref[g], 0.0).astype(BF16) for g in range(SG_GROUPS)]
        masks = _sg_group_masks()
        for ci in range(tm // SG_CHUNK):
            rows = slice(ci * SG_CHUNK, (ci + 1) * SG_CHUNK)
            u, _, _, _, vs = _sg_forward_chunk(pu_ref[rows, :], pv_ref[rows, :], g_ref[...],
                                               b_ref[...], wc, bias_ref[...], masks)
            o_ref[rows, :] = (u * vs).astype(BF16)

    vec = pl.BlockSpec((1, HALF_W), lambda i: (0, 0))
    return _call(
        body, name=name, grid=(n // tm,),
        in_specs=[pl.BlockSpec((tm, HALF_W), lambda i: (i, 0)),
                  pl.BlockSpec((tm, HALF_W), lambda i: (i, 1)), vec, vec,
                  pl.BlockSpec((SG_GROUPS, SG_CHUNK, SG_CHUNK), lambda i: (0, 0, 0)),
                  pl.BlockSpec((SG_CHUNK, HALF_W), lambda i: (0, 0))],
        out_specs=pl.BlockSpec((tm, HALF_W), lambda i: (i, 0)),
        out_shape=_sds((n, HALF_W), BF16),
    )(proj, proj, ln_g, ln_b, w_s, bias_tile)


def sg_bwd(dsg, proj, ln_g, ln_b, w_s, bias_tile, name):
    n = proj.shape[0]
    tm = min(ROW_TILE, n)

    def body(d_ref, pu_ref, pv_ref, g_ref, b_ref, w_ref, bias_ref,
             dp_ref, dw_ref, db_ref, dlg_ref, dlb_ref):
        @pl.when(pl.program_id(0) == 0)
        def _():
            dw_ref[...] = jnp.zeros_like(dw_ref)
            db_ref[...] = jnp.zeros_like(db_ref)
            dlg_ref[...] = jnp.zeros_like(dlg_ref)
            dlb_ref[...] = jnp.zeros_like(dlb_ref)

        causal = _sg_causal()
        wc = [jnp.where(causal, w_ref[g], 0.0).astype(BF16) for g in range(SG_GROUPS)]
        masks = _sg_group_masks()
        ln_g_v = g_ref[...]
        for ci in range(tm // SG_CHUNK):
            rows = slice(ci * SG_CHUNK, (ci + 1) * SG_CHUNK)
            pu = pu_ref[rows, :]
            pv = pv_ref[rows, :]
            u, xhat, rs, vn, vs = _sg_forward_chunk(pu, pv, ln_g_v, b_ref[...], wc,
                                                    bias_ref[...], masks)
            dout = d_ref[rows, :]
            dp_ref[rows, 0:HALF_W] = (dout * vs * _gelu_grad(pu)).astype(BF16)
            dvs = dout * u
            dvs_b = dvs.astype(BF16)
            db_ref[...] += dvs
            dvn = jnp.zeros_like(dvs)
            for g in range(SG_GROUPS):
                dvn = dvn + jnp.where(masks[g], _dot(wc[g], dvs_b, TN), 0.0)
                dwg = _dot(jnp.where(masks[g], dvs_b, jnp.zeros_like(dvs_b)), vn, NT)
                dw_ref[g] += jnp.where(causal, dwg, 0.0)
            dlg_ref[...] += jnp.sum(dvn * xhat, axis=0, keepdims=True)
            dlb_ref[...] += jnp.sum(dvn, axis=0, keepdims=True)
            dxh = dvn * ln_g_v
            dv = rs * (dxh - jnp.mean(dxh, axis=-1, keepdims=True)
                       - xhat * jnp.mean(dxh * xhat, axis=-1, keepdims=True))
            dp_ref[rows, HALF_W:2 * HALF_W] = (dv * _gelu_grad(pv)).astype(BF16)

    vec = pl.BlockSpec((1, HALF_W), lambda i: (0, 0))
    wspec = pl.BlockSpec((SG_GROUPS, SG_CHUNK, SG_CHUNK), lambda i: (0, 0, 0))
    tile = pl.BlockSpec((SG_CHUNK, HALF_W), lambda i: (0, 0))
    return _call(
        body, name=name, grid=(n // tm,),
        in_specs=[pl.BlockSpec((tm, HALF_W), lambda i: (i, 0)),
                  pl.BlockSpec((tm, HALF_W), lambda i: (i, 0)),
                  pl.BlockSpec((tm, HALF_W), lambda i: (i, 1)), vec, vec, wspec, tile],
        out_specs=[pl.BlockSpec((tm, 2 * HALF_W), lambda i: (i, 0)), wspec, tile, vec, vec],
        out_shape=[_sds((n, PROJ_W), BF16), _sds((SG_GROUPS, SG_CHUNK, SG_CHUNK), F32),
                   _sds((SG_CHUNK, HALF_W), F32), _sds((1, HALF_W), F32), _sds((1, HALF_W), F32)],
    )(dsg, proj, proj, ln_g, ln_b, w_s, bias_tile)


CONV_K = 4
CONV_BLOCK = 256


def _shift_down(x, s):
    if s == 0:
        return x
    rolled = pltpu.roll(x, s, 0)
    return jnp.where(_iota2(x.shape, 0) >= s, rolled, 0.0)


def _shift_up(x, s):
    if s == 0:
        return x
    t_len = x.shape[0]
    rolled = pltpu.roll(x, t_len - s, 0)
    return jnp.where(_iota2(x.shape, 0) < t_len - s, rolled, 0.0)


def _conv(x, w):
    y = _shift_down(x, CONV_K - 1) * w[0:1, :]
    for j in range(1, CONV_K):
        y = y + _shift_down(x, CONV_K - 1 - j) * w[j:j + 1, :]
    return y


def dn_conv_fwd(proj3, conv_w, name):
    b, t, _ = proj3.shape
    nblk = 3 * HALF_W // CONV_BLOCK
    first = 2 * HALF_W // CONV_BLOCK
    n_norm = 2 * HALF_W // CONV_BLOCK

    def body(x_ref, w_ref, o_ref):
        s = pl.program_id(1)
        y = _conv(x_ref[0], w_ref[...])
        y = y * _sigmoid(y)

        @pl.when(s < n_norm)
        def _():
            for h in range(CONV_BLOCK // HEAD_DIM):
                cs = slice(h * HEAD_DIM, (h + 1) * HEAD_DIM)
                yh = y[:, cs]
                o_ref[0, :, cs] = yh * lax.rsqrt(jnp.sum(yh * yh, axis=-1, keepdims=True) + EPS)

        @pl.when(s >= n_norm)
        def _():
            o_ref[0] = y

    return _call(
        body, name=name, grid=(b, nblk),
        in_specs=[pl.BlockSpec((1, t, CONV_BLOCK), lambda i, s: (i, 0, first + s)),
                  pl.BlockSpec((CONV_K, CONV_BLOCK), lambda i, s: (0, s))],
        out_specs=pl.BlockSpec((1, t, CONV_BLOCK), lambda i, s: (i, 0, s)),
        out_shape=_sds((b, t, 3 * HALF_W), F32),
    )(proj3, conv_w)


def dn_conv_bwd(dqkv, proj3, conv_w, dproj3, name, comm=None):
    b, t, _ = proj3.shape
    nblk = 3 * HALF_W // CONV_BLOCK
    first = 2 * HALF_W // CONV_BLOCK
    n_norm = 2 * HALF_W // CONV_BLOCK

    def body(d_ref, x_ref, w_ref, dproj_in, dx_ref, dw_ref, ds_ref):
        s = pl.program_id(0)

        @pl.when(pl.program_id(1) == 0)
        def _():
            dw_ref[...] = jnp.zeros_like(dw_ref)

        x = x_ref[0]
        w = w_ref[...]
        c = _conv(x, w)
        sg = _sigmoid(c)
        y = c * sg

        @pl.when(s < n_norm)
        def _():
            for h in range(CONV_BLOCK // HEAD_DIM):
                cs = slice(h * HEAD_DIM, (h + 1) * HEAD_DIM)
                yh = y[:, cs]
                r = lax.rsqrt(jnp.sum(yh * yh, axis=-1, keepdims=True) + EPS)
                nh = yh * r
                dn = d_ref[0, :, cs]
                ds_ref[:, cs] = r * (dn - nh * jnp.sum(dn * nh, axis=-1, keepdims=True))

        @pl.when(s >= n_norm)
        def _():
            ds_ref[...] = d_ref[0]

        dc = ds_ref[...] * (sg * (1.0 + c * (1.0 - sg)))
        dx = _shift_up(dc, CONV_K - 1) * w[0:1, :]
        for j in range(1, CONV_K):
            dx = dx + _shift_up(dc, CONV_K - 1 - j) * w[j:j + 1, :]
        dx_ref[0] = dx.astype(BF16)
        for j in range(CONV_K):
            dw_ref[j:j + 1, :] += jnp.sum(dc * _shift_down(x, CONV_K - 1 - j), axis=0, keepdims=True)

    return _call(
        body, name=name, grid=(nblk, b),
        in_specs=[pl.BlockSpec((1, t, CONV_BLOCK), lambda s, i: (i, 0, s)),
                  pl.BlockSpec((1, t, CONV_BLOCK), lambda s, i: (i, 0, first + s)),
                  pl.BlockSpec((CONV_K, CONV_BLOCK), lambda s, i: (0, s)), _ANY],
        out_specs=[pl.BlockSpec((1, t, CONV_BLOCK), lambda s, i: (i, 0, first + s)),
                   pl.BlockSpec((CONV_K, CONV_BLOCK), lambda s, i: (0, s))],
        out_shape=[_sds(dproj3.shape, BF16), _sds((CONV_K, 3 * HALF_W), F32)],
        scratch=[pltpu.VMEM((t, CONV_BLOCK), F32)],
        input_output_aliases={3: 0}, comm=comm,
    )(dqkv, proj3, conv_w, dproj3)


def _chunk_masks():
    ii = _iota2((DN_CHUNK, DN_CHUNK), 0)
    jj = _iota2((DN_CHUNK, DN_CHUNK), 1)
    return ii >= jj, ii > jj, ii == jj


LOCKSTEP_CHUNKS = 2


def _inv_unit_lower_many(l_mats, eye):
    eye_f = jnp.where(eye, 1.0, 0.0)
    ps = [-l for l in l_mats]
    ts = [eye_f + p for p in ps]
    pss = [_split(p) for p in ps]
    size = 2
    while size < DN_CHUNK:
        ps = [_dot3(s, s) for s in pss]
        pss = [_split(p) for p in ps]
        ts = [t + _dot3(_split(t), s) for t, s in zip(ts, pss)]
        size *= 2
    return ts


def _gates(pba, ea_row, dtb_row):
    beta = _sigmoid(pba)
    g = -ea_row * _softplus(pba + dtb_row)
    return beta, g


def _chunk_decay(gcol):
    incl, strict, eye = _chunk_masks()
    grow = jnp.sum(jnp.where(eye, gcol, 0.0), axis=0, keepdims=True)
    decay = jnp.where(incl, jnp.exp(jnp.where(incl, gcol - grow, 0.0)), 0.0)
    return decay, incl, strict, eye


def dn_chunk_fwd(qkv, proj3, alog_row, dtb_row, name):
    b, t, _ = qkv.shape
    rblk = min(256, t)
    n_in = rblk // DN_CHUNK

    def body(q_ref, k_ref, v_ref, pba_ref, al_ref, dtb_ref,
             u_ref, w_ref, qd_ref, kd_ref, qk_ref, ti_ref, gc_ref):
        ea = jnp.exp(al_ref[...])
        tri = jnp.where(_chunk_masks()[0], 1.0, 0.0)

        _, strict, eye = _chunk_masks()

        def chunk_group(cg, carry):
            items = []
            for sub in range(LOCKSTEP_CHUNKS):
                rows = pl.ds(pl.multiple_of((cg * LOCKSTEP_CHUNKS + sub) * DN_CHUNK, DN_CHUNK), DN_CHUNK)
                beta_all, g_all = _gates(pba_ref[0, rows, :], ea, dtb_ref[...])
                gc = _dot_exact_lhs(tri, g_all)
                gc_ref[0, rows, :] = gc
                for h in range(N_HEADS):
                    items.append((rows, h, beta_all[:, h:h + 1], gc[:, N_HEADS + h:N_HEADS + h + 1]))
            ks, kbs, decays, egs = [], [], [], []
            for rows, h, beta, gcol in items:
                cs = slice(h * HEAD_DIM, (h + 1) * HEAD_DIM)
                k = k_ref[0, rows, cs]
                ks.append(k)
                kbs.append(k * beta)
                decays.append(_chunk_decay(gcol)[0])
                egs.append(jnp.exp(gcol))
            ms = [_bdot(kb, k, NT) for kb, k in zip(kbs, ks)]
            tinvs = _inv_unit_lower_many([jnp.where(strict, m * dc, 0.0) for m, dc in zip(ms, decays)], eye)
            tsps = [_split(t) for t in tinvs]
            for (rows, h, beta, gcol), tsp, tinv in zip(items, tsps, tinvs):
                cs = slice(h * HEAD_DIM, (h + 1) * HEAD_DIM)
                u_ref[0, rows, cs] = _dot3(tsp, _split(v_ref[0, rows, cs] * beta))
                ti_ref[0, h, rows, :] = tinv
            for (rows, h, beta, gcol), tsp, kb, eg in zip(items, tsps, kbs, egs):
                cs = slice(h * HEAD_DIM, (h + 1) * HEAD_DIM)
                w_ref[0, rows, cs] = _dot3(tsp, _split(kb * eg))
            for (rows, h, beta, gcol), k, dc, eg in zip(items, ks, decays, egs):
                cs = slice(h * HEAD_DIM, (h + 1) * HEAD_DIM)
                q = q_ref[0, rows, cs] * QK_SCALE
                qk_ref[0, h, rows, :] = _bdot(q, k, NT) * dc
                qd_ref[0, rows, cs] = q * eg
                kd_ref[0, rows, cs] = k * jnp.exp(gcol[DN_CHUNK - 1:DN_CHUNK, :] - gcol)
            return carry

        lax.fori_loop(0, n_in // LOCKSTEP_CHUNKS, chunk_group, 0)

    def seg(cblk):
        return pl.BlockSpec((1, rblk, HALF_W), lambda i, r: (i, r, cblk))

    vec = pl.BlockSpec((1, LANES), lambda i, r: (0, 0))
    wide = pl.BlockSpec((1, rblk, HALF_W), lambda i, r: (i, r, 0))
    sq = pl.BlockSpec((1, N_HEADS, rblk, DN_CHUNK), lambda i, r: (i, 0, r, 0))
    return _call(
        body, name=name, grid=(b, t // rblk),
        in_specs=[seg(0), seg(1), seg(2),
                  pl.BlockSpec((1, rblk, LANES), lambda i, r: (i, r, GATE_COL_BLOCK)), vec, vec],
        out_specs=[wide, wide, wide, wide, sq, sq,
                   pl.BlockSpec((1, rblk, LANES), lambda i, r: (i, r, 0))],
        out_shape=[_sds((b, t, HALF_W), F32)] * 4
        + [_sds((b, N_HEADS, t, DN_CHUNK), F32)] * 2 + [_sds((b, t, LANES), F32)],
    )(qkv, qkv, qkv, proj3, alog_row, dtb_row)


def dn_scan_fwd(u, w, qd, kd, qk, gc, name):
    b, t, _ = u.shape
    nc = t // DN_CHUNK
    bh = b * N_HEADS

    def body(u_ref, w_ref, qd_ref, kd_ref, qk_ref, gc_ref, o_ref, sin_ref, s_ref):
        @pl.when(pl.program_id(0) == 0)
        def _():
            s_ref[...] = jnp.zeros_like(s_ref)

        items = [(bi, h, slice(h * HEAD_DIM, (h + 1) * HEAD_DIM)) for bi in range(b) for h in range(N_HEADS)]
        sbs = []
        for bi, h, cs in items:
            s = s_ref[bi * N_HEADS + h]
            sin_ref[0, bi * N_HEADS + h] = s
            sbs.append(s.astype(BF16))
        ws = [_bdot(w_ref[bi, :, cs], sb, NN) for (bi, h, cs), sb in zip(items, sbs)]
        qs = [_bdot(qd_ref[bi, :, cs], sb, NN) for (bi, h, cs), sb in zip(items, sbs)]
        vbs = [(u_ref[bi, :, cs] - wsi).astype(BF16) for (bi, h, cs), wsi in zip(items, ws)]
        for (bi, h, cs), qsi, vb in zip(items, qs, vbs):
            o_ref[bi, :, cs] = qsi + _bdot(qk_ref[bi, h], vb, NN)
        for (bi, h, cs), vb in zip(items, vbs):
            gl = jnp.exp(gc_ref[bi, DN_CHUNK - 1:DN_CHUNK, N_HEADS + h:N_HEADS + h + 1])
            idx = bi * N_HEADS + h
            s_ref[idx] = s_ref[idx] * gl + _bdot(kd_ref[bi, :, cs], vb, TN)

    wide = pl.BlockSpec((b, DN_CHUNK, HALF_W), lambda c: (0, c, 0))
    return _call(
        body, name=name, grid=(nc,),
        in_specs=[wide, wide, wide, wide,
                  pl.BlockSpec((b, N_HEADS, DN_CHUNK, DN_CHUNK), lambda c: (0, 0, c, 0)),
                  pl.BlockSpec((b, DN_CHUNK, LANES), lambda c: (0, c, 0))],
        out_specs=[wide, pl.BlockSpec((1, bh, HEAD_DIM, HEAD_DIM), lambda c: (c, 0, 0, 0))],
        out_shape=[_sds((b, t, HALF_W), F32), _sds((nc, bh, HEAD_DIM, HEAD_DIM), F32)],
        scratch=[pltpu.VMEM((bh, HEAD_DIM, HEAD_DIM), F32)],
    )(u, w, qd, kd, qk, gc)


def dn_scan_bwd(do, u, w, qd, kd, qk, gc, s_in, name):
    b, t, _ = u.shape
    nc = t // DN_CHUNK
    bh = b * N_HEADS

    def body(do_ref, u_ref, w_ref, qd_ref, kd_ref, qk_ref, gc_ref, sin_ref,
             du_ref, dw_ref, dqd_ref, dkd_ref, dqk_ref, dgc_ref, ds_ref):
        @pl.when(pl.program_id(0) == 0)
        def _():
            ds_ref[...] = jnp.zeros_like(ds_ref)

        last_row = _iota2((DN_CHUNK, LANES), 0) == DN_CHUNK - 1
        lane = _iota2((DN_CHUNK, LANES), 1)
        items = [(bi, h, slice(h * HEAD_DIM, (h + 1) * HEAD_DIM)) for bi in range(b) for h in range(N_HEADS)]
        sbs = [sin_ref[0, bi * N_HEADS + h].astype(BF16) for bi, h, cs in items]
        wvs = [w_ref[bi, :, cs].astype(BF16) for bi, h, cs in items]
        dovs = [do_ref[bi, :, cs].astype(BF16) for bi, h, cs in items]
        dsbs = [ds_ref[bi * N_HEADS + h].astype(BF16) for bi, h, cs in items]
        vbs = [(u_ref[bi, :, cs] - _dot(wv, sb, NN)).astype(BF16)
               for (bi, h, cs), wv, sb in zip(items, wvs, sbs)]
        for (bi, h, cs), dov, sb in zip(items, dovs, sbs):
            dqd_ref[bi, :, cs] = _dot(dov, sb, NT)
        dvns = [_dot(kd_ref[bi, :, cs].astype(BF16), dsb, NN) + _dot(qk_ref[bi, h].astype(BF16), dov, TN)
                for (bi, h, cs), dsb, dov in zip(items, dsbs, dovs)]
        for (bi, h, cs), vb, dsb, dov in zip(items, vbs, dsbs, dovs):
            dkd_ref[bi, :, cs] = _dot(vb, dsb, NT)
            dqk_ref[bi, h] = _dot(dov, vb, NT)
        dgls = []
        for (bi, h, cs), dvn, sb, wv, dov in zip(items, dvns, sbs, wvs, dovs):
            idx = bi * N_HEADS + h
            du_ref[bi, :, cs] = dvn
            dvn_b = dvn.astype(BF16)
            dw_ref[bi, :, cs] = -_dot(dvn_b, sb, NT)
            gl = jnp.exp(gc_ref[bi, DN_CHUNK - 1:DN_CHUNK, N_HEADS + h:N_HEADS + h + 1])
            ds = ds_ref[idx]
            dgl = jnp.sum(jnp.sum(ds * sin_ref[0, idx], axis=1, keepdims=True), axis=0, keepdims=True)
            dgls.append(dgl * gl)
            ds_ref[idx] = (ds * gl + _dot(qd_ref[bi, :, cs].astype(BF16), dov, TN)
                           - _dot(wv, dvn_b, TN))
        for bi in range(b):
            dgc = jnp.zeros((DN_CHUNK, LANES), F32)
            for h in range(N_HEADS):
                dgc = dgc + jnp.where(jnp.logical_and(last_row, lane == N_HEADS + h),
                                      dgls[bi * N_HEADS + h], 0.0)
            dgc_ref[bi] = dgc

    def rev(c):
        return nc - 1 - c

    wide = pl.BlockSpec((b, DN_CHUNK, HALF_W), lambda c: (0, rev(c), 0))
    sq = pl.BlockSpec((b, N_HEADS, DN_CHUNK, DN_CHUNK), lambda c: (0, 0, rev(c), 0))
    gates = pl.BlockSpec((b, DN_CHUNK, LANES), lambda c: (0, rev(c), 0))
    return _call(
        body, name=name, grid=(nc,),
        in_specs=[wide, wide, wide, wide, wide, sq, gates,
                  pl.BlockSpec((1, bh, HEAD_DIM, HEAD_DIM), lambda c: (rev(c), 0, 0, 0))],
        out_specs=[wide, wide, wide, wide, sq, gates],
        out_shape=[_sds((b, t, HALF_W), F32)] * 4
        + [_sds((b, N_HEADS, t, DN_CHUNK), F32), _sds((b, t, LANES), F32)],
        scratch=[pltpu.VMEM((bh, HEAD_DIM, HEAD_DIM), F32)],
    )(do, u, w, qd, kd, qk, gc, s_in)


def dn_chunk_bwd(qkv, proj3, alog_row, dtb_row, tinv, u, w, du, dw, dqd, dkd, dqk, dgc_scan, dproj3, name,
                 comm=None):
    b, t, _ = qkv.shape
    rblk = min(256, t)
    n_in = rblk // DN_CHUNK

    def body(q_ref, k_ref, v_ref, pba_ref, al_ref, dtb_ref, ti_ref, u_ref, w_ref,
             du_ref, dw_ref, dqd_ref, dkd_ref, dqk_ref, dgs_ref, dproj_in,
             dq_ref, dpba_ref, dal_ref, ddtb_ref):
        @pl.when(jnp.logical_and(pl.program_id(0) == 0, pl.program_id(1) == 0))
        def _():
            dal_ref[...] = jnp.zeros_like(dal_ref)
            ddtb_ref[...] = jnp.zeros_like(ddtb_ref)

        ea = jnp.exp(al_ref[...])
        incl0 = _chunk_masks()[0]
        tri = jnp.where(incl0, 1.0, 0.0)
        tri_up = jnp.where(_iota2((DN_CHUNK, DN_CHUNK), 1) >= _iota2((DN_CHUNK, DN_CHUNK), 0), 1.0, 0.0)
        lane = _iota2((DN_CHUNK, LANES), 1)
        last_col = _iota2((DN_CHUNK, 1), 0) == DN_CHUNK - 1

        _, strict, _ = _chunk_masks()
        gate_lane = jnp.logical_and(lane >= N_HEADS, lane < 2 * N_HEADS)

        def chunk_group(cg, carry):
            tiles, items = [], []
            for sub in range(LOCKSTEP_CHUNKS):
                rows = pl.ds(pl.multiple_of((cg * LOCKSTEP_CHUNKS + sub) * DN_CHUNK, DN_CHUNK), DN_CHUNK)
                pba = pba_ref[0, rows, :]
                beta_all, g_all = _gates(pba, ea, dtb_ref[...])
                gc = _dot_exact_lhs(tri, g_all)
                tiles.append((rows, pba, beta_all, g_all))
                for h in range(N_HEADS):
                    items.append((sub, rows, h, slice(h * HEAD_DIM, (h + 1) * HEAD_DIM),
                                  beta_all[:, h:h + 1], gc[:, N_HEADS + h:N_HEADS + h + 1]))
            decays = [_chunk_decay(gcol)[0] for _, _, _, _, _, gcol in items]
            egs = [jnp.exp(gcol) for _, _, _, _, _, gcol in items]
            qbs = [(q_ref[0, rows, cs] * QK_SCALE).astype(BF16) for _, rows, h, cs, _, _ in items]
            kfs = [k_ref[0, rows, cs].astype(BF16) for _, rows, h, cs, _, _ in items]
            kbs = [k_ref[0, rows, cs] * beta for _, rows, h, cs, beta, _ in items]
            kbbs = [kb.astype(BF16) for kb in kbs]
            tsps = [_split(ti_ref[0, h, rows, :]) for _, rows, h, cs, _, _ in items]
            drus = [_dot3(tsp, _split(du_ref[0, rows, cs]), TN)
                    for (_, rows, h, cs, _, _), tsp in zip(items, tsps)]
            drws = [_dot3(tsp, _split(dw_ref[0, rows, cs]), TN)
                    for (_, rows, h, cs, _, _), tsp in zip(items, tsps)]
            m_kks = [_dot(kbb, kf, NT) for kbb, kf in zip(kbbs, kfs)]
            a_qks = [_dot(qb, kf, NT) for qb, kf in zip(qbs, kfs)]
            dls = [-jnp.where(strict, _dot3(_split(dru), _split(u_ref[0, rows, cs]), NT)
                              + _dot3(_split(drw), _split(w_ref[0, rows, cs]), NT), 0.0)
                   for (_, rows, h, cs, _, _), dru, drw in zip(items, drus, drws)]
            dms = [(dl * dc).astype(BF16) for dl, dc in zip(dls, decays)]
            das = [(dqk_ref[0, h, rows, :] * dc).astype(BF16)
                   for (_, rows, h, cs, _, _), dc in zip(items, decays)]
            dkb_mm = [_dot(dm, kf, NN) for dm, kf in zip(dms, kfs)]
            dk_mm = [_dot(dm, kbb, TN) + _dot(da, qb, TN) for dm, kbb, da, qb in zip(dms, kbbs, das, qbs)]
            dqs_mm = [_dot(da, kf, NN) for da, kf in zip(das, kfs)]
            dgc_tiles = [dgs_ref[0, rows, :] for rows, _, _, _ in tiles]
            dbeta_tiles = [jnp.zeros((DN_CHUNK, LANES), F32) for _ in tiles]
            for n_it, (sub, rows, h, cs, beta, gcol) in enumerate(items):
                eg, dc = egs[n_it], decays[n_it]
                k = k_ref[0, rows, cs]
                q = q_ref[0, rows, cs] * QK_SCALE
                kb, dru, drw = kbs[n_it], drus[n_it], drws[n_it]
                ek = jnp.exp(gcol[DN_CHUNK - 1:DN_CHUNK, :] - gcol)
                e_mat = (dls[n_it] * m_kks[n_it] + dqk_ref[0, h, rows, :] * a_qks[n_it]) * dc
                dkb = drw * eg + dkb_mm[n_it]
                dg = (jnp.sum(drw * kb * eg, axis=-1, keepdims=True)
                      + jnp.sum(e_mat, axis=1, keepdims=True)
                      - _row_to_col(jnp.sum(e_mat, axis=0, keepdims=True)))
                dqd = dqd_ref[0, rows, cs]
                dg = dg + jnp.sum(dqd * q * eg, axis=-1, keepdims=True)
                dkd = dkd_ref[0, rows, cs]
                tk_ = jnp.sum(dkd * k * ek, axis=-1, keepdims=True)
                dg = dg - tk_ + jnp.where(last_col, jnp.sum(tk_, axis=0, keepdims=True), 0.0)
                dbeta = (jnp.sum(dkb * k, axis=-1, keepdims=True)
                         + jnp.sum(dru * v_ref[0, rows, cs], axis=-1, keepdims=True))
                dq_ref[0, rows, cs] = (dqs_mm[n_it] + dqd * eg) * QK_SCALE
                dq_ref[0, rows, pl.ds(HALF_W + h * HEAD_DIM, HEAD_DIM)] = dk_mm[n_it] + dkd * ek + dkb * beta
                dq_ref[0, rows, pl.ds(2 * HALF_W + h * HEAD_DIM, HEAD_DIM)] = dru * beta
                dgc_tiles[sub] = dgc_tiles[sub] + jnp.where(lane == N_HEADS + h, dg, 0.0)
                dbeta_tiles[sub] = dbeta_tiles[sub] + jnp.where(lane == h, dbeta, 0.0)
            for (rows, pba, beta_all, g_all), dgc_tile, dbeta_tile in zip(tiles, dgc_tiles, dbeta_tiles):
                dg_tile = _dot_exact_lhs(tri_up, dgc_tile)
                da_pre = dg_tile * (-ea) * _sigmoid(pba + dtb_ref[...])
                dal_ref[...] += jnp.sum(jnp.where(gate_lane, dg_tile * g_all, 0.0), axis=0, keepdims=True)
                ddtb_ref[...] += jnp.sum(jnp.where(gate_lane, da_pre, 0.0), axis=0, keepdims=True)
                dpba_ref[0, rows, :] = jnp.where(lane < N_HEADS, dbeta_tile * beta_all * (1.0 - beta_all),
                                                 jnp.where(gate_lane, da_pre, 0.0)).astype(BF16)
            return carry

        lax.fori_loop(0, n_in // LOCKSTEP_CHUNKS, chunk_group, 0)

    def seg(cblk):
        return pl.BlockSpec((1, rblk, HALF_W), lambda i, r: (i, r, cblk))

    vec = pl.BlockSpec((1, LANES), lambda i, r: (0, 0))
    wide = pl.BlockSpec((1, rblk, HALF_W), lambda i, r: (i, r, 0))
    sq = pl.BlockSpec((1, N_HEADS, rblk, DN_CHUNK), lambda i, r: (i, 0, r, 0))
    gates = pl.BlockSpec((1, rblk, LANES), lambda i, r: (i, r, 0))
    return _call(
        body, name=name, grid=(b, t // rblk),
        in_specs=[seg(0), seg(1), seg(2),
                  pl.BlockSpec((1, rblk, LANES), lambda i, r: (i, r, GATE_COL_BLOCK)), vec, vec,
                  sq, wide, wide, wide, wide, wide, wide, sq, gates, _ANY],
        out_specs=[pl.BlockSpec((1, rblk, 3 * HALF_W), lambda i, r: (i, r, 0)),
                   pl.BlockSpec((1, rblk, LANES), lambda i, r: (i, r, GATE_COL_BLOCK)), vec, vec],
        out_shape=[_sds((b, t, 3 * HALF_W), F32), _sds(dproj3.shape, BF16),
                   _sds((1, LANES), F32), _sds((1, LANES), F32)],
        input_output_aliases={15: 1}, comm=comm,
    )(qkv, qkv, qkv, proj3, alog_row, dtb_row, tinv, u, w, du, dw, dqd, dkd, dqk, dgc_scan, dproj3)


def dn_out_fwd(o, proj, dn_norm, name):
    n = o.shape[0]
    tm = min(ROW_TILE, n)

    def body(o_ref, z_ref, g_ref, y_ref):
        for h in range(N_HEADS):
            cs = slice(h * HEAD_DIM, (h + 1) * HEAD_DIM)
            oh = o_ref[:, cs]
            z = z_ref[:, cs]
            y = oh * _rms_scale(oh) * g_ref[...]
            y_ref[:, cs] = (y * (z * _sigmoid(z))).astype(BF16)

    half = pl.BlockSpec((tm, HALF_W), lambda i: (i, 0))
    return _call(
        body, name=name, grid=(n // tm,),
        in_specs=[half, pl.BlockSpec((tm, HALF_W), lambda i: (i, 5)),
                  pl.BlockSpec((1, HEAD_DIM), lambda i: (0, 0))],
        out_specs=half, out_shape=_sds((n, HALF_W), BF16),
    )(o, proj, dn_norm)


def dn_out_bwd(dy, o, proj, dn_norm, dproj, name):
    n = o.shape[0]
    tm = min(ROW_TILE, n)

    def body(dy_ref, o_ref, z_ref, g_ref, dproj_in, do_ref, dz_ref, dg_ref):
        @pl.when(pl.program_id(0) == 0)
        def _():
            dg_ref[...] = jnp.zeros_like(dg_ref)

        g = g_ref[...]
        dg = jnp.zeros_like(g)
        for h in range(N_HEADS):
            cs = slice(h * HEAD_DIM, (h + 1) * HEAD_DIM)
            oh = o_ref[:, cs]
            z = z_ref[:, cs]
            d = dy_ref[:, cs]
            r = _rms_scale(oh)
            nh = oh * r
            sz = _sigmoid(z)
            dyn = d * (z * sz)
            dz_ref[:, cs] = (d * (nh * g) * (sz * (1.0 + z * (1.0 - sz)))).astype(BF16)
            dg = dg + jnp.sum(dyn * nh, axis=0, keepdims=True)
            dn = dyn * g
            do_ref[:, cs] = r * (dn - nh * jnp.mean(dn * nh, axis=-1, keepdims=True))
        dg_ref[...] += dg

    half = pl.BlockSpec((tm, HALF_W), lambda i: (i, 0))
    vec = pl.BlockSpec((1, HEAD_DIM), lambda i: (0, 0))
    return _call(
        body, name=name, grid=(n // tm,),
        in_specs=[half, half, pl.BlockSpec((tm, HALF_W), lambda i: (i, 5)), vec, _ANY],
        out_specs=[half, pl.BlockSpec((tm, HALF_W), lambda i: (i, 5)), vec],
        out_shape=[_sds((n, HALF_W), F32), _sds(dproj.shape, BF16), _sds((1, HEAD_DIM), F32)],
        input_output_aliases={4: 1},
    )(dy, o, proj, dn_norm, dproj)


def _adamw_math(w, g, m, v):
    m_new = ADAM_B1 * m + (1.0 - ADAM_B1) * g
    v_new = ADAM_B2 * v + (1.0 - ADAM_B2) * (g * g)
    m_hat = m_new / (1.0 - ADAM_B1 ** ADAM_STEP)
    v_hat = v_new / (1.0 - ADAM_B2 ** ADAM_STEP)
    delta = -ADAM_LR * (m_hat / (jnp.sqrt(v_hat) + ADAM_EPS) + ADAM_WD * w)
    return delta, m_new, v_new


def adamw(w, g, m, v, name):
    r, c = w.shape
    tr = r
    for cand in (256, 352):
        if r % cand == 0 and r > cand:
            tr = cand
            break

    def body(w_ref, g_ref, m_ref, v_ref, d_ref, mo_ref, vo_ref):
        d, mn, vn = _adamw_math(w_ref[...], g_ref[...], m_ref[...], v_ref[...])
        d_ref[...] = d
        mo_ref[...] = mn
        vo_ref[...] = vn

    spec = pl.BlockSpec((tr, c), lambda i: (i, 0))
    return _call(
        body, name=name, grid=(r // tr,),
        in_specs=[spec] * 4, out_specs=[spec] * 3, out_shape=[_sds((r, c), F32)] * 3,
    )(w, g, m, v)


def _place():
    return lax.axis_index("x"), lax.axis_index("y"), lax.axis_index("c")


def _other_chips(x, y):
    return [(1 - x, y), (x, 1 - y), (1 - x, 1 - y)]


_ANY = pl.BlockSpec(memory_space=pl.ANY)


def cast_place(w, shard_idx, name):
    r, cols = w.shape
    tr = r // 2

    def body(j_ref, w_ref, o_ref):
        o_ref[0] = w_ref[...].astype(BF16)

    return pl.pallas_call(
        body, name=name,
        grid_spec=pltpu.PrefetchScalarGridSpec(
            num_scalar_prefetch=1, grid=(r // tr,),
            in_specs=[pl.BlockSpec((tr, cols), lambda i, j: (i, 0))],
            out_specs=pl.BlockSpec((1, tr, cols), lambda i, j: (j[0], i, 0))),
        out_shape=_sds((N_SHARD, r, cols), BF16),
        compiler_params=pltpu.CompilerParams(dimension_semantics=("arbitrary",),
                                             vmem_limit_bytes=VMEM_LIMIT),
    )(shard_idx, w)


class Exchange:
    def __init__(self, inputs, out_shape, aliases, sems, phases):
        self.inputs, self.out_shape, self.aliases = list(inputs), list(out_shape), dict(aliases)
        self.sems, self.phases = list(sems), list(phases)


def run_exchange(ex, name):
    def body(*refs):
        n_in, n_out = len(ex.inputs), len(ex.out_shape)
        for _, fn in ex.phases:
            fn(refs[:n_in], refs[n_in:n_in + n_out], refs[n_in + n_out:])

    return _call(body, name=name, in_specs=[_ANY] * len(ex.inputs), out_specs=[_ANY] * len(ex.out_shape),
                 out_shape=ex.out_shape, scratch=ex.sems, input_output_aliases=ex.aliases)(*ex.inputs)


def merge_exchanges(exs):
    inputs, out_shape, sems, aliases, phases, out_slices = [], [], [], {}, [], []
    for ex in exs:
        i0, o0, s0 = len(inputs), len(out_shape), len(sems)
        inputs += ex.inputs
        out_shape += ex.out_shape
        sems += ex.sems
        for k, m in ex.aliases.items():
            aliases[i0 + k] = o0 + m
        si, so, ss = slice(i0, len(inputs)), slice(o0, len(out_shape)), slice(s0, len(sems))
        out_slices.append(so)
        for step, fn in ex.phases:
            phases.append((step, lambda ins, outs, sm, fn=fn, si=si, so=so, ss=ss: fn(ins[si], outs[so], sm[ss])))
    return Exchange(inputs, out_shape, aliases, sems, phases), out_slices


def _dma_sems(*sizes):
    return [pltpu.SemaphoreType.DMA((s,)) for s in sizes]


def gather_exchange(bufs, small=None, relay_step=-2):
    n = len(bufs)
    n_small = 0 if small is None else 1

    def half(outs, a, blk, hc):
        rh = bufs[a].shape[1] // 2
        return outs[a].at[blk, pl.ds(hc * rh, rh), :]

    def ici(outs, sems, a, k, blk, to):
        return pltpu.make_async_remote_copy(
            src_ref=half(outs, a, blk, to[2]), dst_ref=half(outs, a, blk, to[2]), send_sem=sems[0].at[3 * a + k],
            recv_sem=sems[1].at[3 * a + k], device_id=to, device_id_type=MESH)

    def d2d(outs, sems, a, k, blk, hc, to):
        return pltpu.make_async_remote_copy(
            src_ref=half(outs, a, blk, hc), dst_ref=half(outs, a, blk, hc), send_sem=sems[2].at[3 * a + k],
            recv_sem=sems[3].at[3 * a + k], device_id=to, device_id_type=MESH)

    def small_copy(ins, outs, sems, k, blk, to):
        return pltpu.make_async_remote_copy(
            src_ref=ins[n], dst_ref=outs[n].at[blk], send_sem=sems[0].at[3 * n + k],
            recv_sem=sems[1].at[3 * n + k], device_id=to, device_id_type=MESH)

    def start(ins, outs, sems):
        x, y, c = _place()
        j = 2 * x + y
        if n_small:
            pltpu.make_async_copy(ins[n], outs[n].at[j], sems[4].at[0]).start()
        for k, (px, py) in enumerate(_other_chips(x, y)):
            if n_small:
                small_copy(ins, outs, sems, k, j, (px, py, c)).start()
            for a in range(n):
                ici(outs, sems, a, k, j, (px, py, c)).start()

    def relay(ins, outs, sems):
        x, y, c = _place()
        for k, (px, py) in enumerate(_other_chips(x, y)):
            for a in range(n):
                ici(outs, sems, a, k, 2 * px + py, (px, py, c)).wait_recv()
                d2d(outs, sems, a, k, 2 * px + py, c, (x, y, 1 - c)).start()

    def finish(ins, outs, sems):
        x, y, c = _place()
        j = 2 * x + y
        for k, (px, py) in enumerate(_other_chips(x, y)):
            blk = 2 * px + py
            if n_small:
                small_copy(ins, outs, sems, k, blk, (px, py, c)).wait_recv()
                small_copy(ins, outs, sems, k, j, (px, py, c)).wait_send()
            for a in range(n):
                d2d(outs, sems, a, k, blk, 1 - c, (x, y, 1 - c)).wait_recv()
                ici(outs, sems, a, k, j, (px, py, c)).wait_send()
                d2d(outs, sems, a, k, blk, c, (x, y, 1 - c)).wait_send()
        if n_small:
            pltpu.make_async_copy(ins[n], outs[n].at[j], sems[4].at[0]).wait()

    out_shape = [_sds(b.shape, b.dtype) for b in bufs]
    if n_small:
        out_shape.append(_sds((N_SHARD,) + small.shape, small.dtype))
    return Exchange(list(bufs) + ([small] if n_small else []), out_shape, {a: a for a in range(n)},
                    _dma_sems(3 * n + 3, 3 * n + 3, 3 * n, 3 * n, 1),
                    [(0, start), (relay_step, relay), (-1, finish)])


def _start_then_wait(copies):
    def start(ins, outs, sems):
        for sent, _ in copies(ins, outs, sems):
            sent().start()

    def finish(ins, outs, sems):
        pairs = copies(ins, outs, sems)
        for _, arrival in pairs:
            arrival().wait_recv()
        for sent, _ in pairs:
            sent().wait_send()

    return [(0, start), (-1, finish)]


def pair_exchange(arrs):
    n = len(arrs)

    def copies(ins, outs, sems):
        x, y, c = _place()
        res = []
        for a in range(n):
            def mk(a=a):
                rh = arrs[a].shape[1] // 2
                return pltpu.make_async_remote_copy(
                    src_ref=ins[a].at[:, pl.ds((1 - c) * rh, rh), :], dst_ref=outs[a], send_sem=sems[0].at[a],
                    recv_sem=sems[1].at[a], device_id=(x, y, 1 - c), device_id_type=MESH)
            res.append((mk, mk))
        return res

    return Exchange(arrs, [_sds((a.shape[0], a.shape[1] // 2, a.shape[2]), a.dtype) for a in arrs], {},
                    _dma_sems(n, n), _start_then_wait(copies))


def pair_add(g, s, c_idx, name):
    nb, r, cols = g.shape
    rh = r // 2

    def body(c_ref, g_ref, s_ref, o_ref):
        o_ref[...] = (g_ref[...] + s_ref[...]).astype(BF16)

    return pl.pallas_call(
        body, name=name,
        grid_spec=pltpu.PrefetchScalarGridSpec(
            num_scalar_prefetch=1, grid=(nb,),
            in_specs=[pl.BlockSpec((1, rh, cols), lambda j, c: (j, c[0], 0)),
                      pl.BlockSpec((1, rh, cols), lambda j, c: (j, 0, 0))],
            out_specs=pl.BlockSpec((1, rh, cols), lambda j, c: (j, 0, 0))),
        out_shape=_sds((nb, rh, cols), BF16),
        compiler_params=pltpu.CompilerParams(dimension_semantics=("arbitrary",),
                                             vmem_limit_bytes=VMEM_LIMIT),
    )(c_idx, g, s)


def chip_exchange(arrs):
    n = len(arrs)

    def copies(ins, outs, sems):
        x, y, c = _place()
        j = 2 * x + y
        res = []
        for a in range(n):
            for k, (px, py) in enumerate(_other_chips(x, y)):
                def mk(src_blk, dst_blk, a=a, k=k, to=(px, py, c)):
                    return pltpu.make_async_remote_copy(
                        src_ref=ins[a].at[src_blk], dst_ref=outs[a].at[dst_blk], send_sem=sems[0].at[3 * a + k],
                        recv_sem=sems[1].at[3 * a + k], device_id=to, device_id_type=MESH)
                res.append((functools.partial(mk, 2 * px + py, j), functools.partial(mk, j, 2 * px + py)))
        return res

    return Exchange(arrs, [_sds(a.shape, a.dtype) for a in arrs], {}, _dma_sems(3 * n, 3 * n),
                    _start_then_wait(copies))


def sum_chips(r, p, shard_idx, name):
    nb, rh, cols = r.shape
    tr = rh // 2

    def body(j_ref, p_ref, *refs):
        o_ref = refs[nb]
        j = j_ref[0]
        acc = None
        for i in range(nb):
            term = jnp.where(j == i, p_ref[0], refs[i][0]).astype(F32)
            acc = term if acc is None else acc + term
        o_ref[...] = acc

    def slot(i):
        return pl.BlockSpec((1, tr, cols), lambda t, j: (jnp.where(j[0] == i, (i + 1) % nb, i), t, 0))

    return pl.pallas_call(
        body, name=name,
        grid_spec=pltpu.PrefetchScalarGridSpec(
            num_scalar_prefetch=1, grid=(rh // tr,),
            in_specs=[pl.BlockSpec((1, tr, cols), lambda t, j: (j[0], t, 0))] + [slot(i) for i in range(nb)],
            out_specs=pl.BlockSpec((tr, cols), lambda t, j: (t, 0))),
        out_shape=_sds((rh, cols), F32),
        compiler_params=pltpu.CompilerParams(dimension_semantics=("arbitrary",),
                                             vmem_limit_bytes=VMEM_LIMIT),
    )(shard_idx, p, *([r] * nb))


def pair_swap(arrs):
    n = len(arrs)

    def copies(ins, outs, sems):
        x, y, c = _place()
        res = []
        for a in range(n):
            def mk(a=a):
                return pltpu.make_async_remote_copy(
                    src_ref=ins[a], dst_ref=outs[a], send_sem=sems[0].at[a], recv_sem=sems[1].at[a],
                    device_id=(x, y, 1 - c), device_id_type=MESH)
            res.append((mk, mk))
        return res

    return Exchange(arrs, [_sds(a.shape, a.dtype) for a in arrs], {}, _dma_sems(n, n),
                    _start_then_wait(copies))


def adamw_pair(w, g_mine, g_sib, m, v, c_idx, name):
    r, cols = w.shape
    rh = r // 2
    tr = rh // 2
    nh = rh // tr

    def body(c_ref, w_ref, gm_ref, gs_ref, m_ref, v_ref, g_ref, d_ref, mo_ref, vo_ref):
        mine = (pl.program_id(0) // nh) == c_ref[0]
        g = jnp.where(mine, gm_ref[...], gs_ref[...])
        d, mn, vn = _adamw_math(w_ref[...], g, m_ref[...], v_ref[...])
        g_ref[...] = g
        d_ref[...] = d
        mo_ref[...] = mn
        vo_ref[...] = vn

    full = pl.BlockSpec((tr, cols), lambda i, c: (i, 0))
    part = pl.BlockSpec((tr, cols), lambda i, c: (i % nh, 0))
    return pl.pallas_call(
        body, name=name,
        grid_spec=pltpu.PrefetchScalarGridSpec(
            num_scalar_prefetch=1, grid=(r // tr,),
            in_specs=[full, part, part, full, full], out_specs=[full] * 4),
        out_shape=[_sds((r, cols), F32)] * 4,
        compiler_params=pltpu.CompilerParams(dimension_semantics=("arbitrary",),
                                             vmem_limit_bytes=VMEM_LIMIT),
    )(c_idx, w, g_mine, g_sib, m, v)


N_DEV = 8


def device_gather(pack):
    def copies(ins, outs, sems):
        x, y, c = _place()
        me = 4 * x + 2 * y + c
        res = []
        for k in range(1, N_DEV):
            fx, fy, fc = (k >> 2) & 1, (k >> 1) & 1, k & 1
            px, py, pc = (1 - x if fx else x, 1 - y if fy else y, 1 - c if fc else c)

            def mk(slot, k=k, to=(px, py, pc)):
                return pltpu.make_async_remote_copy(
                    src_ref=ins[0], dst_ref=outs[0].at[slot], send_sem=sems[0].at[k - 1],
                    recv_sem=sems[1].at[k - 1], device_id=to, device_id_type=MESH)
            res.append((functools.partial(mk, me), functools.partial(mk, 4 * px + 2 * py + pc)))
        return res

    return Exchange([pack], [_sds((N_DEV,) + pack.shape, pack.dtype)], {}, _dma_sems(N_DEV - 1, N_DEV - 1),
                    _start_then_wait(copies))


def sum_devices(buf, pack, me_idx, name):
    r, cols = pack.shape

    def body(me_ref, p_ref, *refs):
        o_ref = refs[N_DEV]
        acc = None
        for i in range(N_DEV):
            term = jnp.where(me_ref[0] == i, p_ref[...], refs[i][0])
            acc = term if acc is None else acc + term
        o_ref[...] = acc

    def slot(i):
        return pl.BlockSpec((1, r, cols), lambda t, me: (jnp.where(me[0] == i, (i + 1) % N_DEV, i), 0, 0))

    whole = pl.BlockSpec((r, cols), lambda t, me: (0, 0))
    return pl.pallas_call(
        body, name=name,
        grid_spec=pltpu.PrefetchScalarGridSpec(
            num_scalar_prefetch=1, grid=(1,),
            in_specs=[whole] + [slot(i) for i in range(N_DEV)], out_specs=whole),
        out_shape=_sds((r, cols), F32),
        compiler_params=pltpu.CompilerParams(dimension_semantics=("arbitrary",),
                                             vmem_limit_bytes=VMEM_LIMIT),
    )(me_idx, pack, *([buf] * N_DEV))


SMALL_NAMES = ("ffn1_norm", "mix_norm", "ffn2_norm", "final_norm", "sg_ln_g", "sg_ln_b",
               "dn_norm", "a_log", "dt_bias", "sg_b", "sg_w", "conv_w")


def _to_rows(a):
    flat = a.reshape(-1)
    pad = (-flat.shape[0]) % LANES
    if pad:
        flat = jnp.pad(flat, (0, pad))
    return flat.reshape(-1, LANES)


def _pack_small(parts):
    rows = [_to_rows(parts[k]) for k in SMALL_NAMES]
    pack = jnp.concatenate(rows, axis=0)
    pad = (-pack.shape[0]) % 8
    if pad:
        pack = jnp.pad(pack, ((0, pad), (0, 0)))
    return pack


def _unpack_small(pack, shapes):
    out, r0 = {}, 0
    for k in SMALL_NAMES:
        size = 1
        for s in shapes[k]:
            size *= s
        nrows = -(-size // LANES)
        out[k] = pack[r0:r0 + nrows].reshape(-1)[:size].reshape(shapes[k])
        r0 += nrows
    return out


def kernel(x, ffn1_norm, ffn1_w_gate, ffn1_w_up, ffn1_w_down, mix_norm, w_in, conv_w, a_log, dt_bias, dn_norm, sg_ln_g, sg_ln_b, sg_w, sg_b, w_out, ffn2_norm, ffn2_w_gate, ffn2_w_up, ffn2_w_down, final_norm, loss_target, m_ffn1_norm, m_ffn1_w_gate, m_ffn1_w_up, m_ffn1_w_down, m_mix_norm, m_w_in, m_conv_w, m_a_log, m_dt_bias, m_dn_norm, m_sg_ln_g, m_sg_ln_b, m_sg_w, m_sg_b, m_w_out, m_ffn2_norm, m_ffn2_w_gate, m_ffn2_w_up, m_ffn2_w_down, m_final_norm, v_ffn1_norm, v_ffn1_w_gate, v_ffn1_w_up, v_ffn1_w_down, v_mix_norm, v_w_in, v_conv_w, v_a_log, v_dt_bias, v_dn_norm, v_sg_ln_g, v_sg_ln_b, v_sg_w, v_sg_b, v_w_out, v_ffn2_norm, v_ffn2_w_gate, v_ffn2_w_up, v_ffn2_w_down, v_final_norm):
    bsz, t_len, d = x.shape
    n = bsz * t_len
    xy, yy, cc = _place()
    shard = 2 * xy + yy

    big_names = ["ffn1_w_gate", "ffn1_w_up", "ffn1_w_down", "w_in", "w_out",
                 "ffn2_w_gate", "ffn2_w_up", "ffn2_w_down"]
    big_w = dict(ffn1_w_gate=ffn1_w_gate, ffn1_w_up=ffn1_w_up, ffn1_w_down=ffn1_w_down, w_in=w_in,
                 w_out=w_out, ffn2_w_gate=ffn2_w_gate, ffn2_w_up=ffn2_w_up, ffn2_w_down=ffn2_w_down)
    big_m = dict(ffn1_w_gate=m_ffn1_w_gate, ffn1_w_up=m_ffn1_w_up, ffn1_w_down=m_ffn1_w_down, w_in=m_w_in,
                 w_out=m_w_out, ffn2_w_gate=m_ffn2_w_gate, ffn2_w_up=m_ffn2_w_up, ffn2_w_down=m_ffn2_w_down)
    big_v = dict(ffn1_w_gate=v_ffn1_w_gate, ffn1_w_up=v_ffn1_w_up, ffn1_w_down=v_ffn1_w_down, w_in=v_w_in,
                 w_out=v_w_out, ffn2_w_gate=v_ffn2_w_gate, ffn2_w_up=v_ffn2_w_up, ffn2_w_down=v_ffn2_w_down)
    shard_idx = jnp.reshape(shard, (1,)).astype(jnp.int32)
    c_idx = jnp.reshape(cc, (1,)).astype(jnp.int32)
    transposed = ("ffn1_w_gate", "ffn1_w_up", "ffn2_w_gate", "ffn2_w_up")

    def as2d(a, k):
        return a[0].T if k in transposed else a[0]

    def from2d(a, k):
        return a.T[None] if k in transposed else a[None]

    placed = {k: cast_place(as2d(big_w[k], k), shard_idx, name="cast_" + k) for k in big_names}
    first_names = big_names[:3]
    later_names = big_names[3:]
    res = run_exchange(gather_exchange([placed[k] for k in first_names], conv_w[0]), name="gather_first")
    gw = dict(zip(first_names, res[:3]))
    conv_full = res[3].transpose(1, 0, 2).reshape(CONV_K, 3 * HALF_W)

    x0 = x.reshape(n, d)
    (x1, h1, gate1, up1), later = ffn_fwd(
        x0, ffn1_norm, gw["ffn1_w_gate"], gw["ffn1_w_up"], gw["ffn1_w_down"], name="ffn1_fwd",
        comm=gather_exchange([placed[k] for k in later_names]))
    gw.update(zip(later_names, later))
    w_in_full = gw["w_in"].transpose(1, 0, 2).reshape(d, IN_COLS)
    w_in_full = jnp.pad(w_in_full, ((0, 0), (0, PROJ_W - IN_COLS)))
    w_out_full = gw["w_out"].reshape(2 * HALF_W, d)
    proj, h2 = in_proj_fwd(x1, mix_norm, w_in_full, name="in_proj_fwd")
    proj3 = proj.reshape(bsz, t_len, PROJ_W)
    bias_tile = jnp.repeat(sg_b[0].T, SG_GROUP_DIM, axis=1)
    sg_out = sg_fwd(proj, sg_ln_g, sg_ln_b, sg_w[0], bias_tile, name="sg_fwd")
    qkv = dn_conv_fwd(proj3, conv_full, name="dn_conv_fwd")
    alog_row = jnp.zeros((1, LANES), F32).at[0, N_HEADS:2 * N_HEADS].set(a_log[0])
    dtb_row = jnp.zeros((1, LANES), F32).at[0, N_HEADS:2 * N_HEADS].set(dt_bias[0])
    u_wy, w_wy, q_dec, k_dec, qk, tinv, gc = dn_chunk_fwd(qkv, proj3, alog_row, dtb_row,
                                                           name="dn_chunk_fwd")
    o, s_in = dn_scan_fwd(u_wy, w_wy, q_dec, k_dec, qk, gc, name="dn_scan_fwd")
    dn_out = dn_out_fwd(o.reshape(n, HALF_W), proj, dn_norm, name="dn_out_fwd")
    x2 = out_proj_fwd(x1, sg_out, dn_out, w_out_full, name="out_proj_fwd")
    x3, h3, gate2, up2 = ffn_fwd(x2, ffn2_norm, gw["ffn2_w_gate"], gw["ffn2_w_up"],
                                 gw["ffn2_w_down"], name="ffn2_fwd")
    dx3, d_final_norm, loss_tile = final_loss(x3, final_norm.reshape(1, d),
                                              loss_target.reshape(n, d), name="final_loss")
    loss = lax.psum(loss_tile[0, 0], ("x", "y", "c"))

    dx2, dgate2, dup2, act2, dyh2, d_ffn2_norm = ffn_bwd_act(
        dx3, x2, ffn2_norm, gate2, up2, gw["ffn2_w_gate"], gw["ffn2_w_up"], gw["ffn2_w_down"],
        name="ffn2_bwd_act")
    g_big = {}
    g_big["ffn2_w_gate"], g_big["ffn2_w_up"], g_big["ffn2_w_down"] = ffn_bwd_w(
        h3, dyh2, dgate2, dup2, act2, name="ffn2_bwd_w")

    early = ["ffn2_w_gate", "ffn2_w_up", "ffn2_w_down"]
    (d_sg, d_dn, dx2b), early_sib = out_proj_bwd_x(dx2, w_out_full, name="out_proj_bwd_x",
                                                   comm=pair_exchange([g_big[k] for k in early]))
    early_sums = [pair_add(g_big[k], s, c_idx, name="grad_pair_add_" + k) for k, s in zip(early, early_sib)]
    g_w_out = jnp.concatenate([matmul_tn(sg_out, dx2b, d, name="w_out_grad_sg"),
                               matmul_tn(dn_out, dx2b, d, name="w_out_grad_dn")], axis=0)
    g_big["w_out"] = g_w_out.reshape(N_SHARD, (2 * HALF_W) // N_SHARD, d)

    d_proj, d_sg_w, d_bias_tile, d_ln_g, d_ln_b = sg_bwd(d_sg, proj, sg_ln_g, sg_ln_b, sg_w[0],
                                                         bias_tile, name="sg_bwd")
    d_o, d_proj, d_dn_norm = dn_out_bwd(d_dn, o.reshape(n, HALF_W), proj, dn_norm, d_proj,
                                        name="dn_out_bwd")
    du, dw, dqd, dkd, dqk, dgc_scan = dn_scan_bwd(d_o.reshape(bsz, t_len, HALF_W), u_wy, w_wy, q_dec,
                                                  k_dec, qk, gc, s_in, name="dn_scan_bwd")
    (d_qkv, d_proj3, d_alog_row, d_dtb_row), early_chips = dn_chunk_bwd(
        qkv, proj3, alog_row, dtb_row, tinv, u_wy, w_wy, du, dw, dqd, dkd, dqk, dgc_scan,
        d_proj.reshape(bsz, t_len, PROJ_W), name="dn_chunk_bwd", comm=chip_exchange(early_sums))
    early_halves = [sum_chips(r, p, shard_idx, name="grad_chip_sum_" + k)
                    for k, r, p in zip(early, early_chips, early_sums)]
    (d_proj3, d_conv), early_sib_halves = dn_conv_bwd(d_qkv, proj3, conv_full, d_proj3, name="dn_conv_bwd",
                                                      comm=pair_swap(early_halves))
    d_proj = d_proj3.reshape(n, PROJ_W)
    dx1, d_mix_norm = in_proj_bwd_x(d_proj, w_in_full, x1, mix_norm, dx2, name="in_proj_bwd_x")
    g_w_in = matmul_tn(h2, d_proj, 640, name="w_in_grad")[:, :IN_COLS]
    g_big["w_in"] = g_w_in.reshape(d, N_SHARD, IN_COLS // N_SHARD).transpose(1, 0, 2)

    mid = ["w_in", "w_out"]
    (dx0, dgate1, dup1, act1, dyh1, d_ffn1_norm), mid_sib = ffn_bwd_act(
        dx1, x0, ffn1_norm, gate1, up1, gw["ffn1_w_gate"], gw["ffn1_w_up"], gw["ffn1_w_down"],
        name="ffn1_bwd_act", comm=pair_exchange([g_big[k] for k in mid]))
    mid_sums = [pair_add(g_big[k], s, c_idx, name="grad_pair_add_" + k) for k, s in zip(mid, mid_sib)]
    d_sg_b = d_bias_tile.reshape(SG_CHUNK, SG_GROUPS, SG_GROUP_DIM).sum(axis=-1).T
    small_g = dict(ffn1_norm=d_ffn1_norm, mix_norm=d_mix_norm, ffn2_norm=d_ffn2_norm,
                   final_norm=d_final_norm, sg_ln_g=d_ln_g, sg_ln_b=d_ln_b, dn_norm=d_dn_norm,
                   a_log=d_alog_row[:, N_HEADS:2 * N_HEADS], dt_bias=d_dtb_row[:, N_HEADS:2 * N_HEADS],
                   sg_b=d_sg_b, sg_w=d_sg_w, conv_w=d_conv)
    my_pack = _pack_small(small_g)
    hosted, parts = merge_exchanges([chip_exchange(mid_sums), device_gather(my_pack)])
    f1_grads, hosted_res = ffn_bwd_w(h1, dyh1, dgate1, dup1, act1, name="ffn1_bwd_w", comm=hosted)
    g_big["ffn1_w_gate"], g_big["ffn1_w_up"], g_big["ffn1_w_down"] = f1_grads
    mid_chips, (all_packs,) = hosted_res[parts[0]], hosted_res[parts[1]]
    mid_halves = [sum_chips(r, p, shard_idx, name="grad_chip_sum_" + k)
                  for k, r, p in zip(mid, mid_chips, mid_sums)]
    grad_x = dx0.reshape(bsz, t_len, d)

    late = [k for k in big_names if k not in early and k not in mid]
    g_list = [g_big[k] for k in late]
    from_sibling = run_exchange(pair_exchange(g_list), name="grad_pair_exchange")
    pair_sums = [pair_add(g, s, c_idx, name="grad_pair_add_" + k)
                 for k, g, s in zip(late, g_list, from_sibling)]
    from_chips = run_exchange(chip_exchange(pair_sums), name="grad_chip_exchange")
    halves = [sum_chips(r, p, shard_idx, name="grad_chip_sum_" + k)
              for k, r, p in zip(late, from_chips, pair_sums)]
    sib_halves = run_exchange(pair_swap(mid_halves + halves), name="grad_pair_swap")
    outs = {}
    for k, g_mine, g_sib in zip(early + mid + late, early_halves + mid_halves + halves,
                                list(early_sib_halves) + list(sib_halves)):
        res = adamw_pair(as2d(big_w[k], k), g_mine, g_sib, as2d(big_m[k], k), as2d(big_v[k], k), c_idx,
                         name="adamw_" + k)
        outs[k] = tuple(from2d(a, k) for a in res)

    small_w = dict(ffn1_norm=ffn1_norm, mix_norm=mix_norm, ffn2_norm=ffn2_norm, final_norm=final_norm,
                   sg_ln_g=sg_ln_g, sg_ln_b=sg_ln_b, dn_norm=dn_norm, a_log=a_log, dt_bias=dt_bias,
                   sg_b=sg_b, sg_w=sg_w)
    small_m = dict(ffn1_norm=m_ffn1_norm, mix_norm=m_mix_norm, ffn2_norm=m_ffn2_norm,
                   final_norm=m_final_norm, sg_ln_g=m_sg_ln_g, sg_ln_b=m_sg_ln_b, dn_norm=m_dn_norm,
                   a_log=m_a_log, dt_bias=m_dt_bias, sg_b=m_sg_b, sg_w=m_sg_w)
    small_v = dict(ffn1_norm=v_ffn1_norm, mix_norm=v_mix_norm, ffn2_norm=v_ffn2_norm,
                   final_norm=v_final_norm, sg_ln_g=v_sg_ln_g, sg_ln_b=v_sg_ln_b, dn_norm=v_dn_norm,
                   a_log=v_a_log, dt_bias=v_dt_bias, sg_b=v_sg_b, sg_w=v_sg_w)
    shapes = {k: small_w[k].shape for k in small_w}
    shapes["conv_w"] = (CONV_K, 3 * HALF_W)
    me_idx = jnp.reshape(4 * xy + 2 * yy + cc, (1,)).astype(jnp.int32)
    g_pack = sum_devices(all_packs, my_pack, me_idx, name="small_sum")
    g_small = _unpack_small(g_pack, shapes)
    cw = 3 * HALF_W // N_SHARD
    g_conv = lax.dynamic_slice_in_dim(g_small["conv_w"], shard * cw, cw, axis=1)
    zero_conv = jnp.zeros((CONV_K, 3 * HALF_W), F32)

    def packed(src, conv):
        parts = dict(src)
        parts["conv_w"] = lax.dynamic_update_slice_in_dim(zero_conv, conv[0], shard * cw, axis=1)
        return _pack_small(parts)

    d_pack, m_pack, v_pack = adamw(packed(small_w, conv_w), g_pack, packed(small_m, m_conv_w),
                                   packed(small_v, v_conv_w), name="adamw_small")
    d_small = _unpack_small(d_pack, shapes)
    m_small = _unpack_small(m_pack, shapes)
    v_small = _unpack_small(v_pack, shapes)

    def conv_block(full_arr):
        return lax.dynamic_slice_in_dim(full_arr, shard * cw, cw, axis=1)[None]

    for k in small_w:
        outs[k] = (g_small[k].reshape(small_w[k].shape), d_small[k], m_small[k], v_small[k])
    outs["conv_w"] = (g_conv[None], conv_block(d_small["conv_w"]), conv_block(m_small["conv_w"]),
                      conv_block(v_small["conv_w"]))

    order = ["ffn1_norm", "ffn1_w_gate", "ffn1_w_up", "ffn1_w_down", "mix_norm", "w_in", "conv_w",
             "a_log", "dt_bias", "dn_norm", "sg_ln_g", "sg_ln_b", "sg_w", "sg_b", "w_out", "ffn2_norm",
             "ffn2_w_gate", "ffn2_w_up", "ffn2_w_down", "final_norm"]
    return (loss, grad_x, *[outs[k][0] for k in order], *[outs[k][1] for k in order],
            *[outs[k][2] for k in order], *[outs[k][3] for k in order])
```

```python
import functools

import jax
import jax.numpy as jnp
from jax import lax
from jax.experimental import pallas as pl
from jax.experimental.pallas import tpu as pltpu

F32 = jnp.float32
BF16 = jnp.bfloat16
EPS = 1e-6

D_MODEL = 1024
N_SHARD = 4
HEAD_DIM = 128
N_HEADS = 4
DN_CHUNK = 64
SG_CHUNK = 128
SG_GROUPS = 8
SG_GROUP_DIM = 64
HALF_W = 512
PROJ_W = 3200
IN_COLS = 3080
GATE_COL_BLOCK = 24
QK_SCALE = HEAD_DIM ** -0.5
LANES = 128

ADAM_LR = 0.001
ADAM_B1 = 0.9
ADAM_B2 = 0.999
ADAM_EPS = 1e-08
ADAM_WD = 0.01
ADAM_STEP = 10

VMEM_LIMIT = 56 * 1024 * 1024
ROW_TILE = 512

NN = ((1,), (0,))
NT = ((1,), (1,))
TN = ((0,), (0,))
MESH = pl.DeviceIdType.MESH


def _dot(a, b, dims):
    return lax.dot_general(a, b, (dims, ((), ())), preferred_element_type=F32)


def _bdot(a, b, dims):
    return _dot(a.astype(BF16), b.astype(BF16), dims)


def _split(a):
    hi = a.astype(BF16)
    lo = (a - hi.astype(F32)).astype(BF16)
    return hi, lo


def _dot3(a, b, dims=NN):
    return _dot(a[0], b[0], dims) + (_dot(a[0], b[1], dims) + _dot(a[1], b[0], dims))


def _dot_exact_lhs(a, b):
    ab = a.astype(BF16)
    b1 = b.astype(BF16)
    r1 = b - b1.astype(F32)
    b2 = r1.astype(BF16)
    b3 = (r1 - b2.astype(F32)).astype(BF16)
    return _dot(ab, b1, NN) + (_dot(ab, b2, NN) + _dot(ab, b3, NN))


def _call(body, *, name, out_shape, in_specs, out_specs, grid=(), scratch=(), comm=None, **kw):
    params = dict(vmem_limit_bytes=VMEM_LIMIT)
    if grid:
        params["dimension_semantics"] = ("arbitrary",) * len(grid)
    if comm is None:
        return pl.pallas_call(
            body, name=name, grid=grid, in_specs=in_specs, out_specs=out_specs,
            out_shape=out_shape, scratch_shapes=list(scratch),
            compiler_params=pltpu.CompilerParams(**params), **kw)

    n_in, n_out, n_sc = len(in_specs), len(out_specs), len(scratch)
    c_in, c_out = len(comm.inputs), len(comm.out_shape)
    steps = 1
    for g in grid:
        steps *= g

    def hosted(*refs):
        ins, cins = refs[:n_in], refs[n_in:n_in + c_in]
        o0 = n_in + c_in
        outs, couts = refs[o0:o0 + n_out], refs[o0 + n_out:o0 + n_out + c_out]
        s0 = o0 + n_out + c_out
        sc, csems = refs[s0:s0 + n_sc], refs[s0 + n_sc:]
        lin = 0
        for axis, g in enumerate(grid):
            lin = lin * g + pl.program_id(axis)

        def at(step, fn):
            @pl.when(lin == step % steps)
            def _():
                fn(cins, couts, csems)

        for step, fn in comm.phases:
            if step >= 0:
                at(step, fn)
        body(*ins, *outs, *sc)
        for step, fn in comm.phases:
            if step < 0:
                at(step, fn)

    aliases = dict(kw.pop("input_output_aliases", {}))
    for k, m in comm.aliases.items():
        aliases[n_in + k] = n_out + m
    call = pl.pallas_call(
        hosted, name=name, grid=grid, in_specs=list(in_specs) + [_ANY] * c_in,
        out_specs=list(out_specs) + [_ANY] * c_out, out_shape=list(out_shape) + comm.out_shape,
        scratch_shapes=list(scratch) + comm.sems, input_output_aliases=aliases,
        compiler_params=pltpu.CompilerParams(**params), **kw)

    def run(*args):
        res = call(*args, *comm.inputs)
        return res[:n_out], res[n_out:]

    return run


def _sds(shape, dtype):
    return jax.ShapeDtypeStruct(tuple(shape), dtype)


def _resident(shape):
    zeros = (0,) * len(shape)
    return pl.BlockSpec(tuple(shape), lambda *_: zeros, pipeline_mode=pl.Buffered(1))


def _sigmoid(x):
    return jax.nn.sigmoid(x)


def _softplus(x):
    return jnp.maximum(x, 0.0) + jnp.log(1.0 + jnp.exp(-jnp.abs(x)))


_GELU_C = 0.7978845608028654
_GELU_A = 0.044715


def _gelu(x):
    t = jnp.tanh(_GELU_C * (x + _GELU_A * x * x * x))
    return 0.5 * x * (1.0 + t)


def _gelu_grad(x):
    t = jnp.tanh(_GELU_C * (x + _GELU_A * x * x * x))
    return 0.5 * (1.0 + t) + 0.5 * x * (1.0 - t * t) * _GELU_C * (1.0 + 3.0 * _GELU_A * x * x)


def _silu_grad(x):
    s = _sigmoid(x)
    return s * (1.0 + x * (1.0 - s))


def _rms_scale(xv):
    return lax.rsqrt(jnp.mean(xv * xv, axis=-1, keepdims=True) + EPS)


def _rms_bwd(dh, xv, g):
    r = _rms_scale(xv)
    xn = xv * r
    dg = jnp.sum(dh * xn, axis=0, keepdims=True)
    dxn = dh * g
    dx = r * (dxn - xn * jnp.mean(dxn * xn, axis=-1, keepdims=True))
    return dx, dg


def _iota2(shape, dim):
    return lax.broadcasted_iota(jnp.int32, shape, dim)


def _col_to_row(col):
    n = col.shape[0]
    eye = _iota2((n, n), 0) == _iota2((n, n), 1)
    return jnp.sum(jnp.where(eye, col, 0.0), axis=0, keepdims=True)


def _row_to_col(row):
    n = row.shape[1]
    eye = _iota2((n, n), 0) == _iota2((n, n), 1)
    return jnp.sum(jnp.where(eye, row, 0.0), axis=1, keepdims=True)


def ffn_fwd(x, gnorm, wg, wu, wd, name, comm=None):
    n, d = x.shape
    nb, fb, _ = wg.shape
    tm = min(ROW_TILE, n)

    def body(x_ref, g_ref, wg_ref, wu_ref, wd_ref, xo_ref, h_ref, gate_ref, up_ref, acc_ref):
        xv = x_ref[...]
        h = (xv * _rms_scale(xv) * g_ref[...]).astype(BF16)
        h_ref[...] = h
        for j in range(nb):
            gate = _dot(h, wg_ref[j], NT)
            up = _dot(h, wu_ref[j], NT)
            gate_ref[j] = gate.astype(BF16)
            up_ref[j] = up.astype(BF16)
            part = _dot((gate * _sigmoid(gate) * up).astype(BF16), wd_ref[j], NN)
            if j == 0:
                acc_ref[...] = part
            else:
                acc_ref[...] += part
        xo_ref[...] = xv + 0.5 * acc_ref[...]

    row = pl.BlockSpec((tm, d), lambda i: (i, 0))
    blk = pl.BlockSpec((nb, tm, fb), lambda i: (0, i, 0))
    return _call(
        body, name=name, grid=(n // tm,),
        in_specs=[row, pl.BlockSpec((1, d), lambda i: (0, 0))] + [_resident((nb, fb, d))] * 3,
        out_specs=[row, row, blk, blk],
        out_shape=[_sds((n, d), F32), _sds((n, d), BF16),
                   _sds((nb, n, fb), BF16), _sds((nb, n, fb), BF16)],
        scratch=[pltpu.VMEM((tm, d), F32)], comm=comm,
    )(x, gnorm, wg, wu, wd)


def ffn_bwd_act(dy, x, gnorm, gate, up, wg, wu, wd, name, comm=None):
    n, d = x.shape
    nb, fb, _ = wg.shape
    tm = min(ROW_TILE // 2, n)

    def body(dy_ref, x_ref, g_ref, gate_ref, up_ref, wg_ref, wu_ref, wd_ref,
             dx_ref, dgate_ref, dup_ref, act_ref, dyh_ref, dg_ref, acc_ref):
        @pl.when(pl.program_id(0) == 0)
        def _():
            dg_ref[...] = jnp.zeros_like(dg_ref)

        dyh = (0.5 * dy_ref[...]).astype(BF16)
        dyh_ref[...] = dyh
        for j in range(nb):
            dact = _dot(dyh, wd_ref[j], NT)
            gt = gate_ref[j].astype(F32)
            u = up_ref[j].astype(F32)
            s = _sigmoid(gt)
            silu = gt * s
            dup = (dact * silu).astype(BF16)
            dgate = (dact * u * (s * (1.0 + gt * (1.0 - s)))).astype(BF16)
            dup_ref[j] = dup
            dgate_ref[j] = dgate
            act_ref[j] = (silu * u).astype(BF16)
            part = _dot(dgate, wg_ref[j], NN) + _dot(dup, wu_ref[j], NN)
            if j == 0:
                acc_ref[...] = part
            else:
                acc_ref[...] += part
        dxn, dg = _rms_bwd(acc_ref[...], x_ref[...], g_ref[...])
        dx_ref[...] = dy_ref[...] + dxn
        dg_ref[...] += dg

    row = pl.BlockSpec((tm, d), lambda i: (i, 0))
    blk = pl.BlockSpec((nb, tm, fb), lambda i: (0, i, 0))
    vec = pl.BlockSpec((1, d), lambda i: (0, 0))
    wblk = _resident((nb, fb, d))
    return _call(
        body, name=name, grid=(n // tm,),
        in_specs=[row, row, vec, blk, blk, wblk, wblk, wblk],
        out_specs=[row, blk, blk, blk, row, vec],
        out_shape=[_sds((n, d), F32), _sds((nb, n, fb), BF16), _sds((nb, n, fb), BF16),
                   _sds((nb, n, fb), BF16), _sds((n, d), BF16), _sds((1, d), F32)],
        scratch=[pltpu.VMEM((tm, d), F32)], comm=comm,
    )(dy, x, gnorm, gate, up, wg, wu, wd)


def ffn_bwd_w(h, dyh, dgate, dup, act, name, comm=None):
    n, d = h.shape
    nb, _, fb = dgate.shape
    tk = min(2 * ROW_TILE, n)

    def body(h_ref, dyh_ref, dgate_ref, dup_ref, act_ref, dwg_ref, dwu_ref, dwd_ref):
        @pl.when(pl.program_id(1) == 0)
        def _():
            dwg_ref[...] = jnp.zeros_like(dwg_ref)
            dwu_ref[...] = jnp.zeros_like(dwu_ref)
            dwd_ref[...] = jnp.zeros_like(dwd_ref)

        hv = h_ref[...]
        dwg_ref[0] += _dot(dgate_ref[0], hv, TN)
        dwu_ref[0] += _dot(dup_ref[0], hv, TN)
        dwd_ref[0] += _dot(act_ref[0], dyh_ref[...], TN)

    row = pl.BlockSpec((tk, d), lambda j, k: (k, 0))
    blk = pl.BlockSpec((1, tk, fb), lambda j, k: (j, k, 0))
    return _call(
        body, name=name, grid=(nb, n // tk),
        in_specs=[row, row, blk, blk, blk],
        out_specs=[pl.BlockSpec((1, fb, d), lambda j, k: (j, 0, 0))] * 3,
        out_shape=[_sds((nb, fb, d), F32)] * 3, comm=comm,
    )(h, dyh, dgate, dup, act)


def final_loss(x, gnorm, target, name):
    n, d = x.shape
    tm = min(ROW_TILE, n)

    def body(x_ref, g_ref, t_ref, dx_ref, dg_ref, loss_ref):
        @pl.when(pl.program_id(0) == 0)
        def _():
            dg_ref[...] = jnp.zeros_like(dg_ref)
            loss_ref[...] = jnp.zeros_like(loss_ref)

        xv = x_ref[...]
        y = xv * _rms_scale(xv) * g_ref[...]
        err = y - t_ref[...]
        part = 0.5 * jnp.sum(jnp.mean(err * err, axis=-1, keepdims=True), axis=0, keepdims=True)
        loss_ref[...] += jnp.broadcast_to(part, loss_ref.shape)
        dx, dg = _rms_bwd(err * (1.0 / d), xv, g_ref[...])
        dx_ref[...] = dx
        dg_ref[...] += dg

    row = pl.BlockSpec((tm, d), lambda i: (i, 0))
    vec = pl.BlockSpec((1, d), lambda i: (0, 0))
    return _call(
        body, name=name, grid=(n // tm,),
        in_specs=[row, vec, row],
        out_specs=[row, vec, pl.BlockSpec((1, LANES), lambda i: (0, 0))],
        out_shape=[_sds((n, d), F32), _sds((1, d), F32), _sds((1, LANES), F32)],
    )(x, gnorm, target)


def in_proj_fwd(x, gnorm, w, name):
    n, d = x.shape
    cols = w.shape[1]
    tm = min(ROW_TILE, n)
    tn = 640

    def body(x_ref, g_ref, w_ref, p_ref, h_ref):
        xv = x_ref[...]
        h = (xv * _rms_scale(xv) * g_ref[...]).astype(BF16)
        h_ref[...] = h
        for c0 in range(0, cols, tn):
            p_ref[:, c0:c0 + tn] = _dot(h, w_ref[:, c0:c0 + tn], NN)

    return _call(
        body, name=name, grid=(n // tm,),
        in_specs=[pl.BlockSpec((tm, d), lambda i: (i, 0)),
                  pl.BlockSpec((1, d), lambda i: (0, 0)), _resident((d, cols))],
        out_specs=[pl.BlockSpec((tm, cols), lambda i: (i, 0)),
                   pl.BlockSpec((tm, d), lambda i: (i, 0))],
        out_shape=[_sds((n, cols), F32), _sds((n, d), BF16)],
    )(x, gnorm, w)


def in_proj_bwd_x(dproj, w, x, gnorm, dres, name, comm=None):
    n, d = x.shape
    cols = w.shape[1]
    tm = min(ROW_TILE, n)

    def body(dp_ref, w_ref, x_ref, g_ref, dr_ref, dx_ref, dg_ref):
        @pl.when(pl.program_id(0) == 0)
        def _():
            dg_ref[...] = jnp.zeros_like(dg_ref)

        dh = _dot(dp_ref[...], w_ref[...], NT)
        dxn, dg = _rms_bwd(dh, x_ref[...], g_ref[...])
        dx_ref[...] = dr_ref[...] + dxn
        dg_ref[...] += dg

    row = pl.BlockSpec((tm, d), lambda i: (i, 0))
    vec = pl.BlockSpec((1, d), lambda i: (0, 0))
    return _call(
        body, name=name, grid=(n // tm,),
        in_specs=[pl.BlockSpec((tm, cols), lambda i: (i, 0)),
                  _resident((d, cols)), row, vec, row],
        out_specs=[row, vec],
        out_shape=[_sds((n, d), F32), _sds((1, d), F32)], comm=comm,
    )(dproj, w, x, gnorm, dres)


def matmul_tn(a, b, tn, name):
    n, ka = a.shape
    cb = b.shape[1]
    tk = min(ROW_TILE, n)

    def body(a_ref, b_ref, o_ref):
        @pl.when(pl.program_id(0) == 0)
        def _():
            o_ref[...] = jnp.zeros_like(o_ref)

        av = a_ref[...]
        for c0 in range(0, cb, tn):
            o_ref[:, c0:c0 + tn] += _dot(av, b_ref[:, c0:c0 + tn], TN)

    return _call(
        body, name=name, grid=(n // tk,),
        in_specs=[pl.BlockSpec((tk, ka), lambda k: (k, 0)),
                  pl.BlockSpec((tk, cb), lambda k: (k, 0))],
        out_specs=pl.BlockSpec((ka, cb), lambda k: (0, 0)),
        out_shape=_sds((ka, cb), F32),
    )(a, b)


def out_proj_fwd(x, sg_out, dn_out, w, name):
    n, d = x.shape
    tm = min(ROW_TILE, n)

    def body(x_ref, a_ref, b_ref, w_ref, o_ref):
        o_ref[...] = (x_ref[...] + _dot(a_ref[...], w_ref[0:HALF_W, :], NN)
                      + _dot(b_ref[...], w_ref[HALF_W:2 * HALF_W, :], NN))

    row = pl.BlockSpec((tm, d), lambda i: (i, 0))
    half = pl.BlockSpec((tm, HALF_W), lambda i: (i, 0))
    return _call(
        body, name=name, grid=(n // tm,),
        in_specs=[row, half, half, pl.BlockSpec((2 * HALF_W, d), lambda i: (0, 0))],
        out_specs=row, out_shape=_sds((n, d), F32),
    )(x, sg_out, dn_out, w)


def out_proj_bwd_x(dy, w, name, comm=None):
    n, d = dy.shape
    tm = min(ROW_TILE, n)

    def body(dy_ref, w_ref, dsg_ref, ddn_ref, dyb_ref):
        dyb = dy_ref[...].astype(BF16)
        dyb_ref[...] = dyb
        dsg_ref[...] = _dot(dyb, w_ref[0:HALF_W, :], NT)
        ddn_ref[...] = _dot(dyb, w_ref[HALF_W:2 * HALF_W, :], NT)

    row = pl.BlockSpec((tm, d), lambda i: (i, 0))
    half = pl.BlockSpec((tm, HALF_W), lambda i: (i, 0))
    return _call(
        body, name=name, grid=(n // tm,),
        in_specs=[row, pl.BlockSpec((2 * HALF_W, d), lambda i: (0, 0))],
        out_specs=[half, half, row],
        out_shape=[_sds((n, HALF_W), F32), _sds((n, HALF_W), F32), _sds((n, d), BF16)], comm=comm,
    )(dy, w)


def _sg_group_masks():
    col = _iota2((SG_CHUNK, HALF_W), 1)
    return [jnp.logical_and(col >= g * SG_GROUP_DIM, col < (g + 1) * SG_GROUP_DIM)
            for g in range(SG_GROUPS)]


def _sg_causal():
    return _iota2((SG_CHUNK, SG_CHUNK), 0) >= _iota2((SG_CHUNK, SG_CHUNK), 1)


def _sg_forward_chunk(pu, pv, ln_g, ln_b, wc, bias, masks):
    u = _gelu(pu)
    v = _gelu(pv)
    mu = jnp.mean(v, axis=-1, keepdims=True)
    vc = v - mu
    rs = lax.rsqrt(jnp.mean(vc * vc, axis=-1, keepdims=True) + EPS)
    xhat = vc * rs
    vn = (xhat * ln_g + ln_b).astype(BF16)
    vs = bias
    for g in range(SG_GROUPS):
        vs = vs + jnp.where(masks[g], _dot(wc[g], vn, NN), 0.0)
    return u, xhat, rs, vn, vs


def sg_fwd(proj, ln_g, ln_b, w_s, bias_tile, name):
    n = proj.shape[0]
    tm = min(ROW_TILE, n)

    def body(pu_ref, pv_ref, g_ref, b_ref, w_ref, bias_ref, o_ref):
        causal = _sg_causal()
        wc = [jnp.where(causal, w_ref[g], 0.0).astype(BF16) for g in range(SG_GROUPS)]
        masks = _sg_group_masks()
        for ci in range(tm // SG_CHUNK):
            rows = slice(ci * SG_CHUNK, (ci + 1) * SG_CHUNK)
            u, _, _, _, vs = _sg_forward_chunk(pu_ref[rows, :], pv_ref[rows, :], g_ref[...],
                                               b_ref[...], wc, bias_ref[...], masks)
            o_ref[rows, :] = (u * vs).astype(BF16)

    vec = pl.BlockSpec((1, HALF_W), lambda i: (0, 0))
    return _call(
        body, name=name, grid=(n // tm,),
        in_specs=[pl.BlockSpec((tm, HALF_W), lambda i: (i, 0)),
                  pl.BlockSpec((tm, HALF_W), lambda i: (i, 1)), vec, vec,
                  pl.BlockSpec((SG_GROUPS, SG_CHUNK, SG_CHUNK), lambda i: (0, 0, 0)),
                  pl.BlockSpec((SG_CHUNK, HALF_W), lambda i: (0, 0))],
        out_specs=pl.BlockSpec((tm, HALF_W), lambda i: (i, 0)),
        out_shape=_sds((n, HALF_W), BF16),
    )(proj, proj, ln_g, ln_b, w_s, bias_tile)


def sg_bwd(dsg, proj, ln_g, ln_b, w_s, bias_tile, name):
    n = proj.shape[0]
    tm = min(ROW_TILE, n)

    def body(d_ref, pu_ref, pv_ref, g_ref, b_ref, w_ref, bias_ref,
             dp_ref, dw_ref, db_ref, dlg_ref, dlb_ref):
        @pl.when(pl.program_id(0) == 0)
        def _():
            dw_ref[...] = jnp.zeros_like(dw_ref)
            db_ref[...] = jnp.zeros_like(db_ref)
            dlg_ref[...] = jnp.zeros_like(dlg_ref)
            dlb_ref[...] = jnp.zeros_like(dlb_ref)

        causal = _sg_causal()
        wc = [jnp.where(causal, w_ref[g], 0.0).astype(BF16) for g in range(SG_GROUPS)]
        masks = _sg_group_masks()
        ln_g_v = g_ref[...]
        for ci in range(tm // SG_CHUNK):
            rows = slice(ci * SG_CHUNK, (ci + 1) * SG_CHUNK)
            pu = pu_ref[rows, :]
            pv = pv_ref[rows, :]
            u, xhat, rs, vn, vs = _sg_forward_chunk(pu, pv, ln_g_v, b_ref[...], wc,
                                                    bias_ref[...], masks)
            dout = d_ref[rows, :]
            dp_ref[rows, 0:HALF_W] = (dout * vs * _gelu_grad(pu)).astype(BF16)
            dvs = dout * u
            dvs_b = dvs.astype(BF16)
            db_ref[...] += dvs
            dvn = jnp.zeros_like(dvs)
            for g in range(SG_GROUPS):
                dvn = dvn + jnp.where(masks[g], _dot(wc[g], dvs_b, TN), 0.0)
                dwg = _dot(jnp.where(masks[g], dvs_b, jnp.zeros_like(dvs_b)), vn, NT)
                dw_ref[g] += jnp.where(causal, dwg, 0.0)
            dlg_ref[...] += jnp.sum(dvn * xhat, axis=0, keepdims=True)
            dlb_ref[...] += jnp.sum(dvn, axis=0, keepdims=True)
            dxh = dvn * ln_g_v
            dv = rs * (dxh - jnp.mean(dxh, axis=-1, keepdims=True)
                       - xhat * jnp.mean(dxh * xhat, axis=-1, keepdims=True))
            dp_ref[rows, HALF_W:2 * HALF_W] = (dv * _gelu_grad(pv)).astype(BF16)

    vec = pl.BlockSpec((1, HALF_W), lambda i: (0, 0))
    wspec = pl.BlockSpec((SG_GROUPS, SG_CHUNK, SG_CHUNK), lambda i: (0, 0, 0))
    tile = pl.BlockSpec((SG_CHUNK, HALF_W), lambda i: (0, 0))
    return _call(
        body, name=name, grid=(n // tm,),
        in_specs=[pl.BlockSpec((tm, HALF_W), lambda i: (i, 0)),
                  pl.BlockSpec((tm, HALF_W), lambda i: (i, 0)),
                  pl.BlockSpec((tm, HALF_W), lambda i: (i, 1)), vec, vec, wspec, tile],
        out_specs=[pl.BlockSpec((tm, 2 * HALF_W), lambda i: (i, 0)), wspec, tile, vec, vec],
        out_shape=[_sds((n, PROJ_W), BF16), _sds((SG_GROUPS, SG_CHUNK, SG_CHUNK), F32),
                   _sds((SG_CHUNK, HALF_W), F32), _sds((1, HALF_W), F32), _sds((1, HALF_W), F32)],
    )(dsg, proj, proj, ln_g, ln_b, w_s, bias_tile)


CONV_K = 4
CONV_BLOCK = 256


def _shift_down(x, s, row):
    if s == 0:
        return x
    return jnp.where(row >= s, pltpu.roll(x, s, 0), 0.0)


def _shift_up(x, s, row):
    if s == 0:
        return x
    t_len = x.shape[0]
    return jnp.where(row < t_len - s, pltpu.roll(x, t_len - s, 0), 0.0)


def _conv_taps(x, row):
    return [_shift_down(x, CONV_K - 1 - j, row) for j in range(CONV_K)]


def _conv(taps, w):
    y = taps[0] * w[0:1, :]
    for j in range(1, CONV_K):
        y = y + taps[j] * w[j:j + 1, :]
    return y


def dn_conv_fwd(proj3, conv_w, name):
    b, t, _ = proj3.shape
    nblk = 3 * HALF_W // CONV_BLOCK
    first = 2 * HALF_W // CONV_BLOCK
    n_norm = 2 * HALF_W // CONV_BLOCK

    def body(x_ref, w_ref, o_ref):
        s = pl.program_id(1)
        x = x_ref[0]
        y = _conv(_conv_taps(x, _iota2(x.shape, 0)), w_ref[...])
        y = y * _sigmoid(y)

        @pl.when(s < n_norm)
        def _():
            for h in range(CONV_BLOCK // HEAD_DIM):
                cs = slice(h * HEAD_DIM, (h + 1) * HEAD_DIM)
                yh = y[:, cs]
                o_ref[0, :, cs] = yh * lax.rsqrt(jnp.sum(yh * yh, axis=-1, keepdims=True) + EPS)

        @pl.when(s >= n_norm)
        def _():
            o_ref[0] = y

    return _call(
        body, name=name, grid=(b, nblk),
        in_specs=[pl.BlockSpec((1, t, CONV_BLOCK), lambda i, s: (i, 0, first + s)),
                  pl.BlockSpec((CONV_K, CONV_BLOCK), lambda i, s: (0, s))],
        out_specs=pl.BlockSpec((1, t, CONV_BLOCK), lambda i, s: (i, 0, s)),
        out_shape=_sds((b, t, 3 * HALF_W), F32),
    )(proj3, conv_w)


def dn_conv_bwd(dqkv, proj3, conv_w, dproj3, name, comm=None):
    b, t, _ = proj3.shape
    nblk = 3 * HALF_W // CONV_BLOCK
    first = 2 * HALF_W // CONV_BLOCK
    n_norm = 2 * HALF_W // CONV_BLOCK

    def body(d_ref, x_ref, w_ref, dproj_in, dx_ref, dw_ref, ds_ref):
        s = pl.program_id(0)

        @pl.when(pl.program_id(1) == 0)
        def _():
            dw_ref[...] = jnp.zeros_like(dw_ref)

        x = x_ref[0]
        w = w_ref[...]
        row = _iota2(x.shape, 0)
        taps = _conv_taps(x, row)
        c = _conv(taps, w)
        sg = _sigmoid(c)
        y = c * sg

        @pl.when(s < n_norm)
        def _():
            for h in range(CONV_BLOCK // HEAD_DIM):
                cs = slice(h * HEAD_DIM, (h + 1) * HEAD_DIM)
                yh = y[:, cs]
                r = lax.rsqrt(jnp.sum(yh * yh, axis=-1, keepdims=True) + EPS)
                nh = yh * r
                dn = d_ref[0, :, cs]
                ds_ref[:, cs] = r * (dn - nh * jnp.sum(dn * nh, axis=-1, keepdims=True))

        @pl.when(s >= n_norm)
        def _():
            ds_ref[...] = d_ref[0]

        dc = ds_ref[...] * (sg * (1.0 + c * (1.0 - sg)))
        dx = _shift_up(dc, CONV_K - 1, row) * w[0:1, :]
        for j in range(1, CONV_K):
            dx = dx + _shift_up(dc, CONV_K - 1 - j, row) * w[j:j + 1, :]
        dx_ref[0] = dx.astype(BF16)
        for j in range(CONV_K):
            dw_ref[j:j + 1, :] += jnp.sum(dc * taps[j], axis=0, keepdims=True)

    return _call(
        body, name=name, grid=(nblk, b),
        in_specs=[pl.BlockSpec((1, t, CONV_BLOCK), lambda s, i: (i, 0, s)),
                  pl.BlockSpec((1, t, CONV_BLOCK), lambda s, i: (i, 0, first + s)),
                  pl.BlockSpec((CONV_K, CONV_BLOCK), lambda s, i: (0, s)), _ANY],
        out_specs=[pl.BlockSpec((1, t, CONV_BLOCK), lambda s, i: (i, 0, first + s)),
                   pl.BlockSpec((CONV_K, CONV_BLOCK), lambda s, i: (0, s))],
        out_shape=[_sds(dproj3.shape, BF16), _sds((CONV_K, 3 * HALF_W), F32)],
        scratch=[pltpu.VMEM((t, CONV_BLOCK), F32)],
        input_output_aliases={3: 0}, comm=comm,
    )(dqkv, proj3, conv_w, dproj3)


def _chunk_masks():
    ii = _iota2((DN_CHUNK, DN_CHUNK), 0)
    jj = _iota2((DN_CHUNK, DN_CHUNK), 1)
    return ii >= jj, ii > jj, ii == jj


LOCKSTEP_CHUNKS = 2


def _inv_unit_lower_many(l_mats, eye):
    eye_f = jnp.where(eye, 1.0, 0.0)
    ps = [-l for l in l_mats]
    ts = [eye_f + p for p in ps]
    pss = [_split(p) for p in ps]
    size = 2
    while size < DN_CHUNK:
        ps = [_dot3(s, s) for s in pss]
        pss = [_split(p) for p in ps]
        ts = [t + _dot3(_split(t), s) for t, s in zip(ts, pss)]
        size *= 2
    return ts


def _gates(pba, ea_row, dtb_row):
    beta = _sigmoid(pba)
    g = -ea_row * _softplus(pba + dtb_row)
    return beta, g


def _chunk_decay(gcol):
    incl, strict, eye = _chunk_masks()
    grow = jnp.sum(jnp.where(eye, gcol, 0.0), axis=0, keepdims=True)
    decay = jnp.where(incl, jnp.exp(jnp.where(incl, gcol - grow, 0.0)), 0.0)
    return decay, incl, strict, eye


def dn_chunk_fwd(qkv, proj3, alog_row, dtb_row, name):
    b, t, _ = qkv.shape
    rblk = min(256, t)
    n_in = rblk // DN_CHUNK

    def body(q_ref, k_ref, v_ref, pba_ref, al_ref, dtb_ref,
             u_ref, w_ref, qd_ref, kd_ref, qk_ref, ti_ref, gc_ref):
        ea = jnp.exp(al_ref[...])
        tri = jnp.where(_chunk_masks()[0], 1.0, 0.0)

        _, strict, eye = _chunk_masks()

        def chunk_group(cg, carry):
            items = []
            for sub in range(LOCKSTEP_CHUNKS):
                rows = pl.ds(pl.multiple_of((cg * LOCKSTEP_CHUNKS + sub) * DN_CHUNK, DN_CHUNK), DN_CHUNK)
                beta_all, g_all = _gates(pba_ref[0, rows, :], ea, dtb_ref[...])
                gc = _dot_exact_lhs(tri, g_all)
                gc_ref[0, rows, :] = gc
                for h in range(N_HEADS):
                    items.append((rows, h, beta_all[:, h:h + 1], gc[:, N_HEADS + h:N_HEADS + h + 1]))
            ks, kbs, decays, egs = [], [], [], []
            for rows, h, beta, gcol in items:
                cs = slice(h * HEAD_DIM, (h + 1) * HEAD_DIM)
                k = k_ref[0, rows, cs]
                ks.append(k)
                kbs.append(k * beta)
                decays.append(_chunk_decay(gcol)[0])
                egs.append(jnp.exp(gcol))
            ms = [_bdot(kb, k, NT) for kb, k in zip(kbs, ks)]
            tinvs = _inv_unit_lower_many([jnp.where(strict, m * dc, 0.0) for m, dc in zip(ms, decays)], eye)
            tsps = [_split(t) for t in tinvs]
            for (rows, h, beta, gcol), tsp, tinv in zip(items, tsps, tinvs):
                cs = slice(h * HEAD_DIM, (h + 1) * HEAD_DIM)
                u_ref[0, rows, cs] = _dot3(tsp, _split(v_ref[0, rows, cs] * beta))
                ti_ref[0, h, rows, :] = tinv
            for (rows, h, beta, gcol), tsp, kb, eg in zip(items, tsps, kbs, egs):
                cs = slice(h * HEAD_DIM, (h + 1) * HEAD_DIM)
                w_ref[0, rows, cs] = _dot3(tsp, _split(kb * eg))
            for (rows, h, beta, gcol), k, dc, eg in zip(items, ks, decays, egs):
                cs = slice(h * HEAD_DIM, (h + 1) * HEAD_DIM)
                q = q_ref[0, rows, cs] * QK_SCALE
                qk_ref[0, h, rows, :] = _bdot(q, k, NT) * dc
                qd_ref[0, rows, cs] = q * eg
                kd_ref[0, rows, cs] = k * jnp.exp(gcol[DN_CHUNK - 1:DN_CHUNK, :] - gcol)
            return carry

        lax.fori_loop(0, n_in // LOCKSTEP_CHUNKS, chunk_group, 0)

    def seg(cblk):
        return pl.BlockSpec((1, rblk, HALF_W), lambda i, r: (i, r, cblk))

    vec = pl.BlockSpec((1, LANES), lambda i, r: (0, 0))
    wide = pl.BlockSpec((1, rblk, HALF_W), lambda i, r: (i, r, 0))
    sq = pl.BlockSpec((1, N_HEADS, rblk, DN_CHUNK), lambda i, r: (i, 0, r, 0))
    return _call(
        body, name=name, grid=(b, t // rblk),
        in_specs=[seg(0), seg(1), seg(2),
                  pl.BlockSpec((1, rblk, LANES), lambda i, r: (i, r, GATE_COL_BLOCK)), vec, vec],
        out_specs=[wide, wide, wide, wide, sq, sq,
                   pl.BlockSpec((1, rblk, LANES), lambda i, r: (i, r, 0))],
        out_shape=[_sds((b, t, HALF_W), F32)] * 4
        + [_sds((b, N_HEADS, t, DN_CHUNK), F32)] * 2 + [_sds((b, t, LANES), F32)],
    )(qkv, qkv, qkv, proj3, alog_row, dtb_row)


def dn_scan_fwd(u, w, qd, kd, qk, gc, name):
    b, t, _ = u.shape
    nc = t // DN_CHUNK
    bh = b * N_HEADS

    def body(u_ref, w_ref, qd_ref, kd_ref, qk_ref, gc_ref, o_ref, sin_ref, s_ref):
        @pl.when(pl.program_id(0) == 0)
        def _():
            s_ref[...] = jnp.zeros_like(s_ref)

        items = [(bi, h, slice(h * HEAD_DIM, (h + 1) * HEAD_DIM)) for bi in range(b) for h in range(N_HEADS)]
        sbs = []
        for bi, h, cs in items:
            s = s_ref[bi * N_HEADS + h]
            sin_ref[0, bi * N_HEADS + h] = s
            sbs.append(s.astype(BF16))
        ws = [_bdot(w_ref[bi, :, cs], sb, NN) for (bi, h, cs), sb in zip(items, sbs)]
        qs = [_bdot(qd_ref[bi, :, cs], sb, NN) for (bi, h, cs), sb in zip(items, sbs)]
        vbs = [(u_ref[bi, :, cs] - wsi).astype(BF16) for (bi, h, cs), wsi in zip(items, ws)]
        for (bi, h, cs), qsi, vb in zip(items, qs, vbs):
            o_ref[bi, :, cs] = qsi + _bdot(qk_ref[bi, h], vb, NN)
        for (bi, h, cs), vb in zip(items, vbs):
            gl = jnp.exp(gc_ref[bi, DN_CHUNK - 1:DN_CHUNK, N_HEADS + h:N_HEADS + h + 1])
            idx = bi * N_HEADS + h
            s_ref[idx] = s_ref[idx] * gl + _bdot(kd_ref[bi, :, cs], vb, TN)

    wide = pl.BlockSpec((b, DN_CHUNK, HALF_W), lambda c: (0, c, 0))
    return _call(
        body, name=name, grid=(nc,),
        in_specs=[wide, wide, wide, wide,
                  pl.BlockSpec((b, N_HEADS, DN_CHUNK, DN_CHUNK), lambda c: (0, 0, c, 0)),
                  pl.BlockSpec((b, DN_CHUNK, LANES), lambda c: (0, c, 0))],
        out_specs=[wide, pl.BlockSpec((1, bh, HEAD_DIM, HEAD_DIM), lambda c: (c, 0, 0, 0))],
        out_shape=[_sds((b, t, HALF_W), F32), _sds((nc, bh, HEAD_DIM, HEAD_DIM), F32)],
        scratch=[pltpu.VMEM((bh, HEAD_DIM, HEAD_DIM), F32)],
    )(u, w, qd, kd, qk, gc)


def dn_scan_bwd(do, u, w, qd, kd, qk, gc, s_in, name):
    b, t, _ = u.shape
    nc = t // DN_CHUNK
    bh = b * N_HEADS

    def body(do_ref, u_ref, w_ref, qd_ref, kd_ref, qk_ref, gc_ref, sin_ref,
             du_ref, dw_ref, dqd_ref, dkd_ref, dqk_ref, dgc_ref, ds_ref):
        @pl.when(pl.program_id(0) == 0)
        def _():
            ds_ref[...] = jnp.zeros_like(ds_ref)

        last_row = _iota2((DN_CHUNK, LANES), 0) == DN_CHUNK - 1
        lane = _iota2((DN_CHUNK, LANES), 1)
        items = [(bi, h, slice(h * HEAD_DIM, (h + 1) * HEAD_DIM)) for bi in range(b) for h in range(N_HEADS)]
        sbs = [sin_ref[0, bi * N_HEADS + h].astype(BF16) for bi, h, cs in items]
        wvs = [w_ref[bi, :, cs].astype(BF16) for bi, h, cs in items]
        dovs = [do_ref[bi, :, cs].astype(BF16) for bi, h, cs in items]
        dsbs = [ds_ref[bi * N_HEADS + h].astype(BF16) for bi, h, cs in items]
        vbs = [(u_ref[bi, :, cs] - _dot(wv, sb, NN)).astype(BF16)
               for (bi, h, cs), wv, sb in zip(items, wvs, sbs)]
        for (bi, h, cs), dov, sb in zip(items, dovs, sbs):
            dqd_ref[bi, :, cs] = _dot(dov, sb, NT)
        dvns = [_dot(kd_ref[bi, :, cs].astype(BF16), dsb, NN) + _dot(qk_ref[bi, h].astype(BF16), dov, TN)
                for (bi, h, cs), dsb, dov in zip(items, dsbs, dovs)]
        for (bi, h, cs), vb, dsb, dov in zip(items, vbs, dsbs, dovs):
            dkd_ref[bi, :, cs] = _dot(vb, dsb, NT)
            dqk_ref[bi, h] = _dot(dov, vb, NT)
        dgls = []
        for (bi, h, cs), dvn, sb, wv, dov in zip(items, dvns, sbs, wvs, dovs):
            idx = bi * N_HEADS + h
            du_ref[bi, :, cs] = dvn
            dvn_b = dvn.astype(BF16)
            dw_ref[bi, :, cs] = -_dot(dvn_b, sb, NT)
            gl = jnp.exp(gc_ref[bi, DN_CHUNK - 1:DN_CHUNK, N_HEADS + h:N_HEADS + h + 1])
            ds = ds_ref[idx]
            dgl = jnp.sum(jnp.sum(ds * sin_ref[0, idx], axis=1, keepdims=True), axis=0, keepdims=True)
            dgls.append(dgl * gl)
            ds_ref[idx] = (ds * gl + _dot(qd_ref[bi, :, cs].astype(BF16), dov, TN)
                           - _dot(wv, dvn_b, TN))
        for bi in range(b):
            dgc = jnp.zeros((DN_CHUNK, LANES), F32)
            for h in range(N_HEADS):
                dgc = dgc + jnp.where(jnp.logical_and(last_row, lane == N_HEADS + h),
                                      dgls[bi * N_HEADS + h], 0.0)
            dgc_ref[bi] = dgc

    def rev(c):
        return nc - 1 - c

    wide = pl.BlockSpec((b, DN_CHUNK, HALF_W), lambda c: (0, rev(c), 0))
    sq = pl.BlockSpec((b, N_HEADS, DN_CHUNK, DN_CHUNK), lambda c: (0, 0, rev(c), 0))
    gates = pl.BlockSpec((b, DN_CHUNK, LANES), lambda c: (0, rev(c), 0))
    return _call(
        body, name=name, grid=(nc,),
        in_specs=[wide, wide, wide, wide, wide, sq, gates,
                  pl.BlockSpec((1, bh, HEAD_DIM, HEAD_DIM), lambda c: (rev(c), 0, 0, 0))],
        out_specs=[wide, wide, wide, wide, sq, gates],
        out_shape=[_sds((b, t, HALF_W), F32)] * 4
        + [_sds((b, N_HEADS, t, DN_CHUNK), F32), _sds((b, t, LANES), F32)],
        scratch=[pltpu.VMEM((bh, HEAD_DIM, HEAD_DIM), F32)],
    )(do, u, w, qd, kd, qk, gc, s_in)


def dn_chunk_bwd(qkv, proj3, alog_row, dtb_row, tinv, u, w, du, dw, dqd, dkd, dqk, dgc_scan, dproj3, name,
                 comm=None):
    b, t, _ = qkv.shape
    rblk = min(256, t)
    n_in = rblk // DN_CHUNK

    def body(q_ref, k_ref, v_ref, pba_ref, al_ref, dtb_ref, ti_ref, u_ref, w_ref,
             du_ref, dw_ref, dqd_ref, dkd_ref, dqk_ref, dgs_ref, dproj_in,
             dq_ref, dpba_ref, dal_ref, ddtb_ref):
        @pl.when(jnp.logical_and(pl.program_id(0) == 0, pl.program_id(1) == 0))
        def _():
            dal_ref[...] = jnp.zeros_like(dal_ref)
            ddtb_ref[...] = jnp.zeros_like(ddtb_ref)

        ea = jnp.exp(al_ref[...])
        incl0 = _chunk_masks()[0]
        tri = jnp.where(incl0, 1.0, 0.0)
        tri_up = jnp.where(_iota2((DN_CHUNK, DN_CHUNK), 1) >= _iota2((DN_CHUNK, DN_CHUNK), 0), 1.0, 0.0)
        lane = _iota2((DN_CHUNK, LANES), 1)
        last_col = _iota2((DN_CHUNK, 1), 0) == DN_CHUNK - 1

        _, strict, _ = _chunk_masks()
        gate_lane = jnp.logical_and(lane >= N_HEADS, lane < 2 * N_HEADS)

        def chunk_group(cg, carry):
            tiles, items = [], []
            for sub in range(LOCKSTEP_CHUNKS):
                rows = pl.ds(pl.multiple_of((cg * LOCKSTEP_CHUNKS + sub) * DN_CHUNK, DN_CHUNK), DN_CHUNK)
                pba = pba_ref[0, rows, :]
                beta_all, g_all = _gates(pba, ea, dtb_ref[...])
                gc = _dot_exact_lhs(tri, g_all)
                tiles.append((rows, pba, beta_all, g_all))
                for h in range(N_HEADS):
                    items.append((sub, rows, h, slice(h * HEAD_DIM, (h + 1) * HEAD_DIM),
                                  beta_all[:, h:h + 1], gc[:, N_HEADS + h:N_HEADS + h + 1]))
            decays = [_chunk_decay(gcol)[0] for _, _, _, _, _, gcol in items]
            egs = [jnp.exp(gcol) for _, _, _, _, _, gcol in items]
            qbs = [(q_ref[0, rows, cs] * QK_SCALE).astype(BF16) for _, rows, h, cs, _, _ in items]
            kfs = [k_ref[0, rows, cs].astype(BF16) for _, rows, h, cs, _, _ in items]
            kbs = [k_ref[0, rows, cs] * beta for _, rows, h, cs, beta, _ in items]
            kbbs = [kb.astype(BF16) for kb in kbs]
            tsps = [_split(ti_ref[0, h, rows, :]) for _, rows, h, cs, _, _ in items]
            drus = [_dot3(tsp, _split(du_ref[0, rows, cs]), TN)
                    for (_, rows, h, cs, _, _), tsp in zip(items, tsps)]
            drws = [_dot3(tsp, _split(dw_ref[0, rows, cs]), TN)
                    for (_, rows, h, cs, _, _), tsp in zip(items, tsps)]
            m_kks = [_dot(kbb, kf, NT) for kbb, kf in zip(kbbs, kfs)]
            a_qks = [_dot(qb, kf, NT) for qb, kf in zip(qbs, kfs)]
            dls = [-jnp.where(strict, _dot3(_split(dru), _split(u_ref[0, rows, cs]), NT)
                              + _dot3(_split(drw), _split(w_ref[0, rows, cs]), NT), 0.0)
                   for (_, rows, h, cs, _, _), dru, drw in zip(items, drus, drws)]
            dms = [(dl * dc).astype(BF16) for dl, dc in zip(dls, decays)]
            das = [(dqk_ref[0, h, rows, :] * dc).astype(BF16)
                   for (_, rows, h, cs, _, _), dc in zip(items, decays)]
            dkb_mm = [_dot(dm, kf, NN) for dm, kf in zip(dms, kfs)]
            dk_mm = [_dot(dm, kbb, TN) + _dot(da, qb, TN) for dm, kbb, da, qb in zip(dms, kbbs, das, qbs)]
            dqs_mm = [_dot(da, kf, NN) for da, kf in zip(das, kfs)]
            dgc_tiles = [dgs_ref[0, rows, :] for rows, _, _, _ in tiles]
            dbeta_tiles = [jnp.zeros((DN_CHUNK, LANES), F32) for _ in tiles]
            for n_it, (sub, rows, h, cs, beta, gcol) in enumerate(items):
                eg, dc = egs[n_it], decays[n_it]
                k = k_ref[0, rows, cs]
                q = q_ref[0, rows, cs] * QK_SCALE
                kb, dru, drw = kbs[n_it], drus[n_it], drws[n_it]
                ek = jnp.exp(gcol[DN_CHUNK - 1:DN_CHUNK, :] - gcol)
                e_mat = (dls[n_it] * m_kks[n_it] + dqk_ref[0, h, rows, :] * a_qks[n_it]) * dc
                dkb = drw * eg + dkb_mm[n_it]
                dg = (jnp.sum(drw * kb * eg, axis=-1, keepdims=True)
                      + jnp.sum(e_mat, axis=1, keepdims=True)
                      - _row_to_col(jnp.sum(e_mat, axis=0, keepdims=True)))
                dqd = dqd_ref[0, rows, cs]
                dg = dg + jnp.sum(dqd * q * eg, axis=-1, keepdims=True)
                dkd = dkd_ref[0, rows, cs]
                tk_ = jnp.sum(dkd * k * ek, axis=-1, keepdims=True)
                dg = dg - tk_ + jnp.where(last_col, jnp.sum(tk_, axis=0, keepdims=True), 0.0)
                dbeta = (jnp.sum(dkb * k, axis=-1, keepdims=True)
                         + jnp.sum(dru * v_ref[0, rows, cs], axis=-1, keepdims=True))
                dq_ref[0, rows, cs] = (dqs_mm[n_it] + dqd * eg) * QK_SCALE
                dq_ref[0, rows, pl.ds(HALF_W + h * HEAD_DIM, HEAD_DIM)] = dk_mm[n_it] + dkd * ek + dkb * beta
                dq_ref[0, rows, pl.ds(2 * HALF_W + h * HEAD_DIM, HEAD_DIM)] = dru * beta
                dgc_tiles[sub] = dgc_tiles[sub] + jnp.where(lane == N_HEADS + h, dg, 0.0)
                dbeta_tiles[sub] = dbeta_tiles[sub] + jnp.where(lane == h, dbeta, 0.0)
            for (rows, pba, beta_all, g_all), dgc_tile, dbeta_tile in zip(tiles, dgc_tiles, dbeta_tiles):
                dg_tile = _dot_exact_lhs(tri_up, dgc_tile)
                da_pre = dg_tile * (-ea) * _sigmoid(pba + dtb_ref[...])
                dal_ref[...] += jnp.sum(jnp.where(gate_lane, dg_tile * g_all, 0.0), axis=0, keepdims=True)
                ddtb_ref[...] += jnp.sum(jnp.where(gate_lane, da_pre, 0.0), axis=0, keepdims=True)
                dpba_ref[0, rows, :] = jnp.where(lane < N_HEADS, dbeta_tile * beta_all * (1.0 - beta_all),
                                                 jnp.where(gate_lane, da_pre, 0.0)).astype(BF16)
            return carry

        lax.fori_loop(0, n_in // LOCKSTEP_CHUNKS, chunk_group, 0)

    def seg(cblk):
        return pl.BlockSpec((1, rblk, HALF_W), lambda i, r: (i, r, cblk))

    vec = pl.BlockSpec((1, LANES), lambda i, r: (0, 0))
    wide = pl.BlockSpec((1, rblk, HALF_W), lambda i, r: (i, r, 0))
    sq = pl.BlockSpec((1, N_HEADS, rblk, DN_CHUNK), lambda i, r: (i, 0, r, 0))
    gates = pl.BlockSpec((1, rblk, LANES), lambda i, r: (i, r, 0))
    return _call(
        body, name=name, grid=(b, t // rblk),
        in_specs=[seg(0), seg(1), seg(2),
                  pl.BlockSpec((1, rblk, LANES), lambda i, r: (i, r, GATE_COL_BLOCK)), vec, vec,
                  sq, wide, wide, wide, wide, wide, wide, sq, gates, _ANY],
        out_specs=[pl.BlockSpec((1, rblk, 3 * HALF_W), lambda i, r: (i, r, 0)),
                   pl.BlockSpec((1, rblk, LANES), lambda i, r: (i, r, GATE_COL_BLOCK)), vec, vec],
        out_shape=[_sds((b, t, 3 * HALF_W), F32), _sds(dproj3.shape, BF16),
                   _sds((1, LANES), F32), _sds((1, LANES), F32)],
        input_output_aliases={15: 1}, comm=comm,
    )(qkv, qkv, qkv, proj3, alog_row, dtb_row, tinv, u, w, du, dw, dqd, dkd, dqk, dgc_scan, dproj3)


def dn_out_fwd(o, proj, dn_norm, name):
    n = o.shape[0]
    tm = min(ROW_TILE, n)

    def body(o_ref, z_ref, g_ref, y_ref):
        for h in range(N_HEADS):
            cs = slice(h * HEAD_DIM, (h + 1) * HEAD_DIM)
            oh = o_ref[:, cs]
            z = z_ref[:, cs]
            y = oh * _rms_scale(oh) * g_ref[...]
            y_ref[:, cs] = (y * (z * _sigmoid(z))).astype(BF16)

    half = pl.BlockSpec((tm, HALF_W), lambda i: (i, 0))
    return _call(
        body, name=name, grid=(n // tm,),
        in_specs=[half, pl.BlockSpec((tm, HALF_W), lambda i: (i, 5)),
                  pl.BlockSpec((1, HEAD_DIM), lambda i: (0, 0))],
        out_specs=half, out_shape=_sds((n, HALF_W), BF16),
    )(o, proj, dn_norm)


def dn_out_bwd(dy, o, proj, dn_norm, dproj, name):
    n = o.shape[0]
    tm = min(ROW_TILE, n)

    def body(dy_ref, o_ref, z_ref, g_ref, dproj_in, do_ref, dz_ref, dg_ref):
        @pl.when(pl.program_id(0) == 0)
        def _():
            dg_ref[...] = jnp.zeros_like(dg_ref)

        g = g_ref[...]
        dg = jnp.zeros_like(g)
        for h in range(N_HEADS):
            cs = slice(h * HEAD_DIM, (h + 1) * HEAD_DIM)
            oh = o_ref[:, cs]
            z = z_ref[:, cs]
            d = dy_ref[:, cs]
            r = _rms_scale(oh)
            nh = oh * r
            sz = _sigmoid(z)
            dyn = d * (z * sz)
            dz_ref[:, cs] = (d * (nh * g) * (sz * (1.0 + z * (1.0 - sz)))).astype(BF16)
            dg = dg + jnp.sum(dyn * nh, axis=0, keepdims=True)
            dn = dyn * g
            do_ref[:, cs] = r * (dn - nh * jnp.mean(dn * nh, axis=-1, keepdims=True))
        dg_ref[...] += dg

    half = pl.BlockSpec((tm, HALF_W), lambda i: (i, 0))
    vec = pl.BlockSpec((1, HEAD_DIM), lambda i: (0, 0))
    return _call(
        body, name=name, grid=(n // tm,),
        in_specs=[half, half, pl.BlockSpec((tm, HALF_W), lambda i: (i, 5)), vec, _ANY],
        out_specs=[half, pl.BlockSpec((tm, HALF_W), lambda i: (i, 5)), vec],
        out_shape=[_sds((n, HALF_W), F32), _sds(dproj.shape, BF16), _sds((1, HEAD_DIM), F32)],
        input_output_aliases={4: 1},
    )(dy, o, proj, dn_norm, dproj)


def _adamw_math(w, g, m, v):
    m_new = ADAM_B1 * m + (1.0 - ADAM_B1) * g
    v_new = ADAM_B2 * v + (1.0 - ADAM_B2) * (g * g)
    m_hat = m_new / (1.0 - ADAM_B1 ** ADAM_STEP)
    v_hat = v_new / (1.0 - ADAM_B2 ** ADAM_STEP)
    delta = -ADAM_LR * (m_hat / (jnp.sqrt(v_hat) + ADAM_EPS) + ADAM_WD * w)
    return delta, m_new, v_new


def adamw(w, g, m, v, name):
    r, c = w.shape
    tr = r
    for cand in (256, 352):
        if r % cand == 0 and r > cand:
            tr = cand
            break

    def body(w_ref, g_ref, m_ref, v_ref, d_ref, mo_ref, vo_ref):
        d, mn, vn = _adamw_math(w_ref[...], g_ref[...], m_ref[...], v_ref[...])
        d_ref[...] = d
        mo_ref[...] = mn
        vo_ref[...] = vn

    spec = pl.BlockSpec((tr, c), lambda i: (i, 0))
    return _call(
        body, name=name, grid=(r // tr,),
        in_specs=[spec] * 4, out_specs=[spec] * 3, out_shape=[_sds((r, c), F32)] * 3,
    )(w, g, m, v)


def _place():
    return lax.axis_index("x"), lax.axis_index("y"), lax.axis_index("c")


def _other_chips(x, y):
    return [(1 - x, y), (x, 1 - y), (1 - x, 1 - y)]


_ANY = pl.BlockSpec(memory_space=pl.ANY)


def cast_place(w, shard_idx, name):
    r, cols = w.shape
    tr = r // 2

    def body(j_ref, w_ref, o_ref):
        o_ref[0] = w_ref[...].astype(BF16)

    return pl.pallas_call(
        body, name=name,
        grid_spec=pltpu.PrefetchScalarGridSpec(
            num_scalar_prefetch=1, grid=(r // tr,),
            in_specs=[pl.BlockSpec((tr, cols), lambda i, j: (i, 0))],
            out_specs=pl.BlockSpec((1, tr, cols), lambda i, j: (j[0], i, 0))),
        out_shape=_sds((N_SHARD, r, cols), BF16),
        compiler_params=pltpu.CompilerParams(dimension_semantics=("arbitrary",),
                                             vmem_limit_bytes=VMEM_LIMIT),
    )(shard_idx, w)


class Exchange:
    def __init__(self, inputs, out_shape, aliases, sems, phases):
        self.inputs, self.out_shape, self.aliases = list(inputs), list(out_shape), dict(aliases)
        self.sems, self.phases = list(sems), list(phases)


def run_exchange(ex, name):
    def body(*refs):
        n_in, n_out = len(ex.inputs), len(ex.out_shape)
        for _, fn in ex.phases:
            fn(refs[:n_in], refs[n_in:n_in + n_out], refs[n_in + n_out:])

    return _call(body, name=name, in_specs=[_ANY] * len(ex.inputs), out_specs=[_ANY] * len(ex.out_shape),
                 out_shape=ex.out_shape, scratch=ex.sems, input_output_aliases=ex.aliases)(*ex.inputs)


def merge_exchanges(exs):
    inputs, out_shape, sems, aliases, phases, out_slices = [], [], [], {}, [], []
    for ex in exs:
        i0, o0, s0 = len(inputs), len(out_shape), len(sems)
        inputs += ex.inputs
        out_shape += ex.out_shape
        sems += ex.sems
        for k, m in ex.aliases.items():
            aliases[i0 + k] = o0 + m
        si, so, ss = slice(i0, len(inputs)), slice(o0, len(out_shape)), slice(s0, len(sems))
        out_slices.append(so)
        for step, fn in ex.phases:
            phases.append((step, lambda ins, outs, sm, fn=fn, si=si, so=so, ss=ss: fn(ins[si], outs[so], sm[ss])))
    return Exchange(inputs, out_shape, aliases, sems, phases), out_slices


def _dma_sems(*sizes):
    return [pltpu.SemaphoreType.DMA((s,)) for s in sizes]


def gather_exchange(bufs, small=None, relay_step=-2):
    n = len(bufs)
    n_small = 0 if small is None else 1

    def half(outs, a, blk, hc):
        rh = bufs[a].shape[1] // 2
        return outs[a].at[blk, pl.ds(hc * rh, rh), :]

    def ici(outs, sems, a, k, blk, to):
        return pltpu.make_async_remote_copy(
            src_ref=half(outs, a, blk, to[2]), dst_ref=half(outs, a, blk, to[2]), send_sem=sems[0].at[3 * a + k],
            recv_sem=sems[1].at[3 * a + k], device_id=to, device_id_type=MESH)

    def d2d(outs, sems, a, k, blk, hc, to):
        return pltpu.make_async_remote_copy(
            src_ref=half(outs, a, blk, hc), dst_ref=half(outs, a, blk, hc), send_sem=sems[2].at[3 * a + k],
            recv_sem=sems[3].at[3 * a + k], device_id=to, device_id_type=MESH)

    def small_copy(ins, outs, sems, k, blk, to):
        return pltpu.make_async_remote_copy(
            src_ref=ins[n], dst_ref=outs[n].at[blk], send_sem=sems[0].at[3 * n + k],
            recv_sem=sems[1].at[3 * n + k], device_id=to, device_id_type=MESH)

    def start(ins, outs, sems):
        x, y, c = _place()
        j = 2 * x + y
        if n_small:
            pltpu.make_async_copy(ins[n], outs[n].at[j], sems[4].at[0]).start()
        for k, (px, py) in enumerate(_other_chips(x, y)):
            if n_small:
                small_copy(ins, outs, sems, k, j, (px, py, c)).start()
            for a in range(n):
                ici(outs, sems, a, k, j, (px, py, c)).start()

    def relay(ins, outs, sems):
        x, y, c = _place()
        for k, (px, py) in enumerate(_other_chips(x, y)):
            for a in range(n):
                ici(outs, sems, a, k, 2 * px + py, (px, py, c)).wait_recv()
                d2d(outs, sems, a, k, 2 * px + py, c, (x, y, 1 - c)).start()

    def finish(ins, outs, sems):
        x, y, c = _place()
        j = 2 * x + y
        for k, (px, py) in enumerate(_other_chips(x, y)):
            blk = 2 * px + py
            if n_small:
                small_copy(ins, outs, sems, k, blk, (px, py, c)).wait_recv()
                small_copy(ins, outs, sems, k, j, (px, py, c)).wait_send()
            for a in range(n):
                d2d(outs, sems, a, k, blk, 1 - c, (x, y, 1 - c)).wait_recv()
                ici(outs, sems, a, k, j, (px, py, c)).wait_send()
                d2d(outs, sems, a, k, blk, c, (x, y, 1 - c)).wait_send()
        if n_small:
            pltpu.make_async_copy(ins[n], outs[n].at[j], sems[4].at[0]).wait()

    out_shape = [_sds(b.shape, b.dtype) for b in bufs]
    if n_small:
        out_shape.append(_sds((N_SHARD,) + small.shape, small.dtype))
    return Exchange(list(bufs) + ([small] if n_small else []), out_shape, {a: a for a in range(n)},
                    _dma_sems(3 * n + 3, 3 * n + 3, 3 * n, 3 * n, 1),
                    [(0, start), (relay_step, relay), (-1, finish)])


def _start_then_wait(copies):
    def start(ins, outs, sems):
        for sent, _ in copies(ins, outs, sems):
            sent().start()

    def finish(ins, outs, sems):
        pairs = copies(ins, outs, sems)
        for _, arrival in pairs:
            arrival().wait_recv()
        for sent, _ in pairs:
            sent().wait_send()

    return [(0, start), (-1, finish)]


def pair_exchange(arrs):
    n = len(arrs)

    def copies(ins, outs, sems):
        x, y, c = _place()
        res = []
        for a in range(n):
            def mk(a=a):
                rh = arrs[a].shape[1] // 2
                return pltpu.make_async_remote_copy(
                    src_ref=ins[a].at[:, pl.ds((1 - c) * rh, rh), :], dst_ref=outs[a], send_sem=sems[0].at[a],
                    recv_sem=sems[1].at[a], device_id=(x, y, 1 - c), device_id_type=MESH)
            res.append((mk, mk))
        return res

    return Exchange(arrs, [_sds((a.shape[0], a.shape[1] // 2, a.shape[2]), a.dtype) for a in arrs], {},
                    _dma_sems(n, n), _start_then_wait(copies))


def pair_add(g, s, c_idx, name):
    nb, r, cols = g.shape
    rh = r // 2

    def body(c_ref, g_ref, s_ref, o_ref):
        o_ref[...] = (g_ref[...] + s_ref[...]).astype(BF16)

    return pl.pallas_call(
        body, name=name,
        grid_spec=pltpu.PrefetchScalarGridSpec(
            num_scalar_prefetch=1, grid=(nb,),
            in_specs=[pl.BlockSpec((1, rh, cols), lambda j, c: (j, c[0], 0)),
                      pl.BlockSpec((1, rh, cols), lambda j, c: (j, 0, 0))],
            out_specs=pl.BlockSpec((1, rh, cols), lambda j, c: (j, 0, 0))),
        out_shape=_sds((nb, rh, cols), BF16),
        compiler_params=pltpu.CompilerParams(dimension_semantics=("arbitrary",),
                                             vmem_limit_bytes=VMEM_LIMIT),
    )(c_idx, g, s)


def chip_exchange(arrs):
    n = len(arrs)

    def copies(ins, outs, sems):
        x, y, c = _place()
        j = 2 * x + y
        res = []
        for a in range(n):
            for k, (px, py) in enumerate(_other_chips(x, y)):
                def mk(src_blk, dst_blk, a=a, k=k, to=(px, py, c)):
                    return pltpu.make_async_remote_copy(
                        src_ref=ins[a].at[src_blk], dst_ref=outs[a].at[dst_blk], send_sem=sems[0].at[3 * a + k],
                        recv_sem=sems[1].at[3 * a + k], device_id=to, device_id_type=MESH)
                res.append((functools.partial(mk, 2 * px + py, j), functools.partial(mk, j, 2 * px + py)))
        return res

    return Exchange(arrs, [_sds(a.shape, a.dtype) for a in arrs], {}, _dma_sems(3 * n, 3 * n),
                    _start_then_wait(copies))


def sum_chips(r, p, shard_idx, name):
    nb, rh, cols = r.shape
    tr = rh

    def body(j_ref, p_ref, *refs):
        o_ref = refs[nb]
        j = j_ref[0]
        acc = None
        for i in range(nb):
            term = jnp.where(j == i, p_ref[0], refs[i][0]).astype(F32)
            acc = term if acc is None else acc + term
        o_ref[...] = acc

    def slot(i):
        return pl.BlockSpec((1, tr, cols), lambda t, j: (jnp.where(j[0] == i, (i + 1) % nb, i), t, 0))

    return pl.pallas_call(
        body, name=name,
        grid_spec=pltpu.PrefetchScalarGridSpec(
            num_scalar_prefetch=1, grid=(rh // tr,),
            in_specs=[pl.BlockSpec((1, tr, cols), lambda t, j: (j[0], t, 0))] + [slot(i) for i in range(nb)],
            out_specs=pl.BlockSpec((tr, cols), lambda t, j: (t, 0))),
        out_shape=_sds((rh, cols), F32),
        compiler_params=pltpu.CompilerParams(dimension_semantics=("arbitrary",),
                                             vmem_limit_bytes=VMEM_LIMIT),
    )(shard_idx, p, *([r] * nb))


def pair_swap(arrs):
    n = len(arrs)

    def copies(ins, outs, sems):
        x, y, c = _place()
        res = []
        for a in range(n):
            def mk(a=a):
                return pltpu.make_async_remote_copy(
                    src_ref=ins[a], dst_ref=outs[a], send_sem=sems[0].at[a], recv_sem=sems[1].at[a],
                    device_id=(x, y, 1 - c), device_id_type=MESH)
            res.append((mk, mk))
        return res

    return Exchange(arrs, [_sds(a.shape, a.dtype) for a in arrs], {}, _dma_sems(n, n),
                    _start_then_wait(copies))


def adamw_pair(w, g_mine, g_sib, m, v, c_idx, name):
    r, cols = w.shape
    rh = r // 2
    tr = rh
    nh = rh // tr

    def body(c_ref, w_ref, gm_ref, gs_ref, m_ref, v_ref, g_ref, d_ref, mo_ref, vo_ref):
        mine = (pl.program_id(0) // nh) == c_ref[0]
        g = jnp.where(mine, gm_ref[...], gs_ref[...])
        d, mn, vn = _adamw_math(w_ref[...], g, m_ref[...], v_ref[...])
        g_ref[...] = g
        d_ref[...] = d
        mo_ref[...] = mn
        vo_ref[...] = vn

    full = pl.BlockSpec((tr, cols), lambda i, c: (i, 0))
    part = pl.BlockSpec((tr, cols), lambda i, c: (i % nh, 0))
    return pl.pallas_call(
        body, name=name,
        grid_spec=pltpu.PrefetchScalarGridSpec(
            num_scalar_prefetch=1, grid=(r // tr,),
            in_specs=[full, part, part, full, full], out_specs=[full] * 4),
        out_shape=[_sds((r, cols), F32)] * 4,
        compiler_params=pltpu.CompilerParams(dimension_semantics=("arbitrary",),
                                             vmem_limit_bytes=VMEM_LIMIT),
    )(c_idx, w, g_mine, g_sib, m, v)


N_DEV = 8


def device_gather(pack):
    def copies(ins, outs, sems):
        x, y, c = _place()
        me = 4 * x + 2 * y + c
        res = []
        for k in range(1, N_DEV):
            fx, fy, fc = (k >> 2) & 1, (k >> 1) & 1, k & 1
            px, py, pc = (1 - x if fx else x, 1 - y if fy else y, 1 - c if fc else c)

            def mk(slot, k=k, to=(px, py, pc)):
                return pltpu.make_async_remote_copy(
                    src_ref=ins[0], dst_ref=outs[0].at[slot], send_sem=sems[0].at[k - 1],
                    recv_sem=sems[1].at[k - 1], device_id=to, device_id_type=MESH)
            res.append((functools.partial(mk, me), functools.partial(mk, 4 * px + 2 * py + pc)))
        return res

    return Exchange([pack], [_sds((N_DEV,) + pack.shape, pack.dtype)], {}, _dma_sems(N_DEV - 1, N_DEV - 1),
                    _start_then_wait(copies))


def sum_devices(buf, pack, me_idx, name):
    r, cols = pack.shape

    def body(me_ref, p_ref, *refs):
        o_ref = refs[N_DEV]
        acc = None
        for i in range(N_DEV):
            term = jnp.where(me_ref[0] == i, p_ref[...], refs[i][0])
            acc = term if acc is None else acc + term
        o_ref[...] = acc

    def slot(i):
        return pl.BlockSpec((1, r, cols), lambda t, me: (jnp.where(me[0] == i, (i + 1) % N_DEV, i), 0, 0))

    whole = pl.BlockSpec((r, cols), lambda t, me: (0, 0))
    return pl.pallas_call(
        body, name=name,
        grid_spec=pltpu.PrefetchScalarGridSpec(
            num_scalar_prefetch=1, grid=(1,),
            in_specs=[whole] + [slot(i) for i in range(N_DEV)], out_specs=whole),
        out_shape=_sds((r, cols), F32),
        compiler_params=pltpu.CompilerParams(dimension_semantics=("arbitrary",),
                                             vmem_limit_bytes=VMEM_LIMIT),
    )(me_idx, pack, *([buf] * N_DEV))


SMALL_NAMES = ("ffn1_norm", "mix_norm", "ffn2_norm", "final_norm", "sg_ln_g", "sg_ln_b",
               "dn_norm", "a_log", "dt_bias", "sg_b", "sg_w", "conv_w")


def _to_rows(a):
    flat = a.reshape(-1)
    pad = (-flat.shape[0]) % LANES
    if pad:
        flat = jnp.pad(flat, (0, pad))
    return flat.reshape(-1, LANES)


def _pack_small(parts):
    rows = [_to_rows(parts[k]) for k in SMALL_NAMES]
    pack = jnp.concatenate(rows, axis=0)
    pad = (-pack.shape[0]) % 8
    if pad:
        pack = jnp.pad(pack, ((0, pad), (0, 0)))
    return pack


def _unpack_small(pack, shapes):
    out, r0 = {}, 0
    for k in SMALL_NAMES:
        size = 1
        for s in shapes[k]:
            size *= s
        nrows = -(-size // LANES)
        out[k] = pack[r0:r0 + nrows].reshape(-1)[:size].reshape(shapes[k])
        r0 += nrows
    return out


def kernel(x, ffn1_norm, ffn1_w_gate, ffn1_w_up, ffn1_w_down, mix_norm, w_in, conv_w, a_log, dt_bias, dn_norm, sg_ln_g, sg_ln_b, sg_w, sg_b, w_out, ffn2_norm, ffn2_w_gate, ffn2_w_up, ffn2_w_down, final_norm, loss_target, m_ffn1_norm, m_ffn1_w_gate, m_ffn1_w_up, m_ffn1_w_down, m_mix_norm, m_w_in, m_conv_w, m_a_log, m_dt_bias, m_dn_norm, m_sg_ln_g, m_sg_ln_b, m_sg_w, m_sg_b, m_w_out, m_ffn2_norm, m_ffn2_w_gate, m_ffn2_w_up, m_ffn2_w_down, m_final_norm, v_ffn1_norm, v_ffn1_w_gate, v_ffn1_w_up, v_ffn1_w_down, v_mix_norm, v_w_in, v_conv_w, v_a_log, v_dt_bias, v_dn_norm, v_sg_ln_g, v_sg_ln_b, v_sg_w, v_sg_b, v_w_out, v_ffn2_norm, v_ffn2_w_gate, v_ffn2_w_up, v_ffn2_w_down, v_final_norm):
    bsz, t_len, d = x.shape
    n = bsz * t_len
    xy, yy, cc = _place()
    shard = 2 * xy + yy

    big_names = ["ffn1_w_gate", "ffn1_w_up", "ffn1_w_down", "w_in", "w_out",
                 "ffn2_w_gate", "ffn2_w_up", "ffn2_w_down"]
    big_w = dict(ffn1_w_gate=ffn1_w_gate, ffn1_w_up=ffn1_w_up, ffn1_w_down=ffn1_w_down, w_in=w_in,
                 w_out=w_out, ffn2_w_gate=ffn2_w_gate, ffn2_w_up=ffn2_w_up, ffn2_w_down=ffn2_w_down)
    big_m = dict(ffn1_w_gate=m_ffn1_w_gate, ffn1_w_up=m_ffn1_w_up, ffn1_w_down=m_ffn1_w_down, w_in=m_w_in,
                 w_out=m_w_out, ffn2_w_gate=m_ffn2_w_gate, ffn2_w_up=m_ffn2_w_up, ffn2_w_down=m_ffn2_w_down)
    big_v = dict(ffn1_w_gate=v_ffn1_w_gate, ffn1_w_up=v_ffn1_w_up, ffn1_w_down=v_ffn1_w_down, w_in=v_w_in,
                 w_out=v_w_out, ffn2_w_gate=v_ffn2_w_gate, ffn2_w_up=v_ffn2_w_up, ffn2_w_down=v_ffn2_w_down)
    shard_idx = jnp.reshape(shard, (1,)).astype(jnp.int32)
    c_idx = jnp.reshape(cc, (1,)).astype(jnp.int32)
    transposed = ("ffn1_w_gate", "ffn1_w_up", "ffn2_w_gate", "ffn2_w_up")

    def as2d(a, k):
        return a[0].T if k in transposed else a[0]

    def from2d(a, k):
        return a.T[None] if k in transposed else a[None]

    placed = {k: cast_place(as2d(big_w[k], k), shard_idx, name="cast_" + k) for k in big_names}
    first_names = big_names[:3]
    later_names = big_names[3:]
    res = run_exchange(gather_exchange([placed[k] for k in first_names], conv_w[0]), name="gather_first")
    gw = dict(zip(first_names, res[:3]))
    conv_full = res[3].transpose(1, 0, 2).reshape(CONV_K, 3 * HALF_W)

    x0 = x.reshape(n, d)
    (x1, h1, gate1, up1), later = ffn_fwd(
        x0, ffn1_norm, gw["ffn1_w_gate"], gw["ffn1_w_up"], gw["ffn1_w_down"], name="ffn1_fwd",
        comm=gather_exchange([placed[k] for k in later_names]))
    gw.update(zip(later_names, later))
    w_in_full = gw["w_in"].transpose(1, 0, 2).reshape(d, IN_COLS)
    w_in_full = jnp.pad(w_in_full, ((0, 0), (0, PROJ_W - IN_COLS)))
    w_out_full = gw["w_out"].reshape(2 * HALF_W, d)
    proj, h2 = in_proj_fwd(x1, mix_norm, w_in_full, name="in_proj_fwd")
    proj3 = proj.reshape(bsz, t_len, PROJ_W)
    bias_tile = jnp.repeat(sg_b[0].T, SG_GROUP_DIM, axis=1)
    sg_out = sg_fwd(proj, sg_ln_g, sg_ln_b, sg_w[0], bias_tile, name="sg_fwd")
    qkv = dn_conv_fwd(proj3, conv_full, name="dn_conv_fwd")
    alog_row = jnp.zeros((1, LANES), F32).at[0, N_HEADS:2 * N_HEADS].set(a_log[0])
    dtb_row = jnp.zeros((1, LANES), F32).at[0, N_HEADS:2 * N_HEADS].set(dt_bias[0])
    u_wy, w_wy, q_dec, k_dec, qk, tinv, gc = dn_chunk_fwd(qkv, proj3, alog_row, dtb_row,
                                                           name="dn_chunk_fwd")
    o, s_in = dn_scan_fwd(u_wy, w_wy, q_dec, k_dec, qk, gc, name="dn_scan_fwd")
    dn_out = dn_out_fwd(o.reshape(n, HALF_W), proj, dn_norm, name="dn_out_fwd")
    x2 = out_proj_fwd(x1, sg_out, dn_out, w_out_full, name="out_proj_fwd")
    x3, h3, gate2, up2 = ffn_fwd(x2, ffn2_norm, gw["ffn2_w_gate"], gw["ffn2_w_up"],
                                 gw["ffn2_w_down"], name="ffn2_fwd")
    dx3, d_final_norm, loss_tile = final_loss(x3, final_norm.reshape(1, d),
                                              loss_target.reshape(n, d), name="final_loss")
    loss = lax.psum(loss_tile[0, 0], ("x", "y", "c"))

    dx2, dgate2, dup2, act2, dyh2, d_ffn2_norm = ffn_bwd_act(
        dx3, x2, ffn2_norm, gate2, up2, gw["ffn2_w_gate"], gw["ffn2_w_up"], gw["ffn2_w_down"],
        name="ffn2_bwd_act")
    g_big = {}
    g_big["ffn2_w_gate"], g_big["ffn2_w_up"], g_big["ffn2_w_down"] = ffn_bwd_w(
        h3, dyh2, dgate2, dup2, act2, name="ffn2_bwd_w")

    early = ["ffn2_w_gate", "ffn2_w_up", "ffn2_w_down"]
    (d_sg, d_dn, dx2b), early_sib = out_proj_bwd_x(dx2, w_out_full, name="out_proj_bwd_x",
                                                   comm=pair_exchange([g_big[k] for k in early]))
    early_sums = [pair_add(g_big[k], s, c_idx, name="grad_pair_add_" + k) for k, s in zip(early, early_sib)]
    g_w_out = jnp.concatenate([matmul_tn(sg_out, dx2b, d, name="w_out_grad_sg"),
                               matmul_tn(dn_out, dx2b, d, name="w_out_grad_dn")], axis=0)
    g_big["w_out"] = g_w_out.reshape(N_SHARD, (2 * HALF_W) // N_SHARD, d)

    d_proj, d_sg_w, d_bias_tile, d_ln_g, d_ln_b = sg_bwd(d_sg, proj, sg_ln_g, sg_ln_b, sg_w[0],
                                                         bias_tile, name="sg_bwd")
    d_o, d_proj, d_dn_norm = dn_out_bwd(d_dn, o.reshape(n, HALF_W), proj, dn_norm, d_proj,
                                        name="dn_out_bwd")
    du, dw, dqd, dkd, dqk, dgc_scan = dn_scan_bwd(d_o.reshape(bsz, t_len, HALF_W), u_wy, w_wy, q_dec,
                                                  k_dec, qk, gc, s_in, name="dn_scan_bwd")
    (d_qkv, d_proj3, d_alog_row, d_dtb_row), early_chips = dn_chunk_bwd(
        qkv, proj3, alog_row, dtb_row, tinv, u_wy, w_wy, du, dw, dqd, dkd, dqk, dgc_scan,
        d_proj.reshape(bsz, t_len, PROJ_W), name="dn_chunk_bwd", comm=chip_exchange(early_sums))
    early_halves = [sum_chips(r, p, shard_idx, name="grad_chip_sum_" + k)
                    for k, r, p in zip(early, early_chips, early_sums)]
    (d_proj3, d_conv), early_sib_halves = dn_conv_bwd(d_qkv, proj3, conv_full, d_proj3, name="dn_conv_bwd",
                                                      comm=pair_swap(early_halves))
    d_proj = d_proj3.reshape(n, PROJ_W)
    g_w_in = matmul_tn(h2, d_proj, 640, name="w_in_grad")[:, :IN_COLS]
    g_big["w_in"] = g_w_in.reshape(d, N_SHARD, IN_COLS // N_SHARD).transpose(1, 0, 2)

    mid = ["w_in", "w_out"]
    (dx1, d_mix_norm), mid_sib = in_proj_bwd_x(d_proj, w_in_full, x1, mix_norm, dx2, name="in_proj_bwd_x",
                                               comm=pair_exchange([g_big[k] for k in mid]))
    mid_sums = [pair_add(g_big[k], s, c_idx, name="grad_pair_add_" + k) for k, s in zip(mid, mid_sib)]
    dx0, dgate1, dup1, act1, dyh1, d_ffn1_norm = ffn_bwd_act(
        dx1, x0, ffn1_norm, gate1, up1, gw["ffn1_w_gate"], gw["ffn1_w_up"], gw["ffn1_w_down"],
        name="ffn1_bwd_act")
    d_sg_b = d_bias_tile.reshape(SG_CHUNK, SG_GROUPS, SG_GROUP_DIM).sum(axis=-1).T
    small_g = dict(ffn1_norm=d_ffn1_norm, mix_norm=d_mix_norm, ffn2_norm=d_ffn2_norm,
                   final_norm=d_final_norm, sg_ln_g=d_ln_g, sg_ln_b=d_ln_b, dn_norm=d_dn_norm,
                   a_log=d_alog_row[:, N_HEADS:2 * N_HEADS], dt_bias=d_dtb_row[:, N_HEADS:2 * N_HEADS],
                   sg_b=d_sg_b, sg_w=d_sg_w, conv_w=d_conv)
    my_pack = _pack_small(small_g)
    hosted, parts = merge_exchanges([chip_exchange(mid_sums), device_gather(my_pack)])
    f1_grads, hosted_res = ffn_bwd_w(h1, dyh1, dgate1, dup1, act1, name="ffn1_bwd_w", comm=hosted)
    g_big["ffn1_w_gate"], g_big["ffn1_w_up"], g_big["ffn1_w_down"] = f1_grads
    mid_chips, (all_packs,) = hosted_res[parts[0]], hosted_res[parts[1]]
    mid_halves = [sum_chips(r, p, shard_idx, name="grad_chip_sum_" + k)
                  for k, r, p in zip(mid, mid_chips, mid_sums)]
    grad_x = dx0.reshape(bsz, t_len, d)

    late = [k for k in big_names if k not in early and k not in mid]
    g_list = [g_big[k] for k in late]
    from_sibling = run_exchange(pair_exchange(g_list), name="grad_pair_exchange")
    pair_sums = [pair_add(g, s, c_idx, name="grad_pair_add_" + k)
                 for k, g, s in zip(late, g_list, from_sibling)]
    from_chips = run_exchange(chip_exchange(pair_sums), name="grad_chip_exchange")
    halves = [sum_chips(r, p, shard_idx, name="grad_chip_sum_" + k)
              for k, r, p in zip(late, from_chips, pair_sums)]
    sib_halves = run_exchange(pair_swap(mid_halves + halves), name="grad_pair_swap")
    outs = {}
    for k, g_mine, g_sib in zip(early + mid + late, early_halves + mid_halves + halves,
                                list(early_sib_halves) + list(sib_halves)):
        res = adamw_pair(as2d(big_w[k], k), g_mine, g_sib, as2d(big_m[k], k), as2d(big_v[k], k), c_idx,
                         name="adamw_" + k)
        outs[k] = tuple(from2d(a, k) for a in res)

    small_w = dict(ffn1_norm=ffn1_norm, mix_norm=mix_norm, ffn2_norm=ffn2_norm, final_norm=final_norm,
                   sg_ln_g=sg_ln_g, sg_ln_b=sg_ln_b, dn_norm=dn_norm, a_log=a_log, dt_bias=dt_bias,
                   sg_b=sg_b, sg_w=sg_w)
    small_m = dict(ffn1_norm=m_ffn1_norm, mix_norm=m_mix_norm, ffn2_norm=m_ffn2_norm,
                   final_norm=m_final_norm, sg_ln_g=m_sg_ln_g, sg_ln_b=m_sg_ln_b, dn_norm=m_dn_norm,
                   a_log=m_a_log, dt_bias=m_dt_bias, sg_b=m_sg_b, sg_w=m_sg_w)
    small_v = dict(ffn1_norm=v_ffn1_norm, mix_norm=v_mix_norm, ffn2_norm=v_ffn2_norm,
                   final_norm=v_final_norm, sg_ln_g=v_sg_ln_g, sg_ln_b=v_sg_ln_b, dn_norm=v_dn_norm,
                   a_log=v_a_log, dt_bias=v_dt_bias, sg_b=v_sg_b, sg_w=v_sg_w)
    shapes = {k: small_w[k].shape for k in small_w}
    shapes["conv_w"] = (CONV_K, 3 * HALF_W)
    me_idx = jnp.reshape(4 * xy + 2 * yy + cc, (1,)).astype(jnp.int32)
    g_pack = sum_devices(all_packs, my_pack, me_idx, name="small_sum")
    g_small = _unpack_small(g_pack, shapes)
    cw = 3 * HALF_W // N_SHARD
    g_conv = lax.dynamic_slice_in_dim(g_small["conv_w"], shard * cw, cw, axis=1)
    zero_conv = jnp.zeros((CONV_K, 3 * HALF_W), F32)

    def packed(src, conv):
        parts = dict(src)
        parts["conv_w"] = lax.dynamic_update_slice_in_dim(zero_conv, conv[0], shard * cw, axis=1)
        return _pack_small(parts)

    d_pack, m_pack, v_pack = adamw(packed(small_w, conv_w), g_pack, packed(small_m, m_conv_w),
                                   packed(small_v, v_conv_w), name="adamw_small")
    d_small = _unpack_small(d_pack, shapes)
    m_small = _unpack_small(m_pack, shapes)
    v_small = _unpack_small(v_pack, shapes)

    def conv_block(full_arr):
        return lax.dynamic_slice_in_dim(full_arr, shard * cw, cw, axis=1)[None]

    for k in small_w:
        outs[k] = (g_small[k].reshape(small_w[k].shape), d_small[k], m_small[k], v_small[k])
    outs["conv_w"] = (g_conv[None], conv_block(d_small["conv_w"]), conv_block(m_small["conv_w"]),
                      conv_block(v_small["conv_w"]))

    order = ["ffn1_norm", "ffn1_w_gate", "ffn1_w_up", "ffn1_w_down", "mix_norm", "w_in", "conv_w",
             "a_log", "dt_bias", "dn_norm", "sg_ln_g", "sg_ln_b", "sg_w", "sg_b", "w_out", "ffn2_norm",
             "ffn2_w_gate", "ffn2_w_up", "ffn2_w_down", "final_norm"]
    return (loss, grad_x, *[outs[k][0] for k in order], *[outs[k][1] for k in order],
            *[outs[k][2] for k in order], *[outs[k][3] for k in order])
```

```python
import functools

import jax
import jax.numpy as jnp
from jax import lax
from jax.experimental import pallas as pl
from jax.experimental.pallas import tpu as pltpu

F32 = jnp.float32
BF16 = jnp.bfloat16
EPS = 1e-6

D_MODEL = 1024
N_SHARD = 4
HEAD_DIM = 128
N_HEADS = 4
DN_CHUNK = 64
SG_CHUNK = 128
SG_GROUPS = 8
SG_GROUP_DIM = 64
HALF_W = 512
PROJ_W = 3200
IN_COLS = 3080
GATE_COL_BLOCK = 24
QK_SCALE = HEAD_DIM ** -0.5
LANES = 128

ADAM_LR = 0.001
ADAM_B1 = 0.9
ADAM_B2 = 0.999
ADAM_EPS = 1e-08
ADAM_WD = 0.01
ADAM_STEP = 10

VMEM_LIMIT = 56 * 1024 * 1024
ROW_TILE = 512

NN = ((1,), (0,))
NT = ((1,), (1,))
TN = ((0,), (0,))
MESH = pl.DeviceIdType.MESH


def _dot(a, b, dims):
    return lax.dot_general(a, b, (dims, ((), ())), preferred_element_type=F32)


def _bdot(a, b, dims):
    return _dot(a.astype(BF16), b.astype(BF16), dims)


def _split(a):
    hi = a.astype(BF16)
    lo = (a - hi.astype(F32)).astype(BF16)
    return hi, lo


def _dot3(a, b, dims=NN):
    return _dot(a[0], b[0], dims) + (_dot(a[0], b[1], dims) + _dot(a[1], b[0], dims))


def _dot_exact_lhs(a, b):
    ab = a.astype(BF16)
    b1 = b.astype(BF16)
    r1 = b - b1.astype(F32)
    b2 = r1.astype(BF16)
    b3 = (r1 - b2.astype(F32)).astype(BF16)
    return _dot(ab, b1, NN) + (_dot(ab, b2, NN) + _dot(ab, b3, NN))


def _call(body, *, name, out_shape, in_specs, out_specs, grid=(), scratch=(), comm=None, **kw):
    params = dict(vmem_limit_bytes=VMEM_LIMIT)
    if grid:
        params["dimension_semantics"] = ("arbitrary",) * len(grid)
    if comm is None:
        return pl.pallas_call(
            body, name=name, grid=grid, in_specs=in_specs, out_specs=out_specs,
            out_shape=out_shape, scratch_shapes=list(scratch),
            compiler_params=pltpu.CompilerParams(**params), **kw)

    n_in, n_out, n_sc = len(in_specs), len(out_specs), len(scratch)
    c_in, c_out = len(comm.inputs), len(comm.out_shape)
    steps = 1
    for g in grid:
        steps *= g

    def hosted(*refs):
        ins, cins = refs[:n_in], refs[n_in:n_in + c_in]
        o0 = n_in + c_in
        outs, couts = refs[o0:o0 + n_out], refs[o0 + n_out:o0 + n_out + c_out]
        s0 = o0 + n_out + c_out
        sc, csems = refs[s0:s0 + n_sc], refs[s0 + n_sc:]
        lin = 0
        for axis, g in enumerate(grid):
            lin = lin * g + pl.program_id(axis)

        def at(step, fn):
            @pl.when(lin == step % steps)
            def _():
                fn(cins, couts, csems)

        for step, fn in comm.phases:
            if step >= 0:
                at(step, fn)
        body(*ins, *outs, *sc)
        for step, fn in comm.phases:
            if step < 0:
                at(step, fn)

    aliases = dict(kw.pop("input_output_aliases", {}))
    for k, m in comm.aliases.items():
        aliases[n_in + k] = n_out + m
    call = pl.pallas_call(
        hosted, name=name, grid=grid, in_specs=list(in_specs) + [_ANY] * c_in,
        out_specs=list(out_specs) + [_ANY] * c_out, out_shape=list(out_shape) + comm.out_shape,
        scratch_shapes=list(scratch) + comm.sems, input_output_aliases=aliases,
        compiler_params=pltpu.CompilerParams(**params), **kw)

    def run(*args):
        res = call(*args, *comm.inputs)
        return res[:n_out], res[n_out:]

    return run


def _sds(shape, dtype):
    return jax.ShapeDtypeStruct(tuple(shape), dtype)


def _resident(shape):
    zeros = (0,) * len(shape)
    return pl.BlockSpec(tuple(shape), lambda *_: zeros, pipeline_mode=pl.Buffered(1))


def _sigmoid(x):
    return jax.nn.sigmoid(x)


def _softplus(x):
    return jnp.maximum(x, 0.0) + jnp.log(1.0 + jnp.exp(-jnp.abs(x)))


_GELU_C = 0.7978845608028654
_GELU_A = 0.044715


def _gelu(x):
    t = jnp.tanh(_GELU_C * (x + _GELU_A * x * x * x))
    return 0.5 * x * (1.0 + t)


def _gelu_grad(x):
    t = jnp.tanh(_GELU_C * (x + _GELU_A * x * x * x))
    return 0.5 * (1.0 + t) + 0.5 * x * (1.0 - t * t) * _GELU_C * (1.0 + 3.0 * _GELU_A * x * x)


def _silu_grad(x):
    s = _sigmoid(x)
    return s * (1.0 + x * (1.0 - s))


def _rms_scale(xv):
    return lax.rsqrt(jnp.mean(xv * xv, axis=-1, keepdims=True) + EPS)


def _rms_bwd(dh, xv, g):
    r = _rms_scale(xv)
    xn = xv * r
    dg = jnp.sum(dh * xn, axis=0, keepdims=True)
    dxn = dh * g
    dx = r * (dxn - xn * jnp.mean(dxn * xn, axis=-1, keepdims=True))
    return dx, dg


def _iota2(shape, dim):
    return lax.broadcasted_iota(jnp.int32, shape, dim)


def _col_to_row(col):
    n = col.shape[0]
    eye = _iota2((n, n), 0) == _iota2((n, n), 1)
    return jnp.sum(jnp.where(eye, col, 0.0), axis=0, keepdims=True)


def _row_to_col(row):
    n = row.shape[1]
    eye = _iota2((n, n), 0) == _iota2((n, n), 1)
    return jnp.sum(jnp.where(eye, row, 0.0), axis=1, keepdims=True)


MXU_DIM = 256


def _hidden_chunks(f, step=3 * MXU_DIM):
    return [(c0, min(c0 + step, f)) for c0 in range(0, f, step)]

def ffn_fwd(x, gnorm, wg, wu, wd, name, comm=None):
    n, d = x.shape
    f = wg.shape[0]
    tm = min(ROW_TILE, n)

    def body(x_ref, g_ref, wg_ref, wu_ref, wd_ref, xo_ref, h_ref, gate_ref, up_ref, act_ref, acc_ref):
        xv = x_ref[...]
        h = (xv * _rms_scale(xv) * g_ref[...]).astype(BF16)
        h_ref[...] = h
        for c0, c1 in _hidden_chunks(f):
            gate = _dot(h, wg_ref[c0:c1, :], NT)
            up = _dot(h, wu_ref[c0:c1, :], NT)
            act = (gate * _sigmoid(gate) * up).astype(BF16)
            gate_ref[:, c0:c1] = gate.astype(BF16)
            up_ref[:, c0:c1] = up.astype(BF16)
            act_ref[:, c0:c1] = act
            part = _dot(act, wd_ref[c0:c1, :], NN)
            if c0 == 0:
                acc_ref[...] = part
            else:
                acc_ref[...] += part
        xo_ref[...] = xv + 0.5 * acc_ref[...]

    row = pl.BlockSpec((tm, d), lambda i: (i, 0))
    wide = pl.BlockSpec((tm, f), lambda i: (i, 0))
    return _call(
        body, name=name, grid=(n // tm,),
        in_specs=[row, pl.BlockSpec((1, d), lambda i: (0, 0))] + [_resident((f, d))] * 3,
        out_specs=[row, row, wide, wide, wide],
        out_shape=[_sds((n, d), F32), _sds((n, d), BF16)] + [_sds((n, f), BF16)] * 3,
        scratch=[pltpu.VMEM((tm, d), F32)], comm=comm,
    )(x, gnorm, wg, wu, wd)


def ffn_bwd_act(dy, x, gnorm, gate, up, wg, wu, wd, name, comm=None):
    n, d = x.shape
    f = wg.shape[0]
    tm = min(ROW_TILE // 2, n)

    def body(dy_ref, x_ref, g_ref, gate_ref, up_ref, wg_ref, wu_ref, wd_ref,
             dx_ref, dgate_ref, dup_ref, dyh_ref, dg_ref, acc_ref):
        @pl.when(pl.program_id(0) == 0)
        def _():
            dg_ref[...] = jnp.zeros_like(dg_ref)

        dyh = (0.5 * dy_ref[...]).astype(BF16)
        dyh_ref[...] = dyh
        for c0, c1 in _hidden_chunks(f):
            dact = _dot(dyh, wd_ref[c0:c1, :], NT)
            gt = gate_ref[:, c0:c1].astype(F32)
            u = up_ref[:, c0:c1].astype(F32)
            s = _sigmoid(gt)
            dup = (dact * (gt * s)).astype(BF16)
            dgate = (dact * u * (s * (1.0 + gt * (1.0 - s)))).astype(BF16)
            dup_ref[:, c0:c1] = dup
            dgate_ref[:, c0:c1] = dgate
            part = _dot(dgate, wg_ref[c0:c1, :], NN) + _dot(dup, wu_ref[c0:c1, :], NN)
            if c0 == 0:
                acc_ref[...] = part
            else:
                acc_ref[...] += part
        dxn, dg = _rms_bwd(acc_ref[...], x_ref[...], g_ref[...])
        dx_ref[...] = dy_ref[...] + dxn
        dg_ref[...] += dg

    row = pl.BlockSpec((tm, d), lambda i: (i, 0))
    wide = pl.BlockSpec((tm, f), lambda i: (i, 0))
    vec = pl.BlockSpec((1, d), lambda i: (0, 0))
    wres = _resident((f, d))
    return _call(
        body, name=name, grid=(n // tm,),
        in_specs=[row, row, vec, wide, wide, wres, wres, wres],
        out_specs=[row, wide, wide, row, vec],
        out_shape=[_sds((n, d), F32), _sds((n, f), BF16), _sds((n, f), BF16),
                   _sds((n, d), BF16), _sds((1, d), F32)],
        scratch=[pltpu.VMEM((tm, d), F32)], comm=comm,
    )(dy, x, gnorm, gate, up, wg, wu, wd)


def ffn_bwd_w(h, dyh, dgate, dup, act, name, comm=None):
    n, d = h.shape
    f = dgate.shape[1]
    fh = f // 2
    tk = min(ROW_TILE, n)

    def body(h_ref, dyh_ref, dgate_ref, dup_ref, act_ref, dwg_ref, dwu_ref, dwd_ref):
        @pl.when(pl.program_id(1) == 0)
        def _():
            dwg_ref[...] = jnp.zeros_like(dwg_ref)
            dwu_ref[...] = jnp.zeros_like(dwu_ref)
            dwd_ref[...] = jnp.zeros_like(dwd_ref)

        hv = h_ref[...]
        dyv = dyh_ref[...]
        for c0, c1 in _hidden_chunks(fh, 2 * MXU_DIM):
            dwg_ref[c0:c1, :] += _dot(dgate_ref[:, c0:c1], hv, TN)
            dwu_ref[c0:c1, :] += _dot(dup_ref[:, c0:c1], hv, TN)
            dwd_ref[c0:c1, :] += _dot(act_ref[:, c0:c1], dyv, TN)

    row = pl.BlockSpec((tk, d), lambda j, k: (k, 0))
    blk = pl.BlockSpec((tk, fh), lambda j, k: (k, j))
    return _call(
        body, name=name, grid=(2, n // tk),
        in_specs=[row, row, blk, blk, blk],
        out_specs=[pl.BlockSpec((fh, d), lambda j, k: (j, 0))] * 3,
        out_shape=[_sds((f, d), F32)] * 3, comm=comm,
    )(h, dyh, dgate, dup, act)


def final_loss(x, gnorm, target, name):
    n, d = x.shape
    tm = min(ROW_TILE, n)

    def body(x_ref, g_ref, t_ref, dx_ref, dg_ref, loss_ref):
        @pl.when(pl.program_id(0) == 0)
        def _():
            dg_ref[...] = jnp.zeros_like(dg_ref)
            loss_ref[...] = jnp.zeros_like(loss_ref)

        xv = x_ref[...]
        y = xv * _rms_scale(xv) * g_ref[...]
        err = y - t_ref[...]
        part = 0.5 * jnp.sum(jnp.mean(err * err, axis=-1, keepdims=True), axis=0, keepdims=True)
        loss_ref[...] += jnp.broadcast_to(part, loss_ref.shape)
        dx, dg = _rms_bwd(err * (1.0 / d), xv, g_ref[...])
        dx_ref[...] = dx
        dg_ref[...] += dg

    row = pl.BlockSpec((tm, d), lambda i: (i, 0))
    vec = pl.BlockSpec((1, d), lambda i: (0, 0))
    return _call(
        body, name=name, grid=(n // tm,),
        in_specs=[row, vec, row],
        out_specs=[row, vec, pl.BlockSpec((1, LANES), lambda i: (0, 0))],
        out_shape=[_sds((n, d), F32), _sds((1, d), F32), _sds((1, LANES), F32)],
    )(x, gnorm, target)


def in_proj_fwd(x, gnorm, w, name):
    n, d = x.shape
    cols = w.shape[1]
    tm = min(ROW_TILE, n)
    tn = 640

    def body(x_ref, g_ref, w_ref, p_ref, h_ref):
        xv = x_ref[...]
        h = (xv * _rms_scale(xv) * g_ref[...]).astype(BF16)
        h_ref[...] = h
        for c0 in range(0, cols, tn):
            p_ref[:, c0:c0 + tn] = _dot(h, w_ref[:, c0:c0 + tn], NN)

    return _call(
        body, name=name, grid=(n // tm,),
        in_specs=[pl.BlockSpec((tm, d), lambda i: (i, 0)),
                  pl.BlockSpec((1, d), lambda i: (0, 0)), _resident((d, cols))],
        out_specs=[pl.BlockSpec((tm, cols), lambda i: (i, 0)),
                   pl.BlockSpec((tm, d), lambda i: (i, 0))],
        out_shape=[_sds((n, cols), F32), _sds((n, d), BF16)],
    )(x, gnorm, w)


def in_proj_bwd_x(dproj, w, x, gnorm, dres, name, comm=None):
    n, d = x.shape
    cols = w.shape[1]
    tm = min(ROW_TILE, n)

    def body(dp_ref, w_ref, x_ref, g_ref, dr_ref, dx_ref, dg_ref):
        @pl.when(pl.program_id(0) == 0)
        def _():
            dg_ref[...] = jnp.zeros_like(dg_ref)

        dh = _dot(dp_ref[...], w_ref[...], NT)
        dxn, dg = _rms_bwd(dh, x_ref[...], g_ref[...])
        dx_ref[...] = dr_ref[...] + dxn
        dg_ref[...] += dg

    row = pl.BlockSpec((tm, d), lambda i: (i, 0))
    vec = pl.BlockSpec((1, d), lambda i: (0, 0))
    return _call(
        body, name=name, grid=(n // tm,),
        in_specs=[pl.BlockSpec((tm, cols), lambda i: (i, 0)),
                  _resident((d, cols)), row, vec, row],
        out_specs=[row, vec],
        out_shape=[_sds((n, d), F32), _sds((1, d), F32)], comm=comm,
    )(dproj, w, x, gnorm, dres)


def matmul_tn(a, b, tn, name):
    n, ka = a.shape
    cb = b.shape[1]
    tk = min(ROW_TILE, n)

    def body(a_ref, b_ref, o_ref):
        @pl.when(pl.program_id(0) == 0)
        def _():
            o_ref[...] = jnp.zeros_like(o_ref)

        av = a_ref[...]
        for c0 in range(0, cb, tn):
            o_ref[:, c0:c0 + tn] += _dot(av, b_ref[:, c0:c0 + tn], TN)

    return _call(
        body, name=name, grid=(n // tk,),
        in_specs=[pl.BlockSpec((tk, ka), lambda k: (k, 0)),
                  pl.BlockSpec((tk, cb), lambda k: (k, 0))],
        out_specs=pl.BlockSpec((ka, cb), lambda k: (0, 0)),
        out_shape=_sds((ka, cb), F32),
    )(a, b)


def out_proj_fwd(x, sg_out, dn_out, w, name):
    n, d = x.shape
    tm = min(ROW_TILE, n)

    def body(x_ref, a_ref, b_ref, w_ref, o_ref):
        o_ref[...] = (x_ref[...] + _dot(a_ref[...], w_ref[0:HALF_W, :], NN)
                      + _dot(b_ref[...], w_ref[HALF_W:2 * HALF_W, :], NN))

    row = pl.BlockSpec((tm, d), lambda i: (i, 0))
    half = pl.BlockSpec((tm, HALF_W), lambda i: (i, 0))
    return _call(
        body, name=name, grid=(n // tm,),
        in_specs=[row, half, half, pl.BlockSpec((2 * HALF_W, d), lambda i: (0, 0))],
        out_specs=row, out_shape=_sds((n, d), F32),
    )(x, sg_out, dn_out, w)


def out_proj_bwd_x(dy, w, name, comm=None):
    n, d = dy.shape
    tm = min(ROW_TILE, n)

    def body(dy_ref, w_ref, dsg_ref, ddn_ref, dyb_ref):
        dyb = dy_ref[...].astype(BF16)
        dyb_ref[...] = dyb
        dsg_ref[...] = _dot(dyb, w_ref[0:HALF_W, :], NT)
        ddn_ref[...] = _dot(dyb, w_ref[HALF_W:2 * HALF_W, :], NT)

    row = pl.BlockSpec((tm, d), lambda i: (i, 0))
    half = pl.BlockSpec((tm, HALF_W), lambda i: (i, 0))
    return _call(
        body, name=name, grid=(n // tm,),
        in_specs=[row, pl.BlockSpec((2 * HALF_W, d), lambda i: (0, 0))],
        out_specs=[half, half, row],
        out_shape=[_sds((n, HALF_W), F32), _sds((n, HALF_W), F32), _sds((n, d), BF16)], comm=comm,
    )(dy, w)


def _sg_group_masks():
    col = _iota2((SG_CHUNK, HALF_W), 1)
    return [jnp.logical_and(col >= g * SG_GROUP_DIM, col < (g + 1) * SG_GROUP_DIM)
            for g in range(SG_GROUPS)]


def _sg_causal():
    return _iota2((SG_CHUNK, SG_CHUNK), 0) >= _iota2((SG_CHUNK, SG_CHUNK), 1)


def _sg_forward_chunk(pu, pv, ln_g, ln_b, wc, bias, masks):
    u = _gelu(pu)
    v = _gelu(pv)
    mu = jnp.mean(v, axis=-1, keepdims=True)
    vc = v - mu
    rs = lax.rsqrt(jnp.mean(vc * vc, axis=-1, keepdims=True) + EPS)
    xhat = vc * rs
    vn = (xhat * ln_g + ln_b).astype(BF16)
    vs = bias
    for g in range(SG_GROUPS):
        vs = vs + jnp.where(masks[g], _dot(wc[g], vn, NN), 0.0)
    return u, xhat, rs, vn, vs


def sg_fwd(proj, ln_g, ln_b, w_s, bias_tile, name):
    n = proj.shape[0]
    tm = min(ROW_TILE, n)

    def body(pu_ref, pv_ref, g_ref, b_ref, w_ref, bias_ref, o_ref):
        causal = _sg_causal()
        wc = [jnp.where(causal, w_ref[g], 0.0).astype(BF16) for g in range(SG_GROUPS)]
        masks = _sg_group_masks()
        for ci in range(tm // SG_CHUNK):
            rows = slice(ci * SG_CHUNK, (ci + 1) * SG_CHUNK)
            u, _, _, _, vs = _sg_forward_chunk(pu_ref[rows, :], pv_ref[rows, :], g_ref[...],
                                               b_ref[...], wc, bias_ref[...], masks)
            o_ref[rows, :] = (u * vs).astype(BF16)

    vec = pl.BlockSpec((1, HALF_W), lambda i: (0, 0))
    return _call(
        body, name=name, grid=(n // tm,),
        in_specs=[pl.BlockSpec((tm, HALF_W), lambda i: (i, 0)),
                  pl.BlockSpec((tm, HALF_W), lambda i: (i, 1)), vec, vec,
                  pl.BlockSpec((SG_GROUPS, SG_CHUNK, SG_CHUNK), lambda i: (0, 0, 0)),
                  pl.BlockSpec((SG_CHUNK, HALF_W), lambda i: (0, 0))],
        out_specs=pl.BlockSpec((tm, HALF_W), lambda i: (i, 0)),
        out_shape=_sds((n, HALF_W), BF16),
    )(proj, proj, ln_g, ln_b, w_s, bias_tile)


def sg_bwd(dsg, proj, ln_g, ln_b, w_s, bias_tile, name):
    n = proj.shape[0]
    tm = min(ROW_TILE, n)

    def body(d_ref, pu_ref, pv_ref, g_ref, b_ref, w_ref, bias_ref,
             dp_ref, dw_ref, db_ref, dlg_ref, dlb_ref):
        @pl.when(pl.program_id(0) == 0)
        def _():
            dw_ref[...] = jnp.zeros_like(dw_ref)
            db_ref[...] = jnp.zeros_like(db_ref)
            dlg_ref[...] = jnp.zeros_like(dlg_ref)
            dlb_ref[...] = jnp.zeros_like(dlb_ref)

        causal = _sg_causal()
        wc = [jnp.where(causal, w_ref[g], 0.0).astype(BF16) for g in range(SG_GROUPS)]
        masks = _sg_group_masks()
        ln_g_v = g_ref[...]
        for ci in range(tm // SG_CHUNK):
            rows = slice(ci * SG_CHUNK, (ci + 1) * SG_CHUNK)
            pu = pu_ref[rows, :]
            pv = pv_ref[rows, :]
            u, xhat, rs, vn, vs = _sg_forward_chunk(pu, pv, ln_g_v, b_ref[...], wc,
                                                    bias_ref[...], masks)
            dout = d_ref[rows, :]
            dp_ref[rows, 0:HALF_W] = (dout * vs * _gelu_grad(pu)).astype(BF16)
            dvs = dout * u
            dvs_b = dvs.astype(BF16)
            db_ref[...] += dvs
            dvn = jnp.zeros_like(dvs)
            for g in range(SG_GROUPS):
                dvn = dvn + jnp.where(masks[g], _dot(wc[g], dvs_b, TN), 0.0)
                dwg = _dot(jnp.where(masks[g], dvs_b, jnp.zeros_like(dvs_b)), vn, NT)
                dw_ref[g] += jnp.where(causal, dwg, 0.0)
            dlg_ref[...] += jnp.sum(dvn * xhat, axis=0, keepdims=True)
            dlb_ref[...] += jnp.sum(dvn, axis=0, keepdims=True)
            dxh = dvn * ln_g_v
            dv = rs * (dxh - jnp.mean(dxh, axis=-1, keepdims=True)
                       - xhat * jnp.mean(dxh * xhat, axis=-1, keepdims=True))
            dp_ref[rows, HALF_W:2 * HALF_W] = (dv * _gelu_grad(pv)).astype(BF16)

    vec = pl.BlockSpec((1, HALF_W), lambda i: (0, 0))
    wspec = pl.BlockSpec((SG_GROUPS, SG_CHUNK, SG_CHUNK), lambda i: (0, 0, 0))
    tile = pl.BlockSpec((SG_CHUNK, HALF_W), lambda i: (0, 0))
    return _call(
        body, name=name, grid=(n // tm,),
        in_specs=[pl.BlockSpec((tm, HALF_W), lambda i: (i, 0)),
                  pl.BlockSpec((tm, HALF_W), lambda i: (i, 0)),
                  pl.BlockSpec((tm, HALF_W), lambda i: (i, 1)), vec, vec, wspec, tile],
        out_specs=[pl.BlockSpec((tm, 2 * HALF_W), lambda i: (i, 0)), wspec, tile, vec, vec],
        out_shape=[_sds((n, PROJ_W), BF16), _sds((SG_GROUPS, SG_CHUNK, SG_CHUNK), F32),
                   _sds((SG_CHUNK, HALF_W), F32), _sds((1, HALF_W), F32), _sds((1, HALF_W), F32)],
    )(dsg, proj, proj, ln_g, ln_b, w_s, bias_tile)


CONV_K = 4
CONV_BLOCK = 256


def _shift_down(x, s, row):
    if s == 0:
        return x
    return jnp.where(row >= s, pltpu.roll(x, s, 0), 0.0)


def _shift_up(x, s, row):
    if s == 0:
        return x
    t_len = x.shape[0]
    return jnp.where(row < t_len - s, pltpu.roll(x, t_len - s, 0), 0.0)


def _conv_taps(x, row):
    return [_shift_down(x, CONV_K - 1 - j, row) for j in range(CONV_K)]


def _conv(taps, w):
    y = taps[0] * w[0:1, :]
    for j in range(1, CONV_K):
        y = y + taps[j] * w[j:j + 1, :]
    return y


def dn_conv_fwd(proj3, conv_w, name):
    b, t, _ = proj3.shape
    nblk = 3 * HALF_W // CONV_BLOCK
    first = 2 * HALF_W // CONV_BLOCK
    n_norm = 2 * HALF_W // CONV_BLOCK

    def body(x_ref, w_ref, o_ref):
        s = pl.program_id(1)
        x = x_ref[0]
        y = _conv(_conv_taps(x, _iota2(x.shape, 0)), w_ref[...])
        y = y * _sigmoid(y)

        @pl.when(s < n_norm)
        def _():
            for h in range(CONV_BLOCK // HEAD_DIM):
                cs = slice(h * HEAD_DIM, (h + 1) * HEAD_DIM)
                yh = y[:, cs]
                o_ref[0, :, cs] = yh * lax.rsqrt(jnp.sum(yh * yh, axis=-1, keepdims=True) + EPS)

        @pl.when(s >= n_norm)
        def _():
            o_ref[0] = y

    return _call(
        body, name=name, grid=(b, nblk),
        in_specs=[pl.BlockSpec((1, t, CONV_BLOCK), lambda i, s: (i, 0, first + s)),
                  pl.BlockSpec((CONV_K, CONV_BLOCK), lambda i, s: (0, s))],
        out_specs=pl.BlockSpec((1, t, CONV_BLOCK), lambda i, s: (i, 0, s)),
        out_shape=_sds((b, t, 3 * HALF_W), F32),
    )(proj3, conv_w)


def dn_conv_bwd(dqkv, proj3, conv_w, dproj3, name, comm=None):
    b, t, _ = proj3.shape
    nblk = 3 * HALF_W // CONV_BLOCK
    first = 2 * HALF_W // CONV_BLOCK
    n_norm = 2 * HALF_W // CONV_BLOCK

    def body(d_ref, x_ref, w_ref, dproj_in, dx_ref, dw_ref, ds_ref):
        s = pl.program_id(0)

        @pl.when(pl.program_id(1) == 0)
        def _():
            dw_ref[...] = jnp.zeros_like(dw_ref)

        x = x_ref[0]
        w = w_ref[...]
        row = _iota2(x.shape, 0)
        taps = _conv_taps(x, row)
        c = _conv(taps, w)
        sg = _sigmoid(c)
        y = c * sg

        @pl.when(s < n_norm)
        def _():
            for h in range(CONV_BLOCK // HEAD_DIM):
                cs = slice(h * HEAD_DIM, (h + 1) * HEAD_DIM)
                yh = y[:, cs]
                r = lax.rsqrt(jnp.sum(yh * yh, axis=-1, keepdims=True) + EPS)
                nh = yh * r
                dn = d_ref[0, :, cs]
                ds_ref[:, cs] = r * (dn - nh * jnp.sum(dn * nh, axis=-1, keepdims=True))

        @pl.when(s >= n_norm)
        def _():
            ds_ref[...] = d_ref[0]

        dc = ds_ref[...] * (sg * (1.0 + c * (1.0 - sg)))
        dx = _shift_up(dc, CONV_K - 1, row) * w[0:1, :]
        for j in range(1, CONV_K):
            dx = dx + _shift_up(dc, CONV_K - 1 - j, row) * w[j:j + 1, :]
        dx_ref[0] = dx.astype(BF16)
        for j in range(CONV_K):
            dw_ref[j:j + 1, :] += jnp.sum(dc * taps[j], axis=0, keepdims=True)

    return _call(
        body, name=name, grid=(nblk, b),
        in_specs=[pl.BlockSpec((1, t, CONV_BLOCK), lambda s, i: (i, 0, s)),
                  pl.BlockSpec((1, t, CONV_BLOCK), lambda s, i: (i, 0, first + s)),
                  pl.BlockSpec((CONV_K, CONV_BLOCK), lambda s, i: (0, s)), _ANY],
        out_specs=[pl.BlockSpec((1, t, CONV_BLOCK), lambda s, i: (i, 0, first + s)),
                   pl.BlockSpec((CONV_K, CONV_BLOCK), lambda s, i: (0, s))],
        out_shape=[_sds(dproj3.shape, BF16), _sds((CONV_K, 3 * HALF_W), F32)],
        scratch=[pltpu.VMEM((t, CONV_BLOCK), F32)],
        input_output_aliases={3: 0}, comm=comm,
    )(dqkv, proj3, conv_w, dproj3)


def _chunk_masks():
    ii = _iota2((DN_CHUNK, DN_CHUNK), 0)
    jj = _iota2((DN_CHUNK, DN_CHUNK), 1)
    return ii >= jj, ii > jj, ii == jj


LOCKSTEP_CHUNKS = 4


def _inv_unit_lower_many(l_mats, eye):
    eye_f = jnp.where(eye, 1.0, 0.0)
    ps = [-l for l in l_mats]
    ts = [eye_f + p for p in ps]
    pss = [_split(p) for p in ps]
    size = 2
    while size < DN_CHUNK:
        ps = [_dot3(s, s) for s in pss]
        pss = [_split(p) for p in ps]
        ts = [t + _dot3(_split(t), s) for t, s in zip(ts, pss)]
        size *= 2
    return ts


def _gates(pba, ea_row, dtb_row):
    beta = _sigmoid(pba)
    g = -ea_row * _softplus(pba + dtb_row)
    return beta, g


def _chunk_decay(gcol):
    incl, strict, eye = _chunk_masks()
    grow = jnp.sum(jnp.where(eye, gcol, 0.0), axis=0, keepdims=True)
    decay = jnp.where(incl, jnp.exp(jnp.where(incl, gcol - grow, 0.0)), 0.0)
    return decay, incl, strict, eye


def dn_chunk_fwd(qkv, proj3, alog_row, dtb_row, name):
    b, t, _ = qkv.shape
    rblk = min(256, t)
    n_in = rblk // DN_CHUNK

    def body(q_ref, k_ref, v_ref, pba_ref, al_ref, dtb_ref,
             u_ref, w_ref, qd_ref, kd_ref, qk_ref, ti_ref, gc_ref):
        ea = jnp.exp(al_ref[...])
        tri = jnp.where(_chunk_masks()[0], 1.0, 0.0)

        _, strict, eye = _chunk_masks()

        def chunk_group(cg, carry):
            items = []
            for sub in range(LOCKSTEP_CHUNKS):
                rows = pl.ds(pl.multiple_of((cg * LOCKSTEP_CHUNKS + sub) * DN_CHUNK, DN_CHUNK), DN_CHUNK)
                beta_all, g_all = _gates(pba_ref[0, rows, :], ea, dtb_ref[...])
                gc = _dot_exact_lhs(tri, g_all)
                gc_ref[0, rows, :] = gc
                for h in range(N_HEADS):
                    items.append((rows, h, beta_all[:, h:h + 1], gc[:, N_HEADS + h:N_HEADS + h + 1]))
            ks, kbs, decays, egs = [], [], [], []
            for rows, h, beta, gcol in items:
                cs = slice(h * HEAD_DIM, (h + 1) * HEAD_DIM)
                k = k_ref[0, rows, cs]
                ks.append(k)
                kbs.append(k * beta)
                decays.append(_chunk_decay(gcol)[0])
                egs.append(jnp.exp(gcol))
            ms = [_bdot(kb, k, NT) for kb, k in zip(kbs, ks)]
            tinvs = _inv_unit_lower_many([jnp.where(strict, m * dc, 0.0) for m, dc in zip(ms, decays)], eye)
            tsps = [_split(t) for t in tinvs]
            for (rows, h, beta, gcol), tsp, tinv in zip(items, tsps, tinvs):
                cs = slice(h * HEAD_DIM, (h + 1) * HEAD_DIM)
                u_ref[0, rows, cs] = _dot3(tsp, _split(v_ref[0, rows, cs] * beta))
                ti_ref[0, h, rows, :] = tinv
            for (rows, h, beta, gcol), tsp, kb, eg in zip(items, tsps, kbs, egs):
                cs = slice(h * HEAD_DIM, (h + 1) * HEAD_DIM)
                w_ref[0, rows, cs] = _dot3(tsp, _split(kb * eg))
            for (rows, h, beta, gcol), k, dc, eg in zip(items, ks, decays, egs):
                cs = slice(h * HEAD_DIM, (h + 1) * HEAD_DIM)
                q = q_ref[0, rows, cs] * QK_SCALE
                qk_ref[0, h, rows, :] = _bdot(q, k, NT) * dc
                qd_ref[0, rows, cs] = q * eg
                kd_ref[0, rows, cs] = k * jnp.exp(gcol[DN_CHUNK - 1:DN_CHUNK, :] - gcol)
            return carry

        lax.fori_loop(0, n_in // LOCKSTEP_CHUNKS, chunk_group, 0)

    def seg(cblk):
        return pl.BlockSpec((1, rblk, HALF_W), lambda i, r: (i, r, cblk))

    vec = pl.BlockSpec((1, LANES), lambda i, r: (0, 0))
    wide = pl.BlockSpec((1, rblk, HALF_W), lambda i, r: (i, r, 0))
    sq = pl.BlockSpec((1, N_HEADS, rblk, DN_CHUNK), lambda i, r: (i, 0, r, 0))
    return _call(
        body, name=name, grid=(b, t // rblk),
        in_specs=[seg(0), seg(1), seg(2),
                  pl.BlockSpec((1, rblk, LANES), lambda i, r: (i, r, GATE_COL_BLOCK)), vec, vec],
        out_specs=[wide, wide, wide, wide, sq, sq,
                   pl.BlockSpec((1, rblk, LANES), lambda i, r: (i, r, 0))],
        out_shape=[_sds((b, t, HALF_W), F32)] * 4
        + [_sds((b, N_HEADS, t, DN_CHUNK), F32)] * 2 + [_sds((b, t, LANES), F32)],
    )(qkv, qkv, qkv, proj3, alog_row, dtb_row)


def dn_scan_fwd(u, w, qd, kd, qk, gc, name):
    b, t, _ = u.shape
    nc = t // DN_CHUNK
    bh = b * N_HEADS

    def body(u_ref, w_ref, qd_ref, kd_ref, qk_ref, gc_ref, o_ref, sin_ref, s_ref):
        @pl.when(pl.program_id(0) == 0)
        def _():
            s_ref[...] = jnp.zeros_like(s_ref)

        items = [(bi, h, slice(h * HEAD_DIM, (h + 1) * HEAD_DIM)) for bi in range(b) for h in range(N_HEADS)]
        sbs = []
        for bi, h, cs in items:
            s = s_ref[bi * N_HEADS + h]
            sin_ref[0, bi * N_HEADS + h] = s
            sbs.append(s.astype(BF16))
        ws = [_bdot(w_ref[bi, :, cs], sb, NN) for (bi, h, cs), sb in zip(items, sbs)]
        qs = [_bdot(qd_ref[bi, :, cs], sb, NN) for (bi, h, cs), sb in zip(items, sbs)]
        vbs = [(u_ref[bi, :, cs] - wsi).astype(BF16) for (bi, h, cs), wsi in zip(items, ws)]
        for (bi, h, cs), qsi, vb in zip(items, qs, vbs):
            o_ref[bi, :, cs] = qsi + _bdot(qk_ref[bi, h], vb, NN)
        for (bi, h, cs), vb in zip(items, vbs):
            gl = jnp.exp(gc_ref[bi, DN_CHUNK - 1:DN_CHUNK, N_HEADS + h:N_HEADS + h + 1])
            idx = bi * N_HEADS + h
            s_ref[idx] = s_ref[idx] * gl + _bdot(kd_ref[bi, :, cs], vb, TN)

    wide = pl.BlockSpec((b, DN_CHUNK, HALF_W), lambda c: (0, c, 0))
    return _call(
        body, name=name, grid=(nc,),
        in_specs=[wide, wide, wide, wide,
                  pl.BlockSpec((b, N_HEADS, DN_CHUNK, DN_CHUNK), lambda c: (0, 0, c, 0)),
                  pl.BlockSpec((b, DN_CHUNK, LANES), lambda c: (0, c, 0))],
        out_specs=[wide, pl.BlockSpec((1, bh, HEAD_DIM, HEAD_DIM), lambda c: (c, 0, 0, 0))],
        out_shape=[_sds((b, t, HALF_W), F32), _sds((nc, bh, HEAD_DIM, HEAD_DIM), F32)],
        scratch=[pltpu.VMEM((bh, HEAD_DIM, HEAD_DIM), F32)],
    )(u, w, qd, kd, qk, gc)


def dn_scan_bwd(do, u, w, qd, kd, qk, gc, s_in, name):
    b, t, _ = u.shape
    nc = t // DN_CHUNK
    bh = b * N_HEADS

    def body(do_ref, u_ref, w_ref, qd_ref, kd_ref, qk_ref, gc_ref, sin_ref,
             du_ref, dw_ref, dqd_ref, dkd_ref, dqk_ref, dgc_ref, ds_ref):
        @pl.when(pl.program_id(0) == 0)
        def _():
            ds_ref[...] = jnp.zeros_like(ds_ref)

        last_row = _iota2((DN_CHUNK, LANES), 0) == DN_CHUNK - 1
        lane = _iota2((DN_CHUNK, LANES), 1)
        items = [(bi, h, slice(h * HEAD_DIM, (h + 1) * HEAD_DIM)) for bi in range(b) for h in range(N_HEADS)]
        sbs = [sin_ref[0, bi * N_HEADS + h].astype(BF16) for bi, h, cs in items]
        wvs = [w_ref[bi, :, cs].astype(BF16) for bi, h, cs in items]
        dovs = [do_ref[bi, :, cs].astype(BF16) for bi, h, cs in items]
        dsbs = [ds_ref[bi * N_HEADS + h].astype(BF16) for bi, h, cs in items]
        vbs = [(u_ref[bi, :, cs] - _dot(wv, sb, NN)).astype(BF16)
               for (bi, h, cs), wv, sb in zip(items, wvs, sbs)]
        for (bi, h, cs), dov, sb in zip(items, dovs, sbs):
            dqd_ref[bi, :, cs] = _dot(dov, sb, NT)
        dvns = [_dot(kd_ref[bi, :, cs].astype(BF16), dsb, NN) + _dot(qk_ref[bi, h].astype(BF16), dov, TN)
                for (bi, h, cs), dsb, dov in zip(items, dsbs, dovs)]
        for (bi, h, cs), vb, dsb, dov in zip(items, vbs, dsbs, dovs):
            dkd_ref[bi, :, cs] = _dot(vb, dsb, NT)
            dqk_ref[bi, h] = _dot(dov, vb, NT)
        dgls = []
        for (bi, h, cs), dvn, sb, wv, dov in zip(items, dvns, sbs, wvs, dovs):
            idx = bi * N_HEADS + h
            du_ref[bi, :, cs] = dvn
            dvn_b = dvn.astype(BF16)
            dw_ref[bi, :, cs] = -_dot(dvn_b, sb, NT)
            gl = jnp.exp(gc_ref[bi, DN_CHUNK - 1:DN_CHUNK, N_HEADS + h:N_HEADS + h + 1])
            ds = ds_ref[idx]
            dgl = jnp.sum(jnp.sum(ds * sin_ref[0, idx], axis=1, keepdims=True), axis=0, keepdims=True)
            dgls.append(dgl * gl)
            ds_ref[idx] = (ds * gl + _dot(qd_ref[bi, :, cs].astype(BF16), dov, TN)
                           - _dot(wv, dvn_b, TN))
        for bi in range(b):
            dgc = jnp.zeros((DN_CHUNK, LANES), F32)
            for h in range(N_HEADS):
                dgc = dgc + jnp.where(jnp.logical_and(last_row, lane == N_HEADS + h),
                                      dgls[bi * N_HEADS + h], 0.0)
            dgc_ref[bi] = dgc

    def rev(c):
        return nc - 1 - c

    wide = pl.BlockSpec((b, DN_CHUNK, HALF_W), lambda c: (0, rev(c), 0))
    sq = pl.BlockSpec((b, N_HEADS, DN_CHUNK, DN_CHUNK), lambda c: (0, 0, rev(c), 0))
    gates = pl.BlockSpec((b, DN_CHUNK, LANES), lambda c: (0, rev(c), 0))
    return _call(
        body, name=name, grid=(nc,),
        in_specs=[wide, wide, wide, wide, wide, sq, gates,
                  pl.BlockSpec((1, bh, HEAD_DIM, HEAD_DIM), lambda c: (rev(c), 0, 0, 0))],
        out_specs=[wide, wide, wide, wide, sq, gates],
        out_shape=[_sds((b, t, HALF_W), F32)] * 4
        + [_sds((b, N_HEADS, t, DN_CHUNK), F32), _sds((b, t, LANES), F32)],
        scratch=[pltpu.VMEM((bh, HEAD_DIM, HEAD_DIM), F32)],
    )(do, u, w, qd, kd, qk, gc, s_in)


def dn_chunk_bwd(qkv, proj3, alog_row, dtb_row, tinv, u, w, du, dw, dqd, dkd, dqk, dgc_scan, dproj3, name,
                 comm=None):
    b, t, _ = qkv.shape
    rblk = min(256, t)
    n_in = rblk // DN_CHUNK

    def body(q_ref, k_ref, v_ref, pba_ref, al_ref, dtb_ref, ti_ref, u_ref, w_ref,
             du_ref, dw_ref, dqd_ref, dkd_ref, dqk_ref, dgs_ref, dproj_in,
             dq_ref, dpba_ref, dal_ref, ddtb_ref):
        @pl.when(jnp.logical_and(pl.program_id(0) == 0, pl.program_id(1) == 0))
        def _():
            dal_ref[...] = jnp.zeros_like(dal_ref)
            ddtb_ref[...] = jnp.zeros_like(ddtb_ref)

        ea = jnp.exp(al_ref[...])
        incl0 = _chunk_masks()[0]
        tri = jnp.where(incl0, 1.0, 0.0)
        tri_up = jnp.where(_iota2((DN_CHUNK, DN_CHUNK), 1) >= _iota2((DN_CHUNK, DN_CHUNK), 0), 1.0, 0.0)
        lane = _iota2((DN_CHUNK, LANES), 1)
        last_col = _iota2((DN_CHUNK, 1), 0) == DN_CHUNK - 1

        _, strict, _ = _chunk_masks()
        gate_lane = jnp.logical_and(lane >= N_HEADS, lane < 2 * N_HEADS)

        def chunk_group(cg, carry):
            tiles, items = [], []
            for sub in range(LOCKSTEP_CHUNKS):
                rows = pl.ds(pl.multiple_of((cg * LOCKSTEP_CHUNKS + sub) * DN_CHUNK, DN_CHUNK), DN_CHUNK)
                pba = pba_ref[0, rows, :]
                beta_all, g_all = _gates(pba, ea, dtb_ref[...])
                gc = _dot_exact_lhs(tri, g_all)
                tiles.append((rows, pba, beta_all, g_all))
                for h in range(N_HEADS):
                    items.append((sub, rows, h, slice(h * HEAD_DIM, (h + 1) * HEAD_DIM),
                                  beta_all[:, h:h + 1], gc[:, N_HEADS + h:N_HEADS + h + 1]))
            decays = [_chunk_decay(gcol)[0] for _, _, _, _, _, gcol in items]
            egs = [jnp.exp(gcol) for _, _, _, _, _, gcol in items]
            qbs = [(q_ref[0, rows, cs] * QK_SCALE).astype(BF16) for _, rows, h, cs, _, _ in items]
            kfs = [k_ref[0, rows, cs].astype(BF16) for _, rows, h, cs, _, _ in items]
            kbs = [k_ref[0, rows, cs] * beta for _, rows, h, cs, beta, _ in items]
            kbbs = [kb.astype(BF16) for kb in kbs]
            tsps = [_split(ti_ref[0, h, rows, :]) for _, rows, h, cs, _, _ in items]
            drus = [_dot3(tsp, _split(du_ref[0, rows, cs]), TN)
                    for (_, rows, h, cs, _, _), tsp in zip(items, tsps)]
            drws = [_dot3(tsp, _split(dw_ref[0, rows, cs]), TN)
                    for (_, rows, h, cs, _, _), tsp in zip(items, tsps)]
            m_kks = [_dot(kbb, kf, NT) for kbb, kf in zip(kbbs, kfs)]
            a_qks = [_dot(qb, kf, NT) for qb, kf in zip(qbs, kfs)]
            dls = [-jnp.where(strict, _dot3(_split(dru), _split(u_ref[0, rows, cs]), NT)
                              + _dot3(_split(drw), _split(w_ref[0, rows, cs]), NT), 0.0)
                   for (_, rows, h, cs, _, _), dru, drw in zip(items, drus, drws)]
            dms = [(dl * dc).astype(BF16) for dl, dc in zip(dls, decays)]
            das = [(dqk_ref[0, h, rows, :] * dc).astype(BF16)
                   for (_, rows, h, cs, _, _), dc in zip(items, decays)]
            dkb_mm = [_dot(dm, kf, NN) for dm, kf in zip(dms, kfs)]
            dk_mm = [_dot(dm, kbb, TN) + _dot(da, qb, TN) for dm, kbb, da, qb in zip(dms, kbbs, das, qbs)]
            dqs_mm = [_dot(da, kf, NN) for da, kf in zip(das, kfs)]
            dgc_tiles = [dgs_ref[0, rows, :] for rows, _, _, _ in tiles]
            dbeta_tiles = [jnp.zeros((DN_CHUNK, LANES), F32) for _ in tiles]
            for n_it, (sub, rows, h, cs, beta, gcol) in enumerate(items):
                eg, dc = egs[n_it], decays[n_it]
                k = k_ref[0, rows, cs]
                q = q_ref[0, rows, cs] * QK_SCALE
                kb, dru, drw = kbs[n_it], drus[n_it], drws[n_it]
                ek = jnp.exp(gcol[DN_CHUNK - 1:DN_CHUNK, :] - gcol)
                e_mat = (dls[n_it] * m_kks[n_it] + dqk_ref[0, h, rows, :] * a_qks[n_it]) * dc
                dkb = drw * eg + dkb_mm[n_it]
                dqd = dqd_ref[0, rows, cs]
                dkd = dkd_ref[0, rows, cs]
                kdk = dkd * k * ek
                kdk_total = jnp.sum(jnp.sum(kdk, axis=0, keepdims=True), axis=1, keepdims=True)
                dg = (jnp.sum(drw * kb * eg + dqd * q * eg - kdk, axis=-1, keepdims=True)
                      + jnp.sum(e_mat, axis=1, keepdims=True)
                      - _row_to_col(jnp.sum(e_mat, axis=0, keepdims=True))
                      + jnp.where(last_col, kdk_total, 0.0))
                dbeta = jnp.sum(dkb * k + dru * v_ref[0, rows, cs], axis=-1, keepdims=True)
                dq_ref[0, rows, cs] = (dqs_mm[n_it] + dqd * eg) * QK_SCALE
                dq_ref[0, rows, pl.ds(HALF_W + h * HEAD_DIM, HEAD_DIM)] = dk_mm[n_it] + dkd * ek + dkb * beta
                dq_ref[0, rows, pl.ds(2 * HALF_W + h * HEAD_DIM, HEAD_DIM)] = dru * beta
                dgc_tiles[sub] = dgc_tiles[sub] + jnp.where(lane == N_HEADS + h, dg, 0.0)
                dbeta_tiles[sub] = dbeta_tiles[sub] + jnp.where(lane == h, dbeta, 0.0)
            for (rows, pba, beta_all, g_all), dgc_tile, dbeta_tile in zip(tiles, dgc_tiles, dbeta_tiles):
                dg_tile = _dot_exact_lhs(tri_up, dgc_tile)
                da_pre = dg_tile * (-ea) * _sigmoid(pba + dtb_ref[...])
                dal_ref[...] += jnp.sum(jnp.where(gate_lane, dg_tile * g_all, 0.0), axis=0, keepdims=True)
                ddtb_ref[...] += jnp.sum(jnp.where(gate_lane, da_pre, 0.0), axis=0, keepdims=True)
                dpba_ref[0, rows, :] = jnp.where(lane < N_HEADS, dbeta_tile * beta_all * (1.0 - beta_all),
                                                 jnp.where(gate_lane, da_pre, 0.0)).astype(BF16)
            return carry

        lax.fori_loop(0, n_in // LOCKSTEP_CHUNKS, chunk_group, 0)

    def seg(cblk):
        return pl.BlockSpec((1, rblk, HALF_W), lambda i, r: (i, r, cblk))

    vec = pl.BlockSpec((1, LANES), lambda i, r: (0, 0))
    wide = pl.BlockSpec((1, rblk, HALF_W), lambda i, r: (i, r, 0))
    sq = pl.BlockSpec((1, N_HEADS, rblk, DN_CHUNK), lambda i, r: (i, 0, r, 0))
    gates = pl.BlockSpec((1, rblk, LANES), lambda i, r: (i, r, 0))
    return _call(
        body, name=name, grid=(b, t // rblk),
        in_specs=[seg(0), seg(1), seg(2),
                  pl.BlockSpec((1, rblk, LANES), lambda i, r: (i, r, GATE_COL_BLOCK)), vec, vec,
                  sq, wide, wide, wide, wide, wide, wide, sq, gates, _ANY],
        out_specs=[pl.BlockSpec((1, rblk, 3 * HALF_W), lambda i, r: (i, r, 0)),
                   pl.BlockSpec((1, rblk, LANES), lambda i, r: (i, r, GATE_COL_BLOCK)), vec, vec],
        out_shape=[_sds((b, t, 3 * HALF_W), F32), _sds(dproj3.shape, BF16),
                   _sds((1, LANES), F32), _sds((1, LANES), F32)],
        input_output_aliases={15: 1}, comm=comm,
    )(qkv, qkv, qkv, proj3, alog_row, dtb_row, tinv, u, w, du, dw, dqd, dkd, dqk, dgc_scan, dproj3)


def dn_out_fwd(o, proj, dn_norm, name):
    n = o.shape[0]
    tm = min(ROW_TILE, n)

    def body(o_ref, z_ref, g_ref, y_ref):
        for h in range(N_HEADS):
            cs = slice(h * HEAD_DIM, (h + 1) * HEAD_DIM)
            oh = o_ref[:, cs]
            z = z_ref[:, cs]
            y = oh * _rms_scale(oh) * g_ref[...]
            y_ref[:, cs] = (y * (z * _sigmoid(z))).astype(BF16)

    half = pl.BlockSpec((tm, HALF_W), lambda i: (i, 0))
    return _call(
        body, name=name, grid=(n // tm,),
        in_specs=[half, pl.BlockSpec((tm, HALF_W), lambda i: (i, 5)),
                  pl.BlockSpec((1, HEAD_DIM), lambda i: (0, 0))],
        out_specs=half, out_shape=_sds((n, HALF_W), BF16),
    )(o, proj, dn_norm)


def dn_out_bwd(dy, o, proj, dn_norm, dproj, name):
    n = o.shape[0]
    tm = min(ROW_TILE, n)

    def body(dy_ref, o_ref, z_ref, g_ref, dproj_in, do_ref, dz_ref, dg_ref):
        @pl.when(pl.program_id(0) == 0)
        def _():
            dg_ref[...] = jnp.zeros_like(dg_ref)

        g = g_ref[...]
        dg = jnp.zeros_like(g)
        for h in range(N_HEADS):
            cs = slice(h * HEAD_DIM, (h + 1) * HEAD_DIM)
            oh = o_ref[:, cs]
            z = z_ref[:, cs]
            d = dy_ref[:, cs]
            r = _rms_scale(oh)
            nh = oh * r
            sz = _sigmoid(z)
            dyn = d * (z * sz)
            dz_ref[:, cs] = (d * (nh * g) * (sz * (1.0 + z * (1.0 - sz)))).astype(BF16)
            dg = dg + jnp.sum(dyn * nh, axis=0, keepdims=True)
            dn = dyn * g
            do_ref[:, cs] = r * (dn - nh * jnp.mean(dn * nh, axis=-1, keepdims=True))
        dg_ref[...] += dg

    half = pl.BlockSpec((tm, HALF_W), lambda i: (i, 0))
    vec = pl.BlockSpec((1, HEAD_DIM), lambda i: (0, 0))
    return _call(
        body, name=name, grid=(n // tm,),
        in_specs=[half, half, pl.BlockSpec((tm, HALF_W), lambda i: (i, 5)), vec, _ANY],
        out_specs=[half, pl.BlockSpec((tm, HALF_W), lambda i: (i, 5)), vec],
        out_shape=[_sds((n, HALF_W), F32), _sds(dproj.shape, BF16), _sds((1, HEAD_DIM), F32)],
        input_output_aliases={4: 1},
    )(dy, o, proj, dn_norm, dproj)


def _adamw_math(w, g, m, v):
    m_new = ADAM_B1 * m + (1.0 - ADAM_B1) * g
    v_new = ADAM_B2 * v + (1.0 - ADAM_B2) * (g * g)
    m_hat = m_new / (1.0 - ADAM_B1 ** ADAM_STEP)
    v_hat = v_new / (1.0 - ADAM_B2 ** ADAM_STEP)
    delta = -ADAM_LR * (m_hat / (jnp.sqrt(v_hat) + ADAM_EPS) + ADAM_WD * w)
    return delta, m_new, v_new


def adamw(w, g, m, v, name):
    r, c = w.shape
    tr = r
    for cand in (256, 352):
        if r % cand == 0 and r > cand:
            tr = cand
            break

    def body(w_ref, g_ref, m_ref, v_ref, d_ref, mo_ref, vo_ref):
        d, mn, vn = _adamw_math(w_ref[...], g_ref[...], m_ref[...], v_ref[...])
        d_ref[...] = d
        mo_ref[...] = mn
        vo_ref[...] = vn

    spec = pl.BlockSpec((tr, c), lambda i: (i, 0))
    return _call(
        body, name=name, grid=(r // tr,),
        in_specs=[spec] * 4, out_specs=[spec] * 3, out_shape=[_sds((r, c), F32)] * 3,
    )(w, g, m, v)


def _place():
    return lax.axis_index("x"), lax.axis_index("y"), lax.axis_index("c")


def _other_chips(x, y):
    return [(1 - x, y), (x, 1 - y), (1 - x, 1 - y)]


_ANY = pl.BlockSpec(memory_space=pl.ANY)


def cast_place(w, shard_idx, name):
    r, cols = w.shape
    tr = r // 2

    def body(j_ref, w_ref, o_ref):
        o_ref[0] = w_ref[...].astype(BF16)

    return pl.pallas_call(
        body, name=name,
        grid_spec=pltpu.PrefetchScalarGridSpec(
            num_scalar_prefetch=1, grid=(r // tr,),
            in_specs=[pl.BlockSpec((tr, cols), lambda i, j: (i, 0))],
            out_specs=pl.BlockSpec((1, tr, cols), lambda i, j: (j[0], i, 0))),
        out_shape=_sds((N_SHARD, r, cols), BF16),
        compiler_params=pltpu.CompilerParams(dimension_semantics=("arbitrary",),
                                             vmem_limit_bytes=VMEM_LIMIT),
    )(shard_idx, w)


class Exchange:
    def __init__(self, inputs, out_shape, aliases, sems, phases):
        self.inputs, self.out_shape, self.aliases = list(inputs), list(out_shape), dict(aliases)
        self.sems, self.phases = list(sems), list(phases)


def run_exchange(ex, name):
    def body(*refs):
        n_in, n_out = len(ex.inputs), len(ex.out_shape)
        for _, fn in ex.phases:
            fn(refs[:n_in], refs[n_in:n_in + n_out], refs[n_in + n_out:])

    return _call(body, name=name, in_specs=[_ANY] * len(ex.inputs), out_specs=[_ANY] * len(ex.out_shape),
                 out_shape=ex.out_shape, scratch=ex.sems, input_output_aliases=ex.aliases)(*ex.inputs)


def merge_exchanges(exs):
    inputs, out_shape, sems, aliases, phases, out_slices = [], [], [], {}, [], []
    for ex in exs:
        i0, o0, s0 = len(inputs), len(out_shape), len(sems)
        inputs += ex.inputs
        out_shape += ex.out_shape
        sems += ex.sems
        for k, m in ex.aliases.items():
            aliases[i0 + k] = o0 + m
        si, so, ss = slice(i0, len(inputs)), slice(o0, len(out_shape)), slice(s0, len(sems))
        out_slices.append(so)
        for step, fn in ex.phases:
            phases.append((step, lambda ins, outs, sm, fn=fn, si=si, so=so, ss=ss: fn(ins[si], outs[so], sm[ss])))
    return Exchange(inputs, out_shape, aliases, sems, phases), out_slices


def _dma_sems(*sizes):
    return [pltpu.SemaphoreType.DMA((s,)) for s in sizes]


def gather_exchange(bufs, small=None, relay_step=-2):
    n = len(bufs)
    n_small = 0 if small is None else 1

    def half(outs, a, blk, hc):
        rh = bufs[a].shape[1] // 2
        return outs[a].at[blk, pl.ds(hc * rh, rh), :]

    def ici(outs, sems, a, k, blk, to):
        return pltpu.make_async_remote_copy(
            src_ref=half(outs, a, blk, to[2]), dst_ref=half(outs, a, blk, to[2]), send_sem=sems[0].at[3 * a + k],
            recv_sem=sems[1].at[3 * a + k], device_id=to, device_id_type=MESH)

    def d2d(outs, sems, a, k, blk, hc, to):
        return pltpu.make_async_remote_copy(
            src_ref=half(outs, a, blk, hc), dst_ref=half(outs, a, blk, hc), send_sem=sems[2].at[3 * a + k],
            recv_sem=sems[3].at[3 * a + k], device_id=to, device_id_type=MESH)

    def small_copy(ins, outs, sems, k, blk, to):
        return pltpu.make_async_remote_copy(
            src_ref=ins[n], dst_ref=outs[n].at[blk], send_sem=sems[0].at[3 * n + k],
            recv_sem=sems[1].at[3 * n + k], device_id=to, device_id_type=MESH)

    def start(ins, outs, sems):
        x, y, c = _place()
        j = 2 * x + y
        if n_small:
            pltpu.make_async_copy(ins[n], outs[n].at[j], sems[4].at[0]).start()
        for k, (px, py) in enumerate(_other_chips(x, y)):
            if n_small:
                small_copy(ins, outs, sems, k, j, (px, py, c)).start()
            for a in range(n):
                ici(outs, sems, a, k, j, (px, py, c)).start()

    def relay(ins, outs, sems):
        x, y, c = _place()
        for k, (px, py) in enumerate(_other_chips(x, y)):
            for a in range(n):
                ici(outs, sems, a, k, 2 * px + py, (px, py, c)).wait_recv()
                d2d(outs, sems, a, k, 2 * px + py, c, (x, y, 1 - c)).start()

    def finish(ins, outs, sems):
        x, y, c = _place()
        j = 2 * x + y
        for k, (px, py) in enumerate(_other_chips(x, y)):
            blk = 2 * px + py
            if n_small:
                small_copy(ins, outs, sems, k, blk, (px, py, c)).wait_recv()
                small_copy(ins, outs, sems, k, j, (px, py, c)).wait_send()
            for a in range(n):
                d2d(outs, sems, a, k, blk, 1 - c, (x, y, 1 - c)).wait_recv()
                ici(outs, sems, a, k, j, (px, py, c)).wait_send()
                d2d(outs, sems, a, k, blk, c, (x, y, 1 - c)).wait_send()
        if n_small:
            pltpu.make_async_copy(ins[n], outs[n].at[j], sems[4].at[0]).wait()

    out_shape = [_sds(b.shape, b.dtype) for b in bufs]
    if n_small:
        out_shape.append(_sds((N_SHARD,) + small.shape, small.dtype))
    return Exchange(list(bufs) + ([small] if n_small else []), out_shape, {a: a for a in range(n)},
                    _dma_sems(3 * n + 3, 3 * n + 3, 3 * n, 3 * n, 1),
                    [(0, start), (relay_step, relay), (-1, finish)])


def _start_then_wait(copies):
    def start(ins, outs, sems):
        for sent, _ in copies(ins, outs, sems):
            sent().start()

    def finish(ins, outs, sems):
        pairs = copies(ins, outs, sems)
        for _, arrival in pairs:
            arrival().wait_recv()
        for sent, _ in pairs:
            sent().wait_send()

    return [(0, start), (-1, finish)]


def pair_exchange(arrs):
    n = len(arrs)

    def copies(ins, outs, sems):
        x, y, c = _place()
        res = []
        for a in range(n):
            def mk(a=a):
                rh = arrs[a].shape[1] // 2
                return pltpu.make_async_remote_copy(
                    src_ref=ins[a].at[:, pl.ds((1 - c) * rh, rh), :], dst_ref=outs[a], send_sem=sems[0].at[a],
                    recv_sem=sems[1].at[a], device_id=(x, y, 1 - c), device_id_type=MESH)
            res.append((mk, mk))
        return res

    return Exchange(arrs, [_sds((a.shape[0], a.shape[1] // 2, a.shape[2]), a.dtype) for a in arrs], {},
                    _dma_sems(n, n), _start_then_wait(copies))


def pair_add(g, s, c_idx, name):
    nb, r, cols = g.shape
    rh = r // 2

    def body(c_ref, g_ref, s_ref, o_ref):
        o_ref[...] = (g_ref[...] + s_ref[...]).astype(BF16)

    return pl.pallas_call(
        body, name=name,
        grid_spec=pltpu.PrefetchScalarGridSpec(
            num_scalar_prefetch=1, grid=(nb,),
            in_specs=[pl.BlockSpec((1, rh, cols), lambda j, c: (j, c[0], 0)),
                      pl.BlockSpec((1, rh, cols), lambda j, c: (j, 0, 0))],
            out_specs=pl.BlockSpec((1, rh, cols), lambda j, c: (j, 0, 0))),
        out_shape=_sds((nb, rh, cols), BF16),
        compiler_params=pltpu.CompilerParams(dimension_semantics=("arbitrary",),
                                             vmem_limit_bytes=VMEM_LIMIT),
    )(c_idx, g, s)


def chip_exchange(arrs):
    n = len(arrs)

    def copies(ins, outs, sems):
        x, y, c = _place()
        j = 2 * x + y
        res = []
        for a in range(n):
            for k, (px, py) in enumerate(_other_chips(x, y)):
                def mk(src_blk, dst_blk, a=a, k=k, to=(px, py, c)):
                    return pltpu.make_async_remote_copy(
                        src_ref=ins[a].at[src_blk], dst_ref=outs[a].at[dst_blk], send_sem=sems[0].at[3 * a + k],
                        recv_sem=sems[1].at[3 * a + k], device_id=to, device_id_type=MESH)
                res.append((functools.partial(mk, 2 * px + py, j), functools.partial(mk, j, 2 * px + py)))
        return res

    return Exchange(arrs, [_sds(a.shape, a.dtype) for a in arrs], {}, _dma_sems(3 * n, 3 * n),
                    _start_then_wait(copies))


def sum_chips(r, p, shard_idx, name):
    nb, rh, cols = r.shape
    tr = rh

    def body(j_ref, p_ref, *refs):
        o_ref = refs[nb]
        j = j_ref[0]
        acc = None
        for i in range(nb):
            term = jnp.where(j == i, p_ref[0], refs[i][0]).astype(F32)
            acc = term if acc is None else acc + term
        o_ref[...] = acc

    def slot(i):
        return pl.BlockSpec((1, tr, cols), lambda t, j: (jnp.where(j[0] == i, (i + 1) % nb, i), t, 0))

    return pl.pallas_call(
        body, name=name,
        grid_spec=pltpu.PrefetchScalarGridSpec(
            num_scalar_prefetch=1, grid=(rh // tr,),
            in_specs=[pl.BlockSpec((1, tr, cols), lambda t, j: (j[0], t, 0))] + [slot(i) for i in range(nb)],
            out_specs=pl.BlockSpec((tr, cols), lambda t, j: (t, 0))),
        out_shape=_sds((rh, cols), F32),
        compiler_params=pltpu.CompilerParams(dimension_semantics=("arbitrary",),
                                             vmem_limit_bytes=VMEM_LIMIT),
    )(shard_idx, p, *([r] * nb))


def pair_swap(arrs):
    n = len(arrs)

    def copies(ins, outs, sems):
        x, y, c = _place()
        res = []
        for a in range(n):
            def mk(a=a):
                return pltpu.make_async_remote_copy(
                    src_ref=ins[a], dst_ref=outs[a], send_sem=sems[0].at[a], recv_sem=sems[1].at[a],
                    device_id=(x, y, 1 - c), device_id_type=MESH)
            res.append((mk, mk))
        return res

    return Exchange(arrs, [_sds(a.shape, a.dtype) for a in arrs], {}, _dma_sems(n, n),
                    _start_then_wait(copies))


def adamw_pair(w, g_mine, g_sib, m, v, c_idx, name):
    r, cols = w.shape
    rh = r // 2
    tr = rh
    nh = rh // tr

    def body(c_ref, w_ref, gm_ref, gs_ref, m_ref, v_ref, g_ref, d_ref, mo_ref, vo_ref):
        mine = (pl.program_id(0) // nh) == c_ref[0]
        g = jnp.where(mine, gm_ref[...], gs_ref[...])
        d, mn, vn = _adamw_math(w_ref[...], g, m_ref[...], v_ref[...])
        g_ref[...] = g
        d_ref[...] = d
        mo_ref[...] = mn
        vo_ref[...] = vn

    full = pl.BlockSpec((tr, cols), lambda i, c: (i, 0))
    part = pl.BlockSpec((tr, cols), lambda i, c: (i % nh, 0))
    return pl.pallas_call(
        body, name=name,
        grid_spec=pltpu.PrefetchScalarGridSpec(
            num_scalar_prefetch=1, grid=(r // tr,),
            in_specs=[full, part, part, full, full], out_specs=[full] * 4),
        out_shape=[_sds((r, cols), F32)] * 4,
        compiler_params=pltpu.CompilerParams(dimension_semantics=("arbitrary",),
                                             vmem_limit_bytes=VMEM_LIMIT),
    )(c_idx, w, g_mine, g_sib, m, v)


N_DEV = 8


def device_gather(pack):
    def copies(ins, outs, sems):
        x, y, c = _place()
        me = 4 * x + 2 * y + c
        res = []
        for k in range(1, N_DEV):
            fx, fy, fc = (k >> 2) & 1, (k >> 1) & 1, k & 1
            px, py, pc = (1 - x if fx else x, 1 - y if fy else y, 1 - c if fc else c)

            def mk(slot, k=k, to=(px, py, pc)):
                return pltpu.make_async_remote_copy(
                    src_ref=ins[0], dst_ref=outs[0].at[slot], send_sem=sems[0].at[k - 1],
                    recv_sem=sems[1].at[k - 1], device_id=to, device_id_type=MESH)
            res.append((functools.partial(mk, me), functools.partial(mk, 4 * px + 2 * py + pc)))
        return res

    return Exchange([pack], [_sds((N_DEV,) + pack.shape, pack.dtype)], {}, _dma_sems(N_DEV - 1, N_DEV - 1),
                    _start_then_wait(copies))


def sum_devices(buf, pack, me_idx, name):
    r, cols = pack.shape

    def body(me_ref, p_ref, *refs):
        o_ref = refs[N_DEV]
        acc = None
        for i in range(N_DEV):
            term = jnp.where(me_ref[0] == i, p_ref[...], refs[i][0])
            acc = term if acc is None else acc + term
        o_ref[...] = acc

    def slot(i):
        return pl.BlockSpec((1, r, cols), lambda t, me: (jnp.where(me[0] == i, (i + 1) % N_DEV, i), 0, 0))

    whole = pl.BlockSpec((r, cols), lambda t, me: (0, 0))
    return pl.pallas_call(
        body, name=name,
        grid_spec=pltpu.PrefetchScalarGridSpec(
            num_scalar_prefetch=1, grid=(1,),
            in_specs=[whole] + [slot(i) for i in range(N_DEV)], out_specs=whole),
        out_shape=_sds((r, cols), F32),
        compiler_params=pltpu.CompilerParams(dimension_semantics=("arbitrary",),
                                             vmem_limit_bytes=VMEM_LIMIT),
    )(me_idx, pack, *([buf] * N_DEV))


SMALL_NAMES = ("ffn1_norm", "mix_norm", "ffn2_norm", "final_norm", "sg_ln_g", "sg_ln_b",
               "dn_norm", "a_log", "dt_bias", "sg_b", "sg_w", "conv_w")


def _to_rows(a):
    flat = a.reshape(-1)
    pad = (-flat.shape[0]) % LANES
    if pad:
        flat = jnp.pad(flat, (0, pad))
    return flat.reshape(-1, LANES)


def _pack_small(parts):
    rows = [_to_rows(parts[k]) for k in SMALL_NAMES]
    pack = jnp.concatenate(rows, axis=0)
    pad = (-pack.shape[0]) % 8
    if pad:
        pack = jnp.pad(pack, ((0, pad), (0, 0)))
    return pack


def _unpack_small(pack, shapes):
    out, r0 = {}, 0
    for k in SMALL_NAMES:
        size = 1
        for s in shapes[k]:
            size *= s
        nrows = -(-size // LANES)
        out[k] = pack[r0:r0 + nrows].reshape(-1)[:size].reshape(shapes[k])
        r0 += nrows
    return out


def kernel(x, ffn1_norm, ffn1_w_gate, ffn1_w_up, ffn1_w_down, mix_norm, w_in, conv_w, a_log, dt_bias, dn_norm, sg_ln_g, sg_ln_b, sg_w, sg_b, w_out, ffn2_norm, ffn2_w_gate, ffn2_w_up, ffn2_w_down, final_norm, loss_target, m_ffn1_norm, m_ffn1_w_gate, m_ffn1_w_up, m_ffn1_w_down, m_mix_norm, m_w_in, m_conv_w, m_a_log, m_dt_bias, m_dn_norm, m_sg_ln_g, m_sg_ln_b, m_sg_w, m_sg_b, m_w_out, m_ffn2_norm, m_ffn2_w_gate, m_ffn2_w_up, m_ffn2_w_down, m_final_norm, v_ffn1_norm, v_ffn1_w_gate, v_ffn1_w_up, v_ffn1_w_down, v_mix_norm, v_w_in, v_conv_w, v_a_log, v_dt_bias, v_dn_norm, v_sg_ln_g, v_sg_ln_b, v_sg_w, v_sg_b, v_w_out, v_ffn2_norm, v_ffn2_w_gate, v_ffn2_w_up, v_ffn2_w_down, v_final_norm):
    bsz, t_len, d = x.shape
    n = bsz * t_len
    xy, yy, cc = _place()
    shard = 2 * xy + yy

    big_names = ["ffn1_w_gate", "ffn1_w_up", "ffn1_w_down", "w_in", "w_out",
                 "ffn2_w_gate", "ffn2_w_up", "ffn2_w_down"]
    big_w = dict(ffn1_w_gate=ffn1_w_gate, ffn1_w_up=ffn1_w_up, ffn1_w_down=ffn1_w_down, w_in=w_in,
                 w_out=w_out, ffn2_w_gate=ffn2_w_gate, ffn2_w_up=ffn2_w_up, ffn2_w_down=ffn2_w_down)
    big_m = dict(ffn1_w_gate=m_ffn1_w_gate, ffn1_w_up=m_ffn1_w_up, ffn1_w_down=m_ffn1_w_down, w_in=m_w_in,
                 w_out=m_w_out, ffn2_w_gate=m_ffn2_w_gate, ffn2_w_up=m_ffn2_w_up, ffn2_w_down=m_ffn2_w_down)
    big_v = dict(ffn1_w_gate=v_ffn1_w_gate, ffn1_w_up=v_ffn1_w_up, ffn1_w_down=v_ffn1_w_down, w_in=v_w_in,
                 w_out=v_w_out, ffn2_w_gate=v_ffn2_w_gate, ffn2_w_up=v_ffn2_w_up, ffn2_w_down=v_ffn2_w_down)
    shard_idx = jnp.reshape(shard, (1,)).astype(jnp.int32)
    c_idx = jnp.reshape(cc, (1,)).astype(jnp.int32)
    transposed = ("ffn1_w_gate", "ffn1_w_up", "ffn2_w_gate", "ffn2_w_up")

    def as2d(a, k):
        return a[0].T if k in transposed else a[0]

    def from2d(a, k):
        return a.T[None] if k in transposed else a[None]

    placed = {k: cast_place(as2d(big_w[k], k), shard_idx, name="cast_" + k) for k in big_names}
    first_names = big_names[:3]
    later_names = big_names[3:]
    res = run_exchange(gather_exchange([placed[k] for k in first_names], conv_w[0]), name="gather_first")
    gw = dict(zip(first_names, res[:3]))
    conv_full = res[3].transpose(1, 0, 2).reshape(CONV_K, 3 * HALF_W)

    x0 = x.reshape(n, d)
    def ffn_weights(prefix):
        return [gw[prefix + k].reshape(-1, d) for k in ("_w_gate", "_w_up", "_w_down")]

    def ffn_grad_blocks(grads):
        return [g.reshape(N_SHARD, -1, d) for g in grads]

    (x1, h1, gate1, up1, act1), later = ffn_fwd(
        x0, ffn1_norm, *ffn_weights("ffn1"), name="ffn1_fwd",
        comm=gather_exchange([placed[k] for k in later_names]))
    gw.update(zip(later_names, later))
    w_in_full = gw["w_in"].transpose(1, 0, 2).reshape(d, IN_COLS)
    w_in_full = jnp.pad(w_in_full, ((0, 0), (0, PROJ_W - IN_COLS)))
    w_out_full = gw["w_out"].reshape(2 * HALF_W, d)
    proj, h2 = in_proj_fwd(x1, mix_norm, w_in_full, name="in_proj_fwd")
    proj3 = proj.reshape(bsz, t_len, PROJ_W)
    bias_tile = jnp.repeat(sg_b[0].T, SG_GROUP_DIM, axis=1)
    sg_out = sg_fwd(proj, sg_ln_g, sg_ln_b, sg_w[0], bias_tile, name="sg_fwd")
    qkv = dn_conv_fwd(proj3, conv_full, name="dn_conv_fwd")
    alog_row = jnp.zeros((1, LANES), F32).at[0, N_HEADS:2 * N_HEADS].set(a_log[0])
    dtb_row = jnp.zeros((1, LANES), F32).at[0, N_HEADS:2 * N_HEADS].set(dt_bias[0])
    u_wy, w_wy, q_dec, k_dec, qk, tinv, gc = dn_chunk_fwd(qkv, proj3, alog_row, dtb_row,
                                                           name="dn_chunk_fwd")
    o, s_in = dn_scan_fwd(u_wy, w_wy, q_dec, k_dec, qk, gc, name="dn_scan_fwd")
    dn_out = dn_out_fwd(o.reshape(n, HALF_W), proj, dn_norm, name="dn_out_fwd")
    x2 = out_proj_fwd(x1, sg_out, dn_out, w_out_full, name="out_proj_fwd")
    x3, h3, gate2, up2, act2 = ffn_fwd(x2, ffn2_norm, *ffn_weights("ffn2"), name="ffn2_fwd")
    dx3, d_final_norm, loss_tile = final_loss(x3, final_norm.reshape(1, d),
                                              loss_target.reshape(n, d), name="final_loss")
    loss = lax.psum(loss_tile[0, 0], ("x", "y", "c"))

    dx2, dgate2, dup2, dyh2, d_ffn2_norm = ffn_bwd_act(
        dx3, x2, ffn2_norm, gate2, up2, *ffn_weights("ffn2"), name="ffn2_bwd_act")
    g_big = {}
    g_big["ffn2_w_gate"], g_big["ffn2_w_up"], g_big["ffn2_w_down"] = ffn_grad_blocks(ffn_bwd_w(
        h3, dyh2, dgate2, dup2, act2, name="ffn2_bwd_w"))

    early = ["ffn2_w_gate", "ffn2_w_up", "ffn2_w_down"]
    (d_sg, d_dn, dx2b), early_sib = out_proj_bwd_x(dx2, w_out_full, name="out_proj_bwd_x",
                                                   comm=pair_exchange([g_big[k] for k in early]))
    early_sums = [pair_add(g_big[k], s, c_idx, name="grad_pair_add_" + k) for k, s in zip(early, early_sib)]
    g_w_out = jnp.concatenate([matmul_tn(sg_out, dx2b, d, name="w_out_grad_sg"),
                               matmul_tn(dn_out, dx2b, d, name="w_out_grad_dn")], axis=0)
    g_big["w_out"] = g_w_out.reshape(N_SHARD, (2 * HALF_W) // N_SHARD, d)

    d_proj, d_sg_w, d_bias_tile, d_ln_g, d_ln_b = sg_bwd(d_sg, proj, sg_ln_g, sg_ln_b, sg_w[0],
                                                         bias_tile, name="sg_bwd")
    d_o, d_proj, d_dn_norm = dn_out_bwd(d_dn, o.reshape(n, HALF_W), proj, dn_norm, d_proj,
                                        name="dn_out_bwd")
    du, dw, dqd, dkd, dqk, dgc_scan = dn_scan_bwd(d_o.reshape(bsz, t_len, HALF_W), u_wy, w_wy, q_dec,
                                                  k_dec, qk, gc, s_in, name="dn_scan_bwd")
    (d_qkv, d_proj3, d_alog_row, d_dtb_row), early_chips = dn_chunk_bwd(
        qkv, proj3, alog_row, dtb_row, tinv, u_wy, w_wy, du, dw, dqd, dkd, dqk, dgc_scan,
        d_proj.reshape(bsz, t_len, PROJ_W), name="dn_chunk_bwd", comm=chip_exchange(early_sums))
    early_halves = [sum_chips(r, p, shard_idx, name="grad_chip_sum_" + k)
                    for k, r, p in zip(early, early_chips, early_sums)]
    (d_proj3, d_conv), early_sib_halves = dn_conv_bwd(d_qkv, proj3, conv_full, d_proj3, name="dn_conv_bwd",
                                                      comm=pair_swap(early_halves))
    d_proj = d_proj3.reshape(n, PROJ_W)
    g_w_in = matmul_tn(h2, d_proj, 640, name="w_in_grad")[:, :IN_COLS]
    g_big["w_in"] = g_w_in.reshape(d, N_SHARD, IN_COLS // N_SHARD).transpose(1, 0, 2)

    mid = ["w_in", "w_out"]
    (dx1, d_mix_norm), mid_sib = in_proj_bwd_x(d_proj, w_in_full, x1, mix_norm, dx2, name="in_proj_bwd_x",
                                               comm=pair_exchange([g_big[k] for k in mid]))
    mid_sums = [pair_add(g_big[k], s, c_idx, name="grad_pair_add_" + k) for k, s in zip(mid, mid_sib)]
    dx0, dgate1, dup1, dyh1, d_ffn1_norm = ffn_bwd_act(
        dx1, x0, ffn1_norm, gate1, up1, *ffn_weights("ffn1"), name="ffn1_bwd_act")
    d_sg_b = d_bias_tile.reshape(SG_CHUNK, SG_GROUPS, SG_GROUP_DIM).sum(axis=-1).T
    small_g = dict(ffn1_norm=d_ffn1_norm, mix_norm=d_mix_norm, ffn2_norm=d_ffn2_norm,
                   final_norm=d_final_norm, sg_ln_g=d_ln_g, sg_ln_b=d_ln_b, dn_norm=d_dn_norm,
                   a_log=d_alog_row[:, N_HEADS:2 * N_HEADS], dt_bias=d_dtb_row[:, N_HEADS:2 * N_HEADS],
                   sg_b=d_sg_b, sg_w=d_sg_w, conv_w=d_conv)
    my_pack = _pack_small(small_g)
    hosted, parts = merge_exchanges([chip_exchange(mid_sums), device_gather(my_pack)])
    f1_grads, hosted_res = ffn_bwd_w(h1, dyh1, dgate1, dup1, act1, name="ffn1_bwd_w", comm=hosted)
    g_big["ffn1_w_gate"], g_big["ffn1_w_up"], g_big["ffn1_w_down"] = ffn_grad_blocks(f1_grads)
    mid_chips, (all_packs,) = hosted_res[parts[0]], hosted_res[parts[1]]
    mid_halves = [sum_chips(r, p, shard_idx, name="grad_chip_sum_" + k)
                  for k, r, p in zip(mid, mid_chips, mid_sums)]
    grad_x = dx0.reshape(bsz, t_len, d)

    late = [k for k in big_names if k not in early and k not in mid]
    g_list = [g_big[k] for k in late]
    from_sibling = run_exchange(pair_exchange(g_list), name="grad_pair_exchange")
    pair_sums = [pair_add(g, s, c_idx, name="grad_pair_add_" + k)
                 for k, g, s in zip(late, g_list, from_sibling)]
    from_chips = run_exchange(chip_exchange(pair_sums), name="grad_chip_exchange")
    halves = [sum_chips(r, p, shard_idx, name="grad_chip_sum_" + k)
              for k, r, p in zip(late, from_chips, pair_sums)]
    sib_halves = run_exchange(pair_swap(mid_halves + halves), name="grad_pair_swap")
    outs = {}
    for k, g_mine, g_sib in zip(early + mid + late, early_halves + mid_halves + halves,
                                list(early_sib_halves) + list(sib_halves)):
        res = adamw_pair(as2d(big_w[k], k), g_mine, g_sib, as2d(big_m[k], k), as2d(big_v[k], k), c_idx,
                         name="adamw_" + k)
        outs[k] = tuple(from2d(a, k) for a in res)

    small_w = dict(ffn1_norm=ffn1_norm, mix_norm=mix_norm, ffn2_norm=ffn2_norm, final_norm=final_norm,
                   sg_ln_g=sg_ln_g, sg_ln_b=sg_ln_b, dn_norm=dn_norm, a_log=a_log, dt_bias=dt_bias,
                   sg_b=sg_b, sg_w=sg_w)
    small_m = dict(ffn1_norm=m_ffn1_norm, mix_norm=m_mix_norm, ffn2_norm=m_ffn2_norm,
                   final_norm=m_final_norm, sg_ln_g=m_sg_ln_g, sg_ln_b=m_sg_ln_b, dn_norm=m_dn_norm,
                   a_log=m_a_log, dt_bias=m_dt_bias, sg_b=m_sg_b, sg_w=m_sg_w)
    small_v = dict(ffn1_norm=v_ffn1_norm, mix_norm=v_mix_norm, ffn2_norm=v_ffn2_norm,
                   final_norm=v_final_norm, sg_ln_g=v_sg_ln_g, sg_ln_b=v_sg_ln_b, dn_norm=v_dn_norm,
                   a_log=v_a_log, dt_bias=v_dt_bias, sg_b=v_sg_b, sg_w=v_sg_w)
    shapes = {k: small_w[k].shape for k in small_w}
    shapes["conv_w"] = (CONV_K, 3 * HALF_W)
    me_idx = jnp.reshape(4 * xy + 2 * yy + cc, (1,)).astype(jnp.int32)
    g_pack = sum_devices(all_packs, my_pack, me_idx, name="small_sum")
    g_small = _unpack_small(g_pack, shapes)
    cw = 3 * HALF_W // N_SHARD
    g_conv = lax.dynamic_slice_in_dim(g_small["conv_w"], shard * cw, cw, axis=1)
    zero_conv = jnp.zeros((CONV_K, 3 * HALF_W), F32)

    def packed(src, conv):
        parts = dict(src)
        parts["conv_w"] = lax.dynamic_update_slice_in_dim(zero_conv, conv[0], shard * cw, axis=1)
        return _pack_small(parts)

    d_pack, m_pack, v_pack = adamw(packed(small_w, conv_w), g_pack, packed(small_m, m_conv_w),
                                   packed(small_v, v_conv_w), name="adamw_small")
    d_small = _unpack_small(d_pack, shapes)
    m_small = _unpack_small(m_pack, shapes)
    v_small = _unpack_small(v_pack, shapes)

    def conv_block(full_arr):
        return lax.dynamic_slice_in_dim(full_arr, shard * cw, cw, axis=1)[None]

    for k in small_w:
        outs[k] = (g_small[k].reshape(small_w[k].shape), d_small[k], m_small[k], v_small[k])
    outs["conv_w"] = (g_conv[None], conv_block(d_small["conv_w"]), conv_block(m_small["conv_w"]),
                      conv_block(v_small["conv_w"]))

    order = ["ffn1_norm", "ffn1_w_gate", "ffn1_w_up", "ffn1_w_down", "mix_norm", "w_in", "conv_w",
             "a_log", "dt_bias", "dn_norm", "sg_ln_g", "sg_ln_b", "sg_w", "sg_b", "w_out", "ffn2_norm",
             "ffn2_w_gate", "ffn2_w_up", "ffn2_w_down", "final_norm"]
    return (loss, grad_x, *[outs[k][0] for k in order], *[outs[k][1] for k in order],
            *[outs[k][2] for k in order], *[outs[k][3] for k in order])
```

```python
import functools

import jax
import jax.numpy as jnp
from jax import lax
from jax.experimental import pallas as pl
from jax.experimental.pallas import tpu as pltpu

F32 = jnp.float32
BF16 = jnp.bfloat16
EPS = 1e-6

D_MODEL = 1024
N_SHARD = 4
HEAD_DIM = 128
N_HEADS = 4
DN_CHUNK = 64
SG_CHUNK = 128
SG_GROUPS = 8
SG_GROUP_DIM = 64
HALF_W = 512
PROJ_W = 3200
IN_COLS = 3080
GATE_COL_BLOCK = 24
QK_SCALE = HEAD_DIM ** -0.5
LANES = 128

ADAM_LR = 0.001
ADAM_B1 = 0.9
ADAM_B2 = 0.999
ADAM_EPS = 1e-08
ADAM_WD = 0.01
ADAM_STEP = 10

VMEM_LIMIT = 56 * 1024 * 1024
ROW_TILE = 512

NN = ((1,), (0,))
NT = ((1,), (1,))
TN = ((0,), (0,))
MESH = pl.DeviceIdType.MESH


def _dot(a, b, dims):
    return lax.dot_general(a, b, (dims, ((), ())), preferred_element_type=F32)


def _bdot(a, b, dims):
    return _dot(a.astype(BF16), b.astype(BF16), dims)


def _split(a):
    hi = a.astype(BF16)
    lo = (a - hi.astype(F32)).astype(BF16)
    return hi, lo


def _dot3(a, b, dims=NN):
    return _dot(a[0], b[0], dims) + (_dot(a[0], b[1], dims) + _dot(a[1], b[0], dims))


def _dot_exact_lhs(a, b):
    ab = a.astype(BF16)
    b1 = b.astype(BF16)
    r1 = b - b1.astype(F32)
    b2 = r1.astype(BF16)
    b3 = (r1 - b2.astype(F32)).astype(BF16)
    return _dot(ab, b1, NN) + (_dot(ab, b2, NN) + _dot(ab, b3, NN))


def _call(body, *, name, out_shape, in_specs, out_specs, grid=(), scratch=(), comm=None, **kw):
    params = dict(vmem_limit_bytes=VMEM_LIMIT)
    if grid:
        params["dimension_semantics"] = ("arbitrary",) * len(grid)
    if comm is None:
        return pl.pallas_call(
            body, name=name, grid=grid, in_specs=in_specs, out_specs=out_specs,
            out_shape=out_shape, scratch_shapes=list(scratch),
            compiler_params=pltpu.CompilerParams(**params), **kw)

    n_in, n_out, n_sc = len(in_specs), len(out_specs), len(scratch)
    c_in, c_out = len(comm.inputs), len(comm.out_shape)
    steps = 1
    for g in grid:
        steps *= g

    def hosted(*refs):
        ins, cins = refs[:n_in], refs[n_in:n_in + c_in]
        o0 = n_in + c_in
        outs, couts = refs[o0:o0 + n_out], refs[o0 + n_out:o0 + n_out + c_out]
        s0 = o0 + n_out + c_out
        sc, csems = refs[s0:s0 + n_sc], refs[s0 + n_sc:]
        lin = 0
        for axis, g in enumerate(grid):
            lin = lin * g + pl.program_id(axis)

        def at(step, fn):
            @pl.when(lin == step % steps)
            def _():
                fn(cins, couts, csems)

        for step, fn in comm.phases:
            if step >= 0:
                at(step, fn)
        body(*ins, *outs, *sc)
        for step, fn in comm.phases:
            if step < 0:
                at(step, fn)

    aliases = dict(kw.pop("input_output_aliases", {}))
    for k, m in comm.aliases.items():
        aliases[n_in + k] = n_out + m
    call = pl.pallas_call(
        hosted, name=name, grid=grid, in_specs=list(in_specs) + [_ANY] * c_in,
        out_specs=list(out_specs) + [_ANY] * c_out, out_shape=list(out_shape) + comm.out_shape,
        scratch_shapes=list(scratch) + comm.sems, input_output_aliases=aliases,
        compiler_params=pltpu.CompilerParams(**params), **kw)

    def run(*args):
        res = call(*args, *comm.inputs)
        return res[:n_out], res[n_out:]

    return run


def _sds(shape, dtype):
    return jax.ShapeDtypeStruct(tuple(shape), dtype)


def _resident(shape):
    zeros = (0,) * len(shape)
    return pl.BlockSpec(tuple(shape), lambda *_: zeros, pipeline_mode=pl.Buffered(1))


def _sigmoid(x):
    return jax.nn.sigmoid(x)


def _softplus(x):
    return jnp.maximum(x, 0.0) + jnp.log(1.0 + jnp.exp(-jnp.abs(x)))


_GELU_C = 0.7978845608028654
_GELU_A = 0.044715


def _gelu(x):
    t = jnp.tanh(_GELU_C * (x + _GELU_A * x * x * x))
    return 0.5 * x * (1.0 + t)


def _gelu_grad(x):
    t = jnp.tanh(_GELU_C * (x + _GELU_A * x * x * x))
    return 0.5 * (1.0 + t) + 0.5 * x * (1.0 - t * t) * _GELU_C * (1.0 + 3.0 * _GELU_A * x * x)


def _silu_grad(x):
    s = _sigmoid(x)
    return s * (1.0 + x * (1.0 - s))


def _rms_scale(xv):
    return lax.rsqrt(jnp.mean(xv * xv, axis=-1, keepdims=True) + EPS)


def _rms_bwd(dh, xv, g):
    r = _rms_scale(xv)
    xn = xv * r
    dg = jnp.sum(dh * xn, axis=0, keepdims=True)
    dxn = dh * g
    dx = r * (dxn - xn * jnp.mean(dxn * xn, axis=-1, keepdims=True))
    return dx, dg


def _iota2(shape, dim):
    return lax.broadcasted_iota(jnp.int32, shape, dim)


def _col_to_row(col):
    n = col.shape[0]
    eye = _iota2((n, n), 0) == _iota2((n, n), 1)
    return jnp.sum(jnp.where(eye, col, 0.0), axis=0, keepdims=True)


def _row_to_col(row):
    n = row.shape[1]
    eye = _iota2((n, n), 0) == _iota2((n, n), 1)
    return jnp.sum(jnp.where(eye, row, 0.0), axis=1, keepdims=True)


MXU_DIM = 256


def _hidden_chunks(f, step=3 * MXU_DIM):
    return [(c0, min(c0 + step, f)) for c0 in range(0, f, step)]

def ffn_fwd(x, gnorm, wg, wu, wd, name, comm=None):
    n, d = x.shape
    f = wg.shape[0]
    tm = min(ROW_TILE, n)

    def body(x_ref, g_ref, wg_ref, wu_ref, wd_ref, xo_ref, h_ref, gate_ref, up_ref, act_ref, acc_ref):
        xv = x_ref[...]
        h = (xv * _rms_scale(xv) * g_ref[...]).astype(BF16)
        h_ref[...] = h
        for c0, c1 in _hidden_chunks(f):
            gate = _dot(h, wg_ref[c0:c1, :], NT)
            up = _dot(h, wu_ref[c0:c1, :], NT)
            act = (gate * _sigmoid(gate) * up).astype(BF16)
            gate_ref[:, c0:c1] = gate.astype(BF16)
            up_ref[:, c0:c1] = up.astype(BF16)
            act_ref[:, c0:c1] = act
            part = _dot(act, wd_ref[c0:c1, :], NN)
            if c0 == 0:
                acc_ref[...] = part
            else:
                acc_ref[...] += part
        xo_ref[...] = xv + 0.5 * acc_ref[...]

    row = pl.BlockSpec((tm, d), lambda i: (i, 0))
    wide = pl.BlockSpec((tm, f), lambda i: (i, 0))
    return _call(
        body, name=name, grid=(n // tm,),
        in_specs=[row, pl.BlockSpec((1, d), lambda i: (0, 0))] + [_resident((f, d))] * 3,
        out_specs=[row, row, wide, wide, wide],
        out_shape=[_sds((n, d), F32), _sds((n, d), BF16)] + [_sds((n, f), BF16)] * 3,
        scratch=[pltpu.VMEM((tm, d), F32)], comm=comm,
    )(x, gnorm, wg, wu, wd)


def ffn_bwd_act(dy, x, gnorm, gate, up, wg, wu, wd, name, comm=None):
    n, d = x.shape
    f = wg.shape[0]
    tm = min(ROW_TILE // 2, n)

    def body(dy_ref, x_ref, g_ref, gate_ref, up_ref, wg_ref, wu_ref, wd_ref,
             dx_ref, dgate_ref, dup_ref, dyh_ref, dg_ref, acc_ref):
        @pl.when(pl.program_id(0) == 0)
        def _():
            dg_ref[...] = jnp.zeros_like(dg_ref)

        dyh = (0.5 * dy_ref[...]).astype(BF16)
        dyh_ref[...] = dyh
        for c0, c1 in _hidden_chunks(f):
            dact = _dot(dyh, wd_ref[c0:c1, :], NT)
            gt = gate_ref[:, c0:c1].astype(F32)
            u = up_ref[:, c0:c1].astype(F32)
            s = _sigmoid(gt)
            dup = (dact * (gt * s)).astype(BF16)
            dgate = (dact * u * (s * (1.0 + gt * (1.0 - s)))).astype(BF16)
            dup_ref[:, c0:c1] = dup
            dgate_ref[:, c0:c1] = dgate
            part = _dot(dgate, wg_ref[c0:c1, :], NN) + _dot(dup, wu_ref[c0:c1, :], NN)
            if c0 == 0:
                acc_ref[...] = part
            else:
                acc_ref[...] += part
        dxn, dg = _rms_bwd(acc_ref[...], x_ref[...], g_ref[...])
        dx_ref[...] = dy_ref[...] + dxn
        dg_ref[...] += dg

    row = pl.BlockSpec((tm, d), lambda i: (i, 0))
    wide = pl.BlockSpec((tm, f), lambda i: (i, 0))
    vec = pl.BlockSpec((1, d), lambda i: (0, 0))
    wres = _resident((f, d))
    return _call(
        body, name=name, grid=(n // tm,),
        in_specs=[row, row, vec, wide, wide, wres, wres, wres],
        out_specs=[row, wide, wide, row, vec],
        out_shape=[_sds((n, d), F32), _sds((n, f), BF16), _sds((n, f), BF16),
                   _sds((n, d), BF16), _sds((1, d), F32)],
        scratch=[pltpu.VMEM((tm, d), F32)], comm=comm,
    )(dy, x, gnorm, gate, up, wg, wu, wd)


def ffn_bwd_w(h, dyh, dgate, dup, act, name, comm=None):
    n, d = h.shape
    f = dgate.shape[1]
    fh = f // 2
    tk = min(ROW_TILE, n)

    def body(h_ref, dyh_ref, dgate_ref, dup_ref, act_ref, dwg_ref, dwu_ref, dwd_ref):
        @pl.when(pl.program_id(1) == 0)
        def _():
            dwg_ref[...] = jnp.zeros_like(dwg_ref)
            dwu_ref[...] = jnp.zeros_like(dwu_ref)
            dwd_ref[...] = jnp.zeros_like(dwd_ref)

        hv = h_ref[...]
        dyv = dyh_ref[...]
        for c0, c1 in _hidden_chunks(fh, 2 * MXU_DIM):
            dwg_ref[c0:c1, :] += _dot(dgate_ref[:, c0:c1], hv, TN)
            dwu_ref[c0:c1, :] += _dot(dup_ref[:, c0:c1], hv, TN)
            dwd_ref[c0:c1, :] += _dot(act_ref[:, c0:c1], dyv, TN)

    row = pl.BlockSpec((tk, d), lambda j, k: (k, 0))
    blk = pl.BlockSpec((tk, fh), lambda j, k: (k, j))
    return _call(
        body, name=name, grid=(2, n // tk),
        in_specs=[row, row, blk, blk, blk],
        out_specs=[pl.BlockSpec((fh, d), lambda j, k: (j, 0))] * 3,
        out_shape=[_sds((f, d), F32)] * 3, comm=comm,
    )(h, dyh, dgate, dup, act)


def final_loss(x, gnorm, target, name):
    n, d = x.shape
    tm = min(ROW_TILE, n)

    def body(x_ref, g_ref, t_ref, dx_ref, dg_ref, loss_ref):
        @pl.when(pl.program_id(0) == 0)
        def _():
            dg_ref[...] = jnp.zeros_like(dg_ref)
            loss_ref[...] = jnp.zeros_like(loss_ref)

        xv = x_ref[...]
        y = xv * _rms_scale(xv) * g_ref[...]
        err = y - t_ref[...]
        part = 0.5 * jnp.sum(jnp.mean(err * err, axis=-1, keepdims=True), axis=0, keepdims=True)
        loss_ref[...] += jnp.broadcast_to(part, loss_ref.shape)
        dx, dg = _rms_bwd(err * (1.0 / d), xv, g_ref[...])
        dx_ref[...] = dx
        dg_ref[...] += dg

    row = pl.BlockSpec((tm, d), lambda i: (i, 0))
    vec = pl.BlockSpec((1, d), lambda i: (0, 0))
    return _call(
        body, name=name, grid=(n // tm,),
        in_specs=[row, vec, row],
        out_specs=[row, vec, pl.BlockSpec((1, LANES), lambda i: (0, 0))],
        out_shape=[_sds((n, d), F32), _sds((1, d), F32), _sds((1, LANES), F32)],
    )(x, gnorm, target)


def in_proj_fwd(x, gnorm, w, name):
    n, d = x.shape
    cols = w.shape[1]
    tm = min(ROW_TILE, n)
    tn = 640

    def body(x_ref, g_ref, w_ref, p_ref, h_ref):
        xv = x_ref[...]
        h = (xv * _rms_scale(xv) * g_ref[...]).astype(BF16)
        h_ref[...] = h
        for c0 in range(0, cols, tn):
            p_ref[:, c0:c0 + tn] = _dot(h, w_ref[:, c0:c0 + tn], NN)

    return _call(
        body, name=name, grid=(n // tm,),
        in_specs=[pl.BlockSpec((tm, d), lambda i: (i, 0)),
                  pl.BlockSpec((1, d), lambda i: (0, 0)), _resident((d, cols))],
        out_specs=[pl.BlockSpec((tm, cols), lambda i: (i, 0)),
                   pl.BlockSpec((tm, d), lambda i: (i, 0))],
        out_shape=[_sds((n, cols), F32), _sds((n, d), BF16)],
    )(x, gnorm, w)


def in_proj_bwd_x(dproj, w, x, gnorm, dres, name, comm=None):
    n, d = x.shape
    cols = w.shape[1]
    tm = min(ROW_TILE, n)

    def body(dp_ref, w_ref, x_ref, g_ref, dr_ref, dx_ref, dg_ref):
        @pl.when(pl.program_id(0) == 0)
        def _():
            dg_ref[...] = jnp.zeros_like(dg_ref)

        dh = _dot(dp_ref[...], w_ref[...], NT)
        dxn, dg = _rms_bwd(dh, x_ref[...], g_ref[...])
        dx_ref[...] = dr_ref[...] + dxn
        dg_ref[...] += dg

    row = pl.BlockSpec((tm, d), lambda i: (i, 0))
    vec = pl.BlockSpec((1, d), lambda i: (0, 0))
    return _call(
        body, name=name, grid=(n // tm,),
        in_specs=[pl.BlockSpec((tm, cols), lambda i: (i, 0)),
                  _resident((d, cols)), row, vec, row],
        out_specs=[row, vec],
        out_shape=[_sds((n, d), F32), _sds((1, d), F32)], comm=comm,
    )(dproj, w, x, gnorm, dres)


def matmul_tn(a, b, tn, name):
    n, ka = a.shape
    cb = b.shape[1]
    tk = min(ROW_TILE, n)

    def body(a_ref, b_ref, o_ref):
        @pl.when(pl.program_id(0) == 0)
        def _():
            o_ref[...] = jnp.zeros_like(o_ref)

        av = a_ref[...]
        for c0 in range(0, cb, tn):
            o_ref[:, c0:c0 + tn] += _dot(av, b_ref[:, c0:c0 + tn], TN)

    return _call(
        body, name=name, grid=(n // tk,),
        in_specs=[pl.BlockSpec((tk, ka), lambda k: (k, 0)),
                  pl.BlockSpec((tk, cb), lambda k: (k, 0))],
        out_specs=pl.BlockSpec((ka, cb), lambda k: (0, 0)),
        out_shape=_sds((ka, cb), F32),
    )(a, b)


def out_proj_fwd(x, sg_out, dn_out, w, name):
    n, d = x.shape
    tm = min(ROW_TILE, n)

    def body(x_ref, a_ref, b_ref, w_ref, o_ref):
        o_ref[...] = (x_ref[...] + _dot(a_ref[...], w_ref[0:HALF_W, :], NN)
                      + _dot(b_ref[...], w_ref[HALF_W:2 * HALF_W, :], NN))

    row = pl.BlockSpec((tm, d), lambda i: (i, 0))
    half = pl.BlockSpec((tm, HALF_W), lambda i: (i, 0))
    return _call(
        body, name=name, grid=(n // tm,),
        in_specs=[row, half, half, pl.BlockSpec((2 * HALF_W, d), lambda i: (0, 0))],
        out_specs=row, out_shape=_sds((n, d), F32),
    )(x, sg_out, dn_out, w)


def out_proj_bwd_x(dy, w, name, comm=None):
    n, d = dy.shape
    tm = min(ROW_TILE, n)

    def body(dy_ref, w_ref, dsg_ref, ddn_ref, dyb_ref):
        dyb = dy_ref[...].astype(BF16)
        dyb_ref[...] = dyb
        dsg_ref[...] = _dot(dyb, w_ref[0:HALF_W, :], NT)
        ddn_ref[...] = _dot(dyb, w_ref[HALF_W:2 * HALF_W, :], NT)

    row = pl.BlockSpec((tm, d), lambda i: (i, 0))
    half = pl.BlockSpec((tm, HALF_W), lambda i: (i, 0))
    return _call(
        body, name=name, grid=(n // tm,),
        in_specs=[row, pl.BlockSpec((2 * HALF_W, d), lambda i: (0, 0))],
        out_specs=[half, half, row],
        out_shape=[_sds((n, HALF_W), F32), _sds((n, HALF_W), F32), _sds((n, d), BF16)], comm=comm,
    )(dy, w)


SG_PAIRS = SG_GROUPS // 2


def _sg_low_half():
    return _iota2((SG_CHUNK, LANES), 1) < SG_GROUP_DIM


def _sg_pair_cols(p):
    return slice(p * LANES, (p + 1) * LANES)


def _sg_causal():
    return _iota2((SG_CHUNK, SG_CHUNK), 0) >= _iota2((SG_CHUNK, SG_CHUNK), 1)


def _sg_forward_chunk(pu, pv, ln_g, ln_b, wc, bias, low):
    u = _gelu(pu)
    v = _gelu(pv)
    mu = jnp.mean(v, axis=-1, keepdims=True)
    vc = v - mu
    rs = lax.rsqrt(jnp.mean(vc * vc, axis=-1, keepdims=True) + EPS)
    xhat = vc * rs
    vn = (xhat * ln_g + ln_b).astype(BF16)
    parts = []
    for p in range(SG_PAIRS):
        vn_p = vn[:, _sg_pair_cols(p)]
        parts.append(jnp.where(low, _dot(wc[2 * p], vn_p, NN), _dot(wc[2 * p + 1], vn_p, NN)))
    vs = bias + jnp.concatenate(parts, axis=1)
    return u, xhat, rs, vn, vs


def sg_fwd(proj, ln_g, ln_b, w_s, bias_tile, name):
    n = proj.shape[0]
    tm = min(ROW_TILE, n)

    def body(pu_ref, pv_ref, g_ref, b_ref, w_ref, bias_ref, o_ref):
        causal = _sg_causal()
        wc = [jnp.where(causal, w_ref[g], 0.0).astype(BF16) for g in range(SG_GROUPS)]
        masks = _sg_low_half()
        for ci in range(tm // SG_CHUNK):
            rows = slice(ci * SG_CHUNK, (ci + 1) * SG_CHUNK)
            u, _, _, _, vs = _sg_forward_chunk(pu_ref[rows, :], pv_ref[rows, :], g_ref[...],
                                               b_ref[...], wc, bias_ref[...], masks)
            o_ref[rows, :] = (u * vs).astype(BF16)

    vec = pl.BlockSpec((1, HALF_W), lambda i: (0, 0))
    return _call(
        body, name=name, grid=(n // tm,),
        in_specs=[pl.BlockSpec((tm, HALF_W), lambda i: (i, 0)),
                  pl.BlockSpec((tm, HALF_W), lambda i: (i, 1)), vec, vec,
                  pl.BlockSpec((SG_GROUPS, SG_CHUNK, SG_CHUNK), lambda i: (0, 0, 0)),
                  pl.BlockSpec((SG_CHUNK, HALF_W), lambda i: (0, 0))],
        out_specs=pl.BlockSpec((tm, HALF_W), lambda i: (i, 0)),
        out_shape=_sds((n, HALF_W), BF16),
    )(proj, proj, ln_g, ln_b, w_s, bias_tile)


def sg_bwd(dsg, proj, ln_g, ln_b, w_s, bias_tile, name):
    n = proj.shape[0]
    tm = min(ROW_TILE, n)

    def body(d_ref, pu_ref, pv_ref, g_ref, b_ref, w_ref, bias_ref,
             dp_ref, dw_ref, db_ref, dlg_ref, dlb_ref):
        @pl.when(pl.program_id(0) == 0)
        def _():
            dw_ref[...] = jnp.zeros_like(dw_ref)
            db_ref[...] = jnp.zeros_like(db_ref)
            dlg_ref[...] = jnp.zeros_like(dlg_ref)
            dlb_ref[...] = jnp.zeros_like(dlb_ref)

        causal = _sg_causal()
        wc = [jnp.where(causal, w_ref[g], 0.0).astype(BF16) for g in range(SG_GROUPS)]
        masks = _sg_low_half()
        ln_g_v = g_ref[...]
        for ci in range(tm // SG_CHUNK):
            rows = slice(ci * SG_CHUNK, (ci + 1) * SG_CHUNK)
            pu = pu_ref[rows, :]
            pv = pv_ref[rows, :]
            u, xhat, rs, vn, vs = _sg_forward_chunk(pu, pv, ln_g_v, b_ref[...], wc,
                                                    bias_ref[...], masks)
            dout = d_ref[rows, :]
            dp_ref[rows, 0:HALF_W] = (dout * vs * _gelu_grad(pu)).astype(BF16)
            dvs = dout * u
            dvs_b = dvs.astype(BF16)
            db_ref[...] += dvs
            dvn_parts = []
            for p in range(SG_PAIRS):
                dvs_p = dvs_b[:, _sg_pair_cols(p)]
                vn_p = vn[:, _sg_pair_cols(p)]
                dvn_parts.append(jnp.where(masks, _dot(wc[2 * p], dvs_p, TN), _dot(wc[2 * p + 1], dvs_p, TN)))
                zero = jnp.zeros_like(dvs_p)
                dw_ref[2 * p] += jnp.where(causal, _dot(jnp.where(masks, dvs_p, zero), vn_p, NT), 0.0)
                dw_ref[2 * p + 1] += jnp.where(causal, _dot(jnp.where(masks, zero, dvs_p), vn_p, NT), 0.0)
            dvn = jnp.concatenate(dvn_parts, axis=1)
            dlg_ref[...] += jnp.sum(dvn * xhat, axis=0, keepdims=True)
            dlb_ref[...] += jnp.sum(dvn, axis=0, keepdims=True)
            dxh = dvn * ln_g_v
            dv = rs * (dxh - jnp.mean(dxh, axis=-1, keepdims=True)
                       - xhat * jnp.mean(dxh * xhat, axis=-1, keepdims=True))
            dp_ref[rows, HALF_W:2 * HALF_W] = (dv * _gelu_grad(pv)).astype(BF16)

    vec = pl.BlockSpec((1, HALF_W), lambda i: (0, 0))
    wspec = pl.BlockSpec((SG_GROUPS, SG_CHUNK, SG_CHUNK), lambda i: (0, 0, 0))
    tile = pl.BlockSpec((SG_CHUNK, HALF_W), lambda i: (0, 0))
    return _call(
        body, name=name, grid=(n // tm,),
        in_specs=[pl.BlockSpec((tm, HALF_W), lambda i: (i, 0)),
                  pl.BlockSpec((tm, HALF_W), lambda i: (i, 0)),
                  pl.BlockSpec((tm, HALF_W), lambda i: (i, 1)), vec, vec, wspec, tile],
        out_specs=[pl.BlockSpec((tm, 2 * HALF_W), lambda i: (i, 0)), wspec, tile, vec, vec],
        out_shape=[_sds((n, PROJ_W), BF16), _sds((SG_GROUPS, SG_CHUNK, SG_CHUNK), F32),
                   _sds((SG_CHUNK, HALF_W), F32), _sds((1, HALF_W), F32), _sds((1, HALF_W), F32)],
    )(dsg, proj, proj, ln_g, ln_b, w_s, bias_tile)


CONV_K = 4
CONV_BLOCK = 256


def _shift_down(x, s, row):
    if s == 0:
        return x
    return jnp.where(row >= s, pltpu.roll(x, s, 0), 0.0)


def _shift_up(x, s, row):
    if s == 0:
        return x
    t_len = x.shape[0]
    return jnp.where(row < t_len - s, pltpu.roll(x, t_len - s, 0), 0.0)


def _conv_taps(x, row):
    return [_shift_down(x, CONV_K - 1 - j, row) for j in range(CONV_K)]


def _conv(taps, w):
    y = taps[0] * w[0:1, :]
    for j in range(1, CONV_K):
        y = y + taps[j] * w[j:j + 1, :]
    return y


def dn_conv_fwd(proj3, conv_w, name):
    b, t, _ = proj3.shape
    nblk = 3 * HALF_W // CONV_BLOCK
    first = 2 * HALF_W // CONV_BLOCK
    n_norm = 2 * HALF_W // CONV_BLOCK

    def body(x_ref, w_ref, o_ref):
        s = pl.program_id(1)
        x = x_ref[0]
        y = _conv(_conv_taps(x, _iota2(x.shape, 0)), w_ref[...])
        y = y * _sigmoid(y)

        @pl.when(s < n_norm)
        def _():
            for h in range(CONV_BLOCK // HEAD_DIM):
                cs = slice(h * HEAD_DIM, (h + 1) * HEAD_DIM)
                yh = y[:, cs]
                o_ref[0, :, cs] = yh * lax.rsqrt(jnp.sum(yh * yh, axis=-1, keepdims=True) + EPS)

        @pl.when(s >= n_norm)
        def _():
            o_ref[0] = y

    return _call(
        body, name=name, grid=(b, nblk),
        in_specs=[pl.BlockSpec((1, t, CONV_BLOCK), lambda i, s: (i, 0, first + s)),
                  pl.BlockSpec((CONV_K, CONV_BLOCK), lambda i, s: (0, s))],
        out_specs=pl.BlockSpec((1, t, CONV_BLOCK), lambda i, s: (i, 0, s)),
        out_shape=_sds((b, t, 3 * HALF_W), F32),
    )(proj3, conv_w)


def dn_conv_bwd(dqkv, proj3, conv_w, dproj3, name, comm=None):
    b, t, _ = proj3.shape
    nblk = 3 * HALF_W // CONV_BLOCK
    first = 2 * HALF_W // CONV_BLOCK
    n_norm = 2 * HALF_W // CONV_BLOCK

    def body(d_ref, x_ref, w_ref, dproj_in, dx_ref, dw_ref, ds_ref):
        s = pl.program_id(0)

        @pl.when(pl.program_id(1) == 0)
        def _():
            dw_ref[...] = jnp.zeros_like(dw_ref)

        x = x_ref[0]
        w = w_ref[...]
        row = _iota2(x.shape, 0)
        taps = _conv_taps(x, row)
        c = _conv(taps, w)
        sg = _sigmoid(c)
        y = c * sg

        @pl.when(s < n_norm)
        def _():
            for h in range(CONV_BLOCK // HEAD_DIM):
                cs = slice(h * HEAD_DIM, (h + 1) * HEAD_DIM)
                yh = y[:, cs]
                r = lax.rsqrt(jnp.sum(yh * yh, axis=-1, keepdims=True) + EPS)
                nh = yh * r
                dn = d_ref[0, :, cs]
                ds_ref[:, cs] = r * (dn - nh * jnp.sum(dn * nh, axis=-1, keepdims=True))

        @pl.when(s >= n_norm)
        def _():
            ds_ref[...] = d_ref[0]

        dc = ds_ref[...] * (sg * (1.0 + c * (1.0 - sg)))
        dx = _shift_up(dc, CONV_K - 1, row) * w[0:1, :]
        for j in range(1, CONV_K):
            dx = dx + _shift_up(dc, CONV_K - 1 - j, row) * w[j:j + 1, :]
        dx_ref[0] = dx.astype(BF16)
        for j in range(CONV_K):
            dw_ref[j:j + 1, :] += jnp.sum(dc * taps[j], axis=0, keepdims=True)

    return _call(
        body, name=name, grid=(nblk, b),
        in_specs=[pl.BlockSpec((1, t, CONV_BLOCK), lambda s, i: (i, 0, s)),
                  pl.BlockSpec((1, t, CONV_BLOCK), lambda s, i: (i, 0, first + s)),
                  pl.BlockSpec((CONV_K, CONV_BLOCK), lambda s, i: (0, s)), _ANY],
        out_specs=[pl.BlockSpec((1, t, CONV_BLOCK), lambda s, i: (i, 0, first + s)),
                   pl.BlockSpec((CONV_K, CONV_BLOCK), lambda s, i: (0, s))],
        out_shape=[_sds(dproj3.shape, BF16), _sds((CONV_K, 3 * HALF_W), F32)],
        scratch=[pltpu.VMEM((t, CONV_BLOCK), F32)],
        input_output_aliases={3: 0}, comm=comm,
    )(dqkv, proj3, conv_w, dproj3)


def _chunk_masks():
    ii = _iota2((DN_CHUNK, DN_CHUNK), 0)
    jj = _iota2((DN_CHUNK, DN_CHUNK), 1)
    return ii >= jj, ii > jj, ii == jj


LOCKSTEP_CHUNKS = 4


def _inv_unit_lower_many(l_mats, eye):
    eye_f = jnp.where(eye, 1.0, 0.0)
    ps = [-l for l in l_mats]
    ts = [eye_f + p for p in ps]
    pss = [_split(p) for p in ps]
    size = 2
    while size < DN_CHUNK:
        ps = [_dot3(s, s) for s in pss]
        pss = [_split(p) for p in ps]
        ts = [t + _dot3(_split(t), s) for t, s in zip(ts, pss)]
        size *= 2
    return ts


def _gates(pba, ea_row, dtb_row):
    beta = _sigmoid(pba)
    g = -ea_row * _softplus(pba + dtb_row)
    return beta, g


def _chunk_decay(gcol):
    incl, strict, eye = _chunk_masks()
    grow = jnp.sum(jnp.where(eye, gcol, 0.0), axis=0, keepdims=True)
    decay = jnp.where(incl, jnp.exp(jnp.where(incl, gcol - grow, 0.0)), 0.0)
    return decay, incl, strict, eye


def dn_chunk_fwd(qkv, proj3, alog_row, dtb_row, name):
    b, t, _ = qkv.shape
    rblk = min(256, t)
    n_in = rblk // DN_CHUNK

    def body(q_ref, k_ref, v_ref, pba_ref, al_ref, dtb_ref,
             u_ref, w_ref, qd_ref, kd_ref, qk_ref, ti_ref, gc_ref):
        ea = jnp.exp(al_ref[...])
        tri = jnp.where(_chunk_masks()[0], 1.0, 0.0)

        _, strict, eye = _chunk_masks()

        def chunk_group(cg, carry):
            items = []
            for sub in range(LOCKSTEP_CHUNKS):
                rows = pl.ds(pl.multiple_of((cg * LOCKSTEP_CHUNKS + sub) * DN_CHUNK, DN_CHUNK), DN_CHUNK)
                beta_all, g_all = _gates(pba_ref[0, rows, :], ea, dtb_ref[...])
                gc = _dot_exact_lhs(tri, g_all)
                gc_ref[0, rows, :] = gc
                for h in range(N_HEADS):
                    items.append((rows, h, beta_all[:, h:h + 1], gc[:, N_HEADS + h:N_HEADS + h + 1]))
            ks, kbs, decays, egs = [], [], [], []
            for rows, h, beta, gcol in items:
                cs = slice(h * HEAD_DIM, (h + 1) * HEAD_DIM)
                k = k_ref[0, rows, cs]
                ks.append(k)
                kbs.append(k * beta)
                decays.append(_chunk_decay(gcol)[0])
                egs.append(jnp.exp(gcol))
            ms = [_bdot(kb, k, NT) for kb, k in zip(kbs, ks)]
            tinvs = _inv_unit_lower_many([jnp.where(strict, m * dc, 0.0) for m, dc in zip(ms, decays)], eye)
            tsps = [_split(t) for t in tinvs]
            for (rows, h, beta, gcol), tsp, tinv in zip(items, tsps, tinvs):
                cs = slice(h * HEAD_DIM, (h + 1) * HEAD_DIM)
                u_ref[0, rows, cs] = _dot3(tsp, _split(v_ref[0, rows, cs] * beta))
                ti_ref[0, h, rows, :] = tinv
            for (rows, h, beta, gcol), tsp, kb, eg in zip(items, tsps, kbs, egs):
                cs = slice(h * HEAD_DIM, (h + 1) * HEAD_DIM)
                w_ref[0, rows, cs] = _dot3(tsp, _split(kb * eg))
            for (rows, h, beta, gcol), k, dc, eg in zip(items, ks, decays, egs):
                cs = slice(h * HEAD_DIM, (h + 1) * HEAD_DIM)
                q = q_ref[0, rows, cs] * QK_SCALE
                qk_ref[0, h, rows, :] = _bdot(q, k, NT) * dc
                qd_ref[0, rows, cs] = q * eg
                kd_ref[0, rows, cs] = k * jnp.exp(gcol[DN_CHUNK - 1:DN_CHUNK, :] - gcol)
            return carry

        lax.fori_loop(0, n_in // LOCKSTEP_CHUNKS, chunk_group, 0)

    def seg(cblk):
        return pl.BlockSpec((1, rblk, HALF_W), lambda i, r: (i, r, cblk))

    vec = pl.BlockSpec((1, LANES), lambda i, r: (0, 0))
    wide = pl.BlockSpec((1, rblk, HALF_W), lambda i, r: (i, r, 0))
    sq = pl.BlockSpec((1, N_HEADS, rblk, DN_CHUNK), lambda i, r: (i, 0, r, 0))
    return _call(
        body, name=name, grid=(b, t // rblk),
        in_specs=[seg(0), seg(1), seg(2),
                  pl.BlockSpec((1, rblk, LANES), lambda i, r: (i, r, GATE_COL_BLOCK)), vec, vec],
        out_specs=[wide, wide, wide, wide, sq, sq,
                   pl.BlockSpec((1, rblk, LANES), lambda i, r: (i, r, 0))],
        out_shape=[_sds((b, t, HALF_W), F32)] * 4
        + [_sds((b, N_HEADS, t, DN_CHUNK), F32)] * 2 + [_sds((b, t, LANES), F32)],
    )(qkv, qkv, qkv, proj3, alog_row, dtb_row)


def dn_scan_fwd(u, w, qd, kd, qk, gc, name):
    b, t, _ = u.shape
    nc = t // DN_CHUNK
    bh = b * N_HEADS

    def body(u_ref, w_ref, qd_ref, kd_ref, qk_ref, gc_ref, o_ref, sin_ref, s_ref):
        @pl.when(pl.program_id(0) == 0)
        def _():
            s_ref[...] = jnp.zeros_like(s_ref)

        items = [(bi, h, slice(h * HEAD_DIM, (h + 1) * HEAD_DIM)) for bi in range(b) for h in range(N_HEADS)]
        sbs = []
        for bi, h, cs in items:
            s = s_ref[bi * N_HEADS + h]
            sin_ref[0, bi * N_HEADS + h] = s
            sbs.append(s.astype(BF16))
        ws = [_bdot(w_ref[bi, :, cs], sb, NN) for (bi, h, cs), sb in zip(items, sbs)]
        qs = [_bdot(qd_ref[bi, :, cs], sb, NN) for (bi, h, cs), sb in zip(items, sbs)]
        vbs = [(u_ref[bi, :, cs] - wsi).astype(BF16) for (bi, h, cs), wsi in zip(items, ws)]
        for (bi, h, cs), qsi, vb in zip(items, qs, vbs):
            o_ref[bi, :, cs] = qsi + _bdot(qk_ref[bi, h], vb, NN)
        for (bi, h, cs), vb in zip(items, vbs):
            gl = jnp.exp(gc_ref[bi, DN_CHUNK - 1:DN_CHUNK, N_HEADS + h:N_HEADS + h + 1])
            idx = bi * N_HEADS + h
            s_ref[idx] = s_ref[idx] * gl + _bdot(kd_ref[bi, :, cs], vb, TN)

    wide = pl.BlockSpec((b, DN_CHUNK, HALF_W), lambda c: (0, c, 0))
    return _call(
        body, name=name, grid=(nc,),
        in_specs=[wide, wide, wide, wide,
                  pl.BlockSpec((b, N_HEADS, DN_CHUNK, DN_CHUNK), lambda c: (0, 0, c, 0)),
                  pl.BlockSpec((b, DN_CHUNK, LANES), lambda c: (0, c, 0))],
        out_specs=[wide, pl.BlockSpec((1, bh, HEAD_DIM, HEAD_DIM), lambda c: (c, 0, 0, 0))],
        out_shape=[_sds((b, t, HALF_W), F32), _sds((nc, bh, HEAD_DIM, HEAD_DIM), F32)],
        scratch=[pltpu.VMEM((bh, HEAD_DIM, HEAD_DIM), F32)],
    )(u, w, qd, kd, qk, gc)


def dn_scan_bwd(do, u, w, qd, kd, qk, gc, s_in, name):
    b, t, _ = u.shape
    nc = t // DN_CHUNK
    bh = b * N_HEADS

    def body(do_ref, u_ref, w_ref, qd_ref, kd_ref, qk_ref, gc_ref, sin_ref,
             du_ref, dw_ref, dqd_ref, dkd_ref, dqk_ref, dgc_ref, ds_ref):
        @pl.when(pl.program_id(0) == 0)
        def _():
            ds_ref[...] = jnp.zeros_like(ds_ref)

        last_row = _iota2((DN_CHUNK, LANES), 0) == DN_CHUNK - 1
        lane = _iota2((DN_CHUNK, LANES), 1)
        items = [(bi, h, slice(h * HEAD_DIM, (h + 1) * HEAD_DIM)) for bi in range(b) for h in range(N_HEADS)]
        sbs = [sin_ref[0, bi * N_HEADS + h].astype(BF16) for bi, h, cs in items]
        wvs = [w_ref[bi, :, cs].astype(BF16) for bi, h, cs in items]
        dovs = [do_ref[bi, :, cs].astype(BF16) for bi, h, cs in items]
        dsbs = [ds_ref[bi * N_HEADS + h].astype(BF16) for bi, h, cs in items]
        vbs = [(u_ref[bi, :, cs] - _dot(wv, sb, NN)).astype(BF16)
               for (bi, h, cs), wv, sb in zip(items, wvs, sbs)]
        for (bi, h, cs), dov, sb in zip(items, dovs, sbs):
            dqd_ref[bi, :, cs] = _dot(dov, sb, NT)
        dvns = [_dot(kd_ref[bi, :, cs].astype(BF16), dsb, NN) + _dot(qk_ref[bi, h].astype(BF16), dov, TN)
                for (bi, h, cs), dsb, dov in zip(items, dsbs, dovs)]
        for (bi, h, cs), vb, dsb, dov in zip(items, vbs, dsbs, dovs):
            dkd_ref[bi, :, cs] = _dot(vb, dsb, NT)
            dqk_ref[bi, h] = _dot(dov, vb, NT)
        dgls = []
        for (bi, h, cs), dvn, sb, wv, dov in zip(items, dvns, sbs, wvs, dovs):
            idx = bi * N_HEADS + h
            du_ref[bi, :, cs] = dvn
            dvn_b = dvn.astype(BF16)
            dw_ref[bi, :, cs] = -_dot(dvn_b, sb, NT)
            gl = jnp.exp(gc_ref[bi, DN_CHUNK - 1:DN_CHUNK, N_HEADS + h:N_HEADS + h + 1])
            ds = ds_ref[idx]
            dgl = jnp.sum(jnp.sum(ds * sin_ref[0, idx], axis=1, keepdims=True), axis=0, keepdims=True)
            dgls.append(dgl * gl)
            ds_ref[idx] = (ds * gl + _dot(qd_ref[bi, :, cs].astype(BF16), dov, TN)
                           - _dot(wv, dvn_b, TN))
        for bi in range(b):
            dgc = jnp.zeros((DN_CHUNK, LANES), F32)
            for h in range(N_HEADS):
                dgc = dgc + jnp.where(jnp.logical_and(last_row, lane == N_HEADS + h),
                                      dgls[bi * N_HEADS + h], 0.0)
            dgc_ref[bi] = dgc

    def rev(c):
        return nc - 1 - c

    wide = pl.BlockSpec((b, DN_CHUNK, HALF_W), lambda c: (0, rev(c), 0))
    sq = pl.BlockSpec((b, N_HEADS, DN_CHUNK, DN_CHUNK), lambda c: (0, 0, rev(c), 0))
    gates = pl.BlockSpec((b, DN_CHUNK, LANES), lambda c: (0, rev(c), 0))
    return _call(
        body, name=name, grid=(nc,),
        in_specs=[wide, wide, wide, wide, wide, sq, gates,
                  pl.BlockSpec((1, bh, HEAD_DIM, HEAD_DIM), lambda c: (rev(c), 0, 0, 0))],
        out_specs=[wide, wide, wide, wide, sq, gates],
        out_shape=[_sds((b, t, HALF_W), F32)] * 4
        + [_sds((b, N_HEADS, t, DN_CHUNK), F32), _sds((b, t, LANES), F32)],
        scratch=[pltpu.VMEM((bh, HEAD_DIM, HEAD_DIM), F32)],
    )(do, u, w, qd, kd, qk, gc, s_in)


def dn_chunk_bwd(qkv, proj3, alog_row, dtb_row, tinv, u, w, du, dw, dqd, dkd, dqk, dgc_scan, dproj3, name,
                 comm=None):
    b, t, _ = qkv.shape
    rblk = min(256, t)
    n_in = rblk // DN_CHUNK

    def body(q_ref, k_ref, v_ref, pba_ref, al_ref, dtb_ref, ti_ref, u_ref, w_ref,
             du_ref, dw_ref, dqd_ref, dkd_ref, dqk_ref, dgs_ref, dproj_in,
             dq_ref, dpba_ref, dal_ref, ddtb_ref):
        @pl.when(jnp.logical_and(pl.program_id(0) == 0, pl.program_id(1) == 0))
        def _():
            dal_ref[...] = jnp.zeros_like(dal_ref)
            ddtb_ref[...] = jnp.zeros_like(ddtb_ref)

        ea = jnp.exp(al_ref[...])
        incl0 = _chunk_masks()[0]
        tri = jnp.where(incl0, 1.0, 0.0)
        tri_up = jnp.where(_iota2((DN_CHUNK, DN_CHUNK), 1) >= _iota2((DN_CHUNK, DN_CHUNK), 0), 1.0, 0.0)
        lane = _iota2((DN_CHUNK, LANES), 1)
        last_col = _iota2((DN_CHUNK, 1), 0) == DN_CHUNK - 1

        _, strict, _ = _chunk_masks()
        gate_lane = jnp.logical_and(lane >= N_HEADS, lane < 2 * N_HEADS)

        def chunk_group(cg, carry):
            tiles, items = [], []
            for sub in range(LOCKSTEP_CHUNKS):
                rows = pl.ds(pl.multiple_of((cg * LOCKSTEP_CHUNKS + sub) * DN_CHUNK, DN_CHUNK), DN_CHUNK)
                pba = pba_ref[0, rows, :]
                beta_all, g_all = _gates(pba, ea, dtb_ref[...])
                gc = _dot_exact_lhs(tri, g_all)
                tiles.append((rows, pba, beta_all, g_all))
                for h in range(N_HEADS):
                    items.append((sub, rows, h, slice(h * HEAD_DIM, (h + 1) * HEAD_DIM),
                                  beta_all[:, h:h + 1], gc[:, N_HEADS + h:N_HEADS + h + 1]))
            decays = [_chunk_decay(gcol)[0] for _, _, _, _, _, gcol in items]
            egs = [jnp.exp(gcol) for _, _, _, _, _, gcol in items]
            qbs = [(q_ref[0, rows, cs] * QK_SCALE).astype(BF16) for _, rows, h, cs, _, _ in items]
            kfs = [k_ref[0, rows, cs].astype(BF16) for _, rows, h, cs, _, _ in items]
            kbs = [k_ref[0, rows, cs] * beta for _, rows, h, cs, beta, _ in items]
            kbbs = [kb.astype(BF16) for kb in kbs]
            tsps = [_split(ti_ref[0, h, rows, :]) for _, rows, h, cs, _, _ in items]
            drus = [_dot3(tsp, _split(du_ref[0, rows, cs]), TN)
                    for (_, rows, h, cs, _, _), tsp in zip(items, tsps)]
            drws = [_dot3(tsp, _split(dw_ref[0, rows, cs]), TN)
                    for (_, rows, h, cs, _, _), tsp in zip(items, tsps)]
            m_kks = [_dot(kbb, kf, NT) for kbb, kf in zip(kbbs, kfs)]
            a_qks = [_dot(qb, kf, NT) for qb, kf in zip(qbs, kfs)]
            dls = [-jnp.where(strict, _dot3(_split(dru), _split(u_ref[0, rows, cs]), NT)
                              + _dot3(_split(drw), _split(w_ref[0, rows, cs]), NT), 0.0)
                   for (_, rows, h, cs, _, _), dru, drw in zip(items, drus, drws)]
            dms = [(dl * dc).astype(BF16) for dl, dc in zip(dls, decays)]
            das = [(dqk_ref[0, h, rows, :] * dc).astype(BF16)
                   for (_, rows, h, cs, _, _), dc in zip(items, decays)]
            dkb_mm = [_dot(dm, kf, NN) for dm, kf in zip(dms, kfs)]
            dk_mm = [_dot(dm, kbb, TN) + _dot(da, qb, TN) for dm, kbb, da, qb in zip(dms, kbbs, das, qbs)]
            dqs_mm = [_dot(da, kf, NN) for da, kf in zip(das, kfs)]
            dgc_tiles = [dgs_ref[0, rows, :] for rows, _, _, _ in tiles]
            dbeta_tiles = [jnp.zeros((DN_CHUNK, LANES), F32) for _ in tiles]
            for n_it, (sub, rows, h, cs, beta, gcol) in enumerate(items):
                eg, dc = egs[n_it], decays[n_it]
                k = k_ref[0, rows, cs]
                q = q_ref[0, rows, cs] * QK_SCALE
                kb, dru, drw = kbs[n_it], drus[n_it], drws[n_it]
                ek = jnp.exp(gcol[DN_CHUNK - 1:DN_CHUNK, :] - gcol)
                e_mat = (dls[n_it] * m_kks[n_it] + dqk_ref[0, h, rows, :] * a_qks[n_it]) * dc
                dkb = drw * eg + dkb_mm[n_it]
                dqd = dqd_ref[0, rows, cs]
                dkd = dkd_ref[0, rows, cs]
                kdk = dkd * k * ek
                kdk_total = jnp.sum(jnp.sum(kdk, axis=0, keepdims=True), axis=1, keepdims=True)
                dg = (jnp.sum(drw * kb * eg + dqd * q * eg - kdk, axis=-1, keepdims=True)
                      + jnp.sum(e_mat, axis=1, keepdims=True)
                      - _row_to_col(jnp.sum(e_mat, axis=0, keepdims=True))
                      + jnp.where(last_col, kdk_total, 0.0))
                dbeta = jnp.sum(dkb * k + dru * v_ref[0, rows, cs], axis=-1, keepdims=True)
                dq_ref[0, rows, cs] = (dqs_mm[n_it] + dqd * eg) * QK_SCALE
                dq_ref[0, rows, pl.ds(HALF_W + h * HEAD_DIM, HEAD_DIM)] = dk_mm[n_it] + dkd * ek + dkb * beta
                dq_ref[0, rows, pl.ds(2 * HALF_W + h * HEAD_DIM, HEAD_DIM)] = dru * beta
                dgc_tiles[sub] = dgc_tiles[sub] + jnp.where(lane == N_HEADS + h, dg, 0.0)
                dbeta_tiles[sub] = dbeta_tiles[sub] + jnp.where(lane == h, dbeta, 0.0)
            for (rows, pba, beta_all, g_all), dgc_tile, dbeta_tile in zip(tiles, dgc_tiles, dbeta_tiles):
                dg_tile = _dot_exact_lhs(tri_up, dgc_tile)
                da_pre = dg_tile * (-ea) * _sigmoid(pba + dtb_ref[...])
                dal_ref[...] += jnp.sum(jnp.where(gate_lane, dg_tile * g_all, 0.0), axis=0, keepdims=True)
                ddtb_ref[...] += jnp.sum(jnp.where(gate_lane, da_pre, 0.0), axis=0, keepdims=True)
                dpba_ref[0, rows, :] = jnp.where(lane < N_HEADS, dbeta_tile * beta_all * (1.0 - beta_all),
                                                 jnp.where(gate_lane, da_pre, 0.0)).astype(BF16)
            return carry

        lax.fori_loop(0, n_in // LOCKSTEP_CHUNKS, chunk_group, 0)

    def seg(cblk):
        return pl.BlockSpec((1, rblk, HALF_W), lambda i, r: (i, r, cblk))

    vec = pl.BlockSpec((1, LANES), lambda i, r: (0, 0))
    wide = pl.BlockSpec((1, rblk, HALF_W), lambda i, r: (i, r, 0))
    sq = pl.BlockSpec((1, N_HEADS, rblk, DN_CHUNK), lambda i, r: (i, 0, r, 0))
    gates = pl.BlockSpec((1, rblk, LANES), lambda i, r: (i, r, 0))
    return _call(
        body, name=name, grid=(b, t // rblk),
        in_specs=[seg(0), seg(1), seg(2),
                  pl.BlockSpec((1, rblk, LANES), lambda i, r: (i, r, GATE_COL_BLOCK)), vec, vec,
                  sq, wide, wide, wide, wide, wide, wide, sq, gates, _ANY],
        out_specs=[pl.BlockSpec((1, rblk, 3 * HALF_W), lambda i, r: (i, r, 0)),
                   pl.BlockSpec((1, rblk, LANES), lambda i, r: (i, r, GATE_COL_BLOCK)), vec, vec],
        out_shape=[_sds((b, t, 3 * HALF_W), F32), _sds(dproj3.shape, BF16),
                   _sds((1, LANES), F32), _sds((1, LANES), F32)],
        input_output_aliases={15: 1}, comm=comm,
    )(qkv, qkv, qkv, proj3, alog_row, dtb_row, tinv, u, w, du, dw, dqd, dkd, dqk, dgc_scan, dproj3)


def dn_out_fwd(o, proj, dn_norm, name):
    n = o.shape[0]
    tm = min(ROW_TILE, n)

    def body(o_ref, z_ref, g_ref, y_ref):
        for h in range(N_HEADS):
            cs = slice(h * HEAD_DIM, (h + 1) * HEAD_DIM)
            oh = o_ref[:, cs]
            z = z_ref[:, cs]
            y = oh * _rms_scale(oh) * g_ref[...]
            y_ref[:, cs] = (y * (z * _sigmoid(z))).astype(BF16)

    half = pl.BlockSpec((tm, HALF_W), lambda i: (i, 0))
    return _call(
        body, name=name, grid=(n // tm,),
        in_specs=[half, pl.BlockSpec((tm, HALF_W), lambda i: (i, 5)),
                  pl.BlockSpec((1, HEAD_DIM), lambda i: (0, 0))],
        out_specs=half, out_shape=_sds((n, HALF_W), BF16),
    )(o, proj, dn_norm)


def dn_out_bwd(dy, o, proj, dn_norm, dproj, name):
    n = o.shape[0]
    tm = min(ROW_TILE, n)

    def body(dy_ref, o_ref, z_ref, g_ref, dproj_in, do_ref, dz_ref, dg_ref):
        @pl.when(pl.program_id(0) == 0)
        def _():
            dg_ref[...] = jnp.zeros_like(dg_ref)

        g = g_ref[...]
        dg = jnp.zeros_like(g)
        for h in range(N_HEADS):
            cs = slice(h * HEAD_DIM, (h + 1) * HEAD_DIM)
            oh = o_ref[:, cs]
            z = z_ref[:, cs]
            d = dy_ref[:, cs]
            r = _rms_scale(oh)
            nh = oh * r
            sz = _sigmoid(z)
            dyn = d * (z * sz)
            dz_ref[:, cs] = (d * (nh * g) * (sz * (1.0 + z * (1.0 - sz)))).astype(BF16)
            dg = dg + jnp.sum(dyn * nh, axis=0, keepdims=True)
            dn = dyn * g
            do_ref[:, cs] = r * (dn - nh * jnp.mean(dn * nh, axis=-1, keepdims=True))
        dg_ref[...] += dg

    half = pl.BlockSpec((tm, HALF_W), lambda i: (i, 0))
    vec = pl.BlockSpec((1, HEAD_DIM), lambda i: (0, 0))
    return _call(
        body, name=name, grid=(n // tm,),
        in_specs=[half, half, pl.BlockSpec((tm, HALF_W), lambda i: (i, 5)), vec, _ANY],
        out_specs=[half, pl.BlockSpec((tm, HALF_W), lambda i: (i, 5)), vec],
        out_shape=[_sds((n, HALF_W), F32), _sds(dproj.shape, BF16), _sds((1, HEAD_DIM), F32)],
        input_output_aliases={4: 1},
    )(dy, o, proj, dn_norm, dproj)


def _adamw_math(w, g, m, v):
    m_new = ADAM_B1 * m + (1.0 - ADAM_B1) * g
    v_new = ADAM_B2 * v + (1.0 - ADAM_B2) * (g * g)
    m_hat = m_new / (1.0 - ADAM_B1 ** ADAM_STEP)
    v_hat = v_new / (1.0 - ADAM_B2 ** ADAM_STEP)
    delta = -ADAM_LR * (m_hat / (jnp.sqrt(v_hat) + ADAM_EPS) + ADAM_WD * w)
    return delta, m_new, v_new


def adamw(w, g, m, v, name):
    r, c = w.shape
    tr = r
    for cand in (256, 352):
        if r % cand == 0 and r > cand:
            tr = cand
            break

    def body(w_ref, g_ref, m_ref, v_ref, d_ref, mo_ref, vo_ref):
        d, mn, vn = _adamw_math(w_ref[...], g_ref[...], m_ref[...], v_ref[...])
        d_ref[...] = d
        mo_ref[...] = mn
        vo_ref[...] = vn

    spec = pl.BlockSpec((tr, c), lambda i: (i, 0))
    return _call(
        body, name=name, grid=(r // tr,),
        in_specs=[spec] * 4, out_specs=[spec] * 3, out_shape=[_sds((r, c), F32)] * 3,
    )(w, g, m, v)


def _place():
    return lax.axis_index("x"), lax.axis_index("y"), lax.axis_index("c")


def _other_chips(x, y):
    return [(1 - x, y), (x, 1 - y), (1 - x, 1 - y)]


_ANY = pl.BlockSpec(memory_space=pl.ANY)


def cast_place(w, shard_idx, name):
    r, cols = w.shape
    tr = r // 2

    def body(j_ref, w_ref, o_ref):
        o_ref[0] = w_ref[...].astype(BF16)

    return pl.pallas_call(
        body, name=name,
        grid_spec=pltpu.PrefetchScalarGridSpec(
            num_scalar_prefetch=1, grid=(r // tr,),
            in_specs=[pl.BlockSpec((tr, cols), lambda i, j: (i, 0))],
            out_specs=pl.BlockSpec((1, tr, cols), lambda i, j: (j[0], i, 0))),
        out_shape=_sds((N_SHARD, r, cols), BF16),
        compiler_params=pltpu.CompilerParams(dimension_semantics=("arbitrary",),
                                             vmem_limit_bytes=VMEM_LIMIT),
    )(shard_idx, w)


class Exchange:
    def __init__(self, inputs, out_shape, aliases, sems, phases):
        self.inputs, self.out_shape, self.aliases = list(inputs), list(out_shape), dict(aliases)
        self.sems, self.phases = list(sems), list(phases)


def run_exchange(ex, name):
    def body(*refs):
        n_in, n_out = len(ex.inputs), len(ex.out_shape)
        for _, fn in ex.phases:
            fn(refs[:n_in], refs[n_in:n_in + n_out], refs[n_in + n_out:])

    return _call(body, name=name, in_specs=[_ANY] * len(ex.inputs), out_specs=[_ANY] * len(ex.out_shape),
                 out_shape=ex.out_shape, scratch=ex.sems, input_output_aliases=ex.aliases)(*ex.inputs)


def merge_exchanges(exs):
    inputs, out_shape, sems, aliases, phases, out_slices = [], [], [], {}, [], []
    for ex in exs:
        i0, o0, s0 = len(inputs), len(out_shape), len(sems)
        inputs += ex.inputs
        out_shape += ex.out_shape
        sems += ex.sems
        for k, m in ex.aliases.items():
            aliases[i0 + k] = o0 + m
        si, so, ss = slice(i0, len(inputs)), slice(o0, len(out_shape)), slice(s0, len(sems))
        out_slices.append(so)
        for step, fn in ex.phases:
            phases.append((step, lambda ins, outs, sm, fn=fn, si=si, so=so, ss=ss: fn(ins[si], outs[so], sm[ss])))
    return Exchange(inputs, out_shape, aliases, sems, phases), out_slices


def _dma_sems(*sizes):
    return [pltpu.SemaphoreType.DMA((s,)) for s in sizes]


def gather_exchange(bufs, small=None, relay_step=-2):
    n = len(bufs)
    n_small = 0 if small is None else 1

    def half(outs, a, blk, hc):
        rh = bufs[a].shape[1] // 2
        return outs[a].at[blk, pl.ds(hc * rh, rh), :]

    def ici(outs, sems, a, k, blk, to):
        return pltpu.make_async_remote_copy(
            src_ref=half(outs, a, blk, to[2]), dst_ref=half(outs, a, blk, to[2]), send_sem=sems[0].at[3 * a + k],
            recv_sem=sems[1].at[3 * a + k], device_id=to, device_id_type=MESH)

    def d2d(outs, sems, a, k, blk, hc, to):
        return pltpu.make_async_remote_copy(
            src_ref=half(outs, a, blk, hc), dst_ref=half(outs, a, blk, hc), send_sem=sems[2].at[3 * a + k],
            recv_sem=sems[3].at[3 * a + k], device_id=to, device_id_type=MESH)

    def small_copy(ins, outs, sems, k, blk, to):
        return pltpu.make_async_remote_copy(
            src_ref=ins[n], dst_ref=outs[n].at[blk], send_sem=sems[0].at[3 * n + k],
            recv_sem=sems[1].at[3 * n + k], device_id=to, device_id_type=MESH)

    def start(ins, outs, sems):
        x, y, c = _place()
        j = 2 * x + y
        if n_small:
            pltpu.make_async_copy(ins[n], outs[n].at[j], sems[4].at[0]).start()
        for k, (px, py) in enumerate(_other_chips(x, y)):
            if n_small:
                small_copy(ins, outs, sems, k, j, (px, py, c)).start()
            for a in range(n):
                ici(outs, sems, a, k, j, (px, py, c)).start()

    def relay(ins, outs, sems):
        x, y, c = _place()
        for k, (px, py) in enumerate(_other_chips(x, y)):
            for a in range(n):
                ici(outs, sems, a, k, 2 * px + py, (px, py, c)).wait_recv()
                d2d(outs, sems, a, k, 2 * px + py, c, (x, y, 1 - c)).start()

    def finish(ins, outs, sems):
        x, y, c = _place()
        j = 2 * x + y
        for k, (px, py) in enumerate(_other_chips(x, y)):
            blk = 2 * px + py
            if n_small:
                small_copy(ins, outs, sems, k, blk, (px, py, c)).wait_recv()
                small_copy(ins, outs, sems, k, j, (px, py, c)).wait_send()
            for a in range(n):
                d2d(outs, sems, a, k, blk, 1 - c, (x, y, 1 - c)).wait_recv()
                ici(outs, sems, a, k, j, (px, py, c)).wait_send()
                d2d(outs, sems, a, k, blk, c, (x, y, 1 - c)).wait_send()
        if n_small:
            pltpu.make_async_copy(ins[n], outs[n].at[j], sems[4].at[0]).wait()

    out_shape = [_sds(b.shape, b.dtype) for b in bufs]
    if n_small:
        out_shape.append(_sds((N_SHARD,) + small.shape, small.dtype))
    return Exchange(list(bufs) + ([small] if n_small else []), out_shape, {a: a for a in range(n)},
                    _dma_sems(3 * n + 3, 3 * n + 3, 3 * n, 3 * n, 1),
                    [(0, start), (relay_step, relay), (-1, finish)])


def _start_then_wait(copies):
    def start(ins, outs, sems):
        for sent, _ in copies(ins, outs, sems):
            sent().start()

    def finish(ins, outs, sems):
        pairs = copies(ins, outs, sems)
        for _, arrival in pairs:
            arrival().wait_recv()
        for sent, _ in pairs:
            sent().wait_send()

    return [(0, start), (-1, finish)]


def pair_exchange(arrs):
    n = len(arrs)

    def copies(ins, outs, sems):
        x, y, c = _place()
        res = []
        for a in range(n):
            def mk(a=a):
                rh = arrs[a].shape[1] // 2
                return pltpu.make_async_remote_copy(
                    src_ref=ins[a].at[:, pl.ds((1 - c) * rh, rh), :], dst_ref=outs[a], send_sem=sems[0].at[a],
                    recv_sem=sems[1].at[a], device_id=(x, y, 1 - c), device_id_type=MESH)
            res.append((mk, mk))
        return res

    return Exchange(arrs, [_sds((a.shape[0], a.shape[1] // 2, a.shape[2]), a.dtype) for a in arrs], {},
                    _dma_sems(n, n), _start_then_wait(copies))


def pair_add(g, s, c_idx, name):
    nb, r, cols = g.shape
    rh = r // 2

    def body(c_ref, g_ref, s_ref, o_ref):
        o_ref[...] = (g_ref[...] + s_ref[...]).astype(BF16)

    return pl.pallas_call(
        body, name=name,
        grid_spec=pltpu.PrefetchScalarGridSpec(
            num_scalar_prefetch=1, grid=(nb,),
            in_specs=[pl.BlockSpec((1, rh, cols), lambda j, c: (j, c[0], 0)),
                      pl.BlockSpec((1, rh, cols), lambda j, c: (j, 0, 0))],
            out_specs=pl.BlockSpec((1, rh, cols), lambda j, c: (j, 0, 0))),
        out_shape=_sds((nb, rh, cols), BF16),
        compiler_params=pltpu.CompilerParams(dimension_semantics=("arbitrary",),
                                             vmem_limit_bytes=VMEM_LIMIT),
    )(c_idx, g, s)


def chip_exchange(arrs):
    n = len(arrs)

    def copies(ins, outs, sems):
        x, y, c = _place()
        j = 2 * x + y
        res = []
        for a in range(n):
            for k, (px, py) in enumerate(_other_chips(x, y)):
                def mk(src_blk, dst_blk, a=a, k=k, to=(px, py, c)):
                    return pltpu.make_async_remote_copy(
                        src_ref=ins[a].at[src_blk], dst_ref=outs[a].at[dst_blk], send_sem=sems[0].at[3 * a + k],
                        recv_sem=sems[1].at[3 * a + k], device_id=to, device_id_type=MESH)
                res.append((functools.partial(mk, 2 * px + py, j), functools.partial(mk, j, 2 * px + py)))
        return res

    return Exchange(arrs, [_sds(a.shape, a.dtype) for a in arrs], {}, _dma_sems(3 * n, 3 * n),
                    _start_then_wait(copies))


def sum_chips(r, p, shard_idx, name):
    nb, rh, cols = r.shape
    tr = rh

    def body(j_ref, p_ref, *refs):
        o_ref = refs[nb]
        j = j_ref[0]
        acc = None
        for i in range(nb):
            term = jnp.where(j == i, p_ref[0], refs[i][0]).astype(F32)
            acc = term if acc is None else acc + term
        o_ref[...] = acc

    def slot(i):
        return pl.BlockSpec((1, tr, cols), lambda t, j: (jnp.where(j[0] == i, (i + 1) % nb, i), t, 0))

    return pl.pallas_call(
        body, name=name,
        grid_spec=pltpu.PrefetchScalarGridSpec(
            num_scalar_prefetch=1, grid=(rh // tr,),
            in_specs=[pl.BlockSpec((1, tr, cols), lambda t, j: (j[0], t, 0))] + [slot(i) for i in range(nb)],
            out_specs=pl.BlockSpec((tr, cols), lambda t, j: (t, 0))),
        out_shape=_sds((rh, cols), F32),
        compiler_params=pltpu.CompilerParams(dimension_semantics=("arbitrary",),
                                             vmem_limit_bytes=VMEM_LIMIT),
    )(shard_idx, p, *([r] * nb))


def pair_swap(arrs):
    n = len(arrs)

    def copies(ins, outs, sems):
        x, y, c = _place()
        res = []
        for a in range(n):
            def mk(a=a):
                return pltpu.make_async_remote_copy(
                    src_ref=ins[a], dst_ref=outs[a], send_sem=sems[0].at[a], recv_sem=sems[1].at[a],
                    device_id=(x, y, 1 - c), device_id_type=MESH)
            res.append((mk, mk))
        return res

    return Exchange(arrs, [_sds(a.shape, a.dtype) for a in arrs], {}, _dma_sems(n, n),
                    _start_then_wait(copies))


ADAMW_STEPS_PER_HALF = 4


def adamw_pairs(items, name, comm=None):
    n_items = len(items)
    nh = ADAMW_STEPS_PER_HALF

    def body(*refs):
        ins, outs = refs[:5 * n_items], refs[5 * n_items:]
        mine = (pl.program_id(0) // nh) == lax.axis_index("c")
        for a in range(n_items):
            w_ref, gm_ref, gs_ref, m_ref, v_ref = ins[5 * a:5 * a + 5]
            g_ref, d_ref, mo_ref, vo_ref = outs[4 * a:4 * a + 4]
            g = jnp.where(mine, gm_ref[...], gs_ref[...])
            d, mn, vn = _adamw_math(w_ref[...], g, m_ref[...], v_ref[...])
            g_ref[...] = g
            d_ref[...] = d
            mo_ref[...] = mn
            vo_ref[...] = vn

    in_specs, out_specs, out_shape, args = [], [], [], []
    for w, g_mine, g_sib, m, v in items:
        r, cols = w.shape
        tr = r // (2 * nh)
        full = pl.BlockSpec((tr, cols), lambda i: (i, 0))
        part = pl.BlockSpec((tr, cols), lambda i: (i % nh, 0))
        in_specs += [full, part, part, full, full]
        out_specs += [full] * 4
        out_shape += [_sds((r, cols), F32)] * 4
        args += [w, g_mine, g_sib, m, v]
    res = _call(body, name=name, grid=(2 * nh,), in_specs=in_specs, out_specs=out_specs,
                out_shape=out_shape, comm=comm)(*args)
    own, hosted = (res, None) if comm is None else res
    grouped = [tuple(own[4 * a:4 * a + 4]) for a in range(n_items)]
    return grouped if comm is None else (grouped, hosted)


N_DEV = 8


def device_gather(pack):
    def copies(ins, outs, sems):
        x, y, c = _place()
        me = 4 * x + 2 * y + c
        res = []
        for k in range(1, N_DEV):
            fx, fy, fc = (k >> 2) & 1, (k >> 1) & 1, k & 1
            px, py, pc = (1 - x if fx else x, 1 - y if fy else y, 1 - c if fc else c)

            def mk(slot, k=k, to=(px, py, pc)):
                return pltpu.make_async_remote_copy(
                    src_ref=ins[0], dst_ref=outs[0].at[slot], send_sem=sems[0].at[k - 1],
                    recv_sem=sems[1].at[k - 1], device_id=to, device_id_type=MESH)
            res.append((functools.partial(mk, me), functools.partial(mk, 4 * px + 2 * py + pc)))
        return res

    return Exchange([pack], [_sds((N_DEV,) + pack.shape, pack.dtype)], {}, _dma_sems(N_DEV - 1, N_DEV - 1),
                    _start_then_wait(copies))


def sum_devices(buf, pack, me_idx, name):
    r, cols = pack.shape

    def body(me_ref, p_ref, *refs):
        o_ref = refs[N_DEV]
        acc = None
        for i in range(N_DEV):
            term = jnp.where(me_ref[0] == i, p_ref[...], refs[i][0])
            acc = term if acc is None else acc + term
        o_ref[...] = acc

    def slot(i):
        return pl.BlockSpec((1, r, cols), lambda t, me: (jnp.where(me[0] == i, (i + 1) % N_DEV, i), 0, 0))

    whole = pl.BlockSpec((r, cols), lambda t, me: (0, 0))
    return pl.pallas_call(
        body, name=name,
        grid_spec=pltpu.PrefetchScalarGridSpec(
            num_scalar_prefetch=1, grid=(1,),
            in_specs=[whole] + [slot(i) for i in range(N_DEV)], out_specs=whole),
        out_shape=_sds((r, cols), F32),
        compiler_params=pltpu.CompilerParams(dimension_semantics=("arbitrary",),
                                             vmem_limit_bytes=VMEM_LIMIT),
    )(me_idx, pack, *([buf] * N_DEV))


SMALL_NAMES = ("ffn1_norm", "mix_norm", "ffn2_norm", "final_norm", "sg_ln_g", "sg_ln_b",
               "dn_norm", "a_log", "dt_bias", "sg_b", "sg_w", "conv_w")


def _to_rows(a):
    flat = a.reshape(-1)
    pad = (-flat.shape[0]) % LANES
    if pad:
        flat = jnp.pad(flat, (0, pad))
    return flat.reshape(-1, LANES)


def _pack_small(parts):
    rows = [_to_rows(parts[k]) for k in SMALL_NAMES]
    pack = jnp.concatenate(rows, axis=0)
    pad = (-pack.shape[0]) % 8
    if pad:
        pack = jnp.pad(pack, ((0, pad), (0, 0)))
    return pack


def _unpack_small(pack, shapes):
    out, r0 = {}, 0
    for k in SMALL_NAMES:
        size = 1
        for s in shapes[k]:
            size *= s
        nrows = -(-size // LANES)
        out[k] = pack[r0:r0 + nrows].reshape(-1)[:size].reshape(shapes[k])
        r0 += nrows
    return out


def kernel(x, ffn1_norm, ffn1_w_gate, ffn1_w_up, ffn1_w_down, mix_norm, w_in, conv_w, a_log, dt_bias, dn_norm, sg_ln_g, sg_ln_b, sg_w, sg_b, w_out, ffn2_norm, ffn2_w_gate, ffn2_w_up, ffn2_w_down, final_norm, loss_target, m_ffn1_norm, m_ffn1_w_gate, m_ffn1_w_up, m_ffn1_w_down, m_mix_norm, m_w_in, m_conv_w, m_a_log, m_dt_bias, m_dn_norm, m_sg_ln_g, m_sg_ln_b, m_sg_w, m_sg_b, m_w_out, m_ffn2_norm, m_ffn2_w_gate, m_ffn2_w_up, m_ffn2_w_down, m_final_norm, v_ffn1_norm, v_ffn1_w_gate, v_ffn1_w_up, v_ffn1_w_down, v_mix_norm, v_w_in, v_conv_w, v_a_log, v_dt_bias, v_dn_norm, v_sg_ln_g, v_sg_ln_b, v_sg_w, v_sg_b, v_w_out, v_ffn2_norm, v_ffn2_w_gate, v_ffn2_w_up, v_ffn2_w_down, v_final_norm):
    bsz, t_len, d = x.shape
    n = bsz * t_len
    xy, yy, cc = _place()
    shard = 2 * xy + yy

    big_names = ["ffn1_w_gate", "ffn1_w_up", "ffn1_w_down", "w_in", "w_out",
                 "ffn2_w_gate", "ffn2_w_up", "ffn2_w_down"]
    big_w = dict(ffn1_w_gate=ffn1_w_gate, ffn1_w_up=ffn1_w_up, ffn1_w_down=ffn1_w_down, w_in=w_in,
                 w_out=w_out, ffn2_w_gate=ffn2_w_gate, ffn2_w_up=ffn2_w_up, ffn2_w_down=ffn2_w_down)
    big_m = dict(ffn1_w_gate=m_ffn1_w_gate, ffn1_w_up=m_ffn1_w_up, ffn1_w_down=m_ffn1_w_down, w_in=m_w_in,
                 w_out=m_w_out, ffn2_w_gate=m_ffn2_w_gate, ffn2_w_up=m_ffn2_w_up, ffn2_w_down=m_ffn2_w_down)
    big_v = dict(ffn1_w_gate=v_ffn1_w_gate, ffn1_w_up=v_ffn1_w_up, ffn1_w_down=v_ffn1_w_down, w_in=v_w_in,
                 w_out=v_w_out, ffn2_w_gate=v_ffn2_w_gate, ffn2_w_up=v_ffn2_w_up, ffn2_w_down=v_ffn2_w_down)
    shard_idx = jnp.reshape(shard, (1,)).astype(jnp.int32)
    c_idx = jnp.reshape(cc, (1,)).astype(jnp.int32)
    transposed = ("ffn1_w_gate", "ffn1_w_up", "ffn2_w_gate", "ffn2_w_up")

    def as2d(a, k):
        return a[0].T if k in transposed else a[0]

    def from2d(a, k):
        return a.T[None] if k in transposed else a[None]

    placed = {k: cast_place(as2d(big_w[k], k), shard_idx, name="cast_" + k) for k in big_names}
    first_names = big_names[:3]
    later_names = big_names[3:]
    res = run_exchange(gather_exchange([placed[k] for k in first_names], conv_w[0]), name="gather_first")
    gw = dict(zip(first_names, res[:3]))
    conv_full = res[3].transpose(1, 0, 2).reshape(CONV_K, 3 * HALF_W)

    x0 = x.reshape(n, d)
    def ffn_weights(prefix):
        return [gw[prefix + k].reshape(-1, d) for k in ("_w_gate", "_w_up", "_w_down")]

    def ffn_grad_blocks(grads):
        return [g.reshape(N_SHARD, -1, d) for g in grads]

    (x1, h1, gate1, up1, act1), later = ffn_fwd(
        x0, ffn1_norm, *ffn_weights("ffn1"), name="ffn1_fwd",
        comm=gather_exchange([placed[k] for k in later_names]))
    gw.update(zip(later_names, later))
    w_in_full = gw["w_in"].transpose(1, 0, 2).reshape(d, IN_COLS)
    w_in_full = jnp.pad(w_in_full, ((0, 0), (0, PROJ_W - IN_COLS)))
    w_out_full = gw["w_out"].reshape(2 * HALF_W, d)
    proj, h2 = in_proj_fwd(x1, mix_norm, w_in_full, name="in_proj_fwd")
    proj3 = proj.reshape(bsz, t_len, PROJ_W)
    bias_tile = jnp.repeat(sg_b[0].T, SG_GROUP_DIM, axis=1)
    sg_out = sg_fwd(proj, sg_ln_g, sg_ln_b, sg_w[0], bias_tile, name="sg_fwd")
    qkv = dn_conv_fwd(proj3, conv_full, name="dn_conv_fwd")
    alog_row = jnp.zeros((1, LANES), F32).at[0, N_HEADS:2 * N_HEADS].set(a_log[0])
    dtb_row = jnp.zeros((1, LANES), F32).at[0, N_HEADS:2 * N_HEADS].set(dt_bias[0])
    u_wy, w_wy, q_dec, k_dec, qk, tinv, gc = dn_chunk_fwd(qkv, proj3, alog_row, dtb_row,
                                                           name="dn_chunk_fwd")
    o, s_in = dn_scan_fwd(u_wy, w_wy, q_dec, k_dec, qk, gc, name="dn_scan_fwd")
    dn_out = dn_out_fwd(o.reshape(n, HALF_W), proj, dn_norm, name="dn_out_fwd")
    x2 = out_proj_fwd(x1, sg_out, dn_out, w_out_full, name="out_proj_fwd")
    x3, h3, gate2, up2, act2 = ffn_fwd(x2, ffn2_norm, *ffn_weights("ffn2"), name="ffn2_fwd")
    dx3, d_final_norm, loss_tile = final_loss(x3, final_norm.reshape(1, d),
                                              loss_target.reshape(n, d), name="final_loss")
    loss = lax.psum(loss_tile[0, 0], ("x", "y", "c"))

    dx2, dgate2, dup2, dyh2, d_ffn2_norm = ffn_bwd_act(
        dx3, x2, ffn2_norm, gate2, up2, *ffn_weights("ffn2"), name="ffn2_bwd_act")
    g_big = {}
    g_big["ffn2_w_gate"], g_big["ffn2_w_up"], g_big["ffn2_w_down"] = ffn_grad_blocks(ffn_bwd_w(
        h3, dyh2, dgate2, dup2, act2, name="ffn2_bwd_w"))

    early = ["ffn2_w_gate", "ffn2_w_up", "ffn2_w_down"]
    (d_sg, d_dn, dx2b), early_sib = out_proj_bwd_x(dx2, w_out_full, name="out_proj_bwd_x",
                                                   comm=pair_exchange([g_big[k] for k in early]))
    early_sums = [pair_add(g_big[k], s, c_idx, name="grad_pair_add_" + k) for k, s in zip(early, early_sib)]
    g_w_out = jnp.concatenate([matmul_tn(sg_out, dx2b, d, name="w_out_grad_sg"),
                               matmul_tn(dn_out, dx2b, d, name="w_out_grad_dn")], axis=0)
    g_big["w_out"] = g_w_out.reshape(N_SHARD, (2 * HALF_W) // N_SHARD, d)

    d_proj, d_sg_w, d_bias_tile, d_ln_g, d_ln_b = sg_bwd(d_sg, proj, sg_ln_g, sg_ln_b, sg_w[0],
                                                         bias_tile, name="sg_bwd")
    d_o, d_proj, d_dn_norm = dn_out_bwd(d_dn, o.reshape(n, HALF_W), proj, dn_norm, d_proj,
                                        name="dn_out_bwd")
    du, dw, dqd, dkd, dqk, dgc_scan = dn_scan_bwd(d_o.reshape(bsz, t_len, HALF_W), u_wy, w_wy, q_dec,
                                                  k_dec, qk, gc, s_in, name="dn_scan_bwd")
    (d_qkv, d_proj3, d_alog_row, d_dtb_row), early_chips = dn_chunk_bwd(
        qkv, proj3, alog_row, dtb_row, tinv, u_wy, w_wy, du, dw, dqd, dkd, dqk, dgc_scan,
        d_proj.reshape(bsz, t_len, PROJ_W), name="dn_chunk_bwd", comm=chip_exchange(early_sums))
    early_halves = [sum_chips(r, p, shard_idx, name="grad_chip_sum_" + k)
                    for k, r, p in zip(early, early_chips, early_sums)]
    (d_proj3, d_conv), early_sib_halves = dn_conv_bwd(d_qkv, proj3, conv_full, d_proj3, name="dn_conv_bwd",
                                                      comm=pair_swap(early_halves))
    d_proj = d_proj3.reshape(n, PROJ_W)
    g_w_in = matmul_tn(h2, d_proj, 640, name="w_in_grad")[:, :IN_COLS]
    g_big["w_in"] = g_w_in.reshape(d, N_SHARD, IN_COLS // N_SHARD).transpose(1, 0, 2)

    mid = ["w_in", "w_out"]
    (dx1, d_mix_norm), mid_sib = in_proj_bwd_x(d_proj, w_in_full, x1, mix_norm, dx2, name="in_proj_bwd_x",
                                               comm=pair_exchange([g_big[k] for k in mid]))
    mid_sums = [pair_add(g_big[k], s, c_idx, name="grad_pair_add_" + k) for k, s in zip(mid, mid_sib)]
    dx0, dgate1, dup1, dyh1, d_ffn1_norm = ffn_bwd_act(
        dx1, x0, ffn1_norm, gate1, up1, *ffn_weights("ffn1"), name="ffn1_bwd_act")
    d_sg_b = d_bias_tile.reshape(SG_CHUNK, SG_GROUPS, SG_GROUP_DIM).sum(axis=-1).T
    small_g = dict(ffn1_norm=d_ffn1_norm, mix_norm=d_mix_norm, ffn2_norm=d_ffn2_norm,
                   final_norm=d_final_norm, sg_ln_g=d_ln_g, sg_ln_b=d_ln_b, dn_norm=d_dn_norm,
                   a_log=d_alog_row[:, N_HEADS:2 * N_HEADS], dt_bias=d_dtb_row[:, N_HEADS:2 * N_HEADS],
                   sg_b=d_sg_b, sg_w=d_sg_w, conv_w=d_conv)
    my_pack = _pack_small(small_g)
    hosted, parts = merge_exchanges([chip_exchange(mid_sums), device_gather(my_pack)])
    f1_grads, hosted_res = ffn_bwd_w(h1, dyh1, dgate1, dup1, act1, name="ffn1_bwd_w", comm=hosted)
    g_big["ffn1_w_gate"], g_big["ffn1_w_up"], g_big["ffn1_w_down"] = ffn_grad_blocks(f1_grads)
    mid_chips, (all_packs,) = hosted_res[parts[0]], hosted_res[parts[1]]
    mid_halves = [sum_chips(r, p, shard_idx, name="grad_chip_sum_" + k)
                  for k, r, p in zip(mid, mid_chips, mid_sums)]
    grad_x = dx0.reshape(bsz, t_len, d)

    late = [k for k in big_names if k not in early and k not in mid]
    g_list = [g_big[k] for k in late]
    first_leg, legs = merge_exchanges([pair_exchange(g_list), pair_swap(mid_halves)])
    leg_res = run_exchange(first_leg, name="grad_pair_exchange")
    from_sibling, mid_sib_halves = leg_res[legs[0]], leg_res[legs[1]]
    pair_sums = [pair_add(g, s, c_idx, name="grad_pair_add_" + k)
                 for k, g, s in zip(late, g_list, from_sibling)]

    def adam_items(names, mine, sib):
        return [(as2d(big_w[k], k), gm, gs, as2d(big_m[k], k), as2d(big_v[k], k))
                for k, gm, gs in zip(names, mine, sib)]

    outs = {}
    done = adamw_pairs(
        adam_items(early + mid, early_halves + mid_halves, list(early_sib_halves) + list(mid_sib_halves)),
        name="adamw_early")
    from_chips = run_exchange(chip_exchange(pair_sums), name="grad_chip_exchange")
    halves = [sum_chips(r, p, shard_idx, name="grad_chip_sum_" + k)
              for k, r, p in zip(late, from_chips, pair_sums)]
    sib_halves = run_exchange(pair_swap(halves), name="grad_pair_swap")
    done += adamw_pairs(adam_items(late, halves, sib_halves), name="adamw_late")
    for k, res in zip(early + mid + late, done):
        outs[k] = tuple(from2d(a, k) for a in res)

    small_w = dict(ffn1_norm=ffn1_norm, mix_norm=mix_norm, ffn2_norm=ffn2_norm, final_norm=final_norm,
                   sg_ln_g=sg_ln_g, sg_ln_b=sg_ln_b, dn_norm=dn_norm, a_log=a_log, dt_bias=dt_bias,
                   sg_b=sg_b, sg_w=sg_w)
    small_m = dict(ffn1_norm=m_ffn1_norm, mix_norm=m_mix_norm, ffn2_norm=m_ffn2_norm,
                   final_norm=m_final_norm, sg_ln_g=m_sg_ln_g, sg_ln_b=m_sg_ln_b, dn_norm=m_dn_norm,
                   a_log=m_a_log, dt_bias=m_dt_bias, sg_b=m_sg_b, sg_w=m_sg_w)
    small_v = dict(ffn1_norm=v_ffn1_norm, mix_norm=v_mix_norm, ffn2_norm=v_ffn2_norm,
                   final_norm=v_final_norm, sg_ln_g=v_sg_ln_g, sg_ln_b=v_sg_ln_b, dn_norm=v_dn_norm,
                   a_log=v_a_log, dt_bias=v_dt_bias, sg_b=v_sg_b, sg_w=v_sg_w)
    shapes = {k: small_w[k].shape for k in small_w}
    shapes["conv_w"] = (CONV_K, 3 * HALF_W)
    me_idx = jnp.reshape(4 * xy + 2 * yy + cc, (1,)).astype(jnp.int32)
    g_pack = sum_devices(all_packs, my_pack, me_idx, name="small_sum")
    g_small = _unpack_small(g_pack, shapes)
    cw = 3 * HALF_W // N_SHARD
    g_conv = lax.dynamic_slice_in_dim(g_small["conv_w"], shard * cw, cw, axis=1)
    zero_conv = jnp.zeros((CONV_K, 3 * HALF_W), F32)

    def packed(src, conv):
        parts = dict(src)
        parts["conv_w"] = lax.dynamic_update_slice_in_dim(zero_conv, conv[0], shard * cw, axis=1)
        return _pack_small(parts)

    d_pack, m_pack, v_pack = adamw(packed(small_w, conv_w), g_pack, packed(small_m, m_conv_w),
                                   packed(small_v, v_conv_w), name="adamw_small")
    d_small = _unpack_small(d_pack, shapes)
    m_small = _unpack_small(m_pack, shapes)
    v_small = _unpack_small(v_pack, shapes)

    def conv_block(full_arr):
        return lax.dynamic_slice_in_dim(full_arr, shard * cw, cw, axis=1)[None]

    for k in small_w:
        outs[k] = (g_small[k].reshape(small_w[k].shape), d_small[k], m_small[k], v_small[k])
    outs["conv_w"] = (g_conv[None], conv_block(d_small["conv_w"]), conv_block(m_small["conv_w"]),
                      conv_block(v_small["conv_w"]))

    order = ["ffn1_norm", "ffn1_w_gate", "ffn1_w_up", "ffn1_w_down", "mix_norm", "w_in", "conv_w",
             "a_log", "dt_bias", "dn_norm", "sg_ln_g", "sg_ln_b", "sg_w", "sg_b", "w_out", "ffn2_norm",
             "ffn2_w_gate", "ffn2_w_up", "ffn2_w_down", "final_norm"]
    return (loss, grad_x, *[outs[k][0] for k in order], *[outs[k][1] for k in order],
            *[outs[k][2] for k in order], *[outs[k][3] for k in order])
```

```python
import functools

import jax
import jax.numpy as jnp
from jax import lax
from jax.experimental import pallas as pl
from jax.experimental.pallas import tpu as pltpu

F32 = jnp.float32
BF16 = jnp.bfloat16
EPS = 1e-6

D_MODEL = 1024
N_SHARD = 4
HEAD_DIM = 128
N_HEADS = 4
DN_CHUNK = 64
SG_CHUNK = 128
SG_GROUPS = 8
SG_GROUP_DIM = 64
HALF_W = 512
PROJ_W = 3200
IN_COLS = 3080
GATE_COL_BLOCK = 24
QK_SCALE = HEAD_DIM ** -0.5
LANES = 128

ADAM_LR = 0.001
ADAM_B1 = 0.9
ADAM_B2 = 0.999
ADAM_EPS = 1e-08
ADAM_WD = 0.01
ADAM_STEP = 10

VMEM_LIMIT = 56 * 1024 * 1024
ROW_TILE = 512

NN = ((1,), (0,))
NT = ((1,), (1,))
TN = ((0,), (0,))
MESH = pl.DeviceIdType.MESH


def _dot(a, b, dims):
    return lax.dot_general(a, b, (dims, ((), ())), preferred_element_type=F32)


def _bdot(a, b, dims):
    return _dot(a.astype(BF16), b.astype(BF16), dims)


def _split(a):
    hi = a.astype(BF16)
    lo = (a - hi.astype(F32)).astype(BF16)
    return hi, lo


def _dot3(a, b, dims=NN):
    return _dot(a[0], b[0], dims) + (_dot(a[0], b[1], dims) + _dot(a[1], b[0], dims))


def _dot_exact_lhs(a, b):
    ab = a.astype(BF16)
    b1 = b.astype(BF16)
    r1 = b - b1.astype(F32)
    b2 = r1.astype(BF16)
    b3 = (r1 - b2.astype(F32)).astype(BF16)
    return _dot(ab, b1, NN) + (_dot(ab, b2, NN) + _dot(ab, b3, NN))


def _call(body, *, name, out_shape, in_specs, out_specs, grid=(), scratch=(), comm=None, **kw):
    params = dict(vmem_limit_bytes=VMEM_LIMIT)
    if grid:
        params["dimension_semantics"] = ("arbitrary",) * len(grid)
    if comm is None:
        return pl.pallas_call(
            body, name=name, grid=grid, in_specs=in_specs, out_specs=out_specs,
            out_shape=out_shape, scratch_shapes=list(scratch),
            compiler_params=pltpu.CompilerParams(**params), **kw)

    n_in, n_out, n_sc = len(in_specs), len(out_specs), len(scratch)
    c_in, c_out = len(comm.inputs), len(comm.out_shape)
    steps = 1
    for g in grid:
        steps *= g

    def hosted(*refs):
        ins, cins = refs[:n_in], refs[n_in:n_in + c_in]
        o0 = n_in + c_in
        outs, couts = refs[o0:o0 + n_out], refs[o0 + n_out:o0 + n_out + c_out]
        s0 = o0 + n_out + c_out
        sc, csems = refs[s0:s0 + n_sc], refs[s0 + n_sc:]
        lin = 0
        for axis, g in enumerate(grid):
            lin = lin * g + pl.program_id(axis)

        def at(step, fn):
            @pl.when(lin == step % steps)
            def _():
                fn(cins, couts, csems)

        for step, fn in comm.phases:
            if step >= 0:
                at(step, fn)
        body(*ins, *outs, *sc)
        for step, fn in comm.phases:
            if step < 0:
                at(step, fn)

    aliases = dict(kw.pop("input_output_aliases", {}))
    for k, m in comm.aliases.items():
        aliases[n_in + k] = n_out + m
    call = pl.pallas_call(
        hosted, name=name, grid=grid, in_specs=list(in_specs) + [_ANY] * c_in,
        out_specs=list(out_specs) + [_ANY] * c_out, out_shape=list(out_shape) + comm.out_shape,
        scratch_shapes=list(scratch) + comm.sems, input_output_aliases=aliases,
        compiler_params=pltpu.CompilerParams(**params), **kw)

    def run(*args):
        res = call(*args, *comm.inputs)
        return res[:n_out], res[n_out:]

    return run


def _sds(shape, dtype):
    return jax.ShapeDtypeStruct(tuple(shape), dtype)


def _resident(shape):
    zeros = (0,) * len(shape)
    return pl.BlockSpec(tuple(shape), lambda *_: zeros, pipeline_mode=pl.Buffered(1))


def _sigmoid(x):
    return jax.nn.sigmoid(x)


def _softplus(x):
    return jnp.maximum(x, 0.0) + jnp.log(1.0 + jnp.exp(-jnp.abs(x)))


_GELU_C = 0.7978845608028654
_GELU_A = 0.044715


def _gelu(x):
    t = jnp.tanh(_GELU_C * (x + _GELU_A * x * x * x))
    return 0.5 * x * (1.0 + t)


def _gelu_grad(x):
    t = jnp.tanh(_GELU_C * (x + _GELU_A * x * x * x))
    return 0.5 * (1.0 + t) + 0.5 * x * (1.0 - t * t) * _GELU_C * (1.0 + 3.0 * _GELU_A * x * x)


def _silu_grad(x):
    s = _sigmoid(x)
    return s * (1.0 + x * (1.0 - s))


def _rms_scale(xv):
    return lax.rsqrt(jnp.mean(xv * xv, axis=-1, keepdims=True) + EPS)


def _rms_bwd(dh, xv, g):
    r = _rms_scale(xv)
    xn = xv * r
    dg = jnp.sum(dh * xn, axis=0, keepdims=True)
    dxn = dh * g
    dx = r * (dxn - xn * jnp.mean(dxn * xn, axis=-1, keepdims=True))
    return dx, dg


def _iota2(shape, dim):
    return lax.broadcasted_iota(jnp.int32, shape, dim)


def _col_to_row(col):
    n = col.shape[0]
    eye = _iota2((n, n), 0) == _iota2((n, n), 1)
    return jnp.sum(jnp.where(eye, col, 0.0), axis=0, keepdims=True)


def _row_to_col(row):
    n = row.shape[1]
    eye = _iota2((n, n), 0) == _iota2((n, n), 1)
    return jnp.sum(jnp.where(eye, row, 0.0), axis=1, keepdims=True)


MXU_DIM = 256


def _hidden_chunks(f, step=3 * MXU_DIM):
    return [(c0, min(c0 + step, f)) for c0 in range(0, f, step)]

def ffn_fwd(x, gnorm, wg, wu, wd, name, comm=None):
    n, d = x.shape
    f = wg.shape[0]
    tm = min(ROW_TILE, n)

    def body(x_ref, g_ref, wg_ref, wu_ref, wd_ref, xo_ref, h_ref, gate_ref, up_ref, act_ref, acc_ref):
        xv = x_ref[...]
        h = (xv * _rms_scale(xv) * g_ref[...]).astype(BF16)
        h_ref[...] = h
        for c0, c1 in _hidden_chunks(f):
            gate = _dot(h, wg_ref[c0:c1, :], NT)
            up = _dot(h, wu_ref[c0:c1, :], NT)
            act = (gate * _sigmoid(gate) * up).astype(BF16)
            gate_ref[:, c0:c1] = gate.astype(BF16)
            up_ref[:, c0:c1] = up.astype(BF16)
            act_ref[:, c0:c1] = act
            part = _dot(act, wd_ref[c0:c1, :], NN)
            if c0 == 0:
                acc_ref[...] = part
            else:
                acc_ref[...] += part
        xo_ref[...] = xv + 0.5 * acc_ref[...]

    row = pl.BlockSpec((tm, d), lambda i: (i, 0))
    wide = pl.BlockSpec((tm, f), lambda i: (i, 0))
    return _call(
        body, name=name, grid=(n // tm,),
        in_specs=[row, pl.BlockSpec((1, d), lambda i: (0, 0))] + [_resident((f, d))] * 3,
        out_specs=[row, row, wide, wide, wide],
        out_shape=[_sds((n, d), F32), _sds((n, d), BF16)] + [_sds((n, f), BF16)] * 3,
        scratch=[pltpu.VMEM((tm, d), F32)], comm=comm,
    )(x, gnorm, wg, wu, wd)


def ffn_bwd_act(dy, x, gnorm, gate, up, wg, wu, wd, name, comm=None):
    n, d = x.shape
    f = wg.shape[0]
    tm = min(ROW_TILE // 2, n)

    def body(dy_ref, x_ref, g_ref, gate_ref, up_ref, wg_ref, wu_ref, wd_ref,
             dx_ref, dgate_ref, dup_ref, dyh_ref, dg_ref, acc_ref):
        @pl.when(pl.program_id(0) == 0)
        def _():
            dg_ref[...] = jnp.zeros_like(dg_ref)

        dyh = (0.5 * dy_ref[...]).astype(BF16)
        dyh_ref[...] = dyh
        for c0, c1 in _hidden_chunks(f):
            dact = _dot(dyh, wd_ref[c0:c1, :], NT)
            gt = gate_ref[:, c0:c1].astype(F32)
            u = up_ref[:, c0:c1].astype(F32)
            s = _sigmoid(gt)
            dup = (dact * (gt * s)).astype(BF16)
            dgate = (dact * u * (s * (1.0 + gt * (1.0 - s)))).astype(BF16)
            dup_ref[:, c0:c1] = dup
            dgate_ref[:, c0:c1] = dgate
            part = _dot(dgate, wg_ref[c0:c1, :], NN) + _dot(dup, wu_ref[c0:c1, :], NN)
            if c0 == 0:
                acc_ref[...] = part
            else:
                acc_ref[...] += part
        dxn, dg = _rms_bwd(acc_ref[...], x_ref[...], g_ref[...])
        dx_ref[...] = dy_ref[...] + dxn
        dg_ref[...] += dg

    row = pl.BlockSpec((tm, d), lambda i: (i, 0))
    wide = pl.BlockSpec((tm, f), lambda i: (i, 0))
    vec = pl.BlockSpec((1, d), lambda i: (0, 0))
    wres = _resident((f, d))
    return _call(
        body, name=name, grid=(n // tm,),
        in_specs=[row, row, vec, wide, wide, wres, wres, wres],
        out_specs=[row, wide, wide, row, vec],
        out_shape=[_sds((n, d), F32), _sds((n, f), BF16), _sds((n, f), BF16),
                   _sds((n, d), BF16), _sds((1, d), F32)],
        scratch=[pltpu.VMEM((tm, d), F32)], comm=comm,
    )(dy, x, gnorm, gate, up, wg, wu, wd)


def ffn_bwd_w(wide, rows, pairs, name, comm=None):
    n, d = rows[0].shape
    f = wide[0].shape[1]
    fh = f // 2
    tk = min(ROW_TILE, n)
    n_w, n_r = len(wide), len(rows)

    def body(*refs):
        wide_refs, row_refs, outs = refs[:n_w], refs[n_w:n_w + n_r], refs[n_w + n_r:]

        @pl.when(pl.program_id(1) == 0)
        def _():
            for o_ref in outs:
                o_ref[...] = jnp.zeros_like(o_ref)

        row_vals = [r[...] for r in row_refs]
        for c0, c1 in _hidden_chunks(fh, 2 * MXU_DIM):
            for (i, k), o_ref in zip(pairs, outs):
                o_ref[c0:c1, :] += _dot(wide_refs[i][:, c0:c1], row_vals[k], TN)

    row = pl.BlockSpec((tk, d), lambda j, k: (k, 0))
    blk = pl.BlockSpec((tk, fh), lambda j, k: (k, j))
    return _call(
        body, name=name, grid=(2, n // tk),
        in_specs=[blk] * n_w + [row] * n_r,
        out_specs=[pl.BlockSpec((fh, d), lambda j, k: (j, 0))] * len(pairs),
        out_shape=[_sds((f, d), F32)] * len(pairs), comm=comm,
    )(*wide, *rows)


def final_loss(x, gnorm, target, name):
    n, d = x.shape
    tm = min(ROW_TILE, n)

    def body(x_ref, g_ref, t_ref, dx_ref, dg_ref, loss_ref):
        @pl.when(pl.program_id(0) == 0)
        def _():
            dg_ref[...] = jnp.zeros_like(dg_ref)
            loss_ref[...] = jnp.zeros_like(loss_ref)

        xv = x_ref[...]
        y = xv * _rms_scale(xv) * g_ref[...]
        err = y - t_ref[...]
        part = 0.5 * jnp.sum(jnp.mean(err * err, axis=-1, keepdims=True), axis=0, keepdims=True)
        loss_ref[...] += jnp.broadcast_to(part, loss_ref.shape)
        dx, dg = _rms_bwd(err * (1.0 / d), xv, g_ref[...])
        dx_ref[...] = dx
        dg_ref[...] += dg

    row = pl.BlockSpec((tm, d), lambda i: (i, 0))
    vec = pl.BlockSpec((1, d), lambda i: (0, 0))
    return _call(
        body, name=name, grid=(n // tm,),
        in_specs=[row, vec, row],
        out_specs=[row, vec, pl.BlockSpec((1, LANES), lambda i: (0, 0))],
        out_shape=[_sds((n, d), F32), _sds((1, d), F32), _sds((1, LANES), F32)],
    )(x, gnorm, target)


def in_proj_fwd(x, gnorm, w, name):
    n, d = x.shape
    cols = w.shape[1]
    tm = min(ROW_TILE, n)

    def body(x_ref, g_ref, w_ref, p_ref, h_ref):
        xv = x_ref[...]
        h = (xv * _rms_scale(xv) * g_ref[...]).astype(BF16)
        h_ref[...] = h
        for c0, c1 in _hidden_chunks(cols):
            p_ref[:, c0:c1] = _dot(h, w_ref[:, c0:c1], NN)

    return _call(
        body, name=name, grid=(n // tm,),
        in_specs=[pl.BlockSpec((tm, d), lambda i: (i, 0)),
                  pl.BlockSpec((1, d), lambda i: (0, 0)), _resident((d, cols))],
        out_specs=[pl.BlockSpec((tm, cols), lambda i: (i, 0)),
                   pl.BlockSpec((tm, d), lambda i: (i, 0))],
        out_shape=[_sds((n, cols), F32), _sds((n, d), BF16)],
    )(x, gnorm, w)


def in_proj_bwd_x(dproj, w, x, gnorm, dres, name, comm=None):
    n, d = x.shape
    cols = w.shape[1]
    tm = min(ROW_TILE, n)

    def body(dp_ref, w_ref, x_ref, g_ref, dr_ref, dx_ref, dg_ref, dxh_ref):
        @pl.when(pl.program_id(0) == 0)
        def _():
            dg_ref[...] = jnp.zeros_like(dg_ref)

        dh = _dot(dp_ref[...], w_ref[...], NT)
        dxn, dg = _rms_bwd(dh, x_ref[...], g_ref[...])
        dx = dr_ref[...] + dxn
        dx_ref[...] = dx
        dxh_ref[...] = (0.5 * dx).astype(BF16)
        dg_ref[...] += dg

    row = pl.BlockSpec((tm, d), lambda i: (i, 0))
    vec = pl.BlockSpec((1, d), lambda i: (0, 0))
    return _call(
        body, name=name, grid=(n // tm,),
        in_specs=[pl.BlockSpec((tm, cols), lambda i: (i, 0)),
                  _resident((d, cols)), row, vec, row],
        out_specs=[row, vec, row],
        out_shape=[_sds((n, d), F32), _sds((1, d), F32), _sds((n, d), BF16)], comm=comm,
    )(dproj, w, x, gnorm, dres)


def matmul_tn(a_list, b, tn, name):
    n, cb = b.shape
    widths = [a.shape[1] for a in a_list]
    tk = min(ROW_TILE, n)

    def body(*refs):
        a_refs, b_ref, o_ref = refs[:-2], refs[-2], refs[-1]

        @pl.when(pl.program_id(0) == 0)
        def _():
            o_ref[...] = jnp.zeros_like(o_ref)

        r0 = 0
        for a_ref, ka in zip(a_refs, widths):
            av = a_ref[...]
            for c0, c1 in _hidden_chunks(cb, tn):
                o_ref[r0:r0 + ka, c0:c1] += _dot(av, b_ref[:, c0:c1], TN)
            r0 += ka

    return _call(
        body, name=name, grid=(n // tk,),
        in_specs=[pl.BlockSpec((tk, ka), lambda k: (k, 0)) for ka in widths]
        + [pl.BlockSpec((tk, cb), lambda k: (k, 0))],
        out_specs=pl.BlockSpec((sum(widths), cb), lambda k: (0, 0)),
        out_shape=_sds((sum(widths), cb), F32),
    )(*a_list, b)


def out_proj_fwd(x, sg_out, dn_out, w, name):
    n, d = x.shape
    tm = min(ROW_TILE, n)

    def body(x_ref, a_ref, b_ref, w_ref, o_ref):
        o_ref[...] = (x_ref[...] + _dot(a_ref[...], w_ref[0:HALF_W, :], NN)
                      + _dot(b_ref[...], w_ref[HALF_W:2 * HALF_W, :], NN))

    row = pl.BlockSpec((tm, d), lambda i: (i, 0))
    half = pl.BlockSpec((tm, HALF_W), lambda i: (i, 0))
    return _call(
        body, name=name, grid=(n // tm,),
        in_specs=[row, half, half, pl.BlockSpec((2 * HALF_W, d), lambda i: (0, 0))],
        out_specs=row, out_shape=_sds((n, d), F32),
    )(x, sg_out, dn_out, w)


def out_proj_bwd_x(dy, w, name, comm=None):
    n, d = dy.shape
    tm = min(ROW_TILE, n)

    def body(dy_ref, w_ref, dsg_ref, ddn_ref, dyb_ref):
        dyb = dy_ref[...].astype(BF16)
        dyb_ref[...] = dyb
        dsg_ref[...] = _dot(dyb, w_ref[0:HALF_W, :], NT)
        ddn_ref[...] = _dot(dyb, w_ref[HALF_W:2 * HALF_W, :], NT)

    row = pl.BlockSpec((tm, d), lambda i: (i, 0))
    half = pl.BlockSpec((tm, HALF_W), lambda i: (i, 0))
    return _call(
        body, name=name, grid=(n // tm,),
        in_specs=[row, pl.BlockSpec((2 * HALF_W, d), lambda i: (0, 0))],
        out_specs=[half, half, row],
        out_shape=[_sds((n, HALF_W), F32), _sds((n, HALF_W), F32), _sds((n, d), BF16)], comm=comm,
    )(dy, w)


SG_PAIRS = SG_GROUPS // 2


def _sg_low_half():
    return _iota2((SG_CHUNK, LANES), 1) < SG_GROUP_DIM


def _sg_pair_cols(p):
    return slice(p * LANES, (p + 1) * LANES)


def _sg_causal():
    return _iota2((SG_CHUNK, SG_CHUNK), 0) >= _iota2((SG_CHUNK, SG_CHUNK), 1)


def _sg_forward_chunk(pu, pv, ln_g, ln_b, wc, bias, low):
    u = _gelu(pu)
    v = _gelu(pv)
    mu = jnp.mean(v, axis=-1, keepdims=True)
    vc = v - mu
    rs = lax.rsqrt(jnp.mean(vc * vc, axis=-1, keepdims=True) + EPS)
    xhat = vc * rs
    vn = (xhat * ln_g + ln_b).astype(BF16)
    parts = []
    for p in range(SG_PAIRS):
        vn_p = vn[:, _sg_pair_cols(p)]
        parts.append(jnp.where(low, _dot(wc[2 * p], vn_p, NN), _dot(wc[2 * p + 1], vn_p, NN)))
    vs = bias + jnp.concatenate(parts, axis=1)
    return u, xhat, rs, vn, vs


def sg_fwd(proj, ln_g, ln_b, w_s, bias_tile, name):
    n = proj.shape[0]
    tm = min(ROW_TILE, n)

    def body(pu_ref, pv_ref, g_ref, b_ref, w_ref, bias_ref, o_ref):
        causal = _sg_causal()
        wc = [jnp.where(causal, w_ref[g], 0.0).astype(BF16) for g in range(SG_GROUPS)]
        masks = _sg_low_half()
        for ci in range(tm // SG_CHUNK):
            rows = slice(ci * SG_CHUNK, (ci + 1) * SG_CHUNK)
            u, _, _, _, vs = _sg_forward_chunk(pu_ref[rows, :], pv_ref[rows, :], g_ref[...],
                                               b_ref[...], wc, bias_ref[...], masks)
            o_ref[rows, :] = (u * vs).astype(BF16)

    vec = pl.BlockSpec((1, HALF_W), lambda i: (0, 0))
    return _call(
        body, name=name, grid=(n // tm,),
        in_specs=[pl.BlockSpec((tm, HALF_W), lambda i: (i, 0)),
                  pl.BlockSpec((tm, HALF_W), lambda i: (i, 1)), vec, vec,
                  pl.BlockSpec((SG_GROUPS, SG_CHUNK, SG_CHUNK), lambda i: (0, 0, 0)),
                  pl.BlockSpec((SG_CHUNK, HALF_W), lambda i: (0, 0))],
        out_specs=pl.BlockSpec((tm, HALF_W), lambda i: (i, 0)),
        out_shape=_sds((n, HALF_W), BF16),
    )(proj, proj, ln_g, ln_b, w_s, bias_tile)


def sg_bwd(dsg, proj, ln_g, ln_b, w_s, bias_tile, name):
    n = proj.shape[0]
    tm = min(ROW_TILE, n)

    def body(d_ref, pu_ref, pv_ref, g_ref, b_ref, w_ref, bias_ref,
             dp_ref, dw_ref, db_ref, dlg_ref, dlb_ref):
        @pl.when(pl.program_id(0) == 0)
        def _():
            dw_ref[...] = jnp.zeros_like(dw_ref)
            db_ref[...] = jnp.zeros_like(db_ref)
            dlg_ref[...] = jnp.zeros_like(dlg_ref)
            dlb_ref[...] = jnp.zeros_like(dlb_ref)

        causal = _sg_causal()
        wc = [jnp.where(causal, w_ref[g], 0.0).astype(BF16) for g in range(SG_GROUPS)]
        masks = _sg_low_half()
        ln_g_v = g_ref[...]
        for ci in range(tm // SG_CHUNK):
            rows = slice(ci * SG_CHUNK, (ci + 1) * SG_CHUNK)
            pu = pu_ref[rows, :]
            pv = pv_ref[rows, :]
            u, xhat, rs, vn, vs = _sg_forward_chunk(pu, pv, ln_g_v, b_ref[...], wc,
                                                    bias_ref[...], masks)
            dout = d_ref[rows, :]
            dp_ref[rows, 0:HALF_W] = (dout * vs * _gelu_grad(pu)).astype(BF16)
            dvs = dout * u
            dvs_b = dvs.astype(BF16)
            db_ref[...] += dvs
            dvn_parts = []
            for p in range(SG_PAIRS):
                dvs_p = dvs_b[:, _sg_pair_cols(p)]
                vn_p = vn[:, _sg_pair_cols(p)]
                dvn_parts.append(jnp.where(masks, _dot(wc[2 * p], dvs_p, TN), _dot(wc[2 * p + 1], dvs_p, TN)))
                zero = jnp.zeros_like(dvs_p)
                dw_ref[2 * p] += jnp.where(causal, _dot(jnp.where(masks, dvs_p, zero), vn_p, NT), 0.0)
                dw_ref[2 * p + 1] += jnp.where(causal, _dot(jnp.where(masks, zero, dvs_p), vn_p, NT), 0.0)
            dvn = jnp.concatenate(dvn_parts, axis=1)
            dlg_ref[...] += jnp.sum(dvn * xhat, axis=0, keepdims=True)
            dlb_ref[...] += jnp.sum(dvn, axis=0, keepdims=True)
            dxh = dvn * ln_g_v
            dv = rs * (dxh - jnp.mean(dxh, axis=-1, keepdims=True)
                       - xhat * jnp.mean(dxh * xhat, axis=-1, keepdims=True))
            dp_ref[rows, HALF_W:2 * HALF_W] = (dv * _gelu_grad(pv)).astype(BF16)

    vec = pl.BlockSpec((1, HALF_W), lambda i: (0, 0))
    wspec = pl.BlockSpec((SG_GROUPS, SG_CHUNK, SG_CHUNK), lambda i: (0, 0, 0))
    tile = pl.BlockSpec((SG_CHUNK, HALF_W), lambda i: (0, 0))
    return _call(
        body, name=name, grid=(n // tm,),
        in_specs=[pl.BlockSpec((tm, HALF_W), lambda i: (i, 0)),
                  pl.BlockSpec((tm, HALF_W), lambda i: (i, 0)),
                  pl.BlockSpec((tm, HALF_W), lambda i: (i, 1)), vec, vec, wspec, tile],
        out_specs=[pl.BlockSpec((tm, 2 * HALF_W), lambda i: (i, 0)), wspec, tile, vec, vec],
        out_shape=[_sds((n, PROJ_W), BF16), _sds((SG_GROUPS, SG_CHUNK, SG_CHUNK), F32),
                   _sds((SG_CHUNK, HALF_W), F32), _sds((1, HALF_W), F32), _sds((1, HALF_W), F32)],
    )(dsg, proj, proj, ln_g, ln_b, w_s, bias_tile)


CONV_K = 4
CONV_BLOCK = 256


def _shift_down(x, s, row):
    if s == 0:
        return x
    return jnp.where(row >= s, pltpu.roll(x, s, 0), 0.0)


def _shift_up(x, s, row):
    if s == 0:
        return x
    t_len = x.shape[0]
    return jnp.where(row < t_len - s, pltpu.roll(x, t_len - s, 0), 0.0)


def _conv_taps(x, row):
    return [_shift_down(x, CONV_K - 1 - j, row) for j in range(CONV_K)]


def _conv(taps, w):
    y = taps[0] * w[0:1, :]
    for j in range(1, CONV_K):
        y = y + taps[j] * w[j:j + 1, :]
    return y


def dn_conv_fwd(proj3, conv_w, name):
    b, t, _ = proj3.shape
    nblk = 3 * HALF_W // CONV_BLOCK
    first = 2 * HALF_W // CONV_BLOCK
    n_norm = 2 * HALF_W // CONV_BLOCK

    def body(x_ref, w_ref, o_ref):
        s = pl.program_id(1)
        x = x_ref[0]
        y = _conv(_conv_taps(x, _iota2(x.shape, 0)), w_ref[...])
        y = y * _sigmoid(y)

        @pl.when(s < n_norm)
        def _():
            for h in range(CONV_BLOCK // HEAD_DIM):
                cs = slice(h * HEAD_DIM, (h + 1) * HEAD_DIM)
                yh = y[:, cs]
                o_ref[0, :, cs] = yh * lax.rsqrt(jnp.sum(yh * yh, axis=-1, keepdims=True) + EPS)

        @pl.when(s >= n_norm)
        def _():
            o_ref[0] = y

    return _call(
        body, name=name, grid=(b, nblk),
        in_specs=[pl.BlockSpec((1, t, CONV_BLOCK), lambda i, s: (i, 0, first + s)),
                  pl.BlockSpec((CONV_K, CONV_BLOCK), lambda i, s: (0, s))],
        out_specs=pl.BlockSpec((1, t, CONV_BLOCK), lambda i, s: (i, 0, s)),
        out_shape=_sds((b, t, 3 * HALF_W), F32),
    )(proj3, conv_w)


def dn_conv_bwd(dqkv, proj3, conv_w, dproj3, name, comm=None):
    b, t, _ = proj3.shape
    nblk = 3 * HALF_W // CONV_BLOCK
    first = 2 * HALF_W // CONV_BLOCK
    n_norm = 2 * HALF_W // CONV_BLOCK

    def body(d_ref, x_ref, w_ref, dproj_in, dx_ref, dw_ref, ds_ref):
        s = pl.program_id(0)

        @pl.when(pl.program_id(1) == 0)
        def _():
            dw_ref[...] = jnp.zeros_like(dw_ref)

        x = x_ref[0]
        w = w_ref[...]
        row = _iota2(x.shape, 0)
        taps = _conv_taps(x, row)
        c = _conv(taps, w)
        sg = _sigmoid(c)
        y = c * sg

        @pl.when(s < n_norm)
        def _():
            for h in range(CONV_BLOCK // HEAD_DIM):
                cs = slice(h * HEAD_DIM, (h + 1) * HEAD_DIM)
                yh = y[:, cs]
                r = lax.rsqrt(jnp.sum(yh * yh, axis=-1, keepdims=True) + EPS)
                nh = yh * r
                dn = d_ref[0, :, cs]
                ds_ref[:, cs] = r * (dn - nh * jnp.sum(dn * nh, axis=-1, keepdims=True))

        @pl.when(s >= n_norm)
        def _():
            ds_ref[...] = d_ref[0]

        dc = ds_ref[...] * (sg * (1.0 + c * (1.0 - sg)))
        dx = _shift_up(dc, CONV_K - 1, row) * w[0:1, :]
        for j in range(1, CONV_K):
            dx = dx + _shift_up(dc, CONV_K - 1 - j, row) * w[j:j + 1, :]
        dx_ref[0] = dx.astype(BF16)
        for j in range(CONV_K):
            dw_ref[j:j + 1, :] += jnp.sum(dc * taps[j], axis=0, keepdims=True)

    return _call(
        body, name=name, grid=(nblk, b),
        in_specs=[pl.BlockSpec((1, t, CONV_BLOCK), lambda s, i: (i, 0, s)),
                  pl.BlockSpec((1, t, CONV_BLOCK), lambda s, i: (i, 0, first + s)),
                  pl.BlockSpec((CONV_K, CONV_BLOCK), lambda s, i: (0, s)), _ANY],
        out_specs=[pl.BlockSpec((1, t, CONV_BLOCK), lambda s, i: (i, 0, first + s)),
                   pl.BlockSpec((CONV_K, CONV_BLOCK), lambda s, i: (0, s))],
        out_shape=[_sds(dproj3.shape, BF16), _sds((CONV_K, 3 * HALF_W), F32)],
        scratch=[pltpu.VMEM((t, CONV_BLOCK), F32)],
        input_output_aliases={3: 0}, comm=comm,
    )(dqkv, proj3, conv_w, dproj3)


def _chunk_masks():
    ii = _iota2((DN_CHUNK, DN_CHUNK), 0)
    jj = _iota2((DN_CHUNK, DN_CHUNK), 1)
    return ii >= jj, ii > jj, ii == jj


LOCKSTEP_CHUNKS = 4


def _inv_unit_lower_many(l_mats, eye):
    eye_f = jnp.where(eye, 1.0, 0.0)
    ps = [-l for l in l_mats]
    ts = [eye_f + p for p in ps]
    pss = [_split(p) for p in ps]
    size = 2
    while size < DN_CHUNK:
        ps = [_dot3(s, s) for s in pss]
        pss = [_split(p) for p in ps]
        ts = [t + _dot3(_split(t), s) for t, s in zip(ts, pss)]
        size *= 2
    return ts


def _gates(pba, ea_row, dtb_row):
    beta = _sigmoid(pba)
    g = -ea_row * _softplus(pba + dtb_row)
    return beta, g


def _chunk_decay(gcol):
    incl, strict, eye = _chunk_masks()
    grow = jnp.sum(jnp.where(eye, gcol, 0.0), axis=0, keepdims=True)
    decay = jnp.where(incl, jnp.exp(jnp.where(incl, gcol - grow, 0.0)), 0.0)
    return decay, incl, strict, eye


def dn_chunk_fwd(qkv, proj3, alog_row, dtb_row, name):
    b, t, _ = qkv.shape
    rblk = min(256, t)
    n_in = rblk // DN_CHUNK

    def body(q_ref, k_ref, v_ref, pba_ref, al_ref, dtb_ref,
             u_ref, w_ref, qd_ref, kd_ref, qk_ref, ti_ref, gc_ref):
        ea = jnp.exp(al_ref[...])
        tri = jnp.where(_chunk_masks()[0], 1.0, 0.0)

        _, strict, eye = _chunk_masks()

        def chunk_group(cg, carry):
            items = []
            for sub in range(LOCKSTEP_CHUNKS):
                rows = pl.ds(pl.multiple_of((cg * LOCKSTEP_CHUNKS + sub) * DN_CHUNK, DN_CHUNK), DN_CHUNK)
                beta_all, g_all = _gates(pba_ref[0, rows, :], ea, dtb_ref[...])
                gc = _dot_exact_lhs(tri, g_all)
                gc_ref[0, rows, :] = gc
                for h in range(N_HEADS):
                    items.append((rows, h, beta_all[:, h:h + 1], gc[:, N_HEADS + h:N_HEADS + h + 1]))
            ks, kbs, decays, egs = [], [], [], []
            for rows, h, beta, gcol in items:
                cs = slice(h * HEAD_DIM, (h + 1) * HEAD_DIM)
                k = k_ref[0, rows, cs]
                ks.append(k)
                kbs.append(k * beta)
                decays.append(_chunk_decay(gcol)[0])
                egs.append(jnp.exp(gcol))
            ms = [_bdot(kb, k, NT) for kb, k in zip(kbs, ks)]
            tinvs = _inv_unit_lower_many([jnp.where(strict, m * dc, 0.0) for m, dc in zip(ms, decays)], eye)
            tsps = [_split(t) for t in tinvs]
            for (rows, h, beta, gcol), tsp, tinv in zip(items, tsps, tinvs):
                cs = slice(h * HEAD_DIM, (h + 1) * HEAD_DIM)
                u_ref[0, rows, cs] = _dot3(tsp, _split(v_ref[0, rows, cs] * beta))
                ti_ref[0, h, rows, :] = tinv
            for (rows, h, beta, gcol), tsp, kb, eg in zip(items, tsps, kbs, egs):
                cs = slice(h * HEAD_DIM, (h + 1) * HEAD_DIM)
                w_ref[0, rows, cs] = _dot3(tsp, _split(kb * eg))
            for (rows, h, beta, gcol), k, dc, eg in zip(items, ks, decays, egs):
                cs = slice(h * HEAD_DIM, (h + 1) * HEAD_DIM)
                q = q_ref[0, rows, cs] * QK_SCALE
                qk_ref[0, h, rows, :] = _bdot(q, k, NT) * dc
                qd_ref[0, rows, cs] = q * eg
                kd_ref[0, rows, cs] = k * jnp.exp(gcol[DN_CHUNK - 1:DN_CHUNK, :] - gcol)
            return carry

        lax.fori_loop(0, n_in // LOCKSTEP_CHUNKS, chunk_group, 0)

    def seg(cblk):
        return pl.BlockSpec((1, rblk, HALF_W), lambda i, r: (i, r, cblk))

    vec = pl.BlockSpec((1, LANES), lambda i, r: (0, 0))
    wide = pl.BlockSpec((1, rblk, HALF_W), lambda i, r: (i, r, 0))
    sq = pl.BlockSpec((1, N_HEADS, rblk, DN_CHUNK), lambda i, r: (i, 0, r, 0))
    return _call(
        body, name=name, grid=(b, t // rblk),
        in_specs=[seg(0), seg(1), seg(2),
                  pl.BlockSpec((1, rblk, LANES), lambda i, r: (i, r, GATE_COL_BLOCK)), vec, vec],
        out_specs=[wide, wide, wide, wide, sq, sq,
                   pl.BlockSpec((1, rblk, LANES), lambda i, r: (i, r, 0))],
        out_shape=[_sds((b, t, HALF_W), F32)] * 4
        + [_sds((b, N_HEADS, t, DN_CHUNK), F32)] * 2 + [_sds((b, t, LANES), F32)],
    )(qkv, qkv, qkv, proj3, alog_row, dtb_row)


def dn_scan_fwd(u, w, qd, kd, qk, gc, name):
    b, t, _ = u.shape
    nc = t // DN_CHUNK
    bh = b * N_HEADS

    def body(u_ref, w_ref, qd_ref, kd_ref, qk_ref, gc_ref, o_ref, sin_ref, s_ref):
        @pl.when(pl.program_id(0) == 0)
        def _():
            s_ref[...] = jnp.zeros_like(s_ref)

        items = [(bi, h, slice(h * HEAD_DIM, (h + 1) * HEAD_DIM)) for bi in range(b) for h in range(N_HEADS)]
        sbs = []
        for bi, h, cs in items:
            s = s_ref[bi * N_HEADS + h]
            sin_ref[0, bi * N_HEADS + h] = s
            sbs.append(s.astype(BF16))
        ws = [_bdot(w_ref[bi, :, cs], sb, NN) for (bi, h, cs), sb in zip(items, sbs)]
        qs = [_bdot(qd_ref[bi, :, cs], sb, NN) for (bi, h, cs), sb in zip(items, sbs)]
        vbs = [(u_ref[bi, :, cs] - wsi).astype(BF16) for (bi, h, cs), wsi in zip(items, ws)]
        for (bi, h, cs), qsi, vb in zip(items, qs, vbs):
            o_ref[bi, :, cs] = qsi + _bdot(qk_ref[bi, h], vb, NN)
        for (bi, h, cs), vb in zip(items, vbs):
            gl = jnp.exp(gc_ref[bi, DN_CHUNK - 1:DN_CHUNK, N_HEADS + h:N_HEADS + h + 1])
            idx = bi * N_HEADS + h
            s_ref[idx] = s_ref[idx] * gl + _bdot(kd_ref[bi, :, cs], vb, TN)

    wide = pl.BlockSpec((b, DN_CHUNK, HALF_W), lambda c: (0, c, 0))
    return _call(
        body, name=name, grid=(nc,),
        in_specs=[wide, wide, wide, wide,
                  pl.BlockSpec((b, N_HEADS, DN_CHUNK, DN_CHUNK), lambda c: (0, 0, c, 0)),
                  pl.BlockSpec((b, DN_CHUNK, LANES), lambda c: (0, c, 0))],
        out_specs=[wide, pl.BlockSpec((1, bh, HEAD_DIM, HEAD_DIM), lambda c: (c, 0, 0, 0))],
        out_shape=[_sds((b, t, HALF_W), F32), _sds((nc, bh, HEAD_DIM, HEAD_DIM), F32)],
        scratch=[pltpu.VMEM((bh, HEAD_DIM, HEAD_DIM), F32)],
    )(u, w, qd, kd, qk, gc)


def dn_scan_bwd(do, u, w, qd, kd, qk, gc, s_in, name):
    b, t, _ = u.shape
    nc = t // DN_CHUNK
    bh = b * N_HEADS

    def body(do_ref, u_ref, w_ref, qd_ref, kd_ref, qk_ref, gc_ref, sin_ref,
             du_ref, dw_ref, dqd_ref, dkd_ref, dqk_ref, dgc_ref, ds_ref):
        @pl.when(pl.program_id(0) == 0)
        def _():
            ds_ref[...] = jnp.zeros_like(ds_ref)

        last_row = _iota2((DN_CHUNK, LANES), 0) == DN_CHUNK - 1
        lane = _iota2((DN_CHUNK, LANES), 1)
        items = [(bi, h, slice(h * HEAD_DIM, (h + 1) * HEAD_DIM)) for bi in range(b) for h in range(N_HEADS)]
        sbs = [sin_ref[0, bi * N_HEADS + h].astype(BF16) for bi, h, cs in items]
        wvs = [w_ref[bi, :, cs].astype(BF16) for bi, h, cs in items]
        dovs = [do_ref[bi, :, cs].astype(BF16) for bi, h, cs in items]
        dsbs = [ds_ref[bi * N_HEADS + h].astype(BF16) for bi, h, cs in items]
        vbs = [(u_ref[bi, :, cs] - _dot(wv, sb, NN)).astype(BF16)
               for (bi, h, cs), wv, sb in zip(items, wvs, sbs)]
        for (bi, h, cs), dov, sb in zip(items, dovs, sbs):
            dqd_ref[bi, :, cs] = _dot(dov, sb, NT)
        dvns = [_dot(kd_ref[bi, :, cs].astype(BF16), dsb, NN) + _dot(qk_ref[bi, h].astype(BF16), dov, TN)
                for (bi, h, cs), dsb, dov in zip(items, dsbs, dovs)]
        for (bi, h, cs), vb, dsb, dov in zip(items, vbs, dsbs, dovs):
            dkd_ref[bi, :, cs] = _dot(vb, dsb, NT)
            dqk_ref[bi, h] = _dot(dov, vb, NT)
        dgls = []
        for (bi, h, cs), dvn, sb, wv, dov in zip(items, dvns, sbs, wvs, dovs):
            idx = bi * N_HEADS + h
            du_ref[bi, :, cs] = dvn
            dvn_b = dvn.astype(BF16)
            dw_ref[bi, :, cs] = -_dot(dvn_b, sb, NT)
            gl = jnp.exp(gc_ref[bi, DN_CHUNK - 1:DN_CHUNK, N_HEADS + h:N_HEADS + h + 1])
            ds = ds_ref[idx]
            dgl = jnp.sum(jnp.sum(ds * sin_ref[0, idx], axis=1, keepdims=True), axis=0, keepdims=True)
            dgls.append(dgl * gl)
            ds_ref[idx] = (ds * gl + _dot(qd_ref[bi, :, cs].astype(BF16), dov, TN)
                           - _dot(wv, dvn_b, TN))
        for bi in range(b):
            dgc = jnp.zeros((DN_CHUNK, LANES), F32)
            for h in range(N_HEADS):
                dgc = dgc + jnp.where(jnp.logical_and(last_row, lane == N_HEADS + h),
                                      dgls[bi * N_HEADS + h], 0.0)
            dgc_ref[bi] = dgc

    def rev(c):
        return nc - 1 - c

    wide = pl.BlockSpec((b, DN_CHUNK, HALF_W), lambda c: (0, rev(c), 0))
    sq = pl.BlockSpec((b, N_HEADS, DN_CHUNK, DN_CHUNK), lambda c: (0, 0, rev(c), 0))
    gates = pl.BlockSpec((b, DN_CHUNK, LANES), lambda c: (0, rev(c), 0))
    return _call(
        body, name=name, grid=(nc,),
        in_specs=[wide, wide, wide, wide, wide, sq, gates,
                  pl.BlockSpec((1, bh, HEAD_DIM, HEAD_DIM), lambda c: (rev(c), 0, 0, 0))],
        out_specs=[wide, wide, wide, wide, sq, gates],
        out_shape=[_sds((b, t, HALF_W), F32)] * 4
        + [_sds((b, N_HEADS, t, DN_CHUNK), F32), _sds((b, t, LANES), F32)],
        scratch=[pltpu.VMEM((bh, HEAD_DIM, HEAD_DIM), F32)],
    )(do, u, w, qd, kd, qk, gc, s_in)


def dn_chunk_bwd(qkv, proj3, alog_row, dtb_row, tinv, u, w, du, dw, dqd, dkd, dqk, dgc_scan, dproj3, name,
                 comm=None):
    b, t, _ = qkv.shape
    rblk = min(256, t)
    n_in = rblk // DN_CHUNK

    def body(q_ref, k_ref, v_ref, pba_ref, al_ref, dtb_ref, ti_ref, u_ref, w_ref,
             du_ref, dw_ref, dqd_ref, dkd_ref, dqk_ref, dgs_ref, dproj_in,
             dq_ref, dpba_ref, dal_ref, ddtb_ref):
        @pl.when(jnp.logical_and(pl.program_id(0) == 0, pl.program_id(1) == 0))
        def _():
            dal_ref[...] = jnp.zeros_like(dal_ref)
            ddtb_ref[...] = jnp.zeros_like(ddtb_ref)

        ea = jnp.exp(al_ref[...])
        incl0 = _chunk_masks()[0]
        tri = jnp.where(incl0, 1.0, 0.0)
        tri_up = jnp.where(_iota2((DN_CHUNK, DN_CHUNK), 1) >= _iota2((DN_CHUNK, DN_CHUNK), 0), 1.0, 0.0)
        lane = _iota2((DN_CHUNK, LANES), 1)
        last_col = _iota2((DN_CHUNK, 1), 0) == DN_CHUNK - 1

        _, strict, _ = _chunk_masks()
        gate_lane = jnp.logical_and(lane >= N_HEADS, lane < 2 * N_HEADS)

        def chunk_group(cg, carry):
            tiles, items = [], []
            for sub in range(LOCKSTEP_CHUNKS):
                rows = pl.ds(pl.multiple_of((cg * LOCKSTEP_CHUNKS + sub) * DN_CHUNK, DN_CHUNK), DN_CHUNK)
                pba = pba_ref[0, rows, :]
                beta_all, g_all = _gates(pba, ea, dtb_ref[...])
                gc = _dot_exact_lhs(tri, g_all)
                tiles.append((rows, pba, beta_all, g_all))
                for h in range(N_HEADS):
                    items.append((sub, rows, h, slice(h * HEAD_DIM, (h + 1) * HEAD_DIM),
                                  beta_all[:, h:h + 1], gc[:, N_HEADS + h:N_HEADS + h + 1]))
            decays = [_chunk_decay(gcol)[0] for _, _, _, _, _, gcol in items]
            egs = [jnp.exp(gcol) for _, _, _, _, _, gcol in items]
            qbs = [(q_ref[0, rows, cs] * QK_SCALE).astype(BF16) for _, rows, h, cs, _, _ in items]
            kfs = [k_ref[0, rows, cs].astype(BF16) for _, rows, h, cs, _, _ in items]
            kbs = [k_ref[0, rows, cs] * beta for _, rows, h, cs, beta, _ in items]
            kbbs = [kb.astype(BF16) for kb in kbs]
            tsps = [_split(ti_ref[0, h, rows, :]) for _, rows, h, cs, _, _ in items]
            drus = [_dot3(tsp, _split(du_ref[0, rows, cs]), TN)
                    for (_, rows, h, cs, _, _), tsp in zip(items, tsps)]
            drws = [_dot3(tsp, _split(dw_ref[0, rows, cs]), TN)
                    for (_, rows, h, cs, _, _), tsp in zip(items, tsps)]
            m_kks = [_dot(kbb, kf, NT) for kbb, kf in zip(kbbs, kfs)]
            a_qks = [_dot(qb, kf, NT) for qb, kf in zip(qbs, kfs)]
            dls = [-jnp.where(strict, _dot3(_split(dru), _split(u_ref[0, rows, cs]), NT)
                              + _dot3(_split(drw), _split(w_ref[0, rows, cs]), NT), 0.0)
                   for (_, rows, h, cs, _, _), dru, drw in zip(items, drus, drws)]
            dms = [(dl * dc).astype(BF16) for dl, dc in zip(dls, decays)]
            das = [(dqk_ref[0, h, rows, :] * dc).astype(BF16)
                   for (_, rows, h, cs, _, _), dc in zip(items, decays)]
            dkb_mm = [_dot(dm, kf, NN) for dm, kf in zip(dms, kfs)]
            dk_mm = [_dot(dm, kbb, TN) + _dot(da, qb, TN) for dm, kbb, da, qb in zip(dms, kbbs, das, qbs)]
            dqs_mm = [_dot(da, kf, NN) for da, kf in zip(das, kfs)]
            dgc_tiles = [dgs_ref[0, rows, :] for rows, _, _, _ in tiles]
            dbeta_tiles = [jnp.zeros((DN_CHUNK, LANES), F32) for _ in tiles]
            for n_it, (sub, rows, h, cs, beta, gcol) in enumerate(items):
                eg, dc = egs[n_it], decays[n_it]
                k = k_ref[0, rows, cs]
                q = q_ref[0, rows, cs] * QK_SCALE
                kb, dru, drw = kbs[n_it], drus[n_it], drws[n_it]
                ek = jnp.exp(gcol[DN_CHUNK - 1:DN_CHUNK, :] - gcol)
                e_mat = (dls[n_it] * m_kks[n_it] + dqk_ref[0, h, rows, :] * a_qks[n_it]) * dc
                dkb = drw * eg + dkb_mm[n_it]
                dqd = dqd_ref[0, rows, cs]
                dkd = dkd_ref[0, rows, cs]
                kdk = dkd * k * ek
                kdk_total = jnp.sum(jnp.sum(kdk, axis=0, keepdims=True), axis=1, keepdims=True)
                dg = (jnp.sum(drw * kb * eg + dqd * q * eg - kdk, axis=-1, keepdims=True)
                      + jnp.sum(e_mat, axis=1, keepdims=True)
                      - _row_to_col(jnp.sum(e_mat, axis=0, keepdims=True))
                      + jnp.where(last_col, kdk_total, 0.0))
                dbeta = jnp.sum(dkb * k + dru * v_ref[0, rows, cs], axis=-1, keepdims=True)
                dq_ref[0, rows, cs] = (dqs_mm[n_it] + dqd * eg) * QK_SCALE
                dq_ref[0, rows, pl.ds(HALF_W + h * HEAD_DIM, HEAD_DIM)] = dk_mm[n_it] + dkd * ek + dkb * beta
                dq_ref[0, rows, pl.ds(2 * HALF_W + h * HEAD_DIM, HEAD_DIM)] = dru * beta
                dgc_tiles[sub] = dgc_tiles[sub] + jnp.where(lane == N_HEADS + h, dg, 0.0)
                dbeta_tiles[sub] = dbeta_tiles[sub] + jnp.where(lane == h, dbeta, 0.0)
            for (rows, pba, beta_all, g_all), dgc_tile, dbeta_tile in zip(tiles, dgc_tiles, dbeta_tiles):
                dg_tile = _dot_exact_lhs(tri_up, dgc_tile)
                da_pre = dg_tile * (-ea) * _sigmoid(pba + dtb_ref[...])
                dal_ref[...] += jnp.sum(jnp.where(gate_lane, dg_tile * g_all, 0.0), axis=0, keepdims=True)
                ddtb_ref[...] += jnp.sum(jnp.where(gate_lane, da_pre, 0.0), axis=0, keepdims=True)
                dpba_ref[0, rows, :] = jnp.where(lane < N_HEADS, dbeta_tile * beta_all * (1.0 - beta_all),
                                                 jnp.where(gate_lane, da_pre, 0.0)).astype(BF16)
            return carry

        lax.fori_loop(0, n_in // LOCKSTEP_CHUNKS, chunk_group, 0)

    def seg(cblk):
        return pl.BlockSpec((1, rblk, HALF_W), lambda i, r: (i, r, cblk))

    vec = pl.BlockSpec((1, LANES), lambda i, r: (0, 0))
    wide = pl.BlockSpec((1, rblk, HALF_W), lambda i, r: (i, r, 0))
    sq = pl.BlockSpec((1, N_HEADS, rblk, DN_CHUNK), lambda i, r: (i, 0, r, 0))
    gates = pl.BlockSpec((1, rblk, LANES), lambda i, r: (i, r, 0))
    return _call(
        body, name=name, grid=(b, t // rblk),
        in_specs=[seg(0), seg(1), seg(2),
                  pl.BlockSpec((1, rblk, LANES), lambda i, r: (i, r, GATE_COL_BLOCK)), vec, vec,
                  sq, wide, wide, wide, wide, wide, wide, sq, gates, _ANY],
        out_specs=[pl.BlockSpec((1, rblk, 3 * HALF_W), lambda i, r: (i, r, 0)),
                   pl.BlockSpec((1, rblk, LANES), lambda i, r: (i, r, GATE_COL_BLOCK)), vec, vec],
        out_shape=[_sds((b, t, 3 * HALF_W), F32), _sds(dproj3.shape, BF16),
                   _sds((1, LANES), F32), _sds((1, LANES), F32)],
        input_output_aliases={15: 1}, comm=comm,
    )(qkv, qkv, qkv, proj3, alog_row, dtb_row, tinv, u, w, du, dw, dqd, dkd, dqk, dgc_scan, dproj3)


def dn_out_fwd(o, proj, dn_norm, name):
    n = o.shape[0]
    tm = min(ROW_TILE, n)

    def body(o_ref, z_ref, g_ref, y_ref):
        for h in range(N_HEADS):
            cs = slice(h * HEAD_DIM, (h + 1) * HEAD_DIM)
            oh = o_ref[:, cs]
            z = z_ref[:, cs]
            y = oh * _rms_scale(oh) * g_ref[...]
            y_ref[:, cs] = (y * (z * _sigmoid(z))).astype(BF16)

    half = pl.BlockSpec((tm, HALF_W), lambda i: (i, 0))
    return _call(
        body, name=name, grid=(n // tm,),
        in_specs=[half, pl.BlockSpec((tm, HALF_W), lambda i: (i, 5)),
                  pl.BlockSpec((1, HEAD_DIM), lambda i: (0, 0))],
        out_specs=half, out_shape=_sds((n, HALF_W), BF16),
    )(o, proj, dn_norm)


def dn_out_bwd(dy, o, proj, dn_norm, dproj, name):
    n = o.shape[0]
    tm = min(ROW_TILE, n)

    def body(dy_ref, o_ref, z_ref, g_ref, dproj_in, do_ref, dz_ref, dg_ref):
        @pl.when(pl.program_id(0) == 0)
        def _():
            dg_ref[...] = jnp.zeros_like(dg_ref)

        g = g_ref[...]
        dg = jnp.zeros_like(g)
        for h in range(N_HEADS):
            cs = slice(h * HEAD_DIM, (h + 1) * HEAD_DIM)
            oh = o_ref[:, cs]
            z = z_ref[:, cs]
            d = dy_ref[:, cs]
            r = _rms_scale(oh)
            nh = oh * r
            sz = _sigmoid(z)
            dyn = d * (z * sz)
            dz_ref[:, cs] = (d * (nh * g) * (sz * (1.0 + z * (1.0 - sz)))).astype(BF16)
            dg = dg + jnp.sum(dyn * nh, axis=0, keepdims=True)
            dn = dyn * g
            do_ref[:, cs] = r * (dn - nh * jnp.mean(dn * nh, axis=-1, keepdims=True))
        dg_ref[...] += dg

    half = pl.BlockSpec((tm, HALF_W), lambda i: (i, 0))
    vec = pl.BlockSpec((1, HEAD_DIM), lambda i: (0, 0))
    return _call(
        body, name=name, grid=(n // tm,),
        in_specs=[half, half, pl.BlockSpec((tm, HALF_W), lambda i: (i, 5)), vec, _ANY],
        out_specs=[half, pl.BlockSpec((tm, HALF_W), lambda i: (i, 5)), vec],
        out_shape=[_sds((n, HALF_W), F32), _sds(dproj.shape, BF16), _sds((1, HEAD_DIM), F32)],
        input_output_aliases={4: 1},
    )(dy, o, proj, dn_norm, dproj)


def _adamw_math(w, g, m, v):
    m_new = ADAM_B1 * m + (1.0 - ADAM_B1) * g
    v_new = ADAM_B2 * v + (1.0 - ADAM_B2) * (g * g)
    m_hat = m_new / (1.0 - ADAM_B1 ** ADAM_STEP)
    v_hat = v_new / (1.0 - ADAM_B2 ** ADAM_STEP)
    delta = -ADAM_LR * (m_hat / (jnp.sqrt(v_hat) + ADAM_EPS) + ADAM_WD * w)
    return delta, m_new, v_new


def adamw(w, g, m, v, name):
    r, c = w.shape
    tr = r
    for cand in (256, 352):
        if r % cand == 0 and r > cand:
            tr = cand
            break

    def body(w_ref, g_ref, m_ref, v_ref, d_ref, mo_ref, vo_ref):
        d, mn, vn = _adamw_math(w_ref[...], g_ref[...], m_ref[...], v_ref[...])
        d_ref[...] = d
        mo_ref[...] = mn
        vo_ref[...] = vn

    spec = pl.BlockSpec((tr, c), lambda i: (i, 0))
    return _call(
        body, name=name, grid=(r // tr,),
        in_specs=[spec] * 4, out_specs=[spec] * 3, out_shape=[_sds((r, c), F32)] * 3,
    )(w, g, m, v)


def _place():
    return lax.axis_index("x"), lax.axis_index("y"), lax.axis_index("c")


def _other_chips(x, y):
    return [(1 - x, y), (x, 1 - y), (1 - x, 1 - y)]


_ANY = pl.BlockSpec(memory_space=pl.ANY)


def cast_place(w, shard_idx, name):
    r, cols = w.shape
    tr = r // 2

    def body(j_ref, w_ref, o_ref):
        o_ref[0] = w_ref[...].astype(BF16)

    return pl.pallas_call(
        body, name=name,
        grid_spec=pltpu.PrefetchScalarGridSpec(
            num_scalar_prefetch=1, grid=(r // tr,),
            in_specs=[pl.BlockSpec((tr, cols), lambda i, j: (i, 0))],
            out_specs=pl.BlockSpec((1, tr, cols), lambda i, j: (j[0], i, 0))),
        out_shape=_sds((N_SHARD, r, cols), BF16),
        compiler_params=pltpu.CompilerParams(dimension_semantics=("arbitrary",),
                                             vmem_limit_bytes=VMEM_LIMIT),
    )(shard_idx, w)


class Exchange:
    def __init__(self, inputs, out_shape, aliases, sems, phases):
        self.inputs, self.out_shape, self.aliases = list(inputs), list(out_shape), dict(aliases)
        self.sems, self.phases = list(sems), list(phases)


def run_exchange(ex, name):
    def body(*refs):
        n_in, n_out = len(ex.inputs), len(ex.out_shape)
        for _, fn in ex.phases:
            fn(refs[:n_in], refs[n_in:n_in + n_out], refs[n_in + n_out:])

    return _call(body, name=name, in_specs=[_ANY] * len(ex.inputs), out_specs=[_ANY] * len(ex.out_shape),
                 out_shape=ex.out_shape, scratch=ex.sems, input_output_aliases=ex.aliases)(*ex.inputs)


def merge_exchanges(exs):
    inputs, out_shape, sems, aliases, phases, out_slices = [], [], [], {}, [], []
    for ex in exs:
        i0, o0, s0 = len(inputs), len(out_shape), len(sems)
        inputs += ex.inputs
        out_shape += ex.out_shape
        sems += ex.sems
        for k, m in ex.aliases.items():
            aliases[i0 + k] = o0 + m
        si, so, ss = slice(i0, len(inputs)), slice(o0, len(out_shape)), slice(s0, len(sems))
        out_slices.append(so)
        for step, fn in ex.phases:
            phases.append((step, lambda ins, outs, sm, fn=fn, si=si, so=so, ss=ss: fn(ins[si], outs[so], sm[ss])))
    return Exchange(inputs, out_shape, aliases, sems, phases), out_slices


def _dma_sems(*sizes):
    return [pltpu.SemaphoreType.DMA((s,)) for s in sizes]


def gather_exchange(bufs, small=None, relay_step=-2):
    n = len(bufs)
    n_small = 0 if small is None else 1

    def half(outs, a, blk, hc):
        rh = bufs[a].shape[1] // 2
        return outs[a].at[blk, pl.ds(hc * rh, rh), :]

    def ici(outs, sems, a, k, blk, to):
        return pltpu.make_async_remote_copy(
            src_ref=half(outs, a, blk, to[2]), dst_ref=half(outs, a, blk, to[2]), send_sem=sems[0].at[3 * a + k],
            recv_sem=sems[1].at[3 * a + k], device_id=to, device_id_type=MESH)

    def d2d(outs, sems, a, k, blk, hc, to):
        return pltpu.make_async_remote_copy(
            src_ref=half(outs, a, blk, hc), dst_ref=half(outs, a, blk, hc), send_sem=sems[2].at[3 * a + k],
            recv_sem=sems[3].at[3 * a + k], device_id=to, device_id_type=MESH)

    def small_copy(ins, outs, sems, k, blk, to):
        return pltpu.make_async_remote_copy(
            src_ref=ins[n], dst_ref=outs[n].at[blk], send_sem=sems[0].at[3 * n + k],
            recv_sem=sems[1].at[3 * n + k], device_id=to, device_id_type=MESH)

    def start(ins, outs, sems):
        x, y, c = _place()
        j = 2 * x + y
        if n_small:
            pltpu.make_async_copy(ins[n], outs[n].at[j], sems[4].at[0]).start()
        for k, (px, py) in enumerate(_other_chips(x, y)):
            if n_small:
                small_copy(ins, outs, sems, k, j, (px, py, c)).start()
            for a in range(n):
                ici(outs, sems, a, k, j, (px, py, c)).start()

    def relay(ins, outs, sems):
        x, y, c = _place()
        for k, (px, py) in enumerate(_other_chips(x, y)):
            for a in range(n):
                ici(outs, sems, a, k, 2 * px + py, (px, py, c)).wait_recv()
                d2d(outs, sems, a, k, 2 * px + py, c, (x, y, 1 - c)).start()

    def finish(ins, outs, sems):
        x, y, c = _place()
        j = 2 * x + y
        for k, (px, py) in enumerate(_other_chips(x, y)):
            blk = 2 * px + py
            if n_small:
                small_copy(ins, outs, sems, k, blk, (px, py, c)).wait_recv()
                small_copy(ins, outs, sems, k, j, (px, py, c)).wait_send()
            for a in range(n):
                d2d(outs, sems, a, k, blk, 1 - c, (x, y, 1 - c)).wait_recv()
                ici(outs, sems, a, k, j, (px, py, c)).wait_send()
                d2d(outs, sems, a, k, blk, c, (x, y, 1 - c)).wait_send()
        if n_small:
            pltpu.make_async_copy(ins[n], outs[n].at[j], sems[4].at[0]).wait()

    out_shape = [_sds(b.shape, b.dtype) for b in bufs]
    if n_small:
        out_shape.append(_sds((N_SHARD,) + small.shape, small.dtype))
    return Exchange(list(bufs) + ([small] if n_small else []), out_shape, {a: a for a in range(n)},
                    _dma_sems(3 * n + 3, 3 * n + 3, 3 * n, 3 * n, 1),
                    [(0, start), (relay_step, relay), (-1, finish)])


def _start_then_wait(copies):
    def start(ins, outs, sems):
        for sent, _ in copies(ins, outs, sems):
            sent().start()

    def finish(ins, outs, sems):
        pairs = copies(ins, outs, sems)
        for _, arrival in pairs:
            arrival().wait_recv()
        for sent, _ in pairs:
            sent().wait_send()

    return [(0, start), (-1, finish)]


def pair_exchange(arrs):
    n = len(arrs)

    def copies(ins, outs, sems):
        x, y, c = _place()
        res = []
        for a in range(n):
            def mk(a=a):
                rh = arrs[a].shape[1] // 2
                return pltpu.make_async_remote_copy(
                    src_ref=ins[a].at[:, pl.ds((1 - c) * rh, rh), :], dst_ref=outs[a], send_sem=sems[0].at[a],
                    recv_sem=sems[1].at[a], device_id=(x, y, 1 - c), device_id_type=MESH)
            res.append((mk, mk))
        return res

    return Exchange(arrs, [_sds((a.shape[0], a.shape[1] // 2, a.shape[2]), a.dtype) for a in arrs], {},
                    _dma_sems(n, n), _start_then_wait(copies))


def pair_add(g, s, c_idx, name):
    nb, r, cols = g.shape
    rh = r // 2

    def body(c_ref, g_ref, s_ref, o_ref):
        o_ref[...] = (g_ref[...] + s_ref[...]).astype(BF16)

    return pl.pallas_call(
        body, name=name,
        grid_spec=pltpu.PrefetchScalarGridSpec(
            num_scalar_prefetch=1, grid=(nb,),
            in_specs=[pl.BlockSpec((1, rh, cols), lambda j, c: (j, c[0], 0)),
                      pl.BlockSpec((1, rh, cols), lambda j, c: (j, 0, 0))],
            out_specs=pl.BlockSpec((1, rh, cols), lambda j, c: (j, 0, 0))),
        out_shape=_sds((nb, rh, cols), BF16),
        compiler_params=pltpu.CompilerParams(dimension_semantics=("arbitrary",),
                                             vmem_limit_bytes=VMEM_LIMIT),
    )(c_idx, g, s)


def chip_exchange(arrs):
    n = len(arrs)

    def copies(ins, outs, sems):
        x, y, c = _place()
        j = 2 * x + y
        res = []
        for a in range(n):
            for k, (px, py) in enumerate(_other_chips(x, y)):
                def mk(src_blk, dst_blk, a=a, k=k, to=(px, py, c)):
                    return pltpu.make_async_remote_copy(
                        src_ref=ins[a].at[src_blk], dst_ref=outs[a].at[dst_blk], send_sem=sems[0].at[3 * a + k],
                        recv_sem=sems[1].at[3 * a + k], device_id=to, device_id_type=MESH)
                res.append((functools.partial(mk, 2 * px + py, j), functools.partial(mk, j, 2 * px + py)))
        return res

    return Exchange(arrs, [_sds(a.shape, a.dtype) for a in arrs], {}, _dma_sems(3 * n, 3 * n),
                    _start_then_wait(copies))


def sum_chips(r, p, shard_idx, name):
    nb, rh, cols = r.shape
    tr = rh

    def body(j_ref, p_ref, *refs):
        o_ref = refs[nb]
        j = j_ref[0]
        acc = None
        for i in range(nb):
            term = jnp.where(j == i, p_ref[0], refs[i][0]).astype(F32)
            acc = term if acc is None else acc + term
        o_ref[...] = acc

    def slot(i):
        return pl.BlockSpec((1, tr, cols), lambda t, j: (jnp.where(j[0] == i, (i + 1) % nb, i), t, 0))

    return pl.pallas_call(
        body, name=name,
        grid_spec=pltpu.PrefetchScalarGridSpec(
            num_scalar_prefetch=1, grid=(rh // tr,),
            in_specs=[pl.BlockSpec((1, tr, cols), lambda t, j: (j[0], t, 0))] + [slot(i) for i in range(nb)],
            out_specs=pl.BlockSpec((tr, cols), lambda t, j: (t, 0))),
        out_shape=_sds((rh, cols), F32),
        compiler_params=pltpu.CompilerParams(dimension_semantics=("arbitrary",),
                                             vmem_limit_bytes=VMEM_LIMIT),
    )(shard_idx, p, *([r] * nb))


def pair_swap(arrs):
    n = len(arrs)

    def copies(ins, outs, sems):
        x, y, c = _place()
        res = []
        for a in range(n):
            def mk(a=a):
                return pltpu.make_async_remote_copy(
                    src_ref=ins[a], dst_ref=outs[a], send_sem=sems[0].at[a], recv_sem=sems[1].at[a],
                    device_id=(x, y, 1 - c), device_id_type=MESH)
            res.append((mk, mk))
        return res

    return Exchange(arrs, [_sds(a.shape, a.dtype) for a in arrs], {}, _dma_sems(n, n),
                    _start_then_wait(copies))


ADAMW_STEPS_PER_HALF = 4


def adamw_pairs(items, name, comm=None):
    n_items = len(items)
    nh = ADAMW_STEPS_PER_HALF

    def body(*refs):
        ins, outs = refs[:5 * n_items], refs[5 * n_items:]
        mine = (pl.program_id(0) // nh) == lax.axis_index("c")
        for a in range(n_items):
            w_ref, gm_ref, gs_ref, m_ref, v_ref = ins[5 * a:5 * a + 5]
            g_ref, d_ref, mo_ref, vo_ref = outs[4 * a:4 * a + 4]
            g = jnp.where(mine, gm_ref[...], gs_ref[...])
            d, mn, vn = _adamw_math(w_ref[...], g, m_ref[...], v_ref[...])
            g_ref[...] = g
            d_ref[...] = d
            mo_ref[...] = mn
            vo_ref[...] = vn

    in_specs, out_specs, out_shape, args = [], [], [], []
    for w, g_mine, g_sib, m, v in items:
        r, cols = w.shape
        tr = r // (2 * nh)
        full = pl.BlockSpec((tr, cols), lambda i: (i, 0))
        part = pl.BlockSpec((tr, cols), lambda i: (i % nh, 0))
        in_specs += [full, part, part, full, full]
        out_specs += [full] * 4
        out_shape += [_sds((r, cols), F32)] * 4
        args += [w, g_mine, g_sib, m, v]
    res = _call(body, name=name, grid=(2 * nh,), in_specs=in_specs, out_specs=out_specs,
                out_shape=out_shape, comm=comm)(*args)
    own, hosted = (res, None) if comm is None else res
    grouped = [tuple(own[4 * a:4 * a + 4]) for a in range(n_items)]
    return grouped if comm is None else (grouped, hosted)


N_DEV = 8


def device_gather(pack):
    def copies(ins, outs, sems):
        x, y, c = _place()
        me = 4 * x + 2 * y + c
        res = []
        for k in range(1, N_DEV):
            fx, fy, fc = (k >> 2) & 1, (k >> 1) & 1, k & 1
            px, py, pc = (1 - x if fx else x, 1 - y if fy else y, 1 - c if fc else c)

            def mk(slot, k=k, to=(px, py, pc)):
                return pltpu.make_async_remote_copy(
                    src_ref=ins[0], dst_ref=outs[0].at[slot], send_sem=sems[0].at[k - 1],
                    recv_sem=sems[1].at[k - 1], device_id=to, device_id_type=MESH)
            res.append((functools.partial(mk, me), functools.partial(mk, 4 * px + 2 * py + pc)))
        return res

    return Exchange([pack], [_sds((N_DEV,) + pack.shape, pack.dtype)], {}, _dma_sems(N_DEV - 1, N_DEV - 1),
                    _start_then_wait(copies))


def sum_devices(buf, pack, me_idx, name):
    r, cols = pack.shape

    def body(me_ref, p_ref, *refs):
        o_ref = refs[N_DEV]
        acc = None
        for i in range(N_DEV):
            term = jnp.where(me_ref[0] == i, p_ref[...], refs[i][0])
            acc = term if acc is None else acc + term
        o_ref[...] = acc

    def slot(i):
        return pl.BlockSpec((1, r, cols), lambda t, me: (jnp.where(me[0] == i, (i + 1) % N_DEV, i), 0, 0))

    whole = pl.BlockSpec((r, cols), lambda t, me: (0, 0))
    return pl.pallas_call(
        body, name=name,
        grid_spec=pltpu.PrefetchScalarGridSpec(
            num_scalar_prefetch=1, grid=(1,),
            in_specs=[whole] + [slot(i) for i in range(N_DEV)], out_specs=whole),
        out_shape=_sds((r, cols), F32),
        compiler_params=pltpu.CompilerParams(dimension_semantics=("arbitrary",),
                                             vmem_limit_bytes=VMEM_LIMIT),
    )(me_idx, pack, *([buf] * N_DEV))


SMALL_NAMES = ("ffn1_norm", "mix_norm", "ffn2_norm", "final_norm", "sg_ln_g", "sg_ln_b",
               "dn_norm", "a_log", "dt_bias", "sg_b", "sg_w", "conv_w", "loss")


def _to_rows(a):
    flat = a.reshape(-1)
    pad = (-flat.shape[0]) % LANES
    if pad:
        flat = jnp.pad(flat, (0, pad))
    return flat.reshape(-1, LANES)


def _pack_small(parts):
    rows = [_to_rows(parts[k]) for k in SMALL_NAMES]
    pack = jnp.concatenate(rows, axis=0)
    pad = (-pack.shape[0]) % 8
    if pad:
        pack = jnp.pad(pack, ((0, pad), (0, 0)))
    return pack


def _unpack_small(pack, shapes):
    out, r0 = {}, 0
    for k in SMALL_NAMES:
        size = 1
        for s in shapes[k]:
            size *= s
        nrows = -(-size // LANES)
        out[k] = pack[r0:r0 + nrows].reshape(-1)[:size].reshape(shapes[k])
        r0 += nrows
    return out


def kernel(x, ffn1_norm, ffn1_w_gate, ffn1_w_up, ffn1_w_down, mix_norm, w_in, conv_w, a_log, dt_bias, dn_norm, sg_ln_g, sg_ln_b, sg_w, sg_b, w_out, ffn2_norm, ffn2_w_gate, ffn2_w_up, ffn2_w_down, final_norm, loss_target, m_ffn1_norm, m_ffn1_w_gate, m_ffn1_w_up, m_ffn1_w_down, m_mix_norm, m_w_in, m_conv_w, m_a_log, m_dt_bias, m_dn_norm, m_sg_ln_g, m_sg_ln_b, m_sg_w, m_sg_b, m_w_out, m_ffn2_norm, m_ffn2_w_gate, m_ffn2_w_up, m_ffn2_w_down, m_final_norm, v_ffn1_norm, v_ffn1_w_gate, v_ffn1_w_up, v_ffn1_w_down, v_mix_norm, v_w_in, v_conv_w, v_a_log, v_dt_bias, v_dn_norm, v_sg_ln_g, v_sg_ln_b, v_sg_w, v_sg_b, v_w_out, v_ffn2_norm, v_ffn2_w_gate, v_ffn2_w_up, v_ffn2_w_down, v_final_norm):
    bsz, t_len, d = x.shape
    n = bsz * t_len
    xy, yy, cc = _place()
    shard = 2 * xy + yy

    big_names = ["ffn1_w_gate", "ffn1_w_up", "ffn1_w_down", "w_in", "w_out",
                 "ffn2_w_gate", "ffn2_w_up", "ffn2_w_down"]
    big_w = dict(ffn1_w_gate=ffn1_w_gate, ffn1_w_up=ffn1_w_up, ffn1_w_down=ffn1_w_down, w_in=w_in,
                 w_out=w_out, ffn2_w_gate=ffn2_w_gate, ffn2_w_up=ffn2_w_up, ffn2_w_down=ffn2_w_down)
    big_m = dict(ffn1_w_gate=m_ffn1_w_gate, ffn1_w_up=m_ffn1_w_up, ffn1_w_down=m_ffn1_w_down, w_in=m_w_in,
                 w_out=m_w_out, ffn2_w_gate=m_ffn2_w_gate, ffn2_w_up=m_ffn2_w_up, ffn2_w_down=m_ffn2_w_down)
    big_v = dict(ffn1_w_gate=v_ffn1_w_gate, ffn1_w_up=v_ffn1_w_up, ffn1_w_down=v_ffn1_w_down, w_in=v_w_in,
                 w_out=v_w_out, ffn2_w_gate=v_ffn2_w_gate, ffn2_w_up=v_ffn2_w_up, ffn2_w_down=v_ffn2_w_down)
    shard_idx = jnp.reshape(shard, (1,)).astype(jnp.int32)
    c_idx = jnp.reshape(cc, (1,)).astype(jnp.int32)
    transposed = ("ffn1_w_gate", "ffn1_w_up", "ffn2_w_gate", "ffn2_w_up")

    def as2d(a, k):
        return a[0].T if k in transposed else a[0]

    def from2d(a, k):
        return a.T[None] if k in transposed else a[None]

    placed = {k: cast_place(as2d(big_w[k], k), shard_idx, name="cast_" + k) for k in big_names}
    first_names = big_names[:3]
    later_names = big_names[3:]
    res = run_exchange(gather_exchange([placed[k] for k in first_names], conv_w[0]), name="gather_first")
    gw = dict(zip(first_names, res[:3]))
    conv_full = res[3].transpose(1, 0, 2).reshape(CONV_K, 3 * HALF_W)

    x0 = x.reshape(n, d)
    def ffn_weights(prefix):
        return [gw[prefix + k].reshape(-1, d) for k in ("_w_gate", "_w_up", "_w_down")]

    def ffn_grad_blocks(grads):
        return [g.reshape(N_SHARD, -1, d) for g in grads]

    (x1, h1, gate1, up1, act1), later = ffn_fwd(
        x0, ffn1_norm, *ffn_weights("ffn1"), name="ffn1_fwd",
        comm=gather_exchange([placed[k] for k in later_names]))
    gw.update(zip(later_names, later))
    w_in_full = gw["w_in"].transpose(1, 0, 2).reshape(d, IN_COLS)
    w_in_full = jnp.pad(w_in_full, ((0, 0), (0, PROJ_W - IN_COLS)))
    w_out_full = gw["w_out"].reshape(2 * HALF_W, d)
    proj, h2 = in_proj_fwd(x1, mix_norm, w_in_full, name="in_proj_fwd")
    proj3 = proj.reshape(bsz, t_len, PROJ_W)
    bias_tile = jnp.repeat(sg_b[0].T, SG_GROUP_DIM, axis=1)
    sg_out = sg_fwd(proj, sg_ln_g, sg_ln_b, sg_w[0], bias_tile, name="sg_fwd")
    qkv = dn_conv_fwd(proj3, conv_full, name="dn_conv_fwd")
    alog_row = jnp.zeros((1, LANES), F32).at[0, N_HEADS:2 * N_HEADS].set(a_log[0])
    dtb_row = jnp.zeros((1, LANES), F32).at[0, N_HEADS:2 * N_HEADS].set(dt_bias[0])
    u_wy, w_wy, q_dec, k_dec, qk, tinv, gc = dn_chunk_fwd(qkv, proj3, alog_row, dtb_row,
                                                           name="dn_chunk_fwd")
    o, s_in = dn_scan_fwd(u_wy, w_wy, q_dec, k_dec, qk, gc, name="dn_scan_fwd")
    dn_out = dn_out_fwd(o.reshape(n, HALF_W), proj, dn_norm, name="dn_out_fwd")
    x2 = out_proj_fwd(x1, sg_out, dn_out, w_out_full, name="out_proj_fwd")
    x3, h3, gate2, up2, act2 = ffn_fwd(x2, ffn2_norm, *ffn_weights("ffn2"), name="ffn2_fwd")
    dx3, d_final_norm, loss_tile = final_loss(x3, final_norm.reshape(1, d),
                                              loss_target.reshape(n, d), name="final_loss")

    dx2, dgate2, dup2, dyh2, d_ffn2_norm = ffn_bwd_act(
        dx3, x2, ffn2_norm, gate2, up2, *ffn_weights("ffn2"), name="ffn2_bwd_act")
    g_big = {}
    g_big["ffn2_w_gate"], g_big["ffn2_w_up"], g_big["ffn2_w_down"] = ffn_grad_blocks(ffn_bwd_w(
        [dgate2, dup2, act2], [h3, dyh2], [(0, 0), (1, 0), (2, 1)], name="ffn2_bwd_w"))

    early = ["ffn2_w_gate", "ffn2_w_up", "ffn2_w_down"]
    (d_sg, d_dn, dx2b), early_sib = out_proj_bwd_x(dx2, w_out_full, name="out_proj_bwd_x",
                                                   comm=pair_exchange([g_big[k] for k in early]))
    early_sums = [pair_add(g_big[k], s, c_idx, name="grad_pair_add_" + k) for k, s in zip(early, early_sib)]
    g_w_out = matmul_tn([sg_out, dn_out], dx2b, d, name="w_out_grad")
    g_big["w_out"] = g_w_out.reshape(N_SHARD, (2 * HALF_W) // N_SHARD, d)

    d_proj, d_sg_w, d_bias_tile, d_ln_g, d_ln_b = sg_bwd(d_sg, proj, sg_ln_g, sg_ln_b, sg_w[0],
                                                         bias_tile, name="sg_bwd")
    d_o, d_proj, d_dn_norm = dn_out_bwd(d_dn, o.reshape(n, HALF_W), proj, dn_norm, d_proj,
                                        name="dn_out_bwd")
    du, dw, dqd, dkd, dqk, dgc_scan = dn_scan_bwd(d_o.reshape(bsz, t_len, HALF_W), u_wy, w_wy, q_dec,
                                                  k_dec, qk, gc, s_in, name="dn_scan_bwd")
    (d_qkv, d_proj3, d_alog_row, d_dtb_row), early_chips = dn_chunk_bwd(
        qkv, proj3, alog_row, dtb_row, tinv, u_wy, w_wy, du, dw, dqd, dkd, dqk, dgc_scan,
        d_proj.reshape(bsz, t_len, PROJ_W), name="dn_chunk_bwd", comm=chip_exchange(early_sums))
    early_halves = [sum_chips(r, p, shard_idx, name="grad_chip_sum_" + k)
                    for k, r, p in zip(early, early_chips, early_sums)]
    (d_proj3, d_conv), early_sib_halves = dn_conv_bwd(d_qkv, proj3, conv_full, d_proj3, name="dn_conv_bwd",
                                                      comm=pair_swap(early_halves))
    d_proj = d_proj3.reshape(n, PROJ_W)
    g_w_in = matmul_tn([h2], d_proj, 3 * MXU_DIM, name="w_in_grad")[:, :IN_COLS]
    g_big["w_in"] = g_w_in.reshape(d, N_SHARD, IN_COLS // N_SHARD).transpose(1, 0, 2)

    def reduce_start(names):
        return pair_exchange([g_big[k] for k in names])

    def reduce_pair_sums(names, from_sib):
        return [pair_add(g_big[k], s, c_idx, name="grad_pair_add_" + k) for k, s in zip(names, from_sib)]

    def reduce_chip_sums(names, from_chips, sums):
        return [sum_chips(r, p, shard_idx, name="grad_chip_sum_" + k)
                for k, r, p in zip(names, from_chips, sums)]

    mid = ["w_in", "w_out"]
    (dx1, d_mix_norm, dyh1), mid_sib = in_proj_bwd_x(d_proj, w_in_full, x1, mix_norm, dx2,
                                                     name="in_proj_bwd_x", comm=reduce_start(mid))
    mid_sums = reduce_pair_sums(mid, mid_sib)
    down = ["ffn1_w_down"]
    (g_down,), mid_chips = ffn_bwd_w([act1], [dyh1], [(0, 0)], name="ffn1_bwd_w_down",
                                     comm=chip_exchange(mid_sums))
    g_big["ffn1_w_down"] = g_down.reshape(N_SHARD, -1, d)
    mid_halves = reduce_chip_sums(mid, mid_chips, mid_sums)
    leg, legs = merge_exchanges([reduce_start(down), pair_swap(mid_halves)])
    leg_res = run_exchange(leg, name="grad_pair_exchange_down")
    down_sums = reduce_pair_sums(down, leg_res[legs[0]])
    mid_sib_halves = leg_res[legs[1]]

    dx0, dgate1, dup1, _, d_ffn1_norm = ffn_bwd_act(
        dx1, x0, ffn1_norm, gate1, up1, *ffn_weights("ffn1"), name="ffn1_bwd_act")
    grad_x = dx0.reshape(bsz, t_len, d)
    d_sg_b = d_bias_tile.reshape(SG_CHUNK, SG_GROUPS, SG_GROUP_DIM).sum(axis=-1).T
    small_g = dict(ffn1_norm=d_ffn1_norm, mix_norm=d_mix_norm, ffn2_norm=d_ffn2_norm,
                   final_norm=d_final_norm, sg_ln_g=d_ln_g, sg_ln_b=d_ln_b, dn_norm=d_dn_norm,
                   a_log=d_alog_row[:, N_HEADS:2 * N_HEADS], dt_bias=d_dtb_row[:, N_HEADS:2 * N_HEADS],
                   sg_b=d_sg_b, sg_w=d_sg_w, conv_w=d_conv, loss=loss_tile[:, :1])
    my_pack = _pack_small(small_g)
    hosted, parts = merge_exchanges([chip_exchange(down_sums), device_gather(my_pack)])
    late = ["ffn1_w_gate", "ffn1_w_up"]
    late_grads, hosted_res = ffn_bwd_w([dgate1, dup1], [h1], [(0, 0), (1, 0)], name="ffn1_bwd_w_gate_up",
                                       comm=hosted)
    g_big["ffn1_w_gate"], g_big["ffn1_w_up"] = ffn_grad_blocks(late_grads)
    down_halves = reduce_chip_sums(down, hosted_res[parts[0]], down_sums)
    (all_packs,) = hosted_res[parts[1]]

    leg, legs = merge_exchanges([reduce_start(late), pair_swap(down_halves)])
    leg_res = run_exchange(leg, name="grad_pair_exchange")
    pair_sums = reduce_pair_sums(late, leg_res[legs[0]])
    down_sib_halves = leg_res[legs[1]]

    def adam_items(names, mine, sib):
        return [(as2d(big_w[k], k), gm, gs, as2d(big_m[k], k), as2d(big_v[k], k))
                for k, gm, gs in zip(names, mine, sib)]

    outs = {}
    done = adamw_pairs(
        adam_items(early + mid + down, early_halves + mid_halves + down_halves,
                   list(early_sib_halves) + list(mid_sib_halves) + list(down_sib_halves)),
        name="adamw_early")
    from_chips = run_exchange(chip_exchange(pair_sums), name="grad_chip_exchange")
    halves = reduce_chip_sums(late, from_chips, pair_sums)
    sib_halves = run_exchange(pair_swap(halves), name="grad_pair_swap")
    done += adamw_pairs(adam_items(late, halves, sib_halves), name="adamw_late")
    for k, res in zip(early + mid + down + late, done):
        outs[k] = tuple(from2d(a, k) for a in res)

    small_w = dict(ffn1_norm=ffn1_norm, mix_norm=mix_norm, ffn2_norm=ffn2_norm, final_norm=final_norm,
                   sg_ln_g=sg_ln_g, sg_ln_b=sg_ln_b, dn_norm=dn_norm, a_log=a_log, dt_bias=dt_bias,
                   sg_b=sg_b, sg_w=sg_w)
    small_m = dict(ffn1_norm=m_ffn1_norm, mix_norm=m_mix_norm, ffn2_norm=m_ffn2_norm,
                   final_norm=m_final_norm, sg_ln_g=m_sg_ln_g, sg_ln_b=m_sg_ln_b, dn_norm=m_dn_norm,
                   a_log=m_a_log, dt_bias=m_dt_bias, sg_b=m_sg_b, sg_w=m_sg_w)
    small_v = dict(ffn1_norm=v_ffn1_norm, mix_norm=v_mix_norm, ffn2_norm=v_ffn2_norm,
                   final_norm=v_final_norm, sg_ln_g=v_sg_ln_g, sg_ln_b=v_sg_ln_b, dn_norm=v_dn_norm,
                   a_log=v_a_log, dt_bias=v_dt_bias, sg_b=v_sg_b, sg_w=v_sg_w)
    shapes = {k: small_w[k].shape for k in small_w}
    shapes["conv_w"] = (CONV_K, 3 * HALF_W)
    shapes["loss"] = (1, 1)
    me_idx = jnp.reshape(4 * xy + 2 * yy + cc, (1,)).astype(jnp.int32)
    g_pack = sum_devices(all_packs, my_pack, me_idx, name="small_sum")
    g_small = _unpack_small(g_pack, shapes)
    loss = g_small["loss"].reshape(())
    cw = 3 * HALF_W // N_SHARD
    g_conv = lax.dynamic_slice_in_dim(g_small["conv_w"], shard * cw, cw, axis=1)
    zero_conv = jnp.zeros((CONV_K, 3 * HALF_W), F32)

    def packed(src, conv):
        parts = dict(src)
        parts["conv_w"] = lax.dynamic_update_slice_in_dim(zero_conv, conv[0], shard * cw, axis=1)
        parts["loss"] = jnp.zeros((1, 1), F32)
        return _pack_small(parts)

    d_pack, m_pack, v_pack = adamw(packed(small_w, conv_w), g_pack, packed(small_m, m_conv_w),
                                   packed(small_v, v_conv_w), name="adamw_small")
    d_small = _unpack_small(d_pack, shapes)
    m_small = _unpack_small(m_pack, shapes)
    v_small = _unpack_small(v_pack, shapes)

    def conv_block(full_arr):
        return lax.dynamic_slice_in_dim(full_arr, shard * cw, cw, axis=1)[None]

    for k in small_w:
        outs[k] = (g_small[k].reshape(small_w[k].shape), d_small[k], m_small[k], v_small[k])
    outs["conv_w"] = (g_conv[None], conv_block(d_small["conv_w"]), conv_block(m_small["conv_w"]),
                      conv_block(v_small["conv_w"]))

    order = ["ffn1_norm", "ffn1_w_gate", "ffn1_w_up", "ffn1_w_down", "mix_norm", "w_in", "conv_w",
             "a_log", "dt_bias", "dn_norm", "sg_ln_g", "sg_ln_b", "sg_w", "sg_b", "w_out", "ffn2_norm",
             "ffn2_w_gate", "ffn2_w_up", "ffn2_w_down", "final_norm"]
    return (loss, grad_x, *[outs[k][0] for k in order], *[outs[k][1] for k in order],
            *[outs[k][2] for k in order], *[outs[k][3] for k in order])
```

```python
import functools

import jax
import jax.numpy as jnp
from jax import lax
from jax.experimental import pallas as pl
from jax.experimental.pallas import tpu as pltpu

F32 = jnp.float32
BF16 = jnp.bfloat16
EPS = 1e-6

D_MODEL = 1024
N_SHARD = 4
HEAD_DIM = 128
N_HEADS = 4
DN_CHUNK = 64
SG_CHUNK = 128
SG_GROUPS = 8
SG_GROUP_DIM = 64
HALF_W = 512
PROJ_W = 3200
IN_COLS = 3080
GATE_COL_BLOCK = 24
QK_SCALE = HEAD_DIM ** -0.5
LANES = 128

ADAM_LR = 0.001
ADAM_B1 = 0.9
ADAM_B2 = 0.999
ADAM_EPS = 1e-08
ADAM_WD = 0.01
ADAM_STEP = 10

VMEM_LIMIT = 56 * 1024 * 1024
ROW_TILE = 512

NN = ((1,), (0,))
NT = ((1,), (1,))
TN = ((0,), (0,))
MESH = pl.DeviceIdType.MESH


def _dot(a, b, dims):
    return lax.dot_general(a, b, (dims, ((), ())), preferred_element_type=F32)


def _bdot(a, b, dims):
    return _dot(a.astype(BF16), b.astype(BF16), dims)


def _split(a):
    hi = a.astype(BF16)
    lo = (a - hi.astype(F32)).astype(BF16)
    return hi, lo


def _dot3(a, b, dims=NN):
    return _dot(a[0], b[0], dims) + (_dot(a[0], b[1], dims) + _dot(a[1], b[0], dims))


def _dot_exact_lhs(a, b):
    ab = a.astype(BF16)
    b1 = b.astype(BF16)
    r1 = b - b1.astype(F32)
    b2 = r1.astype(BF16)
    b3 = (r1 - b2.astype(F32)).astype(BF16)
    return _dot(ab, b1, NN) + (_dot(ab, b2, NN) + _dot(ab, b3, NN))


def _call(body, *, name, out_shape, in_specs, out_specs, grid=(), scratch=(), comm=None, **kw):
    params = dict(vmem_limit_bytes=VMEM_LIMIT)
    if grid:
        params["dimension_semantics"] = ("arbitrary",) * len(grid)
    if comm is None:
        return pl.pallas_call(
            body, name=name, grid=grid, in_specs=in_specs, out_specs=out_specs,
            out_shape=out_shape, scratch_shapes=list(scratch),
            compiler_params=pltpu.CompilerParams(**params), **kw)

    n_in, n_out, n_sc = len(in_specs), len(out_specs), len(scratch)
    c_in, c_out = len(comm.inputs), len(comm.out_shape)
    steps = 1
    for g in grid:
        steps *= g

    def hosted(*refs):
        ins, cins = refs[:n_in], refs[n_in:n_in + c_in]
        o0 = n_in + c_in
        outs, couts = refs[o0:o0 + n_out], refs[o0 + n_out:o0 + n_out + c_out]
        s0 = o0 + n_out + c_out
        sc, csems = refs[s0:s0 + n_sc], refs[s0 + n_sc:]
        lin = 0
        for axis, g in enumerate(grid):
            lin = lin * g + pl.program_id(axis)

        def at(step, fn):
            @pl.when(lin == step % steps)
            def _():
                fn(cins, couts, csems)

        for step, fn in comm.phases:
            if step >= 0:
                at(step, fn)
        body(*ins, *outs, *sc)
        for step, fn in comm.phases:
            if step < 0:
                at(step, fn)

    aliases = dict(kw.pop("input_output_aliases", {}))
    for k, m in comm.aliases.items():
        aliases[n_in + k] = n_out + m
    call = pl.pallas_call(
        hosted, name=name, grid=grid, in_specs=list(in_specs) + [_ANY] * c_in,
        out_specs=list(out_specs) + [_ANY] * c_out, out_shape=list(out_shape) + comm.out_shape,
        scratch_shapes=list(scratch) + comm.sems, input_output_aliases=aliases,
        compiler_params=pltpu.CompilerParams(**params), **kw)

    def run(*args):
        res = call(*args, *comm.inputs)
        return res[:n_out], res[n_out:]

    return run


def _sds(shape, dtype):
    return jax.ShapeDtypeStruct(tuple(shape), dtype)


def _resident(shape):
    zeros = (0,) * len(shape)
    return pl.BlockSpec(tuple(shape), lambda *_: zeros, pipeline_mode=pl.Buffered(1))


def _sigmoid(x):
    return jax.nn.sigmoid(x)


def _softplus(x):
    return jnp.maximum(x, 0.0) + jnp.log(1.0 + jnp.exp(-jnp.abs(x)))


_GELU_C = 0.7978845608028654
_GELU_A = 0.044715


def _gelu(x):
    t = jnp.tanh(_GELU_C * (x + _GELU_A * x * x * x))
    return 0.5 * x * (1.0 + t)


def _gelu_grad(x):
    t = jnp.tanh(_GELU_C * (x + _GELU_A * x * x * x))
    return 0.5 * (1.0 + t) + 0.5 * x * (1.0 - t * t) * _GELU_C * (1.0 + 3.0 * _GELU_A * x * x)


def _silu_grad(x):
    s = _sigmoid(x)
    return s * (1.0 + x * (1.0 - s))


def _rms_scale(xv):
    return lax.rsqrt(jnp.mean(xv * xv, axis=-1, keepdims=True) + EPS)


def _rms_bwd(dh, xv, g):
    r = _rms_scale(xv)
    xn = xv * r
    dg = jnp.sum(dh * xn, axis=0, keepdims=True)
    dxn = dh * g
    dx = r * (dxn - xn * jnp.mean(dxn * xn, axis=-1, keepdims=True))
    return dx, dg


def _iota2(shape, dim):
    return lax.broadcasted_iota(jnp.int32, shape, dim)


def _col_to_row(col):
    n = col.shape[0]
    eye = _iota2((n, n), 0) == _iota2((n, n), 1)
    return jnp.sum(jnp.where(eye, col, 0.0), axis=0, keepdims=True)


def _row_to_col(row):
    n = row.shape[1]
    eye = _iota2((n, n), 0) == _iota2((n, n), 1)
    return jnp.sum(jnp.where(eye, row, 0.0), axis=1, keepdims=True)


MXU_DIM = 256


def _hidden_chunks(f, step=3 * MXU_DIM):
    return [(c0, min(c0 + step, f)) for c0 in range(0, f, step)]

def ffn_fwd(x, gnorm, wg, wu, wd, name, comm=None):
    n, d = x.shape
    f = wg.shape[0]
    tm = min(ROW_TILE, n)

    def body(x_ref, g_ref, wg_ref, wu_ref, wd_ref, xo_ref, h_ref, dgate_ref, dup_ref, act_ref, acc_ref):
        xv = x_ref[...]
        h = (xv * _rms_scale(xv) * g_ref[...]).astype(BF16)
        h_ref[...] = h
        for c0, c1 in _hidden_chunks(f):
            gate = _dot(h, wg_ref[c0:c1, :], NT)
            up = _dot(h, wu_ref[c0:c1, :], NT)
            s = _sigmoid(gate)
            silu = gate * s
            act = (silu * up).astype(BF16)
            dgate_ref[:, c0:c1] = (up * (s * (1.0 + gate * (1.0 - s)))).astype(BF16)
            dup_ref[:, c0:c1] = silu.astype(BF16)
            act_ref[:, c0:c1] = act
            part = _dot(act, wd_ref[c0:c1, :], NN)
            if c0 == 0:
                acc_ref[...] = part
            else:
                acc_ref[...] += part
        xo_ref[...] = xv + 0.5 * acc_ref[...]

    row = pl.BlockSpec((tm, d), lambda i: (i, 0))
    wide = pl.BlockSpec((tm, f), lambda i: (i, 0))
    return _call(
        body, name=name, grid=(n // tm,),
        in_specs=[row, pl.BlockSpec((1, d), lambda i: (0, 0))] + [_resident((f, d))] * 3,
        out_specs=[row, row, wide, wide, wide],
        out_shape=[_sds((n, d), F32), _sds((n, d), BF16)] + [_sds((n, f), BF16)] * 3,
        scratch=[pltpu.VMEM((tm, d), F32)], comm=comm,
    )(x, gnorm, wg, wu, wd)


def ffn_bwd_act(dy, x, gnorm, act_dgate, act_dup, wg, wu, wd, name, comm=None):
    n, d = x.shape
    f = wg.shape[0]
    tm = min(ROW_TILE // 2, n)

    def body(dy_ref, x_ref, g_ref, adg_ref, adu_ref, wg_ref, wu_ref, wd_ref,
             dx_ref, dgate_ref, dup_ref, dyh_ref, dg_ref, acc_ref):
        @pl.when(pl.program_id(0) == 0)
        def _():
            dg_ref[...] = jnp.zeros_like(dg_ref)

        dyh = (0.5 * dy_ref[...]).astype(BF16)
        dyh_ref[...] = dyh
        for c0, c1 in _hidden_chunks(f):
            dact = _dot(dyh, wd_ref[c0:c1, :], NT)
            dup = (dact * adu_ref[:, c0:c1].astype(F32)).astype(BF16)
            dgate = (dact * adg_ref[:, c0:c1].astype(F32)).astype(BF16)
            dup_ref[:, c0:c1] = dup
            dgate_ref[:, c0:c1] = dgate
            part = _dot(dgate, wg_ref[c0:c1, :], NN) + _dot(dup, wu_ref[c0:c1, :], NN)
            if c0 == 0:
                acc_ref[...] = part
            else:
                acc_ref[...] += part
        dxn, dg = _rms_bwd(acc_ref[...], x_ref[...], g_ref[...])
        dx_ref[...] = dy_ref[...] + dxn
        dg_ref[...] += dg

    row = pl.BlockSpec((tm, d), lambda i: (i, 0))
    wide = pl.BlockSpec((tm, f), lambda i: (i, 0))
    vec = pl.BlockSpec((1, d), lambda i: (0, 0))
    wres = _resident((f, d))
    return _call(
        body, name=name, grid=(n // tm,),
        in_specs=[row, row, vec, wide, wide, wres, wres, wres],
        out_specs=[row, wide, wide, row, vec],
        out_shape=[_sds((n, d), F32), _sds((n, f), BF16), _sds((n, f), BF16),
                   _sds((n, d), BF16), _sds((1, d), F32)],
        scratch=[pltpu.VMEM((tm, d), F32)], comm=comm,
    )(dy, x, gnorm, act_dgate, act_dup, wg, wu, wd)


def ffn_bwd_w(wide, rows, pairs, name, comm=None):
    n, d = rows[0].shape
    f = wide[0].shape[1]
    fh = f // 2
    tk = min(ROW_TILE, n)
    n_w, n_r = len(wide), len(rows)

    def body(*refs):
        wide_refs, row_refs, outs = refs[:n_w], refs[n_w:n_w + n_r], refs[n_w + n_r:]

        @pl.when(pl.program_id(1) == 0)
        def _():
            for o_ref in outs:
                o_ref[...] = jnp.zeros_like(o_ref)

        row_vals = [r[...] for r in row_refs]
        for c0, c1 in _hidden_chunks(fh, 2 * MXU_DIM):
            for (i, k), o_ref in zip(pairs, outs):
                o_ref[c0:c1, :] += _dot(wide_refs[i][:, c0:c1], row_vals[k], TN)

    row = pl.BlockSpec((tk, d), lambda j, k: (k, 0))
    blk = pl.BlockSpec((tk, fh), lambda j, k: (k, j))
    return _call(
        body, name=name, grid=(2, n // tk),
        in_specs=[blk] * n_w + [row] * n_r,
        out_specs=[pl.BlockSpec((fh, d), lambda j, k: (j, 0))] * len(pairs),
        out_shape=[_sds((f, d), F32)] * len(pairs), comm=comm,
    )(*wide, *rows)


def final_loss(x, gnorm, target, name):
    n, d = x.shape
    tm = min(ROW_TILE, n)

    def body(x_ref, g_ref, t_ref, dx_ref, dg_ref, loss_ref):
        @pl.when(pl.program_id(0) == 0)
        def _():
            dg_ref[...] = jnp.zeros_like(dg_ref)
            loss_ref[...] = jnp.zeros_like(loss_ref)

        xv = x_ref[...]
        y = xv * _rms_scale(xv) * g_ref[...]
        err = y - t_ref[...]
        part = 0.5 * jnp.sum(jnp.mean(err * err, axis=-1, keepdims=True), axis=0, keepdims=True)
        loss_ref[...] += jnp.broadcast_to(part, loss_ref.shape)
        dx, dg = _rms_bwd(err * (1.0 / d), xv, g_ref[...])
        dx_ref[...] = dx
        dg_ref[...] += dg

    row = pl.BlockSpec((tm, d), lambda i: (i, 0))
    vec = pl.BlockSpec((1, d), lambda i: (0, 0))
    return _call(
        body, name=name, grid=(n // tm,),
        in_specs=[row, vec, row],
        out_specs=[row, vec, pl.BlockSpec((1, LANES), lambda i: (0, 0))],
        out_shape=[_sds((n, d), F32), _sds((1, d), F32), _sds((1, LANES), F32)],
    )(x, gnorm, target)


def in_proj_fwd(x, gnorm, w, name):
    n, d = x.shape
    cols = w.shape[1]
    tm = min(ROW_TILE, n)

    def body(x_ref, g_ref, w_ref, p_ref, h_ref):
        xv = x_ref[...]
        h = (xv * _rms_scale(xv) * g_ref[...]).astype(BF16)
        h_ref[...] = h
        for c0, c1 in _hidden_chunks(cols):
            p_ref[:, c0:c1] = _dot(h, w_ref[:, c0:c1], NN)

    return _call(
        body, name=name, grid=(n // tm,),
        in_specs=[pl.BlockSpec((tm, d), lambda i: (i, 0)),
                  pl.BlockSpec((1, d), lambda i: (0, 0)), _resident((d, cols))],
        out_specs=[pl.BlockSpec((tm, cols), lambda i: (i, 0)),
                   pl.BlockSpec((tm, d), lambda i: (i, 0))],
        out_shape=[_sds((n, cols), F32), _sds((n, d), BF16)],
    )(x, gnorm, w)


def in_proj_bwd_x(dproj, w, x, gnorm, dres, name, comm=None):
    n, d = x.shape
    cols = w.shape[1]
    tm = min(ROW_TILE, n)

    def body(dp_ref, w_ref, x_ref, g_ref, dr_ref, dx_ref, dg_ref, dxh_ref):
        @pl.when(pl.program_id(0) == 0)
        def _():
            dg_ref[...] = jnp.zeros_like(dg_ref)

        dh = _dot(dp_ref[...], w_ref[...], NT)
        dxn, dg = _rms_bwd(dh, x_ref[...], g_ref[...])
        dx = dr_ref[...] + dxn
        dx_ref[...] = dx
        dxh_ref[...] = (0.5 * dx).astype(BF16)
        dg_ref[...] += dg

    row = pl.BlockSpec((tm, d), lambda i: (i, 0))
    vec = pl.BlockSpec((1, d), lambda i: (0, 0))
    return _call(
        body, name=name, grid=(n // tm,),
        in_specs=[pl.BlockSpec((tm, cols), lambda i: (i, 0)),
                  _resident((d, cols)), row, vec, row],
        out_specs=[row, vec, row],
        out_shape=[_sds((n, d), F32), _sds((1, d), F32), _sds((n, d), BF16)], comm=comm,
    )(dproj, w, x, gnorm, dres)


def matmul_tn(a_list, b, tn, name):
    n, cb = b.shape
    widths = [a.shape[1] for a in a_list]
    tk = min(ROW_TILE, n)

    def body(*refs):
        a_refs, b_ref, o_ref = refs[:-2], refs[-2], refs[-1]

        @pl.when(pl.program_id(0) == 0)
        def _():
            o_ref[...] = jnp.zeros_like(o_ref)

        r0 = 0
        for a_ref, ka in zip(a_refs, widths):
            av = a_ref[...]
            for c0, c1 in _hidden_chunks(cb, tn):
                o_ref[r0:r0 + ka, c0:c1] += _dot(av, b_ref[:, c0:c1], TN)
            r0 += ka

    return _call(
        body, name=name, grid=(n // tk,),
        in_specs=[pl.BlockSpec((tk, ka), lambda k: (k, 0)) for ka in widths]
        + [pl.BlockSpec((tk, cb), lambda k: (k, 0))],
        out_specs=pl.BlockSpec((sum(widths), cb), lambda k: (0, 0)),
        out_shape=_sds((sum(widths), cb), F32),
    )(*a_list, b)


def out_proj_fwd(x, sg_out, dn_out, w, name):
    n, d = x.shape
    tm = min(ROW_TILE, n)

    def body(x_ref, a_ref, b_ref, w_ref, o_ref):
        o_ref[...] = (x_ref[...] + _dot(a_ref[...], w_ref[0:HALF_W, :], NN)
                      + _dot(b_ref[...], w_ref[HALF_W:2 * HALF_W, :], NN))

    row = pl.BlockSpec((tm, d), lambda i: (i, 0))
    half = pl.BlockSpec((tm, HALF_W), lambda i: (i, 0))
    return _call(
        body, name=name, grid=(n // tm,),
        in_specs=[row, half, half, pl.BlockSpec((2 * HALF_W, d), lambda i: (0, 0))],
        out_specs=row, out_shape=_sds((n, d), F32),
    )(x, sg_out, dn_out, w)


def out_proj_bwd_x(dy, w, name, comm=None):
    n, d = dy.shape
    tm = min(ROW_TILE, n)

    def body(dy_ref, w_ref, dsg_ref, ddn_ref, dyb_ref):
        dyb = dy_ref[...].astype(BF16)
        dyb_ref[...] = dyb
        dsg_ref[...] = _dot(dyb, w_ref[0:HALF_W, :], NT)
        ddn_ref[...] = _dot(dyb, w_ref[HALF_W:2 * HALF_W, :], NT)

    row = pl.BlockSpec((tm, d), lambda i: (i, 0))
    half = pl.BlockSpec((tm, HALF_W), lambda i: (i, 0))
    return _call(
        body, name=name, grid=(n // tm,),
        in_specs=[row, pl.BlockSpec((2 * HALF_W, d), lambda i: (0, 0))],
        out_specs=[half, half, row],
        out_shape=[_sds((n, HALF_W), F32), _sds((n, HALF_W), F32), _sds((n, d), BF16)], comm=comm,
    )(dy, w)


SG_PAIRS = SG_GROUPS // 2


def _sg_low_half():
    return _iota2((SG_CHUNK, LANES), 1) < SG_GROUP_DIM


def _sg_pair_cols(p):
    return slice(p * LANES, (p + 1) * LANES)


def _sg_causal():
    return _iota2((SG_CHUNK, SG_CHUNK), 0) >= _iota2((SG_CHUNK, SG_CHUNK), 1)


def _sg_forward_chunk(pu, pv, ln_g, ln_b, wc, bias, low):
    u = _gelu(pu)
    v = _gelu(pv)
    mu = jnp.mean(v, axis=-1, keepdims=True)
    vc = v - mu
    rs = lax.rsqrt(jnp.mean(vc * vc, axis=-1, keepdims=True) + EPS)
    xhat = vc * rs
    vn = (xhat * ln_g + ln_b).astype(BF16)
    parts = []
    for p in range(SG_PAIRS):
        vn_p = vn[:, _sg_pair_cols(p)]
        parts.append(jnp.where(low, _dot(wc[2 * p], vn_p, NN), _dot(wc[2 * p + 1], vn_p, NN)))
    vs = bias + jnp.concatenate(parts, axis=1)
    return u, xhat, rs, vn, vs


def sg_fwd(proj, ln_g, ln_b, w_s, bias_tile, name):
    n = proj.shape[0]
    tm = min(ROW_TILE, n)

    def body(pu_ref, pv_ref, g_ref, b_ref, w_ref, bias_ref, o_ref):
        causal = _sg_causal()
        wc = [jnp.where(causal, w_ref[g], 0.0).astype(BF16) for g in range(SG_GROUPS)]
        masks = _sg_low_half()
        for ci in range(tm // SG_CHUNK):
            rows = slice(ci * SG_CHUNK, (ci + 1) * SG_CHUNK)
            u, _, _, _, vs = _sg_forward_chunk(pu_ref[rows, :], pv_ref[rows, :], g_ref[...],
                                               b_ref[...], wc, bias_ref[...], masks)
            o_ref[rows, :] = (u * vs).astype(BF16)

    vec = pl.BlockSpec((1, HALF_W), lambda i: (0, 0))
    return _call(
        body, name=name, grid=(n // tm,),
        in_specs=[pl.BlockSpec((tm, HALF_W), lambda i: (i, 0)),
                  pl.BlockSpec((tm, HALF_W), lambda i: (i, 1)), vec, vec,
                  pl.BlockSpec((SG_GROUPS, SG_CHUNK, SG_CHUNK), lambda i: (0, 0, 0)),
                  pl.BlockSpec((SG_CHUNK, HALF_W), lambda i: (0, 0))],
        out_specs=pl.BlockSpec((tm, HALF_W), lambda i: (i, 0)),
        out_shape=_sds((n, HALF_W), BF16),
    )(proj, proj, ln_g, ln_b, w_s, bias_tile)


def sg_bwd(dsg, proj, ln_g, ln_b, w_s, bias_tile, name):
    n = proj.shape[0]
    tm = min(ROW_TILE, n)

    def body(d_ref, pu_ref, pv_ref, g_ref, b_ref, w_ref, bias_ref,
             dp_ref, dw_ref, db_ref, dlg_ref, dlb_ref):
        @pl.when(pl.program_id(0) == 0)
        def _():
            dw_ref[...] = jnp.zeros_like(dw_ref)
            db_ref[...] = jnp.zeros_like(db_ref)
            dlg_ref[...] = jnp.zeros_like(dlg_ref)
            dlb_ref[...] = jnp.zeros_like(dlb_ref)

        causal = _sg_causal()
        wc = [jnp.where(causal, w_ref[g], 0.0).astype(BF16) for g in range(SG_GROUPS)]
        masks = _sg_low_half()
        ln_g_v = g_ref[...]
        for ci in range(tm // SG_CHUNK):
            rows = slice(ci * SG_CHUNK, (ci + 1) * SG_CHUNK)
            pu = pu_ref[rows, :]
            pv = pv_ref[rows, :]
            u, xhat, rs, vn, vs = _sg_forward_chunk(pu, pv, ln_g_v, b_ref[...], wc,
                                                    bias_ref[...], masks)
            dout = d_ref[rows, :]
            dp_ref[rows, 0:HALF_W] = (dout * vs * _gelu_grad(pu)).astype(BF16)
            dvs = dout * u
            dvs_b = dvs.astype(BF16)
            db_ref[...] += dvs
            dvn_parts = []
            for p in range(SG_PAIRS):
                dvs_p = dvs_b[:, _sg_pair_cols(p)]
                vn_p = vn[:, _sg_pair_cols(p)]
                dvn_parts.append(jnp.where(masks, _dot(wc[2 * p], dvs_p, TN), _dot(wc[2 * p + 1], dvs_p, TN)))
                zero = jnp.zeros_like(dvs_p)
                dw_ref[2 * p] += jnp.where(causal, _dot(jnp.where(masks, dvs_p, zero), vn_p, NT), 0.0)
                dw_ref[2 * p + 1] += jnp.where(causal, _dot(jnp.where(masks, zero, dvs_p), vn_p, NT), 0.0)
            dvn = jnp.concatenate(dvn_parts, axis=1)
            dlg_ref[...] += jnp.sum(dvn * xhat, axis=0, keepdims=True)
            dlb_ref[...] += jnp.sum(dvn, axis=0, keepdims=True)
            dxh = dvn * ln_g_v
            dv = rs * (dxh - jnp.mean(dxh, axis=-1, keepdims=True)
                       - xhat * jnp.mean(dxh * xhat, axis=-1, keepdims=True))
            dp_ref[rows, HALF_W:2 * HALF_W] = (dv * _gelu_grad(pv)).astype(BF16)

    vec = pl.BlockSpec((1, HALF_W), lambda i: (0, 0))
    wspec = pl.BlockSpec((SG_GROUPS, SG_CHUNK, SG_CHUNK), lambda i: (0, 0, 0))
    tile = pl.BlockSpec((SG_CHUNK, HALF_W), lambda i: (0, 0))
    return _call(
        body, name=name, grid=(n // tm,),
        in_specs=[pl.BlockSpec((tm, HALF_W), lambda i: (i, 0)),
                  pl.BlockSpec((tm, HALF_W), lambda i: (i, 0)),
                  pl.BlockSpec((tm, HALF_W), lambda i: (i, 1)), vec, vec, wspec, tile],
        out_specs=[pl.BlockSpec((tm, 2 * HALF_W), lambda i: (i, 0)), wspec, tile, vec, vec],
        out_shape=[_sds((n, PROJ_W), BF16), _sds((SG_GROUPS, SG_CHUNK, SG_CHUNK), F32),
                   _sds((SG_CHUNK, HALF_W), F32), _sds((1, HALF_W), F32), _sds((1, HALF_W), F32)],
    )(dsg, proj, proj, ln_g, ln_b, w_s, bias_tile)


CONV_K = 4
CONV_BLOCK = 256


def _shift_down(x, s, row):
    if s == 0:
        return x
    return jnp.where(row >= s, pltpu.roll(x, s, 0), 0.0)


def _shift_up(x, s, row):
    if s == 0:
        return x
    t_len = x.shape[0]
    return jnp.where(row < t_len - s, pltpu.roll(x, t_len - s, 0), 0.0)


def _conv_taps(x, row):
    return [_shift_down(x, CONV_K - 1 - j, row) for j in range(CONV_K)]


def _conv(taps, w):
    y = taps[0] * w[0:1, :]
    for j in range(1, CONV_K):
        y = y + taps[j] * w[j:j + 1, :]
    return y


def dn_conv_fwd(proj3, conv_w, name):
    b, t, _ = proj3.shape
    nblk = 3 * HALF_W // CONV_BLOCK
    first = 2 * HALF_W // CONV_BLOCK
    n_norm = 2 * HALF_W // CONV_BLOCK

    def body(x_ref, w_ref, o_ref):
        s = pl.program_id(1)
        x = x_ref[0]
        y = _conv(_conv_taps(x, _iota2(x.shape, 0)), w_ref[...])
        y = y * _sigmoid(y)

        @pl.when(s < n_norm)
        def _():
            for h in range(CONV_BLOCK // HEAD_DIM):
                cs = slice(h * HEAD_DIM, (h + 1) * HEAD_DIM)
                yh = y[:, cs]
                o_ref[0, :, cs] = yh * lax.rsqrt(jnp.sum(yh * yh, axis=-1, keepdims=True) + EPS)

        @pl.when(s >= n_norm)
        def _():
            o_ref[0] = y

    return _call(
        body, name=name, grid=(b, nblk),
        in_specs=[pl.BlockSpec((1, t, CONV_BLOCK), lambda i, s: (i, 0, first + s)),
                  pl.BlockSpec((CONV_K, CONV_BLOCK), lambda i, s: (0, s))],
        out_specs=pl.BlockSpec((1, t, CONV_BLOCK), lambda i, s: (i, 0, s)),
        out_shape=_sds((b, t, 3 * HALF_W), F32),
    )(proj3, conv_w)


def dn_conv_bwd(dqkv, proj3, conv_w, dproj3, name, comm=None):
    b, t, _ = proj3.shape
    nblk = 3 * HALF_W // CONV_BLOCK
    first = 2 * HALF_W // CONV_BLOCK
    n_norm = 2 * HALF_W // CONV_BLOCK

    def body(d_ref, x_ref, w_ref, dproj_in, dx_ref, dw_ref, ds_ref):
        s = pl.program_id(0)

        @pl.when(pl.program_id(1) == 0)
        def _():
            dw_ref[...] = jnp.zeros_like(dw_ref)

        x = x_ref[0]
        w = w_ref[...]
        row = _iota2(x.shape, 0)
        taps = _conv_taps(x, row)
        c = _conv(taps, w)
        sg = _sigmoid(c)
        y = c * sg

        @pl.when(s < n_norm)
        def _():
            for h in range(CONV_BLOCK // HEAD_DIM):
                cs = slice(h * HEAD_DIM, (h + 1) * HEAD_DIM)
                yh = y[:, cs]
                r = lax.rsqrt(jnp.sum(yh * yh, axis=-1, keepdims=True) + EPS)
                nh = yh * r
                dn = d_ref[0, :, cs]
                ds_ref[:, cs] = r * (dn - nh * jnp.sum(dn * nh, axis=-1, keepdims=True))

        @pl.when(s >= n_norm)
        def _():
            ds_ref[...] = d_ref[0]

        dc = ds_ref[...] * (sg * (1.0 + c * (1.0 - sg)))
        dx = _shift_up(dc, CONV_K - 1, row) * w[0:1, :]
        for j in range(1, CONV_K):
            dx = dx + _shift_up(dc, CONV_K - 1 - j, row) * w[j:j + 1, :]
        dx_ref[0] = dx.astype(BF16)
        for j in range(CONV_K):
            dw_ref[j:j + 1, :] += jnp.sum(dc * taps[j], axis=0, keepdims=True)

    return _call(
        body, name=name, grid=(nblk, b),
        in_specs=[pl.BlockSpec((1, t, CONV_BLOCK), lambda s, i: (i, 0, s)),
                  pl.BlockSpec((1, t, CONV_BLOCK), lambda s, i: (i, 0, first + s)),
                  pl.BlockSpec((CONV_K, CONV_BLOCK), lambda s, i: (0, s)), _ANY],
        out_specs=[pl.BlockSpec((1, t, CONV_BLOCK), lambda s, i: (i, 0, first + s)),
                   pl.BlockSpec((CONV_K, CONV_BLOCK), lambda s, i: (0, s))],
        out_shape=[_sds(dproj3.shape, BF16), _sds((CONV_K, 3 * HALF_W), F32)],
        scratch=[pltpu.VMEM((t, CONV_BLOCK), F32)],
        input_output_aliases={3: 0}, comm=comm,
    )(dqkv, proj3, conv_w, dproj3)


def _chunk_masks():
    ii = _iota2((DN_CHUNK, DN_CHUNK), 0)
    jj = _iota2((DN_CHUNK, DN_CHUNK), 1)
    return ii >= jj, ii > jj, ii == jj


LOCKSTEP_CHUNKS = 4


def _inv_unit_lower_many(l_mats, eye):
    eye_f = jnp.where(eye, 1.0, 0.0)
    ps = [-l for l in l_mats]
    ts = [eye_f + p for p in ps]
    pss = [_split(p) for p in ps]
    size = 2
    while size < DN_CHUNK:
        ps = [_dot3(s, s) for s in pss]
        pss = [_split(p) for p in ps]
        ts = [t + _dot3(_split(t), s) for t, s in zip(ts, pss)]
        size *= 2
    return ts


def _gates(pba, ea_row, dtb_row):
    beta = _sigmoid(pba)
    g = -ea_row * _softplus(pba + dtb_row)
    return beta, g


def _chunk_decay(gcol):
    incl, strict, eye = _chunk_masks()
    grow = jnp.sum(jnp.where(eye, gcol, 0.0), axis=0, keepdims=True)
    decay = jnp.where(incl, jnp.exp(jnp.where(incl, gcol - grow, 0.0)), 0.0)
    return decay, incl, strict, eye


def dn_chunk_fwd(qkv, proj3, alog_row, dtb_row, name):
    b, t, _ = qkv.shape
    rblk = min(256, t)
    n_in = rblk // DN_CHUNK

    def body(q_ref, k_ref, v_ref, pba_ref, al_ref, dtb_ref,
             u_ref, w_ref, qd_ref, kd_ref, qk_ref, ti_ref, gc_ref):
        ea = jnp.exp(al_ref[...])
        tri = jnp.where(_chunk_masks()[0], 1.0, 0.0)

        _, strict, eye = _chunk_masks()

        def chunk_group(cg, carry):
            items = []
            for sub in range(LOCKSTEP_CHUNKS):
                rows = pl.ds(pl.multiple_of((cg * LOCKSTEP_CHUNKS + sub) * DN_CHUNK, DN_CHUNK), DN_CHUNK)
                beta_all, g_all = _gates(pba_ref[0, rows, :], ea, dtb_ref[...])
                gc = _dot_exact_lhs(tri, g_all)
                gc_ref[0, rows, :] = gc
                for h in range(N_HEADS):
                    items.append((rows, h, beta_all[:, h:h + 1], gc[:, N_HEADS + h:N_HEADS + h + 1]))
            ks, kbs, decays, egs = [], [], [], []
            for rows, h, beta, gcol in items:
                cs = slice(h * HEAD_DIM, (h + 1) * HEAD_DIM)
                k = k_ref[0, rows, cs]
                ks.append(k)
                kbs.append(k * beta)
                decays.append(_chunk_decay(gcol)[0])
                egs.append(jnp.exp(gcol))
            ms = [_bdot(kb, k, NT) for kb, k in zip(kbs, ks)]
            tinvs = _inv_unit_lower_many([jnp.where(strict, m * dc, 0.0) for m, dc in zip(ms, decays)], eye)
            tsps = [_split(t) for t in tinvs]
            for (rows, h, beta, gcol), tsp, tinv in zip(items, tsps, tinvs):
                cs = slice(h * HEAD_DIM, (h + 1) * HEAD_DIM)
                u_ref[0, rows, cs] = _dot3(tsp, _split(v_ref[0, rows, cs] * beta))
                ti_ref[0, h, rows, :] = tinv
            for (rows, h, beta, gcol), tsp, kb, eg in zip(items, tsps, kbs, egs):
                cs = slice(h * HEAD_DIM, (h + 1) * HEAD_DIM)
                w_ref[0, rows, cs] = _dot3(tsp, _split(kb * eg))
            for (rows, h, beta, gcol), k, dc, eg in zip(items, ks, decays, egs):
                cs = slice(h * HEAD_DIM, (h + 1) * HEAD_DIM)
                q = q_ref[0, rows, cs] * QK_SCALE
                qk_ref[0, h, rows, :] = _bdot(q, k, NT) * dc
                qd_ref[0, rows, cs] = q * eg
                kd_ref[0, rows, cs] = k * jnp.exp(gcol[DN_CHUNK - 1:DN_CHUNK, :] - gcol)
            return carry

        lax.fori_loop(0, n_in // LOCKSTEP_CHUNKS, chunk_group, 0)

    def seg(cblk):
        return pl.BlockSpec((1, rblk, HALF_W), lambda i, r: (i, r, cblk))

    vec = pl.BlockSpec((1, LANES), lambda i, r: (0, 0))
    wide = pl.BlockSpec((1, rblk, HALF_W), lambda i, r: (i, r, 0))
    sq = pl.BlockSpec((1, N_HEADS, rblk, DN_CHUNK), lambda i, r: (i, 0, r, 0))
    return _call(
        body, name=name, grid=(b, t // rblk),
        in_specs=[seg(0), seg(1), seg(2),
                  pl.BlockSpec((1, rblk, LANES), lambda i, r: (i, r, GATE_COL_BLOCK)), vec, vec],
        out_specs=[wide, wide, wide, wide, sq, sq,
                   pl.BlockSpec((1, rblk, LANES), lambda i, r: (i, r, 0))],
        out_shape=[_sds((b, t, HALF_W), F32)] * 4
        + [_sds((b, N_HEADS, t, DN_CHUNK), F32)] * 2 + [_sds((b, t, LANES), F32)],
    )(qkv, qkv, qkv, proj3, alog_row, dtb_row)


def dn_scan_fwd(u, w, qd, kd, qk, gc, name):
    b, t, _ = u.shape
    nc = t // DN_CHUNK
    bh = b * N_HEADS

    def body(u_ref, w_ref, qd_ref, kd_ref, qk_ref, gc_ref, o_ref, sin_ref, s_ref):
        @pl.when(pl.program_id(0) == 0)
        def _():
            s_ref[...] = jnp.zeros_like(s_ref)

        items = [(bi, h, slice(h * HEAD_DIM, (h + 1) * HEAD_DIM)) for bi in range(b) for h in range(N_HEADS)]
        sbs = []
        for bi, h, cs in items:
            s = s_ref[bi * N_HEADS + h]
            sin_ref[0, bi * N_HEADS + h] = s
            sbs.append(s.astype(BF16))
        ws = [_bdot(w_ref[bi, :, cs], sb, NN) for (bi, h, cs), sb in zip(items, sbs)]
        qs = [_bdot(qd_ref[bi, :, cs], sb, NN) for (bi, h, cs), sb in zip(items, sbs)]
        vbs = [(u_ref[bi, :, cs] - wsi).astype(BF16) for (bi, h, cs), wsi in zip(items, ws)]
        for (bi, h, cs), qsi, vb in zip(items, qs, vbs):
            o_ref[bi, :, cs] = qsi + _bdot(qk_ref[bi, h], vb, NN)
        for (bi, h, cs), vb in zip(items, vbs):
            gl = jnp.exp(gc_ref[bi, DN_CHUNK - 1:DN_CHUNK, N_HEADS + h:N_HEADS + h + 1])
            idx = bi * N_HEADS + h
            s_ref[idx] = s_ref[idx] * gl + _bdot(kd_ref[bi, :, cs], vb, TN)

    wide = pl.BlockSpec((b, DN_CHUNK, HALF_W), lambda c: (0, c, 0))
    return _call(
        body, name=name, grid=(nc,),
        in_specs=[wide, wide, wide, wide,
                  pl.BlockSpec((b, N_HEADS, DN_CHUNK, DN_CHUNK), lambda c: (0, 0, c, 0)),
                  pl.BlockSpec((b, DN_CHUNK, LANES), lambda c: (0, c, 0))],
        out_specs=[wide, pl.BlockSpec((1, bh, HEAD_DIM, HEAD_DIM), lambda c: (c, 0, 0, 0))],
        out_shape=[_sds((b, t, HALF_W), F32), _sds((nc, bh, HEAD_DIM, HEAD_DIM), F32)],
        scratch=[pltpu.VMEM((bh, HEAD_DIM, HEAD_DIM), F32)],
    )(u, w, qd, kd, qk, gc)


def dn_scan_bwd(do, u, w, qd, kd, qk, gc, s_in, name):
    b, t, _ = u.shape
    nc = t // DN_CHUNK
    bh = b * N_HEADS

    def body(do_ref, u_ref, w_ref, qd_ref, kd_ref, qk_ref, gc_ref, sin_ref,
             du_ref, dw_ref, dqd_ref, dkd_ref, dqk_ref, dgc_ref, ds_ref):
        @pl.when(pl.program_id(0) == 0)
        def _():
            ds_ref[...] = jnp.zeros_like(ds_ref)

        last_row = _iota2((DN_CHUNK, LANES), 0) == DN_CHUNK - 1
        lane = _iota2((DN_CHUNK, LANES), 1)
        items = [(bi, h, slice(h * HEAD_DIM, (h + 1) * HEAD_DIM)) for bi in range(b) for h in range(N_HEADS)]
        sbs = [sin_ref[0, bi * N_HEADS + h].astype(BF16) for bi, h, cs in items]
        wvs = [w_ref[bi, :, cs].astype(BF16) for bi, h, cs in items]
        dovs = [do_ref[bi, :, cs].astype(BF16) for bi, h, cs in items]
        dsbs = [ds_ref[bi * N_HEADS + h].astype(BF16) for bi, h, cs in items]
        vbs = [(u_ref[bi, :, cs] - _dot(wv, sb, NN)).astype(BF16)
               for (bi, h, cs), wv, sb in zip(items, wvs, sbs)]
        for (bi, h, cs), dov, sb in zip(items, dovs, sbs):
            dqd_ref[bi, :, cs] = _dot(dov, sb, NT)
        dvns = [_dot(kd_ref[bi, :, cs].astype(BF16), dsb, NN) + _dot(qk_ref[bi, h].astype(BF16), dov, TN)
                for (bi, h, cs), dsb, dov in zip(items, dsbs, dovs)]
        for (bi, h, cs), vb, dsb, dov in zip(items, vbs, dsbs, dovs):
            dkd_ref[bi, :, cs] = _dot(vb, dsb, NT)
            dqk_ref[bi, h] = _dot(dov, vb, NT)
        dgls = []
        for (bi, h, cs), dvn, sb, wv, dov in zip(items, dvns, sbs, wvs, dovs):
            idx = bi * N_HEADS + h
            du_ref[bi, :, cs] = dvn
            dvn_b = dvn.astype(BF16)
            dw_ref[bi, :, cs] = -_dot(dvn_b, sb, NT)
            gl = jnp.exp(gc_ref[bi, DN_CHUNK - 1:DN_CHUNK, N_HEADS + h:N_HEADS + h + 1])
            ds = ds_ref[idx]
            dgl = jnp.sum(jnp.sum(ds * sin_ref[0, idx], axis=1, keepdims=True), axis=0, keepdims=True)
            dgls.append(dgl * gl)
            ds_ref[idx] = (ds * gl + _dot(qd_ref[bi, :, cs].astype(BF16), dov, TN)
                           - _dot(wv, dvn_b, TN))
        for bi in range(b):
            dgc = jnp.zeros((DN_CHUNK, LANES), F32)
            for h in range(N_HEADS):
                dgc = dgc + jnp.where(jnp.logical_and(last_row, lane == N_HEADS + h),
                                      dgls[bi * N_HEADS + h], 0.0)
            dgc_ref[bi] = dgc

    def rev(c):
        return nc - 1 - c

    wide = pl.BlockSpec((b, DN_CHUNK, HALF_W), lambda c: (0, rev(c), 0))
    sq = pl.BlockSpec((b, N_HEADS, DN_CHUNK, DN_CHUNK), lambda c: (0, 0, rev(c), 0))
    gates = pl.BlockSpec((b, DN_CHUNK, LANES), lambda c: (0, rev(c), 0))
    return _call(
        body, name=name, grid=(nc,),
        in_specs=[wide, wide, wide, wide, wide, sq, gates,
                  pl.BlockSpec((1, bh, HEAD_DIM, HEAD_DIM), lambda c: (rev(c), 0, 0, 0))],
        out_specs=[wide, wide, wide, wide, sq, gates],
        out_shape=[_sds((b, t, HALF_W), F32)] * 4
        + [_sds((b, N_HEADS, t, DN_CHUNK), F32), _sds((b, t, LANES), F32)],
        scratch=[pltpu.VMEM((bh, HEAD_DIM, HEAD_DIM), F32)],
    )(do, u, w, qd, kd, qk, gc, s_in)


def dn_chunk_bwd(qkv, proj3, alog_row, dtb_row, tinv, u, w, du, dw, dqd, dkd, dqk, dgc_scan, dproj3, name,
                 comm=None):
    b, t, _ = qkv.shape
    rblk = min(256, t)
    n_in = rblk // DN_CHUNK

    def body(q_ref, k_ref, v_ref, pba_ref, al_ref, dtb_ref, ti_ref, u_ref, w_ref,
             du_ref, dw_ref, dqd_ref, dkd_ref, dqk_ref, dgs_ref, dproj_in,
             dq_ref, dpba_ref, dal_ref, ddtb_ref):
        @pl.when(jnp.logical_and(pl.program_id(0) == 0, pl.program_id(1) == 0))
        def _():
            dal_ref[...] = jnp.zeros_like(dal_ref)
            ddtb_ref[...] = jnp.zeros_like(ddtb_ref)

        ea = jnp.exp(al_ref[...])
        incl0 = _chunk_masks()[0]
        tri = jnp.where(incl0, 1.0, 0.0)
        tri_up = jnp.where(_iota2((DN_CHUNK, DN_CHUNK), 1) >= _iota2((DN_CHUNK, DN_CHUNK), 0), 1.0, 0.0)
        lane = _iota2((DN_CHUNK, LANES), 1)
        last_col = _iota2((DN_CHUNK, 1), 0) == DN_CHUNK - 1

        _, strict, _ = _chunk_masks()
        gate_lane = jnp.logical_and(lane >= N_HEADS, lane < 2 * N_HEADS)

        def chunk_group(cg, carry):
            tiles, items = [], []
            for sub in range(LOCKSTEP_CHUNKS):
                rows = pl.ds(pl.multiple_of((cg * LOCKSTEP_CHUNKS + sub) * DN_CHUNK, DN_CHUNK), DN_CHUNK)
                pba = pba_ref[0, rows, :]
                beta_all, g_all = _gates(pba, ea, dtb_ref[...])
                gc = _dot_exact_lhs(tri, g_all)
                tiles.append((rows, pba, beta_all, g_all))
                for h in range(N_HEADS):
                    items.append((sub, rows, h, slice(h * HEAD_DIM, (h + 1) * HEAD_DIM),
                                  beta_all[:, h:h + 1], gc[:, N_HEADS + h:N_HEADS + h + 1]))
            decays = [_chunk_decay(gcol)[0] for _, _, _, _, _, gcol in items]
            egs = [jnp.exp(gcol) for _, _, _, _, _, gcol in items]
            qbs = [(q_ref[0, rows, cs] * QK_SCALE).astype(BF16) for _, rows, h, cs, _, _ in items]
            kfs = [k_ref[0, rows, cs].astype(BF16) for _, rows, h, cs, _, _ in items]
            kbs = [k_ref[0, rows, cs] * beta for _, rows, h, cs, beta, _ in items]
            kbbs = [kb.astype(BF16) for kb in kbs]
            tsps = [_split(ti_ref[0, h, rows, :]) for _, rows, h, cs, _, _ in items]
            drus = [_dot3(tsp, _split(du_ref[0, rows, cs]), TN)
                    for (_, rows, h, cs, _, _), tsp in zip(items, tsps)]
            drws = [_dot3(tsp, _split(dw_ref[0, rows, cs]), TN)
                    for (_, rows, h, cs, _, _), tsp in zip(items, tsps)]
            m_kks = [_dot(kbb, kf, NT) for kbb, kf in zip(kbbs, kfs)]
            a_qks = [_dot(qb, kf, NT) for qb, kf in zip(qbs, kfs)]
            dls = [-jnp.where(strict, _dot3(_split(dru), _split(u_ref[0, rows, cs]), NT)
                              + _dot3(_split(drw), _split(w_ref[0, rows, cs]), NT), 0.0)
                   for (_, rows, h, cs, _, _), dru, drw in zip(items, drus, drws)]
            dms = [(dl * dc).astype(BF16) for dl, dc in zip(dls, decays)]
            das = [(dqk_ref[0, h, rows, :] * dc).astype(BF16)
                   for (_, rows, h, cs, _, _), dc in zip(items, decays)]
            dkb_mm = [_dot(dm, kf, NN) for dm, kf in zip(dms, kfs)]
            dk_mm = [_dot(dm, kbb, TN) + _dot(da, qb, TN) for dm, kbb, da, qb in zip(dms, kbbs, das, qbs)]
            dqs_mm = [_dot(da, kf, NN) for da, kf in zip(das, kfs)]
            dgc_tiles = [dgs_ref[0, rows, :] for rows, _, _, _ in tiles]
            dbeta_tiles = [jnp.zeros((DN_CHUNK, LANES), F32) for _ in tiles]
            for n_it, (sub, rows, h, cs, beta, gcol) in enumerate(items):
                eg, dc = egs[n_it], decays[n_it]
                k = k_ref[0, rows, cs]
                q = q_ref[0, rows, cs] * QK_SCALE
                kb, dru, drw = kbs[n_it], drus[n_it], drws[n_it]
                ek = jnp.exp(gcol[DN_CHUNK - 1:DN_CHUNK, :] - gcol)
                e_mat = (dls[n_it] * m_kks[n_it] + dqk_ref[0, h, rows, :] * a_qks[n_it]) * dc
                dkb = drw * eg + dkb_mm[n_it]
                dqd = dqd_ref[0, rows, cs]
                dkd = dkd_ref[0, rows, cs]
                kdk = dkd * k * ek
                kdk_total = jnp.sum(jnp.sum(kdk, axis=0, keepdims=True), axis=1, keepdims=True)
                dg = (jnp.sum(drw * kb * eg + dqd * q * eg - kdk, axis=-1, keepdims=True)
                      + jnp.sum(e_mat, axis=1, keepdims=True)
                      - _row_to_col(jnp.sum(e_mat, axis=0, keepdims=True))
                      + jnp.where(last_col, kdk_total, 0.0))
                dbeta = jnp.sum(dkb * k + dru * v_ref[0, rows, cs], axis=-1, keepdims=True)
                dq_ref[0, rows, cs] = (dqs_mm[n_it] + dqd * eg) * QK_SCALE
                dq_ref[0, rows, pl.ds(HALF_W + h * HEAD_DIM, HEAD_DIM)] = dk_mm[n_it] + dkd * ek + dkb * beta
                dq_ref[0, rows, pl.ds(2 * HALF_W + h * HEAD_DIM, HEAD_DIM)] = dru * beta
                dgc_tiles[sub] = dgc_tiles[sub] + jnp.where(lane == N_HEADS + h, dg, 0.0)
                dbeta_tiles[sub] = dbeta_tiles[sub] + jnp.where(lane == h, dbeta, 0.0)
            for (rows, pba, beta_all, g_all), dgc_tile, dbeta_tile in zip(tiles, dgc_tiles, dbeta_tiles):
                dg_tile = _dot_exact_lhs(tri_up, dgc_tile)
                da_pre = dg_tile * (-ea) * _sigmoid(pba + dtb_ref[...])
                dal_ref[...] += jnp.sum(jnp.where(gate_lane, dg_tile * g_all, 0.0), axis=0, keepdims=True)
                ddtb_ref[...] += jnp.sum(jnp.where(gate_lane, da_pre, 0.0), axis=0, keepdims=True)
                dpba_ref[0, rows, :] = jnp.where(lane < N_HEADS, dbeta_tile * beta_all * (1.0 - beta_all),
                                                 jnp.where(gate_lane, da_pre, 0.0)).astype(BF16)
            return carry

        lax.fori_loop(0, n_in // LOCKSTEP_CHUNKS, chunk_group, 0)

    def seg(cblk):
        return pl.BlockSpec((1, rblk, HALF_W), lambda i, r: (i, r, cblk))

    vec = pl.BlockSpec((1, LANES), lambda i, r: (0, 0))
    wide = pl.BlockSpec((1, rblk, HALF_W), lambda i, r: (i, r, 0))
    sq = pl.BlockSpec((1, N_HEADS, rblk, DN_CHUNK), lambda i, r: (i, 0, r, 0))
    gates = pl.BlockSpec((1, rblk, LANES), lambda i, r: (i, r, 0))
    return _call(
        body, name=name, grid=(b, t // rblk),
        in_specs=[seg(0), seg(1), seg(2),
                  pl.BlockSpec((1, rblk, LANES), lambda i, r: (i, r, GATE_COL_BLOCK)), vec, vec,
                  sq, wide, wide, wide, wide, wide, wide, sq, gates, _ANY],
        out_specs=[pl.BlockSpec((1, rblk, 3 * HALF_W), lambda i, r: (i, r, 0)),
                   pl.BlockSpec((1, rblk, LANES), lambda i, r: (i, r, GATE_COL_BLOCK)), vec, vec],
        out_shape=[_sds((b, t, 3 * HALF_W), F32), _sds(dproj3.shape, BF16),
                   _sds((1, LANES), F32), _sds((1, LANES), F32)],
        input_output_aliases={15: 1}, comm=comm,
    )(qkv, qkv, qkv, proj3, alog_row, dtb_row, tinv, u, w, du, dw, dqd, dkd, dqk, dgc_scan, dproj3)


def dn_out_fwd(o, proj, dn_norm, name):
    n = o.shape[0]
    tm = min(ROW_TILE, n)

    def body(o_ref, z_ref, g_ref, y_ref):
        for h in range(N_HEADS):
            cs = slice(h * HEAD_DIM, (h + 1) * HEAD_DIM)
            oh = o_ref[:, cs]
            z = z_ref[:, cs]
            y = oh * _rms_scale(oh) * g_ref[...]
            y_ref[:, cs] = (y * (z * _sigmoid(z))).astype(BF16)

    half = pl.BlockSpec((tm, HALF_W), lambda i: (i, 0))
    return _call(
        body, name=name, grid=(n // tm,),
        in_specs=[half, pl.BlockSpec((tm, HALF_W), lambda i: (i, 5)),
                  pl.BlockSpec((1, HEAD_DIM), lambda i: (0, 0))],
        out_specs=half, out_shape=_sds((n, HALF_W), BF16),
    )(o, proj, dn_norm)


def dn_out_bwd(dy, o, proj, dn_norm, dproj, name):
    n = o.shape[0]
    tm = min(ROW_TILE, n)

    def body(dy_ref, o_ref, z_ref, g_ref, dproj_in, do_ref, dz_ref, dg_ref):
        @pl.when(pl.program_id(0) == 0)
        def _():
            dg_ref[...] = jnp.zeros_like(dg_ref)

        g = g_ref[...]
        dg = jnp.zeros_like(g)
        for h in range(N_HEADS):
            cs = slice(h * HEAD_DIM, (h + 1) * HEAD_DIM)
            oh = o_ref[:, cs]
            z = z_ref[:, cs]
            d = dy_ref[:, cs]
            r = _rms_scale(oh)
            nh = oh * r
            sz = _sigmoid(z)
            dyn = d * (z * sz)
            dz_ref[:, cs] = (d * (nh * g) * (sz * (1.0 + z * (1.0 - sz)))).astype(BF16)
            dg = dg + jnp.sum(dyn * nh, axis=0, keepdims=True)
            dn = dyn * g
            do_ref[:, cs] = r * (dn - nh * jnp.mean(dn * nh, axis=-1, keepdims=True))
        dg_ref[...] += dg

    half = pl.BlockSpec((tm, HALF_W), lambda i: (i, 0))
    vec = pl.BlockSpec((1, HEAD_DIM), lambda i: (0, 0))
    return _call(
        body, name=name, grid=(n // tm,),
        in_specs=[half, half, pl.BlockSpec((tm, HALF_W), lambda i: (i, 5)), vec, _ANY],
        out_specs=[half, pl.BlockSpec((tm, HALF_W), lambda i: (i, 5)), vec],
        out_shape=[_sds((n, HALF_W), F32), _sds(dproj.shape, BF16), _sds((1, HEAD_DIM), F32)],
        input_output_aliases={4: 1},
    )(dy, o, proj, dn_norm, dproj)


def _adamw_math(w, g, m, v):
    m_new = ADAM_B1 * m + (1.0 - ADAM_B1) * g
    v_new = ADAM_B2 * v + (1.0 - ADAM_B2) * (g * g)
    m_hat = m_new / (1.0 - ADAM_B1 ** ADAM_STEP)
    v_hat = v_new / (1.0 - ADAM_B2 ** ADAM_STEP)
    delta = -ADAM_LR * (m_hat / (jnp.sqrt(v_hat) + ADAM_EPS) + ADAM_WD * w)
    return delta, m_new, v_new


def adamw(w, g, m, v, name):
    r, c = w.shape
    tr = r
    for cand in (256, 352):
        if r % cand == 0 and r > cand:
            tr = cand
            break

    def body(w_ref, g_ref, m_ref, v_ref, d_ref, mo_ref, vo_ref):
        d, mn, vn = _adamw_math(w_ref[...], g_ref[...], m_ref[...], v_ref[...])
        d_ref[...] = d
        mo_ref[...] = mn
        vo_ref[...] = vn

    spec = pl.BlockSpec((tr, c), lambda i: (i, 0))
    return _call(
        body, name=name, grid=(r // tr,),
        in_specs=[spec] * 4, out_specs=[spec] * 3, out_shape=[_sds((r, c), F32)] * 3,
    )(w, g, m, v)


def _place():
    return lax.axis_index("x"), lax.axis_index("y"), lax.axis_index("c")


def _other_chips(x, y):
    return [(1 - x, y), (x, 1 - y), (1 - x, 1 - y)]


_ANY = pl.BlockSpec(memory_space=pl.ANY)


def cast_place(w, shard_idx, name):
    r, cols = w.shape
    tr = r // 2

    def body(j_ref, w_ref, o_ref):
        o_ref[0] = w_ref[...].astype(BF16)

    return pl.pallas_call(
        body, name=name,
        grid_spec=pltpu.PrefetchScalarGridSpec(
            num_scalar_prefetch=1, grid=(r // tr,),
            in_specs=[pl.BlockSpec((tr, cols), lambda i, j: (i, 0))],
            out_specs=pl.BlockSpec((1, tr, cols), lambda i, j: (j[0], i, 0))),
        out_shape=_sds((N_SHARD, r, cols), BF16),
        compiler_params=pltpu.CompilerParams(dimension_semantics=("arbitrary",),
                                             vmem_limit_bytes=VMEM_LIMIT),
    )(shard_idx, w)


class Exchange:
    def __init__(self, inputs, out_shape, aliases, sems, phases):
        self.inputs, self.out_shape, self.aliases = list(inputs), list(out_shape), dict(aliases)
        self.sems, self.phases = list(sems), list(phases)


def run_exchange(ex, name):
    def body(*refs):
        n_in, n_out = len(ex.inputs), len(ex.out_shape)
        for _, fn in ex.phases:
            fn(refs[:n_in], refs[n_in:n_in + n_out], refs[n_in + n_out:])

    return _call(body, name=name, in_specs=[_ANY] * len(ex.inputs), out_specs=[_ANY] * len(ex.out_shape),
                 out_shape=ex.out_shape, scratch=ex.sems, input_output_aliases=ex.aliases)(*ex.inputs)


def merge_exchanges(exs):
    inputs, out_shape, sems, aliases, phases, out_slices = [], [], [], {}, [], []
    for ex in exs:
        i0, o0, s0 = len(inputs), len(out_shape), len(sems)
        inputs += ex.inputs
        out_shape += ex.out_shape
        sems += ex.sems
        for k, m in ex.aliases.items():
            aliases[i0 + k] = o0 + m
        si, so, ss = slice(i0, len(inputs)), slice(o0, len(out_shape)), slice(s0, len(sems))
        out_slices.append(so)
        for step, fn in ex.phases:
            phases.append((step, lambda ins, outs, sm, fn=fn, si=si, so=so, ss=ss: fn(ins[si], outs[so], sm[ss])))
    return Exchange(inputs, out_shape, aliases, sems, phases), out_slices


def _dma_sems(*sizes):
    return [pltpu.SemaphoreType.DMA((s,)) for s in sizes]


def gather_exchange(bufs, small=None, relay_step=-2):
    n = len(bufs)
    n_small = 0 if small is None else 1

    def half(outs, a, blk, hc):
        rh = bufs[a].shape[1] // 2
        return outs[a].at[blk, pl.ds(hc * rh, rh), :]

    def ici(outs, sems, a, k, blk, to):
        return pltpu.make_async_remote_copy(
            src_ref=half(outs, a, blk, to[2]), dst_ref=half(outs, a, blk, to[2]), send_sem=sems[0].at[3 * a + k],
            recv_sem=sems[1].at[3 * a + k], device_id=to, device_id_type=MESH)

    def d2d(outs, sems, a, k, blk, hc, to):
        return pltpu.make_async_remote_copy(
            src_ref=half(outs, a, blk, hc), dst_ref=half(outs, a, blk, hc), send_sem=sems[2].at[3 * a + k],
            recv_sem=sems[3].at[3 * a + k], device_id=to, device_id_type=MESH)

    def small_copy(ins, outs, sems, k, blk, to):
        return pltpu.make_async_remote_copy(
            src_ref=ins[n], dst_ref=outs[n].at[blk], send_sem=sems[0].at[3 * n + k],
            recv_sem=sems[1].at[3 * n + k], device_id=to, device_id_type=MESH)

    def start(ins, outs, sems):
        x, y, c = _place()
        j = 2 * x + y
        if n_small:
            pltpu.make_async_copy(ins[n], outs[n].at[j], sems[4].at[0]).start()
        for k, (px, py) in enumerate(_other_chips(x, y)):
            if n_small:
                small_copy(ins, outs, sems, k, j, (px, py, c)).start()
            for a in range(n):
                ici(outs, sems, a, k, j, (px, py, c)).start()

    def relay(ins, outs, sems):
        x, y, c = _place()
        for k, (px, py) in enumerate(_other_chips(x, y)):
            for a in range(n):
                ici(outs, sems, a, k, 2 * px + py, (px, py, c)).wait_recv()
                d2d(outs, sems, a, k, 2 * px + py, c, (x, y, 1 - c)).start()

    def finish(ins, outs, sems):
        x, y, c = _place()
        j = 2 * x + y
        for k, (px, py) in enumerate(_other_chips(x, y)):
            blk = 2 * px + py
            if n_small:
                small_copy(ins, outs, sems, k, blk, (px, py, c)).wait_recv()
                small_copy(ins, outs, sems, k, j, (px, py, c)).wait_send()
            for a in range(n):
                d2d(outs, sems, a, k, blk, 1 - c, (x, y, 1 - c)).wait_recv()
                ici(outs, sems, a, k, j, (px, py, c)).wait_send()
                d2d(outs, sems, a, k, blk, c, (x, y, 1 - c)).wait_send()
        if n_small:
            pltpu.make_async_copy(ins[n], outs[n].at[j], sems[4].at[0]).wait()

    out_shape = [_sds(b.shape, b.dtype) for b in bufs]
    if n_small:
        out_shape.append(_sds((N_SHARD,) + small.shape, small.dtype))
    return Exchange(list(bufs) + ([small] if n_small else []), out_shape, {a: a for a in range(n)},
                    _dma_sems(3 * n + 3, 3 * n + 3, 3 * n, 3 * n, 1),
                    [(0, start), (relay_step, relay), (-1, finish)])


def _start_then_wait(copies):
    def start(ins, outs, sems):
        for sent, _ in copies(ins, outs, sems):
            sent().start()

    def finish(ins, outs, sems):
        pairs = copies(ins, outs, sems)
        for _, arrival in pairs:
            arrival().wait_recv()
        for sent, _ in pairs:
            sent().wait_send()

    return [(0, start), (-1, finish)]


def pair_exchange(arrs):
    n = len(arrs)

    def copies(ins, outs, sems):
        x, y, c = _place()
        res = []
        for a in range(n):
            def mk(a=a):
                rh = arrs[a].shape[1] // 2
                return pltpu.make_async_remote_copy(
                    src_ref=ins[a].at[:, pl.ds((1 - c) * rh, rh), :], dst_ref=outs[a], send_sem=sems[0].at[a],
                    recv_sem=sems[1].at[a], device_id=(x, y, 1 - c), device_id_type=MESH)
            res.append((mk, mk))
        return res

    return Exchange(arrs, [_sds((a.shape[0], a.shape[1] // 2, a.shape[2]), a.dtype) for a in arrs], {},
                    _dma_sems(n, n), _start_then_wait(copies))


def pair_add(g, s, c_idx, name):
    nb, r, cols = g.shape
    rh = r // 2

    def body(c_ref, g_ref, s_ref, o_ref):
        o_ref[...] = (g_ref[...] + s_ref[...]).astype(BF16)

    return pl.pallas_call(
        body, name=name,
        grid_spec=pltpu.PrefetchScalarGridSpec(
            num_scalar_prefetch=1, grid=(nb,),
            in_specs=[pl.BlockSpec((1, rh, cols), lambda j, c: (j, c[0], 0)),
                      pl.BlockSpec((1, rh, cols), lambda j, c: (j, 0, 0))],
            out_specs=pl.BlockSpec((1, rh, cols), lambda j, c: (j, 0, 0))),
        out_shape=_sds((nb, rh, cols), BF16),
        compiler_params=pltpu.CompilerParams(dimension_semantics=("arbitrary",),
                                             vmem_limit_bytes=VMEM_LIMIT),
    )(c_idx, g, s)


def chip_exchange(arrs):
    n = len(arrs)

    def copies(ins, outs, sems):
        x, y, c = _place()
        j = 2 * x + y
        res = []
        for a in range(n):
            for k, (px, py) in enumerate(_other_chips(x, y)):
                def mk(src_blk, dst_blk, a=a, k=k, to=(px, py, c)):
                    return pltpu.make_async_remote_copy(
                        src_ref=ins[a].at[src_blk], dst_ref=outs[a].at[dst_blk], send_sem=sems[0].at[3 * a + k],
                        recv_sem=sems[1].at[3 * a + k], device_id=to, device_id_type=MESH)
                res.append((functools.partial(mk, 2 * px + py, j), functools.partial(mk, j, 2 * px + py)))
        return res

    return Exchange(arrs, [_sds(a.shape, a.dtype) for a in arrs], {}, _dma_sems(3 * n, 3 * n),
                    _start_then_wait(copies))


def sum_chips(r, p, shard_idx, name):
    nb, rh, cols = r.shape
    tr = rh

    def body(j_ref, p_ref, *refs):
        o_ref = refs[nb]
        j = j_ref[0]
        acc = None
        for i in range(nb):
            term = jnp.where(j == i, p_ref[0], refs[i][0]).astype(F32)
            acc = term if acc is None else acc + term
        o_ref[...] = acc

    def slot(i):
        return pl.BlockSpec((1, tr, cols), lambda t, j: (jnp.where(j[0] == i, (i + 1) % nb, i), t, 0))

    return pl.pallas_call(
        body, name=name,
        grid_spec=pltpu.PrefetchScalarGridSpec(
            num_scalar_prefetch=1, grid=(rh // tr,),
            in_specs=[pl.BlockSpec((1, tr, cols), lambda t, j: (j[0], t, 0))] + [slot(i) for i in range(nb)],
            out_specs=pl.BlockSpec((tr, cols), lambda t, j: (t, 0))),
        out_shape=_sds((rh, cols), F32),
        compiler_params=pltpu.CompilerParams(dimension_semantics=("arbitrary",),
                                             vmem_limit_bytes=VMEM_LIMIT),
    )(shard_idx, p, *([r] * nb))


def pair_swap(arrs):
    n = len(arrs)

    def copies(ins, outs, sems):
        x, y, c = _place()
        res = []
        for a in range(n):
            def mk(a=a):
                return pltpu.make_async_remote_copy(
                    src_ref=ins[a], dst_ref=outs[a], send_sem=sems[0].at[a], recv_sem=sems[1].at[a],
                    device_id=(x, y, 1 - c), device_id_type=MESH)
            res.append((mk, mk))
        return res

    return Exchange(arrs, [_sds(a.shape, a.dtype) for a in arrs], {}, _dma_sems(n, n),
                    _start_then_wait(copies))


ADAMW_STEPS_PER_HALF = 4


def adamw_pairs(items, name, comm=None):
    n_items = len(items)
    nh = ADAMW_STEPS_PER_HALF

    def body(*refs):
        ins, outs = refs[:5 * n_items], refs[5 * n_items:]
        mine = (pl.program_id(0) // nh) == lax.axis_index("c")
        for a in range(n_items):
            w_ref, gm_ref, gs_ref, m_ref, v_ref = ins[5 * a:5 * a + 5]
            g_ref, d_ref, mo_ref, vo_ref = outs[4 * a:4 * a + 4]
            g = jnp.where(mine, gm_ref[...], gs_ref[...])
            d, mn, vn = _adamw_math(w_ref[...], g, m_ref[...], v_ref[...])
            g_ref[...] = g
            d_ref[...] = d
            mo_ref[...] = mn
            vo_ref[...] = vn

    in_specs, out_specs, out_shape, args = [], [], [], []
    for w, g_mine, g_sib, m, v in items:
        r, cols = w.shape
        tr = r // (2 * nh)
        full = pl.BlockSpec((tr, cols), lambda i: (i, 0))
        part = pl.BlockSpec((tr, cols), lambda i: (i % nh, 0))
        in_specs += [full, part, part, full, full]
        out_specs += [full] * 4
        out_shape += [_sds((r, cols), F32)] * 4
        args += [w, g_mine, g_sib, m, v]
    res = _call(body, name=name, grid=(2 * nh,), in_specs=in_specs, out_specs=out_specs,
                out_shape=out_shape, comm=comm)(*args)
    own, hosted = (res, None) if comm is None else res
    grouped = [tuple(own[4 * a:4 * a + 4]) for a in range(n_items)]
    return grouped if comm is None else (grouped, hosted)


N_DEV = 8


def device_gather(pack):
    def copies(ins, outs, sems):
        x, y, c = _place()
        me = 4 * x + 2 * y + c
        res = []
        for k in range(1, N_DEV):
            fx, fy, fc = (k >> 2) & 1, (k >> 1) & 1, k & 1
            px, py, pc = (1 - x if fx else x, 1 - y if fy else y, 1 - c if fc else c)

            def mk(slot, k=k, to=(px, py, pc)):
                return pltpu.make_async_remote_copy(
                    src_ref=ins[0], dst_ref=outs[0].at[slot], send_sem=sems[0].at[k - 1],
                    recv_sem=sems[1].at[k - 1], device_id=to, device_id_type=MESH)
            res.append((functools.partial(mk, me), functools.partial(mk, 4 * px + 2 * py + pc)))
        return res

    return Exchange([pack], [_sds((N_DEV,) + pack.shape, pack.dtype)], {}, _dma_sems(N_DEV - 1, N_DEV - 1),
                    _start_then_wait(copies))


def sum_devices(buf, pack, me_idx, name):
    r, cols = pack.shape

    def body(me_ref, p_ref, *refs):
        o_ref = refs[N_DEV]
        acc = None
        for i in range(N_DEV):
            term = jnp.where(me_ref[0] == i, p_ref[...], refs[i][0])
            acc = term if acc is None else acc + term
        o_ref[...] = acc

    def slot(i):
        return pl.BlockSpec((1, r, cols), lambda t, me: (jnp.where(me[0] == i, (i + 1) % N_DEV, i), 0, 0))

    whole = pl.BlockSpec((r, cols), lambda t, me: (0, 0))
    return pl.pallas_call(
        body, name=name,
        grid_spec=pltpu.PrefetchScalarGridSpec(
            num_scalar_prefetch=1, grid=(1,),
            in_specs=[whole] + [slot(i) for i in range(N_DEV)], out_specs=whole),
        out_shape=_sds((r, cols), F32),
        compiler_params=pltpu.CompilerParams(dimension_semantics=("arbitrary",),
                                             vmem_limit_bytes=VMEM_LIMIT),
    )(me_idx, pack, *([buf] * N_DEV))


SMALL_NAMES = ("ffn1_norm", "mix_norm", "ffn2_norm", "final_norm", "sg_ln_g", "sg_ln_b",
               "dn_norm", "a_log", "dt_bias", "sg_b", "sg_w", "conv_w", "loss")


def _to_rows(a):
    flat = a.reshape(-1)
    pad = (-flat.shape[0]) % LANES
    if pad:
        flat = jnp.pad(flat, (0, pad))
    return flat.reshape(-1, LANES)


def _pack_small(parts):
    rows = [_to_rows(parts[k]) for k in SMALL_NAMES]
    pack = jnp.concatenate(rows, axis=0)
    pad = (-pack.shape[0]) % 8
    if pad:
        pack = jnp.pad(pack, ((0, pad), (0, 0)))
    return pack


def _unpack_small(pack, shapes):
    out, r0 = {}, 0
    for k in SMALL_NAMES:
        size = 1
        for s in shapes[k]:
            size *= s
        nrows = -(-size // LANES)
        out[k] = pack[r0:r0 + nrows].reshape(-1)[:size].reshape(shapes[k])
        r0 += nrows
    return out


def kernel(x, ffn1_norm, ffn1_w_gate, ffn1_w_up, ffn1_w_down, mix_norm, w_in, conv_w, a_log, dt_bias, dn_norm, sg_ln_g, sg_ln_b, sg_w, sg_b, w_out, ffn2_norm, ffn2_w_gate, ffn2_w_up, ffn2_w_down, final_norm, loss_target, m_ffn1_norm, m_ffn1_w_gate, m_ffn1_w_up, m_ffn1_w_down, m_mix_norm, m_w_in, m_conv_w, m_a_log, m_dt_bias, m_dn_norm, m_sg_ln_g, m_sg_ln_b, m_sg_w, m_sg_b, m_w_out, m_ffn2_norm, m_ffn2_w_gate, m_ffn2_w_up, m_ffn2_w_down, m_final_norm, v_ffn1_norm, v_ffn1_w_gate, v_ffn1_w_up, v_ffn1_w_down, v_mix_norm, v_w_in, v_conv_w, v_a_log, v_dt_bias, v_dn_norm, v_sg_ln_g, v_sg_ln_b, v_sg_w, v_sg_b, v_w_out, v_ffn2_norm, v_ffn2_w_gate, v_ffn2_w_up, v_ffn2_w_down, v_final_norm):
    bsz, t_len, d = x.shape
    n = bsz * t_len
    xy, yy, cc = _place()
    shard = 2 * xy + yy

    big_names = ["ffn1_w_gate", "ffn1_w_up", "ffn1_w_down", "w_in", "w_out",
                 "ffn2_w_gate", "ffn2_w_up", "ffn2_w_down"]
    big_w = dict(ffn1_w_gate=ffn1_w_gate, ffn1_w_up=ffn1_w_up, ffn1_w_down=ffn1_w_down, w_in=w_in,
                 w_out=w_out, ffn2_w_gate=ffn2_w_gate, ffn2_w_up=ffn2_w_up, ffn2_w_down=ffn2_w_down)
    big_m = dict(ffn1_w_gate=m_ffn1_w_gate, ffn1_w_up=m_ffn1_w_up, ffn1_w_down=m_ffn1_w_down, w_in=m_w_in,
                 w_out=m_w_out, ffn2_w_gate=m_ffn2_w_gate, ffn2_w_up=m_ffn2_w_up, ffn2_w_down=m_ffn2_w_down)
    big_v = dict(ffn1_w_gate=v_ffn1_w_gate, ffn1_w_up=v_ffn1_w_up, ffn1_w_down=v_ffn1_w_down, w_in=v_w_in,
                 w_out=v_w_out, ffn2_w_gate=v_ffn2_w_gate, ffn2_w_up=v_ffn2_w_up, ffn2_w_down=v_ffn2_w_down)
    shard_idx = jnp.reshape(shard, (1,)).astype(jnp.int32)
    c_idx = jnp.reshape(cc, (1,)).astype(jnp.int32)
    transposed = ("ffn1_w_gate", "ffn1_w_up", "ffn2_w_gate", "ffn2_w_up")

    def as2d(a, k):
        return a[0].T if k in transposed else a[0]

    def from2d(a, k):
        return a.T[None] if k in transposed else a[None]

    placed = {k: cast_place(as2d(big_w[k], k), shard_idx, name="cast_" + k) for k in big_names}
    first_names = big_names[:3]
    later_names = big_names[3:]
    res = run_exchange(gather_exchange([placed[k] for k in first_names], conv_w[0]), name="gather_first")
    gw = dict(zip(first_names, res[:3]))
    conv_full = res[3].transpose(1, 0, 2).reshape(CONV_K, 3 * HALF_W)

    x0 = x.reshape(n, d)
    def ffn_weights(prefix):
        return [gw[prefix + k].reshape(-1, d) for k in ("_w_gate", "_w_up", "_w_down")]

    def ffn_grad_blocks(grads):
        return [g.reshape(N_SHARD, -1, d) for g in grads]

    (x1, h1, adg1, adu1, act1), later = ffn_fwd(
        x0, ffn1_norm, *ffn_weights("ffn1"), name="ffn1_fwd",
        comm=gather_exchange([placed[k] for k in later_names]))
    gw.update(zip(later_names, later))
    w_in_full = gw["w_in"].transpose(1, 0, 2).reshape(d, IN_COLS)
    w_in_full = jnp.pad(w_in_full, ((0, 0), (0, PROJ_W - IN_COLS)))
    w_out_full = gw["w_out"].reshape(2 * HALF_W, d)
    proj, h2 = in_proj_fwd(x1, mix_norm, w_in_full, name="in_proj_fwd")
    proj3 = proj.reshape(bsz, t_len, PROJ_W)
    bias_tile = jnp.repeat(sg_b[0].T, SG_GROUP_DIM, axis=1)
    sg_out = sg_fwd(proj, sg_ln_g, sg_ln_b, sg_w[0], bias_tile, name="sg_fwd")
    qkv = dn_conv_fwd(proj3, conv_full, name="dn_conv_fwd")
    alog_row = jnp.zeros((1, LANES), F32).at[0, N_HEADS:2 * N_HEADS].set(a_log[0])
    dtb_row = jnp.zeros((1, LANES), F32).at[0, N_HEADS:2 * N_HEADS].set(dt_bias[0])
    u_wy, w_wy, q_dec, k_dec, qk, tinv, gc = dn_chunk_fwd(qkv, proj3, alog_row, dtb_row,
                                                           name="dn_chunk_fwd")
    o, s_in = dn_scan_fwd(u_wy, w_wy, q_dec, k_dec, qk, gc, name="dn_scan_fwd")
    dn_out = dn_out_fwd(o.reshape(n, HALF_W), proj, dn_norm, name="dn_out_fwd")
    x2 = out_proj_fwd(x1, sg_out, dn_out, w_out_full, name="out_proj_fwd")
    x3, h3, adg2, adu2, act2 = ffn_fwd(x2, ffn2_norm, *ffn_weights("ffn2"), name="ffn2_fwd")
    dx3, d_final_norm, loss_tile = final_loss(x3, final_norm.reshape(1, d),
                                              loss_target.reshape(n, d), name="final_loss")

    dx2, dgate2, dup2, dyh2, d_ffn2_norm = ffn_bwd_act(
        dx3, x2, ffn2_norm, adg2, adu2, *ffn_weights("ffn2"), name="ffn2_bwd_act")
    g_big = {}
    g_big["ffn2_w_gate"], g_big["ffn2_w_up"], g_big["ffn2_w_down"] = ffn_grad_blocks(ffn_bwd_w(
        [dgate2, dup2, act2], [h3, dyh2], [(0, 0), (1, 0), (2, 1)], name="ffn2_bwd_w"))

    early = ["ffn2_w_gate", "ffn2_w_up", "ffn2_w_down"]
    (d_sg, d_dn, dx2b), early_sib = out_proj_bwd_x(dx2, w_out_full, name="out_proj_bwd_x",
                                                   comm=pair_exchange([g_big[k] for k in early]))
    early_sums = [pair_add(g_big[k], s, c_idx, name="grad_pair_add_" + k) for k, s in zip(early, early_sib)]
    g_w_out = matmul_tn([sg_out, dn_out], dx2b, d, name="w_out_grad")
    g_big["w_out"] = g_w_out.reshape(N_SHARD, (2 * HALF_W) // N_SHARD, d)

    d_proj, d_sg_w, d_bias_tile, d_ln_g, d_ln_b = sg_bwd(d_sg, proj, sg_ln_g, sg_ln_b, sg_w[0],
                                                         bias_tile, name="sg_bwd")
    d_o, d_proj, d_dn_norm = dn_out_bwd(d_dn, o.reshape(n, HALF_W), proj, dn_norm, d_proj,
                                        name="dn_out_bwd")
    du, dw, dqd, dkd, dqk, dgc_scan = dn_scan_bwd(d_o.reshape(bsz, t_len, HALF_W), u_wy, w_wy, q_dec,
                                                  k_dec, qk, gc, s_in, name="dn_scan_bwd")
    (d_qkv, d_proj3, d_alog_row, d_dtb_row), early_chips = dn_chunk_bwd(
        qkv, proj3, alog_row, dtb_row, tinv, u_wy, w_wy, du, dw, dqd, dkd, dqk, dgc_scan,
        d_proj.reshape(bsz, t_len, PROJ_W), name="dn_chunk_bwd", comm=chip_exchange(early_sums))
    early_halves = [sum_chips(r, p, shard_idx, name="grad_chip_sum_" + k)
                    for k, r, p in zip(early, early_chips, early_sums)]
    (d_proj3, d_conv), early_sib_halves = dn_conv_bwd(d_qkv, proj3, conv_full, d_proj3, name="dn_conv_bwd",
                                                      comm=pair_swap(early_halves))
    d_proj = d_proj3.reshape(n, PROJ_W)
    g_w_in = matmul_tn([h2], d_proj, 3 * MXU_DIM, name="w_in_grad")[:, :IN_COLS]
    g_big["w_in"] = g_w_in.reshape(d, N_SHARD, IN_COLS // N_SHARD).transpose(1, 0, 2)

    def reduce_start(names):
        return pair_exchange([g_big[k] for k in names])

    def reduce_pair_sums(names, from_sib):
        return [pair_add(g_big[k], s, c_idx, name="grad_pair_add_" + k) for k, s in zip(names, from_sib)]

    def reduce_chip_sums(names, from_chips, sums):
        return [sum_chips(r, p, shard_idx, name="grad_chip_sum_" + k)
                for k, r, p in zip(names, from_chips, sums)]

    mid = ["w_in", "w_out"]
    (dx1, d_mix_norm, dyh1), mid_sib = in_proj_bwd_x(d_proj, w_in_full, x1, mix_norm, dx2,
                                                     name="in_proj_bwd_x", comm=reduce_start(mid))
    mid_sums = reduce_pair_sums(mid, mid_sib)
    down = ["ffn1_w_down"]
    (g_down,), mid_chips = ffn_bwd_w([act1], [dyh1], [(0, 0)], name="ffn1_bwd_w_down",
                                     comm=chip_exchange(mid_sums))
    g_big["ffn1_w_down"] = g_down.reshape(N_SHARD, -1, d)
    mid_halves = reduce_chip_sums(mid, mid_chips, mid_sums)
    leg, legs = merge_exchanges([reduce_start(down), pair_swap(mid_halves)])
    leg_res = run_exchange(leg, name="grad_pair_exchange_down")
    down_sums = reduce_pair_sums(down, leg_res[legs[0]])
    mid_sib_halves = leg_res[legs[1]]

    dx0, dgate1, dup1, _, d_ffn1_norm = ffn_bwd_act(
        dx1, x0, ffn1_norm, adg1, adu1, *ffn_weights("ffn1"), name="ffn1_bwd_act")
    grad_x = dx0.reshape(bsz, t_len, d)
    d_sg_b = d_bias_tile.reshape(SG_CHUNK, SG_GROUPS, SG_GROUP_DIM).sum(axis=-1).T
    small_g = dict(ffn1_norm=d_ffn1_norm, mix_norm=d_mix_norm, ffn2_norm=d_ffn2_norm,
                   final_norm=d_final_norm, sg_ln_g=d_ln_g, sg_ln_b=d_ln_b, dn_norm=d_dn_norm,
                   a_log=d_alog_row[:, N_HEADS:2 * N_HEADS], dt_bias=d_dtb_row[:, N_HEADS:2 * N_HEADS],
                   sg_b=d_sg_b, sg_w=d_sg_w, conv_w=d_conv, loss=loss_tile[:, :1])
    my_pack = _pack_small(small_g)
    hosted, parts = merge_exchanges([chip_exchange(down_sums), device_gather(my_pack)])
    late = ["ffn1_w_gate", "ffn1_w_up"]
    late_grads, hosted_res = ffn_bwd_w([dgate1, dup1], [h1], [(0, 0), (1, 0)], name="ffn1_bwd_w_gate_up",
                                       comm=hosted)
    g_big["ffn1_w_gate"], g_big["ffn1_w_up"] = ffn_grad_blocks(late_grads)
    down_halves = reduce_chip_sums(down, hosted_res[parts[0]], down_sums)
    (all_packs,) = hosted_res[parts[1]]

    leg, legs = merge_exchanges([reduce_start(late), pair_swap(down_halves)])
    leg_res = run_exchange(leg, name="grad_pair_exchange")
    pair_sums = reduce_pair_sums(late, leg_res[legs[0]])
    down_sib_halves = leg_res[legs[1]]

    def adam_items(names, mine, sib):
        return [(as2d(big_w[k], k), gm, gs, as2d(big_m[k], k), as2d(big_v[k], k))
                for k, gm, gs in zip(names, mine, sib)]

    outs = {}
    done = adamw_pairs(
        adam_items(early + mid + down, early_halves + mid_halves + down_halves,
                   list(early_sib_halves) + list(mid_sib_halves) + list(down_sib_halves)),
        name="adamw_early")
    from_chips = run_exchange(chip_exchange(pair_sums), name="grad_chip_exchange")
    halves = reduce_chip_sums(late, from_chips, pair_sums)
    sib_halves = run_exchange(pair_swap(halves), name="grad_pair_swap")
    done += adamw_pairs(adam_items(late, halves, sib_halves), name="adamw_late")
    for k, res in zip(early + mid + down + late, done):
        outs[k] = tuple(from2d(a, k) for a in res)

    small_w = dict(ffn1_norm=ffn1_norm, mix_norm=mix_norm, ffn2_norm=ffn2_norm, final_norm=final_norm,
                   sg_ln_g=sg_ln_g, sg_ln_b=sg_ln_b, dn_norm=dn_norm, a_log=a_log, dt_bias=dt_bias,
                   sg_b=sg_b, sg_w=sg_w)
    small_m = dict(ffn1_norm=m_ffn1_norm, mix_norm=m_mix_norm, ffn2_norm=m_ffn2_norm,
                   final_norm=m_final_norm, sg_ln_g=m_sg_ln_g, sg_ln_b=m_sg_ln_b, dn_norm=m_dn_norm,
                   a_log=m_a_log, dt_bias=m_dt_bias, sg_b=m_sg_b, sg_w=m_sg_w)
    small_v = dict(ffn1_norm=v_ffn1_norm, mix_norm=v_mix_norm, ffn2_norm=v_ffn2_norm,
                   final_norm=v_final_norm, sg_ln_g=v_sg_ln_g, sg_ln_b=v_sg_ln_b, dn_norm=v_dn_norm,
                   a_log=v_a_log, dt_bias=v_dt_bias, sg_b=v_sg_b, sg_w=v_sg_w)
    shapes = {k: small_w[k].shape for k in small_w}
    shapes["conv_w"] = (CONV_K, 3 * HALF_W)
    shapes["loss"] = (1, 1)
    me_idx = jnp.reshape(4 * xy + 2 * yy + cc, (1,)).astype(jnp.int32)
    g_pack = sum_devices(all_packs, my_pack, me_idx, name="small_sum")
    g_small = _unpack_small(g_pack, shapes)
    loss = g_small["loss"].reshape(())
    cw = 3 * HALF_W // N_SHARD
    g_conv = lax.dynamic_slice_in_dim(g_small["conv_w"], shard * cw, cw, axis=1)
    zero_conv = jnp.zeros((CONV_K, 3 * HALF_W), F32)

    def packed(src, conv):
        parts = dict(src)
        parts["conv_w"] = lax.dynamic_update_slice_in_dim(zero_conv, conv[0], shard * cw, axis=1)
        parts["loss"] = jnp.zeros((1, 1), F32)
        return _pack_small(parts)

    d_pack, m_pack, v_pack = adamw(packed(small_w, conv_w), g_pack, packed(small_m, m_conv_w),
                                   packed(small_v, v_conv_w), name="adamw_small")
    d_small = _unpack_small(d_pack, shapes)
    m_small = _unpack_small(m_pack, shapes)
    v_small = _unpack_small(v_pack, shapes)

    def conv_block(full_arr):
        return lax.dynamic_slice_in_dim(full_arr, shard * cw, cw, axis=1)[None]

    for k in small_w:
        outs[k] = (g_small[k].reshape(small_w[k].shape), d_small[k], m_small[k], v_small[k])
    outs["conv_w"] = (g_conv[None], conv_block(d_small["conv_w"]), conv_block(m_small["conv_w"]),
                      conv_block(v_small["conv_w"]))

    order = ["ffn1_norm", "ffn1_w_gate", "ffn1_w_up", "ffn1_w_down", "mix_norm", "w_in", "conv_w",
             "a_log", "dt_bias", "dn_norm", "sg_ln_g", "sg_ln_b", "sg_w", "sg_b", "w_out", "ffn2_norm",
             "ffn2_w_gate", "ffn2_w_up", "ffn2_w_down", "final_norm"]
    return (loss, grad_x, *[outs[k][0] for k in order], *[outs[k][1] for k in order],
            *[outs[k][2] for k in order], *[outs[k][3] for k in order])
```

```python
import functools

import jax
import jax.numpy as jnp
from jax import lax
from jax.experimental import pallas as pl
from jax.experimental.pallas import tpu as pltpu

F32 = jnp.float32
BF16 = jnp.bfloat16
EPS = 1e-6

D_MODEL = 1024
N_SHARD = 4
HEAD_DIM = 128
N_HEADS = 4
DN_CHUNK = 64
SG_CHUNK = 128
SG_GROUPS = 8
SG_GROUP_DIM = 64
HALF_W = 512
PROJ_W = 3200
IN_COLS = 3080
GATE_COL_BLOCK = 24
QK_SCALE = HEAD_DIM ** -0.5
LANES = 128

ADAM_LR = 0.001
ADAM_B1 = 0.9
ADAM_B2 = 0.999
ADAM_EPS = 1e-08
ADAM_WD = 0.01
ADAM_STEP = 10

VMEM_LIMIT = 56 * 1024 * 1024
ROW_TILE = 512

NN = ((1,), (0,))
NT = ((1,), (1,))
TN = ((0,), (0,))
MESH = pl.DeviceIdType.MESH


def _dot(a, b, dims):
    return lax.dot_general(a, b, (dims, ((), ())), preferred_element_type=F32)


def _bdot(a, b, dims):
    return _dot(a.astype(BF16), b.astype(BF16), dims)


def _split(a):
    hi = a.astype(BF16)
    lo = (a - hi.astype(F32)).astype(BF16)
    return hi, lo


def _dot3(a, b, dims=NN):
    return _dot(a[0], b[0], dims) + (_dot(a[0], b[1], dims) + _dot(a[1], b[0], dims))


def _dot_exact_lhs(a, b):
    ab = a.astype(BF16)
    b1 = b.astype(BF16)
    r1 = b - b1.astype(F32)
    b2 = r1.astype(BF16)
    b3 = (r1 - b2.astype(F32)).astype(BF16)
    return _dot(ab, b1, NN) + (_dot(ab, b2, NN) + _dot(ab, b3, NN))


def _call(body, *, name, out_shape, in_specs, out_specs, grid=(), scratch=(), comm=None, **kw):
    params = dict(vmem_limit_bytes=VMEM_LIMIT)
    if grid:
        params["dimension_semantics"] = ("arbitrary",) * len(grid)
    if comm is None:
        return pl.pallas_call(
            body, name=name, grid=grid, in_specs=in_specs, out_specs=out_specs,
            out_shape=out_shape, scratch_shapes=list(scratch),
            compiler_params=pltpu.CompilerParams(**params), **kw)

    n_in, n_out, n_sc = len(in_specs), len(out_specs), len(scratch)
    c_in, c_out = len(comm.inputs), len(comm.out_shape)
    steps = 1
    for g in grid:
        steps *= g

    def hosted(*refs):
        ins, cins = refs[:n_in], refs[n_in:n_in + c_in]
        o0 = n_in + c_in
        outs, couts = refs[o0:o0 + n_out], refs[o0 + n_out:o0 + n_out + c_out]
        s0 = o0 + n_out + c_out
        sc, csems = refs[s0:s0 + n_sc], refs[s0 + n_sc:]
        lin = 0
        for axis, g in enumerate(grid):
            lin = lin * g + pl.program_id(axis)

        def at(step, fn):
            @pl.when(lin == step % steps)
            def _():
                fn(cins, couts, csems)

        for step, fn in comm.phases:
            if step >= 0:
                at(step, fn)
        body(*ins, *outs, *sc)
        for step, fn in comm.phases:
            if step < 0:
                at(step, fn)

    aliases = dict(kw.pop("input_output_aliases", {}))
    for k, m in comm.aliases.items():
        aliases[n_in + k] = n_out + m
    call = pl.pallas_call(
        hosted, name=name, grid=grid, in_specs=list(in_specs) + [_ANY] * c_in,
        out_specs=list(out_specs) + [_ANY] * c_out, out_shape=list(out_shape) + comm.out_shape,
        scratch_shapes=list(scratch) + comm.sems, input_output_aliases=aliases,
        compiler_params=pltpu.CompilerParams(**params), **kw)

    def run(*args):
        res = call(*args, *comm.inputs)
        return res[:n_out], res[n_out:]

    return run


def _sds(shape, dtype):
    return jax.ShapeDtypeStruct(tuple(shape), dtype)


def _resident(shape):
    zeros = (0,) * len(shape)
    return pl.BlockSpec(tuple(shape), lambda *_: zeros, pipeline_mode=pl.Buffered(1))


def _sigmoid(x):
    return jax.nn.sigmoid(x)


def _softplus(x):
    return jnp.maximum(x, 0.0) + jnp.log(1.0 + jnp.exp(-jnp.abs(x)))


_GELU_C = 0.7978845608028654
_GELU_A = 0.044715


def _gelu(x):
    t = jnp.tanh(_GELU_C * (x + _GELU_A * x * x * x))
    return 0.5 * x * (1.0 + t)


def _gelu_grad(x):
    t = jnp.tanh(_GELU_C * (x + _GELU_A * x * x * x))
    return 0.5 * (1.0 + t) + 0.5 * x * (1.0 - t * t) * _GELU_C * (1.0 + 3.0 * _GELU_A * x * x)


def _silu_grad(x):
    s = _sigmoid(x)
    return s * (1.0 + x * (1.0 - s))


def _rms_scale(xv):
    return lax.rsqrt(jnp.mean(xv * xv, axis=-1, keepdims=True) + EPS)


def _rms_bwd(dh, xv, g):
    r = _rms_scale(xv)
    xn = xv * r
    dg = jnp.sum(dh * xn, axis=0, keepdims=True)
    dxn = dh * g
    dx = r * (dxn - xn * jnp.mean(dxn * xn, axis=-1, keepdims=True))
    return dx, dg


def _iota2(shape, dim):
    return lax.broadcasted_iota(jnp.int32, shape, dim)


def _col_to_row(col):
    n = col.shape[0]
    eye = _iota2((n, n), 0) == _iota2((n, n), 1)
    return jnp.sum(jnp.where(eye, col, 0.0), axis=0, keepdims=True)


def _row_to_col(row):
    n = row.shape[1]
    eye = _iota2((n, n), 0) == _iota2((n, n), 1)
    return jnp.sum(jnp.where(eye, row, 0.0), axis=1, keepdims=True)


MXU_DIM = 256


def _hidden_chunks(f, step=3 * MXU_DIM):
    return [(c0, min(c0 + step, f)) for c0 in range(0, f, step)]

def ffn_fwd(x, gnorm, wg, wu, wd, name, comm=None):
    n, d = x.shape
    f = wg.shape[0]
    tm = min(ROW_TILE, n)

    def body(x_ref, g_ref, wg_ref, wu_ref, wd_ref, xo_ref, h_ref, gate_ref, up_ref, act_ref, acc_ref):
        xv = x_ref[...]
        h = (xv * _rms_scale(xv) * g_ref[...]).astype(BF16)
        h_ref[...] = h
        for c0, c1 in _hidden_chunks(f):
            gate = _dot(h, wg_ref[c0:c1, :], NT)
            up = _dot(h, wu_ref[c0:c1, :], NT)
            act = (gate * _sigmoid(gate) * up).astype(BF16)
            gate_ref[:, c0:c1] = gate.astype(BF16)
            up_ref[:, c0:c1] = up.astype(BF16)
            act_ref[:, c0:c1] = act
            part = _dot(act, wd_ref[c0:c1, :], NN)
            if c0 == 0:
                acc_ref[...] = part
            else:
                acc_ref[...] += part
        xo_ref[...] = xv + 0.5 * acc_ref[...]

    row = pl.BlockSpec((tm, d), lambda i: (i, 0))
    wide = pl.BlockSpec((tm, f), lambda i: (i, 0))
    return _call(
        body, name=name, grid=(n // tm,),
        in_specs=[row, pl.BlockSpec((1, d), lambda i: (0, 0))] + [_resident((f, d))] * 3,
        out_specs=[row, row, wide, wide, wide],
        out_shape=[_sds((n, d), F32), _sds((n, d), BF16)] + [_sds((n, f), BF16)] * 3,
        scratch=[pltpu.VMEM((tm, d), F32)], comm=comm,
    )(x, gnorm, wg, wu, wd)


def ffn_bwd_act(dy, dyh, x, gnorm, gate, up, wg, wu, wd, name):
    n, d = x.shape
    f = wg.shape[0]
    fh = f // 2
    tm = min(ROW_TILE, n)

    def half_grads(dyh_ref, gate_ref, up_ref, wg_ref, wu_ref, wd_ref, dgate_ref, dup_ref, dh_ref, fresh):
        dyh_v = dyh_ref[...]
        for c0, c1 in _hidden_chunks(fh):
            dact = _dot(dyh_v, wd_ref[c0:c1, :], NT)
            gt = gate_ref[:, c0:c1].astype(F32)
            u = up_ref[:, c0:c1].astype(F32)
            s = _sigmoid(gt)
            dup = (dact * (gt * s)).astype(BF16)
            dgate = (dact * u * (s * (1.0 + gt * (1.0 - s)))).astype(BF16)
            dup_ref[:, c0:c1] = dup
            dgate_ref[:, c0:c1] = dgate
            part = _dot(dgate, wg_ref[c0:c1, :], NN) + _dot(dup, wu_ref[c0:c1, :], NN)
            if fresh and c0 == 0:
                dh_ref[...] = part
            else:
                dh_ref[...] += part

    def first_body(dyh_ref, gate_ref, up_ref, wg_ref, wu_ref, wd_ref, dgate_ref, dup_ref, dh_ref):
        half_grads(dyh_ref, gate_ref, up_ref, wg_ref, wu_ref, wd_ref, dgate_ref, dup_ref, dh_ref, True)

    def second_body(dy_ref, dyh_ref, x_ref, g_ref, gate_ref, up_ref, wg_ref, wu_ref, wd_ref, dh_ref,
                    dgate_in, dup_in, dx_ref, dgate_ref, dup_ref, dg_ref, acc_ref):
        @pl.when(pl.program_id(0) == 0)
        def _():
            dg_ref[...] = jnp.zeros_like(dg_ref)

        acc_ref[...] = dh_ref[...]
        half_grads(dyh_ref, gate_ref, up_ref, wg_ref, wu_ref, wd_ref, dgate_ref, dup_ref, acc_ref, False)
        dxn, dg = _rms_bwd(acc_ref[...], x_ref[...], g_ref[...])
        dx_ref[...] = dy_ref[...] + dxn
        dg_ref[...] += dg

    row = pl.BlockSpec((tm, d), lambda i: (i, 0))
    vec = pl.BlockSpec((1, d), lambda i: (0, 0))

    def half(j):
        return pl.BlockSpec((tm, fh), lambda i: (i, j))

    def w_half(j):
        return pl.BlockSpec((fh, d), lambda i: (j, 0), pipeline_mode=pl.Buffered(1))

    wide = _sds((n, f), BF16)
    dgate, dup, dh = _call(
        first_body, name=name + "_lo", grid=(n // tm,),
        in_specs=[row, half(0), half(0), w_half(0), w_half(0), w_half(0)],
        out_specs=[half(0), half(0), row],
        out_shape=[wide, wide, _sds((n, d), F32)],
    )(dyh, gate, up, wg, wu, wd)
    dx, dgate, dup, dg = _call(
        second_body, name=name + "_hi", grid=(n // tm,),
        in_specs=[row, row, row, vec, half(1), half(1), w_half(1), w_half(1), w_half(1), row, _ANY, _ANY],
        out_specs=[row, half(1), half(1), vec],
        out_shape=[_sds((n, d), F32), wide, wide, _sds((1, d), F32)],
        scratch=[pltpu.VMEM((tm, d), F32)], input_output_aliases={10: 1, 11: 2},
    )(dy, dyh, x, gnorm, gate, up, wg, wu, wd, dh, dgate, dup)
    return dx, dgate, dup, dg


def ffn_bwd_w(wide, rows, pairs, name, comm=None):
    n, d = rows[0].shape
    f = wide[0].shape[1]
    fh = f // 2
    tk = min(ROW_TILE, n)
    n_w, n_r = len(wide), len(rows)

    def body(*refs):
        wide_refs, row_refs, outs = refs[:n_w], refs[n_w:n_w + n_r], refs[n_w + n_r:]

        @pl.when(pl.program_id(1) == 0)
        def _():
            for o_ref in outs:
                o_ref[...] = jnp.zeros_like(o_ref)

        row_vals = [r[...] for r in row_refs]
        for c0, c1 in _hidden_chunks(fh, 2 * MXU_DIM):
            for (i, k), o_ref in zip(pairs, outs):
                o_ref[c0:c1, :] += _dot(wide_refs[i][:, c0:c1], row_vals[k], TN)

    row = pl.BlockSpec((tk, d), lambda j, k: (k, 0))
    blk = pl.BlockSpec((tk, fh), lambda j, k: (k, j))
    return _call(
        body, name=name, grid=(2, n // tk),
        in_specs=[blk] * n_w + [row] * n_r,
        out_specs=[pl.BlockSpec((fh, d), lambda j, k: (j, 0))] * len(pairs),
        out_shape=[_sds((f, d), F32)] * len(pairs), comm=comm,
    )(*wide, *rows)


def final_loss(x, gnorm, target, name):
    n, d = x.shape
    tm = min(ROW_TILE, n)

    def body(x_ref, g_ref, t_ref, dx_ref, dg_ref, loss_ref, dxh_ref):
        @pl.when(pl.program_id(0) == 0)
        def _():
            dg_ref[...] = jnp.zeros_like(dg_ref)
            loss_ref[...] = jnp.zeros_like(loss_ref)

        xv = x_ref[...]
        y = xv * _rms_scale(xv) * g_ref[...]
        err = y - t_ref[...]
        part = 0.5 * jnp.sum(jnp.mean(err * err, axis=-1, keepdims=True), axis=0, keepdims=True)
        loss_ref[...] += jnp.broadcast_to(part, loss_ref.shape)
        dx, dg = _rms_bwd(err * (1.0 / d), xv, g_ref[...])
        dx_ref[...] = dx
        dxh_ref[...] = (0.5 * dx).astype(BF16)
        dg_ref[...] += dg

    row = pl.BlockSpec((tm, d), lambda i: (i, 0))
    vec = pl.BlockSpec((1, d), lambda i: (0, 0))
    return _call(
        body, name=name, grid=(n // tm,),
        in_specs=[row, vec, row],
        out_specs=[row, vec, pl.BlockSpec((1, LANES), lambda i: (0, 0)), row],
        out_shape=[_sds((n, d), F32), _sds((1, d), F32), _sds((1, LANES), F32), _sds((n, d), BF16)],
    )(x, gnorm, target)


def in_proj_fwd(x, gnorm, w, name):
    n, d = x.shape
    cols = w.shape[1]
    tm = min(ROW_TILE, n)

    def body(x_ref, g_ref, w_ref, p_ref, h_ref):
        xv = x_ref[...]
        h = (xv * _rms_scale(xv) * g_ref[...]).astype(BF16)
        h_ref[...] = h
        for c0, c1 in _hidden_chunks(cols):
            p_ref[:, c0:c1] = _dot(h, w_ref[:, c0:c1], NN)

    return _call(
        body, name=name, grid=(n // tm,),
        in_specs=[pl.BlockSpec((tm, d), lambda i: (i, 0)),
                  pl.BlockSpec((1, d), lambda i: (0, 0)), _resident((d, cols))],
        out_specs=[pl.BlockSpec((tm, cols), lambda i: (i, 0)),
                   pl.BlockSpec((tm, d), lambda i: (i, 0))],
        out_shape=[_sds((n, cols), F32), _sds((n, d), BF16)],
    )(x, gnorm, w)


def in_proj_bwd_x(dproj, w, x, gnorm, dres, name, comm=None):
    n, d = x.shape
    cols = w.shape[1]
    tm = min(ROW_TILE, n)

    def body(dp_ref, w_ref, x_ref, g_ref, dr_ref, dx_ref, dg_ref, dxh_ref):
        @pl.when(pl.program_id(0) == 0)
        def _():
            dg_ref[...] = jnp.zeros_like(dg_ref)

        dh = _dot(dp_ref[...], w_ref[...], NT)
        dxn, dg = _rms_bwd(dh, x_ref[...], g_ref[...])
        dx = dr_ref[...] + dxn
        dx_ref[...] = dx
        dxh_ref[...] = (0.5 * dx).astype(BF16)
        dg_ref[...] += dg

    row = pl.BlockSpec((tm, d), lambda i: (i, 0))
    vec = pl.BlockSpec((1, d), lambda i: (0, 0))
    return _call(
        body, name=name, grid=(n // tm,),
        in_specs=[pl.BlockSpec((tm, cols), lambda i: (i, 0)),
                  _resident((d, cols)), row, vec, row],
        out_specs=[row, vec, row],
        out_shape=[_sds((n, d), F32), _sds((1, d), F32), _sds((n, d), BF16)], comm=comm,
    )(dproj, w, x, gnorm, dres)


def matmul_tn(a_list, b, tn, name):
    n, cb = b.shape
    widths = [a.shape[1] for a in a_list]
    tk = min(ROW_TILE, n)

    def body(*refs):
        a_refs, b_ref, o_ref = refs[:-2], refs[-2], refs[-1]

        @pl.when(pl.program_id(0) == 0)
        def _():
            o_ref[...] = jnp.zeros_like(o_ref)

        r0 = 0
        for a_ref, ka in zip(a_refs, widths):
            av = a_ref[...]
            for c0, c1 in _hidden_chunks(cb, tn):
                o_ref[r0:r0 + ka, c0:c1] += _dot(av, b_ref[:, c0:c1], TN)
            r0 += ka

    return _call(
        body, name=name, grid=(n // tk,),
        in_specs=[pl.BlockSpec((tk, ka), lambda k: (k, 0)) for ka in widths]
        + [pl.BlockSpec((tk, cb), lambda k: (k, 0))],
        out_specs=pl.BlockSpec((sum(widths), cb), lambda k: (0, 0)),
        out_shape=_sds((sum(widths), cb), F32),
    )(*a_list, b)


def out_proj_fwd(x, sg_out, dn_out, w, name):
    n, d = x.shape
    tm = min(ROW_TILE, n)

    def body(x_ref, a_ref, b_ref, w_ref, o_ref):
        o_ref[...] = (x_ref[...] + _dot(a_ref[...], w_ref[0:HALF_W, :], NN)
                      + _dot(b_ref[...], w_ref[HALF_W:2 * HALF_W, :], NN))

    row = pl.BlockSpec((tm, d), lambda i: (i, 0))
    half = pl.BlockSpec((tm, HALF_W), lambda i: (i, 0))
    return _call(
        body, name=name, grid=(n // tm,),
        in_specs=[row, half, half, pl.BlockSpec((2 * HALF_W, d), lambda i: (0, 0))],
        out_specs=row, out_shape=_sds((n, d), F32),
    )(x, sg_out, dn_out, w)


def out_proj_bwd_x(dy, w, name, comm=None):
    n, d = dy.shape
    tm = min(ROW_TILE, n)

    def body(dy_ref, w_ref, dsg_ref, ddn_ref, dyb_ref):
        dyb = dy_ref[...].astype(BF16)
        dyb_ref[...] = dyb
        dsg_ref[...] = _dot(dyb, w_ref[0:HALF_W, :], NT)
        ddn_ref[...] = _dot(dyb, w_ref[HALF_W:2 * HALF_W, :], NT)

    row = pl.BlockSpec((tm, d), lambda i: (i, 0))
    half = pl.BlockSpec((tm, HALF_W), lambda i: (i, 0))
    return _call(
        body, name=name, grid=(n // tm,),
        in_specs=[row, pl.BlockSpec((2 * HALF_W, d), lambda i: (0, 0))],
        out_specs=[half, half, row],
        out_shape=[_sds((n, HALF_W), F32), _sds((n, HALF_W), F32), _sds((n, d), BF16)], comm=comm,
    )(dy, w)


SG_PAIRS = SG_GROUPS // 2


def _sg_low_half():
    return _iota2((SG_CHUNK, LANES), 1) < SG_GROUP_DIM


def _sg_pair_cols(p):
    return slice(p * LANES, (p + 1) * LANES)


def _sg_causal():
    return _iota2((SG_CHUNK, SG_CHUNK), 0) >= _iota2((SG_CHUNK, SG_CHUNK), 1)


def _sg_forward_chunk(pu, pv, ln_g, ln_b, wc, bias, low):
    u = _gelu(pu)
    v = _gelu(pv)
    mu = jnp.mean(v, axis=-1, keepdims=True)
    vc = v - mu
    rs = lax.rsqrt(jnp.mean(vc * vc, axis=-1, keepdims=True) + EPS)
    xhat = vc * rs
    vn = (xhat * ln_g + ln_b).astype(BF16)
    parts = []
    for p in range(SG_PAIRS):
        vn_p = vn[:, _sg_pair_cols(p)]
        parts.append(jnp.where(low, _dot(wc[2 * p], vn_p, NN), _dot(wc[2 * p + 1], vn_p, NN)))
    vs = bias + jnp.concatenate(parts, axis=1)
    return u, xhat, rs, vn, vs


def sg_fwd(proj, ln_g, ln_b, w_s, bias_tile, name):
    n = proj.shape[0]
    tm = min(ROW_TILE, n)

    def body(pu_ref, pv_ref, g_ref, b_ref, w_ref, bias_ref, o_ref):
        causal = _sg_causal()
        wc = [jnp.where(causal, w_ref[g], 0.0).astype(BF16) for g in range(SG_GROUPS)]
        masks = _sg_low_half()
        for ci in range(tm // SG_CHUNK):
            rows = slice(ci * SG_CHUNK, (ci + 1) * SG_CHUNK)
            u, _, _, _, vs = _sg_forward_chunk(pu_ref[rows, :], pv_ref[rows, :], g_ref[...],
                                               b_ref[...], wc, bias_ref[...], masks)
            o_ref[rows, :] = (u * vs).astype(BF16)

    vec = pl.BlockSpec((1, HALF_W), lambda i: (0, 0))
    return _call(
        body, name=name, grid=(n // tm,),
        in_specs=[pl.BlockSpec((tm, HALF_W), lambda i: (i, 0)),
                  pl.BlockSpec((tm, HALF_W), lambda i: (i, 1)), vec, vec,
                  pl.BlockSpec((SG_GROUPS, SG_CHUNK, SG_CHUNK), lambda i: (0, 0, 0)),
                  pl.BlockSpec((SG_CHUNK, HALF_W), lambda i: (0, 0))],
        out_specs=pl.BlockSpec((tm, HALF_W), lambda i: (i, 0)),
        out_shape=_sds((n, HALF_W), BF16),
    )(proj, proj, ln_g, ln_b, w_s, bias_tile)


def sg_bwd(dsg, proj, ln_g, ln_b, w_s, bias_tile, name):
    n = proj.shape[0]
    tm = min(ROW_TILE, n)

    def body(d_ref, pu_ref, pv_ref, g_ref, b_ref, w_ref, bias_ref,
             dp_ref, dw_ref, db_ref, dlg_ref, dlb_ref):
        @pl.when(pl.program_id(0) == 0)
        def _():
            dw_ref[...] = jnp.zeros_like(dw_ref)
            db_ref[...] = jnp.zeros_like(db_ref)
            dlg_ref[...] = jnp.zeros_like(dlg_ref)
            dlb_ref[...] = jnp.zeros_like(dlb_ref)

        causal = _sg_causal()
        wc = [jnp.where(causal, w_ref[g], 0.0).astype(BF16) for g in range(SG_GROUPS)]
        masks = _sg_low_half()
        ln_g_v = g_ref[...]
        for ci in range(tm // SG_CHUNK):
            rows = slice(ci * SG_CHUNK, (ci + 1) * SG_CHUNK)
            pu = pu_ref[rows, :]
            pv = pv_ref[rows, :]
            u, xhat, rs, vn, vs = _sg_forward_chunk(pu, pv, ln_g_v, b_ref[...], wc,
                                                    bias_ref[...], masks)
            dout = d_ref[rows, :]
            dp_ref[rows, 0:HALF_W] = (dout * vs * _gelu_grad(pu)).astype(BF16)
            dvs = dout * u
            dvs_b = dvs.astype(BF16)
            db_ref[...] += dvs
            dvn_parts = []
            for p in range(SG_PAIRS):
                dvs_p = dvs_b[:, _sg_pair_cols(p)]
                vn_p = vn[:, _sg_pair_cols(p)]
                dvn_parts.append(jnp.where(masks, _dot(wc[2 * p], dvs_p, TN), _dot(wc[2 * p + 1], dvs_p, TN)))
                zero = jnp.zeros_like(dvs_p)
                dw_ref[2 * p] += jnp.where(causal, _dot(jnp.where(masks, dvs_p, zero), vn_p, NT), 0.0)
                dw_ref[2 * p + 1] += jnp.where(causal, _dot(jnp.where(masks, zero, dvs_p), vn_p, NT), 0.0)
            dvn = jnp.concatenate(dvn_parts, axis=1)
            dlg_ref[...] += jnp.sum(dvn * xhat, axis=0, keepdims=True)
            dlb_ref[...] += jnp.sum(dvn, axis=0, keepdims=True)
            dxh = dvn * ln_g_v
            dv = rs * (dxh - jnp.mean(dxh, axis=-1, keepdims=True)
                       - xhat * jnp.mean(dxh * xhat, axis=-1, keepdims=True))
            dp_ref[rows, HALF_W:2 * HALF_W] = (dv * _gelu_grad(pv)).astype(BF16)

    vec = pl.BlockSpec((1, HALF_W), lambda i: (0, 0))
    wspec = pl.BlockSpec((SG_GROUPS, SG_CHUNK, SG_CHUNK), lambda i: (0, 0, 0))
    tile = pl.BlockSpec((SG_CHUNK, HALF_W), lambda i: (0, 0))
    return _call(
        body, name=name, grid=(n // tm,),
        in_specs=[pl.BlockSpec((tm, HALF_W), lambda i: (i, 0)),
                  pl.BlockSpec((tm, HALF_W), lambda i: (i, 0)),
                  pl.BlockSpec((tm, HALF_W), lambda i: (i, 1)), vec, vec, wspec, tile],
        out_specs=[pl.BlockSpec((tm, 2 * HALF_W), lambda i: (i, 0)), wspec, tile, vec, vec],
        out_shape=[_sds((n, PROJ_W), BF16), _sds((SG_GROUPS, SG_CHUNK, SG_CHUNK), F32),
                   _sds((SG_CHUNK, HALF_W), F32), _sds((1, HALF_W), F32), _sds((1, HALF_W), F32)],
    )(dsg, proj, proj, ln_g, ln_b, w_s, bias_tile)


CONV_K = 4
CONV_BLOCK = 256


def _shift_down(x, s, row):
    if s == 0:
        return x
    return jnp.where(row >= s, pltpu.roll(x, s, 0), 0.0)


def _shift_up(x, s, row):
    if s == 0:
        return x
    t_len = x.shape[0]
    return jnp.where(row < t_len - s, pltpu.roll(x, t_len - s, 0), 0.0)


def _conv_taps(x, row):
    return [_shift_down(x, CONV_K - 1 - j, row) for j in range(CONV_K)]


def _conv(taps, w):
    y = taps[0] * w[0:1, :]
    for j in range(1, CONV_K):
        y = y + taps[j] * w[j:j + 1, :]
    return y


def dn_conv_fwd(proj3, conv_w, name):
    b, t, _ = proj3.shape
    nblk = 3 * HALF_W // CONV_BLOCK
    first = 2 * HALF_W // CONV_BLOCK
    n_norm = 2 * HALF_W // CONV_BLOCK

    def body(x_ref, w_ref, o_ref):
        s = pl.program_id(1)
        x = x_ref[0]
        y = _conv(_conv_taps(x, _iota2(x.shape, 0)), w_ref[...])
        y = y * _sigmoid(y)

        @pl.when(s < n_norm)
        def _():
            for h in range(CONV_BLOCK // HEAD_DIM):
                cs = slice(h * HEAD_DIM, (h + 1) * HEAD_DIM)
                yh = y[:, cs]
                o_ref[0, :, cs] = yh * lax.rsqrt(jnp.sum(yh * yh, axis=-1, keepdims=True) + EPS)

        @pl.when(s >= n_norm)
        def _():
            o_ref[0] = y

    return _call(
        body, name=name, grid=(b, nblk),
        in_specs=[pl.BlockSpec((1, t, CONV_BLOCK), lambda i, s: (i, 0, first + s)),
                  pl.BlockSpec((CONV_K, CONV_BLOCK), lambda i, s: (0, s))],
        out_specs=pl.BlockSpec((1, t, CONV_BLOCK), lambda i, s: (i, 0, s)),
        out_shape=_sds((b, t, 3 * HALF_W), F32),
    )(proj3, conv_w)


def dn_conv_bwd(dqkv, proj3, conv_w, dproj3, name, comm=None):
    b, t, _ = proj3.shape
    nblk = 3 * HALF_W // CONV_BLOCK
    first = 2 * HALF_W // CONV_BLOCK
    n_norm = 2 * HALF_W // CONV_BLOCK

    def body(d_ref, x_ref, w_ref, dproj_in, dx_ref, dw_ref, ds_ref):
        s = pl.program_id(0)

        @pl.when(pl.program_id(1) == 0)
        def _():
            dw_ref[...] = jnp.zeros_like(dw_ref)

        x = x_ref[0]
        w = w_ref[...]
        row = _iota2(x.shape, 0)
        taps = _conv_taps(x, row)
        c = _conv(taps, w)
        sg = _sigmoid(c)
        y = c * sg

        @pl.when(s < n_norm)
        def _():
            for h in range(CONV_BLOCK // HEAD_DIM):
                cs = slice(h * HEAD_DIM, (h + 1) * HEAD_DIM)
                yh = y[:, cs]
                r = lax.rsqrt(jnp.sum(yh * yh, axis=-1, keepdims=True) + EPS)
                nh = yh * r
                dn = d_ref[0, :, cs]
                ds_ref[:, cs] = r * (dn - nh * jnp.sum(dn * nh, axis=-1, keepdims=True))

        @pl.when(s >= n_norm)
        def _():
            ds_ref[...] = d_ref[0]

        dc = ds_ref[...] * (sg * (1.0 + c * (1.0 - sg)))
        dx = _shift_up(dc, CONV_K - 1, row) * w[0:1, :]
        for j in range(1, CONV_K):
            dx = dx + _shift_up(dc, CONV_K - 1 - j, row) * w[j:j + 1, :]
        dx_ref[0] = dx.astype(BF16)
        for j in range(CONV_K):
            dw_ref[j:j + 1, :] += jnp.sum(dc * taps[j], axis=0, keepdims=True)

    return _call(
        body, name=name, grid=(nblk, b),
        in_specs=[pl.BlockSpec((1, t, CONV_BLOCK), lambda s, i: (i, 0, s)),
                  pl.BlockSpec((1, t, CONV_BLOCK), lambda s, i: (i, 0, first + s)),
                  pl.BlockSpec((CONV_K, CONV_BLOCK), lambda s, i: (0, s)), _ANY],
        out_specs=[pl.BlockSpec((1, t, CONV_BLOCK), lambda s, i: (i, 0, first + s)),
                   pl.BlockSpec((CONV_K, CONV_BLOCK), lambda s, i: (0, s))],
        out_shape=[_sds(dproj3.shape, BF16), _sds((CONV_K, 3 * HALF_W), F32)],
        scratch=[pltpu.VMEM((t, CONV_BLOCK), F32)],
        input_output_aliases={3: 0}, comm=comm,
    )(dqkv, proj3, conv_w, dproj3)


def _chunk_masks():
    ii = _iota2((DN_CHUNK, DN_CHUNK), 0)
    jj = _iota2((DN_CHUNK, DN_CHUNK), 1)
    return ii >= jj, ii > jj, ii == jj


LOCKSTEP_CHUNKS = 4


def _inv_unit_lower_many(l_mats, eye):
    eye_f = jnp.where(eye, 1.0, 0.0)
    ps = [-l for l in l_mats]
    ts = [eye_f + p for p in ps]
    pss = [_split(p) for p in ps]
    size = 2
    while size < DN_CHUNK:
        ps = [_dot3(s, s) for s in pss]
        pss = [_split(p) for p in ps]
        ts = [t + _dot3(_split(t), s) for t, s in zip(ts, pss)]
        size *= 2
    return ts


def _gates(pba, ea_row, dtb_row):
    beta = _sigmoid(pba)
    g = -ea_row * _softplus(pba + dtb_row)
    return beta, g


def _chunk_decay(gcol):
    incl, strict, eye = _chunk_masks()
    grow = jnp.sum(jnp.where(eye, gcol, 0.0), axis=0, keepdims=True)
    decay = jnp.where(incl, jnp.exp(jnp.where(incl, gcol - grow, 0.0)), 0.0)
    return decay, incl, strict, eye


def dn_chunk_fwd(qkv, proj3, alog_row, dtb_row, name):
    b, t, _ = qkv.shape
    rblk = min(256, t)
    n_in = rblk // DN_CHUNK

    def body(q_ref, k_ref, v_ref, pba_ref, al_ref, dtb_ref,
             u_ref, w_ref, qd_ref, kd_ref, qk_ref, ti_ref, gc_ref):
        ea = jnp.exp(al_ref[...])
        tri = jnp.where(_chunk_masks()[0], 1.0, 0.0)

        _, strict, eye = _chunk_masks()

        def chunk_group(cg, carry):
            items = []
            for sub in range(LOCKSTEP_CHUNKS):
                rows = pl.ds(pl.multiple_of((cg * LOCKSTEP_CHUNKS + sub) * DN_CHUNK, DN_CHUNK), DN_CHUNK)
                beta_all, g_all = _gates(pba_ref[0, rows, :], ea, dtb_ref[...])
                gc = _dot_exact_lhs(tri, g_all)
                gc_ref[0, rows, :] = gc
                for h in range(N_HEADS):
                    items.append((rows, h, beta_all[:, h:h + 1], gc[:, N_HEADS + h:N_HEADS + h + 1]))
            ks, kbs, decays, egs = [], [], [], []
            for rows, h, beta, gcol in items:
                cs = slice(h * HEAD_DIM, (h + 1) * HEAD_DIM)
                k = k_ref[0, rows, cs]
                ks.append(k)
                kbs.append(k * beta)
                decays.append(_chunk_decay(gcol)[0])
                egs.append(jnp.exp(gcol))
            ms = [_bdot(kb, k, NT) for kb, k in zip(kbs, ks)]
            tinvs = _inv_unit_lower_many([jnp.where(strict, m * dc, 0.0) for m, dc in zip(ms, decays)], eye)
            tsps = [_split(t) for t in tinvs]
            for (rows, h, beta, gcol), tsp, tinv in zip(items, tsps, tinvs):
                cs = slice(h * HEAD_DIM, (h + 1) * HEAD_DIM)
                u_ref[0, rows, cs] = _dot3(tsp, _split(v_ref[0, rows, cs] * beta))
                ti_ref[0, h, rows, :] = tinv
            for (rows, h, beta, gcol), tsp, kb, eg in zip(items, tsps, kbs, egs):
                cs = slice(h * HEAD_DIM, (h + 1) * HEAD_DIM)
                w_ref[0, rows, cs] = _dot3(tsp, _split(kb * eg))
            for (rows, h, beta, gcol), k, dc, eg in zip(items, ks, decays, egs):
                cs = slice(h * HEAD_DIM, (h + 1) * HEAD_DIM)
                q = q_ref[0, rows, cs] * QK_SCALE
                qk_ref[0, h, rows, :] = _bdot(q, k, NT) * dc
                qd_ref[0, rows, cs] = q * eg
                kd_ref[0, rows, cs] = k * jnp.exp(gcol[DN_CHUNK - 1:DN_CHUNK, :] - gcol)
            return carry

        lax.fori_loop(0, n_in // LOCKSTEP_CHUNKS, chunk_group, 0)

    def seg(cblk):
        return pl.BlockSpec((1, rblk, HALF_W), lambda i, r: (i, r, cblk))

    vec = pl.BlockSpec((1, LANES), lambda i, r: (0, 0))
    wide = pl.BlockSpec((1, rblk, HALF_W), lambda i, r: (i, r, 0))
    sq = pl.BlockSpec((1, N_HEADS, rblk, DN_CHUNK), lambda i, r: (i, 0, r, 0))
    return _call(
        body, name=name, grid=(b, t // rblk),
        in_specs=[seg(0), seg(1), seg(2),
                  pl.BlockSpec((1, rblk, LANES), lambda i, r: (i, r, GATE_COL_BLOCK)), vec, vec],
        out_specs=[wide, wide, wide, wide, sq, sq,
                   pl.BlockSpec((1, rblk, LANES), lambda i, r: (i, r, 0))],
        out_shape=[_sds((b, t, HALF_W), F32)] * 4
        + [_sds((b, N_HEADS, t, DN_CHUNK), F32)] * 2 + [_sds((b, t, LANES), F32)],
    )(qkv, qkv, qkv, proj3, alog_row, dtb_row)


def dn_scan_fwd(u, w, qd, kd, qk, gc, name):
    b, t, _ = u.shape
    nc = t // DN_CHUNK
    bh = b * N_HEADS

    def body(u_ref, w_ref, qd_ref, kd_ref, qk_ref, gc_ref, o_ref, sin_ref, s_ref):
        @pl.when(pl.program_id(0) == 0)
        def _():
            s_ref[...] = jnp.zeros_like(s_ref)

        items = [(bi, h, slice(h * HEAD_DIM, (h + 1) * HEAD_DIM)) for bi in range(b) for h in range(N_HEADS)]
        sbs = []
        for bi, h, cs in items:
            s = s_ref[bi * N_HEADS + h]
            sin_ref[0, bi * N_HEADS + h] = s
            sbs.append(s.astype(BF16))
        ws = [_bdot(w_ref[bi, :, cs], sb, NN) for (bi, h, cs), sb in zip(items, sbs)]
        qs = [_bdot(qd_ref[bi, :, cs], sb, NN) for (bi, h, cs), sb in zip(items, sbs)]
        vbs = [(u_ref[bi, :, cs] - wsi).astype(BF16) for (bi, h, cs), wsi in zip(items, ws)]
        for (bi, h, cs), qsi, vb in zip(items, qs, vbs):
            o_ref[bi, :, cs] = qsi + _bdot(qk_ref[bi, h], vb, NN)
        for (bi, h, cs), vb in zip(items, vbs):
            gl = jnp.exp(gc_ref[bi, DN_CHUNK - 1:DN_CHUNK, N_HEADS + h:N_HEADS + h + 1])
            idx = bi * N_HEADS + h
            s_ref[idx] = s_ref[idx] * gl + _bdot(kd_ref[bi, :, cs], vb, TN)

    wide = pl.BlockSpec((b, DN_CHUNK, HALF_W), lambda c: (0, c, 0))
    return _call(
        body, name=name, grid=(nc,),
        in_specs=[wide, wide, wide, wide,
                  pl.BlockSpec((b, N_HEADS, DN_CHUNK, DN_CHUNK), lambda c: (0, 0, c, 0)),
                  pl.BlockSpec((b, DN_CHUNK, LANES), lambda c: (0, c, 0))],
        out_specs=[wide, pl.BlockSpec((1, bh, HEAD_DIM, HEAD_DIM), lambda c: (c, 0, 0, 0))],
        out_shape=[_sds((b, t, HALF_W), F32), _sds((nc, bh, HEAD_DIM, HEAD_DIM), F32)],
        scratch=[pltpu.VMEM((bh, HEAD_DIM, HEAD_DIM), F32)],
    )(u, w, qd, kd, qk, gc)


def dn_scan_bwd(do, u, w, qd, kd, qk, gc, s_in, name):
    b, t, _ = u.shape
    nc = t // DN_CHUNK
    bh = b * N_HEADS

    def body(do_ref, u_ref, w_ref, qd_ref, kd_ref, qk_ref, gc_ref, sin_ref,
             du_ref, dw_ref, dqd_ref, dkd_ref, dqk_ref, dgc_ref, ds_ref):
        @pl.when(pl.program_id(0) == 0)
        def _():
            ds_ref[...] = jnp.zeros_like(ds_ref)

        last_row = _iota2((DN_CHUNK, LANES), 0) == DN_CHUNK - 1
        lane = _iota2((DN_CHUNK, LANES), 1)
        items = [(bi, h, slice(h * HEAD_DIM, (h + 1) * HEAD_DIM)) for bi in range(b) for h in range(N_HEADS)]
        sbs = [sin_ref[0, bi * N_HEADS + h].astype(BF16) for bi, h, cs in items]
        wvs = [w_ref[bi, :, cs].astype(BF16) for bi, h, cs in items]
        dovs = [do_ref[bi, :, cs].astype(BF16) for bi, h, cs in items]
        dsbs = [ds_ref[bi * N_HEADS + h].astype(BF16) for bi, h, cs in items]
        vbs = [(u_ref[bi, :, cs] - _dot(wv, sb, NN)).astype(BF16)
               for (bi, h, cs), wv, sb in zip(items, wvs, sbs)]
        for (bi, h, cs), dov, sb in zip(items, dovs, sbs):
            dqd_ref[bi, :, cs] = _dot(dov, sb, NT)
        dvns = [_dot(kd_ref[bi, :, cs].astype(BF16), dsb, NN) + _dot(qk_ref[bi, h].astype(BF16), dov, TN)
                for (bi, h, cs), dsb, dov in zip(items, dsbs, dovs)]
        for (bi, h, cs), vb, dsb, dov in zip(items, vbs, dsbs, dovs):
            dkd_ref[bi, :, cs] = _dot(vb, dsb, NT)
            dqk_ref[bi, h] = _dot(dov, vb, NT)
        dgls = []
        for (bi, h, cs), dvn, sb, wv, dov in zip(items, dvns, sbs, wvs, dovs):
            idx = bi * N_HEADS + h
            du_ref[bi, :, cs] = dvn
            dvn_b = dvn.astype(BF16)
            dw_ref[bi, :, cs] = -_dot(dvn_b, sb, NT)
            gl = jnp.exp(gc_ref[bi, DN_CHUNK - 1:DN_CHUNK, N_HEADS + h:N_HEADS + h + 1])
            ds = ds_ref[idx]
            dgl = jnp.sum(jnp.sum(ds * sin_ref[0, idx], axis=1, keepdims=True), axis=0, keepdims=True)
            dgls.append(dgl * gl)
            ds_ref[idx] = (ds * gl + _dot(qd_ref[bi, :, cs].astype(BF16), dov, TN)
                           - _dot(wv, dvn_b, TN))
        for bi in range(b):
            dgc = jnp.zeros((DN_CHUNK, LANES), F32)
            for h in range(N_HEADS):
                dgc = dgc + jnp.where(jnp.logical_and(last_row, lane == N_HEADS + h),
                                      dgls[bi * N_HEADS + h], 0.0)
            dgc_ref[bi] = dgc

    def rev(c):
        return nc - 1 - c

    wide = pl.BlockSpec((b, DN_CHUNK, HALF_W), lambda c: (0, rev(c), 0))
    sq = pl.BlockSpec((b, N_HEADS, DN_CHUNK, DN_CHUNK), lambda c: (0, 0, rev(c), 0))
    gates = pl.BlockSpec((b, DN_CHUNK, LANES), lambda c: (0, rev(c), 0))
    return _call(
        body, name=name, grid=(nc,),
        in_specs=[wide, wide, wide, wide, wide, sq, gates,
                  pl.BlockSpec((1, bh, HEAD_DIM, HEAD_DIM), lambda c: (rev(c), 0, 0, 0))],
        out_specs=[wide, wide, wide, wide, sq, gates],
        out_shape=[_sds((b, t, HALF_W), F32)] * 4
        + [_sds((b, N_HEADS, t, DN_CHUNK), F32), _sds((b, t, LANES), F32)],
        scratch=[pltpu.VMEM((bh, HEAD_DIM, HEAD_DIM), F32)],
    )(do, u, w, qd, kd, qk, gc, s_in)


def dn_chunk_bwd(qkv, proj3, alog_row, dtb_row, tinv, u, w, du, dw, dqd, dkd, dqk, dgc_scan, dproj3, name,
                 comm=None):
    b, t, _ = qkv.shape
    rblk = min(256, t)
    n_in = rblk // DN_CHUNK

    def body(q_ref, k_ref, v_ref, pba_ref, al_ref, dtb_ref, ti_ref, u_ref, w_ref,
             du_ref, dw_ref, dqd_ref, dkd_ref, dqk_ref, dgs_ref, dproj_in,
             dq_ref, dpba_ref, dal_ref, ddtb_ref):
        @pl.when(jnp.logical_and(pl.program_id(0) == 0, pl.program_id(1) == 0))
        def _():
            dal_ref[...] = jnp.zeros_like(dal_ref)
            ddtb_ref[...] = jnp.zeros_like(ddtb_ref)

        ea = jnp.exp(al_ref[...])
        incl0 = _chunk_masks()[0]
        tri = jnp.where(incl0, 1.0, 0.0)
        tri_up = jnp.where(_iota2((DN_CHUNK, DN_CHUNK), 1) >= _iota2((DN_CHUNK, DN_CHUNK), 0), 1.0, 0.0)
        lane = _iota2((DN_CHUNK, LANES), 1)
        last_col = _iota2((DN_CHUNK, 1), 0) == DN_CHUNK - 1

        _, strict, _ = _chunk_masks()
        gate_lane = jnp.logical_and(lane >= N_HEADS, lane < 2 * N_HEADS)

        def chunk_group(cg, carry):
            tiles, items = [], []
            for sub in range(LOCKSTEP_CHUNKS):
                rows = pl.ds(pl.multiple_of((cg * LOCKSTEP_CHUNKS + sub) * DN_CHUNK, DN_CHUNK), DN_CHUNK)
                pba = pba_ref[0, rows, :]
                beta_all, g_all = _gates(pba, ea, dtb_ref[...])
                gc = _dot_exact_lhs(tri, g_all)
                tiles.append((rows, pba, beta_all, g_all))
                for h in range(N_HEADS):
                    items.append((sub, rows, h, slice(h * HEAD_DIM, (h + 1) * HEAD_DIM),
                                  beta_all[:, h:h + 1], gc[:, N_HEADS + h:N_HEADS + h + 1]))
            decays = [_chunk_decay(gcol)[0] for _, _, _, _, _, gcol in items]
            egs = [jnp.exp(gcol) for _, _, _, _, _, gcol in items]
            qbs = [(q_ref[0, rows, cs] * QK_SCALE).astype(BF16) for _, rows, h, cs, _, _ in items]
            kfs = [k_ref[0, rows, cs].astype(BF16) for _, rows, h, cs, _, _ in items]
            kbs = [k_ref[0, rows, cs] * beta for _, rows, h, cs, beta, _ in items]
            kbbs = [kb.astype(BF16) for kb in kbs]
            tsps = [_split(ti_ref[0, h, rows, :]) for _, rows, h, cs, _, _ in items]
            drus = [_dot3(tsp, _split(du_ref[0, rows, cs]), TN)
                    for (_, rows, h, cs, _, _), tsp in zip(items, tsps)]
            drws = [_dot3(tsp, _split(dw_ref[0, rows, cs]), TN)
                    for (_, rows, h, cs, _, _), tsp in zip(items, tsps)]
            m_kks = [_dot(kbb, kf, NT) for kbb, kf in zip(kbbs, kfs)]
            a_qks = [_dot(qb, kf, NT) for qb, kf in zip(qbs, kfs)]
            dls = [-jnp.where(strict, _dot3(_split(dru), _split(u_ref[0, rows, cs]), NT)
                              + _dot3(_split(drw), _split(w_ref[0, rows, cs]), NT), 0.0)
                   for (_, rows, h, cs, _, _), dru, drw in zip(items, drus, drws)]
            dms = [(dl * dc).astype(BF16) for dl, dc in zip(dls, decays)]
            das = [(dqk_ref[0, h, rows, :] * dc).astype(BF16)
                   for (_, rows, h, cs, _, _), dc in zip(items, decays)]
            dkb_mm = [_dot(dm, kf, NN) for dm, kf in zip(dms, kfs)]
            dk_mm = [_dot(dm, kbb, TN) + _dot(da, qb, TN) for dm, kbb, da, qb in zip(dms, kbbs, das, qbs)]
            dqs_mm = [_dot(da, kf, NN) for da, kf in zip(das, kfs)]
            dgc_tiles = [dgs_ref[0, rows, :] for rows, _, _, _ in tiles]
            dbeta_tiles = [jnp.zeros((DN_CHUNK, LANES), F32) for _ in tiles]
            for n_it, (sub, rows, h, cs, beta, gcol) in enumerate(items):
                eg, dc = egs[n_it], decays[n_it]
                k = k_ref[0, rows, cs]
                q = q_ref[0, rows, cs] * QK_SCALE
                kb, dru, drw = kbs[n_it], drus[n_it], drws[n_it]
                ek = jnp.exp(gcol[DN_CHUNK - 1:DN_CHUNK, :] - gcol)
                e_mat = (dls[n_it] * m_kks[n_it] + dqk_ref[0, h, rows, :] * a_qks[n_it]) * dc
                dkb = drw * eg + dkb_mm[n_it]
                dqd = dqd_ref[0, rows, cs]
                dkd = dkd_ref[0, rows, cs]
                kdk = dkd * k * ek
                kdk_total = jnp.sum(jnp.sum(kdk, axis=0, keepdims=True), axis=1, keepdims=True)
                dg = (jnp.sum(drw * kb * eg + dqd * q * eg - kdk, axis=-1, keepdims=True)
                      + jnp.sum(e_mat, axis=1, keepdims=True)
                      - _row_to_col(jnp.sum(e_mat, axis=0, keepdims=True))
                      + jnp.where(last_col, kdk_total, 0.0))
                dbeta = jnp.sum(dkb * k + dru * v_ref[0, rows, cs], axis=-1, keepdims=True)
                dq_ref[0, rows, cs] = (dqs_mm[n_it] + dqd * eg) * QK_SCALE
                dq_ref[0, rows, pl.ds(HALF_W + h * HEAD_DIM, HEAD_DIM)] = dk_mm[n_it] + dkd * ek + dkb * beta
                dq_ref[0, rows, pl.ds(2 * HALF_W + h * HEAD_DIM, HEAD_DIM)] = dru * beta
                dgc_tiles[sub] = dgc_tiles[sub] + jnp.where(lane == N_HEADS + h, dg, 0.0)
                dbeta_tiles[sub] = dbeta_tiles[sub] + jnp.where(lane == h, dbeta, 0.0)
            for (rows, pba, beta_all, g_all), dgc_tile, dbeta_tile in zip(tiles, dgc_tiles, dbeta_tiles):
                dg_tile = _dot_exact_lhs(tri_up, dgc_tile)
                da_pre = dg_tile * (-ea) * _sigmoid(pba + dtb_ref[...])
                dal_ref[...] += jnp.sum(jnp.where(gate_lane, dg_tile * g_all, 0.0), axis=0, keepdims=True)
                ddtb_ref[...] += jnp.sum(jnp.where(gate_lane, da_pre, 0.0), axis=0, keepdims=True)
                dpba_ref[0, rows, :] = jnp.where(lane < N_HEADS, dbeta_tile * beta_all * (1.0 - beta_all),
                                                 jnp.where(gate_lane, da_pre, 0.0)).astype(BF16)
            return carry

        lax.fori_loop(0, n_in // LOCKSTEP_CHUNKS, chunk_group, 0)

    def seg(cblk):
        return pl.BlockSpec((1, rblk, HALF_W), lambda i, r: (i, r, cblk))

    vec = pl.BlockSpec((1, LANES), lambda i, r: (0, 0))
    wide = pl.BlockSpec((1, rblk, HALF_W), lambda i, r: (i, r, 0))
    sq = pl.BlockSpec((1, N_HEADS, rblk, DN_CHUNK), lambda i, r: (i, 0, r, 0))
    gates = pl.BlockSpec((1, rblk, LANES), lambda i, r: (i, r, 0))
    return _call(
        body, name=name, grid=(b, t // rblk),
        in_specs=[seg(0), seg(1), seg(2),
                  pl.BlockSpec((1, rblk, LANES), lambda i, r: (i, r, GATE_COL_BLOCK)), vec, vec,
                  sq, wide, wide, wide, wide, wide, wide, sq, gates, _ANY],
        out_specs=[pl.BlockSpec((1, rblk, 3 * HALF_W), lambda i, r: (i, r, 0)),
                   pl.BlockSpec((1, rblk, LANES), lambda i, r: (i, r, GATE_COL_BLOCK)), vec, vec],
        out_shape=[_sds((b, t, 3 * HALF_W), F32), _sds(dproj3.shape, BF16),
                   _sds((1, LANES), F32), _sds((1, LANES), F32)],
        input_output_aliases={15: 1}, comm=comm,
    )(qkv, qkv, qkv, proj3, alog_row, dtb_row, tinv, u, w, du, dw, dqd, dkd, dqk, dgc_scan, dproj3)


def dn_out_fwd(o, proj, dn_norm, name):
    n = o.shape[0]
    tm = min(ROW_TILE, n)

    def body(o_ref, z_ref, g_ref, y_ref):
        for h in range(N_HEADS):
            cs = slice(h * HEAD_DIM, (h + 1) * HEAD_DIM)
            oh = o_ref[:, cs]
            z = z_ref[:, cs]
            y = oh * _rms_scale(oh) * g_ref[...]
            y_ref[:, cs] = (y * (z * _sigmoid(z))).astype(BF16)

    half = pl.BlockSpec((tm, HALF_W), lambda i: (i, 0))
    return _call(
        body, name=name, grid=(n // tm,),
        in_specs=[half, pl.BlockSpec((tm, HALF_W), lambda i: (i, 5)),
                  pl.BlockSpec((1, HEAD_DIM), lambda i: (0, 0))],
        out_specs=half, out_shape=_sds((n, HALF_W), BF16),
    )(o, proj, dn_norm)


def dn_out_bwd(dy, o, proj, dn_norm, dproj, name):
    n = o.shape[0]
    tm = min(ROW_TILE, n)

    def body(dy_ref, o_ref, z_ref, g_ref, dproj_in, do_ref, dz_ref, dg_ref):
        @pl.when(pl.program_id(0) == 0)
        def _():
            dg_ref[...] = jnp.zeros_like(dg_ref)

        g = g_ref[...]
        dg = jnp.zeros_like(g)
        for h in range(N_HEADS):
            cs = slice(h * HEAD_DIM, (h + 1) * HEAD_DIM)
            oh = o_ref[:, cs]
            z = z_ref[:, cs]
            d = dy_ref[:, cs]
            r = _rms_scale(oh)
            nh = oh * r
            sz = _sigmoid(z)
            dyn = d * (z * sz)
            dz_ref[:, cs] = (d * (nh * g) * (sz * (1.0 + z * (1.0 - sz)))).astype(BF16)
            dg = dg + jnp.sum(dyn * nh, axis=0, keepdims=True)
            dn = dyn * g
            do_ref[:, cs] = r * (dn - nh * jnp.mean(dn * nh, axis=-1, keepdims=True))
        dg_ref[...] += dg

    half = pl.BlockSpec((tm, HALF_W), lambda i: (i, 0))
    vec = pl.BlockSpec((1, HEAD_DIM), lambda i: (0, 0))
    return _call(
        body, name=name, grid=(n // tm,),
        in_specs=[half, half, pl.BlockSpec((tm, HALF_W), lambda i: (i, 5)), vec, _ANY],
        out_specs=[half, pl.BlockSpec((tm, HALF_W), lambda i: (i, 5)), vec],
        out_shape=[_sds((n, HALF_W), F32), _sds(dproj.shape, BF16), _sds((1, HEAD_DIM), F32)],
        input_output_aliases={4: 1},
    )(dy, o, proj, dn_norm, dproj)


def _adamw_math(w, g, m, v):
    m_new = ADAM_B1 * m + (1.0 - ADAM_B1) * g
    v_new = ADAM_B2 * v + (1.0 - ADAM_B2) * (g * g)
    m_hat = m_new / (1.0 - ADAM_B1 ** ADAM_STEP)
    v_hat = v_new / (1.0 - ADAM_B2 ** ADAM_STEP)
    delta = -ADAM_LR * (m_hat / (jnp.sqrt(v_hat) + ADAM_EPS) + ADAM_WD * w)
    return delta, m_new, v_new


def adamw(w, g, m, v, name):
    r, c = w.shape
    tr = r
    for cand in (256, 352):
        if r % cand == 0 and r > cand:
            tr = cand
            break

    def body(w_ref, g_ref, m_ref, v_ref, d_ref, mo_ref, vo_ref):
        d, mn, vn = _adamw_math(w_ref[...], g_ref[...], m_ref[...], v_ref[...])
        d_ref[...] = d
        mo_ref[...] = mn
        vo_ref[...] = vn

    spec = pl.BlockSpec((tr, c), lambda i: (i, 0))
    return _call(
        body, name=name, grid=(r // tr,),
        in_specs=[spec] * 4, out_specs=[spec] * 3, out_shape=[_sds((r, c), F32)] * 3,
    )(w, g, m, v)


def _place():
    return lax.axis_index("x"), lax.axis_index("y"), lax.axis_index("c")


def _other_chips(x, y):
    return [(1 - x, y), (x, 1 - y), (1 - x, 1 - y)]


_ANY = pl.BlockSpec(memory_space=pl.ANY)


def cast_place(w, shard_idx, name):
    r, cols = w.shape
    tr = r // 2

    def body(j_ref, w_ref, o_ref):
        o_ref[0] = w_ref[...].astype(BF16)

    return pl.pallas_call(
        body, name=name,
        grid_spec=pltpu.PrefetchScalarGridSpec(
            num_scalar_prefetch=1, grid=(r // tr,),
            in_specs=[pl.BlockSpec((tr, cols), lambda i, j: (i, 0))],
            out_specs=pl.BlockSpec((1, tr, cols), lambda i, j: (j[0], i, 0))),
        out_shape=_sds((N_SHARD, r, cols), BF16),
        compiler_params=pltpu.CompilerParams(dimension_semantics=("arbitrary",),
                                             vmem_limit_bytes=VMEM_LIMIT),
    )(shard_idx, w)


class Exchange:
    def __init__(self, inputs, out_shape, aliases, sems, phases):
        self.inputs, self.out_shape, self.aliases = list(inputs), list(out_shape), dict(aliases)
        self.sems, self.phases = list(sems), list(phases)


def run_exchange(ex, name):
    def body(*refs):
        n_in, n_out = len(ex.inputs), len(ex.out_shape)
        for _, fn in ex.phases:
            fn(refs[:n_in], refs[n_in:n_in + n_out], refs[n_in + n_out:])

    return _call(body, name=name, in_specs=[_ANY] * len(ex.inputs), out_specs=[_ANY] * len(ex.out_shape),
                 out_shape=ex.out_shape, scratch=ex.sems, input_output_aliases=ex.aliases)(*ex.inputs)


def merge_exchanges(exs):
    inputs, out_shape, sems, aliases, phases, out_slices = [], [], [], {}, [], []
    for ex in exs:
        i0, o0, s0 = len(inputs), len(out_shape), len(sems)
        inputs += ex.inputs
        out_shape += ex.out_shape
        sems += ex.sems
        for k, m in ex.aliases.items():
            aliases[i0 + k] = o0 + m
        si, so, ss = slice(i0, len(inputs)), slice(o0, len(out_shape)), slice(s0, len(sems))
        out_slices.append(so)
        for step, fn in ex.phases:
            phases.append((step, lambda ins, outs, sm, fn=fn, si=si, so=so, ss=ss: fn(ins[si], outs[so], sm[ss])))
    return Exchange(inputs, out_shape, aliases, sems, phases), out_slices


def _dma_sems(*sizes):
    return [pltpu.SemaphoreType.DMA((s,)) for s in sizes]


def gather_exchange(bufs, small=None, relay_step=-2):
    n = len(bufs)
    n_small = 0 if small is None else 1

    def half(outs, a, blk, hc):
        rh = bufs[a].shape[1] // 2
        return outs[a].at[blk, pl.ds(hc * rh, rh), :]

    def ici(outs, sems, a, k, blk, to):
        return pltpu.make_async_remote_copy(
            src_ref=half(outs, a, blk, to[2]), dst_ref=half(outs, a, blk, to[2]), send_sem=sems[0].at[3 * a + k],
            recv_sem=sems[1].at[3 * a + k], device_id=to, device_id_type=MESH)

    def d2d(outs, sems, a, k, blk, hc, to):
        return pltpu.make_async_remote_copy(
            src_ref=half(outs, a, blk, hc), dst_ref=half(outs, a, blk, hc), send_sem=sems[2].at[3 * a + k],
            recv_sem=sems[3].at[3 * a + k], device_id=to, device_id_type=MESH)

    def small_copy(ins, outs, sems, k, blk, to):
        return pltpu.make_async_remote_copy(
            src_ref=ins[n], dst_ref=outs[n].at[blk], send_sem=sems[0].at[3 * n + k],
            recv_sem=sems[1].at[3 * n + k], device_id=to, device_id_type=MESH)

    def start(ins, outs, sems):
        x, y, c = _place()
        j = 2 * x + y
        if n_small:
            pltpu.make_async_copy(ins[n], outs[n].at[j], sems[4].at[0]).start()
        for k, (px, py) in enumerate(_other_chips(x, y)):
            if n_small:
                small_copy(ins, outs, sems, k, j, (px, py, c)).start()
            for a in range(n):
                ici(outs, sems, a, k, j, (px, py, c)).start()

    def relay(ins, outs, sems):
        x, y, c = _place()
        for k, (px, py) in enumerate(_other_chips(x, y)):
            for a in range(n):
                ici(outs, sems, a, k, 2 * px + py, (px, py, c)).wait_recv()
                d2d(outs, sems, a, k, 2 * px + py, c, (x, y, 1 - c)).start()

    def finish(ins, outs, sems):
        x, y, c = _place()
        j = 2 * x + y
        for k, (px, py) in enumerate(_other_chips(x, y)):
            blk = 2 * px + py
            if n_small:
                small_copy(ins, outs, sems, k, blk, (px, py, c)).wait_recv()
                small_copy(ins, outs, sems, k, j, (px, py, c)).wait_send()
            for a in range(n):
                d2d(outs, sems, a, k, blk, 1 - c, (x, y, 1 - c)).wait_recv()
                ici(outs, sems, a, k, j, (px, py, c)).wait_send()
                d2d(outs, sems, a, k, blk, c, (x, y, 1 - c)).wait_send()
        if n_small:
            pltpu.make_async_copy(ins[n], outs[n].at[j], sems[4].at[0]).wait()

    out_shape = [_sds(b.shape, b.dtype) for b in bufs]
    if n_small:
        out_shape.append(_sds((N_SHARD,) + small.shape, small.dtype))
    return Exchange(list(bufs) + ([small] if n_small else []), out_shape, {a: a for a in range(n)},
                    _dma_sems(3 * n + 3, 3 * n + 3, 3 * n, 3 * n, 1),
                    [(0, start), (relay_step, relay), (-1, finish)])


def _start_then_wait(copies):
    def start(ins, outs, sems):
        for sent, _ in copies(ins, outs, sems):
            sent().start()

    def finish(ins, outs, sems):
        pairs = copies(ins, outs, sems)
        for _, arrival in pairs:
            arrival().wait_recv()
        for sent, _ in pairs:
            sent().wait_send()

    return [(0, start), (-1, finish)]


def pair_exchange(arrs):
    n = len(arrs)

    def copies(ins, outs, sems):
        x, y, c = _place()
        res = []
        for a in range(n):
            def mk(a=a):
                rh = arrs[a].shape[1] // 2
                return pltpu.make_async_remote_copy(
                    src_ref=ins[a].at[:, pl.ds((1 - c) * rh, rh), :], dst_ref=outs[a], send_sem=sems[0].at[a],
                    recv_sem=sems[1].at[a], device_id=(x, y, 1 - c), device_id_type=MESH)
            res.append((mk, mk))
        return res

    return Exchange(arrs, [_sds((a.shape[0], a.shape[1] // 2, a.shape[2]), a.dtype) for a in arrs], {},
                    _dma_sems(n, n), _start_then_wait(copies))


def pair_add(g, s, c_idx, name):
    nb, r, cols = g.shape
    rh = r // 2

    def body(c_ref, g_ref, s_ref, o_ref):
        o_ref[...] = (g_ref[...] + s_ref[...]).astype(BF16)

    return pl.pallas_call(
        body, name=name,
        grid_spec=pltpu.PrefetchScalarGridSpec(
            num_scalar_prefetch=1, grid=(nb,),
            in_specs=[pl.BlockSpec((1, rh, cols), lambda j, c: (j, c[0], 0)),
                      pl.BlockSpec((1, rh, cols), lambda j, c: (j, 0, 0))],
            out_specs=pl.BlockSpec((1, rh, cols), lambda j, c: (j, 0, 0))),
        out_shape=_sds((nb, rh, cols), BF16),
        compiler_params=pltpu.CompilerParams(dimension_semantics=("arbitrary",),
                                             vmem_limit_bytes=VMEM_LIMIT),
    )(c_idx, g, s)


def chip_exchange(arrs):
    n = len(arrs)

    def copies(ins, outs, sems):
        x, y, c = _place()
        j = 2 * x + y
        res = []
        for a in range(n):
            for k, (px, py) in enumerate(_other_chips(x, y)):
                def mk(src_blk, dst_blk, a=a, k=k, to=(px, py, c)):
                    return pltpu.make_async_remote_copy(
                        src_ref=ins[a].at[src_blk], dst_ref=outs[a].at[dst_blk], send_sem=sems[0].at[3 * a + k],
                        recv_sem=sems[1].at[3 * a + k], device_id=to, device_id_type=MESH)
                res.append((functools.partial(mk, 2 * px + py, j), functools.partial(mk, j, 2 * px + py)))
        return res

    return Exchange(arrs, [_sds(a.shape, a.dtype) for a in arrs], {}, _dma_sems(3 * n, 3 * n),
                    _start_then_wait(copies))


def sum_chips(r, p, shard_idx, name):
    nb, rh, cols = r.shape
    tr = rh

    def body(j_ref, p_ref, *refs):
        o_ref = refs[nb]
        j = j_ref[0]
        acc = None
        for i in range(nb):
            term = jnp.where(j == i, p_ref[0], refs[i][0]).astype(F32)
            acc = term if acc is None else acc + term
        o_ref[...] = acc

    def slot(i):
        return pl.BlockSpec((1, tr, cols), lambda t, j: (jnp.where(j[0] == i, (i + 1) % nb, i), t, 0))

    return pl.pallas_call(
        body, name=name,
        grid_spec=pltpu.PrefetchScalarGridSpec(
            num_scalar_prefetch=1, grid=(rh // tr,),
            in_specs=[pl.BlockSpec((1, tr, cols), lambda t, j: (j[0], t, 0))] + [slot(i) for i in range(nb)],
            out_specs=pl.BlockSpec((tr, cols), lambda t, j: (t, 0))),
        out_shape=_sds((rh, cols), F32),
        compiler_params=pltpu.CompilerParams(dimension_semantics=("arbitrary",),
                                             vmem_limit_bytes=VMEM_LIMIT),
    )(shard_idx, p, *([r] * nb))


def pair_swap(arrs):
    n = len(arrs)

    def copies(ins, outs, sems):
        x, y, c = _place()
        res = []
        for a in range(n):
            def mk(a=a):
                return pltpu.make_async_remote_copy(
                    src_ref=ins[a], dst_ref=outs[a], send_sem=sems[0].at[a], recv_sem=sems[1].at[a],
                    device_id=(x, y, 1 - c), device_id_type=MESH)
            res.append((mk, mk))
        return res

    return Exchange(arrs, [_sds(a.shape, a.dtype) for a in arrs], {}, _dma_sems(n, n),
                    _start_then_wait(copies))


ADAMW_STEPS_PER_HALF = 4


def adamw_pairs(items, name, comm=None):
    n_items = len(items)
    nh = ADAMW_STEPS_PER_HALF

    def body(*refs):
        ins, outs = refs[:5 * n_items], refs[5 * n_items:]
        mine = (pl.program_id(0) // nh) == lax.axis_index("c")
        for a in range(n_items):
            w_ref, gm_ref, gs_ref, m_ref, v_ref = ins[5 * a:5 * a + 5]
            g_ref, d_ref, mo_ref, vo_ref = outs[4 * a:4 * a + 4]
            g = jnp.where(mine, gm_ref[...], gs_ref[...])
            d, mn, vn = _adamw_math(w_ref[...], g, m_ref[...], v_ref[...])
            g_ref[...] = g
            d_ref[...] = d
            mo_ref[...] = mn
            vo_ref[...] = vn

    in_specs, out_specs, out_shape, args = [], [], [], []
    for w, g_mine, g_sib, m, v in items:
        r, cols = w.shape
        tr = r // (2 * nh)
        full = pl.BlockSpec((tr, cols), lambda i: (i, 0))
        part = pl.BlockSpec((tr, cols), lambda i: (i % nh, 0))
        in_specs += [full, part, part, full, full]
        out_specs += [full] * 4
        out_shape += [_sds((r, cols), F32)] * 4
        args += [w, g_mine, g_sib, m, v]
    res = _call(body, name=name, grid=(2 * nh,), in_specs=in_specs, out_specs=out_specs,
                out_shape=out_shape, comm=comm)(*args)
    own, hosted = (res, None) if comm is None else res
    grouped = [tuple(own[4 * a:4 * a + 4]) for a in range(n_items)]
    return grouped if comm is None else (grouped, hosted)


N_DEV = 8


def device_gather(pack):
    def copies(ins, outs, sems):
        x, y, c = _place()
        me = 4 * x + 2 * y + c
        res = []
        for k in range(1, N_DEV):
            fx, fy, fc = (k >> 2) & 1, (k >> 1) & 1, k & 1
            px, py, pc = (1 - x if fx else x, 1 - y if fy else y, 1 - c if fc else c)

            def mk(slot, k=k, to=(px, py, pc)):
                return pltpu.make_async_remote_copy(
                    src_ref=ins[0], dst_ref=outs[0].at[slot], send_sem=sems[0].at[k - 1],
                    recv_sem=sems[1].at[k - 1], device_id=to, device_id_type=MESH)
            res.append((functools.partial(mk, me), functools.partial(mk, 4 * px + 2 * py + pc)))
        return res

    return Exchange([pack], [_sds((N_DEV,) + pack.shape, pack.dtype)], {}, _dma_sems(N_DEV - 1, N_DEV - 1),
                    _start_then_wait(copies))


def sum_devices(buf, pack, me_idx, name):
    r, cols = pack.shape

    def body(me_ref, p_ref, *refs):
        o_ref = refs[N_DEV]
        acc = None
        for i in range(N_DEV):
            term = jnp.where(me_ref[0] == i, p_ref[...], refs[i][0])
            acc = term if acc is None else acc + term
        o_ref[...] = acc

    def slot(i):
        return pl.BlockSpec((1, r, cols), lambda t, me: (jnp.where(me[0] == i, (i + 1) % N_DEV, i), 0, 0))

    whole = pl.BlockSpec((r, cols), lambda t, me: (0, 0))
    return pl.pallas_call(
        body, name=name,
        grid_spec=pltpu.PrefetchScalarGridSpec(
            num_scalar_prefetch=1, grid=(1,),
            in_specs=[whole] + [slot(i) for i in range(N_DEV)], out_specs=whole),
        out_shape=_sds((r, cols), F32),
        compiler_params=pltpu.CompilerParams(dimension_semantics=("arbitrary",),
                                             vmem_limit_bytes=VMEM_LIMIT),
    )(me_idx, pack, *([buf] * N_DEV))


SMALL_NAMES = ("ffn1_norm", "mix_norm", "ffn2_norm", "final_norm", "sg_ln_g", "sg_ln_b",
               "dn_norm", "a_log", "dt_bias", "sg_b", "sg_w", "conv_w", "loss")


def _to_rows(a):
    flat = a.reshape(-1)
    pad = (-flat.shape[0]) % LANES
    if pad:
        flat = jnp.pad(flat, (0, pad))
    return flat.reshape(-1, LANES)


def _pack_small(parts):
    rows = [_to_rows(parts[k]) for k in SMALL_NAMES]
    pack = jnp.concatenate(rows, axis=0)
    pad = (-pack.shape[0]) % 8
    if pad:
        pack = jnp.pad(pack, ((0, pad), (0, 0)))
    return pack


def _unpack_small(pack, shapes):
    out, r0 = {}, 0
    for k in SMALL_NAMES:
        size = 1
        for s in shapes[k]:
            size *= s
        nrows = -(-size // LANES)
        out[k] = pack[r0:r0 + nrows].reshape(-1)[:size].reshape(shapes[k])
        r0 += nrows
    return out


def kernel(x, ffn1_norm, ffn1_w_gate, ffn1_w_up, ffn1_w_down, mix_norm, w_in, conv_w, a_log, dt_bias, dn_norm, sg_ln_g, sg_ln_b, sg_w, sg_b, w_out, ffn2_norm, ffn2_w_gate, ffn2_w_up, ffn2_w_down, final_norm, loss_target, m_ffn1_norm, m_ffn1_w_gate, m_ffn1_w_up, m_ffn1_w_down, m_mix_norm, m_w_in, m_conv_w, m_a_log, m_dt_bias, m_dn_norm, m_sg_ln_g, m_sg_ln_b, m_sg_w, m_sg_b, m_w_out, m_ffn2_norm, m_ffn2_w_gate, m_ffn2_w_up, m_ffn2_w_down, m_final_norm, v_ffn1_norm, v_ffn1_w_gate, v_ffn1_w_up, v_ffn1_w_down, v_mix_norm, v_w_in, v_conv_w, v_a_log, v_dt_bias, v_dn_norm, v_sg_ln_g, v_sg_ln_b, v_sg_w, v_sg_b, v_w_out, v_ffn2_norm, v_ffn2_w_gate, v_ffn2_w_up, v_ffn2_w_down, v_final_norm):
    bsz, t_len, d = x.shape
    n = bsz * t_len
    xy, yy, cc = _place()
    shard = 2 * xy + yy

    big_names = ["ffn1_w_gate", "ffn1_w_up", "ffn1_w_down", "w_in", "w_out",
                 "ffn2_w_gate", "ffn2_w_up", "ffn2_w_down"]
    big_w = dict(ffn1_w_gate=ffn1_w_gate, ffn1_w_up=ffn1_w_up, ffn1_w_down=ffn1_w_down, w_in=w_in,
                 w_out=w_out, ffn2_w_gate=ffn2_w_gate, ffn2_w_up=ffn2_w_up, ffn2_w_down=ffn2_w_down)
    big_m = dict(ffn1_w_gate=m_ffn1_w_gate, ffn1_w_up=m_ffn1_w_up, ffn1_w_down=m_ffn1_w_down, w_in=m_w_in,
                 w_out=m_w_out, ffn2_w_gate=m_ffn2_w_gate, ffn2_w_up=m_ffn2_w_up, ffn2_w_down=m_ffn2_w_down)
    big_v = dict(ffn1_w_gate=v_ffn1_w_gate, ffn1_w_up=v_ffn1_w_up, ffn1_w_down=v_ffn1_w_down, w_in=v_w_in,
                 w_out=v_w_out, ffn2_w_gate=v_ffn2_w_gate, ffn2_w_up=v_ffn2_w_up, ffn2_w_down=v_ffn2_w_down)
    shard_idx = jnp.reshape(shard, (1,)).astype(jnp.int32)
    c_idx = jnp.reshape(cc, (1,)).astype(jnp.int32)
    transposed = ("ffn1_w_gate", "ffn1_w_up", "ffn2_w_gate", "ffn2_w_up")

    def as2d(a, k):
        return a[0].T if k in transposed else a[0]

    def from2d(a, k):
        return a.T[None] if k in transposed else a[None]

    placed = {k: cast_place(as2d(big_w[k], k), shard_idx, name="cast_" + k) for k in big_names}
    first_names = big_names[:3]
    later_names = big_names[3:]
    res = run_exchange(gather_exchange([placed[k] for k in first_names], conv_w[0]), name="gather_first")
    gw = dict(zip(first_names, res[:3]))
    conv_full = res[3].transpose(1, 0, 2).reshape(CONV_K, 3 * HALF_W)

    x0 = x.reshape(n, d)
    def ffn_weights(prefix):
        return [gw[prefix + k].reshape(-1, d) for k in ("_w_gate", "_w_up", "_w_down")]

    def ffn_grad_blocks(grads):
        return [g.reshape(N_SHARD, -1, d) for g in grads]

    (x1, h1, gate1, up1, act1), later = ffn_fwd(
        x0, ffn1_norm, *ffn_weights("ffn1"), name="ffn1_fwd",
        comm=gather_exchange([placed[k] for k in later_names]))
    gw.update(zip(later_names, later))
    w_in_full = gw["w_in"].transpose(1, 0, 2).reshape(d, IN_COLS)
    w_in_full = jnp.pad(w_in_full, ((0, 0), (0, PROJ_W - IN_COLS)))
    w_out_full = gw["w_out"].reshape(2 * HALF_W, d)
    proj, h2 = in_proj_fwd(x1, mix_norm, w_in_full, name="in_proj_fwd")
    proj3 = proj.reshape(bsz, t_len, PROJ_W)
    bias_tile = jnp.repeat(sg_b[0].T, SG_GROUP_DIM, axis=1)
    sg_out = sg_fwd(proj, sg_ln_g, sg_ln_b, sg_w[0], bias_tile, name="sg_fwd")
    qkv = dn_conv_fwd(proj3, conv_full, name="dn_conv_fwd")
    alog_row = jnp.zeros((1, LANES), F32).at[0, N_HEADS:2 * N_HEADS].set(a_log[0])
    dtb_row = jnp.zeros((1, LANES), F32).at[0, N_HEADS:2 * N_HEADS].set(dt_bias[0])
    u_wy, w_wy, q_dec, k_dec, qk, tinv, gc = dn_chunk_fwd(qkv, proj3, alog_row, dtb_row,
                                                           name="dn_chunk_fwd")
    o, s_in = dn_scan_fwd(u_wy, w_wy, q_dec, k_dec, qk, gc, name="dn_scan_fwd")
    dn_out = dn_out_fwd(o.reshape(n, HALF_W), proj, dn_norm, name="dn_out_fwd")
    x2 = out_proj_fwd(x1, sg_out, dn_out, w_out_full, name="out_proj_fwd")
    x3, h3, gate2, up2, act2 = ffn_fwd(x2, ffn2_norm, *ffn_weights("ffn2"), name="ffn2_fwd")
    dx3, d_final_norm, loss_tile, dyh2 = final_loss(x3, final_norm.reshape(1, d),
                                                    loss_target.reshape(n, d), name="final_loss")

    dx2, dgate2, dup2, d_ffn2_norm = ffn_bwd_act(
        dx3, dyh2, x2, ffn2_norm, gate2, up2, *ffn_weights("ffn2"), name="ffn2_bwd_act")
    g_big = {}
    g_big["ffn2_w_gate"], g_big["ffn2_w_up"], g_big["ffn2_w_down"] = ffn_grad_blocks(ffn_bwd_w(
        [dgate2, dup2, act2], [h3, dyh2], [(0, 0), (1, 0), (2, 1)], name="ffn2_bwd_w"))

    early = ["ffn2_w_gate", "ffn2_w_up", "ffn2_w_down"]
    (d_sg, d_dn, dx2b), early_sib = out_proj_bwd_x(dx2, w_out_full, name="out_proj_bwd_x",
                                                   comm=pair_exchange([g_big[k] for k in early]))
    early_sums = [pair_add(g_big[k], s, c_idx, name="grad_pair_add_" + k) for k, s in zip(early, early_sib)]
    g_w_out = matmul_tn([sg_out, dn_out], dx2b, d, name="w_out_grad")
    g_big["w_out"] = g_w_out.reshape(N_SHARD, (2 * HALF_W) // N_SHARD, d)

    d_proj, d_sg_w, d_bias_tile, d_ln_g, d_ln_b = sg_bwd(d_sg, proj, sg_ln_g, sg_ln_b, sg_w[0],
                                                         bias_tile, name="sg_bwd")
    d_o, d_proj, d_dn_norm = dn_out_bwd(d_dn, o.reshape(n, HALF_W), proj, dn_norm, d_proj,
                                        name="dn_out_bwd")
    du, dw, dqd, dkd, dqk, dgc_scan = dn_scan_bwd(d_o.reshape(bsz, t_len, HALF_W), u_wy, w_wy, q_dec,
                                                  k_dec, qk, gc, s_in, name="dn_scan_bwd")
    (d_qkv, d_proj3, d_alog_row, d_dtb_row), early_chips = dn_chunk_bwd(
        qkv, proj3, alog_row, dtb_row, tinv, u_wy, w_wy, du, dw, dqd, dkd, dqk, dgc_scan,
        d_proj.reshape(bsz, t_len, PROJ_W), name="dn_chunk_bwd", comm=chip_exchange(early_sums))
    early_halves = [sum_chips(r, p, shard_idx, name="grad_chip_sum_" + k)
                    for k, r, p in zip(early, early_chips, early_sums)]
    (d_proj3, d_conv), early_sib_halves = dn_conv_bwd(d_qkv, proj3, conv_full, d_proj3, name="dn_conv_bwd",
                                                      comm=pair_swap(early_halves))
    d_proj = d_proj3.reshape(n, PROJ_W)
    g_w_in = matmul_tn([h2], d_proj, 3 * MXU_DIM, name="w_in_grad")[:, :IN_COLS]
    g_big["w_in"] = g_w_in.reshape(d, N_SHARD, IN_COLS // N_SHARD).transpose(1, 0, 2)

    def reduce_start(names):
        return pair_exchange([g_big[k] for k in names])

    def reduce_pair_sums(names, from_sib):
        return [pair_add(g_big[k], s, c_idx, name="grad_pair_add_" + k) for k, s in zip(names, from_sib)]

    def reduce_chip_sums(names, from_chips, sums):
        return [sum_chips(r, p, shard_idx, name="grad_chip_sum_" + k)
                for k, r, p in zip(names, from_chips, sums)]

    mid = ["w_in", "w_out"]
    (dx1, d_mix_norm, dyh1), mid_sib = in_proj_bwd_x(d_proj, w_in_full, x1, mix_norm, dx2,
                                                     name="in_proj_bwd_x", comm=reduce_start(mid))
    mid_sums = reduce_pair_sums(mid, mid_sib)
    down = ["ffn1_w_down"]
    (g_down,), mid_chips = ffn_bwd_w([act1], [dyh1], [(0, 0)], name="ffn1_bwd_w_down",
                                     comm=chip_exchange(mid_sums))
    g_big["ffn1_w_down"] = g_down.reshape(N_SHARD, -1, d)
    mid_halves = reduce_chip_sums(mid, mid_chips, mid_sums)
    leg, legs = merge_exchanges([reduce_start(down), pair_swap(mid_halves)])
    leg_res = run_exchange(leg, name="grad_pair_exchange_down")
    down_sums = reduce_pair_sums(down, leg_res[legs[0]])
    mid_sib_halves = leg_res[legs[1]]

    dx0, dgate1, dup1, d_ffn1_norm = ffn_bwd_act(
        dx1, dyh1, x0, ffn1_norm, gate1, up1, *ffn_weights("ffn1"), name="ffn1_bwd_act")
    grad_x = dx0.reshape(bsz, t_len, d)
    d_sg_b = d_bias_tile.reshape(SG_CHUNK, SG_GROUPS, SG_GROUP_DIM).sum(axis=-1).T
    small_g = dict(ffn1_norm=d_ffn1_norm, mix_norm=d_mix_norm, ffn2_norm=d_ffn2_norm,
                   final_norm=d_final_norm, sg_ln_g=d_ln_g, sg_ln_b=d_ln_b, dn_norm=d_dn_norm,
                   a_log=d_alog_row[:, N_HEADS:2 * N_HEADS], dt_bias=d_dtb_row[:, N_HEADS:2 * N_HEADS],
                   sg_b=d_sg_b, sg_w=d_sg_w, conv_w=d_conv, loss=loss_tile[:, :1])
    my_pack = _pack_small(small_g)
    hosted, parts = merge_exchanges([chip_exchange(down_sums), device_gather(my_pack)])
    late = ["ffn1_w_gate", "ffn1_w_up"]
    late_grads, hosted_res = ffn_bwd_w([dgate1, dup1], [h1], [(0, 0), (1, 0)], name="ffn1_bwd_w_gate_up",
                                       comm=hosted)
    g_big["ffn1_w_gate"], g_big["ffn1_w_up"] = ffn_grad_blocks(late_grads)
    down_halves = reduce_chip_sums(down, hosted_res[parts[0]], down_sums)
    (all_packs,) = hosted_res[parts[1]]

    leg, legs = merge_exchanges([reduce_start(late), pair_swap(down_halves)])
    leg_res = run_exchange(leg, name="grad_pair_exchange")
    pair_sums = reduce_pair_sums(late, leg_res[legs[0]])
    down_sib_halves = leg_res[legs[1]]

    def adam_items(names, mine, sib):
        return [(as2d(big_w[k], k), gm, gs, as2d(big_m[k], k), as2d(big_v[k], k))
                for k, gm, gs in zip(names, mine, sib)]

    outs = {}
    done = adamw_pairs(
        adam_items(early + mid + down, early_halves + mid_halves + down_halves,
                   list(early_sib_halves) + list(mid_sib_halves) + list(down_sib_halves)),
        name="adamw_early")
    from_chips = run_exchange(chip_exchange(pair_sums), name="grad_chip_exchange")
    halves = reduce_chip_sums(late, from_chips, pair_sums)
    sib_halves = run_exchange(pair_swap(halves), name="grad_pair_swap")
    done += adamw_pairs(adam_items(late, halves, sib_halves), name="adamw_late")
    for k, res in zip(early + mid + down + late, done):
        outs[k] = tuple(from2d(a, k) for a in res)

    small_w = dict(ffn1_norm=ffn1_norm, mix_norm=mix_norm, ffn2_norm=ffn2_norm, final_norm=final_norm,
                   sg_ln_g=sg_ln_g, sg_ln_b=sg_ln_b, dn_norm=dn_norm, a_log=a_log, dt_bias=dt_bias,
                   sg_b=sg_b, sg_w=sg_w)
    small_m = dict(ffn1_norm=m_ffn1_norm, mix_norm=m_mix_norm, ffn2_norm=m_ffn2_norm,
                   final_norm=m_final_norm, sg_ln_g=m_sg_ln_g, sg_ln_b=m_sg_ln_b, dn_norm=m_dn_norm,
                   a_log=m_a_log, dt_bias=m_dt_bias, sg_b=m_sg_b, sg_w=m_sg_w)
    small_v = dict(ffn1_norm=v_ffn1_norm, mix_norm=v_mix_norm, ffn2_norm=v_ffn2_norm,
                   final_norm=v_final_norm, sg_ln_g=v_sg_ln_g, sg_ln_b=v_sg_ln_b, dn_norm=v_dn_norm,
                   a_log=v_a_log, dt_bias=v_dt_bias, sg_b=v_sg_b, sg_w=v_sg_w)
    shapes = {k: small_w[k].shape for k in small_w}
    shapes["conv_w"] = (CONV_K, 3 * HALF_W)
    shapes["loss"] = (1, 1)
    me_idx = jnp.reshape(4 * xy + 2 * yy + cc, (1,)).astype(jnp.int32)
    g_pack = sum_devices(all_packs, my_pack, me_idx, name="small_sum")
    g_small = _unpack_small(g_pack, shapes)
    loss = g_small["loss"].reshape(())
    cw = 3 * HALF_W // N_SHARD
    g_conv = lax.dynamic_slice_in_dim(g_small["conv_w"], shard * cw, cw, axis=1)
    zero_conv = jnp.zeros((CONV_K, 3 * HALF_W), F32)

    def packed(src, conv):
        parts = dict(src)
        parts["conv_w"] = lax.dynamic_update_slice_in_dim(zero_conv, conv[0], shard * cw, axis=1)
        parts["loss"] = jnp.zeros((1, 1), F32)
        return _pack_small(parts)

    d_pack, m_pack, v_pack = adamw(packed(small_w, conv_w), g_pack, packed(small_m, m_conv_w),
                                   packed(small_v, v_conv_w), name="adamw_small")
    d_small = _unpack_small(d_pack, shapes)
    m_small = _unpack_small(m_pack, shapes)
    v_small = _unpack_small(v_pack, shapes)

    def conv_block(full_arr):
        return lax.dynamic_slice_in_dim(full_arr, shard * cw, cw, axis=1)[None]

    for k in small_w:
        outs[k] = (g_small[k].reshape(small_w[k].shape), d_small[k], m_small[k], v_small[k])
    outs["conv_w"] = (g_conv[None], conv_block(d_small["conv_w"]), conv_block(m_small["conv_w"]),
                      conv_block(v_small["conv_w"]))

    order = ["ffn1_norm", "ffn1_w_gate", "ffn1_w_up", "ffn1_w_down", "mix_norm", "w_in", "conv_w",
             "a_log", "dt_bias", "dn_norm", "sg_ln_g", "sg_ln_b", "sg_w", "sg_b", "w_out", "ffn2_norm",
             "ffn2_w_gate", "ffn2_w_up", "ffn2_w_down", "final_norm"]
    return (loss, grad_x, *[outs[k][0] for k in order], *[outs[k][1] for k in order],
            *[outs[k][2] for k in order], *[outs[k][3] for k in order])
```

```python
import functools

import jax
import jax.numpy as jnp
from jax import lax
from jax.experimental import pallas as pl
from jax.experimental.pallas import tpu as pltpu

F32 = jnp.float32
BF16 = jnp.bfloat16
EPS = 1e-6

D_MODEL = 1024
N_SHARD = 4
HEAD_DIM = 128
N_HEADS = 4
DN_CHUNK = 64
SG_CHUNK = 128
SG_GROUPS = 8
SG_GROUP_DIM = 64
HALF_W = 512
PROJ_W = 3200
IN_COLS = 3080
GATE_COL_BLOCK = 24
QK_SCALE = HEAD_DIM ** -0.5
LANES = 128

ADAM_LR = 0.001
ADAM_B1 = 0.9
ADAM_B2 = 0.999
ADAM_EPS = 1e-08
ADAM_WD = 0.01
ADAM_STEP = 10

VMEM_LIMIT = 56 * 1024 * 1024
ROW_TILE = 512

NN = ((1,), (0,))
NT = ((1,), (1,))
TN = ((0,), (0,))
MESH = pl.DeviceIdType.MESH


def _dot(a, b, dims):
    return lax.dot_general(a, b, (dims, ((), ())), preferred_element_type=F32)


def _bdot(a, b, dims):
    return _dot(a.astype(BF16), b.astype(BF16), dims)


def _split(a):
    hi = a.astype(BF16)
    lo = (a - hi.astype(F32)).astype(BF16)
    return hi, lo


def _dot3(a, b, dims=NN):
    return _dot(a[0], b[0], dims) + (_dot(a[0], b[1], dims) + _dot(a[1], b[0], dims))


def _dot_exact_lhs(a, b):
    ab = a.astype(BF16)
    b1 = b.astype(BF16)
    r1 = b - b1.astype(F32)
    b2 = r1.astype(BF16)
    b3 = (r1 - b2.astype(F32)).astype(BF16)
    return _dot(ab, b1, NN) + (_dot(ab, b2, NN) + _dot(ab, b3, NN))


def _call(body, *, name, out_shape, in_specs, out_specs, grid=(), scratch=(), comm=None, **kw):
    params = dict(vmem_limit_bytes=VMEM_LIMIT)
    if grid:
        params["dimension_semantics"] = ("arbitrary",) * len(grid)
    if comm is None:
        return pl.pallas_call(
            body, name=name, grid=grid, in_specs=in_specs, out_specs=out_specs,
            out_shape=out_shape, scratch_shapes=list(scratch),
            compiler_params=pltpu.CompilerParams(**params), **kw)

    n_in, n_out, n_sc = len(in_specs), len(out_specs), len(scratch)
    c_in, c_out = len(comm.inputs), len(comm.out_shape)
    steps = 1
    for g in grid:
        steps *= g

    def hosted(*refs):
        ins, cins = refs[:n_in], refs[n_in:n_in + c_in]
        o0 = n_in + c_in
        outs, couts = refs[o0:o0 + n_out], refs[o0 + n_out:o0 + n_out + c_out]
        s0 = o0 + n_out + c_out
        sc, csems = refs[s0:s0 + n_sc], refs[s0 + n_sc:]
        lin = 0
        for axis, g in enumerate(grid):
            lin = lin * g + pl.program_id(axis)

        def at(step, fn):
            @pl.when(lin == step % steps)
            def _():
                fn(cins, couts, csems)

        for step, fn in comm.phases:
            if step >= 0:
                at(step, fn)
        body(*ins, *outs, *sc)
        for step, fn in comm.phases:
            if step < 0:
                at(step, fn)

    aliases = dict(kw.pop("input_output_aliases", {}))
    for k, m in comm.aliases.items():
        aliases[n_in + k] = n_out + m
    call = pl.pallas_call(
        hosted, name=name, grid=grid, in_specs=list(in_specs) + [_ANY] * c_in,
        out_specs=list(out_specs) + [_ANY] * c_out, out_shape=list(out_shape) + comm.out_shape,
        scratch_shapes=list(scratch) + comm.sems, input_output_aliases=aliases,
        compiler_params=pltpu.CompilerParams(**params), **kw)

    def run(*args):
        res = call(*args, *comm.inputs)
        return res[:n_out], res[n_out:]

    return run


def _sds(shape, dtype):
    return jax.ShapeDtypeStruct(tuple(shape), dtype)


def _resident(shape):
    zeros = (0,) * len(shape)
    return pl.BlockSpec(tuple(shape), lambda *_: zeros, pipeline_mode=pl.Buffered(1))


def _sigmoid(x):
    return jax.nn.sigmoid(x)


def _softplus(x):
    return jnp.maximum(x, 0.0) + jnp.log(1.0 + jnp.exp(-jnp.abs(x)))


_GELU_C = 0.7978845608028654
_GELU_A = 0.044715


def _gelu(x):
    t = jnp.tanh(_GELU_C * (x + _GELU_A * x * x * x))
    return 0.5 * x * (1.0 + t)


def _gelu_grad(x):
    t = jnp.tanh(_GELU_C * (x + _GELU_A * x * x * x))
    return 0.5 * (1.0 + t) + 0.5 * x * (1.0 - t * t) * _GELU_C * (1.0 + 3.0 * _GELU_A * x * x)


def _silu_grad(x):
    s = _sigmoid(x)
    return s * (1.0 + x * (1.0 - s))


def _rms_scale(xv):
    return lax.rsqrt(jnp.mean(xv * xv, axis=-1, keepdims=True) + EPS)


def _rms_bwd(dh, xv, g):
    r = _rms_scale(xv)
    xn = xv * r
    dg = jnp.sum(dh * xn, axis=0, keepdims=True)
    dxn = dh * g
    dx = r * (dxn - xn * jnp.mean(dxn * xn, axis=-1, keepdims=True))
    return dx, dg


def _iota2(shape, dim):
    return lax.broadcasted_iota(jnp.int32, shape, dim)


def _col_to_row(col):
    n = col.shape[0]
    eye = _iota2((n, n), 0) == _iota2((n, n), 1)
    return jnp.sum(jnp.where(eye, col, 0.0), axis=0, keepdims=True)


def _row_to_col(row):
    n = row.shape[1]
    eye = _iota2((n, n), 0) == _iota2((n, n), 1)
    return jnp.sum(jnp.where(eye, row, 0.0), axis=1, keepdims=True)


MXU_DIM = 256


def _hidden_chunks(f, step=3 * MXU_DIM):
    return [(c0, min(c0 + step, f)) for c0 in range(0, f, step)]

def ffn_fwd(x, gnorm, wg, wu, wd, name, comm=None):
    n, d = x.shape
    f = wg.shape[0]
    tm = min(ROW_TILE, n)
    fused = wd is not None

    def body(x_ref, g_ref, wg_ref, wu_ref, *rest):
        if fused:
            wd_ref, xo_ref, h_ref, gate_ref, up_ref, act_ref, acc_ref = rest
        else:
            h_ref, gate_ref, up_ref, act_ref = rest
        xv = x_ref[...]
        h = (xv * _rms_scale(xv) * g_ref[...]).astype(BF16)
        h_ref[...] = h
        for c0, c1 in _hidden_chunks(f):
            gate = _dot(h, wg_ref[c0:c1, :], NT)
            up = _dot(h, wu_ref[c0:c1, :], NT)
            act = (gate * _sigmoid(gate) * up).astype(BF16)
            gate_ref[:, c0:c1] = gate.astype(BF16)
            up_ref[:, c0:c1] = up.astype(BF16)
            act_ref[:, c0:c1] = act
            if fused:
                part = _dot(act, wd_ref[c0:c1, :], NN)
                if c0 == 0:
                    acc_ref[...] = part
                else:
                    acc_ref[...] += part
        if fused:
            xo_ref[...] = xv + 0.5 * acc_ref[...]

    row = pl.BlockSpec((tm, d), lambda i: (i, 0))
    wide = pl.BlockSpec((tm, f), lambda i: (i, 0))
    n_w = 3 if fused else 2
    return _call(
        body, name=name, grid=(n // tm,),
        in_specs=[row, pl.BlockSpec((1, d), lambda i: (0, 0))] + [_resident((f, d))] * n_w,
        out_specs=([row] if fused else []) + [row, wide, wide, wide],
        out_shape=([_sds((n, d), F32)] if fused else []) + [_sds((n, d), BF16)] + [_sds((n, f), BF16)] * 3,
        scratch=[pltpu.VMEM((tm, d), F32)] if fused else [], comm=comm,
    )(*([x, gnorm, wg, wu] + ([wd] if fused else [])))


def ffn_down(x, act, wd, name, comm=None):
    n, d = x.shape
    f = wd.shape[0]
    tm = min(ROW_TILE, n)

    def body(x_ref, a_ref, w_ref, o_ref):
        o_ref[...] = x_ref[...] + 0.5 * _dot(a_ref[...], w_ref[...], NN)

    row = pl.BlockSpec((tm, d), lambda i: (i, 0))
    return _call(
        body, name=name, grid=(n // tm,),
        in_specs=[row, pl.BlockSpec((tm, f), lambda i: (i, 0)), _resident((f, d))],
        out_specs=[row], out_shape=[_sds((n, d), F32)], comm=comm,
    )(x, act, wd)


def ffn_bwd_act(dy, x, gnorm, gate, up, wg, wu, wd, name, comm=None):
    n, d = x.shape
    f = wg.shape[0]
    tm = min(ROW_TILE // 2, n)

    def body(dy_ref, x_ref, g_ref, gate_ref, up_ref, wg_ref, wu_ref, wd_ref,
             dx_ref, dgate_ref, dup_ref, dyh_ref, dg_ref, acc_ref):
        @pl.when(pl.program_id(0) == 0)
        def _():
            dg_ref[...] = jnp.zeros_like(dg_ref)

        dyh = (0.5 * dy_ref[...]).astype(BF16)
        dyh_ref[...] = dyh
        for c0, c1 in _hidden_chunks(f):
            dact = _dot(dyh, wd_ref[c0:c1, :], NT)
            gt = gate_ref[:, c0:c1].astype(F32)
            u = up_ref[:, c0:c1].astype(F32)
            s = _sigmoid(gt)
            dup = (dact * (gt * s)).astype(BF16)
            dgate = (dact * u * (s * (1.0 + gt * (1.0 - s)))).astype(BF16)
            dup_ref[:, c0:c1] = dup
            dgate_ref[:, c0:c1] = dgate
            part = _dot(dgate, wg_ref[c0:c1, :], NN) + _dot(dup, wu_ref[c0:c1, :], NN)
            if c0 == 0:
                acc_ref[...] = part
            else:
                acc_ref[...] += part
        dxn, dg = _rms_bwd(acc_ref[...], x_ref[...], g_ref[...])
        dx_ref[...] = dy_ref[...] + dxn
        dg_ref[...] += dg

    row = pl.BlockSpec((tm, d), lambda i: (i, 0))
    wide = pl.BlockSpec((tm, f), lambda i: (i, 0))
    vec = pl.BlockSpec((1, d), lambda i: (0, 0))
    wres = _resident((f, d))
    return _call(
        body, name=name, grid=(n // tm,),
        in_specs=[row, row, vec, wide, wide, wres, wres, wres],
        out_specs=[row, wide, wide, row, vec],
        out_shape=[_sds((n, d), F32), _sds((n, f), BF16), _sds((n, f), BF16),
                   _sds((n, d), BF16), _sds((1, d), F32)],
        scratch=[pltpu.VMEM((tm, d), F32)], comm=comm,
    )(dy, x, gnorm, gate, up, wg, wu, wd)


def ffn_bwd_w(wide, rows, pairs, name, comm=None):
    n, d = rows[0].shape
    f = wide[0].shape[1]
    fh = f // 2
    tk = min(ROW_TILE, n)
    n_w, n_r = len(wide), len(rows)

    def body(*refs):
        wide_refs, row_refs, outs = refs[:n_w], refs[n_w:n_w + n_r], refs[n_w + n_r:]

        @pl.when(pl.program_id(1) == 0)
        def _():
            for o_ref in outs:
                o_ref[...] = jnp.zeros_like(o_ref)

        row_vals = [r[...] for r in row_refs]
        for c0, c1 in _hidden_chunks(fh, 2 * MXU_DIM):
            for (i, k), o_ref in zip(pairs, outs):
                o_ref[c0:c1, :] += _dot(wide_refs[i][:, c0:c1], row_vals[k], TN)

    row = pl.BlockSpec((tk, d), lambda j, k: (k, 0))
    blk = pl.BlockSpec((tk, fh), lambda j, k: (k, j))
    return _call(
        body, name=name, grid=(2, n // tk),
        in_specs=[blk] * n_w + [row] * n_r,
        out_specs=[pl.BlockSpec((fh, d), lambda j, k: (j, 0))] * len(pairs),
        out_shape=[_sds((f, d), F32)] * len(pairs), comm=comm,
    )(*wide, *rows)


def final_loss(x, gnorm, target, name):
    n, d = x.shape
    tm = min(ROW_TILE, n)

    def body(x_ref, g_ref, t_ref, dx_ref, dg_ref, loss_ref):
        @pl.when(pl.program_id(0) == 0)
        def _():
            dg_ref[...] = jnp.zeros_like(dg_ref)
            loss_ref[...] = jnp.zeros_like(loss_ref)

        xv = x_ref[...]
        y = xv * _rms_scale(xv) * g_ref[...]
        err = y - t_ref[...]
        part = 0.5 * jnp.sum(jnp.mean(err * err, axis=-1, keepdims=True), axis=0, keepdims=True)
        loss_ref[...] += jnp.broadcast_to(part, loss_ref.shape)
        dx, dg = _rms_bwd(err * (1.0 / d), xv, g_ref[...])
        dx_ref[...] = dx
        dg_ref[...] += dg

    row = pl.BlockSpec((tm, d), lambda i: (i, 0))
    vec = pl.BlockSpec((1, d), lambda i: (0, 0))
    return _call(
        body, name=name, grid=(n // tm,),
        in_specs=[row, vec, row],
        out_specs=[row, vec, pl.BlockSpec((1, LANES), lambda i: (0, 0))],
        out_shape=[_sds((n, d), F32), _sds((1, d), F32), _sds((1, LANES), F32)],
    )(x, gnorm, target)


def in_proj_fwd(x, gnorm, w, name):
    n, d = x.shape
    cols = w.shape[1]
    tm = min(ROW_TILE, n)

    def body(x_ref, g_ref, w_ref, p_ref, h_ref):
        xv = x_ref[...]
        h = (xv * _rms_scale(xv) * g_ref[...]).astype(BF16)
        h_ref[...] = h
        for c0, c1 in _hidden_chunks(cols):
            p_ref[:, c0:c1] = _dot(h, w_ref[:, c0:c1], NN)

    return _call(
        body, name=name, grid=(n // tm,),
        in_specs=[pl.BlockSpec((tm, d), lambda i: (i, 0)),
                  pl.BlockSpec((1, d), lambda i: (0, 0)), _resident((d, cols))],
        out_specs=[pl.BlockSpec((tm, cols), lambda i: (i, 0)),
                   pl.BlockSpec((tm, d), lambda i: (i, 0))],
        out_shape=[_sds((n, cols), F32), _sds((n, d), BF16)],
    )(x, gnorm, w)


def in_proj_bwd_x(dproj, w, x, gnorm, dres, name, comm=None):
    n, d = x.shape
    cols = w.shape[1]
    tm = min(ROW_TILE, n)

    def body(dp_ref, w_ref, x_ref, g_ref, dr_ref, dx_ref, dg_ref, dxh_ref):
        @pl.when(pl.program_id(0) == 0)
        def _():
            dg_ref[...] = jnp.zeros_like(dg_ref)

        dh = _dot(dp_ref[...], w_ref[...], NT)
        dxn, dg = _rms_bwd(dh, x_ref[...], g_ref[...])
        dx = dr_ref[...] + dxn
        dx_ref[...] = dx
        dxh_ref[...] = (0.5 * dx).astype(BF16)
        dg_ref[...] += dg

    row = pl.BlockSpec((tm, d), lambda i: (i, 0))
    vec = pl.BlockSpec((1, d), lambda i: (0, 0))
    return _call(
        body, name=name, grid=(n // tm,),
        in_specs=[pl.BlockSpec((tm, cols), lambda i: (i, 0)),
                  _resident((d, cols)), row, vec, row],
        out_specs=[row, vec, row],
        out_shape=[_sds((n, d), F32), _sds((1, d), F32), _sds((n, d), BF16)], comm=comm,
    )(dproj, w, x, gnorm, dres)


def matmul_tn(a_list, b, tn, name):
    n, cb = b.shape
    widths = [a.shape[1] for a in a_list]
    tk = min(ROW_TILE, n)

    def body(*refs):
        a_refs, b_ref, o_ref = refs[:-2], refs[-2], refs[-1]

        @pl.when(pl.program_id(0) == 0)
        def _():
            o_ref[...] = jnp.zeros_like(o_ref)

        r0 = 0
        for a_ref, ka in zip(a_refs, widths):
            av = a_ref[...]
            for c0, c1 in _hidden_chunks(cb, tn):
                o_ref[r0:r0 + ka, c0:c1] += _dot(av, b_ref[:, c0:c1], TN)
            r0 += ka

    return _call(
        body, name=name, grid=(n // tk,),
        in_specs=[pl.BlockSpec((tk, ka), lambda k: (k, 0)) for ka in widths]
        + [pl.BlockSpec((tk, cb), lambda k: (k, 0))],
        out_specs=pl.BlockSpec((sum(widths), cb), lambda k: (0, 0)),
        out_shape=_sds((sum(widths), cb), F32),
    )(*a_list, b)


def out_proj_fwd(x, sg_out, dn_out, w, name):
    n, d = x.shape
    tm = min(ROW_TILE, n)

    def body(x_ref, a_ref, b_ref, w_ref, o_ref):
        o_ref[...] = (x_ref[...] + _dot(a_ref[...], w_ref[0:HALF_W, :], NN)
                      + _dot(b_ref[...], w_ref[HALF_W:2 * HALF_W, :], NN))

    row = pl.BlockSpec((tm, d), lambda i: (i, 0))
    half = pl.BlockSpec((tm, HALF_W), lambda i: (i, 0))
    return _call(
        body, name=name, grid=(n // tm,),
        in_specs=[row, half, half, pl.BlockSpec((2 * HALF_W, d), lambda i: (0, 0))],
        out_specs=row, out_shape=_sds((n, d), F32),
    )(x, sg_out, dn_out, w)


def out_proj_bwd_x(dy, w, name, comm=None):
    n, d = dy.shape
    tm = min(ROW_TILE, n)

    def body(dy_ref, w_ref, dsg_ref, ddn_ref, dyb_ref):
        dyb = dy_ref[...].astype(BF16)
        dyb_ref[...] = dyb
        dsg_ref[...] = _dot(dyb, w_ref[0:HALF_W, :], NT)
        ddn_ref[...] = _dot(dyb, w_ref[HALF_W:2 * HALF_W, :], NT)

    row = pl.BlockSpec((tm, d), lambda i: (i, 0))
    half = pl.BlockSpec((tm, HALF_W), lambda i: (i, 0))
    return _call(
        body, name=name, grid=(n // tm,),
        in_specs=[row, pl.BlockSpec((2 * HALF_W, d), lambda i: (0, 0))],
        out_specs=[half, half, row],
        out_shape=[_sds((n, HALF_W), F32), _sds((n, HALF_W), F32), _sds((n, d), BF16)], comm=comm,
    )(dy, w)


SG_PAIRS = SG_GROUPS // 2


def _sg_low_half():
    return _iota2((SG_CHUNK, LANES), 1) < SG_GROUP_DIM


def _sg_pair_cols(p):
    return slice(p * LANES, (p + 1) * LANES)


def _sg_causal():
    return _iota2((SG_CHUNK, SG_CHUNK), 0) >= _iota2((SG_CHUNK, SG_CHUNK), 1)


def _sg_forward_chunk(pu, pv, ln_g, ln_b, wc, bias, low):
    u = _gelu(pu)
    v = _gelu(pv)
    mu = jnp.mean(v, axis=-1, keepdims=True)
    vc = v - mu
    rs = lax.rsqrt(jnp.mean(vc * vc, axis=-1, keepdims=True) + EPS)
    xhat = vc * rs
    vn = (xhat * ln_g + ln_b).astype(BF16)
    parts = []
    for p in range(SG_PAIRS):
        vn_p = vn[:, _sg_pair_cols(p)]
        parts.append(jnp.where(low, _dot(wc[2 * p], vn_p, NN), _dot(wc[2 * p + 1], vn_p, NN)))
    vs = bias + jnp.concatenate(parts, axis=1)
    return u, xhat, rs, vn, vs


def sg_fwd(proj, ln_g, ln_b, w_s, bias_tile, name):
    n = proj.shape[0]
    tm = min(ROW_TILE, n)

    def body(pu_ref, pv_ref, g_ref, b_ref, w_ref, bias_ref, o_ref):
        causal = _sg_causal()
        wc = [jnp.where(causal, w_ref[g], 0.0).astype(BF16) for g in range(SG_GROUPS)]
        masks = _sg_low_half()
        for ci in range(tm // SG_CHUNK):
            rows = slice(ci * SG_CHUNK, (ci + 1) * SG_CHUNK)
            u, _, _, _, vs = _sg_forward_chunk(pu_ref[rows, :], pv_ref[rows, :], g_ref[...],
                                               b_ref[...], wc, bias_ref[...], masks)
            o_ref[rows, :] = (u * vs).astype(BF16)

    vec = pl.BlockSpec((1, HALF_W), lambda i: (0, 0))
    return _call(
        body, name=name, grid=(n // tm,),
        in_specs=[pl.BlockSpec((tm, HALF_W), lambda i: (i, 0)),
                  pl.BlockSpec((tm, HALF_W), lambda i: (i, 1)), vec, vec,
                  pl.BlockSpec((SG_GROUPS, SG_CHUNK, SG_CHUNK), lambda i: (0, 0, 0)),
                  pl.BlockSpec((SG_CHUNK, HALF_W), lambda i: (0, 0))],
        out_specs=pl.BlockSpec((tm, HALF_W), lambda i: (i, 0)),
        out_shape=_sds((n, HALF_W), BF16),
    )(proj, proj, ln_g, ln_b, w_s, bias_tile)


def sg_bwd(dsg, proj, ln_g, ln_b, w_s, bias_tile, name):
    n = proj.shape[0]
    tm = min(ROW_TILE, n)

    def body(d_ref, pu_ref, pv_ref, g_ref, b_ref, w_ref, bias_ref,
             dp_ref, dw_ref, db_ref, dlg_ref, dlb_ref):
        @pl.when(pl.program_id(0) == 0)
        def _():
            dw_ref[...] = jnp.zeros_like(dw_ref)
            db_ref[...] = jnp.zeros_like(db_ref)
            dlg_ref[...] = jnp.zeros_like(dlg_ref)
            dlb_ref[...] = jnp.zeros_like(dlb_ref)

        causal = _sg_causal()
        wc = [jnp.where(causal, w_ref[g], 0.0).astype(BF16) for g in range(SG_GROUPS)]
        masks = _sg_low_half()
        ln_g_v = g_ref[...]
        for ci in range(tm // SG_CHUNK):
            rows = slice(ci * SG_CHUNK, (ci + 1) * SG_CHUNK)
            pu = pu_ref[rows, :]
            pv = pv_ref[rows, :]
            u, xhat, rs, vn, vs = _sg_forward_chunk(pu, pv, ln_g_v, b_ref[...], wc,
                                                    bias_ref[...], masks)
            dout = d_ref[rows, :]
            dp_ref[rows, 0:HALF_W] = (dout * vs * _gelu_grad(pu)).astype(BF16)
            dvs = dout * u
            dvs_b = dvs.astype(BF16)
            db_ref[...] += dvs
            dvn_parts = []
            for p in range(SG_PAIRS):
                dvs_p = dvs_b[:, _sg_pair_cols(p)]
                vn_p = vn[:, _sg_pair_cols(p)]
                dvn_parts.append(jnp.where(masks, _dot(wc[2 * p], dvs_p, TN), _dot(wc[2 * p + 1], dvs_p, TN)))
                zero = jnp.zeros_like(dvs_p)
                dw_ref[2 * p] += jnp.where(causal, _dot(jnp.where(masks, dvs_p, zero), vn_p, NT), 0.0)
                dw_ref[2 * p + 1] += jnp.where(causal, _dot(jnp.where(masks, zero, dvs_p), vn_p, NT), 0.0)
            dvn = jnp.concatenate(dvn_parts, axis=1)
            dlg_ref[...] += jnp.sum(dvn * xhat, axis=0, keepdims=True)
            dlb_ref[...] += jnp.sum(dvn, axis=0, keepdims=True)
            dxh = dvn * ln_g_v
            dv = rs * (dxh - jnp.mean(dxh, axis=-1, keepdims=True)
                       - xhat * jnp.mean(dxh * xhat, axis=-1, keepdims=True))
            dp_ref[rows, HALF_W:2 * HALF_W] = (dv * _gelu_grad(pv)).astype(BF16)

    vec = pl.BlockSpec((1, HALF_W), lambda i: (0, 0))
    wspec = pl.BlockSpec((SG_GROUPS, SG_CHUNK, SG_CHUNK), lambda i: (0, 0, 0))
    tile = pl.BlockSpec((SG_CHUNK, HALF_W), lambda i: (0, 0))
    return _call(
        body, name=name, grid=(n // tm,),
        in_specs=[pl.BlockSpec((tm, HALF_W), lambda i: (i, 0)),
                  pl.BlockSpec((tm, HALF_W), lambda i: (i, 0)),
                  pl.BlockSpec((tm, HALF_W), lambda i: (i, 1)), vec, vec, wspec, tile],
        out_specs=[pl.BlockSpec((tm, 2 * HALF_W), lambda i: (i, 0)), wspec, tile, vec, vec],
        out_shape=[_sds((n, PROJ_W), BF16), _sds((SG_GROUPS, SG_CHUNK, SG_CHUNK), F32),
                   _sds((SG_CHUNK, HALF_W), F32), _sds((1, HALF_W), F32), _sds((1, HALF_W), F32)],
    )(dsg, proj, proj, ln_g, ln_b, w_s, bias_tile)


CONV_K = 4
CONV_BLOCK = 256


def _shift_down(x, s, row):
    if s == 0:
        return x
    return jnp.where(row >= s, pltpu.roll(x, s, 0), 0.0)


def _shift_up(x, s, row):
    if s == 0:
        return x
    t_len = x.shape[0]
    return jnp.where(row < t_len - s, pltpu.roll(x, t_len - s, 0), 0.0)


def _conv_taps(x, row):
    return [_shift_down(x, CONV_K - 1 - j, row) for j in range(CONV_K)]


def _conv(taps, w):
    y = taps[0] * w[0:1, :]
    for j in range(1, CONV_K):
        y = y + taps[j] * w[j:j + 1, :]
    return y


def dn_conv_fwd(proj3, conv_w, name):
    b, t, _ = proj3.shape
    nblk = 3 * HALF_W // CONV_BLOCK
    first = 2 * HALF_W // CONV_BLOCK
    n_norm = 2 * HALF_W // CONV_BLOCK

    def body(x_ref, w_ref, o_ref):
        s = pl.program_id(1)
        x = x_ref[0]
        y = _conv(_conv_taps(x, _iota2(x.shape, 0)), w_ref[...])
        y = y * _sigmoid(y)

        @pl.when(s < n_norm)
        def _():
            for h in range(CONV_BLOCK // HEAD_DIM):
                cs = slice(h * HEAD_DIM, (h + 1) * HEAD_DIM)
                yh = y[:, cs]
                o_ref[0, :, cs] = yh * lax.rsqrt(jnp.sum(yh * yh, axis=-1, keepdims=True) + EPS)

        @pl.when(s >= n_norm)
        def _():
            o_ref[0] = y

    return _call(
        body, name=name, grid=(b, nblk),
        in_specs=[pl.BlockSpec((1, t, CONV_BLOCK), lambda i, s: (i, 0, first + s)),
                  pl.BlockSpec((CONV_K, CONV_BLOCK), lambda i, s: (0, s))],
        out_specs=pl.BlockSpec((1, t, CONV_BLOCK), lambda i, s: (i, 0, s)),
        out_shape=_sds((b, t, 3 * HALF_W), F32),
    )(proj3, conv_w)


def dn_conv_bwd(dqkv, proj3, conv_w, dproj3, name, comm=None):
    b, t, _ = proj3.shape
    nblk = 3 * HALF_W // CONV_BLOCK
    first = 2 * HALF_W // CONV_BLOCK
    n_norm = 2 * HALF_W // CONV_BLOCK

    def body(d_ref, x_ref, w_ref, dproj_in, dx_ref, dw_ref, ds_ref):
        s = pl.program_id(0)

        @pl.when(pl.program_id(1) == 0)
        def _():
            dw_ref[...] = jnp.zeros_like(dw_ref)

        x = x_ref[0]
        w = w_ref[...]
        row = _iota2(x.shape, 0)
        taps = _conv_taps(x, row)
        c = _conv(taps, w)
        sg = _sigmoid(c)
        y = c * sg

        @pl.when(s < n_norm)
        def _():
            for h in range(CONV_BLOCK // HEAD_DIM):
                cs = slice(h * HEAD_DIM, (h + 1) * HEAD_DIM)
                yh = y[:, cs]
                r = lax.rsqrt(jnp.sum(yh * yh, axis=-1, keepdims=True) + EPS)
                nh = yh * r
                dn = d_ref[0, :, cs]
                ds_ref[:, cs] = r * (dn - nh * jnp.sum(dn * nh, axis=-1, keepdims=True))

        @pl.when(s >= n_norm)
        def _():
            ds_ref[...] = d_ref[0]

        dc = ds_ref[...] * (sg * (1.0 + c * (1.0 - sg)))
        dx = _shift_up(dc, CONV_K - 1, row) * w[0:1, :]
        for j in range(1, CONV_K):
            dx = dx + _shift_up(dc, CONV_K - 1 - j, row) * w[j:j + 1, :]
        dx_ref[0] = dx.astype(BF16)
        for j in range(CONV_K):
            dw_ref[j:j + 1, :] += jnp.sum(dc * taps[j], axis=0, keepdims=True)

    return _call(
        body, name=name, grid=(nblk, b),
        in_specs=[pl.BlockSpec((1, t, CONV_BLOCK), lambda s, i: (i, 0, s)),
                  pl.BlockSpec((1, t, CONV_BLOCK), lambda s, i: (i, 0, first + s)),
                  pl.BlockSpec((CONV_K, CONV_BLOCK), lambda s, i: (0, s)), _ANY],
        out_specs=[pl.BlockSpec((1, t, CONV_BLOCK), lambda s, i: (i, 0, first + s)),
                   pl.BlockSpec((CONV_K, CONV_BLOCK), lambda s, i: (0, s))],
        out_shape=[_sds(dproj3.shape, BF16), _sds((CONV_K, 3 * HALF_W), F32)],
        scratch=[pltpu.VMEM((t, CONV_BLOCK), F32)],
        input_output_aliases={3: 0}, comm=comm,
    )(dqkv, proj3, conv_w, dproj3)


def _chunk_masks():
    ii = _iota2((DN_CHUNK, DN_CHUNK), 0)
    jj = _iota2((DN_CHUNK, DN_CHUNK), 1)
    return ii >= jj, ii > jj, ii == jj


LOCKSTEP_CHUNKS = 4


def _inv_unit_lower_many(l_mats, eye):
    eye_f = jnp.where(eye, 1.0, 0.0)
    ps = [-l for l in l_mats]
    ts = [eye_f + p for p in ps]
    pss = [_split(p) for p in ps]
    size = 2
    while size < DN_CHUNK:
        ps = [_dot3(s, s) for s in pss]
        pss = [_split(p) for p in ps]
        ts = [t + _dot3(_split(t), s) for t, s in zip(ts, pss)]
        size *= 2
    return ts


def _gates(pba, ea_row, dtb_row):
    beta = _sigmoid(pba)
    g = -ea_row * _softplus(pba + dtb_row)
    return beta, g


def _chunk_decay(gcol):
    incl, strict, eye = _chunk_masks()
    grow = jnp.sum(jnp.where(eye, gcol, 0.0), axis=0, keepdims=True)
    decay = jnp.where(incl, jnp.exp(jnp.where(incl, gcol - grow, 0.0)), 0.0)
    return decay, incl, strict, eye


def dn_chunk_fwd(qkv, proj3, alog_row, dtb_row, name, comm=None):
    b, t, _ = qkv.shape
    rblk = min(256, t)
    n_in = rblk // DN_CHUNK

    def body(q_ref, k_ref, v_ref, pba_ref, al_ref, dtb_ref,
             u_ref, w_ref, qd_ref, kd_ref, qk_ref, ti_ref, gc_ref):
        ea = jnp.exp(al_ref[...])
        tri = jnp.where(_chunk_masks()[0], 1.0, 0.0)

        _, strict, eye = _chunk_masks()

        def chunk_group(cg, carry):
            items = []
            for sub in range(LOCKSTEP_CHUNKS):
                rows = pl.ds(pl.multiple_of((cg * LOCKSTEP_CHUNKS + sub) * DN_CHUNK, DN_CHUNK), DN_CHUNK)
                beta_all, g_all = _gates(pba_ref[0, rows, :], ea, dtb_ref[...])
                gc = _dot_exact_lhs(tri, g_all)
                gc_ref[0, rows, :] = gc
                for h in range(N_HEADS):
                    items.append((rows, h, beta_all[:, h:h + 1], gc[:, N_HEADS + h:N_HEADS + h + 1]))
            ks, kbs, decays, egs = [], [], [], []
            for rows, h, beta, gcol in items:
                cs = slice(h * HEAD_DIM, (h + 1) * HEAD_DIM)
                k = k_ref[0, rows, cs]
                ks.append(k)
                kbs.append(k * beta)
                decays.append(_chunk_decay(gcol)[0])
                egs.append(jnp.exp(gcol))
            ms = [_bdot(kb, k, NT) for kb, k in zip(kbs, ks)]
            tinvs = _inv_unit_lower_many([jnp.where(strict, m * dc, 0.0) for m, dc in zip(ms, decays)], eye)
            tsps = [_split(t) for t in tinvs]
            for (rows, h, beta, gcol), tsp, tinv in zip(items, tsps, tinvs):
                cs = slice(h * HEAD_DIM, (h + 1) * HEAD_DIM)
                u_ref[0, rows, cs] = _dot3(tsp, _split(v_ref[0, rows, cs] * beta))
                ti_ref[0, h, rows, :] = tinv
            for (rows, h, beta, gcol), tsp, kb, eg in zip(items, tsps, kbs, egs):
                cs = slice(h * HEAD_DIM, (h + 1) * HEAD_DIM)
                w_ref[0, rows, cs] = _dot3(tsp, _split(kb * eg))
            for (rows, h, beta, gcol), k, dc, eg in zip(items, ks, decays, egs):
                cs = slice(h * HEAD_DIM, (h + 1) * HEAD_DIM)
                q = q_ref[0, rows, cs] * QK_SCALE
                qk_ref[0, h, rows, :] = _bdot(q, k, NT) * dc
                qd_ref[0, rows, cs] = q * eg
                kd_ref[0, rows, cs] = k * jnp.exp(gcol[DN_CHUNK - 1:DN_CHUNK, :] - gcol)
            return carry

        lax.fori_loop(0, n_in // LOCKSTEP_CHUNKS, chunk_group, 0)

    def seg(cblk):
        return pl.BlockSpec((1, rblk, HALF_W), lambda i, r: (i, r, cblk))

    vec = pl.BlockSpec((1, LANES), lambda i, r: (0, 0))
    wide = pl.BlockSpec((1, rblk, HALF_W), lambda i, r: (i, r, 0))
    sq = pl.BlockSpec((1, N_HEADS, rblk, DN_CHUNK), lambda i, r: (i, 0, r, 0))
    return _call(
        body, name=name, grid=(b, t // rblk),
        in_specs=[seg(0), seg(1), seg(2),
                  pl.BlockSpec((1, rblk, LANES), lambda i, r: (i, r, GATE_COL_BLOCK)), vec, vec],
        out_specs=[wide, wide, wide, wide, sq, sq,
                   pl.BlockSpec((1, rblk, LANES), lambda i, r: (i, r, 0))],
        out_shape=[_sds((b, t, HALF_W), F32)] * 4
        + [_sds((b, N_HEADS, t, DN_CHUNK), F32)] * 2 + [_sds((b, t, LANES), F32)], comm=comm,
    )(qkv, qkv, qkv, proj3, alog_row, dtb_row)


def dn_scan_fwd(u, w, qd, kd, qk, gc, name):
    b, t, _ = u.shape
    nc = t // DN_CHUNK
    bh = b * N_HEADS

    def body(u_ref, w_ref, qd_ref, kd_ref, qk_ref, gc_ref, o_ref, sin_ref, s_ref):
        @pl.when(pl.program_id(0) == 0)
        def _():
            s_ref[...] = jnp.zeros_like(s_ref)

        items = [(bi, h, slice(h * HEAD_DIM, (h + 1) * HEAD_DIM)) for bi in range(b) for h in range(N_HEADS)]
        sbs = []
        for bi, h, cs in items:
            s = s_ref[bi * N_HEADS + h]
            sin_ref[0, bi * N_HEADS + h] = s
            sbs.append(s.astype(BF16))
        ws = [_bdot(w_ref[bi, :, cs], sb, NN) for (bi, h, cs), sb in zip(items, sbs)]
        qs = [_bdot(qd_ref[bi, :, cs], sb, NN) for (bi, h, cs), sb in zip(items, sbs)]
        vbs = [(u_ref[bi, :, cs] - wsi).astype(BF16) for (bi, h, cs), wsi in zip(items, ws)]
        for (bi, h, cs), qsi, vb in zip(items, qs, vbs):
            o_ref[bi, :, cs] = qsi + _bdot(qk_ref[bi, h], vb, NN)
        for (bi, h, cs), vb in zip(items, vbs):
            gl = jnp.exp(gc_ref[bi, DN_CHUNK - 1:DN_CHUNK, N_HEADS + h:N_HEADS + h + 1])
            idx = bi * N_HEADS + h
            s_ref[idx] = s_ref[idx] * gl + _bdot(kd_ref[bi, :, cs], vb, TN)

    wide = pl.BlockSpec((b, DN_CHUNK, HALF_W), lambda c: (0, c, 0))
    return _call(
        body, name=name, grid=(nc,),
        in_specs=[wide, wide, wide, wide,
                  pl.BlockSpec((b, N_HEADS, DN_CHUNK, DN_CHUNK), lambda c: (0, 0, c, 0)),
                  pl.BlockSpec((b, DN_CHUNK, LANES), lambda c: (0, c, 0))],
        out_specs=[wide, pl.BlockSpec((1, bh, HEAD_DIM, HEAD_DIM), lambda c: (c, 0, 0, 0))],
        out_shape=[_sds((b, t, HALF_W), F32), _sds((nc, bh, HEAD_DIM, HEAD_DIM), F32)],
        scratch=[pltpu.VMEM((bh, HEAD_DIM, HEAD_DIM), F32)],
    )(u, w, qd, kd, qk, gc)


def dn_scan_bwd(do, u, w, qd, kd, qk, gc, s_in, name):
    b, t, _ = u.shape
    nc = t // DN_CHUNK
    bh = b * N_HEADS

    def body(do_ref, u_ref, w_ref, qd_ref, kd_ref, qk_ref, gc_ref, sin_ref,
             du_ref, dw_ref, dqd_ref, dkd_ref, dqk_ref, dgc_ref, ds_ref):
        @pl.when(pl.program_id(0) == 0)
        def _():
            ds_ref[...] = jnp.zeros_like(ds_ref)

        last_row = _iota2((DN_CHUNK, LANES), 0) == DN_CHUNK - 1
        lane = _iota2((DN_CHUNK, LANES), 1)
        items = [(bi, h, slice(h * HEAD_DIM, (h + 1) * HEAD_DIM)) for bi in range(b) for h in range(N_HEADS)]
        sbs = [sin_ref[0, bi * N_HEADS + h].astype(BF16) for bi, h, cs in items]
        wvs = [w_ref[bi, :, cs].astype(BF16) for bi, h, cs in items]
        dovs = [do_ref[bi, :, cs].astype(BF16) for bi, h, cs in items]
        dsbs = [ds_ref[bi * N_HEADS + h].astype(BF16) for bi, h, cs in items]
        vbs = [(u_ref[bi, :, cs] - _dot(wv, sb, NN)).astype(BF16)
               for (bi, h, cs), wv, sb in zip(items, wvs, sbs)]
        for (bi, h, cs), dov, sb in zip(items, dovs, sbs):
            dqd_ref[bi, :, cs] = _dot(dov, sb, NT)
        dvns = [_dot(kd_ref[bi, :, cs].astype(BF16), dsb, NN) + _dot(qk_ref[bi, h].astype(BF16), dov, TN)
                for (bi, h, cs), dsb, dov in zip(items, dsbs, dovs)]
        for (bi, h, cs), vb, dsb, dov in zip(items, vbs, dsbs, dovs):
            dkd_ref[bi, :, cs] = _dot(vb, dsb, NT)
            dqk_ref[bi, h] = _dot(dov, vb, NT)
        dgls = []
        for (bi, h, cs), dvn, sb, wv, dov in zip(items, dvns, sbs, wvs, dovs):
            idx = bi * N_HEADS + h
            du_ref[bi, :, cs] = dvn
            dvn_b = dvn.astype(BF16)
            dw_ref[bi, :, cs] = -_dot(dvn_b, sb, NT)
            gl = jnp.exp(gc_ref[bi, DN_CHUNK - 1:DN_CHUNK, N_HEADS + h:N_HEADS + h + 1])
            ds = ds_ref[idx]
            dgl = jnp.sum(jnp.sum(ds * sin_ref[0, idx], axis=1, keepdims=True), axis=0, keepdims=True)
            dgls.append(dgl * gl)
            ds_ref[idx] = (ds * gl + _dot(qd_ref[bi, :, cs].astype(BF16), dov, TN)
                           - _dot(wv, dvn_b, TN))
        for bi in range(b):
            dgc = jnp.zeros((DN_CHUNK, LANES), F32)
            for h in range(N_HEADS):
                dgc = dgc + jnp.where(jnp.logical_and(last_row, lane == N_HEADS + h),
                                      dgls[bi * N_HEADS + h], 0.0)
            dgc_ref[bi] = dgc

    def rev(c):
        return nc - 1 - c

    wide = pl.BlockSpec((b, DN_CHUNK, HALF_W), lambda c: (0, rev(c), 0))
    sq = pl.BlockSpec((b, N_HEADS, DN_CHUNK, DN_CHUNK), lambda c: (0, 0, rev(c), 0))
    gates = pl.BlockSpec((b, DN_CHUNK, LANES), lambda c: (0, rev(c), 0))
    return _call(
        body, name=name, grid=(nc,),
        in_specs=[wide, wide, wide, wide, wide, sq, gates,
                  pl.BlockSpec((1, bh, HEAD_DIM, HEAD_DIM), lambda c: (rev(c), 0, 0, 0))],
        out_specs=[wide, wide, wide, wide, sq, gates],
        out_shape=[_sds((b, t, HALF_W), F32)] * 4
        + [_sds((b, N_HEADS, t, DN_CHUNK), F32), _sds((b, t, LANES), F32)],
        scratch=[pltpu.VMEM((bh, HEAD_DIM, HEAD_DIM), F32)],
    )(do, u, w, qd, kd, qk, gc, s_in)


def dn_chunk_bwd(qkv, proj3, alog_row, dtb_row, tinv, u, w, du, dw, dqd, dkd, dqk, dgc_scan, dproj3, name,
                 comm=None):
    b, t, _ = qkv.shape
    rblk = min(256, t)
    n_in = rblk // DN_CHUNK

    def body(q_ref, k_ref, v_ref, pba_ref, al_ref, dtb_ref, ti_ref, u_ref, w_ref,
             du_ref, dw_ref, dqd_ref, dkd_ref, dqk_ref, dgs_ref, dproj_in,
             dq_ref, dpba_ref, dal_ref, ddtb_ref):
        @pl.when(jnp.logical_and(pl.program_id(0) == 0, pl.program_id(1) == 0))
        def _():
            dal_ref[...] = jnp.zeros_like(dal_ref)
            ddtb_ref[...] = jnp.zeros_like(ddtb_ref)

        ea = jnp.exp(al_ref[...])
        incl0 = _chunk_masks()[0]
        tri = jnp.where(incl0, 1.0, 0.0)
        tri_up = jnp.where(_iota2((DN_CHUNK, DN_CHUNK), 1) >= _iota2((DN_CHUNK, DN_CHUNK), 0), 1.0, 0.0)
        lane = _iota2((DN_CHUNK, LANES), 1)
        last_col = _iota2((DN_CHUNK, 1), 0) == DN_CHUNK - 1

        _, strict, _ = _chunk_masks()
        gate_lane = jnp.logical_and(lane >= N_HEADS, lane < 2 * N_HEADS)

        def chunk_group(cg, carry):
            tiles, items = [], []
            for sub in range(LOCKSTEP_CHUNKS):
                rows = pl.ds(pl.multiple_of((cg * LOCKSTEP_CHUNKS + sub) * DN_CHUNK, DN_CHUNK), DN_CHUNK)
                pba = pba_ref[0, rows, :]
                beta_all, g_all = _gates(pba, ea, dtb_ref[...])
                gc = _dot_exact_lhs(tri, g_all)
                tiles.append((rows, pba, beta_all, g_all))
                for h in range(N_HEADS):
                    items.append((sub, rows, h, slice(h * HEAD_DIM, (h + 1) * HEAD_DIM),
                                  beta_all[:, h:h + 1], gc[:, N_HEADS + h:N_HEADS + h + 1]))
            decays = [_chunk_decay(gcol)[0] for _, _, _, _, _, gcol in items]
            egs = [jnp.exp(gcol) for _, _, _, _, _, gcol in items]
            qbs = [(q_ref[0, rows, cs] * QK_SCALE).astype(BF16) for _, rows, h, cs, _, _ in items]
            kfs = [k_ref[0, rows, cs].astype(BF16) for _, rows, h, cs, _, _ in items]
            kbs = [k_ref[0, rows, cs] * beta for _, rows, h, cs, beta, _ in items]
            kbbs = [kb.astype(BF16) for kb in kbs]
            tsps = [_split(ti_ref[0, h, rows, :]) for _, rows, h, cs, _, _ in items]
            drus = [_dot3(tsp, _split(du_ref[0, rows, cs]), TN)
                    for (_, rows, h, cs, _, _), tsp in zip(items, tsps)]
            drws = [_dot3(tsp, _split(dw_ref[0, rows, cs]), TN)
                    for (_, rows, h, cs, _, _), tsp in zip(items, tsps)]
            m_kks = [_dot(kbb, kf, NT) for kbb, kf in zip(kbbs, kfs)]
            a_qks = [_dot(qb, kf, NT) for qb, kf in zip(qbs, kfs)]
            dls = [-jnp.where(strict, _dot3(_split(dru), _split(u_ref[0, rows, cs]), NT)
                              + _dot3(_split(drw), _split(w_ref[0, rows, cs]), NT), 0.0)
                   for (_, rows, h, cs, _, _), dru, drw in zip(items, drus, drws)]
            dms = [(dl * dc).astype(BF16) for dl, dc in zip(dls, decays)]
            das = [(dqk_ref[0, h, rows, :] * dc).astype(BF16)
                   for (_, rows, h, cs, _, _), dc in zip(items, decays)]
            dkb_mm = [_dot(dm, kf, NN) for dm, kf in zip(dms, kfs)]
            dk_mm = [_dot(dm, kbb, TN) + _dot(da, qb, TN) for dm, kbb, da, qb in zip(dms, kbbs, das, qbs)]
            dqs_mm = [_dot(da, kf, NN) for da, kf in zip(das, kfs)]
            dgc_tiles = [dgs_ref[0, rows, :] for rows, _, _, _ in tiles]
            dbeta_tiles = [jnp.zeros((DN_CHUNK, LANES), F32) for _ in tiles]
            for n_it, (sub, rows, h, cs, beta, gcol) in enumerate(items):
                eg, dc = egs[n_it], decays[n_it]
                k = k_ref[0, rows, cs]
                q = q_ref[0, rows, cs] * QK_SCALE
                kb, dru, drw = kbs[n_it], drus[n_it], drws[n_it]
                ek = jnp.exp(gcol[DN_CHUNK - 1:DN_CHUNK, :] - gcol)
                e_mat = (dls[n_it] * m_kks[n_it] + dqk_ref[0, h, rows, :] * a_qks[n_it]) * dc
                dkb = drw * eg + dkb_mm[n_it]
                dqd = dqd_ref[0, rows, cs]
                dkd = dkd_ref[0, rows, cs]
                kdk = dkd * k * ek
                kdk_total = jnp.sum(jnp.sum(kdk, axis=0, keepdims=True), axis=1, keepdims=True)
                dg = (jnp.sum(drw * kb * eg + dqd * q * eg - kdk, axis=-1, keepdims=True)
                      + jnp.sum(e_mat, axis=1, keepdims=True)
                      - _row_to_col(jnp.sum(e_mat, axis=0, keepdims=True))
                      + jnp.where(last_col, kdk_total, 0.0))
                dbeta = jnp.sum(dkb * k + dru * v_ref[0, rows, cs], axis=-1, keepdims=True)
                dq_ref[0, rows, cs] = (dqs_mm[n_it] + dqd * eg) * QK_SCALE
                dq_ref[0, rows, pl.ds(HALF_W + h * HEAD_DIM, HEAD_DIM)] = dk_mm[n_it] + dkd * ek + dkb * beta
                dq_ref[0, rows, pl.ds(2 * HALF_W + h * HEAD_DIM, HEAD_DIM)] = dru * beta
                dgc_tiles[sub] = dgc_tiles[sub] + jnp.where(lane == N_HEADS + h, dg, 0.0)
                dbeta_tiles[sub] = dbeta_tiles[sub] + jnp.where(lane == h, dbeta, 0.0)
            for (rows, pba, beta_all, g_all), dgc_tile, dbeta_tile in zip(tiles, dgc_tiles, dbeta_tiles):
                dg_tile = _dot_exact_lhs(tri_up, dgc_tile)
                da_pre = dg_tile * (-ea) * _sigmoid(pba + dtb_ref[...])
                dal_ref[...] += jnp.sum(jnp.where(gate_lane, dg_tile * g_all, 0.0), axis=0, keepdims=True)
                ddtb_ref[...] += jnp.sum(jnp.where(gate_lane, da_pre, 0.0), axis=0, keepdims=True)
                dpba_ref[0, rows, :] = jnp.where(lane < N_HEADS, dbeta_tile * beta_all * (1.0 - beta_all),
                                                 jnp.where(gate_lane, da_pre, 0.0)).astype(BF16)
            return carry

        lax.fori_loop(0, n_in // LOCKSTEP_CHUNKS, chunk_group, 0)

    def seg(cblk):
        return pl.BlockSpec((1, rblk, HALF_W), lambda i, r: (i, r, cblk))

    vec = pl.BlockSpec((1, LANES), lambda i, r: (0, 0))
    wide = pl.BlockSpec((1, rblk, HALF_W), lambda i, r: (i, r, 0))
    sq = pl.BlockSpec((1, N_HEADS, rblk, DN_CHUNK), lambda i, r: (i, 0, r, 0))
    gates = pl.BlockSpec((1, rblk, LANES), lambda i, r: (i, r, 0))
    return _call(
        body, name=name, grid=(b, t // rblk),
        in_specs=[seg(0), seg(1), seg(2),
                  pl.BlockSpec((1, rblk, LANES), lambda i, r: (i, r, GATE_COL_BLOCK)), vec, vec,
                  sq, wide, wide, wide, wide, wide, wide, sq, gates, _ANY],
        out_specs=[pl.BlockSpec((1, rblk, 3 * HALF_W), lambda i, r: (i, r, 0)),
                   pl.BlockSpec((1, rblk, LANES), lambda i, r: (i, r, GATE_COL_BLOCK)), vec, vec],
        out_shape=[_sds((b, t, 3 * HALF_W), F32), _sds(dproj3.shape, BF16),
                   _sds((1, LANES), F32), _sds((1, LANES), F32)],
        input_output_aliases={15: 1}, comm=comm,
    )(qkv, qkv, qkv, proj3, alog_row, dtb_row, tinv, u, w, du, dw, dqd, dkd, dqk, dgc_scan, dproj3)


def dn_out_fwd(o, proj, dn_norm, name):
    n = o.shape[0]
    tm = min(ROW_TILE, n)

    def body(o_ref, z_ref, g_ref, y_ref):
        for h in range(N_HEADS):
            cs = slice(h * HEAD_DIM, (h + 1) * HEAD_DIM)
            oh = o_ref[:, cs]
            z = z_ref[:, cs]
            y = oh * _rms_scale(oh) * g_ref[...]
            y_ref[:, cs] = (y * (z * _sigmoid(z))).astype(BF16)

    half = pl.BlockSpec((tm, HALF_W), lambda i: (i, 0))
    return _call(
        body, name=name, grid=(n // tm,),
        in_specs=[half, pl.BlockSpec((tm, HALF_W), lambda i: (i, 5)),
                  pl.BlockSpec((1, HEAD_DIM), lambda i: (0, 0))],
        out_specs=half, out_shape=_sds((n, HALF_W), BF16),
    )(o, proj, dn_norm)


def dn_out_bwd(dy, o, proj, dn_norm, dproj, name):
    n = o.shape[0]
    tm = min(ROW_TILE, n)

    def body(dy_ref, o_ref, z_ref, g_ref, dproj_in, do_ref, dz_ref, dg_ref):
        @pl.when(pl.program_id(0) == 0)
        def _():
            dg_ref[...] = jnp.zeros_like(dg_ref)

        g = g_ref[...]
        dg = jnp.zeros_like(g)
        for h in range(N_HEADS):
            cs = slice(h * HEAD_DIM, (h + 1) * HEAD_DIM)
            oh = o_ref[:, cs]
            z = z_ref[:, cs]
            d = dy_ref[:, cs]
            r = _rms_scale(oh)
            nh = oh * r
            sz = _sigmoid(z)
            dyn = d * (z * sz)
            dz_ref[:, cs] = (d * (nh * g) * (sz * (1.0 + z * (1.0 - sz)))).astype(BF16)
            dg = dg + jnp.sum(dyn * nh, axis=0, keepdims=True)
            dn = dyn * g
            do_ref[:, cs] = r * (dn - nh * jnp.mean(dn * nh, axis=-1, keepdims=True))
        dg_ref[...] += dg

    half = pl.BlockSpec((tm, HALF_W), lambda i: (i, 0))
    vec = pl.BlockSpec((1, HEAD_DIM), lambda i: (0, 0))
    return _call(
        body, name=name, grid=(n // tm,),
        in_specs=[half, half, pl.BlockSpec((tm, HALF_W), lambda i: (i, 5)), vec, _ANY],
        out_specs=[half, pl.BlockSpec((tm, HALF_W), lambda i: (i, 5)), vec],
        out_shape=[_sds((n, HALF_W), F32), _sds(dproj.shape, BF16), _sds((1, HEAD_DIM), F32)],
        input_output_aliases={4: 1},
    )(dy, o, proj, dn_norm, dproj)


def _adamw_math(w, g, m, v):
    m_new = ADAM_B1 * m + (1.0 - ADAM_B1) * g
    v_new = ADAM_B2 * v + (1.0 - ADAM_B2) * (g * g)
    m_hat = m_new / (1.0 - ADAM_B1 ** ADAM_STEP)
    v_hat = v_new / (1.0 - ADAM_B2 ** ADAM_STEP)
    delta = -ADAM_LR * (m_hat / (jnp.sqrt(v_hat) + ADAM_EPS) + ADAM_WD * w)
    return delta, m_new, v_new


def adamw(w, g, m, v, name):
    r, c = w.shape
    tr = r
    for cand in (256, 352):
        if r % cand == 0 and r > cand:
            tr = cand
            break

    def body(w_ref, g_ref, m_ref, v_ref, d_ref, mo_ref, vo_ref):
        d, mn, vn = _adamw_math(w_ref[...], g_ref[...], m_ref[...], v_ref[...])
        d_ref[...] = d
        mo_ref[...] = mn
        vo_ref[...] = vn

    spec = pl.BlockSpec((tr, c), lambda i: (i, 0))
    return _call(
        body, name=name, grid=(r // tr,),
        in_specs=[spec] * 4, out_specs=[spec] * 3, out_shape=[_sds((r, c), F32)] * 3,
    )(w, g, m, v)


def _place():
    return lax.axis_index("x"), lax.axis_index("y"), lax.axis_index("c")


def _other_chips(x, y):
    return [(1 - x, y), (x, 1 - y), (1 - x, 1 - y)]


_ANY = pl.BlockSpec(memory_space=pl.ANY)


def cast_place(w, shard_idx, name):
    r, cols = w.shape
    tr = r // 2

    def body(j_ref, w_ref, o_ref):
        o_ref[0] = w_ref[...].astype(BF16)

    return pl.pallas_call(
        body, name=name,
        grid_spec=pltpu.PrefetchScalarGridSpec(
            num_scalar_prefetch=1, grid=(r // tr,),
            in_specs=[pl.BlockSpec((tr, cols), lambda i, j: (i, 0))],
            out_specs=pl.BlockSpec((1, tr, cols), lambda i, j: (j[0], i, 0))),
        out_shape=_sds((N_SHARD, r, cols), BF16),
        compiler_params=pltpu.CompilerParams(dimension_semantics=("arbitrary",),
                                             vmem_limit_bytes=VMEM_LIMIT),
    )(shard_idx, w)


class Exchange:
    def __init__(self, inputs, out_shape, aliases, sems, phases):
        self.inputs, self.out_shape, self.aliases = list(inputs), list(out_shape), dict(aliases)
        self.sems, self.phases = list(sems), list(phases)


def run_exchange(ex, name):
    def body(*refs):
        n_in, n_out = len(ex.inputs), len(ex.out_shape)
        for _, fn in ex.phases:
            fn(refs[:n_in], refs[n_in:n_in + n_out], refs[n_in + n_out:])

    return _call(body, name=name, in_specs=[_ANY] * len(ex.inputs), out_specs=[_ANY] * len(ex.out_shape),
                 out_shape=ex.out_shape, scratch=ex.sems, input_output_aliases=ex.aliases)(*ex.inputs)


def merge_exchanges(exs):
    inputs, out_shape, sems, aliases, phases, out_slices = [], [], [], {}, [], []
    for ex in exs:
        i0, o0, s0 = len(inputs), len(out_shape), len(sems)
        inputs += ex.inputs
        out_shape += ex.out_shape
        sems += ex.sems
        for k, m in ex.aliases.items():
            aliases[i0 + k] = o0 + m
        si, so, ss = slice(i0, len(inputs)), slice(o0, len(out_shape)), slice(s0, len(sems))
        out_slices.append(so)
        for step, fn in ex.phases:
            phases.append((step, lambda ins, outs, sm, fn=fn, si=si, so=so, ss=ss: fn(ins[si], outs[so], sm[ss])))
    return Exchange(inputs, out_shape, aliases, sems, phases), out_slices


def _dma_sems(*sizes):
    return [pltpu.SemaphoreType.DMA((s,)) for s in sizes]


def gather_exchange(bufs, small=None, relay_step=-2):
    n = len(bufs)
    n_small = 0 if small is None else 1

    def half(outs, a, blk, hc):
        rh = bufs[a].shape[1] // 2
        return outs[a].at[blk, pl.ds(hc * rh, rh), :]

    def ici(outs, sems, a, k, blk, to):
        return pltpu.make_async_remote_copy(
            src_ref=half(outs, a, blk, to[2]), dst_ref=half(outs, a, blk, to[2]), send_sem=sems[0].at[3 * a + k],
            recv_sem=sems[1].at[3 * a + k], device_id=to, device_id_type=MESH)

    def d2d(outs, sems, a, k, blk, hc, to):
        return pltpu.make_async_remote_copy(
            src_ref=half(outs, a, blk, hc), dst_ref=half(outs, a, blk, hc), send_sem=sems[2].at[3 * a + k],
            recv_sem=sems[3].at[3 * a + k], device_id=to, device_id_type=MESH)

    def small_copy(ins, outs, sems, k, blk, to):
        return pltpu.make_async_remote_copy(
            src_ref=ins[n], dst_ref=outs[n].at[blk], send_sem=sems[0].at[3 * n + k],
            recv_sem=sems[1].at[3 * n + k], device_id=to, device_id_type=MESH)

    def start(ins, outs, sems):
        x, y, c = _place()
        j = 2 * x + y
        if n_small:
            pltpu.make_async_copy(ins[n], outs[n].at[j], sems[4].at[0]).start()
        for k, (px, py) in enumerate(_other_chips(x, y)):
            if n_small:
                small_copy(ins, outs, sems, k, j, (px, py, c)).start()
            for a in range(n):
                ici(outs, sems, a, k, j, (px, py, c)).start()

    def relay(ins, outs, sems):
        x, y, c = _place()
        for k, (px, py) in enumerate(_other_chips(x, y)):
            for a in range(n):
                ici(outs, sems, a, k, 2 * px + py, (px, py, c)).wait_recv()
                d2d(outs, sems, a, k, 2 * px + py, c, (x, y, 1 - c)).start()

    def finish(ins, outs, sems):
        x, y, c = _place()
        j = 2 * x + y
        for k, (px, py) in enumerate(_other_chips(x, y)):
            blk = 2 * px + py
            if n_small:
                small_copy(ins, outs, sems, k, blk, (px, py, c)).wait_recv()
                small_copy(ins, outs, sems, k, j, (px, py, c)).wait_send()
            for a in range(n):
                d2d(outs, sems, a, k, blk, 1 - c, (x, y, 1 - c)).wait_recv()
                ici(outs, sems, a, k, j, (px, py, c)).wait_send()
                d2d(outs, sems, a, k, blk, c, (x, y, 1 - c)).wait_send()
        if n_small:
            pltpu.make_async_copy(ins[n], outs[n].at[j], sems[4].at[0]).wait()

    out_shape = [_sds(b.shape, b.dtype) for b in bufs]
    if n_small:
        out_shape.append(_sds((N_SHARD,) + small.shape, small.dtype))
    return Exchange(list(bufs) + ([small] if n_small else []), out_shape, {a: a for a in range(n)},
                    _dma_sems(3 * n + 3, 3 * n + 3, 3 * n, 3 * n, 1),
                    [(0, start), (relay_step, relay), (-1, finish)])


def _start_then_wait(copies):
    def start(ins, outs, sems):
        for sent, _ in copies(ins, outs, sems):
            sent().start()

    def finish(ins, outs, sems):
        pairs = copies(ins, outs, sems)
        for _, arrival in pairs:
            arrival().wait_recv()
        for sent, _ in pairs:
            sent().wait_send()

    return [(0, start), (-1, finish)]


def pair_exchange(arrs):
    n = len(arrs)

    def copies(ins, outs, sems):
        x, y, c = _place()
        res = []
        for a in range(n):
            def mk(a=a):
                rh = arrs[a].shape[1] // 2
                return pltpu.make_async_remote_copy(
                    src_ref=ins[a].at[:, pl.ds((1 - c) * rh, rh), :], dst_ref=outs[a], send_sem=sems[0].at[a],
                    recv_sem=sems[1].at[a], device_id=(x, y, 1 - c), device_id_type=MESH)
            res.append((mk, mk))
        return res

    return Exchange(arrs, [_sds((a.shape[0], a.shape[1] // 2, a.shape[2]), a.dtype) for a in arrs], {},
                    _dma_sems(n, n), _start_then_wait(copies))


def pair_add(g, s, c_idx, name):
    nb, r, cols = g.shape
    rh = r // 2

    def body(c_ref, g_ref, s_ref, o_ref):
        o_ref[...] = (g_ref[...] + s_ref[...]).astype(BF16)

    return pl.pallas_call(
        body, name=name,
        grid_spec=pltpu.PrefetchScalarGridSpec(
            num_scalar_prefetch=1, grid=(nb,),
            in_specs=[pl.BlockSpec((1, rh, cols), lambda j, c: (j, c[0], 0)),
                      pl.BlockSpec((1, rh, cols), lambda j, c: (j, 0, 0))],
            out_specs=pl.BlockSpec((1, rh, cols), lambda j, c: (j, 0, 0))),
        out_shape=_sds((nb, rh, cols), BF16),
        compiler_params=pltpu.CompilerParams(dimension_semantics=("arbitrary",),
                                             vmem_limit_bytes=VMEM_LIMIT),
    )(c_idx, g, s)


def chip_exchange(arrs):
    n = len(arrs)

    def copies(ins, outs, sems):
        x, y, c = _place()
        j = 2 * x + y
        res = []
        for a in range(n):
            for k, (px, py) in enumerate(_other_chips(x, y)):
                def mk(src_blk, dst_blk, a=a, k=k, to=(px, py, c)):
                    return pltpu.make_async_remote_copy(
                        src_ref=ins[a].at[src_blk], dst_ref=outs[a].at[dst_blk], send_sem=sems[0].at[3 * a + k],
                        recv_sem=sems[1].at[3 * a + k], device_id=to, device_id_type=MESH)
                res.append((functools.partial(mk, 2 * px + py, j), functools.partial(mk, j, 2 * px + py)))
        return res

    return Exchange(arrs, [_sds(a.shape, a.dtype) for a in arrs], {}, _dma_sems(3 * n, 3 * n),
                    _start_then_wait(copies))


def sum_chips(r, p, shard_idx, name):
    nb, rh, cols = r.shape
    tr = rh

    def body(j_ref, p_ref, *refs):
        o_ref = refs[nb]
        j = j_ref[0]
        acc = None
        for i in range(nb):
            term = jnp.where(j == i, p_ref[0], refs[i][0]).astype(F32)
            acc = term if acc is None else acc + term
        o_ref[...] = acc

    def slot(i):
        return pl.BlockSpec((1, tr, cols), lambda t, j: (jnp.where(j[0] == i, (i + 1) % nb, i), t, 0))

    return pl.pallas_call(
        body, name=name,
        grid_spec=pltpu.PrefetchScalarGridSpec(
            num_scalar_prefetch=1, grid=(rh // tr,),
            in_specs=[pl.BlockSpec((1, tr, cols), lambda t, j: (j[0], t, 0))] + [slot(i) for i in range(nb)],
            out_specs=pl.BlockSpec((tr, cols), lambda t, j: (t, 0))),
        out_shape=_sds((rh, cols), F32),
        compiler_params=pltpu.CompilerParams(dimension_semantics=("arbitrary",),
                                             vmem_limit_bytes=VMEM_LIMIT),
    )(shard_idx, p, *([r] * nb))


def pair_swap(arrs):
    n = len(arrs)

    def copies(ins, outs, sems):
        x, y, c = _place()
        res = []
        for a in range(n):
            def mk(a=a):
                return pltpu.make_async_remote_copy(
                    src_ref=ins[a], dst_ref=outs[a], send_sem=sems[0].at[a], recv_sem=sems[1].at[a],
                    device_id=(x, y, 1 - c), device_id_type=MESH)
            res.append((mk, mk))
        return res

    return Exchange(arrs, [_sds(a.shape, a.dtype) for a in arrs], {}, _dma_sems(n, n),
                    _start_then_wait(copies))


ADAMW_STEPS_PER_HALF = 4


def adamw_pairs(items, name, comm=None):
    n_items = len(items)
    nh = ADAMW_STEPS_PER_HALF

    def body(*refs):
        ins, outs = refs[:5 * n_items], refs[5 * n_items:]
        mine = (pl.program_id(0) // nh) == lax.axis_index("c")
        for a in range(n_items):
            w_ref, gm_ref, gs_ref, m_ref, v_ref = ins[5 * a:5 * a + 5]
            g_ref, d_ref, mo_ref, vo_ref = outs[4 * a:4 * a + 4]
            g = jnp.where(mine, gm_ref[...], gs_ref[...])
            d, mn, vn = _adamw_math(w_ref[...], g, m_ref[...], v_ref[...])
            g_ref[...] = g
            d_ref[...] = d
            mo_ref[...] = mn
            vo_ref[...] = vn

    in_specs, out_specs, out_shape, args = [], [], [], []
    for w, g_mine, g_sib, m, v in items:
        r, cols = w.shape
        tr = r // (2 * nh)
        full = pl.BlockSpec((tr, cols), lambda i: (i, 0))
        part = pl.BlockSpec((tr, cols), lambda i: (i % nh, 0))
        in_specs += [full, part, part, full, full]
        out_specs += [full] * 4
        out_shape += [_sds((r, cols), F32)] * 4
        args += [w, g_mine, g_sib, m, v]
    res = _call(body, name=name, grid=(2 * nh,), in_specs=in_specs, out_specs=out_specs,
                out_shape=out_shape, comm=comm)(*args)
    own, hosted = (res, None) if comm is None else res
    grouped = [tuple(own[4 * a:4 * a + 4]) for a in range(n_items)]
    return grouped if comm is None else (grouped, hosted)


N_DEV = 8


def device_gather(pack):
    def copies(ins, outs, sems):
        x, y, c = _place()
        me = 4 * x + 2 * y + c
        res = []
        for k in range(1, N_DEV):
            fx, fy, fc = (k >> 2) & 1, (k >> 1) & 1, k & 1
            px, py, pc = (1 - x if fx else x, 1 - y if fy else y, 1 - c if fc else c)

            def mk(slot, k=k, to=(px, py, pc)):
                return pltpu.make_async_remote_copy(
                    src_ref=ins[0], dst_ref=outs[0].at[slot], send_sem=sems[0].at[k - 1],
                    recv_sem=sems[1].at[k - 1], device_id=to, device_id_type=MESH)
            res.append((functools.partial(mk, me), functools.partial(mk, 4 * px + 2 * py + pc)))
        return res

    return Exchange([pack], [_sds((N_DEV,) + pack.shape, pack.dtype)], {}, _dma_sems(N_DEV - 1, N_DEV - 1),
                    _start_then_wait(copies))


def sum_devices(buf, pack, me_idx, name):
    r, cols = pack.shape

    def body(me_ref, p_ref, *refs):
        o_ref = refs[N_DEV]
        acc = None
        for i in range(N_DEV):
            term = jnp.where(me_ref[0] == i, p_ref[...], refs[i][0])
            acc = term if acc is None else acc + term
        o_ref[...] = acc

    def slot(i):
        return pl.BlockSpec((1, r, cols), lambda t, me: (jnp.where(me[0] == i, (i + 1) % N_DEV, i), 0, 0))

    whole = pl.BlockSpec((r, cols), lambda t, me: (0, 0))
    return pl.pallas_call(
        body, name=name,
        grid_spec=pltpu.PrefetchScalarGridSpec(
            num_scalar_prefetch=1, grid=(1,),
            in_specs=[whole] + [slot(i) for i in range(N_DEV)], out_specs=whole),
        out_shape=_sds((r, cols), F32),
        compiler_params=pltpu.CompilerParams(dimension_semantics=("arbitrary",),
                                             vmem_limit_bytes=VMEM_LIMIT),
    )(me_idx, pack, *([buf] * N_DEV))


SMALL_NAMES = ("ffn1_norm", "mix_norm", "ffn2_norm", "final_norm", "sg_ln_g", "sg_ln_b",
               "dn_norm", "a_log", "dt_bias", "sg_b", "sg_w", "conv_w", "loss")


def _to_rows(a):
    flat = a.reshape(-1)
    pad = (-flat.shape[0]) % LANES
    if pad:
        flat = jnp.pad(flat, (0, pad))
    return flat.reshape(-1, LANES)


def _pack_small(parts):
    rows = [_to_rows(parts[k]) for k in SMALL_NAMES]
    pack = jnp.concatenate(rows, axis=0)
    pad = (-pack.shape[0]) % 8
    if pad:
        pack = jnp.pad(pack, ((0, pad), (0, 0)))
    return pack


def _unpack_small(pack, shapes):
    out, r0 = {}, 0
    for k in SMALL_NAMES:
        size = 1
        for s in shapes[k]:
            size *= s
        nrows = -(-size // LANES)
        out[k] = pack[r0:r0 + nrows].reshape(-1)[:size].reshape(shapes[k])
        r0 += nrows
    return out


def kernel(x, ffn1_norm, ffn1_w_gate, ffn1_w_up, ffn1_w_down, mix_norm, w_in, conv_w, a_log, dt_bias, dn_norm, sg_ln_g, sg_ln_b, sg_w, sg_b, w_out, ffn2_norm, ffn2_w_gate, ffn2_w_up, ffn2_w_down, final_norm, loss_target, m_ffn1_norm, m_ffn1_w_gate, m_ffn1_w_up, m_ffn1_w_down, m_mix_norm, m_w_in, m_conv_w, m_a_log, m_dt_bias, m_dn_norm, m_sg_ln_g, m_sg_ln_b, m_sg_w, m_sg_b, m_w_out, m_ffn2_norm, m_ffn2_w_gate, m_ffn2_w_up, m_ffn2_w_down, m_final_norm, v_ffn1_norm, v_ffn1_w_gate, v_ffn1_w_up, v_ffn1_w_down, v_mix_norm, v_w_in, v_conv_w, v_a_log, v_dt_bias, v_dn_norm, v_sg_ln_g, v_sg_ln_b, v_sg_w, v_sg_b, v_w_out, v_ffn2_norm, v_ffn2_w_gate, v_ffn2_w_up, v_ffn2_w_down, v_final_norm):
    bsz, t_len, d = x.shape
    n = bsz * t_len
    xy, yy, cc = _place()
    shard = 2 * xy + yy

    big_names = ["ffn1_w_gate", "ffn1_w_up", "ffn1_w_down", "w_in", "w_out",
                 "ffn2_w_gate", "ffn2_w_up", "ffn2_w_down"]
    big_w = dict(ffn1_w_gate=ffn1_w_gate, ffn1_w_up=ffn1_w_up, ffn1_w_down=ffn1_w_down, w_in=w_in,
                 w_out=w_out, ffn2_w_gate=ffn2_w_gate, ffn2_w_up=ffn2_w_up, ffn2_w_down=ffn2_w_down)
    big_m = dict(ffn1_w_gate=m_ffn1_w_gate, ffn1_w_up=m_ffn1_w_up, ffn1_w_down=m_ffn1_w_down, w_in=m_w_in,
                 w_out=m_w_out, ffn2_w_gate=m_ffn2_w_gate, ffn2_w_up=m_ffn2_w_up, ffn2_w_down=m_ffn2_w_down)
    big_v = dict(ffn1_w_gate=v_ffn1_w_gate, ffn1_w_up=v_ffn1_w_up, ffn1_w_down=v_ffn1_w_down, w_in=v_w_in,
                 w_out=v_w_out, ffn2_w_gate=v_ffn2_w_gate, ffn2_w_up=v_ffn2_w_up, ffn2_w_down=v_ffn2_w_down)
    shard_idx = jnp.reshape(shard, (1,)).astype(jnp.int32)
    c_idx = jnp.reshape(cc, (1,)).astype(jnp.int32)
    transposed = ("ffn1_w_gate", "ffn1_w_up", "ffn2_w_gate", "ffn2_w_up")

    def as2d(a, k):
        return a[0].T if k in transposed else a[0]

    def from2d(a, k):
        return a.T[None] if k in transposed else a[None]

    placed = {k: cast_place(as2d(big_w[k], k), shard_idx, name="cast_" + k) for k in big_names}
    first_names = ["ffn1_w_gate", "ffn1_w_up"]
    second_names = ["ffn1_w_down", "w_in"]
    third_names = ["w_out", "ffn2_w_gate", "ffn2_w_up", "ffn2_w_down"]
    res = run_exchange(gather_exchange([placed[k] for k in first_names], conv_w[0]), name="gather_first")
    gw = dict(zip(first_names, res[:2]))
    conv_full = res[2].transpose(1, 0, 2).reshape(CONV_K, 3 * HALF_W)

    x0 = x.reshape(n, d)
    def ffn_weights(prefix):
        return [gw[prefix + k].reshape(-1, d) for k in ("_w_gate", "_w_up", "_w_down")]

    def ffn_grad_blocks(grads):
        return [g.reshape(N_SHARD, -1, d) for g in grads]

    (h1, gate1, up1, act1), second = ffn_fwd(
        x0, ffn1_norm, gw["ffn1_w_gate"].reshape(-1, d), gw["ffn1_w_up"].reshape(-1, d), None,
        name="ffn1_fwd", comm=gather_exchange([placed[k] for k in second_names]))
    gw.update(zip(second_names, second))
    (x1,) = ffn_down(x0, act1, gw["ffn1_w_down"].reshape(-1, d), name="ffn1_down")
    w_in_full = gw["w_in"].transpose(1, 0, 2).reshape(d, IN_COLS)
    w_in_full = jnp.pad(w_in_full, ((0, 0), (0, PROJ_W - IN_COLS)))
    proj, h2 = in_proj_fwd(x1, mix_norm, w_in_full, name="in_proj_fwd")
    proj3 = proj.reshape(bsz, t_len, PROJ_W)
    bias_tile = jnp.repeat(sg_b[0].T, SG_GROUP_DIM, axis=1)
    sg_out = sg_fwd(proj, sg_ln_g, sg_ln_b, sg_w[0], bias_tile, name="sg_fwd")
    qkv = dn_conv_fwd(proj3, conv_full, name="dn_conv_fwd")
    alog_row = jnp.zeros((1, LANES), F32).at[0, N_HEADS:2 * N_HEADS].set(a_log[0])
    dtb_row = jnp.zeros((1, LANES), F32).at[0, N_HEADS:2 * N_HEADS].set(dt_bias[0])
    (u_wy, w_wy, q_dec, k_dec, qk, tinv, gc), third = dn_chunk_fwd(
        qkv, proj3, alog_row, dtb_row, name="dn_chunk_fwd",
        comm=gather_exchange([placed[k] for k in third_names]))
    gw.update(zip(third_names, third))
    w_out_full = gw["w_out"].reshape(2 * HALF_W, d)
    o, s_in = dn_scan_fwd(u_wy, w_wy, q_dec, k_dec, qk, gc, name="dn_scan_fwd")
    dn_out = dn_out_fwd(o.reshape(n, HALF_W), proj, dn_norm, name="dn_out_fwd")
    x2 = out_proj_fwd(x1, sg_out, dn_out, w_out_full, name="out_proj_fwd")
    x3, h3, gate2, up2, act2 = ffn_fwd(x2, ffn2_norm, *ffn_weights("ffn2"), name="ffn2_fwd")
    dx3, d_final_norm, loss_tile = final_loss(x3, final_norm.reshape(1, d),
                                              loss_target.reshape(n, d), name="final_loss")

    dx2, dgate2, dup2, dyh2, d_ffn2_norm = ffn_bwd_act(
        dx3, x2, ffn2_norm, gate2, up2, *ffn_weights("ffn2"), name="ffn2_bwd_act")
    g_big = {}
    g_big["ffn2_w_gate"], g_big["ffn2_w_up"], g_big["ffn2_w_down"] = ffn_grad_blocks(ffn_bwd_w(
        [dgate2, dup2, act2], [h3, dyh2], [(0, 0), (1, 0), (2, 1)], name="ffn2_bwd_w"))

    early = ["ffn2_w_gate", "ffn2_w_up", "ffn2_w_down"]
    (d_sg, d_dn, dx2b), early_sib = out_proj_bwd_x(dx2, w_out_full, name="out_proj_bwd_x",
                                                   comm=pair_exchange([g_big[k] for k in early]))
    early_sums = [pair_add(g_big[k], s, c_idx, name="grad_pair_add_" + k) for k, s in zip(early, early_sib)]
    g_w_out = matmul_tn([sg_out, dn_out], dx2b, d, name="w_out_grad")
    g_big["w_out"] = g_w_out.reshape(N_SHARD, (2 * HALF_W) // N_SHARD, d)

    d_proj, d_sg_w, d_bias_tile, d_ln_g, d_ln_b = sg_bwd(d_sg, proj, sg_ln_g, sg_ln_b, sg_w[0],
                                                         bias_tile, name="sg_bwd")
    d_o, d_proj, d_dn_norm = dn_out_bwd(d_dn, o.reshape(n, HALF_W), proj, dn_norm, d_proj,
                                        name="dn_out_bwd")
    du, dw, dqd, dkd, dqk, dgc_scan = dn_scan_bwd(d_o.reshape(bsz, t_len, HALF_W), u_wy, w_wy, q_dec,
                                                  k_dec, qk, gc, s_in, name="dn_scan_bwd")
    (d_qkv, d_proj3, d_alog_row, d_dtb_row), early_chips = dn_chunk_bwd(
        qkv, proj3, alog_row, dtb_row, tinv, u_wy, w_wy, du, dw, dqd, dkd, dqk, dgc_scan,
        d_proj.reshape(bsz, t_len, PROJ_W), name="dn_chunk_bwd", comm=chip_exchange(early_sums))
    early_halves = [sum_chips(r, p, shard_idx, name="grad_chip_sum_" + k)
                    for k, r, p in zip(early, early_chips, early_sums)]
    (d_proj3, d_conv), early_sib_halves = dn_conv_bwd(d_qkv, proj3, conv_full, d_proj3, name="dn_conv_bwd",
                                                      comm=pair_swap(early_halves))
    d_proj = d_proj3.reshape(n, PROJ_W)
    g_w_in = matmul_tn([h2], d_proj, 3 * MXU_DIM, name="w_in_grad")[:, :IN_COLS]
    g_big["w_in"] = g_w_in.reshape(d, N_SHARD, IN_COLS // N_SHARD).transpose(1, 0, 2)

    def reduce_start(names):
        return pair_exchange([g_big[k] for k in names])

    def reduce_pair_sums(names, from_sib):
        return [pair_add(g_big[k], s, c_idx, name="grad_pair_add_" + k) for k, s in zip(names, from_sib)]

    def reduce_chip_sums(names, from_chips, sums):
        return [sum_chips(r, p, shard_idx, name="grad_chip_sum_" + k)
                for k, r, p in zip(names, from_chips, sums)]

    mid = ["w_in", "w_out"]
    (dx1, d_mix_norm, dyh1), mid_sib = in_proj_bwd_x(d_proj, w_in_full, x1, mix_norm, dx2,
                                                     name="in_proj_bwd_x", comm=reduce_start(mid))
    mid_sums = reduce_pair_sums(mid, mid_sib)
    down = ["ffn1_w_down"]
    (g_down,), mid_chips = ffn_bwd_w([act1], [dyh1], [(0, 0)], name="ffn1_bwd_w_down",
                                     comm=chip_exchange(mid_sums))
    g_big["ffn1_w_down"] = g_down.reshape(N_SHARD, -1, d)
    mid_halves = reduce_chip_sums(mid, mid_chips, mid_sums)
    leg, legs = merge_exchanges([reduce_start(down), pair_swap(mid_halves)])
    leg_res = run_exchange(leg, name="grad_pair_exchange_down")
    down_sums = reduce_pair_sums(down, leg_res[legs[0]])
    mid_sib_halves = leg_res[legs[1]]

    dx0, dgate1, dup1, _, d_ffn1_norm = ffn_bwd_act(
        dx1, x0, ffn1_norm, gate1, up1, *ffn_weights("ffn1"), name="ffn1_bwd_act")
    grad_x = dx0.reshape(bsz, t_len, d)
    d_sg_b = d_bias_tile.reshape(SG_CHUNK, SG_GROUPS, SG_GROUP_DIM).sum(axis=-1).T
    small_g = dict(ffn1_norm=d_ffn1_norm, mix_norm=d_mix_norm, ffn2_norm=d_ffn2_norm,
                   final_norm=d_final_norm, sg_ln_g=d_ln_g, sg_ln_b=d_ln_b, dn_norm=d_dn_norm,
                   a_log=d_alog_row[:, N_HEADS:2 * N_HEADS], dt_bias=d_dtb_row[:, N_HEADS:2 * N_HEADS],
                   sg_b=d_sg_b, sg_w=d_sg_w, conv_w=d_conv, loss=loss_tile[:, :1])
    my_pack = _pack_small(small_g)
    hosted, parts = merge_exchanges([chip_exchange(down_sums), device_gather(my_pack)])
    late = ["ffn1_w_gate", "ffn1_w_up"]
    late_grads, hosted_res = ffn_bwd_w([dgate1, dup1], [h1], [(0, 0), (1, 0)], name="ffn1_bwd_w_gate_up",
                                       comm=hosted)
    g_big["ffn1_w_gate"], g_big["ffn1_w_up"] = ffn_grad_blocks(late_grads)
    down_halves = reduce_chip_sums(down, hosted_res[parts[0]], down_sums)
    (all_packs,) = hosted_res[parts[1]]

    leg, legs = merge_exchanges([reduce_start(late), pair_swap(down_halves)])
    leg_res = run_exchange(leg, name="grad_pair_exchange")
    pair_sums = reduce_pair_sums(late, leg_res[legs[0]])
    down_sib_halves = leg_res[legs[1]]

    def adam_items(names, mine, sib):
        return [(as2d(big_w[k], k), gm, gs, as2d(big_m[k], k), as2d(big_v[k], k))
                for k, gm, gs in zip(names, mine, sib)]

    outs = {}
    done = adamw_pairs(
        adam_items(early + mid + down, early_halves + mid_halves + down_halves,
                   list(early_sib_halves) + list(mid_sib_halves) + list(down_sib_halves)),
        name="adamw_early")
    from_chips = run_exchange(chip_exchange(pair_sums), name="grad_chip_exchange")
    halves = reduce_chip_sums(late, from_chips, pair_sums)
    sib_halves = run_exchange(pair_swap(halves), name="grad_pair_swap")
    done += adamw_pairs(adam_items(late, halves, sib_halves), name="adamw_late")
    for k, res in zip(early + mid + down + late, done):
        outs[k] = tuple(from2d(a, k) for a in res)

    small_w = dict(ffn1_norm=ffn1_norm, mix_norm=mix_norm, ffn2_norm=ffn2_norm, final_norm=final_norm,
                   sg_ln_g=sg_ln_g, sg_ln_b=sg_ln_b, dn_norm=dn_norm, a_log=a_log, dt_bias=dt_bias,
                   sg_b=sg_b, sg_w=sg_w)
    small_m = dict(ffn1_norm=m_ffn1_norm, mix_norm=m_mix_norm, ffn2_norm=m_ffn2_norm,
                   final_norm=m_final_norm, sg_ln_g=m_sg_ln_g, sg_ln_b=m_sg_ln_b, dn_norm=m_dn_norm,
                   a_log=m_a_log, dt_bias=m_dt_bias, sg_b=m_sg_b, sg_w=m_sg_w)
    small_v = dict(ffn1_norm=v_ffn1_norm, mix_norm=v_mix_norm, ffn2_norm=v_ffn2_norm,
                   final_norm=v_final_norm, sg_ln_g=v_sg_ln_g, sg_ln_b=v_sg_ln_b, dn_norm=v_dn_norm,
                   a_log=v_a_log, dt_bias=v_dt_bias, sg_b=v_sg_b, sg_w=v_sg_w)
    shapes = {k: small_w[k].shape for k in small_w}
    shapes["conv_w"] = (CONV_K, 3 * HALF_W)
    shapes["loss"] = (1, 1)
    me_idx = jnp.reshape(4 * xy + 2 * yy + cc, (1,)).astype(jnp.int32)
    g_pack = sum_devices(all_packs, my_pack, me_idx, name="small_sum")
    g_small = _unpack_small(g_pack, shapes)
    loss = g_small["loss"].reshape(())
    cw = 3 * HALF_W // N_SHARD
    g_conv = lax.dynamic_slice_in_dim(g_small["conv_w"], shard * cw, cw, axis=1)
    zero_conv = jnp.zeros((CONV_K, 3 * HALF_W), F32)

    def packed(src, conv):
        parts = dict(src)
        parts["conv_w"] = lax.dynamic_update_slice_in_dim(zero_conv, conv[0], shard * cw, axis=1)
        parts["loss"] = jnp.zeros((1, 1), F32)
        return _pack_small(parts)

    d_pack, m_pack, v_pack = adamw(packed(small_w, conv_w), g_pack, packed(small_m, m_conv_w),
                                   packed(small_v, v_conv_w), name="adamw_small")
    d_small = _unpack_small(d_pack, shapes)
    m_small = _unpack_small(m_pack, shapes)
    v_small = _unpack_small(v_pack, shapes)

    def conv_block(full_arr):
        return lax.dynamic_slice_in_dim(full_arr, shard * cw, cw, axis=1)[None]

    for k in small_w:
        outs[k] = (g_small[k].reshape(small_w[k].shape), d_small[k], m_small[k], v_small[k])
    outs["conv_w"] = (g_conv[None], conv_block(d_small["conv_w"]), conv_block(m_small["conv_w"]),
                      conv_block(v_small["conv_w"]))

    order = ["ffn1_norm", "ffn1_w_gate", "ffn1_w_up", "ffn1_w_down", "mix_norm", "w_in", "conv_w",
             "a_log", "dt_bias", "dn_norm", "sg_ln_g", "sg_ln_b", "sg_w", "sg_b", "w_out", "ffn2_norm",
             "ffn2_w_gate", "ffn2_w_up", "ffn2_w_down", "final_norm"]
    return (loss, grad_x, *[outs[k][0] for k in order], *[outs[k][1] for k in order],
            *[outs[k][2] for k in order], *[outs[k][3] for k in order])
```

```python
import functools

import jax
import jax.numpy as jnp
from jax import lax
from jax.experimental import pallas as pl
from jax.experimental.pallas import tpu as pltpu

F32 = jnp.float32
BF16 = jnp.bfloat16
EPS = 1e-6

D_MODEL = 1024
N_SHARD = 4
HEAD_DIM = 128
N_HEADS = 4
DN_CHUNK = 64
SG_CHUNK = 128
SG_GROUPS = 8
SG_GROUP_DIM = 64
HALF_W = 512
PROJ_W = 3200
IN_COLS = 3080
GATE_COL_BLOCK = 24
QK_SCALE = HEAD_DIM ** -0.5
LANES = 128

ADAM_LR = 0.001
ADAM_B1 = 0.9
ADAM_B2 = 0.999
ADAM_EPS = 1e-08
ADAM_WD = 0.01
ADAM_STEP = 10

VMEM_LIMIT = 56 * 1024 * 1024
ROW_TILE = 512

NN = ((1,), (0,))
NT = ((1,), (1,))
TN = ((0,), (0,))
MESH = pl.DeviceIdType.MESH


def _dot(a, b, dims):
    return lax.dot_general(a, b, (dims, ((), ())), preferred_element_type=F32)


def _bdot(a, b, dims):
    return _dot(a.astype(BF16), b.astype(BF16), dims)


def _split(a):
    hi = a.astype(BF16)
    lo = (a - hi.astype(F32)).astype(BF16)
    return hi, lo


def _dot3(a, b, dims=NN):
    return _dot(a[0], b[0], dims) + (_dot(a[0], b[1], dims) + _dot(a[1], b[0], dims))


def _dot_exact_lhs(a, b):
    ab = a.astype(BF16)
    b1 = b.astype(BF16)
    r1 = b - b1.astype(F32)
    b2 = r1.astype(BF16)
    b3 = (r1 - b2.astype(F32)).astype(BF16)
    return _dot(ab, b1, NN) + (_dot(ab, b2, NN) + _dot(ab, b3, NN))


def _call(body, *, name, out_shape, in_specs, out_specs, grid=(), scratch=(), comm=None, **kw):
    params = dict(vmem_limit_bytes=VMEM_LIMIT)
    if grid:
        params["dimension_semantics"] = ("arbitrary",) * len(grid)
    if comm is None:
        return pl.pallas_call(
            body, name=name, grid=grid, in_specs=in_specs, out_specs=out_specs,
            out_shape=out_shape, scratch_shapes=list(scratch),
            compiler_params=pltpu.CompilerParams(**params), **kw)

    n_in, n_out, n_sc = len(in_specs), len(out_specs), len(scratch)
    c_in, c_out = len(comm.inputs), len(comm.out_shape)
    steps = 1
    for g in grid:
        steps *= g

    def hosted(*refs):
        ins, cins = refs[:n_in], refs[n_in:n_in + c_in]
        o0 = n_in + c_in
        outs, couts = refs[o0:o0 + n_out], refs[o0 + n_out:o0 + n_out + c_out]
        s0 = o0 + n_out + c_out
        sc, csems = refs[s0:s0 + n_sc], refs[s0 + n_sc:]
        lin = 0
        for axis, g in enumerate(grid):
            lin = lin * g + pl.program_id(axis)

        def at(step, fn):
            @pl.when(lin == step % steps)
            def _():
                fn(cins, couts, csems)

        for step, fn in comm.phases:
            if step >= 0:
                at(step, fn)
        body(*ins, *outs, *sc)
        for step, fn in comm.phases:
            if step < 0:
                at(step, fn)

    aliases = dict(kw.pop("input_output_aliases", {}))
    for k, m in comm.aliases.items():
        aliases[n_in + k] = n_out + m
    call = pl.pallas_call(
        hosted, name=name, grid=grid, in_specs=list(in_specs) + [_ANY] * c_in,
        out_specs=list(out_specs) + [_ANY] * c_out, out_shape=list(out_shape) + comm.out_shape,
        scratch_shapes=list(scratch) + comm.sems, input_output_aliases=aliases,
        compiler_params=pltpu.CompilerParams(**params), **kw)

    def run(*args):
        res = call(*args, *comm.inputs)
        return res[:n_out], res[n_out:]

    return run


def _sds(shape, dtype):
    return jax.ShapeDtypeStruct(tuple(shape), dtype)


def _resident(shape):
    zeros = (0,) * len(shape)
    return pl.BlockSpec(tuple(shape), lambda *_: zeros, pipeline_mode=pl.Buffered(1))


def _sigmoid(x):
    return jax.nn.sigmoid(x)


def _softplus(x):
    return jnp.maximum(x, 0.0) + jnp.log(1.0 + jnp.exp(-jnp.abs(x)))


_GELU_C = 0.7978845608028654
_GELU_A = 0.044715


def _gelu(x):
    t = jnp.tanh(_GELU_C * (x + _GELU_A * x * x * x))
    return 0.5 * x * (1.0 + t)


def _gelu_grad(x):
    t = jnp.tanh(_GELU_C * (x + _GELU_A * x * x * x))
    return 0.5 * (1.0 + t) + 0.5 * x * (1.0 - t * t) * _GELU_C * (1.0 + 3.0 * _GELU_A * x * x)


def _silu_grad(x):
    s = _sigmoid(x)
    return s * (1.0 + x * (1.0 - s))


def _rms_scale(xv):
    return lax.rsqrt(jnp.mean(xv * xv, axis=-1, keepdims=True) + EPS)


def _rms_bwd(dh, xv, g):
    r = _rms_scale(xv)
    xn = xv * r
    dg = jnp.sum(dh * xn, axis=0, keepdims=True)
    dxn = dh * g
    dx = r * (dxn - xn * jnp.mean(dxn * xn, axis=-1, keepdims=True))
    return dx, dg


def _iota2(shape, dim):
    return lax.broadcasted_iota(jnp.int32, shape, dim)


def _col_to_row(col):
    n = col.shape[0]
    eye = _iota2((n, n), 0) == _iota2((n, n), 1)
    return jnp.sum(jnp.where(eye, col, 0.0), axis=0, keepdims=True)


def _row_to_col(row):
    n = row.shape[1]
    eye = _iota2((n, n), 0) == _iota2((n, n), 1)
    return jnp.sum(jnp.where(eye, row, 0.0), axis=1, keepdims=True)


MXU_DIM = 256


def _hidden_chunks(f, step=3 * MXU_DIM):
    return [(c0, min(c0 + step, f)) for c0 in range(0, f, step)]

def ffn_fwd(x, gnorm, wg, wu, wd, name, comm=None):
    n, d = x.shape
    f = wg.shape[0]
    tm = min(ROW_TILE, n)
    fused = wd is not None

    def body(x_ref, g_ref, wg_ref, wu_ref, *rest):
        if fused:
            wd_ref, xo_ref, h_ref, gate_ref, up_ref, act_ref, acc_ref = rest
        else:
            h_ref, gate_ref, up_ref, act_ref = rest
        xv = x_ref[...]
        h = (xv * _rms_scale(xv) * g_ref[...]).astype(BF16)
        h_ref[...] = h
        for c0, c1 in _hidden_chunks(f):
            gate = _dot(h, wg_ref[c0:c1, :], NT)
            up = _dot(h, wu_ref[c0:c1, :], NT)
            act = (gate * _sigmoid(gate) * up).astype(BF16)
            gate_ref[:, c0:c1] = gate.astype(BF16)
            up_ref[:, c0:c1] = up.astype(BF16)
            act_ref[:, c0:c1] = act
            if fused:
                part = _dot(act, wd_ref[c0:c1, :], NN)
                if c0 == 0:
                    acc_ref[...] = part
                else:
                    acc_ref[...] += part
        if fused:
            xo_ref[...] = xv + 0.5 * acc_ref[...]

    row = pl.BlockSpec((tm, d), lambda i: (i, 0))
    wide = pl.BlockSpec((tm, f), lambda i: (i, 0))
    n_w = 3 if fused else 2
    return _call(
        body, name=name, grid=(n // tm,),
        in_specs=[row, pl.BlockSpec((1, d), lambda i: (0, 0))] + [_resident((f, d))] * n_w,
        out_specs=([row] if fused else []) + [row, wide, wide, wide],
        out_shape=([_sds((n, d), F32)] if fused else []) + [_sds((n, d), BF16)] + [_sds((n, f), BF16)] * 3,
        scratch=[pltpu.VMEM((tm, d), F32)] if fused else [], comm=comm,
    )(*([x, gnorm, wg, wu] + ([wd] if fused else [])))


def ffn_down(x, act, wd, name, comm=None):
    n, d = x.shape
    f = wd.shape[0]
    tm = min(ROW_TILE, n)

    def body(x_ref, a_ref, w_ref, o_ref):
        o_ref[...] = x_ref[...] + 0.5 * _dot(a_ref[...], w_ref[...], NN)

    row = pl.BlockSpec((tm, d), lambda i: (i, 0))
    return _call(
        body, name=name, grid=(n // tm,),
        in_specs=[row, pl.BlockSpec((tm, f), lambda i: (i, 0)), _resident((f, d))],
        out_specs=[row], out_shape=[_sds((n, d), F32)], comm=comm,
    )(x, act, wd)


def ffn_bwd_act(dy, x, gnorm, gate, up, wg, wu, wd, name, comm=None):
    n, d = x.shape
    f = wg.shape[0]
    tm = min(ROW_TILE // 2, n)

    def body(dy_ref, x_ref, g_ref, gate_ref, up_ref, wg_ref, wu_ref, wd_ref,
             dx_ref, dgate_ref, dup_ref, dyh_ref, dg_ref, acc_ref):
        @pl.when(pl.program_id(0) == 0)
        def _():
            dg_ref[...] = jnp.zeros_like(dg_ref)

        dyh = (0.5 * dy_ref[...]).astype(BF16)
        dyh_ref[...] = dyh
        for c0, c1 in _hidden_chunks(f):
            dact = _dot(dyh, wd_ref[c0:c1, :], NT)
            gt = gate_ref[:, c0:c1].astype(F32)
            u = up_ref[:, c0:c1].astype(F32)
            s = _sigmoid(gt)
            dup = (dact * (gt * s)).astype(BF16)
            dgate = (dact * u * (s * (1.0 + gt * (1.0 - s)))).astype(BF16)
            dup_ref[:, c0:c1] = dup
            dgate_ref[:, c0:c1] = dgate
            part = _dot(dgate, wg_ref[c0:c1, :], NN) + _dot(dup, wu_ref[c0:c1, :], NN)
            if c0 == 0:
                acc_ref[...] = part
            else:
                acc_ref[...] += part
        dxn, dg = _rms_bwd(acc_ref[...], x_ref[...], g_ref[...])
        dx_ref[...] = dy_ref[...] + dxn
        dg_ref[...] += dg

    row = pl.BlockSpec((tm, d), lambda i: (i, 0))
    wide = pl.BlockSpec((tm, f), lambda i: (i, 0))
    vec = pl.BlockSpec((1, d), lambda i: (0, 0))
    wres = _resident((f, d))
    return _call(
        body, name=name, grid=(n // tm,),
        in_specs=[row, row, vec, wide, wide, wres, wres, wres],
        out_specs=[row, wide, wide, row, vec],
        out_shape=[_sds((n, d), F32), _sds((n, f), BF16), _sds((n, f), BF16),
                   _sds((n, d), BF16), _sds((1, d), F32)],
        scratch=[pltpu.VMEM((tm, d), F32)], comm=comm,
    )(dy, x, gnorm, gate, up, wg, wu, wd)


def ffn_bwd_w(wide, rows, pairs, name, comm=None):
    n, d = rows[0].shape
    f = wide[0].shape[1]
    fh = f // 2
    tk = min(ROW_TILE, n)
    n_w, n_r = len(wide), len(rows)

    def body(*refs):
        wide_refs, row_refs, outs = refs[:n_w], refs[n_w:n_w + n_r], refs[n_w + n_r:]

        @pl.when(pl.program_id(1) == 0)
        def _():
            for o_ref in outs:
                o_ref[...] = jnp.zeros_like(o_ref)

        row_vals = [r[...] for r in row_refs]
        for c0, c1 in _hidden_chunks(fh, 2 * MXU_DIM):
            for (i, k), o_ref in zip(pairs, outs):
                o_ref[c0:c1, :] += _dot(wide_refs[i][:, c0:c1], row_vals[k], TN)

    row = pl.BlockSpec((tk, d), lambda j, k: (k, 0))
    blk = pl.BlockSpec((tk, fh), lambda j, k: (k, j))
    return _call(
        body, name=name, grid=(2, n // tk),
        in_specs=[blk] * n_w + [row] * n_r,
        out_specs=[pl.BlockSpec((fh, d), lambda j, k: (j, 0))] * len(pairs),
        out_shape=[_sds((f, d), F32)] * len(pairs), comm=comm,
    )(*wide, *rows)


def final_loss(x, gnorm, target, name):
    n, d = x.shape
    tm = min(ROW_TILE, n)

    def body(x_ref, g_ref, t_ref, dx_ref, dg_ref, loss_ref):
        @pl.when(pl.program_id(0) == 0)
        def _():
            dg_ref[...] = jnp.zeros_like(dg_ref)
            loss_ref[...] = jnp.zeros_like(loss_ref)

        xv = x_ref[...]
        y = xv * _rms_scale(xv) * g_ref[...]
        err = y - t_ref[...]
        part = 0.5 * jnp.sum(jnp.mean(err * err, axis=-1, keepdims=True), axis=0, keepdims=True)
        loss_ref[...] += jnp.broadcast_to(part, loss_ref.shape)
        dx, dg = _rms_bwd(err * (1.0 / d), xv, g_ref[...])
        dx_ref[...] = dx
        dg_ref[...] += dg

    row = pl.BlockSpec((tm, d), lambda i: (i, 0))
    vec = pl.BlockSpec((1, d), lambda i: (0, 0))
    return _call(
        body, name=name, grid=(n // tm,),
        in_specs=[row, vec, row],
        out_specs=[row, vec, pl.BlockSpec((1, LANES), lambda i: (0, 0))],
        out_shape=[_sds((n, d), F32), _sds((1, d), F32), _sds((1, LANES), F32)],
    )(x, gnorm, target)


def in_proj_fwd(x, gnorm, w, name, comm=None):
    n, d = x.shape
    cols = w.shape[1]
    tm = min(ROW_TILE, n)

    def body(x_ref, g_ref, w_ref, p_ref, h_ref):
        xv = x_ref[...]
        h = (xv * _rms_scale(xv) * g_ref[...]).astype(BF16)
        h_ref[...] = h
        for c0, c1 in _hidden_chunks(cols):
            p_ref[:, c0:c1] = _dot(h, w_ref[:, c0:c1], NN)

    return _call(
        body, name=name, grid=(n // tm,),
        in_specs=[pl.BlockSpec((tm, d), lambda i: (i, 0)),
                  pl.BlockSpec((1, d), lambda i: (0, 0)), _resident((d, cols))],
        out_specs=[pl.BlockSpec((tm, cols), lambda i: (i, 0)),
                   pl.BlockSpec((tm, d), lambda i: (i, 0))],
        out_shape=[_sds((n, cols), F32), _sds((n, d), BF16)], comm=comm,
    )(x, gnorm, w)


def in_proj_bwd_x(dproj, w, x, gnorm, dres, name, comm=None):
    n, d = x.shape
    cols = w.shape[1]
    tm = min(ROW_TILE, n)

    def body(dp_ref, w_ref, x_ref, g_ref, dr_ref, dx_ref, dg_ref, dxh_ref):
        @pl.when(pl.program_id(0) == 0)
        def _():
            dg_ref[...] = jnp.zeros_like(dg_ref)

        dh = _dot(dp_ref[...], w_ref[...], NT)
        dxn, dg = _rms_bwd(dh, x_ref[...], g_ref[...])
        dx = dr_ref[...] + dxn
        dx_ref[...] = dx
        dxh_ref[...] = (0.5 * dx).astype(BF16)
        dg_ref[...] += dg

    row = pl.BlockSpec((tm, d), lambda i: (i, 0))
    vec = pl.BlockSpec((1, d), lambda i: (0, 0))
    return _call(
        body, name=name, grid=(n // tm,),
        in_specs=[pl.BlockSpec((tm, cols), lambda i: (i, 0)),
                  _resident((d, cols)), row, vec, row],
        out_specs=[row, vec, row],
        out_shape=[_sds((n, d), F32), _sds((1, d), F32), _sds((n, d), BF16)], comm=comm,
    )(dproj, w, x, gnorm, dres)


def matmul_tn(a_list, b, tn, name):
    n, cb = b.shape
    widths = [a.shape[1] for a in a_list]
    tk = min(ROW_TILE, n)

    def body(*refs):
        a_refs, b_ref, o_ref = refs[:-2], refs[-2], refs[-1]

        @pl.when(pl.program_id(0) == 0)
        def _():
            o_ref[...] = jnp.zeros_like(o_ref)

        r0 = 0
        for a_ref, ka in zip(a_refs, widths):
            av = a_ref[...]
            for c0, c1 in _hidden_chunks(cb, tn):
                o_ref[r0:r0 + ka, c0:c1] += _dot(av, b_ref[:, c0:c1], TN)
            r0 += ka

    return _call(
        body, name=name, grid=(n // tk,),
        in_specs=[pl.BlockSpec((tk, ka), lambda k: (k, 0)) for ka in widths]
        + [pl.BlockSpec((tk, cb), lambda k: (k, 0))],
        out_specs=pl.BlockSpec((sum(widths), cb), lambda k: (0, 0)),
        out_shape=_sds((sum(widths), cb), F32),
    )(*a_list, b)


def out_proj_fwd(x, sg_out, dn_out, w, name):
    n, d = x.shape
    tm = min(ROW_TILE, n)

    def body(x_ref, a_ref, b_ref, w_ref, o_ref):
        o_ref[...] = (x_ref[...] + _dot(a_ref[...], w_ref[0:HALF_W, :], NN)
                      + _dot(b_ref[...], w_ref[HALF_W:2 * HALF_W, :], NN))

    row = pl.BlockSpec((tm, d), lambda i: (i, 0))
    half = pl.BlockSpec((tm, HALF_W), lambda i: (i, 0))
    return _call(
        body, name=name, grid=(n // tm,),
        in_specs=[row, half, half, pl.BlockSpec((2 * HALF_W, d), lambda i: (0, 0))],
        out_specs=row, out_shape=_sds((n, d), F32),
    )(x, sg_out, dn_out, w)


def out_proj_bwd_x(dy, w, name, comm=None):
    n, d = dy.shape
    tm = min(ROW_TILE, n)

    def body(dy_ref, w_ref, dsg_ref, ddn_ref, dyb_ref):
        dyb = dy_ref[...].astype(BF16)
        dyb_ref[...] = dyb
        dsg_ref[...] = _dot(dyb, w_ref[0:HALF_W, :], NT)
        ddn_ref[...] = _dot(dyb, w_ref[HALF_W:2 * HALF_W, :], NT)

    row = pl.BlockSpec((tm, d), lambda i: (i, 0))
    half = pl.BlockSpec((tm, HALF_W), lambda i: (i, 0))
    return _call(
        body, name=name, grid=(n // tm,),
        in_specs=[row, pl.BlockSpec((2 * HALF_W, d), lambda i: (0, 0))],
        out_specs=[half, half, row],
        out_shape=[_sds((n, HALF_W), F32), _sds((n, HALF_W), F32), _sds((n, d), BF16)], comm=comm,
    )(dy, w)


SG_PAIRS = SG_GROUPS // 2


def _sg_low_half():
    return _iota2((SG_CHUNK, LANES), 1) < SG_GROUP_DIM


def _sg_pair_cols(p):
    return slice(p * LANES, (p + 1) * LANES)


def _sg_causal():
    return _iota2((SG_CHUNK, SG_CHUNK), 0) >= _iota2((SG_CHUNK, SG_CHUNK), 1)


def _sg_forward_chunk(pu, pv, ln_g, ln_b, wc, bias, low):
    u = _gelu(pu)
    v = _gelu(pv)
    mu = jnp.mean(v, axis=-1, keepdims=True)
    vc = v - mu
    rs = lax.rsqrt(jnp.mean(vc * vc, axis=-1, keepdims=True) + EPS)
    xhat = vc * rs
    vn = (xhat * ln_g + ln_b).astype(BF16)
    parts = []
    for p in range(SG_PAIRS):
        vn_p = vn[:, _sg_pair_cols(p)]
        parts.append(jnp.where(low, _dot(wc[2 * p], vn_p, NN), _dot(wc[2 * p + 1], vn_p, NN)))
    vs = bias + jnp.concatenate(parts, axis=1)
    return u, xhat, rs, vn, vs


def sg_fwd(proj, ln_g, ln_b, w_s, bias_tile, name):
    n = proj.shape[0]
    tm = min(ROW_TILE, n)

    def body(pu_ref, pv_ref, g_ref, b_ref, w_ref, bias_ref, o_ref):
        causal = _sg_causal()
        wc = [jnp.where(causal, w_ref[g], 0.0).astype(BF16) for g in range(SG_GROUPS)]
        masks = _sg_low_half()
        for ci in range(tm // SG_CHUNK):
            rows = slice(ci * SG_CHUNK, (ci + 1) * SG_CHUNK)
            u, _, _, _, vs = _sg_forward_chunk(pu_ref[rows, :], pv_ref[rows, :], g_ref[...],
                                               b_ref[...], wc, bias_ref[...], masks)
            o_ref[rows, :] = (u * vs).astype(BF16)

    vec = pl.BlockSpec((1, HALF_W), lambda i: (0, 0))
    return _call(
        body, name=name, grid=(n // tm,),
        in_specs=[pl.BlockSpec((tm, HALF_W), lambda i: (i, 0)),
                  pl.BlockSpec((tm, HALF_W), lambda i: (i, 1)), vec, vec,
                  pl.BlockSpec((SG_GROUPS, SG_CHUNK, SG_CHUNK), lambda i: (0, 0, 0)),
                  pl.BlockSpec((SG_CHUNK, HALF_W), lambda i: (0, 0))],
        out_specs=pl.BlockSpec((tm, HALF_W), lambda i: (i, 0)),
        out_shape=_sds((n, HALF_W), BF16),
    )(proj, proj, ln_g, ln_b, w_s, bias_tile)


def sg_bwd(dsg, proj, ln_g, ln_b, w_s, bias_tile, name):
    n = proj.shape[0]
    tm = min(ROW_TILE, n)

    def body(d_ref, pu_ref, pv_ref, g_ref, b_ref, w_ref, bias_ref,
             dp_ref, dw_ref, db_ref, dlg_ref, dlb_ref):
        @pl.when(pl.program_id(0) == 0)
        def _():
            dw_ref[...] = jnp.zeros_like(dw_ref)
            db_ref[...] = jnp.zeros_like(db_ref)
            dlg_ref[...] = jnp.zeros_like(dlg_ref)
            dlb_ref[...] = jnp.zeros_like(dlb_ref)

        causal = _sg_causal()
        wc = [jnp.where(causal, w_ref[g], 0.0).astype(BF16) for g in range(SG_GROUPS)]
        masks = _sg_low_half()
        ln_g_v = g_ref[...]
        for ci in range(tm // SG_CHUNK):
            rows = slice(ci * SG_CHUNK, (ci + 1) * SG_CHUNK)
            pu = pu_ref[rows, :]
            pv = pv_ref[rows, :]
            u, xhat, rs, vn, vs = _sg_forward_chunk(pu, pv, ln_g_v, b_ref[...], wc,
                                                    bias_ref[...], masks)
            dout = d_ref[rows, :]
            dp_ref[rows, 0:HALF_W] = (dout * vs * _gelu_grad(pu)).astype(BF16)
            dvs = dout * u
            dvs_b = dvs.astype(BF16)
            db_ref[...] += dvs
            dvn_parts = []
            for p in range(SG_PAIRS):
                dvs_p = dvs_b[:, _sg_pair_cols(p)]
                vn_p = vn[:, _sg_pair_cols(p)]
                dvn_parts.append(jnp.where(masks, _dot(wc[2 * p], dvs_p, TN), _dot(wc[2 * p + 1], dvs_p, TN)))
                zero = jnp.zeros_like(dvs_p)
                dw_ref[2 * p] += jnp.where(causal, _dot(jnp.where(masks, dvs_p, zero), vn_p, NT), 0.0)
                dw_ref[2 * p + 1] += jnp.where(causal, _dot(jnp.where(masks, zero, dvs_p), vn_p, NT), 0.0)
            dvn = jnp.concatenate(dvn_parts, axis=1)
            dlg_ref[...] += jnp.sum(dvn * xhat, axis=0, keepdims=True)
            dlb_ref[...] += jnp.sum(dvn, axis=0, keepdims=True)
            dxh = dvn * ln_g_v
            dv = rs * (dxh - jnp.mean(dxh, axis=-1, keepdims=True)
                       - xhat * jnp.mean(dxh * xhat, axis=-1, keepdims=True))
            dp_ref[rows, HALF_W:2 * HALF_W] = (dv * _gelu_grad(pv)).astype(BF16)

    vec = pl.BlockSpec((1, HALF_W), lambda i: (0, 0))
    wspec = pl.BlockSpec((SG_GROUPS, SG_CHUNK, SG_CHUNK), lambda i: (0, 0, 0))
    tile = pl.BlockSpec((SG_CHUNK, HALF_W), lambda i: (0, 0))
    return _call(
        body, name=name, grid=(n // tm,),
        in_specs=[pl.BlockSpec((tm, HALF_W), lambda i: (i, 0)),
                  pl.BlockSpec((tm, HALF_W), lambda i: (i, 0)),
                  pl.BlockSpec((tm, HALF_W), lambda i: (i, 1)), vec, vec, wspec, tile],
        out_specs=[pl.BlockSpec((tm, 2 * HALF_W), lambda i: (i, 0)), wspec, tile, vec, vec],
        out_shape=[_sds((n, PROJ_W), BF16), _sds((SG_GROUPS, SG_CHUNK, SG_CHUNK), F32),
                   _sds((SG_CHUNK, HALF_W), F32), _sds((1, HALF_W), F32), _sds((1, HALF_W), F32)],
    )(dsg, proj, proj, ln_g, ln_b, w_s, bias_tile)


CONV_K = 4
CONV_BLOCK = 256


def _shift_down(x, s, row):
    if s == 0:
        return x
    return jnp.where(row >= s, pltpu.roll(x, s, 0), 0.0)


def _shift_up(x, s, row):
    if s == 0:
        return x
    t_len = x.shape[0]
    return jnp.where(row < t_len - s, pltpu.roll(x, t_len - s, 0), 0.0)


def _conv_taps(x, row):
    return [_shift_down(x, CONV_K - 1 - j, row) for j in range(CONV_K)]


def _conv(taps, w):
    y = taps[0] * w[0:1, :]
    for j in range(1, CONV_K):
        y = y + taps[j] * w[j:j + 1, :]
    return y


def dn_conv_fwd(proj3, conv_w, name):
    b, t, _ = proj3.shape
    nblk = 3 * HALF_W // CONV_BLOCK
    first = 2 * HALF_W // CONV_BLOCK
    n_norm = 2 * HALF_W // CONV_BLOCK

    def body(x_ref, w_ref, o_ref):
        s = pl.program_id(1)
        x = x_ref[0]
        y = _conv(_conv_taps(x, _iota2(x.shape, 0)), w_ref[...])
        y = y * _sigmoid(y)

        @pl.when(s < n_norm)
        def _():
            for h in range(CONV_BLOCK // HEAD_DIM):
                cs = slice(h * HEAD_DIM, (h + 1) * HEAD_DIM)
                yh = y[:, cs]
                o_ref[0, :, cs] = yh * lax.rsqrt(jnp.sum(yh * yh, axis=-1, keepdims=True) + EPS)

        @pl.when(s >= n_norm)
        def _():
            o_ref[0] = y

    return _call(
        body, name=name, grid=(b, nblk),
        in_specs=[pl.BlockSpec((1, t, CONV_BLOCK), lambda i, s: (i, 0, first + s)),
                  pl.BlockSpec((CONV_K, CONV_BLOCK), lambda i, s: (0, s))],
        out_specs=pl.BlockSpec((1, t, CONV_BLOCK), lambda i, s: (i, 0, s)),
        out_shape=_sds((b, t, 3 * HALF_W), F32),
    )(proj3, conv_w)


def dn_conv_bwd(dqkv, proj3, conv_w, dproj3, name, comm=None):
    b, t, _ = proj3.shape
    nblk = 3 * HALF_W // CONV_BLOCK
    first = 2 * HALF_W // CONV_BLOCK
    n_norm = 2 * HALF_W // CONV_BLOCK

    def body(d_ref, x_ref, w_ref, dproj_in, dx_ref, dw_ref, ds_ref):
        s = pl.program_id(0)

        @pl.when(pl.program_id(1) == 0)
        def _():
            dw_ref[...] = jnp.zeros_like(dw_ref)

        x = x_ref[0]
        w = w_ref[...]
        row = _iota2(x.shape, 0)
        taps = _conv_taps(x, row)
        c = _conv(taps, w)
        sg = _sigmoid(c)
        y = c * sg

        @pl.when(s < n_norm)
        def _():
            for h in range(CONV_BLOCK // HEAD_DIM):
                cs = slice(h * HEAD_DIM, (h + 1) * HEAD_DIM)
                yh = y[:, cs]
                r = lax.rsqrt(jnp.sum(yh * yh, axis=-1, keepdims=True) + EPS)
                nh = yh * r
                dn = d_ref[0, :, cs]
                ds_ref[:, cs] = r * (dn - nh * jnp.sum(dn * nh, axis=-1, keepdims=True))

        @pl.when(s >= n_norm)
        def _():
            ds_ref[...] = d_ref[0]

        dc = ds_ref[...] * (sg * (1.0 + c * (1.0 - sg)))
        dx = _shift_up(dc, CONV_K - 1, row) * w[0:1, :]
        for j in range(1, CONV_K):
            dx = dx + _shift_up(dc, CONV_K - 1 - j, row) * w[j:j + 1, :]
        dx_ref[0] = dx.astype(BF16)
        for j in range(CONV_K):
            dw_ref[j:j + 1, :] += jnp.sum(dc * taps[j], axis=0, keepdims=True)

    return _call(
        body, name=name, grid=(nblk, b),
        in_specs=[pl.BlockSpec((1, t, CONV_BLOCK), lambda s, i: (i, 0, s)),
                  pl.BlockSpec((1, t, CONV_BLOCK), lambda s, i: (i, 0, first + s)),
                  pl.BlockSpec((CONV_K, CONV_BLOCK), lambda s, i: (0, s)), _ANY],
        out_specs=[pl.BlockSpec((1, t, CONV_BLOCK), lambda s, i: (i, 0, first + s)),
                   pl.BlockSpec((CONV_K, CONV_BLOCK), lambda s, i: (0, s))],
        out_shape=[_sds(dproj3.shape, BF16), _sds((CONV_K, 3 * HALF_W), F32)],
        scratch=[pltpu.VMEM((t, CONV_BLOCK), F32)],
        input_output_aliases={3: 0}, comm=comm,
    )(dqkv, proj3, conv_w, dproj3)


def _chunk_masks():
    ii = _iota2((DN_CHUNK, DN_CHUNK), 0)
    jj = _iota2((DN_CHUNK, DN_CHUNK), 1)
    return ii >= jj, ii > jj, ii == jj


LOCKSTEP_CHUNKS = 4


def _inv_unit_lower_many(l_mats, eye):
    eye_f = jnp.where(eye, 1.0, 0.0)
    ps = [-l for l in l_mats]
    ts = [eye_f + p for p in ps]
    pss = [_split(p) for p in ps]
    size = 2
    while size < DN_CHUNK:
        ps = [_dot3(s, s) for s in pss]
        pss = [_split(p) for p in ps]
        ts = [t + _dot3(_split(t), s) for t, s in zip(ts, pss)]
        size *= 2
    return ts


def _gates(pba, ea_row, dtb_row):
    beta = _sigmoid(pba)
    g = -ea_row * _softplus(pba + dtb_row)
    return beta, g


def _chunk_decay(gcol):
    incl, strict, eye = _chunk_masks()
    grow = jnp.sum(jnp.where(eye, gcol, 0.0), axis=0, keepdims=True)
    decay = jnp.where(incl, jnp.exp(jnp.where(incl, gcol - grow, 0.0)), 0.0)
    return decay, incl, strict, eye


def dn_chunk_fwd(qkv, proj3, alog_row, dtb_row, name, comm=None):
    b, t, _ = qkv.shape
    rblk = min(256, t)
    n_in = rblk // DN_CHUNK

    def body(q_ref, k_ref, v_ref, pba_ref, al_ref, dtb_ref,
             u_ref, w_ref, qd_ref, kd_ref, qk_ref, ti_ref, gc_ref):
        ea = jnp.exp(al_ref[...])
        tri = jnp.where(_chunk_masks()[0], 1.0, 0.0)

        _, strict, eye = _chunk_masks()

        def chunk_group(cg, carry):
            items = []
            for sub in range(LOCKSTEP_CHUNKS):
                rows = pl.ds(pl.multiple_of((cg * LOCKSTEP_CHUNKS + sub) * DN_CHUNK, DN_CHUNK), DN_CHUNK)
                beta_all, g_all = _gates(pba_ref[0, rows, :], ea, dtb_ref[...])
                gc = _dot_exact_lhs(tri, g_all)
                gc_ref[0, rows, :] = gc
                for h in range(N_HEADS):
                    items.append((rows, h, beta_all[:, h:h + 1], gc[:, N_HEADS + h:N_HEADS + h + 1]))
            ks, kbs, decays, egs = [], [], [], []
            for rows, h, beta, gcol in items:
                cs = slice(h * HEAD_DIM, (h + 1) * HEAD_DIM)
                k = k_ref[0, rows, cs]
                ks.append(k)
                kbs.append(k * beta)
                decays.append(_chunk_decay(gcol)[0])
                egs.append(jnp.exp(gcol))
            ms = [_bdot(kb, k, NT) for kb, k in zip(kbs, ks)]
            tinvs = _inv_unit_lower_many([jnp.where(strict, m * dc, 0.0) for m, dc in zip(ms, decays)], eye)
            tsps = [_split(t) for t in tinvs]
            for (rows, h, beta, gcol), tsp, tinv in zip(items, tsps, tinvs):
                cs = slice(h * HEAD_DIM, (h + 1) * HEAD_DIM)
                u_ref[0, rows, cs] = _dot3(tsp, _split(v_ref[0, rows, cs] * beta))
                ti_ref[0, h, rows, :] = tinv
            for (rows, h, beta, gcol), tsp, kb, eg in zip(items, tsps, kbs, egs):
                cs = slice(h * HEAD_DIM, (h + 1) * HEAD_DIM)
                w_ref[0, rows, cs] = _dot3(tsp, _split(kb * eg))
            for (rows, h, beta, gcol), k, dc, eg in zip(items, ks, decays, egs):
                cs = slice(h * HEAD_DIM, (h + 1) * HEAD_DIM)
                q = q_ref[0, rows, cs] * QK_SCALE
                qk_ref[0, h, rows, :] = _bdot(q, k, NT) * dc
                qd_ref[0, rows, cs] = q * eg
                kd_ref[0, rows, cs] = k * jnp.exp(gcol[DN_CHUNK - 1:DN_CHUNK, :] - gcol)
            return carry

        lax.fori_loop(0, n_in // LOCKSTEP_CHUNKS, chunk_group, 0)

    def seg(cblk):
        return pl.BlockSpec((1, rblk, HALF_W), lambda i, r: (i, r, cblk))

    vec = pl.BlockSpec((1, LANES), lambda i, r: (0, 0))
    wide = pl.BlockSpec((1, rblk, HALF_W), lambda i, r: (i, r, 0))
    sq = pl.BlockSpec((1, N_HEADS, rblk, DN_CHUNK), lambda i, r: (i, 0, r, 0))
    return _call(
        body, name=name, grid=(b, t // rblk),
        in_specs=[seg(0), seg(1), seg(2),
                  pl.BlockSpec((1, rblk, LANES), lambda i, r: (i, r, GATE_COL_BLOCK)), vec, vec],
        out_specs=[wide, wide, wide, wide, sq, sq,
                   pl.BlockSpec((1, rblk, LANES), lambda i, r: (i, r, 0))],
        out_shape=[_sds((b, t, HALF_W), F32)] * 4
        + [_sds((b, N_HEADS, t, DN_CHUNK), F32)] * 2 + [_sds((b, t, LANES), F32)], comm=comm,
    )(qkv, qkv, qkv, proj3, alog_row, dtb_row)


def dn_scan_fwd(u, w, qd, kd, qk, gc, name):
    b, t, _ = u.shape
    nc = t // DN_CHUNK
    bh = b * N_HEADS

    def body(u_ref, w_ref, qd_ref, kd_ref, qk_ref, gc_ref, o_ref, sin_ref, s_ref):
        @pl.when(pl.program_id(0) == 0)
        def _():
            s_ref[...] = jnp.zeros_like(s_ref)

        items = [(bi, h, slice(h * HEAD_DIM, (h + 1) * HEAD_DIM)) for bi in range(b) for h in range(N_HEADS)]
        sbs = []
        for bi, h, cs in items:
            s = s_ref[bi * N_HEADS + h]
            sin_ref[0, bi * N_HEADS + h] = s
            sbs.append(s.astype(BF16))
        ws = [_bdot(w_ref[bi, :, cs], sb, NN) for (bi, h, cs), sb in zip(items, sbs)]
        qs = [_bdot(qd_ref[bi, :, cs], sb, NN) for (bi, h, cs), sb in zip(items, sbs)]
        vbs = [(u_ref[bi, :, cs] - wsi).astype(BF16) for (bi, h, cs), wsi in zip(items, ws)]
        for (bi, h, cs), qsi, vb in zip(items, qs, vbs):
            o_ref[bi, :, cs] = qsi + _bdot(qk_ref[bi, h], vb, NN)
        for (bi, h, cs), vb in zip(items, vbs):
            gl = jnp.exp(gc_ref[bi, DN_CHUNK - 1:DN_CHUNK, N_HEADS + h:N_HEADS + h + 1])
            idx = bi * N_HEADS + h
            s_ref[idx] = s_ref[idx] * gl + _bdot(kd_ref[bi, :, cs], vb, TN)

    wide = pl.BlockSpec((b, DN_CHUNK, HALF_W), lambda c: (0, c, 0))
    return _call(
        body, name=name, grid=(nc,),
        in_specs=[wide, wide, wide, wide,
                  pl.BlockSpec((b, N_HEADS, DN_CHUNK, DN_CHUNK), lambda c: (0, 0, c, 0)),
                  pl.BlockSpec((b, DN_CHUNK, LANES), lambda c: (0, c, 0))],
        out_specs=[wide, pl.BlockSpec((1, bh, HEAD_DIM, HEAD_DIM), lambda c: (c, 0, 0, 0))],
        out_shape=[_sds((b, t, HALF_W), F32), _sds((nc, bh, HEAD_DIM, HEAD_DIM), F32)],
        scratch=[pltpu.VMEM((bh, HEAD_DIM, HEAD_DIM), F32)],
    )(u, w, qd, kd, qk, gc)


def dn_scan_bwd(do, u, w, qd, kd, qk, gc, s_in, name):
    b, t, _ = u.shape
    nc = t // DN_CHUNK
    bh = b * N_HEADS

    def body(do_ref, u_ref, w_ref, qd_ref, kd_ref, qk_ref, gc_ref, sin_ref,
             du_ref, dw_ref, dqd_ref, dkd_ref, dqk_ref, dgc_ref, ds_ref):
        @pl.when(pl.program_id(0) == 0)
        def _():
            ds_ref[...] = jnp.zeros_like(ds_ref)

        last_row = _iota2((DN_CHUNK, LANES), 0) == DN_CHUNK - 1
        lane = _iota2((DN_CHUNK, LANES), 1)
        items = [(bi, h, slice(h * HEAD_DIM, (h + 1) * HEAD_DIM)) for bi in range(b) for h in range(N_HEADS)]
        sbs = [sin_ref[0, bi * N_HEADS + h].astype(BF16) for bi, h, cs in items]
        wvs = [w_ref[bi, :, cs].astype(BF16) for bi, h, cs in items]
        dovs = [do_ref[bi, :, cs].astype(BF16) for bi, h, cs in items]
        dsbs = [ds_ref[bi * N_HEADS + h].astype(BF16) for bi, h, cs in items]
        vbs = [(u_ref[bi, :, cs] - _dot(wv, sb, NN)).astype(BF16)
               for (bi, h, cs), wv, sb in zip(items, wvs, sbs)]
        for (bi, h, cs), dov, sb in zip(items, dovs, sbs):
            dqd_ref[bi, :, cs] = _dot(dov, sb, NT)
        dvns = [_dot(kd_ref[bi, :, cs].astype(BF16), dsb, NN) + _dot(qk_ref[bi, h].astype(BF16), dov, TN)
                for (bi, h, cs), dsb, dov in zip(items, dsbs, dovs)]
        for (bi, h, cs), vb, dsb, dov in zip(items, vbs, dsbs, dovs):
            dkd_ref[bi, :, cs] = _dot(vb, dsb, NT)
            dqk_ref[bi, h] = _dot(dov, vb, NT)
        dgls = []
        for (bi, h, cs), dvn, sb, wv, dov in zip(items, dvns, sbs, wvs, dovs):
            idx = bi * N_HEADS + h
            du_ref[bi, :, cs] = dvn
            dvn_b = dvn.astype(BF16)
            dw_ref[bi, :, cs] = -_dot(dvn_b, sb, NT)
            gl = jnp.exp(gc_ref[bi, DN_CHUNK - 1:DN_CHUNK, N_HEADS + h:N_HEADS + h + 1])
            ds = ds_ref[idx]
            dgl = jnp.sum(jnp.sum(ds * sin_ref[0, idx], axis=1, keepdims=True), axis=0, keepdims=True)
            dgls.append(dgl * gl)
            ds_ref[idx] = (ds * gl + _dot(qd_ref[bi, :, cs].astype(BF16), dov, TN)
                           - _dot(wv, dvn_b, TN))
        for bi in range(b):
            dgc = jnp.zeros((DN_CHUNK, LANES), F32)
            for h in range(N_HEADS):
                dgc = dgc + jnp.where(jnp.logical_and(last_row, lane == N_HEADS + h),
                                      dgls[bi * N_HEADS + h], 0.0)
            dgc_ref[bi] = dgc

    def rev(c):
        return nc - 1 - c

    wide = pl.BlockSpec((b, DN_CHUNK, HALF_W), lambda c: (0, rev(c), 0))
    sq = pl.BlockSpec((b, N_HEADS, DN_CHUNK, DN_CHUNK), lambda c: (0, 0, rev(c), 0))
    gates = pl.BlockSpec((b, DN_CHUNK, LANES), lambda c: (0, rev(c), 0))
    return _call(
        body, name=name, grid=(nc,),
        in_specs=[wide, wide, wide, wide, wide, sq, gates,
                  pl.BlockSpec((1, bh, HEAD_DIM, HEAD_DIM), lambda c: (rev(c), 0, 0, 0))],
        out_specs=[wide, wide, wide, wide, sq, gates],
        out_shape=[_sds((b, t, HALF_W), F32)] * 4
        + [_sds((b, N_HEADS, t, DN_CHUNK), F32), _sds((b, t, LANES), F32)],
        scratch=[pltpu.VMEM((bh, HEAD_DIM, HEAD_DIM), F32)],
    )(do, u, w, qd, kd, qk, gc, s_in)


def dn_chunk_bwd(qkv, proj3, alog_row, dtb_row, tinv, u, w, du, dw, dqd, dkd, dqk, dgc_scan, dproj3, name,
                 comm=None):
    b, t, _ = qkv.shape
    rblk = min(256, t)
    n_in = rblk // DN_CHUNK

    def body(q_ref, k_ref, v_ref, pba_ref, al_ref, dtb_ref, ti_ref, u_ref, w_ref,
             du_ref, dw_ref, dqd_ref, dkd_ref, dqk_ref, dgs_ref, dproj_in,
             dq_ref, dpba_ref, dal_ref, ddtb_ref):
        @pl.when(jnp.logical_and(pl.program_id(0) == 0, pl.program_id(1) == 0))
        def _():
            dal_ref[...] = jnp.zeros_like(dal_ref)
            ddtb_ref[...] = jnp.zeros_like(ddtb_ref)

        ea = jnp.exp(al_ref[...])
        incl0 = _chunk_masks()[0]
        tri = jnp.where(incl0, 1.0, 0.0)
        tri_up = jnp.where(_iota2((DN_CHUNK, DN_CHUNK), 1) >= _iota2((DN_CHUNK, DN_CHUNK), 0), 1.0, 0.0)
        lane = _iota2((DN_CHUNK, LANES), 1)
        last_col = _iota2((DN_CHUNK, 1), 0) == DN_CHUNK - 1

        _, strict, _ = _chunk_masks()
        gate_lane = jnp.logical_and(lane >= N_HEADS, lane < 2 * N_HEADS)

        def chunk_group(cg, carry):
            tiles, items = [], []
            for sub in range(LOCKSTEP_CHUNKS):
                rows = pl.ds(pl.multiple_of((cg * LOCKSTEP_CHUNKS + sub) * DN_CHUNK, DN_CHUNK), DN_CHUNK)
                pba = pba_ref[0, rows, :]
                beta_all, g_all = _gates(pba, ea, dtb_ref[...])
                gc = _dot_exact_lhs(tri, g_all)
                tiles.append((rows, pba, beta_all, g_all))
                for h in range(N_HEADS):
                    items.append((sub, rows, h, slice(h * HEAD_DIM, (h + 1) * HEAD_DIM),
                                  beta_all[:, h:h + 1], gc[:, N_HEADS + h:N_HEADS + h + 1]))
            decays = [_chunk_decay(gcol)[0] for _, _, _, _, _, gcol in items]
            egs = [jnp.exp(gcol) for _, _, _, _, _, gcol in items]
            qbs = [(q_ref[0, rows, cs] * QK_SCALE).astype(BF16) for _, rows, h, cs, _, _ in items]
            kfs = [k_ref[0, rows, cs].astype(BF16) for _, rows, h, cs, _, _ in items]
            kbs = [k_ref[0, rows, cs] * beta for _, rows, h, cs, beta, _ in items]
            kbbs = [kb.astype(BF16) for kb in kbs]
            tsps = [_split(ti_ref[0, h, rows, :]) for _, rows, h, cs, _, _ in items]
            drus = [_dot3(tsp, _split(du_ref[0, rows, cs]), TN)
                    for (_, rows, h, cs, _, _), tsp in zip(items, tsps)]
            drws = [_dot3(tsp, _split(dw_ref[0, rows, cs]), TN)
                    for (_, rows, h, cs, _, _), tsp in zip(items, tsps)]
            m_kks = [_dot(kbb, kf, NT) for kbb, kf in zip(kbbs, kfs)]
            a_qks = [_dot(qb, kf, NT) for qb, kf in zip(qbs, kfs)]
            dls = [-jnp.where(strict, _dot3(_split(dru), _split(u_ref[0, rows, cs]), NT)
                              + _dot3(_split(drw), _split(w_ref[0, rows, cs]), NT), 0.0)
                   for (_, rows, h, cs, _, _), dru, drw in zip(items, drus, drws)]
            dms = [(dl * dc).astype(BF16) for dl, dc in zip(dls, decays)]
            das = [(dqk_ref[0, h, rows, :] * dc).astype(BF16)
                   for (_, rows, h, cs, _, _), dc in zip(items, decays)]
            dkb_mm = [_dot(dm, kf, NN) for dm, kf in zip(dms, kfs)]
            dk_mm = [_dot(dm, kbb, TN) + _dot(da, qb, TN) for dm, kbb, da, qb in zip(dms, kbbs, das, qbs)]
            dqs_mm = [_dot(da, kf, NN) for da, kf in zip(das, kfs)]
            dgc_tiles = [dgs_ref[0, rows, :] for rows, _, _, _ in tiles]
            dbeta_tiles = [jnp.zeros((DN_CHUNK, LANES), F32) for _ in tiles]
            for n_it, (sub, rows, h, cs, beta, gcol) in enumerate(items):
                eg, dc = egs[n_it], decays[n_it]
                k = k_ref[0, rows, cs]
                q = q_ref[0, rows, cs] * QK_SCALE
                kb, dru, drw = kbs[n_it], drus[n_it], drws[n_it]
                ek = jnp.exp(gcol[DN_CHUNK - 1:DN_CHUNK, :] - gcol)
                e_mat = (dls[n_it] * m_kks[n_it] + dqk_ref[0, h, rows, :] * a_qks[n_it]) * dc
                dkb = drw * eg + dkb_mm[n_it]
                dqd = dqd_ref[0, rows, cs]
                dkd = dkd_ref[0, rows, cs]
                kdk = dkd * k * ek
                kdk_total = jnp.sum(jnp.sum(kdk, axis=0, keepdims=True), axis=1, keepdims=True)
                dg = (jnp.sum(drw * kb * eg + dqd * q * eg - kdk, axis=-1, keepdims=True)
                      + jnp.sum(e_mat, axis=1, keepdims=True)
                      - _row_to_col(jnp.sum(e_mat, axis=0, keepdims=True))
                      + jnp.where(last_col, kdk_total, 0.0))
                dbeta = jnp.sum(dkb * k + dru * v_ref[0, rows, cs], axis=-1, keepdims=True)
                dq_ref[0, rows, cs] = (dqs_mm[n_it] + dqd * eg) * QK_SCALE
                dq_ref[0, rows, pl.ds(HALF_W + h * HEAD_DIM, HEAD_DIM)] = dk_mm[n_it] + dkd * ek + dkb * beta
                dq_ref[0, rows, pl.ds(2 * HALF_W + h * HEAD_DIM, HEAD_DIM)] = dru * beta
                dgc_tiles[sub] = dgc_tiles[sub] + jnp.where(lane == N_HEADS + h, dg, 0.0)
                dbeta_tiles[sub] = dbeta_tiles[sub] + jnp.where(lane == h, dbeta, 0.0)
            for (rows, pba, beta_all, g_all), dgc_tile, dbeta_tile in zip(tiles, dgc_tiles, dbeta_tiles):
                dg_tile = _dot_exact_lhs(tri_up, dgc_tile)
                da_pre = dg_tile * (-ea) * _sigmoid(pba + dtb_ref[...])
                dal_ref[...] += jnp.sum(jnp.where(gate_lane, dg_tile * g_all, 0.0), axis=0, keepdims=True)
                ddtb_ref[...] += jnp.sum(jnp.where(gate_lane, da_pre, 0.0), axis=0, keepdims=True)
                dpba_ref[0, rows, :] = jnp.where(lane < N_HEADS, dbeta_tile * beta_all * (1.0 - beta_all),
                                                 jnp.where(gate_lane, da_pre, 0.0)).astype(BF16)
            return carry

        lax.fori_loop(0, n_in // LOCKSTEP_CHUNKS, chunk_group, 0)

    def seg(cblk):
        return pl.BlockSpec((1, rblk, HALF_W), lambda i, r: (i, r, cblk))

    vec = pl.BlockSpec((1, LANES), lambda i, r: (0, 0))
    wide = pl.BlockSpec((1, rblk, HALF_W), lambda i, r: (i, r, 0))
    sq = pl.BlockSpec((1, N_HEADS, rblk, DN_CHUNK), lambda i, r: (i, 0, r, 0))
    gates = pl.BlockSpec((1, rblk, LANES), lambda i, r: (i, r, 0))
    return _call(
        body, name=name, grid=(b, t // rblk),
        in_specs=[seg(0), seg(1), seg(2),
                  pl.BlockSpec((1, rblk, LANES), lambda i, r: (i, r, GATE_COL_BLOCK)), vec, vec,
                  sq, wide, wide, wide, wide, wide, wide, sq, gates, _ANY],
        out_specs=[pl.BlockSpec((1, rblk, 3 * HALF_W), lambda i, r: (i, r, 0)),
                   pl.BlockSpec((1, rblk, LANES), lambda i, r: (i, r, GATE_COL_BLOCK)), vec, vec],
        out_shape=[_sds((b, t, 3 * HALF_W), F32), _sds(dproj3.shape, BF16),
                   _sds((1, LANES), F32), _sds((1, LANES), F32)],
        input_output_aliases={15: 1}, comm=comm,
    )(qkv, qkv, qkv, proj3, alog_row, dtb_row, tinv, u, w, du, dw, dqd, dkd, dqk, dgc_scan, dproj3)


def dn_out_fwd(o, proj, dn_norm, name):
    n = o.shape[0]
    tm = min(ROW_TILE, n)

    def body(o_ref, z_ref, g_ref, y_ref):
        for h in range(N_HEADS):
            cs = slice(h * HEAD_DIM, (h + 1) * HEAD_DIM)
            oh = o_ref[:, cs]
            z = z_ref[:, cs]
            y = oh * _rms_scale(oh) * g_ref[...]
            y_ref[:, cs] = (y * (z * _sigmoid(z))).astype(BF16)

    half = pl.BlockSpec((tm, HALF_W), lambda i: (i, 0))
    return _call(
        body, name=name, grid=(n // tm,),
        in_specs=[half, pl.BlockSpec((tm, HALF_W), lambda i: (i, 5)),
                  pl.BlockSpec((1, HEAD_DIM), lambda i: (0, 0))],
        out_specs=half, out_shape=_sds((n, HALF_W), BF16),
    )(o, proj, dn_norm)


def dn_out_bwd(dy, o, proj, dn_norm, dproj, name):
    n = o.shape[0]
    tm = min(ROW_TILE, n)

    def body(dy_ref, o_ref, z_ref, g_ref, dproj_in, do_ref, dz_ref, dg_ref):
        @pl.when(pl.program_id(0) == 0)
        def _():
            dg_ref[...] = jnp.zeros_like(dg_ref)

        g = g_ref[...]
        dg = jnp.zeros_like(g)
        for h in range(N_HEADS):
            cs = slice(h * HEAD_DIM, (h + 1) * HEAD_DIM)
            oh = o_ref[:, cs]
            z = z_ref[:, cs]
            d = dy_ref[:, cs]
            r = _rms_scale(oh)
            nh = oh * r
            sz = _sigmoid(z)
            dyn = d * (z * sz)
            dz_ref[:, cs] = (d * (nh * g) * (sz * (1.0 + z * (1.0 - sz)))).astype(BF16)
            dg = dg + jnp.sum(dyn * nh, axis=0, keepdims=True)
            dn = dyn * g
            do_ref[:, cs] = r * (dn - nh * jnp.mean(dn * nh, axis=-1, keepdims=True))
        dg_ref[...] += dg

    half = pl.BlockSpec((tm, HALF_W), lambda i: (i, 0))
    vec = pl.BlockSpec((1, HEAD_DIM), lambda i: (0, 0))
    return _call(
        body, name=name, grid=(n // tm,),
        in_specs=[half, half, pl.BlockSpec((tm, HALF_W), lambda i: (i, 5)), vec, _ANY],
        out_specs=[half, pl.BlockSpec((tm, HALF_W), lambda i: (i, 5)), vec],
        out_shape=[_sds((n, HALF_W), F32), _sds(dproj.shape, BF16), _sds((1, HEAD_DIM), F32)],
        input_output_aliases={4: 1},
    )(dy, o, proj, dn_norm, dproj)


def _adamw_math(w, g, m, v):
    m_new = ADAM_B1 * m + (1.0 - ADAM_B1) * g
    v_new = ADAM_B2 * v + (1.0 - ADAM_B2) * (g * g)
    m_hat = m_new / (1.0 - ADAM_B1 ** ADAM_STEP)
    v_hat = v_new / (1.0 - ADAM_B2 ** ADAM_STEP)
    delta = -ADAM_LR * (m_hat / (jnp.sqrt(v_hat) + ADAM_EPS) + ADAM_WD * w)
    return delta, m_new, v_new


def adamw(w, g, m, v, name):
    r, c = w.shape
    tr = r
    for cand in (256, 352):
        if r % cand == 0 and r > cand:
            tr = cand
            break

    def body(w_ref, g_ref, m_ref, v_ref, d_ref, mo_ref, vo_ref):
        d, mn, vn = _adamw_math(w_ref[...], g_ref[...], m_ref[...], v_ref[...])
        d_ref[...] = d
        mo_ref[...] = mn
        vo_ref[...] = vn

    spec = pl.BlockSpec((tr, c), lambda i: (i, 0))
    return _call(
        body, name=name, grid=(r // tr,),
        in_specs=[spec] * 4, out_specs=[spec] * 3, out_shape=[_sds((r, c), F32)] * 3,
    )(w, g, m, v)


def _place():
    return lax.axis_index("x"), lax.axis_index("y"), lax.axis_index("c")


def _other_chips(x, y):
    return [(1 - x, y), (x, 1 - y), (1 - x, 1 - y)]


_ANY = pl.BlockSpec(memory_space=pl.ANY)


def cast_place(w, shard_idx, name):
    r, cols = w.shape
    tr = r // 2

    def body(j_ref, w_ref, o_ref):
        o_ref[0] = w_ref[...].astype(BF16)

    return pl.pallas_call(
        body, name=name,
        grid_spec=pltpu.PrefetchScalarGridSpec(
            num_scalar_prefetch=1, grid=(r // tr,),
            in_specs=[pl.BlockSpec((tr, cols), lambda i, j: (i, 0))],
            out_specs=pl.BlockSpec((1, tr, cols), lambda i, j: (j[0], i, 0))),
        out_shape=_sds((N_SHARD, r, cols), BF16),
        compiler_params=pltpu.CompilerParams(dimension_semantics=("arbitrary",),
                                             vmem_limit_bytes=VMEM_LIMIT),
    )(shard_idx, w)


class Exchange:
    def __init__(self, inputs, out_shape, aliases, sems, phases):
        self.inputs, self.out_shape, self.aliases = list(inputs), list(out_shape), dict(aliases)
        self.sems, self.phases = list(sems), list(phases)


def run_exchange(ex, name):
    def body(*refs):
        n_in, n_out = len(ex.inputs), len(ex.out_shape)
        for _, fn in ex.phases:
            fn(refs[:n_in], refs[n_in:n_in + n_out], refs[n_in + n_out:])

    return _call(body, name=name, in_specs=[_ANY] * len(ex.inputs), out_specs=[_ANY] * len(ex.out_shape),
                 out_shape=ex.out_shape, scratch=ex.sems, input_output_aliases=ex.aliases)(*ex.inputs)


def merge_exchanges(exs):
    inputs, out_shape, sems, aliases, phases, out_slices = [], [], [], {}, [], []
    for ex in exs:
        i0, o0, s0 = len(inputs), len(out_shape), len(sems)
        inputs += ex.inputs
        out_shape += ex.out_shape
        sems += ex.sems
        for k, m in ex.aliases.items():
            aliases[i0 + k] = o0 + m
        si, so, ss = slice(i0, len(inputs)), slice(o0, len(out_shape)), slice(s0, len(sems))
        out_slices.append(so)
        for step, fn in ex.phases:
            phases.append((step, lambda ins, outs, sm, fn=fn, si=si, so=so, ss=ss: fn(ins[si], outs[so], sm[ss])))
    return Exchange(inputs, out_shape, aliases, sems, phases), out_slices


def _dma_sems(*sizes):
    return [pltpu.SemaphoreType.DMA((s,)) for s in sizes]


def gather_exchange(bufs, small=None, relay_step=-2):
    n = len(bufs)
    n_small = 0 if small is None else 1

    def half(outs, a, blk, hc):
        rh = bufs[a].shape[1] // 2
        return outs[a].at[blk, pl.ds(hc * rh, rh), :]

    def ici(outs, sems, a, k, blk, to):
        return pltpu.make_async_remote_copy(
            src_ref=half(outs, a, blk, to[2]), dst_ref=half(outs, a, blk, to[2]), send_sem=sems[0].at[3 * a + k],
            recv_sem=sems[1].at[3 * a + k], device_id=to, device_id_type=MESH)

    def d2d(outs, sems, a, k, blk, hc, to):
        return pltpu.make_async_remote_copy(
            src_ref=half(outs, a, blk, hc), dst_ref=half(outs, a, blk, hc), send_sem=sems[2].at[3 * a + k],
            recv_sem=sems[3].at[3 * a + k], device_id=to, device_id_type=MESH)

    def small_copy(ins, outs, sems, k, blk, to):
        return pltpu.make_async_remote_copy(
            src_ref=ins[n], dst_ref=outs[n].at[blk], send_sem=sems[0].at[3 * n + k],
            recv_sem=sems[1].at[3 * n + k], device_id=to, device_id_type=MESH)

    def start(ins, outs, sems):
        x, y, c = _place()
        j = 2 * x + y
        if n_small:
            pltpu.make_async_copy(ins[n], outs[n].at[j], sems[4].at[0]).start()
        for k, (px, py) in enumerate(_other_chips(x, y)):
            if n_small:
                small_copy(ins, outs, sems, k, j, (px, py, c)).start()
            for a in range(n):
                ici(outs, sems, a, k, j, (px, py, c)).start()

    def relay(ins, outs, sems):
        x, y, c = _place()
        for k, (px, py) in enumerate(_other_chips(x, y)):
            for a in range(n):
                ici(outs, sems, a, k, 2 * px + py, (px, py, c)).wait_recv()
                d2d(outs, sems, a, k, 2 * px + py, c, (x, y, 1 - c)).start()

    def finish(ins, outs, sems):
        x, y, c = _place()
        j = 2 * x + y
        for k, (px, py) in enumerate(_other_chips(x, y)):
            blk = 2 * px + py
            if n_small:
                small_copy(ins, outs, sems, k, blk, (px, py, c)).wait_recv()
                small_copy(ins, outs, sems, k, j, (px, py, c)).wait_send()
            for a in range(n):
                d2d(outs, sems, a, k, blk, 1 - c, (x, y, 1 - c)).wait_recv()
                ici(outs, sems, a, k, j, (px, py, c)).wait_send()
                d2d(outs, sems, a, k, blk, c, (x, y, 1 - c)).wait_send()
        if n_small:
            pltpu.make_async_copy(ins[n], outs[n].at[j], sems[4].at[0]).wait()

    out_shape = [_sds(b.shape, b.dtype) for b in bufs]
    if n_small:
        out_shape.append(_sds((N_SHARD,) + small.shape, small.dtype))
    return Exchange(list(bufs) + ([small] if n_small else []), out_shape, {a: a for a in range(n)},
                    _dma_sems(3 * n + 3, 3 * n + 3, 3 * n, 3 * n, 1),
                    [(0, start), (relay_step, relay), (-1, finish)])


def _start_then_wait(copies):
    def start(ins, outs, sems):
        for sent, _ in copies(ins, outs, sems):
            sent().start()

    def finish(ins, outs, sems):
        pairs = copies(ins, outs, sems)
        for _, arrival in pairs:
            arrival().wait_recv()
        for sent, _ in pairs:
            sent().wait_send()

    return [(0, start), (-1, finish)]


def pair_exchange(arrs):
    n = len(arrs)

    def copies(ins, outs, sems):
        x, y, c = _place()
        res = []
        for a in range(n):
            def mk(a=a):
                rh = arrs[a].shape[1] // 2
                return pltpu.make_async_remote_copy(
                    src_ref=ins[a].at[:, pl.ds((1 - c) * rh, rh), :], dst_ref=outs[a], send_sem=sems[0].at[a],
                    recv_sem=sems[1].at[a], device_id=(x, y, 1 - c), device_id_type=MESH)
            res.append((mk, mk))
        return res

    return Exchange(arrs, [_sds((a.shape[0], a.shape[1] // 2, a.shape[2]), a.dtype) for a in arrs], {},
                    _dma_sems(n, n), _start_then_wait(copies))


def pair_add(g, s, c_idx, name):
    nb, r, cols = g.shape
    rh = r // 2

    def body(c_ref, g_ref, s_ref, o_ref):
        o_ref[...] = (g_ref[...] + s_ref[...]).astype(BF16)

    return pl.pallas_call(
        body, name=name,
        grid_spec=pltpu.PrefetchScalarGridSpec(
            num_scalar_prefetch=1, grid=(nb,),
            in_specs=[pl.BlockSpec((1, rh, cols), lambda j, c: (j, c[0], 0)),
                      pl.BlockSpec((1, rh, cols), lambda j, c: (j, 0, 0))],
            out_specs=pl.BlockSpec((1, rh, cols), lambda j, c: (j, 0, 0))),
        out_shape=_sds((nb, rh, cols), BF16),
        compiler_params=pltpu.CompilerParams(dimension_semantics=("arbitrary",),
                                             vmem_limit_bytes=VMEM_LIMIT),
    )(c_idx, g, s)


def chip_exchange(arrs):
    n = len(arrs)

    def copies(ins, outs, sems):
        x, y, c = _place()
        j = 2 * x + y
        res = []
        for a in range(n):
            for k, (px, py) in enumerate(_other_chips(x, y)):
                def mk(src_blk, dst_blk, a=a, k=k, to=(px, py, c)):
                    return pltpu.make_async_remote_copy(
                        src_ref=ins[a].at[src_blk], dst_ref=outs[a].at[dst_blk], send_sem=sems[0].at[3 * a + k],
                        recv_sem=sems[1].at[3 * a + k], device_id=to, device_id_type=MESH)
                res.append((functools.partial(mk, 2 * px + py, j), functools.partial(mk, j, 2 * px + py)))
        return res

    return Exchange(arrs, [_sds(a.shape, a.dtype) for a in arrs], {}, _dma_sems(3 * n, 3 * n),
                    _start_then_wait(copies))


def sum_chips(r, p, shard_idx, name):
    nb, rh, cols = r.shape
    tr = rh

    def body(j_ref, p_ref, *refs):
        o_ref = refs[nb]
        j = j_ref[0]
        acc = None
        for i in range(nb):
            term = jnp.where(j == i, p_ref[0], refs[i][0]).astype(F32)
            acc = term if acc is None else acc + term
        o_ref[...] = acc

    def slot(i):
        return pl.BlockSpec((1, tr, cols), lambda t, j: (jnp.where(j[0] == i, (i + 1) % nb, i), t, 0))

    return pl.pallas_call(
        body, name=name,
        grid_spec=pltpu.PrefetchScalarGridSpec(
            num_scalar_prefetch=1, grid=(rh // tr,),
            in_specs=[pl.BlockSpec((1, tr, cols), lambda t, j: (j[0], t, 0))] + [slot(i) for i in range(nb)],
            out_specs=pl.BlockSpec((tr, cols), lambda t, j: (t, 0))),
        out_shape=_sds((rh, cols), F32),
        compiler_params=pltpu.CompilerParams(dimension_semantics=("arbitrary",),
                                             vmem_limit_bytes=VMEM_LIMIT),
    )(shard_idx, p, *([r] * nb))


def pair_swap(arrs):
    n = len(arrs)

    def copies(ins, outs, sems):
        x, y, c = _place()
        res = []
        for a in range(n):
            def mk(a=a):
                return pltpu.make_async_remote_copy(
                    src_ref=ins[a], dst_ref=outs[a], send_sem=sems[0].at[a], recv_sem=sems[1].at[a],
                    device_id=(x, y, 1 - c), device_id_type=MESH)
            res.append((mk, mk))
        return res

    return Exchange(arrs, [_sds(a.shape, a.dtype) for a in arrs], {}, _dma_sems(n, n),
                    _start_then_wait(copies))


ADAMW_STEPS_PER_HALF = 4


def adamw_pairs(items, name, comm=None):
    n_items = len(items)
    nh = ADAMW_STEPS_PER_HALF

    def body(*refs):
        ins, outs = refs[:5 * n_items], refs[5 * n_items:]
        mine = (pl.program_id(0) // nh) == lax.axis_index("c")
        for a in range(n_items):
            w_ref, gm_ref, gs_ref, m_ref, v_ref = ins[5 * a:5 * a + 5]
            g_ref, d_ref, mo_ref, vo_ref = outs[4 * a:4 * a + 4]
            g = jnp.where(mine, gm_ref[...], gs_ref[...])
            d, mn, vn = _adamw_math(w_ref[...], g, m_ref[...], v_ref[...])
            g_ref[...] = g
            d_ref[...] = d
            mo_ref[...] = mn
            vo_ref[...] = vn

    in_specs, out_specs, out_shape, args = [], [], [], []
    for w, g_mine, g_sib, m, v in items:
        r, cols = w.shape
        tr = r // (2 * nh)
        full = pl.BlockSpec((tr, cols), lambda i: (i, 0))
        part = pl.BlockSpec((tr, cols), lambda i: (i % nh, 0))
        in_specs += [full, part, part, full, full]
        out_specs += [full] * 4
        out_shape += [_sds((r, cols), F32)] * 4
        args += [w, g_mine, g_sib, m, v]
    res = _call(body, name=name, grid=(2 * nh,), in_specs=in_specs, out_specs=out_specs,
                out_shape=out_shape, comm=comm)(*args)
    own, hosted = (res, None) if comm is None else res
    grouped = [tuple(own[4 * a:4 * a + 4]) for a in range(n_items)]
    return grouped if comm is None else (grouped, hosted)


N_DEV = 8


def device_gather(pack):
    def copies(ins, outs, sems):
        x, y, c = _place()
        me = 4 * x + 2 * y + c
        res = []
        for k in range(1, N_DEV):
            fx, fy, fc = (k >> 2) & 1, (k >> 1) & 1, k & 1
            px, py, pc = (1 - x if fx else x, 1 - y if fy else y, 1 - c if fc else c)

            def mk(slot, k=k, to=(px, py, pc)):
                return pltpu.make_async_remote_copy(
                    src_ref=ins[0], dst_ref=outs[0].at[slot], send_sem=sems[0].at[k - 1],
                    recv_sem=sems[1].at[k - 1], device_id=to, device_id_type=MESH)
            res.append((functools.partial(mk, me), functools.partial(mk, 4 * px + 2 * py + pc)))
        return res

    return Exchange([pack], [_sds((N_DEV,) + pack.shape, pack.dtype)], {}, _dma_sems(N_DEV - 1, N_DEV - 1),
                    _start_then_wait(copies))


def sum_devices(buf, pack, me_idx, name):
    r, cols = pack.shape

    def body(me_ref, p_ref, *refs):
        o_ref = refs[N_DEV]
        acc = None
        for i in range(N_DEV):
            term = jnp.where(me_ref[0] == i, p_ref[...], refs[i][0])
            acc = term if acc is None else acc + term
        o_ref[...] = acc

    def slot(i):
        return pl.BlockSpec((1, r, cols), lambda t, me: (jnp.where(me[0] == i, (i + 1) % N_DEV, i), 0, 0))

    whole = pl.BlockSpec((r, cols), lambda t, me: (0, 0))
    return pl.pallas_call(
        body, name=name,
        grid_spec=pltpu.PrefetchScalarGridSpec(
            num_scalar_prefetch=1, grid=(1,),
            in_specs=[whole] + [slot(i) for i in range(N_DEV)], out_specs=whole),
        out_shape=_sds((r, cols), F32),
        compiler_params=pltpu.CompilerParams(dimension_semantics=("arbitrary",),
                                             vmem_limit_bytes=VMEM_LIMIT),
    )(me_idx, pack, *([buf] * N_DEV))


SMALL_NAMES = ("ffn1_norm", "mix_norm", "ffn2_norm", "final_norm", "sg_ln_g", "sg_ln_b",
               "dn_norm", "a_log", "dt_bias", "sg_b", "sg_w", "conv_w", "loss")


def _to_rows(a):
    flat = a.reshape(-1)
    pad = (-flat.shape[0]) % LANES
    if pad:
        flat = jnp.pad(flat, (0, pad))
    return flat.reshape(-1, LANES)


def _pack_small(parts):
    rows = [_to_rows(parts[k]) for k in SMALL_NAMES]
    pack = jnp.concatenate(rows, axis=0)
    pad = (-pack.shape[0]) % 8
    if pad:
        pack = jnp.pad(pack, ((0, pad), (0, 0)))
    return pack


def _unpack_small(pack, shapes):
    out, r0 = {}, 0
    for k in SMALL_NAMES:
        size = 1
        for s in shapes[k]:
            size *= s
        nrows = -(-size // LANES)
        out[k] = pack[r0:r0 + nrows].reshape(-1)[:size].reshape(shapes[k])
        r0 += nrows
    return out


def kernel(x, ffn1_norm, ffn1_w_gate, ffn1_w_up, ffn1_w_down, mix_norm, w_in, conv_w, a_log, dt_bias, dn_norm, sg_ln_g, sg_ln_b, sg_w, sg_b, w_out, ffn2_norm, ffn2_w_gate, ffn2_w_up, ffn2_w_down, final_norm, loss_target, m_ffn1_norm, m_ffn1_w_gate, m_ffn1_w_up, m_ffn1_w_down, m_mix_norm, m_w_in, m_conv_w, m_a_log, m_dt_bias, m_dn_norm, m_sg_ln_g, m_sg_ln_b, m_sg_w, m_sg_b, m_w_out, m_ffn2_norm, m_ffn2_w_gate, m_ffn2_w_up, m_ffn2_w_down, m_final_norm, v_ffn1_norm, v_ffn1_w_gate, v_ffn1_w_up, v_ffn1_w_down, v_mix_norm, v_w_in, v_conv_w, v_a_log, v_dt_bias, v_dn_norm, v_sg_ln_g, v_sg_ln_b, v_sg_w, v_sg_b, v_w_out, v_ffn2_norm, v_ffn2_w_gate, v_ffn2_w_up, v_ffn2_w_down, v_final_norm):
    bsz, t_len, d = x.shape
    n = bsz * t_len
    xy, yy, cc = _place()
    shard = 2 * xy + yy

    big_names = ["ffn1_w_gate", "ffn1_w_up", "ffn1_w_down", "w_in", "w_out",
                 "ffn2_w_gate", "ffn2_w_up", "ffn2_w_down"]
    big_w = dict(ffn1_w_gate=ffn1_w_gate, ffn1_w_up=ffn1_w_up, ffn1_w_down=ffn1_w_down, w_in=w_in,
                 w_out=w_out, ffn2_w_gate=ffn2_w_gate, ffn2_w_up=ffn2_w_up, ffn2_w_down=ffn2_w_down)
    big_m = dict(ffn1_w_gate=m_ffn1_w_gate, ffn1_w_up=m_ffn1_w_up, ffn1_w_down=m_ffn1_w_down, w_in=m_w_in,
                 w_out=m_w_out, ffn2_w_gate=m_ffn2_w_gate, ffn2_w_up=m_ffn2_w_up, ffn2_w_down=m_ffn2_w_down)
    big_v = dict(ffn1_w_gate=v_ffn1_w_gate, ffn1_w_up=v_ffn1_w_up, ffn1_w_down=v_ffn1_w_down, w_in=v_w_in,
                 w_out=v_w_out, ffn2_w_gate=v_ffn2_w_gate, ffn2_w_up=v_ffn2_w_up, ffn2_w_down=v_ffn2_w_down)
    shard_idx = jnp.reshape(shard, (1,)).astype(jnp.int32)
    c_idx = jnp.reshape(cc, (1,)).astype(jnp.int32)
    transposed = ("ffn1_w_gate", "ffn1_w_up", "ffn2_w_gate", "ffn2_w_up")

    def as2d(a, k):
        return a[0].T if k in transposed else a[0]

    def from2d(a, k):
        return a.T[None] if k in transposed else a[None]

    placed = {k: cast_place(as2d(big_w[k], k), shard_idx, name="cast_" + k) for k in big_names}
    first_names = ["ffn1_w_gate", "ffn1_w_up"]
    second_names = ["ffn1_w_down", "w_in"]
    third_names = ["w_out", "ffn2_w_gate"]
    fourth_names = ["ffn2_w_up", "ffn2_w_down"]
    res = run_exchange(gather_exchange([placed[k] for k in first_names], conv_w[0]), name="gather_first")
    gw = dict(zip(first_names, res[:2]))
    conv_full = res[2].transpose(1, 0, 2).reshape(CONV_K, 3 * HALF_W)

    x0 = x.reshape(n, d)
    def ffn_weights(prefix):
        return [gw[prefix + k].reshape(-1, d) for k in ("_w_gate", "_w_up", "_w_down")]

    def ffn_grad_blocks(grads):
        return [g.reshape(N_SHARD, -1, d) for g in grads]

    (h1, gate1, up1, act1), second = ffn_fwd(
        x0, ffn1_norm, gw["ffn1_w_gate"].reshape(-1, d), gw["ffn1_w_up"].reshape(-1, d), None,
        name="ffn1_fwd", comm=gather_exchange([placed[k] for k in second_names]))
    gw.update(zip(second_names, second))
    (x1,) = ffn_down(x0, act1, gw["ffn1_w_down"].reshape(-1, d), name="ffn1_down")
    w_in_full = gw["w_in"].transpose(1, 0, 2).reshape(d, IN_COLS)
    w_in_full = jnp.pad(w_in_full, ((0, 0), (0, PROJ_W - IN_COLS)))
    (proj, h2), third = in_proj_fwd(x1, mix_norm, w_in_full, name="in_proj_fwd",
                                    comm=gather_exchange([placed[k] for k in third_names]))
    gw.update(zip(third_names, third))
    proj3 = proj.reshape(bsz, t_len, PROJ_W)
    bias_tile = jnp.repeat(sg_b[0].T, SG_GROUP_DIM, axis=1)
    sg_out = sg_fwd(proj, sg_ln_g, sg_ln_b, sg_w[0], bias_tile, name="sg_fwd")
    qkv = dn_conv_fwd(proj3, conv_full, name="dn_conv_fwd")
    alog_row = jnp.zeros((1, LANES), F32).at[0, N_HEADS:2 * N_HEADS].set(a_log[0])
    dtb_row = jnp.zeros((1, LANES), F32).at[0, N_HEADS:2 * N_HEADS].set(dt_bias[0])
    (u_wy, w_wy, q_dec, k_dec, qk, tinv, gc), fourth = dn_chunk_fwd(
        qkv, proj3, alog_row, dtb_row, name="dn_chunk_fwd",
        comm=gather_exchange([placed[k] for k in fourth_names]))
    gw.update(zip(fourth_names, fourth))
    w_out_full = gw["w_out"].reshape(2 * HALF_W, d)
    o, s_in = dn_scan_fwd(u_wy, w_wy, q_dec, k_dec, qk, gc, name="dn_scan_fwd")
    dn_out = dn_out_fwd(o.reshape(n, HALF_W), proj, dn_norm, name="dn_out_fwd")
    x2 = out_proj_fwd(x1, sg_out, dn_out, w_out_full, name="out_proj_fwd")
    x3, h3, gate2, up2, act2 = ffn_fwd(x2, ffn2_norm, *ffn_weights("ffn2"), name="ffn2_fwd")
    dx3, d_final_norm, loss_tile = final_loss(x3, final_norm.reshape(1, d),
                                              loss_target.reshape(n, d), name="final_loss")

    dx2, dgate2, dup2, dyh2, d_ffn2_norm = ffn_bwd_act(
        dx3, x2, ffn2_norm, gate2, up2, *ffn_weights("ffn2"), name="ffn2_bwd_act")
    g_big = {}
    g_big["ffn2_w_gate"], g_big["ffn2_w_up"], g_big["ffn2_w_down"] = ffn_grad_blocks(ffn_bwd_w(
        [dgate2, dup2, act2], [h3, dyh2], [(0, 0), (1, 0), (2, 1)], name="ffn2_bwd_w"))

    early = ["ffn2_w_gate", "ffn2_w_up", "ffn2_w_down"]
    (d_sg, d_dn, dx2b), early_sib = out_proj_bwd_x(dx2, w_out_full, name="out_proj_bwd_x",
                                                   comm=pair_exchange([g_big[k] for k in early]))
    early_sums = [pair_add(g_big[k], s, c_idx, name="grad_pair_add_" + k) for k, s in zip(early, early_sib)]
    g_w_out = matmul_tn([sg_out, dn_out], dx2b, d, name="w_out_grad")
    g_big["w_out"] = g_w_out.reshape(N_SHARD, (2 * HALF_W) // N_SHARD, d)

    d_proj, d_sg_w, d_bias_tile, d_ln_g, d_ln_b = sg_bwd(d_sg, proj, sg_ln_g, sg_ln_b, sg_w[0],
                                                         bias_tile, name="sg_bwd")
    d_o, d_proj, d_dn_norm = dn_out_bwd(d_dn, o.reshape(n, HALF_W), proj, dn_norm, d_proj,
                                        name="dn_out_bwd")
    du, dw, dqd, dkd, dqk, dgc_scan = dn_scan_bwd(d_o.reshape(bsz, t_len, HALF_W), u_wy, w_wy, q_dec,
                                                  k_dec, qk, gc, s_in, name="dn_scan_bwd")
    (d_qkv, d_proj3, d_alog_row, d_dtb_row), early_chips = dn_chunk_bwd(
        qkv, proj3, alog_row, dtb_row, tinv, u_wy, w_wy, du, dw, dqd, dkd, dqk, dgc_scan,
        d_proj.reshape(bsz, t_len, PROJ_W), name="dn_chunk_bwd", comm=chip_exchange(early_sums))
    early_halves = [sum_chips(r, p, shard_idx, name="grad_chip_sum_" + k)
                    for k, r, p in zip(early, early_chips, early_sums)]
    (d_proj3, d_conv), early_sib_halves = dn_conv_bwd(d_qkv, proj3, conv_full, d_proj3, name="dn_conv_bwd",
                                                      comm=pair_swap(early_halves))
    d_proj = d_proj3.reshape(n, PROJ_W)
    g_w_in = matmul_tn([h2], d_proj, 3 * MXU_DIM, name="w_in_grad")[:, :IN_COLS]
    g_big["w_in"] = g_w_in.reshape(d, N_SHARD, IN_COLS // N_SHARD).transpose(1, 0, 2)

    def reduce_start(names):
        return pair_exchange([g_big[k] for k in names])

    def reduce_pair_sums(names, from_sib):
        return [pair_add(g_big[k], s, c_idx, name="grad_pair_add_" + k) for k, s in zip(names, from_sib)]

    def reduce_chip_sums(names, from_chips, sums):
        return [sum_chips(r, p, shard_idx, name="grad_chip_sum_" + k)
                for k, r, p in zip(names, from_chips, sums)]

    mid = ["w_in", "w_out"]
    (dx1, d_mix_norm, dyh1), mid_sib = in_proj_bwd_x(d_proj, w_in_full, x1, mix_norm, dx2,
                                                     name="in_proj_bwd_x", comm=reduce_start(mid))
    mid_sums = reduce_pair_sums(mid, mid_sib)
    down = ["ffn1_w_down"]
    (g_down,), mid_chips = ffn_bwd_w([act1], [dyh1], [(0, 0)], name="ffn1_bwd_w_down",
                                     comm=chip_exchange(mid_sums))
    g_big["ffn1_w_down"] = g_down.reshape(N_SHARD, -1, d)
    mid_halves = reduce_chip_sums(mid, mid_chips, mid_sums)
    leg, legs = merge_exchanges([reduce_start(down), pair_swap(mid_halves)])
    leg_res = run_exchange(leg, name="grad_pair_exchange_down")
    down_sums = reduce_pair_sums(down, leg_res[legs[0]])
    mid_sib_halves = leg_res[legs[1]]

    dx0, dgate1, dup1, _, d_ffn1_norm = ffn_bwd_act(
        dx1, x0, ffn1_norm, gate1, up1, *ffn_weights("ffn1"), name="ffn1_bwd_act")
    grad_x = dx0.reshape(bsz, t_len, d)
    d_sg_b = d_bias_tile.reshape(SG_CHUNK, SG_GROUPS, SG_GROUP_DIM).sum(axis=-1).T
    small_g = dict(ffn1_norm=d_ffn1_norm, mix_norm=d_mix_norm, ffn2_norm=d_ffn2_norm,
                   final_norm=d_final_norm, sg_ln_g=d_ln_g, sg_ln_b=d_ln_b, dn_norm=d_dn_norm,
                   a_log=d_alog_row[:, N_HEADS:2 * N_HEADS], dt_bias=d_dtb_row[:, N_HEADS:2 * N_HEADS],
                   sg_b=d_sg_b, sg_w=d_sg_w, conv_w=d_conv, loss=loss_tile[:, :1])
    my_pack = _pack_small(small_g)
    hosted, parts = merge_exchanges([chip_exchange(down_sums), device_gather(my_pack)])
    late = ["ffn1_w_gate", "ffn1_w_up"]
    late_grads, hosted_res = ffn_bwd_w([dgate1, dup1], [h1], [(0, 0), (1, 0)], name="ffn1_bwd_w_gate_up",
                                       comm=hosted)
    g_big["ffn1_w_gate"], g_big["ffn1_w_up"] = ffn_grad_blocks(late_grads)
    down_halves = reduce_chip_sums(down, hosted_res[parts[0]], down_sums)
    (all_packs,) = hosted_res[parts[1]]

    leg, legs = merge_exchanges([reduce_start(late), pair_swap(down_halves)])
    leg_res = run_exchange(leg, name="grad_pair_exchange")
    pair_sums = reduce_pair_sums(late, leg_res[legs[0]])
    down_sib_halves = leg_res[legs[1]]

    def adam_items(names, mine, sib):
        return [(as2d(big_w[k], k), gm, gs, as2d(big_m[k], k), as2d(big_v[k], k))
                for k, gm, gs in zip(names, mine, sib)]

    outs = {}
    done = adamw_pairs(
        adam_items(early + mid + down, early_halves + mid_halves + down_halves,
                   list(early_sib_halves) + list(mid_sib_halves) + list(down_sib_halves)),
        name="adamw_early")
    from_chips = run_exchange(chip_exchange(pair_sums), name="grad_chip_exchange")
    halves = reduce_chip_sums(late, from_chips, pair_sums)
    sib_halves = run_exchange(pair_swap(halves), name="grad_pair_swap")
    done += adamw_pairs(adam_items(late, halves, sib_halves), name="adamw_late")
    for k, res in zip(early + mid + down + late, done):
        outs[k] = tuple(from2d(a, k) for a in res)

    small_w = dict(ffn1_norm=ffn1_norm, mix_norm=mix_norm, ffn2_norm=ffn2_norm, final_norm=final_norm,
                   sg_ln_g=sg_ln_g, sg_ln_b=sg_ln_b, dn_norm=dn_norm, a_log=a_log, dt_bias=dt_bias,
                   sg_b=sg_b, sg_w=sg_w)
    small_m = dict(ffn1_norm=m_ffn1_norm, mix_norm=m_mix_norm, ffn2_norm=m_ffn2_norm,
                   final_norm=m_final_norm, sg_ln_g=m_sg_ln_g, sg_ln_b=m_sg_ln_b, dn_norm=m_dn_norm,
                   a_log=m_a_log, dt_bias=m_dt_bias, sg_b=m_sg_b, sg_w=m_sg_w)
    small_v = dict(ffn1_norm=v_ffn1_norm, mix_norm=v_mix_norm, ffn2_norm=v_ffn2_norm,
                   final_norm=v_final_norm, sg_ln_g=v_sg_ln_g, sg_ln_b=v_sg_ln_b, dn_norm=v_dn_norm,
                   a_log=v_a_log, dt_bias=v_dt_bias, sg_b=v_sg_b, sg_w=v_sg_w)
    shapes = {k: small_w[k].shape for k in small_w}
    shapes["conv_w"] = (CONV_K, 3 * HALF_W)
    shapes["loss"] = (1, 1)
    me_idx = jnp.reshape(4 * xy + 2 * yy + cc, (1,)).astype(jnp.int32)
    g_pack = sum_devices(all_packs, my_pack, me_idx, name="small_sum")
    g_small = _unpack_small(g_pack, shapes)
    loss = g_small["loss"].reshape(())
    cw = 3 * HALF_W // N_SHARD
    g_conv = lax.dynamic_slice_in_dim(g_small["conv_w"], shard * cw, cw, axis=1)
    zero_conv = jnp.zeros((CONV_K, 3 * HALF_W), F32)

    def packed(src, conv):
        parts = dict(src)
        parts["conv_w"] = lax.dynamic_update_slice_in_dim(zero_conv, conv[0], shard * cw, axis=1)
        parts["loss"] = jnp.zeros((1, 1), F32)
        return _pack_small(parts)

    d_pack, m_pack, v_pack = adamw(packed(small_w, conv_w), g_pack, packed(small_m, m_conv_w),
                                   packed(small_v, v_conv_w), name="adamw_small")
    d_small = _unpack_small(d_pack, shapes)
    m_small = _unpack_small(m_pack, shapes)
    v_small = _unpack_small(v_pack, shapes)

    def conv_block(full_arr):
        return lax.dynamic_slice_in_dim(full_arr, shard * cw, cw, axis=1)[None]

    for k in small_w:
        outs[k] = (g_small[k].reshape(small_w[k].shape), d_small[k], m_small[k], v_small[k])
    outs["conv_w"] = (g_conv[None], conv_block(d_small["conv_w"]), conv_block(m_small["conv_w"]),
                      conv_block(v_small["conv_w"]))

    order = ["ffn1_norm", "ffn1_w_gate", "ffn1_w_up", "ffn1_w_down", "mix_norm", "w_in", "conv_w",
             "a_log", "dt_bias", "dn_norm", "sg_ln_g", "sg_ln_b", "sg_w", "sg_b", "w_out", "ffn2_norm",
             "ffn2_w_gate", "ffn2_w_up", "ffn2_w_down", "final_norm"]
    return (loss, grad_x, *[outs[k][0] for k in order], *[outs[k][1] for k in order],
            *[outs[k][2] for k in order], *[outs[k][3] for k in order])
```

```python
import functools

import jax
import jax.numpy as jnp
from jax import lax
from jax.experimental import pallas as pl
from jax.experimental.pallas import tpu as pltpu

F32 = jnp.float32
BF16 = jnp.bfloat16
EPS = 1e-6

D_MODEL = 1024
N_SHARD = 4
HEAD_DIM = 128
N_HEADS = 4
DN_CHUNK = 64
SG_CHUNK = 128
SG_GROUPS = 8
SG_GROUP_DIM = 64
HALF_W = 512
PROJ_W = 3200
IN_COLS = 3080
GATE_COL_BLOCK = 24
QK_SCALE = HEAD_DIM ** -0.5
LANES = 128

ADAM_LR = 0.001
ADAM_B1 = 0.9
ADAM_B2 = 0.999
ADAM_EPS = 1e-08
ADAM_WD = 0.01
ADAM_STEP = 10

VMEM_LIMIT = 56 * 1024 * 1024
ROW_TILE = 512

NN = ((1,), (0,))
NT = ((1,), (1,))
TN = ((0,), (0,))
MESH = pl.DeviceIdType.MESH


def _dot(a, b, dims):
    return lax.dot_general(a, b, (dims, ((), ())), preferred_element_type=F32)


def _bdot(a, b, dims):
    return _dot(a.astype(BF16), b.astype(BF16), dims)


def _split(a):
    hi = a.astype(BF16)
    lo = (a - hi.astype(F32)).astype(BF16)
    return hi, lo


def _dot3(a, b, dims=NN):
    return _dot(a[0], b[0], dims) + (_dot(a[0], b[1], dims) + _dot(a[1], b[0], dims))


def _dot_exact_lhs(a, b):
    ab = a.astype(BF16)
    b1 = b.astype(BF16)
    r1 = b - b1.astype(F32)
    b2 = r1.astype(BF16)
    b3 = (r1 - b2.astype(F32)).astype(BF16)
    return _dot(ab, b1, NN) + (_dot(ab, b2, NN) + _dot(ab, b3, NN))


def _call(body, *, name, out_shape, in_specs, out_specs, grid=(), scratch=(), comm=None, **kw):
    params = dict(vmem_limit_bytes=VMEM_LIMIT)
    if grid:
        params["dimension_semantics"] = ("arbitrary",) * len(grid)
    if comm is None:
        return pl.pallas_call(
            body, name=name, grid=grid, in_specs=in_specs, out_specs=out_specs,
            out_shape=out_shape, scratch_shapes=list(scratch),
            compiler_params=pltpu.CompilerParams(**params), **kw)

    n_in, n_out, n_sc = len(in_specs), len(out_specs), len(scratch)
    c_in, c_out = len(comm.inputs), len(comm.out_shape)
    steps = 1
    for g in grid:
        steps *= g

    def hosted(*refs):
        ins, cins = refs[:n_in], refs[n_in:n_in + c_in]
        o0 = n_in + c_in
        outs, couts = refs[o0:o0 + n_out], refs[o0 + n_out:o0 + n_out + c_out]
        s0 = o0 + n_out + c_out
        sc, csems = refs[s0:s0 + n_sc], refs[s0 + n_sc:]
        lin = 0
        for axis, g in enumerate(grid):
            lin = lin * g + pl.program_id(axis)

        def at(step, fn):
            @pl.when(lin == step % steps)
            def _():
                fn(cins, couts, csems)

        for step, fn in comm.phases:
            if step >= 0:
                at(step, fn)
        body(*ins, *outs, *sc)
        for step, fn in comm.phases:
            if step < 0:
                at(step, fn)

    aliases = dict(kw.pop("input_output_aliases", {}))
    for k, m in comm.aliases.items():
        aliases[n_in + k] = n_out + m
    call = pl.pallas_call(
        hosted, name=name, grid=grid, in_specs=list(in_specs) + [_ANY] * c_in,
        out_specs=list(out_specs) + [_ANY] * c_out, out_shape=list(out_shape) + comm.out_shape,
        scratch_shapes=list(scratch) + comm.sems, input_output_aliases=aliases,
        compiler_params=pltpu.CompilerParams(**params), **kw)

    def run(*args):
        res = call(*args, *comm.inputs)
        return res[:n_out], res[n_out:]

    return run


def _sds(shape, dtype):
    return jax.ShapeDtypeStruct(tuple(shape), dtype)


def _resident(shape):
    zeros = (0,) * len(shape)
    return pl.BlockSpec(tuple(shape), lambda *_: zeros, pipeline_mode=pl.Buffered(1))


def _sigmoid(x):
    return jax.nn.sigmoid(x)


def _softplus(x):
    return jnp.maximum(x, 0.0) + jnp.log(1.0 + jnp.exp(-jnp.abs(x)))


_GELU_C = 0.7978845608028654
_GELU_A = 0.044715


def _gelu(x):
    t = jnp.tanh(_GELU_C * (x + _GELU_A * x * x * x))
    return 0.5 * x * (1.0 + t)


def _gelu_grad(x):
    t = jnp.tanh(_GELU_C * (x + _GELU_A * x * x * x))
    return 0.5 * (1.0 + t) + 0.5 * x * (1.0 - t * t) * _GELU_C * (1.0 + 3.0 * _GELU_A * x * x)


def _silu_grad(x):
    s = _sigmoid(x)
    return s * (1.0 + x * (1.0 - s))


def _rms_scale(xv):
    return lax.rsqrt(jnp.mean(xv * xv, axis=-1, keepdims=True) + EPS)


def _rms_bwd(dh, xv, g):
    r = _rms_scale(xv)
    xn = xv * r
    dg = jnp.sum(dh * xn, axis=0, keepdims=True)
    dxn = dh * g
    dx = r * (dxn - xn * jnp.mean(dxn * xn, axis=-1, keepdims=True))
    return dx, dg


def _iota2(shape, dim):
    return lax.broadcasted_iota(jnp.int32, shape, dim)


def _col_to_row(col):
    n = col.shape[0]
    eye = _iota2((n, n), 0) == _iota2((n, n), 1)
    return jnp.sum(jnp.where(eye, col, 0.0), axis=0, keepdims=True)


def _row_to_col(row):
    n = row.shape[1]
    eye = _iota2((n, n), 0) == _iota2((n, n), 1)
    return jnp.sum(jnp.where(eye, row, 0.0), axis=1, keepdims=True)


MXU_DIM = 256


def _hidden_chunks(f, step=3 * MXU_DIM):
    return [(c0, min(c0 + step, f)) for c0 in range(0, f, step)]

def ffn_fwd(x, gnorm, wg, wu, wd, name, comm=None):
    n, d = x.shape
    f = wg.shape[0]
    tm = min(ROW_TILE, n)
    fused = wd is not None

    def body(x_ref, g_ref, wg_ref, wu_ref, *rest):
        if fused:
            wd_ref, xo_ref, h_ref, gate_ref, up_ref, act_ref, acc_ref = rest
        else:
            h_ref, gate_ref, up_ref, act_ref = rest
        xv = x_ref[...]
        h = (xv * _rms_scale(xv) * g_ref[...]).astype(BF16)
        h_ref[...] = h
        chunks = _hidden_chunks(f)

        def gate_up(c0, c1):
            return _dot(h, wg_ref[c0:c1, :], NT), _dot(h, wu_ref[c0:c1, :], NT)

        nxt = gate_up(*chunks[0])
        for idx, (c0, c1) in enumerate(chunks):
            gate, up = nxt
            if idx + 1 < len(chunks):
                nxt = gate_up(*chunks[idx + 1])
            act = (gate * _sigmoid(gate) * up).astype(BF16)
            gate_ref[:, c0:c1] = gate.astype(BF16)
            up_ref[:, c0:c1] = up.astype(BF16)
            act_ref[:, c0:c1] = act
            if fused:
                part = _dot(act, wd_ref[c0:c1, :], NN)
                if c0 == 0:
                    acc_ref[...] = part
                else:
                    acc_ref[...] += part
        if fused:
            xo_ref[...] = xv + 0.5 * acc_ref[...]

    row = pl.BlockSpec((tm, d), lambda i: (i, 0))
    wide = pl.BlockSpec((tm, f), lambda i: (i, 0))
    n_w = 3 if fused else 2
    return _call(
        body, name=name, grid=(n // tm,),
        in_specs=[row, pl.BlockSpec((1, d), lambda i: (0, 0))] + [_resident((f, d))] * n_w,
        out_specs=([row] if fused else []) + [row, wide, wide, wide],
        out_shape=([_sds((n, d), F32)] if fused else []) + [_sds((n, d), BF16)] + [_sds((n, f), BF16)] * 3,
        scratch=[pltpu.VMEM((tm, d), F32)] if fused else [], comm=comm,
    )(*([x, gnorm, wg, wu] + ([wd] if fused else [])))


def ffn_down(x, act, wd, name, comm=None):
    n, d = x.shape
    f = wd.shape[0]
    tm = min(ROW_TILE, n)

    def body(x_ref, a_ref, w_ref, o_ref):
        o_ref[...] = x_ref[...] + 0.5 * _dot(a_ref[...], w_ref[...], NN)

    row = pl.BlockSpec((tm, d), lambda i: (i, 0))
    return _call(
        body, name=name, grid=(n // tm,),
        in_specs=[row, pl.BlockSpec((tm, f), lambda i: (i, 0)), _resident((f, d))],
        out_specs=[row], out_shape=[_sds((n, d), F32)], comm=comm,
    )(x, act, wd)


def ffn_bwd_act(dy, x, gnorm, gate, up, wg, wu, wd, name, comm=None):
    n, d = x.shape
    f = wg.shape[0]
    tm = min(ROW_TILE // 2, n)

    def body(dy_ref, x_ref, g_ref, gate_ref, up_ref, wg_ref, wu_ref, wd_ref,
             dx_ref, dgate_ref, dup_ref, dyh_ref, dg_ref, acc_ref):
        @pl.when(pl.program_id(0) == 0)
        def _():
            dg_ref[...] = jnp.zeros_like(dg_ref)

        dyh = (0.5 * dy_ref[...]).astype(BF16)
        dyh_ref[...] = dyh
        chunks = _hidden_chunks(f)
        next_dact = _dot(dyh, wd_ref[chunks[0][0]:chunks[0][1], :], NT)
        for idx, (c0, c1) in enumerate(chunks):
            dact = next_dact
            if idx + 1 < len(chunks):
                n0, n1 = chunks[idx + 1]
                next_dact = _dot(dyh, wd_ref[n0:n1, :], NT)
            gt = gate_ref[:, c0:c1].astype(F32)
            u = up_ref[:, c0:c1].astype(F32)
            s = _sigmoid(gt)
            dup = (dact * (gt * s)).astype(BF16)
            dgate = (dact * u * (s * (1.0 + gt * (1.0 - s)))).astype(BF16)
            dup_ref[:, c0:c1] = dup
            dgate_ref[:, c0:c1] = dgate
            part = _dot(dgate, wg_ref[c0:c1, :], NN) + _dot(dup, wu_ref[c0:c1, :], NN)
            if c0 == 0:
                acc_ref[...] = part
            else:
                acc_ref[...] += part
        dxn, dg = _rms_bwd(acc_ref[...], x_ref[...], g_ref[...])
        dx_ref[...] = dy_ref[...] + dxn
        dg_ref[...] += dg

    row = pl.BlockSpec((tm, d), lambda i: (i, 0))
    wide = pl.BlockSpec((tm, f), lambda i: (i, 0))
    vec = pl.BlockSpec((1, d), lambda i: (0, 0))
    wres = _resident((f, d))
    return _call(
        body, name=name, grid=(n // tm,),
        in_specs=[row, row, vec, wide, wide, wres, wres, wres],
        out_specs=[row, wide, wide, row, vec],
        out_shape=[_sds((n, d), F32), _sds((n, f), BF16), _sds((n, f), BF16),
                   _sds((n, d), BF16), _sds((1, d), F32)],
        scratch=[pltpu.VMEM((tm, d), F32)], comm=comm,
    )(dy, x, gnorm, gate, up, wg, wu, wd)


def ffn_bwd_w(wide, rows, pairs, name, comm=None):
    n, d = rows[0].shape
    f = wide[0].shape[1]
    fh = f // 2
    tk = min(ROW_TILE, n)
    n_w, n_r = len(wide), len(rows)

    def body(*refs):
        wide_refs, row_refs, outs = refs[:n_w], refs[n_w:n_w + n_r], refs[n_w + n_r:]

        @pl.when(pl.program_id(1) == 0)
        def _():
            for o_ref in outs:
                o_ref[...] = jnp.zeros_like(o_ref)

        row_vals = [r[...] for r in row_refs]
        for c0, c1 in _hidden_chunks(fh, 2 * MXU_DIM):
            for (i, k), o_ref in zip(pairs, outs):
                o_ref[c0:c1, :] += _dot(wide_refs[i][:, c0:c1], row_vals[k], TN)

    row = pl.BlockSpec((tk, d), lambda j, k: (k, 0))
    blk = pl.BlockSpec((tk, fh), lambda j, k: (k, j))
    return _call(
        body, name=name, grid=(2, n // tk),
        in_specs=[blk] * n_w + [row] * n_r,
        out_specs=[pl.BlockSpec((fh, d), lambda j, k: (j, 0))] * len(pairs),
        out_shape=[_sds((f, d), F32)] * len(pairs), comm=comm,
    )(*wide, *rows)


def final_loss(x, gnorm, target, name):
    n, d = x.shape
    tm = min(ROW_TILE, n)

    def body(x_ref, g_ref, t_ref, dx_ref, dg_ref, loss_ref):
        @pl.when(pl.program_id(0) == 0)
        def _():
            dg_ref[...] = jnp.zeros_like(dg_ref)
            loss_ref[...] = jnp.zeros_like(loss_ref)

        xv = x_ref[...]
        y = xv * _rms_scale(xv) * g_ref[...]
        err = y - t_ref[...]
        part = 0.5 * jnp.sum(jnp.mean(err * err, axis=-1, keepdims=True), axis=0, keepdims=True)
        loss_ref[...] += jnp.broadcast_to(part, loss_ref.shape)
        dx, dg = _rms_bwd(err * (1.0 / d), xv, g_ref[...])
        dx_ref[...] = dx
        dg_ref[...] += dg

    row = pl.BlockSpec((tm, d), lambda i: (i, 0))
    vec = pl.BlockSpec((1, d), lambda i: (0, 0))
    return _call(
        body, name=name, grid=(n // tm,),
        in_specs=[row, vec, row],
        out_specs=[row, vec, pl.BlockSpec((1, LANES), lambda i: (0, 0))],
        out_shape=[_sds((n, d), F32), _sds((1, d), F32), _sds((1, LANES), F32)],
    )(x, gnorm, target)


def in_proj_fwd(x, gnorm, w, name, comm=None):
    n, d = x.shape
    cols = w.shape[1]
    tm = min(ROW_TILE, n)

    def body(x_ref, g_ref, w_ref, p_ref, h_ref):
        xv = x_ref[...]
        h = (xv * _rms_scale(xv) * g_ref[...]).astype(BF16)
        h_ref[...] = h
        for c0, c1 in _hidden_chunks(cols):
            p_ref[:, c0:c1] = _dot(h, w_ref[:, c0:c1], NN)

    return _call(
        body, name=name, grid=(n // tm,),
        in_specs=[pl.BlockSpec((tm, d), lambda i: (i, 0)),
                  pl.BlockSpec((1, d), lambda i: (0, 0)), _resident((d, cols))],
        out_specs=[pl.BlockSpec((tm, cols), lambda i: (i, 0)),
                   pl.BlockSpec((tm, d), lambda i: (i, 0))],
        out_shape=[_sds((n, cols), F32), _sds((n, d), BF16)], comm=comm,
    )(x, gnorm, w)


def in_proj_bwd_x(dproj, w, x, gnorm, dres, name, comm=None):
    n, d = x.shape
    cols = w.shape[1]
    tm = min(ROW_TILE, n)

    def body(dp_ref, w_ref, x_ref, g_ref, dr_ref, dx_ref, dg_ref, dxh_ref):
        @pl.when(pl.program_id(0) == 0)
        def _():
            dg_ref[...] = jnp.zeros_like(dg_ref)

        dh = _dot(dp_ref[...], w_ref[...], NT)
        dxn, dg = _rms_bwd(dh, x_ref[...], g_ref[...])
        dx = dr_ref[...] + dxn
        dx_ref[...] = dx
        dxh_ref[...] = (0.5 * dx).astype(BF16)
        dg_ref[...] += dg

    row = pl.BlockSpec((tm, d), lambda i: (i, 0))
    vec = pl.BlockSpec((1, d), lambda i: (0, 0))
    return _call(
        body, name=name, grid=(n // tm,),
        in_specs=[pl.BlockSpec((tm, cols), lambda i: (i, 0)),
                  _resident((d, cols)), row, vec, row],
        out_specs=[row, vec, row],
        out_shape=[_sds((n, d), F32), _sds((1, d), F32), _sds((n, d), BF16)], comm=comm,
    )(dproj, w, x, gnorm, dres)


def matmul_tn(a_list, b, tn, name):
    n, cb = b.shape
    widths = [a.shape[1] for a in a_list]
    tk = min(ROW_TILE, n)

    def body(*refs):
        a_refs, b_ref, o_ref = refs[:-2], refs[-2], refs[-1]

        @pl.when(pl.program_id(0) == 0)
        def _():
            o_ref[...] = jnp.zeros_like(o_ref)

        r0 = 0
        for a_ref, ka in zip(a_refs, widths):
            av = a_ref[...]
            for c0, c1 in _hidden_chunks(cb, tn):
                o_ref[r0:r0 + ka, c0:c1] += _dot(av, b_ref[:, c0:c1], TN)
            r0 += ka

    return _call(
        body, name=name, grid=(n // tk,),
        in_specs=[pl.BlockSpec((tk, ka), lambda k: (k, 0)) for ka in widths]
        + [pl.BlockSpec((tk, cb), lambda k: (k, 0))],
        out_specs=pl.BlockSpec((sum(widths), cb), lambda k: (0, 0)),
        out_shape=_sds((sum(widths), cb), F32),
    )(*a_list, b)


def out_proj_fwd(x, sg_out, dn_out, w, name):
    n, d = x.shape
    tm = min(ROW_TILE, n)

    def body(x_ref, a_ref, b_ref, w_ref, o_ref):
        o_ref[...] = (x_ref[...] + _dot(a_ref[...], w_ref[0:HALF_W, :], NN)
                      + _dot(b_ref[...], w_ref[HALF_W:2 * HALF_W, :], NN))

    row = pl.BlockSpec((tm, d), lambda i: (i, 0))
    half = pl.BlockSpec((tm, HALF_W), lambda i: (i, 0))
    return _call(
        body, name=name, grid=(n // tm,),
        in_specs=[row, half, half, pl.BlockSpec((2 * HALF_W, d), lambda i: (0, 0))],
        out_specs=row, out_shape=_sds((n, d), F32),
    )(x, sg_out, dn_out, w)


def out_proj_bwd_x(dy, w, name, comm=None):
    n, d = dy.shape
    tm = min(ROW_TILE, n)

    def body(dy_ref, w_ref, dsg_ref, ddn_ref, dyb_ref):
        dyb = dy_ref[...].astype(BF16)
        dyb_ref[...] = dyb
        dsg_ref[...] = _dot(dyb, w_ref[0:HALF_W, :], NT)
        ddn_ref[...] = _dot(dyb, w_ref[HALF_W:2 * HALF_W, :], NT)

    row = pl.BlockSpec((tm, d), lambda i: (i, 0))
    half = pl.BlockSpec((tm, HALF_W), lambda i: (i, 0))
    return _call(
        body, name=name, grid=(n // tm,),
        in_specs=[row, pl.BlockSpec((2 * HALF_W, d), lambda i: (0, 0))],
        out_specs=[half, half, row],
        out_shape=[_sds((n, HALF_W), F32), _sds((n, HALF_W), F32), _sds((n, d), BF16)], comm=comm,
    )(dy, w)


SG_PAIRS = SG_GROUPS // 2


def _sg_low_half():
    return _iota2((SG_CHUNK, LANES), 1) < SG_GROUP_DIM


def _sg_pair_cols(p):
    return slice(p * LANES, (p + 1) * LANES)


def _sg_causal():
    return _iota2((SG_CHUNK, SG_CHUNK), 0) >= _iota2((SG_CHUNK, SG_CHUNK), 1)


def _sg_forward_chunk(pu, pv, ln_g, ln_b, wc, bias, low):
    u = _gelu(pu)
    v = _gelu(pv)
    mu = jnp.mean(v, axis=-1, keepdims=True)
    vc = v - mu
    rs = lax.rsqrt(jnp.mean(vc * vc, axis=-1, keepdims=True) + EPS)
    xhat = vc * rs
    vn = (xhat * ln_g + ln_b).astype(BF16)
    parts = []
    for p in range(SG_PAIRS):
        vn_p = vn[:, _sg_pair_cols(p)]
        parts.append(jnp.where(low, _dot(wc[2 * p], vn_p, NN), _dot(wc[2 * p + 1], vn_p, NN)))
    vs = bias + jnp.concatenate(parts, axis=1)
    return u, xhat, rs, vn, vs


def sg_fwd(proj, ln_g, ln_b, w_s, bias_tile, name):
    n = proj.shape[0]
    tm = min(ROW_TILE, n)

    def body(pu_ref, pv_ref, g_ref, b_ref, w_ref, bias_ref, o_ref):
        causal = _sg_causal()
        wc = [jnp.where(causal, w_ref[g], 0.0).astype(BF16) for g in range(SG_GROUPS)]
        masks = _sg_low_half()
        for ci in range(tm // SG_CHUNK):
            rows = slice(ci * SG_CHUNK, (ci + 1) * SG_CHUNK)
            u, _, _, _, vs = _sg_forward_chunk(pu_ref[rows, :], pv_ref[rows, :], g_ref[...],
                                               b_ref[...], wc, bias_ref[...], masks)
            o_ref[rows, :] = (u * vs).astype(BF16)

    vec = pl.BlockSpec((1, HALF_W), lambda i: (0, 0))
    return _call(
        body, name=name, grid=(n // tm,),
        in_specs=[pl.BlockSpec((tm, HALF_W), lambda i: (i, 0)),
                  pl.BlockSpec((tm, HALF_W), lambda i: (i, 1)), vec, vec,
                  pl.BlockSpec((SG_GROUPS, SG_CHUNK, SG_CHUNK), lambda i: (0, 0, 0)),
                  pl.BlockSpec((SG_CHUNK, HALF_W), lambda i: (0, 0))],
        out_specs=pl.BlockSpec((tm, HALF_W), lambda i: (i, 0)),
        out_shape=_sds((n, HALF_W), BF16),
    )(proj, proj, ln_g, ln_b, w_s, bias_tile)


def sg_bwd(dsg, proj, ln_g, ln_b, w_s, bias_tile, name):
    n = proj.shape[0]
    tm = min(ROW_TILE, n)

    def body(d_ref, pu_ref, pv_ref, g_ref, b_ref, w_ref, bias_ref,
             dp_ref, dw_ref, db_ref, dlg_ref, dlb_ref):
        @pl.when(pl.program_id(0) == 0)
        def _():
            dw_ref[...] = jnp.zeros_like(dw_ref)
            db_ref[...] = jnp.zeros_like(db_ref)
            dlg_ref[...] = jnp.zeros_like(dlg_ref)
            dlb_ref[...] = jnp.zeros_like(dlb_ref)

        causal = _sg_causal()
        wc = [jnp.where(causal, w_ref[g], 0.0).astype(BF16) for g in range(SG_GROUPS)]
        masks = _sg_low_half()
        ln_g_v = g_ref[...]
        for ci in range(tm // SG_CHUNK):
            rows = slice(ci * SG_CHUNK, (ci + 1) * SG_CHUNK)
            pu = pu_ref[rows, :]
            pv = pv_ref[rows, :]
            u, xhat, rs, vn, vs = _sg_forward_chunk(pu, pv, ln_g_v, b_ref[...], wc,
                                                    bias_ref[...], masks)
            dout = d_ref[rows, :]
            dp_ref[rows, 0:HALF_W] = (dout * vs * _gelu_grad(pu)).astype(BF16)
            dvs = dout * u
            dvs_b = dvs.astype(BF16)
            db_ref[...] += dvs
            dvn_parts = []
            for p in range(SG_PAIRS):
                dvs_p = dvs_b[:, _sg_pair_cols(p)]
                vn_p = vn[:, _sg_pair_cols(p)]
                dvn_parts.append(jnp.where(masks, _dot(wc[2 * p], dvs_p, TN), _dot(wc[2 * p + 1], dvs_p, TN)))
                zero = jnp.zeros_like(dvs_p)
                dw_ref[2 * p] += jnp.where(causal, _dot(jnp.where(masks, dvs_p, zero), vn_p, NT), 0.0)
                dw_ref[2 * p + 1] += jnp.where(causal, _dot(jnp.where(masks, zero, dvs_p), vn_p, NT), 0.0)
            dvn = jnp.concatenate(dvn_parts, axis=1)
            dlg_ref[...] += jnp.sum(dvn * xhat, axis=0, keepdims=True)
            dlb_ref[...] += jnp.sum(dvn, axis=0, keepdims=True)
            dxh = dvn * ln_g_v
            dv = rs * (dxh - jnp.mean(dxh, axis=-1, keepdims=True)
                       - xhat * jnp.mean(dxh * xhat, axis=-1, keepdims=True))
            dp_ref[rows, HALF_W:2 * HALF_W] = (dv * _gelu_grad(pv)).astype(BF16)

    vec = pl.BlockSpec((1, HALF_W), lambda i: (0, 0))
    wspec = pl.BlockSpec((SG_GROUPS, SG_CHUNK, SG_CHUNK), lambda i: (0, 0, 0))
    tile = pl.BlockSpec((SG_CHUNK, HALF_W), lambda i: (0, 0))
    return _call(
        body, name=name, grid=(n // tm,),
        in_specs=[pl.BlockSpec((tm, HALF_W), lambda i: (i, 0)),
                  pl.BlockSpec((tm, HALF_W), lambda i: (i, 0)),
                  pl.BlockSpec((tm, HALF_W), lambda i: (i, 1)), vec, vec, wspec, tile],
        out_specs=[pl.BlockSpec((tm, 2 * HALF_W), lambda i: (i, 0)), wspec, tile, vec, vec],
        out_shape=[_sds((n, PROJ_W), BF16), _sds((SG_GROUPS, SG_CHUNK, SG_CHUNK), F32),
                   _sds((SG_CHUNK, HALF_W), F32), _sds((1, HALF_W), F32), _sds((1, HALF_W), F32)],
    )(dsg, proj, proj, ln_g, ln_b, w_s, bias_tile)


CONV_K = 4
CONV_BLOCK = 256


def _shift_down(x, s, row):
    if s == 0:
        return x
    return jnp.where(row >= s, pltpu.roll(x, s, 0), 0.0)


def _shift_up(x, s, row):
    if s == 0:
        return x
    t_len = x.shape[0]
    return jnp.where(row < t_len - s, pltpu.roll(x, t_len - s, 0), 0.0)


def _conv_taps(x, row):
    return [_shift_down(x, CONV_K - 1 - j, row) for j in range(CONV_K)]


def _conv(taps, w):
    y = taps[0] * w[0:1, :]
    for j in range(1, CONV_K):
        y = y + taps[j] * w[j:j + 1, :]
    return y


def dn_conv_fwd(proj3, conv_w, name):
    b, t, _ = proj3.shape
    nblk = 3 * HALF_W // CONV_BLOCK
    first = 2 * HALF_W // CONV_BLOCK
    n_norm = 2 * HALF_W // CONV_BLOCK

    def body(x_ref, w_ref, o_ref):
        s = pl.program_id(1)
        x = x_ref[0]
        y = _conv(_conv_taps(x, _iota2(x.shape, 0)), w_ref[...])
        y = y * _sigmoid(y)

        @pl.when(s < n_norm)
        def _():
            for h in range(CONV_BLOCK // HEAD_DIM):
                cs = slice(h * HEAD_DIM, (h + 1) * HEAD_DIM)
                yh = y[:, cs]
                o_ref[0, :, cs] = yh * lax.rsqrt(jnp.sum(yh * yh, axis=-1, keepdims=True) + EPS)

        @pl.when(s >= n_norm)
        def _():
            o_ref[0] = y

    return _call(
        body, name=name, grid=(b, nblk),
        in_specs=[pl.BlockSpec((1, t, CONV_BLOCK), lambda i, s: (i, 0, first + s)),
                  pl.BlockSpec((CONV_K, CONV_BLOCK), lambda i, s: (0, s))],
        out_specs=pl.BlockSpec((1, t, CONV_BLOCK), lambda i, s: (i, 0, s)),
        out_shape=_sds((b, t, 3 * HALF_W), F32),
    )(proj3, conv_w)


def dn_conv_bwd(dqkv, proj3, conv_w, dproj3, name, comm=None):
    b, t, _ = proj3.shape
    nblk = 3 * HALF_W // CONV_BLOCK
    first = 2 * HALF_W // CONV_BLOCK
    n_norm = 2 * HALF_W // CONV_BLOCK

    def body(d_ref, x_ref, w_ref, dproj_in, dx_ref, dw_ref, ds_ref):
        s = pl.program_id(0)

        @pl.when(pl.program_id(1) == 0)
        def _():
            dw_ref[...] = jnp.zeros_like(dw_ref)

        x = x_ref[0]
        w = w_ref[...]
        row = _iota2(x.shape, 0)
        taps = _conv_taps(x, row)
        c = _conv(taps, w)
        sg = _sigmoid(c)
        y = c * sg

        @pl.when(s < n_norm)
        def _():
            for h in range(CONV_BLOCK // HEAD_DIM):
                cs = slice(h * HEAD_DIM, (h + 1) * HEAD_DIM)
                yh = y[:, cs]
                r = lax.rsqrt(jnp.sum(yh * yh, axis=-1, keepdims=True) + EPS)
                nh = yh * r
                dn = d_ref[0, :, cs]
                ds_ref[:, cs] = r * (dn - nh * jnp.sum(dn * nh, axis=-1, keepdims=True))

        @pl.when(s >= n_norm)
        def _():
            ds_ref[...] = d_ref[0]

        dc = ds_ref[...] * (sg * (1.0 + c * (1.0 - sg)))
        dx = _shift_up(dc, CONV_K - 1, row) * w[0:1, :]
        for j in range(1, CONV_K):
            dx = dx + _shift_up(dc, CONV_K - 1 - j, row) * w[j:j + 1, :]
        dx_ref[0] = dx.astype(BF16)
        for j in range(CONV_K):
            dw_ref[j:j + 1, :] += jnp.sum(dc * taps[j], axis=0, keepdims=True)

    return _call(
        body, name=name, grid=(nblk, b),
        in_specs=[pl.BlockSpec((1, t, CONV_BLOCK), lambda s, i: (i, 0, s)),
                  pl.BlockSpec((1, t, CONV_BLOCK), lambda s, i: (i, 0, first + s)),
                  pl.BlockSpec((CONV_K, CONV_BLOCK), lambda s, i: (0, s)), _ANY],
        out_specs=[pl.BlockSpec((1, t, CONV_BLOCK), lambda s, i: (i, 0, first + s)),
                   pl.BlockSpec((CONV_K, CONV_BLOCK), lambda s, i: (0, s))],
        out_shape=[_sds(dproj3.shape, BF16), _sds((CONV_K, 3 * HALF_W), F32)],
        scratch=[pltpu.VMEM((t, CONV_BLOCK), F32)],
        input_output_aliases={3: 0}, comm=comm,
    )(dqkv, proj3, conv_w, dproj3)


def _chunk_masks():
    ii = _iota2((DN_CHUNK, DN_CHUNK), 0)
    jj = _iota2((DN_CHUNK, DN_CHUNK), 1)
    return ii >= jj, ii > jj, ii == jj


LOCKSTEP_CHUNKS = 4


def _inv_unit_lower_many(l_mats, eye):
    eye_f = jnp.where(eye, 1.0, 0.0)
    ps = [-l for l in l_mats]
    ts = [eye_f + p for p in ps]
    pss = [_split(p) for p in ps]
    size = 2
    while size < DN_CHUNK:
        ps = [_dot3(s, s) for s in pss]
        pss = [_split(p) for p in ps]
        ts = [t + _dot3(_split(t), s) for t, s in zip(ts, pss)]
        size *= 2
    return ts


def _gates(pba, ea_row, dtb_row):
    beta = _sigmoid(pba)
    g = -ea_row * _softplus(pba + dtb_row)
    return beta, g


def _chunk_decay(gcol):
    incl, strict, eye = _chunk_masks()
    grow = jnp.sum(jnp.where(eye, gcol, 0.0), axis=0, keepdims=True)
    decay = jnp.where(incl, jnp.exp(jnp.where(incl, gcol - grow, 0.0)), 0.0)
    return decay, incl, strict, eye


def dn_chunk_fwd(qkv, proj3, alog_row, dtb_row, name, comm=None):
    b, t, _ = qkv.shape
    rblk = min(256, t)
    n_in = rblk // DN_CHUNK

    def body(q_ref, k_ref, v_ref, pba_ref, al_ref, dtb_ref,
             u_ref, w_ref, qd_ref, kd_ref, qk_ref, ti_ref, gc_ref):
        ea = jnp.exp(al_ref[...])
        tri = jnp.where(_chunk_masks()[0], 1.0, 0.0)

        _, strict, eye = _chunk_masks()

        def chunk_group(cg, carry):
            items = []
            for sub in range(LOCKSTEP_CHUNKS):
                rows = pl.ds(pl.multiple_of((cg * LOCKSTEP_CHUNKS + sub) * DN_CHUNK, DN_CHUNK), DN_CHUNK)
                beta_all, g_all = _gates(pba_ref[0, rows, :], ea, dtb_ref[...])
                gc = _dot_exact_lhs(tri, g_all)
                gc_ref[0, rows, :] = gc
                for h in range(N_HEADS):
                    items.append((rows, h, beta_all[:, h:h + 1], gc[:, N_HEADS + h:N_HEADS + h + 1]))
            ks, kbs, decays, egs = [], [], [], []
            for rows, h, beta, gcol in items:
                cs = slice(h * HEAD_DIM, (h + 1) * HEAD_DIM)
                k = k_ref[0, rows, cs]
                ks.append(k)
                kbs.append(k * beta)
                decays.append(_chunk_decay(gcol)[0])
                egs.append(jnp.exp(gcol))
            ms = [_bdot(kb, k, NT) for kb, k in zip(kbs, ks)]
            tinvs = _inv_unit_lower_many([jnp.where(strict, m * dc, 0.0) for m, dc in zip(ms, decays)], eye)
            tsps = [_split(t) for t in tinvs]
            for (rows, h, beta, gcol), tsp, tinv in zip(items, tsps, tinvs):
                cs = slice(h * HEAD_DIM, (h + 1) * HEAD_DIM)
                u_ref[0, rows, cs] = _dot3(tsp, _split(v_ref[0, rows, cs] * beta))
                ti_ref[0, h, rows, :] = tinv
            for (rows, h, beta, gcol), tsp, kb, eg in zip(items, tsps, kbs, egs):
                cs = slice(h * HEAD_DIM, (h + 1) * HEAD_DIM)
                w_ref[0, rows, cs] = _dot3(tsp, _split(kb * eg))
            for (rows, h, beta, gcol), k, dc, eg in zip(items, ks, decays, egs):
                cs = slice(h * HEAD_DIM, (h + 1) * HEAD_DIM)
                q = q_ref[0, rows, cs] * QK_SCALE
                qk_ref[0, h, rows, :] = _bdot(q, k, NT) * dc
                qd_ref[0, rows, cs] = q * eg
                kd_ref[0, rows, cs] = k * jnp.exp(gcol[DN_CHUNK - 1:DN_CHUNK, :] - gcol)
            return carry

        lax.fori_loop(0, n_in // LOCKSTEP_CHUNKS, chunk_group, 0)

    def seg(cblk):
        return pl.BlockSpec((1, rblk, HALF_W), lambda i, r: (i, r, cblk))

    vec = pl.BlockSpec((1, LANES), lambda i, r: (0, 0))
    wide = pl.BlockSpec((1, rblk, HALF_W), lambda i, r: (i, r, 0))
    sq = pl.BlockSpec((1, N_HEADS, rblk, DN_CHUNK), lambda i, r: (i, 0, r, 0))
    return _call(
        body, name=name, grid=(b, t // rblk),
        in_specs=[seg(0), seg(1), seg(2),
                  pl.BlockSpec((1, rblk, LANES), lambda i, r: (i, r, GATE_COL_BLOCK)), vec, vec],
        out_specs=[wide, wide, wide, wide, sq, sq,
                   pl.BlockSpec((1, rblk, LANES), lambda i, r: (i, r, 0))],
        out_shape=[_sds((b, t, HALF_W), F32)] * 4
        + [_sds((b, N_HEADS, t, DN_CHUNK), F32)] * 2 + [_sds((b, t, LANES), F32)], comm=comm,
    )(qkv, qkv, qkv, proj3, alog_row, dtb_row)


def dn_scan_fwd(u, w, qd, kd, qk, gc, name):
    b, t, _ = u.shape
    nc = t // DN_CHUNK
    bh = b * N_HEADS

    def body(u_ref, w_ref, qd_ref, kd_ref, qk_ref, gc_ref, o_ref, sin_ref, s_ref):
        @pl.when(pl.program_id(0) == 0)
        def _():
            s_ref[...] = jnp.zeros_like(s_ref)

        items = [(bi, h, slice(h * HEAD_DIM, (h + 1) * HEAD_DIM)) for bi in range(b) for h in range(N_HEADS)]
        sbs = []
        for bi, h, cs in items:
            s = s_ref[bi * N_HEADS + h]
            sin_ref[0, bi * N_HEADS + h] = s
            sbs.append(s.astype(BF16))
        ws = [_bdot(w_ref[bi, :, cs], sb, NN) for (bi, h, cs), sb in zip(items, sbs)]
        qs = [_bdot(qd_ref[bi, :, cs], sb, NN) for (bi, h, cs), sb in zip(items, sbs)]
        vbs = [(u_ref[bi, :, cs] - wsi).astype(BF16) for (bi, h, cs), wsi in zip(items, ws)]
        for (bi, h, cs), qsi, vb in zip(items, qs, vbs):
            o_ref[bi, :, cs] = qsi + _bdot(qk_ref[bi, h], vb, NN)
        for (bi, h, cs), vb in zip(items, vbs):
            gl = jnp.exp(gc_ref[bi, DN_CHUNK - 1:DN_CHUNK, N_HEADS + h:N_HEADS + h + 1])
            idx = bi * N_HEADS + h
            s_ref[idx] = s_ref[idx] * gl + _bdot(kd_ref[bi, :, cs], vb, TN)

    wide = pl.BlockSpec((b, DN_CHUNK, HALF_W), lambda c: (0, c, 0))
    return _call(
        body, name=name, grid=(nc,),
        in_specs=[wide, wide, wide, wide,
                  pl.BlockSpec((b, N_HEADS, DN_CHUNK, DN_CHUNK), lambda c: (0, 0, c, 0)),
                  pl.BlockSpec((b, DN_CHUNK, LANES), lambda c: (0, c, 0))],
        out_specs=[wide, pl.BlockSpec((1, bh, HEAD_DIM, HEAD_DIM), lambda c: (c, 0, 0, 0))],
        out_shape=[_sds((b, t, HALF_W), F32), _sds((nc, bh, HEAD_DIM, HEAD_DIM), F32)],
        scratch=[pltpu.VMEM((bh, HEAD_DIM, HEAD_DIM), F32)],
    )(u, w, qd, kd, qk, gc)


def dn_scan_bwd(do, u, w, qd, kd, qk, gc, s_in, name):
    b, t, _ = u.shape
    nc = t // DN_CHUNK
    bh = b * N_HEADS

    def body(do_ref, u_ref, w_ref, qd_ref, kd_ref, qk_ref, gc_ref, sin_ref,
             du_ref, dw_ref, dqd_ref, dkd_ref, dqk_ref, dgc_ref, ds_ref):
        @pl.when(pl.program_id(0) == 0)
        def _():
            ds_ref[...] = jnp.zeros_like(ds_ref)

        last_row = _iota2((DN_CHUNK, LANES), 0) == DN_CHUNK - 1
        lane = _iota2((DN_CHUNK, LANES), 1)
        items = [(bi, h, slice(h * HEAD_DIM, (h + 1) * HEAD_DIM)) for bi in range(b) for h in range(N_HEADS)]
        sbs = [sin_ref[0, bi * N_HEADS + h].astype(BF16) for bi, h, cs in items]
        wvs = [w_ref[bi, :, cs].astype(BF16) for bi, h, cs in items]
        dovs = [do_ref[bi, :, cs].astype(BF16) for bi, h, cs in items]
        dsbs = [ds_ref[bi * N_HEADS + h].astype(BF16) for bi, h, cs in items]
        vbs = [(u_ref[bi, :, cs] - _dot(wv, sb, NN)).astype(BF16)
               for (bi, h, cs), wv, sb in zip(items, wvs, sbs)]
        for (bi, h, cs), dov, sb in zip(items, dovs, sbs):
            dqd_ref[bi, :, cs] = _dot(dov, sb, NT)
        dvns = [_dot(kd_ref[bi, :, cs].astype(BF16), dsb, NN) + _dot(qk_ref[bi, h].astype(BF16), dov, TN)
                for (bi, h, cs), dsb, dov in zip(items, dsbs, dovs)]
        for (bi, h, cs), vb, dsb, dov in zip(items, vbs, dsbs, dovs):
            dkd_ref[bi, :, cs] = _dot(vb, dsb, NT)
            dqk_ref[bi, h] = _dot(dov, vb, NT)
        dgls = []
        for (bi, h, cs), dvn, sb, wv, dov in zip(items, dvns, sbs, wvs, dovs):
            idx = bi * N_HEADS + h
            du_ref[bi, :, cs] = dvn
            dvn_b = dvn.astype(BF16)
            dw_ref[bi, :, cs] = -_dot(dvn_b, sb, NT)
            gl = jnp.exp(gc_ref[bi, DN_CHUNK - 1:DN_CHUNK, N_HEADS + h:N_HEADS + h + 1])
            ds = ds_ref[idx]
            dgl = jnp.sum(jnp.sum(ds * sin_ref[0, idx], axis=1, keepdims=True), axis=0, keepdims=True)
            dgls.append(dgl * gl)
            ds_ref[idx] = (ds * gl + _dot(qd_ref[bi, :, cs].astype(BF16), dov, TN)
                           - _dot(wv, dvn_b, TN))
        for bi in range(b):
            dgc = jnp.zeros((DN_CHUNK, LANES), F32)
            for h in range(N_HEADS):
                dgc = dgc + jnp.where(jnp.logical_and(last_row, lane == N_HEADS + h),
                                      dgls[bi * N_HEADS + h], 0.0)
            dgc_ref[bi] = dgc

    def rev(c):
        return nc - 1 - c

    wide = pl.BlockSpec((b, DN_CHUNK, HALF_W), lambda c: (0, rev(c), 0))
    sq = pl.BlockSpec((b, N_HEADS, DN_CHUNK, DN_CHUNK), lambda c: (0, 0, rev(c), 0))
    gates = pl.BlockSpec((b, DN_CHUNK, LANES), lambda c: (0, rev(c), 0))
    return _call(
        body, name=name, grid=(nc,),
        in_specs=[wide, wide, wide, wide, wide, sq, gates,
                  pl.BlockSpec((1, bh, HEAD_DIM, HEAD_DIM), lambda c: (rev(c), 0, 0, 0))],
        out_specs=[wide, wide, wide, wide, sq, gates],
        out_shape=[_sds((b, t, HALF_W), F32)] * 4
        + [_sds((b, N_HEADS, t, DN_CHUNK), F32), _sds((b, t, LANES), F32)],
        scratch=[pltpu.VMEM((bh, HEAD_DIM, HEAD_DIM), F32)],
    )(do, u, w, qd, kd, qk, gc, s_in)


def dn_chunk_bwd(qkv, proj3, alog_row, dtb_row, tinv, u, w, du, dw, dqd, dkd, dqk, dgc_scan, dproj3, name,
                 comm=None):
    b, t, _ = qkv.shape
    rblk = min(256, t)
    n_in = rblk // DN_CHUNK

    def body(q_ref, k_ref, v_ref, pba_ref, al_ref, dtb_ref, ti_ref, u_ref, w_ref,
             du_ref, dw_ref, dqd_ref, dkd_ref, dqk_ref, dgs_ref, dproj_in,
             dq_ref, dpba_ref, dal_ref, ddtb_ref):
        @pl.when(jnp.logical_and(pl.program_id(0) == 0, pl.program_id(1) == 0))
        def _():
            dal_ref[...] = jnp.zeros_like(dal_ref)
            ddtb_ref[...] = jnp.zeros_like(ddtb_ref)

        ea = jnp.exp(al_ref[...])
        incl0 = _chunk_masks()[0]
        tri = jnp.where(incl0, 1.0, 0.0)
        tri_up = jnp.where(_iota2((DN_CHUNK, DN_CHUNK), 1) >= _iota2((DN_CHUNK, DN_CHUNK), 0), 1.0, 0.0)
        lane = _iota2((DN_CHUNK, LANES), 1)
        last_col = _iota2((DN_CHUNK, 1), 0) == DN_CHUNK - 1

        _, strict, _ = _chunk_masks()
        gate_lane = jnp.logical_and(lane >= N_HEADS, lane < 2 * N_HEADS)

        def chunk_group(cg, carry):
            tiles, items = [], []
            for sub in range(LOCKSTEP_CHUNKS):
                rows = pl.ds(pl.multiple_of((cg * LOCKSTEP_CHUNKS + sub) * DN_CHUNK, DN_CHUNK), DN_CHUNK)
                pba = pba_ref[0, rows, :]
                beta_all, g_all = _gates(pba, ea, dtb_ref[...])
                gc = _dot_exact_lhs(tri, g_all)
                tiles.append((rows, pba, beta_all, g_all))
                for h in range(N_HEADS):
                    items.append((sub, rows, h, slice(h * HEAD_DIM, (h + 1) * HEAD_DIM),
                                  beta_all[:, h:h + 1], gc[:, N_HEADS + h:N_HEADS + h + 1]))
            decays = [_chunk_decay(gcol)[0] for _, _, _, _, _, gcol in items]
            egs = [jnp.exp(gcol) for _, _, _, _, _, gcol in items]
            qbs = [(q_ref[0, rows, cs] * QK_SCALE).astype(BF16) for _, rows, h, cs, _, _ in items]
            kfs = [k_ref[0, rows, cs].astype(BF16) for _, rows, h, cs, _, _ in items]
            kbs = [k_ref[0, rows, cs] * beta for _, rows, h, cs, beta, _ in items]
            kbbs = [kb.astype(BF16) for kb in kbs]
            tsps = [_split(ti_ref[0, h, rows, :]) for _, rows, h, cs, _, _ in items]
            drus = [_dot3(tsp, _split(du_ref[0, rows, cs]), TN)
                    for (_, rows, h, cs, _, _), tsp in zip(items, tsps)]
            drws = [_dot3(tsp, _split(dw_ref[0, rows, cs]), TN)
                    for (_, rows, h, cs, _, _), tsp in zip(items, tsps)]
            m_kks = [_dot(kbb, kf, NT) for kbb, kf in zip(kbbs, kfs)]
            a_qks = [_dot(qb, kf, NT) for qb, kf in zip(qbs, kfs)]
            dls = [-jnp.where(strict, _dot3(_split(dru), _split(u_ref[0, rows, cs]), NT)
                              + _dot3(_split(drw), _split(w_ref[0, rows, cs]), NT), 0.0)
                   for (_, rows, h, cs, _, _), dru, drw in zip(items, drus, drws)]
            dms = [(dl * dc).astype(BF16) for dl, dc in zip(dls, decays)]
            das = [(dqk_ref[0, h, rows, :] * dc).astype(BF16)
                   for (_, rows, h, cs, _, _), dc in zip(items, decays)]
            dkb_mm = [_dot(dm, kf, NN) for dm, kf in zip(dms, kfs)]
            dk_mm = [_dot(dm, kbb, TN) + _dot(da, qb, TN) for dm, kbb, da, qb in zip(dms, kbbs, das, qbs)]
            dqs_mm = [_dot(da, kf, NN) for da, kf in zip(das, kfs)]
            dgc_tiles = [dgs_ref[0, rows, :] for rows, _, _, _ in tiles]
            dbeta_tiles = [jnp.zeros((DN_CHUNK, LANES), F32) for _ in tiles]
            for n_it, (sub, rows, h, cs, beta, gcol) in enumerate(items):
                eg, dc = egs[n_it], decays[n_it]
                k = k_ref[0, rows, cs]
                q = q_ref[0, rows, cs] * QK_SCALE
                kb, dru, drw = kbs[n_it], drus[n_it], drws[n_it]
                ek = jnp.exp(gcol[DN_CHUNK - 1:DN_CHUNK, :] - gcol)
                e_mat = (dls[n_it] * m_kks[n_it] + dqk_ref[0, h, rows, :] * a_qks[n_it]) * dc
                dkb = drw * eg + dkb_mm[n_it]
                dqd = dqd_ref[0, rows, cs]
                dkd = dkd_ref[0, rows, cs]
                kdk = dkd * k * ek
                kdk_total = jnp.sum(jnp.sum(kdk, axis=0, keepdims=True), axis=1, keepdims=True)
                dg = (jnp.sum(drw * kb * eg + dqd * q * eg - kdk, axis=-1, keepdims=True)
                      + jnp.sum(e_mat, axis=1, keepdims=True)
                      - _row_to_col(jnp.sum(e_mat, axis=0, keepdims=True))
                      + jnp.where(last_col, kdk_total, 0.0))
                dbeta = jnp.sum(dkb * k + dru * v_ref[0, rows, cs], axis=-1, keepdims=True)
                dq_ref[0, rows, cs] = (dqs_mm[n_it] + dqd * eg) * QK_SCALE
                dq_ref[0, rows, pl.ds(HALF_W + h * HEAD_DIM, HEAD_DIM)] = dk_mm[n_it] + dkd * ek + dkb * beta
                dq_ref[0, rows, pl.ds(2 * HALF_W + h * HEAD_DIM, HEAD_DIM)] = dru * beta
                dgc_tiles[sub] = dgc_tiles[sub] + jnp.where(lane == N_HEADS + h, dg, 0.0)
                dbeta_tiles[sub] = dbeta_tiles[sub] + jnp.where(lane == h, dbeta, 0.0)
            for (rows, pba, beta_all, g_all), dgc_tile, dbeta_tile in zip(tiles, dgc_tiles, dbeta_tiles):
                dg_tile = _dot_exact_lhs(tri_up, dgc_tile)
                da_pre = dg_tile * (-ea) * _sigmoid(pba + dtb_ref[...])
                dal_ref[...] += jnp.sum(jnp.where(gate_lane, dg_tile * g_all, 0.0), axis=0, keepdims=True)
                ddtb_ref[...] += jnp.sum(jnp.where(gate_lane, da_pre, 0.0), axis=0, keepdims=True)
                dpba_ref[0, rows, :] = jnp.where(lane < N_HEADS, dbeta_tile * beta_all * (1.0 - beta_all),
                                                 jnp.where(gate_lane, da_pre, 0.0)).astype(BF16)
            return carry

        lax.fori_loop(0, n_in // LOCKSTEP_CHUNKS, chunk_group, 0)

    def seg(cblk):
        return pl.BlockSpec((1, rblk, HALF_W), lambda i, r: (i, r, cblk))

    vec = pl.BlockSpec((1, LANES), lambda i, r: (0, 0))
    wide = pl.BlockSpec((1, rblk, HALF_W), lambda i, r: (i, r, 0))
    sq = pl.BlockSpec((1, N_HEADS, rblk, DN_CHUNK), lambda i, r: (i, 0, r, 0))
    gates = pl.BlockSpec((1, rblk, LANES), lambda i, r: (i, r, 0))
    return _call(
        body, name=name, grid=(b, t // rblk),
        in_specs=[seg(0), seg(1), seg(2),
                  pl.BlockSpec((1, rblk, LANES), lambda i, r: (i, r, GATE_COL_BLOCK)), vec, vec,
                  sq, wide, wide, wide, wide, wide, wide, sq, gates, _ANY],
        out_specs=[pl.BlockSpec((1, rblk, 3 * HALF_W), lambda i, r: (i, r, 0)),
                   pl.BlockSpec((1, rblk, LANES), lambda i, r: (i, r, GATE_COL_BLOCK)), vec, vec],
        out_shape=[_sds((b, t, 3 * HALF_W), F32), _sds(dproj3.shape, BF16),
                   _sds((1, LANES), F32), _sds((1, LANES), F32)],
        input_output_aliases={15: 1}, comm=comm,
    )(qkv, qkv, qkv, proj3, alog_row, dtb_row, tinv, u, w, du, dw, dqd, dkd, dqk, dgc_scan, dproj3)


def dn_out_fwd(o, proj, dn_norm, name):
    n = o.shape[0]
    tm = min(ROW_TILE, n)

    def body(o_ref, z_ref, g_ref, y_ref):
        for h in range(N_HEADS):
            cs = slice(h * HEAD_DIM, (h + 1) * HEAD_DIM)
            oh = o_ref[:, cs]
            z = z_ref[:, cs]
            y = oh * _rms_scale(oh) * g_ref[...]
            y_ref[:, cs] = (y * (z * _sigmoid(z))).astype(BF16)

    half = pl.BlockSpec((tm, HALF_W), lambda i: (i, 0))
    return _call(
        body, name=name, grid=(n // tm,),
        in_specs=[half, pl.BlockSpec((tm, HALF_W), lambda i: (i, 5)),
                  pl.BlockSpec((1, HEAD_DIM), lambda i: (0, 0))],
        out_specs=half, out_shape=_sds((n, HALF_W), BF16),
    )(o, proj, dn_norm)


def dn_out_bwd(dy, o, proj, dn_norm, dproj, name):
    n = o.shape[0]
    tm = min(ROW_TILE, n)

    def body(dy_ref, o_ref, z_ref, g_ref, dproj_in, do_ref, dz_ref, dg_ref):
        @pl.when(pl.program_id(0) == 0)
        def _():
            dg_ref[...] = jnp.zeros_like(dg_ref)

        g = g_ref[...]
        dg = jnp.zeros_like(g)
        for h in range(N_HEADS):
            cs = slice(h * HEAD_DIM, (h + 1) * HEAD_DIM)
            oh = o_ref[:, cs]
            z = z_ref[:, cs]
            d = dy_ref[:, cs]
            r = _rms_scale(oh)
            nh = oh * r
            sz = _sigmoid(z)
            dyn = d * (z * sz)
            dz_ref[:, cs] = (d * (nh * g) * (sz * (1.0 + z * (1.0 - sz)))).astype(BF16)
            dg = dg + jnp.sum(dyn * nh, axis=0, keepdims=True)
            dn = dyn * g
            do_ref[:, cs] = r * (dn - nh * jnp.mean(dn * nh, axis=-1, keepdims=True))
        dg_ref[...] += dg

    half = pl.BlockSpec((tm, HALF_W), lambda i: (i, 0))
    vec = pl.BlockSpec((1, HEAD_DIM), lambda i: (0, 0))
    return _call(
        body, name=name, grid=(n // tm,),
        in_specs=[half, half, pl.BlockSpec((tm, HALF_W), lambda i: (i, 5)), vec, _ANY],
        out_specs=[half, pl.BlockSpec((tm, HALF_W), lambda i: (i, 5)), vec],
        out_shape=[_sds((n, HALF_W), F32), _sds(dproj.shape, BF16), _sds((1, HEAD_DIM), F32)],
        input_output_aliases={4: 1},
    )(dy, o, proj, dn_norm, dproj)


def _adamw_math(w, g, m, v):
    m_new = ADAM_B1 * m + (1.0 - ADAM_B1) * g
    v_new = ADAM_B2 * v + (1.0 - ADAM_B2) * (g * g)
    m_hat = m_new / (1.0 - ADAM_B1 ** ADAM_STEP)
    v_hat = v_new / (1.0 - ADAM_B2 ** ADAM_STEP)
    delta = -ADAM_LR * (m_hat / (jnp.sqrt(v_hat) + ADAM_EPS) + ADAM_WD * w)
    return delta, m_new, v_new


def adamw(w, g, m, v, name):
    r, c = w.shape
    tr = r
    for cand in (256, 352):
        if r % cand == 0 and r > cand:
            tr = cand
            break

    def body(w_ref, g_ref, m_ref, v_ref, d_ref, mo_ref, vo_ref):
        d, mn, vn = _adamw_math(w_ref[...], g_ref[...], m_ref[...], v_ref[...])
        d_ref[...] = d
        mo_ref[...] = mn
        vo_ref[...] = vn

    spec = pl.BlockSpec((tr, c), lambda i: (i, 0))
    return _call(
        body, name=name, grid=(r // tr,),
        in_specs=[spec] * 4, out_specs=[spec] * 3, out_shape=[_sds((r, c), F32)] * 3,
    )(w, g, m, v)


def _place():
    return lax.axis_index("x"), lax.axis_index("y"), lax.axis_index("c")


def _other_chips(x, y):
    return [(1 - x, y), (x, 1 - y), (1 - x, 1 - y)]


_ANY = pl.BlockSpec(memory_space=pl.ANY)


def cast_place(w, shard_idx, name):
    r, cols = w.shape
    tr = r // 2

    def body(j_ref, w_ref, o_ref):
        o_ref[0] = w_ref[...].astype(BF16)

    return pl.pallas_call(
        body, name=name,
        grid_spec=pltpu.PrefetchScalarGridSpec(
            num_scalar_prefetch=1, grid=(r // tr,),
            in_specs=[pl.BlockSpec((tr, cols), lambda i, j: (i, 0))],
            out_specs=pl.BlockSpec((1, tr, cols), lambda i, j: (j[0], i, 0))),
        out_shape=_sds((N_SHARD, r, cols), BF16),
        compiler_params=pltpu.CompilerParams(dimension_semantics=("arbitrary",),
                                             vmem_limit_bytes=VMEM_LIMIT),
    )(shard_idx, w)


class Exchange:
    def __init__(self, inputs, out_shape, aliases, sems, phases):
        self.inputs, self.out_shape, self.aliases = list(inputs), list(out_shape), dict(aliases)
        self.sems, self.phases = list(sems), list(phases)


def run_exchange(ex, name):
    def body(*refs):
        n_in, n_out = len(ex.inputs), len(ex.out_shape)
        for _, fn in ex.phases:
            fn(refs[:n_in], refs[n_in:n_in + n_out], refs[n_in + n_out:])

    return _call(body, name=name, in_specs=[_ANY] * len(ex.inputs), out_specs=[_ANY] * len(ex.out_shape),
                 out_shape=ex.out_shape, scratch=ex.sems, input_output_aliases=ex.aliases)(*ex.inputs)


def merge_exchanges(exs):
    inputs, out_shape, sems, aliases, phases, out_slices = [], [], [], {}, [], []
    for ex in exs:
        i0, o0, s0 = len(inputs), len(out_shape), len(sems)
        inputs += ex.inputs
        out_shape += ex.out_shape
        sems += ex.sems
        for k, m in ex.aliases.items():
            aliases[i0 + k] = o0 + m
        si, so, ss = slice(i0, len(inputs)), slice(o0, len(out_shape)), slice(s0, len(sems))
        out_slices.append(so)
        for step, fn in ex.phases:
            phases.append((step, lambda ins, outs, sm, fn=fn, si=si, so=so, ss=ss: fn(ins[si], outs[so], sm[ss])))
    return Exchange(inputs, out_shape, aliases, sems, phases), out_slices


def _dma_sems(*sizes):
    return [pltpu.SemaphoreType.DMA((s,)) for s in sizes]


def gather_exchange(bufs, small=None, relay_step=-2):
    n = len(bufs)
    n_small = 0 if small is None else 1

    def half(outs, a, blk, hc):
        rh = bufs[a].shape[1] // 2
        return outs[a].at[blk, pl.ds(hc * rh, rh), :]

    def ici(outs, sems, a, k, blk, to):
        return pltpu.make_async_remote_copy(
            src_ref=half(outs, a, blk, to[2]), dst_ref=half(outs, a, blk, to[2]), send_sem=sems[0].at[3 * a + k],
            recv_sem=sems[1].at[3 * a + k], device_id=to, device_id_type=MESH)

    def d2d(outs, sems, a, k, blk, hc, to):
        return pltpu.make_async_remote_copy(
            src_ref=half(outs, a, blk, hc), dst_ref=half(outs, a, blk, hc), send_sem=sems[2].at[3 * a + k],
            recv_sem=sems[3].at[3 * a + k], device_id=to, device_id_type=MESH)

    def small_copy(ins, outs, sems, k, blk, to):
        return pltpu.make_async_remote_copy(
            src_ref=ins[n], dst_ref=outs[n].at[blk], send_sem=sems[0].at[3 * n + k],
            recv_sem=sems[1].at[3 * n + k], device_id=to, device_id_type=MESH)

    def start(ins, outs, sems):
        x, y, c = _place()
        j = 2 * x + y
        if n_small:
            pltpu.make_async_copy(ins[n], outs[n].at[j], sems[4].at[0]).start()
        for k, (px, py) in enumerate(_other_chips(x, y)):
            if n_small:
                small_copy(ins, outs, sems, k, j, (px, py, c)).start()
            for a in range(n):
                ici(outs, sems, a, k, j, (px, py, c)).start()

    def relay(ins, outs, sems):
        x, y, c = _place()
        for k, (px, py) in enumerate(_other_chips(x, y)):
            for a in range(n):
                ici(outs, sems, a, k, 2 * px + py, (px, py, c)).wait_recv()
                d2d(outs, sems, a, k, 2 * px + py, c, (x, y, 1 - c)).start()

    def finish(ins, outs, sems):
        x, y, c = _place()
        j = 2 * x + y
        for k, (px, py) in enumerate(_other_chips(x, y)):
            blk = 2 * px + py
            if n_small:
                small_copy(ins, outs, sems, k, blk, (px, py, c)).wait_recv()
                small_copy(ins, outs, sems, k, j, (px, py, c)).wait_send()
            for a in range(n):
                d2d(outs, sems, a, k, blk, 1 - c, (x, y, 1 - c)).wait_recv()
                ici(outs, sems, a, k, j, (px, py, c)).wait_send()
                d2d(outs, sems, a, k, blk, c, (x, y, 1 - c)).wait_send()
        if n_small:
            pltpu.make_async_copy(ins[n], outs[n].at[j], sems[4].at[0]).wait()

    out_shape = [_sds(b.shape, b.dtype) for b in bufs]
    if n_small:
        out_shape.append(_sds((N_SHARD,) + small.shape, small.dtype))
    return Exchange(list(bufs) + ([small] if n_small else []), out_shape, {a: a for a in range(n)},
                    _dma_sems(3 * n + 3, 3 * n + 3, 3 * n, 3 * n, 1),
                    [(0, start), (relay_step, relay), (-1, finish)])


def _start_then_wait(copies):
    def start(ins, outs, sems):
        for sent, _ in copies(ins, outs, sems):
            sent().start()

    def finish(ins, outs, sems):
        pairs = copies(ins, outs, sems)
        for _, arrival in pairs:
            arrival().wait_recv()
        for sent, _ in pairs:
            sent().wait_send()

    return [(0, start), (-1, finish)]


def pair_exchange(arrs):
    n = len(arrs)

    def copies(ins, outs, sems):
        x, y, c = _place()
        res = []
        for a in range(n):
            def mk(a=a):
                rh = arrs[a].shape[1] // 2
                return pltpu.make_async_remote_copy(
                    src_ref=ins[a].at[:, pl.ds((1 - c) * rh, rh), :], dst_ref=outs[a], send_sem=sems[0].at[a],
                    recv_sem=sems[1].at[a], device_id=(x, y, 1 - c), device_id_type=MESH)
            res.append((mk, mk))
        return res

    return Exchange(arrs, [_sds((a.shape[0], a.shape[1] // 2, a.shape[2]), a.dtype) for a in arrs], {},
                    _dma_sems(n, n), _start_then_wait(copies))


def pair_add(g, s, c_idx, name):
    nb, r, cols = g.shape
    rh = r // 2

    def body(c_ref, g_ref, s_ref, o_ref):
        o_ref[...] = (g_ref[...] + s_ref[...]).astype(BF16)

    return pl.pallas_call(
        body, name=name,
        grid_spec=pltpu.PrefetchScalarGridSpec(
            num_scalar_prefetch=1, grid=(nb,),
            in_specs=[pl.BlockSpec((1, rh, cols), lambda j, c: (j, c[0], 0)),
                      pl.BlockSpec((1, rh, cols), lambda j, c: (j, 0, 0))],
            out_specs=pl.BlockSpec((1, rh, cols), lambda j, c: (j, 0, 0))),
        out_shape=_sds((nb, rh, cols), BF16),
        compiler_params=pltpu.CompilerParams(dimension_semantics=("arbitrary",),
                                             vmem_limit_bytes=VMEM_LIMIT),
    )(c_idx, g, s)


def chip_exchange(arrs):
    n = len(arrs)

    def copies(ins, outs, sems):
        x, y, c = _place()
        j = 2 * x + y
        res = []
        for a in range(n):
            for k, (px, py) in enumerate(_other_chips(x, y)):
                def mk(src_blk, dst_blk, a=a, k=k, to=(px, py, c)):
                    return pltpu.make_async_remote_copy(
                        src_ref=ins[a].at[src_blk], dst_ref=outs[a].at[dst_blk], send_sem=sems[0].at[3 * a + k],
                        recv_sem=sems[1].at[3 * a + k], device_id=to, device_id_type=MESH)
                res.append((functools.partial(mk, 2 * px + py, j), functools.partial(mk, j, 2 * px + py)))
        return res

    return Exchange(arrs, [_sds(a.shape, a.dtype) for a in arrs], {}, _dma_sems(3 * n, 3 * n),
                    _start_then_wait(copies))


def sum_chips(r, p, shard_idx, name):
    nb, rh, cols = r.shape
    tr = rh

    def body(j_ref, p_ref, *refs):
        o_ref = refs[nb]
        j = j_ref[0]
        acc = None
        for i in range(nb):
            term = jnp.where(j == i, p_ref[0], refs[i][0]).astype(F32)
            acc = term if acc is None else acc + term
        o_ref[...] = acc

    def slot(i):
        return pl.BlockSpec((1, tr, cols), lambda t, j: (jnp.where(j[0] == i, (i + 1) % nb, i), t, 0))

    return pl.pallas_call(
        body, name=name,
        grid_spec=pltpu.PrefetchScalarGridSpec(
            num_scalar_prefetch=1, grid=(rh // tr,),
            in_specs=[pl.BlockSpec((1, tr, cols), lambda t, j: (j[0], t, 0))] + [slot(i) for i in range(nb)],
            out_specs=pl.BlockSpec((tr, cols), lambda t, j: (t, 0))),
        out_shape=_sds((rh, cols), F32),
        compiler_params=pltpu.CompilerParams(dimension_semantics=("arbitrary",),
                                             vmem_limit_bytes=VMEM_LIMIT),
    )(shard_idx, p, *([r] * nb))


def pair_swap(arrs):
    n = len(arrs)

    def copies(ins, outs, sems):
        x, y, c = _place()
        res = []
        for a in range(n):
            def mk(a=a):
                return pltpu.make_async_remote_copy(
                    src_ref=ins[a], dst_ref=outs[a], send_sem=sems[0].at[a], recv_sem=sems[1].at[a],
                    device_id=(x, y, 1 - c), device_id_type=MESH)
            res.append((mk, mk))
        return res

    return Exchange(arrs, [_sds(a.shape, a.dtype) for a in arrs], {}, _dma_sems(n, n),
                    _start_then_wait(copies))


ADAMW_STEPS_PER_HALF = 4


def adamw_pairs(items, name, comm=None):
    n_items = len(items)
    nh = ADAMW_STEPS_PER_HALF

    def body(*refs):
        ins, outs = refs[:5 * n_items], refs[5 * n_items:]
        mine = (pl.program_id(0) // nh) == lax.axis_index("c")
        for a in range(n_items):
            w_ref, gm_ref, gs_ref, m_ref, v_ref = ins[5 * a:5 * a + 5]
            g_ref, d_ref, mo_ref, vo_ref = outs[4 * a:4 * a + 4]
            g = jnp.where(mine, gm_ref[...], gs_ref[...])
            d, mn, vn = _adamw_math(w_ref[...], g, m_ref[...], v_ref[...])
            g_ref[...] = g
            d_ref[...] = d
            mo_ref[...] = mn
            vo_ref[...] = vn

    in_specs, out_specs, out_shape, args = [], [], [], []
    for w, g_mine, g_sib, m, v in items:
        r, cols = w.shape
        tr = r // (2 * nh)
        full = pl.BlockSpec((tr, cols), lambda i: (i, 0))
        part = pl.BlockSpec((tr, cols), lambda i: (i % nh, 0))
        in_specs += [full, part, part, full, full]
        out_specs += [full] * 4
        out_shape += [_sds((r, cols), F32)] * 4
        args += [w, g_mine, g_sib, m, v]
    res = _call(body, name=name, grid=(2 * nh,), in_specs=in_specs, out_specs=out_specs,
                out_shape=out_shape, comm=comm)(*args)
    own, hosted = (res, None) if comm is None else res
    grouped = [tuple(own[4 * a:4 * a + 4]) for a in range(n_items)]
    return grouped if comm is None else (grouped, hosted)


N_DEV = 8


def device_gather(pack):
    def copies(ins, outs, sems):
        x, y, c = _place()
        me = 4 * x + 2 * y + c
        res = []
        for k in range(1, N_DEV):
            fx, fy, fc = (k >> 2) & 1, (k >> 1) & 1, k & 1
            px, py, pc = (1 - x if fx else x, 1 - y if fy else y, 1 - c if fc else c)

            def mk(slot, k=k, to=(px, py, pc)):
                return pltpu.make_async_remote_copy(
                    src_ref=ins[0], dst_ref=outs[0].at[slot], send_sem=sems[0].at[k - 1],
                    recv_sem=sems[1].at[k - 1], device_id=to, device_id_type=MESH)
            res.append((functools.partial(mk, me), functools.partial(mk, 4 * px + 2 * py + pc)))
        return res

    return Exchange([pack], [_sds((N_DEV,) + pack.shape, pack.dtype)], {}, _dma_sems(N_DEV - 1, N_DEV - 1),
                    _start_then_wait(copies))


def sum_devices(buf, pack, me_idx, name):
    r, cols = pack.shape

    def body(me_ref, p_ref, *refs):
        o_ref = refs[N_DEV]
        acc = None
        for i in range(N_DEV):
            term = jnp.where(me_ref[0] == i, p_ref[...], refs[i][0])
            acc = term if acc is None else acc + term
        o_ref[...] = acc

    def slot(i):
        return pl.BlockSpec((1, r, cols), lambda t, me: (jnp.where(me[0] == i, (i + 1) % N_DEV, i), 0, 0))

    whole = pl.BlockSpec((r, cols), lambda t, me: (0, 0))
    return pl.pallas_call(
        body, name=name,
        grid_spec=pltpu.PrefetchScalarGridSpec(
            num_scalar_prefetch=1, grid=(1,),
            in_specs=[whole] + [slot(i) for i in range(N_DEV)], out_specs=whole),
        out_shape=_sds((r, cols), F32),
        compiler_params=pltpu.CompilerParams(dimension_semantics=("arbitrary",),
                                             vmem_limit_bytes=VMEM_LIMIT),
    )(me_idx, pack, *([buf] * N_DEV))


SMALL_NAMES = ("ffn1_norm", "mix_norm", "ffn2_norm", "final_norm", "sg_ln_g", "sg_ln_b",
               "dn_norm", "a_log", "dt_bias", "sg_b", "sg_w", "conv_w", "loss")


def _to_rows(a):
    flat = a.reshape(-1)
    pad = (-flat.shape[0]) % LANES
    if pad:
        flat = jnp.pad(flat, (0, pad))
    return flat.reshape(-1, LANES)


def _pack_small(parts):
    rows = [_to_rows(parts[k]) for k in SMALL_NAMES]
    pack = jnp.concatenate(rows, axis=0)
    pad = (-pack.shape[0]) % 8
    if pad:
        pack = jnp.pad(pack, ((0, pad), (0, 0)))
    return pack


def _unpack_small(pack, shapes):
    out, r0 = {}, 0
    for k in SMALL_NAMES:
        size = 1
        for s in shapes[k]:
            size *= s
        nrows = -(-size // LANES)
        out[k] = pack[r0:r0 + nrows].reshape(-1)[:size].reshape(shapes[k])
        r0 += nrows
    return out


def kernel(x, ffn1_norm, ffn1_w_gate, ffn1_w_up, ffn1_w_down, mix_norm, w_in, conv_w, a_log, dt_bias, dn_norm, sg_ln_g, sg_ln_b, sg_w, sg_b, w_out, ffn2_norm, ffn2_w_gate, ffn2_w_up, ffn2_w_down, final_norm, loss_target, m_ffn1_norm, m_ffn1_w_gate, m_ffn1_w_up, m_ffn1_w_down, m_mix_norm, m_w_in, m_conv_w, m_a_log, m_dt_bias, m_dn_norm, m_sg_ln_g, m_sg_ln_b, m_sg_w, m_sg_b, m_w_out, m_ffn2_norm, m_ffn2_w_gate, m_ffn2_w_up, m_ffn2_w_down, m_final_norm, v_ffn1_norm, v_ffn1_w_gate, v_ffn1_w_up, v_ffn1_w_down, v_mix_norm, v_w_in, v_conv_w, v_a_log, v_dt_bias, v_dn_norm, v_sg_ln_g, v_sg_ln_b, v_sg_w, v_sg_b, v_w_out, v_ffn2_norm, v_ffn2_w_gate, v_ffn2_w_up, v_ffn2_w_down, v_final_norm):
    bsz, t_len, d = x.shape
    n = bsz * t_len
    xy, yy, cc = _place()
    shard = 2 * xy + yy

    big_names = ["ffn1_w_gate", "ffn1_w_up", "ffn1_w_down", "w_in", "w_out",
                 "ffn2_w_gate", "ffn2_w_up", "ffn2_w_down"]
    big_w = dict(ffn1_w_gate=ffn1_w_gate, ffn1_w_up=ffn1_w_up, ffn1_w_down=ffn1_w_down, w_in=w_in,
                 w_out=w_out, ffn2_w_gate=ffn2_w_gate, ffn2_w_up=ffn2_w_up, ffn2_w_down=ffn2_w_down)
    big_m = dict(ffn1_w_gate=m_ffn1_w_gate, ffn1_w_up=m_ffn1_w_up, ffn1_w_down=m_ffn1_w_down, w_in=m_w_in,
                 w_out=m_w_out, ffn2_w_gate=m_ffn2_w_gate, ffn2_w_up=m_ffn2_w_up, ffn2_w_down=m_ffn2_w_down)
    big_v = dict(ffn1_w_gate=v_ffn1_w_gate, ffn1_w_up=v_ffn1_w_up, ffn1_w_down=v_ffn1_w_down, w_in=v_w_in,
                 w_out=v_w_out, ffn2_w_gate=v_ffn2_w_gate, ffn2_w_up=v_ffn2_w_up, ffn2_w_down=v_ffn2_w_down)
    shard_idx = jnp.reshape(shard, (1,)).astype(jnp.int32)
    c_idx = jnp.reshape(cc, (1,)).astype(jnp.int32)
    transposed = ("ffn1_w_gate", "ffn1_w_up", "ffn2_w_gate", "ffn2_w_up")

    def as2d(a, k):
        return a[0].T if k in transposed else a[0]

    def from2d(a, k):
        return a.T[None] if k in transposed else a[None]

    placed = {k: cast_place(as2d(big_w[k], k), shard_idx, name="cast_" + k) for k in big_names}
    first_names = ["ffn1_w_gate", "ffn1_w_up"]
    second_names = ["ffn1_w_down", "w_in"]
    third_names = ["w_out", "ffn2_w_gate"]
    fourth_names = ["ffn2_w_up", "ffn2_w_down"]
    res = run_exchange(gather_exchange([placed[k] for k in first_names], conv_w[0]), name="gather_first")
    gw = dict(zip(first_names, res[:2]))
    conv_full = res[2].transpose(1, 0, 2).reshape(CONV_K, 3 * HALF_W)

    x0 = x.reshape(n, d)
    def ffn_weights(prefix):
        return [gw[prefix + k].reshape(-1, d) for k in ("_w_gate", "_w_up", "_w_down")]

    def ffn_grad_blocks(grads):
        return [g.reshape(N_SHARD, -1, d) for g in grads]

    (h1, gate1, up1, act1), second = ffn_fwd(
        x0, ffn1_norm, gw["ffn1_w_gate"].reshape(-1, d), gw["ffn1_w_up"].reshape(-1, d), None,
        name="ffn1_fwd", comm=gather_exchange([placed[k] for k in second_names]))
    gw.update(zip(second_names, second))
    (x1,) = ffn_down(x0, act1, gw["ffn1_w_down"].reshape(-1, d), name="ffn1_down")
    w_in_full = gw["w_in"].transpose(1, 0, 2).reshape(d, IN_COLS)
    w_in_full = jnp.pad(w_in_full, ((0, 0), (0, PROJ_W - IN_COLS)))
    (proj, h2), third = in_proj_fwd(x1, mix_norm, w_in_full, name="in_proj_fwd",
                                    comm=gather_exchange([placed[k] for k in third_names]))
    gw.update(zip(third_names, third))
    proj3 = proj.reshape(bsz, t_len, PROJ_W)
    bias_tile = jnp.repeat(sg_b[0].T, SG_GROUP_DIM, axis=1)
    sg_out = sg_fwd(proj, sg_ln_g, sg_ln_b, sg_w[0], bias_tile, name="sg_fwd")
    qkv = dn_conv_fwd(proj3, conv_full, name="dn_conv_fwd")
    alog_row = jnp.zeros((1, LANES), F32).at[0, N_HEADS:2 * N_HEADS].set(a_log[0])
    dtb_row = jnp.zeros((1, LANES), F32).at[0, N_HEADS:2 * N_HEADS].set(dt_bias[0])
    (u_wy, w_wy, q_dec, k_dec, qk, tinv, gc), fourth = dn_chunk_fwd(
        qkv, proj3, alog_row, dtb_row, name="dn_chunk_fwd",
        comm=gather_exchange([placed[k] for k in fourth_names]))
    gw.update(zip(fourth_names, fourth))
    w_out_full = gw["w_out"].reshape(2 * HALF_W, d)
    o, s_in = dn_scan_fwd(u_wy, w_wy, q_dec, k_dec, qk, gc, name="dn_scan_fwd")
    dn_out = dn_out_fwd(o.reshape(n, HALF_W), proj, dn_norm, name="dn_out_fwd")
    x2 = out_proj_fwd(x1, sg_out, dn_out, w_out_full, name="out_proj_fwd")
    x3, h3, gate2, up2, act2 = ffn_fwd(x2, ffn2_norm, *ffn_weights("ffn2"), name="ffn2_fwd")
    dx3, d_final_norm, loss_tile = final_loss(x3, final_norm.reshape(1, d),
                                              loss_target.reshape(n, d), name="final_loss")

    dx2, dgate2, dup2, dyh2, d_ffn2_norm = ffn_bwd_act(
        dx3, x2, ffn2_norm, gate2, up2, *ffn_weights("ffn2"), name="ffn2_bwd_act")
    g_big = {}
    g_big["ffn2_w_gate"], g_big["ffn2_w_up"], g_big["ffn2_w_down"] = ffn_grad_blocks(ffn_bwd_w(
        [dgate2, dup2, act2], [h3, dyh2], [(0, 0), (1, 0), (2, 1)], name="ffn2_bwd_w"))

    early = ["ffn2_w_gate", "ffn2_w_up", "ffn2_w_down"]
    (d_sg, d_dn, dx2b), early_sib = out_proj_bwd_x(dx2, w_out_full, name="out_proj_bwd_x",
                                                   comm=pair_exchange([g_big[k] for k in early]))
    early_sums = [pair_add(g_big[k], s, c_idx, name="grad_pair_add_" + k) for k, s in zip(early, early_sib)]
    g_w_out = matmul_tn([sg_out, dn_out], dx2b, d, name="w_out_grad")
    g_big["w_out"] = g_w_out.reshape(N_SHARD, (2 * HALF_W) // N_SHARD, d)

    d_proj, d_sg_w, d_bias_tile, d_ln_g, d_ln_b = sg_bwd(d_sg, proj, sg_ln_g, sg_ln_b, sg_w[0],
                                                         bias_tile, name="sg_bwd")
    d_o, d_proj, d_dn_norm = dn_out_bwd(d_dn, o.reshape(n, HALF_W), proj, dn_norm, d_proj,
                                        name="dn_out_bwd")
    du, dw, dqd, dkd, dqk, dgc_scan = dn_scan_bwd(d_o.reshape(bsz, t_len, HALF_W), u_wy, w_wy, q_dec,
                                                  k_dec, qk, gc, s_in, name="dn_scan_bwd")
    (d_qkv, d_proj3, d_alog_row, d_dtb_row), early_chips = dn_chunk_bwd(
        qkv, proj3, alog_row, dtb_row, tinv, u_wy, w_wy, du, dw, dqd, dkd, dqk, dgc_scan,
        d_proj.reshape(bsz, t_len, PROJ_W), name="dn_chunk_bwd", comm=chip_exchange(early_sums))
    early_halves = [sum_chips(r, p, shard_idx, name="grad_chip_sum_" + k)
                    for k, r, p in zip(early, early_chips, early_sums)]
    d_proj3, d_conv = dn_conv_bwd(d_qkv, proj3, conv_full, d_proj3, name="dn_conv_bwd")
    d_proj = d_proj3.reshape(n, PROJ_W)
    g_w_in = matmul_tn([h2], d_proj, 3 * MXU_DIM, name="w_in_grad")[:, :IN_COLS]
    g_big["w_in"] = g_w_in.reshape(d, N_SHARD, IN_COLS // N_SHARD).transpose(1, 0, 2)

    def reduce_start(names):
        return pair_exchange([g_big[k] for k in names])

    def reduce_pair_sums(names, from_sib):
        return [pair_add(g_big[k], s, c_idx, name="grad_pair_add_" + k) for k, s in zip(names, from_sib)]

    def reduce_chip_sums(names, from_chips, sums):
        return [sum_chips(r, p, shard_idx, name="grad_chip_sum_" + k)
                for k, r, p in zip(names, from_chips, sums)]

    mid = ["w_in", "w_out"]
    (dx1, d_mix_norm, dyh1), mid_sib = in_proj_bwd_x(d_proj, w_in_full, x1, mix_norm, dx2,
                                                     name="in_proj_bwd_x", comm=reduce_start(mid))
    mid_sums = reduce_pair_sums(mid, mid_sib)
    down = ["ffn1_w_down"]
    (g_down,), mid_chips = ffn_bwd_w([act1], [dyh1], [(0, 0)], name="ffn1_bwd_w_down",
                                     comm=chip_exchange(mid_sums))
    g_big["ffn1_w_down"] = g_down.reshape(N_SHARD, -1, d)
    mid_halves = reduce_chip_sums(mid, mid_chips, mid_sums)
    leg, legs = merge_exchanges([reduce_start(down), pair_swap(mid_halves), pair_swap(early_halves)])
    leg_res = run_exchange(leg, name="grad_pair_exchange_down")
    down_sums = reduce_pair_sums(down, leg_res[legs[0]])
    mid_sib_halves, early_sib_halves = leg_res[legs[1]], leg_res[legs[2]]

    dx0, dgate1, dup1, _, d_ffn1_norm = ffn_bwd_act(
        dx1, x0, ffn1_norm, gate1, up1, *ffn_weights("ffn1"), name="ffn1_bwd_act")
    grad_x = dx0.reshape(bsz, t_len, d)
    d_sg_b = d_bias_tile.reshape(SG_CHUNK, SG_GROUPS, SG_GROUP_DIM).sum(axis=-1).T
    small_g = dict(ffn1_norm=d_ffn1_norm, mix_norm=d_mix_norm, ffn2_norm=d_ffn2_norm,
                   final_norm=d_final_norm, sg_ln_g=d_ln_g, sg_ln_b=d_ln_b, dn_norm=d_dn_norm,
                   a_log=d_alog_row[:, N_HEADS:2 * N_HEADS], dt_bias=d_dtb_row[:, N_HEADS:2 * N_HEADS],
                   sg_b=d_sg_b, sg_w=d_sg_w, conv_w=d_conv, loss=loss_tile[:, :1])
    my_pack = _pack_small(small_g)
    hosted, parts = merge_exchanges([chip_exchange(down_sums), device_gather(my_pack)])
    late = ["ffn1_w_gate", "ffn1_w_up"]
    late_grads, hosted_res = ffn_bwd_w([dgate1, dup1], [h1], [(0, 0), (1, 0)], name="ffn1_bwd_w_gate_up",
                                       comm=hosted)
    g_big["ffn1_w_gate"], g_big["ffn1_w_up"] = ffn_grad_blocks(late_grads)
    down_halves = reduce_chip_sums(down, hosted_res[parts[0]], down_sums)
    (all_packs,) = hosted_res[parts[1]]

    leg, legs = merge_exchanges([reduce_start(late), pair_swap(down_halves)])
    leg_res = run_exchange(leg, name="grad_pair_exchange")
    pair_sums = reduce_pair_sums(late, leg_res[legs[0]])
    down_sib_halves = leg_res[legs[1]]

    def adam_items(names, mine, sib):
        return [(as2d(big_w[k], k), gm, gs, as2d(big_m[k], k), as2d(big_v[k], k))
                for k, gm, gs in zip(names, mine, sib)]

    outs = {}
    done = adamw_pairs(
        adam_items(early + mid + down, early_halves + mid_halves + down_halves,
                   list(early_sib_halves) + list(mid_sib_halves) + list(down_sib_halves)),
        name="adamw_early")
    from_chips = run_exchange(chip_exchange(pair_sums), name="grad_chip_exchange")
    halves = reduce_chip_sums(late, from_chips, pair_sums)
    sib_halves = run_exchange(pair_swap(halves), name="grad_pair_swap")
    done += adamw_pairs(adam_items(late, halves, sib_halves), name="adamw_late")
    for k, res in zip(early + mid + down + late, done):
        outs[k] = tuple(from2d(a, k) for a in res)

    small_w = dict(ffn1_norm=ffn1_norm, mix_norm=mix_norm, ffn2_norm=ffn2_norm, final_norm=final_norm,
                   sg_ln_g=sg_ln_g, sg_ln_b=sg_ln_b, dn_norm=dn_norm, a_log=a_log, dt_bias=dt_bias,
                   sg_b=sg_b, sg_w=sg_w)
    small_m = dict(ffn1_norm=m_ffn1_norm, mix_norm=m_mix_norm, ffn2_norm=m_ffn2_norm,
                   final_norm=m_final_norm, sg_ln_g=m_sg_ln_g, sg_ln_b=m_sg_ln_b, dn_norm=m_dn_norm,
                   a_log=m_a_log, dt_bias=m_dt_bias, sg_b=m_sg_b, sg_w=m_sg_w)
    small_v = dict(ffn1_norm=v_ffn1_norm, mix_norm=v_mix_norm, ffn2_norm=v_ffn2_norm,
                   final_norm=v_final_norm, sg_ln_g=v_sg_ln_g, sg_ln_b=v_sg_ln_b, dn_norm=v_dn_norm,
                   a_log=v_a_log, dt_bias=v_dt_bias, sg_b=v_sg_b, sg_w=v_sg_w)
    shapes = {k: small_w[k].shape for k in small_w}
    shapes["conv_w"] = (CONV_K, 3 * HALF_W)
    shapes["loss"] = (1, 1)
    me_idx = jnp.reshape(4 * xy + 2 * yy + cc, (1,)).astype(jnp.int32)
    g_pack = sum_devices(all_packs, my_pack, me_idx, name="small_sum")
    g_small = _unpack_small(g_pack, shapes)
    loss = g_small["loss"].reshape(())
    cw = 3 * HALF_W // N_SHARD
    g_conv = lax.dynamic_slice_in_dim(g_small["conv_w"], shard * cw, cw, axis=1)
    zero_conv = jnp.zeros((CONV_K, 3 * HALF_W), F32)

    def packed(src, conv):
        parts = dict(src)
        parts["conv_w"] = lax.dynamic_update_slice_in_dim(zero_conv, conv[0], shard * cw, axis=1)
        parts["loss"] = jnp.zeros((1, 1), F32)
        return _pack_small(parts)

    d_pack, m_pack, v_pack = adamw(packed(small_w, conv_w), g_pack, packed(small_m, m_conv_w),
                                   packed(small_v, v_conv_w), name="adamw_small")
    d_small = _unpack_small(d_pack, shapes)
    m_small = _unpack_small(m_pack, shapes)
    v_small = _unpack_small(v_pack, shapes)

    def conv_block(full_arr):
        return lax.dynamic_slice_in_dim(full_arr, shard * cw, cw, axis=1)[None]

    for k in small_w:
        outs[k] = (g_small[k].reshape(small_w[k].shape), d_small[k], m_small[k], v_small[k])
    outs["conv_w"] = (g_conv[None], conv_block(d_small["conv_w"]), conv_block(m_small["conv_w"]),
                      conv_block(v_small["conv_w"]))

    order = ["ffn1_norm", "ffn1_w_gate", "ffn1_w_up", "ffn1_w_down", "mix_norm", "w_in", "conv_w",
             "a_log", "dt_bias", "dn_norm", "sg_ln_g", "sg_ln_b", "sg_w", "sg_b", "w_out", "ffn2_norm",
             "ffn2_w_gate", "ffn2_w_up", "ffn2_w_down", "final_norm"]
    return (loss, grad_x, *[outs[k][0] for k in order], *[outs[k][1] for k in order],
            *[outs[k][2] for k in order], *[outs[k][3] for k in order])
```

```python
import functools

import jax
import jax.numpy as jnp
from jax import lax
from jax.experimental import pallas as pl
from jax.experimental.pallas import tpu as pltpu

F32 = jnp.float32
BF16 = jnp.bfloat16
EPS = 1e-6

D_MODEL = 1024
N_SHARD = 4
HEAD_DIM = 128
N_HEADS = 4
DN_CHUNK = 64
SG_CHUNK = 128
SG_GROUPS = 8
SG_GROUP_DIM = 64
HALF_W = 512
PROJ_W = 3200
IN_COLS = 3080
GATE_COL_BLOCK = 24
QK_SCALE = HEAD_DIM ** -0.5
LANES = 128

ADAM_LR = 0.001
ADAM_B1 = 0.9
ADAM_B2 = 0.999
ADAM_EPS = 1e-08
ADAM_WD = 0.01
ADAM_STEP = 10

VMEM_LIMIT = 56 * 1024 * 1024
ROW_TILE = 512

NN = ((1,), (0,))
NT = ((1,), (1,))
TN = ((0,), (0,))
MESH = pl.DeviceIdType.MESH


def _dot(a, b, dims):
    return lax.dot_general(a, b, (dims, ((), ())), preferred_element_type=F32)


def _bdot(a, b, dims):
    return _dot(a.astype(BF16), b.astype(BF16), dims)


def _split(a):
    hi = a.astype(BF16)
    lo = (a - hi.astype(F32)).astype(BF16)
    return hi, lo


def _dot3(a, b, dims=NN):
    return _dot(a[0], b[0], dims) + (_dot(a[0], b[1], dims) + _dot(a[1], b[0], dims))


def _dot_exact_lhs(a, b):
    ab = a.astype(BF16)
    b1 = b.astype(BF16)
    r1 = b - b1.astype(F32)
    b2 = r1.astype(BF16)
    b3 = (r1 - b2.astype(F32)).astype(BF16)
    return _dot(ab, b1, NN) + (_dot(ab, b2, NN) + _dot(ab, b3, NN))


def _call(body, *, name, out_shape, in_specs, out_specs, grid=(), scratch=(), comm=None, **kw):
    params = dict(vmem_limit_bytes=VMEM_LIMIT)
    if grid:
        params["dimension_semantics"] = ("arbitrary",) * len(grid)
    if comm is None:
        return pl.pallas_call(
            body, name=name, grid=grid, in_specs=in_specs, out_specs=out_specs,
            out_shape=out_shape, scratch_shapes=list(scratch),
            compiler_params=pltpu.CompilerParams(**params), **kw)

    n_in, n_out, n_sc = len(in_specs), len(out_specs), len(scratch)
    c_in, c_out = len(comm.inputs), len(comm.out_shape)
    steps = 1
    for g in grid:
        steps *= g

    def hosted(*refs):
        ins, cins = refs[:n_in], refs[n_in:n_in + c_in]
        o0 = n_in + c_in
        outs, couts = refs[o0:o0 + n_out], refs[o0 + n_out:o0 + n_out + c_out]
        s0 = o0 + n_out + c_out
        sc, csems = refs[s0:s0 + n_sc], refs[s0 + n_sc:]
        lin = 0
        for axis, g in enumerate(grid):
            lin = lin * g + pl.program_id(axis)

        def at(step, fn):
            @pl.when(lin == step % steps)
            def _():
                fn(cins, couts, csems)

        for step, fn in comm.phases:
            if step >= 0:
                at(step, fn)
        body(*ins, *outs, *sc)
        for step, fn in comm.phases:
            if step < 0:
                at(step, fn)

    aliases = dict(kw.pop("input_output_aliases", {}))
    for k, m in comm.aliases.items():
        aliases[n_in + k] = n_out + m
    call = pl.pallas_call(
        hosted, name=name, grid=grid, in_specs=list(in_specs) + [_ANY] * c_in,
        out_specs=list(out_specs) + [_ANY] * c_out, out_shape=list(out_shape) + comm.out_shape,
        scratch_shapes=list(scratch) + comm.sems, input_output_aliases=aliases,
        compiler_params=pltpu.CompilerParams(**params), **kw)

    def run(*args):
        res = call(*args, *comm.inputs)
        return res[:n_out], res[n_out:]

    return run


def _sds(shape, dtype):
    return jax.ShapeDtypeStruct(tuple(shape), dtype)


def _resident(shape):
    zeros = (0,) * len(shape)
    return pl.BlockSpec(tuple(shape), lambda *_: zeros, pipeline_mode=pl.Buffered(1))


def _sigmoid(x):
    return jax.nn.sigmoid(x)


def _softplus(x):
    return jnp.maximum(x, 0.0) + jnp.log(1.0 + jnp.exp(-jnp.abs(x)))


_GELU_C = 0.7978845608028654
_GELU_A = 0.044715


def _gelu_tanh(x):
    return jnp.tanh(_GELU_C * (x + _GELU_A * x * x * x))


def _gelu(x, t):
    return 0.5 * x * (1.0 + t)


def _gelu_grad(x, t):
    return 0.5 * (1.0 + t) + 0.5 * x * (1.0 - t * t) * _GELU_C * (1.0 + 3.0 * _GELU_A * x * x)


def _silu_grad(x):
    s = _sigmoid(x)
    return s * (1.0 + x * (1.0 - s))


def _rms_scale(xv):
    return lax.rsqrt(jnp.mean(xv * xv, axis=-1, keepdims=True) + EPS)


def _rms_bwd(dh, xv, g):
    r = _rms_scale(xv)
    xn = xv * r
    dg = jnp.sum(dh * xn, axis=0, keepdims=True)
    dxn = dh * g
    dx = r * (dxn - xn * jnp.mean(dxn * xn, axis=-1, keepdims=True))
    return dx, dg


def _iota2(shape, dim):
    return lax.broadcasted_iota(jnp.int32, shape, dim)


def _col_to_row(col):
    n = col.shape[0]
    eye = _iota2((n, n), 0) == _iota2((n, n), 1)
    return jnp.sum(jnp.where(eye, col, 0.0), axis=0, keepdims=True)


def _row_to_col(row):
    n = row.shape[1]
    eye = _iota2((n, n), 0) == _iota2((n, n), 1)
    return jnp.sum(jnp.where(eye, row, 0.0), axis=1, keepdims=True)


MXU_DIM = 256


def _hidden_chunks(f, step=3 * MXU_DIM):
    return [(c0, min(c0 + step, f)) for c0 in range(0, f, step)]

def ffn_fwd(x, gnorm, wg, wu, wd, name, comm=None):
    n, d = x.shape
    f = wg.shape[0]
    tm = min(ROW_TILE, n)
    fused = wd is not None

    def body(x_ref, g_ref, wg_ref, wu_ref, *rest):
        if fused:
            wd_ref, xo_ref, h_ref, gate_ref, up_ref, act_ref, acc_ref = rest
        else:
            h_ref, gate_ref, up_ref, act_ref = rest
        xv = x_ref[...]
        h = (xv * _rms_scale(xv) * g_ref[...]).astype(BF16)
        h_ref[...] = h
        chunks = _hidden_chunks(f)

        def gate_up(c0, c1):
            return _dot(h, wg_ref[c0:c1, :], NT), _dot(h, wu_ref[c0:c1, :], NT)

        nxt = gate_up(*chunks[0])
        for idx, (c0, c1) in enumerate(chunks):
            gate, up = nxt
            if idx + 1 < len(chunks):
                nxt = gate_up(*chunks[idx + 1])
            act = (gate * _sigmoid(gate) * up).astype(BF16)
            gate_ref[:, c0:c1] = gate.astype(BF16)
            up_ref[:, c0:c1] = up.astype(BF16)
            act_ref[:, c0:c1] = act
            if fused:
                part = _dot(act, wd_ref[c0:c1, :], NN)
                if c0 == 0:
                    acc_ref[...] = part
                else:
                    acc_ref[...] += part
        if fused:
            xo_ref[...] = xv + 0.5 * acc_ref[...]

    row = pl.BlockSpec((tm, d), lambda i: (i, 0))
    wide = pl.BlockSpec((tm, f), lambda i: (i, 0))
    n_w = 3 if fused else 2
    return _call(
        body, name=name, grid=(n // tm,),
        in_specs=[row, pl.BlockSpec((1, d), lambda i: (0, 0))] + [_resident((f, d))] * n_w,
        out_specs=([row] if fused else []) + [row, wide, wide, wide],
        out_shape=([_sds((n, d), F32)] if fused else []) + [_sds((n, d), BF16)] + [_sds((n, f), BF16)] * 3,
        scratch=[pltpu.VMEM((tm, d), F32)] if fused else [], comm=comm,
    )(*([x, gnorm, wg, wu] + ([wd] if fused else [])))


def ffn_down(x, act, wd, name, comm=None):
    n, d = x.shape
    f = wd.shape[0]
    tm = min(ROW_TILE, n)

    def body(x_ref, a_ref, w_ref, o_ref):
        o_ref[...] = x_ref[...] + 0.5 * _dot(a_ref[...], w_ref[...], NN)

    row = pl.BlockSpec((tm, d), lambda i: (i, 0))
    return _call(
        body, name=name, grid=(n // tm,),
        in_specs=[row, pl.BlockSpec((tm, f), lambda i: (i, 0)), _resident((f, d))],
        out_specs=[row], out_shape=[_sds((n, d), F32)], comm=comm,
    )(x, act, wd)


def ffn_bwd_act(dy, x, gnorm, gate, up, wg, wu, wd, name, comm=None):
    n, d = x.shape
    f = wg.shape[0]
    tm = min(ROW_TILE // 2, n)

    def body(dy_ref, x_ref, g_ref, gate_ref, up_ref, wg_ref, wu_ref, wd_ref,
             dx_ref, dgate_ref, dup_ref, dyh_ref, dg_ref, acc_ref):
        @pl.when(pl.program_id(0) == 0)
        def _():
            dg_ref[...] = jnp.zeros_like(dg_ref)

        dyh = (0.5 * dy_ref[...]).astype(BF16)
        dyh_ref[...] = dyh
        chunks = _hidden_chunks(f, 2 * MXU_DIM)
        next_dact = _dot(dyh, wd_ref[chunks[0][0]:chunks[0][1], :], NT)
        for idx, (c0, c1) in enumerate(chunks):
            dact = next_dact
            if idx + 1 < len(chunks):
                n0, n1 = chunks[idx + 1]
                next_dact = _dot(dyh, wd_ref[n0:n1, :], NT)
            gt = gate_ref[:, c0:c1].astype(F32)
            u = up_ref[:, c0:c1].astype(F32)
            s = _sigmoid(gt)
            dup = (dact * (gt * s)).astype(BF16)
            dgate = (dact * u * (s * (1.0 + gt * (1.0 - s)))).astype(BF16)
            dup_ref[:, c0:c1] = dup
            dgate_ref[:, c0:c1] = dgate
            part = _dot(dgate, wg_ref[c0:c1, :], NN) + _dot(dup, wu_ref[c0:c1, :], NN)
            if c0 == 0:
                acc_ref[...] = part
            else:
                acc_ref[...] += part
        dxn, dg = _rms_bwd(acc_ref[...], x_ref[...], g_ref[...])
        dx_ref[...] = dy_ref[...] + dxn
        dg_ref[...] += dg

    row = pl.BlockSpec((tm, d), lambda i: (i, 0))
    wide = pl.BlockSpec((tm, f), lambda i: (i, 0))
    vec = pl.BlockSpec((1, d), lambda i: (0, 0))
    wres = _resident((f, d))
    return _call(
        body, name=name, grid=(n // tm,),
        in_specs=[row, row, vec, wide, wide, wres, wres, wres],
        out_specs=[row, wide, wide, row, vec],
        out_shape=[_sds((n, d), F32), _sds((n, f), BF16), _sds((n, f), BF16),
                   _sds((n, d), BF16), _sds((1, d), F32)],
        scratch=[pltpu.VMEM((tm, d), F32)], comm=comm,
    )(dy, x, gnorm, gate, up, wg, wu, wd)


def ffn_bwd_w(wide, rows, pairs, name, comm=None):
    n, d = rows[0].shape
    f = wide[0].shape[1]
    fh = f // 2
    tk = min(ROW_TILE, n)
    n_w, n_r = len(wide), len(rows)

    def body(*refs):
        wide_refs, row_refs, outs = refs[:n_w], refs[n_w:n_w + n_r], refs[n_w + n_r:]

        @pl.when(pl.program_id(1) == 0)
        def _():
            for o_ref in outs:
                o_ref[...] = jnp.zeros_like(o_ref)

        row_vals = [r[...] for r in row_refs]
        for c0, c1 in _hidden_chunks(fh, 2 * MXU_DIM):
            for (i, k), o_ref in zip(pairs, outs):
                o_ref[c0:c1, :] += _dot(wide_refs[i][:, c0:c1], row_vals[k], TN)

    row = pl.BlockSpec((tk, d), lambda j, k: (k, 0))
    blk = pl.BlockSpec((tk, fh), lambda j, k: (k, j))
    return _call(
        body, name=name, grid=(2, n // tk),
        in_specs=[blk] * n_w + [row] * n_r,
        out_specs=[pl.BlockSpec((fh, d), lambda j, k: (j, 0))] * len(pairs),
        out_shape=[_sds((f, d), F32)] * len(pairs), comm=comm,
    )(*wide, *rows)


def final_loss(x, gnorm, target, name):
    n, d = x.shape
    tm = min(ROW_TILE, n)

    def body(x_ref, g_ref, t_ref, dx_ref, dg_ref, loss_ref):
        @pl.when(pl.program_id(0) == 0)
        def _():
            dg_ref[...] = jnp.zeros_like(dg_ref)
            loss_ref[...] = jnp.zeros_like(loss_ref)

        xv = x_ref[...]
        y = xv * _rms_scale(xv) * g_ref[...]
        err = y - t_ref[...]
        part = 0.5 * jnp.sum(jnp.mean(err * err, axis=-1, keepdims=True), axis=0, keepdims=True)
        loss_ref[...] += jnp.broadcast_to(part, loss_ref.shape)
        dx, dg = _rms_bwd(err * (1.0 / d), xv, g_ref[...])
        dx_ref[...] = dx
        dg_ref[...] += dg

    row = pl.BlockSpec((tm, d), lambda i: (i, 0))
    vec = pl.BlockSpec((1, d), lambda i: (0, 0))
    return _call(
        body, name=name, grid=(n // tm,),
        in_specs=[row, vec, row],
        out_specs=[row, vec, pl.BlockSpec((1, LANES), lambda i: (0, 0))],
        out_shape=[_sds((n, d), F32), _sds((1, d), F32), _sds((1, LANES), F32)],
    )(x, gnorm, target)


def in_proj_fwd(x, gnorm, w, name, comm=None):
    n, d = x.shape
    cols = w.shape[1]
    tm = min(ROW_TILE, n)

    def body(x_ref, g_ref, w_ref, p_ref, h_ref):
        xv = x_ref[...]
        h = (xv * _rms_scale(xv) * g_ref[...]).astype(BF16)
        h_ref[...] = h
        for c0, c1 in _hidden_chunks(cols):
            p_ref[:, c0:c1] = _dot(h, w_ref[:, c0:c1], NN)

    return _call(
        body, name=name, grid=(n // tm,),
        in_specs=[pl.BlockSpec((tm, d), lambda i: (i, 0)),
                  pl.BlockSpec((1, d), lambda i: (0, 0)), _resident((d, cols))],
        out_specs=[pl.BlockSpec((tm, cols), lambda i: (i, 0)),
                   pl.BlockSpec((tm, d), lambda i: (i, 0))],
        out_shape=[_sds((n, cols), F32), _sds((n, d), BF16)], comm=comm,
    )(x, gnorm, w)


def in_proj_bwd_x(dproj, w, x, gnorm, dres, name, comm=None):
    n, d = x.shape
    cols = w.shape[1]
    tm = min(ROW_TILE, n)

    def body(dp_ref, w_ref, x_ref, g_ref, dr_ref, dx_ref, dg_ref, dxh_ref):
        @pl.when(pl.program_id(0) == 0)
        def _():
            dg_ref[...] = jnp.zeros_like(dg_ref)

        dh = _dot(dp_ref[...], w_ref[...], NT)
        dxn, dg = _rms_bwd(dh, x_ref[...], g_ref[...])
        dx = dr_ref[...] + dxn
        dx_ref[...] = dx
        dxh_ref[...] = (0.5 * dx).astype(BF16)
        dg_ref[...] += dg

    row = pl.BlockSpec((tm, d), lambda i: (i, 0))
    vec = pl.BlockSpec((1, d), lambda i: (0, 0))
    return _call(
        body, name=name, grid=(n // tm,),
        in_specs=[pl.BlockSpec((tm, cols), lambda i: (i, 0)),
                  _resident((d, cols)), row, vec, row],
        out_specs=[row, vec, row],
        out_shape=[_sds((n, d), F32), _sds((1, d), F32), _sds((n, d), BF16)], comm=comm,
    )(dproj, w, x, gnorm, dres)


def matmul_tn(a_list, b, tn, name):
    n, cb = b.shape
    widths = [a.shape[1] for a in a_list]
    tk = min(ROW_TILE, n)

    def body(*refs):
        a_refs, b_ref, o_ref = refs[:-2], refs[-2], refs[-1]

        @pl.when(pl.program_id(0) == 0)
        def _():
            o_ref[...] = jnp.zeros_like(o_ref)

        r0 = 0
        for a_ref, ka in zip(a_refs, widths):
            av = a_ref[...]
            for c0, c1 in _hidden_chunks(cb, tn):
                o_ref[r0:r0 + ka, c0:c1] += _dot(av, b_ref[:, c0:c1], TN)
            r0 += ka

    return _call(
        body, name=name, grid=(n // tk,),
        in_specs=[pl.BlockSpec((tk, ka), lambda k: (k, 0)) for ka in widths]
        + [pl.BlockSpec((tk, cb), lambda k: (k, 0))],
        out_specs=pl.BlockSpec((sum(widths), cb), lambda k: (0, 0)),
        out_shape=_sds((sum(widths), cb), F32),
    )(*a_list, b)


def out_proj_fwd(x, sg_out, dn_out, w, name):
    n, d = x.shape
    tm = min(ROW_TILE, n)

    def body(x_ref, a_ref, b_ref, w_ref, o_ref):
        o_ref[...] = (x_ref[...] + _dot(a_ref[...], w_ref[0:HALF_W, :], NN)
                      + _dot(b_ref[...], w_ref[HALF_W:2 * HALF_W, :], NN))

    row = pl.BlockSpec((tm, d), lambda i: (i, 0))
    half = pl.BlockSpec((tm, HALF_W), lambda i: (i, 0))
    return _call(
        body, name=name, grid=(n // tm,),
        in_specs=[row, half, half, pl.BlockSpec((2 * HALF_W, d), lambda i: (0, 0))],
        out_specs=row, out_shape=_sds((n, d), F32),
    )(x, sg_out, dn_out, w)


def out_proj_bwd_x(dy, w, name, comm=None):
    n, d = dy.shape
    tm = min(ROW_TILE, n)

    def body(dy_ref, w_ref, dsg_ref, ddn_ref, dyb_ref):
        dyb = dy_ref[...].astype(BF16)
        dyb_ref[...] = dyb
        dsg_ref[...] = _dot(dyb, w_ref[0:HALF_W, :], NT)
        ddn_ref[...] = _dot(dyb, w_ref[HALF_W:2 * HALF_W, :], NT)

    row = pl.BlockSpec((tm, d), lambda i: (i, 0))
    half = pl.BlockSpec((tm, HALF_W), lambda i: (i, 0))
    return _call(
        body, name=name, grid=(n // tm,),
        in_specs=[row, pl.BlockSpec((2 * HALF_W, d), lambda i: (0, 0))],
        out_specs=[half, half, row],
        out_shape=[_sds((n, HALF_W), F32), _sds((n, HALF_W), F32), _sds((n, d), BF16)], comm=comm,
    )(dy, w)


SG_PAIRS = SG_GROUPS // 2


def _sg_low_half():
    return _iota2((SG_CHUNK, LANES), 1) < SG_GROUP_DIM


def _sg_pair_cols(p):
    return slice(p * LANES, (p + 1) * LANES)


def _sg_causal():
    return _iota2((SG_CHUNK, SG_CHUNK), 0) >= _iota2((SG_CHUNK, SG_CHUNK), 1)


def _sg_forward_chunk(pu, pv, ln_g, ln_b, wc, bias, low):
    tu, tv = _gelu_tanh(pu), _gelu_tanh(pv)
    u = _gelu(pu, tu)
    v = _gelu(pv, tv)
    mu = jnp.mean(v, axis=-1, keepdims=True)
    vc = v - mu
    rs = lax.rsqrt(jnp.mean(vc * vc, axis=-1, keepdims=True) + EPS)
    xhat = vc * rs
    vn = (xhat * ln_g + ln_b).astype(BF16)
    parts = []
    for p in range(SG_PAIRS):
        vn_p = vn[:, _sg_pair_cols(p)]
        parts.append(jnp.where(low, _dot(wc[2 * p], vn_p, NN), _dot(wc[2 * p + 1], vn_p, NN)))
    vs = bias + jnp.concatenate(parts, axis=1)
    return u, xhat, rs, vn, vs, tu, tv


def sg_fwd(proj, ln_g, ln_b, w_s, bias_tile, name):
    n = proj.shape[0]
    tm = min(ROW_TILE, n)

    def body(pu_ref, pv_ref, g_ref, b_ref, w_ref, bias_ref, o_ref):
        causal = _sg_causal()
        wc = [jnp.where(causal, w_ref[g], 0.0).astype(BF16) for g in range(SG_GROUPS)]
        masks = _sg_low_half()
        for ci in range(tm // SG_CHUNK):
            rows = slice(ci * SG_CHUNK, (ci + 1) * SG_CHUNK)
            u, _, _, _, vs, _, _ = _sg_forward_chunk(pu_ref[rows, :], pv_ref[rows, :], g_ref[...],
                                                     b_ref[...], wc, bias_ref[...], masks)
            o_ref[rows, :] = (u * vs).astype(BF16)

    vec = pl.BlockSpec((1, HALF_W), lambda i: (0, 0))
    return _call(
        body, name=name, grid=(n // tm,),
        in_specs=[pl.BlockSpec((tm, HALF_W), lambda i: (i, 0)),
                  pl.BlockSpec((tm, HALF_W), lambda i: (i, 1)), vec, vec,
                  pl.BlockSpec((SG_GROUPS, SG_CHUNK, SG_CHUNK), lambda i: (0, 0, 0)),
                  pl.BlockSpec((SG_CHUNK, HALF_W), lambda i: (0, 0))],
        out_specs=pl.BlockSpec((tm, HALF_W), lambda i: (i, 0)),
        out_shape=_sds((n, HALF_W), BF16),
    )(proj, proj, ln_g, ln_b, w_s, bias_tile)


def sg_bwd(dsg, proj, ln_g, ln_b, w_s, bias_tile, name):
    n = proj.shape[0]
    tm = min(ROW_TILE, n)

    def body(d_ref, pu_ref, pv_ref, g_ref, b_ref, w_ref, bias_ref,
             dp_ref, dw_ref, db_ref, dlg_ref, dlb_ref):
        @pl.when(pl.program_id(0) == 0)
        def _():
            dw_ref[...] = jnp.zeros_like(dw_ref)
            db_ref[...] = jnp.zeros_like(db_ref)
            dlg_ref[...] = jnp.zeros_like(dlg_ref)
            dlb_ref[...] = jnp.zeros_like(dlb_ref)

        causal = _sg_causal()
        wc = [jnp.where(causal, w_ref[g], 0.0).astype(BF16) for g in range(SG_GROUPS)]
        masks = _sg_low_half()
        ln_g_v = g_ref[...]
        for ci in range(tm // SG_CHUNK):
            rows = slice(ci * SG_CHUNK, (ci + 1) * SG_CHUNK)
            pu = pu_ref[rows, :]
            pv = pv_ref[rows, :]
            u, xhat, rs, vn, vs, tu, tv = _sg_forward_chunk(pu, pv, ln_g_v, b_ref[...], wc,
                                                            bias_ref[...], masks)
            dout = d_ref[rows, :]
            dp_ref[rows, 0:HALF_W] = (dout * vs * _gelu_grad(pu, tu)).astype(BF16)
            dvs = dout * u
            dvs_b = dvs.astype(BF16)
            db_ref[...] += dvs
            dvn_parts = []
            for p in range(SG_PAIRS):
                dvs_p = dvs_b[:, _sg_pair_cols(p)]
                vn_p = vn[:, _sg_pair_cols(p)]
                dvn_parts.append(jnp.where(masks, _dot(wc[2 * p], dvs_p, TN), _dot(wc[2 * p + 1], dvs_p, TN)))
                zero = jnp.zeros_like(dvs_p)
                dw_ref[2 * p] += jnp.where(causal, _dot(jnp.where(masks, dvs_p, zero), vn_p, NT), 0.0)
                dw_ref[2 * p + 1] += jnp.where(causal, _dot(jnp.where(masks, zero, dvs_p), vn_p, NT), 0.0)
            dvn = jnp.concatenate(dvn_parts, axis=1)
            dlg_ref[...] += jnp.sum(dvn * xhat, axis=0, keepdims=True)
            dlb_ref[...] += jnp.sum(dvn, axis=0, keepdims=True)
            dxh = dvn * ln_g_v
            dv = rs * (dxh - jnp.mean(dxh, axis=-1, keepdims=True)
                       - xhat * jnp.mean(dxh * xhat, axis=-1, keepdims=True))
            dp_ref[rows, HALF_W:2 * HALF_W] = (dv * _gelu_grad(pv, tv)).astype(BF16)

    vec = pl.BlockSpec((1, HALF_W), lambda i: (0, 0))
    wspec = pl.BlockSpec((SG_GROUPS, SG_CHUNK, SG_CHUNK), lambda i: (0, 0, 0))
    tile = pl.BlockSpec((SG_CHUNK, HALF_W), lambda i: (0, 0))
    return _call(
        body, name=name, grid=(n // tm,),
        in_specs=[pl.BlockSpec((tm, HALF_W), lambda i: (i, 0)),
                  pl.BlockSpec((tm, HALF_W), lambda i: (i, 0)),
                  pl.BlockSpec((tm, HALF_W), lambda i: (i, 1)), vec, vec, wspec, tile],
        out_specs=[pl.BlockSpec((tm, 2 * HALF_W), lambda i: (i, 0)), wspec, tile, vec, vec],
        out_shape=[_sds((n, PROJ_W), BF16), _sds((SG_GROUPS, SG_CHUNK, SG_CHUNK), F32),
                   _sds((SG_CHUNK, HALF_W), F32), _sds((1, HALF_W), F32), _sds((1, HALF_W), F32)],
    )(dsg, proj, proj, ln_g, ln_b, w_s, bias_tile)


CONV_K = 4
CONV_BLOCK = 256


def _shift_down(x, s, row):
    if s == 0:
        return x
    return jnp.where(row >= s, pltpu.roll(x, s, 0), 0.0)


def _shift_up(x, s, row):
    if s == 0:
        return x
    t_len = x.shape[0]
    return jnp.where(row < t_len - s, pltpu.roll(x, t_len - s, 0), 0.0)


def _conv_taps(x, row):
    return [_shift_down(x, CONV_K - 1 - j, row) for j in range(CONV_K)]


def _conv(taps, w):
    y = taps[0] * w[0:1, :]
    for j in range(1, CONV_K):
        y = y + taps[j] * w[j:j + 1, :]
    return y


def dn_conv_fwd(proj3, conv_w, name):
    b, t, _ = proj3.shape
    nblk = 3 * HALF_W // CONV_BLOCK
    first = 2 * HALF_W // CONV_BLOCK
    n_norm = 2 * HALF_W // CONV_BLOCK

    def body(x_ref, w_ref, o_ref):
        s = pl.program_id(1)
        x = x_ref[0]
        y = _conv(_conv_taps(x, _iota2(x.shape, 0)), w_ref[...])
        y = y * _sigmoid(y)

        @pl.when(s < n_norm)
        def _():
            for h in range(CONV_BLOCK // HEAD_DIM):
                cs = slice(h * HEAD_DIM, (h + 1) * HEAD_DIM)
                yh = y[:, cs]
                o_ref[0, :, cs] = yh * lax.rsqrt(jnp.sum(yh * yh, axis=-1, keepdims=True) + EPS)

        @pl.when(s >= n_norm)
        def _():
            o_ref[0] = y

    return _call(
        body, name=name, grid=(b, nblk),
        in_specs=[pl.BlockSpec((1, t, CONV_BLOCK), lambda i, s: (i, 0, first + s)),
                  pl.BlockSpec((CONV_K, CONV_BLOCK), lambda i, s: (0, s))],
        out_specs=pl.BlockSpec((1, t, CONV_BLOCK), lambda i, s: (i, 0, s)),
        out_shape=_sds((b, t, 3 * HALF_W), F32),
    )(proj3, conv_w)


def dn_conv_bwd(dqkv, proj3, conv_w, dproj3, name, comm=None):
    b, t, _ = proj3.shape
    nblk = 3 * HALF_W // CONV_BLOCK
    first = 2 * HALF_W // CONV_BLOCK
    n_norm = 2 * HALF_W // CONV_BLOCK

    def body(d_ref, x_ref, w_ref, dproj_in, dx_ref, dw_ref, ds_ref):
        s = pl.program_id(0)

        @pl.when(pl.program_id(1) == 0)
        def _():
            dw_ref[...] = jnp.zeros_like(dw_ref)

        x = x_ref[0]
        w = w_ref[...]
        row = _iota2(x.shape, 0)
        taps = _conv_taps(x, row)
        c = _conv(taps, w)
        sg = _sigmoid(c)
        y = c * sg

        @pl.when(s < n_norm)
        def _():
            for h in range(CONV_BLOCK // HEAD_DIM):
                cs = slice(h * HEAD_DIM, (h + 1) * HEAD_DIM)
                yh = y[:, cs]
                r = lax.rsqrt(jnp.sum(yh * yh, axis=-1, keepdims=True) + EPS)
                nh = yh * r
                dn = d_ref[0, :, cs]
                ds_ref[:, cs] = r * (dn - nh * jnp.sum(dn * nh, axis=-1, keepdims=True))

        @pl.when(s >= n_norm)
        def _():
            ds_ref[...] = d_ref[0]

        dc = ds_ref[...] * (sg * (1.0 + c * (1.0 - sg)))
        dx = _shift_up(dc, CONV_K - 1, row) * w[0:1, :]
        for j in range(1, CONV_K):
            dx = dx + _shift_up(dc, CONV_K - 1 - j, row) * w[j:j + 1, :]
        dx_ref[0] = dx.astype(BF16)
        for j in range(CONV_K):
            dw_ref[j:j + 1, :] += jnp.sum(dc * taps[j], axis=0, keepdims=True)

    return _call(
        body, name=name, grid=(nblk, b),
        in_specs=[pl.BlockSpec((1, t, CONV_BLOCK), lambda s, i: (i, 0, s)),
                  pl.BlockSpec((1, t, CONV_BLOCK), lambda s, i: (i, 0, first + s)),
                  pl.BlockSpec((CONV_K, CONV_BLOCK), lambda s, i: (0, s)), _ANY],
        out_specs=[pl.BlockSpec((1, t, CONV_BLOCK), lambda s, i: (i, 0, first + s)),
                   pl.BlockSpec((CONV_K, CONV_BLOCK), lambda s, i: (0, s))],
        out_shape=[_sds(dproj3.shape, BF16), _sds((CONV_K, 3 * HALF_W), F32)],
        scratch=[pltpu.VMEM((t, CONV_BLOCK), F32)],
        input_output_aliases={3: 0}, comm=comm,
    )(dqkv, proj3, conv_w, dproj3)


def _chunk_masks():
    ii = _iota2((DN_CHUNK, DN_CHUNK), 0)
    jj = _iota2((DN_CHUNK, DN_CHUNK), 1)
    return ii >= jj, ii > jj, ii == jj


LOCKSTEP_CHUNKS = 4


def _inv_unit_lower_many(l_mats, eye):
    eye_f = jnp.where(eye, 1.0, 0.0)
    ps = [-l for l in l_mats]
    ts = [eye_f + p for p in ps]
    pss = [_split(p) for p in ps]
    size = 2
    while size < DN_CHUNK:
        ps = [_dot3(s, s) for s in pss]
        pss = [_split(p) for p in ps]
        ts = [t + _dot3(_split(t), s) for t, s in zip(ts, pss)]
        size *= 2
    return ts


def _gates(pba, ea_row, dtb_row):
    beta = _sigmoid(pba)
    g = -ea_row * _softplus(pba + dtb_row)
    return beta, g


def _chunk_decay(gcol):
    incl, strict, eye = _chunk_masks()
    grow = jnp.sum(jnp.where(eye, gcol, 0.0), axis=0, keepdims=True)
    decay = jnp.where(incl, jnp.exp(jnp.where(incl, gcol - grow, 0.0)), 0.0)
    return decay, incl, strict, eye


def dn_chunk_fwd(qkv, proj3, alog_row, dtb_row, name, comm=None):
    b, t, _ = qkv.shape
    rblk = min(256, t)
    n_in = rblk // DN_CHUNK

    def body(q_ref, k_ref, v_ref, pba_ref, al_ref, dtb_ref,
             u_ref, w_ref, qd_ref, kd_ref, qk_ref, ti_ref, gc_ref):
        ea = jnp.exp(al_ref[...])
        tri = jnp.where(_chunk_masks()[0], 1.0, 0.0)

        _, strict, eye = _chunk_masks()

        def chunk_group(cg, carry):
            items = []
            for sub in range(LOCKSTEP_CHUNKS):
                rows = pl.ds(pl.multiple_of((cg * LOCKSTEP_CHUNKS + sub) * DN_CHUNK, DN_CHUNK), DN_CHUNK)
                beta_all, g_all = _gates(pba_ref[0, rows, :], ea, dtb_ref[...])
                gc = _dot_exact_lhs(tri, g_all)
                gc_ref[0, rows, :] = gc
                for h in range(N_HEADS):
                    items.append((rows, h, beta_all[:, h:h + 1], gc[:, N_HEADS + h:N_HEADS + h + 1]))
            ks, kbs, decays, egs = [], [], [], []
            for rows, h, beta, gcol in items:
                cs = slice(h * HEAD_DIM, (h + 1) * HEAD_DIM)
                k = k_ref[0, rows, cs]
                ks.append(k)
                kbs.append(k * beta)
                decays.append(_chunk_decay(gcol)[0])
                egs.append(jnp.exp(gcol))
            ms = [_bdot(kb, k, NT) for kb, k in zip(kbs, ks)]
            tinvs = _inv_unit_lower_many([jnp.where(strict, m * dc, 0.0) for m, dc in zip(ms, decays)], eye)
            tsps = [_split(t) for t in tinvs]
            for (rows, h, beta, gcol), tsp, tinv in zip(items, tsps, tinvs):
                cs = slice(h * HEAD_DIM, (h + 1) * HEAD_DIM)
                u_ref[0, rows, cs] = _dot3(tsp, _split(v_ref[0, rows, cs] * beta))
                ti_ref[0, h, rows, :] = tinv
            for (rows, h, beta, gcol), tsp, kb, eg in zip(items, tsps, kbs, egs):
                cs = slice(h * HEAD_DIM, (h + 1) * HEAD_DIM)
                w_ref[0, rows, cs] = _dot3(tsp, _split(kb * eg))
            for (rows, h, beta, gcol), k, dc, eg in zip(items, ks, decays, egs):
                cs = slice(h * HEAD_DIM, (h + 1) * HEAD_DIM)
                q = q_ref[0, rows, cs] * QK_SCALE
                qk_ref[0, h, rows, :] = _bdot(q, k, NT) * dc
                qd_ref[0, rows, cs] = q * eg
                kd_ref[0, rows, cs] = k * jnp.exp(gcol[DN_CHUNK - 1:DN_CHUNK, :] - gcol)
            return carry

        lax.fori_loop(0, n_in // LOCKSTEP_CHUNKS, chunk_group, 0)

    def seg(cblk):
        return pl.BlockSpec((1, rblk, HALF_W), lambda i, r: (i, r, cblk))

    vec = pl.BlockSpec((1, LANES), lambda i, r: (0, 0))
    wide = pl.BlockSpec((1, rblk, HALF_W), lambda i, r: (i, r, 0))
    sq = pl.BlockSpec((1, N_HEADS, rblk, DN_CHUNK), lambda i, r: (i, 0, r, 0))
    return _call(
        body, name=name, grid=(b, t // rblk),
        in_specs=[seg(0), seg(1), seg(2),
                  pl.BlockSpec((1, rblk, LANES), lambda i, r: (i, r, GATE_COL_BLOCK)), vec, vec],
        out_specs=[wide, wide, wide, wide, sq, sq,
                   pl.BlockSpec((1, rblk, LANES), lambda i, r: (i, r, 0))],
        out_shape=[_sds((b, t, HALF_W), F32)] * 4
        + [_sds((b, N_HEADS, t, DN_CHUNK), F32)] * 2 + [_sds((b, t, LANES), F32)], comm=comm,
    )(qkv, qkv, qkv, proj3, alog_row, dtb_row)


def dn_scan_fwd(u, w, qd, kd, qk, gc, name):
    b, t, _ = u.shape
    nc = t // DN_CHUNK
    bh = b * N_HEADS

    def body(u_ref, w_ref, qd_ref, kd_ref, qk_ref, gc_ref, o_ref, sin_ref, s_ref):
        @pl.when(pl.program_id(0) == 0)
        def _():
            s_ref[...] = jnp.zeros_like(s_ref)

        items = [(bi, h, slice(h * HEAD_DIM, (h + 1) * HEAD_DIM)) for bi in range(b) for h in range(N_HEADS)]
        sbs = []
        for bi, h, cs in items:
            s = s_ref[bi * N_HEADS + h]
            sin_ref[0, bi * N_HEADS + h] = s
            sbs.append(s.astype(BF16))
        ws = [_bdot(w_ref[bi, :, cs], sb, NN) for (bi, h, cs), sb in zip(items, sbs)]
        qs = [_bdot(qd_ref[bi, :, cs], sb, NN) for (bi, h, cs), sb in zip(items, sbs)]
        vbs = [(u_ref[bi, :, cs] - wsi).astype(BF16) for (bi, h, cs), wsi in zip(items, ws)]
        for (bi, h, cs), qsi, vb in zip(items, qs, vbs):
            o_ref[bi, :, cs] = qsi + _bdot(qk_ref[bi, h], vb, NN)
        for (bi, h, cs), vb in zip(items, vbs):
            gl = jnp.exp(gc_ref[bi, DN_CHUNK - 1:DN_CHUNK, N_HEADS + h:N_HEADS + h + 1])
            idx = bi * N_HEADS + h
            s_ref[idx] = s_ref[idx] * gl + _bdot(kd_ref[bi, :, cs], vb, TN)

    wide = pl.BlockSpec((b, DN_CHUNK, HALF_W), lambda c: (0, c, 0))
    return _call(
        body, name=name, grid=(nc,),
        in_specs=[wide, wide, wide, wide,
                  pl.BlockSpec((b, N_HEADS, DN_CHUNK, DN_CHUNK), lambda c: (0, 0, c, 0)),
                  pl.BlockSpec((b, DN_CHUNK, LANES), lambda c: (0, c, 0))],
        out_specs=[wide, pl.BlockSpec((1, bh, HEAD_DIM, HEAD_DIM), lambda c: (c, 0, 0, 0))],
        out_shape=[_sds((b, t, HALF_W), F32), _sds((nc, bh, HEAD_DIM, HEAD_DIM), F32)],
        scratch=[pltpu.VMEM((bh, HEAD_DIM, HEAD_DIM), F32)],
    )(u, w, qd, kd, qk, gc)


def dn_scan_bwd(do, u, w, qd, kd, qk, gc, s_in, name):
    b, t, _ = u.shape
    nc = t // DN_CHUNK
    bh = b * N_HEADS

    def body(do_ref, u_ref, w_ref, qd_ref, kd_ref, qk_ref, gc_ref, sin_ref,
             du_ref, dw_ref, dqd_ref, dkd_ref, dqk_ref, dgc_ref, ds_ref):
        @pl.when(pl.program_id(0) == 0)
        def _():
            ds_ref[...] = jnp.zeros_like(ds_ref)

        last_row = _iota2((DN_CHUNK, LANES), 0) == DN_CHUNK - 1
        lane = _iota2((DN_CHUNK, LANES), 1)
        items = [(bi, h, slice(h * HEAD_DIM, (h + 1) * HEAD_DIM)) for bi in range(b) for h in range(N_HEADS)]
        sbs = [sin_ref[0, bi * N_HEADS + h].astype(BF16) for bi, h, cs in items]
        wvs = [w_ref[bi, :, cs].astype(BF16) for bi, h, cs in items]
        dovs = [do_ref[bi, :, cs].astype(BF16) for bi, h, cs in items]
        dsbs = [ds_ref[bi * N_HEADS + h].astype(BF16) for bi, h, cs in items]
        vbs = [(u_ref[bi, :, cs] - _dot(wv, sb, NN)).astype(BF16)
               for (bi, h, cs), wv, sb in zip(items, wvs, sbs)]
        for (bi, h, cs), dov, sb in zip(items, dovs, sbs):
            dqd_ref[bi, :, cs] = _dot(dov, sb, NT)
        dvns = [_dot(kd_ref[bi, :, cs].astype(BF16), dsb, NN) + _dot(qk_ref[bi, h].astype(BF16), dov, TN)
                for (bi, h, cs), dsb, dov in zip(items, dsbs, dovs)]
        for (bi, h, cs), vb, dsb, dov in zip(items, vbs, dsbs, dovs):
            dkd_ref[bi, :, cs] = _dot(vb, dsb, NT)
            dqk_ref[bi, h] = _dot(dov, vb, NT)
        dgls = []
        for (bi, h, cs), dvn, sb, wv, dov in zip(items, dvns, sbs, wvs, dovs):
            idx = bi * N_HEADS + h
            du_ref[bi, :, cs] = dvn
            dvn_b = dvn.astype(BF16)
            dw_ref[bi, :, cs] = -_dot(dvn_b, sb, NT)
            gl = jnp.exp(gc_ref[bi, DN_CHUNK - 1:DN_CHUNK, N_HEADS + h:N_HEADS + h + 1])
            ds = ds_ref[idx]
            dgl = jnp.sum(jnp.sum(ds * sin_ref[0, idx], axis=1, keepdims=True), axis=0, keepdims=True)
            dgls.append(dgl * gl)
            ds_ref[idx] = (ds * gl + _dot(qd_ref[bi, :, cs].astype(BF16), dov, TN)
                           - _dot(wv, dvn_b, TN))
        for bi in range(b):
            dgc = jnp.zeros((DN_CHUNK, LANES), F32)
            for h in range(N_HEADS):
                dgc = dgc + jnp.where(jnp.logical_and(last_row, lane == N_HEADS + h),
                                      dgls[bi * N_HEADS + h], 0.0)
            dgc_ref[bi] = dgc

    def rev(c):
        return nc - 1 - c

    wide = pl.BlockSpec((b, DN_CHUNK, HALF_W), lambda c: (0, rev(c), 0))
    sq = pl.BlockSpec((b, N_HEADS, DN_CHUNK, DN_CHUNK), lambda c: (0, 0, rev(c), 0))
    gates = pl.BlockSpec((b, DN_CHUNK, LANES), lambda c: (0, rev(c), 0))
    return _call(
        body, name=name, grid=(nc,),
        in_specs=[wide, wide, wide, wide, wide, sq, gates,
                  pl.BlockSpec((1, bh, HEAD_DIM, HEAD_DIM), lambda c: (rev(c), 0, 0, 0))],
        out_specs=[wide, wide, wide, wide, sq, gates],
        out_shape=[_sds((b, t, HALF_W), F32)] * 4
        + [_sds((b, N_HEADS, t, DN_CHUNK), F32), _sds((b, t, LANES), F32)],
        scratch=[pltpu.VMEM((bh, HEAD_DIM, HEAD_DIM), F32)],
    )(do, u, w, qd, kd, qk, gc, s_in)


def dn_chunk_bwd(qkv, proj3, alog_row, dtb_row, tinv, u, w, du, dw, dqd, dkd, dqk, dgc_scan, dproj3, name,
                 comm=None):
    b, t, _ = qkv.shape
    rblk = min(256, t)
    n_in = rblk // DN_CHUNK

    def body(q_ref, k_ref, v_ref, pba_ref, al_ref, dtb_ref, ti_ref, u_ref, w_ref,
             du_ref, dw_ref, dqd_ref, dkd_ref, dqk_ref, dgs_ref, dproj_in,
             dq_ref, dpba_ref, dal_ref, ddtb_ref):
        @pl.when(jnp.logical_and(pl.program_id(0) == 0, pl.program_id(1) == 0))
        def _():
            dal_ref[...] = jnp.zeros_like(dal_ref)
            ddtb_ref[...] = jnp.zeros_like(ddtb_ref)

        ea = jnp.exp(al_ref[...])
        incl0 = _chunk_masks()[0]
        tri = jnp.where(incl0, 1.0, 0.0)
        tri_up = jnp.where(_iota2((DN_CHUNK, DN_CHUNK), 1) >= _iota2((DN_CHUNK, DN_CHUNK), 0), 1.0, 0.0)
        lane = _iota2((DN_CHUNK, LANES), 1)
        last_col = _iota2((DN_CHUNK, 1), 0) == DN_CHUNK - 1

        _, strict, _ = _chunk_masks()
        gate_lane = jnp.logical_and(lane >= N_HEADS, lane < 2 * N_HEADS)

        def chunk_group(cg, carry):
            tiles, items = [], []
            for sub in range(LOCKSTEP_CHUNKS):
                rows = pl.ds(pl.multiple_of((cg * LOCKSTEP_CHUNKS + sub) * DN_CHUNK, DN_CHUNK), DN_CHUNK)
                pba = pba_ref[0, rows, :]
                beta_all, g_all = _gates(pba, ea, dtb_ref[...])
                gc = _dot_exact_lhs(tri, g_all)
                tiles.append((rows, pba, beta_all, g_all))
                for h in range(N_HEADS):
                    items.append((sub, rows, h, slice(h * HEAD_DIM, (h + 1) * HEAD_DIM),
                                  beta_all[:, h:h + 1], gc[:, N_HEADS + h:N_HEADS + h + 1]))
            decays = [_chunk_decay(gcol)[0] for _, _, _, _, _, gcol in items]
            egs = [jnp.exp(gcol) for _, _, _, _, _, gcol in items]
            qbs = [(q_ref[0, rows, cs] * QK_SCALE).astype(BF16) for _, rows, h, cs, _, _ in items]
            kfs = [k_ref[0, rows, cs].astype(BF16) for _, rows, h, cs, _, _ in items]
            kbs = [k_ref[0, rows, cs] * beta for _, rows, h, cs, beta, _ in items]
            kbbs = [kb.astype(BF16) for kb in kbs]
            tsps = [_split(ti_ref[0, h, rows, :]) for _, rows, h, cs, _, _ in items]
            drus = [_dot3(tsp, _split(du_ref[0, rows, cs]), TN)
                    for (_, rows, h, cs, _, _), tsp in zip(items, tsps)]
            drws = [_dot3(tsp, _split(dw_ref[0, rows, cs]), TN)
                    for (_, rows, h, cs, _, _), tsp in zip(items, tsps)]
            m_kks = [_dot(kbb, kf, NT) for kbb, kf in zip(kbbs, kfs)]
            a_qks = [_dot(qb, kf, NT) for qb, kf in zip(qbs, kfs)]
            dls = [-jnp.where(strict, _dot3(_split(dru), _split(u_ref[0, rows, cs]), NT)
                              + _dot3(_split(drw), _split(w_ref[0, rows, cs]), NT), 0.0)
                   for (_, rows, h, cs, _, _), dru, drw in zip(items, drus, drws)]
            dms = [(dl * dc).astype(BF16) for dl, dc in zip(dls, decays)]
            das = [(dqk_ref[0, h, rows, :] * dc).astype(BF16)
                   for (_, rows, h, cs, _, _), dc in zip(items, decays)]
            dkb_mm = [_dot(dm, kf, NN) for dm, kf in zip(dms, kfs)]
            dk_mm = [_dot(dm, kbb, TN) + _dot(da, qb, TN) for dm, kbb, da, qb in zip(dms, kbbs, das, qbs)]
            dqs_mm = [_dot(da, kf, NN) for da, kf in zip(das, kfs)]
            dgc_tiles = [dgs_ref[0, rows, :] for rows, _, _, _ in tiles]
            dbeta_tiles = [jnp.zeros((DN_CHUNK, LANES), F32) for _ in tiles]
            for n_it, (sub, rows, h, cs, beta, gcol) in enumerate(items):
                eg, dc = egs[n_it], decays[n_it]
                k = k_ref[0, rows, cs]
                q = q_ref[0, rows, cs] * QK_SCALE
                kb, dru, drw = kbs[n_it], drus[n_it], drws[n_it]
                ek = jnp.exp(gcol[DN_CHUNK - 1:DN_CHUNK, :] - gcol)
                e_mat = (dls[n_it] * m_kks[n_it] + dqk_ref[0, h, rows, :] * a_qks[n_it]) * dc
                dkb = drw * eg + dkb_mm[n_it]
                dqd = dqd_ref[0, rows, cs]
                dkd = dkd_ref[0, rows, cs]
                kdk = dkd * k * ek
                kdk_total = jnp.sum(jnp.sum(kdk, axis=0, keepdims=True), axis=1, keepdims=True)
                dg = (jnp.sum(drw * kb * eg + dqd * q * eg - kdk, axis=-1, keepdims=True)
                      + jnp.sum(e_mat, axis=1, keepdims=True)
                      - _row_to_col(jnp.sum(e_mat, axis=0, keepdims=True))
                      + jnp.where(last_col, kdk_total, 0.0))
                dbeta = jnp.sum(dkb * k + dru * v_ref[0, rows, cs], axis=-1, keepdims=True)
                dq_ref[0, rows, cs] = (dqs_mm[n_it] + dqd * eg) * QK_SCALE
                dq_ref[0, rows, pl.ds(HALF_W + h * HEAD_DIM, HEAD_DIM)] = dk_mm[n_it] + dkd * ek + dkb * beta
                dq_ref[0, rows, pl.ds(2 * HALF_W + h * HEAD_DIM, HEAD_DIM)] = dru * beta
                dgc_tiles[sub] = dgc_tiles[sub] + jnp.where(lane == N_HEADS + h, dg, 0.0)
                dbeta_tiles[sub] = dbeta_tiles[sub] + jnp.where(lane == h, dbeta, 0.0)
            for (rows, pba, beta_all, g_all), dgc_tile, dbeta_tile in zip(tiles, dgc_tiles, dbeta_tiles):
                dg_tile = _dot_exact_lhs(tri_up, dgc_tile)
                da_pre = dg_tile * (-ea) * _sigmoid(pba + dtb_ref[...])
                dal_ref[...] += jnp.sum(jnp.where(gate_lane, dg_tile * g_all, 0.0), axis=0, keepdims=True)
                ddtb_ref[...] += jnp.sum(jnp.where(gate_lane, da_pre, 0.0), axis=0, keepdims=True)
                dpba_ref[0, rows, :] = jnp.where(lane < N_HEADS, dbeta_tile * beta_all * (1.0 - beta_all),
                                                 jnp.where(gate_lane, da_pre, 0.0)).astype(BF16)
            return carry

        lax.fori_loop(0, n_in // LOCKSTEP_CHUNKS, chunk_group, 0)

    def seg(cblk):
        return pl.BlockSpec((1, rblk, HALF_W), lambda i, r: (i, r, cblk))

    vec = pl.BlockSpec((1, LANES), lambda i, r: (0, 0))
    wide = pl.BlockSpec((1, rblk, HALF_W), lambda i, r: (i, r, 0))
    sq = pl.BlockSpec((1, N_HEADS, rblk, DN_CHUNK), lambda i, r: (i, 0, r, 0))
    gates = pl.BlockSpec((1, rblk, LANES), lambda i, r: (i, r, 0))
    return _call(
        body, name=name, grid=(b, t // rblk),
        in_specs=[seg(0), seg(1), seg(2),
                  pl.BlockSpec((1, rblk, LANES), lambda i, r: (i, r, GATE_COL_BLOCK)), vec, vec,
                  sq, wide, wide, wide, wide, wide, wide, sq, gates, _ANY],
        out_specs=[pl.BlockSpec((1, rblk, 3 * HALF_W), lambda i, r: (i, r, 0)),
                   pl.BlockSpec((1, rblk, LANES), lambda i, r: (i, r, GATE_COL_BLOCK)), vec, vec],
        out_shape=[_sds((b, t, 3 * HALF_W), F32), _sds(dproj3.shape, BF16),
                   _sds((1, LANES), F32), _sds((1, LANES), F32)],
        input_output_aliases={15: 1}, comm=comm,
    )(qkv, qkv, qkv, proj3, alog_row, dtb_row, tinv, u, w, du, dw, dqd, dkd, dqk, dgc_scan, dproj3)


def dn_out_fwd(o, proj, dn_norm, name):
    n = o.shape[0]
    tm = min(ROW_TILE, n)

    def body(o_ref, z_ref, g_ref, y_ref):
        for h in range(N_HEADS):
            cs = slice(h * HEAD_DIM, (h + 1) * HEAD_DIM)
            oh = o_ref[:, cs]
            z = z_ref[:, cs]
            y = oh * _rms_scale(oh) * g_ref[...]
            y_ref[:, cs] = (y * (z * _sigmoid(z))).astype(BF16)

    half = pl.BlockSpec((tm, HALF_W), lambda i: (i, 0))
    return _call(
        body, name=name, grid=(n // tm,),
        in_specs=[half, pl.BlockSpec((tm, HALF_W), lambda i: (i, 5)),
                  pl.BlockSpec((1, HEAD_DIM), lambda i: (0, 0))],
        out_specs=half, out_shape=_sds((n, HALF_W), BF16),
    )(o, proj, dn_norm)


def dn_out_bwd(dy, o, proj, dn_norm, dproj, name):
    n = o.shape[0]
    tm = min(ROW_TILE, n)

    def body(dy_ref, o_ref, z_ref, g_ref, dproj_in, do_ref, dz_ref, dg_ref):
        @pl.when(pl.program_id(0) == 0)
        def _():
            dg_ref[...] = jnp.zeros_like(dg_ref)

        g = g_ref[...]
        dg = jnp.zeros_like(g)
        for h in range(N_HEADS):
            cs = slice(h * HEAD_DIM, (h + 1) * HEAD_DIM)
            oh = o_ref[:, cs]
            z = z_ref[:, cs]
            d = dy_ref[:, cs]
            r = _rms_scale(oh)
            nh = oh * r
            sz = _sigmoid(z)
            dyn = d * (z * sz)
            dz_ref[:, cs] = (d * (nh * g) * (sz * (1.0 + z * (1.0 - sz)))).astype(BF16)
            dg = dg + jnp.sum(dyn * nh, axis=0, keepdims=True)
            dn = dyn * g
            do_ref[:, cs] = r * (dn - nh * jnp.mean(dn * nh, axis=-1, keepdims=True))
        dg_ref[...] += dg

    half = pl.BlockSpec((tm, HALF_W), lambda i: (i, 0))
    vec = pl.BlockSpec((1, HEAD_DIM), lambda i: (0, 0))
    return _call(
        body, name=name, grid=(n // tm,),
        in_specs=[half, half, pl.BlockSpec((tm, HALF_W), lambda i: (i, 5)), vec, _ANY],
        out_specs=[half, pl.BlockSpec((tm, HALF_W), lambda i: (i, 5)), vec],
        out_shape=[_sds((n, HALF_W), F32), _sds(dproj.shape, BF16), _sds((1, HEAD_DIM), F32)],
        input_output_aliases={4: 1},
    )(dy, o, proj, dn_norm, dproj)


def _adamw_math(w, g, m, v):
    m_new = ADAM_B1 * m + (1.0 - ADAM_B1) * g
    v_new = ADAM_B2 * v + (1.0 - ADAM_B2) * (g * g)
    m_hat = m_new / (1.0 - ADAM_B1 ** ADAM_STEP)
    v_hat = v_new / (1.0 - ADAM_B2 ** ADAM_STEP)
    delta = -ADAM_LR * (m_hat / (jnp.sqrt(v_hat) + ADAM_EPS) + ADAM_WD * w)
    return delta, m_new, v_new


def adamw(w, g, m, v, name):
    r, c = w.shape
    tr = r
    for cand in (256, 352):
        if r % cand == 0 and r > cand:
            tr = cand
            break

    def body(w_ref, g_ref, m_ref, v_ref, d_ref, mo_ref, vo_ref):
        d, mn, vn = _adamw_math(w_ref[...], g_ref[...], m_ref[...], v_ref[...])
        d_ref[...] = d
        mo_ref[...] = mn
        vo_ref[...] = vn

    spec = pl.BlockSpec((tr, c), lambda i: (i, 0))
    return _call(
        body, name=name, grid=(r // tr,),
        in_specs=[spec] * 4, out_specs=[spec] * 3, out_shape=[_sds((r, c), F32)] * 3,
    )(w, g, m, v)


def _place():
    return lax.axis_index("x"), lax.axis_index("y"), lax.axis_index("c")


def _other_chips(x, y):
    return [(1 - x, y), (x, 1 - y), (1 - x, 1 - y)]


_ANY = pl.BlockSpec(memory_space=pl.ANY)


def cast_place(w, shard_idx, name):
    r, cols = w.shape
    tr = r // 2

    def body(j_ref, w_ref, o_ref):
        o_ref[0] = w_ref[...].astype(BF16)

    return pl.pallas_call(
        body, name=name,
        grid_spec=pltpu.PrefetchScalarGridSpec(
            num_scalar_prefetch=1, grid=(r // tr,),
            in_specs=[pl.BlockSpec((tr, cols), lambda i, j: (i, 0))],
            out_specs=pl.BlockSpec((1, tr, cols), lambda i, j: (j[0], i, 0))),
        out_shape=_sds((N_SHARD, r, cols), BF16),
        compiler_params=pltpu.CompilerParams(dimension_semantics=("arbitrary",),
                                             vmem_limit_bytes=VMEM_LIMIT),
    )(shard_idx, w)


class Exchange:
    def __init__(self, inputs, out_shape, aliases, sems, phases):
        self.inputs, self.out_shape, self.aliases = list(inputs), list(out_shape), dict(aliases)
        self.sems, self.phases = list(sems), list(phases)


def run_exchange(ex, name):
    def body(*refs):
        n_in, n_out = len(ex.inputs), len(ex.out_shape)
        for _, fn in ex.phases:
            fn(refs[:n_in], refs[n_in:n_in + n_out], refs[n_in + n_out:])

    return _call(body, name=name, in_specs=[_ANY] * len(ex.inputs), out_specs=[_ANY] * len(ex.out_shape),
                 out_shape=ex.out_shape, scratch=ex.sems, input_output_aliases=ex.aliases)(*ex.inputs)


def merge_exchanges(exs):
    inputs, out_shape, sems, aliases, phases, out_slices = [], [], [], {}, [], []
    for ex in exs:
        i0, o0, s0 = len(inputs), len(out_shape), len(sems)
        inputs += ex.inputs
        out_shape += ex.out_shape
        sems += ex.sems
        for k, m in ex.aliases.items():
            aliases[i0 + k] = o0 + m
        si, so, ss = slice(i0, len(inputs)), slice(o0, len(out_shape)), slice(s0, len(sems))
        out_slices.append(so)
        for step, fn in ex.phases:
            phases.append((step, lambda ins, outs, sm, fn=fn, si=si, so=so, ss=ss: fn(ins[si], outs[so], sm[ss])))
    return Exchange(inputs, out_shape, aliases, sems, phases), out_slices


def _dma_sems(*sizes):
    return [pltpu.SemaphoreType.DMA((s,)) for s in sizes]


def gather_exchange(bufs, small=None, relay_step=-2):
    n = len(bufs)
    n_small = 0 if small is None else 1

    def half(outs, a, blk, hc):
        rh = bufs[a].shape[1] // 2
        return outs[a].at[blk, pl.ds(hc * rh, rh), :]

    def ici(outs, sems, a, k, blk, to):
        return pltpu.make_async_remote_copy(
            src_ref=half(outs, a, blk, to[2]), dst_ref=half(outs, a, blk, to[2]), send_sem=sems[0].at[3 * a + k],
            recv_sem=sems[1].at[3 * a + k], device_id=to, device_id_type=MESH)

    def d2d(outs, sems, a, k, blk, hc, to):
        return pltpu.make_async_remote_copy(
            src_ref=half(outs, a, blk, hc), dst_ref=half(outs, a, blk, hc), send_sem=sems[2].at[3 * a + k],
            recv_sem=sems[3].at[3 * a + k], device_id=to, device_id_type=MESH)

    def small_copy(ins, outs, sems, k, blk, to):
        return pltpu.make_async_remote_copy(
            src_ref=ins[n], dst_ref=outs[n].at[blk], send_sem=sems[0].at[3 * n + k],
            recv_sem=sems[1].at[3 * n + k], device_id=to, device_id_type=MESH)

    def start(ins, outs, sems):
        x, y, c = _place()
        j = 2 * x + y
        if n_small:
            pltpu.make_async_copy(ins[n], outs[n].at[j], sems[4].at[0]).start()
        for k, (px, py) in enumerate(_other_chips(x, y)):
            if n_small:
                small_copy(ins, outs, sems, k, j, (px, py, c)).start()
            for a in range(n):
                ici(outs, sems, a, k, j, (px, py, c)).start()

    def relay(ins, outs, sems):
        x, y, c = _place()
        for k, (px, py) in enumerate(_other_chips(x, y)):
            for a in range(n):
                ici(outs, sems, a, k, 2 * px + py, (px, py, c)).wait_recv()
                d2d(outs, sems, a, k, 2 * px + py, c, (x, y, 1 - c)).start()

    def finish(ins, outs, sems):
        x, y, c = _place()
        j = 2 * x + y
        for k, (px, py) in enumerate(_other_chips(x, y)):
            blk = 2 * px + py
            if n_small:
                small_copy(ins, outs, sems, k, blk, (px, py, c)).wait_recv()
                small_copy(ins, outs, sems, k, j, (px, py, c)).wait_send()
            for a in range(n):
                d2d(outs, sems, a, k, blk, 1 - c, (x, y, 1 - c)).wait_recv()
                ici(outs, sems, a, k, j, (px, py, c)).wait_send()
                d2d(outs, sems, a, k, blk, c, (x, y, 1 - c)).wait_send()
        if n_small:
            pltpu.make_async_copy(ins[n], outs[n].at[j], sems[4].at[0]).wait()

    out_shape = [_sds(b.shape, b.dtype) for b in bufs]
    if n_small:
        out_shape.append(_sds((N_SHARD,) + small.shape, small.dtype))
    return Exchange(list(bufs) + ([small] if n_small else []), out_shape, {a: a for a in range(n)},
                    _dma_sems(3 * n + 3, 3 * n + 3, 3 * n, 3 * n, 1),
                    [(0, start), (relay_step, relay), (-1, finish)])


def _start_then_wait(copies):
    def start(ins, outs, sems):
        for sent, _ in copies(ins, outs, sems):
            sent().start()

    def finish(ins, outs, sems):
        pairs = copies(ins, outs, sems)
        for _, arrival in pairs:
            arrival().wait_recv()
        for sent, _ in pairs:
            sent().wait_send()

    return [(0, start), (-1, finish)]


def pair_exchange(arrs):
    n = len(arrs)

    def copies(ins, outs, sems):
        x, y, c = _place()
        res = []
        for a in range(n):
            def mk(a=a):
                rh = arrs[a].shape[1] // 2
                return pltpu.make_async_remote_copy(
                    src_ref=ins[a].at[:, pl.ds((1 - c) * rh, rh), :], dst_ref=outs[a], send_sem=sems[0].at[a],
                    recv_sem=sems[1].at[a], device_id=(x, y, 1 - c), device_id_type=MESH)
            res.append((mk, mk))
        return res

    return Exchange(arrs, [_sds((a.shape[0], a.shape[1] // 2, a.shape[2]), a.dtype) for a in arrs], {},
                    _dma_sems(n, n), _start_then_wait(copies))


def pair_add(g, s, c_idx, name):
    nb, r, cols = g.shape
    rh = r // 2

    def body(c_ref, g_ref, s_ref, o_ref):
        o_ref[...] = (g_ref[...] + s_ref[...]).astype(BF16)

    return pl.pallas_call(
        body, name=name,
        grid_spec=pltpu.PrefetchScalarGridSpec(
            num_scalar_prefetch=1, grid=(nb,),
            in_specs=[pl.BlockSpec((1, rh, cols), lambda j, c: (j, c[0], 0)),
                      pl.BlockSpec((1, rh, cols), lambda j, c: (j, 0, 0))],
            out_specs=pl.BlockSpec((1, rh, cols), lambda j, c: (j, 0, 0))),
        out_shape=_sds((nb, rh, cols), BF16),
        compiler_params=pltpu.CompilerParams(dimension_semantics=("arbitrary",),
                                             vmem_limit_bytes=VMEM_LIMIT),
    )(c_idx, g, s)


def chip_exchange(arrs):
    n = len(arrs)

    def copies(ins, outs, sems):
        x, y, c = _place()
        j = 2 * x + y
        res = []
        for a in range(n):
            for k, (px, py) in enumerate(_other_chips(x, y)):
                def mk(src_blk, dst_blk, a=a, k=k, to=(px, py, c)):
                    return pltpu.make_async_remote_copy(
                        src_ref=ins[a].at[src_blk], dst_ref=outs[a].at[dst_blk], send_sem=sems[0].at[3 * a + k],
                        recv_sem=sems[1].at[3 * a + k], device_id=to, device_id_type=MESH)
                res.append((functools.partial(mk, 2 * px + py, j), functools.partial(mk, j, 2 * px + py)))
        return res

    return Exchange(arrs, [_sds(a.shape, a.dtype) for a in arrs], {}, _dma_sems(3 * n, 3 * n),
                    _start_then_wait(copies))


def sum_chips(r, p, shard_idx, name):
    nb, rh, cols = r.shape
    tr = rh

    def body(j_ref, p_ref, *refs):
        o_ref = refs[nb]
        j = j_ref[0]
        acc = None
        for i in range(nb):
            term = jnp.where(j == i, p_ref[0], refs[i][0]).astype(F32)
            acc = term if acc is None else acc + term
        o_ref[...] = acc

    def slot(i):
        return pl.BlockSpec((1, tr, cols), lambda t, j: (jnp.where(j[0] == i, (i + 1) % nb, i), t, 0))

    return pl.pallas_call(
        body, name=name,
        grid_spec=pltpu.PrefetchScalarGridSpec(
            num_scalar_prefetch=1, grid=(rh // tr,),
            in_specs=[pl.BlockSpec((1, tr, cols), lambda t, j: (j[0], t, 0))] + [slot(i) for i in range(nb)],
            out_specs=pl.BlockSpec((tr, cols), lambda t, j: (t, 0))),
        out_shape=_sds((rh, cols), F32),
        compiler_params=pltpu.CompilerParams(dimension_semantics=("arbitrary",),
                                             vmem_limit_bytes=VMEM_LIMIT),
    )(shard_idx, p, *([r] * nb))


def pair_swap(arrs):
    n = len(arrs)

    def copies(ins, outs, sems):
        x, y, c = _place()
        res = []
        for a in range(n):
            def mk(a=a):
                return pltpu.make_async_remote_copy(
                    src_ref=ins[a], dst_ref=outs[a], send_sem=sems[0].at[a], recv_sem=sems[1].at[a],
                    device_id=(x, y, 1 - c), device_id_type=MESH)
            res.append((mk, mk))
        return res

    return Exchange(arrs, [_sds(a.shape, a.dtype) for a in arrs], {}, _dma_sems(n, n),
                    _start_then_wait(copies))


ADAMW_STEPS_PER_HALF = 4


def adamw_pairs(items, name, comm=None):
    n_items = len(items)
    nh = ADAMW_STEPS_PER_HALF

    def body(*refs):
        ins, outs = refs[:5 * n_items], refs[5 * n_items:]
        mine = (pl.program_id(0) // nh) == lax.axis_index("c")
        for a in range(n_items):
            w_ref, gm_ref, gs_ref, m_ref, v_ref = ins[5 * a:5 * a + 5]
            g_ref, d_ref, mo_ref, vo_ref = outs[4 * a:4 * a + 4]
            g = jnp.where(mine, gm_ref[...], gs_ref[...])
            d, mn, vn = _adamw_math(w_ref[...], g, m_ref[...], v_ref[...])
            g_ref[...] = g
            d_ref[...] = d
            mo_ref[...] = mn
            vo_ref[...] = vn

    in_specs, out_specs, out_shape, args = [], [], [], []
    for w, g_mine, g_sib, m, v in items:
        r, cols = w.shape
        tr = r // (2 * nh)
        full = pl.BlockSpec((tr, cols), lambda i: (i, 0))
        part = pl.BlockSpec((tr, cols), lambda i: (i % nh, 0))
        in_specs += [full, part, part, full, full]
        out_specs += [full] * 4
        out_shape += [_sds((r, cols), F32)] * 4
        args += [w, g_mine, g_sib, m, v]
    res = _call(body, name=name, grid=(2 * nh,), in_specs=in_specs, out_specs=out_specs,
                out_shape=out_shape, comm=comm)(*args)
    own, hosted = (res, None) if comm is None else res
    grouped = [tuple(own[4 * a:4 * a + 4]) for a in range(n_items)]
    return grouped if comm is None else (grouped, hosted)


N_DEV = 8


def device_gather(pack):
    def copies(ins, outs, sems):
        x, y, c = _place()
        me = 4 * x + 2 * y + c
        res = []
        for k in range(1, N_DEV):
            fx, fy, fc = (k >> 2) & 1, (k >> 1) & 1, k & 1
            px, py, pc = (1 - x if fx else x, 1 - y if fy else y, 1 - c if fc else c)

            def mk(slot, k=k, to=(px, py, pc)):
                return pltpu.make_async_remote_copy(
                    src_ref=ins[0], dst_ref=outs[0].at[slot], send_sem=sems[0].at[k - 1],
                    recv_sem=sems[1].at[k - 1], device_id=to, device_id_type=MESH)
            res.append((functools.partial(mk, me), functools.partial(mk, 4 * px + 2 * py + pc)))
        return res

    return Exchange([pack], [_sds((N_DEV,) + pack.shape, pack.dtype)], {}, _dma_sems(N_DEV - 1, N_DEV - 1),
                    _start_then_wait(copies))


def sum_devices(buf, pack, me_idx, name):
    r, cols = pack.shape

    def body(me_ref, p_ref, *refs):
        o_ref = refs[N_DEV]
        acc = None
        for i in range(N_DEV):
            term = jnp.where(me_ref[0] == i, p_ref[...], refs[i][0])
            acc = term if acc is None else acc + term
        o_ref[...] = acc

    def slot(i):
        return pl.BlockSpec((1, r, cols), lambda t, me: (jnp.where(me[0] == i, (i + 1) % N_DEV, i), 0, 0))

    whole = pl.BlockSpec((r, cols), lambda t, me: (0, 0))
    return pl.pallas_call(
        body, name=name,
        grid_spec=pltpu.PrefetchScalarGridSpec(
            num_scalar_prefetch=1, grid=(1,),
            in_specs=[whole] + [slot(i) for i in range(N_DEV)], out_specs=whole),
        out_shape=_sds((r, cols), F32),
        compiler_params=pltpu.CompilerParams(dimension_semantics=("arbitrary",),
                                             vmem_limit_bytes=VMEM_LIMIT),
    )(me_idx, pack, *([buf] * N_DEV))


SMALL_NAMES = ("ffn1_norm", "mix_norm", "ffn2_norm", "final_norm", "sg_ln_g", "sg_ln_b",
               "dn_norm", "a_log", "dt_bias", "sg_b", "sg_w", "conv_w", "loss")


def _to_rows(a):
    flat = a.reshape(-1)
    pad = (-flat.shape[0]) % LANES
    if pad:
        flat = jnp.pad(flat, (0, pad))
    return flat.reshape(-1, LANES)


def _pack_small(parts):
    rows = [_to_rows(parts[k]) for k in SMALL_NAMES]
    pack = jnp.concatenate(rows, axis=0)
    pad = (-pack.shape[0]) % 8
    if pad:
        pack = jnp.pad(pack, ((0, pad), (0, 0)))
    return pack


def _unpack_small(pack, shapes):
    out, r0 = {}, 0
    for k in SMALL_NAMES:
        size = 1
        for s in shapes[k]:
            size *= s
        nrows = -(-size // LANES)
        out[k] = pack[r0:r0 + nrows].reshape(-1)[:size].reshape(shapes[k])
        r0 += nrows
    return out


def kernel(x, ffn1_norm, ffn1_w_gate, ffn1_w_up, ffn1_w_down, mix_norm, w_in, conv_w, a_log, dt_bias, dn_norm, sg_ln_g, sg_ln_b, sg_w, sg_b, w_out, ffn2_norm, ffn2_w_gate, ffn2_w_up, ffn2_w_down, final_norm, loss_target, m_ffn1_norm, m_ffn1_w_gate, m_ffn1_w_up, m_ffn1_w_down, m_mix_norm, m_w_in, m_conv_w, m_a_log, m_dt_bias, m_dn_norm, m_sg_ln_g, m_sg_ln_b, m_sg_w, m_sg_b, m_w_out, m_ffn2_norm, m_ffn2_w_gate, m_ffn2_w_up, m_ffn2_w_down, m_final_norm, v_ffn1_norm, v_ffn1_w_gate, v_ffn1_w_up, v_ffn1_w_down, v_mix_norm, v_w_in, v_conv_w, v_a_log, v_dt_bias, v_dn_norm, v_sg_ln_g, v_sg_ln_b, v_sg_w, v_sg_b, v_w_out, v_ffn2_norm, v_ffn2_w_gate, v_ffn2_w_up, v_ffn2_w_down, v_final_norm):
    bsz, t_len, d = x.shape
    n = bsz * t_len
    xy, yy, cc = _place()
    shard = 2 * xy + yy

    big_names = ["ffn1_w_gate", "ffn1_w_up", "ffn1_w_down", "w_in", "w_out",
                 "ffn2_w_gate", "ffn2_w_up", "ffn2_w_down"]
    big_w = dict(ffn1_w_gate=ffn1_w_gate, ffn1_w_up=ffn1_w_up, ffn1_w_down=ffn1_w_down, w_in=w_in,
                 w_out=w_out, ffn2_w_gate=ffn2_w_gate, ffn2_w_up=ffn2_w_up, ffn2_w_down=ffn2_w_down)
    big_m = dict(ffn1_w_gate=m_ffn1_w_gate, ffn1_w_up=m_ffn1_w_up, ffn1_w_down=m_ffn1_w_down, w_in=m_w_in,
                 w_out=m_w_out, ffn2_w_gate=m_ffn2_w_gate, ffn2_w_up=m_ffn2_w_up, ffn2_w_down=m_ffn2_w_down)
    big_v = dict(ffn1_w_gate=v_ffn1_w_gate, ffn1_w_up=v_ffn1_w_up, ffn1_w_down=v_ffn1_w_down, w_in=v_w_in,
                 w_out=v_w_out, ffn2_w_gate=v_ffn2_w_gate, ffn2_w_up=v_ffn2_w_up, ffn2_w_down=v_ffn2_w_down)
    shard_idx = jnp.reshape(shard, (1,)).astype(jnp.int32)
    c_idx = jnp.reshape(cc, (1,)).astype(jnp.int32)
    transposed = ("ffn1_w_gate", "ffn1_w_up", "ffn2_w_gate", "ffn2_w_up")

    def as2d(a, k):
        return a[0].T if k in transposed else a[0]

    def from2d(a, k):
        return a.T[None] if k in transposed else a[None]

    placed = {k: cast_place(as2d(big_w[k], k), shard_idx, name="cast_" + k) for k in big_names}
    first_names = ["ffn1_w_gate", "ffn1_w_up"]
    second_names = ["ffn1_w_down", "w_in"]
    third_names = ["w_out", "ffn2_w_gate"]
    fourth_names = ["ffn2_w_up", "ffn2_w_down"]
    res = run_exchange(gather_exchange([placed[k] for k in first_names], conv_w[0]), name="gather_first")
    gw = dict(zip(first_names, res[:2]))
    conv_full = res[2].transpose(1, 0, 2).reshape(CONV_K, 3 * HALF_W)

    x0 = x.reshape(n, d)
    def ffn_weights(prefix):
        return [gw[prefix + k].reshape(-1, d) for k in ("_w_gate", "_w_up", "_w_down")]

    def ffn_grad_blocks(grads):
        return [g.reshape(N_SHARD, -1, d) for g in grads]

    (h1, gate1, up1, act1), second = ffn_fwd(
        x0, ffn1_norm, gw["ffn1_w_gate"].reshape(-1, d), gw["ffn1_w_up"].reshape(-1, d), None,
        name="ffn1_fwd", comm=gather_exchange([placed[k] for k in second_names]))
    gw.update(zip(second_names, second))
    (x1,) = ffn_down(x0, act1, gw["ffn1_w_down"].reshape(-1, d), name="ffn1_down")
    w_in_full = gw["w_in"].transpose(1, 0, 2).reshape(d, IN_COLS)
    w_in_full = jnp.pad(w_in_full, ((0, 0), (0, PROJ_W - IN_COLS)))
    (proj, h2), third = in_proj_fwd(x1, mix_norm, w_in_full, name="in_proj_fwd",
                                    comm=gather_exchange([placed[k] for k in third_names]))
    gw.update(zip(third_names, third))
    proj3 = proj.reshape(bsz, t_len, PROJ_W)
    bias_tile = jnp.repeat(sg_b[0].T, SG_GROUP_DIM, axis=1)
    sg_out = sg_fwd(proj, sg_ln_g, sg_ln_b, sg_w[0], bias_tile, name="sg_fwd")
    qkv = dn_conv_fwd(proj3, conv_full, name="dn_conv_fwd")
    alog_row = jnp.zeros((1, LANES), F32).at[0, N_HEADS:2 * N_HEADS].set(a_log[0])
    dtb_row = jnp.zeros((1, LANES), F32).at[0, N_HEADS:2 * N_HEADS].set(dt_bias[0])
    (u_wy, w_wy, q_dec, k_dec, qk, tinv, gc), fourth = dn_chunk_fwd(
        qkv, proj3, alog_row, dtb_row, name="dn_chunk_fwd",
        comm=gather_exchange([placed[k] for k in fourth_names]))
    gw.update(zip(fourth_names, fourth))
    w_out_full = gw["w_out"].reshape(2 * HALF_W, d)
    o, s_in = dn_scan_fwd(u_wy, w_wy, q_dec, k_dec, qk, gc, name="dn_scan_fwd")
    dn_out = dn_out_fwd(o.reshape(n, HALF_W), proj, dn_norm, name="dn_out_fwd")
    x2 = out_proj_fwd(x1, sg_out, dn_out, w_out_full, name="out_proj_fwd")
    x3, h3, gate2, up2, act2 = ffn_fwd(x2, ffn2_norm, *ffn_weights("ffn2"), name="ffn2_fwd")
    dx3, d_final_norm, loss_tile = final_loss(x3, final_norm.reshape(1, d),
                                              loss_target.reshape(n, d), name="final_loss")

    dx2, dgate2, dup2, dyh2, d_ffn2_norm = ffn_bwd_act(
        dx3, x2, ffn2_norm, gate2, up2, *ffn_weights("ffn2"), name="ffn2_bwd_act")
    g_big = {}
    g_big["ffn2_w_gate"], g_big["ffn2_w_up"], g_big["ffn2_w_down"] = ffn_grad_blocks(ffn_bwd_w(
        [dgate2, dup2, act2], [h3, dyh2], [(0, 0), (1, 0), (2, 1)], name="ffn2_bwd_w"))

    early = ["ffn2_w_gate", "ffn2_w_up", "ffn2_w_down"]
    (d_sg, d_dn, dx2b), early_sib = out_proj_bwd_x(dx2, w_out_full, name="out_proj_bwd_x",
                                                   comm=pair_exchange([g_big[k] for k in early]))
    early_sums = [pair_add(g_big[k], s, c_idx, name="grad_pair_add_" + k) for k, s in zip(early, early_sib)]
    g_w_out = matmul_tn([sg_out, dn_out], dx2b, d, name="w_out_grad")
    g_big["w_out"] = g_w_out.reshape(N_SHARD, (2 * HALF_W) // N_SHARD, d)

    d_proj, d_sg_w, d_bias_tile, d_ln_g, d_ln_b = sg_bwd(d_sg, proj, sg_ln_g, sg_ln_b, sg_w[0],
                                                         bias_tile, name="sg_bwd")
    d_o, d_proj, d_dn_norm = dn_out_bwd(d_dn, o.reshape(n, HALF_W), proj, dn_norm, d_proj,
                                        name="dn_out_bwd")
    du, dw, dqd, dkd, dqk, dgc_scan = dn_scan_bwd(d_o.reshape(bsz, t_len, HALF_W), u_wy, w_wy, q_dec,
                                                  k_dec, qk, gc, s_in, name="dn_scan_bwd")
    (d_qkv, d_proj3, d_alog_row, d_dtb_row), early_chips = dn_chunk_bwd(
        qkv, proj3, alog_row, dtb_row, tinv, u_wy, w_wy, du, dw, dqd, dkd, dqk, dgc_scan,
        d_proj.reshape(bsz, t_len, PROJ_W), name="dn_chunk_bwd", comm=chip_exchange(early_sums))
    early_halves = [sum_chips(r, p, shard_idx, name="grad_chip_sum_" + k)
                    for k, r, p in zip(early, early_chips, early_sums)]
    d_proj3, d_conv = dn_conv_bwd(d_qkv, proj3, conv_full, d_proj3, name="dn_conv_bwd")
    d_proj = d_proj3.reshape(n, PROJ_W)
    g_w_in = matmul_tn([h2], d_proj, 3 * MXU_DIM, name="w_in_grad")[:, :IN_COLS]
    g_big["w_in"] = g_w_in.reshape(d, N_SHARD, IN_COLS // N_SHARD).transpose(1, 0, 2)

    def reduce_start(names):
        return pair_exchange([g_big[k] for k in names])

    def reduce_pair_sums(names, from_sib):
        return [pair_add(g_big[k], s, c_idx, name="grad_pair_add_" + k) for k, s in zip(names, from_sib)]

    def reduce_chip_sums(names, from_chips, sums):
        return [sum_chips(r, p, shard_idx, name="grad_chip_sum_" + k)
                for k, r, p in zip(names, from_chips, sums)]

    mid = ["w_in", "w_out"]
    (dx1, d_mix_norm, dyh1), mid_sib = in_proj_bwd_x(d_proj, w_in_full, x1, mix_norm, dx2,
                                                     name="in_proj_bwd_x", comm=reduce_start(mid))
    mid_sums = reduce_pair_sums(mid, mid_sib)
    down = ["ffn1_w_down"]
    (g_down,), mid_chips = ffn_bwd_w([act1], [dyh1], [(0, 0)], name="ffn1_bwd_w_down",
                                     comm=chip_exchange(mid_sums))
    g_big["ffn1_w_down"] = g_down.reshape(N_SHARD, -1, d)
    mid_halves = reduce_chip_sums(mid, mid_chips, mid_sums)
    leg, legs = merge_exchanges([reduce_start(down), pair_swap(mid_halves), pair_swap(early_halves)])
    leg_res = run_exchange(leg, name="grad_pair_exchange_down")
    down_sums = reduce_pair_sums(down, leg_res[legs[0]])
    mid_sib_halves, early_sib_halves = leg_res[legs[1]], leg_res[legs[2]]

    dx0, dgate1, dup1, _, d_ffn1_norm = ffn_bwd_act(
        dx1, x0, ffn1_norm, gate1, up1, *ffn_weights("ffn1"), name="ffn1_bwd_act")
    grad_x = dx0.reshape(bsz, t_len, d)
    d_sg_b = d_bias_tile.reshape(SG_CHUNK, SG_GROUPS, SG_GROUP_DIM).sum(axis=-1).T
    small_g = dict(ffn1_norm=d_ffn1_norm, mix_norm=d_mix_norm, ffn2_norm=d_ffn2_norm,
                   final_norm=d_final_norm, sg_ln_g=d_ln_g, sg_ln_b=d_ln_b, dn_norm=d_dn_norm,
                   a_log=d_alog_row[:, N_HEADS:2 * N_HEADS], dt_bias=d_dtb_row[:, N_HEADS:2 * N_HEADS],
                   sg_b=d_sg_b, sg_w=d_sg_w, conv_w=d_conv, loss=loss_tile[:, :1])
    my_pack = _pack_small(small_g)
    hosted, parts = merge_exchanges([chip_exchange(down_sums), device_gather(my_pack)])
    late = ["ffn1_w_gate", "ffn1_w_up"]
    late_grads, hosted_res = ffn_bwd_w([dgate1, dup1], [h1], [(0, 0), (1, 0)], name="ffn1_bwd_w_gate_up",
                                       comm=hosted)
    g_big["ffn1_w_gate"], g_big["ffn1_w_up"] = ffn_grad_blocks(late_grads)
    down_halves = reduce_chip_sums(down, hosted_res[parts[0]], down_sums)
    (all_packs,) = hosted_res[parts[1]]

    leg, legs = merge_exchanges([reduce_start(late), pair_swap(down_halves)])
    leg_res = run_exchange(leg, name="grad_pair_exchange")
    pair_sums = reduce_pair_sums(late, leg_res[legs[0]])
    down_sib_halves = leg_res[legs[1]]

    def adam_items(names, mine, sib):
        return [(as2d(big_w[k], k), gm, gs, as2d(big_m[k], k), as2d(big_v[k], k))
                for k, gm, gs in zip(names, mine, sib)]

    outs = {}
    done = adamw_pairs(
        adam_items(early + mid + down, early_halves + mid_halves + down_halves,
                   list(early_sib_halves) + list(mid_sib_halves) + list(down_sib_halves)),
        name="adamw_early")
    from_chips = run_exchange(chip_exchange(pair_sums), name="grad_chip_exchange")
    halves = reduce_chip_sums(late, from_chips, pair_sums)
    sib_halves = run_exchange(pair_swap(halves), name="grad_pair_swap")
    done += adamw_pairs(adam_items(late, halves, sib_halves), name="adamw_late")
    for k, res in zip(early + mid + down + late, done):
        outs[k] = tuple(from2d(a, k) for a in res)

    small_w = dict(ffn1_norm=ffn1_norm, mix_norm=mix_norm, ffn2_norm=ffn2_norm, final_norm=final_norm,
                   sg_ln_g=sg_ln_g, sg_ln_b=sg_ln_b, dn_norm=dn_norm, a_log=a_log, dt_bias=dt_bias,
                   sg_b=sg_b, sg_w=sg_w)
    small_m = dict(ffn1_norm=m_ffn1_norm, mix_norm=m_mix_norm, ffn2_norm=m_ffn2_norm,
                   final_norm=m_final_norm, sg_ln_g=m_sg_ln_g, sg_ln_b=m_sg_ln_b, dn_norm=m_dn_norm,
                   a_log=m_a_log, dt_bias=m_dt_bias, sg_b=m_sg_b, sg_w=m_sg_w)
    small_v = dict(ffn1_norm=v_ffn1_norm, mix_norm=v_mix_norm, ffn2_norm=v_ffn2_norm,
                   final_norm=v_final_norm, sg_ln_g=v_sg_ln_g, sg_ln_b=v_sg_ln_b, dn_norm=v_dn_norm,
                   a_log=v_a_log, dt_bias=v_dt_bias, sg_b=v_sg_b, sg_w=v_sg_w)
    shapes = {k: small_w[k].shape for k in small_w}
    shapes["conv_w"] = (CONV_K, 3 * HALF_W)
    shapes["loss"] = (1, 1)
    me_idx = jnp.reshape(4 * xy + 2 * yy + cc, (1,)).astype(jnp.int32)
    g_pack = sum_devices(all_packs, my_pack, me_idx, name="small_sum")
    g_small = _unpack_small(g_pack, shapes)
    loss = g_small["loss"].reshape(())
    cw = 3 * HALF_W // N_SHARD
    g_conv = lax.dynamic_slice_in_dim(g_small["conv_w"], shard * cw, cw, axis=1)
    zero_conv = jnp.zeros((CONV_K, 3 * HALF_W), F32)

    def packed(src, conv):
        parts = dict(src)
        parts["conv_w"] = lax.dynamic_update_slice_in_dim(zero_conv, conv[0], shard * cw, axis=1)
        parts["loss"] = jnp.zeros((1, 1), F32)
        return _pack_small(parts)

    d_pack, m_pack, v_pack = adamw(packed(small_w, conv_w), g_pack, packed(small_m, m_conv_w),
                                   packed(small_v, v_conv_w), name="adamw_small")
    d_small = _unpack_small(d_pack, shapes)
    m_small = _unpack_small(m_pack, shapes)
    v_small = _unpack_small(v_pack, shapes)

    def conv_block(full_arr):
        return lax.dynamic_slice_in_dim(full_arr, shard * cw, cw, axis=1)[None]

    for k in small_w:
        outs[k] = (g_small[k].reshape(small_w[k].shape), d_small[k], m_small[k], v_small[k])
    outs["conv_w"] = (g_conv[None], conv_block(d_small["conv_w"]), conv_block(m_small["conv_w"]),
                      conv_block(v_small["conv_w"]))

    order = ["ffn1_norm", "ffn1_w_gate", "ffn1_w_up", "ffn1_w_down", "mix_norm", "w_in", "conv_w",
             "a_log", "dt_bias", "dn_norm", "sg_ln_g", "sg_ln_b", "sg_w", "sg_b", "w_out", "ffn2_norm",
             "ffn2_w_gate", "ffn2_w_up", "ffn2_w_down", "final_norm"]
    return (loss, grad_x, *[outs[k][0] for k in order], *[outs[k][1] for k in order],
            *[outs[k][2] for k in order], *[outs[k][3] for k in order])
```

```python
import functools

import jax
import jax.numpy as jnp
from jax import lax
from jax.experimental import pallas as pl
from jax.experimental.pallas import tpu as pltpu

F32 = jnp.float32
BF16 = jnp.bfloat16
EPS = 1e-6

D_MODEL = 1024
N_SHARD = 4
HEAD_DIM = 128
N_HEADS = 4
DN_CHUNK = 64
SG_CHUNK = 128
SG_GROUPS = 8
SG_GROUP_DIM = 64
HALF_W = 512
PROJ_W = 3200
IN_COLS = 3080
GATE_COL_BLOCK = 24
QK_SCALE = HEAD_DIM ** -0.5
LANES = 128

ADAM_LR = 0.001
ADAM_B1 = 0.9
ADAM_B2 = 0.999
ADAM_EPS = 1e-08
ADAM_WD = 0.01
ADAM_STEP = 10

VMEM_LIMIT = 56 * 1024 * 1024
ROW_TILE = 512

NN = ((1,), (0,))
NT = ((1,), (1,))
TN = ((0,), (0,))
MESH = pl.DeviceIdType.MESH


def _dot(a, b, dims):
    return lax.dot_general(a, b, (dims, ((), ())), preferred_element_type=F32)


def _bdot(a, b, dims):
    return _dot(a.astype(BF16), b.astype(BF16), dims)


def _split(a):
    hi = a.astype(BF16)
    lo = (a - hi.astype(F32)).astype(BF16)
    return hi, lo


def _dot3(a, b, dims=NN):
    return _dot(a[0], b[0], dims) + (_dot(a[0], b[1], dims) + _dot(a[1], b[0], dims))


def _dot_exact_lhs(a, b):
    ab = a.astype(BF16)
    b1 = b.astype(BF16)
    r1 = b - b1.astype(F32)
    b2 = r1.astype(BF16)
    b3 = (r1 - b2.astype(F32)).astype(BF16)
    return _dot(ab, b1, NN) + (_dot(ab, b2, NN) + _dot(ab, b3, NN))


def _call(body, *, name, out_shape, in_specs, out_specs, grid=(), scratch=(), comm=None, **kw):
    params = dict(vmem_limit_bytes=VMEM_LIMIT)
    if grid:
        params["dimension_semantics"] = ("arbitrary",) * len(grid)
    if comm is None:
        return pl.pallas_call(
            body, name=name, grid=grid, in_specs=in_specs, out_specs=out_specs,
            out_shape=out_shape, scratch_shapes=list(scratch),
            compiler_params=pltpu.CompilerParams(**params), **kw)

    n_in, n_out, n_sc = len(in_specs), len(out_specs), len(scratch)
    c_in, c_out = len(comm.inputs), len(comm.out_shape)
    steps = 1
    for g in grid:
        steps *= g

    def hosted(*refs):
        ins, cins = refs[:n_in], refs[n_in:n_in + c_in]
        o0 = n_in + c_in
        outs, couts = refs[o0:o0 + n_out], refs[o0 + n_out:o0 + n_out + c_out]
        s0 = o0 + n_out + c_out
        sc, csems = refs[s0:s0 + n_sc], refs[s0 + n_sc:]
        lin = 0
        for axis, g in enumerate(grid):
            lin = lin * g + pl.program_id(axis)

        def at(step, fn):
            @pl.when(lin == step % steps)
            def _():
                fn(cins, couts, csems)

        for step, fn in comm.phases:
            if step >= 0:
                at(step, fn)
        body(*ins, *outs, *sc)
        for step, fn in comm.phases:
            if step < 0:
                at(step, fn)

    aliases = dict(kw.pop("input_output_aliases", {}))
    for k, m in comm.aliases.items():
        aliases[n_in + k] = n_out + m
    call = pl.pallas_call(
        hosted, name=name, grid=grid, in_specs=list(in_specs) + [_ANY] * c_in,
        out_specs=list(out_specs) + [_ANY] * c_out, out_shape=list(out_shape) + comm.out_shape,
        scratch_shapes=list(scratch) + comm.sems, input_output_aliases=aliases,
        compiler_params=pltpu.CompilerParams(**params), **kw)

    def run(*args):
        res = call(*args, *comm.inputs)
        return res[:n_out], res[n_out:]

    return run


def _sds(shape, dtype):
    return jax.ShapeDtypeStruct(tuple(shape), dtype)


def _resident(shape):
    zeros = (0,) * len(shape)
    return pl.BlockSpec(tuple(shape), lambda *_: zeros, pipeline_mode=pl.Buffered(1))


def _sigmoid(x):
    return jax.nn.sigmoid(x)


def _softplus(x):
    return jnp.maximum(x, 0.0) + jnp.log(1.0 + jnp.exp(-jnp.abs(x)))


_GELU_C = 0.7978845608028654
_GELU_A = 0.044715


def _gelu_tanh(x):
    return jnp.tanh(_GELU_C * (x + _GELU_A * x * x * x))


def _gelu(x, t):
    return 0.5 * x * (1.0 + t)


def _gelu_grad(x, t):
    return 0.5 * (1.0 + t) + 0.5 * x * (1.0 - t * t) * _GELU_C * (1.0 + 3.0 * _GELU_A * x * x)


def _silu_grad(x):
    s = _sigmoid(x)
    return s * (1.0 + x * (1.0 - s))


def _rms_scale(xv):
    return lax.rsqrt(jnp.mean(xv * xv, axis=-1, keepdims=True) + EPS)


def _rms_bwd(dh, xv, g):
    r = _rms_scale(xv)
    xn = xv * r
    dg = jnp.sum(dh * xn, axis=0, keepdims=True)
    dxn = dh * g
    dx = r * (dxn - xn * jnp.mean(dxn * xn, axis=-1, keepdims=True))
    return dx, dg


def _iota2(shape, dim):
    return lax.broadcasted_iota(jnp.int32, shape, dim)


def _col_to_row(col):
    n = col.shape[0]
    eye = _iota2((n, n), 0) == _iota2((n, n), 1)
    return jnp.sum(jnp.where(eye, col, 0.0), axis=0, keepdims=True)


def _row_to_col(row):
    n = row.shape[1]
    eye = _iota2((n, n), 0) == _iota2((n, n), 1)
    return jnp.sum(jnp.where(eye, row, 0.0), axis=1, keepdims=True)


MXU_DIM = 256


def _hidden_chunks(f, step=3 * MXU_DIM):
    return [(c0, min(c0 + step, f)) for c0 in range(0, f, step)]

def ffn_fwd(x, gnorm, wg, wu, wd, name, comm=None):
    n, d = x.shape
    f = wg.shape[0]
    tm = min(ROW_TILE, n)
    fused = wd is not None

    def body(x_ref, g_ref, wg_ref, wu_ref, *rest):
        if fused:
            wd_ref, xo_ref, h_ref, gate_ref, up_ref, act_ref, acc_ref = rest
        else:
            h_ref, gate_ref, up_ref, act_ref = rest
        xv = x_ref[...]
        h = (xv * _rms_scale(xv) * g_ref[...]).astype(BF16)
        h_ref[...] = h
        chunks = _hidden_chunks(f, 2 * MXU_DIM)

        def gate_up(c0, c1):
            return _dot(h, wg_ref[c0:c1, :], NT), _dot(h, wu_ref[c0:c1, :], NT)

        nxt = gate_up(*chunks[0])
        for idx, (c0, c1) in enumerate(chunks):
            gate, up = nxt
            if idx + 1 < len(chunks):
                nxt = gate_up(*chunks[idx + 1])
            act = (gate * _sigmoid(gate) * up).astype(BF16)
            gate_ref[:, c0:c1] = gate.astype(BF16)
            up_ref[:, c0:c1] = up.astype(BF16)
            act_ref[:, c0:c1] = act
            if fused:
                part = _dot(act, wd_ref[c0:c1, :], NN)
                if c0 == 0:
                    acc_ref[...] = part
                else:
                    acc_ref[...] += part
        if fused:
            xo_ref[...] = xv + 0.5 * acc_ref[...]

    row = pl.BlockSpec((tm, d), lambda i: (i, 0))
    wide = pl.BlockSpec((tm, f), lambda i: (i, 0))
    n_w = 3 if fused else 2
    return _call(
        body, name=name, grid=(n // tm,),
        in_specs=[row, pl.BlockSpec((1, d), lambda i: (0, 0))] + [_resident((f, d))] * n_w,
        out_specs=([row] if fused else []) + [row, wide, wide, wide],
        out_shape=([_sds((n, d), F32)] if fused else []) + [_sds((n, d), BF16)] + [_sds((n, f), BF16)] * 3,
        scratch=[pltpu.VMEM((tm, d), F32)] if fused else [], comm=comm,
    )(*([x, gnorm, wg, wu] + ([wd] if fused else [])))


def ffn_down(x, act, wd, name, comm=None):
    n, d = x.shape
    f = wd.shape[0]
    tm = min(ROW_TILE, n)

    def body(x_ref, a_ref, w_ref, o_ref):
        o_ref[...] = x_ref[...] + 0.5 * _dot(a_ref[...], w_ref[...], NN)

    row = pl.BlockSpec((tm, d), lambda i: (i, 0))
    return _call(
        body, name=name, grid=(n // tm,),
        in_specs=[row, pl.BlockSpec((tm, f), lambda i: (i, 0)), _resident((f, d))],
        out_specs=[row], out_shape=[_sds((n, d), F32)], comm=comm,
    )(x, act, wd)


def ffn_bwd_act(dy, x, gnorm, gate, up, wg, wu, wd, name, comm=None):
    n, d = x.shape
    f = wg.shape[0]
    tm = min(ROW_TILE // 2, n)

    def body(dy_ref, x_ref, g_ref, gate_ref, up_ref, wg_ref, wu_ref, wd_ref,
             dx_ref, dgate_ref, dup_ref, dyh_ref, dg_ref, acc_ref):
        @pl.when(pl.program_id(0) == 0)
        def _():
            dg_ref[...] = jnp.zeros_like(dg_ref)

        dyh = (0.5 * dy_ref[...]).astype(BF16)
        dyh_ref[...] = dyh
        chunks = _hidden_chunks(f, 2 * MXU_DIM)
        next_dact = _dot(dyh, wd_ref[chunks[0][0]:chunks[0][1], :], NT)
        for idx, (c0, c1) in enumerate(chunks):
            dact = next_dact
            if idx + 1 < len(chunks):
                n0, n1 = chunks[idx + 1]
                next_dact = _dot(dyh, wd_ref[n0:n1, :], NT)
            gt = gate_ref[:, c0:c1].astype(F32)
            u = up_ref[:, c0:c1].astype(F32)
            s = _sigmoid(gt)
            dup = (dact * (gt * s)).astype(BF16)
            dgate = (dact * u * (s * (1.0 + gt * (1.0 - s)))).astype(BF16)
            dup_ref[:, c0:c1] = dup
            dgate_ref[:, c0:c1] = dgate
            part = _dot(dgate, wg_ref[c0:c1, :], NN) + _dot(dup, wu_ref[c0:c1, :], NN)
            if c0 == 0:
                acc_ref[...] = part
            else:
                acc_ref[...] += part
        dxn, dg = _rms_bwd(acc_ref[...], x_ref[...], g_ref[...])
        dx_ref[...] = dy_ref[...] + dxn
        dg_ref[...] += dg

    row = pl.BlockSpec((tm, d), lambda i: (i, 0))
    wide = pl.BlockSpec((tm, f), lambda i: (i, 0))
    vec = pl.BlockSpec((1, d), lambda i: (0, 0))
    wres = _resident((f, d))
    return _call(
        body, name=name, grid=(n // tm,),
        in_specs=[row, row, vec, wide, wide, wres, wres, wres],
        out_specs=[row, wide, wide, row, vec],
        out_shape=[_sds((n, d), F32), _sds((n, f), BF16), _sds((n, f), BF16),
                   _sds((n, d), BF16), _sds((1, d), F32)],
        scratch=[pltpu.VMEM((tm, d), F32)], comm=comm,
    )(dy, x, gnorm, gate, up, wg, wu, wd)


def ffn_bwd_w(wide, rows, pairs, name, comm=None):
    n, d = rows[0].shape
    f = wide[0].shape[1]
    fh = f // 2
    tk = min(ROW_TILE, n)
    n_w, n_r = len(wide), len(rows)

    def body(*refs):
        wide_refs, row_refs, outs = refs[:n_w], refs[n_w:n_w + n_r], refs[n_w + n_r:]

        @pl.when(pl.program_id(1) == 0)
        def _():
            for o_ref in outs:
                o_ref[...] = jnp.zeros_like(o_ref)

        row_vals = [r[...] for r in row_refs]
        for c0, c1 in _hidden_chunks(fh, 2 * MXU_DIM):
            for (i, k), o_ref in zip(pairs, outs):
                o_ref[c0:c1, :] += _dot(wide_refs[i][:, c0:c1], row_vals[k], TN)

    row = pl.BlockSpec((tk, d), lambda j, k: (k, 0))
    blk = pl.BlockSpec((tk, fh), lambda j, k: (k, j))
    return _call(
        body, name=name, grid=(2, n // tk),
        in_specs=[blk] * n_w + [row] * n_r,
        out_specs=[pl.BlockSpec((fh, d), lambda j, k: (j, 0))] * len(pairs),
        out_shape=[_sds((f, d), F32)] * len(pairs), comm=comm,
    )(*wide, *rows)


def final_loss(x, gnorm, target, name):
    n, d = x.shape
    tm = min(ROW_TILE, n)

    def body(x_ref, g_ref, t_ref, dx_ref, dg_ref, loss_ref):
        @pl.when(pl.program_id(0) == 0)
        def _():
            dg_ref[...] = jnp.zeros_like(dg_ref)
            loss_ref[...] = jnp.zeros_like(loss_ref)

        xv = x_ref[...]
        y = xv * _rms_scale(xv) * g_ref[...]
        err = y - t_ref[...]
        part = 0.5 * jnp.sum(jnp.mean(err * err, axis=-1, keepdims=True), axis=0, keepdims=True)
        loss_ref[...] += jnp.broadcast_to(part, loss_ref.shape)
        dx, dg = _rms_bwd(err * (1.0 / d), xv, g_ref[...])
        dx_ref[...] = dx
        dg_ref[...] += dg

    row = pl.BlockSpec((tm, d), lambda i: (i, 0))
    vec = pl.BlockSpec((1, d), lambda i: (0, 0))
    return _call(
        body, name=name, grid=(n // tm,),
        in_specs=[row, vec, row],
        out_specs=[row, vec, pl.BlockSpec((1, LANES), lambda i: (0, 0))],
        out_shape=[_sds((n, d), F32), _sds((1, d), F32), _sds((1, LANES), F32)],
    )(x, gnorm, target)


def in_proj_fwd(x, gnorm, w, name, comm=None):
    n, d = x.shape
    cols = w.shape[1]
    tm = min(ROW_TILE, n)

    def body(x_ref, g_ref, w_ref, p_ref, h_ref):
        xv = x_ref[...]
        h = (xv * _rms_scale(xv) * g_ref[...]).astype(BF16)
        h_ref[...] = h
        for c0, c1 in _hidden_chunks(cols):
            p_ref[:, c0:c1] = _dot(h, w_ref[:, c0:c1], NN)

    return _call(
        body, name=name, grid=(n // tm,),
        in_specs=[pl.BlockSpec((tm, d), lambda i: (i, 0)),
                  pl.BlockSpec((1, d), lambda i: (0, 0)), _resident((d, cols))],
        out_specs=[pl.BlockSpec((tm, cols), lambda i: (i, 0)),
                   pl.BlockSpec((tm, d), lambda i: (i, 0))],
        out_shape=[_sds((n, cols), F32), _sds((n, d), BF16)], comm=comm,
    )(x, gnorm, w)


def in_proj_bwd_x(dproj, w, x, gnorm, dres, name, comm=None):
    n, d = x.shape
    cols = w.shape[1]
    tm = min(ROW_TILE, n)

    def body(dp_ref, w_ref, x_ref, g_ref, dr_ref, dx_ref, dg_ref, dxh_ref):
        @pl.when(pl.program_id(0) == 0)
        def _():
            dg_ref[...] = jnp.zeros_like(dg_ref)

        dh = _dot(dp_ref[...], w_ref[...], NT)
        dxn, dg = _rms_bwd(dh, x_ref[...], g_ref[...])
        dx = dr_ref[...] + dxn
        dx_ref[...] = dx
        dxh_ref[...] = (0.5 * dx).astype(BF16)
        dg_ref[...] += dg

    row = pl.BlockSpec((tm, d), lambda i: (i, 0))
    vec = pl.BlockSpec((1, d), lambda i: (0, 0))
    return _call(
        body, name=name, grid=(n // tm,),
        in_specs=[pl.BlockSpec((tm, cols), lambda i: (i, 0)),
                  _resident((d, cols)), row, vec, row],
        out_specs=[row, vec, row],
        out_shape=[_sds((n, d), F32), _sds((1, d), F32), _sds((n, d), BF16)], comm=comm,
    )(dproj, w, x, gnorm, dres)


def matmul_tn(a_list, b, tn, name):
    n, cb = b.shape
    widths = [a.shape[1] for a in a_list]
    tk = min(ROW_TILE, n)

    def body(*refs):
        a_refs, b_ref, o_ref = refs[:-2], refs[-2], refs[-1]

        @pl.when(pl.program_id(0) == 0)
        def _():
            o_ref[...] = jnp.zeros_like(o_ref)

        r0 = 0
        for a_ref, ka in zip(a_refs, widths):
            av = a_ref[...]
            for c0, c1 in _hidden_chunks(cb, tn):
                o_ref[r0:r0 + ka, c0:c1] += _dot(av, b_ref[:, c0:c1], TN)
            r0 += ka

    return _call(
        body, name=name, grid=(n // tk,),
        in_specs=[pl.BlockSpec((tk, ka), lambda k: (k, 0)) for ka in widths]
        + [pl.BlockSpec((tk, cb), lambda k: (k, 0))],
        out_specs=pl.BlockSpec((sum(widths), cb), lambda k: (0, 0)),
        out_shape=_sds((sum(widths), cb), F32),
    )(*a_list, b)


def out_proj_fwd(x, sg_out, dn_out, w, name):
    n, d = x.shape
    tm = min(ROW_TILE, n)

    def body(x_ref, a_ref, b_ref, w_ref, o_ref):
        o_ref[...] = (x_ref[...] + _dot(a_ref[...], w_ref[0:HALF_W, :], NN)
                      + _dot(b_ref[...], w_ref[HALF_W:2 * HALF_W, :], NN))

    row = pl.BlockSpec((tm, d), lambda i: (i, 0))
    half = pl.BlockSpec((tm, HALF_W), lambda i: (i, 0))
    return _call(
        body, name=name, grid=(n // tm,),
        in_specs=[row, half, half, pl.BlockSpec((2 * HALF_W, d), lambda i: (0, 0))],
        out_specs=row, out_shape=_sds((n, d), F32),
    )(x, sg_out, dn_out, w)


def out_proj_bwd_x(dy, w, name, comm=None):
    n, d = dy.shape
    tm = min(ROW_TILE, n)

    def body(dy_ref, w_ref, dsg_ref, ddn_ref, dyb_ref):
        dyb = dy_ref[...].astype(BF16)
        dyb_ref[...] = dyb
        dsg_ref[...] = _dot(dyb, w_ref[0:HALF_W, :], NT)
        ddn_ref[...] = _dot(dyb, w_ref[HALF_W:2 * HALF_W, :], NT)

    row = pl.BlockSpec((tm, d), lambda i: (i, 0))
    half = pl.BlockSpec((tm, HALF_W), lambda i: (i, 0))
    return _call(
        body, name=name, grid=(n // tm,),
        in_specs=[row, pl.BlockSpec((2 * HALF_W, d), lambda i: (0, 0))],
        out_specs=[half, half, row],
        out_shape=[_sds((n, HALF_W), F32), _sds((n, HALF_W), F32), _sds((n, d), BF16)], comm=comm,
    )(dy, w)


SG_PAIRS = SG_GROUPS // 2


def _sg_low_half():
    return _iota2((SG_CHUNK, LANES), 1) < SG_GROUP_DIM


def _sg_pair_cols(p):
    return slice(p * LANES, (p + 1) * LANES)


def _sg_causal():
    return _iota2((SG_CHUNK, SG_CHUNK), 0) >= _iota2((SG_CHUNK, SG_CHUNK), 1)


def _sg_forward_chunk(pu, pv, ln_g, ln_b, wc, bias, low):
    tu, tv = _gelu_tanh(pu), _gelu_tanh(pv)
    u = _gelu(pu, tu)
    v = _gelu(pv, tv)
    mu = jnp.mean(v, axis=-1, keepdims=True)
    vc = v - mu
    rs = lax.rsqrt(jnp.mean(vc * vc, axis=-1, keepdims=True) + EPS)
    xhat = vc * rs
    vn = (xhat * ln_g + ln_b).astype(BF16)
    parts = []
    for p in range(SG_PAIRS):
        vn_p = vn[:, _sg_pair_cols(p)]
        parts.append(jnp.where(low, _dot(wc[2 * p], vn_p, NN), _dot(wc[2 * p + 1], vn_p, NN)))
    vs = bias + jnp.concatenate(parts, axis=1)
    return u, xhat, rs, vn, vs, tu, tv


def sg_fwd(proj, ln_g, ln_b, w_s, bias_tile, name):
    n = proj.shape[0]
    tm = min(ROW_TILE, n)

    def body(pu_ref, pv_ref, g_ref, b_ref, w_ref, bias_ref, o_ref):
        causal = _sg_causal()
        wc = [jnp.where(causal, w_ref[g], 0.0).astype(BF16) for g in range(SG_GROUPS)]
        masks = _sg_low_half()
        for ci in range(tm // SG_CHUNK):
            rows = slice(ci * SG_CHUNK, (ci + 1) * SG_CHUNK)
            u, _, _, _, vs, _, _ = _sg_forward_chunk(pu_ref[rows, :], pv_ref[rows, :], g_ref[...],
                                                     b_ref[...], wc, bias_ref[...], masks)
            o_ref[rows, :] = (u * vs).astype(BF16)

    vec = pl.BlockSpec((1, HALF_W), lambda i: (0, 0))
    return _call(
        body, name=name, grid=(n // tm,),
        in_specs=[pl.BlockSpec((tm, HALF_W), lambda i: (i, 0)),
                  pl.BlockSpec((tm, HALF_W), lambda i: (i, 1)), vec, vec,
                  pl.BlockSpec((SG_GROUPS, SG_CHUNK, SG_CHUNK), lambda i: (0, 0, 0)),
                  pl.BlockSpec((SG_CHUNK, HALF_W), lambda i: (0, 0))],
        out_specs=pl.BlockSpec((tm, HALF_W), lambda i: (i, 0)),
        out_shape=_sds((n, HALF_W), BF16),
    )(proj, proj, ln_g, ln_b, w_s, bias_tile)


def sg_bwd(dsg, proj, ln_g, ln_b, w_s, bias_tile, name):
    n = proj.shape[0]
    tm = min(ROW_TILE, n)

    def body(d_ref, pu_ref, pv_ref, g_ref, b_ref, w_ref, bias_ref,
             dp_ref, dw_ref, db_ref, dlg_ref, dlb_ref):
        @pl.when(pl.program_id(0) == 0)
        def _():
            dw_ref[...] = jnp.zeros_like(dw_ref)
            db_ref[...] = jnp.zeros_like(db_ref)
            dlg_ref[...] = jnp.zeros_like(dlg_ref)
            dlb_ref[...] = jnp.zeros_like(dlb_ref)

        causal = _sg_causal()
        wc = [jnp.where(causal, w_ref[g], 0.0).astype(BF16) for g in range(SG_GROUPS)]
        masks = _sg_low_half()
        ln_g_v = g_ref[...]
        for ci in range(tm // SG_CHUNK):
            rows = slice(ci * SG_CHUNK, (ci + 1) * SG_CHUNK)
            pu = pu_ref[rows, :]
            pv = pv_ref[rows, :]
            u, xhat, rs, vn, vs, tu, tv = _sg_forward_chunk(pu, pv, ln_g_v, b_ref[...], wc,
                                                            bias_ref[...], masks)
            dout = d_ref[rows, :]
            dp_ref[rows, 0:HALF_W] = (dout * vs * _gelu_grad(pu, tu)).astype(BF16)
            dvs = dout * u
            dvs_b = dvs.astype(BF16)
            db_ref[...] += dvs
            dvn_parts = []
            for p in range(SG_PAIRS):
                dvs_p = dvs_b[:, _sg_pair_cols(p)]
                vn_p = vn[:, _sg_pair_cols(p)]
                dvn_parts.append(jnp.where(masks, _dot(wc[2 * p], dvs_p, TN), _dot(wc[2 * p + 1], dvs_p, TN)))
                zero = jnp.zeros_like(dvs_p)
                dw_ref[2 * p] += jnp.where(causal, _dot(jnp.where(masks, dvs_p, zero), vn_p, NT), 0.0)
                dw_ref[2 * p + 1] += jnp.where(causal, _dot(jnp.where(masks, zero, dvs_p), vn_p, NT), 0.0)
            dvn = jnp.concatenate(dvn_parts, axis=1)
            dlg_ref[...] += jnp.sum(dvn * xhat, axis=0, keepdims=True)
            dlb_ref[...] += jnp.sum(dvn, axis=0, keepdims=True)
            dxh = dvn * ln_g_v
            dv = rs * (dxh - jnp.mean(dxh, axis=-1, keepdims=True)
                       - xhat * jnp.mean(dxh * xhat, axis=-1, keepdims=True))
            dp_ref[rows, HALF_W:2 * HALF_W] = (dv * _gelu_grad(pv, tv)).astype(BF16)

    vec = pl.BlockSpec((1, HALF_W), lambda i: (0, 0))
    wspec = pl.BlockSpec((SG_GROUPS, SG_CHUNK, SG_CHUNK), lambda i: (0, 0, 0))
    tile = pl.BlockSpec((SG_CHUNK, HALF_W), lambda i: (0, 0))
    return _call(
        body, name=name, grid=(n // tm,),
        in_specs=[pl.BlockSpec((tm, HALF_W), lambda i: (i, 0)),
                  pl.BlockSpec((tm, HALF_W), lambda i: (i, 0)),
                  pl.BlockSpec((tm, HALF_W), lambda i: (i, 1)), vec, vec, wspec, tile],
        out_specs=[pl.BlockSpec((tm, 2 * HALF_W), lambda i: (i, 0)), wspec, tile, vec, vec],
        out_shape=[_sds((n, PROJ_W), BF16), _sds((SG_GROUPS, SG_CHUNK, SG_CHUNK), F32),
                   _sds((SG_CHUNK, HALF_W), F32), _sds((1, HALF_W), F32), _sds((1, HALF_W), F32)],
    )(dsg, proj, proj, ln_g, ln_b, w_s, bias_tile)


CONV_K = 4
CONV_BLOCK = 256


def _shift_down(x, s, row):
    if s == 0:
        return x
    return jnp.where(row >= s, pltpu.roll(x, s, 0), 0.0)


def _shift_up(x, s, row):
    if s == 0:
        return x
    t_len = x.shape[0]
    return jnp.where(row < t_len - s, pltpu.roll(x, t_len - s, 0), 0.0)


def _conv_taps(x, row):
    return [_shift_down(x, CONV_K - 1 - j, row) for j in range(CONV_K)]


def _conv(taps, w):
    y = taps[0] * w[0:1, :]
    for j in range(1, CONV_K):
        y = y + taps[j] * w[j:j + 1, :]
    return y


def dn_conv_fwd(proj3, conv_w, name):
    b, t, _ = proj3.shape
    nblk = 3 * HALF_W // CONV_BLOCK
    first = 2 * HALF_W // CONV_BLOCK
    n_norm = 2 * HALF_W // CONV_BLOCK

    def body(x_ref, w_ref, o_ref):
        s = pl.program_id(1)
        x = x_ref[0]
        y = _conv(_conv_taps(x, _iota2(x.shape, 0)), w_ref[...])
        y = y * _sigmoid(y)

        @pl.when(s < n_norm)
        def _():
            for h in range(CONV_BLOCK // HEAD_DIM):
                cs = slice(h * HEAD_DIM, (h + 1) * HEAD_DIM)
                yh = y[:, cs]
                o_ref[0, :, cs] = yh * lax.rsqrt(jnp.sum(yh * yh, axis=-1, keepdims=True) + EPS)

        @pl.when(s >= n_norm)
        def _():
            o_ref[0] = y

    return _call(
        body, name=name, grid=(b, nblk),
        in_specs=[pl.BlockSpec((1, t, CONV_BLOCK), lambda i, s: (i, 0, first + s)),
                  pl.BlockSpec((CONV_K, CONV_BLOCK), lambda i, s: (0, s))],
        out_specs=pl.BlockSpec((1, t, CONV_BLOCK), lambda i, s: (i, 0, s)),
        out_shape=_sds((b, t, 3 * HALF_W), F32),
    )(proj3, conv_w)


def dn_conv_bwd(dqkv, proj3, conv_w, dproj3, name, comm=None):
    b, t, _ = proj3.shape
    nblk = 3 * HALF_W // CONV_BLOCK
    first = 2 * HALF_W // CONV_BLOCK
    n_norm = 2 * HALF_W // CONV_BLOCK

    def body(d_ref, x_ref, w_ref, dproj_in, dx_ref, dw_ref, ds_ref):
        s = pl.program_id(0)

        @pl.when(pl.program_id(1) == 0)
        def _():
            dw_ref[...] = jnp.zeros_like(dw_ref)

        x = x_ref[0]
        w = w_ref[...]
        row = _iota2(x.shape, 0)
        taps = _conv_taps(x, row)
        c = _conv(taps, w)
        sg = _sigmoid(c)
        y = c * sg

        @pl.when(s < n_norm)
        def _():
            for h in range(CONV_BLOCK // HEAD_DIM):
                cs = slice(h * HEAD_DIM, (h + 1) * HEAD_DIM)
                yh = y[:, cs]
                r = lax.rsqrt(jnp.sum(yh * yh, axis=-1, keepdims=True) + EPS)
                nh = yh * r
                dn = d_ref[0, :, cs]
                ds_ref[:, cs] = r * (dn - nh * jnp.sum(dn * nh, axis=-1, keepdims=True))

        @pl.when(s >= n_norm)
        def _():
            ds_ref[...] = d_ref[0]

        dc = ds_ref[...] * (sg * (1.0 + c * (1.0 - sg)))
        dx = _shift_up(dc, CONV_K - 1, row) * w[0:1, :]
        for j in range(1, CONV_K):
            dx = dx + _shift_up(dc, CONV_K - 1 - j, row) * w[j:j + 1, :]
        dx_ref[0] = dx.astype(BF16)
        for j in range(CONV_K):
            dw_ref[j:j + 1, :] += jnp.sum(dc * taps[j], axis=0, keepdims=True)

    return _call(
        body, name=name, grid=(nblk, b),
        in_specs=[pl.BlockSpec((1, t, CONV_BLOCK), lambda s, i: (i, 0, s)),
                  pl.BlockSpec((1, t, CONV_BLOCK), lambda s, i: (i, 0, first + s)),
                  pl.BlockSpec((CONV_K, CONV_BLOCK), lambda s, i: (0, s)), _ANY],
        out_specs=[pl.BlockSpec((1, t, CONV_BLOCK), lambda s, i: (i, 0, first + s)),
                   pl.BlockSpec((CONV_K, CONV_BLOCK), lambda s, i: (0, s))],
        out_shape=[_sds(dproj3.shape, BF16), _sds((CONV_K, 3 * HALF_W), F32)],
        scratch=[pltpu.VMEM((t, CONV_BLOCK), F32)],
        input_output_aliases={3: 0}, comm=comm,
    )(dqkv, proj3, conv_w, dproj3)


def _chunk_masks():
    ii = _iota2((DN_CHUNK, DN_CHUNK), 0)
    jj = _iota2((DN_CHUNK, DN_CHUNK), 1)
    return ii >= jj, ii > jj, ii == jj


LOCKSTEP_CHUNKS = 4


def _inv_unit_lower_many(l_mats, eye):
    eye_f = jnp.where(eye, 1.0, 0.0)
    ps = [-l for l in l_mats]
    ts = [eye_f + p for p in ps]
    pss = [_split(p) for p in ps]
    size = 2
    while size < DN_CHUNK:
        ps = [_dot3(s, s) for s in pss]
        pss = [_split(p) for p in ps]
        ts = [t + _dot3(_split(t), s) for t, s in zip(ts, pss)]
        size *= 2
    return ts


def _gates(pba, ea_row, dtb_row):
    beta = _sigmoid(pba)
    g = -ea_row * _softplus(pba + dtb_row)
    return beta, g


def _chunk_decay(gcol):
    incl, strict, eye = _chunk_masks()
    grow = jnp.sum(jnp.where(eye, gcol, 0.0), axis=0, keepdims=True)
    decay = jnp.where(incl, jnp.exp(jnp.where(incl, gcol - grow, 0.0)), 0.0)
    return decay, incl, strict, eye


def dn_chunk_fwd(qkv, proj3, alog_row, dtb_row, name, comm=None):
    b, t, _ = qkv.shape
    rblk = min(256, t)
    n_in = rblk // DN_CHUNK

    def body(q_ref, k_ref, v_ref, pba_ref, al_ref, dtb_ref,
             u_ref, w_ref, qd_ref, kd_ref, qk_ref, ti_ref, gc_ref):
        ea = jnp.exp(al_ref[...])
        tri = jnp.where(_chunk_masks()[0], 1.0, 0.0)

        _, strict, eye = _chunk_masks()

        def chunk_group(cg, carry):
            items = []
            for sub in range(LOCKSTEP_CHUNKS):
                rows = pl.ds(pl.multiple_of((cg * LOCKSTEP_CHUNKS + sub) * DN_CHUNK, DN_CHUNK), DN_CHUNK)
                beta_all, g_all = _gates(pba_ref[0, rows, :], ea, dtb_ref[...])
                gc = _dot_exact_lhs(tri, g_all)
                gc_ref[0, rows, :] = gc
                for h in range(N_HEADS):
                    items.append((rows, h, beta_all[:, h:h + 1], gc[:, N_HEADS + h:N_HEADS + h + 1]))
            ks, kbs, decays, egs = [], [], [], []
            for rows, h, beta, gcol in items:
                cs = slice(h * HEAD_DIM, (h + 1) * HEAD_DIM)
                k = k_ref[0, rows, cs]
                ks.append(k)
                kbs.append(k * beta)
                decays.append(_chunk_decay(gcol)[0])
                egs.append(jnp.exp(gcol))
            ms = [_bdot(kb, k, NT) for kb, k in zip(kbs, ks)]
            tinvs = _inv_unit_lower_many([jnp.where(strict, m * dc, 0.0) for m, dc in zip(ms, decays)], eye)
            tsps = [_split(t) for t in tinvs]
            for (rows, h, beta, gcol), tsp, tinv in zip(items, tsps, tinvs):
                cs = slice(h * HEAD_DIM, (h + 1) * HEAD_DIM)
                u_ref[0, rows, cs] = _dot3(tsp, _split(v_ref[0, rows, cs] * beta))
                ti_ref[0, h, rows, :] = tinv
            for (rows, h, beta, gcol), tsp, kb, eg in zip(items, tsps, kbs, egs):
                cs = slice(h * HEAD_DIM, (h + 1) * HEAD_DIM)
                w_ref[0, rows, cs] = _dot3(tsp, _split(kb * eg))
            for (rows, h, beta, gcol), k, dc, eg in zip(items, ks, decays, egs):
                cs = slice(h * HEAD_DIM, (h + 1) * HEAD_DIM)
                q = q_ref[0, rows, cs] * QK_SCALE
                qk_ref[0, h, rows, :] = _bdot(q, k, NT) * dc
                qd_ref[0, rows, cs] = q * eg
                kd_ref[0, rows, cs] = k * jnp.exp(gcol[DN_CHUNK - 1:DN_CHUNK, :] - gcol)
            return carry

        lax.fori_loop(0, n_in // LOCKSTEP_CHUNKS, chunk_group, 0)

    def seg(cblk):
        return pl.BlockSpec((1, rblk, HALF_W), lambda i, r: (i, r, cblk))

    vec = pl.BlockSpec((1, LANES), lambda i, r: (0, 0))
    wide = pl.BlockSpec((1, rblk, HALF_W), lambda i, r: (i, r, 0))
    sq = pl.BlockSpec((1, N_HEADS, rblk, DN_CHUNK), lambda i, r: (i, 0, r, 0))
    return _call(
        body, name=name, grid=(b, t // rblk),
        in_specs=[seg(0), seg(1), seg(2),
                  pl.BlockSpec((1, rblk, LANES), lambda i, r: (i, r, GATE_COL_BLOCK)), vec, vec],
        out_specs=[wide, wide, wide, wide, sq, sq,
                   pl.BlockSpec((1, rblk, LANES), lambda i, r: (i, r, 0))],
        out_shape=[_sds((b, t, HALF_W), F32)] * 4
        + [_sds((b, N_HEADS, t, DN_CHUNK), F32)] * 2 + [_sds((b, t, LANES), F32)], comm=comm,
    )(qkv, qkv, qkv, proj3, alog_row, dtb_row)


def dn_scan_fwd(u, w, qd, kd, qk, gc, name):
    b, t, _ = u.shape
    nc = t // DN_CHUNK
    bh = b * N_HEADS

    def body(u_ref, w_ref, qd_ref, kd_ref, qk_ref, gc_ref, o_ref, sin_ref, s_ref):
        @pl.when(pl.program_id(0) == 0)
        def _():
            s_ref[...] = jnp.zeros_like(s_ref)

        items = [(bi, h, slice(h * HEAD_DIM, (h + 1) * HEAD_DIM)) for bi in range(b) for h in range(N_HEADS)]
        sbs = []
        for bi, h, cs in items:
            s = s_ref[bi * N_HEADS + h]
            sin_ref[0, bi * N_HEADS + h] = s
            sbs.append(s.astype(BF16))
        ws = [_bdot(w_ref[bi, :, cs], sb, NN) for (bi, h, cs), sb in zip(items, sbs)]
        qs = [_bdot(qd_ref[bi, :, cs], sb, NN) for (bi, h, cs), sb in zip(items, sbs)]
        vbs = [(u_ref[bi, :, cs] - wsi).astype(BF16) for (bi, h, cs), wsi in zip(items, ws)]
        for (bi, h, cs), qsi, vb in zip(items, qs, vbs):
            o_ref[bi, :, cs] = qsi + _bdot(qk_ref[bi, h], vb, NN)
        for (bi, h, cs), vb in zip(items, vbs):
            gl = jnp.exp(gc_ref[bi, DN_CHUNK - 1:DN_CHUNK, N_HEADS + h:N_HEADS + h + 1])
            idx = bi * N_HEADS + h
            s_ref[idx] = s_ref[idx] * gl + _bdot(kd_ref[bi, :, cs], vb, TN)

    wide = pl.BlockSpec((b, DN_CHUNK, HALF_W), lambda c: (0, c, 0))
    return _call(
        body, name=name, grid=(nc,),
        in_specs=[wide, wide, wide, wide,
                  pl.BlockSpec((b, N_HEADS, DN_CHUNK, DN_CHUNK), lambda c: (0, 0, c, 0)),
                  pl.BlockSpec((b, DN_CHUNK, LANES), lambda c: (0, c, 0))],
        out_specs=[wide, pl.BlockSpec((1, bh, HEAD_DIM, HEAD_DIM), lambda c: (c, 0, 0, 0))],
        out_shape=[_sds((b, t, HALF_W), F32), _sds((nc, bh, HEAD_DIM, HEAD_DIM), F32)],
        scratch=[pltpu.VMEM((bh, HEAD_DIM, HEAD_DIM), F32)],
    )(u, w, qd, kd, qk, gc)


def dn_scan_bwd(do, u, w, qd, kd, qk, gc, s_in, name):
    b, t, _ = u.shape
    nc = t // DN_CHUNK
    bh = b * N_HEADS

    def body(do_ref, u_ref, w_ref, qd_ref, kd_ref, qk_ref, gc_ref, sin_ref,
             du_ref, dw_ref, dqd_ref, dkd_ref, dqk_ref, dgc_ref, ds_ref):
        @pl.when(pl.program_id(0) == 0)
        def _():
            ds_ref[...] = jnp.zeros_like(ds_ref)

        last_row = _iota2((DN_CHUNK, LANES), 0) == DN_CHUNK - 1
        lane = _iota2((DN_CHUNK, LANES), 1)
        items = [(bi, h, slice(h * HEAD_DIM, (h + 1) * HEAD_DIM)) for bi in range(b) for h in range(N_HEADS)]
        sbs = [sin_ref[0, bi * N_HEADS + h].astype(BF16) for bi, h, cs in items]
        wvs = [w_ref[bi, :, cs].astype(BF16) for bi, h, cs in items]
        dovs = [do_ref[bi, :, cs].astype(BF16) for bi, h, cs in items]
        dsbs = [ds_ref[bi * N_HEADS + h].astype(BF16) for bi, h, cs in items]
        vbs = [(u_ref[bi, :, cs] - _dot(wv, sb, NN)).astype(BF16)
               for (bi, h, cs), wv, sb in zip(items, wvs, sbs)]
        for (bi, h, cs), dov, sb in zip(items, dovs, sbs):
            dqd_ref[bi, :, cs] = _dot(dov, sb, NT)
        dvns = [_dot(kd_ref[bi, :, cs].astype(BF16), dsb, NN) + _dot(qk_ref[bi, h].astype(BF16), dov, TN)
                for (bi, h, cs), dsb, dov in zip(items, dsbs, dovs)]
        for (bi, h, cs), vb, dsb, dov in zip(items, vbs, dsbs, dovs):
            dkd_ref[bi, :, cs] = _dot(vb, dsb, NT)
            dqk_ref[bi, h] = _dot(dov, vb, NT)
        dgls = []
        for (bi, h, cs), dvn, sb, wv, dov in zip(items, dvns, sbs, wvs, dovs):
            idx = bi * N_HEADS + h
            du_ref[bi, :, cs] = dvn
            dvn_b = dvn.astype(BF16)
            dw_ref[bi, :, cs] = -_dot(dvn_b, sb, NT)
            gl = jnp.exp(gc_ref[bi, DN_CHUNK - 1:DN_CHUNK, N_HEADS + h:N_HEADS + h + 1])
            ds = ds_ref[idx]
            dgl = jnp.sum(jnp.sum(ds * sin_ref[0, idx], axis=1, keepdims=True), axis=0, keepdims=True)
            dgls.append(dgl * gl)
            ds_ref[idx] = (ds * gl + _dot(qd_ref[bi, :, cs].astype(BF16), dov, TN)
                           - _dot(wv, dvn_b, TN))
        for bi in range(b):
            dgc = jnp.zeros((DN_CHUNK, LANES), F32)
            for h in range(N_HEADS):
                dgc = dgc + jnp.where(jnp.logical_and(last_row, lane == N_HEADS + h),
                                      dgls[bi * N_HEADS + h], 0.0)
            dgc_ref[bi] = dgc

    def rev(c):
        return nc - 1 - c

    wide = pl.BlockSpec((b, DN_CHUNK, HALF_W), lambda c: (0, rev(c), 0))
    sq = pl.BlockSpec((b, N_HEADS, DN_CHUNK, DN_CHUNK), lambda c: (0, 0, rev(c), 0))
    gates = pl.BlockSpec((b, DN_CHUNK, LANES), lambda c: (0, rev(c), 0))
    return _call(
        body, name=name, grid=(nc,),
        in_specs=[wide, wide, wide, wide, wide, sq, gates,
                  pl.BlockSpec((1, bh, HEAD_DIM, HEAD_DIM), lambda c: (rev(c), 0, 0, 0))],
        out_specs=[wide, wide, wide, wide, sq, gates],
        out_shape=[_sds((b, t, HALF_W), F32)] * 4
        + [_sds((b, N_HEADS, t, DN_CHUNK), F32), _sds((b, t, LANES), F32)],
        scratch=[pltpu.VMEM((bh, HEAD_DIM, HEAD_DIM), F32)],
    )(do, u, w, qd, kd, qk, gc, s_in)


def dn_chunk_bwd(qkv, proj3, alog_row, dtb_row, tinv, u, w, du, dw, dqd, dkd, dqk, dgc_scan, dproj3, name,
                 comm=None):
    b, t, _ = qkv.shape
    rblk = min(256, t)
    n_in = rblk // DN_CHUNK

    def body(q_ref, k_ref, v_ref, pba_ref, al_ref, dtb_ref, ti_ref, u_ref, w_ref,
             du_ref, dw_ref, dqd_ref, dkd_ref, dqk_ref, dgs_ref, dproj_in,
             dq_ref, dpba_ref, dal_ref, ddtb_ref):
        @pl.when(jnp.logical_and(pl.program_id(0) == 0, pl.program_id(1) == 0))
        def _():
            dal_ref[...] = jnp.zeros_like(dal_ref)
            ddtb_ref[...] = jnp.zeros_like(ddtb_ref)

        ea = jnp.exp(al_ref[...])
        incl0 = _chunk_masks()[0]
        tri = jnp.where(incl0, 1.0, 0.0)
        tri_up = jnp.where(_iota2((DN_CHUNK, DN_CHUNK), 1) >= _iota2((DN_CHUNK, DN_CHUNK), 0), 1.0, 0.0)
        lane = _iota2((DN_CHUNK, LANES), 1)
        last_col = _iota2((DN_CHUNK, 1), 0) == DN_CHUNK - 1

        _, strict, _ = _chunk_masks()
        gate_lane = jnp.logical_and(lane >= N_HEADS, lane < 2 * N_HEADS)

        def chunk_group(cg, carry):
            tiles, items = [], []
            for sub in range(LOCKSTEP_CHUNKS):
                rows = pl.ds(pl.multiple_of((cg * LOCKSTEP_CHUNKS + sub) * DN_CHUNK, DN_CHUNK), DN_CHUNK)
                pba = pba_ref[0, rows, :]
                beta_all, g_all = _gates(pba, ea, dtb_ref[...])
                gc = _dot_exact_lhs(tri, g_all)
                tiles.append((rows, pba, beta_all, g_all))
                for h in range(N_HEADS):
                    items.append((sub, rows, h, slice(h * HEAD_DIM, (h + 1) * HEAD_DIM),
                                  beta_all[:, h:h + 1], gc[:, N_HEADS + h:N_HEADS + h + 1]))
            decays = [_chunk_decay(gcol)[0] for _, _, _, _, _, gcol in items]
            egs = [jnp.exp(gcol) for _, _, _, _, _, gcol in items]
            qbs = [(q_ref[0, rows, cs] * QK_SCALE).astype(BF16) for _, rows, h, cs, _, _ in items]
            kfs = [k_ref[0, rows, cs].astype(BF16) for _, rows, h, cs, _, _ in items]
            kbs = [k_ref[0, rows, cs] * beta for _, rows, h, cs, beta, _ in items]
            kbbs = [kb.astype(BF16) for kb in kbs]
            tsps = [_split(ti_ref[0, h, rows, :]) for _, rows, h, cs, _, _ in items]
            drus = [_dot3(tsp, _split(du_ref[0, rows, cs]), TN)
                    for (_, rows, h, cs, _, _), tsp in zip(items, tsps)]
            drws = [_dot3(tsp, _split(dw_ref[0, rows, cs]), TN)
                    for (_, rows, h, cs, _, _), tsp in zip(items, tsps)]
            m_kks = [_dot(kbb, kf, NT) for kbb, kf in zip(kbbs, kfs)]
            a_qks = [_dot(qb, kf, NT) for qb, kf in zip(qbs, kfs)]
            dls = [-jnp.where(strict, _dot3(_split(dru), _split(u_ref[0, rows, cs]), NT)
                              + _dot3(_split(drw), _split(w_ref[0, rows, cs]), NT), 0.0)
                   for (_, rows, h, cs, _, _), dru, drw in zip(items, drus, drws)]
            dms = [(dl * dc).astype(BF16) for dl, dc in zip(dls, decays)]
            das = [(dqk_ref[0, h, rows, :] * dc).astype(BF16)
                   for (_, rows, h, cs, _, _), dc in zip(items, decays)]
            dkb_mm = [_dot(dm, kf, NN) for dm, kf in zip(dms, kfs)]
            dk_mm = [_dot(dm, kbb, TN) + _dot(da, qb, TN) for dm, kbb, da, qb in zip(dms, kbbs, das, qbs)]
            dqs_mm = [_dot(da, kf, NN) for da, kf in zip(das, kfs)]
            dgc_tiles = [dgs_ref[0, rows, :] for rows, _, _, _ in tiles]
            dbeta_tiles = [jnp.zeros((DN_CHUNK, LANES), F32) for _ in tiles]
            for n_it, (sub, rows, h, cs, beta, gcol) in enumerate(items):
                eg, dc = egs[n_it], decays[n_it]
                k = k_ref[0, rows, cs]
                q = q_ref[0, rows, cs] * QK_SCALE
                kb, dru, drw = kbs[n_it], drus[n_it], drws[n_it]
                ek = jnp.exp(gcol[DN_CHUNK - 1:DN_CHUNK, :] - gcol)
                e_mat = (dls[n_it] * m_kks[n_it] + dqk_ref[0, h, rows, :] * a_qks[n_it]) * dc
                dkb = drw * eg + dkb_mm[n_it]
                dqd = dqd_ref[0, rows, cs]
                dkd = dkd_ref[0, rows, cs]
                kdk = dkd * k * ek
                kdk_total = jnp.sum(jnp.sum(kdk, axis=0, keepdims=True), axis=1, keepdims=True)
                dg = (jnp.sum(drw * kb * eg + dqd * q * eg - kdk, axis=-1, keepdims=True)
                      + jnp.sum(e_mat, axis=1, keepdims=True)
                      - _row_to_col(jnp.sum(e_mat, axis=0, keepdims=True))
                      + jnp.where(last_col, kdk_total, 0.0))
                dbeta = jnp.sum(dkb * k + dru * v_ref[0, rows, cs], axis=-1, keepdims=True)
                dq_ref[0, rows, cs] = (dqs_mm[n_it] + dqd * eg) * QK_SCALE
                dq_ref[0, rows, pl.ds(HALF_W + h * HEAD_DIM, HEAD_DIM)] = dk_mm[n_it] + dkd * ek + dkb * beta
                dq_ref[0, rows, pl.ds(2 * HALF_W + h * HEAD_DIM, HEAD_DIM)] = dru * beta
                dgc_tiles[sub] = dgc_tiles[sub] + jnp.where(lane == N_HEADS + h, dg, 0.0)
                dbeta_tiles[sub] = dbeta_tiles[sub] + jnp.where(lane == h, dbeta, 0.0)
            for (rows, pba, beta_all, g_all), dgc_tile, dbeta_tile in zip(tiles, dgc_tiles, dbeta_tiles):
                dg_tile = _dot_exact_lhs(tri_up, dgc_tile)
                da_pre = dg_tile * (-ea) * _sigmoid(pba + dtb_ref[...])
                dal_ref[...] += jnp.sum(jnp.where(gate_lane, dg_tile * g_all, 0.0), axis=0, keepdims=True)
                ddtb_ref[...] += jnp.sum(jnp.where(gate_lane, da_pre, 0.0), axis=0, keepdims=True)
                dpba_ref[0, rows, :] = jnp.where(lane < N_HEADS, dbeta_tile * beta_all * (1.0 - beta_all),
                                                 jnp.where(gate_lane, da_pre, 0.0)).astype(BF16)
            return carry

        lax.fori_loop(0, n_in // LOCKSTEP_CHUNKS, chunk_group, 0)

    def seg(cblk):
        return pl.BlockSpec((1, rblk, HALF_W), lambda i, r: (i, r, cblk))

    vec = pl.BlockSpec((1, LANES), lambda i, r: (0, 0))
    wide = pl.BlockSpec((1, rblk, HALF_W), lambda i, r: (i, r, 0))
    sq = pl.BlockSpec((1, N_HEADS, rblk, DN_CHUNK), lambda i, r: (i, 0, r, 0))
    gates = pl.BlockSpec((1, rblk, LANES), lambda i, r: (i, r, 0))
    return _call(
        body, name=name, grid=(b, t // rblk),
        in_specs=[seg(0), seg(1), seg(2),
                  pl.BlockSpec((1, rblk, LANES), lambda i, r: (i, r, GATE_COL_BLOCK)), vec, vec,
                  sq, wide, wide, wide, wide, wide, wide, sq, gates, _ANY],
        out_specs=[pl.BlockSpec((1, rblk, 3 * HALF_W), lambda i, r: (i, r, 0)),
                   pl.BlockSpec((1, rblk, LANES), lambda i, r: (i, r, GATE_COL_BLOCK)), vec, vec],
        out_shape=[_sds((b, t, 3 * HALF_W), F32), _sds(dproj3.shape, BF16),
                   _sds((1, LANES), F32), _sds((1, LANES), F32)],
        input_output_aliases={15: 1}, comm=comm,
    )(qkv, qkv, qkv, proj3, alog_row, dtb_row, tinv, u, w, du, dw, dqd, dkd, dqk, dgc_scan, dproj3)


def dn_out_fwd(o, proj, dn_norm, name):
    n = o.shape[0]
    tm = min(ROW_TILE, n)

    def body(o_ref, z_ref, g_ref, y_ref):
        for h in range(N_HEADS):
            cs = slice(h * HEAD_DIM, (h + 1) * HEAD_DIM)
            oh = o_ref[:, cs]
            z = z_ref[:, cs]
            y = oh * _rms_scale(oh) * g_ref[...]
            y_ref[:, cs] = (y * (z * _sigmoid(z))).astype(BF16)

    half = pl.BlockSpec((tm, HALF_W), lambda i: (i, 0))
    return _call(
        body, name=name, grid=(n // tm,),
        in_specs=[half, pl.BlockSpec((tm, HALF_W), lambda i: (i, 5)),
                  pl.BlockSpec((1, HEAD_DIM), lambda i: (0, 0))],
        out_specs=half, out_shape=_sds((n, HALF_W), BF16),
    )(o, proj, dn_norm)


def dn_out_bwd(dy, o, proj, dn_norm, dproj, name):
    n = o.shape[0]
    tm = min(ROW_TILE, n)

    def body(dy_ref, o_ref, z_ref, g_ref, dproj_in, do_ref, dz_ref, dg_ref):
        @pl.when(pl.program_id(0) == 0)
        def _():
            dg_ref[...] = jnp.zeros_like(dg_ref)

        g = g_ref[...]
        dg = jnp.zeros_like(g)
        for h in range(N_HEADS):
            cs = slice(h * HEAD_DIM, (h + 1) * HEAD_DIM)
            oh = o_ref[:, cs]
            z = z_ref[:, cs]
            d = dy_ref[:, cs]
            r = _rms_scale(oh)
            nh = oh * r
            sz = _sigmoid(z)
            dyn = d * (z * sz)
            dz_ref[:, cs] = (d * (nh * g) * (sz * (1.0 + z * (1.0 - sz)))).astype(BF16)
            dg = dg + jnp.sum(dyn * nh, axis=0, keepdims=True)
            dn = dyn * g
            do_ref[:, cs] = r * (dn - nh * jnp.mean(dn * nh, axis=-1, keepdims=True))
        dg_ref[...] += dg

    half = pl.BlockSpec((tm, HALF_W), lambda i: (i, 0))
    vec = pl.BlockSpec((1, HEAD_DIM), lambda i: (0, 0))
    return _call(
        body, name=name, grid=(n // tm,),
        in_specs=[half, half, pl.BlockSpec((tm, HALF_W), lambda i: (i, 5)), vec, _ANY],
        out_specs=[half, pl.BlockSpec((tm, HALF_W), lambda i: (i, 5)), vec],
        out_shape=[_sds((n, HALF_W), F32), _sds(dproj.shape, BF16), _sds((1, HEAD_DIM), F32)],
        input_output_aliases={4: 1},
    )(dy, o, proj, dn_norm, dproj)


def _adamw_math(w, g, m, v):
    m_new = ADAM_B1 * m + (1.0 - ADAM_B1) * g
    v_new = ADAM_B2 * v + (1.0 - ADAM_B2) * (g * g)
    m_hat = m_new / (1.0 - ADAM_B1 ** ADAM_STEP)
    v_hat = v_new / (1.0 - ADAM_B2 ** ADAM_STEP)
    delta = -ADAM_LR * (m_hat / (jnp.sqrt(v_hat) + ADAM_EPS) + ADAM_WD * w)
    return delta, m_new, v_new


def adamw(w, g, m, v, name):
    r, c = w.shape
    tr = r
    for cand in (256, 352):
        if r % cand == 0 and r > cand:
            tr = cand
            break

    def body(w_ref, g_ref, m_ref, v_ref, d_ref, mo_ref, vo_ref):
        d, mn, vn = _adamw_math(w_ref[...], g_ref[...], m_ref[...], v_ref[...])
        d_ref[...] = d
        mo_ref[...] = mn
        vo_ref[...] = vn

    spec = pl.BlockSpec((tr, c), lambda i: (i, 0))
    return _call(
        body, name=name, grid=(r // tr,),
        in_specs=[spec] * 4, out_specs=[spec] * 3, out_shape=[_sds((r, c), F32)] * 3,
    )(w, g, m, v)


def _place():
    return lax.axis_index("x"), lax.axis_index("y"), lax.axis_index("c")


def _other_chips(x, y):
    return [(1 - x, y), (x, 1 - y), (1 - x, 1 - y)]


_ANY = pl.BlockSpec(memory_space=pl.ANY)


def cast_place(w, shard_idx, name):
    r, cols = w.shape
    tr = r // 2

    def body(j_ref, w_ref, o_ref):
        o_ref[0] = w_ref[...].astype(BF16)

    return pl.pallas_call(
        body, name=name,
        grid_spec=pltpu.PrefetchScalarGridSpec(
            num_scalar_prefetch=1, grid=(r // tr,),
            in_specs=[pl.BlockSpec((tr, cols), lambda i, j: (i, 0))],
            out_specs=pl.BlockSpec((1, tr, cols), lambda i, j: (j[0], i, 0))),
        out_shape=_sds((N_SHARD, r, cols), BF16),
        compiler_params=pltpu.CompilerParams(dimension_semantics=("arbitrary",),
                                             vmem_limit_bytes=VMEM_LIMIT),
    )(shard_idx, w)


class Exchange:
    def __init__(self, inputs, out_shape, aliases, sems, phases):
        self.inputs, self.out_shape, self.aliases = list(inputs), list(out_shape), dict(aliases)
        self.sems, self.phases = list(sems), list(phases)


def run_exchange(ex, name):
    def body(*refs):
        n_in, n_out = len(ex.inputs), len(ex.out_shape)
        for _, fn in ex.phases:
            fn(refs[:n_in], refs[n_in:n_in + n_out], refs[n_in + n_out:])

    return _call(body, name=name, in_specs=[_ANY] * len(ex.inputs), out_specs=[_ANY] * len(ex.out_shape),
                 out_shape=ex.out_shape, scratch=ex.sems, input_output_aliases=ex.aliases)(*ex.inputs)


def merge_exchanges(exs):
    inputs, out_shape, sems, aliases, phases, out_slices = [], [], [], {}, [], []
    for ex in exs:
        i0, o0, s0 = len(inputs), len(out_shape), len(sems)
        inputs += ex.inputs
        out_shape += ex.out_shape
        sems += ex.sems
        for k, m in ex.aliases.items():
            aliases[i0 + k] = o0 + m
        si, so, ss = slice(i0, len(inputs)), slice(o0, len(out_shape)), slice(s0, len(sems))
        out_slices.append(so)
        for step, fn in ex.phases:
            phases.append((step, lambda ins, outs, sm, fn=fn, si=si, so=so, ss=ss: fn(ins[si], outs[so], sm[ss])))
    return Exchange(inputs, out_shape, aliases, sems, phases), out_slices


def _dma_sems(*sizes):
    return [pltpu.SemaphoreType.DMA((s,)) for s in sizes]


def gather_exchange(bufs, small=None, relay_step=-2):
    n = len(bufs)
    n_small = 0 if small is None else 1

    def half(outs, a, blk, hc):
        rh = bufs[a].shape[1] // 2
        return outs[a].at[blk, pl.ds(hc * rh, rh), :]

    def ici(outs, sems, a, k, blk, to):
        return pltpu.make_async_remote_copy(
            src_ref=half(outs, a, blk, to[2]), dst_ref=half(outs, a, blk, to[2]), send_sem=sems[0].at[3 * a + k],
            recv_sem=sems[1].at[3 * a + k], device_id=to, device_id_type=MESH)

    def d2d(outs, sems, a, k, blk, hc, to):
        return pltpu.make_async_remote_copy(
            src_ref=half(outs, a, blk, hc), dst_ref=half(outs, a, blk, hc), send_sem=sems[2].at[3 * a + k],
            recv_sem=sems[3].at[3 * a + k], device_id=to, device_id_type=MESH)

    def small_copy(ins, outs, sems, k, blk, to):
        return pltpu.make_async_remote_copy(
            src_ref=ins[n], dst_ref=outs[n].at[blk], send_sem=sems[0].at[3 * n + k],
            recv_sem=sems[1].at[3 * n + k], device_id=to, device_id_type=MESH)

    def start(ins, outs, sems):
        x, y, c = _place()
        j = 2 * x + y
        if n_small:
            pltpu.make_async_copy(ins[n], outs[n].at[j], sems[4].at[0]).start()
        for k, (px, py) in enumerate(_other_chips(x, y)):
            if n_small:
                small_copy(ins, outs, sems, k, j, (px, py, c)).start()
            for a in range(n):
                ici(outs, sems, a, k, j, (px, py, c)).start()

    def relay(ins, outs, sems):
        x, y, c = _place()
        for k, (px, py) in enumerate(_other_chips(x, y)):
            for a in range(n):
                ici(outs, sems, a, k, 2 * px + py, (px, py, c)).wait_recv()
                d2d(outs, sems, a, k, 2 * px + py, c, (x, y, 1 - c)).start()

    def finish(ins, outs, sems):
        x, y, c = _place()
        j = 2 * x + y
        for k, (px, py) in enumerate(_other_chips(x, y)):
            blk = 2 * px + py
            if n_small:
                small_copy(ins, outs, sems, k, blk, (px, py, c)).wait_recv()
                small_copy(ins, outs, sems, k, j, (px, py, c)).wait_send()
            for a in range(n):
                d2d(outs, sems, a, k, blk, 1 - c, (x, y, 1 - c)).wait_recv()
                ici(outs, sems, a, k, j, (px, py, c)).wait_send()
                d2d(outs, sems, a, k, blk, c, (x, y, 1 - c)).wait_send()
        if n_small:
            pltpu.make_async_copy(ins[n], outs[n].at[j], sems[4].at[0]).wait()

    out_shape = [_sds(b.shape, b.dtype) for b in bufs]
    if n_small:
        out_shape.append(_sds((N_SHARD,) + small.shape, small.dtype))
    return Exchange(list(bufs) + ([small] if n_small else []), out_shape, {a: a for a in range(n)},
                    _dma_sems(3 * n + 3, 3 * n + 3, 3 * n, 3 * n, 1),
                    [(0, start), (relay_step, relay), (-1, finish)])


def _start_then_wait(copies):
    def start(ins, outs, sems):
        for sent, _ in copies(ins, outs, sems):
            sent().start()

    def finish(ins, outs, sems):
        pairs = copies(ins, outs, sems)
        for _, arrival in pairs:
            arrival().wait_recv()
        for sent, _ in pairs:
            sent().wait_send()

    return [(0, start), (-1, finish)]


def pair_exchange(arrs):
    n = len(arrs)

    def copies(ins, outs, sems):
        x, y, c = _place()
        res = []
        for a in range(n):
            def mk(a=a):
                rh = arrs[a].shape[1] // 2
                return pltpu.make_async_remote_copy(
                    src_ref=ins[a].at[:, pl.ds((1 - c) * rh, rh), :], dst_ref=outs[a], send_sem=sems[0].at[a],
                    recv_sem=sems[1].at[a], device_id=(x, y, 1 - c), device_id_type=MESH)
            res.append((mk, mk))
        return res

    return Exchange(arrs, [_sds((a.shape[0], a.shape[1] // 2, a.shape[2]), a.dtype) for a in arrs], {},
                    _dma_sems(n, n), _start_then_wait(copies))


def pair_add(g, s, c_idx, name):
    nb, r, cols = g.shape
    rh = r // 2

    def body(c_ref, g_ref, s_ref, o_ref):
        o_ref[...] = (g_ref[...] + s_ref[...]).astype(BF16)

    return pl.pallas_call(
        body, name=name,
        grid_spec=pltpu.PrefetchScalarGridSpec(
            num_scalar_prefetch=1, grid=(nb,),
            in_specs=[pl.BlockSpec((1, rh, cols), lambda j, c: (j, c[0], 0)),
                      pl.BlockSpec((1, rh, cols), lambda j, c: (j, 0, 0))],
            out_specs=pl.BlockSpec((1, rh, cols), lambda j, c: (j, 0, 0))),
        out_shape=_sds((nb, rh, cols), BF16),
        compiler_params=pltpu.CompilerParams(dimension_semantics=("arbitrary",),
                                             vmem_limit_bytes=VMEM_LIMIT),
    )(c_idx, g, s)


def chip_exchange(arrs):
    n = len(arrs)

    def copies(ins, outs, sems):
        x, y, c = _place()
        j = 2 * x + y
        res = []
        for a in range(n):
            for k, (px, py) in enumerate(_other_chips(x, y)):
                def mk(src_blk, dst_blk, a=a, k=k, to=(px, py, c)):
                    return pltpu.make_async_remote_copy(
                        src_ref=ins[a].at[src_blk], dst_ref=outs[a].at[dst_blk], send_sem=sems[0].at[3 * a + k],
                        recv_sem=sems[1].at[3 * a + k], device_id=to, device_id_type=MESH)
                res.append((functools.partial(mk, 2 * px + py, j), functools.partial(mk, j, 2 * px + py)))
        return res

    return Exchange(arrs, [_sds(a.shape, a.dtype) for a in arrs], {}, _dma_sems(3 * n, 3 * n),
                    _start_then_wait(copies))


def sum_chips(r, p, shard_idx, name):
    nb, rh, cols = r.shape
    tr = rh

    def body(j_ref, p_ref, *refs):
        o_ref = refs[nb]
        j = j_ref[0]
        acc = None
        for i in range(nb):
            term = jnp.where(j == i, p_ref[0], refs[i][0]).astype(F32)
            acc = term if acc is None else acc + term
        o_ref[...] = acc

    def slot(i):
        return pl.BlockSpec((1, tr, cols), lambda t, j: (jnp.where(j[0] == i, (i + 1) % nb, i), t, 0))

    return pl.pallas_call(
        body, name=name,
        grid_spec=pltpu.PrefetchScalarGridSpec(
            num_scalar_prefetch=1, grid=(rh // tr,),
            in_specs=[pl.BlockSpec((1, tr, cols), lambda t, j: (j[0], t, 0))] + [slot(i) for i in range(nb)],
            out_specs=pl.BlockSpec((tr, cols), lambda t, j: (t, 0))),
        out_shape=_sds((rh, cols), F32),
        compiler_params=pltpu.CompilerParams(dimension_semantics=("arbitrary",),
                                             vmem_limit_bytes=VMEM_LIMIT),
    )(shard_idx, p, *([r] * nb))


def pair_swap(arrs):
    n = len(arrs)

    def copies(ins, outs, sems):
        x, y, c = _place()
        res = []
        for a in range(n):
            def mk(a=a):
                return pltpu.make_async_remote_copy(
                    src_ref=ins[a], dst_ref=outs[a], send_sem=sems[0].at[a], recv_sem=sems[1].at[a],
                    device_id=(x, y, 1 - c), device_id_type=MESH)
            res.append((mk, mk))
        return res

    return Exchange(arrs, [_sds(a.shape, a.dtype) for a in arrs], {}, _dma_sems(n, n),
                    _start_then_wait(copies))


ADAMW_STEPS_PER_HALF = 4


def adamw_pairs(items, name, comm=None):
    n_items = len(items)
    nh = ADAMW_STEPS_PER_HALF

    def body(*refs):
        ins, outs = refs[:5 * n_items], refs[5 * n_items:]
        mine = (pl.program_id(0) // nh) == lax.axis_index("c")
        for a in range(n_items):
            w_ref, gm_ref, gs_ref, m_ref, v_ref = ins[5 * a:5 * a + 5]
            g_ref, d_ref, mo_ref, vo_ref = outs[4 * a:4 * a + 4]
            g = jnp.where(mine, gm_ref[...], gs_ref[...])
            d, mn, vn = _adamw_math(w_ref[...], g, m_ref[...], v_ref[...])
            g_ref[...] = g
            d_ref[...] = d
            mo_ref[...] = mn
            vo_ref[...] = vn

    in_specs, out_specs, out_shape, args = [], [], [], []
    for w, g_mine, g_sib, m, v in items:
        r, cols = w.shape
        tr = r // (2 * nh)
        full = pl.BlockSpec((tr, cols), lambda i: (i, 0))
        part = pl.BlockSpec((tr, cols), lambda i: (i % nh, 0))
        in_specs += [full, part, part, full, full]
        out_specs += [full] * 4
        out_shape += [_sds((r, cols), F32)] * 4
        args += [w, g_mine, g_sib, m, v]
    res = _call(body, name=name, grid=(2 * nh,), in_specs=in_specs, out_specs=out_specs,
                out_shape=out_shape, comm=comm)(*args)
    own, hosted = (res, None) if comm is None else res
    grouped = [tuple(own[4 * a:4 * a + 4]) for a in range(n_items)]
    return grouped if comm is None else (grouped, hosted)


N_DEV = 8


def device_gather(pack):
    def copies(ins, outs, sems):
        x, y, c = _place()
        me = 4 * x + 2 * y + c
        res = []
        for k in range(1, N_DEV):
            fx, fy, fc = (k >> 2) & 1, (k >> 1) & 1, k & 1
            px, py, pc = (1 - x if fx else x, 1 - y if fy else y, 1 - c if fc else c)

            def mk(slot, k=k, to=(px, py, pc)):
                return pltpu.make_async_remote_copy(
                    src_ref=ins[0], dst_ref=outs[0].at[slot], send_sem=sems[0].at[k - 1],
                    recv_sem=sems[1].at[k - 1], device_id=to, device_id_type=MESH)
            res.append((functools.partial(mk, me), functools.partial(mk, 4 * px + 2 * py + pc)))
        return res

    return Exchange([pack], [_sds((N_DEV,) + pack.shape, pack.dtype)], {}, _dma_sems(N_DEV - 1, N_DEV - 1),
                    _start_then_wait(copies))


def sum_devices(buf, pack, me_idx, name):
    r, cols = pack.shape

    def body(me_ref, p_ref, *refs):
        o_ref = refs[N_DEV]
        acc = None
        for i in range(N_DEV):
            term = jnp.where(me_ref[0] == i, p_ref[...], refs[i][0])
            acc = term if acc is None else acc + term
        o_ref[...] = acc

    def slot(i):
        return pl.BlockSpec((1, r, cols), lambda t, me: (jnp.where(me[0] == i, (i + 1) % N_DEV, i), 0, 0))

    whole = pl.BlockSpec((r, cols), lambda t, me: (0, 0))
    return pl.pallas_call(
        body, name=name,
        grid_spec=pltpu.PrefetchScalarGridSpec(
            num_scalar_prefetch=1, grid=(1,),
            in_specs=[whole] + [slot(i) for i in range(N_DEV)], out_specs=whole),
        out_shape=_sds((r, cols), F32),
        compiler_params=pltpu.CompilerParams(dimension_semantics=("arbitrary",),
                                             vmem_limit_bytes=VMEM_LIMIT),
    )(me_idx, pack, *([buf] * N_DEV))


SMALL_NAMES = ("ffn1_norm", "mix_norm", "ffn2_norm", "final_norm", "sg_ln_g", "sg_ln_b",
               "dn_norm", "a_log", "dt_bias", "sg_b", "sg_w", "conv_w", "loss")


def _to_rows(a):
    flat = a.reshape(-1)
    pad = (-flat.shape[0]) % LANES
    if pad:
        flat = jnp.pad(flat, (0, pad))
    return flat.reshape(-1, LANES)


def _pack_small(parts):
    rows = [_to_rows(parts[k]) for k in SMALL_NAMES]
    pack = jnp.concatenate(rows, axis=0)
    pad = (-pack.shape[0]) % 8
    if pad:
        pack = jnp.pad(pack, ((0, pad), (0, 0)))
    return pack


def _unpack_small(pack, shapes):
    out, r0 = {}, 0
    for k in SMALL_NAMES:
        size = 1
        for s in shapes[k]:
            size *= s
        nrows = -(-size // LANES)
        out[k] = pack[r0:r0 + nrows].reshape(-1)[:size].reshape(shapes[k])
        r0 += nrows
    return out


def kernel(x, ffn1_norm, ffn1_w_gate, ffn1_w_up, ffn1_w_down, mix_norm, w_in, conv_w, a_log, dt_bias, dn_norm, sg_ln_g, sg_ln_b, sg_w, sg_b, w_out, ffn2_norm, ffn2_w_gate, ffn2_w_up, ffn2_w_down, final_norm, loss_target, m_ffn1_norm, m_ffn1_w_gate, m_ffn1_w_up, m_ffn1_w_down, m_mix_norm, m_w_in, m_conv_w, m_a_log, m_dt_bias, m_dn_norm, m_sg_ln_g, m_sg_ln_b, m_sg_w, m_sg_b, m_w_out, m_ffn2_norm, m_ffn2_w_gate, m_ffn2_w_up, m_ffn2_w_down, m_final_norm, v_ffn1_norm, v_ffn1_w_gate, v_ffn1_w_up, v_ffn1_w_down, v_mix_norm, v_w_in, v_conv_w, v_a_log, v_dt_bias, v_dn_norm, v_sg_ln_g, v_sg_ln_b, v_sg_w, v_sg_b, v_w_out, v_ffn2_norm, v_ffn2_w_gate, v_ffn2_w_up, v_ffn2_w_down, v_final_norm):
    bsz, t_len, d = x.shape
    n = bsz * t_len
    xy, yy, cc = _place()
    shard = 2 * xy + yy

    big_names = ["ffn1_w_gate", "ffn1_w_up", "ffn1_w_down", "w_in", "w_out",
                 "ffn2_w_gate", "ffn2_w_up", "ffn2_w_down"]
    big_w = dict(ffn1_w_gate=ffn1_w_gate, ffn1_w_up=ffn1_w_up, ffn1_w_down=ffn1_w_down, w_in=w_in,
                 w_out=w_out, ffn2_w_gate=ffn2_w_gate, ffn2_w_up=ffn2_w_up, ffn2_w_down=ffn2_w_down)
    big_m = dict(ffn1_w_gate=m_ffn1_w_gate, ffn1_w_up=m_ffn1_w_up, ffn1_w_down=m_ffn1_w_down, w_in=m_w_in,
                 w_out=m_w_out, ffn2_w_gate=m_ffn2_w_gate, ffn2_w_up=m_ffn2_w_up, ffn2_w_down=m_ffn2_w_down)
    big_v = dict(ffn1_w_gate=v_ffn1_w_gate, ffn1_w_up=v_ffn1_w_up, ffn1_w_down=v_ffn1_w_down, w_in=v_w_in,
                 w_out=v_w_out, ffn2_w_gate=v_ffn2_w_gate, ffn2_w_up=v_ffn2_w_up, ffn2_w_down=v_ffn2_w_down)
    shard_idx = jnp.reshape(shard, (1,)).astype(jnp.int32)
    c_idx = jnp.reshape(cc, (1,)).astype(jnp.int32)
    transposed = ("ffn1_w_gate", "ffn1_w_up", "ffn2_w_gate", "ffn2_w_up")

    def as2d(a, k):
        return a[0].T if k in transposed else a[0]

    def from2d(a, k):
        return a.T[None] if k in transposed else a[None]

    placed = {k: cast_place(as2d(big_w[k], k), shard_idx, name="cast_" + k) for k in big_names}
    first_names = ["ffn1_w_gate", "ffn1_w_up"]
    second_names = ["ffn1_w_down", "w_in"]
    third_names = ["w_out", "ffn2_w_gate"]
    fourth_names = ["ffn2_w_up", "ffn2_w_down"]
    res = run_exchange(gather_exchange([placed[k] for k in first_names], conv_w[0]), name="gather_first")
    gw = dict(zip(first_names, res[:2]))
    conv_full = res[2].transpose(1, 0, 2).reshape(CONV_K, 3 * HALF_W)

    x0 = x.reshape(n, d)
    def ffn_weights(prefix):
        return [gw[prefix + k].reshape(-1, d) for k in ("_w_gate", "_w_up", "_w_down")]

    def ffn_grad_blocks(grads):
        return [g.reshape(N_SHARD, -1, d) for g in grads]

    (h1, gate1, up1, act1), second = ffn_fwd(
        x0, ffn1_norm, gw["ffn1_w_gate"].reshape(-1, d), gw["ffn1_w_up"].reshape(-1, d), None,
        name="ffn1_fwd", comm=gather_exchange([placed[k] for k in second_names]))
    gw.update(zip(second_names, second))
    (x1,) = ffn_down(x0, act1, gw["ffn1_w_down"].reshape(-1, d), name="ffn1_down")
    w_in_full = gw["w_in"].transpose(1, 0, 2).reshape(d, IN_COLS)
    w_in_full = jnp.pad(w_in_full, ((0, 0), (0, PROJ_W - IN_COLS)))
    (proj, h2), third = in_proj_fwd(x1, mix_norm, w_in_full, name="in_proj_fwd",
                                    comm=gather_exchange([placed[k] for k in third_names]))
    gw.update(zip(third_names, third))
    proj3 = proj.reshape(bsz, t_len, PROJ_W)
    bias_tile = jnp.repeat(sg_b[0].T, SG_GROUP_DIM, axis=1)
    sg_out = sg_fwd(proj, sg_ln_g, sg_ln_b, sg_w[0], bias_tile, name="sg_fwd")
    qkv = dn_conv_fwd(proj3, conv_full, name="dn_conv_fwd")
    alog_row = jnp.zeros((1, LANES), F32).at[0, N_HEADS:2 * N_HEADS].set(a_log[0])
    dtb_row = jnp.zeros((1, LANES), F32).at[0, N_HEADS:2 * N_HEADS].set(dt_bias[0])
    (u_wy, w_wy, q_dec, k_dec, qk, tinv, gc), fourth = dn_chunk_fwd(
        qkv, proj3, alog_row, dtb_row, name="dn_chunk_fwd",
        comm=gather_exchange([placed[k] for k in fourth_names]))
    gw.update(zip(fourth_names, fourth))
    w_out_full = gw["w_out"].reshape(2 * HALF_W, d)
    o, s_in = dn_scan_fwd(u_wy, w_wy, q_dec, k_dec, qk, gc, name="dn_scan_fwd")
    dn_out = dn_out_fwd(o.reshape(n, HALF_W), proj, dn_norm, name="dn_out_fwd")
    x2 = out_proj_fwd(x1, sg_out, dn_out, w_out_full, name="out_proj_fwd")
    x3, h3, gate2, up2, act2 = ffn_fwd(x2, ffn2_norm, *ffn_weights("ffn2"), name="ffn2_fwd")
    dx3, d_final_norm, loss_tile = final_loss(x3, final_norm.reshape(1, d),
                                              loss_target.reshape(n, d), name="final_loss")

    dx2, dgate2, dup2, dyh2, d_ffn2_norm = ffn_bwd_act(
        dx3, x2, ffn2_norm, gate2, up2, *ffn_weights("ffn2"), name="ffn2_bwd_act")
    g_big = {}
    g_big["ffn2_w_gate"], g_big["ffn2_w_up"], g_big["ffn2_w_down"] = ffn_grad_blocks(ffn_bwd_w(
        [dgate2, dup2, act2], [h3, dyh2], [(0, 0), (1, 0), (2, 1)], name="ffn2_bwd_w"))

    early = ["ffn2_w_gate", "ffn2_w_up", "ffn2_w_down"]
    (d_sg, d_dn, dx2b), early_sib = out_proj_bwd_x(dx2, w_out_full, name="out_proj_bwd_x",
                                                   comm=pair_exchange([g_big[k] for k in early]))
    early_sums = [pair_add(g_big[k], s, c_idx, name="grad_pair_add_" + k) for k, s in zip(early, early_sib)]
    g_w_out = matmul_tn([sg_out, dn_out], dx2b, d, name="w_out_grad")
    g_big["w_out"] = g_w_out.reshape(N_SHARD, (2 * HALF_W) // N_SHARD, d)

    d_proj, d_sg_w, d_bias_tile, d_ln_g, d_ln_b = sg_bwd(d_sg, proj, sg_ln_g, sg_ln_b, sg_w[0],
                                                         bias_tile, name="sg_bwd")
    d_o, d_proj, d_dn_norm = dn_out_bwd(d_dn, o.reshape(n, HALF_W), proj, dn_norm, d_proj,
                                        name="dn_out_bwd")
    du, dw, dqd, dkd, dqk, dgc_scan = dn_scan_bwd(d_o.reshape(bsz, t_len, HALF_W), u_wy, w_wy, q_dec,
                                                  k_dec, qk, gc, s_in, name="dn_scan_bwd")
    (d_qkv, d_proj3, d_alog_row, d_dtb_row), early_chips = dn_chunk_bwd(
        qkv, proj3, alog_row, dtb_row, tinv, u_wy, w_wy, du, dw, dqd, dkd, dqk, dgc_scan,
        d_proj.reshape(bsz, t_len, PROJ_W), name="dn_chunk_bwd", comm=chip_exchange(early_sums))
    early_halves = [sum_chips(r, p, shard_idx, name="grad_chip_sum_" + k)
                    for k, r, p in zip(early, early_chips, early_sums)]
    d_proj3, d_conv = dn_conv_bwd(d_qkv, proj3, conv_full, d_proj3, name="dn_conv_bwd")
    d_proj = d_proj3.reshape(n, PROJ_W)
    g_w_in = matmul_tn([h2], d_proj, 3 * MXU_DIM, name="w_in_grad")[:, :IN_COLS]
    g_big["w_in"] = g_w_in.reshape(d, N_SHARD, IN_COLS // N_SHARD).transpose(1, 0, 2)

    def reduce_start(names):
        return pair_exchange([g_big[k] for k in names])

    def reduce_pair_sums(names, from_sib):
        return [pair_add(g_big[k], s, c_idx, name="grad_pair_add_" + k) for k, s in zip(names, from_sib)]

    def reduce_chip_sums(names, from_chips, sums):
        return [sum_chips(r, p, shard_idx, name="grad_chip_sum_" + k)
                for k, r, p in zip(names, from_chips, sums)]

    mid = ["w_in", "w_out"]
    (dx1, d_mix_norm, dyh1), mid_sib = in_proj_bwd_x(d_proj, w_in_full, x1, mix_norm, dx2,
                                                     name="in_proj_bwd_x", comm=reduce_start(mid))
    mid_sums = reduce_pair_sums(mid, mid_sib)
    down = ["ffn1_w_down"]
    (g_down,), mid_chips = ffn_bwd_w([act1], [dyh1], [(0, 0)], name="ffn1_bwd_w_down",
                                     comm=chip_exchange(mid_sums))
    g_big["ffn1_w_down"] = g_down.reshape(N_SHARD, -1, d)
    mid_halves = reduce_chip_sums(mid, mid_chips, mid_sums)
    leg, legs = merge_exchanges([reduce_start(down), pair_swap(mid_halves), pair_swap(early_halves)])
    leg_res = run_exchange(leg, name="grad_pair_exchange_down")
    down_sums = reduce_pair_sums(down, leg_res[legs[0]])
    mid_sib_halves, early_sib_halves = leg_res[legs[1]], leg_res[legs[2]]

    dx0, dgate1, dup1, _, d_ffn1_norm = ffn_bwd_act(
        dx1, x0, ffn1_norm, gate1, up1, *ffn_weights("ffn1"), name="ffn1_bwd_act")
    grad_x = dx0.reshape(bsz, t_len, d)
    d_sg_b = d_bias_tile.reshape(SG_CHUNK, SG_GROUPS, SG_GROUP_DIM).sum(axis=-1).T
    small_g = dict(ffn1_norm=d_ffn1_norm, mix_norm=d_mix_norm, ffn2_norm=d_ffn2_norm,
                   final_norm=d_final_norm, sg_ln_g=d_ln_g, sg_ln_b=d_ln_b, dn_norm=d_dn_norm,
                   a_log=d_alog_row[:, N_HEADS:2 * N_HEADS], dt_bias=d_dtb_row[:, N_HEADS:2 * N_HEADS],
                   sg_b=d_sg_b, sg_w=d_sg_w, conv_w=d_conv, loss=loss_tile[:, :1])
    my_pack = _pack_small(small_g)
    hosted, parts = merge_exchanges([chip_exchange(down_sums), device_gather(my_pack)])
    late = ["ffn1_w_gate", "ffn1_w_up"]
    late_grads, hosted_res = ffn_bwd_w([dgate1, dup1], [h1], [(0, 0), (1, 0)], name="ffn1_bwd_w_gate_up",
                                       comm=hosted)
    g_big["ffn1_w_gate"], g_big["ffn1_w_up"] = ffn_grad_blocks(late_grads)
    down_halves = reduce_chip_sums(down, hosted_res[parts[0]], down_sums)
    (all_packs,) = hosted_res[parts[1]]

    leg, legs = merge_exchanges([reduce_start(late), pair_swap(down_halves)])
    leg_res = run_exchange(leg, name="grad_pair_exchange")
    pair_sums = reduce_pair_sums(late, leg_res[legs[0]])
    down_sib_halves = leg_res[legs[1]]

    def adam_items(names, mine, sib):
        return [(as2d(big_w[k], k), gm, gs, as2d(big_m[k], k), as2d(big_v[k], k))
                for k, gm, gs in zip(names, mine, sib)]

    outs = {}
    done = adamw_pairs(
        adam_items(early + mid + down, early_halves + mid_halves + down_halves,
                   list(early_sib_halves) + list(mid_sib_halves) + list(down_sib_halves)),
        name="adamw_early")
    from_chips = run_exchange(chip_exchange(pair_sums), name="grad_chip_exchange")
    halves = reduce_chip_sums(late, from_chips, pair_sums)
    sib_halves = run_exchange(pair_swap(halves), name="grad_pair_swap")
    done += adamw_pairs(adam_items(late, halves, sib_halves), name="adamw_late")
    for k, res in zip(early + mid + down + late, done):
        outs[k] = tuple(from2d(a, k) for a in res)

    small_w = dict(ffn1_norm=ffn1_norm, mix_norm=mix_norm, ffn2_norm=ffn2_norm, final_norm=final_norm,
                   sg_ln_g=sg_ln_g, sg_ln_b=sg_ln_b, dn_norm=dn_norm, a_log=a_log, dt_bias=dt_bias,
                   sg_b=sg_b, sg_w=sg_w)
    small_m = dict(ffn1_norm=m_ffn1_norm, mix_norm=m_mix_norm, ffn2_norm=m_ffn2_norm,
                   final_norm=m_final_norm, sg_ln_g=m_sg_ln_g, sg_ln_b=m_sg_ln_b, dn_norm=m_dn_norm,
                   a_log=m_a_log, dt_bias=m_dt_bias, sg_b=m_sg_b, sg_w=m_sg_w)
    small_v = dict(ffn1_norm=v_ffn1_norm, mix_norm=v_mix_norm, ffn2_norm=v_ffn2_norm,
                   final_norm=v_final_norm, sg_ln_g=v_sg_ln_g, sg_ln_b=v_sg_ln_b, dn_norm=v_dn_norm,
                   a_log=v_a_log, dt_bias=v_dt_bias, sg_b=v_sg_b, sg_w=v_sg_w)
    shapes = {k: small_w[k].shape for k in small_w}
    shapes["conv_w"] = (CONV_K, 3 * HALF_W)
    shapes["loss"] = (1, 1)
    me_idx = jnp.reshape(4 * xy + 2 * yy + cc, (1,)).astype(jnp.int32)
    g_pack = sum_devices(all_packs, my_pack, me_idx, name="small_sum")
    g_small = _unpack_small(g_pack, shapes)
    loss = g_small["loss"].reshape(())
    cw = 3 * HALF_W // N_SHARD
    g_conv = lax.dynamic_slice_in_dim(g_small["conv_w"], shard * cw, cw, axis=1)
    zero_conv = jnp.zeros((CONV_K, 3 * HALF_W), F32)

    def packed(src, conv):
        parts = dict(src)
        parts["conv_w"] = lax.dynamic_update_slice_in_dim(zero_conv, conv[0], shard * cw, axis=1)
        parts["loss"] = jnp.zeros((1, 1), F32)
        return _pack_small(parts)

    d_pack, m_pack, v_pack = adamw(packed(small_w, conv_w), g_pack, packed(small_m, m_conv_w),
                                   packed(small_v, v_conv_w), name="adamw_small")
    d_small = _unpack_small(d_pack, shapes)
    m_small = _unpack_small(m_pack, shapes)
    v_small = _unpack_small(v_pack, shapes)

    def conv_block(full_arr):
        return lax.dynamic_slice_in_dim(full_arr, shard * cw, cw, axis=1)[None]

    for k in small_w:
        outs[k] = (g_small[k].reshape(small_w[k].shape), d_small[k], m_small[k], v_small[k])
    outs["conv_w"] = (g_conv[None], conv_block(d_small["conv_w"]), conv_block(m_small["conv_w"]),
                      conv_block(v_small["conv_w"]))

    order = ["ffn1_norm", "ffn1_w_gate", "ffn1_w_up", "ffn1_w_down", "mix_norm", "w_in", "conv_w",
             "a_log", "dt_bias", "dn_norm", "sg_ln_g", "sg_ln_b", "sg_w", "sg_b", "w_out", "ffn2_norm",
             "ffn2_w_gate", "ffn2_w_up", "ffn2_w_down", "final_norm"]
    return (loss, grad_x, *[outs[k][0] for k in order], *[outs[k][1] for k in order],
            *[outs[k][2] for k in order], *[outs[k][3] for k in order])
```

```python
import functools

import jax
import jax.numpy as jnp
from jax import lax
from jax.experimental import pallas as pl
from jax.experimental.pallas import tpu as pltpu

F32 = jnp.float32
BF16 = jnp.bfloat16
EPS = 1e-6

D_MODEL = 1024
N_SHARD = 4
HEAD_DIM = 128
N_HEADS = 4
DN_CHUNK = 64
SG_CHUNK = 128
SG_GROUPS = 8
SG_GROUP_DIM = 64
HALF_W = 512
PROJ_W = 3200
IN_COLS = 3080
GATE_COL_BLOCK = 24
QK_SCALE = HEAD_DIM ** -0.5
LANES = 128

ADAM_LR = 0.001
ADAM_B1 = 0.9
ADAM_B2 = 0.999
ADAM_EPS = 1e-08
ADAM_WD = 0.01
ADAM_STEP = 10

VMEM_LIMIT = 56 * 1024 * 1024
ROW_TILE = 512

NN = ((1,), (0,))
NT = ((1,), (1,))
TN = ((0,), (0,))
MESH = pl.DeviceIdType.MESH


def _dot(a, b, dims):
    return lax.dot_general(a, b, (dims, ((), ())), preferred_element_type=F32)


def _bdot(a, b, dims):
    return _dot(a.astype(BF16), b.astype(BF16), dims)


def _split(a):
    hi = a.astype(BF16)
    lo = (a - hi.astype(F32)).astype(BF16)
    return hi, lo


def _dot3(a, b, dims=NN):
    return _dot(a[0], b[0], dims) + (_dot(a[0], b[1], dims) + _dot(a[1], b[0], dims))


def _dot_exact_lhs(a, b):
    ab = a.astype(BF16)
    b1 = b.astype(BF16)
    r1 = b - b1.astype(F32)
    b2 = r1.astype(BF16)
    b3 = (r1 - b2.astype(F32)).astype(BF16)
    return _dot(ab, b1, NN) + (_dot(ab, b2, NN) + _dot(ab, b3, NN))


def _call(body, *, name, out_shape, in_specs, out_specs, grid=(), scratch=(), comm=None, **kw):
    params = dict(vmem_limit_bytes=VMEM_LIMIT)
    if grid:
        params["dimension_semantics"] = ("arbitrary",) * len(grid)
    if comm is None:
        return pl.pallas_call(
            body, name=name, grid=grid, in_specs=in_specs, out_specs=out_specs,
            out_shape=out_shape, scratch_shapes=list(scratch),
            compiler_params=pltpu.CompilerParams(**params), **kw)

    n_in, n_out, n_sc = len(in_specs), len(out_specs), len(scratch)
    c_in, c_out = len(comm.inputs), len(comm.out_shape)
    steps = 1
    for g in grid:
        steps *= g

    def hosted(*refs):
        ins, cins = refs[:n_in], refs[n_in:n_in + c_in]
        o0 = n_in + c_in
        outs, couts = refs[o0:o0 + n_out], refs[o0 + n_out:o0 + n_out + c_out]
        s0 = o0 + n_out + c_out
        sc, csems = refs[s0:s0 + n_sc], refs[s0 + n_sc:]
        lin = 0
        for axis, g in enumerate(grid):
            lin = lin * g + pl.program_id(axis)

        def at(step, fn):
            @pl.when(lin == step % steps)
            def _():
                fn(cins, couts, csems)

        for step, fn in comm.phases:
            if step >= 0:
                at(step, fn)
        body(*ins, *outs, *sc)
        for step, fn in comm.phases:
            if step < 0:
                at(step, fn)

    aliases = dict(kw.pop("input_output_aliases", {}))
    for k, m in comm.aliases.items():
        aliases[n_in + k] = n_out + m
    call = pl.pallas_call(
        hosted, name=name, grid=grid, in_specs=list(in_specs) + [_ANY] * c_in,
        out_specs=list(out_specs) + [_ANY] * c_out, out_shape=list(out_shape) + comm.out_shape,
        scratch_shapes=list(scratch) + comm.sems, input_output_aliases=aliases,
        compiler_params=pltpu.CompilerParams(**params), **kw)

    def run(*args):
        res = call(*args, *comm.inputs)
        return res[:n_out], res[n_out:]

    return run


def _sds(shape, dtype):
    return jax.ShapeDtypeStruct(tuple(shape), dtype)


def _resident(shape):
    zeros = (0,) * len(shape)
    return pl.BlockSpec(tuple(shape), lambda *_: zeros, pipeline_mode=pl.Buffered(1))


def _sigmoid(x):
    return jax.nn.sigmoid(x)


def _softplus(x):
    return jnp.maximum(x, 0.0) + jnp.log(1.0 + jnp.exp(-jnp.abs(x)))


_GELU_C = 0.7978845608028654
_GELU_A = 0.044715


def _gelu_tanh(x):
    return jnp.tanh(_GELU_C * (x + _GELU_A * x * x * x))


def _gelu(x, t):
    return 0.5 * x * (1.0 + t)


def _gelu_grad(x, t):
    return 0.5 * (1.0 + t) + 0.5 * x * (1.0 - t * t) * _GELU_C * (1.0 + 3.0 * _GELU_A * x * x)


def _silu_grad(x):
    s = _sigmoid(x)
    return s * (1.0 + x * (1.0 - s))


def _rms_scale(xv):
    return lax.rsqrt(jnp.mean(xv * xv, axis=-1, keepdims=True) + EPS)


def _rms_bwd(dh, xv, g):
    r = _rms_scale(xv)
    xn = xv * r
    dg = jnp.sum(dh * xn, axis=0, keepdims=True)
    dxn = dh * g
    dx = r * (dxn - xn * jnp.mean(dxn * xn, axis=-1, keepdims=True))
    return dx, dg


def _iota2(shape, dim):
    return lax.broadcasted_iota(jnp.int32, shape, dim)


def _col_to_row(col):
    n = col.shape[0]
    eye = _iota2((n, n), 0) == _iota2((n, n), 1)
    return jnp.sum(jnp.where(eye, col, 0.0), axis=0, keepdims=True)


def _row_to_col(row):
    n = row.shape[1]
    eye = _iota2((n, n), 0) == _iota2((n, n), 1)
    return jnp.sum(jnp.where(eye, row, 0.0), axis=1, keepdims=True)


MXU_DIM = 256


def _hidden_chunks(f, step=3 * MXU_DIM):
    return [(c0, min(c0 + step, f)) for c0 in range(0, f, step)]

def ffn_fwd(x, gnorm, wg, wu, wd, name, comm=None):
    n, d = x.shape
    f = wg.shape[0]
    tm = min(ROW_TILE, n)
    fused = wd is not None

    def body(x_ref, g_ref, wg_ref, wu_ref, *rest):
        if fused:
            wd_ref, xo_ref, h_ref, gate_ref, up_ref, act_ref, acc_ref = rest
        else:
            h_ref, gate_ref, up_ref, act_ref = rest
        xv = x_ref[...]
        h = (xv * _rms_scale(xv) * g_ref[...]).astype(BF16)
        h_ref[...] = h
        chunks = _hidden_chunks(f)

        def gate_up(c0, c1):
            return _dot(h, wg_ref[c0:c1, :], NT), _dot(h, wu_ref[c0:c1, :], NT)

        nxt = gate_up(*chunks[0])
        for idx, (c0, c1) in enumerate(chunks):
            gate, up = nxt
            if idx + 1 < len(chunks):
                nxt = gate_up(*chunks[idx + 1])
            act = (gate * _sigmoid(gate) * up).astype(BF16)
            gate_ref[:, c0:c1] = gate.astype(BF16)
            up_ref[:, c0:c1] = up.astype(BF16)
            act_ref[:, c0:c1] = act
            if fused:
                part = _dot(act, wd_ref[c0:c1, :], NN)
                if c0 == 0:
                    acc_ref[...] = part
                else:
                    acc_ref[...] += part
        if fused:
            xo_ref[...] = xv + 0.5 * acc_ref[...]

    row = pl.BlockSpec((tm, d), lambda i: (i, 0))
    wide = pl.BlockSpec((tm, f), lambda i: (i, 0))
    n_w = 3 if fused else 2
    return _call(
        body, name=name, grid=(n // tm,),
        in_specs=[row, pl.BlockSpec((1, d), lambda i: (0, 0))] + [_resident((f, d))] * n_w,
        out_specs=([row] if fused else []) + [row, wide, wide, wide],
        out_shape=([_sds((n, d), F32)] if fused else []) + [_sds((n, d), BF16)] + [_sds((n, f), BF16)] * 3,
        scratch=[pltpu.VMEM((tm, d), F32)] if fused else [], comm=comm,
    )(*([x, gnorm, wg, wu] + ([wd] if fused else [])))


def ffn_down(x, act, wd, name, comm=None):
    n, d = x.shape
    f = wd.shape[0]
    tm = min(ROW_TILE, n)

    def body(x_ref, a_ref, w_ref, o_ref):
        o_ref[...] = x_ref[...] + 0.5 * _dot(a_ref[...], w_ref[...], NN)

    row = pl.BlockSpec((tm, d), lambda i: (i, 0))
    return _call(
        body, name=name, grid=(n // tm,),
        in_specs=[row, pl.BlockSpec((tm, f), lambda i: (i, 0)), _resident((f, d))],
        out_specs=[row], out_shape=[_sds((n, d), F32)], comm=comm,
    )(x, act, wd)


def ffn_bwd_act(dy, x, gnorm, gate, up, wg, wu, wd, name, comm=None):
    n, d = x.shape
    f = wg.shape[0]
    tm = min(ROW_TILE // 2, n)

    def body(dy_ref, x_ref, g_ref, gate_ref, up_ref, wg_ref, wu_ref, wd_ref,
             dx_ref, dgate_ref, dup_ref, dyh_ref, dg_ref, acc_ref):
        @pl.when(pl.program_id(0) == 0)
        def _():
            dg_ref[...] = jnp.zeros_like(dg_ref)

        dyh = (0.5 * dy_ref[...]).astype(BF16)
        dyh_ref[...] = dyh
        chunks = _hidden_chunks(f, 2 * MXU_DIM)
        next_dact = _dot(dyh, wd_ref[chunks[0][0]:chunks[0][1], :], NT)
        for idx, (c0, c1) in enumerate(chunks):
            dact = next_dact
            if idx + 1 < len(chunks):
                n0, n1 = chunks[idx + 1]
                next_dact = _dot(dyh, wd_ref[n0:n1, :], NT)
            gt = gate_ref[:, c0:c1].astype(F32)
            u = up_ref[:, c0:c1].astype(F32)
            s = _sigmoid(gt)
            dup = (dact * (gt * s)).astype(BF16)
            dgate = (dact * u * (s * (1.0 + gt * (1.0 - s)))).astype(BF16)
            dup_ref[:, c0:c1] = dup
            dgate_ref[:, c0:c1] = dgate
            part = _dot(dgate, wg_ref[c0:c1, :], NN) + _dot(dup, wu_ref[c0:c1, :], NN)
            if c0 == 0:
                acc_ref[...] = part
            else:
                acc_ref[...] += part
        dxn, dg = _rms_bwd(acc_ref[...], x_ref[...], g_ref[...])
        dx_ref[...] = dy_ref[...] + dxn
        dg_ref[...] += dg

    row = pl.BlockSpec((tm, d), lambda i: (i, 0))
    wide = pl.BlockSpec((tm, f), lambda i: (i, 0))
    vec = pl.BlockSpec((1, d), lambda i: (0, 0))
    wres = _resident((f, d))
    return _call(
        body, name=name, grid=(n // tm,),
        in_specs=[row, row, vec, wide, wide, wres, wres, wres],
        out_specs=[row, wide, wide, row, vec],
        out_shape=[_sds((n, d), F32), _sds((n, f), BF16), _sds((n, f), BF16),
                   _sds((n, d), BF16), _sds((1, d), F32)],
        scratch=[pltpu.VMEM((tm, d), F32)], comm=comm,
    )(dy, x, gnorm, gate, up, wg, wu, wd)


def ffn_bwd_w(wide, rows, pairs, name, comm=None):
    n, d = rows[0].shape
    f = wide[0].shape[1]
    fh = f // 2
    tk = min(ROW_TILE, n)
    n_w, n_r = len(wide), len(rows)

    def body(*refs):
        wide_refs, row_refs, outs = refs[:n_w], refs[n_w:n_w + n_r], refs[n_w + n_r:]

        @pl.when(pl.program_id(1) == 0)
        def _():
            for o_ref in outs:
                o_ref[...] = jnp.zeros_like(o_ref)

        row_vals = [r[...] for r in row_refs]
        for c0, c1 in _hidden_chunks(fh, 2 * MXU_DIM):
            for (i, k), o_ref in zip(pairs, outs):
                o_ref[c0:c1, :] += _dot(wide_refs[i][:, c0:c1], row_vals[k], TN)

    row = pl.BlockSpec((tk, d), lambda j, k: (k, 0))
    blk = pl.BlockSpec((tk, fh), lambda j, k: (k, j))
    return _call(
        body, name=name, grid=(2, n // tk),
        in_specs=[blk] * n_w + [row] * n_r,
        out_specs=[pl.BlockSpec((fh, d), lambda j, k: (j, 0))] * len(pairs),
        out_shape=[_sds((f, d), F32)] * len(pairs), comm=comm,
    )(*wide, *rows)


def final_loss(x, gnorm, target, name):
    n, d = x.shape
    tm = min(ROW_TILE, n)

    def body(x_ref, g_ref, t_ref, dx_ref, dg_ref, loss_ref):
        @pl.when(pl.program_id(0) == 0)
        def _():
            dg_ref[...] = jnp.zeros_like(dg_ref)
            loss_ref[...] = jnp.zeros_like(loss_ref)

        xv = x_ref[...]
        y = xv * _rms_scale(xv) * g_ref[...]
        err = y - t_ref[...]
        part = 0.5 * jnp.sum(jnp.mean(err * err, axis=-1, keepdims=True), axis=0, keepdims=True)
        loss_ref[...] += jnp.broadcast_to(part, loss_ref.shape)
        dx, dg = _rms_bwd(err * (1.0 / d), xv, g_ref[...])
        dx_ref[...] = dx
        dg_ref[...] += dg

    row = pl.BlockSpec((tm, d), lambda i: (i, 0))
    vec = pl.BlockSpec((1, d), lambda i: (0, 0))
    return _call(
        body, name=name, grid=(n // tm,),
        in_specs=[row, vec, row],
        out_specs=[row, vec, pl.BlockSpec((1, LANES), lambda i: (0, 0))],
        out_shape=[_sds((n, d), F32), _sds((1, d), F32), _sds((1, LANES), F32)],
    )(x, gnorm, target)


def in_proj_fwd(x, gnorm, w, name, comm=None):
    n, d = x.shape
    cols = w.shape[1]
    tm = min(ROW_TILE, n)

    def body(x_ref, g_ref, w_ref, p_ref, h_ref):
        xv = x_ref[...]
        h = (xv * _rms_scale(xv) * g_ref[...]).astype(BF16)
        h_ref[...] = h
        for c0, c1 in _hidden_chunks(cols):
            p_ref[:, c0:c1] = _dot(h, w_ref[:, c0:c1], NN)

    return _call(
        body, name=name, grid=(n // tm,),
        in_specs=[pl.BlockSpec((tm, d), lambda i: (i, 0)),
                  pl.BlockSpec((1, d), lambda i: (0, 0)), _resident((d, cols))],
        out_specs=[pl.BlockSpec((tm, cols), lambda i: (i, 0)),
                   pl.BlockSpec((tm, d), lambda i: (i, 0))],
        out_shape=[_sds((n, cols), F32), _sds((n, d), BF16)], comm=comm,
    )(x, gnorm, w)


def in_proj_bwd_x(dproj, w, x, gnorm, dres, name, comm=None):
    n, d = x.shape
    cols = w.shape[1]
    tm = min(ROW_TILE, n)

    def body(dp_ref, w_ref, x_ref, g_ref, dr_ref, dx_ref, dg_ref, dxh_ref):
        @pl.when(pl.program_id(0) == 0)
        def _():
            dg_ref[...] = jnp.zeros_like(dg_ref)

        dh = _dot(dp_ref[...], w_ref[...], NT)
        dxn, dg = _rms_bwd(dh, x_ref[...], g_ref[...])
        dx = dr_ref[...] + dxn
        dx_ref[...] = dx
        dxh_ref[...] = (0.5 * dx).astype(BF16)
        dg_ref[...] += dg

    row = pl.BlockSpec((tm, d), lambda i: (i, 0))
    vec = pl.BlockSpec((1, d), lambda i: (0, 0))
    return _call(
        body, name=name, grid=(n // tm,),
        in_specs=[pl.BlockSpec((tm, cols), lambda i: (i, 0)),
                  _resident((d, cols)), row, vec, row],
        out_specs=[row, vec, row],
        out_shape=[_sds((n, d), F32), _sds((1, d), F32), _sds((n, d), BF16)], comm=comm,
    )(dproj, w, x, gnorm, dres)


def matmul_tn(a_list, b, tn, name):
    n, cb = b.shape
    widths = [a.shape[1] for a in a_list]
    tk = min(ROW_TILE, n)

    def body(*refs):
        a_refs, b_ref, o_ref = refs[:-2], refs[-2], refs[-1]

        @pl.when(pl.program_id(0) == 0)
        def _():
            o_ref[...] = jnp.zeros_like(o_ref)

        r0 = 0
        for a_ref, ka in zip(a_refs, widths):
            av = a_ref[...]
            for c0, c1 in _hidden_chunks(cb, tn):
                o_ref[r0:r0 + ka, c0:c1] += _dot(av, b_ref[:, c0:c1], TN)
            r0 += ka

    return _call(
        body, name=name, grid=(n // tk,),
        in_specs=[pl.BlockSpec((tk, ka), lambda k: (k, 0)) for ka in widths]
        + [pl.BlockSpec((tk, cb), lambda k: (k, 0))],
        out_specs=pl.BlockSpec((sum(widths), cb), lambda k: (0, 0)),
        out_shape=_sds((sum(widths), cb), F32),
    )(*a_list, b)


def out_proj_fwd(x, sg_out, dn_out, w, name):
    n, d = x.shape
    tm = min(ROW_TILE, n)

    def body(x_ref, a_ref, b_ref, w_ref, o_ref):
        o_ref[...] = (x_ref[...] + _dot(a_ref[...], w_ref[0:HALF_W, :], NN)
                      + _dot(b_ref[...], w_ref[HALF_W:2 * HALF_W, :], NN))

    row = pl.BlockSpec((tm, d), lambda i: (i, 0))
    half = pl.BlockSpec((tm, HALF_W), lambda i: (i, 0))
    return _call(
        body, name=name, grid=(n // tm,),
        in_specs=[row, half, half, pl.BlockSpec((2 * HALF_W, d), lambda i: (0, 0))],
        out_specs=row, out_shape=_sds((n, d), F32),
    )(x, sg_out, dn_out, w)


def out_proj_bwd_x(dy, w, name, comm=None):
    n, d = dy.shape
    tm = min(ROW_TILE, n)

    def body(dy_ref, w_ref, dsg_ref, ddn_ref, dyb_ref):
        dyb = dy_ref[...].astype(BF16)
        dyb_ref[...] = dyb
        dsg_ref[...] = _dot(dyb, w_ref[0:HALF_W, :], NT)
        ddn_ref[...] = _dot(dyb, w_ref[HALF_W:2 * HALF_W, :], NT)

    row = pl.BlockSpec((tm, d), lambda i: (i, 0))
    half = pl.BlockSpec((tm, HALF_W), lambda i: (i, 0))
    return _call(
        body, name=name, grid=(n // tm,),
        in_specs=[row, pl.BlockSpec((2 * HALF_W, d), lambda i: (0, 0))],
        out_specs=[half, half, row],
        out_shape=[_sds((n, HALF_W), F32), _sds((n, HALF_W), F32), _sds((n, d), BF16)], comm=comm,
    )(dy, w)


SG_PAIRS = SG_GROUPS // 2


def _sg_low_half():
    return _iota2((SG_CHUNK, LANES), 1) < SG_GROUP_DIM


def _sg_pair_cols(p):
    return slice(p * LANES, (p + 1) * LANES)


def _sg_causal():
    return _iota2((SG_CHUNK, SG_CHUNK), 0) >= _iota2((SG_CHUNK, SG_CHUNK), 1)


def _sg_forward_chunk(pu, pv, ln_g, ln_b, wc, bias, low):
    tu, tv = _gelu_tanh(pu), _gelu_tanh(pv)
    u = _gelu(pu, tu)
    v = _gelu(pv, tv)
    mu = jnp.mean(v, axis=-1, keepdims=True)
    vc = v - mu
    rs = lax.rsqrt(jnp.mean(vc * vc, axis=-1, keepdims=True) + EPS)
    xhat = vc * rs
    vn = (xhat * ln_g + ln_b).astype(BF16)
    parts = []
    for p in range(SG_PAIRS):
        vn_p = vn[:, _sg_pair_cols(p)]
        parts.append(jnp.where(low, _dot(wc[2 * p], vn_p, NN), _dot(wc[2 * p + 1], vn_p, NN)))
    vs = bias + jnp.concatenate(parts, axis=1)
    return u, xhat, rs, vn, vs, tu, tv


def sg_fwd(proj, ln_g, ln_b, w_s, bias_tile, name):
    n = proj.shape[0]
    tm = min(ROW_TILE, n)

    def body(pu_ref, pv_ref, g_ref, b_ref, w_ref, bias_ref, o_ref):
        causal = _sg_causal()
        wc = [jnp.where(causal, w_ref[g], 0.0).astype(BF16) for g in range(SG_GROUPS)]
        masks = _sg_low_half()
        for ci in range(tm // SG_CHUNK):
            rows = slice(ci * SG_CHUNK, (ci + 1) * SG_CHUNK)
            u, _, _, _, vs, _, _ = _sg_forward_chunk(pu_ref[rows, :], pv_ref[rows, :], g_ref[...],
                                                     b_ref[...], wc, bias_ref[...], masks)
            o_ref[rows, :] = (u * vs).astype(BF16)

    vec = pl.BlockSpec((1, HALF_W), lambda i: (0, 0))
    return _call(
        body, name=name, grid=(n // tm,),
        in_specs=[pl.BlockSpec((tm, HALF_W), lambda i: (i, 0)),
                  pl.BlockSpec((tm, HALF_W), lambda i: (i, 1)), vec, vec,
                  pl.BlockSpec((SG_GROUPS, SG_CHUNK, SG_CHUNK), lambda i: (0, 0, 0)),
                  pl.BlockSpec((SG_CHUNK, HALF_W), lambda i: (0, 0))],
        out_specs=pl.BlockSpec((tm, HALF_W), lambda i: (i, 0)),
        out_shape=_sds((n, HALF_W), BF16),
    )(proj, proj, ln_g, ln_b, w_s, bias_tile)


def sg_bwd(dsg, proj, ln_g, ln_b, w_s, bias_tile, name):
    n = proj.shape[0]
    tm = min(ROW_TILE, n)

    def body(d_ref, pu_ref, pv_ref, g_ref, b_ref, w_ref, bias_ref,
             dp_ref, dw_ref, db_ref, dlg_ref, dlb_ref):
        @pl.when(pl.program_id(0) == 0)
        def _():
            dw_ref[...] = jnp.zeros_like(dw_ref)
            db_ref[...] = jnp.zeros_like(db_ref)
            dlg_ref[...] = jnp.zeros_like(dlg_ref)
            dlb_ref[...] = jnp.zeros_like(dlb_ref)

        causal = _sg_causal()
        wc = [jnp.where(causal, w_ref[g], 0.0).astype(BF16) for g in range(SG_GROUPS)]
        masks = _sg_low_half()
        ln_g_v = g_ref[...]
        for ci in range(tm // SG_CHUNK):
            rows = slice(ci * SG_CHUNK, (ci + 1) * SG_CHUNK)
            pu = pu_ref[rows, :]
            pv = pv_ref[rows, :]
            u, xhat, rs, vn, vs, tu, tv = _sg_forward_chunk(pu, pv, ln_g_v, b_ref[...], wc,
                                                            bias_ref[...], masks)
            dout = d_ref[rows, :]
            dp_ref[rows, 0:HALF_W] = (dout * vs * _gelu_grad(pu, tu)).astype(BF16)
            dvs = dout * u
            dvs_b = dvs.astype(BF16)
            db_ref[...] += dvs
            dvn_parts = []
            for p in range(SG_PAIRS):
                dvs_p = dvs_b[:, _sg_pair_cols(p)]
                vn_p = vn[:, _sg_pair_cols(p)]
                dvn_parts.append(jnp.where(masks, _dot(wc[2 * p], dvs_p, TN), _dot(wc[2 * p + 1], dvs_p, TN)))
                zero = jnp.zeros_like(dvs_p)
                dw_ref[2 * p] += jnp.where(causal, _dot(jnp.where(masks, dvs_p, zero), vn_p, NT), 0.0)
                dw_ref[2 * p + 1] += jnp.where(causal, _dot(jnp.where(masks, zero, dvs_p), vn_p, NT), 0.0)
            dvn = jnp.concatenate(dvn_parts, axis=1)
            dlg_ref[...] += jnp.sum(dvn * xhat, axis=0, keepdims=True)
            dlb_ref[...] += jnp.sum(dvn, axis=0, keepdims=True)
            dxh = dvn * ln_g_v
            dv = rs * (dxh - jnp.mean(dxh, axis=-1, keepdims=True)
                       - xhat * jnp.mean(dxh * xhat, axis=-1, keepdims=True))
            dp_ref[rows, HALF_W:2 * HALF_W] = (dv * _gelu_grad(pv, tv)).astype(BF16)

    vec = pl.BlockSpec((1, HALF_W), lambda i: (0, 0))
    wspec = pl.BlockSpec((SG_GROUPS, SG_CHUNK, SG_CHUNK), lambda i: (0, 0, 0))
    tile = pl.BlockSpec((SG_CHUNK, HALF_W), lambda i: (0, 0))
    return _call(
        body, name=name, grid=(n // tm,),
        in_specs=[pl.BlockSpec((tm, HALF_W), lambda i: (i, 0)),
                  pl.BlockSpec((tm, HALF_W), lambda i: (i, 0)),
                  pl.BlockSpec((tm, HALF_W), lambda i: (i, 1)), vec, vec, wspec, tile],
        out_specs=[pl.BlockSpec((tm, 2 * HALF_W), lambda i: (i, 0)), wspec, tile, vec, vec],
        out_shape=[_sds((n, PROJ_W), BF16), _sds((SG_GROUPS, SG_CHUNK, SG_CHUNK), F32),
                   _sds((SG_CHUNK, HALF_W), F32), _sds((1, HALF_W), F32), _sds((1, HALF_W), F32)],
    )(dsg, proj, proj, ln_g, ln_b, w_s, bias_tile)


CONV_K = 4
CONV_BLOCK = 256


SUBLANES = 8


def _shift_down(x, s, row):
    if s == 0:
        return x
    rolled = pltpu.roll(x, s, 0)
    head = jnp.where(row[:SUBLANES] >= s, rolled[:SUBLANES], 0.0)
    return jnp.concatenate([head, rolled[SUBLANES:]], axis=0)


def _shift_up(x, s, row):
    if s == 0:
        return x
    t_len = x.shape[0]
    rolled = pltpu.roll(x, t_len - s, 0)
    tail = jnp.where(row[t_len - SUBLANES:] < t_len - s, rolled[t_len - SUBLANES:], 0.0)
    return jnp.concatenate([rolled[:t_len - SUBLANES], tail], axis=0)


def _conv_taps(x, row):
    return [_shift_down(x, CONV_K - 1 - j, row) for j in range(CONV_K)]


def _conv(taps, w):
    y = taps[0] * w[0:1, :]
    for j in range(1, CONV_K):
        y = y + taps[j] * w[j:j + 1, :]
    return y


def dn_conv_fwd(proj3, conv_w, name):
    b, t, _ = proj3.shape
    nblk = 3 * HALF_W // CONV_BLOCK
    first = 2 * HALF_W // CONV_BLOCK
    n_norm = 2 * HALF_W // CONV_BLOCK

    def body(x_ref, w_ref, o_ref):
        s = pl.program_id(1)
        x = x_ref[0]
        y = _conv(_conv_taps(x, _iota2(x.shape, 0)), w_ref[...])
        y = y * _sigmoid(y)

        @pl.when(s < n_norm)
        def _():
            for h in range(CONV_BLOCK // HEAD_DIM):
                cs = slice(h * HEAD_DIM, (h + 1) * HEAD_DIM)
                yh = y[:, cs]
                o_ref[0, :, cs] = yh * lax.rsqrt(jnp.sum(yh * yh, axis=-1, keepdims=True) + EPS)

        @pl.when(s >= n_norm)
        def _():
            o_ref[0] = y

    return _call(
        body, name=name, grid=(b, nblk),
        in_specs=[pl.BlockSpec((1, t, CONV_BLOCK), lambda i, s: (i, 0, first + s)),
                  pl.BlockSpec((CONV_K, CONV_BLOCK), lambda i, s: (0, s))],
        out_specs=pl.BlockSpec((1, t, CONV_BLOCK), lambda i, s: (i, 0, s)),
        out_shape=_sds((b, t, 3 * HALF_W), F32),
    )(proj3, conv_w)


def dn_conv_bwd(dqkv, proj3, conv_w, dproj3, name, comm=None):
    b, t, _ = proj3.shape
    nblk = 3 * HALF_W // CONV_BLOCK
    first = 2 * HALF_W // CONV_BLOCK
    n_norm = 2 * HALF_W // CONV_BLOCK

    def body(d_ref, x_ref, w_ref, dproj_in, dx_ref, dw_ref, ds_ref):
        s = pl.program_id(0)

        @pl.when(pl.program_id(1) == 0)
        def _():
            dw_ref[...] = jnp.zeros_like(dw_ref)

        x = x_ref[0]
        w = w_ref[...]
        row = _iota2(x.shape, 0)
        taps = _conv_taps(x, row)
        c = _conv(taps, w)
        sg = _sigmoid(c)
        y = c * sg

        @pl.when(s < n_norm)
        def _():
            for h in range(CONV_BLOCK // HEAD_DIM):
                cs = slice(h * HEAD_DIM, (h + 1) * HEAD_DIM)
                yh = y[:, cs]
                r = lax.rsqrt(jnp.sum(yh * yh, axis=-1, keepdims=True) + EPS)
                nh = yh * r
                dn = d_ref[0, :, cs]
                ds_ref[:, cs] = r * (dn - nh * jnp.sum(dn * nh, axis=-1, keepdims=True))

        @pl.when(s >= n_norm)
        def _():
            ds_ref[...] = d_ref[0]

        dc = ds_ref[...] * (sg * (1.0 + c * (1.0 - sg)))
        dx = _shift_up(dc, CONV_K - 1, row) * w[0:1, :]
        for j in range(1, CONV_K):
            dx = dx + _shift_up(dc, CONV_K - 1 - j, row) * w[j:j + 1, :]
        dx_ref[0] = dx.astype(BF16)
        for j in range(CONV_K):
            dw_ref[j:j + 1, :] += jnp.sum(dc * taps[j], axis=0, keepdims=True)

    return _call(
        body, name=name, grid=(nblk, b),
        in_specs=[pl.BlockSpec((1, t, CONV_BLOCK), lambda s, i: (i, 0, s)),
                  pl.BlockSpec((1, t, CONV_BLOCK), lambda s, i: (i, 0, first + s)),
                  pl.BlockSpec((CONV_K, CONV_BLOCK), lambda s, i: (0, s)), _ANY],
        out_specs=[pl.BlockSpec((1, t, CONV_BLOCK), lambda s, i: (i, 0, first + s)),
                   pl.BlockSpec((CONV_K, CONV_BLOCK), lambda s, i: (0, s))],
        out_shape=[_sds(dproj3.shape, BF16), _sds((CONV_K, 3 * HALF_W), F32)],
        scratch=[pltpu.VMEM((t, CONV_BLOCK), F32)],
        input_output_aliases={3: 0}, comm=comm,
    )(dqkv, proj3, conv_w, dproj3)


def _chunk_masks():
    ii = _iota2((DN_CHUNK, DN_CHUNK), 0)
    jj = _iota2((DN_CHUNK, DN_CHUNK), 1)
    return ii >= jj, ii > jj, ii == jj


LOCKSTEP_CHUNKS = 4


def _inv_unit_lower_many(l_mats, eye):
    eye_f = jnp.where(eye, 1.0, 0.0)
    ps = [-l for l in l_mats]
    ts = [eye_f + p for p in ps]
    pss = [_split(p) for p in ps]
    size = 2
    while size < DN_CHUNK:
        ps = [_dot3(s, s) for s in pss]
        pss = [_split(p) for p in ps]
        ts = [t + _dot3(_split(t), s) for t, s in zip(ts, pss)]
        size *= 2
    return ts


def _gates(pba, ea_row, dtb_row):
    beta = _sigmoid(pba)
    g = -ea_row * _softplus(pba + dtb_row)
    return beta, g


def _chunk_decay(gcol):
    incl, strict, eye = _chunk_masks()
    grow = jnp.sum(jnp.where(eye, gcol, 0.0), axis=0, keepdims=True)
    decay = jnp.where(incl, jnp.exp(jnp.where(incl, gcol - grow, 0.0)), 0.0)
    return decay, incl, strict, eye


def dn_chunk_fwd(qkv, proj3, alog_row, dtb_row, name, comm=None):
    b, t, _ = qkv.shape
    rblk = min(256, t)
    n_in = rblk // DN_CHUNK

    def body(q_ref, k_ref, v_ref, pba_ref, al_ref, dtb_ref,
             u_ref, w_ref, qd_ref, kd_ref, qk_ref, ti_ref, gc_ref):
        ea = jnp.exp(al_ref[...])
        tri = jnp.where(_chunk_masks()[0], 1.0, 0.0)

        _, strict, eye = _chunk_masks()

        def chunk_group(cg, carry):
            items = []
            for sub in range(LOCKSTEP_CHUNKS):
                rows = pl.ds(pl.multiple_of((cg * LOCKSTEP_CHUNKS + sub) * DN_CHUNK, DN_CHUNK), DN_CHUNK)
                beta_all, g_all = _gates(pba_ref[0, rows, :], ea, dtb_ref[...])
                gc = _dot_exact_lhs(tri, g_all)
                gc_ref[0, rows, :] = gc
                for h in range(N_HEADS):
                    items.append((rows, h, beta_all[:, h:h + 1], gc[:, N_HEADS + h:N_HEADS + h + 1]))
            ks, kbs, decays, egs = [], [], [], []
            for rows, h, beta, gcol in items:
                cs = slice(h * HEAD_DIM, (h + 1) * HEAD_DIM)
                k = k_ref[0, rows, cs]
                ks.append(k)
                kbs.append(k * beta)
                decays.append(_chunk_decay(gcol)[0])
                egs.append(jnp.exp(gcol))
            ms = [_bdot(kb, k, NT) for kb, k in zip(kbs, ks)]
            tinvs = _inv_unit_lower_many([jnp.where(strict, m * dc, 0.0) for m, dc in zip(ms, decays)], eye)
            tsps = [_split(t) for t in tinvs]
            for (rows, h, beta, gcol), tsp, tinv in zip(items, tsps, tinvs):
                cs = slice(h * HEAD_DIM, (h + 1) * HEAD_DIM)
                u_ref[0, rows, cs] = _dot3(tsp, _split(v_ref[0, rows, cs] * beta))
                ti_ref[0, h, rows, :] = tinv
            for (rows, h, beta, gcol), tsp, kb, eg in zip(items, tsps, kbs, egs):
                cs = slice(h * HEAD_DIM, (h + 1) * HEAD_DIM)
                w_ref[0, rows, cs] = _dot3(tsp, _split(kb * eg))
            for (rows, h, beta, gcol), k, dc, eg in zip(items, ks, decays, egs):
                cs = slice(h * HEAD_DIM, (h + 1) * HEAD_DIM)
                q = q_ref[0, rows, cs] * QK_SCALE
                qk_ref[0, h, rows, :] = _bdot(q, k, NT) * dc
                qd_ref[0, rows, cs] = q * eg
                kd_ref[0, rows, cs] = k * jnp.exp(gcol[DN_CHUNK - 1:DN_CHUNK, :] - gcol)
            return carry

        lax.fori_loop(0, n_in // LOCKSTEP_CHUNKS, chunk_group, 0)

    def seg(cblk):
        return pl.BlockSpec((1, rblk, HALF_W), lambda i, r: (i, r, cblk))

    vec = pl.BlockSpec((1, LANES), lambda i, r: (0, 0))
    wide = pl.BlockSpec((1, rblk, HALF_W), lambda i, r: (i, r, 0))
    sq = pl.BlockSpec((1, N_HEADS, rblk, DN_CHUNK), lambda i, r: (i, 0, r, 0))
    return _call(
        body, name=name, grid=(b, t // rblk),
        in_specs=[seg(0), seg(1), seg(2),
                  pl.BlockSpec((1, rblk, LANES), lambda i, r: (i, r, GATE_COL_BLOCK)), vec, vec],
        out_specs=[wide, wide, wide, wide, sq, sq,
                   pl.BlockSpec((1, rblk, LANES), lambda i, r: (i, r, 0))],
        out_shape=[_sds((b, t, HALF_W), F32)] * 4
        + [_sds((b, N_HEADS, t, DN_CHUNK), F32)] * 2 + [_sds((b, t, LANES), F32)], comm=comm,
    )(qkv, qkv, qkv, proj3, alog_row, dtb_row)


def dn_scan_fwd(u, w, qd, kd, qk, gc, name):
    b, t, _ = u.shape
    nc = t // DN_CHUNK
    bh = b * N_HEADS

    def body(u_ref, w_ref, qd_ref, kd_ref, qk_ref, gc_ref, o_ref, sin_ref, s_ref):
        @pl.when(pl.program_id(0) == 0)
        def _():
            s_ref[...] = jnp.zeros_like(s_ref)

        items = [(bi, h, slice(h * HEAD_DIM, (h + 1) * HEAD_DIM)) for bi in range(b) for h in range(N_HEADS)]
        sbs = []
        for bi, h, cs in items:
            s = s_ref[bi * N_HEADS + h]
            sin_ref[0, bi * N_HEADS + h] = s
            sbs.append(s.astype(BF16))
        ws = [_bdot(w_ref[bi, :, cs], sb, NN) for (bi, h, cs), sb in zip(items, sbs)]
        qs = [_bdot(qd_ref[bi, :, cs], sb, NN) for (bi, h, cs), sb in zip(items, sbs)]
        vbs = [(u_ref[bi, :, cs] - wsi).astype(BF16) for (bi, h, cs), wsi in zip(items, ws)]
        for (bi, h, cs), qsi, vb in zip(items, qs, vbs):
            o_ref[bi, :, cs] = qsi + _bdot(qk_ref[bi, h], vb, NN)
        for (bi, h, cs), vb in zip(items, vbs):
            gl = jnp.exp(gc_ref[bi, DN_CHUNK - 1:DN_CHUNK, N_HEADS + h:N_HEADS + h + 1])
            idx = bi * N_HEADS + h
            s_ref[idx] = s_ref[idx] * gl + _bdot(kd_ref[bi, :, cs], vb, TN)

    wide = pl.BlockSpec((b, DN_CHUNK, HALF_W), lambda c: (0, c, 0))
    return _call(
        body, name=name, grid=(nc,),
        in_specs=[wide, wide, wide, wide,
                  pl.BlockSpec((b, N_HEADS, DN_CHUNK, DN_CHUNK), lambda c: (0, 0, c, 0)),
                  pl.BlockSpec((b, DN_CHUNK, LANES), lambda c: (0, c, 0))],
        out_specs=[wide, pl.BlockSpec((1, bh, HEAD_DIM, HEAD_DIM), lambda c: (c, 0, 0, 0))],
        out_shape=[_sds((b, t, HALF_W), F32), _sds((nc, bh, HEAD_DIM, HEAD_DIM), F32)],
        scratch=[pltpu.VMEM((bh, HEAD_DIM, HEAD_DIM), F32)],
    )(u, w, qd, kd, qk, gc)


def dn_scan_bwd(do, u, w, qd, kd, qk, gc, s_in, name):
    b, t, _ = u.shape
    nc = t // DN_CHUNK
    bh = b * N_HEADS

    def body(do_ref, u_ref, w_ref, qd_ref, kd_ref, qk_ref, gc_ref, sin_ref,
             du_ref, dw_ref, dqd_ref, dkd_ref, dqk_ref, dgc_ref, ds_ref):
        @pl.when(pl.program_id(0) == 0)
        def _():
            ds_ref[...] = jnp.zeros_like(ds_ref)

        last_row = _iota2((DN_CHUNK, LANES), 0) == DN_CHUNK - 1
        lane = _iota2((DN_CHUNK, LANES), 1)
        items = [(bi, h, slice(h * HEAD_DIM, (h + 1) * HEAD_DIM)) for bi in range(b) for h in range(N_HEADS)]
        sbs = [sin_ref[0, bi * N_HEADS + h].astype(BF16) for bi, h, cs in items]
        wvs = [w_ref[bi, :, cs].astype(BF16) for bi, h, cs in items]
        dovs = [do_ref[bi, :, cs].astype(BF16) for bi, h, cs in items]
        dsbs = [ds_ref[bi * N_HEADS + h].astype(BF16) for bi, h, cs in items]
        vbs = [(u_ref[bi, :, cs] - _dot(wv, sb, NN)).astype(BF16)
               for (bi, h, cs), wv, sb in zip(items, wvs, sbs)]
        for (bi, h, cs), dov, sb in zip(items, dovs, sbs):
            dqd_ref[bi, :, cs] = _dot(dov, sb, NT)
        dvns = [_dot(kd_ref[bi, :, cs].astype(BF16), dsb, NN) + _dot(qk_ref[bi, h].astype(BF16), dov, TN)
                for (bi, h, cs), dsb, dov in zip(items, dsbs, dovs)]
        for (bi, h, cs), vb, dsb, dov in zip(items, vbs, dsbs, dovs):
            dkd_ref[bi, :, cs] = _dot(vb, dsb, NT)
            dqk_ref[bi, h] = _dot(dov, vb, NT)
        dgls = []
        for (bi, h, cs), dvn, sb, wv, dov in zip(items, dvns, sbs, wvs, dovs):
            idx = bi * N_HEADS + h
            du_ref[bi, :, cs] = dvn
            dvn_b = dvn.astype(BF16)
            dw_ref[bi, :, cs] = -_dot(dvn_b, sb, NT)
            gl = jnp.exp(gc_ref[bi, DN_CHUNK - 1:DN_CHUNK, N_HEADS + h:N_HEADS + h + 1])
            ds = ds_ref[idx]
            dgl = jnp.sum(jnp.sum(ds * sin_ref[0, idx], axis=1, keepdims=True), axis=0, keepdims=True)
            dgls.append(dgl * gl)
            ds_ref[idx] = (ds * gl + _dot(qd_ref[bi, :, cs].astype(BF16), dov, TN)
                           - _dot(wv, dvn_b, TN))
        for bi in range(b):
            dgc = jnp.zeros((DN_CHUNK, LANES), F32)
            for h in range(N_HEADS):
                dgc = dgc + jnp.where(jnp.logical_and(last_row, lane == N_HEADS + h),
                                      dgls[bi * N_HEADS + h], 0.0)
            dgc_ref[bi] = dgc

    def rev(c):
        return nc - 1 - c

    wide = pl.BlockSpec((b, DN_CHUNK, HALF_W), lambda c: (0, rev(c), 0))
    sq = pl.BlockSpec((b, N_HEADS, DN_CHUNK, DN_CHUNK), lambda c: (0, 0, rev(c), 0))
    gates = pl.BlockSpec((b, DN_CHUNK, LANES), lambda c: (0, rev(c), 0))
    return _call(
        body, name=name, grid=(nc,),
        in_specs=[wide, wide, wide, wide, wide, sq, gates,
                  pl.BlockSpec((1, bh, HEAD_DIM, HEAD_DIM), lambda c: (rev(c), 0, 0, 0))],
        out_specs=[wide, wide, wide, wide, sq, gates],
        out_shape=[_sds((b, t, HALF_W), F32)] * 4
        + [_sds((b, N_HEADS, t, DN_CHUNK), F32), _sds((b, t, LANES), F32)],
        scratch=[pltpu.VMEM((bh, HEAD_DIM, HEAD_DIM), F32)],
    )(do, u, w, qd, kd, qk, gc, s_in)


def dn_chunk_bwd(qkv, proj3, alog_row, dtb_row, tinv, u, w, du, dw, dqd, dkd, dqk, dgc_scan, dproj3, name,
                 comm=None):
    b, t, _ = qkv.shape
    rblk = min(256, t)
    n_in = rblk // DN_CHUNK

    def body(q_ref, k_ref, v_ref, pba_ref, al_ref, dtb_ref, ti_ref, u_ref, w_ref,
             du_ref, dw_ref, dqd_ref, dkd_ref, dqk_ref, dgs_ref, dproj_in,
             dq_ref, dpba_ref, dal_ref, ddtb_ref):
        @pl.when(jnp.logical_and(pl.program_id(0) == 0, pl.program_id(1) == 0))
        def _():
            dal_ref[...] = jnp.zeros_like(dal_ref)
            ddtb_ref[...] = jnp.zeros_like(ddtb_ref)

        ea = jnp.exp(al_ref[...])
        incl0 = _chunk_masks()[0]
        tri = jnp.where(incl0, 1.0, 0.0)
        tri_up = jnp.where(_iota2((DN_CHUNK, DN_CHUNK), 1) >= _iota2((DN_CHUNK, DN_CHUNK), 0), 1.0, 0.0)
        lane = _iota2((DN_CHUNK, LANES), 1)
        last_col = _iota2((DN_CHUNK, 1), 0) == DN_CHUNK - 1

        _, strict, _ = _chunk_masks()
        gate_lane = jnp.logical_and(lane >= N_HEADS, lane < 2 * N_HEADS)

        def chunk_group(cg, carry):
            tiles, items = [], []
            for sub in range(LOCKSTEP_CHUNKS):
                rows = pl.ds(pl.multiple_of((cg * LOCKSTEP_CHUNKS + sub) * DN_CHUNK, DN_CHUNK), DN_CHUNK)
                pba = pba_ref[0, rows, :]
                beta_all, g_all = _gates(pba, ea, dtb_ref[...])
                gc = _dot_exact_lhs(tri, g_all)
                tiles.append((rows, pba, beta_all, g_all))
                for h in range(N_HEADS):
                    items.append((sub, rows, h, slice(h * HEAD_DIM, (h + 1) * HEAD_DIM),
                                  beta_all[:, h:h + 1], gc[:, N_HEADS + h:N_HEADS + h + 1]))
            decays = [_chunk_decay(gcol)[0] for _, _, _, _, _, gcol in items]
            egs = [jnp.exp(gcol) for _, _, _, _, _, gcol in items]
            qbs = [(q_ref[0, rows, cs] * QK_SCALE).astype(BF16) for _, rows, h, cs, _, _ in items]
            kfs = [k_ref[0, rows, cs].astype(BF16) for _, rows, h, cs, _, _ in items]
            kbs = [k_ref[0, rows, cs] * beta for _, rows, h, cs, beta, _ in items]
            kbbs = [kb.astype(BF16) for kb in kbs]
            tsps = [_split(ti_ref[0, h, rows, :]) for _, rows, h, cs, _, _ in items]
            drus = [_dot3(tsp, _split(du_ref[0, rows, cs]), TN)
                    for (_, rows, h, cs, _, _), tsp in zip(items, tsps)]
            drws = [_dot3(tsp, _split(dw_ref[0, rows, cs]), TN)
                    for (_, rows, h, cs, _, _), tsp in zip(items, tsps)]
            m_kks = [_dot(kbb, kf, NT) for kbb, kf in zip(kbbs, kfs)]
            a_qks = [_dot(qb, kf, NT) for qb, kf in zip(qbs, kfs)]
            dls = [-jnp.where(strict, _dot3(_split(dru), _split(u_ref[0, rows, cs]), NT)
                              + _dot3(_split(drw), _split(w_ref[0, rows, cs]), NT), 0.0)
                   for (_, rows, h, cs, _, _), dru, drw in zip(items, drus, drws)]
            dms = [(dl * dc).astype(BF16) for dl, dc in zip(dls, decays)]
            das = [(dqk_ref[0, h, rows, :] * dc).astype(BF16)
                   for (_, rows, h, cs, _, _), dc in zip(items, decays)]
            dkb_mm = [_dot(dm, kf, NN) for dm, kf in zip(dms, kfs)]
            dk_mm = [_dot(dm, kbb, TN) + _dot(da, qb, TN) for dm, kbb, da, qb in zip(dms, kbbs, das, qbs)]
            dqs_mm = [_dot(da, kf, NN) for da, kf in zip(das, kfs)]
            dgc_tiles = [dgs_ref[0, rows, :] for rows, _, _, _ in tiles]
            dbeta_tiles = [jnp.zeros((DN_CHUNK, LANES), F32) for _ in tiles]
            for n_it, (sub, rows, h, cs, beta, gcol) in enumerate(items):
                eg, dc = egs[n_it], decays[n_it]
                k = k_ref[0, rows, cs]
                q = q_ref[0, rows, cs] * QK_SCALE
                kb, dru, drw = kbs[n_it], drus[n_it], drws[n_it]
                ek = jnp.exp(gcol[DN_CHUNK - 1:DN_CHUNK, :] - gcol)
                e_mat = (dls[n_it] * m_kks[n_it] + dqk_ref[0, h, rows, :] * a_qks[n_it]) * dc
                dkb = drw * eg + dkb_mm[n_it]
                dqd = dqd_ref[0, rows, cs]
                dkd = dkd_ref[0, rows, cs]
                kdk = dkd * k * ek
                kdk_total = jnp.sum(jnp.sum(kdk, axis=0, keepdims=True), axis=1, keepdims=True)
                dg = (jnp.sum(drw * kb * eg + dqd * q * eg - kdk, axis=-1, keepdims=True)
                      + jnp.sum(e_mat, axis=1, keepdims=True)
                      - _row_to_col(jnp.sum(e_mat, axis=0, keepdims=True))
                      + jnp.where(last_col, kdk_total, 0.0))
                dbeta = jnp.sum(dkb * k + dru * v_ref[0, rows, cs], axis=-1, keepdims=True)
                dq_ref[0, rows, cs] = (dqs_mm[n_it] + dqd * eg) * QK_SCALE
                dq_ref[0, rows, pl.ds(HALF_W + h * HEAD_DIM, HEAD_DIM)] = dk_mm[n_it] + dkd * ek + dkb * beta
                dq_ref[0, rows, pl.ds(2 * HALF_W + h * HEAD_DIM, HEAD_DIM)] = dru * beta
                dgc_tiles[sub] = dgc_tiles[sub] + jnp.where(lane == N_HEADS + h, dg, 0.0)
                dbeta_tiles[sub] = dbeta_tiles[sub] + jnp.where(lane == h, dbeta, 0.0)
            for (rows, pba, beta_all, g_all), dgc_tile, dbeta_tile in zip(tiles, dgc_tiles, dbeta_tiles):
                dg_tile = _dot_exact_lhs(tri_up, dgc_tile)
                da_pre = dg_tile * (-ea) * _sigmoid(pba + dtb_ref[...])
                dal_ref[...] += jnp.sum(jnp.where(gate_lane, dg_tile * g_all, 0.0), axis=0, keepdims=True)
                ddtb_ref[...] += jnp.sum(jnp.where(gate_lane, da_pre, 0.0), axis=0, keepdims=True)
                dpba_ref[0, rows, :] = jnp.where(lane < N_HEADS, dbeta_tile * beta_all * (1.0 - beta_all),
                                                 jnp.where(gate_lane, da_pre, 0.0)).astype(BF16)
            return carry

        lax.fori_loop(0, n_in // LOCKSTEP_CHUNKS, chunk_group, 0)

    def seg(cblk):
        return pl.BlockSpec((1, rblk, HALF_W), lambda i, r: (i, r, cblk))

    vec = pl.BlockSpec((1, LANES), lambda i, r: (0, 0))
    wide = pl.BlockSpec((1, rblk, HALF_W), lambda i, r: (i, r, 0))
    sq = pl.BlockSpec((1, N_HEADS, rblk, DN_CHUNK), lambda i, r: (i, 0, r, 0))
    gates = pl.BlockSpec((1, rblk, LANES), lambda i, r: (i, r, 0))
    return _call(
        body, name=name, grid=(b, t // rblk),
        in_specs=[seg(0), seg(1), seg(2),
                  pl.BlockSpec((1, rblk, LANES), lambda i, r: (i, r, GATE_COL_BLOCK)), vec, vec,
                  sq, wide, wide, wide, wide, wide, wide, sq, gates, _ANY],
        out_specs=[pl.BlockSpec((1, rblk, 3 * HALF_W), lambda i, r: (i, r, 0)),
                   pl.BlockSpec((1, rblk, LANES), lambda i, r: (i, r, GATE_COL_BLOCK)), vec, vec],
        out_shape=[_sds((b, t, 3 * HALF_W), F32), _sds(dproj3.shape, BF16),
                   _sds((1, LANES), F32), _sds((1, LANES), F32)],
        input_output_aliases={15: 1}, comm=comm,
    )(qkv, qkv, qkv, proj3, alog_row, dtb_row, tinv, u, w, du, dw, dqd, dkd, dqk, dgc_scan, dproj3)


def dn_out_fwd(o, proj, dn_norm, name):
    n = o.shape[0]
    tm = min(ROW_TILE, n)

    def body(o_ref, z_ref, g_ref, y_ref):
        for h in range(N_HEADS):
            cs = slice(h * HEAD_DIM, (h + 1) * HEAD_DIM)
            oh = o_ref[:, cs]
            z = z_ref[:, cs]
            y = oh * _rms_scale(oh) * g_ref[...]
            y_ref[:, cs] = (y * (z * _sigmoid(z))).astype(BF16)

    half = pl.BlockSpec((tm, HALF_W), lambda i: (i, 0))
    return _call(
        body, name=name, grid=(n // tm,),
        in_specs=[half, pl.BlockSpec((tm, HALF_W), lambda i: (i, 5)),
                  pl.BlockSpec((1, HEAD_DIM), lambda i: (0, 0))],
        out_specs=half, out_shape=_sds((n, HALF_W), BF16),
    )(o, proj, dn_norm)


def dn_out_bwd(dy, o, proj, dn_norm, dproj, name):
    n = o.shape[0]
    tm = min(ROW_TILE, n)

    def body(dy_ref, o_ref, z_ref, g_ref, dproj_in, do_ref, dz_ref, dg_ref):
        @pl.when(pl.program_id(0) == 0)
        def _():
            dg_ref[...] = jnp.zeros_like(dg_ref)

        g = g_ref[...]
        dg = jnp.zeros_like(g)
        for h in range(N_HEADS):
            cs = slice(h * HEAD_DIM, (h + 1) * HEAD_DIM)
            oh = o_ref[:, cs]
            z = z_ref[:, cs]
            d = dy_ref[:, cs]
            r = _rms_scale(oh)
            nh = oh * r
            sz = _sigmoid(z)
            dyn = d * (z * sz)
            dz_ref[:, cs] = (d * (nh * g) * (sz * (1.0 + z * (1.0 - sz)))).astype(BF16)
            dg = dg + jnp.sum(dyn * nh, axis=0, keepdims=True)
            dn = dyn * g
            do_ref[:, cs] = r * (dn - nh * jnp.mean(dn * nh, axis=-1, keepdims=True))
        dg_ref[...] += dg

    half = pl.BlockSpec((tm, HALF_W), lambda i: (i, 0))
    vec = pl.BlockSpec((1, HEAD_DIM), lambda i: (0, 0))
    return _call(
        body, name=name, grid=(n // tm,),
        in_specs=[half, half, pl.BlockSpec((tm, HALF_W), lambda i: (i, 5)), vec, _ANY],
        out_specs=[half, pl.BlockSpec((tm, HALF_W), lambda i: (i, 5)), vec],
        out_shape=[_sds((n, HALF_W), F32), _sds(dproj.shape, BF16), _sds((1, HEAD_DIM), F32)],
        input_output_aliases={4: 1},
    )(dy, o, proj, dn_norm, dproj)


def _adamw_math(w, g, m, v):
    m_new = ADAM_B1 * m + (1.0 - ADAM_B1) * g
    v_new = ADAM_B2 * v + (1.0 - ADAM_B2) * (g * g)
    m_hat = m_new / (1.0 - ADAM_B1 ** ADAM_STEP)
    v_hat = v_new / (1.0 - ADAM_B2 ** ADAM_STEP)
    delta = -ADAM_LR * (m_hat / (jnp.sqrt(v_hat) + ADAM_EPS) + ADAM_WD * w)
    return delta, m_new, v_new


def adamw(w, g, m, v, name):
    r, c = w.shape
    tr = r
    for cand in (256, 352):
        if r % cand == 0 and r > cand:
            tr = cand
            break

    def body(w_ref, g_ref, m_ref, v_ref, d_ref, mo_ref, vo_ref):
        d, mn, vn = _adamw_math(w_ref[...], g_ref[...], m_ref[...], v_ref[...])
        d_ref[...] = d
        mo_ref[...] = mn
        vo_ref[...] = vn

    spec = pl.BlockSpec((tr, c), lambda i: (i, 0))
    return _call(
        body, name=name, grid=(r // tr,),
        in_specs=[spec] * 4, out_specs=[spec] * 3, out_shape=[_sds((r, c), F32)] * 3,
    )(w, g, m, v)


def _place():
    return lax.axis_index("x"), lax.axis_index("y"), lax.axis_index("c")


def _other_chips(x, y):
    return [(1 - x, y), (x, 1 - y), (1 - x, 1 - y)]


_ANY = pl.BlockSpec(memory_space=pl.ANY)


def cast_place(w, shard_idx, name):
    r, cols = w.shape
    tr = r // 2

    def body(j_ref, w_ref, o_ref):
        o_ref[0] = w_ref[...].astype(BF16)

    return pl.pallas_call(
        body, name=name,
        grid_spec=pltpu.PrefetchScalarGridSpec(
            num_scalar_prefetch=1, grid=(r // tr,),
            in_specs=[pl.BlockSpec((tr, cols), lambda i, j: (i, 0))],
            out_specs=pl.BlockSpec((1, tr, cols), lambda i, j: (j[0], i, 0))),
        out_shape=_sds((N_SHARD, r, cols), BF16),
        compiler_params=pltpu.CompilerParams(dimension_semantics=("arbitrary",),
                                             vmem_limit_bytes=VMEM_LIMIT),
    )(shard_idx, w)


class Exchange:
    def __init__(self, inputs, out_shape, aliases, sems, phases):
        self.inputs, self.out_shape, self.aliases = list(inputs), list(out_shape), dict(aliases)
        self.sems, self.phases = list(sems), list(phases)


def run_exchange(ex, name):
    def body(*refs):
        n_in, n_out = len(ex.inputs), len(ex.out_shape)
        for _, fn in ex.phases:
            fn(refs[:n_in], refs[n_in:n_in + n_out], refs[n_in + n_out:])

    return _call(body, name=name, in_specs=[_ANY] * len(ex.inputs), out_specs=[_ANY] * len(ex.out_shape),
                 out_shape=ex.out_shape, scratch=ex.sems, input_output_aliases=ex.aliases)(*ex.inputs)


def merge_exchanges(exs):
    inputs, out_shape, sems, aliases, phases, out_slices = [], [], [], {}, [], []
    for ex in exs:
        i0, o0, s0 = len(inputs), len(out_shape), len(sems)
        inputs += ex.inputs
        out_shape += ex.out_shape
        sems += ex.sems
        for k, m in ex.aliases.items():
            aliases[i0 + k] = o0 + m
        si, so, ss = slice(i0, len(inputs)), slice(o0, len(out_shape)), slice(s0, len(sems))
        out_slices.append(so)
        for step, fn in ex.phases:
            phases.append((step, lambda ins, outs, sm, fn=fn, si=si, so=so, ss=ss: fn(ins[si], outs[so], sm[ss])))
    return Exchange(inputs, out_shape, aliases, sems, phases), out_slices


def _dma_sems(*sizes):
    return [pltpu.SemaphoreType.DMA((s,)) for s in sizes]


def gather_exchange(bufs, small=None, relay_step=-2):
    n = len(bufs)
    n_small = 0 if small is None else 1

    def half(outs, a, blk, hc):
        rh = bufs[a].shape[1] // 2
        return outs[a].at[blk, pl.ds(hc * rh, rh), :]

    def ici(outs, sems, a, k, blk, to):
        return pltpu.make_async_remote_copy(
            src_ref=half(outs, a, blk, to[2]), dst_ref=half(outs, a, blk, to[2]), send_sem=sems[0].at[3 * a + k],
            recv_sem=sems[1].at[3 * a + k], device_id=to, device_id_type=MESH)

    def d2d(outs, sems, a, k, blk, hc, to):
        return pltpu.make_async_remote_copy(
            src_ref=half(outs, a, blk, hc), dst_ref=half(outs, a, blk, hc), send_sem=sems[2].at[3 * a + k],
            recv_sem=sems[3].at[3 * a + k], device_id=to, device_id_type=MESH)

    def small_copy(ins, outs, sems, k, blk, to):
        return pltpu.make_async_remote_copy(
            src_ref=ins[n], dst_ref=outs[n].at[blk], send_sem=sems[0].at[3 * n + k],
            recv_sem=sems[1].at[3 * n + k], device_id=to, device_id_type=MESH)

    def start(ins, outs, sems):
        x, y, c = _place()
        j = 2 * x + y
        if n_small:
            pltpu.make_async_copy(ins[n], outs[n].at[j], sems[4].at[0]).start()
        for k, (px, py) in enumerate(_other_chips(x, y)):
            if n_small:
                small_copy(ins, outs, sems, k, j, (px, py, c)).start()
            for a in range(n):
                ici(outs, sems, a, k, j, (px, py, c)).start()

    def relay(ins, outs, sems):
        x, y, c = _place()
        for k, (px, py) in enumerate(_other_chips(x, y)):
            for a in range(n):
                ici(outs, sems, a, k, 2 * px + py, (px, py, c)).wait_recv()
                d2d(outs, sems, a, k, 2 * px + py, c, (x, y, 1 - c)).start()

    def finish(ins, outs, sems):
        x, y, c = _place()
        j = 2 * x + y
        for k, (px, py) in enumerate(_other_chips(x, y)):
            blk = 2 * px + py
            if n_small:
                small_copy(ins, outs, sems, k, blk, (px, py, c)).wait_recv()
                small_copy(ins, outs, sems, k, j, (px, py, c)).wait_send()
            for a in range(n):
                d2d(outs, sems, a, k, blk, 1 - c, (x, y, 1 - c)).wait_recv()
                ici(outs, sems, a, k, j, (px, py, c)).wait_send()
                d2d(outs, sems, a, k, blk, c, (x, y, 1 - c)).wait_send()
        if n_small:
            pltpu.make_async_copy(ins[n], outs[n].at[j], sems[4].at[0]).wait()

    out_shape = [_sds(b.shape, b.dtype) for b in bufs]
    if n_small:
        out_shape.append(_sds((N_SHARD,) + small.shape, small.dtype))
    return Exchange(list(bufs) + ([small] if n_small else []), out_shape, {a: a for a in range(n)},
                    _dma_sems(3 * n + 3, 3 * n + 3, 3 * n, 3 * n, 1),
                    [(0, start), (relay_step, relay), (-1, finish)])


def _start_then_wait(copies):
    def start(ins, outs, sems):
        for sent, _ in copies(ins, outs, sems):
            sent().start()

    def finish(ins, outs, sems):
        pairs = copies(ins, outs, sems)
        for _, arrival in pairs:
            arrival().wait_recv()
        for sent, _ in pairs:
            sent().wait_send()

    return [(0, start), (-1, finish)]


def pair_exchange(arrs):
    n = len(arrs)

    def copies(ins, outs, sems):
        x, y, c = _place()
        res = []
        for a in range(n):
            def mk(a=a):
                rh = arrs[a].shape[1] // 2
                return pltpu.make_async_remote_copy(
                    src_ref=ins[a].at[:, pl.ds((1 - c) * rh, rh), :], dst_ref=outs[a], send_sem=sems[0].at[a],
                    recv_sem=sems[1].at[a], device_id=(x, y, 1 - c), device_id_type=MESH)
            res.append((mk, mk))
        return res

    return Exchange(arrs, [_sds((a.shape[0], a.shape[1] // 2, a.shape[2]), a.dtype) for a in arrs], {},
                    _dma_sems(n, n), _start_then_wait(copies))


def pair_add(g, s, c_idx, name):
    nb, r, cols = g.shape
    rh = r // 2

    def body(c_ref, g_ref, s_ref, o_ref):
        o_ref[...] = (g_ref[...] + s_ref[...]).astype(BF16)

    return pl.pallas_call(
        body, name=name,
        grid_spec=pltpu.PrefetchScalarGridSpec(
            num_scalar_prefetch=1, grid=(nb,),
            in_specs=[pl.BlockSpec((1, rh, cols), lambda j, c: (j, c[0], 0)),
                      pl.BlockSpec((1, rh, cols), lambda j, c: (j, 0, 0))],
            out_specs=pl.BlockSpec((1, rh, cols), lambda j, c: (j, 0, 0))),
        out_shape=_sds((nb, rh, cols), BF16),
        compiler_params=pltpu.CompilerParams(dimension_semantics=("arbitrary",),
                                             vmem_limit_bytes=VMEM_LIMIT),
    )(c_idx, g, s)


def chip_exchange(arrs):
    n = len(arrs)

    def copies(ins, outs, sems):
        x, y, c = _place()
        j = 2 * x + y
        res = []
        for a in range(n):
            for k, (px, py) in enumerate(_other_chips(x, y)):
                def mk(src_blk, dst_blk, a=a, k=k, to=(px, py, c)):
                    return pltpu.make_async_remote_copy(
                        src_ref=ins[a].at[src_blk], dst_ref=outs[a].at[dst_blk], send_sem=sems[0].at[3 * a + k],
                        recv_sem=sems[1].at[3 * a + k], device_id=to, device_id_type=MESH)
                res.append((functools.partial(mk, 2 * px + py, j), functools.partial(mk, j, 2 * px + py)))
        return res

    return Exchange(arrs, [_sds(a.shape, a.dtype) for a in arrs], {}, _dma_sems(3 * n, 3 * n),
                    _start_then_wait(copies))


def sum_chips(r, p, shard_idx, name):
    nb, rh, cols = r.shape
    tr = rh

    def body(j_ref, p_ref, *refs):
        o_ref = refs[nb]
        j = j_ref[0]
        acc = None
        for i in range(nb):
            term = jnp.where(j == i, p_ref[0], refs[i][0]).astype(F32)
            acc = term if acc is None else acc + term
        o_ref[...] = acc

    def slot(i):
        return pl.BlockSpec((1, tr, cols), lambda t, j: (jnp.where(j[0] == i, (i + 1) % nb, i), t, 0))

    return pl.pallas_call(
        body, name=name,
        grid_spec=pltpu.PrefetchScalarGridSpec(
            num_scalar_prefetch=1, grid=(rh // tr,),
            in_specs=[pl.BlockSpec((1, tr, cols), lambda t, j: (j[0], t, 0))] + [slot(i) for i in range(nb)],
            out_specs=pl.BlockSpec((tr, cols), lambda t, j: (t, 0))),
        out_shape=_sds((rh, cols), F32),
        compiler_params=pltpu.CompilerParams(dimension_semantics=("arbitrary",),
                                             vmem_limit_bytes=VMEM_LIMIT),
    )(shard_idx, p, *([r] * nb))


def pair_swap(arrs):
    n = len(arrs)

    def copies(ins, outs, sems):
        x, y, c = _place()
        res = []
        for a in range(n):
            def mk(a=a):
                return pltpu.make_async_remote_copy(
                    src_ref=ins[a], dst_ref=outs[a], send_sem=sems[0].at[a], recv_sem=sems[1].at[a],
                    device_id=(x, y, 1 - c), device_id_type=MESH)
            res.append((mk, mk))
        return res

    return Exchange(arrs, [_sds(a.shape, a.dtype) for a in arrs], {}, _dma_sems(n, n),
                    _start_then_wait(copies))


ADAMW_STEPS_PER_HALF = 4


def adamw_pairs(items, name, comm=None):
    n_items = len(items)
    nh = ADAMW_STEPS_PER_HALF

    def body(*refs):
        ins, outs = refs[:5 * n_items], refs[5 * n_items:]
        mine = (pl.program_id(0) // nh) == lax.axis_index("c")
        for a in range(n_items):
            w_ref, gm_ref, gs_ref, m_ref, v_ref = ins[5 * a:5 * a + 5]
            g_ref, d_ref, mo_ref, vo_ref = outs[4 * a:4 * a + 4]
            g = jnp.where(mine, gm_ref[...], gs_ref[...])
            d, mn, vn = _adamw_math(w_ref[...], g, m_ref[...], v_ref[...])
            g_ref[...] = g
            d_ref[...] = d
            mo_ref[...] = mn
            vo_ref[...] = vn

    in_specs, out_specs, out_shape, args = [], [], [], []
    for w, g_mine, g_sib, m, v in items:
        r, cols = w.shape
        tr = r // (2 * nh)
        full = pl.BlockSpec((tr, cols), lambda i: (i, 0))
        part = pl.BlockSpec((tr, cols), lambda i: (i % nh, 0))
        in_specs += [full, part, part, full, full]
        out_specs += [full] * 4
        out_shape += [_sds((r, cols), F32)] * 4
        args += [w, g_mine, g_sib, m, v]
    res = _call(body, name=name, grid=(2 * nh,), in_specs=in_specs, out_specs=out_specs,
                out_shape=out_shape, comm=comm)(*args)
    own, hosted = (res, None) if comm is None else res
    grouped = [tuple(own[4 * a:4 * a + 4]) for a in range(n_items)]
    return grouped if comm is None else (grouped, hosted)


N_DEV = 8


def device_gather(pack):
    def copies(ins, outs, sems):
        x, y, c = _place()
        me = 4 * x + 2 * y + c
        res = []
        for k in range(1, N_DEV):
            fx, fy, fc = (k >> 2) & 1, (k >> 1) & 1, k & 1
            px, py, pc = (1 - x if fx else x, 1 - y if fy else y, 1 - c if fc else c)

            def mk(slot, k=k, to=(px, py, pc)):
                return pltpu.make_async_remote_copy(
                    src_ref=ins[0], dst_ref=outs[0].at[slot], send_sem=sems[0].at[k - 1],
                    recv_sem=sems[1].at[k - 1], device_id=to, device_id_type=MESH)
            res.append((functools.partial(mk, me), functools.partial(mk, 4 * px + 2 * py + pc)))
        return res

    return Exchange([pack], [_sds((N_DEV,) + pack.shape, pack.dtype)], {}, _dma_sems(N_DEV - 1, N_DEV - 1),
                    _start_then_wait(copies))


def sum_devices(buf, pack, me_idx, name):
    r, cols = pack.shape

    def body(me_ref, p_ref, *refs):
        o_ref = refs[N_DEV]
        acc = None
        for i in range(N_DEV):
            term = jnp.where(me_ref[0] == i, p_ref[...], refs[i][0])
            acc = term if acc is None else acc + term
        o_ref[...] = acc

    def slot(i):
        return pl.BlockSpec((1, r, cols), lambda t, me: (jnp.where(me[0] == i, (i + 1) % N_DEV, i), 0, 0))

    whole = pl.BlockSpec((r, cols), lambda t, me: (0, 0))
    return pl.pallas_call(
        body, name=name,
        grid_spec=pltpu.PrefetchScalarGridSpec(
            num_scalar_prefetch=1, grid=(1,),
            in_specs=[whole] + [slot(i) for i in range(N_DEV)], out_specs=whole),
        out_shape=_sds((r, cols), F32),
        compiler_params=pltpu.CompilerParams(dimension_semantics=("arbitrary",),
                                             vmem_limit_bytes=VMEM_LIMIT),
    )(me_idx, pack, *([buf] * N_DEV))


SMALL_NAMES = ("ffn1_norm", "mix_norm", "ffn2_norm", "final_norm", "sg_ln_g", "sg_ln_b",
               "dn_norm", "a_log", "dt_bias", "sg_b", "sg_w", "conv_w", "loss")


def _to_rows(a):
    flat = a.reshape(-1)
    pad = (-flat.shape[0]) % LANES
    if pad:
        flat = jnp.pad(flat, (0, pad))
    return flat.reshape(-1, LANES)


def _pack_small(parts):
    rows = [_to_rows(parts[k]) for k in SMALL_NAMES]
    pack = jnp.concatenate(rows, axis=0)
    pad = (-pack.shape[0]) % 8
    if pad:
        pack = jnp.pad(pack, ((0, pad), (0, 0)))
    return pack


def _unpack_small(pack, shapes):
    out, r0 = {}, 0
    for k in SMALL_NAMES:
        size = 1
        for s in shapes[k]:
            size *= s
        nrows = -(-size // LANES)
        out[k] = pack[r0:r0 + nrows].reshape(-1)[:size].reshape(shapes[k])
        r0 += nrows
    return out


def kernel(x, ffn1_norm, ffn1_w_gate, ffn1_w_up, ffn1_w_down, mix_norm, w_in, conv_w, a_log, dt_bias, dn_norm, sg_ln_g, sg_ln_b, sg_w, sg_b, w_out, ffn2_norm, ffn2_w_gate, ffn2_w_up, ffn2_w_down, final_norm, loss_target, m_ffn1_norm, m_ffn1_w_gate, m_ffn1_w_up, m_ffn1_w_down, m_mix_norm, m_w_in, m_conv_w, m_a_log, m_dt_bias, m_dn_norm, m_sg_ln_g, m_sg_ln_b, m_sg_w, m_sg_b, m_w_out, m_ffn2_norm, m_ffn2_w_gate, m_ffn2_w_up, m_ffn2_w_down, m_final_norm, v_ffn1_norm, v_ffn1_w_gate, v_ffn1_w_up, v_ffn1_w_down, v_mix_norm, v_w_in, v_conv_w, v_a_log, v_dt_bias, v_dn_norm, v_sg_ln_g, v_sg_ln_b, v_sg_w, v_sg_b, v_w_out, v_ffn2_norm, v_ffn2_w_gate, v_ffn2_w_up, v_ffn2_w_down, v_final_norm):
    bsz, t_len, d = x.shape
    n = bsz * t_len
    xy, yy, cc = _place()
    shard = 2 * xy + yy

    big_names = ["ffn1_w_gate", "ffn1_w_up", "ffn1_w_down", "w_in", "w_out",
                 "ffn2_w_gate", "ffn2_w_up", "ffn2_w_down"]
    big_w = dict(ffn1_w_gate=ffn1_w_gate, ffn1_w_up=ffn1_w_up, ffn1_w_down=ffn1_w_down, w_in=w_in,
                 w_out=w_out, ffn2_w_gate=ffn2_w_gate, ffn2_w_up=ffn2_w_up, ffn2_w_down=ffn2_w_down)
    big_m = dict(ffn1_w_gate=m_ffn1_w_gate, ffn1_w_up=m_ffn1_w_up, ffn1_w_down=m_ffn1_w_down, w_in=m_w_in,
                 w_out=m_w_out, ffn2_w_gate=m_ffn2_w_gate, ffn2_w_up=m_ffn2_w_up, ffn2_w_down=m_ffn2_w_down)
    big_v = dict(ffn1_w_gate=v_ffn1_w_gate, ffn1_w_up=v_ffn1_w_up, ffn1_w_down=v_ffn1_w_down, w_in=v_w_in,
                 w_out=v_w_out, ffn2_w_gate=v_ffn2_w_gate, ffn2_w_up=v_ffn2_w_up, ffn2_w_down=v_ffn2_w_down)
    shard_idx = jnp.reshape(shard, (1,)).astype(jnp.int32)
    c_idx = jnp.reshape(cc, (1,)).astype(jnp.int32)
    transposed = ("ffn1_w_gate", "ffn1_w_up", "ffn2_w_gate", "ffn2_w_up")

    def as2d(a, k):
        return a[0].T if k in transposed else a[0]

    def from2d(a, k):
        return a.T[None] if k in transposed else a[None]

    placed = {k: cast_place(as2d(big_w[k], k), shard_idx, name="cast_" + k) for k in big_names}
    first_names = ["ffn1_w_gate", "ffn1_w_up"]
    second_names = ["ffn1_w_down", "w_in"]
    third_names = ["w_out", "ffn2_w_gate"]
    fourth_names = ["ffn2_w_up", "ffn2_w_down"]
    res = run_exchange(gather_exchange([placed[k] for k in first_names], conv_w[0]), name="gather_first")
    gw = dict(zip(first_names, res[:2]))
    conv_full = res[2].transpose(1, 0, 2).reshape(CONV_K, 3 * HALF_W)

    x0 = x.reshape(n, d)
    def ffn_weights(prefix):
        return [gw[prefix + k].reshape(-1, d) for k in ("_w_gate", "_w_up", "_w_down")]

    def ffn_grad_blocks(grads):
        return [g.reshape(N_SHARD, -1, d) for g in grads]

    (h1, gate1, up1, act1), second = ffn_fwd(
        x0, ffn1_norm, gw["ffn1_w_gate"].reshape(-1, d), gw["ffn1_w_up"].reshape(-1, d), None,
        name="ffn1_fwd", comm=gather_exchange([placed[k] for k in second_names]))
    gw.update(zip(second_names, second))
    (x1,) = ffn_down(x0, act1, gw["ffn1_w_down"].reshape(-1, d), name="ffn1_down")
    w_in_full = gw["w_in"].transpose(1, 0, 2).reshape(d, IN_COLS)
    w_in_full = jnp.pad(w_in_full, ((0, 0), (0, PROJ_W - IN_COLS)))
    (proj, h2), third = in_proj_fwd(x1, mix_norm, w_in_full, name="in_proj_fwd",
                                    comm=gather_exchange([placed[k] for k in third_names]))
    gw.update(zip(third_names, third))
    proj3 = proj.reshape(bsz, t_len, PROJ_W)
    bias_tile = jnp.repeat(sg_b[0].T, SG_GROUP_DIM, axis=1)
    sg_out = sg_fwd(proj, sg_ln_g, sg_ln_b, sg_w[0], bias_tile, name="sg_fwd")
    qkv = dn_conv_fwd(proj3, conv_full, name="dn_conv_fwd")
    alog_row = jnp.zeros((1, LANES), F32).at[0, N_HEADS:2 * N_HEADS].set(a_log[0])
    dtb_row = jnp.zeros((1, LANES), F32).at[0, N_HEADS:2 * N_HEADS].set(dt_bias[0])
    (u_wy, w_wy, q_dec, k_dec, qk, tinv, gc), fourth = dn_chunk_fwd(
        qkv, proj3, alog_row, dtb_row, name="dn_chunk_fwd",
        comm=gather_exchange([placed[k] for k in fourth_names]))
    gw.update(zip(fourth_names, fourth))
    w_out_full = gw["w_out"].reshape(2 * HALF_W, d)
    o, s_in = dn_scan_fwd(u_wy, w_wy, q_dec, k_dec, qk, gc, name="dn_scan_fwd")
    dn_out = dn_out_fwd(o.reshape(n, HALF_W), proj, dn_norm, name="dn_out_fwd")
    x2 = out_proj_fwd(x1, sg_out, dn_out, w_out_full, name="out_proj_fwd")
    x3, h3, gate2, up2, act2 = ffn_fwd(x2, ffn2_norm, *ffn_weights("ffn2"), name="ffn2_fwd")
    dx3, d_final_norm, loss_tile = final_loss(x3, final_norm.reshape(1, d),
                                              loss_target.reshape(n, d), name="final_loss")

    dx2, dgate2, dup2, dyh2, d_ffn2_norm = ffn_bwd_act(
        dx3, x2, ffn2_norm, gate2, up2, *ffn_weights("ffn2"), name="ffn2_bwd_act")
    g_big = {}
    g_big["ffn2_w_gate"], g_big["ffn2_w_up"], g_big["ffn2_w_down"] = ffn_grad_blocks(ffn_bwd_w(
        [dgate2, dup2, act2], [h3, dyh2], [(0, 0), (1, 0), (2, 1)], name="ffn2_bwd_w"))

    early = ["ffn2_w_gate", "ffn2_w_up", "ffn2_w_down"]
    (d_sg, d_dn, dx2b), early_sib = out_proj_bwd_x(dx2, w_out_full, name="out_proj_bwd_x",
                                                   comm=pair_exchange([g_big[k] for k in early]))
    early_sums = [pair_add(g_big[k], s, c_idx, name="grad_pair_add_" + k) for k, s in zip(early, early_sib)]
    g_w_out = matmul_tn([sg_out, dn_out], dx2b, d, name="w_out_grad")
    g_big["w_out"] = g_w_out.reshape(N_SHARD, (2 * HALF_W) // N_SHARD, d)

    d_proj, d_sg_w, d_bias_tile, d_ln_g, d_ln_b = sg_bwd(d_sg, proj, sg_ln_g, sg_ln_b, sg_w[0],
                                                         bias_tile, name="sg_bwd")
    d_o, d_proj, d_dn_norm = dn_out_bwd(d_dn, o.reshape(n, HALF_W), proj, dn_norm, d_proj,
                                        name="dn_out_bwd")
    du, dw, dqd, dkd, dqk, dgc_scan = dn_scan_bwd(d_o.reshape(bsz, t_len, HALF_W), u_wy, w_wy, q_dec,
                                                  k_dec, qk, gc, s_in, name="dn_scan_bwd")
    (d_qkv, d_proj3, d_alog_row, d_dtb_row), early_chips = dn_chunk_bwd(
        qkv, proj3, alog_row, dtb_row, tinv, u_wy, w_wy, du, dw, dqd, dkd, dqk, dgc_scan,
        d_proj.reshape(bsz, t_len, PROJ_W), name="dn_chunk_bwd", comm=chip_exchange(early_sums))
    early_halves = [sum_chips(r, p, shard_idx, name="grad_chip_sum_" + k)
                    for k, r, p in zip(early, early_chips, early_sums)]
    d_proj3, d_conv = dn_conv_bwd(d_qkv, proj3, conv_full, d_proj3, name="dn_conv_bwd")
    d_proj = d_proj3.reshape(n, PROJ_W)
    g_w_in = matmul_tn([h2], d_proj, 3 * MXU_DIM, name="w_in_grad")[:, :IN_COLS]
    g_big["w_in"] = g_w_in.reshape(d, N_SHARD, IN_COLS // N_SHARD).transpose(1, 0, 2)

    def reduce_start(names):
        return pair_exchange([g_big[k] for k in names])

    def reduce_pair_sums(names, from_sib):
        return [pair_add(g_big[k], s, c_idx, name="grad_pair_add_" + k) for k, s in zip(names, from_sib)]

    def reduce_chip_sums(names, from_chips, sums):
        return [sum_chips(r, p, shard_idx, name="grad_chip_sum_" + k)
                for k, r, p in zip(names, from_chips, sums)]

    mid = ["w_in", "w_out"]
    (dx1, d_mix_norm, dyh1), mid_sib = in_proj_bwd_x(d_proj, w_in_full, x1, mix_norm, dx2,
                                                     name="in_proj_bwd_x", comm=reduce_start(mid))
    mid_sums = reduce_pair_sums(mid, mid_sib)
    down = ["ffn1_w_down"]
    (g_down,), mid_chips = ffn_bwd_w([act1], [dyh1], [(0, 0)], name="ffn1_bwd_w_down",
                                     comm=chip_exchange(mid_sums))
    g_big["ffn1_w_down"] = g_down.reshape(N_SHARD, -1, d)
    mid_halves = reduce_chip_sums(mid, mid_chips, mid_sums)
    leg, legs = merge_exchanges([reduce_start(down), pair_swap(mid_halves), pair_swap(early_halves)])
    leg_res = run_exchange(leg, name="grad_pair_exchange_down")
    down_sums = reduce_pair_sums(down, leg_res[legs[0]])
    mid_sib_halves, early_sib_halves = leg_res[legs[1]], leg_res[legs[2]]

    dx0, dgate1, dup1, _, d_ffn1_norm = ffn_bwd_act(
        dx1, x0, ffn1_norm, gate1, up1, *ffn_weights("ffn1"), name="ffn1_bwd_act")
    grad_x = dx0.reshape(bsz, t_len, d)
    d_sg_b = d_bias_tile.reshape(SG_CHUNK, SG_GROUPS, SG_GROUP_DIM).sum(axis=-1).T
    small_g = dict(ffn1_norm=d_ffn1_norm, mix_norm=d_mix_norm, ffn2_norm=d_ffn2_norm,
                   final_norm=d_final_norm, sg_ln_g=d_ln_g, sg_ln_b=d_ln_b, dn_norm=d_dn_norm,
                   a_log=d_alog_row[:, N_HEADS:2 * N_HEADS], dt_bias=d_dtb_row[:, N_HEADS:2 * N_HEADS],
                   sg_b=d_sg_b, sg_w=d_sg_w, conv_w=d_conv, loss=loss_tile[:, :1])
    my_pack = _pack_small(small_g)
    hosted, parts = merge_exchanges([chip_exchange(down_sums), device_gather(my_pack)])
    late = ["ffn1_w_gate", "ffn1_w_up"]
    late_grads, hosted_res = ffn_bwd_w([dgate1, dup1], [h1], [(0, 0), (1, 0)], name="ffn1_bwd_w_gate_up",
                                       comm=hosted)
    g_big["ffn1_w_gate"], g_big["ffn1_w_up"] = ffn_grad_blocks(late_grads)
    down_halves = reduce_chip_sums(down, hosted_res[parts[0]], down_sums)
    (all_packs,) = hosted_res[parts[1]]

    leg, legs = merge_exchanges([reduce_start(late), pair_swap(down_halves)])
    leg_res = run_exchange(leg, name="grad_pair_exchange")
    pair_sums = reduce_pair_sums(late, leg_res[legs[0]])
    down_sib_halves = leg_res[legs[1]]

    def adam_items(names, mine, sib):
        return [(as2d(big_w[k], k), gm, gs, as2d(big_m[k], k), as2d(big_v[k], k))
                for k, gm, gs in zip(names, mine, sib)]

    outs = {}
    done = adamw_pairs(
        adam_items(early + mid + down, early_halves + mid_halves + down_halves,
                   list(early_sib_halves) + list(mid_sib_halves) + list(down_sib_halves)),
        name="adamw_early")
    from_chips = run_exchange(chip_exchange(pair_sums), name="grad_chip_exchange")
    halves = reduce_chip_sums(late, from_chips, pair_sums)
    sib_halves = run_exchange(pair_swap(halves), name="grad_pair_swap")
    done += adamw_pairs(adam_items(late, halves, sib_halves), name="adamw_late")
    for k, res in zip(early + mid + down + late, done):
        outs[k] = tuple(from2d(a, k) for a in res)

    small_w = dict(ffn1_norm=ffn1_norm, mix_norm=mix_norm, ffn2_norm=ffn2_norm, final_norm=final_norm,
                   sg_ln_g=sg_ln_g, sg_ln_b=sg_ln_b, dn_norm=dn_norm, a_log=a_log, dt_bias=dt_bias,
                   sg_b=sg_b, sg_w=sg_w)
    small_m = dict(ffn1_norm=m_ffn1_norm, mix_norm=m_mix_norm, ffn2_norm=m_ffn2_norm,
                   final_norm=m_final_norm, sg_ln_g=m_sg_ln_g, sg_ln_b=m_sg_ln_b, dn_norm=m_dn_norm,
                   a_log=m_a_log, dt_bias=m_dt_bias, sg_b=m_sg_b, sg_w=m_sg_w)
    small_v = dict(ffn1_norm=v_ffn1_norm, mix_norm=v_mix_norm, ffn2_norm=v_ffn2_norm,
                   final_norm=v_final_norm, sg_ln_g=v_sg_ln_g, sg_ln_b=v_sg_ln_b, dn_norm=v_dn_norm,
                   a_log=v_a_log, dt_bias=v_dt_bias, sg_b=v_sg_b, sg_w=v_sg_w)
    shapes = {k: small_w[k].shape for k in small_w}
    shapes["conv_w"] = (CONV_K, 3 * HALF_W)
    shapes["loss"] = (1, 1)
    me_idx = jnp.reshape(4 * xy + 2 * yy + cc, (1,)).astype(jnp.int32)
    g_pack = sum_devices(all_packs, my_pack, me_idx, name="small_sum")
    g_small = _unpack_small(g_pack, shapes)
    loss = g_small["loss"].reshape(())
    cw = 3 * HALF_W // N_SHARD
    g_conv = lax.dynamic_slice_in_dim(g_small["conv_w"], shard * cw, cw, axis=1)
    zero_conv = jnp.zeros((CONV_K, 3 * HALF_W), F32)

    def packed(src, conv):
        parts = dict(src)
        parts["conv_w"] = lax.dynamic_update_slice_in_dim(zero_conv, conv[0], shard * cw, axis=1)
        parts["loss"] = jnp.zeros((1, 1), F32)
        return _pack_small(parts)

    d_pack, m_pack, v_pack = adamw(packed(small_w, conv_w), g_pack, packed(small_m, m_conv_w),
                                   packed(small_v, v_conv_w), name="adamw_small")
    d_small = _unpack_small(d_pack, shapes)
    m_small = _unpack_small(m_pack, shapes)
    v_small = _unpack_small(v_pack, shapes)

    def conv_block(full_arr):
        return lax.dynamic_slice_in_dim(full_arr, shard * cw, cw, axis=1)[None]

    for k in small_w:
        outs[k] = (g_small[k].reshape(small_w[k].shape), d_small[k], m_small[k], v_small[k])
    outs["conv_w"] = (g_conv[None], conv_block(d_small["conv_w"]), conv_block(m_small["conv_w"]),
                      conv_block(v_small["conv_w"]))

    order = ["ffn1_norm", "ffn1_w_gate", "ffn1_w_up", "ffn1_w_down", "mix_norm", "w_in", "conv_w",
             "a_log", "dt_bias", "dn_norm", "sg_ln_g", "sg_ln_b", "sg_w", "sg_b", "w_out", "ffn2_norm",
             "ffn2_w_gate", "ffn2_w_up", "ffn2_w_down", "final_norm"]
    return (loss, grad_x, *[outs[k][0] for k in order], *[outs[k][1] for k in order],
            *[outs[k][2] for k in order], *[outs[k][3] for k in order])
```

```python
import functools

import jax
import jax.numpy as jnp
from jax import lax
from jax.experimental import pallas as pl
from jax.experimental.pallas import tpu as pltpu

F32 = jnp.float32
BF16 = jnp.bfloat16
EPS = 1e-6

D_MODEL = 1024
N_SHARD = 4
HEAD_DIM = 128
N_HEADS = 4
DN_CHUNK = 64
SG_CHUNK = 128
SG_GROUPS = 8
SG_GROUP_DIM = 64
HALF_W = 512
PROJ_W = 3200
IN_COLS = 3080
GATE_COL_BLOCK = 24
QK_SCALE = HEAD_DIM ** -0.5
LANES = 128

ADAM_LR = 0.001
ADAM_B1 = 0.9
ADAM_B2 = 0.999
ADAM_EPS = 1e-08
ADAM_WD = 0.01
ADAM_STEP = 10

VMEM_LIMIT = 56 * 1024 * 1024
ROW_TILE = 512

NN = ((1,), (0,))
NT = ((1,), (1,))
TN = ((0,), (0,))
MESH = pl.DeviceIdType.MESH


def _dot(a, b, dims):
    return lax.dot_general(a, b, (dims, ((), ())), preferred_element_type=F32)


def _bdot(a, b, dims):
    return _dot(a.astype(BF16), b.astype(BF16), dims)


def _split(a):
    hi = a.astype(BF16)
    lo = (a - hi.astype(F32)).astype(BF16)
    return hi, lo


def _dot3(a, b, dims=NN):
    return _dot(a[0], b[0], dims) + (_dot(a[0], b[1], dims) + _dot(a[1], b[0], dims))


def _dot_exact_lhs(a, b):
    ab = a.astype(BF16)
    b1 = b.astype(BF16)
    r1 = b - b1.astype(F32)
    b2 = r1.astype(BF16)
    b3 = (r1 - b2.astype(F32)).astype(BF16)
    return _dot(ab, b1, NN) + (_dot(ab, b2, NN) + _dot(ab, b3, NN))


def _call(body, *, name, out_shape, in_specs, out_specs, grid=(), scratch=(), comm=None, **kw):
    params = dict(vmem_limit_bytes=VMEM_LIMIT)
    if grid:
        params["dimension_semantics"] = ("arbitrary",) * len(grid)
    if comm is None:
        return pl.pallas_call(
            body, name=name, grid=grid, in_specs=in_specs, out_specs=out_specs,
            out_shape=out_shape, scratch_shapes=list(scratch),
            compiler_params=pltpu.CompilerParams(**params), **kw)

    n_in, n_out, n_sc = len(in_specs), len(out_specs), len(scratch)
    c_in, c_out = len(comm.inputs), len(comm.out_shape)
    steps = 1
    for g in grid:
        steps *= g

    def hosted(*refs):
        ins, cins = refs[:n_in], refs[n_in:n_in + c_in]
        o0 = n_in + c_in
        outs, couts = refs[o0:o0 + n_out], refs[o0 + n_out:o0 + n_out + c_out]
        s0 = o0 + n_out + c_out
        sc, csems = refs[s0:s0 + n_sc], refs[s0 + n_sc:]
        lin = 0
        for axis, g in enumerate(grid):
            lin = lin * g + pl.program_id(axis)

        def at(step, fn):
            @pl.when(lin == step % steps)
            def _():
                fn(cins, couts, csems)

        for step, fn in comm.phases:
            if step >= 0:
                at(step, fn)
        body(*ins, *outs, *sc)
        for step, fn in comm.phases:
            if step < 0:
                at(step, fn)

    aliases = dict(kw.pop("input_output_aliases", {}))
    for k, m in comm.aliases.items():
        aliases[n_in + k] = n_out + m
    call = pl.pallas_call(
        hosted, name=name, grid=grid, in_specs=list(in_specs) + [_ANY] * c_in,
        out_specs=list(out_specs) + [_ANY] * c_out, out_shape=list(out_shape) + comm.out_shape,
        scratch_shapes=list(scratch) + comm.sems, input_output_aliases=aliases,
        compiler_params=pltpu.CompilerParams(**params), **kw)

    def run(*args):
        res = call(*args, *comm.inputs)
        return res[:n_out], res[n_out:]

    return run


def _sds(shape, dtype):
    return jax.ShapeDtypeStruct(tuple(shape), dtype)


def _resident(shape):
    zeros = (0,) * len(shape)
    return pl.BlockSpec(tuple(shape), lambda *_: zeros, pipeline_mode=pl.Buffered(1))


def _sigmoid(x):
    return jax.nn.sigmoid(x)


def _softplus(x):
    return jnp.maximum(x, 0.0) + jnp.log(1.0 + jnp.exp(-jnp.abs(x)))


_GELU_C = 0.7978845608028654
_GELU_A = 0.044715


def _gelu_tanh(x):
    return jnp.tanh(_GELU_C * (x + _GELU_A * x * x * x))


def _gelu(x, t):
    return 0.5 * x * (1.0 + t)


def _gelu_grad(x, t):
    return 0.5 * (1.0 + t) + 0.5 * x * (1.0 - t * t) * _GELU_C * (1.0 + 3.0 * _GELU_A * x * x)


def _silu_grad(x):
    s = _sigmoid(x)
    return s * (1.0 + x * (1.0 - s))


def _rms_scale(xv):
    return lax.rsqrt(jnp.mean(xv * xv, axis=-1, keepdims=True) + EPS)


def _rms_bwd(dh, xv, g):
    r = _rms_scale(xv)
    xn = xv * r
    dg = jnp.sum(dh * xn, axis=0, keepdims=True)
    dxn = dh * g
    dx = r * (dxn - xn * jnp.mean(dxn * xn, axis=-1, keepdims=True))
    return dx, dg


def _iota2(shape, dim):
    return lax.broadcasted_iota(jnp.int32, shape, dim)


def _col_to_row(col):
    n = col.shape[0]
    eye = _iota2((n, n), 0) == _iota2((n, n), 1)
    return jnp.sum(jnp.where(eye, col, 0.0), axis=0, keepdims=True)


def _row_to_col(row):
    n = row.shape[1]
    eye = _iota2((n, n), 0) == _iota2((n, n), 1)
    return jnp.sum(jnp.where(eye, row, 0.0), axis=1, keepdims=True)


MXU_DIM = 256


def _hidden_chunks(f, step=3 * MXU_DIM):
    return [(c0, min(c0 + step, f)) for c0 in range(0, f, step)]

def ffn_fwd(x, gnorm, wg, wu, wd, name, comm=None):
    n, d = x.shape
    f = wg.shape[0]
    tm = min(ROW_TILE, n)
    fused = wd is not None

    def body(x_ref, g_ref, wg_ref, wu_ref, *rest):
        if fused:
            wd_ref, xo_ref, h_ref, gate_ref, up_ref, act_ref, acc_ref = rest
        else:
            h_ref, gate_ref, up_ref, act_ref = rest
        xv = x_ref[...]
        h = (xv * _rms_scale(xv) * g_ref[...]).astype(BF16)
        h_ref[...] = h
        chunks = _hidden_chunks(f)

        def gate_up(c0, c1):
            return _dot(h, wg_ref[c0:c1, :], NT), _dot(h, wu_ref[c0:c1, :], NT)

        nxt = gate_up(*chunks[0])
        for idx, (c0, c1) in enumerate(chunks):
            gate, up = nxt
            if idx + 1 < len(chunks):
                nxt = gate_up(*chunks[idx + 1])
            act = (gate * _sigmoid(gate) * up).astype(BF16)
            gate_ref[:, c0:c1] = gate.astype(BF16)
            up_ref[:, c0:c1] = up.astype(BF16)
            act_ref[:, c0:c1] = act
            if fused:
                part = _dot(act, wd_ref[c0:c1, :], NN)
                if c0 == 0:
                    acc_ref[...] = part
                else:
                    acc_ref[...] += part
        if fused:
            xo_ref[...] = xv + 0.5 * acc_ref[...]

    row = pl.BlockSpec((tm, d), lambda i: (i, 0))
    wide = pl.BlockSpec((tm, f), lambda i: (i, 0))
    n_w = 3 if fused else 2
    return _call(
        body, name=name, grid=(n // tm,),
        in_specs=[row, pl.BlockSpec((1, d), lambda i: (0, 0))] + [_resident((f, d))] * n_w,
        out_specs=([row] if fused else []) + [row, wide, wide, wide],
        out_shape=([_sds((n, d), F32)] if fused else []) + [_sds((n, d), BF16)] + [_sds((n, f), BF16)] * 3,
        scratch=[pltpu.VMEM((tm, d), F32)] if fused else [], comm=comm,
    )(*([x, gnorm, wg, wu] + ([wd] if fused else [])))


def ffn_down(x, act, wd, name, comm=None):
    n, d = x.shape
    f = wd.shape[0]
    tm = min(ROW_TILE, n)

    def body(x_ref, a_ref, w_ref, o_ref):
        o_ref[...] = x_ref[...] + 0.5 * _dot(a_ref[...], w_ref[...], NN)

    row = pl.BlockSpec((tm, d), lambda i: (i, 0))
    return _call(
        body, name=name, grid=(n // tm,),
        in_specs=[row, pl.BlockSpec((tm, f), lambda i: (i, 0)), _resident((f, d))],
        out_specs=[row], out_shape=[_sds((n, d), F32)], comm=comm,
    )(x, act, wd)


def ffn_bwd_act(dy, x, gnorm, gate, up, wg, wu, wd, name, comm=None):
    n, d = x.shape
    f = wg.shape[0]
    tm = min(ROW_TILE // 2, n)

    def body(dy_ref, x_ref, g_ref, gate_ref, up_ref, wg_ref, wu_ref, wd_ref,
             dx_ref, dgate_ref, dup_ref, dyh_ref, dg_ref, acc_ref):
        @pl.when(pl.program_id(0) == 0)
        def _():
            dg_ref[...] = jnp.zeros_like(dg_ref)

        dyh = (0.5 * dy_ref[...]).astype(BF16)
        dyh_ref[...] = dyh
        chunks = _hidden_chunks(f, 2 * MXU_DIM)
        next_dact = _dot(dyh, wd_ref[chunks[0][0]:chunks[0][1], :], NT)
        for idx, (c0, c1) in enumerate(chunks):
            dact = next_dact
            if idx + 1 < len(chunks):
                n0, n1 = chunks[idx + 1]
                next_dact = _dot(dyh, wd_ref[n0:n1, :], NT)
            gt = gate_ref[:, c0:c1].astype(F32)
            u = up_ref[:, c0:c1].astype(F32)
            s = _sigmoid(gt)
            dup = (dact * (gt * s)).astype(BF16)
            dgate = (dact * u * (s * (1.0 + gt * (1.0 - s)))).astype(BF16)
            dup_ref[:, c0:c1] = dup
            dgate_ref[:, c0:c1] = dgate
            part = _dot(dgate, wg_ref[c0:c1, :], NN) + _dot(dup, wu_ref[c0:c1, :], NN)
            if c0 == 0:
                acc_ref[...] = part
            else:
                acc_ref[...] += part
        dxn, dg = _rms_bwd(acc_ref[...], x_ref[...], g_ref[...])
        dx_ref[...] = dy_ref[...] + dxn
        dg_ref[...] += dg

    row = pl.BlockSpec((tm, d), lambda i: (i, 0))
    wide = pl.BlockSpec((tm, f), lambda i: (i, 0))
    vec = pl.BlockSpec((1, d), lambda i: (0, 0))
    wres = _resident((f, d))
    return _call(
        body, name=name, grid=(n // tm,),
        in_specs=[row, row, vec, wide, wide, wres, wres, wres],
        out_specs=[row, wide, wide, row, vec],
        out_shape=[_sds((n, d), F32), _sds((n, f), BF16), _sds((n, f), BF16),
                   _sds((n, d), BF16), _sds((1, d), F32)],
        scratch=[pltpu.VMEM((tm, d), F32)], comm=comm,
    )(dy, x, gnorm, gate, up, wg, wu, wd)


def ffn_bwd_w(wide, rows, pairs, name, comm=None):
    n, d = rows[0].shape
    f = wide[0].shape[1]
    fh = f // 2
    tk = min(ROW_TILE, n)
    n_w, n_r = len(wide), len(rows)

    def body(*refs):
        wide_refs, row_refs, outs = refs[:n_w], refs[n_w:n_w + n_r], refs[n_w + n_r:]

        @pl.when(pl.program_id(1) == 0)
        def _():
            for o_ref in outs:
                o_ref[...] = jnp.zeros_like(o_ref)

        row_vals = [r[...] for r in row_refs]
        for c0, c1 in _hidden_chunks(fh, 2 * MXU_DIM):
            for (i, k), o_ref in zip(pairs, outs):
                o_ref[c0:c1, :] += _dot(wide_refs[i][:, c0:c1], row_vals[k], TN)

    row = pl.BlockSpec((tk, d), lambda j, k: (k, 0))
    blk = pl.BlockSpec((tk, fh), lambda j, k: (k, j))
    return _call(
        body, name=name, grid=(2, n // tk),
        in_specs=[blk] * n_w + [row] * n_r,
        out_specs=[pl.BlockSpec((fh, d), lambda j, k: (j, 0))] * len(pairs),
        out_shape=[_sds((f, d), F32)] * len(pairs), comm=comm,
    )(*wide, *rows)


def final_loss(x, gnorm, target, name):
    n, d = x.shape
    tm = min(ROW_TILE, n)

    def body(x_ref, g_ref, t_ref, dx_ref, dg_ref, loss_ref):
        @pl.when(pl.program_id(0) == 0)
        def _():
            dg_ref[...] = jnp.zeros_like(dg_ref)
            loss_ref[...] = jnp.zeros_like(loss_ref)

        xv = x_ref[...]
        y = xv * _rms_scale(xv) * g_ref[...]
        err = y - t_ref[...]
        part = 0.5 * jnp.sum(jnp.mean(err * err, axis=-1, keepdims=True), axis=0, keepdims=True)
        loss_ref[...] += jnp.broadcast_to(part, loss_ref.shape)
        dx, dg = _rms_bwd(err * (1.0 / d), xv, g_ref[...])
        dx_ref[...] = dx
        dg_ref[...] += dg

    row = pl.BlockSpec((tm, d), lambda i: (i, 0))
    vec = pl.BlockSpec((1, d), lambda i: (0, 0))
    return _call(
        body, name=name, grid=(n // tm,),
        in_specs=[row, vec, row],
        out_specs=[row, vec, pl.BlockSpec((1, LANES), lambda i: (0, 0))],
        out_shape=[_sds((n, d), F32), _sds((1, d), F32), _sds((1, LANES), F32)],
    )(x, gnorm, target)


def in_proj_fwd(x, gnorm, w, name, comm=None):
    n, d = x.shape
    cols = w.shape[1]
    tm = min(2 * ROW_TILE, n)

    def body(x_ref, g_ref, w_ref, p_ref, h_ref):
        xv = x_ref[...]
        h = (xv * _rms_scale(xv) * g_ref[...]).astype(BF16)
        h_ref[...] = h
        for c0, c1 in _hidden_chunks(cols):
            p_ref[:, c0:c1] = _dot(h, w_ref[:, c0:c1], NN)

    return _call(
        body, name=name, grid=(n // tm,),
        in_specs=[pl.BlockSpec((tm, d), lambda i: (i, 0)),
                  pl.BlockSpec((1, d), lambda i: (0, 0)), _resident((d, cols))],
        out_specs=[pl.BlockSpec((tm, cols), lambda i: (i, 0)),
                   pl.BlockSpec((tm, d), lambda i: (i, 0))],
        out_shape=[_sds((n, cols), F32), _sds((n, d), BF16)], comm=comm,
    )(x, gnorm, w)


def in_proj_bwd_x(dproj, w, x, gnorm, dres, name, comm=None):
    n, d = x.shape
    cols = w.shape[1]
    tm = min(ROW_TILE, n)

    def body(dp_ref, w_ref, x_ref, g_ref, dr_ref, dx_ref, dg_ref, dxh_ref):
        @pl.when(pl.program_id(0) == 0)
        def _():
            dg_ref[...] = jnp.zeros_like(dg_ref)

        dh = _dot(dp_ref[...], w_ref[...], NT)
        dxn, dg = _rms_bwd(dh, x_ref[...], g_ref[...])
        dx = dr_ref[...] + dxn
        dx_ref[...] = dx
        dxh_ref[...] = (0.5 * dx).astype(BF16)
        dg_ref[...] += dg

    row = pl.BlockSpec((tm, d), lambda i: (i, 0))
    vec = pl.BlockSpec((1, d), lambda i: (0, 0))
    return _call(
        body, name=name, grid=(n // tm,),
        in_specs=[pl.BlockSpec((tm, cols), lambda i: (i, 0)),
                  _resident((d, cols)), row, vec, row],
        out_specs=[row, vec, row],
        out_shape=[_sds((n, d), F32), _sds((1, d), F32), _sds((n, d), BF16)], comm=comm,
    )(dproj, w, x, gnorm, dres)


def matmul_tn(a_list, b, tn, name):
    n, cb = b.shape
    widths = [a.shape[1] for a in a_list]
    tk = min(ROW_TILE, n)

    def body(*refs):
        a_refs, b_ref, o_ref = refs[:-2], refs[-2], refs[-1]

        @pl.when(pl.program_id(0) == 0)
        def _():
            o_ref[...] = jnp.zeros_like(o_ref)

        r0 = 0
        for a_ref, ka in zip(a_refs, widths):
            av = a_ref[...]
            for c0, c1 in _hidden_chunks(cb, tn):
                o_ref[r0:r0 + ka, c0:c1] += _dot(av, b_ref[:, c0:c1], TN)
            r0 += ka

    return _call(
        body, name=name, grid=(n // tk,),
        in_specs=[pl.BlockSpec((tk, ka), lambda k: (k, 0)) for ka in widths]
        + [pl.BlockSpec((tk, cb), lambda k: (k, 0))],
        out_specs=pl.BlockSpec((sum(widths), cb), lambda k: (0, 0)),
        out_shape=_sds((sum(widths), cb), F32),
    )(*a_list, b)


def out_proj_fwd(x, sg_out, dn_out, w, name):
    n, d = x.shape
    tm = min(ROW_TILE, n)

    def body(x_ref, a_ref, b_ref, w_ref, o_ref):
        o_ref[...] = (x_ref[...] + _dot(a_ref[...], w_ref[0:HALF_W, :], NN)
                      + _dot(b_ref[...], w_ref[HALF_W:2 * HALF_W, :], NN))

    row = pl.BlockSpec((tm, d), lambda i: (i, 0))
    half = pl.BlockSpec((tm, HALF_W), lambda i: (i, 0))
    return _call(
        body, name=name, grid=(n // tm,),
        in_specs=[row, half, half, pl.BlockSpec((2 * HALF_W, d), lambda i: (0, 0))],
        out_specs=row, out_shape=_sds((n, d), F32),
    )(x, sg_out, dn_out, w)


def out_proj_bwd_x(dy, w, name, comm=None):
    n, d = dy.shape
    tm = min(ROW_TILE, n)

    def body(dy_ref, w_ref, dsg_ref, ddn_ref, dyb_ref):
        dyb = dy_ref[...].astype(BF16)
        dyb_ref[...] = dyb
        dsg_ref[...] = _dot(dyb, w_ref[0:HALF_W, :], NT)
        ddn_ref[...] = _dot(dyb, w_ref[HALF_W:2 * HALF_W, :], NT)

    row = pl.BlockSpec((tm, d), lambda i: (i, 0))
    half = pl.BlockSpec((tm, HALF_W), lambda i: (i, 0))
    return _call(
        body, name=name, grid=(n // tm,),
        in_specs=[row, pl.BlockSpec((2 * HALF_W, d), lambda i: (0, 0))],
        out_specs=[half, half, row],
        out_shape=[_sds((n, HALF_W), F32), _sds((n, HALF_W), F32), _sds((n, d), BF16)], comm=comm,
    )(dy, w)


SG_PAIRS = SG_GROUPS // 2


def _sg_low_half():
    return _iota2((SG_CHUNK, LANES), 1) < SG_GROUP_DIM


def _sg_pair_cols(p):
    return slice(p * LANES, (p + 1) * LANES)


def _sg_causal():
    return _iota2((SG_CHUNK, SG_CHUNK), 0) >= _iota2((SG_CHUNK, SG_CHUNK), 1)


def _sg_forward_chunk(pu, pv, ln_g, ln_b, wc, bias, low):
    tu, tv = _gelu_tanh(pu), _gelu_tanh(pv)
    u = _gelu(pu, tu)
    v = _gelu(pv, tv)
    mu = jnp.mean(v, axis=-1, keepdims=True)
    vc = v - mu
    rs = lax.rsqrt(jnp.mean(vc * vc, axis=-1, keepdims=True) + EPS)
    xhat = vc * rs
    vn = (xhat * ln_g + ln_b).astype(BF16)
    parts = []
    for p in range(SG_PAIRS):
        vn_p = vn[:, _sg_pair_cols(p)]
        parts.append(jnp.where(low, _dot(wc[2 * p], vn_p, NN), _dot(wc[2 * p + 1], vn_p, NN)))
    vs = bias + jnp.concatenate(parts, axis=1)
    return u, xhat, rs, vn, vs, tu, tv


def sg_fwd(proj, ln_g, ln_b, w_s, bias_tile, name):
    n = proj.shape[0]
    tm = min(ROW_TILE, n)

    def body(pu_ref, pv_ref, g_ref, b_ref, w_ref, bias_ref, o_ref):
        causal = _sg_causal()
        wc = [jnp.where(causal, w_ref[g], 0.0).astype(BF16) for g in range(SG_GROUPS)]
        masks = _sg_low_half()
        for ci in range(tm // SG_CHUNK):
            rows = slice(ci * SG_CHUNK, (ci + 1) * SG_CHUNK)
            u, _, _, _, vs, _, _ = _sg_forward_chunk(pu_ref[rows, :], pv_ref[rows, :], g_ref[...],
                                                     b_ref[...], wc, bias_ref[...], masks)
            o_ref[rows, :] = (u * vs).astype(BF16)

    vec = pl.BlockSpec((1, HALF_W), lambda i: (0, 0))
    return _call(
        body, name=name, grid=(n // tm,),
        in_specs=[pl.BlockSpec((tm, HALF_W), lambda i: (i, 0)),
                  pl.BlockSpec((tm, HALF_W), lambda i: (i, 1)), vec, vec,
                  pl.BlockSpec((SG_GROUPS, SG_CHUNK, SG_CHUNK), lambda i: (0, 0, 0)),
                  pl.BlockSpec((SG_CHUNK, HALF_W), lambda i: (0, 0))],
        out_specs=pl.BlockSpec((tm, HALF_W), lambda i: (i, 0)),
        out_shape=_sds((n, HALF_W), BF16),
    )(proj, proj, ln_g, ln_b, w_s, bias_tile)


def sg_bwd(dsg, proj, ln_g, ln_b, w_s, bias_tile, name):
    n = proj.shape[0]
    tm = min(ROW_TILE, n)

    def body(d_ref, pu_ref, pv_ref, g_ref, b_ref, w_ref, bias_ref,
             dp_ref, dw_ref, db_ref, dlg_ref, dlb_ref):
        @pl.when(pl.program_id(0) == 0)
        def _():
            dw_ref[...] = jnp.zeros_like(dw_ref)
            db_ref[...] = jnp.zeros_like(db_ref)
            dlg_ref[...] = jnp.zeros_like(dlg_ref)
            dlb_ref[...] = jnp.zeros_like(dlb_ref)

        causal = _sg_causal()
        wc = [jnp.where(causal, w_ref[g], 0.0).astype(BF16) for g in range(SG_GROUPS)]
        masks = _sg_low_half()
        ln_g_v = g_ref[...]
        for ci in range(tm // SG_CHUNK):
            rows = slice(ci * SG_CHUNK, (ci + 1) * SG_CHUNK)
            pu = pu_ref[rows, :]
            pv = pv_ref[rows, :]
            u, xhat, rs, vn, vs, tu, tv = _sg_forward_chunk(pu, pv, ln_g_v, b_ref[...], wc,
                                                            bias_ref[...], masks)
            dout = d_ref[rows, :]
            dp_ref[rows, 0:HALF_W] = (dout * vs * _gelu_grad(pu, tu)).astype(BF16)
            dvs = dout * u
            dvs_b = dvs.astype(BF16)
            db_ref[...] += dvs
            dvn_parts = []
            for p in range(SG_PAIRS):
                dvs_p = dvs_b[:, _sg_pair_cols(p)]
                vn_p = vn[:, _sg_pair_cols(p)]
                dvn_parts.append(jnp.where(masks, _dot(wc[2 * p], dvs_p, TN), _dot(wc[2 * p + 1], dvs_p, TN)))
                zero = jnp.zeros_like(dvs_p)
                dw_ref[2 * p] += jnp.where(causal, _dot(jnp.where(masks, dvs_p, zero), vn_p, NT), 0.0)
                dw_ref[2 * p + 1] += jnp.where(causal, _dot(jnp.where(masks, zero, dvs_p), vn_p, NT), 0.0)
            dvn = jnp.concatenate(dvn_parts, axis=1)
            dlg_ref[...] += jnp.sum(dvn * xhat, axis=0, keepdims=True)
            dlb_ref[...] += jnp.sum(dvn, axis=0, keepdims=True)
            dxh = dvn * ln_g_v
            dv = rs * (dxh - jnp.mean(dxh, axis=-1, keepdims=True)
                       - xhat * jnp.mean(dxh * xhat, axis=-1, keepdims=True))
            dp_ref[rows, HALF_W:2 * HALF_W] = (dv * _gelu_grad(pv, tv)).astype(BF16)

    vec = pl.BlockSpec((1, HALF_W), lambda i: (0, 0))
    wspec = pl.BlockSpec((SG_GROUPS, SG_CHUNK, SG_CHUNK), lambda i: (0, 0, 0))
    tile = pl.BlockSpec((SG_CHUNK, HALF_W), lambda i: (0, 0))
    return _call(
        body, name=name, grid=(n // tm,),
        in_specs=[pl.BlockSpec((tm, HALF_W), lambda i: (i, 0)),
                  pl.BlockSpec((tm, HALF_W), lambda i: (i, 0)),
                  pl.BlockSpec((tm, HALF_W), lambda i: (i, 1)), vec, vec, wspec, tile],
        out_specs=[pl.BlockSpec((tm, 2 * HALF_W), lambda i: (i, 0)), wspec, tile, vec, vec],
        out_shape=[_sds((n, PROJ_W), BF16), _sds((SG_GROUPS, SG_CHUNK, SG_CHUNK), F32),
                   _sds((SG_CHUNK, HALF_W), F32), _sds((1, HALF_W), F32), _sds((1, HALF_W), F32)],
    )(dsg, proj, proj, ln_g, ln_b, w_s, bias_tile)


CONV_K = 4
CONV_BLOCK = 256


def _shift_down(x, s, row):
    if s == 0:
        return x
    return jnp.where(row >= s, pltpu.roll(x, s, 0), 0.0)


def _shift_up(x, s, row):
    if s == 0:
        return x
    t_len = x.shape[0]
    return jnp.where(row < t_len - s, pltpu.roll(x, t_len - s, 0), 0.0)


def _conv_taps(x, row):
    return [_shift_down(x, CONV_K - 1 - j, row) for j in range(CONV_K)]


def _conv(taps, w):
    y = taps[0] * w[0:1, :]
    for j in range(1, CONV_K):
        y = y + taps[j] * w[j:j + 1, :]
    return y


def dn_conv_fwd(proj3, conv_w, name):
    b, t, _ = proj3.shape
    nblk = 3 * HALF_W // CONV_BLOCK
    first = 2 * HALF_W // CONV_BLOCK
    n_norm = 2 * HALF_W // CONV_BLOCK

    def body(x_ref, w_ref, o_ref):
        s = pl.program_id(1)
        x = x_ref[0]
        y = _conv(_conv_taps(x, _iota2(x.shape, 0)), w_ref[...])
        y = y * _sigmoid(y)

        @pl.when(s < n_norm)
        def _():
            for h in range(CONV_BLOCK // HEAD_DIM):
                cs = slice(h * HEAD_DIM, (h + 1) * HEAD_DIM)
                yh = y[:, cs]
                o_ref[0, :, cs] = yh * lax.rsqrt(jnp.sum(yh * yh, axis=-1, keepdims=True) + EPS)

        @pl.when(s >= n_norm)
        def _():
            o_ref[0] = y

    return _call(
        body, name=name, grid=(b, nblk),
        in_specs=[pl.BlockSpec((1, t, CONV_BLOCK), lambda i, s: (i, 0, first + s)),
                  pl.BlockSpec((CONV_K, CONV_BLOCK), lambda i, s: (0, s))],
        out_specs=pl.BlockSpec((1, t, CONV_BLOCK), lambda i, s: (i, 0, s)),
        out_shape=_sds((b, t, 3 * HALF_W), F32),
    )(proj3, conv_w)


def dn_conv_bwd(dqkv, proj3, conv_w, dproj3, name, comm=None):
    b, t, _ = proj3.shape
    nblk = 3 * HALF_W // CONV_BLOCK
    first = 2 * HALF_W // CONV_BLOCK
    n_norm = 2 * HALF_W // CONV_BLOCK

    def body(d_ref, x_ref, w_ref, dproj_in, dx_ref, dw_ref, ds_ref):
        s = pl.program_id(0)

        @pl.when(pl.program_id(1) == 0)
        def _():
            dw_ref[...] = jnp.zeros_like(dw_ref)

        x = x_ref[0]
        w = w_ref[...]
        row = _iota2(x.shape, 0)
        taps = _conv_taps(x, row)
        c = _conv(taps, w)
        sg = _sigmoid(c)
        y = c * sg

        @pl.when(s < n_norm)
        def _():
            for h in range(CONV_BLOCK // HEAD_DIM):
                cs = slice(h * HEAD_DIM, (h + 1) * HEAD_DIM)
                yh = y[:, cs]
                r = lax.rsqrt(jnp.sum(yh * yh, axis=-1, keepdims=True) + EPS)
                nh = yh * r
                dn = d_ref[0, :, cs]
                ds_ref[:, cs] = r * (dn - nh * jnp.sum(dn * nh, axis=-1, keepdims=True))

        @pl.when(s >= n_norm)
        def _():
            ds_ref[...] = d_ref[0]

        dc = ds_ref[...] * (sg * (1.0 + c * (1.0 - sg)))
        dx = _shift_up(dc, CONV_K - 1, row) * w[0:1, :]
        for j in range(1, CONV_K):
            dx = dx + _shift_up(dc, CONV_K - 1 - j, row) * w[j:j + 1, :]
        dx_ref[0] = dx.astype(BF16)
        for j in range(CONV_K):
            dw_ref[j:j + 1, :] += jnp.sum(dc * taps[j], axis=0, keepdims=True)

    return _call(
        body, name=name, grid=(nblk, b),
        in_specs=[pl.BlockSpec((1, t, CONV_BLOCK), lambda s, i: (i, 0, s)),
                  pl.BlockSpec((1, t, CONV_BLOCK), lambda s, i: (i, 0, first + s)),
                  pl.BlockSpec((CONV_K, CONV_BLOCK), lambda s, i: (0, s)), _ANY],
        out_specs=[pl.BlockSpec((1, t, CONV_BLOCK), lambda s, i: (i, 0, first + s)),
                   pl.BlockSpec((CONV_K, CONV_BLOCK), lambda s, i: (0, s))],
        out_shape=[_sds(dproj3.shape, BF16), _sds((CONV_K, 3 * HALF_W), F32)],
        scratch=[pltpu.VMEM((t, CONV_BLOCK), F32)],
        input_output_aliases={3: 0}, comm=comm,
    )(dqkv, proj3, conv_w, dproj3)


def _chunk_masks():
    ii = _iota2((DN_CHUNK, DN_CHUNK), 0)
    jj = _iota2((DN_CHUNK, DN_CHUNK), 1)
    return ii >= jj, ii > jj, ii == jj


LOCKSTEP_CHUNKS = 4


def _inv_unit_lower_many(l_mats, eye):
    eye_f = jnp.where(eye, 1.0, 0.0)
    ps = [-l for l in l_mats]
    ts = [eye_f + p for p in ps]
    pss = [_split(p) for p in ps]
    size = 2
    while size < DN_CHUNK:
        ps = [_dot3(s, s) for s in pss]
        pss = [_split(p) for p in ps]
        ts = [t + _dot3(_split(t), s) for t, s in zip(ts, pss)]
        size *= 2
    return ts


def _gates(pba, ea_row, dtb_row):
    beta = _sigmoid(pba)
    g = -ea_row * _softplus(pba + dtb_row)
    return beta, g


def _chunk_decay(gcol):
    incl, strict, eye = _chunk_masks()
    grow = jnp.sum(jnp.where(eye, gcol, 0.0), axis=0, keepdims=True)
    decay = jnp.where(incl, jnp.exp(jnp.where(incl, gcol - grow, 0.0)), 0.0)
    return decay, incl, strict, eye


def dn_chunk_fwd(qkv, proj3, alog_row, dtb_row, name, comm=None):
    b, t, _ = qkv.shape
    rblk = min(512, t)
    n_in = rblk // DN_CHUNK

    def body(q_ref, k_ref, v_ref, pba_ref, al_ref, dtb_ref,
             u_ref, w_ref, qd_ref, kd_ref, qk_ref, ti_ref, gc_ref):
        ea = jnp.exp(al_ref[...])
        tri = jnp.where(_chunk_masks()[0], 1.0, 0.0)

        _, strict, eye = _chunk_masks()

        def chunk_group(cg, carry):
            items = []
            for sub in range(LOCKSTEP_CHUNKS):
                rows = pl.ds(pl.multiple_of((cg * LOCKSTEP_CHUNKS + sub) * DN_CHUNK, DN_CHUNK), DN_CHUNK)
                beta_all, g_all = _gates(pba_ref[0, rows, :], ea, dtb_ref[...])
                gc = _dot_exact_lhs(tri, g_all)
                gc_ref[0, rows, :] = gc
                for h in range(N_HEADS):
                    items.append((rows, h, beta_all[:, h:h + 1], gc[:, N_HEADS + h:N_HEADS + h + 1]))
            ks, kbs, decays, egs = [], [], [], []
            for rows, h, beta, gcol in items:
                cs = slice(h * HEAD_DIM, (h + 1) * HEAD_DIM)
                k = k_ref[0, rows, cs]
                ks.append(k)
                kbs.append(k * beta)
                decays.append(_chunk_decay(gcol)[0])
                egs.append(jnp.exp(gcol))
            ms = [_bdot(kb, k, NT) for kb, k in zip(kbs, ks)]
            tinvs = _inv_unit_lower_many([jnp.where(strict, m * dc, 0.0) for m, dc in zip(ms, decays)], eye)
            tsps = [_split(t) for t in tinvs]
            for (rows, h, beta, gcol), tsp, tinv in zip(items, tsps, tinvs):
                cs = slice(h * HEAD_DIM, (h + 1) * HEAD_DIM)
                u_ref[0, rows, cs] = _dot3(tsp, _split(v_ref[0, rows, cs] * beta))
                ti_ref[0, h, rows, :] = tinv
            for (rows, h, beta, gcol), tsp, kb, eg in zip(items, tsps, kbs, egs):
                cs = slice(h * HEAD_DIM, (h + 1) * HEAD_DIM)
                w_ref[0, rows, cs] = _dot3(tsp, _split(kb * eg))
            for (rows, h, beta, gcol), k, dc, eg in zip(items, ks, decays, egs):
                cs = slice(h * HEAD_DIM, (h + 1) * HEAD_DIM)
                q = q_ref[0, rows, cs] * QK_SCALE
                qk_ref[0, h, rows, :] = _bdot(q, k, NT) * dc
                qd_ref[0, rows, cs] = q * eg
                kd_ref[0, rows, cs] = k * jnp.exp(gcol[DN_CHUNK - 1:DN_CHUNK, :] - gcol)
            return carry

        lax.fori_loop(0, n_in // LOCKSTEP_CHUNKS, chunk_group, 0)

    def seg(cblk):
        return pl.BlockSpec((1, rblk, HALF_W), lambda i, r: (i, r, cblk))

    vec = pl.BlockSpec((1, LANES), lambda i, r: (0, 0))
    wide = pl.BlockSpec((1, rblk, HALF_W), lambda i, r: (i, r, 0))
    sq = pl.BlockSpec((1, N_HEADS, rblk, DN_CHUNK), lambda i, r: (i, 0, r, 0))
    return _call(
        body, name=name, grid=(b, t // rblk),
        in_specs=[seg(0), seg(1), seg(2),
                  pl.BlockSpec((1, rblk, LANES), lambda i, r: (i, r, GATE_COL_BLOCK)), vec, vec],
        out_specs=[wide, wide, wide, wide, sq, sq,
                   pl.BlockSpec((1, rblk, LANES), lambda i, r: (i, r, 0))],
        out_shape=[_sds((b, t, HALF_W), F32)] * 4
        + [_sds((b, N_HEADS, t, DN_CHUNK), F32)] * 2 + [_sds((b, t, LANES), F32)], comm=comm,
    )(qkv, qkv, qkv, proj3, alog_row, dtb_row)


def dn_scan_fwd(u, w, qd, kd, qk, gc, name):
    b, t, _ = u.shape
    nc = t // DN_CHUNK
    bh = b * N_HEADS

    def body(u_ref, w_ref, qd_ref, kd_ref, qk_ref, gc_ref, o_ref, sin_ref, s_ref):
        @pl.when(pl.program_id(0) == 0)
        def _():
            s_ref[...] = jnp.zeros_like(s_ref)

        items = [(bi, h, slice(h * HEAD_DIM, (h + 1) * HEAD_DIM)) for bi in range(b) for h in range(N_HEADS)]
        sbs = []
        for bi, h, cs in items:
            s = s_ref[bi * N_HEADS + h]
            sin_ref[0, bi * N_HEADS + h] = s
            sbs.append(s.astype(BF16))
        ws = [_bdot(w_ref[bi, :, cs], sb, NN) for (bi, h, cs), sb in zip(items, sbs)]
        qs = [_bdot(qd_ref[bi, :, cs], sb, NN) for (bi, h, cs), sb in zip(items, sbs)]
        vbs = [(u_ref[bi, :, cs] - wsi).astype(BF16) for (bi, h, cs), wsi in zip(items, ws)]
        for (bi, h, cs), qsi, vb in zip(items, qs, vbs):
            o_ref[bi, :, cs] = qsi + _bdot(qk_ref[bi, h], vb, NN)
        for (bi, h, cs), vb in zip(items, vbs):
            gl = jnp.exp(gc_ref[bi, DN_CHUNK - 1:DN_CHUNK, N_HEADS + h:N_HEADS + h + 1])
            idx = bi * N_HEADS + h
            s_ref[idx] = s_ref[idx] * gl + _bdot(kd_ref[bi, :, cs], vb, TN)

    wide = pl.BlockSpec((b, DN_CHUNK, HALF_W), lambda c: (0, c, 0))
    return _call(
        body, name=name, grid=(nc,),
        in_specs=[wide, wide, wide, wide,
                  pl.BlockSpec((b, N_HEADS, DN_CHUNK, DN_CHUNK), lambda c: (0, 0, c, 0)),
                  pl.BlockSpec((b, DN_CHUNK, LANES), lambda c: (0, c, 0))],
        out_specs=[wide, pl.BlockSpec((1, bh, HEAD_DIM, HEAD_DIM), lambda c: (c, 0, 0, 0))],
        out_shape=[_sds((b, t, HALF_W), F32), _sds((nc, bh, HEAD_DIM, HEAD_DIM), F32)],
        scratch=[pltpu.VMEM((bh, HEAD_DIM, HEAD_DIM), F32)],
    )(u, w, qd, kd, qk, gc)


def dn_scan_bwd(do, u, w, qd, kd, qk, gc, s_in, name):
    b, t, _ = u.shape
    nc = t // DN_CHUNK
    bh = b * N_HEADS

    def body(do_ref, u_ref, w_ref, qd_ref, kd_ref, qk_ref, gc_ref, sin_ref,
             du_ref, dw_ref, dqd_ref, dkd_ref, dqk_ref, dgc_ref, ds_ref):
        @pl.when(pl.program_id(0) == 0)
        def _():
            ds_ref[...] = jnp.zeros_like(ds_ref)

        last_row = _iota2((DN_CHUNK, LANES), 0) == DN_CHUNK - 1
        lane = _iota2((DN_CHUNK, LANES), 1)
        items = [(bi, h, slice(h * HEAD_DIM, (h + 1) * HEAD_DIM)) for bi in range(b) for h in range(N_HEADS)]
        sbs = [sin_ref[0, bi * N_HEADS + h].astype(BF16) for bi, h, cs in items]
        wvs = [w_ref[bi, :, cs].astype(BF16) for bi, h, cs in items]
        dovs = [do_ref[bi, :, cs].astype(BF16) for bi, h, cs in items]
        dsbs = [ds_ref[bi * N_HEADS + h].astype(BF16) for bi, h, cs in items]
        vbs = [(u_ref[bi, :, cs] - _dot(wv, sb, NN)).astype(BF16)
               for (bi, h, cs), wv, sb in zip(items, wvs, sbs)]
        for (bi, h, cs), dov, sb in zip(items, dovs, sbs):
            dqd_ref[bi, :, cs] = _dot(dov, sb, NT)
        dvns = [_dot(kd_ref[bi, :, cs].astype(BF16), dsb, NN) + _dot(qk_ref[bi, h].astype(BF16), dov, TN)
                for (bi, h, cs), dsb, dov in zip(items, dsbs, dovs)]
        for (bi, h, cs), vb, dsb, dov in zip(items, vbs, dsbs, dovs):
            dkd_ref[bi, :, cs] = _dot(vb, dsb, NT)
            dqk_ref[bi, h] = _dot(dov, vb, NT)
        dgls = []
        for (bi, h, cs), dvn, sb, wv, dov in zip(items, dvns, sbs, wvs, dovs):
            idx = bi * N_HEADS + h
            du_ref[bi, :, cs] = dvn
            dvn_b = dvn.astype(BF16)
            dw_ref[bi, :, cs] = -_dot(dvn_b, sb, NT)
            gl = jnp.exp(gc_ref[bi, DN_CHUNK - 1:DN_CHUNK, N_HEADS + h:N_HEADS + h + 1])
            ds = ds_ref[idx]
            dgl = jnp.sum(jnp.sum(ds * sin_ref[0, idx], axis=1, keepdims=True), axis=0, keepdims=True)
            dgls.append(dgl * gl)
            ds_ref[idx] = (ds * gl + _dot(qd_ref[bi, :, cs].astype(BF16), dov, TN)
                           - _dot(wv, dvn_b, TN))
        for bi in range(b):
            dgc = jnp.zeros((DN_CHUNK, LANES), F32)
            for h in range(N_HEADS):
                dgc = dgc + jnp.where(jnp.logical_and(last_row, lane == N_HEADS + h),
                                      dgls[bi * N_HEADS + h], 0.0)
            dgc_ref[bi] = dgc

    def rev(c):
        return nc - 1 - c

    wide = pl.BlockSpec((b, DN_CHUNK, HALF_W), lambda c: (0, rev(c), 0))
    sq = pl.BlockSpec((b, N_HEADS, DN_CHUNK, DN_CHUNK), lambda c: (0, 0, rev(c), 0))
    gates = pl.BlockSpec((b, DN_CHUNK, LANES), lambda c: (0, rev(c), 0))
    return _call(
        body, name=name, grid=(nc,),
        in_specs=[wide, wide, wide, wide, wide, sq, gates,
                  pl.BlockSpec((1, bh, HEAD_DIM, HEAD_DIM), lambda c: (rev(c), 0, 0, 0))],
        out_specs=[wide, wide, wide, wide, sq, gates],
        out_shape=[_sds((b, t, HALF_W), F32)] * 4
        + [_sds((b, N_HEADS, t, DN_CHUNK), F32), _sds((b, t, LANES), F32)],
        scratch=[pltpu.VMEM((bh, HEAD_DIM, HEAD_DIM), F32)],
    )(do, u, w, qd, kd, qk, gc, s_in)


def dn_chunk_bwd(qkv, proj3, alog_row, dtb_row, tinv, u, w, du, dw, dqd, dkd, dqk, dgc_scan, dproj3, name,
                 comm=None):
    b, t, _ = qkv.shape
    rblk = min(512, t)
    n_in = rblk // DN_CHUNK

    def body(q_ref, k_ref, v_ref, pba_ref, al_ref, dtb_ref, ti_ref, u_ref, w_ref,
             du_ref, dw_ref, dqd_ref, dkd_ref, dqk_ref, dgs_ref, dproj_in,
             dq_ref, dpba_ref, dal_ref, ddtb_ref):
        @pl.when(jnp.logical_and(pl.program_id(0) == 0, pl.program_id(1) == 0))
        def _():
            dal_ref[...] = jnp.zeros_like(dal_ref)
            ddtb_ref[...] = jnp.zeros_like(ddtb_ref)

        ea = jnp.exp(al_ref[...])
        incl0 = _chunk_masks()[0]
        tri = jnp.where(incl0, 1.0, 0.0)
        tri_up = jnp.where(_iota2((DN_CHUNK, DN_CHUNK), 1) >= _iota2((DN_CHUNK, DN_CHUNK), 0), 1.0, 0.0)
        lane = _iota2((DN_CHUNK, LANES), 1)
        last_col = _iota2((DN_CHUNK, 1), 0) == DN_CHUNK - 1

        _, strict, _ = _chunk_masks()
        gate_lane = jnp.logical_and(lane >= N_HEADS, lane < 2 * N_HEADS)

        def chunk_group(cg, carry):
            tiles, items = [], []
            for sub in range(LOCKSTEP_CHUNKS):
                rows = pl.ds(pl.multiple_of((cg * LOCKSTEP_CHUNKS + sub) * DN_CHUNK, DN_CHUNK), DN_CHUNK)
                pba = pba_ref[0, rows, :]
                beta_all, g_all = _gates(pba, ea, dtb_ref[...])
                gc = _dot_exact_lhs(tri, g_all)
                tiles.append((rows, pba, beta_all, g_all))
                for h in range(N_HEADS):
                    items.append((sub, rows, h, slice(h * HEAD_DIM, (h + 1) * HEAD_DIM),
                                  beta_all[:, h:h + 1], gc[:, N_HEADS + h:N_HEADS + h + 1]))
            decays = [_chunk_decay(gcol)[0] for _, _, _, _, _, gcol in items]
            egs = [jnp.exp(gcol) for _, _, _, _, _, gcol in items]
            qbs = [(q_ref[0, rows, cs] * QK_SCALE).astype(BF16) for _, rows, h, cs, _, _ in items]
            kfs = [k_ref[0, rows, cs].astype(BF16) for _, rows, h, cs, _, _ in items]
            kbs = [k_ref[0, rows, cs] * beta for _, rows, h, cs, beta, _ in items]
            kbbs = [kb.astype(BF16) for kb in kbs]
            tsps = [_split(ti_ref[0, h, rows, :]) for _, rows, h, cs, _, _ in items]
            drus = [_dot3(tsp, _split(du_ref[0, rows, cs]), TN)
                    for (_, rows, h, cs, _, _), tsp in zip(items, tsps)]
            drws = [_dot3(tsp, _split(dw_ref[0, rows, cs]), TN)
                    for (_, rows, h, cs, _, _), tsp in zip(items, tsps)]
            m_kks = [_dot(kbb, kf, NT) for kbb, kf in zip(kbbs, kfs)]
            a_qks = [_dot(qb, kf, NT) for qb, kf in zip(qbs, kfs)]
            dls = [-jnp.where(strict, _dot3(_split(dru), _split(u_ref[0, rows, cs]), NT)
                              + _dot3(_split(drw), _split(w_ref[0, rows, cs]), NT), 0.0)
                   for (_, rows, h, cs, _, _), dru, drw in zip(items, drus, drws)]
            dms = [(dl * dc).astype(BF16) for dl, dc in zip(dls, decays)]
            das = [(dqk_ref[0, h, rows, :] * dc).astype(BF16)
                   for (_, rows, h, cs, _, _), dc in zip(items, decays)]
            dkb_mm = [_dot(dm, kf, NN) for dm, kf in zip(dms, kfs)]
            dk_mm = [_dot(dm, kbb, TN) + _dot(da, qb, TN) for dm, kbb, da, qb in zip(dms, kbbs, das, qbs)]
            dqs_mm = [_dot(da, kf, NN) for da, kf in zip(das, kfs)]
            dgc_tiles = [dgs_ref[0, rows, :] for rows, _, _, _ in tiles]
            dbeta_tiles = [jnp.zeros((DN_CHUNK, LANES), F32) for _ in tiles]
            for n_it, (sub, rows, h, cs, beta, gcol) in enumerate(items):
                eg, dc = egs[n_it], decays[n_it]
                k = k_ref[0, rows, cs]
                q = q_ref[0, rows, cs] * QK_SCALE
                kb, dru, drw = kbs[n_it], drus[n_it], drws[n_it]
                ek = jnp.exp(gcol[DN_CHUNK - 1:DN_CHUNK, :] - gcol)
                e_mat = (dls[n_it] * m_kks[n_it] + dqk_ref[0, h, rows, :] * a_qks[n_it]) * dc
                dkb = drw * eg + dkb_mm[n_it]
                dqd = dqd_ref[0, rows, cs]
                dkd = dkd_ref[0, rows, cs]
                kdk = dkd * k * ek
                kdk_total = jnp.sum(jnp.sum(kdk, axis=0, keepdims=True), axis=1, keepdims=True)
                dg = (jnp.sum(drw * kb * eg + dqd * q * eg - kdk, axis=-1, keepdims=True)
                      + jnp.sum(e_mat, axis=1, keepdims=True)
                      - _row_to_col(jnp.sum(e_mat, axis=0, keepdims=True))
                      + jnp.where(last_col, kdk_total, 0.0))
                dbeta = jnp.sum(dkb * k + dru * v_ref[0, rows, cs], axis=-1, keepdims=True)
                dq_ref[0, rows, cs] = (dqs_mm[n_it] + dqd * eg) * QK_SCALE
                dq_ref[0, rows, pl.ds(HALF_W + h * HEAD_DIM, HEAD_DIM)] = dk_mm[n_it] + dkd * ek + dkb * beta
                dq_ref[0, rows, pl.ds(2 * HALF_W + h * HEAD_DIM, HEAD_DIM)] = dru * beta
                dgc_tiles[sub] = dgc_tiles[sub] + jnp.where(lane == N_HEADS + h, dg, 0.0)
                dbeta_tiles[sub] = dbeta_tiles[sub] + jnp.where(lane == h, dbeta, 0.0)
            for (rows, pba, beta_all, g_all), dgc_tile, dbeta_tile in zip(tiles, dgc_tiles, dbeta_tiles):
                dg_tile = _dot_exact_lhs(tri_up, dgc_tile)
                da_pre = dg_tile * (-ea) * _sigmoid(pba + dtb_ref[...])
                dal_ref[...] += jnp.sum(jnp.where(gate_lane, dg_tile * g_all, 0.0), axis=0, keepdims=True)
                ddtb_ref[...] += jnp.sum(jnp.where(gate_lane, da_pre, 0.0), axis=0, keepdims=True)
                dpba_ref[0, rows, :] = jnp.where(lane < N_HEADS, dbeta_tile * beta_all * (1.0 - beta_all),
                                                 jnp.where(gate_lane, da_pre, 0.0)).astype(BF16)
            return carry

        lax.fori_loop(0, n_in // LOCKSTEP_CHUNKS, chunk_group, 0)

    def seg(cblk):
        return pl.BlockSpec((1, rblk, HALF_W), lambda i, r: (i, r, cblk))

    vec = pl.BlockSpec((1, LANES), lambda i, r: (0, 0))
    wide = pl.BlockSpec((1, rblk, HALF_W), lambda i, r: (i, r, 0))
    sq = pl.BlockSpec((1, N_HEADS, rblk, DN_CHUNK), lambda i, r: (i, 0, r, 0))
    gates = pl.BlockSpec((1, rblk, LANES), lambda i, r: (i, r, 0))
    return _call(
        body, name=name, grid=(b, t // rblk),
        in_specs=[seg(0), seg(1), seg(2),
                  pl.BlockSpec((1, rblk, LANES), lambda i, r: (i, r, GATE_COL_BLOCK)), vec, vec,
                  sq, wide, wide, wide, wide, wide, wide, sq, gates, _ANY],
        out_specs=[pl.BlockSpec((1, rblk, 3 * HALF_W), lambda i, r: (i, r, 0)),
                   pl.BlockSpec((1, rblk, LANES), lambda i, r: (i, r, GATE_COL_BLOCK)), vec, vec],
        out_shape=[_sds((b, t, 3 * HALF_W), F32), _sds(dproj3.shape, BF16),
                   _sds((1, LANES), F32), _sds((1, LANES), F32)],
        input_output_aliases={15: 1}, comm=comm,
    )(qkv, qkv, qkv, proj3, alog_row, dtb_row, tinv, u, w, du, dw, dqd, dkd, dqk, dgc_scan, dproj3)


def dn_out_fwd(o, proj, dn_norm, name):
    n = o.shape[0]
    tm = min(ROW_TILE, n)

    def body(o_ref, z_ref, g_ref, y_ref):
        for h in range(N_HEADS):
            cs = slice(h * HEAD_DIM, (h + 1) * HEAD_DIM)
            oh = o_ref[:, cs]
            z = z_ref[:, cs]
            y = oh * _rms_scale(oh) * g_ref[...]
            y_ref[:, cs] = (y * (z * _sigmoid(z))).astype(BF16)

    half = pl.BlockSpec((tm, HALF_W), lambda i: (i, 0))
    return _call(
        body, name=name, grid=(n // tm,),
        in_specs=[half, pl.BlockSpec((tm, HALF_W), lambda i: (i, 5)),
                  pl.BlockSpec((1, HEAD_DIM), lambda i: (0, 0))],
        out_specs=half, out_shape=_sds((n, HALF_W), BF16),
    )(o, proj, dn_norm)


def dn_out_bwd(dy, o, proj, dn_norm, dproj, name):
    n = o.shape[0]
    tm = min(ROW_TILE, n)

    def body(dy_ref, o_ref, z_ref, g_ref, dproj_in, do_ref, dz_ref, dg_ref):
        @pl.when(pl.program_id(0) == 0)
        def _():
            dg_ref[...] = jnp.zeros_like(dg_ref)

        g = g_ref[...]
        dg = jnp.zeros_like(g)
        for h in range(N_HEADS):
            cs = slice(h * HEAD_DIM, (h + 1) * HEAD_DIM)
            oh = o_ref[:, cs]
            z = z_ref[:, cs]
            d = dy_ref[:, cs]
            r = _rms_scale(oh)
            nh = oh * r
            sz = _sigmoid(z)
            dyn = d * (z * sz)
            dz_ref[:, cs] = (d * (nh * g) * (sz * (1.0 + z * (1.0 - sz)))).astype(BF16)
            dg = dg + jnp.sum(dyn * nh, axis=0, keepdims=True)
            dn = dyn * g
            do_ref[:, cs] = r * (dn - nh * jnp.mean(dn * nh, axis=-1, keepdims=True))
        dg_ref[...] += dg

    half = pl.BlockSpec((tm, HALF_W), lambda i: (i, 0))
    vec = pl.BlockSpec((1, HEAD_DIM), lambda i: (0, 0))
    return _call(
        body, name=name, grid=(n // tm,),
        in_specs=[half, half, pl.BlockSpec((tm, HALF_W), lambda i: (i, 5)), vec, _ANY],
        out_specs=[half, pl.BlockSpec((tm, HALF_W), lambda i: (i, 5)), vec],
        out_shape=[_sds((n, HALF_W), F32), _sds(dproj.shape, BF16), _sds((1, HEAD_DIM), F32)],
        input_output_aliases={4: 1},
    )(dy, o, proj, dn_norm, dproj)


def _adamw_math(w, g, m, v):
    m_new = ADAM_B1 * m + (1.0 - ADAM_B1) * g
    v_new = ADAM_B2 * v + (1.0 - ADAM_B2) * (g * g)
    m_hat = m_new / (1.0 - ADAM_B1 ** ADAM_STEP)
    v_hat = v_new / (1.0 - ADAM_B2 ** ADAM_STEP)
    delta = -ADAM_LR * (m_hat / (jnp.sqrt(v_hat) + ADAM_EPS) + ADAM_WD * w)
    return delta, m_new, v_new


def adamw(w, g, m, v, name):
    r, c = w.shape
    tr = r
    for cand in (256, 352):
        if r % cand == 0 and r > cand:
            tr = cand
            break

    def body(w_ref, g_ref, m_ref, v_ref, d_ref, mo_ref, vo_ref):
        d, mn, vn = _adamw_math(w_ref[...], g_ref[...], m_ref[...], v_ref[...])
        d_ref[...] = d
        mo_ref[...] = mn
        vo_ref[...] = vn

    spec = pl.BlockSpec((tr, c), lambda i: (i, 0))
    return _call(
        body, name=name, grid=(r // tr,),
        in_specs=[spec] * 4, out_specs=[spec] * 3, out_shape=[_sds((r, c), F32)] * 3,
    )(w, g, m, v)


def _place():
    return lax.axis_index("x"), lax.axis_index("y"), lax.axis_index("c")


def _other_chips(x, y):
    return [(1 - x, y), (x, 1 - y), (1 - x, 1 - y)]


_ANY = pl.BlockSpec(memory_space=pl.ANY)


def cast_place(w, shard_idx, name):
    r, cols = w.shape
    tr = r // 2

    def body(j_ref, w_ref, o_ref):
        o_ref[0] = w_ref[...].astype(BF16)

    return pl.pallas_call(
        body, name=name,
        grid_spec=pltpu.PrefetchScalarGridSpec(
            num_scalar_prefetch=1, grid=(r // tr,),
            in_specs=[pl.BlockSpec((tr, cols), lambda i, j: (i, 0))],
            out_specs=pl.BlockSpec((1, tr, cols), lambda i, j: (j[0], i, 0))),
        out_shape=_sds((N_SHARD, r, cols), BF16),
        compiler_params=pltpu.CompilerParams(dimension_semantics=("arbitrary",),
                                             vmem_limit_bytes=VMEM_LIMIT),
    )(shard_idx, w)


class Exchange:
    def __init__(self, inputs, out_shape, aliases, sems, phases):
        self.inputs, self.out_shape, self.aliases = list(inputs), list(out_shape), dict(aliases)
        self.sems, self.phases = list(sems), list(phases)


def run_exchange(ex, name):
    def body(*refs):
        n_in, n_out = len(ex.inputs), len(ex.out_shape)
        for _, fn in ex.phases:
            fn(refs[:n_in], refs[n_in:n_in + n_out], refs[n_in + n_out:])

    return _call(body, name=name, in_specs=[_ANY] * len(ex.inputs), out_specs=[_ANY] * len(ex.out_shape),
                 out_shape=ex.out_shape, scratch=ex.sems, input_output_aliases=ex.aliases)(*ex.inputs)


def merge_exchanges(exs):
    inputs, out_shape, sems, aliases, phases, out_slices = [], [], [], {}, [], []
    for ex in exs:
        i0, o0, s0 = len(inputs), len(out_shape), len(sems)
        inputs += ex.inputs
        out_shape += ex.out_shape
        sems += ex.sems
        for k, m in ex.aliases.items():
            aliases[i0 + k] = o0 + m
        si, so, ss = slice(i0, len(inputs)), slice(o0, len(out_shape)), slice(s0, len(sems))
        out_slices.append(so)
        for step, fn in ex.phases:
            phases.append((step, lambda ins, outs, sm, fn=fn, si=si, so=so, ss=ss: fn(ins[si], outs[so], sm[ss])))
    return Exchange(inputs, out_shape, aliases, sems, phases), out_slices


def _dma_sems(*sizes):
    return [pltpu.SemaphoreType.DMA((s,)) for s in sizes]


def gather_exchange(bufs, small=None, relay_step=-2):
    n = len(bufs)
    n_small = 0 if small is None else 1

    def half(outs, a, blk, hc):
        rh = bufs[a].shape[1] // 2
        return outs[a].at[blk, pl.ds(hc * rh, rh), :]

    def ici(outs, sems, a, k, blk, to):
        return pltpu.make_async_remote_copy(
            src_ref=half(outs, a, blk, to[2]), dst_ref=half(outs, a, blk, to[2]), send_sem=sems[0].at[3 * a + k],
            recv_sem=sems[1].at[3 * a + k], device_id=to, device_id_type=MESH)

    def d2d(outs, sems, a, k, blk, hc, to):
        return pltpu.make_async_remote_copy(
            src_ref=half(outs, a, blk, hc), dst_ref=half(outs, a, blk, hc), send_sem=sems[2].at[3 * a + k],
            recv_sem=sems[3].at[3 * a + k], device_id=to, device_id_type=MESH)

    def small_copy(ins, outs, sems, k, blk, to):
        return pltpu.make_async_remote_copy(
            src_ref=ins[n], dst_ref=outs[n].at[blk], send_sem=sems[0].at[3 * n + k],
            recv_sem=sems[1].at[3 * n + k], device_id=to, device_id_type=MESH)

    def start(ins, outs, sems):
        x, y, c = _place()
        j = 2 * x + y
        if n_small:
            pltpu.make_async_copy(ins[n], outs[n].at[j], sems[4].at[0]).start()
        for k, (px, py) in enumerate(_other_chips(x, y)):
            if n_small:
                small_copy(ins, outs, sems, k, j, (px, py, c)).start()
            for a in range(n):
                ici(outs, sems, a, k, j, (px, py, c)).start()

    def relay(ins, outs, sems):
        x, y, c = _place()
        for k, (px, py) in enumerate(_other_chips(x, y)):
            for a in range(n):
                ici(outs, sems, a, k, 2 * px + py, (px, py, c)).wait_recv()
                d2d(outs, sems, a, k, 2 * px + py, c, (x, y, 1 - c)).start()

    def finish(ins, outs, sems):
        x, y, c = _place()
        j = 2 * x + y
        for k, (px, py) in enumerate(_other_chips(x, y)):
            blk = 2 * px + py
            if n_small:
                small_copy(ins, outs, sems, k, blk, (px, py, c)).wait_recv()
                small_copy(ins, outs, sems, k, j, (px, py, c)).wait_send()
            for a in range(n):
                d2d(outs, sems, a, k, blk, 1 - c, (x, y, 1 - c)).wait_recv()
                ici(outs, sems, a, k, j, (px, py, c)).wait_send()
                d2d(outs, sems, a, k, blk, c, (x, y, 1 - c)).wait_send()
        if n_small:
            pltpu.make_async_copy(ins[n], outs[n].at[j], sems[4].at[0]).wait()

    out_shape = [_sds(b.shape, b.dtype) for b in bufs]
    if n_small:
        out_shape.append(_sds((N_SHARD,) + small.shape, small.dtype))
    return Exchange(list(bufs) + ([small] if n_small else []), out_shape, {a: a for a in range(n)},
                    _dma_sems(3 * n + 3, 3 * n + 3, 3 * n, 3 * n, 1),
                    [(0, start), (relay_step, relay), (-1, finish)])


def _start_then_wait(copies):
    def start(ins, outs, sems):
        for sent, _ in copies(ins, outs, sems):
            sent().start()

    def finish(ins, outs, sems):
        pairs = copies(ins, outs, sems)
        for _, arrival in pairs:
            arrival().wait_recv()
        for sent, _ in pairs:
            sent().wait_send()

    return [(0, start), (-1, finish)]


def pair_exchange(arrs):
    n = len(arrs)

    def copies(ins, outs, sems):
        x, y, c = _place()
        res = []
        for a in range(n):
            def mk(a=a):
                rh = arrs[a].shape[1] // 2
                return pltpu.make_async_remote_copy(
                    src_ref=ins[a].at[:, pl.ds((1 - c) * rh, rh), :], dst_ref=outs[a], send_sem=sems[0].at[a],
                    recv_sem=sems[1].at[a], device_id=(x, y, 1 - c), device_id_type=MESH)
            res.append((mk, mk))
        return res

    return Exchange(arrs, [_sds((a.shape[0], a.shape[1] // 2, a.shape[2]), a.dtype) for a in arrs], {},
                    _dma_sems(n, n), _start_then_wait(copies))


def pair_add(g, s, c_idx, name):
    nb, r, cols = g.shape
    rh = r // 2

    def body(c_ref, g_ref, s_ref, o_ref):
        o_ref[...] = (g_ref[...] + s_ref[...]).astype(BF16)

    return pl.pallas_call(
        body, name=name,
        grid_spec=pltpu.PrefetchScalarGridSpec(
            num_scalar_prefetch=1, grid=(nb,),
            in_specs=[pl.BlockSpec((1, rh, cols), lambda j, c: (j, c[0], 0)),
                      pl.BlockSpec((1, rh, cols), lambda j, c: (j, 0, 0))],
            out_specs=pl.BlockSpec((1, rh, cols), lambda j, c: (j, 0, 0))),
        out_shape=_sds((nb, rh, cols), BF16),
        compiler_params=pltpu.CompilerParams(dimension_semantics=("arbitrary",),
                                             vmem_limit_bytes=VMEM_LIMIT),
    )(c_idx, g, s)


def chip_exchange(arrs):
    n = len(arrs)

    def copies(ins, outs, sems):
        x, y, c = _place()
        j = 2 * x + y
        res = []
        for a in range(n):
            for k, (px, py) in enumerate(_other_chips(x, y)):
                def mk(src_blk, dst_blk, a=a, k=k, to=(px, py, c)):
                    return pltpu.make_async_remote_copy(
                        src_ref=ins[a].at[src_blk], dst_ref=outs[a].at[dst_blk], send_sem=sems[0].at[3 * a + k],
                        recv_sem=sems[1].at[3 * a + k], device_id=to, device_id_type=MESH)
                res.append((functools.partial(mk, 2 * px + py, j), functools.partial(mk, j, 2 * px + py)))
        return res

    return Exchange(arrs, [_sds(a.shape, a.dtype) for a in arrs], {}, _dma_sems(3 * n, 3 * n),
                    _start_then_wait(copies))


def sum_chips(r, p, shard_idx, name):
    nb, rh, cols = r.shape
    tr = rh

    def body(j_ref, p_ref, *refs):
        o_ref = refs[nb]
        j = j_ref[0]
        acc = None
        for i in range(nb):
            term = jnp.where(j == i, p_ref[0], refs[i][0]).astype(F32)
            acc = term if acc is None else acc + term
        o_ref[...] = acc

    def slot(i):
        return pl.BlockSpec((1, tr, cols), lambda t, j: (jnp.where(j[0] == i, (i + 1) % nb, i), t, 0))

    return pl.pallas_call(
        body, name=name,
        grid_spec=pltpu.PrefetchScalarGridSpec(
            num_scalar_prefetch=1, grid=(rh // tr,),
            in_specs=[pl.BlockSpec((1, tr, cols), lambda t, j: (j[0], t, 0))] + [slot(i) for i in range(nb)],
            out_specs=pl.BlockSpec((tr, cols), lambda t, j: (t, 0))),
        out_shape=_sds((rh, cols), F32),
        compiler_params=pltpu.CompilerParams(dimension_semantics=("arbitrary",),
                                             vmem_limit_bytes=VMEM_LIMIT),
    )(shard_idx, p, *([r] * nb))


def pair_swap(arrs):
    n = len(arrs)

    def copies(ins, outs, sems):
        x, y, c = _place()
        res = []
        for a in range(n):
            def mk(a=a):
                return pltpu.make_async_remote_copy(
                    src_ref=ins[a], dst_ref=outs[a], send_sem=sems[0].at[a], recv_sem=sems[1].at[a],
                    device_id=(x, y, 1 - c), device_id_type=MESH)
            res.append((mk, mk))
        return res

    return Exchange(arrs, [_sds(a.shape, a.dtype) for a in arrs], {}, _dma_sems(n, n),
                    _start_then_wait(copies))


ADAMW_STEPS_PER_HALF = 4


def adamw_pairs(items, name, comm=None):
    n_items = len(items)
    nh = ADAMW_STEPS_PER_HALF

    def body(*refs):
        ins, outs = refs[:5 * n_items], refs[5 * n_items:]
        mine = (pl.program_id(0) // nh) == lax.axis_index("c")
        for a in range(n_items):
            w_ref, gm_ref, gs_ref, m_ref, v_ref = ins[5 * a:5 * a + 5]
            g_ref, d_ref, mo_ref, vo_ref = outs[4 * a:4 * a + 4]
            g = jnp.where(mine, gm_ref[...], gs_ref[...])
            d, mn, vn = _adamw_math(w_ref[...], g, m_ref[...], v_ref[...])
            g_ref[...] = g
            d_ref[...] = d
            mo_ref[...] = mn
            vo_ref[...] = vn

    in_specs, out_specs, out_shape, args = [], [], [], []
    for w, g_mine, g_sib, m, v in items:
        r, cols = w.shape
        tr = r // (2 * nh)
        full = pl.BlockSpec((tr, cols), lambda i: (i, 0))
        part = pl.BlockSpec((tr, cols), lambda i: (i % nh, 0))
        in_specs += [full, part, part, full, full]
        out_specs += [full] * 4
        out_shape += [_sds((r, cols), F32)] * 4
        args += [w, g_mine, g_sib, m, v]
    res = _call(body, name=name, grid=(2 * nh,), in_specs=in_specs, out_specs=out_specs,
                out_shape=out_shape, comm=comm)(*args)
    own, hosted = (res, None) if comm is None else res
    grouped = [tuple(own[4 * a:4 * a + 4]) for a in range(n_items)]
    return grouped if comm is None else (grouped, hosted)


N_DEV = 8


def device_gather(pack):
    def copies(ins, outs, sems):
        x, y, c = _place()
        me = 4 * x + 2 * y + c
        res = []
        for k in range(1, N_DEV):
            fx, fy, fc = (k >> 2) & 1, (k >> 1) & 1, k & 1
            px, py, pc = (1 - x if fx else x, 1 - y if fy else y, 1 - c if fc else c)

            def mk(slot, k=k, to=(px, py, pc)):
                return pltpu.make_async_remote_copy(
                    src_ref=ins[0], dst_ref=outs[0].at[slot], send_sem=sems[0].at[k - 1],
                    recv_sem=sems[1].at[k - 1], device_id=to, device_id_type=MESH)
            res.append((functools.partial(mk, me), functools.partial(mk, 4 * px + 2 * py + pc)))
        return res

    return Exchange([pack], [_sds((N_DEV,) + pack.shape, pack.dtype)], {}, _dma_sems(N_DEV - 1, N_DEV - 1),
                    _start_then_wait(copies))


def sum_devices(buf, pack, me_idx, name):
    r, cols = pack.shape

    def body(me_ref, p_ref, *refs):
        o_ref = refs[N_DEV]
        acc = None
        for i in range(N_DEV):
            term = jnp.where(me_ref[0] == i, p_ref[...], refs[i][0])
            acc = term if acc is None else acc + term
        o_ref[...] = acc

    def slot(i):
        return pl.BlockSpec((1, r, cols), lambda t, me: (jnp.where(me[0] == i, (i + 1) % N_DEV, i), 0, 0))

    whole = pl.BlockSpec((r, cols), lambda t, me: (0, 0))
    return pl.pallas_call(
        body, name=name,
        grid_spec=pltpu.PrefetchScalarGridSpec(
            num_scalar_prefetch=1, grid=(1,),
            in_specs=[whole] + [slot(i) for i in range(N_DEV)], out_specs=whole),
        out_shape=_sds((r, cols), F32),
        compiler_params=pltpu.CompilerParams(dimension_semantics=("arbitrary",),
                                             vmem_limit_bytes=VMEM_LIMIT),
    )(me_idx, pack, *([buf] * N_DEV))


SMALL_NAMES = ("ffn1_norm", "mix_norm", "ffn2_norm", "final_norm", "sg_ln_g", "sg_ln_b",
               "dn_norm", "a_log", "dt_bias", "sg_b", "sg_w", "conv_w", "loss")


def _to_rows(a):
    flat = a.reshape(-1)
    pad = (-flat.shape[0]) % LANES
    if pad:
        flat = jnp.pad(flat, (0, pad))
    return flat.reshape(-1, LANES)


def _pack_small(parts):
    rows = [_to_rows(parts[k]) for k in SMALL_NAMES]
    pack = jnp.concatenate(rows, axis=0)
    pad = (-pack.shape[0]) % 8
    if pad:
        pack = jnp.pad(pack, ((0, pad), (0, 0)))
    return pack


def _unpack_small(pack, shapes):
    out, r0 = {}, 0
    for k in SMALL_NAMES:
        size = 1
        for s in shapes[k]:
            size *= s
        nrows = -(-size // LANES)
        out[k] = pack[r0:r0 + nrows].reshape(-1)[:size].reshape(shapes[k])
        r0 += nrows
    return out


def kernel(x, ffn1_norm, ffn1_w_gate, ffn1_w_up, ffn1_w_down, mix_norm, w_in, conv_w, a_log, dt_bias, dn_norm, sg_ln_g, sg_ln_b, sg_w, sg_b, w_out, ffn2_norm, ffn2_w_gate, ffn2_w_up, ffn2_w_down, final_norm, loss_target, m_ffn1_norm, m_ffn1_w_gate, m_ffn1_w_up, m_ffn1_w_down, m_mix_norm, m_w_in, m_conv_w, m_a_log, m_dt_bias, m_dn_norm, m_sg_ln_g, m_sg_ln_b, m_sg_w, m_sg_b, m_w_out, m_ffn2_norm, m_ffn2_w_gate, m_ffn2_w_up, m_ffn2_w_down, m_final_norm, v_ffn1_norm, v_ffn1_w_gate, v_ffn1_w_up, v_ffn1_w_down, v_mix_norm, v_w_in, v_conv_w, v_a_log, v_dt_bias, v_dn_norm, v_sg_ln_g, v_sg_ln_b, v_sg_w, v_sg_b, v_w_out, v_ffn2_norm, v_ffn2_w_gate, v_ffn2_w_up, v_ffn2_w_down, v_final_norm):
    bsz, t_len, d = x.shape
    n = bsz * t_len
    xy, yy, cc = _place()
    shard = 2 * xy + yy

    big_names = ["ffn1_w_gate", "ffn1_w_up", "ffn1_w_down", "w_in", "w_out",
                 "ffn2_w_gate", "ffn2_w_up", "ffn2_w_down"]
    big_w = dict(ffn1_w_gate=ffn1_w_gate, ffn1_w_up=ffn1_w_up, ffn1_w_down=ffn1_w_down, w_in=w_in,
                 w_out=w_out, ffn2_w_gate=ffn2_w_gate, ffn2_w_up=ffn2_w_up, ffn2_w_down=ffn2_w_down)
    big_m = dict(ffn1_w_gate=m_ffn1_w_gate, ffn1_w_up=m_ffn1_w_up, ffn1_w_down=m_ffn1_w_down, w_in=m_w_in,
                 w_out=m_w_out, ffn2_w_gate=m_ffn2_w_gate, ffn2_w_up=m_ffn2_w_up, ffn2_w_down=m_ffn2_w_down)
    big_v = dict(ffn1_w_gate=v_ffn1_w_gate, ffn1_w_up=v_ffn1_w_up, ffn1_w_down=v_ffn1_w_down, w_in=v_w_in,
                 w_out=v_w_out, ffn2_w_gate=v_ffn2_w_gate, ffn2_w_up=v_ffn2_w_up, ffn2_w_down=v_ffn2_w_down)
    shard_idx = jnp.reshape(shard, (1,)).astype(jnp.int32)
    c_idx = jnp.reshape(cc, (1,)).astype(jnp.int32)
    transposed = ("ffn1_w_gate", "ffn1_w_up", "ffn2_w_gate", "ffn2_w_up")

    def as2d(a, k):
        return a[0].T if k in transposed else a[0]

    def from2d(a, k):
        return a.T[None] if k in transposed else a[None]

    placed = {k: cast_place(as2d(big_w[k], k), shard_idx, name="cast_" + k) for k in big_names}
    first_names = ["ffn1_w_gate", "ffn1_w_up"]
    second_names = ["ffn1_w_down", "w_in"]
    third_names = ["w_out", "ffn2_w_gate"]
    fourth_names = ["ffn2_w_up", "ffn2_w_down"]
    res = run_exchange(gather_exchange([placed[k] for k in first_names], conv_w[0]), name="gather_first")
    gw = dict(zip(first_names, res[:2]))
    conv_full = res[2].transpose(1, 0, 2).reshape(CONV_K, 3 * HALF_W)

    x0 = x.reshape(n, d)
    def ffn_weights(prefix):
        return [gw[prefix + k].reshape(-1, d) for k in ("_w_gate", "_w_up", "_w_down")]

    def ffn_grad_blocks(grads):
        return [g.reshape(N_SHARD, -1, d) for g in grads]

    (h1, gate1, up1, act1), second = ffn_fwd(
        x0, ffn1_norm, gw["ffn1_w_gate"].reshape(-1, d), gw["ffn1_w_up"].reshape(-1, d), None,
        name="ffn1_fwd", comm=gather_exchange([placed[k] for k in second_names]))
    gw.update(zip(second_names, second))
    (x1,) = ffn_down(x0, act1, gw["ffn1_w_down"].reshape(-1, d), name="ffn1_down")
    w_in_full = gw["w_in"].transpose(1, 0, 2).reshape(d, IN_COLS)
    w_in_full = jnp.pad(w_in_full, ((0, 0), (0, PROJ_W - IN_COLS)))
    (proj, h2), third = in_proj_fwd(x1, mix_norm, w_in_full, name="in_proj_fwd",
                                    comm=gather_exchange([placed[k] for k in third_names]))
    gw.update(zip(third_names, third))
    proj3 = proj.reshape(bsz, t_len, PROJ_W)
    bias_tile = jnp.repeat(sg_b[0].T, SG_GROUP_DIM, axis=1)
    sg_out = sg_fwd(proj, sg_ln_g, sg_ln_b, sg_w[0], bias_tile, name="sg_fwd")
    qkv = dn_conv_fwd(proj3, conv_full, name="dn_conv_fwd")
    alog_row = jnp.zeros((1, LANES), F32).at[0, N_HEADS:2 * N_HEADS].set(a_log[0])
    dtb_row = jnp.zeros((1, LANES), F32).at[0, N_HEADS:2 * N_HEADS].set(dt_bias[0])
    (u_wy, w_wy, q_dec, k_dec, qk, tinv, gc), fourth = dn_chunk_fwd(
        qkv, proj3, alog_row, dtb_row, name="dn_chunk_fwd",
        comm=gather_exchange([placed[k] for k in fourth_names]))
    gw.update(zip(fourth_names, fourth))
    w_out_full = gw["w_out"].reshape(2 * HALF_W, d)
    o, s_in = dn_scan_fwd(u_wy, w_wy, q_dec, k_dec, qk, gc, name="dn_scan_fwd")
    dn_out = dn_out_fwd(o.reshape(n, HALF_W), proj, dn_norm, name="dn_out_fwd")
    x2 = out_proj_fwd(x1, sg_out, dn_out, w_out_full, name="out_proj_fwd")
    x3, h3, gate2, up2, act2 = ffn_fwd(x2, ffn2_norm, *ffn_weights("ffn2"), name="ffn2_fwd")
    dx3, d_final_norm, loss_tile = final_loss(x3, final_norm.reshape(1, d),
                                              loss_target.reshape(n, d), name="final_loss")

    dx2, dgate2, dup2, dyh2, d_ffn2_norm = ffn_bwd_act(
        dx3, x2, ffn2_norm, gate2, up2, *ffn_weights("ffn2"), name="ffn2_bwd_act")
    g_big = {}
    g_big["ffn2_w_gate"], g_big["ffn2_w_up"], g_big["ffn2_w_down"] = ffn_grad_blocks(ffn_bwd_w(
        [dgate2, dup2, act2], [h3, dyh2], [(0, 0), (1, 0), (2, 1)], name="ffn2_bwd_w"))

    early = ["ffn2_w_gate", "ffn2_w_up", "ffn2_w_down"]
    (d_sg, d_dn, dx2b), early_sib = out_proj_bwd_x(dx2, w_out_full, name="out_proj_bwd_x",
                                                   comm=pair_exchange([g_big[k] for k in early]))
    early_sums = [pair_add(g_big[k], s, c_idx, name="grad_pair_add_" + k) for k, s in zip(early, early_sib)]
    g_w_out = matmul_tn([sg_out, dn_out], dx2b, d, name="w_out_grad")
    g_big["w_out"] = g_w_out.reshape(N_SHARD, (2 * HALF_W) // N_SHARD, d)

    d_proj, d_sg_w, d_bias_tile, d_ln_g, d_ln_b = sg_bwd(d_sg, proj, sg_ln_g, sg_ln_b, sg_w[0],
                                                         bias_tile, name="sg_bwd")
    d_o, d_proj, d_dn_norm = dn_out_bwd(d_dn, o.reshape(n, HALF_W), proj, dn_norm, d_proj,
                                        name="dn_out_bwd")
    du, dw, dqd, dkd, dqk, dgc_scan = dn_scan_bwd(d_o.reshape(bsz, t_len, HALF_W), u_wy, w_wy, q_dec,
                                                  k_dec, qk, gc, s_in, name="dn_scan_bwd")
    (d_qkv, d_proj3, d_alog_row, d_dtb_row), early_chips = dn_chunk_bwd(
        qkv, proj3, alog_row, dtb_row, tinv, u_wy, w_wy, du, dw, dqd, dkd, dqk, dgc_scan,
        d_proj.reshape(bsz, t_len, PROJ_W), name="dn_chunk_bwd", comm=chip_exchange(early_sums))
    early_halves = [sum_chips(r, p, shard_idx, name="grad_chip_sum_" + k)
                    for k, r, p in zip(early, early_chips, early_sums)]
    d_proj3, d_conv = dn_conv_bwd(d_qkv, proj3, conv_full, d_proj3, name="dn_conv_bwd")
    d_proj = d_proj3.reshape(n, PROJ_W)
    g_w_in = matmul_tn([h2], d_proj, 3 * MXU_DIM, name="w_in_grad")[:, :IN_COLS]
    g_big["w_in"] = g_w_in.reshape(d, N_SHARD, IN_COLS // N_SHARD).transpose(1, 0, 2)

    def reduce_start(names):
        return pair_exchange([g_big[k] for k in names])

    def reduce_pair_sums(names, from_sib):
        return [pair_add(g_big[k], s, c_idx, name="grad_pair_add_" + k) for k, s in zip(names, from_sib)]

    def reduce_chip_sums(names, from_chips, sums):
        return [sum_chips(r, p, shard_idx, name="grad_chip_sum_" + k)
                for k, r, p in zip(names, from_chips, sums)]

    mid = ["w_in", "w_out"]
    (dx1, d_mix_norm, dyh1), mid_sib = in_proj_bwd_x(d_proj, w_in_full, x1, mix_norm, dx2,
                                                     name="in_proj_bwd_x", comm=reduce_start(mid))
    mid_sums = reduce_pair_sums(mid, mid_sib)
    down = ["ffn1_w_down"]
    (g_down,), mid_chips = ffn_bwd_w([act1], [dyh1], [(0, 0)], name="ffn1_bwd_w_down",
                                     comm=chip_exchange(mid_sums))
    g_big["ffn1_w_down"] = g_down.reshape(N_SHARD, -1, d)
    mid_halves = reduce_chip_sums(mid, mid_chips, mid_sums)
    leg, legs = merge_exchanges([reduce_start(down), pair_swap(mid_halves), pair_swap(early_halves)])
    leg_res = run_exchange(leg, name="grad_pair_exchange_down")
    down_sums = reduce_pair_sums(down, leg_res[legs[0]])
    mid_sib_halves, early_sib_halves = leg_res[legs[1]], leg_res[legs[2]]

    dx0, dgate1, dup1, _, d_ffn1_norm = ffn_bwd_act(
        dx1, x0, ffn1_norm, gate1, up1, *ffn_weights("ffn1"), name="ffn1_bwd_act")
    grad_x = dx0.reshape(bsz, t_len, d)
    d_sg_b = d_bias_tile.reshape(SG_CHUNK, SG_GROUPS, SG_GROUP_DIM).sum(axis=-1).T
    small_g = dict(ffn1_norm=d_ffn1_norm, mix_norm=d_mix_norm, ffn2_norm=d_ffn2_norm,
                   final_norm=d_final_norm, sg_ln_g=d_ln_g, sg_ln_b=d_ln_b, dn_norm=d_dn_norm,
                   a_log=d_alog_row[:, N_HEADS:2 * N_HEADS], dt_bias=d_dtb_row[:, N_HEADS:2 * N_HEADS],
                   sg_b=d_sg_b, sg_w=d_sg_w, conv_w=d_conv, loss=loss_tile[:, :1])
    my_pack = _pack_small(small_g)
    hosted, parts = merge_exchanges([chip_exchange(down_sums), device_gather(my_pack)])
    late = ["ffn1_w_gate", "ffn1_w_up"]
    late_grads, hosted_res = ffn_bwd_w([dgate1, dup1], [h1], [(0, 0), (1, 0)], name="ffn1_bwd_w_gate_up",
                                       comm=hosted)
    g_big["ffn1_w_gate"], g_big["ffn1_w_up"] = ffn_grad_blocks(late_grads)
    down_halves = reduce_chip_sums(down, hosted_res[parts[0]], down_sums)
    (all_packs,) = hosted_res[parts[1]]

    leg, legs = merge_exchanges([reduce_start(late), pair_swap(down_halves)])
    leg_res = run_exchange(leg, name="grad_pair_exchange")
    pair_sums = reduce_pair_sums(late, leg_res[legs[0]])
    down_sib_halves = leg_res[legs[1]]

    def adam_items(names, mine, sib):
        return [(as2d(big_w[k], k), gm, gs, as2d(big_m[k], k), as2d(big_v[k], k))
                for k, gm, gs in zip(names, mine, sib)]

    outs = {}
    done = adamw_pairs(
        adam_items(early + mid + down, early_halves + mid_halves + down_halves,
                   list(early_sib_halves) + list(mid_sib_halves) + list(down_sib_halves)),
        name="adamw_early")
    from_chips = run_exchange(chip_exchange(pair_sums), name="grad_chip_exchange")
    halves = reduce_chip_sums(late, from_chips, pair_sums)
    sib_halves = run_exchange(pair_swap(halves), name="grad_pair_swap")
    done += adamw_pairs(adam_items(late, halves, sib_halves), name="adamw_late")
    for k, res in zip(early + mid + down + late, done):
        outs[k] = tuple(from2d(a, k) for a in res)

    small_w = dict(ffn1_norm=ffn1_norm, mix_norm=mix_norm, ffn2_norm=ffn2_norm, final_norm=final_norm,
                   sg_ln_g=sg_ln_g, sg_ln_b=sg_ln_b, dn_norm=dn_norm, a_log=a_log, dt_bias=dt_bias,
                   sg_b=sg_b, sg_w=sg_w)
    small_m = dict(ffn1_norm=m_ffn1_norm, mix_norm=m_mix_norm, ffn2_norm=m_ffn2_norm,
                   final_norm=m_final_norm, sg_ln_g=m_sg_ln_g, sg_ln_b=m_sg_ln_b, dn_norm=m_dn_norm,
                   a_log=m_a_log, dt_bias=m_dt_bias, sg_b=m_sg_b, sg_w=m_sg_w)
    small_v = dict(ffn1_norm=v_ffn1_norm, mix_norm=v_mix_norm, ffn2_norm=v_ffn2_norm,
                   final_norm=v_final_norm, sg_ln_g=v_sg_ln_g, sg_ln_b=v_sg_ln_b, dn_norm=v_dn_norm,
                   a_log=v_a_log, dt_bias=v_dt_bias, sg_b=v_sg_b, sg_w=v_sg_w)
    shapes = {k: small_w[k].shape for k in small_w}
    shapes["conv_w"] = (CONV_K, 3 * HALF_W)
    shapes["loss"] = (1, 1)
    me_idx = jnp.reshape(4 * xy + 2 * yy + cc, (1,)).astype(jnp.int32)
    g_pack = sum_devices(all_packs, my_pack, me_idx, name="small_sum")
    g_small = _unpack_small(g_pack, shapes)
    loss = g_small["loss"].reshape(())
    cw = 3 * HALF_W // N_SHARD
    g_conv = lax.dynamic_slice_in_dim(g_small["conv_w"], shard * cw, cw, axis=1)
    zero_conv = jnp.zeros((CONV_K, 3 * HALF_W), F32)

    def packed(src, conv):
        parts = dict(src)
        parts["conv_w"] = lax.dynamic_update_slice_in_dim(zero_conv, conv[0], shard * cw, axis=1)
        parts["loss"] = jnp.zeros((1, 1), F32)
        return _pack_small(parts)

    d_pack, m_pack, v_pack = adamw(packed(small_w, conv_w), g_pack, packed(small_m, m_conv_w),
                                   packed(small_v, v_conv_w), name="adamw_small")
    d_small = _unpack_small(d_pack, shapes)
    m_small = _unpack_small(m_pack, shapes)
    v_small = _unpack_small(v_pack, shapes)

    def conv_block(full_arr):
        return lax.dynamic_slice_in_dim(full_arr, shard * cw, cw, axis=1)[None]

    for k in small_w:
        outs[k] = (g_small[k].reshape(small_w[k].shape), d_small[k], m_small[k], v_small[k])
    outs["conv_w"] = (g_conv[None], conv_block(d_small["conv_w"]), conv_block(m_small["conv_w"]),
                      conv_block(v_small["conv_w"]))

    order = ["ffn1_norm", "ffn1_w_gate", "ffn1_w_up", "ffn1_w_down", "mix_norm", "w_in", "conv_w",
             "a_log", "dt_bias", "dn_norm", "sg_ln_g", "sg_ln_b", "sg_w", "sg_b", "w_out", "ffn2_norm",
             "ffn2_w_gate", "ffn2_w_up", "ffn2_w_down", "final_norm"]
    return (loss, grad_x, *[outs[k][0] for k in order], *[outs[k][1] for k in order],
            *[outs[k][2] for k in order], *[outs[k][3] for k in order])
```

```python
import functools

import jax
import jax.numpy as jnp
from jax import lax
from jax.experimental import pallas as pl
from jax.experimental.pallas import tpu as pltpu

F32 = jnp.float32
BF16 = jnp.bfloat16
EPS = 1e-6

D_MODEL = 1024
N_SHARD = 4
HEAD_DIM = 128
N_HEADS = 4
DN_CHUNK = 64
SG_CHUNK = 128
SG_GROUPS = 8
SG_GROUP_DIM = 64
HALF_W = 512
PROJ_W = 3200
IN_COLS = 3080
GATE_COL_BLOCK = 24
QK_SCALE = HEAD_DIM ** -0.5
LANES = 128

ADAM_LR = 0.001
ADAM_B1 = 0.9
ADAM_B2 = 0.999
ADAM_EPS = 1e-08
ADAM_WD = 0.01
ADAM_STEP = 10

VMEM_LIMIT = 56 * 1024 * 1024
ROW_TILE = 512

NN = ((1,), (0,))
NT = ((1,), (1,))
TN = ((0,), (0,))
MESH = pl.DeviceIdType.MESH


def _dot(a, b, dims):
    return lax.dot_general(a, b, (dims, ((), ())), preferred_element_type=F32)


def _bdot(a, b, dims):
    return _dot(a.astype(BF16), b.astype(BF16), dims)


def _split(a):
    hi = a.astype(BF16)
    lo = (a - hi.astype(F32)).astype(BF16)
    return hi, lo


def _dot3(a, b, dims=NN):
    return _dot(a[0], b[0], dims) + (_dot(a[0], b[1], dims) + _dot(a[1], b[0], dims))


def _dot_exact_lhs(a, b):
    ab = a.astype(BF16)
    b1 = b.astype(BF16)
    r1 = b - b1.astype(F32)
    b2 = r1.astype(BF16)
    b3 = (r1 - b2.astype(F32)).astype(BF16)
    return _dot(ab, b1, NN) + (_dot(ab, b2, NN) + _dot(ab, b3, NN))


def _call(body, *, name, out_shape, in_specs, out_specs, grid=(), scratch=(), comm=None, **kw):
    params = dict(vmem_limit_bytes=VMEM_LIMIT)
    if grid:
        params["dimension_semantics"] = ("arbitrary",) * len(grid)
    if comm is None:
        return pl.pallas_call(
            body, name=name, grid=grid, in_specs=in_specs, out_specs=out_specs,
            out_shape=out_shape, scratch_shapes=list(scratch),
            compiler_params=pltpu.CompilerParams(**params), **kw)

    n_in, n_out, n_sc = len(in_specs), len(out_specs), len(scratch)
    c_in, c_out = len(comm.inputs), len(comm.out_shape)
    steps = 1
    for g in grid:
        steps *= g

    def hosted(*refs):
        ins, cins = refs[:n_in], refs[n_in:n_in + c_in]
        o0 = n_in + c_in
        outs, couts = refs[o0:o0 + n_out], refs[o0 + n_out:o0 + n_out + c_out]
        s0 = o0 + n_out + c_out
        sc, csems = refs[s0:s0 + n_sc], refs[s0 + n_sc:]
        lin = 0
        for axis, g in enumerate(grid):
            lin = lin * g + pl.program_id(axis)

        def at(step, fn):
            @pl.when(lin == step % steps)
            def _():
                fn(cins, couts, csems)

        for step, fn in comm.phases:
            if step >= 0:
                at(step, fn)
        body(*ins, *outs, *sc)
        for step, fn in comm.phases:
            if step < 0:
                at(step, fn)

    aliases = dict(kw.pop("input_output_aliases", {}))
    for k, m in comm.aliases.items():
        aliases[n_in + k] = n_out + m
    call = pl.pallas_call(
        hosted, name=name, grid=grid, in_specs=list(in_specs) + [_ANY] * c_in,
        out_specs=list(out_specs) + [_ANY] * c_out, out_shape=list(out_shape) + comm.out_shape,
        scratch_shapes=list(scratch) + comm.sems, input_output_aliases=aliases,
        compiler_params=pltpu.CompilerParams(**params), **kw)

    def run(*args):
        res = call(*args, *comm.inputs)
        return res[:n_out], res[n_out:]

    return run


def _sds(shape, dtype):
    return jax.ShapeDtypeStruct(tuple(shape), dtype)


def _resident(shape):
    zeros = (0,) * len(shape)
    return pl.BlockSpec(tuple(shape), lambda *_: zeros, pipeline_mode=pl.Buffered(1))


def _sigmoid(x):
    return jax.nn.sigmoid(x)


def _softplus(x):
    return jnp.maximum(x, 0.0) + jnp.log(1.0 + jnp.exp(-jnp.abs(x)))


_GELU_C = 0.7978845608028654
_GELU_A = 0.044715


def _gelu_tanh(x):
    return jnp.tanh(_GELU_C * (x + _GELU_A * x * x * x))


def _gelu(x, t):
    return 0.5 * x * (1.0 + t)


def _gelu_grad(x, t):
    return 0.5 * (1.0 + t) + 0.5 * x * (1.0 - t * t) * _GELU_C * (1.0 + 3.0 * _GELU_A * x * x)


def _silu_grad(x):
    s = _sigmoid(x)
    return s * (1.0 + x * (1.0 - s))


def _rms_scale(xv):
    return lax.rsqrt(jnp.mean(xv * xv, axis=-1, keepdims=True) + EPS)


def _rms_bwd(dh, xv, g):
    r = _rms_scale(xv)
    xn = xv * r
    dg = jnp.sum(dh * xn, axis=0, keepdims=True)
    dxn = dh * g
    dx = r * (dxn - xn * jnp.mean(dxn * xn, axis=-1, keepdims=True))
    return dx, dg


def _iota2(shape, dim):
    return lax.broadcasted_iota(jnp.int32, shape, dim)


def _col_to_row(col):
    n = col.shape[0]
    eye = _iota2((n, n), 0) == _iota2((n, n), 1)
    return jnp.sum(jnp.where(eye, col, 0.0), axis=0, keepdims=True)


def _row_to_col(row):
    n = row.shape[1]
    eye = _iota2((n, n), 0) == _iota2((n, n), 1)
    return jnp.sum(jnp.where(eye, row, 0.0), axis=1, keepdims=True)


MXU_DIM = 256
EW_ROWS = 64


def _hidden_chunks(f, step=3 * MXU_DIM):
    return [(c0, min(c0 + step, f)) for c0 in range(0, f, step)]

def ffn_fwd(x, gnorm, wg, wu, wd, name, comm=None):
    n, d = x.shape
    f = wg.shape[0]
    tm = min(ROW_TILE, n)
    fused = wd is not None

    def body(x_ref, g_ref, wg_ref, wu_ref, *rest):
        if fused:
            wd_ref, xo_ref, h_ref, gate_ref, up_ref, act_ref, acc_ref = rest
        else:
            h_ref, gate_ref, up_ref, act_ref = rest
        xv = x_ref[...]
        h = (xv * _rms_scale(xv) * g_ref[...]).astype(BF16)
        h_ref[...] = h
        chunks = _hidden_chunks(f)

        def gate_up(c0, c1):
            return _dot(h, wg_ref[c0:c1, :], NT), _dot(h, wu_ref[c0:c1, :], NT)

        nxt = gate_up(*chunks[0])
        for idx, (c0, c1) in enumerate(chunks):
            gate, up = nxt
            if idx + 1 < len(chunks):
                nxt = gate_up(*chunks[idx + 1])
            act = (gate * _sigmoid(gate) * up).astype(BF16)
            gate_ref[:, c0:c1] = gate.astype(BF16)
            up_ref[:, c0:c1] = up.astype(BF16)
            act_ref[:, c0:c1] = act
            if fused:
                part = _dot(act, wd_ref[c0:c1, :], NN)
                if c0 == 0:
                    acc_ref[...] = part
                else:
                    acc_ref[...] += part
        if fused:
            xo_ref[...] = xv + 0.5 * acc_ref[...]

    row = pl.BlockSpec((tm, d), lambda i: (i, 0))
    wide = pl.BlockSpec((tm, f), lambda i: (i, 0))
    n_w = 3 if fused else 2
    return _call(
        body, name=name, grid=(n // tm,),
        in_specs=[row, pl.BlockSpec((1, d), lambda i: (0, 0))] + [_resident((f, d))] * n_w,
        out_specs=([row] if fused else []) + [row, wide, wide, wide],
        out_shape=([_sds((n, d), F32)] if fused else []) + [_sds((n, d), BF16)] + [_sds((n, f), BF16)] * 3,
        scratch=[pltpu.VMEM((tm, d), F32)] if fused else [], comm=comm,
    )(*([x, gnorm, wg, wu] + ([wd] if fused else [])))


def ffn_down(x, act, wd, name, comm=None):
    n, d = x.shape
    f = wd.shape[0]
    tm = min(ROW_TILE, n)

    def body(x_ref, a_ref, w_ref, o_ref):
        o_ref[...] = x_ref[...] + 0.5 * _dot(a_ref[...], w_ref[...], NN)

    row = pl.BlockSpec((tm, d), lambda i: (i, 0))
    return _call(
        body, name=name, grid=(n // tm,),
        in_specs=[row, pl.BlockSpec((tm, f), lambda i: (i, 0)), _resident((f, d))],
        out_specs=[row], out_shape=[_sds((n, d), F32)], comm=comm,
    )(x, act, wd)


def ffn_bwd_act(dy, x, gnorm, gate, up, wg, wu, wd, name, comm=None):
    n, d = x.shape
    f = wg.shape[0]
    tm = min(ROW_TILE // 2, n)

    def body(dy_ref, x_ref, g_ref, gate_ref, up_ref, wg_ref, wu_ref, wd_ref,
             dx_ref, dgate_ref, dup_ref, dyh_ref, dg_ref, acc_ref):
        @pl.when(pl.program_id(0) == 0)
        def _():
            dg_ref[...] = jnp.zeros_like(dg_ref)

        dyh = (0.5 * dy_ref[...]).astype(BF16)
        dyh_ref[...] = dyh
        chunks = _hidden_chunks(f, 2 * MXU_DIM)
        next_dact = _dot(dyh, wd_ref[chunks[0][0]:chunks[0][1], :], NT)
        for idx, (c0, c1) in enumerate(chunks):
            dact = next_dact
            if idx + 1 < len(chunks):
                n0, n1 = chunks[idx + 1]
                next_dact = _dot(dyh, wd_ref[n0:n1, :], NT)
            for r0 in range(0, tm, EW_ROWS):
                rs = slice(r0, min(r0 + EW_ROWS, tm))
                gt = gate_ref[rs, c0:c1].astype(F32)
                u = up_ref[rs, c0:c1].astype(F32)
                da = dact[rs]
                s = _sigmoid(gt)
                dup_ref[rs, c0:c1] = (da * (gt * s)).astype(BF16)
                dgate_ref[rs, c0:c1] = (da * u * (s * (1.0 + gt * (1.0 - s)))).astype(BF16)
            part = (_dot(dgate_ref[:, c0:c1], wg_ref[c0:c1, :], NN)
                    + _dot(dup_ref[:, c0:c1], wu_ref[c0:c1, :], NN))
            if c0 == 0:
                acc_ref[...] = part
            else:
                acc_ref[...] += part
        dxn, dg = _rms_bwd(acc_ref[...], x_ref[...], g_ref[...])
        dx_ref[...] = dy_ref[...] + dxn
        dg_ref[...] += dg

    row = pl.BlockSpec((tm, d), lambda i: (i, 0))
    wide = pl.BlockSpec((tm, f), lambda i: (i, 0))
    vec = pl.BlockSpec((1, d), lambda i: (0, 0))
    wres = _resident((f, d))
    return _call(
        body, name=name, grid=(n // tm,),
        in_specs=[row, row, vec, wide, wide, wres, wres, wres],
        out_specs=[row, wide, wide, row, vec],
        out_shape=[_sds((n, d), F32), _sds((n, f), BF16), _sds((n, f), BF16),
                   _sds((n, d), BF16), _sds((1, d), F32)],
        scratch=[pltpu.VMEM((tm, d), F32)], comm=comm,
    )(dy, x, gnorm, gate, up, wg, wu, wd)


def ffn_bwd_w(wide, rows, pairs, name, comm=None):
    n, d = rows[0].shape
    f = wide[0].shape[1]
    fh = f // 2
    tk = min(ROW_TILE, n)
    n_w, n_r = len(wide), len(rows)

    def body(*refs):
        wide_refs, row_refs, outs = refs[:n_w], refs[n_w:n_w + n_r], refs[n_w + n_r:]

        @pl.when(pl.program_id(1) == 0)
        def _():
            for o_ref in outs:
                o_ref[...] = jnp.zeros_like(o_ref)

        row_vals = [r[...] for r in row_refs]
        for c0, c1 in _hidden_chunks(fh, 2 * MXU_DIM):
            for (i, k), o_ref in zip(pairs, outs):
                o_ref[c0:c1, :] += _dot(wide_refs[i][:, c0:c1], row_vals[k], TN)

    row = pl.BlockSpec((tk, d), lambda j, k: (k, 0))
    blk = pl.BlockSpec((tk, fh), lambda j, k: (k, j))
    return _call(
        body, name=name, grid=(2, n // tk),
        in_specs=[blk] * n_w + [row] * n_r,
        out_specs=[pl.BlockSpec((fh, d), lambda j, k: (j, 0))] * len(pairs),
        out_shape=[_sds((f, d), F32)] * len(pairs), comm=comm,
    )(*wide, *rows)


def final_loss(x, gnorm, target, name):
    n, d = x.shape
    tm = min(ROW_TILE, n)

    def body(x_ref, g_ref, t_ref, dx_ref, dg_ref, loss_ref):
        @pl.when(pl.program_id(0) == 0)
        def _():
            dg_ref[...] = jnp.zeros_like(dg_ref)
            loss_ref[...] = jnp.zeros_like(loss_ref)

        xv = x_ref[...]
        y = xv * _rms_scale(xv) * g_ref[...]
        err = y - t_ref[...]
        part = 0.5 * jnp.sum(jnp.mean(err * err, axis=-1, keepdims=True), axis=0, keepdims=True)
        loss_ref[...] += jnp.broadcast_to(part, loss_ref.shape)
        dx, dg = _rms_bwd(err * (1.0 / d), xv, g_ref[...])
        dx_ref[...] = dx
        dg_ref[...] += dg

    row = pl.BlockSpec((tm, d), lambda i: (i, 0))
    vec = pl.BlockSpec((1, d), lambda i: (0, 0))
    return _call(
        body, name=name, grid=(n // tm,),
        in_specs=[row, vec, row],
        out_specs=[row, vec, pl.BlockSpec((1, LANES), lambda i: (0, 0))],
        out_shape=[_sds((n, d), F32), _sds((1, d), F32), _sds((1, LANES), F32)],
    )(x, gnorm, target)


def in_proj_fwd(x, gnorm, w, name, comm=None):
    n, d = x.shape
    cols = w.shape[1]
    tm = min(ROW_TILE, n)

    def body(x_ref, g_ref, w_ref, p_ref, h_ref):
        xv = x_ref[...]
        h = (xv * _rms_scale(xv) * g_ref[...]).astype(BF16)
        h_ref[...] = h
        for c0, c1 in _hidden_chunks(cols):
            p_ref[:, c0:c1] = _dot(h, w_ref[:, c0:c1], NN)

    return _call(
        body, name=name, grid=(n // tm,),
        in_specs=[pl.BlockSpec((tm, d), lambda i: (i, 0)),
                  pl.BlockSpec((1, d), lambda i: (0, 0)), _resident((d, cols))],
        out_specs=[pl.BlockSpec((tm, cols), lambda i: (i, 0)),
                   pl.BlockSpec((tm, d), lambda i: (i, 0))],
        out_shape=[_sds((n, cols), F32), _sds((n, d), BF16)], comm=comm,
    )(x, gnorm, w)


def in_proj_bwd_x(dproj, w, x, gnorm, dres, name, comm=None):
    n, d = x.shape
    cols = w.shape[1]
    tm = min(ROW_TILE, n)

    def body(dp_ref, w_ref, x_ref, g_ref, dr_ref, dx_ref, dg_ref, dxh_ref):
        @pl.when(pl.program_id(0) == 0)
        def _():
            dg_ref[...] = jnp.zeros_like(dg_ref)

        dh = _dot(dp_ref[...], w_ref[...], NT)
        dxn, dg = _rms_bwd(dh, x_ref[...], g_ref[...])
        dx = dr_ref[...] + dxn
        dx_ref[...] = dx
        dxh_ref[...] = (0.5 * dx).astype(BF16)
        dg_ref[...] += dg

    row = pl.BlockSpec((tm, d), lambda i: (i, 0))
    vec = pl.BlockSpec((1, d), lambda i: (0, 0))
    return _call(
        body, name=name, grid=(n // tm,),
        in_specs=[pl.BlockSpec((tm, cols), lambda i: (i, 0)),
                  _resident((d, cols)), row, vec, row],
        out_specs=[row, vec, row],
        out_shape=[_sds((n, d), F32), _sds((1, d), F32), _sds((n, d), BF16)], comm=comm,
    )(dproj, w, x, gnorm, dres)


def matmul_tn(a_list, b, tn, name):
    n, cb = b.shape
    widths = [a.shape[1] for a in a_list]
    tk = min(ROW_TILE, n)

    def body(*refs):
        a_refs, b_ref, o_ref = refs[:-2], refs[-2], refs[-1]

        @pl.when(pl.program_id(0) == 0)
        def _():
            o_ref[...] = jnp.zeros_like(o_ref)

        r0 = 0
        for a_ref, ka in zip(a_refs, widths):
            av = a_ref[...]
            for c0, c1 in _hidden_chunks(cb, tn):
                o_ref[r0:r0 + ka, c0:c1] += _dot(av, b_ref[:, c0:c1], TN)
            r0 += ka

    return _call(
        body, name=name, grid=(n // tk,),
        in_specs=[pl.BlockSpec((tk, ka), lambda k: (k, 0)) for ka in widths]
        + [pl.BlockSpec((tk, cb), lambda k: (k, 0))],
        out_specs=pl.BlockSpec((sum(widths), cb), lambda k: (0, 0)),
        out_shape=_sds((sum(widths), cb), F32),
    )(*a_list, b)


def out_proj_fwd(x, sg_out, dn_out, w, name):
    n, d = x.shape
    tm = min(ROW_TILE, n)

    def body(x_ref, a_ref, b_ref, w_ref, o_ref):
        o_ref[...] = (x_ref[...] + _dot(a_ref[...], w_ref[0:HALF_W, :], NN)
                      + _dot(b_ref[...], w_ref[HALF_W:2 * HALF_W, :], NN))

    row = pl.BlockSpec((tm, d), lambda i: (i, 0))
    half = pl.BlockSpec((tm, HALF_W), lambda i: (i, 0))
    return _call(
        body, name=name, grid=(n // tm,),
        in_specs=[row, half, half, pl.BlockSpec((2 * HALF_W, d), lambda i: (0, 0))],
        out_specs=row, out_shape=_sds((n, d), F32),
    )(x, sg_out, dn_out, w)


def out_proj_bwd_x(dy, w, name, comm=None):
    n, d = dy.shape
    tm = min(ROW_TILE, n)

    def body(dy_ref, w_ref, dsg_ref, ddn_ref, dyb_ref):
        dyb = dy_ref[...].astype(BF16)
        dyb_ref[...] = dyb
        dsg_ref[...] = _dot(dyb, w_ref[0:HALF_W, :], NT)
        ddn_ref[...] = _dot(dyb, w_ref[HALF_W:2 * HALF_W, :], NT)

    row = pl.BlockSpec((tm, d), lambda i: (i, 0))
    half = pl.BlockSpec((tm, HALF_W), lambda i: (i, 0))
    return _call(
        body, name=name, grid=(n // tm,),
        in_specs=[row, pl.BlockSpec((2 * HALF_W, d), lambda i: (0, 0))],
        out_specs=[half, half, row],
        out_shape=[_sds((n, HALF_W), F32), _sds((n, HALF_W), F32), _sds((n, d), BF16)], comm=comm,
    )(dy, w)


SG_PAIRS = SG_GROUPS // 2


def _sg_low_half():
    return _iota2((SG_CHUNK, LANES), 1) < SG_GROUP_DIM


def _sg_pair_cols(p):
    return slice(p * LANES, (p + 1) * LANES)


def _sg_causal():
    return _iota2((SG_CHUNK, SG_CHUNK), 0) >= _iota2((SG_CHUNK, SG_CHUNK), 1)


def _sg_forward_chunk(pu, pv, ln_g, ln_b, wc, bias, low):
    tu, tv = _gelu_tanh(pu), _gelu_tanh(pv)
    u = _gelu(pu, tu)
    v = _gelu(pv, tv)
    mu = jnp.mean(v, axis=-1, keepdims=True)
    vc = v - mu
    rs = lax.rsqrt(jnp.mean(vc * vc, axis=-1, keepdims=True) + EPS)
    xhat = vc * rs
    vn = (xhat * ln_g + ln_b).astype(BF16)
    parts = []
    for p in range(SG_PAIRS):
        vn_p = vn[:, _sg_pair_cols(p)]
        parts.append(jnp.where(low, _dot(wc[2 * p], vn_p, NN), _dot(wc[2 * p + 1], vn_p, NN)))
    vs = bias + jnp.concatenate(parts, axis=1)
    return u, xhat, rs, vn, vs, tu, tv


def sg_fwd(proj, ln_g, ln_b, w_s, bias_tile, name):
    n = proj.shape[0]
    tm = min(ROW_TILE, n)

    def body(pu_ref, pv_ref, g_ref, b_ref, w_ref, bias_ref, o_ref):
        causal = _sg_causal()
        wc = [jnp.where(causal, w_ref[g], 0.0).astype(BF16) for g in range(SG_GROUPS)]
        masks = _sg_low_half()
        for ci in range(tm // SG_CHUNK):
            rows = slice(ci * SG_CHUNK, (ci + 1) * SG_CHUNK)
            u, _, _, _, vs, _, _ = _sg_forward_chunk(pu_ref[rows, :], pv_ref[rows, :], g_ref[...],
                                                     b_ref[...], wc, bias_ref[...], masks)
            o_ref[rows, :] = (u * vs).astype(BF16)

    vec = pl.BlockSpec((1, HALF_W), lambda i: (0, 0))
    return _call(
        body, name=name, grid=(n // tm,),
        in_specs=[pl.BlockSpec((tm, HALF_W), lambda i: (i, 0)),
                  pl.BlockSpec((tm, HALF_W), lambda i: (i, 1)), vec, vec,
                  pl.BlockSpec((SG_GROUPS, SG_CHUNK, SG_CHUNK), lambda i: (0, 0, 0)),
                  pl.BlockSpec((SG_CHUNK, HALF_W), lambda i: (0, 0))],
        out_specs=pl.BlockSpec((tm, HALF_W), lambda i: (i, 0)),
        out_shape=_sds((n, HALF_W), BF16),
    )(proj, proj, ln_g, ln_b, w_s, bias_tile)


def sg_bwd(dsg, proj, ln_g, ln_b, w_s, bias_tile, name):
    n = proj.shape[0]
    tm = min(ROW_TILE, n)

    def body(d_ref, pu_ref, pv_ref, g_ref, b_ref, w_ref, bias_ref,
             dp_ref, dw_ref, db_ref, dlg_ref, dlb_ref):
        @pl.when(pl.program_id(0) == 0)
        def _():
            dw_ref[...] = jnp.zeros_like(dw_ref)
            db_ref[...] = jnp.zeros_like(db_ref)
            dlg_ref[...] = jnp.zeros_like(dlg_ref)
            dlb_ref[...] = jnp.zeros_like(dlb_ref)

        causal = _sg_causal()
        wc = [jnp.where(causal, w_ref[g], 0.0).astype(BF16) for g in range(SG_GROUPS)]
        masks = _sg_low_half()
        ln_g_v = g_ref[...]
        for ci in range(tm // SG_CHUNK):
            rows = slice(ci * SG_CHUNK, (ci + 1) * SG_CHUNK)
            pu = pu_ref[rows, :]
            pv = pv_ref[rows, :]
            u, xhat, rs, vn, vs, tu, tv = _sg_forward_chunk(pu, pv, ln_g_v, b_ref[...], wc,
                                                            bias_ref[...], masks)
            dout = d_ref[rows, :]
            dp_ref[rows, 0:HALF_W] = (dout * vs * _gelu_grad(pu, tu)).astype(BF16)
            dvs = dout * u
            dvs_b = dvs.astype(BF16)
            db_ref[...] += dvs
            dvn_parts = []
            for p in range(SG_PAIRS):
                dvs_p = dvs_b[:, _sg_pair_cols(p)]
                vn_p = vn[:, _sg_pair_cols(p)]
                dvn_parts.append(jnp.where(masks, _dot(wc[2 * p], dvs_p, TN), _dot(wc[2 * p + 1], dvs_p, TN)))
                zero = jnp.zeros_like(dvs_p)
                dw_ref[2 * p] += jnp.where(causal, _dot(jnp.where(masks, dvs_p, zero), vn_p, NT), 0.0)
                dw_ref[2 * p + 1] += jnp.where(causal, _dot(jnp.where(masks, zero, dvs_p), vn_p, NT), 0.0)
            dvn = jnp.concatenate(dvn_parts, axis=1)
            dlg_ref[...] += jnp.sum(dvn * xhat, axis=0, keepdims=True)
            dlb_ref[...] += jnp.sum(dvn, axis=0, keepdims=True)
            dxh = dvn * ln_g_v
            dv = rs * (dxh - jnp.mean(dxh, axis=-1, keepdims=True)
                       - xhat * jnp.mean(dxh * xhat, axis=-1, keepdims=True))
            dp_ref[rows, HALF_W:2 * HALF_W] = (dv * _gelu_grad(pv, tv)).astype(BF16)

    vec = pl.BlockSpec((1, HALF_W), lambda i: (0, 0))
    wspec = pl.BlockSpec((SG_GROUPS, SG_CHUNK, SG_CHUNK), lambda i: (0, 0, 0))
    tile = pl.BlockSpec((SG_CHUNK, HALF_W), lambda i: (0, 0))
    return _call(
        body, name=name, grid=(n // tm,),
        in_specs=[pl.BlockSpec((tm, HALF_W), lambda i: (i, 0)),
                  pl.BlockSpec((tm, HALF_W), lambda i: (i, 0)),
                  pl.BlockSpec((tm, HALF_W), lambda i: (i, 1)), vec, vec, wspec, tile],
        out_specs=[pl.BlockSpec((tm, 2 * HALF_W), lambda i: (i, 0)), wspec, tile, vec, vec],
        out_shape=[_sds((n, PROJ_W), BF16), _sds((SG_GROUPS, SG_CHUNK, SG_CHUNK), F32),
                   _sds((SG_CHUNK, HALF_W), F32), _sds((1, HALF_W), F32), _sds((1, HALF_W), F32)],
    )(dsg, proj, proj, ln_g, ln_b, w_s, bias_tile)


CONV_K = 4
CONV_BLOCK = 256


def _shift_down(x, s, row):
    if s == 0:
        return x
    return jnp.where(row >= s, pltpu.roll(x, s, 0), 0.0)


def _shift_up(x, s, row):
    if s == 0:
        return x
    t_len = x.shape[0]
    return jnp.where(row < t_len - s, pltpu.roll(x, t_len - s, 0), 0.0)


def _conv_taps(x, row):
    return [_shift_down(x, CONV_K - 1 - j, row) for j in range(CONV_K)]


def _conv(taps, w):
    y = taps[0] * w[0:1, :]
    for j in range(1, CONV_K):
        y = y + taps[j] * w[j:j + 1, :]
    return y


def dn_conv_fwd(proj3, conv_w, name):
    b, t, _ = proj3.shape
    nblk = 3 * HALF_W // CONV_BLOCK
    first = 2 * HALF_W // CONV_BLOCK
    n_norm = 2 * HALF_W // CONV_BLOCK

    def body(x_ref, w_ref, o_ref):
        s = pl.program_id(1)
        x = x_ref[0]
        y = _conv(_conv_taps(x, _iota2(x.shape, 0)), w_ref[...])
        y = y * _sigmoid(y)

        @pl.when(s < n_norm)
        def _():
            for h in range(CONV_BLOCK // HEAD_DIM):
                cs = slice(h * HEAD_DIM, (h + 1) * HEAD_DIM)
                yh = y[:, cs]
                o_ref[0, :, cs] = yh * lax.rsqrt(jnp.sum(yh * yh, axis=-1, keepdims=True) + EPS)

        @pl.when(s >= n_norm)
        def _():
            o_ref[0] = y

    return _call(
        body, name=name, grid=(b, nblk),
        in_specs=[pl.BlockSpec((1, t, CONV_BLOCK), lambda i, s: (i, 0, first + s)),
                  pl.BlockSpec((CONV_K, CONV_BLOCK), lambda i, s: (0, s))],
        out_specs=pl.BlockSpec((1, t, CONV_BLOCK), lambda i, s: (i, 0, s)),
        out_shape=_sds((b, t, 3 * HALF_W), F32),
    )(proj3, conv_w)


def dn_conv_bwd(dqkv, proj3, conv_w, dproj3, name, comm=None):
    b, t, _ = proj3.shape
    nblk = 3 * HALF_W // CONV_BLOCK
    first = 2 * HALF_W // CONV_BLOCK
    n_norm = 2 * HALF_W // CONV_BLOCK

    def body(d_ref, x_ref, w_ref, dproj_in, dx_ref, dw_ref, ds_ref):
        s = pl.program_id(0)

        @pl.when(pl.program_id(1) == 0)
        def _():
            dw_ref[...] = jnp.zeros_like(dw_ref)

        x = x_ref[0]
        w = w_ref[...]
        row = _iota2(x.shape, 0)
        taps = _conv_taps(x, row)
        c = _conv(taps, w)
        sg = _sigmoid(c)
        y = c * sg

        @pl.when(s < n_norm)
        def _():
            for h in range(CONV_BLOCK // HEAD_DIM):
                cs = slice(h * HEAD_DIM, (h + 1) * HEAD_DIM)
                yh = y[:, cs]
                r = lax.rsqrt(jnp.sum(yh * yh, axis=-1, keepdims=True) + EPS)
                nh = yh * r
                dn = d_ref[0, :, cs]
                ds_ref[:, cs] = r * (dn - nh * jnp.sum(dn * nh, axis=-1, keepdims=True))

        @pl.when(s >= n_norm)
        def _():
            ds_ref[...] = d_ref[0]

        dc = ds_ref[...] * (sg * (1.0 + c * (1.0 - sg)))
        dx = _shift_up(dc, CONV_K - 1, row) * w[0:1, :]
        for j in range(1, CONV_K):
            dx = dx + _shift_up(dc, CONV_K - 1 - j, row) * w[j:j + 1, :]
        dx_ref[0] = dx.astype(BF16)
        for j in range(CONV_K):
            dw_ref[j:j + 1, :] += jnp.sum(dc * taps[j], axis=0, keepdims=True)

    return _call(
        body, name=name, grid=(nblk, b),
        in_specs=[pl.BlockSpec((1, t, CONV_BLOCK), lambda s, i: (i, 0, s)),
                  pl.BlockSpec((1, t, CONV_BLOCK), lambda s, i: (i, 0, first + s)),
                  pl.BlockSpec((CONV_K, CONV_BLOCK), lambda s, i: (0, s)), _ANY],
        out_specs=[pl.BlockSpec((1, t, CONV_BLOCK), lambda s, i: (i, 0, first + s)),
                   pl.BlockSpec((CONV_K, CONV_BLOCK), lambda s, i: (0, s))],
        out_shape=[_sds(dproj3.shape, BF16), _sds((CONV_K, 3 * HALF_W), F32)],
        scratch=[pltpu.VMEM((t, CONV_BLOCK), F32)],
        input_output_aliases={3: 0}, comm=comm,
    )(dqkv, proj3, conv_w, dproj3)


def _chunk_masks():
    ii = _iota2((DN_CHUNK, DN_CHUNK), 0)
    jj = _iota2((DN_CHUNK, DN_CHUNK), 1)
    return ii >= jj, ii > jj, ii == jj


LOCKSTEP_CHUNKS = 4


def _inv_unit_lower_many(l_mats, eye):
    eye_f = jnp.where(eye, 1.0, 0.0)
    ps = [-l for l in l_mats]
    ts = [eye_f + p for p in ps]
    pss = [_split(p) for p in ps]
    size = 2
    while size < DN_CHUNK:
        ps = [_dot3(s, s) for s in pss]
        pss = [_split(p) for p in ps]
        ts = [t + _dot3(_split(t), s) for t, s in zip(ts, pss)]
        size *= 2
    return ts


def _gates(pba, ea_row, dtb_row):
    beta = _sigmoid(pba)
    g = -ea_row * _softplus(pba + dtb_row)
    return beta, g


def _chunk_decay(gcol):
    incl, strict, eye = _chunk_masks()
    grow = jnp.sum(jnp.where(eye, gcol, 0.0), axis=0, keepdims=True)
    decay = jnp.where(incl, jnp.exp(jnp.where(incl, gcol - grow, 0.0)), 0.0)
    return decay, incl, strict, eye


def dn_chunk_fwd(qkv, proj3, alog_row, dtb_row, name, comm=None):
    b, t, _ = qkv.shape
    rblk = min(256, t)
    n_in = rblk // DN_CHUNK

    def body(q_ref, k_ref, v_ref, pba_ref, al_ref, dtb_ref,
             u_ref, w_ref, qd_ref, kd_ref, qk_ref, ti_ref, gc_ref):
        ea = jnp.exp(al_ref[...])
        tri = jnp.where(_chunk_masks()[0], 1.0, 0.0)

        _, strict, eye = _chunk_masks()

        def chunk_group(cg, carry):
            items = []
            for sub in range(LOCKSTEP_CHUNKS):
                rows = pl.ds(pl.multiple_of((cg * LOCKSTEP_CHUNKS + sub) * DN_CHUNK, DN_CHUNK), DN_CHUNK)
                beta_all, g_all = _gates(pba_ref[0, rows, :], ea, dtb_ref[...])
                gc = _dot_exact_lhs(tri, g_all)
                gc_ref[0, rows, :] = gc
                for h in range(N_HEADS):
                    items.append((rows, h, beta_all[:, h:h + 1], gc[:, N_HEADS + h:N_HEADS + h + 1]))
            ks, kbs, decays, egs = [], [], [], []
            for rows, h, beta, gcol in items:
                cs = slice(h * HEAD_DIM, (h + 1) * HEAD_DIM)
                k = k_ref[0, rows, cs]
                ks.append(k)
                kbs.append(k * beta)
                decays.append(_chunk_decay(gcol)[0])
                egs.append(jnp.exp(gcol))
            ms = [_bdot(kb, k, NT) for kb, k in zip(kbs, ks)]
            tinvs = _inv_unit_lower_many([jnp.where(strict, m * dc, 0.0) for m, dc in zip(ms, decays)], eye)
            tsps = [_split(t) for t in tinvs]
            for (rows, h, beta, gcol), tsp, tinv in zip(items, tsps, tinvs):
                cs = slice(h * HEAD_DIM, (h + 1) * HEAD_DIM)
                u_ref[0, rows, cs] = _dot3(tsp, _split(v_ref[0, rows, cs] * beta))
                ti_ref[0, h, rows, :] = tinv
            for (rows, h, beta, gcol), tsp, kb, eg in zip(items, tsps, kbs, egs):
                cs = slice(h * HEAD_DIM, (h + 1) * HEAD_DIM)
                w_ref[0, rows, cs] = _dot3(tsp, _split(kb * eg))
            for (rows, h, beta, gcol), k, dc, eg in zip(items, ks, decays, egs):
                cs = slice(h * HEAD_DIM, (h + 1) * HEAD_DIM)
                q = q_ref[0, rows, cs] * QK_SCALE
                qk_ref[0, h, rows, :] = _bdot(q, k, NT) * dc
                qd_ref[0, rows, cs] = q * eg
                kd_ref[0, rows, cs] = k * jnp.exp(gcol[DN_CHUNK - 1:DN_CHUNK, :] - gcol)
            return carry

        lax.fori_loop(0, n_in // LOCKSTEP_CHUNKS, chunk_group, 0)

    def seg(cblk):
        return pl.BlockSpec((1, rblk, HALF_W), lambda i, r: (i, r, cblk))

    vec = pl.BlockSpec((1, LANES), lambda i, r: (0, 0))
    wide = pl.BlockSpec((1, rblk, HALF_W), lambda i, r: (i, r, 0))
    sq = pl.BlockSpec((1, N_HEADS, rblk, DN_CHUNK), lambda i, r: (i, 0, r, 0))
    return _call(
        body, name=name, grid=(b, t // rblk),
        in_specs=[seg(0), seg(1), seg(2),
                  pl.BlockSpec((1, rblk, LANES), lambda i, r: (i, r, GATE_COL_BLOCK)), vec, vec],
        out_specs=[wide, wide, wide, wide, sq, sq,
                   pl.BlockSpec((1, rblk, LANES), lambda i, r: (i, r, 0))],
        out_shape=[_sds((b, t, HALF_W), F32)] * 4
        + [_sds((b, N_HEADS, t, DN_CHUNK), F32)] * 2 + [_sds((b, t, LANES), F32)], comm=comm,
    )(qkv, qkv, qkv, proj3, alog_row, dtb_row)


def dn_scan_fwd(u, w, qd, kd, qk, gc, name):
    b, t, _ = u.shape
    nc = t // DN_CHUNK
    bh = b * N_HEADS

    def body(u_ref, w_ref, qd_ref, kd_ref, qk_ref, gc_ref, o_ref, sin_ref, s_ref):
        @pl.when(pl.program_id(0) == 0)
        def _():
            s_ref[...] = jnp.zeros_like(s_ref)

        items = [(bi, h, slice(h * HEAD_DIM, (h + 1) * HEAD_DIM)) for bi in range(b) for h in range(N_HEADS)]
        sbs = []
        for bi, h, cs in items:
            s = s_ref[bi * N_HEADS + h]
            sin_ref[0, bi * N_HEADS + h] = s
            sbs.append(s.astype(BF16))
        ws = [_bdot(w_ref[bi, :, cs], sb, NN) for (bi, h, cs), sb in zip(items, sbs)]
        qs = [_bdot(qd_ref[bi, :, cs], sb, NN) for (bi, h, cs), sb in zip(items, sbs)]
        vbs = [(u_ref[bi, :, cs] - wsi).astype(BF16) for (bi, h, cs), wsi in zip(items, ws)]
        for (bi, h, cs), qsi, vb in zip(items, qs, vbs):
            o_ref[bi, :, cs] = qsi + _bdot(qk_ref[bi, h], vb, NN)
        for (bi, h, cs), vb in zip(items, vbs):
            gl = jnp.exp(gc_ref[bi, DN_CHUNK - 1:DN_CHUNK, N_HEADS + h:N_HEADS + h + 1])
            idx = bi * N_HEADS + h
            s_ref[idx] = s_ref[idx] * gl + _bdot(kd_ref[bi, :, cs], vb, TN)

    wide = pl.BlockSpec((b, DN_CHUNK, HALF_W), lambda c: (0, c, 0))
    return _call(
        body, name=name, grid=(nc,),
        in_specs=[wide, wide, wide, wide,
                  pl.BlockSpec((b, N_HEADS, DN_CHUNK, DN_CHUNK), lambda c: (0, 0, c, 0)),
                  pl.BlockSpec((b, DN_CHUNK, LANES), lambda c: (0, c, 0))],
        out_specs=[wide, pl.BlockSpec((1, bh, HEAD_DIM, HEAD_DIM), lambda c: (c, 0, 0, 0))],
        out_shape=[_sds((b, t, HALF_W), F32), _sds((nc, bh, HEAD_DIM, HEAD_DIM), F32)],
        scratch=[pltpu.VMEM((bh, HEAD_DIM, HEAD_DIM), F32)],
    )(u, w, qd, kd, qk, gc)


def dn_scan_bwd(do, u, w, qd, kd, qk, gc, s_in, name):
    b, t, _ = u.shape
    nc = t // DN_CHUNK
    bh = b * N_HEADS

    def body(do_ref, u_ref, w_ref, qd_ref, kd_ref, qk_ref, gc_ref, sin_ref,
             du_ref, dw_ref, dqd_ref, dkd_ref, dqk_ref, dgc_ref, ds_ref):
        @pl.when(pl.program_id(0) == 0)
        def _():
            ds_ref[...] = jnp.zeros_like(ds_ref)

        last_row = _iota2((DN_CHUNK, LANES), 0) == DN_CHUNK - 1
        lane = _iota2((DN_CHUNK, LANES), 1)
        items = [(bi, h, slice(h * HEAD_DIM, (h + 1) * HEAD_DIM)) for bi in range(b) for h in range(N_HEADS)]
        sbs = [sin_ref[0, bi * N_HEADS + h].astype(BF16) for bi, h, cs in items]
        wvs = [w_ref[bi, :, cs].astype(BF16) for bi, h, cs in items]
        dovs = [do_ref[bi, :, cs].astype(BF16) for bi, h, cs in items]
        dsbs = [ds_ref[bi * N_HEADS + h].astype(BF16) for bi, h, cs in items]
        vbs = [(u_ref[bi, :, cs] - _dot(wv, sb, NN)).astype(BF16)
               for (bi, h, cs), wv, sb in zip(items, wvs, sbs)]
        for (bi, h, cs), dov, sb in zip(items, dovs, sbs):
            dqd_ref[bi, :, cs] = _dot(dov, sb, NT)
        dvns = [_dot(kd_ref[bi, :, cs].astype(BF16), dsb, NN) + _dot(qk_ref[bi, h].astype(BF16), dov, TN)
                for (bi, h, cs), dsb, dov in zip(items, dsbs, dovs)]
        for (bi, h, cs), vb, dsb, dov in zip(items, vbs, dsbs, dovs):
            dkd_ref[bi, :, cs] = _dot(vb, dsb, NT)
            dqk_ref[bi, h] = _dot(dov, vb, NT)
        dgls = []
        for (bi, h, cs), dvn, sb, wv, dov in zip(items, dvns, sbs, wvs, dovs):
            idx = bi * N_HEADS + h
            du_ref[bi, :, cs] = dvn
            dvn_b = dvn.astype(BF16)
            dw_ref[bi, :, cs] = -_dot(dvn_b, sb, NT)
            gl = jnp.exp(gc_ref[bi, DN_CHUNK - 1:DN_CHUNK, N_HEADS + h:N_HEADS + h + 1])
            ds = ds_ref[idx]
            dgl = jnp.sum(jnp.sum(ds * sin_ref[0, idx], axis=1, keepdims=True), axis=0, keepdims=True)
            dgls.append(dgl * gl)
            ds_ref[idx] = (ds * gl + _dot(qd_ref[bi, :, cs].astype(BF16), dov, TN)
                           - _dot(wv, dvn_b, TN))
        for bi in range(b):
            dgc = jnp.zeros((DN_CHUNK, LANES), F32)
            for h in range(N_HEADS):
                dgc = dgc + jnp.where(jnp.logical_and(last_row, lane == N_HEADS + h),
                                      dgls[bi * N_HEADS + h], 0.0)
            dgc_ref[bi] = dgc

    def rev(c):
        return nc - 1 - c

    wide = pl.BlockSpec((b, DN_CHUNK, HALF_W), lambda c: (0, rev(c), 0))
    sq = pl.BlockSpec((b, N_HEADS, DN_CHUNK, DN_CHUNK), lambda c: (0, 0, rev(c), 0))
    gates = pl.BlockSpec((b, DN_CHUNK, LANES), lambda c: (0, rev(c), 0))
    return _call(
        body, name=name, grid=(nc,),
        in_specs=[wide, wide, wide, wide, wide, sq, gates,
                  pl.BlockSpec((1, bh, HEAD_DIM, HEAD_DIM), lambda c: (rev(c), 0, 0, 0))],
        out_specs=[wide, wide, wide, wide, sq, gates],
        out_shape=[_sds((b, t, HALF_W), F32)] * 4
        + [_sds((b, N_HEADS, t, DN_CHUNK), F32), _sds((b, t, LANES), F32)],
        scratch=[pltpu.VMEM((bh, HEAD_DIM, HEAD_DIM), F32)],
    )(do, u, w, qd, kd, qk, gc, s_in)


def dn_chunk_bwd(qkv, proj3, alog_row, dtb_row, tinv, u, w, du, dw, dqd, dkd, dqk, dgc_scan, dproj3, name,
                 comm=None):
    b, t, _ = qkv.shape
    rblk = min(256, t)
    n_in = rblk // DN_CHUNK

    def body(q_ref, k_ref, v_ref, pba_ref, al_ref, dtb_ref, ti_ref, u_ref, w_ref,
             du_ref, dw_ref, dqd_ref, dkd_ref, dqk_ref, dgs_ref, dproj_in,
             dq_ref, dpba_ref, dal_ref, ddtb_ref):
        @pl.when(jnp.logical_and(pl.program_id(0) == 0, pl.program_id(1) == 0))
        def _():
            dal_ref[...] = jnp.zeros_like(dal_ref)
            ddtb_ref[...] = jnp.zeros_like(ddtb_ref)

        ea = jnp.exp(al_ref[...])
        incl0 = _chunk_masks()[0]
        tri = jnp.where(incl0, 1.0, 0.0)
        tri_up = jnp.where(_iota2((DN_CHUNK, DN_CHUNK), 1) >= _iota2((DN_CHUNK, DN_CHUNK), 0), 1.0, 0.0)
        lane = _iota2((DN_CHUNK, LANES), 1)
        last_col = _iota2((DN_CHUNK, 1), 0) == DN_CHUNK - 1

        _, strict, _ = _chunk_masks()
        gate_lane = jnp.logical_and(lane >= N_HEADS, lane < 2 * N_HEADS)

        def chunk_group(cg, carry):
            tiles, items = [], []
            for sub in range(LOCKSTEP_CHUNKS):
                rows = pl.ds(pl.multiple_of((cg * LOCKSTEP_CHUNKS + sub) * DN_CHUNK, DN_CHUNK), DN_CHUNK)
                pba = pba_ref[0, rows, :]
                beta_all, g_all = _gates(pba, ea, dtb_ref[...])
                gc = _dot_exact_lhs(tri, g_all)
                tiles.append((rows, pba, beta_all, g_all))
                for h in range(N_HEADS):
                    items.append((sub, rows, h, slice(h * HEAD_DIM, (h + 1) * HEAD_DIM),
                                  beta_all[:, h:h + 1], gc[:, N_HEADS + h:N_HEADS + h + 1]))
            decays = [_chunk_decay(gcol)[0] for _, _, _, _, _, gcol in items]
            egs = [jnp.exp(gcol) for _, _, _, _, _, gcol in items]
            qbs = [(q_ref[0, rows, cs] * QK_SCALE).astype(BF16) for _, rows, h, cs, _, _ in items]
            kfs = [k_ref[0, rows, cs].astype(BF16) for _, rows, h, cs, _, _ in items]
            kbs = [k_ref[0, rows, cs] * beta for _, rows, h, cs, beta, _ in items]
            kbbs = [kb.astype(BF16) for kb in kbs]
            tsps = [_split(ti_ref[0, h, rows, :]) for _, rows, h, cs, _, _ in items]
            drus = [_dot3(tsp, _split(du_ref[0, rows, cs]), TN)
                    for (_, rows, h, cs, _, _), tsp in zip(items, tsps)]
            drws = [_dot3(tsp, _split(dw_ref[0, rows, cs]), TN)
                    for (_, rows, h, cs, _, _), tsp in zip(items, tsps)]
            m_kks = [_dot(kbb, kf, NT) for kbb, kf in zip(kbbs, kfs)]
            a_qks = [_dot(qb, kf, NT) for qb, kf in zip(qbs, kfs)]
            dls = [-jnp.where(strict, _dot3(_split(dru), _split(u_ref[0, rows, cs]), NT)
                              + _dot3(_split(drw), _split(w_ref[0, rows, cs]), NT), 0.0)
                   for (_, rows, h, cs, _, _), dru, drw in zip(items, drus, drws)]
            dms = [(dl * dc).astype(BF16) for dl, dc in zip(dls, decays)]
            das = [(dqk_ref[0, h, rows, :] * dc).astype(BF16)
                   for (_, rows, h, cs, _, _), dc in zip(items, decays)]
            dkb_mm = [_dot(dm, kf, NN) for dm, kf in zip(dms, kfs)]
            dk_mm = [_dot(dm, kbb, TN) + _dot(da, qb, TN) for dm, kbb, da, qb in zip(dms, kbbs, das, qbs)]
            dqs_mm = [_dot(da, kf, NN) for da, kf in zip(das, kfs)]
            dgc_tiles = [dgs_ref[0, rows, :] for rows, _, _, _ in tiles]
            dbeta_tiles = [jnp.zeros((DN_CHUNK, LANES), F32) for _ in tiles]
            for n_it, (sub, rows, h, cs, beta, gcol) in enumerate(items):
                eg, dc = egs[n_it], decays[n_it]
                k = k_ref[0, rows, cs]
                q = q_ref[0, rows, cs] * QK_SCALE
                kb, dru, drw = kbs[n_it], drus[n_it], drws[n_it]
                ek = jnp.exp(gcol[DN_CHUNK - 1:DN_CHUNK, :] - gcol)
                e_mat = (dls[n_it] * m_kks[n_it] + dqk_ref[0, h, rows, :] * a_qks[n_it]) * dc
                dkb = drw * eg + dkb_mm[n_it]
                dqd = dqd_ref[0, rows, cs]
                dkd = dkd_ref[0, rows, cs]
                kdk = dkd * k * ek
                kdk_total = jnp.sum(jnp.sum(kdk, axis=0, keepdims=True), axis=1, keepdims=True)
                dg = (jnp.sum(drw * kb * eg + dqd * q * eg - kdk, axis=-1, keepdims=True)
                      + jnp.sum(e_mat, axis=1, keepdims=True)
                      - _row_to_col(jnp.sum(e_mat, axis=0, keepdims=True))
                      + jnp.where(last_col, kdk_total, 0.0))
                dbeta = jnp.sum(dkb * k + dru * v_ref[0, rows, cs], axis=-1, keepdims=True)
                dq_ref[0, rows, cs] = (dqs_mm[n_it] + dqd * eg) * QK_SCALE
                dq_ref[0, rows, pl.ds(HALF_W + h * HEAD_DIM, HEAD_DIM)] = dk_mm[n_it] + dkd * ek + dkb * beta
                dq_ref[0, rows, pl.ds(2 * HALF_W + h * HEAD_DIM, HEAD_DIM)] = dru * beta
                dgc_tiles[sub] = dgc_tiles[sub] + jnp.where(lane == N_HEADS + h, dg, 0.0)
                dbeta_tiles[sub] = dbeta_tiles[sub] + jnp.where(lane == h, dbeta, 0.0)
            for (rows, pba, beta_all, g_all), dgc_tile, dbeta_tile in zip(tiles, dgc_tiles, dbeta_tiles):
                dg_tile = _dot_exact_lhs(tri_up, dgc_tile)
                da_pre = dg_tile * (-ea) * _sigmoid(pba + dtb_ref[...])
                dal_ref[...] += jnp.sum(jnp.where(gate_lane, dg_tile * g_all, 0.0), axis=0, keepdims=True)
                ddtb_ref[...] += jnp.sum(jnp.where(gate_lane, da_pre, 0.0), axis=0, keepdims=True)
                dpba_ref[0, rows, :] = jnp.where(lane < N_HEADS, dbeta_tile * beta_all * (1.0 - beta_all),
                                                 jnp.where(gate_lane, da_pre, 0.0)).astype(BF16)
            return carry

        lax.fori_loop(0, n_in // LOCKSTEP_CHUNKS, chunk_group, 0)

    def seg(cblk):
        return pl.BlockSpec((1, rblk, HALF_W), lambda i, r: (i, r, cblk))

    vec = pl.BlockSpec((1, LANES), lambda i, r: (0, 0))
    wide = pl.BlockSpec((1, rblk, HALF_W), lambda i, r: (i, r, 0))
    sq = pl.BlockSpec((1, N_HEADS, rblk, DN_CHUNK), lambda i, r: (i, 0, r, 0))
    gates = pl.BlockSpec((1, rblk, LANES), lambda i, r: (i, r, 0))
    return _call(
        body, name=name, grid=(b, t // rblk),
        in_specs=[seg(0), seg(1), seg(2),
                  pl.BlockSpec((1, rblk, LANES), lambda i, r: (i, r, GATE_COL_BLOCK)), vec, vec,
                  sq, wide, wide, wide, wide, wide, wide, sq, gates, _ANY],
        out_specs=[pl.BlockSpec((1, rblk, 3 * HALF_W), lambda i, r: (i, r, 0)),
                   pl.BlockSpec((1, rblk, LANES), lambda i, r: (i, r, GATE_COL_BLOCK)), vec, vec],
        out_shape=[_sds((b, t, 3 * HALF_W), F32), _sds(dproj3.shape, BF16),
                   _sds((1, LANES), F32), _sds((1, LANES), F32)],
        input_output_aliases={15: 1}, comm=comm,
    )(qkv, qkv, qkv, proj3, alog_row, dtb_row, tinv, u, w, du, dw, dqd, dkd, dqk, dgc_scan, dproj3)


def dn_out_fwd(o, proj, dn_norm, name):
    n = o.shape[0]
    tm = min(ROW_TILE, n)

    def body(o_ref, z_ref, g_ref, y_ref):
        for h in range(N_HEADS):
            cs = slice(h * HEAD_DIM, (h + 1) * HEAD_DIM)
            oh = o_ref[:, cs]
            z = z_ref[:, cs]
            y = oh * _rms_scale(oh) * g_ref[...]
            y_ref[:, cs] = (y * (z * _sigmoid(z))).astype(BF16)

    half = pl.BlockSpec((tm, HALF_W), lambda i: (i, 0))
    return _call(
        body, name=name, grid=(n // tm,),
        in_specs=[half, pl.BlockSpec((tm, HALF_W), lambda i: (i, 5)),
                  pl.BlockSpec((1, HEAD_DIM), lambda i: (0, 0))],
        out_specs=half, out_shape=_sds((n, HALF_W), BF16),
    )(o, proj, dn_norm)


def dn_out_bwd(dy, o, proj, dn_norm, dproj, name):
    n = o.shape[0]
    tm = min(ROW_TILE, n)

    def body(dy_ref, o_ref, z_ref, g_ref, dproj_in, do_ref, dz_ref, dg_ref):
        @pl.when(pl.program_id(0) == 0)
        def _():
            dg_ref[...] = jnp.zeros_like(dg_ref)

        g = g_ref[...]
        dg = jnp.zeros_like(g)
        for h in range(N_HEADS):
            cs = slice(h * HEAD_DIM, (h + 1) * HEAD_DIM)
            oh = o_ref[:, cs]
            z = z_ref[:, cs]
            d = dy_ref[:, cs]
            r = _rms_scale(oh)
            nh = oh * r
            sz = _sigmoid(z)
            dyn = d * (z * sz)
            dz_ref[:, cs] = (d * (nh * g) * (sz * (1.0 + z * (1.0 - sz)))).astype(BF16)
            dg = dg + jnp.sum(dyn * nh, axis=0, keepdims=True)
            dn = dyn * g
            do_ref[:, cs] = r * (dn - nh * jnp.mean(dn * nh, axis=-1, keepdims=True))
        dg_ref[...] += dg

    half = pl.BlockSpec((tm, HALF_W), lambda i: (i, 0))
    vec = pl.BlockSpec((1, HEAD_DIM), lambda i: (0, 0))
    return _call(
        body, name=name, grid=(n // tm,),
        in_specs=[half, half, pl.BlockSpec((tm, HALF_W), lambda i: (i, 5)), vec, _ANY],
        out_specs=[half, pl.BlockSpec((tm, HALF_W), lambda i: (i, 5)), vec],
        out_shape=[_sds((n, HALF_W), F32), _sds(dproj.shape, BF16), _sds((1, HEAD_DIM), F32)],
        input_output_aliases={4: 1},
    )(dy, o, proj, dn_norm, dproj)


def _adamw_math(w, g, m, v):
    m_new = ADAM_B1 * m + (1.0 - ADAM_B1) * g
    v_new = ADAM_B2 * v + (1.0 - ADAM_B2) * (g * g)
    m_hat = m_new / (1.0 - ADAM_B1 ** ADAM_STEP)
    v_hat = v_new / (1.0 - ADAM_B2 ** ADAM_STEP)
    delta = -ADAM_LR * (m_hat / (jnp.sqrt(v_hat) + ADAM_EPS) + ADAM_WD * w)
    return delta, m_new, v_new


def adamw(w, g, m, v, name):
    r, c = w.shape
    tr = r
    for cand in (256, 352):
        if r % cand == 0 and r > cand:
            tr = cand
            break

    def body(w_ref, g_ref, m_ref, v_ref, d_ref, mo_ref, vo_ref):
        d, mn, vn = _adamw_math(w_ref[...], g_ref[...], m_ref[...], v_ref[...])
        d_ref[...] = d
        mo_ref[...] = mn
        vo_ref[...] = vn

    spec = pl.BlockSpec((tr, c), lambda i: (i, 0))
    return _call(
        body, name=name, grid=(r // tr,),
        in_specs=[spec] * 4, out_specs=[spec] * 3, out_shape=[_sds((r, c), F32)] * 3,
    )(w, g, m, v)


def _place():
    return lax.axis_index("x"), lax.axis_index("y"), lax.axis_index("c")


def _other_chips(x, y):
    return [(1 - x, y), (x, 1 - y), (1 - x, 1 - y)]


_ANY = pl.BlockSpec(memory_space=pl.ANY)


def cast_place(w, shard_idx, name):
    r, cols = w.shape
    tr = r // 2

    def body(j_ref, w_ref, o_ref):
        o_ref[0] = w_ref[...].astype(BF16)

    return pl.pallas_call(
        body, name=name,
        grid_spec=pltpu.PrefetchScalarGridSpec(
            num_scalar_prefetch=1, grid=(r // tr,),
            in_specs=[pl.BlockSpec((tr, cols), lambda i, j: (i, 0))],
            out_specs=pl.BlockSpec((1, tr, cols), lambda i, j: (j[0], i, 0))),
        out_shape=_sds((N_SHARD, r, cols), BF16),
        compiler_params=pltpu.CompilerParams(dimension_semantics=("arbitrary",),
                                             vmem_limit_bytes=VMEM_LIMIT),
    )(shard_idx, w)


class Exchange:
    def __init__(self, inputs, out_shape, aliases, sems, phases):
        self.inputs, self.out_shape, self.aliases = list(inputs), list(out_shape), dict(aliases)
        self.sems, self.phases = list(sems), list(phases)


def run_exchange(ex, name):
    def body(*refs):
        n_in, n_out = len(ex.inputs), len(ex.out_shape)
        for _, fn in ex.phases:
            fn(refs[:n_in], refs[n_in:n_in + n_out], refs[n_in + n_out:])

    return _call(body, name=name, in_specs=[_ANY] * len(ex.inputs), out_specs=[_ANY] * len(ex.out_shape),
                 out_shape=ex.out_shape, scratch=ex.sems, input_output_aliases=ex.aliases)(*ex.inputs)


def merge_exchanges(exs):
    inputs, out_shape, sems, aliases, phases, out_slices = [], [], [], {}, [], []
    for ex in exs:
        i0, o0, s0 = len(inputs), len(out_shape), len(sems)
        inputs += ex.inputs
        out_shape += ex.out_shape
        sems += ex.sems
        for k, m in ex.aliases.items():
            aliases[i0 + k] = o0 + m
        si, so, ss = slice(i0, len(inputs)), slice(o0, len(out_shape)), slice(s0, len(sems))
        out_slices.append(so)
        for step, fn in ex.phases:
            phases.append((step, lambda ins, outs, sm, fn=fn, si=si, so=so, ss=ss: fn(ins[si], outs[so], sm[ss])))
    return Exchange(inputs, out_shape, aliases, sems, phases), out_slices


def _dma_sems(*sizes):
    return [pltpu.SemaphoreType.DMA((s,)) for s in sizes]


def gather_exchange(bufs, small=None, relay_step=-2):
    n = len(bufs)
    n_small = 0 if small is None else 1

    def half(outs, a, blk, hc):
        rh = bufs[a].shape[1] // 2
        return outs[a].at[blk, pl.ds(hc * rh, rh), :]

    def ici(outs, sems, a, k, blk, to):
        return pltpu.make_async_remote_copy(
            src_ref=half(outs, a, blk, to[2]), dst_ref=half(outs, a, blk, to[2]), send_sem=sems[0].at[3 * a + k],
            recv_sem=sems[1].at[3 * a + k], device_id=to, device_id_type=MESH)

    def d2d(outs, sems, a, k, blk, hc, to):
        return pltpu.make_async_remote_copy(
            src_ref=half(outs, a, blk, hc), dst_ref=half(outs, a, blk, hc), send_sem=sems[2].at[3 * a + k],
            recv_sem=sems[3].at[3 * a + k], device_id=to, device_id_type=MESH)

    def small_copy(ins, outs, sems, k, blk, to):
        return pltpu.make_async_remote_copy(
            src_ref=ins[n], dst_ref=outs[n].at[blk], send_sem=sems[0].at[3 * n + k],
            recv_sem=sems[1].at[3 * n + k], device_id=to, device_id_type=MESH)

    def start(ins, outs, sems):
        x, y, c = _place()
        j = 2 * x + y
        if n_small:
            pltpu.make_async_copy(ins[n], outs[n].at[j], sems[4].at[0]).start()
        for k, (px, py) in enumerate(_other_chips(x, y)):
            if n_small:
                small_copy(ins, outs, sems, k, j, (px, py, c)).start()
            for a in range(n):
                ici(outs, sems, a, k, j, (px, py, c)).start()

    def relay(ins, outs, sems):
        x, y, c = _place()
        for k, (px, py) in enumerate(_other_chips(x, y)):
            for a in range(n):
                ici(outs, sems, a, k, 2 * px + py, (px, py, c)).wait_recv()
                d2d(outs, sems, a, k, 2 * px + py, c, (x, y, 1 - c)).start()

    def finish(ins, outs, sems):
        x, y, c = _place()
        j = 2 * x + y
        for k, (px, py) in enumerate(_other_chips(x, y)):
            blk = 2 * px + py
            if n_small:
                small_copy(ins, outs, sems, k, blk, (px, py, c)).wait_recv()
                small_copy(ins, outs, sems, k, j, (px, py, c)).wait_send()
            for a in range(n):
                d2d(outs, sems, a, k, blk, 1 - c, (x, y, 1 - c)).wait_recv()
                ici(outs, sems, a, k, j, (px, py, c)).wait_send()
                d2d(outs, sems, a, k, blk, c, (x, y, 1 - c)).wait_send()
        if n_small:
            pltpu.make_async_copy(ins[n], outs[n].at[j], sems[4].at[0]).wait()

    out_shape = [_sds(b.shape, b.dtype) for b in bufs]
    if n_small:
        out_shape.append(_sds((N_SHARD,) + small.shape, small.dtype))
    return Exchange(list(bufs) + ([small] if n_small else []), out_shape, {a: a for a in range(n)},
                    _dma_sems(3 * n + 3, 3 * n + 3, 3 * n, 3 * n, 1),
                    [(0, start), (relay_step, relay), (-1, finish)])


def _start_then_wait(copies):
    def start(ins, outs, sems):
        for sent, _ in copies(ins, outs, sems):
            sent().start()

    def finish(ins, outs, sems):
        pairs = copies(ins, outs, sems)
        for _, arrival in pairs:
            arrival().wait_recv()
        for sent, _ in pairs:
            sent().wait_send()

    return [(0, start), (-1, finish)]


def pair_exchange(arrs):
    n = len(arrs)

    def copies(ins, outs, sems):
        x, y, c = _place()
        res = []
        for a in range(n):
            def mk(a=a):
                rh = arrs[a].shape[1] // 2
                return pltpu.make_async_remote_copy(
                    src_ref=ins[a].at[:, pl.ds((1 - c) * rh, rh), :], dst_ref=outs[a], send_sem=sems[0].at[a],
                    recv_sem=sems[1].at[a], device_id=(x, y, 1 - c), device_id_type=MESH)
            res.append((mk, mk))
        return res

    return Exchange(arrs, [_sds((a.shape[0], a.shape[1] // 2, a.shape[2]), a.dtype) for a in arrs], {},
                    _dma_sems(n, n), _start_then_wait(copies))


def pair_add(g, s, c_idx, name):
    nb, r, cols = g.shape
    rh = r // 2

    def body(c_ref, g_ref, s_ref, o_ref):
        o_ref[...] = (g_ref[...] + s_ref[...]).astype(BF16)

    return pl.pallas_call(
        body, name=name,
        grid_spec=pltpu.PrefetchScalarGridSpec(
            num_scalar_prefetch=1, grid=(nb,),
            in_specs=[pl.BlockSpec((1, rh, cols), lambda j, c: (j, c[0], 0)),
                      pl.BlockSpec((1, rh, cols), lambda j, c: (j, 0, 0))],
            out_specs=pl.BlockSpec((1, rh, cols), lambda j, c: (j, 0, 0))),
        out_shape=_sds((nb, rh, cols), BF16),
        compiler_params=pltpu.CompilerParams(dimension_semantics=("arbitrary",),
                                             vmem_limit_bytes=VMEM_LIMIT),
    )(c_idx, g, s)


def chip_exchange(arrs):
    n = len(arrs)

    def copies(ins, outs, sems):
        x, y, c = _place()
        j = 2 * x + y
        res = []
        for a in range(n):
            for k, (px, py) in enumerate(_other_chips(x, y)):
                def mk(src_blk, dst_blk, a=a, k=k, to=(px, py, c)):
                    return pltpu.make_async_remote_copy(
                        src_ref=ins[a].at[src_blk], dst_ref=outs[a].at[dst_blk], send_sem=sems[0].at[3 * a + k],
                        recv_sem=sems[1].at[3 * a + k], device_id=to, device_id_type=MESH)
                res.append((functools.partial(mk, 2 * px + py, j), functools.partial(mk, j, 2 * px + py)))
        return res

    return Exchange(arrs, [_sds(a.shape, a.dtype) for a in arrs], {}, _dma_sems(3 * n, 3 * n),
                    _start_then_wait(copies))


def sum_chips(r, p, shard_idx, name):
    nb, rh, cols = r.shape
    tr = rh

    def body(j_ref, p_ref, *refs):
        o_ref = refs[nb]
        j = j_ref[0]
        acc = None
        for i in range(nb):
            term = jnp.where(j == i, p_ref[0], refs[i][0]).astype(F32)
            acc = term if acc is None else acc + term
        o_ref[...] = acc

    def slot(i):
        return pl.BlockSpec((1, tr, cols), lambda t, j: (jnp.where(j[0] == i, (i + 1) % nb, i), t, 0))

    return pl.pallas_call(
        body, name=name,
        grid_spec=pltpu.PrefetchScalarGridSpec(
            num_scalar_prefetch=1, grid=(rh // tr,),
            in_specs=[pl.BlockSpec((1, tr, cols), lambda t, j: (j[0], t, 0))] + [slot(i) for i in range(nb)],
            out_specs=pl.BlockSpec((tr, cols), lambda t, j: (t, 0))),
        out_shape=_sds((rh, cols), F32),
        compiler_params=pltpu.CompilerParams(dimension_semantics=("arbitrary",),
                                             vmem_limit_bytes=VMEM_LIMIT),
    )(shard_idx, p, *([r] * nb))


def pair_swap(arrs):
    n = len(arrs)

    def copies(ins, outs, sems):
        x, y, c = _place()
        res = []
        for a in range(n):
            def mk(a=a):
                return pltpu.make_async_remote_copy(
                    src_ref=ins[a], dst_ref=outs[a], send_sem=sems[0].at[a], recv_sem=sems[1].at[a],
                    device_id=(x, y, 1 - c), device_id_type=MESH)
            res.append((mk, mk))
        return res

    return Exchange(arrs, [_sds(a.shape, a.dtype) for a in arrs], {}, _dma_sems(n, n),
                    _start_then_wait(copies))


ADAMW_STEPS_PER_HALF = 4


def adamw_pairs(items, name, comm=None):
    n_items = len(items)
    nh = ADAMW_STEPS_PER_HALF

    def body(*refs):
        ins, outs = refs[:5 * n_items], refs[5 * n_items:]
        mine = (pl.program_id(0) // nh) == lax.axis_index("c")
        for a in range(n_items):
            w_ref, gm_ref, gs_ref, m_ref, v_ref = ins[5 * a:5 * a + 5]
            g_ref, d_ref, mo_ref, vo_ref = outs[4 * a:4 * a + 4]
            g = jnp.where(mine, gm_ref[...], gs_ref[...])
            d, mn, vn = _adamw_math(w_ref[...], g, m_ref[...], v_ref[...])
            g_ref[...] = g
            d_ref[...] = d
            mo_ref[...] = mn
            vo_ref[...] = vn

    in_specs, out_specs, out_shape, args = [], [], [], []
    for w, g_mine, g_sib, m, v in items:
        r, cols = w.shape
        tr = r // (2 * nh)
        full = pl.BlockSpec((tr, cols), lambda i: (i, 0))
        part = pl.BlockSpec((tr, cols), lambda i: (i % nh, 0))
        in_specs += [full, part, part, full, full]
        out_specs += [full] * 4
        out_shape += [_sds((r, cols), F32)] * 4
        args += [w, g_mine, g_sib, m, v]
    res = _call(body, name=name, grid=(2 * nh,), in_specs=in_specs, out_specs=out_specs,
                out_shape=out_shape, comm=comm)(*args)
    own, hosted = (res, None) if comm is None else res
    grouped = [tuple(own[4 * a:4 * a + 4]) for a in range(n_items)]
    return grouped if comm is None else (grouped, hosted)


N_DEV = 8


def device_gather(pack):
    def copies(ins, outs, sems):
        x, y, c = _place()
        me = 4 * x + 2 * y + c
        res = []
        for k in range(1, N_DEV):
            fx, fy, fc = (k >> 2) & 1, (k >> 1) & 1, k & 1
            px, py, pc = (1 - x if fx else x, 1 - y if fy else y, 1 - c if fc else c)

            def mk(slot, k=k, to=(px, py, pc)):
                return pltpu.make_async_remote_copy(
                    src_ref=ins[0], dst_ref=outs[0].at[slot], send_sem=sems[0].at[k - 1],
                    recv_sem=sems[1].at[k - 1], device_id=to, device_id_type=MESH)
            res.append((functools.partial(mk, me), functools.partial(mk, 4 * px + 2 * py + pc)))
        return res

    return Exchange([pack], [_sds((N_DEV,) + pack.shape, pack.dtype)], {}, _dma_sems(N_DEV - 1, N_DEV - 1),
                    _start_then_wait(copies))


def sum_devices(buf, pack, me_idx, name):
    r, cols = pack.shape

    def body(me_ref, p_ref, *refs):
        o_ref = refs[N_DEV]
        acc = None
        for i in range(N_DEV):
            term = jnp.where(me_ref[0] == i, p_ref[...], refs[i][0])
            acc = term if acc is None else acc + term
        o_ref[...] = acc

    def slot(i):
        return pl.BlockSpec((1, r, cols), lambda t, me: (jnp.where(me[0] == i, (i + 1) % N_DEV, i), 0, 0))

    whole = pl.BlockSpec((r, cols), lambda t, me: (0, 0))
    return pl.pallas_call(
        body, name=name,
        grid_spec=pltpu.PrefetchScalarGridSpec(
            num_scalar_prefetch=1, grid=(1,),
            in_specs=[whole] + [slot(i) for i in range(N_DEV)], out_specs=whole),
        out_shape=_sds((r, cols), F32),
        compiler_params=pltpu.CompilerParams(dimension_semantics=("arbitrary",),
                                             vmem_limit_bytes=VMEM_LIMIT),
    )(me_idx, pack, *([buf] * N_DEV))


SMALL_NAMES = ("ffn1_norm", "mix_norm", "ffn2_norm", "final_norm", "sg_ln_g", "sg_ln_b",
               "dn_norm", "a_log", "dt_bias", "sg_b", "sg_w", "conv_w", "loss")


def _to_rows(a):
    flat = a.reshape(-1)
    pad = (-flat.shape[0]) % LANES
    if pad:
        flat = jnp.pad(flat, (0, pad))
    return flat.reshape(-1, LANES)


def _pack_small(parts):
    rows = [_to_rows(parts[k]) for k in SMALL_NAMES]
    pack = jnp.concatenate(rows, axis=0)
    pad = (-pack.shape[0]) % 8
    if pad:
        pack = jnp.pad(pack, ((0, pad), (0, 0)))
    return pack


def _unpack_small(pack, shapes):
    out, r0 = {}, 0
    for k in SMALL_NAMES:
        size = 1
        for s in shapes[k]:
            size *= s
        nrows = -(-size // LANES)
        out[k] = pack[r0:r0 + nrows].reshape(-1)[:size].reshape(shapes[k])
        r0 += nrows
    return out


def kernel(x, ffn1_norm, ffn1_w_gate, ffn1_w_up, ffn1_w_down, mix_norm, w_in, conv_w, a_log, dt_bias, dn_norm, sg_ln_g, sg_ln_b, sg_w, sg_b, w_out, ffn2_norm, ffn2_w_gate, ffn2_w_up, ffn2_w_down, final_norm, loss_target, m_ffn1_norm, m_ffn1_w_gate, m_ffn1_w_up, m_ffn1_w_down, m_mix_norm, m_w_in, m_conv_w, m_a_log, m_dt_bias, m_dn_norm, m_sg_ln_g, m_sg_ln_b, m_sg_w, m_sg_b, m_w_out, m_ffn2_norm, m_ffn2_w_gate, m_ffn2_w_up, m_ffn2_w_down, m_final_norm, v_ffn1_norm, v_ffn1_w_gate, v_ffn1_w_up, v_ffn1_w_down, v_mix_norm, v_w_in, v_conv_w, v_a_log, v_dt_bias, v_dn_norm, v_sg_ln_g, v_sg_ln_b, v_sg_w, v_sg_b, v_w_out, v_ffn2_norm, v_ffn2_w_gate, v_ffn2_w_up, v_ffn2_w_down, v_final_norm):
    bsz, t_len, d = x.shape
    n = bsz * t_len
    xy, yy, cc = _place()
    shard = 2 * xy + yy

    big_names = ["ffn1_w_gate", "ffn1_w_up", "ffn1_w_down", "w_in", "w_out",
                 "ffn2_w_gate", "ffn2_w_up", "ffn2_w_down"]
    big_w = dict(ffn1_w_gate=ffn1_w_gate, ffn1_w_up=ffn1_w_up, ffn1_w_down=ffn1_w_down, w_in=w_in,
                 w_out=w_out, ffn2_w_gate=ffn2_w_gate, ffn2_w_up=ffn2_w_up, ffn2_w_down=ffn2_w_down)
    big_m = dict(ffn1_w_gate=m_ffn1_w_gate, ffn1_w_up=m_ffn1_w_up, ffn1_w_down=m_ffn1_w_down, w_in=m_w_in,
                 w_out=m_w_out, ffn2_w_gate=m_ffn2_w_gate, ffn2_w_up=m_ffn2_w_up, ffn2_w_down=m_ffn2_w_down)
    big_v = dict(ffn1_w_gate=v_ffn1_w_gate, ffn1_w_up=v_ffn1_w_up, ffn1_w_down=v_ffn1_w_down, w_in=v_w_in,
                 w_out=v_w_out, ffn2_w_gate=v_ffn2_w_gate, ffn2_w_up=v_ffn2_w_up, ffn2_w_down=v_ffn2_w_down)
    shard_idx = jnp.reshape(shard, (1,)).astype(jnp.int32)
    c_idx = jnp.reshape(cc, (1,)).astype(jnp.int32)
    transposed = ("ffn1_w_gate", "ffn1_w_up", "ffn2_w_gate", "ffn2_w_up")

    def as2d(a, k):
        return a[0].T if k in transposed else a[0]

    def from2d(a, k):
        return a.T[None] if k in transposed else a[None]

    placed = {k: cast_place(as2d(big_w[k], k), shard_idx, name="cast_" + k) for k in big_names}
    first_names = ["ffn1_w_gate", "ffn1_w_up"]
    second_names = ["ffn1_w_down", "w_in"]
    third_names = ["w_out", "ffn2_w_gate"]
    fourth_names = ["ffn2_w_up", "ffn2_w_down"]
    res = run_exchange(gather_exchange([placed[k] for k in first_names], conv_w[0]), name="gather_first")
    gw = dict(zip(first_names, res[:2]))
    conv_full = res[2].transpose(1, 0, 2).reshape(CONV_K, 3 * HALF_W)

    x0 = x.reshape(n, d)
    def ffn_weights(prefix):
        return [gw[prefix + k].reshape(-1, d) for k in ("_w_gate", "_w_up", "_w_down")]

    def ffn_grad_blocks(grads):
        return [g.reshape(N_SHARD, -1, d) for g in grads]

    (h1, gate1, up1, act1), second = ffn_fwd(
        x0, ffn1_norm, gw["ffn1_w_gate"].reshape(-1, d), gw["ffn1_w_up"].reshape(-1, d), None,
        name="ffn1_fwd", comm=gather_exchange([placed[k] for k in second_names]))
    gw.update(zip(second_names, second))
    (x1,) = ffn_down(x0, act1, gw["ffn1_w_down"].reshape(-1, d), name="ffn1_down")
    w_in_full = gw["w_in"].transpose(1, 0, 2).reshape(d, IN_COLS)
    w_in_full = jnp.pad(w_in_full, ((0, 0), (0, PROJ_W - IN_COLS)))
    (proj, h2), third = in_proj_fwd(x1, mix_norm, w_in_full, name="in_proj_fwd",
                                    comm=gather_exchange([placed[k] for k in third_names]))
    gw.update(zip(third_names, third))
    proj3 = proj.reshape(bsz, t_len, PROJ_W)
    bias_tile = jnp.repeat(sg_b[0].T, SG_GROUP_DIM, axis=1)
    sg_out = sg_fwd(proj, sg_ln_g, sg_ln_b, sg_w[0], bias_tile, name="sg_fwd")
    qkv = dn_conv_fwd(proj3, conv_full, name="dn_conv_fwd")
    alog_row = jnp.zeros((1, LANES), F32).at[0, N_HEADS:2 * N_HEADS].set(a_log[0])
    dtb_row = jnp.zeros((1, LANES), F32).at[0, N_HEADS:2 * N_HEADS].set(dt_bias[0])
    (u_wy, w_wy, q_dec, k_dec, qk, tinv, gc), fourth = dn_chunk_fwd(
        qkv, proj3, alog_row, dtb_row, name="dn_chunk_fwd",
        comm=gather_exchange([placed[k] for k in fourth_names]))
    gw.update(zip(fourth_names, fourth))
    w_out_full = gw["w_out"].reshape(2 * HALF_W, d)
    o, s_in = dn_scan_fwd(u_wy, w_wy, q_dec, k_dec, qk, gc, name="dn_scan_fwd")
    dn_out = dn_out_fwd(o.reshape(n, HALF_W), proj, dn_norm, name="dn_out_fwd")
    x2 = out_proj_fwd(x1, sg_out, dn_out, w_out_full, name="out_proj_fwd")
    x3, h3, gate2, up2, act2 = ffn_fwd(x2, ffn2_norm, *ffn_weights("ffn2"), name="ffn2_fwd")
    dx3, d_final_norm, loss_tile = final_loss(x3, final_norm.reshape(1, d),
                                              loss_target.reshape(n, d), name="final_loss")

    dx2, dgate2, dup2, dyh2, d_ffn2_norm = ffn_bwd_act(
        dx3, x2, ffn2_norm, gate2, up2, *ffn_weights("ffn2"), name="ffn2_bwd_act")
    g_big = {}
    g_big["ffn2_w_gate"], g_big["ffn2_w_up"], g_big["ffn2_w_down"] = ffn_grad_blocks(ffn_bwd_w(
        [dgate2, dup2, act2], [h3, dyh2], [(0, 0), (1, 0), (2, 1)], name="ffn2_bwd_w"))

    early = ["ffn2_w_gate", "ffn2_w_up", "ffn2_w_down"]
    (d_sg, d_dn, dx2b), early_sib = out_proj_bwd_x(dx2, w_out_full, name="out_proj_bwd_x",
                                                   comm=pair_exchange([g_big[k] for k in early]))
    early_sums = [pair_add(g_big[k], s, c_idx, name="grad_pair_add_" + k) for k, s in zip(early, early_sib)]
    g_w_out = matmul_tn([sg_out, dn_out], dx2b, d, name="w_out_grad")
    g_big["w_out"] = g_w_out.reshape(N_SHARD, (2 * HALF_W) // N_SHARD, d)

    d_proj, d_sg_w, d_bias_tile, d_ln_g, d_ln_b = sg_bwd(d_sg, proj, sg_ln_g, sg_ln_b, sg_w[0],
                                                         bias_tile, name="sg_bwd")
    d_o, d_proj, d_dn_norm = dn_out_bwd(d_dn, o.reshape(n, HALF_W), proj, dn_norm, d_proj,
                                        name="dn_out_bwd")
    du, dw, dqd, dkd, dqk, dgc_scan = dn_scan_bwd(d_o.reshape(bsz, t_len, HALF_W), u_wy, w_wy, q_dec,
                                                  k_dec, qk, gc, s_in, name="dn_scan_bwd")
    (d_qkv, d_proj3, d_alog_row, d_dtb_row), early_chips = dn_chunk_bwd(
        qkv, proj3, alog_row, dtb_row, tinv, u_wy, w_wy, du, dw, dqd, dkd, dqk, dgc_scan,
        d_proj.reshape(bsz, t_len, PROJ_W), name="dn_chunk_bwd", comm=chip_exchange(early_sums))
    early_halves = [sum_chips(r, p, shard_idx, name="grad_chip_sum_" + k)
                    for k, r, p in zip(early, early_chips, early_sums)]
    d_proj3, d_conv = dn_conv_bwd(d_qkv, proj3, conv_full, d_proj3, name="dn_conv_bwd")
    d_proj = d_proj3.reshape(n, PROJ_W)
    g_w_in = matmul_tn([h2], d_proj, 3 * MXU_DIM, name="w_in_grad")[:, :IN_COLS]
    g_big["w_in"] = g_w_in.reshape(d, N_SHARD, IN_COLS // N_SHARD).transpose(1, 0, 2)

    def reduce_start(names):
        return pair_exchange([g_big[k] for k in names])

    def reduce_pair_sums(names, from_sib):
        return [pair_add(g_big[k], s, c_idx, name="grad_pair_add_" + k) for k, s in zip(names, from_sib)]

    def reduce_chip_sums(names, from_chips, sums):
        return [sum_chips(r, p, shard_idx, name="grad_chip_sum_" + k)
                for k, r, p in zip(names, from_chips, sums)]

    mid = ["w_in", "w_out"]
    (dx1, d_mix_norm, dyh1), mid_sib = in_proj_bwd_x(d_proj, w_in_full, x1, mix_norm, dx2,
                                                     name="in_proj_bwd_x", comm=reduce_start(mid))
    mid_sums = reduce_pair_sums(mid, mid_sib)
    down = ["ffn1_w_down"]
    (g_down,), mid_chips = ffn_bwd_w([act1], [dyh1], [(0, 0)], name="ffn1_bwd_w_down",
                                     comm=chip_exchange(mid_sums))
    g_big["ffn1_w_down"] = g_down.reshape(N_SHARD, -1, d)
    mid_halves = reduce_chip_sums(mid, mid_chips, mid_sums)
    leg, legs = merge_exchanges([reduce_start(down), pair_swap(mid_halves), pair_swap(early_halves)])
    leg_res = run_exchange(leg, name="grad_pair_exchange_down")
    down_sums = reduce_pair_sums(down, leg_res[legs[0]])
    mid_sib_halves, early_sib_halves = leg_res[legs[1]], leg_res[legs[2]]

    dx0, dgate1, dup1, _, d_ffn1_norm = ffn_bwd_act(
        dx1, x0, ffn1_norm, gate1, up1, *ffn_weights("ffn1"), name="ffn1_bwd_act")
    grad_x = dx0.reshape(bsz, t_len, d)
    d_sg_b = d_bias_tile.reshape(SG_CHUNK, SG_GROUPS, SG_GROUP_DIM).sum(axis=-1).T
    small_g = dict(ffn1_norm=d_ffn1_norm, mix_norm=d_mix_norm, ffn2_norm=d_ffn2_norm,
                   final_norm=d_final_norm, sg_ln_g=d_ln_g, sg_ln_b=d_ln_b, dn_norm=d_dn_norm,
                   a_log=d_alog_row[:, N_HEADS:2 * N_HEADS], dt_bias=d_dtb_row[:, N_HEADS:2 * N_HEADS],
                   sg_b=d_sg_b, sg_w=d_sg_w, conv_w=d_conv, loss=loss_tile[:, :1])
    my_pack = _pack_small(small_g)
    hosted, parts = merge_exchanges([chip_exchange(down_sums), device_gather(my_pack)])
    late = ["ffn1_w_gate", "ffn1_w_up"]
    late_grads, hosted_res = ffn_bwd_w([dgate1, dup1], [h1], [(0, 0), (1, 0)], name="ffn1_bwd_w_gate_up",
                                       comm=hosted)
    g_big["ffn1_w_gate"], g_big["ffn1_w_up"] = ffn_grad_blocks(late_grads)
    down_halves = reduce_chip_sums(down, hosted_res[parts[0]], down_sums)
    (all_packs,) = hosted_res[parts[1]]

    leg, legs = merge_exchanges([reduce_start(late), pair_swap(down_halves)])
    leg_res = run_exchange(leg, name="grad_pair_exchange")
    pair_sums = reduce_pair_sums(late, leg_res[legs[0]])
    down_sib_halves = leg_res[legs[1]]

    def adam_items(names, mine, sib):
        return [(as2d(big_w[k], k), gm, gs, as2d(big_m[k], k), as2d(big_v[k], k))
                for k, gm, gs in zip(names, mine, sib)]

    outs = {}
    done = adamw_pairs(
        adam_items(early + mid + down, early_halves + mid_halves + down_halves,
                   list(early_sib_halves) + list(mid_sib_halves) + list(down_sib_halves)),
        name="adamw_early")
    from_chips = run_exchange(chip_exchange(pair_sums), name="grad_chip_exchange")
    halves = reduce_chip_sums(late, from_chips, pair_sums)
    sib_halves = run_exchange(pair_swap(halves), name="grad_pair_swap")
    done += adamw_pairs(adam_items(late, halves, sib_halves), name="adamw_late")
    for k, res in zip(early + mid + down + late, done):
        outs[k] = tuple(from2d(a, k) for a in res)

    small_w = dict(ffn1_norm=ffn1_norm, mix_norm=mix_norm, ffn2_norm=ffn2_norm, final_norm=final_norm,
                   sg_ln_g=sg_ln_g, sg_ln_b=sg_ln_b, dn_norm=dn_norm, a_log=a_log, dt_bias=dt_bias,
                   sg_b=sg_b, sg_w=sg_w)
    small_m = dict(ffn1_norm=m_ffn1_norm, mix_norm=m_mix_norm, ffn2_norm=m_ffn2_norm,
                   final_norm=m_final_norm, sg_ln_g=m_sg_ln_g, sg_ln_b=m_sg_ln_b, dn_norm=m_dn_norm,
                   a_log=m_a_log, dt_bias=m_dt_bias, sg_b=m_sg_b, sg_w=m_sg_w)
    small_v = dict(ffn1_norm=v_ffn1_norm, mix_norm=v_mix_norm, ffn2_norm=v_ffn2_norm,
                   final_norm=v_final_norm, sg_ln_g=v_sg_ln_g, sg_ln_b=v_sg_ln_b, dn_norm=v_dn_norm,
                   a_log=v_a_log, dt_bias=v_dt_bias, sg_b=v_sg_b, sg_w=v_sg_w)
    shapes = {k: small_w[k].shape for k in small_w}
    shapes["conv_w"] = (CONV_K, 3 * HALF_W)
    shapes["loss"] = (1, 1)
    me_idx = jnp.reshape(4 * xy + 2 * yy + cc, (1,)).astype(jnp.int32)
    g_pack = sum_devices(all_packs, my_pack, me_idx, name="small_sum")
    g_small = _unpack_small(g_pack, shapes)
    loss = g_small["loss"].reshape(())
    cw = 3 * HALF_W // N_SHARD
    g_conv = lax.dynamic_slice_in_dim(g_small["conv_w"], shard * cw, cw, axis=1)
    zero_conv = jnp.zeros((CONV_K, 3 * HALF_W), F32)

    def packed(src, conv):
        parts = dict(src)
        parts["conv_w"] = lax.dynamic_update_slice_in_dim(zero_conv, conv[0], shard * cw, axis=1)
        parts["loss"] = jnp.zeros((1, 1), F32)
        return _pack_small(parts)

    d_pack, m_pack, v_pack = adamw(packed(small_w, conv_w), g_pack, packed(small_m, m_conv_w),
                                   packed(small_v, v_conv_w), name="adamw_small")
    d_small = _unpack_small(d_pack, shapes)
    m_small = _unpack_small(m_pack, shapes)
    v_small = _unpack_small(v_pack, shapes)

    def conv_block(full_arr):
        return lax.dynamic_slice_in_dim(full_arr, shard * cw, cw, axis=1)[None]

    for k in small_w:
        outs[k] = (g_small[k].reshape(small_w[k].shape), d_small[k], m_small[k], v_small[k])
    outs["conv_w"] = (g_conv[None], conv_block(d_small["conv_w"]), conv_block(m_small["conv_w"]),
                      conv_block(v_small["conv_w"]))

    order = ["ffn1_norm", "ffn1_w_gate", "ffn1_w_up", "ffn1_w_down", "mix_norm", "w_in", "conv_w",
             "a_log", "dt_bias", "dn_norm", "sg_ln_g", "sg_ln_b", "sg_w", "sg_b", "w_out", "ffn2_norm",
             "ffn2_w_gate", "ffn2_w_up", "ffn2_w_down", "final_norm"]
    return (loss, grad_x, *[outs[k][0] for k in order], *[outs[k][1] for k in order],
            *[outs[k][2] for k in order], *[outs[k][3] for k in order])
```

```python
import functools

import jax
import jax.numpy as jnp
from jax import lax
from jax.experimental import pallas as pl
from jax.experimental.pallas import tpu as pltpu

F32 = jnp.float32
BF16 = jnp.bfloat16
EPS = 1e-6

D_MODEL = 1024
N_SHARD = 4
HEAD_DIM = 128
N_HEADS = 4
DN_CHUNK = 64
SG_CHUNK = 128
SG_GROUPS = 8
SG_GROUP_DIM = 64
HALF_W = 512
PROJ_W = 3200
IN_COLS = 3080
GATE_COL_BLOCK = 24
QK_SCALE = HEAD_DIM ** -0.5
LANES = 128

ADAM_LR = 0.001
ADAM_B1 = 0.9
ADAM_B2 = 0.999
ADAM_EPS = 1e-08
ADAM_WD = 0.01
ADAM_STEP = 10

VMEM_LIMIT = 56 * 1024 * 1024
ROW_TILE = 512

NN = ((1,), (0,))
NT = ((1,), (1,))
TN = ((0,), (0,))
MESH = pl.DeviceIdType.MESH


def _dot(a, b, dims):
    return lax.dot_general(a, b, (dims, ((), ())), preferred_element_type=F32)


def _bdot(a, b, dims):
    return _dot(a.astype(BF16), b.astype(BF16), dims)


def _split(a):
    hi = a.astype(BF16)
    lo = (a - hi.astype(F32)).astype(BF16)
    return hi, lo


def _dot3(a, b, dims=NN):
    return _dot(a[0], b[0], dims) + (_dot(a[0], b[1], dims) + _dot(a[1], b[0], dims))


def _dot_exact_lhs(a, b):
    ab = a.astype(BF16)
    b1 = b.astype(BF16)
    r1 = b - b1.astype(F32)
    b2 = r1.astype(BF16)
    b3 = (r1 - b2.astype(F32)).astype(BF16)
    return _dot(ab, b1, NN) + (_dot(ab, b2, NN) + _dot(ab, b3, NN))


def _call(body, *, name, out_shape, in_specs, out_specs, grid=(), scratch=(), comm=None, **kw):
    params = dict(vmem_limit_bytes=VMEM_LIMIT)
    if grid:
        params["dimension_semantics"] = ("arbitrary",) * len(grid)
    if comm is None:
        return pl.pallas_call(
            body, name=name, grid=grid, in_specs=in_specs, out_specs=out_specs,
            out_shape=out_shape, scratch_shapes=list(scratch),
            compiler_params=pltpu.CompilerParams(**params), **kw)

    n_in, n_out, n_sc = len(in_specs), len(out_specs), len(scratch)
    c_in, c_out = len(comm.inputs), len(comm.out_shape)
    steps = 1
    for g in grid:
        steps *= g

    def hosted(*refs):
        ins, cins = refs[:n_in], refs[n_in:n_in + c_in]
        o0 = n_in + c_in
        outs, couts = refs[o0:o0 + n_out], refs[o0 + n_out:o0 + n_out + c_out]
        s0 = o0 + n_out + c_out
        sc, csems = refs[s0:s0 + n_sc], refs[s0 + n_sc:]
        lin = 0
        for axis, g in enumerate(grid):
            lin = lin * g + pl.program_id(axis)

        def at(step, fn):
            @pl.when(lin == step % steps)
            def _():
                fn(cins, couts, csems)

        for step, fn in comm.phases:
            if step >= 0:
                at(step, fn)
        body(*ins, *outs, *sc)
        for step, fn in comm.phases:
            if step < 0:
                at(step, fn)

    aliases = dict(kw.pop("input_output_aliases", {}))
    for k, m in comm.aliases.items():
        aliases[n_in + k] = n_out + m
    call = pl.pallas_call(
        hosted, name=name, grid=grid, in_specs=list(in_specs) + [_ANY] * c_in,
        out_specs=list(out_specs) + [_ANY] * c_out, out_shape=list(out_shape) + comm.out_shape,
        scratch_shapes=list(scratch) + comm.sems, input_output_aliases=aliases,
        compiler_params=pltpu.CompilerParams(**params), **kw)

    def run(*args):
        res = call(*args, *comm.inputs)
        return res[:n_out], res[n_out:]

    return run


def _sds(shape, dtype):
    return jax.ShapeDtypeStruct(tuple(shape), dtype)


def _resident(shape):
    zeros = (0,) * len(shape)
    return pl.BlockSpec(tuple(shape), lambda *_: zeros, pipeline_mode=pl.Buffered(1))


def _sigmoid(x):
    return jax.nn.sigmoid(x)


def _softplus(x):
    return jnp.maximum(x, 0.0) + jnp.log(1.0 + jnp.exp(-jnp.abs(x)))


_GELU_C = 0.7978845608028654
_GELU_A = 0.044715


def _gelu_tanh(x):
    return jnp.tanh(_GELU_C * (x + _GELU_A * x * x * x))


def _gelu(x, t):
    return 0.5 * x * (1.0 + t)


def _gelu_grad(x, t):
    return 0.5 * (1.0 + t) + 0.5 * x * (1.0 - t * t) * _GELU_C * (1.0 + 3.0 * _GELU_A * x * x)


def _silu_grad(x):
    s = _sigmoid(x)
    return s * (1.0 + x * (1.0 - s))


def _rms_scale(xv):
    return lax.rsqrt(jnp.mean(xv * xv, axis=-1, keepdims=True) + EPS)


def _rms_bwd(dh, xv, g):
    r = _rms_scale(xv)
    xn = xv * r
    dg = jnp.sum(dh * xn, axis=0, keepdims=True)
    dxn = dh * g
    dx = r * (dxn - xn * jnp.mean(dxn * xn, axis=-1, keepdims=True))
    return dx, dg


def _iota2(shape, dim):
    return lax.broadcasted_iota(jnp.int32, shape, dim)


def _col_to_row(col):
    n = col.shape[0]
    eye = _iota2((n, n), 0) == _iota2((n, n), 1)
    return jnp.sum(jnp.where(eye, col, 0.0), axis=0, keepdims=True)


def _row_to_col(row):
    n = row.shape[1]
    eye = _iota2((n, n), 0) == _iota2((n, n), 1)
    return jnp.sum(jnp.where(eye, row, 0.0), axis=1, keepdims=True)


MXU_DIM = 256


def _hidden_chunks(f, step=3 * MXU_DIM):
    return [(c0, min(c0 + step, f)) for c0 in range(0, f, step)]

def ffn_fwd(x, gnorm, wg, wu, wd, name, comm=None):
    n, d = x.shape
    f = wg.shape[0]
    tm = min(ROW_TILE, n)
    fused = wd is not None

    def body(x_ref, g_ref, wg_ref, wu_ref, *rest):
        if fused:
            wd_ref, xo_ref, h_ref, gate_ref, up_ref, act_ref, acc_ref = rest
        else:
            h_ref, gate_ref, up_ref, act_ref = rest
        xv = x_ref[...]
        h = (xv * _rms_scale(xv) * g_ref[...]).astype(BF16)
        h_ref[...] = h
        chunks = _hidden_chunks(f)

        def gate_up(c0, c1):
            return _dot(h, wg_ref[c0:c1, :], NT), _dot(h, wu_ref[c0:c1, :], NT)

        nxt = gate_up(*chunks[0])
        for idx, (c0, c1) in enumerate(chunks):
            gate, up = nxt
            if idx + 1 < len(chunks):
                nxt = gate_up(*chunks[idx + 1])
            act = (gate * _sigmoid(gate) * up).astype(BF16)
            gate_ref[:, c0:c1] = gate.astype(BF16)
            up_ref[:, c0:c1] = up.astype(BF16)
            act_ref[:, c0:c1] = act
            if fused:
                part = _dot(act, wd_ref[c0:c1, :], NN)
                if c0 == 0:
                    acc_ref[...] = part
                else:
                    acc_ref[...] += part
        if fused:
            xo_ref[...] = xv + 0.5 * acc_ref[...]

    row = pl.BlockSpec((tm, d), lambda i: (i, 0))
    wide = pl.BlockSpec((tm, f), lambda i: (i, 0))
    n_w = 3 if fused else 2
    return _call(
        body, name=name, grid=(n // tm,),
        in_specs=[row, pl.BlockSpec((1, d), lambda i: (0, 0))] + [_resident((f, d))] * n_w,
        out_specs=([row] if fused else []) + [row, wide, wide, wide],
        out_shape=([_sds((n, d), F32)] if fused else []) + [_sds((n, d), BF16)] + [_sds((n, f), BF16)] * 3,
        scratch=[pltpu.VMEM((tm, d), F32)] if fused else [], comm=comm,
    )(*([x, gnorm, wg, wu] + ([wd] if fused else [])))


def ffn_down(x, act, wd, name, comm=None):
    n, d = x.shape
    f = wd.shape[0]
    tm = min(ROW_TILE, n)

    def body(x_ref, a_ref, w_ref, o_ref):
        o_ref[...] = x_ref[...] + 0.5 * _dot(a_ref[...], w_ref[...], NN)

    row = pl.BlockSpec((tm, d), lambda i: (i, 0))
    return _call(
        body, name=name, grid=(n // tm,),
        in_specs=[row, pl.BlockSpec((tm, f), lambda i: (i, 0)), _resident((f, d))],
        out_specs=[row], out_shape=[_sds((n, d), F32)], comm=comm,
    )(x, act, wd)


def ffn_bwd_act(dy, x, gnorm, gate, up, wg, wu, wd, name, comm=None):
    n, d = x.shape
    f = wg.shape[0]
    tm = min(ROW_TILE // 2, n)

    def body(dy_ref, x_ref, g_ref, gate_ref, up_ref, wg_ref, wu_ref, wd_ref,
             dx_ref, dgate_ref, dup_ref, dyh_ref, dg_ref, acc_ref):
        @pl.when(pl.program_id(0) == 0)
        def _():
            dg_ref[...] = jnp.zeros_like(dg_ref)

        dyh = (0.5 * dy_ref[...]).astype(BF16)
        dyh_ref[...] = dyh
        chunks = _hidden_chunks(f, 2 * MXU_DIM)
        ahead = [_dot(dyh, wd_ref[c0:c1, :], NT) for c0, c1 in chunks[:2]]
        for idx, (c0, c1) in enumerate(chunks):
            dact = ahead.pop(0)
            if idx + 2 < len(chunks):
                n0, n1 = chunks[idx + 2]
                ahead.append(_dot(dyh, wd_ref[n0:n1, :], NT))
            gt = gate_ref[:, c0:c1].astype(F32)
            u = up_ref[:, c0:c1].astype(F32)
            s = _sigmoid(gt)
            dup = (dact * (gt * s)).astype(BF16)
            dgate = (dact * u * (s * (1.0 + gt * (1.0 - s)))).astype(BF16)
            dup_ref[:, c0:c1] = dup
            dgate_ref[:, c0:c1] = dgate
            part = _dot(dgate, wg_ref[c0:c1, :], NN) + _dot(dup, wu_ref[c0:c1, :], NN)
            if c0 == 0:
                acc_ref[...] = part
            else:
                acc_ref[...] += part
        dxn, dg = _rms_bwd(acc_ref[...], x_ref[...], g_ref[...])
        dx_ref[...] = dy_ref[...] + dxn
        dg_ref[...] += dg

    row = pl.BlockSpec((tm, d), lambda i: (i, 0))
    wide = pl.BlockSpec((tm, f), lambda i: (i, 0))
    vec = pl.BlockSpec((1, d), lambda i: (0, 0))
    wres = _resident((f, d))
    return _call(
        body, name=name, grid=(n // tm,),
        in_specs=[row, row, vec, wide, wide, wres, wres, wres],
        out_specs=[row, wide, wide, row, vec],
        out_shape=[_sds((n, d), F32), _sds((n, f), BF16), _sds((n, f), BF16),
                   _sds((n, d), BF16), _sds((1, d), F32)],
        scratch=[pltpu.VMEM((tm, d), F32)], comm=comm,
    )(dy, x, gnorm, gate, up, wg, wu, wd)


def ffn_bwd_w(wide, rows, pairs, name, comm=None):
    n, d = rows[0].shape
    f = wide[0].shape[1]
    fh = f // 2
    tk = min(ROW_TILE, n)
    n_w, n_r = len(wide), len(rows)

    def body(*refs):
        wide_refs, row_refs, outs = refs[:n_w], refs[n_w:n_w + n_r], refs[n_w + n_r:]

        @pl.when(pl.program_id(1) == 0)
        def _():
            for o_ref in outs:
                o_ref[...] = jnp.zeros_like(o_ref)

        row_vals = [r[...] for r in row_refs]
        for c0, c1 in _hidden_chunks(fh, 2 * MXU_DIM):
            for (i, k), o_ref in zip(pairs, outs):
                o_ref[c0:c1, :] += _dot(wide_refs[i][:, c0:c1], row_vals[k], TN)

    row = pl.BlockSpec((tk, d), lambda j, k: (k, 0))
    blk = pl.BlockSpec((tk, fh), lambda j, k: (k, j))
    return _call(
        body, name=name, grid=(2, n // tk),
        in_specs=[blk] * n_w + [row] * n_r,
        out_specs=[pl.BlockSpec((fh, d), lambda j, k: (j, 0))] * len(pairs),
        out_shape=[_sds((f, d), F32)] * len(pairs), comm=comm,
    )(*wide, *rows)


def final_loss(x, gnorm, target, name):
    n, d = x.shape
    tm = min(ROW_TILE, n)

    def body(x_ref, g_ref, t_ref, dx_ref, dg_ref, loss_ref):
        @pl.when(pl.program_id(0) == 0)
        def _():
            dg_ref[...] = jnp.zeros_like(dg_ref)
            loss_ref[...] = jnp.zeros_like(loss_ref)

        xv = x_ref[...]
        y = xv * _rms_scale(xv) * g_ref[...]
        err = y - t_ref[...]
        part = 0.5 * jnp.sum(jnp.mean(err * err, axis=-1, keepdims=True), axis=0, keepdims=True)
        loss_ref[...] += jnp.broadcast_to(part, loss_ref.shape)
        dx, dg = _rms_bwd(err * (1.0 / d), xv, g_ref[...])
        dx_ref[...] = dx
        dg_ref[...] += dg

    row = pl.BlockSpec((tm, d), lambda i: (i, 0))
    vec = pl.BlockSpec((1, d), lambda i: (0, 0))
    return _call(
        body, name=name, grid=(n // tm,),
        in_specs=[row, vec, row],
        out_specs=[row, vec, pl.BlockSpec((1, LANES), lambda i: (0, 0))],
        out_shape=[_sds((n, d), F32), _sds((1, d), F32), _sds((1, LANES), F32)],
    )(x, gnorm, target)


def in_proj_fwd(x, gnorm, w, name, comm=None):
    n, d = x.shape
    cols = w.shape[1]
    tm = min(ROW_TILE, n)

    def body(x_ref, g_ref, w_ref, p_ref, h_ref):
        xv = x_ref[...]
        h = (xv * _rms_scale(xv) * g_ref[...]).astype(BF16)
        h_ref[...] = h
        for c0, c1 in _hidden_chunks(cols):
            p_ref[:, c0:c1] = _dot(h, w_ref[:, c0:c1], NN)

    return _call(
        body, name=name, grid=(n // tm,),
        in_specs=[pl.BlockSpec((tm, d), lambda i: (i, 0)),
                  pl.BlockSpec((1, d), lambda i: (0, 0)), _resident((d, cols))],
        out_specs=[pl.BlockSpec((tm, cols), lambda i: (i, 0)),
                   pl.BlockSpec((tm, d), lambda i: (i, 0))],
        out_shape=[_sds((n, cols), F32), _sds((n, d), BF16)], comm=comm,
    )(x, gnorm, w)


def in_proj_bwd_x(dproj, w, x, gnorm, dres, name, comm=None):
    n, d = x.shape
    cols = w.shape[1]
    tm = min(ROW_TILE, n)

    def body(dp_ref, w_ref, x_ref, g_ref, dr_ref, dx_ref, dg_ref, dxh_ref):
        @pl.when(pl.program_id(0) == 0)
        def _():
            dg_ref[...] = jnp.zeros_like(dg_ref)

        dh = _dot(dp_ref[...], w_ref[...], NT)
        dxn, dg = _rms_bwd(dh, x_ref[...], g_ref[...])
        dx = dr_ref[...] + dxn
        dx_ref[...] = dx
        dxh_ref[...] = (0.5 * dx).astype(BF16)
        dg_ref[...] += dg

    row = pl.BlockSpec((tm, d), lambda i: (i, 0))
    vec = pl.BlockSpec((1, d), lambda i: (0, 0))
    return _call(
        body, name=name, grid=(n // tm,),
        in_specs=[pl.BlockSpec((tm, cols), lambda i: (i, 0)),
                  _resident((d, cols)), row, vec, row],
        out_specs=[row, vec, row],
        out_shape=[_sds((n, d), F32), _sds((1, d), F32), _sds((n, d), BF16)], comm=comm,
    )(dproj, w, x, gnorm, dres)


def matmul_tn(a_list, b, tn, name):
    n, cb = b.shape
    widths = [a.shape[1] for a in a_list]
    tk = min(ROW_TILE, n)

    def body(*refs):
        a_refs, b_ref, o_ref = refs[:-2], refs[-2], refs[-1]

        @pl.when(pl.program_id(0) == 0)
        def _():
            o_ref[...] = jnp.zeros_like(o_ref)

        r0 = 0
        for a_ref, ka in zip(a_refs, widths):
            av = a_ref[...]
            for c0, c1 in _hidden_chunks(cb, tn):
                o_ref[r0:r0 + ka, c0:c1] += _dot(av, b_ref[:, c0:c1], TN)
            r0 += ka

    return _call(
        body, name=name, grid=(n // tk,),
        in_specs=[pl.BlockSpec((tk, ka), lambda k: (k, 0)) for ka in widths]
        + [pl.BlockSpec((tk, cb), lambda k: (k, 0))],
        out_specs=pl.BlockSpec((sum(widths), cb), lambda k: (0, 0)),
        out_shape=_sds((sum(widths), cb), F32),
    )(*a_list, b)


def out_proj_fwd(x, sg_out, dn_out, w, name):
    n, d = x.shape
    tm = min(ROW_TILE, n)

    def body(x_ref, a_ref, b_ref, w_ref, o_ref):
        o_ref[...] = (x_ref[...] + _dot(a_ref[...], w_ref[0:HALF_W, :], NN)
                      + _dot(b_ref[...], w_ref[HALF_W:2 * HALF_W, :], NN))

    row = pl.BlockSpec((tm, d), lambda i: (i, 0))
    half = pl.BlockSpec((tm, HALF_W), lambda i: (i, 0))
    return _call(
        body, name=name, grid=(n // tm,),
        in_specs=[row, half, half, pl.BlockSpec((2 * HALF_W, d), lambda i: (0, 0))],
        out_specs=row, out_shape=_sds((n, d), F32),
    )(x, sg_out, dn_out, w)


def out_proj_bwd_x(dy, w, name, comm=None):
    n, d = dy.shape
    tm = min(ROW_TILE, n)

    def body(dy_ref, w_ref, dsg_ref, ddn_ref, dyb_ref):
        dyb = dy_ref[...].astype(BF16)
        dyb_ref[...] = dyb
        dsg_ref[...] = _dot(dyb, w_ref[0:HALF_W, :], NT)
        ddn_ref[...] = _dot(dyb, w_ref[HALF_W:2 * HALF_W, :], NT)

    row = pl.BlockSpec((tm, d), lambda i: (i, 0))
    half = pl.BlockSpec((tm, HALF_W), lambda i: (i, 0))
    return _call(
        body, name=name, grid=(n // tm,),
        in_specs=[row, pl.BlockSpec((2 * HALF_W, d), lambda i: (0, 0))],
        out_specs=[half, half, row],
        out_shape=[_sds((n, HALF_W), F32), _sds((n, HALF_W), F32), _sds((n, d), BF16)], comm=comm,
    )(dy, w)


SG_PAIRS = SG_GROUPS // 2


def _sg_low_half():
    return _iota2((SG_CHUNK, LANES), 1) < SG_GROUP_DIM


def _sg_pair_cols(p):
    return slice(p * LANES, (p + 1) * LANES)


def _sg_causal():
    return _iota2((SG_CHUNK, SG_CHUNK), 0) >= _iota2((SG_CHUNK, SG_CHUNK), 1)


def _sg_forward_chunk(pu, pv, ln_g, ln_b, wc, bias, low):
    tu, tv = _gelu_tanh(pu), _gelu_tanh(pv)
    u = _gelu(pu, tu)
    v = _gelu(pv, tv)
    mu = jnp.mean(v, axis=-1, keepdims=True)
    vc = v - mu
    rs = lax.rsqrt(jnp.mean(vc * vc, axis=-1, keepdims=True) + EPS)
    xhat = vc * rs
    vn = (xhat * ln_g + ln_b).astype(BF16)
    parts = []
    for p in range(SG_PAIRS):
        vn_p = vn[:, _sg_pair_cols(p)]
        parts.append(jnp.where(low, _dot(wc[2 * p], vn_p, NN), _dot(wc[2 * p + 1], vn_p, NN)))
    vs = bias + jnp.concatenate(parts, axis=1)
    return u, xhat, rs, vn, vs, tu, tv


def sg_fwd(proj, ln_g, ln_b, w_s, bias_tile, name):
    n = proj.shape[0]
    tm = min(ROW_TILE, n)

    def body(pu_ref, pv_ref, g_ref, b_ref, w_ref, bias_ref, o_ref):
        causal = _sg_causal()
        wc = [jnp.where(causal, w_ref[g], 0.0).astype(BF16) for g in range(SG_GROUPS)]
        masks = _sg_low_half()
        for ci in range(tm // SG_CHUNK):
            rows = slice(ci * SG_CHUNK, (ci + 1) * SG_CHUNK)
            u, _, _, _, vs, _, _ = _sg_forward_chunk(pu_ref[rows, :], pv_ref[rows, :], g_ref[...],
                                                     b_ref[...], wc, bias_ref[...], masks)
            o_ref[rows, :] = (u * vs).astype(BF16)

    vec = pl.BlockSpec((1, HALF_W), lambda i: (0, 0))
    return _call(
        body, name=name, grid=(n // tm,),
        in_specs=[pl.BlockSpec((tm, HALF_W), lambda i: (i, 0)),
                  pl.BlockSpec((tm, HALF_W), lambda i: (i, 1)), vec, vec,
                  pl.BlockSpec((SG_GROUPS, SG_CHUNK, SG_CHUNK), lambda i: (0, 0, 0)),
                  pl.BlockSpec((SG_CHUNK, HALF_W), lambda i: (0, 0))],
        out_specs=pl.BlockSpec((tm, HALF_W), lambda i: (i, 0)),
        out_shape=_sds((n, HALF_W), BF16),
    )(proj, proj, ln_g, ln_b, w_s, bias_tile)


def sg_bwd(dsg, proj, ln_g, ln_b, w_s, bias_tile, name):
    n = proj.shape[0]
    tm = min(ROW_TILE, n)

    def body(d_ref, pu_ref, pv_ref, g_ref, b_ref, w_ref, bias_ref,
             dp_ref, dw_ref, db_ref, dlg_ref, dlb_ref):
        @pl.when(pl.program_id(0) == 0)
        def _():
            dw_ref[...] = jnp.zeros_like(dw_ref)
            db_ref[...] = jnp.zeros_like(db_ref)
            dlg_ref[...] = jnp.zeros_like(dlg_ref)
            dlb_ref[...] = jnp.zeros_like(dlb_ref)

        causal = _sg_causal()
        wc = [jnp.where(causal, w_ref[g], 0.0).astype(BF16) for g in range(SG_GROUPS)]
        masks = _sg_low_half()
        ln_g_v = g_ref[...]
        for ci in range(tm // SG_CHUNK):
            rows = slice(ci * SG_CHUNK, (ci + 1) * SG_CHUNK)
            pu = pu_ref[rows, :]
            pv = pv_ref[rows, :]
            u, xhat, rs, vn, vs, tu, tv = _sg_forward_chunk(pu, pv, ln_g_v, b_ref[...], wc,
                                                            bias_ref[...], masks)
            dout = d_ref[rows, :]
            dp_ref[rows, 0:HALF_W] = (dout * vs * _gelu_grad(pu, tu)).astype(BF16)
            dvs = dout * u
            dvs_b = dvs.astype(BF16)
            db_ref[...] += dvs
            dvn_parts = []
            for p in range(SG_PAIRS):
                dvs_p = dvs_b[:, _sg_pair_cols(p)]
                vn_p = vn[:, _sg_pair_cols(p)]
                dvn_parts.append(jnp.where(masks, _dot(wc[2 * p], dvs_p, TN), _dot(wc[2 * p + 1], dvs_p, TN)))
                zero = jnp.zeros_like(dvs_p)
                dw_ref[2 * p] += jnp.where(causal, _dot(jnp.where(masks, dvs_p, zero), vn_p, NT), 0.0)
                dw_ref[2 * p + 1] += jnp.where(causal, _dot(jnp.where(masks, zero, dvs_p), vn_p, NT), 0.0)
            dvn = jnp.concatenate(dvn_parts, axis=1)
            dlg_ref[...] += jnp.sum(dvn * xhat, axis=0, keepdims=True)
            dlb_ref[...] += jnp.sum(dvn, axis=0, keepdims=True)
            dxh = dvn * ln_g_v
            dv = rs * (dxh - jnp.mean(dxh, axis=-1, keepdims=True)
                       - xhat * jnp.mean(dxh * xhat, axis=-1, keepdims=True))
            dp_ref[rows, HALF_W:2 * HALF_W] = (dv * _gelu_grad(pv, tv)).astype(BF16)

    vec = pl.BlockSpec((1, HALF_W), lambda i: (0, 0))
    wspec = pl.BlockSpec((SG_GROUPS, SG_CHUNK, SG_CHUNK), lambda i: (0, 0, 0))
    tile = pl.BlockSpec((SG_CHUNK, HALF_W), lambda i: (0, 0))
    return _call(
        body, name=name, grid=(n // tm,),
        in_specs=[pl.BlockSpec((tm, HALF_W), lambda i: (i, 0)),
                  pl.BlockSpec((tm, HALF_W), lambda i: (i, 0)),
                  pl.BlockSpec((tm, HALF_W), lambda i: (i, 1)), vec, vec, wspec, tile],
        out_specs=[pl.BlockSpec((tm, 2 * HALF_W), lambda i: (i, 0)), wspec, tile, vec, vec],
        out_shape=[_sds((n, PROJ_W), BF16), _sds((SG_GROUPS, SG_CHUNK, SG_CHUNK), F32),
                   _sds((SG_CHUNK, HALF_W), F32), _sds((1, HALF_W), F32), _sds((1, HALF_W), F32)],
    )(dsg, proj, proj, ln_g, ln_b, w_s, bias_tile)


CONV_K = 4
CONV_BLOCK = 256


def _shift_down(x, s, row):
    if s == 0:
        return x
    return jnp.where(row >= s, pltpu.roll(x, s, 0), 0.0)


def _shift_up(x, s, row):
    if s == 0:
        return x
    t_len = x.shape[0]
    return jnp.where(row < t_len - s, pltpu.roll(x, t_len - s, 0), 0.0)


def _conv_taps(x, row):
    return [_shift_down(x, CONV_K - 1 - j, row) for j in range(CONV_K)]


def _conv(taps, w):
    y = taps[0] * w[0:1, :]
    for j in range(1, CONV_K):
        y = y + taps[j] * w[j:j + 1, :]
    return y


def dn_conv_fwd(proj3, conv_w, name):
    b, t, _ = proj3.shape
    nblk = 3 * HALF_W // CONV_BLOCK
    first = 2 * HALF_W // CONV_BLOCK
    n_norm = 2 * HALF_W // CONV_BLOCK

    def body(x_ref, w_ref, o_ref):
        s = pl.program_id(1)
        x = x_ref[0]
        y = _conv(_conv_taps(x, _iota2(x.shape, 0)), w_ref[...])
        y = y * _sigmoid(y)

        @pl.when(s < n_norm)
        def _():
            for h in range(CONV_BLOCK // HEAD_DIM):
                cs = slice(h * HEAD_DIM, (h + 1) * HEAD_DIM)
                yh = y[:, cs]
                o_ref[0, :, cs] = yh * lax.rsqrt(jnp.sum(yh * yh, axis=-1, keepdims=True) + EPS)

        @pl.when(s >= n_norm)
        def _():
            o_ref[0] = y

    return _call(
        body, name=name, grid=(b, nblk),
        in_specs=[pl.BlockSpec((1, t, CONV_BLOCK), lambda i, s: (i, 0, first + s)),
                  pl.BlockSpec((CONV_K, CONV_BLOCK), lambda i, s: (0, s))],
        out_specs=pl.BlockSpec((1, t, CONV_BLOCK), lambda i, s: (i, 0, s)),
        out_shape=_sds((b, t, 3 * HALF_W), F32),
    )(proj3, conv_w)


def dn_conv_bwd(dqkv, proj3, conv_w, dproj3, name, comm=None):
    b, t, _ = proj3.shape
    nblk = 3 * HALF_W // CONV_BLOCK
    first = 2 * HALF_W // CONV_BLOCK
    n_norm = 2 * HALF_W // CONV_BLOCK

    def body(d_ref, x_ref, w_ref, dproj_in, dx_ref, dw_ref, ds_ref):
        s = pl.program_id(0)

        @pl.when(pl.program_id(1) == 0)
        def _():
            dw_ref[...] = jnp.zeros_like(dw_ref)

        x = x_ref[0]
        w = w_ref[...]
        row = _iota2(x.shape, 0)
        taps = _conv_taps(x, row)
        c = _conv(taps, w)
        sg = _sigmoid(c)
        y = c * sg

        @pl.when(s < n_norm)
        def _():
            for h in range(CONV_BLOCK // HEAD_DIM):
                cs = slice(h * HEAD_DIM, (h + 1) * HEAD_DIM)
                yh = y[:, cs]
                r = lax.rsqrt(jnp.sum(yh * yh, axis=-1, keepdims=True) + EPS)
                nh = yh * r
                dn = d_ref[0, :, cs]
                ds_ref[:, cs] = r * (dn - nh * jnp.sum(dn * nh, axis=-1, keepdims=True))

        @pl.when(s >= n_norm)
        def _():
            ds_ref[...] = d_ref[0]

        dc = ds_ref[...] * (sg * (1.0 + c * (1.0 - sg)))
        dx = _shift_up(dc, CONV_K - 1, row) * w[0:1, :]
        for j in range(1, CONV_K):
            dx = dx + _shift_up(dc, CONV_K - 1 - j, row) * w[j:j + 1, :]
        dx_ref[0] = dx.astype(BF16)
        for j in range(CONV_K):
            dw_ref[j:j + 1, :] += jnp.sum(dc * taps[j], axis=0, keepdims=True)

    return _call(
        body, name=name, grid=(nblk, b),
        in_specs=[pl.BlockSpec((1, t, CONV_BLOCK), lambda s, i: (i, 0, s)),
                  pl.BlockSpec((1, t, CONV_BLOCK), lambda s, i: (i, 0, first + s)),
                  pl.BlockSpec((CONV_K, CONV_BLOCK), lambda s, i: (0, s)), _ANY],
        out_specs=[pl.BlockSpec((1, t, CONV_BLOCK), lambda s, i: (i, 0, first + s)),
                   pl.BlockSpec((CONV_K, CONV_BLOCK), lambda s, i: (0, s))],
        out_shape=[_sds(dproj3.shape, BF16), _sds((CONV_K, 3 * HALF_W), F32)],
        scratch=[pltpu.VMEM((t, CONV_BLOCK), F32)],
        input_output_aliases={3: 0}, comm=comm,
    )(dqkv, proj3, conv_w, dproj3)


def _chunk_masks():
    ii = _iota2((DN_CHUNK, DN_CHUNK), 0)
    jj = _iota2((DN_CHUNK, DN_CHUNK), 1)
    return ii >= jj, ii > jj, ii == jj


LOCKSTEP_CHUNKS = 4


def _inv_unit_lower_many(l_mats, eye):
    eye_f = jnp.where(eye, 1.0, 0.0)
    ps = [-l for l in l_mats]
    ts = [eye_f + p for p in ps]
    pss = [_split(p) for p in ps]
    size = 2
    while size < DN_CHUNK:
        ps = [_dot3(s, s) for s in pss]
        pss = [_split(p) for p in ps]
        ts = [t + _dot3(_split(t), s) for t, s in zip(ts, pss)]
        size *= 2
    return ts


def _gates(pba, ea_row, dtb_row):
    beta = _sigmoid(pba)
    g = -ea_row * _softplus(pba + dtb_row)
    return beta, g


def _chunk_decay(gcol):
    incl, strict, eye = _chunk_masks()
    grow = jnp.sum(jnp.where(eye, gcol, 0.0), axis=0, keepdims=True)
    decay = jnp.where(incl, jnp.exp(jnp.where(incl, gcol - grow, 0.0)), 0.0)
    return decay, incl, strict, eye


def dn_chunk_fwd(qkv, proj3, alog_row, dtb_row, name, comm=None):
    b, t, _ = qkv.shape
    rblk = min(256, t)
    n_in = rblk // DN_CHUNK

    def body(q_ref, k_ref, v_ref, pba_ref, al_ref, dtb_ref,
             u_ref, w_ref, qd_ref, kd_ref, qk_ref, ti_ref, gc_ref):
        ea = jnp.exp(al_ref[...])
        tri = jnp.where(_chunk_masks()[0], 1.0, 0.0)

        _, strict, eye = _chunk_masks()

        def chunk_group(cg, carry):
            items = []
            for sub in range(LOCKSTEP_CHUNKS):
                rows = pl.ds(pl.multiple_of((cg * LOCKSTEP_CHUNKS + sub) * DN_CHUNK, DN_CHUNK), DN_CHUNK)
                beta_all, g_all = _gates(pba_ref[0, rows, :], ea, dtb_ref[...])
                gc = _dot_exact_lhs(tri, g_all)
                gc_ref[0, rows, :] = gc
                for h in range(N_HEADS):
                    items.append((rows, h, beta_all[:, h:h + 1], gc[:, N_HEADS + h:N_HEADS + h + 1]))
            ks, kbs, decays, egs = [], [], [], []
            for rows, h, beta, gcol in items:
                cs = slice(h * HEAD_DIM, (h + 1) * HEAD_DIM)
                k = k_ref[0, rows, cs]
                ks.append(k)
                kbs.append(k * beta)
                decays.append(_chunk_decay(gcol)[0])
                egs.append(jnp.exp(gcol))
            ms = [_bdot(kb, k, NT) for kb, k in zip(kbs, ks)]
            tinvs = _inv_unit_lower_many([jnp.where(strict, m * dc, 0.0) for m, dc in zip(ms, decays)], eye)
            tsps = [_split(t) for t in tinvs]
            for (rows, h, beta, gcol), tsp, tinv in zip(items, tsps, tinvs):
                cs = slice(h * HEAD_DIM, (h + 1) * HEAD_DIM)
                u_ref[0, rows, cs] = _dot3(tsp, _split(v_ref[0, rows, cs] * beta))
                ti_ref[0, h, rows, :] = tinv
            for (rows, h, beta, gcol), tsp, kb, eg in zip(items, tsps, kbs, egs):
                cs = slice(h * HEAD_DIM, (h + 1) * HEAD_DIM)
                w_ref[0, rows, cs] = _dot3(tsp, _split(kb * eg))
            for (rows, h, beta, gcol), k, dc, eg in zip(items, ks, decays, egs):
                cs = slice(h * HEAD_DIM, (h + 1) * HEAD_DIM)
                q = q_ref[0, rows, cs] * QK_SCALE
                qk_ref[0, h, rows, :] = _bdot(q, k, NT) * dc
                qd_ref[0, rows, cs] = q * eg
                kd_ref[0, rows, cs] = k * jnp.exp(gcol[DN_CHUNK - 1:DN_CHUNK, :] - gcol)
            return carry

        lax.fori_loop(0, n_in // LOCKSTEP_CHUNKS, chunk_group, 0)

    def seg(cblk):
        return pl.BlockSpec((1, rblk, HALF_W), lambda i, r: (i, r, cblk))

    vec = pl.BlockSpec((1, LANES), lambda i, r: (0, 0))
    wide = pl.BlockSpec((1, rblk, HALF_W), lambda i, r: (i, r, 0))
    sq = pl.BlockSpec((1, N_HEADS, rblk, DN_CHUNK), lambda i, r: (i, 0, r, 0))
    return _call(
        body, name=name, grid=(b, t // rblk),
        in_specs=[seg(0), seg(1), seg(2),
                  pl.BlockSpec((1, rblk, LANES), lambda i, r: (i, r, GATE_COL_BLOCK)), vec, vec],
        out_specs=[wide, wide, wide, wide, sq, sq,
                   pl.BlockSpec((1, rblk, LANES), lambda i, r: (i, r, 0))],
        out_shape=[_sds((b, t, HALF_W), F32)] * 4
        + [_sds((b, N_HEADS, t, DN_CHUNK), F32)] * 2 + [_sds((b, t, LANES), F32)], comm=comm,
    )(qkv, qkv, qkv, proj3, alog_row, dtb_row)


def dn_scan_fwd(u, w, qd, kd, qk, gc, name):
    b, t, _ = u.shape
    nc = t // DN_CHUNK
    bh = b * N_HEADS

    def body(u_ref, w_ref, qd_ref, kd_ref, qk_ref, gc_ref, o_ref, sin_ref, s_ref):
        @pl.when(pl.program_id(0) == 0)
        def _():
            s_ref[...] = jnp.zeros_like(s_ref)

        items = [(bi, h, slice(h * HEAD_DIM, (h + 1) * HEAD_DIM)) for bi in range(b) for h in range(N_HEADS)]
        sbs = []
        for bi, h, cs in items:
            s = s_ref[bi * N_HEADS + h]
            sin_ref[0, bi * N_HEADS + h] = s
            sbs.append(s.astype(BF16))
        ws = [_bdot(w_ref[bi, :, cs], sb, NN) for (bi, h, cs), sb in zip(items, sbs)]
        qs = [_bdot(qd_ref[bi, :, cs], sb, NN) for (bi, h, cs), sb in zip(items, sbs)]
        vbs = [(u_ref[bi, :, cs] - wsi).astype(BF16) for (bi, h, cs), wsi in zip(items, ws)]
        for (bi, h, cs), qsi, vb in zip(items, qs, vbs):
            o_ref[bi, :, cs] = qsi + _bdot(qk_ref[bi, h], vb, NN)
        for (bi, h, cs), vb in zip(items, vbs):
            gl = jnp.exp(gc_ref[bi, DN_CHUNK - 1:DN_CHUNK, N_HEADS + h:N_HEADS + h + 1])
            idx = bi * N_HEADS + h
            s_ref[idx] = s_ref[idx] * gl + _bdot(kd_ref[bi, :, cs], vb, TN)

    wide = pl.BlockSpec((b, DN_CHUNK, HALF_W), lambda c: (0, c, 0))
    return _call(
        body, name=name, grid=(nc,),
        in_specs=[wide, wide, wide, wide,
                  pl.BlockSpec((b, N_HEADS, DN_CHUNK, DN_CHUNK), lambda c: (0, 0, c, 0)),
                  pl.BlockSpec((b, DN_CHUNK, LANES), lambda c: (0, c, 0))],
        out_specs=[wide, pl.BlockSpec((1, bh, HEAD_DIM, HEAD_DIM), lambda c: (c, 0, 0, 0))],
        out_shape=[_sds((b, t, HALF_W), F32), _sds((nc, bh, HEAD_DIM, HEAD_DIM), F32)],
        scratch=[pltpu.VMEM((bh, HEAD_DIM, HEAD_DIM), F32)],
    )(u, w, qd, kd, qk, gc)


def dn_scan_bwd(do, u, w, qd, kd, qk, gc, s_in, name):
    b, t, _ = u.shape
    nc = t // DN_CHUNK
    bh = b * N_HEADS

    def body(do_ref, u_ref, w_ref, qd_ref, kd_ref, qk_ref, gc_ref, sin_ref,
             du_ref, dw_ref, dqd_ref, dkd_ref, dqk_ref, dgc_ref, ds_ref):
        @pl.when(pl.program_id(0) == 0)
        def _():
            ds_ref[...] = jnp.zeros_like(ds_ref)

        last_row = _iota2((DN_CHUNK, LANES), 0) == DN_CHUNK - 1
        lane = _iota2((DN_CHUNK, LANES), 1)
        items = [(bi, h, slice(h * HEAD_DIM, (h + 1) * HEAD_DIM)) for bi in range(b) for h in range(N_HEADS)]
        sbs = [sin_ref[0, bi * N_HEADS + h].astype(BF16) for bi, h, cs in items]
        wvs = [w_ref[bi, :, cs].astype(BF16) for bi, h, cs in items]
        dovs = [do_ref[bi, :, cs].astype(BF16) for bi, h, cs in items]
        dsbs = [ds_ref[bi * N_HEADS + h].astype(BF16) for bi, h, cs in items]
        vbs = [(u_ref[bi, :, cs] - _dot(wv, sb, NN)).astype(BF16)
               for (bi, h, cs), wv, sb in zip(items, wvs, sbs)]
        for (bi, h, cs), dov, sb in zip(items, dovs, sbs):
            dqd_ref[bi, :, cs] = _dot(dov, sb, NT)
        dvns = [_dot(kd_ref[bi, :, cs].astype(BF16), dsb, NN) + _dot(qk_ref[bi, h].astype(BF16), dov, TN)
                for (bi, h, cs), dsb, dov in zip(items, dsbs, dovs)]
        for (bi, h, cs), vb, dsb, dov in zip(items, vbs, dsbs, dovs):
            dkd_ref[bi, :, cs] = _dot(vb, dsb, NT)
            dqk_ref[bi, h] = _dot(dov, vb, NT)
        dgls = []
        for (bi, h, cs), dvn, sb, wv, dov in zip(items, dvns, sbs, wvs, dovs):
            idx = bi * N_HEADS + h
            du_ref[bi, :, cs] = dvn
            dvn_b = dvn.astype(BF16)
            dw_ref[bi, :, cs] = -_dot(dvn_b, sb, NT)
            gl = jnp.exp(gc_ref[bi, DN_CHUNK - 1:DN_CHUNK, N_HEADS + h:N_HEADS + h + 1])
            ds = ds_ref[idx]
            dgl = jnp.sum(jnp.sum(ds * sin_ref[0, idx], axis=1, keepdims=True), axis=0, keepdims=True)
            dgls.append(dgl * gl)
            ds_ref[idx] = (ds * gl + _dot(qd_ref[bi, :, cs].astype(BF16), dov, TN)
                           - _dot(wv, dvn_b, TN))
        for bi in range(b):
            dgc = jnp.zeros((DN_CHUNK, LANES), F32)
            for h in range(N_HEADS):
                dgc = dgc + jnp.where(jnp.logical_and(last_row, lane == N_HEADS + h),
                                      dgls[bi * N_HEADS + h], 0.0)
            dgc_ref[bi] = dgc

    def rev(c):
        return nc - 1 - c

    wide = pl.BlockSpec((b, DN_CHUNK, HALF_W), lambda c: (0, rev(c), 0))
    sq = pl.BlockSpec((b, N_HEADS, DN_CHUNK, DN_CHUNK), lambda c: (0, 0, rev(c), 0))
    gates = pl.BlockSpec((b, DN_CHUNK, LANES), lambda c: (0, rev(c), 0))
    return _call(
        body, name=name, grid=(nc,),
        in_specs=[wide, wide, wide, wide, wide, sq, gates,
                  pl.BlockSpec((1, bh, HEAD_DIM, HEAD_DIM), lambda c: (rev(c), 0, 0, 0))],
        out_specs=[wide, wide, wide, wide, sq, gates],
        out_shape=[_sds((b, t, HALF_W), F32)] * 4
        + [_sds((b, N_HEADS, t, DN_CHUNK), F32), _sds((b, t, LANES), F32)],
        scratch=[pltpu.VMEM((bh, HEAD_DIM, HEAD_DIM), F32)],
    )(do, u, w, qd, kd, qk, gc, s_in)


def dn_chunk_bwd(qkv, proj3, alog_row, dtb_row, tinv, u, w, du, dw, dqd, dkd, dqk, dgc_scan, dproj3, name,
                 comm=None):
    b, t, _ = qkv.shape
    rblk = min(256, t)
    n_in = rblk // DN_CHUNK

    def body(q_ref, k_ref, v_ref, pba_ref, al_ref, dtb_ref, ti_ref, u_ref, w_ref,
             du_ref, dw_ref, dqd_ref, dkd_ref, dqk_ref, dgs_ref, dproj_in,
             dq_ref, dpba_ref, dal_ref, ddtb_ref):
        @pl.when(jnp.logical_and(pl.program_id(0) == 0, pl.program_id(1) == 0))
        def _():
            dal_ref[...] = jnp.zeros_like(dal_ref)
            ddtb_ref[...] = jnp.zeros_like(ddtb_ref)

        ea = jnp.exp(al_ref[...])
        incl0 = _chunk_masks()[0]
        tri = jnp.where(incl0, 1.0, 0.0)
        tri_up = jnp.where(_iota2((DN_CHUNK, DN_CHUNK), 1) >= _iota2((DN_CHUNK, DN_CHUNK), 0), 1.0, 0.0)
        lane = _iota2((DN_CHUNK, LANES), 1)
        last_col = _iota2((DN_CHUNK, 1), 0) == DN_CHUNK - 1

        _, strict, _ = _chunk_masks()
        gate_lane = jnp.logical_and(lane >= N_HEADS, lane < 2 * N_HEADS)

        def chunk_group(cg, carry):
            tiles, items = [], []
            for sub in range(LOCKSTEP_CHUNKS):
                rows = pl.ds(pl.multiple_of((cg * LOCKSTEP_CHUNKS + sub) * DN_CHUNK, DN_CHUNK), DN_CHUNK)
                pba = pba_ref[0, rows, :]
                beta_all, g_all = _gates(pba, ea, dtb_ref[...])
                gc = _dot_exact_lhs(tri, g_all)
                tiles.append((rows, pba, beta_all, g_all))
                for h in range(N_HEADS):
                    items.append((sub, rows, h, slice(h * HEAD_DIM, (h + 1) * HEAD_DIM),
                                  beta_all[:, h:h + 1], gc[:, N_HEADS + h:N_HEADS + h + 1]))
            decays = [_chunk_decay(gcol)[0] for _, _, _, _, _, gcol in items]
            egs = [jnp.exp(gcol) for _, _, _, _, _, gcol in items]
            qbs = [(q_ref[0, rows, cs] * QK_SCALE).astype(BF16) for _, rows, h, cs, _, _ in items]
            kfs = [k_ref[0, rows, cs].astype(BF16) for _, rows, h, cs, _, _ in items]
            kbs = [k_ref[0, rows, cs] * beta for _, rows, h, cs, beta, _ in items]
            kbbs = [kb.astype(BF16) for kb in kbs]
            tsps = [_split(ti_ref[0, h, rows, :]) for _, rows, h, cs, _, _ in items]
            drus = [_dot3(tsp, _split(du_ref[0, rows, cs]), TN)
                    for (_, rows, h, cs, _, _), tsp in zip(items, tsps)]
            drws = [_dot3(tsp, _split(dw_ref[0, rows, cs]), TN)
                    for (_, rows, h, cs, _, _), tsp in zip(items, tsps)]
            m_kks = [_dot(kbb, kf, NT) for kbb, kf in zip(kbbs, kfs)]
            a_qks = [_dot(qb, kf, NT) for qb, kf in zip(qbs, kfs)]
            dls = [-jnp.where(strict, _dot3(_split(dru), _split(u_ref[0, rows, cs]), NT)
                              + _dot3(_split(drw), _split(w_ref[0, rows, cs]), NT), 0.0)
                   for (_, rows, h, cs, _, _), dru, drw in zip(items, drus, drws)]
            dms = [(dl * dc).astype(BF16) for dl, dc in zip(dls, decays)]
            das = [(dqk_ref[0, h, rows, :] * dc).astype(BF16)
                   for (_, rows, h, cs, _, _), dc in zip(items, decays)]
            dkb_mm = [_dot(dm, kf, NN) for dm, kf in zip(dms, kfs)]
            dk_mm = [_dot(dm, kbb, TN) + _dot(da, qb, TN) for dm, kbb, da, qb in zip(dms, kbbs, das, qbs)]
            dqs_mm = [_dot(da, kf, NN) for da, kf in zip(das, kfs)]
            dgc_tiles = [dgs_ref[0, rows, :] for rows, _, _, _ in tiles]
            dbeta_tiles = [jnp.zeros((DN_CHUNK, LANES), F32) for _ in tiles]
            for n_it, (sub, rows, h, cs, beta, gcol) in enumerate(items):
                eg, dc = egs[n_it], decays[n_it]
                k = k_ref[0, rows, cs]
                q = q_ref[0, rows, cs] * QK_SCALE
                kb, dru, drw = kbs[n_it], drus[n_it], drws[n_it]
                ek = jnp.exp(gcol[DN_CHUNK - 1:DN_CHUNK, :] - gcol)
                e_mat = (dls[n_it] * m_kks[n_it] + dqk_ref[0, h, rows, :] * a_qks[n_it]) * dc
                dkb = drw * eg + dkb_mm[n_it]
                dqd = dqd_ref[0, rows, cs]
                dkd = dkd_ref[0, rows, cs]
                kdk = dkd * k * ek
                kdk_total = jnp.sum(jnp.sum(kdk, axis=0, keepdims=True), axis=1, keepdims=True)
                dg = (jnp.sum(drw * kb * eg + dqd * q * eg - kdk, axis=-1, keepdims=True)
                      + jnp.sum(e_mat, axis=1, keepdims=True)
                      - _row_to_col(jnp.sum(e_mat, axis=0, keepdims=True))
                      + jnp.where(last_col, kdk_total, 0.0))
                dbeta = jnp.sum(dkb * k + dru * v_ref[0, rows, cs], axis=-1, keepdims=True)
                dq_ref[0, rows, cs] = (dqs_mm[n_it] + dqd * eg) * QK_SCALE
                dq_ref[0, rows, pl.ds(HALF_W + h * HEAD_DIM, HEAD_DIM)] = dk_mm[n_it] + dkd * ek + dkb * beta
                dq_ref[0, rows, pl.ds(2 * HALF_W + h * HEAD_DIM, HEAD_DIM)] = dru * beta
                dgc_tiles[sub] = dgc_tiles[sub] + jnp.where(lane == N_HEADS + h, dg, 0.0)
                dbeta_tiles[sub] = dbeta_tiles[sub] + jnp.where(lane == h, dbeta, 0.0)
            for (rows, pba, beta_all, g_all), dgc_tile, dbeta_tile in zip(tiles, dgc_tiles, dbeta_tiles):
                dg_tile = _dot_exact_lhs(tri_up, dgc_tile)
                da_pre = dg_tile * (-ea) * _sigmoid(pba + dtb_ref[...])
                dal_ref[...] += jnp.sum(jnp.where(gate_lane, dg_tile * g_all, 0.0), axis=0, keepdims=True)
                ddtb_ref[...] += jnp.sum(jnp.where(gate_lane, da_pre, 0.0), axis=0, keepdims=True)
                dpba_ref[0, rows, :] = jnp.where(lane < N_HEADS, dbeta_tile * beta_all * (1.0 - beta_all),
                                                 jnp.where(gate_lane, da_pre, 0.0)).astype(BF16)
            return carry

        lax.fori_loop(0, n_in // LOCKSTEP_CHUNKS, chunk_group, 0)

    def seg(cblk):
        return pl.BlockSpec((1, rblk, HALF_W), lambda i, r: (i, r, cblk))

    vec = pl.BlockSpec((1, LANES), lambda i, r: (0, 0))
    wide = pl.BlockSpec((1, rblk, HALF_W), lambda i, r: (i, r, 0))
    sq = pl.BlockSpec((1, N_HEADS, rblk, DN_CHUNK), lambda i, r: (i, 0, r, 0))
    gates = pl.BlockSpec((1, rblk, LANES), lambda i, r: (i, r, 0))
    return _call(
        body, name=name, grid=(b, t // rblk),
        in_specs=[seg(0), seg(1), seg(2),
                  pl.BlockSpec((1, rblk, LANES), lambda i, r: (i, r, GATE_COL_BLOCK)), vec, vec,
                  sq, wide, wide, wide, wide, wide, wide, sq, gates, _ANY],
        out_specs=[pl.BlockSpec((1, rblk, 3 * HALF_W), lambda i, r: (i, r, 0)),
                   pl.BlockSpec((1, rblk, LANES), lambda i, r: (i, r, GATE_COL_BLOCK)), vec, vec],
        out_shape=[_sds((b, t, 3 * HALF_W), F32), _sds(dproj3.shape, BF16),
                   _sds((1, LANES), F32), _sds((1, LANES), F32)],
        input_output_aliases={15: 1}, comm=comm,
    )(qkv, qkv, qkv, proj3, alog_row, dtb_row, tinv, u, w, du, dw, dqd, dkd, dqk, dgc_scan, dproj3)


def dn_out_fwd(o, proj, dn_norm, name):
    n = o.shape[0]
    tm = min(ROW_TILE, n)

    def body(o_ref, z_ref, g_ref, y_ref):
        for h in range(N_HEADS):
            cs = slice(h * HEAD_DIM, (h + 1) * HEAD_DIM)
            oh = o_ref[:, cs]
            z = z_ref[:, cs]
            y = oh * _rms_scale(oh) * g_ref[...]
            y_ref[:, cs] = (y * (z * _sigmoid(z))).astype(BF16)

    half = pl.BlockSpec((tm, HALF_W), lambda i: (i, 0))
    return _call(
        body, name=name, grid=(n // tm,),
        in_specs=[half, pl.BlockSpec((tm, HALF_W), lambda i: (i, 5)),
                  pl.BlockSpec((1, HEAD_DIM), lambda i: (0, 0))],
        out_specs=half, out_shape=_sds((n, HALF_W), BF16),
    )(o, proj, dn_norm)


def dn_out_bwd(dy, o, proj, dn_norm, dproj, name):
    n = o.shape[0]
    tm = min(ROW_TILE, n)

    def body(dy_ref, o_ref, z_ref, g_ref, dproj_in, do_ref, dz_ref, dg_ref):
        @pl.when(pl.program_id(0) == 0)
        def _():
            dg_ref[...] = jnp.zeros_like(dg_ref)

        g = g_ref[...]
        dg = jnp.zeros_like(g)
        for h in range(N_HEADS):
            cs = slice(h * HEAD_DIM, (h + 1) * HEAD_DIM)
            oh = o_ref[:, cs]
            z = z_ref[:, cs]
            d = dy_ref[:, cs]
            r = _rms_scale(oh)
            nh = oh * r
            sz = _sigmoid(z)
            dyn = d * (z * sz)
            dz_ref[:, cs] = (d * (nh * g) * (sz * (1.0 + z * (1.0 - sz)))).astype(BF16)
            dg = dg + jnp.sum(dyn * nh, axis=0, keepdims=True)
            dn = dyn * g
            do_ref[:, cs] = r * (dn - nh * jnp.mean(dn * nh, axis=-1, keepdims=True))
        dg_ref[...] += dg

    half = pl.BlockSpec((tm, HALF_W), lambda i: (i, 0))
    vec = pl.BlockSpec((1, HEAD_DIM), lambda i: (0, 0))
    return _call(
        body, name=name, grid=(n // tm,),
        in_specs=[half, half, pl.BlockSpec((tm, HALF_W), lambda i: (i, 5)), vec, _ANY],
        out_specs=[half, pl.BlockSpec((tm, HALF_W), lambda i: (i, 5)), vec],
        out_shape=[_sds((n, HALF_W), F32), _sds(dproj.shape, BF16), _sds((1, HEAD_DIM), F32)],
        input_output_aliases={4: 1},
    )(dy, o, proj, dn_norm, dproj)


def _adamw_math(w, g, m, v):
    m_new = ADAM_B1 * m + (1.0 - ADAM_B1) * g
    v_new = ADAM_B2 * v + (1.0 - ADAM_B2) * (g * g)
    m_hat = m_new / (1.0 - ADAM_B1 ** ADAM_STEP)
    v_hat = v_new / (1.0 - ADAM_B2 ** ADAM_STEP)
    delta = -ADAM_LR * (m_hat / (jnp.sqrt(v_hat) + ADAM_EPS) + ADAM_WD * w)
    return delta, m_new, v_new


def adamw(w, g, m, v, name):
    r, c = w.shape
    tr = r
    for cand in (256, 352):
        if r % cand == 0 and r > cand:
            tr = cand
            break

    def body(w_ref, g_ref, m_ref, v_ref, d_ref, mo_ref, vo_ref):
        d, mn, vn = _adamw_math(w_ref[...], g_ref[...], m_ref[...], v_ref[...])
        d_ref[...] = d
        mo_ref[...] = mn
        vo_ref[...] = vn

    spec = pl.BlockSpec((tr, c), lambda i: (i, 0))
    return _call(
        body, name=name, grid=(r // tr,),
        in_specs=[spec] * 4, out_specs=[spec] * 3, out_shape=[_sds((r, c), F32)] * 3,
    )(w, g, m, v)


def _place():
    return lax.axis_index("x"), lax.axis_index("y"), lax.axis_index("c")


def _other_chips(x, y):
    return [(1 - x, y), (x, 1 - y), (1 - x, 1 - y)]


_ANY = pl.BlockSpec(memory_space=pl.ANY)


def cast_place(w, shard_idx, name):
    r, cols = w.shape
    tr = r // 2

    def body(j_ref, w_ref, o_ref):
        o_ref[0] = w_ref[...].astype(BF16)

    return pl.pallas_call(
        body, name=name,
        grid_spec=pltpu.PrefetchScalarGridSpec(
            num_scalar_prefetch=1, grid=(r // tr,),
            in_specs=[pl.BlockSpec((tr, cols), lambda i, j: (i, 0))],
            out_specs=pl.BlockSpec((1, tr, cols), lambda i, j: (j[0], i, 0))),
        out_shape=_sds((N_SHARD, r, cols), BF16),
        compiler_params=pltpu.CompilerParams(dimension_semantics=("arbitrary",),
                                             vmem_limit_bytes=VMEM_LIMIT),
    )(shard_idx, w)


class Exchange:
    def __init__(self, inputs, out_shape, aliases, sems, phases):
        self.inputs, self.out_shape, self.aliases = list(inputs), list(out_shape), dict(aliases)
        self.sems, self.phases = list(sems), list(phases)


def run_exchange(ex, name):
    def body(*refs):
        n_in, n_out = len(ex.inputs), len(ex.out_shape)
        for _, fn in ex.phases:
            fn(refs[:n_in], refs[n_in:n_in + n_out], refs[n_in + n_out:])

    return _call(body, name=name, in_specs=[_ANY] * len(ex.inputs), out_specs=[_ANY] * len(ex.out_shape),
                 out_shape=ex.out_shape, scratch=ex.sems, input_output_aliases=ex.aliases)(*ex.inputs)


def merge_exchanges(exs):
    inputs, out_shape, sems, aliases, phases, out_slices = [], [], [], {}, [], []
    for ex in exs:
        i0, o0, s0 = len(inputs), len(out_shape), len(sems)
        inputs += ex.inputs
        out_shape += ex.out_shape
        sems += ex.sems
        for k, m in ex.aliases.items():
            aliases[i0 + k] = o0 + m
        si, so, ss = slice(i0, len(inputs)), slice(o0, len(out_shape)), slice(s0, len(sems))
        out_slices.append(so)
        for step, fn in ex.phases:
            phases.append((step, lambda ins, outs, sm, fn=fn, si=si, so=so, ss=ss: fn(ins[si], outs[so], sm[ss])))
    return Exchange(inputs, out_shape, aliases, sems, phases), out_slices


def _dma_sems(*sizes):
    return [pltpu.SemaphoreType.DMA((s,)) for s in sizes]


def gather_exchange(bufs, small=None, relay_step=-2):
    n = len(bufs)
    n_small = 0 if small is None else 1

    def half(outs, a, blk, hc):
        rh = bufs[a].shape[1] // 2
        return outs[a].at[blk, pl.ds(hc * rh, rh), :]

    def ici(outs, sems, a, k, blk, to):
        return pltpu.make_async_remote_copy(
            src_ref=half(outs, a, blk, to[2]), dst_ref=half(outs, a, blk, to[2]), send_sem=sems[0].at[3 * a + k],
            recv_sem=sems[1].at[3 * a + k], device_id=to, device_id_type=MESH)

    def d2d(outs, sems, a, k, blk, hc, to):
        return pltpu.make_async_remote_copy(
            src_ref=half(outs, a, blk, hc), dst_ref=half(outs, a, blk, hc), send_sem=sems[2].at[3 * a + k],
            recv_sem=sems[3].at[3 * a + k], device_id=to, device_id_type=MESH)

    def small_copy(ins, outs, sems, k, blk, to):
        return pltpu.make_async_remote_copy(
            src_ref=ins[n], dst_ref=outs[n].at[blk], send_sem=sems[0].at[3 * n + k],
            recv_sem=sems[1].at[3 * n + k], device_id=to, device_id_type=MESH)

    def start(ins, outs, sems):
        x, y, c = _place()
        j = 2 * x + y
        if n_small:
            pltpu.make_async_copy(ins[n], outs[n].at[j], sems[4].at[0]).start()
        for k, (px, py) in enumerate(_other_chips(x, y)):
            if n_small:
                small_copy(ins, outs, sems, k, j, (px, py, c)).start()
            for a in range(n):
                ici(outs, sems, a, k, j, (px, py, c)).start()

    def relay(ins, outs, sems):
        x, y, c = _place()
        for k, (px, py) in enumerate(_other_chips(x, y)):
            for a in range(n):
                ici(outs, sems, a, k, 2 * px + py, (px, py, c)).wait_recv()
                d2d(outs, sems, a, k, 2 * px + py, c, (x, y, 1 - c)).start()

    def finish(ins, outs, sems):
        x, y, c = _place()
        j = 2 * x + y
        for k, (px, py) in enumerate(_other_chips(x, y)):
            blk = 2 * px + py
            if n_small:
                small_copy(ins, outs, sems, k, blk, (px, py, c)).wait_recv()
                small_copy(ins, outs, sems, k, j, (px, py, c)).wait_send()
            for a in range(n):
                d2d(outs, sems, a, k, blk, 1 - c, (x, y, 1 - c)).wait_recv()
                ici(outs, sems, a, k, j, (px, py, c)).wait_send()
                d2d(outs, sems, a, k, blk, c, (x, y, 1 - c)).wait_send()
        if n_small:
            pltpu.make_async_copy(ins[n], outs[n].at[j], sems[4].at[0]).wait()

    out_shape = [_sds(b.shape, b.dtype) for b in bufs]
    if n_small:
        out_shape.append(_sds((N_SHARD,) + small.shape, small.dtype))
    return Exchange(list(bufs) + ([small] if n_small else []), out_shape, {a: a for a in range(n)},
                    _dma_sems(3 * n + 3, 3 * n + 3, 3 * n, 3 * n, 1),
                    [(0, start), (relay_step, relay), (-1, finish)])


def _start_then_wait(copies):
    def start(ins, outs, sems):
        for sent, _ in copies(ins, outs, sems):
            sent().start()

    def finish(ins, outs, sems):
        pairs = copies(ins, outs, sems)
        for _, arrival in pairs:
            arrival().wait_recv()
        for sent, _ in pairs:
            sent().wait_send()

    return [(0, start), (-1, finish)]


def pair_exchange(arrs):
    n = len(arrs)

    def copies(ins, outs, sems):
        x, y, c = _place()
        res = []
        for a in range(n):
            def mk(a=a):
                rh = arrs[a].shape[1] // 2
                return pltpu.make_async_remote_copy(
                    src_ref=ins[a].at[:, pl.ds((1 - c) * rh, rh), :], dst_ref=outs[a], send_sem=sems[0].at[a],
                    recv_sem=sems[1].at[a], device_id=(x, y, 1 - c), device_id_type=MESH)
            res.append((mk, mk))
        return res

    return Exchange(arrs, [_sds((a.shape[0], a.shape[1] // 2, a.shape[2]), a.dtype) for a in arrs], {},
                    _dma_sems(n, n), _start_then_wait(copies))


def pair_add(g, s, c_idx, name):
    nb, r, cols = g.shape
    rh = r // 2

    def body(c_ref, g_ref, s_ref, o_ref):
        o_ref[...] = (g_ref[...] + s_ref[...]).astype(BF16)

    return pl.pallas_call(
        body, name=name,
        grid_spec=pltpu.PrefetchScalarGridSpec(
            num_scalar_prefetch=1, grid=(nb,),
            in_specs=[pl.BlockSpec((1, rh, cols), lambda j, c: (j, c[0], 0)),
                      pl.BlockSpec((1, rh, cols), lambda j, c: (j, 0, 0))],
            out_specs=pl.BlockSpec((1, rh, cols), lambda j, c: (j, 0, 0))),
        out_shape=_sds((nb, rh, cols), BF16),
        compiler_params=pltpu.CompilerParams(dimension_semantics=("arbitrary",),
                                             vmem_limit_bytes=VMEM_LIMIT),
    )(c_idx, g, s)


def chip_exchange(arrs):
    n = len(arrs)

    def copies(ins, outs, sems):
        x, y, c = _place()
        j = 2 * x + y
        res = []
        for a in range(n):
            for k, (px, py) in enumerate(_other_chips(x, y)):
                def mk(src_blk, dst_blk, a=a, k=k, to=(px, py, c)):
                    return pltpu.make_async_remote_copy(
                        src_ref=ins[a].at[src_blk], dst_ref=outs[a].at[dst_blk], send_sem=sems[0].at[3 * a + k],
                        recv_sem=sems[1].at[3 * a + k], device_id=to, device_id_type=MESH)
                res.append((functools.partial(mk, 2 * px + py, j), functools.partial(mk, j, 2 * px + py)))
        return res

    return Exchange(arrs, [_sds(a.shape, a.dtype) for a in arrs], {}, _dma_sems(3 * n, 3 * n),
                    _start_then_wait(copies))


def sum_chips(r, p, shard_idx, name):
    nb, rh, cols = r.shape
    tr = rh

    def body(j_ref, p_ref, *refs):
        o_ref = refs[nb]
        j = j_ref[0]
        acc = None
        for i in range(nb):
            term = jnp.where(j == i, p_ref[0], refs[i][0]).astype(F32)
            acc = term if acc is None else acc + term
        o_ref[...] = acc

    def slot(i):
        return pl.BlockSpec((1, tr, cols), lambda t, j: (jnp.where(j[0] == i, (i + 1) % nb, i), t, 0))

    return pl.pallas_call(
        body, name=name,
        grid_spec=pltpu.PrefetchScalarGridSpec(
            num_scalar_prefetch=1, grid=(rh // tr,),
            in_specs=[pl.BlockSpec((1, tr, cols), lambda t, j: (j[0], t, 0))] + [slot(i) for i in range(nb)],
            out_specs=pl.BlockSpec((tr, cols), lambda t, j: (t, 0))),
        out_shape=_sds((rh, cols), F32),
        compiler_params=pltpu.CompilerParams(dimension_semantics=("arbitrary",),
                                             vmem_limit_bytes=VMEM_LIMIT),
    )(shard_idx, p, *([r] * nb))


def pair_swap(arrs):
    n = len(arrs)

    def copies(ins, outs, sems):
        x, y, c = _place()
        res = []
        for a in range(n):
            def mk(a=a):
                return pltpu.make_async_remote_copy(
                    src_ref=ins[a], dst_ref=outs[a], send_sem=sems[0].at[a], recv_sem=sems[1].at[a],
                    device_id=(x, y, 1 - c), device_id_type=MESH)
            res.append((mk, mk))
        return res

    return Exchange(arrs, [_sds(a.shape, a.dtype) for a in arrs], {}, _dma_sems(n, n),
                    _start_then_wait(copies))


ADAMW_STEPS_PER_HALF = 4


def adamw_pairs(items, name, comm=None):
    n_items = len(items)
    nh = ADAMW_STEPS_PER_HALF

    def body(*refs):
        ins, outs = refs[:5 * n_items], refs[5 * n_items:]
        mine = (pl.program_id(0) // nh) == lax.axis_index("c")
        for a in range(n_items):
            w_ref, gm_ref, gs_ref, m_ref, v_ref = ins[5 * a:5 * a + 5]
            g_ref, d_ref, mo_ref, vo_ref = outs[4 * a:4 * a + 4]
            g = jnp.where(mine, gm_ref[...], gs_ref[...])
            d, mn, vn = _adamw_math(w_ref[...], g, m_ref[...], v_ref[...])
            g_ref[...] = g
            d_ref[...] = d
            mo_ref[...] = mn
            vo_ref[...] = vn

    in_specs, out_specs, out_shape, args = [], [], [], []
    for w, g_mine, g_sib, m, v in items:
        r, cols = w.shape
        tr = r // (2 * nh)
        full = pl.BlockSpec((tr, cols), lambda i: (i, 0))
        part = pl.BlockSpec((tr, cols), lambda i: (i % nh, 0))
        in_specs += [full, part, part, full, full]
        out_specs += [full] * 4
        out_shape += [_sds((r, cols), F32)] * 4
        args += [w, g_mine, g_sib, m, v]
    res = _call(body, name=name, grid=(2 * nh,), in_specs=in_specs, out_specs=out_specs,
                out_shape=out_shape, comm=comm)(*args)
    own, hosted = (res, None) if comm is None else res
    grouped = [tuple(own[4 * a:4 * a + 4]) for a in range(n_items)]
    return grouped if comm is None else (grouped, hosted)


N_DEV = 8


def device_gather(pack):
    def copies(ins, outs, sems):
        x, y, c = _place()
        me = 4 * x + 2 * y + c
        res = []
        for k in range(1, N_DEV):
            fx, fy, fc = (k >> 2) & 1, (k >> 1) & 1, k & 1
            px, py, pc = (1 - x if fx else x, 1 - y if fy else y, 1 - c if fc else c)

            def mk(slot, k=k, to=(px, py, pc)):
                return pltpu.make_async_remote_copy(
                    src_ref=ins[0], dst_ref=outs[0].at[slot], send_sem=sems[0].at[k - 1],
                    recv_sem=sems[1].at[k - 1], device_id=to, device_id_type=MESH)
            res.append((functools.partial(mk, me), functools.partial(mk, 4 * px + 2 * py + pc)))
        return res

    return Exchange([pack], [_sds((N_DEV,) + pack.shape, pack.dtype)], {}, _dma_sems(N_DEV - 1, N_DEV - 1),
                    _start_then_wait(copies))


def sum_devices(buf, pack, me_idx, name):
    r, cols = pack.shape

    def body(me_ref, p_ref, *refs):
        o_ref = refs[N_DEV]
        acc = None
        for i in range(N_DEV):
            term = jnp.where(me_ref[0] == i, p_ref[...], refs[i][0])
            acc = term if acc is None else acc + term
        o_ref[...] = acc

    def slot(i):
        return pl.BlockSpec((1, r, cols), lambda t, me: (jnp.where(me[0] == i, (i + 1) % N_DEV, i), 0, 0))

    whole = pl.BlockSpec((r, cols), lambda t, me: (0, 0))
    return pl.pallas_call(
        body, name=name,
        grid_spec=pltpu.PrefetchScalarGridSpec(
            num_scalar_prefetch=1, grid=(1,),
            in_specs=[whole] + [slot(i) for i in range(N_DEV)], out_specs=whole),
        out_shape=_sds((r, cols), F32),
        compiler_params=pltpu.CompilerParams(dimension_semantics=("arbitrary",),
                                             vmem_limit_bytes=VMEM_LIMIT),
    )(me_idx, pack, *([buf] * N_DEV))


SMALL_NAMES = ("ffn1_norm", "mix_norm", "ffn2_norm", "final_norm", "sg_ln_g", "sg_ln_b",
               "dn_norm", "a_log", "dt_bias", "sg_b", "sg_w", "conv_w", "loss")


def _to_rows(a):
    flat = a.reshape(-1)
    pad = (-flat.shape[0]) % LANES
    if pad:
        flat = jnp.pad(flat, (0, pad))
    return flat.reshape(-1, LANES)


def _pack_small(parts):
    rows = [_to_rows(parts[k]) for k in SMALL_NAMES]
    pack = jnp.concatenate(rows, axis=0)
    pad = (-pack.shape[0]) % 8
    if pad:
        pack = jnp.pad(pack, ((0, pad), (0, 0)))
    return pack


def _unpack_small(pack, shapes):
    out, r0 = {}, 0
    for k in SMALL_NAMES:
        size = 1
        for s in shapes[k]:
            size *= s
        nrows = -(-size // LANES)
        out[k] = pack[r0:r0 + nrows].reshape(-1)[:size].reshape(shapes[k])
        r0 += nrows
    return out


def kernel(x, ffn1_norm, ffn1_w_gate, ffn1_w_up, ffn1_w_down, mix_norm, w_in, conv_w, a_log, dt_bias, dn_norm, sg_ln_g, sg_ln_b, sg_w, sg_b, w_out, ffn2_norm, ffn2_w_gate, ffn2_w_up, ffn2_w_down, final_norm, loss_target, m_ffn1_norm, m_ffn1_w_gate, m_ffn1_w_up, m_ffn1_w_down, m_mix_norm, m_w_in, m_conv_w, m_a_log, m_dt_bias, m_dn_norm, m_sg_ln_g, m_sg_ln_b, m_sg_w, m_sg_b, m_w_out, m_ffn2_norm, m_ffn2_w_gate, m_ffn2_w_up, m_ffn2_w_down, m_final_norm, v_ffn1_norm, v_ffn1_w_gate, v_ffn1_w_up, v_ffn1_w_down, v_mix_norm, v_w_in, v_conv_w, v_a_log, v_dt_bias, v_dn_norm, v_sg_ln_g, v_sg_ln_b, v_sg_w, v_sg_b, v_w_out, v_ffn2_norm, v_ffn2_w_gate, v_ffn2_w_up, v_ffn2_w_down, v_final_norm):
    bsz, t_len, d = x.shape
    n = bsz * t_len
    xy, yy, cc = _place()
    shard = 2 * xy + yy

    big_names = ["ffn1_w_gate", "ffn1_w_up", "ffn1_w_down", "w_in", "w_out",
                 "ffn2_w_gate", "ffn2_w_up", "ffn2_w_down"]
    big_w = dict(ffn1_w_gate=ffn1_w_gate, ffn1_w_up=ffn1_w_up, ffn1_w_down=ffn1_w_down, w_in=w_in,
                 w_out=w_out, ffn2_w_gate=ffn2_w_gate, ffn2_w_up=ffn2_w_up, ffn2_w_down=ffn2_w_down)
    big_m = dict(ffn1_w_gate=m_ffn1_w_gate, ffn1_w_up=m_ffn1_w_up, ffn1_w_down=m_ffn1_w_down, w_in=m_w_in,
                 w_out=m_w_out, ffn2_w_gate=m_ffn2_w_gate, ffn2_w_up=m_ffn2_w_up, ffn2_w_down=m_ffn2_w_down)
    big_v = dict(ffn1_w_gate=v_ffn1_w_gate, ffn1_w_up=v_ffn1_w_up, ffn1_w_down=v_ffn1_w_down, w_in=v_w_in,
                 w_out=v_w_out, ffn2_w_gate=v_ffn2_w_gate, ffn2_w_up=v_ffn2_w_up, ffn2_w_down=v_ffn2_w_down)
    shard_idx = jnp.reshape(shard, (1,)).astype(jnp.int32)
    c_idx = jnp.reshape(cc, (1,)).astype(jnp.int32)
    transposed = ("ffn1_w_gate", "ffn1_w_up", "ffn2_w_gate", "ffn2_w_up")

    def as2d(a, k):
        return a[0].T if k in transposed else a[0]

    def from2d(a, k):
        return a.T[None] if k in transposed else a[None]

    placed = {k: cast_place(as2d(big_w[k], k), shard_idx, name="cast_" + k) for k in big_names}
    first_names = ["ffn1_w_gate", "ffn1_w_up"]
    second_names = ["ffn1_w_down", "w_in"]
    third_names = ["w_out", "ffn2_w_gate"]
    fourth_names = ["ffn2_w_up", "ffn2_w_down"]
    res = run_exchange(gather_exchange([placed[k] for k in first_names], conv_w[0]), name="gather_first")
    gw = dict(zip(first_names, res[:2]))
    conv_full = res[2].transpose(1, 0, 2).reshape(CONV_K, 3 * HALF_W)

    x0 = x.reshape(n, d)
    def ffn_weights(prefix):
        return [gw[prefix + k].reshape(-1, d) for k in ("_w_gate", "_w_up", "_w_down")]

    def ffn_grad_blocks(grads):
        return [g.reshape(N_SHARD, -1, d) for g in grads]

    (h1, gate1, up1, act1), second = ffn_fwd(
        x0, ffn1_norm, gw["ffn1_w_gate"].reshape(-1, d), gw["ffn1_w_up"].reshape(-1, d), None,
        name="ffn1_fwd", comm=gather_exchange([placed[k] for k in second_names]))
    gw.update(zip(second_names, second))
    (x1,) = ffn_down(x0, act1, gw["ffn1_w_down"].reshape(-1, d), name="ffn1_down")
    w_in_full = gw["w_in"].transpose(1, 0, 2).reshape(d, IN_COLS)
    w_in_full = jnp.pad(w_in_full, ((0, 0), (0, PROJ_W - IN_COLS)))
    (proj, h2), third = in_proj_fwd(x1, mix_norm, w_in_full, name="in_proj_fwd",
                                    comm=gather_exchange([placed[k] for k in third_names]))
    gw.update(zip(third_names, third))
    proj3 = proj.reshape(bsz, t_len, PROJ_W)
    bias_tile = jnp.repeat(sg_b[0].T, SG_GROUP_DIM, axis=1)
    sg_out = sg_fwd(proj, sg_ln_g, sg_ln_b, sg_w[0], bias_tile, name="sg_fwd")
    qkv = dn_conv_fwd(proj3, conv_full, name="dn_conv_fwd")
    alog_row = jnp.zeros((1, LANES), F32).at[0, N_HEADS:2 * N_HEADS].set(a_log[0])
    dtb_row = jnp.zeros((1, LANES), F32).at[0, N_HEADS:2 * N_HEADS].set(dt_bias[0])
    (u_wy, w_wy, q_dec, k_dec, qk, tinv, gc), fourth = dn_chunk_fwd(
        qkv, proj3, alog_row, dtb_row, name="dn_chunk_fwd",
        comm=gather_exchange([placed[k] for k in fourth_names]))
    gw.update(zip(fourth_names, fourth))
    w_out_full = gw["w_out"].reshape(2 * HALF_W, d)
    o, s_in = dn_scan_fwd(u_wy, w_wy, q_dec, k_dec, qk, gc, name="dn_scan_fwd")
    dn_out = dn_out_fwd(o.reshape(n, HALF_W), proj, dn_norm, name="dn_out_fwd")
    x2 = out_proj_fwd(x1, sg_out, dn_out, w_out_full, name="out_proj_fwd")
    x3, h3, gate2, up2, act2 = ffn_fwd(x2, ffn2_norm, *ffn_weights("ffn2"), name="ffn2_fwd")
    dx3, d_final_norm, loss_tile = final_loss(x3, final_norm.reshape(1, d),
                                              loss_target.reshape(n, d), name="final_loss")

    dx2, dgate2, dup2, dyh2, d_ffn2_norm = ffn_bwd_act(
        dx3, x2, ffn2_norm, gate2, up2, *ffn_weights("ffn2"), name="ffn2_bwd_act")
    g_big = {}
    g_big["ffn2_w_gate"], g_big["ffn2_w_up"], g_big["ffn2_w_down"] = ffn_grad_blocks(ffn_bwd_w(
        [dgate2, dup2, act2], [h3, dyh2], [(0, 0), (1, 0), (2, 1)], name="ffn2_bwd_w"))

    early = ["ffn2_w_gate", "ffn2_w_up", "ffn2_w_down"]
    (d_sg, d_dn, dx2b), early_sib = out_proj_bwd_x(dx2, w_out_full, name="out_proj_bwd_x",
                                                   comm=pair_exchange([g_big[k] for k in early]))
    early_sums = [pair_add(g_big[k], s, c_idx, name="grad_pair_add_" + k) for k, s in zip(early, early_sib)]
    g_w_out = matmul_tn([sg_out, dn_out], dx2b, d, name="w_out_grad")
    g_big["w_out"] = g_w_out.reshape(N_SHARD, (2 * HALF_W) // N_SHARD, d)

    d_proj, d_sg_w, d_bias_tile, d_ln_g, d_ln_b = sg_bwd(d_sg, proj, sg_ln_g, sg_ln_b, sg_w[0],
                                                         bias_tile, name="sg_bwd")
    d_o, d_proj, d_dn_norm = dn_out_bwd(d_dn, o.reshape(n, HALF_W), proj, dn_norm, d_proj,
                                        name="dn_out_bwd")
    du, dw, dqd, dkd, dqk, dgc_scan = dn_scan_bwd(d_o.reshape(bsz, t_len, HALF_W), u_wy, w_wy, q_dec,
                                                  k_dec, qk, gc, s_in, name="dn_scan_bwd")
    (d_qkv, d_proj3, d_alog_row, d_dtb_row), early_chips = dn_chunk_bwd(
        qkv, proj3, alog_row, dtb_row, tinv, u_wy, w_wy, du, dw, dqd, dkd, dqk, dgc_scan,
        d_proj.reshape(bsz, t_len, PROJ_W), name="dn_chunk_bwd", comm=chip_exchange(early_sums))
    early_halves = [sum_chips(r, p, shard_idx, name="grad_chip_sum_" + k)
                    for k, r, p in zip(early, early_chips, early_sums)]
    d_proj3, d_conv = dn_conv_bwd(d_qkv, proj3, conv_full, d_proj3, name="dn_conv_bwd")
    d_proj = d_proj3.reshape(n, PROJ_W)
    g_w_in = matmul_tn([h2], d_proj, 3 * MXU_DIM, name="w_in_grad")[:, :IN_COLS]
    g_big["w_in"] = g_w_in.reshape(d, N_SHARD, IN_COLS // N_SHARD).transpose(1, 0, 2)

    def reduce_start(names):
        return pair_exchange([g_big[k] for k in names])

    def reduce_pair_sums(names, from_sib):
        return [pair_add(g_big[k], s, c_idx, name="grad_pair_add_" + k) for k, s in zip(names, from_sib)]

    def reduce_chip_sums(names, from_chips, sums):
        return [sum_chips(r, p, shard_idx, name="grad_chip_sum_" + k)
                for k, r, p in zip(names, from_chips, sums)]

    mid = ["w_in", "w_out"]
    (dx1, d_mix_norm, dyh1), mid_sib = in_proj_bwd_x(d_proj, w_in_full, x1, mix_norm, dx2,
                                                     name="in_proj_bwd_x", comm=reduce_start(mid))
    mid_sums = reduce_pair_sums(mid, mid_sib)
    down = ["ffn1_w_down"]
    (g_down,), mid_chips = ffn_bwd_w([act1], [dyh1], [(0, 0)], name="ffn1_bwd_w_down",
                                     comm=chip_exchange(mid_sums))
    g_big["ffn1_w_down"] = g_down.reshape(N_SHARD, -1, d)
    mid_halves = reduce_chip_sums(mid, mid_chips, mid_sums)
    leg, legs = merge_exchanges([reduce_start(down), pair_swap(mid_halves), pair_swap(early_halves)])
    leg_res = run_exchange(leg, name="grad_pair_exchange_down")
    down_sums = reduce_pair_sums(down, leg_res[legs[0]])
    mid_sib_halves, early_sib_halves = leg_res[legs[1]], leg_res[legs[2]]

    dx0, dgate1, dup1, _, d_ffn1_norm = ffn_bwd_act(
        dx1, x0, ffn1_norm, gate1, up1, *ffn_weights("ffn1"), name="ffn1_bwd_act")
    grad_x = dx0.reshape(bsz, t_len, d)
    d_sg_b = d_bias_tile.reshape(SG_CHUNK, SG_GROUPS, SG_GROUP_DIM).sum(axis=-1).T
    small_g = dict(ffn1_norm=d_ffn1_norm, mix_norm=d_mix_norm, ffn2_norm=d_ffn2_norm,
                   final_norm=d_final_norm, sg_ln_g=d_ln_g, sg_ln_b=d_ln_b, dn_norm=d_dn_norm,
                   a_log=d_alog_row[:, N_HEADS:2 * N_HEADS], dt_bias=d_dtb_row[:, N_HEADS:2 * N_HEADS],
                   sg_b=d_sg_b, sg_w=d_sg_w, conv_w=d_conv, loss=loss_tile[:, :1])
    my_pack = _pack_small(small_g)
    hosted, parts = merge_exchanges([chip_exchange(down_sums), device_gather(my_pack)])
    late = ["ffn1_w_gate", "ffn1_w_up"]
    late_grads, hosted_res = ffn_bwd_w([dgate1, dup1], [h1], [(0, 0), (1, 0)], name="ffn1_bwd_w_gate_up",
                                       comm=hosted)
    g_big["ffn1_w_gate"], g_big["ffn1_w_up"] = ffn_grad_blocks(late_grads)
    down_halves = reduce_chip_sums(down, hosted_res[parts[0]], down_sums)
    (all_packs,) = hosted_res[parts[1]]

    leg, legs = merge_exchanges([reduce_start(late), pair_swap(down_halves)])
    leg_res = run_exchange(leg, name="grad_pair_exchange")
    pair_sums = reduce_pair_sums(late, leg_res[legs[0]])
    down_sib_halves = leg_res[legs[1]]

    def adam_items(names, mine, sib):
        return [(as2d(big_w[k], k), gm, gs, as2d(big_m[k], k), as2d(big_v[k], k))
                for k, gm, gs in zip(names, mine, sib)]

    outs = {}
    done = adamw_pairs(
        adam_items(early + mid + down, early_halves + mid_halves + down_halves,
                   list(early_sib_halves) + list(mid_sib_halves) + list(down_sib_halves)),
        name="adamw_early")
    from_chips = run_exchange(chip_exchange(pair_sums), name="grad_chip_exchange")
    halves = reduce_chip_sums(late, from_chips, pair_sums)
    sib_halves = run_exchange(pair_swap(halves), name="grad_pair_swap")
    done += adamw_pairs(adam_items(late, halves, sib_halves), name="adamw_late")
    for k, res in zip(early + mid + down + late, done):
        outs[k] = tuple(from2d(a, k) for a in res)

    small_w = dict(ffn1_norm=ffn1_norm, mix_norm=mix_norm, ffn2_norm=ffn2_norm, final_norm=final_norm,
                   sg_ln_g=sg_ln_g, sg_ln_b=sg_ln_b, dn_norm=dn_norm, a_log=a_log, dt_bias=dt_bias,
                   sg_b=sg_b, sg_w=sg_w)
    small_m = dict(ffn1_norm=m_ffn1_norm, mix_norm=m_mix_norm, ffn2_norm=m_ffn2_norm,
                   final_norm=m_final_norm, sg_ln_g=m_sg_ln_g, sg_ln_b=m_sg_ln_b, dn_norm=m_dn_norm,
                   a_log=m_a_log, dt_bias=m_dt_bias, sg_b=m_sg_b, sg_w=m_sg_w)
    small_v = dict(ffn1_norm=v_ffn1_norm, mix_norm=v_mix_norm, ffn2_norm=v_ffn2_norm,
                   final_norm=v_final_norm, sg_ln_g=v_sg_ln_g, sg_ln_b=v_sg_ln_b, dn_norm=v_dn_norm,
                   a_log=v_a_log, dt_bias=v_dt_bias, sg_b=v_sg_b, sg_w=v_sg_w)
    shapes = {k: small_w[k].shape for k in small_w}
    shapes["conv_w"] = (CONV_K, 3 * HALF_W)
    shapes["loss"] = (1, 1)
    me_idx = jnp.reshape(4 * xy + 2 * yy + cc, (1,)).astype(jnp.int32)
    g_pack = sum_devices(all_packs, my_pack, me_idx, name="small_sum")
    g_small = _unpack_small(g_pack, shapes)
    loss = g_small["loss"].reshape(())
    cw = 3 * HALF_W // N_SHARD
    g_conv = lax.dynamic_slice_in_dim(g_small["conv_w"], shard * cw, cw, axis=1)
    zero_conv = jnp.zeros((CONV_K, 3 * HALF_W), F32)

    def packed(src, conv):
        parts = dict(src)
        parts["conv_w"] = lax.dynamic_update_slice_in_dim(zero_conv, conv[0], shard * cw, axis=1)
        parts["loss"] = jnp.zeros((1, 1), F32)
        return _pack_small(parts)

    d_pack, m_pack, v_pack = adamw(packed(small_w, conv_w), g_pack, packed(small_m, m_conv_w),
                                   packed(small_v, v_conv_w), name="adamw_small")
    d_small = _unpack_small(d_pack, shapes)
    m_small = _unpack_small(m_pack, shapes)
    v_small = _unpack_small(v_pack, shapes)

    def conv_block(full_arr):
        return lax.dynamic_slice_in_dim(full_arr, shard * cw, cw, axis=1)[None]

    for k in small_w:
        outs[k] = (g_small[k].reshape(small_w[k].shape), d_small[k], m_small[k], v_small[k])
    outs["conv_w"] = (g_conv[None], conv_block(d_small["conv_w"]), conv_block(m_small["conv_w"]),
                      conv_block(v_small["conv_w"]))

    order = ["ffn1_norm", "ffn1_w_gate", "ffn1_w_up", "ffn1_w_down", "mix_norm", "w_in", "conv_w",
             "a_log", "dt_bias", "dn_norm", "sg_ln_g", "sg_ln_b", "sg_w", "sg_b", "w_out", "ffn2_norm",
             "ffn2_w_gate", "ffn2_w_up", "ffn2_w_down", "final_norm"]
    return (loss, grad_x, *[outs[k][0] for k in order], *[outs[k][1] for k in order],
            *[outs[k][2] for k in order], *[outs[k][3] for k in order])
```
